```python
import math
import jax, jax.numpy as jnp
from jax import lax
import numpy as np

D_MODEL = 1024
BATCH = 8
SEQ = 4096
DEPTH = 1

HEAD_DIM = 64
N_Q_HEADS = 8
N_KV_HEADS = 2
Q_PER_KV = N_Q_HEADS // N_KV_HEADS
ATTN_WIDTH = N_Q_HEADS * HEAD_DIM
KV_WIDTH = N_KV_HEADS * HEAD_DIM
WINDOW = 128
BLOCK = 128
N_BUCKETS = 32
MAX_DISTANCE = 128
CONV_WIDTH = D_MODEL - ATTN_WIDTH
CONV_GROUPS = 8
CONV_K = 3
IN_PROJ_WIDTH = ATTN_WIDTH + 2 * KV_WIDTH + 3 * CONV_WIDTH
D_FF = -(-8 * D_MODEL // (3 * 256)) * 256
N_MOD = 6
EPS = 1e-6
NEG_INF = -1e30

kernel_name = "hymba_swa_sink_shortconv_adaln_block"


def rmsnorm(x, g):
    xf = x.astype(jnp.float32)
    y = xf * lax.rsqrt(jnp.mean(xf * xf, axis=-1, keepdims=True) + EPS)
    return (y * g.astype(jnp.float32)).astype(x.dtype)


def t5_bucket(dist):
    max_exact = N_BUCKETS // 2
    is_small = dist < max_exact
    d = jnp.maximum(dist, 1).astype(jnp.float32)
    large = max_exact + (jnp.log(d / max_exact) / math.log(MAX_DISTANCE / max_exact)
                         * (N_BUCKETS - max_exact)).astype(jnp.int32)
    large = jnp.minimum(large, N_BUCKETS - 1)
    return jnp.where(is_small, dist, large)


def banded_sink_attention(q, k, v, sinks, rel_bias):
    b, s = q.shape[0], q.shape[1]
    nb = s // BLOCK
    qb = q.reshape(b, nb, BLOCK, N_KV_HEADS, Q_PER_KV, HEAD_DIM)
    kb = k.reshape(b, nb, BLOCK, N_KV_HEADS, HEAD_DIM)
    vb = v.reshape(b, nb, BLOCK, N_KV_HEADS, HEAD_DIM)

    def with_prev(t):
        prev = jnp.concatenate([jnp.zeros_like(t[:, :1]), t[:, :-1]], axis=1)
        return jnp.concatenate([prev, t], axis=2)

    kw, vw = with_prev(kb), with_prev(vb)
    scores = jnp.einsum("bnqkgd,bnskd->bnkgqs", qb, kw).astype(jnp.float32) * (HEAD_DIM ** -0.5)

    qi = jnp.arange(BLOCK, dtype=jnp.int32)[:, None]
    sj = jnp.arange(2 * BLOCK, dtype=jnp.int32)[None, :]
    dist = qi + BLOCK - sj
    bias = rel_bias[t5_bucket(jnp.maximum(dist, 0))]
    bias = jnp.transpose(bias, (2, 0, 1)).reshape(N_KV_HEADS, Q_PER_KV, BLOCK, 2 * BLOCK)
    scores = scores + bias.astype(jnp.float32)

    in_window = (dist >= 0) & (dist < WINDOW)
    key_pos = jnp.arange(nb, dtype=jnp.int32)[:, None] * BLOCK - BLOCK + sj
    valid = in_window[None] & (key_pos >= 0)[:, None, :]
    scores = jnp.where(valid[None, :, None, None], scores, NEG_INF)

    sink = sinks.astype(jnp.float32).reshape(N_KV_HEADS, Q_PER_KV)[None, None, :, :, None, None]
    m = jnp.maximum(jnp.max(scores, axis=-1, keepdims=True), sink)
    p = jnp.exp(scores - m)
    p = p / (jnp.sum(p, axis=-1, keepdims=True) + jnp.exp(sink - m))
    out = jnp.einsum("bnkgqs,bnskd->bnqkgd", p.astype(v.dtype), vw)
    return out.reshape(b, s, ATTN_WIDTH)


def causal_short_conv(u, w):
    s = u.shape[1]
    up = jnp.pad(u, ((0, 0), (CONV_K - 1, 0), (0, 0)))
    return w[0] * up[:, 0:s] + w[1] * up[:, 1:s + 1] + w[2] * up[:, 2:s + 2]


def _fwd_setup_inputs(seed: int = 0) -> dict:
    key = jax.random.key(seed)
    ks = jax.random.split(key, 20)
    f32 = jnp.float32
    n = lambda k, shape, scale: (jax.random.normal(k, shape, f32) * scale)
    return {
        "x": n(ks[0], (BATCH, SEQ, D_MODEL), 1.0),
        "c": n(ks[1], (BATCH, D_MODEL), 1.0),
        "rel_bias": n(ks[2], (N_BUCKETS, N_Q_HEADS), 0.5),
        "w_ada": n(ks[3], (DEPTH, D_MODEL, N_MOD * D_MODEL), D_MODEL ** -0.5),
        "b_ada": n(ks[4], (DEPTH, N_MOD * D_MODEL), 0.01),
        "g_norm1": 1.0 + n(ks[5], (DEPTH, D_MODEL), 0.02),
        "w_in": n(ks[6], (DEPTH, D_MODEL, IN_PROJ_WIDTH), D_MODEL ** -0.5),
        "sinks": n(ks[7], (DEPTH, N_Q_HEADS), 1.0),
        "conv_w": n(ks[8], (DEPTH, CONV_K, CONV_WIDTH), CONV_K ** -0.5),
        "g_attn_out": 1.0 + n(ks[9], (DEPTH, ATTN_WIDTH), 0.02),
        "g_conv_out": 1.0 + n(ks[10], (DEPTH, CONV_WIDTH), 0.02),
        "w_out": n(ks[11], (DEPTH, D_MODEL, D_MODEL), D_MODEL ** -0.5),
        "g_norm2": 1.0 + n(ks[12], (DEPTH, D_MODEL), 0.02),
        "w_gu": n(ks[13], (DEPTH, D_MODEL, 2 * D_FF), D_MODEL ** -0.5),
        "w_down": n(ks[14], (DEPTH, D_FF, D_MODEL), D_FF ** -0.5),
        "g_final": 1.0 + n(ks[15], (D_MODEL,), 0.02),
    }


def _fwd_reference(x, c, rel_bias, w_ada, b_ada, g_norm1, w_in, sinks, conv_w, g_attn_out,
              g_conv_out, w_out, g_norm2, w_gu, w_down, g_final):
    b, s = x.shape[0], x.shape[1]
    cond = jax.nn.silu(c)
    splits = [ATTN_WIDTH, ATTN_WIDTH + KV_WIDTH, ATTN_WIDTH + 2 * KV_WIDTH,
              ATTN_WIDTH + 2 * KV_WIDTH + CONV_WIDTH, ATTN_WIDTH + 2 * KV_WIDTH + 2 * CONV_WIDTH]
    for l in range(DEPTH):
        mod = (cond @ w_ada[l] + b_ada[l])[:, None, :]
        sh1, sc1, g1, sh2, sc2, g2 = jnp.split(mod, N_MOD, axis=-1)

        h = rmsnorm(x, g_norm1[l]) * (1.0 + sc1) + sh1
        proj = h @ w_in[l]
        q, k, v, gate_b, gate_c, xc = jnp.split(proj, splits, axis=-1)
        q = q.reshape(b, s, N_Q_HEADS, HEAD_DIM)
        k = k.reshape(b, s, N_KV_HEADS, HEAD_DIM)
        v = v.reshape(b, s, N_KV_HEADS, HEAD_DIM)
        attn = banded_sink_attention(q, k, v, sinks[l], rel_bias)
        conv = gate_b * causal_short_conv(gate_c * xc, conv_w[l])
        merged = jnp.concatenate([rmsnorm(attn, g_attn_out[l]), rmsnorm(conv, g_conv_out[l])], axis=-1)
        x = x + g1 * (merged @ w_out[l])

        h2 = rmsnorm(x, g_norm2[l]) * (1.0 + sc2) + sh2
        gate, up = jnp.split(h2 @ w_gu[l], 2, axis=-1)
        x = x + g2 * ((jax.nn.silu(gate) * up) @ w_down[l])
    return rmsnorm(x, g_final)


import jax as _jax
import jax.numpy as _jnp

TWIN_FORMAT = 'train_step'
FWD_PARAMS = ['x', 'c', 'rel_bias', 'w_ada', 'b_ada', 'g_norm1', 'w_in', 'sinks', 'conv_w', 'g_attn_out', 'g_conv_out', 'w_out', 'g_norm2', 'w_gu', 'w_down', 'g_final']
TWIN_WEIGHTS = ['rel_bias', 'w_ada', 'b_ada', 'g_norm1', 'w_in', 'sinks', 'conv_w', 'g_attn_out', 'g_conv_out', 'w_out', 'g_norm2', 'w_gu', 'w_down', 'g_final']
TWIN_DIFF_INPUT = 'x'
TWIN_INPUTS = ['x', 'c', 'rel_bias', 'w_ada', 'b_ada', 'g_norm1', 'w_in', 'sinks', 'conv_w', 'g_attn_out', 'g_conv_out', 'w_out', 'g_norm2', 'w_gu', 'w_down', 'g_final', 'loss_target', 'm_rel_bias', 'm_w_ada', 'm_b_ada', 'm_g_norm1', 'm_w_in', 'm_sinks', 'm_conv_w', 'm_g_attn_out', 'm_g_conv_out', 'm_w_out', 'm_g_norm2', 'm_w_gu', 'm_w_down', 'm_g_final', 'v_rel_bias', 'v_w_ada', 'v_b_ada', 'v_g_norm1', 'v_w_in', 'v_sinks', 'v_conv_w', 'v_g_attn_out', 'v_g_conv_out', 'v_w_out', 'v_g_norm2', 'v_w_gu', 'v_w_down', 'v_g_final']
TWIN_OUTPUTS = ['loss', 'grad_x', 'grad_rel_bias', 'grad_w_ada', 'grad_b_ada', 'grad_g_norm1', 'grad_w_in', 'grad_sinks', 'grad_conv_w', 'grad_g_attn_out', 'grad_g_conv_out', 'grad_w_out', 'grad_g_norm2', 'grad_w_gu', 'grad_w_down', 'grad_g_final', 'delta_rel_bias', 'delta_w_ada', 'delta_b_ada', 'delta_g_norm1', 'delta_w_in', 'delta_sinks', 'delta_conv_w', 'delta_g_attn_out', 'delta_g_conv_out', 'delta_w_out', 'delta_g_norm2', 'delta_w_gu', 'delta_w_down', 'delta_g_final', 'new_m_rel_bias', 'new_m_w_ada', 'new_m_b_ada', 'new_m_g_norm1', 'new_m_w_in', 'new_m_sinks', 'new_m_conv_w', 'new_m_g_attn_out', 'new_m_g_conv_out', 'new_m_w_out', 'new_m_g_norm2', 'new_m_w_gu', 'new_m_w_down', 'new_m_g_final', 'new_v_rel_bias', 'new_v_w_ada', 'new_v_b_ada', 'new_v_g_norm1', 'new_v_w_in', 'new_v_sinks', 'new_v_conv_w', 'new_v_g_attn_out', 'new_v_g_conv_out', 'new_v_w_out', 'new_v_g_norm2', 'new_v_w_gu', 'new_v_w_down', 'new_v_g_final']
TWIN_LEAF_KINDS = {'loss': 'loss', 'grad_x': 'grad_x', 'grad_rel_bias': 'grad_w', 'grad_w_ada': 'grad_w', 'grad_b_ada': 'grad_w', 'grad_g_norm1': 'grad_w', 'grad_w_in': 'grad_w', 'grad_sinks': 'grad_w', 'grad_conv_w': 'grad_w', 'grad_g_attn_out': 'grad_w', 'grad_g_conv_out': 'grad_w', 'grad_w_out': 'grad_w', 'grad_g_norm2': 'grad_w', 'grad_w_gu': 'grad_w', 'grad_w_down': 'grad_w', 'grad_g_final': 'grad_w', 'delta_rel_bias': 'delta_w', 'delta_w_ada': 'delta_w', 'delta_b_ada': 'delta_w', 'delta_g_norm1': 'delta_w', 'delta_w_in': 'delta_w', 'delta_sinks': 'delta_w', 'delta_conv_w': 'delta_w', 'delta_g_attn_out': 'delta_w', 'delta_g_conv_out': 'delta_w', 'delta_w_out': 'delta_w', 'delta_g_norm2': 'delta_w', 'delta_w_gu': 'delta_w', 'delta_w_down': 'delta_w', 'delta_g_final': 'delta_w', 'new_m_rel_bias': 'new_m', 'new_m_w_ada': 'new_m', 'new_m_b_ada': 'new_m', 'new_m_g_norm1': 'new_m', 'new_m_w_in': 'new_m', 'new_m_sinks': 'new_m', 'new_m_conv_w': 'new_m', 'new_m_g_attn_out': 'new_m', 'new_m_g_conv_out': 'new_m', 'new_m_w_out': 'new_m', 'new_m_g_norm2': 'new_m', 'new_m_w_gu': 'new_m', 'new_m_w_down': 'new_m', 'new_m_g_final': 'new_m', 'new_v_rel_bias': 'new_v', 'new_v_w_ada': 'new_v', 'new_v_b_ada': 'new_v', 'new_v_g_norm1': 'new_v', 'new_v_w_in': 'new_v', 'new_v_sinks': 'new_v', 'new_v_conv_w': 'new_v', 'new_v_g_attn_out': 'new_v', 'new_v_g_conv_out': 'new_v', 'new_v_w_out': 'new_v', 'new_v_g_norm2': 'new_v', 'new_v_w_gu': 'new_v', 'new_v_w_down': 'new_v', 'new_v_g_final': 'new_v'}


def _forward(args):
    return _fwd_reference(*[args[k] for k in FWD_PARAMS])


def _output_shape():
    out = _jax.eval_shape(lambda: _forward(_fwd_setup_inputs(0)))
    return out.shape, out.dtype

N_MICROBATCH = 1
ADAM_LR = 0.001
ADAM_B1 = 0.9
ADAM_B2 = 0.999
ADAM_EPS = 1e-08
ADAM_WD = 0.01
ADAM_STEP = 10
PER_EXAMPLE_BATCH_AXIS = {'x': 0, 'c': 0, 'loss_target': 0}
SHARED_INPUTS = []
_WEIGHT_DTYPES = {'rel_bias': _jnp.float32, 'w_ada': _jnp.float32, 'b_ada': _jnp.float32, 'g_norm1': _jnp.float32, 'w_in': _jnp.float32, 'sinks': _jnp.float32, 'conv_w': _jnp.float32, 'g_attn_out': _jnp.float32, 'g_conv_out': _jnp.float32, 'w_out': _jnp.float32, 'g_norm2': _jnp.float32, 'w_gu': _jnp.float32, 'w_down': _jnp.float32, 'g_final': _jnp.float32}
MOMENT_SCALE = {'rel_bias': 5.266023e-02, 'w_ada': 9.416245e-02, 'b_ada': 1.550366e-01, 'g_norm1': 1.275341e-01, 'w_in': 1.095758e-01, 'sinks': 1.955802e-02, 'conv_w': 1.036704e-01, 'g_attn_out': 1.196089e-01, 'g_conv_out': 1.014297e-01, 'w_out': 1.086629e-01, 'g_norm2': 1.206173e-01, 'w_gu': 5.689012e-02, 'w_down': 9.295348e-02, 'g_final': 3.320315e+01}


def _to_microbatches(a, axis):
    t = _jnp.moveaxis(a, axis, 0)
    t = t.reshape((N_MICROBATCH, t.shape[0] // N_MICROBATCH) + t.shape[1:])
    return _jnp.moveaxis(t, 1, axis + 1)


def setup_inputs(seed: int = 0) -> dict:
    inp = _fwd_setup_inputs(seed)
    key = _jax.random.fold_in(_jax.random.key(seed), 7919)
    shape, _ = _output_shape()
    out = dict(inp)
    out["loss_target"] = _jax.random.normal(_jax.random.fold_in(key, 0), shape, _jnp.float32)
    for i, name in enumerate(TWIN_WEIGHTS):
        w = inp[name].astype(_jnp.float32)
        if MOMENT_SCALE is None:
            s = _jnp.sqrt(_jnp.mean(_jnp.square(w)) + 1e-30)
        else:
            s = MOMENT_SCALE[name]
        km, kv = _jax.random.split(_jax.random.fold_in(key, i + 1))
        out[name] = w
        out["m_" + name] = s * _jax.random.normal(km, w.shape, _jnp.float32)
        out["v_" + name] = (s * s) * _jax.random.uniform(kv, w.shape, _jnp.float32, 0.5, 1.5)
    if N_MICROBATCH > 1:
        for name, axis in PER_EXAMPLE_BATCH_AXIS.items():
            out[name] = _to_microbatches(out[name], axis)
    return {'x': out['x'], 'c': out['c'], 'rel_bias': out['rel_bias'], 'w_ada': out['w_ada'], 'b_ada': out['b_ada'], 'g_norm1': out['g_norm1'], 'w_in': out['w_in'], 'sinks': out['sinks'], 'conv_w': out['conv_w'], 'g_attn_out': out['g_attn_out'], 'g_conv_out': out['g_conv_out'], 'w_out': out['w_out'], 'g_norm2': out['g_norm2'], 'w_gu': out['w_gu'], 'w_down': out['w_down'], 'g_final': out['g_final'], 'loss_target': out['loss_target'], 'm_rel_bias': out['m_rel_bias'], 'm_w_ada': out['m_w_ada'], 'm_b_ada': out['m_b_ada'], 'm_g_norm1': out['m_g_norm1'], 'm_w_in': out['m_w_in'], 'm_sinks': out['m_sinks'], 'm_conv_w': out['m_conv_w'], 'm_g_attn_out': out['m_g_attn_out'], 'm_g_conv_out': out['m_g_conv_out'], 'm_w_out': out['m_w_out'], 'm_g_norm2': out['m_g_norm2'], 'm_w_gu': out['m_w_gu'], 'm_w_down': out['m_w_down'], 'm_g_final': out['m_g_final'], 'v_rel_bias': out['v_rel_bias'], 'v_w_ada': out['v_w_ada'], 'v_b_ada': out['v_b_ada'], 'v_g_norm1': out['v_g_norm1'], 'v_w_in': out['v_w_in'], 'v_sinks': out['v_sinks'], 'v_conv_w': out['v_conv_w'], 'v_g_attn_out': out['v_g_attn_out'], 'v_g_conv_out': out['v_g_conv_out'], 'v_w_out': out['v_w_out'], 'v_g_norm2': out['v_g_norm2'], 'v_w_gu': out['v_w_gu'], 'v_w_down': out['v_w_down'], 'v_g_final': out['v_g_final']}


def _loss(weights, diff, rest, loss_target):
    with _jax.named_scope("forward"):
        args = {**rest, TWIN_DIFF_INPUT: diff, **{k: w.astype(_WEIGHT_DTYPES[k]) for k, w in weights.items()}}
        y = _forward(args)
    with _jax.named_scope("loss_head"):
        err = _jnp.square(y.astype(_jnp.float32) - loss_target)
        return 0.5 * _jnp.sum(_jnp.mean(err, axis=-1)) if err.ndim else 0.5 * err


def _adamw(w, g, m, v):
    m = ADAM_B1 * m + (1.0 - ADAM_B1) * g
    v = ADAM_B2 * v + (1.0 - ADAM_B2) * _jnp.square(g)
    m_hat = m / (1.0 - ADAM_B1 ** ADAM_STEP)
    v_hat = v / (1.0 - ADAM_B2 ** ADAM_STEP)
    delta = -ADAM_LR * (m_hat / (_jnp.sqrt(v_hat) + ADAM_EPS) + ADAM_WD * w)
    return delta, m, v


def reference(x, c, rel_bias, w_ada, b_ada, g_norm1, w_in, sinks, conv_w, g_attn_out, g_conv_out, w_out, g_norm2, w_gu, w_down, g_final, loss_target, m_rel_bias, m_w_ada, m_b_ada, m_g_norm1, m_w_in, m_sinks, m_conv_w, m_g_attn_out, m_g_conv_out, m_w_out, m_g_norm2, m_w_gu, m_w_down, m_g_final, v_rel_bias, v_w_ada, v_b_ada, v_g_norm1, v_w_in, v_sinks, v_conv_w, v_g_attn_out, v_g_conv_out, v_w_out, v_g_norm2, v_w_gu, v_w_down, v_g_final):
    given = dict(x=x, c=c, rel_bias=rel_bias, w_ada=w_ada, b_ada=b_ada, g_norm1=g_norm1, w_in=w_in, sinks=sinks, conv_w=conv_w, g_attn_out=g_attn_out, g_conv_out=g_conv_out, w_out=w_out, g_norm2=g_norm2, w_gu=w_gu, w_down=w_down, g_final=g_final, loss_target=loss_target, m_rel_bias=m_rel_bias, m_w_ada=m_w_ada, m_b_ada=m_b_ada, m_g_norm1=m_g_norm1, m_w_in=m_w_in, m_sinks=m_sinks, m_conv_w=m_conv_w, m_g_attn_out=m_g_attn_out, m_g_conv_out=m_g_conv_out, m_w_out=m_w_out, m_g_norm2=m_g_norm2, m_w_gu=m_w_gu, m_w_down=m_w_down, m_g_final=m_g_final, v_rel_bias=v_rel_bias, v_w_ada=v_w_ada, v_b_ada=v_b_ada, v_g_norm1=v_g_norm1, v_w_in=v_w_in, v_sinks=v_sinks, v_conv_w=v_conv_w, v_g_attn_out=v_g_attn_out, v_g_conv_out=v_g_conv_out, v_w_out=v_w_out, v_g_norm2=v_g_norm2, v_w_gu=v_w_gu, v_w_down=v_w_down, v_g_final=v_g_final)
    weights = {n: given[n] for n in TWIN_WEIGHTS}
    shared = {n: given[n] for n in SHARED_INPUTS}
    per_example = {n: given[n] for n in ['x', 'c']}
    grad_fn = _jax.value_and_grad(_loss, argnums=(0, 1))

    def one_microbatch(ex, loss_target):
        ex = dict(ex)
        diff = ex.pop(TWIN_DIFF_INPUT)
        return grad_fn(weights, diff, {**shared, **ex}, loss_target)

    if N_MICROBATCH == 1:
        loss, (grad_w, grad_x) = one_microbatch(per_example, given["loss_target"])
    else:
        def body(carry, xs):
            loss_sum, grad_sum = carry
            l_k, (gw_k, gx_k) = one_microbatch(xs[0], xs[1])
            with _jax.named_scope("update"):
                return (loss_sum + l_k, _jax.tree.map(_jnp.add, grad_sum, gw_k)), gx_k

        init = (_jnp.zeros((), _jnp.float32), _jax.tree.map(_jnp.zeros_like, weights))
        (loss, grad_w), grad_x = _jax.lax.scan(body, init, (per_example, given["loss_target"]))
    with _jax.named_scope("update"):
        delta_w, new_m, new_v = {}, {}, {}
        for n in TWIN_WEIGHTS:
            delta_w[n], new_m[n], new_v[n] = _adamw(weights[n], grad_w[n], given["m_" + n], given["v_" + n])
    return (loss, grad_x, *[grad_w[n] for n in TWIN_WEIGHTS], *[delta_w[n] for n in TWIN_WEIGHTS],
            *[new_m[n] for n in TWIN_WEIGHTS], *[new_v[n] for n in TWIN_WEIGHTS])
```

```python
import functools
import math

import jax
import jax.numpy as jnp
from jax import lax
from jax.experimental import pallas as pl
from jax.experimental.pallas import tpu as pltpu

F32 = jnp.float32
BF16 = jnp.bfloat16

D_MODEL = 1024
HEAD_DIM = 64
N_Q_HEADS = 8
ATTN_WIDTH = 512
KV_WIDTH = 128
CONV_WIDTH = 512
IN_PROJ_WIDTH = 2304
D_FF = 2816
N_MOD = 6
N_BUCKETS = 32
MAX_DISTANCE = 128
BLOCK = 128
EPS = 1e-6
NEG_INF = -1e30
SCALE = HEAD_DIM ** -0.5
N_DEV = 8

ADAM_LR = 0.001
ADAM_B1 = 0.9
ADAM_B2 = 0.999
ADAM_EPS = 1e-08
ADAM_WD = 0.01
ADAM_STEP = 10

SH1, SC1, G1, SH2, SC2, G2 = range(6)

VMEM_LIMIT_LARGE = 56 * 1024 * 1024
MESH_ID = pl.DeviceIdType.MESH

OFF_DMOD = 0
OFF_RELB = OFF_DMOD + N_MOD * D_MODEL
OFF_GN1 = OFF_RELB + N_BUCKETS * N_Q_HEADS
OFF_SINK = OFF_GN1 + D_MODEL
OFF_GATT = OFF_SINK + 128
OFF_GCV = OFF_GATT + ATTN_WIDTH
OFF_GN2 = OFF_GCV + CONV_WIDTH
OFF_GFIN = OFF_GN2 + D_MODEL
OFF_CONVW = OFF_GFIN + D_MODEL
OFF_LOSS = OFF_CONVW + 3 * CONV_WIDTH
PACKED = OFF_LOSS + 128


def _params(sem=None, vmem=None):
    return pltpu.CompilerParams(dimension_semantics=sem, vmem_limit_bytes=vmem)


def _full(shape):
    nd = len(shape)
    return pl.BlockSpec(shape, lambda *_: (0,) * nd)


def _rows(tm, width):
    return pl.BlockSpec((tm, width), lambda i, *_: (i, 0))


def _sigmoid(x):
    return 1.0 / (1.0 + jnp.exp(-x))


def _rsqrt_mean_sq(x):
    return lax.rsqrt(jnp.mean(x * x, axis=-1, keepdims=True) + EPS)


def _colsum(x):
    return jnp.sum(x, axis=0, keepdims=True)


def _dot(a, b):
    return jnp.dot(a, b, preferred_element_type=F32)


def _dot_nt(a, b):
    return lax.dot_general(a, b, (((1,), (1,)), ((), ())), preferred_element_type=F32)


def _dot_tn(a, b):
    return lax.dot_general(a, b, (((0,), (0,)), ((), ())), preferred_element_type=F32)


def _mesh_position():
    return lax.axis_index("x"), lax.axis_index("y"), lax.axis_index("c")


def _linear(p):
    return 4 * p[0] + 2 * p[1] + p[2]


def _all_gather(arrs, name, to_bf16, big):
    n = len(arrs)
    out_dtype = BF16 if to_bf16 else F32

    def body(*refs):
        in_refs, out_refs = refs[:n], refs[n:2 * n]
        rest = refs[2 * n:]
        if to_bf16:
            stage, rest = rest[:n], rest[n:]
            for a in range(n):
                stage[a][...] = in_refs[a][...].astype(BF16)
            srcs = stage
        else:
            srcs = in_refs
        send_sems, recv_sems, local_sems = rest
        x, y, c = _mesh_position()
        me, sibling = (x, y, c), (x, y, 1 - c)
        chips = [(1 - x, y), (x, 1 - y), (1 - x, 1 - y)]

        def slot(a, p):
            return out_refs[a].at[_linear(p)]

        def copy(k, a, block, to, src=None):
            return pltpu.make_async_remote_copy(
                src_ref=slot(a, block) if src is None else src,
                dst_ref=slot(a, block),
                send_sem=send_sems.at[k * n + a],
                recv_sem=recv_sems.at[k * n + a],
                device_id=to,
                device_id_type=MESH_ID,
            )

        mine = [pltpu.make_async_copy(srcs[a], slot(a, me), local_sems.at[a]) for a in range(n)]
        for cp in mine:
            cp.start()
        first = [copy(0, a, me, sibling, src=srcs[a]) for a in range(n)]
        for j, chip in enumerate(chips):
            first += [copy(1 + j, a, me, (*chip, c), src=srcs[a]) for a in range(n)]
        for cp in first:
            cp.start()
        passed = []
        for j, chip in enumerate(chips):
            for a in range(n):
                copy(1 + j, a, (*chip, c), me).wait_recv()
                fwd = copy(4 + j, a, (*chip, c), sibling)
                fwd.start()
                passed.append(fwd)
        for a in range(n):
            copy(0, a, sibling, me).wait_recv()
        for j, chip in enumerate(chips):
            for a in range(n):
                copy(4 + j, a, (*chip, 1 - c), me).wait_recv()
        for cp in first + passed:
            cp.wait_send()
        for cp in mine:
            cp.wait()

    vmem = pl.BlockSpec(memory_space=pltpu.VMEM)
    out_space = pl.BlockSpec(memory_space=pl.ANY) if big else vmem
    scratch = [pltpu.VMEM(a.shape, BF16) for a in arrs] if to_bf16 else []
    scratch += [pltpu.SemaphoreType.DMA((7 * n,)), pltpu.SemaphoreType.DMA((7 * n,)),
                pltpu.SemaphoreType.DMA((n,))]
    outs = pl.pallas_call(
        body, name=name,
        out_shape=[jax.ShapeDtypeStruct((N_DEV,) + a.shape, out_dtype) for a in arrs],
        in_specs=[vmem] * n, out_specs=[out_space] * n,
        scratch_shapes=scratch,
        compiler_params=_params(vmem=VMEM_LIMIT_LARGE if big else None),
    )(*arrs)
    return list(outs)


def _shard_exchange(arrs, name):
    n = len(arrs)

    def body(*refs):
        in_refs, out_refs = refs[:n], refs[n:2 * n]
        send_sems, recv_sems, local_sems = refs[2 * n:]
        x, y, c = _mesh_position()
        me = (x, y, c)
        mine = [pltpu.make_async_copy(in_refs[a].at[_linear(me)], out_refs[a].at[_linear(me)],
                                      local_sems.at[a]) for a in range(n)]
        for cp in mine:
            cp.start()
        copies = []
        for k in range(1, N_DEV):
            peer = (1 - x if k & 4 else x, 1 - y if k & 2 else y, 1 - c if k & 1 else c)
            for a in range(n):
                copies.append(pltpu.make_async_remote_copy(
                    src_ref=in_refs[a].at[_linear(peer)],
                    dst_ref=out_refs[a].at[_linear(me)],
                    send_sem=send_sems.at[(k - 1) * n + a],
                    recv_sem=recv_sems.at[(k - 1) * n + a],
                    device_id=peer, device_id_type=MESH_ID))
        for cp in copies:
            cp.start()
        for cp in copies:
            cp.wait()
        for cp in mine:
            cp.wait()

    hbm = pl.BlockSpec(memory_space=pl.ANY)
    outs = pl.pallas_call(
        body, name=name,
        out_shape=[jax.ShapeDtypeStruct(a.shape, a.dtype) for a in arrs],
        in_specs=[hbm] * n, out_specs=[hbm] * n,
        scratch_shapes=[pltpu.SemaphoreType.DMA((7 * n,)), pltpu.SemaphoreType.DMA((7 * n,)),
                        pltpu.SemaphoreType.DMA((n,))],
    )(*arrs)
    return list(outs)


def _silu_rows(c):
    def body(c_ref, o_ref):
        v = c_ref[...]
        o_ref[...] = v * _sigmoid(v)

    return pl.pallas_call(body, name="cond_silu", out_shape=jax.ShapeDtypeStruct(c.shape, F32))(c)


def _mod_columns(cond_all, w_ada, b_cols):
    def body(c_ref, w_ref, b_ref, o_ref):
        o_ref[...] = _dot(c_ref[...], w_ref[...]) + b_ref[...]

    return pl.pallas_call(body, name="mod_columns",
                          out_shape=jax.ShapeDtypeStruct((N_DEV, w_ada.shape[1]), F32))(cond_all, w_ada, b_cols)


def _in_proj(x, mod, g_norm1, w_in, tm):
    s = x.shape[0]

    def body(x_ref, mod_ref, g_ref, w_ref, h_ref, q_ref, kv_ref, gb_ref, gc_ref, xc_ref):
        xf = x_ref[...]
        n = xf * _rsqrt_mean_sq(xf) * g_ref[...]
        h = (n * (1.0 + mod_ref[SC1:SC1 + 1, :]) + mod_ref[SH1:SH1 + 1, :]).astype(BF16)
        h_ref[...] = h
        p = _dot(h, w_ref[...])
        q_ref[...] = p[:, 0:512].astype(BF16)
        kv_ref[...] = p[:, 512:768].astype(BF16)
        gb_ref[...] = p[:, 768:1280]
        gc_ref[...] = p[:, 1280:1792]
        xc_ref[...] = p[:, 1792:2304]

    return pl.pallas_call(
        body, name="in_proj", grid=(s // tm,),
        in_specs=[_rows(tm, D_MODEL), _full((8, D_MODEL)), _full((1, D_MODEL)), _full((D_MODEL, IN_PROJ_WIDTH))],
        out_specs=[_rows(tm, D_MODEL), _rows(tm, 512), _rows(tm, 256), _rows(tm, 512), _rows(tm, 512), _rows(tm, 512)],
        out_shape=[jax.ShapeDtypeStruct((s, D_MODEL), BF16), jax.ShapeDtypeStruct((s, 512), BF16),
                   jax.ShapeDtypeStruct((s, 256), BF16), jax.ShapeDtypeStruct((s, 512), F32),
                   jax.ShapeDtypeStruct((s, 512), F32), jax.ShapeDtypeStruct((s, 512), F32)],
        compiler_params=_params(("arbitrary",), VMEM_LIMIT_LARGE),
    )(x, mod, g_norm1, w_in)


def _t5_bucket(dist):
    max_exact = N_BUCKETS // 2
    is_small = dist < max_exact
    d = jnp.maximum(dist, 1).astype(F32)
    large = max_exact + (jnp.log(d / max_exact) / math.log(MAX_DISTANCE / max_exact)
                         * (N_BUCKETS - max_exact)).astype(jnp.int32)
    large = jnp.minimum(large, N_BUCKETS - 1)
    return jnp.where(is_small, dist, large)


def _bucket_table():
    qi = jnp.arange(BLOCK, dtype=jnp.int32)[:, None]
    sj = jnp.arange(2 * BLOCK, dtype=jnp.int32)[None, :]
    return _t5_bucket(jnp.maximum(qi + BLOCK - sj, 0))


def _window_mask():
    qi = lax.broadcasted_iota(jnp.int32, (BLOCK, 2 * BLOCK), 0)
    sj = lax.broadcasted_iota(jnp.int32, (BLOCK, 2 * BLOCK), 1)
    dist = qi + BLOCK - sj
    return (dist >= 0) & (dist < BLOCK)


def _bias_table(rel_bias, bucket):
    def body(rb_ref, bk_ref, o_ref):
        bk = bk_ref[...]
        inside = _window_mask()
        for h in range(N_Q_HEADS):
            acc = jnp.zeros((BLOCK, 2 * BLOCK), F32)
            for b in range(N_BUCKETS):
                acc = jnp.where(bk == b, rb_ref[b, h], acc)
            o_ref[h] = jnp.where(inside, acc, NEG_INF)

    return pl.pallas_call(
        body, name="bias_table",
        in_specs=[pl.BlockSpec(memory_space=pltpu.SMEM), pl.BlockSpec(memory_space=pltpu.VMEM)],
        out_shape=jax.ShapeDtypeStruct((N_Q_HEADS, BLOCK, 2 * BLOCK), F32),
    )(rel_bias, bucket)


def _load_kv_window(kv_ref, n):
    prev = jnp.maximum(n - 1, 0)
    kvw = jnp.concatenate([kv_ref[pl.ds(pl.multiple_of(prev * BLOCK, BLOCK), BLOCK), :],
                           kv_ref[pl.ds(pl.multiple_of(n * BLOCK, BLOCK), BLOCK), :]], axis=0)
    k, v = kvw[:, 0:128], kvw[:, 128:256]
    k_sw = pltpu.roll(k.astype(F32), 64, 1).astype(BF16)
    v_sw = pltpu.roll(v.astype(F32), 64, 1).astype(BF16)
    return (k, k_sw), (v, v_sw)


def _conv_taps(gc, xc, gc_prev, xc_prev, n):
    u = gc * xc
    before = jnp.where(n > 0, gc_prev * xc_prev, 0.0)
    row = lax.broadcasted_iota(jnp.int32, u.shape, 0)
    u1 = jnp.where(row == 0, before[7:8, :], pltpu.roll(u, 1, 0))
    u2 = jnp.where(row == 0, before[6:7, :], jnp.where(row == 1, before[7:8, :], pltpu.roll(u, 2, 0)))
    return u, u1, u2


def _mixer_fwd(q, kv, gb, gc, xc, bias, sinks, conv_w, g_attn, g_conv):
    s = q.shape[0]
    nb = s // BLOCK

    def body(sink_ref, q_ref, kv_ref, gb_ref, gc_ref, xc_ref, gcp_ref, xcp_ref, bias_ref, cw_ref, ga_ref, gcv_ref,
             attn_ref, merged_ref, lse_ref):
        n = pl.program_id(0)
        ks, vs = _load_kv_window(kv_ref, n)
        lane = lax.broadcasted_iota(jnp.int32, (BLOCK, BLOCK), 1)
        low = lane < HEAD_DIM
        col = lax.broadcasted_iota(jnp.int32, (BLOCK, 2 * BLOCK), 1)
        no_prev = (col < BLOCK) & (n == 0)
        lse_all = jnp.zeros((BLOCK, BLOCK), F32)
        pairs = []
        for p in range(4):
            qp = q_ref[:, 128 * p:128 * (p + 1)].astype(F32)
            kvh = p // 2
            res = []
            for e in range(2):
                h = 2 * p + e
                qm = jnp.where(low if e == 0 else ~low, qp, 0.0).astype(BF16)
                sw = 0 if kvh == e else 1
                sc = _dot_nt(qm, ks[sw]) * SCALE + bias_ref[h]
                sc = jnp.where(no_prev, NEG_INF, sc)
                sink = sink_ref[h]
                m = jnp.maximum(jnp.max(sc, axis=-1, keepdims=True), sink)
                pe = jnp.exp(sc - m)
                den = jnp.sum(pe, axis=-1, keepdims=True) + jnp.exp(sink - m)
                res.append(_dot(pe.astype(BF16), vs[sw]) / den)
                lse_all = lse_all + jnp.where(lane == h, m + jnp.log(den), 0.0)
            pairs.append(jnp.where(low, res[0], res[1]))
        attn = jnp.concatenate(pairs, axis=1)
        attn_ref[...] = attn
        lse_ref[...] = lse_all
        u, u1, u2 = _conv_taps(gc_ref[...], xc_ref[...], gcp_ref[...], xcp_ref[...], n)
        cw = cw_ref[...]
        cv = gb_ref[...] * (cw[0:1, :] * u2 + cw[1:2, :] * u1 + cw[2:3, :] * u)
        an = attn * _rsqrt_mean_sq(attn) * ga_ref[...]
        cn = cv * _rsqrt_mean_sq(cv) * gcv_ref[...]
        merged_ref[...] = jnp.concatenate([an, cn], axis=1).astype(BF16)

    blk = lambda w: pl.BlockSpec((BLOCK, w), lambda n: (n, 0))
    prev8 = pl.BlockSpec((8, 512), lambda n: (jnp.maximum(n * (BLOCK // 8) - 1, 0), 0))
    return pl.pallas_call(
        body, name="mixer_fwd", grid=(nb,),
        in_specs=[pl.BlockSpec(memory_space=pltpu.SMEM), blk(512), _full((s, 256)), blk(512), blk(512), blk(512),
                  prev8, prev8, _full((N_Q_HEADS, BLOCK, 2 * BLOCK)), _full((3, 512)), _full((1, 512)),
                  _full((1, 512))],
        out_specs=[blk(512), blk(1024), blk(128)],
        out_shape=[jax.ShapeDtypeStruct((s, 512), F32), jax.ShapeDtypeStruct((s, 1024), BF16),
                   jax.ShapeDtypeStruct((s, 128), F32)],
        compiler_params=_params(("arbitrary",)),
    )(sinks, q, kv, gb, gc, xc, gc, xc, bias, conv_w, g_attn, g_conv)


def _out_proj(merged, x, mod, w_out, tm):
    s = x.shape[0]

    def body(m_ref, x_ref, mod_ref, w_ref, o_ref, x1_ref):
        o = _dot(m_ref[...], w_ref[...])
        o_ref[...] = o.astype(BF16)
        x1_ref[...] = x_ref[...] + mod_ref[G1:G1 + 1, :] * o

    return pl.pallas_call(
        body, name="out_proj", grid=(s // tm,),
        in_specs=[_rows(tm, D_MODEL), _rows(tm, D_MODEL), _full((8, D_MODEL)), _full((D_MODEL, D_MODEL))],
        out_specs=[_rows(tm, D_MODEL), _rows(tm, D_MODEL)],
        out_shape=[jax.ShapeDtypeStruct((s, D_MODEL), BF16), jax.ShapeDtypeStruct((s, D_MODEL), F32)],
        compiler_params=_params(("arbitrary",)),
    )(merged, x, mod, w_out)


def _ffn_up(x1, mod, g_norm2, w_gu, tm, tn):
    s = x1.shape[0]
    nj = D_FF // tn

    def body(x_ref, mod_ref, g_ref, wg_ref, wu_ref, h_ref, gate_ref, up_ref, act_ref):
        @pl.when(pl.program_id(1) == 0)
        def _():
            xf = x_ref[...]
            n = xf * _rsqrt_mean_sq(xf) * g_ref[...]
            h_ref[...] = (n * (1.0 + mod_ref[SC2:SC2 + 1, :]) + mod_ref[SH2:SH2 + 1, :]).astype(BF16)

        h = h_ref[...]
        gate = _dot(h, wg_ref[...])
        up = _dot(h, wu_ref[...])
        gate_ref[...] = gate.astype(BF16)
        up_ref[...] = up.astype(BF16)
        act_ref[...] = (gate * _sigmoid(gate) * up).astype(BF16)

    tile = pl.BlockSpec((tm, tn), lambda i, j: (i, j))
    return pl.pallas_call(
        body, name="ffn_up", grid=(s // tm, nj),
        in_specs=[_rows(tm, D_MODEL), _full((8, D_MODEL)), _full((1, D_MODEL)),
                  pl.BlockSpec((D_MODEL, tn), lambda i, j: (0, j)),
                  pl.BlockSpec((D_MODEL, tn), lambda i, j: (0, j + nj))],
        out_specs=[_rows(tm, D_MODEL), tile, tile, tile],
        out_shape=[jax.ShapeDtypeStruct((s, D_MODEL), BF16)] + [jax.ShapeDtypeStruct((s, D_FF), BF16)] * 3,
        compiler_params=_params(("arbitrary", "arbitrary"), VMEM_LIMIT_LARGE),
    )(x1, mod, g_norm2, w_gu, w_gu)


def _ffn_down_loss(act, x1, mod, w_down, g_final, target, tm):
    s = x1.shape[0]

    def body(a_ref, x1_ref, mod_ref, w_ref, gf_ref, t_ref, o_ref, dx2_ref, small_ref):
        @pl.when(pl.program_id(0) == 0)
        def _():
            small_ref[...] = jnp.zeros_like(small_ref)

        o = _dot(a_ref[...], w_ref[...])
        o_ref[...] = o.astype(BF16)
        x2 = x1_ref[...] + mod_ref[G2:G2 + 1, :] * o
        r = _rsqrt_mean_sq(x2)
        xn = x2 * r
        gf = gf_ref[...]
        err = xn * gf - t_ref[...]
        dy = err * (1.0 / D_MODEL)
        dxn = dy * gf
        dx2_ref[...] = r * (dxn - xn * jnp.mean(dxn * xn, axis=-1, keepdims=True))
        small_ref[0:1, :] += _colsum(dy * xn)
        small_ref[1:2, :] += _colsum(err * err)

        @pl.when(pl.program_id(0) == pl.num_programs(0) - 1)
        def _():
            total = jnp.sum(small_ref[1:2, :], axis=-1, keepdims=True) * (0.5 / D_MODEL)
            small_ref[2:3, :] = jnp.broadcast_to(total, (1, D_MODEL))

    return pl.pallas_call(
        body, name="ffn_down_loss", grid=(s // tm,),
        in_specs=[_rows(tm, D_FF), _rows(tm, D_MODEL), _full((8, D_MODEL)), _full((D_FF, D_MODEL)),
                  _full((1, D_MODEL)), _rows(tm, D_MODEL)],
        out_specs=[_rows(tm, D_MODEL), _rows(tm, D_MODEL), _full((8, D_MODEL))],
        out_shape=[jax.ShapeDtypeStruct((s, D_MODEL), BF16), jax.ShapeDtypeStruct((s, D_MODEL), F32),
                   jax.ShapeDtypeStruct((8, D_MODEL), F32)],
        compiler_params=_params(("arbitrary",), VMEM_LIMIT_LARGE),
    )(act, x1, mod, w_down, g_final, target)


def _ffn_down_bwd(dx2, o2, gate, up, mod, w_down, tm):
    s = dx2.shape[0]

    def body(dx_ref, o_ref, gate_ref, up_ref, mod_ref, w_ref, do_ref, dgu_ref, small_ref):
        @pl.when(pl.program_id(0) == 0)
        def _():
            small_ref[...] = jnp.zeros_like(small_ref)

        dx = dx_ref[...]
        small_ref[0:1, :] += _colsum(dx * o_ref[...].astype(F32))
        do = (dx * mod_ref[G2:G2 + 1, :]).astype(BF16)
        do_ref[...] = do
        dact = _dot_nt(do, w_ref[...])
        gate = gate_ref[...].astype(F32)
        sg = _sigmoid(gate)
        dgu_ref[:, 0:D_FF] = (dact * up_ref[...].astype(F32) * (sg * (1.0 + gate * (1.0 - sg)))).astype(BF16)
        dgu_ref[:, D_FF:2 * D_FF] = (dact * (gate * sg)).astype(BF16)

    return pl.pallas_call(
        body, name="ffn_down_bwd", grid=(s // tm,),
        in_specs=[_rows(tm, D_MODEL), _rows(tm, D_MODEL), _rows(tm, D_FF), _rows(tm, D_FF), _full((8, D_MODEL)),
                  _full((D_FF, D_MODEL))],
        out_specs=[_rows(tm, D_MODEL), _rows(tm, 2 * D_FF), _full((8, D_MODEL))],
        out_shape=[jax.ShapeDtypeStruct((s, D_MODEL), BF16), jax.ShapeDtypeStruct((s, 2 * D_FF), BF16),
                   jax.ShapeDtypeStruct((8, D_MODEL), F32)],
        compiler_params=_params(("arbitrary",), VMEM_LIMIT_LARGE),
    )(dx2, o2, gate, up, mod, w_down)


def _norm_mod_bwd(dh, xf, g, scale_row, small_ref):
    r = _rsqrt_mean_sq(xf)
    xn = xf * r
    small_ref[0:1, :] += _colsum(dh)
    small_ref[1:2, :] += _colsum(dh * (xn * g))
    dn = dh * (1.0 + scale_row)
    small_ref[2:3, :] += _colsum(dn * xn)
    dxn = dn * g
    return r * (dxn - xn * jnp.mean(dxn * xn, axis=-1, keepdims=True))


def _ffn_up_bwd(dgu, x1, dx2, mod, g_norm2, w_gu, tm, tk):
    s = x1.shape[0]
    nk = (2 * D_FF) // tk

    def body(dgu_ref, x_ref, dx2_ref, mod_ref, g_ref, w_ref, dx1_ref, small_ref, acc_ref):
        i, k = pl.program_id(0), pl.program_id(1)

        @pl.when((i == 0) & (k == 0))
        def _():
            small_ref[...] = jnp.zeros_like(small_ref)

        part = _dot_nt(dgu_ref[...], w_ref[...])

        @pl.when(k == 0)
        def _():
            acc_ref[...] = part

        @pl.when(k > 0)
        def _():
            acc_ref[...] += part

        @pl.when(k == nk - 1)
        def _():
            dx1_ref[...] = dx2_ref[...] + _norm_mod_bwd(acc_ref[...], x_ref[...], g_ref[...],
                                                        mod_ref[SC2:SC2 + 1, :], small_ref)

    return pl.pallas_call(
        body, name="ffn_up_bwd", grid=(s // tm, nk),
        in_specs=[pl.BlockSpec((tm, tk), lambda i, k: (i, k)), _rows(tm, D_MODEL), _rows(tm, D_MODEL),
                  _full((8, D_MODEL)), _full((1, D_MODEL)), pl.BlockSpec((D_MODEL, tk), lambda i, k: (0, k))],
        out_specs=[_rows(tm, D_MODEL), _full((8, D_MODEL))],
        out_shape=[jax.ShapeDtypeStruct((s, D_MODEL), F32), jax.ShapeDtypeStruct((8, D_MODEL), F32)],
        scratch_shapes=[pltpu.VMEM((tm, D_MODEL), F32)],
        compiler_params=_params(("arbitrary", "arbitrary"), VMEM_LIMIT_LARGE),
    )(dgu, x1, dx2, mod, g_norm2, w_gu)


def _out_proj_bwd(dx1, o1, mod, w_out, tm):
    s = dx1.shape[0]

    def body(dx_ref, o_ref, mod_ref, w_ref, do_ref, dm_ref, small_ref):
        @pl.when(pl.program_id(0) == 0)
        def _():
            small_ref[...] = jnp.zeros_like(small_ref)

        dx = dx_ref[...]
        small_ref[0:1, :] += _colsum(dx * o_ref[...].astype(F32))
        do = (dx * mod_ref[G1:G1 + 1, :]).astype(BF16)
        do_ref[...] = do
        dm_ref[...] = _dot_nt(do, w_ref[...])

    return pl.pallas_call(
        body, name="out_proj_bwd", grid=(s // tm,),
        in_specs=[_rows(tm, D_MODEL), _rows(tm, D_MODEL), _full((8, D_MODEL)), _full((D_MODEL, D_MODEL))],
        out_specs=[_rows(tm, D_MODEL), _rows(tm, D_MODEL), _full((8, D_MODEL))],
        out_shape=[jax.ShapeDtypeStruct((s, D_MODEL), BF16), jax.ShapeDtypeStruct((s, D_MODEL), F32),
                   jax.ShapeDtypeStruct((8, D_MODEL), F32)],
        compiler_params=_params(("arbitrary",)),
    )(dx1, o1, mod, w_out)


def _group_norm_bwd(dm, a, g):
    r = _rsqrt_mean_sq(a)
    an = a * r
    dan = dm * g
    return r * (dan - an * jnp.mean(dan * an, axis=-1, keepdims=True)), _colsum(dm * an)


def _mixer_bwd(q, kv, gb, gc, xc, bias, sinks, conv_w, g_attn, g_conv, attn, lse, dmerged):
    s = q.shape[0]
    nb = s // BLOCK

    def body(sink_ref, q_ref, kv_ref, gb_ref, gc_ref, xc_ref, gcp_ref, xcp_ref, bias_ref, cw_ref, ga_ref, gcv_ref,
             attn_ref, lse_ref, dm_ref,
             dq_ref, dkv_ref, dgb_ref, dgc_ref, dxc_ref, dbias_ref, dsink_ref, small_ref, carry_ref):
        step = pl.program_id(0)
        n = nb - 1 - step

        @pl.when(step == 0)
        def _():
            dkv_ref[...] = jnp.zeros_like(dkv_ref)
            dbias_ref[...] = jnp.zeros_like(dbias_ref)
            dsink_ref[...] = jnp.zeros_like(dsink_ref)
            small_ref[...] = jnp.zeros_like(small_ref)
            carry_ref[...] = jnp.zeros_like(carry_ref)

        dm = dm_ref[...]
        gbv, gcv_, xcv = gb_ref[...], gc_ref[...], xc_ref[...]
        u, u1, u2 = _conv_taps(gcv_, xcv, gcp_ref[...], xcp_ref[...], n)
        cw = cw_ref[...]
        yv = cw[0:1, :] * u2 + cw[1:2, :] * u1 + cw[2:3, :] * u
        dcv, dg_conv = _group_norm_bwd(dm[:, 512:1024], gbv * yv, gcv_ref[...])
        small_ref[1:2, :] += dg_conv
        dgb_ref[...] = (dcv * yv).astype(BF16)
        dy = dcv * gbv
        nxt = carry_ref[...]
        row = lax.broadcasted_iota(jnp.int32, dy.shape, 0)
        d1 = jnp.where(row == BLOCK - 1, nxt[0:1, :], pltpu.roll(dy, BLOCK - 1, 0))
        d2 = jnp.where(row == BLOCK - 2, nxt[0:1, :],
                       jnp.where(row == BLOCK - 1, nxt[1:2, :], pltpu.roll(dy, BLOCK - 2, 0)))
        du = cw[2:3, :] * dy + cw[1:2, :] * d1 + cw[0:1, :] * d2
        dgc_ref[...] = (du * xcv).astype(BF16)
        dxc_ref[...] = (du * gcv_).astype(BF16)
        small_ref[2:3, :] += _colsum(dy * u2)
        small_ref[3:4, :] += _colsum(dy * u1)
        small_ref[4:5, :] += _colsum(dy * u)
        carry_ref[...] = dy[0:8, :]

        attn_v = attn_ref[...]
        dout, dg_attn = _group_norm_bwd(dm[:, 0:512], attn_v, ga_ref[...])
        small_ref[0:1, :] += dg_attn
        ks, vs = _load_kv_window(kv_ref, n)
        lane = lax.broadcasted_iota(jnp.int32, (BLOCK, BLOCK), 1)
        low = lane < HEAD_DIM
        col = lax.broadcasted_iota(jnp.int32, (BLOCK, 2 * BLOCK), 1)
        no_prev = (col < BLOCK) & (n == 0)
        lse_all = lse_ref[...]
        dk = jnp.zeros((2 * BLOCK, BLOCK), F32)
        dv = jnp.zeros((2 * BLOCK, BLOCK), F32)
        dsink = jnp.zeros((BLOCK, BLOCK), F32)
        dq_pairs = []
        for p in range(4):
            qp = q_ref[:, 128 * p:128 * (p + 1)].astype(F32)
            do_p = dout[:, 128 * p:128 * (p + 1)]
            prod = do_p * attn_v[:, 128 * p:128 * (p + 1)]
            kvh = p // 2
            res = []
            for e in range(2):
                h = 2 * p + e
                half = low if e == 0 else ~low
                qm = jnp.where(half, qp, 0.0).astype(BF16)
                dom = jnp.where(half, do_p, 0.0).astype(BF16)
                delta = jnp.sum(jnp.where(half, prod, 0.0), axis=-1, keepdims=True)
                lse_h = jnp.sum(jnp.where(lane == h, lse_all, 0.0), axis=-1, keepdims=True)
                sw = 0 if kvh == e else 1
                sc = _dot_nt(qm, ks[sw]) * SCALE + bias_ref[h]
                sc = jnp.where(no_prev, NEG_INF, sc)
                pr = jnp.exp(sc - lse_h)
                dp = _dot_nt(dom, vs[sw])
                ds = pr * (dp - delta)
                dbias_ref[h] += ds
                dsink = dsink + jnp.where(lane == h, -jnp.exp(sink_ref[h] - lse_h) * delta, 0.0)
                dsb = ds.astype(BF16)
                res.append(_dot(dsb, ks[sw]) * SCALE)
                dk_h = _dot_tn(dsb, qm) * SCALE
                dv_h = _dot_tn(pr.astype(BF16), dom)
                if sw:
                    dk_h = pltpu.roll(dk_h, 64, 1)
                    dv_h = pltpu.roll(dv_h, 64, 1)
                dk = dk + dk_h
                dv = dv + dv_h
            dq_pairs.append(jnp.where(low, res[0], res[1]))
        dq_ref[...] = jnp.concatenate(dq_pairs, axis=1).astype(BF16)
        dsink_ref[...] += dsink
        dkv_win = jnp.concatenate([dk, dv], axis=1)
        prev = jnp.maximum(n - 1, 0)
        dkv_ref[pl.ds(pl.multiple_of(prev * BLOCK, BLOCK), BLOCK), :] += dkv_win[0:BLOCK, :]
        dkv_ref[pl.ds(pl.multiple_of(n * BLOCK, BLOCK), BLOCK), :] += dkv_win[BLOCK:2 * BLOCK, :]

        @pl.when(step == nb - 1)
        def _():
            small_ref[5:6, :] = jnp.concatenate([_colsum(dsink_ref[...]), jnp.zeros((1, 512 - BLOCK), F32)], axis=1)

    blk = lambda w: pl.BlockSpec((BLOCK, w), lambda t: (nb - 1 - t, 0))
    prev8 = pl.BlockSpec((8, 512), lambda t: (jnp.maximum((nb - 1 - t) * (BLOCK // 8) - 1, 0), 0))
    bf = lambda w: jax.ShapeDtypeStruct((s, w), BF16)
    return pl.pallas_call(
        body, name="mixer_bwd", grid=(nb,),
        in_specs=[pl.BlockSpec(memory_space=pltpu.SMEM), blk(512), _full((s, 256)), blk(512), blk(512), blk(512),
                  prev8, prev8, _full((N_Q_HEADS, BLOCK, 2 * BLOCK)), _full((3, 512)), _full((1, 512)),
                  _full((1, 512)), blk(512), blk(128), blk(1024)],
        out_specs=[blk(512), _full((s, 256)), blk(512), blk(512), blk(512), _full((N_Q_HEADS, BLOCK, 2 * BLOCK)),
                   _full((BLOCK, BLOCK)), _full((8, 512))],
        out_shape=[bf(512), jax.ShapeDtypeStruct((s, 256), F32), bf(512), bf(512), bf(512),
                   jax.ShapeDtypeStruct((N_Q_HEADS, BLOCK, 2 * BLOCK), F32), jax.ShapeDtypeStruct((BLOCK, BLOCK), F32),
                   jax.ShapeDtypeStruct((8, 512), F32)],
        scratch_shapes=[pltpu.VMEM((8, 512), F32)],
        compiler_params=_params(("arbitrary",), VMEM_LIMIT_LARGE),
    )(sinks, q, kv, gb, gc, xc, gc, xc, bias, conv_w, g_attn, g_conv, attn, lse, dmerged)


def _in_proj_bwd(dq, dkv, dgb, dgc, dxc, x, dx1, mod, g_norm1, w_in, tm):
    s = x.shape[0]

    def body(dq_ref, dkv_ref, dgb_ref, dgc_ref, dxc_ref, x_ref, dx1_ref, mod_ref, g_ref, w_ref,
             dproj_ref, dx_ref, small_ref):
        @pl.when(pl.program_id(0) == 0)
        def _():
            small_ref[...] = jnp.zeros_like(small_ref)

        dproj = jnp.concatenate([dq_ref[...], dkv_ref[...].astype(BF16), dgb_ref[...], dgc_ref[...], dxc_ref[...]],
                                axis=1)
        dproj_ref[...] = dproj
        dh = _dot_nt(dproj, w_ref[...])
        dx_ref[...] = dx1_ref[...] + _norm_mod_bwd(dh, x_ref[...], g_ref[...], mod_ref[SC1:SC1 + 1, :], small_ref)

    return pl.pallas_call(
        body, name="in_proj_bwd", grid=(s // tm,),
        in_specs=[_rows(tm, 512), _rows(tm, 256), _rows(tm, 512), _rows(tm, 512), _rows(tm, 512),
                  _rows(tm, D_MODEL), _rows(tm, D_MODEL), _full((8, D_MODEL)), _full((1, D_MODEL)),
                  _full((D_MODEL, IN_PROJ_WIDTH))],
        out_specs=[_rows(tm, IN_PROJ_WIDTH), _rows(tm, D_MODEL), _full((8, D_MODEL))],
        out_shape=[jax.ShapeDtypeStruct((s, IN_PROJ_WIDTH), BF16), jax.ShapeDtypeStruct((s, D_MODEL), F32),
                   jax.ShapeDtypeStruct((8, D_MODEL), F32)],
        compiler_params=_params(("arbitrary",), VMEM_LIMIT_LARGE),
    )(dq, dkv, dgb, dgc, dxc, x, dx1, mod, g_norm1, w_in)


def _weight_grad(a, b, tn, ts, name):
    s, k = a.shape
    n = b.shape[1]

    def body(a_ref, b_ref, o_ref):
        part = _dot_tn(a_ref[...], b_ref[...])

        @pl.when(pl.program_id(1) == 0)
        def _():
            o_ref[...] = part

        @pl.when(pl.program_id(1) > 0)
        def _():
            o_ref[...] += part

    return pl.pallas_call(
        body, name=name, grid=(n // tn, s // ts),
        in_specs=[pl.BlockSpec((ts, k), lambda j, t: (t, 0)), pl.BlockSpec((ts, tn), lambda j, t: (t, j))],
        out_specs=pl.BlockSpec((k, tn), lambda j, t: (0, j)),
        out_shape=jax.ShapeDtypeStruct((k, n), F32),
        compiler_params=_params(("arbitrary", "arbitrary"), VMEM_LIMIT_LARGE),
    )(a, b)


def _rel_bias_grad(dbias, bucket):
    def body(db_ref, bk_ref, o_ref, rows_ref):
        bk = bk_ref[...]
        for b in range(N_BUCKETS):
            sel = (bk == b).astype(F32)
            for h in range(N_Q_HEADS):
                rows_ref[8 * b + h:8 * b + h + 1, :] = _colsum(db_ref[h] * sel)
        o_ref[...] = jnp.sum(rows_ref[...], axis=-1, keepdims=True)

    return pl.pallas_call(
        body, name="rel_bias_grad",
        out_shape=jax.ShapeDtypeStruct((N_BUCKETS * N_Q_HEADS, 1), F32),
        scratch_shapes=[pltpu.VMEM((N_BUCKETS * N_Q_HEADS, 2 * BLOCK), F32)],
    )(dbias, bucket)


def _sum_slots(parts):
    def body(p_ref, o_ref):
        acc = p_ref[0]
        for k in range(1, N_DEV):
            acc = acc + p_ref[k]
        o_ref[...] = acc

    return pl.pallas_call(body, name="sum_small_grads",
                          out_shape=jax.ShapeDtypeStruct(parts.shape[1:], F32))(parts)


def _w_ada_grad(cond_t, dmod_cols):
    def body(c_ref, d_ref, o_ref):
        o_ref[...] = _dot(c_ref[...], d_ref[...])

    return pl.pallas_call(body, name="w_ada_grad",
                          out_shape=jax.ShapeDtypeStruct((cond_t.shape[0], dmod_cols.shape[1]), F32))(cond_t, dmod_cols)


def _adam_math(w, g, m, v):
    m = ADAM_B1 * m + (1.0 - ADAM_B1) * g
    v = ADAM_B2 * v + (1.0 - ADAM_B2) * (g * g)
    m_hat = m / (1.0 - ADAM_B1 ** ADAM_STEP)
    v_hat = v / (1.0 - ADAM_B2 ** ADAM_STEP)
    delta = -ADAM_LR * (m_hat / (jnp.sqrt(v_hat) + ADAM_EPS) + ADAM_WD * w)
    return delta, m, v


def _adamw(w, m, v, g, tr, name):
    r, c = w.shape
    parts = g.ndim == 3

    def body(w_ref, m_ref, v_ref, g_ref, go_ref, d_ref, mo_ref, vo_ref):
        if parts:
            gv = g_ref[0].astype(F32)
            for k in range(1, N_DEV):
                gv = gv + g_ref[k].astype(F32)
        else:
            gv = g_ref[...]
        go_ref[...] = gv
        d_ref[...], mo_ref[...], vo_ref[...] = _adam_math(w_ref[...], gv, m_ref[...], v_ref[...])

    tile = pl.BlockSpec((tr, c), lambda i: (i, 0))
    gspec = pl.BlockSpec((N_DEV, tr, c), lambda i: (0, i, 0)) if parts else tile
    return pl.pallas_call(
        body, name=name, grid=(r // tr,),
        in_specs=[tile, tile, tile, gspec], out_specs=[tile] * 4,
        out_shape=[jax.ShapeDtypeStruct((r, c), F32)] * 4,
        compiler_params=_params(("arbitrary",)),
    )(w, m, v, g)


def _local_step(x, target, mod, w_in, w_out, w_gu, w_down, rel_bias, g_norm1, sinks, conv_w, g_attn, g_conv,
                g_norm2, g_final):
    s = x.shape[0]
    tm = min(512, s)
    tm_small = min(256, s)
    bucket = _bucket_table()
    bias = _bias_table(rel_bias, bucket)

    h, q, kv, gb, gc, xc = _in_proj(x, mod, g_norm1, w_in, tm)
    attn, merged, lse = _mixer_fwd(q, kv, gb, gc, xc, bias, sinks, conv_w, g_attn, g_conv)
    o1, x1 = _out_proj(merged, x, mod, w_out, tm)
    h2, gate, up, act = _ffn_up(x1, mod, g_norm2, w_gu, tm, D_FF // 2)
    o2, dx2, fin = _ffn_down_loss(act, x1, mod, w_down, g_final, target, tm)

    do2, dgu, sm_g2 = _ffn_down_bwd(dx2, o2, gate, up, mod, w_down, tm_small)
    dw_down = _weight_grad(act, do2, 512, tm, "w_down_grad")
    dx1, sm_2 = _ffn_up_bwd(dgu, x1, dx2, mod, g_norm2, w_gu, tm, D_FF // 2)
    dw_gu = _weight_grad(h2, dgu, D_FF // 2, tm, "w_gu_grad")
    do1, dmerged, sm_g1 = _out_proj_bwd(dx1, o1, mod, w_out, tm)
    dw_out = _weight_grad(merged, do1, D_MODEL, tm, "w_out_grad")
    dq, dkv, dgb, dgc, dxc, dbias, dsink, sm_mix = _mixer_bwd(
        q, kv, gb, gc, xc, bias, sinks, conv_w, g_attn, g_conv, attn, lse, dmerged)
    dproj, dx, sm_1 = _in_proj_bwd(dq, dkv, dgb, dgc, dxc, x, dx1, mod, g_norm1, w_in, tm)
    dw_in = _weight_grad(h, dproj, IN_PROJ_WIDTH // 2, tm, "w_in_grad")
    d_rel = _rel_bias_grad(dbias, bucket)

    packed = jnp.concatenate([
        sm_1[0], sm_1[1], sm_g1[0], sm_2[0], sm_2[1], sm_g2[0],
        d_rel[:, 0],
        sm_1[2],
        sm_mix[5, 0:128],
        sm_mix[0], sm_mix[1],
        sm_2[2],
        fin[0],
        sm_mix[2], sm_mix[3], sm_mix[4],
        fin[2, 0:128],
    ])[None, :]
    return dx, dw_in, dw_out, dw_gu, dw_down, packed


def _to_shards_cols(g):
    r, c8 = g.shape
    return g.reshape(r, N_DEV, c8 // N_DEV).transpose(1, 0, 2).astype(BF16)


def _to_shards_rows(g):
    r8, c = g.shape
    return g.reshape(N_DEV, r8 // N_DEV, c).astype(BF16)


def kernel(x, c, rel_bias, w_ada, b_ada, g_norm1, w_in, sinks, conv_w, g_attn_out, g_conv_out, w_out, g_norm2, w_gu, w_down, g_final, loss_target, m_rel_bias, m_w_ada, m_b_ada, m_g_norm1, m_w_in, m_sinks, m_conv_w, m_g_attn_out, m_g_conv_out, m_w_out, m_g_norm2, m_w_gu, m_w_down, m_g_final, v_rel_bias, v_w_ada, v_b_ada, v_g_norm1, v_w_in, v_sinks, v_conv_w, v_g_attn_out, v_g_conv_out, v_w_out, v_g_norm2, v_w_gu, v_w_down, v_g_final):
    me = _linear(_mesh_position())
    ada_cols = w_ada.shape[2]

    cond = _silu_rows(c)
    cond_all, conv_w_all = _all_gather([cond, conv_w[0]], "gather_cond", to_bf16=False, big=False)
    cond_all = cond_all[:, 0, :]
    conv_cols = conv_w.shape[2]
    conv_w_full = conv_w_all.transpose(1, 0, 2).reshape(3, CONV_WIDTH)
    b_cols = lax.dynamic_slice_in_dim(b_ada, me * ada_cols, ada_cols, axis=1)
    mod_cols = _mod_columns(cond_all, w_ada[0], b_cols)
    mod_all = _all_gather([mod_cols], "gather_mod", to_bf16=False, big=False)[0]
    mod = lax.dynamic_index_in_dim(mod_all, me, axis=1, keepdims=False).reshape(N_MOD, D_MODEL)
    mod = jnp.concatenate([mod, jnp.zeros((2, D_MODEL), F32)], axis=0)

    gw_in, gw_out, gw_gu, gw_down = _all_gather([w_in[0], w_out[0], w_gu[0], w_down[0]], "gather_weights",
                                                to_bf16=True, big=True)
    w_in_f = gw_in.transpose(1, 0, 2).reshape(D_MODEL, IN_PROJ_WIDTH)
    w_gu_f = gw_gu.transpose(1, 0, 2).reshape(D_MODEL, 2 * D_FF)
    w_out_f = gw_out.reshape(D_MODEL, D_MODEL)
    w_down_f = gw_down.reshape(D_FF, D_MODEL)

    dx, dw_in, dw_out, dw_gu, dw_down, packed = _local_step(
        x[0], loss_target[0], mod, w_in_f, w_out_f, w_gu_f, w_down_f, rel_bias, g_norm1, sinks[0], conv_w_full,
        g_attn_out, g_conv_out, g_norm2, g_final[None, :])

    parts = _shard_exchange([_to_shards_cols(dw_in), _to_shards_rows(dw_out), _to_shards_cols(dw_gu),
                             _to_shards_rows(dw_down)], "exchange_weight_grads")
    g_in, d_in, nm_in, nv_in = _adamw(w_in[0], m_w_in[0], v_w_in[0], parts[0], 256, "adamw_w_in")
    g_out, d_out, nm_out, nv_out = _adamw(w_out[0], m_w_out[0], v_w_out[0], parts[1], 128, "adamw_w_out")
    g_gu, d_gu, nm_gu, nv_gu = _adamw(w_gu[0], m_w_gu[0], v_w_gu[0], parts[2], 256, "adamw_w_gu")
    g_down, d_down, nm_down, nv_down = _adamw(w_down[0], m_w_down[0], v_w_down[0], parts[3], 176, "adamw_w_down")

    packed_all = _all_gather([packed], "gather_small_grads", to_bf16=False, big=False)[0]
    small = _sum_slots(packed_all)[0]
    dmod_all = packed_all[:, 0, OFF_DMOD:OFF_DMOD + N_MOD * D_MODEL]
    dmod_cols = lax.dynamic_slice_in_dim(dmod_all, me * ada_cols, ada_cols, axis=1)
    cond_t = jnp.zeros((D_MODEL, 128), F32).at[:, 0:N_DEV].set(cond_all.T)
    dmod_pad = jnp.zeros((128, ada_cols), F32).at[0:N_DEV, :].set(dmod_cols)
    g_ada = _w_ada_grad(cond_t, dmod_pad)
    g_ada, d_ada, nm_ada, nv_ada = _adamw(w_ada[0], m_w_ada[0], v_w_ada[0], g_ada, 256, "adamw_w_ada")

    loss = small[OFF_LOSS]
    seg = lambda off, n: small[off:off + n]
    conv_g_full = seg(OFF_CONVW, 3 * CONV_WIDTH).reshape(3, CONV_WIDTH)
    small_grads = {
        "rel_bias": seg(OFF_RELB, 256).reshape(N_BUCKETS, N_Q_HEADS),
        "b_ada": seg(OFF_DMOD, N_MOD * D_MODEL).reshape(1, N_MOD * D_MODEL),
        "g_norm1": seg(OFF_GN1, D_MODEL).reshape(1, D_MODEL),
        "sinks": seg(OFF_SINK, N_Q_HEADS).reshape(1, N_Q_HEADS),
        "conv_w": lax.dynamic_slice_in_dim(conv_g_full, me * conv_cols, conv_cols, axis=1)[None],
        "g_attn_out": seg(OFF_GATT, ATTN_WIDTH).reshape(1, ATTN_WIDTH),
        "g_conv_out": seg(OFF_GCV, CONV_WIDTH).reshape(1, CONV_WIDTH),
        "g_norm2": seg(OFF_GN2, D_MODEL).reshape(1, D_MODEL),
        "g_final": seg(OFF_GFIN, D_MODEL),
    }
    small_state = {
        "rel_bias": (rel_bias, m_rel_bias, v_rel_bias), "b_ada": (b_ada, m_b_ada, v_b_ada),
        "g_norm1": (g_norm1, m_g_norm1, v_g_norm1), "sinks": (sinks, m_sinks, v_sinks),
        "conv_w": (conv_w, m_conv_w, v_conv_w), "g_attn_out": (g_attn_out, m_g_attn_out, v_g_attn_out),
        "g_conv_out": (g_conv_out, m_g_conv_out, v_g_conv_out), "g_norm2": (g_norm2, m_g_norm2, v_g_norm2),
        "g_final": (g_final, m_g_final, v_g_final),
    }
    names = list(small_grads)
    sizes = [small_grads[k].size for k in names]
    total = sum(sizes)
    padded = -(-total // 1024) * 1024

    def pack(arrs):
        flat = jnp.concatenate([a.reshape(-1) for a in arrs] + [jnp.ones((padded - total,), F32)])
        return flat.reshape(padded // 128, 128)

    sw = pack([small_state[k][0] for k in names])
    sm = pack([small_state[k][1] for k in names])
    sv = pack([small_state[k][2] for k in names])
    sg = pack([small_grads[k] for k in names])
    _, sd, snm, snv = _adamw(sw, sm, sv, sg, padded // 128, "adamw_small")

    def unpack(flat2d):
        flat = flat2d.reshape(-1)
        out, off = {}, 0
        for k, n in zip(names, sizes):
            out[k] = flat[off:off + n].reshape(small_grads[k].shape)
            off += n
        return out

    sd, snm, snv = unpack(sd), unpack(snm), unpack(snv)

    big = {
        "w_ada": (g_ada[None], d_ada[None], nm_ada[None], nv_ada[None]),
        "w_in": (g_in[None], d_in[None], nm_in[None], nv_in[None]),
        "w_out": (g_out[None], d_out[None], nm_out[None], nv_out[None]),
        "w_gu": (g_gu[None], d_gu[None], nm_gu[None], nv_gu[None]),
        "w_down": (g_down[None], d_down[None], nm_down[None], nv_down[None]),
    }
    order = ["rel_bias", "w_ada", "b_ada", "g_norm1", "w_in", "sinks", "conv_w", "g_attn_out", "g_conv_out", "w_out",
             "g_norm2", "w_gu", "w_down", "g_final"]
    grads = [big[k][0] if k in big else small_grads[k] for k in order]
    deltas = [big[k][1] if k in big else sd[k] for k in order]
    new_m = [big[k][2] if k in big else snm[k] for k in order]
    new_v = [big[k][3] if k in big else snv[k] for k in order]
    return (loss, dx[None], *grads, *deltas, *new_m, *new_v)
```

```python
import functools
import math

import jax
import jax.numpy as jnp
from jax import lax
from jax.experimental import pallas as pl
from jax.experimental.pallas import tpu as pltpu

F32 = jnp.float32
BF16 = jnp.bfloat16

D_MODEL = 1024
HEAD_DIM = 64
N_Q_HEADS = 8
ATTN_WIDTH = 512
KV_WIDTH = 128
CONV_WIDTH = 512
IN_PROJ_WIDTH = 2304
D_FF = 2816
N_MOD = 6
N_BUCKETS = 32
MAX_DISTANCE = 128
BLOCK = 128
EPS = 1e-6
NEG_INF = -1e30
SCALE = HEAD_DIM ** -0.5
N_DEV = 8

ADAM_LR = 0.001
ADAM_B1 = 0.9
ADAM_B2 = 0.999
ADAM_EPS = 1e-08
ADAM_WD = 0.01
ADAM_STEP = 10

SH1, SC1, G1, SH2, SC2, G2 = range(6)

VMEM_LIMIT_LARGE = 56 * 1024 * 1024
MESH_ID = pl.DeviceIdType.MESH

OFF_DMOD = 0
OFF_RELB = OFF_DMOD + N_MOD * D_MODEL
OFF_GN1 = OFF_RELB + N_BUCKETS * N_Q_HEADS
OFF_SINK = OFF_GN1 + D_MODEL
OFF_GATT = OFF_SINK + 128
OFF_GCV = OFF_GATT + ATTN_WIDTH
OFF_GN2 = OFF_GCV + CONV_WIDTH
OFF_GFIN = OFF_GN2 + D_MODEL
OFF_CONVW = OFF_GFIN + D_MODEL
OFF_LOSS = OFF_CONVW + 3 * CONV_WIDTH
PACKED = OFF_LOSS + 128


def _params(sem=None, vmem=None):
    return pltpu.CompilerParams(dimension_semantics=sem, vmem_limit_bytes=vmem)


def _full(shape):
    nd = len(shape)
    return pl.BlockSpec(shape, lambda *_: (0,) * nd)


def _rows(tm, width):
    return pl.BlockSpec((tm, width), lambda i, *_: (i, 0))


def _sigmoid(x):
    return 1.0 / (1.0 + jnp.exp(-x))


def _rsqrt_mean_sq(x):
    return lax.rsqrt(jnp.mean(x * x, axis=-1, keepdims=True) + EPS)


def _colsum(x):
    return jnp.sum(x, axis=0, keepdims=True)


def _dot(a, b):
    return jnp.dot(a, b, preferred_element_type=F32)


def _dot_nt(a, b):
    return lax.dot_general(a, b, (((1,), (1,)), ((), ())), preferred_element_type=F32)


def _dot_tn(a, b):
    return lax.dot_general(a, b, (((0,), (0,)), ((), ())), preferred_element_type=F32)


def _mesh_position():
    return lax.axis_index("x"), lax.axis_index("y"), lax.axis_index("c")


def _linear(p):
    return 4 * p[0] + 2 * p[1] + p[2]


def _all_gather(arrs, name, to_bf16, big):
    n = len(arrs)
    out_dtype = BF16 if to_bf16 else F32

    def body(*refs):
        in_refs, out_refs = refs[:n], refs[n:2 * n]
        rest = refs[2 * n:]
        if to_bf16:
            stage, rest = rest[:n], rest[n:]
            for a in range(n):
                stage[a][...] = in_refs[a][...].astype(BF16)
            srcs = stage
        else:
            srcs = in_refs
        send_sems, recv_sems, local_sems = rest
        x, y, c = _mesh_position()
        me, sibling = (x, y, c), (x, y, 1 - c)
        chips = [(1 - x, y), (x, 1 - y), (1 - x, 1 - y)]

        def slot(a, p):
            return out_refs[a].at[_linear(p)]

        def copy(k, a, block, to, src=None):
            return pltpu.make_async_remote_copy(
                src_ref=slot(a, block) if src is None else src,
                dst_ref=slot(a, block),
                send_sem=send_sems.at[k * n + a],
                recv_sem=recv_sems.at[k * n + a],
                device_id=to,
                device_id_type=MESH_ID,
            )

        mine = [pltpu.make_async_copy(srcs[a], slot(a, me), local_sems.at[a]) for a in range(n)]
        for cp in mine:
            cp.start()
        first = [copy(0, a, me, sibling, src=srcs[a]) for a in range(n)]
        for j, chip in enumerate(chips):
            first += [copy(1 + j, a, me, (*chip, c), src=srcs[a]) for a in range(n)]
        for cp in first:
            cp.start()
        passed = []
        for j, chip in enumerate(chips):
            for a in range(n):
                copy(1 + j, a, (*chip, c), me).wait_recv()
                fwd = copy(4 + j, a, (*chip, c), sibling)
                fwd.start()
                passed.append(fwd)
        for a in range(n):
            copy(0, a, sibling, me).wait_recv()
        for j, chip in enumerate(chips):
            for a in range(n):
                copy(4 + j, a, (*chip, 1 - c), me).wait_recv()
        for cp in first + passed:
            cp.wait_send()
        for cp in mine:
            cp.wait()

    vmem = pl.BlockSpec(memory_space=pltpu.VMEM)
    out_space = pl.BlockSpec(memory_space=pl.ANY) if big else vmem
    scratch = [pltpu.VMEM(a.shape, BF16) for a in arrs] if to_bf16 else []
    scratch += [pltpu.SemaphoreType.DMA((7 * n,)), pltpu.SemaphoreType.DMA((7 * n,)),
                pltpu.SemaphoreType.DMA((n,))]
    outs = pl.pallas_call(
        body, name=name,
        out_shape=[jax.ShapeDtypeStruct((N_DEV,) + a.shape, out_dtype) for a in arrs],
        in_specs=[vmem] * n, out_specs=[out_space] * n,
        scratch_shapes=scratch,
        compiler_params=_params(vmem=VMEM_LIMIT_LARGE if big else None),
    )(*arrs)
    return list(outs)


def _peer(k):
    x, y, c = _mesh_position()
    return (1 - x if k & 4 else x, 1 - y if k & 2 else y, 1 - c if k & 1 else c)


HBM_SPEC = pl.BlockSpec(memory_space=pltpu.HBM)
SEM_SPEC = pl.BlockSpec(memory_space=pltpu.SEMAPHORE)
DATAFLOW = pltpu.SideEffectType.DATAFLOW_SIDE_EFFECTING


def _exchange_start(src, name):
    r, c = src.shape[1:]

    def body(src_ref, land_ref, send_sems, recv_sems, src_thru, land_thru, token):
        for k in range(1, N_DEV):
            peer = _peer(k)
            pltpu.make_async_remote_copy(
                src_ref=src_ref.at[_linear(peer)], dst_ref=land_ref.at[k - 1],
                send_sem=send_sems.at[k - 1], recv_sem=recv_sems.at[k - 1],
                device_id=peer, device_id_type=MESH_ID).start()
        token[...] = jnp.zeros_like(token)

    land = lax.empty((N_DEV - 1, r, c), src.dtype)
    return pl.pallas_call(
        body, name=name,
        out_shape=(pltpu.SemaphoreType.DMA((N_DEV - 1,)), pltpu.SemaphoreType.DMA((N_DEV - 1,)),
                   pltpu.HBM(src.shape, src.dtype), pltpu.HBM(land.shape, land.dtype),
                   jax.ShapeDtypeStruct((8, 128), F32)),
        in_specs=(HBM_SPEC, HBM_SPEC),
        out_specs=(SEM_SPEC, SEM_SPEC, HBM_SPEC, HBM_SPEC, pl.BlockSpec(memory_space=pltpu.VMEM)),
        input_output_aliases={0: 2, 1: 3},
        compiler_params=pltpu.CompilerParams(has_side_effects=DATAFLOW),
    )(pltpu.with_memory_space_constraint(src, pltpu.HBM), pltpu.with_memory_space_constraint(land, pltpu.HBM))


def _exchange_wait(started, after, name):
    send_sems, recv_sems, src_thru, land_thru, _ = started

    def body(src_ref, land_ref, send_sems, recv_sems, after_ref, src_out, land_out):
        for k in range(1, N_DEV):
            cp = pltpu.make_async_remote_copy(
                src_ref=src_ref.at[0], dst_ref=land_ref.at[k - 1],
                send_sem=send_sems.at[k - 1], recv_sem=recv_sems.at[k - 1],
                device_id=_peer(k), device_id_type=MESH_ID)
            cp.wait_send()
            cp.wait_recv()

    return pl.pallas_call(
        body, name=name,
        out_shape=(pltpu.HBM(src_thru.shape, src_thru.dtype), pltpu.HBM(land_thru.shape, land_thru.dtype)),
        in_specs=(HBM_SPEC, HBM_SPEC, SEM_SPEC, SEM_SPEC, pl.BlockSpec(memory_space=pl.ANY)),
        out_specs=(HBM_SPEC, HBM_SPEC), input_output_aliases={0: 0, 1: 1},
        compiler_params=pltpu.CompilerParams(has_side_effects=DATAFLOW),
    )(src_thru, land_thru, send_sems, recv_sems, after)


def _silu_rows(c):
    def body(c_ref, o_ref):
        v = c_ref[...]
        o_ref[...] = v * _sigmoid(v)

    return pl.pallas_call(body, name="cond_silu", out_shape=jax.ShapeDtypeStruct(c.shape, F32))(c)


def _mod_columns(cond_all, w_ada, b_cols):
    def body(c_ref, w_ref, b_ref, o_ref):
        o_ref[...] = _dot(c_ref[...], w_ref[...]) + b_ref[...]

    return pl.pallas_call(body, name="mod_columns",
                          out_shape=jax.ShapeDtypeStruct((N_DEV, w_ada.shape[1]), F32))(cond_all, w_ada, b_cols)


def _in_proj(x, mod, g_norm1, w_in, tm):
    s = x.shape[0]

    def body(x_ref, mod_ref, g_ref, w_ref, h_ref, q_ref, kv_ref, gb_ref, gc_ref, xc_ref):
        xf = x_ref[...]
        n = xf * _rsqrt_mean_sq(xf) * g_ref[...]
        h = (n * (1.0 + mod_ref[SC1:SC1 + 1, :]) + mod_ref[SH1:SH1 + 1, :]).astype(BF16)
        h_ref[...] = h
        p = _dot_nt(h, w_ref[...])
        q_ref[...] = p[:, 0:512].astype(BF16)
        kv_ref[...] = p[:, 512:768].astype(BF16)
        gb_ref[...] = p[:, 768:1280]
        gc_ref[...] = p[:, 1280:1792]
        xc_ref[...] = p[:, 1792:2304]

    return pl.pallas_call(
        body, name="in_proj", grid=(s // tm,),
        in_specs=[_rows(tm, D_MODEL), _full((8, D_MODEL)), _full((1, D_MODEL)), _full((IN_PROJ_WIDTH, D_MODEL))],
        out_specs=[_rows(tm, D_MODEL), _rows(tm, 512), _rows(tm, 256), _rows(tm, 512), _rows(tm, 512), _rows(tm, 512)],
        out_shape=[jax.ShapeDtypeStruct((s, D_MODEL), BF16), jax.ShapeDtypeStruct((s, 512), BF16),
                   jax.ShapeDtypeStruct((s, 256), BF16), jax.ShapeDtypeStruct((s, 512), F32),
                   jax.ShapeDtypeStruct((s, 512), F32), jax.ShapeDtypeStruct((s, 512), F32)],
        compiler_params=_params(("arbitrary",), VMEM_LIMIT_LARGE),
    )(x, mod, g_norm1, w_in)


def _t5_bucket(dist):
    max_exact = N_BUCKETS // 2
    is_small = dist < max_exact
    d = jnp.maximum(dist, 1).astype(F32)
    large = max_exact + (jnp.log(d / max_exact) / math.log(MAX_DISTANCE / max_exact)
                         * (N_BUCKETS - max_exact)).astype(jnp.int32)
    large = jnp.minimum(large, N_BUCKETS - 1)
    return jnp.where(is_small, dist, large)


def _bucket_table():
    qi = jnp.arange(BLOCK, dtype=jnp.int32)[:, None]
    sj = jnp.arange(2 * BLOCK, dtype=jnp.int32)[None, :]
    return _t5_bucket(jnp.maximum(qi + BLOCK - sj, 0))


def _window_mask():
    qi = lax.broadcasted_iota(jnp.int32, (BLOCK, 2 * BLOCK), 0)
    sj = lax.broadcasted_iota(jnp.int32, (BLOCK, 2 * BLOCK), 1)
    dist = qi + BLOCK - sj
    return (dist >= 0) & (dist < BLOCK)


def _bias_table(rel_bias, bucket):
    def body(rb_ref, bk_ref, o_ref):
        bk = bk_ref[...]
        inside = _window_mask()
        for h in range(N_Q_HEADS):
            acc = jnp.zeros((BLOCK, 2 * BLOCK), F32)
            for b in range(N_BUCKETS):
                acc = jnp.where(bk == b, rb_ref[b, h], acc)
            o_ref[h] = jnp.where(inside, acc, NEG_INF)

    return pl.pallas_call(
        body, name="bias_table",
        in_specs=[pl.BlockSpec(memory_space=pltpu.SMEM), pl.BlockSpec(memory_space=pltpu.VMEM)],
        out_shape=jax.ShapeDtypeStruct((N_Q_HEADS, BLOCK, 2 * BLOCK), F32),
    )(rel_bias, bucket)


def _load_kv_window(kv_ref, n):
    prev = jnp.maximum(n - 1, 0)
    kvw = jnp.concatenate([kv_ref[pl.ds(pl.multiple_of(prev * BLOCK, BLOCK), BLOCK), :],
                           kv_ref[pl.ds(pl.multiple_of(n * BLOCK, BLOCK), BLOCK), :]], axis=0)
    k, v = kvw[:, 0:128], kvw[:, 128:256]
    k_sw = pltpu.roll(k.astype(F32), 64, 1).astype(BF16)
    v_sw = pltpu.roll(v.astype(F32), 64, 1).astype(BF16)
    return (k, k_sw), (v, v_sw)


def _conv_taps(gc, xc, gc_prev, xc_prev, n):
    u = gc * xc
    before = jnp.where(n > 0, gc_prev * xc_prev, 0.0)
    row = lax.broadcasted_iota(jnp.int32, u.shape, 0)
    u1 = jnp.where(row == 0, before[7:8, :], pltpu.roll(u, 1, 0))
    u2 = jnp.where(row == 0, before[6:7, :], jnp.where(row == 1, before[7:8, :], pltpu.roll(u, 2, 0)))
    return u, u1, u2


def _mixer_fwd(q, kv, gb, gc, xc, bias, sinks, conv_w, g_attn, g_conv):
    s = q.shape[0]
    nb = s // BLOCK

    def body(sink_ref, q_ref, kv_ref, gb_ref, gc_ref, xc_ref, gcp_ref, xcp_ref, bias_ref, cw_ref, ga_ref, gcv_ref,
             attn_ref, merged_ref, lse_ref):
        n = pl.program_id(0)
        ks, vs = _load_kv_window(kv_ref, n)
        lane = lax.broadcasted_iota(jnp.int32, (BLOCK, BLOCK), 1)
        low = lane < HEAD_DIM
        col = lax.broadcasted_iota(jnp.int32, (BLOCK, 2 * BLOCK), 1)
        no_prev = (col < BLOCK) & (n == 0)
        lse_all = jnp.zeros((BLOCK, BLOCK), F32)
        pairs = []
        for p in range(4):
            qp = q_ref[:, 128 * p:128 * (p + 1)].astype(F32)
            kvh = p // 2
            res = []
            for e in range(2):
                h = 2 * p + e
                qm = jnp.where(low if e == 0 else ~low, qp, 0.0).astype(BF16)
                sw = 0 if kvh == e else 1
                sc = _dot_nt(qm, ks[sw]) * SCALE + bias_ref[h]
                sc = jnp.where(no_prev, NEG_INF, sc)
                sink = sink_ref[h]
                m = jnp.maximum(jnp.max(sc, axis=-1, keepdims=True), sink)
                pe = jnp.exp(sc - m)
                den = jnp.sum(pe, axis=-1, keepdims=True) + jnp.exp(sink - m)
                res.append(_dot(pe.astype(BF16), vs[sw]) / den)
                lse_all = lse_all + jnp.where(lane == h, m + jnp.log(den), 0.0)
            pairs.append(jnp.where(low, res[0], res[1]))
        attn = jnp.concatenate(pairs, axis=1)
        attn_ref[...] = attn
        lse_ref[...] = lse_all
        u, u1, u2 = _conv_taps(gc_ref[...], xc_ref[...], gcp_ref[...], xcp_ref[...], n)
        cw = cw_ref[...]
        cv = gb_ref[...] * (cw[0:1, :] * u2 + cw[1:2, :] * u1 + cw[2:3, :] * u)
        an = attn * _rsqrt_mean_sq(attn) * ga_ref[...]
        cn = cv * _rsqrt_mean_sq(cv) * gcv_ref[...]
        merged_ref[...] = jnp.concatenate([an, cn], axis=1).astype(BF16)

    blk = lambda w: pl.BlockSpec((BLOCK, w), lambda n: (n, 0))
    prev8 = pl.BlockSpec((8, 512), lambda n: (jnp.maximum(n * (BLOCK // 8) - 1, 0), 0))
    return pl.pallas_call(
        body, name="mixer_fwd", grid=(nb,),
        in_specs=[pl.BlockSpec(memory_space=pltpu.SMEM), blk(512), _full((s, 256)), blk(512), blk(512), blk(512),
                  prev8, prev8, _full((N_Q_HEADS, BLOCK, 2 * BLOCK)), _full((3, 512)), _full((1, 512)),
                  _full((1, 512))],
        out_specs=[blk(512), blk(1024), blk(128)],
        out_shape=[jax.ShapeDtypeStruct((s, 512), F32), jax.ShapeDtypeStruct((s, 1024), BF16),
                   jax.ShapeDtypeStruct((s, 128), F32)],
        compiler_params=_params(("arbitrary",)),
    )(sinks, q, kv, gb, gc, xc, gc, xc, bias, conv_w, g_attn, g_conv)


def _out_proj(merged, x, mod, w_out, tm):
    s = x.shape[0]

    def body(m_ref, x_ref, mod_ref, w_ref, o_ref, x1_ref):
        o = _dot(m_ref[...], w_ref[...])
        o_ref[...] = o.astype(BF16)
        x1_ref[...] = x_ref[...] + mod_ref[G1:G1 + 1, :] * o

    return pl.pallas_call(
        body, name="out_proj", grid=(s // tm,),
        in_specs=[_rows(tm, D_MODEL), _rows(tm, D_MODEL), _full((8, D_MODEL)), _full((D_MODEL, D_MODEL))],
        out_specs=[_rows(tm, D_MODEL), _rows(tm, D_MODEL)],
        out_shape=[jax.ShapeDtypeStruct((s, D_MODEL), BF16), jax.ShapeDtypeStruct((s, D_MODEL), F32)],
        compiler_params=_params(("arbitrary",)),
    )(merged, x, mod, w_out)


def _ffn_up(x1, mod, g_norm2, w_gu, tm, tn):
    s = x1.shape[0]
    nj = D_FF // tn

    def body(x_ref, mod_ref, g_ref, wg_ref, wu_ref, h_ref, gate_ref, up_ref, act_ref):
        @pl.when(pl.program_id(1) == 0)
        def _():
            xf = x_ref[...]
            n = xf * _rsqrt_mean_sq(xf) * g_ref[...]
            h_ref[...] = (n * (1.0 + mod_ref[SC2:SC2 + 1, :]) + mod_ref[SH2:SH2 + 1, :]).astype(BF16)

        h = h_ref[...]
        gate = _dot_nt(h, wg_ref[...])
        up = _dot_nt(h, wu_ref[...])
        gate_ref[...] = gate.astype(BF16)
        up_ref[...] = up.astype(BF16)
        act_ref[...] = (gate * _sigmoid(gate) * up).astype(BF16)

    tile = pl.BlockSpec((tm, tn), lambda i, j: (i, j))
    return pl.pallas_call(
        body, name="ffn_up", grid=(s // tm, nj),
        in_specs=[_rows(tm, D_MODEL), _full((8, D_MODEL)), _full((1, D_MODEL)),
                  pl.BlockSpec((tn, D_MODEL), lambda i, j: (j, 0)),
                  pl.BlockSpec((tn, D_MODEL), lambda i, j: (j + nj, 0))],
        out_specs=[_rows(tm, D_MODEL), tile, tile, tile],
        out_shape=[jax.ShapeDtypeStruct((s, D_MODEL), BF16)] + [jax.ShapeDtypeStruct((s, D_FF), BF16)] * 3,
        compiler_params=_params(("arbitrary", "arbitrary"), VMEM_LIMIT_LARGE),
    )(x1, mod, g_norm2, w_gu, w_gu)


def _ffn_down_loss(act, x1, mod, w_down, g_final, target, tm):
    s = x1.shape[0]

    def body(a_ref, x1_ref, mod_ref, w_ref, gf_ref, t_ref, o_ref, dx2_ref, small_ref):
        @pl.when(pl.program_id(0) == 0)
        def _():
            small_ref[...] = jnp.zeros_like(small_ref)

        o = _dot(a_ref[...], w_ref[...])
        o_ref[...] = o.astype(BF16)
        x2 = x1_ref[...] + mod_ref[G2:G2 + 1, :] * o
        r = _rsqrt_mean_sq(x2)
        xn = x2 * r
        gf = gf_ref[...]
        err = xn * gf - t_ref[...]
        dy = err * (1.0 / D_MODEL)
        dxn = dy * gf
        dx2_ref[...] = r * (dxn - xn * jnp.mean(dxn * xn, axis=-1, keepdims=True))
        small_ref[0:1, :] += _colsum(dy * xn)
        small_ref[1:2, :] += _colsum(err * err)

        @pl.when(pl.program_id(0) == pl.num_programs(0) - 1)
        def _():
            total = jnp.sum(small_ref[1:2, :], axis=-1, keepdims=True) * (0.5 / D_MODEL)
            small_ref[2:3, :] = jnp.broadcast_to(total, (1, D_MODEL))

    return pl.pallas_call(
        body, name="ffn_down_loss", grid=(s // tm,),
        in_specs=[_rows(tm, D_FF), _rows(tm, D_MODEL), _full((8, D_MODEL)), _full((D_FF, D_MODEL)),
                  _full((1, D_MODEL)), _rows(tm, D_MODEL)],
        out_specs=[_rows(tm, D_MODEL), _rows(tm, D_MODEL), _full((8, D_MODEL))],
        out_shape=[jax.ShapeDtypeStruct((s, D_MODEL), BF16), jax.ShapeDtypeStruct((s, D_MODEL), F32),
                   jax.ShapeDtypeStruct((8, D_MODEL), F32)],
        compiler_params=_params(("arbitrary",), VMEM_LIMIT_LARGE),
    )(act, x1, mod, w_down, g_final, target)


def _ffn_down_bwd(dx2, o2, gate, up, mod, w_down, tm):
    s = dx2.shape[0]

    def body(dx_ref, o_ref, gate_ref, up_ref, mod_ref, w_ref, do_ref, dgu_ref, small_ref):
        @pl.when(pl.program_id(0) == 0)
        def _():
            small_ref[...] = jnp.zeros_like(small_ref)

        dx = dx_ref[...]
        small_ref[0:1, :] += _colsum(dx * o_ref[...].astype(F32))
        do = (dx * mod_ref[G2:G2 + 1, :]).astype(BF16)
        do_ref[...] = do
        dact = _dot_nt(do, w_ref[...])
        gate = gate_ref[...].astype(F32)
        sg = _sigmoid(gate)
        dgu_ref[:, 0:D_FF] = (dact * up_ref[...].astype(F32) * (sg * (1.0 + gate * (1.0 - sg)))).astype(BF16)
        dgu_ref[:, D_FF:2 * D_FF] = (dact * (gate * sg)).astype(BF16)

    return pl.pallas_call(
        body, name="ffn_down_bwd", grid=(s // tm,),
        in_specs=[_rows(tm, D_MODEL), _rows(tm, D_MODEL), _rows(tm, D_FF), _rows(tm, D_FF), _full((8, D_MODEL)),
                  _full((D_FF, D_MODEL))],
        out_specs=[_rows(tm, D_MODEL), _rows(tm, 2 * D_FF), _full((8, D_MODEL))],
        out_shape=[jax.ShapeDtypeStruct((s, D_MODEL), BF16), jax.ShapeDtypeStruct((s, 2 * D_FF), BF16),
                   jax.ShapeDtypeStruct((8, D_MODEL), F32)],
        compiler_params=_params(("arbitrary",), VMEM_LIMIT_LARGE),
    )(dx2, o2, gate, up, mod, w_down)


def _norm_mod_bwd(dh, xf, g, scale_row, small_ref):
    r = _rsqrt_mean_sq(xf)
    xn = xf * r
    small_ref[0:1, :] += _colsum(dh)
    small_ref[1:2, :] += _colsum(dh * (xn * g))
    dn = dh * (1.0 + scale_row)
    small_ref[2:3, :] += _colsum(dn * xn)
    dxn = dn * g
    return r * (dxn - xn * jnp.mean(dxn * xn, axis=-1, keepdims=True))


def _ffn_up_bwd(dgu, x1, dx2, mod, g_norm2, w_gu, tm, tk):
    s = x1.shape[0]
    nk = (2 * D_FF) // tk

    def body(dgu_ref, x_ref, dx2_ref, mod_ref, g_ref, w_ref, dx1_ref, small_ref, acc_ref):
        i, k = pl.program_id(0), pl.program_id(1)

        @pl.when((i == 0) & (k == 0))
        def _():
            small_ref[...] = jnp.zeros_like(small_ref)

        part = _dot(dgu_ref[...], w_ref[...])

        @pl.when(k == 0)
        def _():
            acc_ref[...] = part

        @pl.when(k > 0)
        def _():
            acc_ref[...] += part

        @pl.when(k == nk - 1)
        def _():
            dx1_ref[...] = dx2_ref[...] + _norm_mod_bwd(acc_ref[...], x_ref[...], g_ref[...],
                                                        mod_ref[SC2:SC2 + 1, :], small_ref)

    return pl.pallas_call(
        body, name="ffn_up_bwd", grid=(s // tm, nk),
        in_specs=[pl.BlockSpec((tm, tk), lambda i, k: (i, k)), _rows(tm, D_MODEL), _rows(tm, D_MODEL),
                  _full((8, D_MODEL)), _full((1, D_MODEL)), pl.BlockSpec((tk, D_MODEL), lambda i, k: (k, 0))],
        out_specs=[_rows(tm, D_MODEL), _full((8, D_MODEL))],
        out_shape=[jax.ShapeDtypeStruct((s, D_MODEL), F32), jax.ShapeDtypeStruct((8, D_MODEL), F32)],
        scratch_shapes=[pltpu.VMEM((tm, D_MODEL), F32)],
        compiler_params=_params(("arbitrary", "arbitrary"), VMEM_LIMIT_LARGE),
    )(dgu, x1, dx2, mod, g_norm2, w_gu)


def _out_proj_bwd(dx1, o1, mod, w_out, tm):
    s = dx1.shape[0]

    def body(dx_ref, o_ref, mod_ref, w_ref, do_ref, dm_ref, small_ref):
        @pl.when(pl.program_id(0) == 0)
        def _():
            small_ref[...] = jnp.zeros_like(small_ref)

        dx = dx_ref[...]
        small_ref[0:1, :] += _colsum(dx * o_ref[...].astype(F32))
        do = (dx * mod_ref[G1:G1 + 1, :]).astype(BF16)
        do_ref[...] = do
        dm_ref[...] = _dot_nt(do, w_ref[...])

    return pl.pallas_call(
        body, name="out_proj_bwd", grid=(s // tm,),
        in_specs=[_rows(tm, D_MODEL), _rows(tm, D_MODEL), _full((8, D_MODEL)), _full((D_MODEL, D_MODEL))],
        out_specs=[_rows(tm, D_MODEL), _rows(tm, D_MODEL), _full((8, D_MODEL))],
        out_shape=[jax.ShapeDtypeStruct((s, D_MODEL), BF16), jax.ShapeDtypeStruct((s, D_MODEL), F32),
                   jax.ShapeDtypeStruct((8, D_MODEL), F32)],
        compiler_params=_params(("arbitrary",)),
    )(dx1, o1, mod, w_out)


def _group_norm_bwd(dm, a, g):
    r = _rsqrt_mean_sq(a)
    an = a * r
    dan = dm * g
    return r * (dan - an * jnp.mean(dan * an, axis=-1, keepdims=True)), _colsum(dm * an)


def _mixer_bwd(q, kv, gb, gc, xc, bias, sinks, conv_w, g_attn, g_conv, attn, lse, dmerged):
    s = q.shape[0]
    nb = s // BLOCK

    def body(sink_ref, q_ref, kv_ref, gb_ref, gc_ref, xc_ref, gcp_ref, xcp_ref, bias_ref, cw_ref, ga_ref, gcv_ref,
             attn_ref, lse_ref, dm_ref,
             dq_ref, dkv_ref, dgb_ref, dgc_ref, dxc_ref, dbias_ref, dsink_ref, small_ref, carry_ref):
        step = pl.program_id(0)
        n = nb - 1 - step

        @pl.when(step == 0)
        def _():
            dkv_ref[...] = jnp.zeros_like(dkv_ref)
            dbias_ref[...] = jnp.zeros_like(dbias_ref)
            dsink_ref[...] = jnp.zeros_like(dsink_ref)
            small_ref[...] = jnp.zeros_like(small_ref)
            carry_ref[...] = jnp.zeros_like(carry_ref)

        dm = dm_ref[...]
        gbv, gcv_, xcv = gb_ref[...], gc_ref[...], xc_ref[...]
        u, u1, u2 = _conv_taps(gcv_, xcv, gcp_ref[...], xcp_ref[...], n)
        cw = cw_ref[...]
        yv = cw[0:1, :] * u2 + cw[1:2, :] * u1 + cw[2:3, :] * u
        dcv, dg_conv = _group_norm_bwd(dm[:, 512:1024], gbv * yv, gcv_ref[...])
        small_ref[1:2, :] += dg_conv
        dgb_ref[...] = (dcv * yv).astype(BF16)
        dy = dcv * gbv
        nxt = carry_ref[...]
        row = lax.broadcasted_iota(jnp.int32, dy.shape, 0)
        d1 = jnp.where(row == BLOCK - 1, nxt[0:1, :], pltpu.roll(dy, BLOCK - 1, 0))
        d2 = jnp.where(row == BLOCK - 2, nxt[0:1, :],
                       jnp.where(row == BLOCK - 1, nxt[1:2, :], pltpu.roll(dy, BLOCK - 2, 0)))
        du = cw[2:3, :] * dy + cw[1:2, :] * d1 + cw[0:1, :] * d2
        dgc_ref[...] = (du * xcv).astype(BF16)
        dxc_ref[...] = (du * gcv_).astype(BF16)
        small_ref[2:3, :] += _colsum(dy * u2)
        small_ref[3:4, :] += _colsum(dy * u1)
        small_ref[4:5, :] += _colsum(dy * u)
        carry_ref[...] = dy[0:8, :]

        attn_v = attn_ref[...]
        dout, dg_attn = _group_norm_bwd(dm[:, 0:512], attn_v, ga_ref[...])
        small_ref[0:1, :] += dg_attn
        ks, vs = _load_kv_window(kv_ref, n)
        lane = lax.broadcasted_iota(jnp.int32, (BLOCK, BLOCK), 1)
        low = lane < HEAD_DIM
        col = lax.broadcasted_iota(jnp.int32, (BLOCK, 2 * BLOCK), 1)
        no_prev = (col < BLOCK) & (n == 0)
        lse_all = lse_ref[...]
        dk = jnp.zeros((2 * BLOCK, BLOCK), F32)
        dv = jnp.zeros((2 * BLOCK, BLOCK), F32)
        dsink = jnp.zeros((BLOCK, BLOCK), F32)
        dq_pairs = []
        for p in range(4):
            qp = q_ref[:, 128 * p:128 * (p + 1)].astype(F32)
            do_p = dout[:, 128 * p:128 * (p + 1)]
            prod = do_p * attn_v[:, 128 * p:128 * (p + 1)]
            kvh = p // 2
            res = []
            for e in range(2):
                h = 2 * p + e
                half = low if e == 0 else ~low
                qm = jnp.where(half, qp, 0.0).astype(BF16)
                dom = jnp.where(half, do_p, 0.0).astype(BF16)
                delta = jnp.sum(jnp.where(half, prod, 0.0), axis=-1, keepdims=True)
                lse_h = jnp.sum(jnp.where(lane == h, lse_all, 0.0), axis=-1, keepdims=True)
                sw = 0 if kvh == e else 1
                sc = _dot_nt(qm, ks[sw]) * SCALE + bias_ref[h]
                sc = jnp.where(no_prev, NEG_INF, sc)
                pr = jnp.exp(sc - lse_h)
                dp = _dot_nt(dom, vs[sw])
                ds = pr * (dp - delta)
                dbias_ref[h] += ds
                dsink = dsink + jnp.where(lane == h, -jnp.exp(sink_ref[h] - lse_h) * delta, 0.0)
                dsb = ds.astype(BF16)
                res.append(_dot(dsb, ks[sw]) * SCALE)
                dk_h = _dot_tn(dsb, qm) * SCALE
                dv_h = _dot_tn(pr.astype(BF16), dom)
                if sw:
                    dk_h = pltpu.roll(dk_h, 64, 1)
                    dv_h = pltpu.roll(dv_h, 64, 1)
                dk = dk + dk_h
                dv = dv + dv_h
            dq_pairs.append(jnp.where(low, res[0], res[1]))
        dq_ref[...] = jnp.concatenate(dq_pairs, axis=1).astype(BF16)
        dsink_ref[...] += dsink
        dkv_win = jnp.concatenate([dk, dv], axis=1)
        prev = jnp.maximum(n - 1, 0)
        dkv_ref[pl.ds(pl.multiple_of(prev * BLOCK, BLOCK), BLOCK), :] += dkv_win[0:BLOCK, :]
        dkv_ref[pl.ds(pl.multiple_of(n * BLOCK, BLOCK), BLOCK), :] += dkv_win[BLOCK:2 * BLOCK, :]

        @pl.when(step == nb - 1)
        def _():
            small_ref[5:6, :] = jnp.concatenate([_colsum(dsink_ref[...]), jnp.zeros((1, 512 - BLOCK), F32)], axis=1)

    blk = lambda w: pl.BlockSpec((BLOCK, w), lambda t: (nb - 1 - t, 0))
    prev8 = pl.BlockSpec((8, 512), lambda t: (jnp.maximum((nb - 1 - t) * (BLOCK // 8) - 1, 0), 0))
    bf = lambda w: jax.ShapeDtypeStruct((s, w), BF16)
    return pl.pallas_call(
        body, name="mixer_bwd", grid=(nb,),
        in_specs=[pl.BlockSpec(memory_space=pltpu.SMEM), blk(512), _full((s, 256)), blk(512), blk(512), blk(512),
                  prev8, prev8, _full((N_Q_HEADS, BLOCK, 2 * BLOCK)), _full((3, 512)), _full((1, 512)),
                  _full((1, 512)), blk(512), blk(128), blk(1024)],
        out_specs=[blk(512), _full((s, 256)), blk(512), blk(512), blk(512), _full((N_Q_HEADS, BLOCK, 2 * BLOCK)),
                   _full((BLOCK, BLOCK)), _full((8, 512))],
        out_shape=[bf(512), jax.ShapeDtypeStruct((s, 256), F32), bf(512), bf(512), bf(512),
                   jax.ShapeDtypeStruct((N_Q_HEADS, BLOCK, 2 * BLOCK), F32), jax.ShapeDtypeStruct((BLOCK, BLOCK), F32),
                   jax.ShapeDtypeStruct((8, 512), F32)],
        scratch_shapes=[pltpu.VMEM((8, 512), F32)],
        compiler_params=_params(("arbitrary",), VMEM_LIMIT_LARGE),
    )(sinks, q, kv, gb, gc, xc, gc, xc, bias, conv_w, g_attn, g_conv, attn, lse, dmerged)


def _in_proj_bwd(dq, dkv, dgb, dgc, dxc, x, dx1, mod, g_norm1, w_in, tm):
    s = x.shape[0]

    def body(dq_ref, dkv_ref, dgb_ref, dgc_ref, dxc_ref, x_ref, dx1_ref, mod_ref, g_ref, w_ref,
             dproj_ref, dx_ref, small_ref):
        @pl.when(pl.program_id(0) == 0)
        def _():
            small_ref[...] = jnp.zeros_like(small_ref)

        dproj = jnp.concatenate([dq_ref[...], dkv_ref[...].astype(BF16), dgb_ref[...], dgc_ref[...], dxc_ref[...]],
                                axis=1)
        dproj_ref[...] = dproj
        dh = _dot(dproj, w_ref[...])
        dx_ref[...] = dx1_ref[...] + _norm_mod_bwd(dh, x_ref[...], g_ref[...], mod_ref[SC1:SC1 + 1, :], small_ref)

    return pl.pallas_call(
        body, name="in_proj_bwd", grid=(s // tm,),
        in_specs=[_rows(tm, 512), _rows(tm, 256), _rows(tm, 512), _rows(tm, 512), _rows(tm, 512),
                  _rows(tm, D_MODEL), _rows(tm, D_MODEL), _full((8, D_MODEL)), _full((1, D_MODEL)),
                  _full((IN_PROJ_WIDTH, D_MODEL))],
        out_specs=[_rows(tm, IN_PROJ_WIDTH), _rows(tm, D_MODEL), _full((8, D_MODEL))],
        out_shape=[jax.ShapeDtypeStruct((s, IN_PROJ_WIDTH), BF16), jax.ShapeDtypeStruct((s, D_MODEL), F32),
                   jax.ShapeDtypeStruct((8, D_MODEL), F32)],
        compiler_params=_params(("arbitrary",), VMEM_LIMIT_LARGE),
    )(dq, dkv, dgb, dgc, dxc, x, dx1, mod, g_norm1, w_in)


def _weight_grad(a, b, tk, ts, name, after=None):
    s, k = a.shape
    n = b.shape[1]
    nt = s // ts
    extra = [] if after is None else [after]

    def body(a_ref, b_ref, *rest):
        o_ref, acc_ref = rest[-2:]
        t = pl.program_id(1)
        part = _dot_tn(a_ref[...], b_ref[...])

        @pl.when(t == 0)
        def _():
            acc_ref[...] = part

        @pl.when(t > 0)
        def _():
            acc_ref[...] += part

        @pl.when(t == nt - 1)
        def _():
            o_ref[...] = acc_ref[...].astype(BF16)

    return pl.pallas_call(
        body, name=name, grid=(k // tk, nt),
        in_specs=[pl.BlockSpec((ts, tk), lambda i, t: (t, i)), pl.BlockSpec((ts, n), lambda i, t: (t, 0))]
        + [pl.BlockSpec(memory_space=pl.ANY)] * len(extra),
        out_specs=pl.BlockSpec((tk, n), lambda i, t: (i, 0)),
        out_shape=jax.ShapeDtypeStruct((k, n), BF16),
        scratch_shapes=[pltpu.VMEM((tk, n), F32)],
        compiler_params=_params(("arbitrary", "arbitrary"), VMEM_LIMIT_LARGE),
    )(a, b, *extra)


def _rel_bias_grad(dbias, bucket):
    def body(db_ref, bk_ref, o_ref, rows_ref):
        bk = bk_ref[...]
        for b in range(N_BUCKETS):
            sel = (bk == b).astype(F32)
            for h in range(N_Q_HEADS):
                rows_ref[8 * b + h:8 * b + h + 1, :] = _colsum(db_ref[h] * sel)
        o_ref[...] = jnp.sum(rows_ref[...], axis=-1, keepdims=True)

    return pl.pallas_call(
        body, name="rel_bias_grad",
        out_shape=jax.ShapeDtypeStruct((N_BUCKETS * N_Q_HEADS, 1), F32),
        scratch_shapes=[pltpu.VMEM((N_BUCKETS * N_Q_HEADS, 2 * BLOCK), F32)],
    )(dbias, bucket)


def _sum_slots(parts):
    def body(p_ref, o_ref):
        acc = p_ref[0]
        for k in range(1, N_DEV):
            acc = acc + p_ref[k]
        o_ref[...] = acc

    return pl.pallas_call(body, name="sum_small_grads",
                          out_shape=jax.ShapeDtypeStruct(parts.shape[1:], F32))(parts)


def _w_ada_grad(cond_t, dmod_cols):
    def body(c_ref, d_ref, o_ref):
        o_ref[...] = _dot(c_ref[...], d_ref[...])

    return pl.pallas_call(body, name="w_ada_grad",
                          out_shape=jax.ShapeDtypeStruct((cond_t.shape[0], dmod_cols.shape[1]), F32))(cond_t, dmod_cols)


def _adam_math(w, g, m, v):
    m = ADAM_B1 * m + (1.0 - ADAM_B1) * g
    v = ADAM_B2 * v + (1.0 - ADAM_B2) * (g * g)
    m_hat = m / (1.0 - ADAM_B1 ** ADAM_STEP)
    v_hat = v / (1.0 - ADAM_B2 ** ADAM_STEP)
    delta = -ADAM_LR * (m_hat / (jnp.sqrt(v_hat) + ADAM_EPS) + ADAM_WD * w)
    return delta, m, v


def _sum_parts(local, land, me, tr, name):
    r, c = local.shape[1:]

    def body(me_ref, own_ref, land_ref, o_ref):
        acc = own_ref[0].astype(F32)
        for k in range(N_DEV - 1):
            acc = acc + land_ref[k].astype(F32)
        o_ref[...] = acc

    return pl.pallas_call(
        body, name=name,
        grid_spec=pltpu.PrefetchScalarGridSpec(
            num_scalar_prefetch=1, grid=(r // tr,),
            in_specs=[pl.BlockSpec((1, tr, c), lambda i, me_ref: (me_ref[0], i, 0)),
                      pl.BlockSpec((N_DEV - 1, tr, c), lambda i, me_ref: (0, i, 0))],
            out_specs=pl.BlockSpec((tr, c), lambda i, me_ref: (i, 0))),
        out_shape=jax.ShapeDtypeStruct((r, c), F32),
        compiler_params=_params(("arbitrary",)),
    )(me, local, land)


def _adamw(w, m, v, g, tr, name):
    r, c = w.shape

    def body(w_ref, m_ref, v_ref, g_ref, d_ref, mo_ref, vo_ref):
        d_ref[...], mo_ref[...], vo_ref[...] = _adam_math(w_ref[...], g_ref[...], m_ref[...], v_ref[...])

    tile = pl.BlockSpec((tr, c), lambda i: (i, 0))
    return pl.pallas_call(
        body, name=name, grid=(r // tr,),
        in_specs=[tile] * 4, out_specs=[tile] * 3,
        out_shape=[jax.ShapeDtypeStruct((r, c), F32)] * 3,
        compiler_params=_params(("arbitrary",)),
    )(w, m, v, g)


def _behind(a, token):
    return a + token[0:a.shape[0], 0:1]


def _local_step(x, target, mod, w_in_t, w_out, w_gu_t, w_down, rel_bias, g_norm1, sinks, conv_w, g_attn, g_conv,
                g_norm2, g_final, exchange):
    s = x.shape[0]
    tm = min(512, s)
    tm_small = min(256, s)
    bucket = _bucket_table()
    bias = _bias_table(rel_bias, bucket)

    h, q, kv, gb, gc, xc = _in_proj(x, mod, g_norm1, w_in_t, tm)
    attn, merged, lse = _mixer_fwd(q, kv, gb, gc, xc, bias, sinks, conv_w, g_attn, g_conv)
    o1, x1 = _out_proj(merged, x, mod, w_out, tm)
    h2, gate, up, act = _ffn_up(x1, mod, g_norm2, w_gu_t, tm, D_FF // 2)
    o2, dx2, fin = _ffn_down_loss(act, x1, mod, w_down, g_final, target, tm)

    do2, dgu, sm_g2 = _ffn_down_bwd(dx2, o2, gate, up, mod, w_down, tm_small)
    mod = _behind(mod, exchange("w_down", _weight_grad(act, do2, D_FF // 2, tm, "w_down_grad")))
    dx1, sm_2 = _ffn_up_bwd(dgu, x1, dx2, mod, g_norm2, w_gu_t, tm, D_FF // 2)
    mod = _behind(mod, exchange("w_gu", _weight_grad(dgu, h2, D_FF // 2, tm, "w_gu_grad")))
    do1, dmerged, sm_g1 = _out_proj_bwd(dx1, o1, mod, w_out, tm)
    g_attn_b = _behind(g_attn, exchange("w_out", _weight_grad(merged, do1, D_MODEL, tm, "w_out_grad")))
    dq, dkv, dgb, dgc, dxc, dbias, dsink, sm_mix = _mixer_bwd(
        q, kv, gb, gc, xc, bias, sinks, conv_w, g_attn_b, g_conv, attn, lse, dmerged)
    dproj, dx, sm_1 = _in_proj_bwd(dq, dkv, dgb, dgc, dxc, x, dx1, mod, g_norm1, w_in_t, tm)
    d_rel = _rel_bias_grad(dbias, bucket)

    packed = jnp.concatenate([
        sm_1[0], sm_1[1], sm_g1[0], sm_2[0], sm_2[1], sm_g2[0],
        d_rel[:, 0],
        sm_1[2],
        sm_mix[5, 0:128],
        sm_mix[0], sm_mix[1],
        sm_2[2],
        fin[0],
        sm_mix[2], sm_mix[3], sm_mix[4],
        fin[2, 0:128],
    ])[None, :]
    return dx, dproj, h, packed


def kernel(x, c, rel_bias, w_ada, b_ada, g_norm1, w_in, sinks, conv_w, g_attn_out, g_conv_out, w_out, g_norm2, w_gu, w_down, g_final, loss_target, m_rel_bias, m_w_ada, m_b_ada, m_g_norm1, m_w_in, m_sinks, m_conv_w, m_g_attn_out, m_g_conv_out, m_w_out, m_g_norm2, m_w_gu, m_w_down, m_g_final, v_rel_bias, v_w_ada, v_b_ada, v_g_norm1, v_w_in, v_sinks, v_conv_w, v_g_attn_out, v_g_conv_out, v_w_out, v_g_norm2, v_w_gu, v_w_down, v_g_final):
    me = _linear(_mesh_position())
    me_arr = jnp.reshape(me, (1,)).astype(jnp.int32)
    ada_cols = w_ada.shape[2]
    tm = min(512, x.shape[1])

    cond = _silu_rows(c)
    cond_all, conv_w_all = _all_gather([cond, conv_w[0]], "gather_cond", to_bf16=False, big=False)
    cond_all = cond_all[:, 0, :]
    conv_cols = conv_w.shape[2]
    conv_w_full = conv_w_all.transpose(1, 0, 2).reshape(3, CONV_WIDTH)
    b_cols = lax.dynamic_slice_in_dim(b_ada, me * ada_cols, ada_cols, axis=1)
    mod_cols = _mod_columns(cond_all, w_ada[0], b_cols)
    mod_all = _all_gather([mod_cols], "gather_mod", to_bf16=False, big=False)[0]
    mod = lax.dynamic_index_in_dim(mod_all, me, axis=1, keepdims=False).reshape(N_MOD, D_MODEL)
    mod = jnp.concatenate([mod, jnp.zeros((2, D_MODEL), F32)], axis=0)

    gw_in, gw_out, gw_gu, gw_down = _all_gather([w_in[0].T, w_out[0], w_gu[0].T, w_down[0]], "gather_weights",
                                                to_bf16=True, big=True)
    w_in_t = gw_in.reshape(IN_PROJ_WIDTH, D_MODEL)
    w_gu_t = gw_gu.reshape(2 * D_FF, D_MODEL)
    w_out_f = gw_out.reshape(D_MODEL, D_MODEL)
    w_down_f = gw_down.reshape(D_FF, D_MODEL)

    started = {}

    def exchange(name, dw):
        st = _exchange_start(dw.reshape(N_DEV, dw.shape[0] // N_DEV, dw.shape[1]), "exchange_start_" + name)
        started[name] = st
        return st[4]

    dx, dproj, h, packed = _local_step(
        x[0], loss_target[0], mod, w_in_t, w_out_f, w_gu_t, w_down_f, rel_bias, g_norm1, sinks[0], conv_w_full,
        g_attn_out, g_conv_out, g_norm2, g_final[None, :], exchange)

    packed_all = _all_gather([packed], "gather_small_grads", to_bf16=False, big=False)[0]
    exchange("w_in", _weight_grad(dproj, h, IN_PROJ_WIDTH // 2, tm, "w_in_grad", after=packed_all))
    small = _sum_slots(packed_all)[0]
    dmod_all = packed_all[:, 0, OFF_DMOD:OFF_DMOD + N_MOD * D_MODEL]
    dmod_cols = lax.dynamic_slice_in_dim(dmod_all, me * ada_cols, ada_cols, axis=1)
    cond_t = jnp.zeros((D_MODEL, 128), F32).at[:, 0:N_DEV].set(cond_all.T)
    dmod_pad = jnp.zeros((128, ada_cols), F32).at[0:N_DEV, :].set(dmod_cols)
    g_ada = _w_ada_grad(cond_t, dmod_pad)
    d_ada, nm_ada, nv_ada = _adamw(w_ada[0], m_w_ada[0], v_w_ada[0], g_ada, 256, "adamw_w_ada")

    loss = small[OFF_LOSS]
    seg = lambda off, n: small[off:off + n]
    conv_g_full = seg(OFF_CONVW, 3 * CONV_WIDTH).reshape(3, CONV_WIDTH)
    small_grads = {
        "rel_bias": seg(OFF_RELB, 256).reshape(N_BUCKETS, N_Q_HEADS),
        "b_ada": seg(OFF_DMOD, N_MOD * D_MODEL).reshape(1, N_MOD * D_MODEL),
        "g_norm1": seg(OFF_GN1, D_MODEL).reshape(1, D_MODEL),
        "sinks": seg(OFF_SINK, N_Q_HEADS).reshape(1, N_Q_HEADS),
        "conv_w": lax.dynamic_slice_in_dim(conv_g_full, me * conv_cols, conv_cols, axis=1)[None],
        "g_attn_out": seg(OFF_GATT, ATTN_WIDTH).reshape(1, ATTN_WIDTH),
        "g_conv_out": seg(OFF_GCV, CONV_WIDTH).reshape(1, CONV_WIDTH),
        "g_norm2": seg(OFF_GN2, D_MODEL).reshape(1, D_MODEL),
        "g_final": seg(OFF_GFIN, D_MODEL),
    }
    small_state = {
        "rel_bias": (rel_bias, m_rel_bias, v_rel_bias), "b_ada": (b_ada, m_b_ada, v_b_ada),
        "g_norm1": (g_norm1, m_g_norm1, v_g_norm1), "sinks": (sinks, m_sinks, v_sinks),
        "conv_w": (conv_w, m_conv_w, v_conv_w), "g_attn_out": (g_attn_out, m_g_attn_out, v_g_attn_out),
        "g_conv_out": (g_conv_out, m_g_conv_out, v_g_conv_out), "g_norm2": (g_norm2, m_g_norm2, v_g_norm2),
        "g_final": (g_final, m_g_final, v_g_final),
    }
    names = list(small_grads)
    sizes = [small_grads[k].size for k in names]
    total = sum(sizes)
    padded = -(-total // 1024) * 1024

    def pack(arrs):
        flat = jnp.concatenate([a.reshape(-1) for a in arrs] + [jnp.ones((padded - total,), F32)])
        return flat.reshape(padded // 128, 128)

    sw = pack([small_state[k][0] for k in names])
    sm = pack([small_state[k][1] for k in names])
    sv = pack([small_state[k][2] for k in names])
    sg = pack([small_grads[k] for k in names])
    sd, snm, snv = _adamw(sw, sm, sv, sg, padded // 128, "adamw_small")

    def unpack(flat2d):
        flat = flat2d.reshape(-1)
        out, off = {}, 0
        for k, n in zip(names, sizes):
            out[k] = flat[off:off + n].reshape(small_grads[k].shape)
            off += n
        return out

    sd_all = sd
    sd, snm, snv = unpack(sd), unpack(snm), unpack(snv)

    def finish(name, after, tr):
        src, land = _exchange_wait(started[name], after, "exchange_wait_" + name)
        return _sum_parts(src, land, me_arr, tr, "sum_parts_" + name)

    g_down = finish("w_down", sd_all, 176)
    d_down, nm_down, nv_down = _adamw(w_down[0], m_w_down[0], v_w_down[0], g_down, 176, "adamw_w_down")
    g_gu = finish("w_gu", nv_down, 352).T
    d_gu, nm_gu, nv_gu = _adamw(w_gu[0], m_w_gu[0], v_w_gu[0], g_gu, 256, "adamw_w_gu")
    g_out = finish("w_out", nv_gu, 128)
    d_out, nm_out, nv_out = _adamw(w_out[0], m_w_out[0], v_w_out[0], g_out, 128, "adamw_w_out")
    g_in = finish("w_in", nv_out, 144).T
    d_in, nm_in, nv_in = _adamw(w_in[0], m_w_in[0], v_w_in[0], g_in, 256, "adamw_w_in")

    big = {
        "w_ada": (g_ada[None], d_ada[None], nm_ada[None], nv_ada[None]),
        "w_in": (g_in[None], d_in[None], nm_in[None], nv_in[None]),
        "w_out": (g_out[None], d_out[None], nm_out[None], nv_out[None]),
        "w_gu": (g_gu[None], d_gu[None], nm_gu[None], nv_gu[None]),
        "w_down": (g_down[None], d_down[None], nm_down[None], nv_down[None]),
    }
    order = ["rel_bias", "w_ada", "b_ada", "g_norm1", "w_in", "sinks", "conv_w", "g_attn_out", "g_conv_out", "w_out",
             "g_norm2", "w_gu", "w_down", "g_final"]
    grads = [big[k][0] if k in big else small_grads[k] for k in order]
    deltas = [big[k][1] if k in big else sd[k] for k in order]
    new_m = [big[k][2] if k in big else snm[k] for k in order]
    new_v = [big[k][3] if k in big else snv[k] for k in order]
    return (loss, dx[None], *grads, *deltas, *new_m, *new_v)
```

```python
import functools
import math

import jax
import jax.numpy as jnp
from jax import lax
from jax.experimental import pallas as pl
from jax.experimental.pallas import tpu as pltpu

F32 = jnp.float32
BF16 = jnp.bfloat16

D_MODEL = 1024
HEAD_DIM = 64
N_Q_HEADS = 8
ATTN_WIDTH = 512
KV_WIDTH = 128
CONV_WIDTH = 512
IN_PROJ_WIDTH = 2304
D_FF = 2816
N_MOD = 6
N_BUCKETS = 32
MAX_DISTANCE = 128
BLOCK = 128
EPS = 1e-6
NEG_INF = -1e30
SCALE = HEAD_DIM ** -0.5
N_DEV = 8

ADAM_LR = 0.001
ADAM_B1 = 0.9
ADAM_B2 = 0.999
ADAM_EPS = 1e-08
ADAM_WD = 0.01
ADAM_STEP = 10

SH1, SC1, G1, SH2, SC2, G2 = range(6)

VMEM_LIMIT_LARGE = 56 * 1024 * 1024
MESH_ID = pl.DeviceIdType.MESH

OFF_DMOD = 0
OFF_RELB = OFF_DMOD + N_MOD * D_MODEL
OFF_GN1 = OFF_RELB + N_BUCKETS * N_Q_HEADS
OFF_SINK = OFF_GN1 + D_MODEL
OFF_GATT = OFF_SINK + 128
OFF_GCV = OFF_GATT + ATTN_WIDTH
OFF_GN2 = OFF_GCV + CONV_WIDTH
OFF_GFIN = OFF_GN2 + D_MODEL
OFF_CONVW = OFF_GFIN + D_MODEL
OFF_LOSS = OFF_CONVW + 3 * CONV_WIDTH
PACKED = OFF_LOSS + 128


def _params(sem=None, vmem=None):
    return pltpu.CompilerParams(dimension_semantics=sem, vmem_limit_bytes=vmem)


def _full(shape):
    nd = len(shape)
    return pl.BlockSpec(shape, lambda *_: (0,) * nd)


def _rows(tm, width):
    return pl.BlockSpec((tm, width), lambda i, *_: (i, 0))


def _sigmoid(x):
    return 1.0 / (1.0 + jnp.exp(-x))


def _rsqrt_mean_sq(x):
    return lax.rsqrt(jnp.mean(x * x, axis=-1, keepdims=True) + EPS)


def _colsum(x):
    return jnp.sum(x, axis=0, keepdims=True)


def _dot(a, b):
    return jnp.dot(a, b, preferred_element_type=F32)


def _dot_nt(a, b):
    return lax.dot_general(a, b, (((1,), (1,)), ((), ())), preferred_element_type=F32)


def _dot_tn(a, b):
    return lax.dot_general(a, b, (((0,), (0,)), ((), ())), preferred_element_type=F32)


def _mesh_position():
    return lax.axis_index("x"), lax.axis_index("y"), lax.axis_index("c")


def _linear(p):
    return 4 * p[0] + 2 * p[1] + p[2]


def _all_gather(arrs, name, to_bf16, big):
    n = len(arrs)
    out_dtype = BF16 if to_bf16 else F32

    def body(*refs):
        in_refs, out_refs = refs[:n], refs[n:2 * n]
        rest = refs[2 * n:]
        if to_bf16:
            stage, rest = rest[:n], rest[n:]
            for a in range(n):
                stage[a][...] = in_refs[a][...].astype(BF16)
            srcs = stage
        else:
            srcs = in_refs
        send_sems, recv_sems, local_sems = rest
        x, y, c = _mesh_position()
        me, sibling = (x, y, c), (x, y, 1 - c)
        chips = [(1 - x, y), (x, 1 - y), (1 - x, 1 - y)]

        def slot(a, p):
            return out_refs[a].at[_linear(p)]

        def copy(k, a, block, to, src=None):
            return pltpu.make_async_remote_copy(
                src_ref=slot(a, block) if src is None else src,
                dst_ref=slot(a, block),
                send_sem=send_sems.at[k * n + a],
                recv_sem=recv_sems.at[k * n + a],
                device_id=to,
                device_id_type=MESH_ID,
            )

        mine = [pltpu.make_async_copy(srcs[a], slot(a, me), local_sems.at[a]) for a in range(n)]
        for cp in mine:
            cp.start()
        first = [copy(0, a, me, sibling, src=srcs[a]) for a in range(n)]
        for j, chip in enumerate(chips):
            first += [copy(1 + j, a, me, (*chip, c), src=srcs[a]) for a in range(n)]
        for cp in first:
            cp.start()
        passed = []
        for j, chip in enumerate(chips):
            for a in range(n):
                copy(1 + j, a, (*chip, c), me).wait_recv()
                fwd = copy(4 + j, a, (*chip, c), sibling)
                fwd.start()
                passed.append(fwd)
        for a in range(n):
            copy(0, a, sibling, me).wait_recv()
        for j, chip in enumerate(chips):
            for a in range(n):
                copy(4 + j, a, (*chip, 1 - c), me).wait_recv()
        for cp in first + passed:
            cp.wait_send()
        for cp in mine:
            cp.wait()

    vmem = pl.BlockSpec(memory_space=pltpu.VMEM)
    out_space = pl.BlockSpec(memory_space=pl.ANY) if big else vmem
    scratch = [pltpu.VMEM(a.shape, BF16) for a in arrs] if to_bf16 else []
    scratch += [pltpu.SemaphoreType.DMA((7 * n,)), pltpu.SemaphoreType.DMA((7 * n,)),
                pltpu.SemaphoreType.DMA((n,))]
    outs = pl.pallas_call(
        body, name=name,
        out_shape=[jax.ShapeDtypeStruct((N_DEV,) + a.shape, out_dtype) for a in arrs],
        in_specs=[vmem] * n, out_specs=[out_space] * n,
        scratch_shapes=scratch,
        compiler_params=_params(vmem=VMEM_LIMIT_LARGE if big else None),
    )(*arrs)
    return list(outs)


def _peer(k):
    x, y, c = _mesh_position()
    return (1 - x if k & 4 else x, 1 - y if k & 2 else y, 1 - c if k & 1 else c)


HBM_SPEC = pl.BlockSpec(memory_space=pltpu.HBM)
SEM_SPEC = pl.BlockSpec(memory_space=pltpu.SEMAPHORE)
DATAFLOW = pltpu.SideEffectType.DATAFLOW_SIDE_EFFECTING


def _exchange_start(src, name):
    r, c = src.shape[1:]

    def body(src_ref, land_ref, send_sems, recv_sems, src_thru, land_thru, token):
        for k in range(1, N_DEV):
            peer = _peer(k)
            pltpu.make_async_remote_copy(
                src_ref=src_ref.at[_linear(peer)], dst_ref=land_ref.at[k - 1],
                send_sem=send_sems.at[k - 1], recv_sem=recv_sems.at[k - 1],
                device_id=peer, device_id_type=MESH_ID).start()
        token[...] = jnp.zeros_like(token)

    land = lax.empty((N_DEV - 1, r, c), src.dtype)
    return pl.pallas_call(
        body, name=name,
        out_shape=(pltpu.SemaphoreType.DMA((N_DEV - 1,)), pltpu.SemaphoreType.DMA((N_DEV - 1,)),
                   pltpu.HBM(src.shape, src.dtype), pltpu.HBM(land.shape, land.dtype),
                   jax.ShapeDtypeStruct((8, 128), F32)),
        in_specs=(HBM_SPEC, HBM_SPEC),
        out_specs=(SEM_SPEC, SEM_SPEC, HBM_SPEC, HBM_SPEC, pl.BlockSpec(memory_space=pltpu.VMEM)),
        input_output_aliases={0: 2, 1: 3},
        compiler_params=pltpu.CompilerParams(has_side_effects=DATAFLOW),
    )(pltpu.with_memory_space_constraint(src, pltpu.HBM), pltpu.with_memory_space_constraint(land, pltpu.HBM))


def _exchange_wait(started, after, name):
    send_sems, recv_sems, src_thru, land_thru, _ = started

    def body(src_ref, land_ref, send_sems, recv_sems, *rest):
        for k in range(1, N_DEV):
            cp = pltpu.make_async_remote_copy(
                src_ref=src_ref.at[0], dst_ref=land_ref.at[k - 1],
                send_sem=send_sems.at[k - 1], recv_sem=recv_sems.at[k - 1],
                device_id=_peer(k), device_id_type=MESH_ID)
            cp.wait_send()
            cp.wait_recv()

    return pl.pallas_call(
        body, name=name,
        out_shape=(pltpu.HBM(src_thru.shape, src_thru.dtype), pltpu.HBM(land_thru.shape, land_thru.dtype)),
        in_specs=(HBM_SPEC, HBM_SPEC, SEM_SPEC, SEM_SPEC) + (pl.BlockSpec(memory_space=pl.ANY),) * len(after),
        out_specs=(HBM_SPEC, HBM_SPEC), input_output_aliases={0: 0, 1: 1},
        compiler_params=pltpu.CompilerParams(has_side_effects=DATAFLOW),
    )(src_thru, land_thru, send_sems, recv_sems, *after)


def _silu_rows(c):
    def body(c_ref, o_ref):
        v = c_ref[...]
        o_ref[...] = v * _sigmoid(v)

    return pl.pallas_call(body, name="cond_silu", out_shape=jax.ShapeDtypeStruct(c.shape, F32))(c)


def _mod_columns(cond_all, w_ada, b_cols):
    def body(c_ref, w_ref, b_ref, o_ref):
        o_ref[...] = _dot(c_ref[...], w_ref[...]) + b_ref[...]

    return pl.pallas_call(body, name="mod_columns",
                          out_shape=jax.ShapeDtypeStruct((N_DEV, w_ada.shape[1]), F32))(cond_all, w_ada, b_cols)


def _in_proj(x, mod, g_norm1, w_in, tm):
    s = x.shape[0]

    def body(x_ref, mod_ref, g_ref, w_ref, h_ref, q_ref, kv_ref, gb_ref, gc_ref, xc_ref):
        xf = x_ref[...]
        n = xf * _rsqrt_mean_sq(xf) * g_ref[...]
        h = (n * (1.0 + mod_ref[SC1:SC1 + 1, :]) + mod_ref[SH1:SH1 + 1, :]).astype(BF16)
        h_ref[...] = h
        p = _dot_nt(h, w_ref[...])
        q_ref[...] = p[:, 0:512].astype(BF16)
        kv_ref[...] = p[:, 512:768].astype(BF16)
        gb_ref[...] = p[:, 768:1280]
        gc_ref[...] = p[:, 1280:1792]
        xc_ref[...] = p[:, 1792:2304]

    return pl.pallas_call(
        body, name="in_proj", grid=(s // tm,),
        in_specs=[_rows(tm, D_MODEL), _full((8, D_MODEL)), _full((1, D_MODEL)), _full((IN_PROJ_WIDTH, D_MODEL))],
        out_specs=[_rows(tm, D_MODEL), _rows(tm, 512), _rows(tm, 256), _rows(tm, 512), _rows(tm, 512), _rows(tm, 512)],
        out_shape=[jax.ShapeDtypeStruct((s, D_MODEL), BF16), jax.ShapeDtypeStruct((s, 512), BF16),
                   jax.ShapeDtypeStruct((s, 256), BF16), jax.ShapeDtypeStruct((s, 512), F32),
                   jax.ShapeDtypeStruct((s, 512), F32), jax.ShapeDtypeStruct((s, 512), F32)],
        compiler_params=_params(("arbitrary",), VMEM_LIMIT_LARGE),
    )(x, mod, g_norm1, w_in)


def _t5_bucket(dist):
    max_exact = N_BUCKETS // 2
    is_small = dist < max_exact
    d = jnp.maximum(dist, 1).astype(F32)
    large = max_exact + (jnp.log(d / max_exact) / math.log(MAX_DISTANCE / max_exact)
                         * (N_BUCKETS - max_exact)).astype(jnp.int32)
    large = jnp.minimum(large, N_BUCKETS - 1)
    return jnp.where(is_small, dist, large)


def _bucket_table():
    qi = jnp.arange(BLOCK, dtype=jnp.int32)[:, None]
    sj = jnp.arange(2 * BLOCK, dtype=jnp.int32)[None, :]
    return _t5_bucket(jnp.maximum(qi + BLOCK - sj, 0))


def _window_mask():
    qi = lax.broadcasted_iota(jnp.int32, (BLOCK, 2 * BLOCK), 0)
    sj = lax.broadcasted_iota(jnp.int32, (BLOCK, 2 * BLOCK), 1)
    dist = qi + BLOCK - sj
    return (dist >= 0) & (dist < BLOCK)


def _bias_table(rel_bias, bucket):
    def body(rb_ref, bk_ref, o_ref):
        bk = bk_ref[...]
        inside = _window_mask()
        for h in range(N_Q_HEADS):
            acc = jnp.zeros((BLOCK, 2 * BLOCK), F32)
            for b in range(N_BUCKETS):
                acc = jnp.where(bk == b, rb_ref[b, h], acc)
            o_ref[h] = jnp.where(inside, acc, NEG_INF)

    return pl.pallas_call(
        body, name="bias_table",
        in_specs=[pl.BlockSpec(memory_space=pltpu.SMEM), pl.BlockSpec(memory_space=pltpu.VMEM)],
        out_shape=jax.ShapeDtypeStruct((N_Q_HEADS, BLOCK, 2 * BLOCK), F32),
    )(rel_bias, bucket)


def _load_kv_window(kv_ref, n):
    prev = jnp.maximum(n - 1, 0)
    kvw = jnp.concatenate([kv_ref[pl.ds(pl.multiple_of(prev * BLOCK, BLOCK), BLOCK), :],
                           kv_ref[pl.ds(pl.multiple_of(n * BLOCK, BLOCK), BLOCK), :]], axis=0)
    k, v = kvw[:, 0:128], kvw[:, 128:256]
    k_sw = pltpu.roll(k.astype(F32), 64, 1).astype(BF16)
    v_sw = pltpu.roll(v.astype(F32), 64, 1).astype(BF16)
    return (k, k_sw), (v, v_sw)


def _conv_taps(gc, xc, gc_prev, xc_prev, n):
    u = gc * xc
    before = jnp.where(n > 0, gc_prev * xc_prev, 0.0)
    row = lax.broadcasted_iota(jnp.int32, u.shape, 0)
    u1 = jnp.where(row == 0, before[7:8, :], pltpu.roll(u, 1, 0))
    u2 = jnp.where(row == 0, before[6:7, :], jnp.where(row == 1, before[7:8, :], pltpu.roll(u, 2, 0)))
    return u, u1, u2


def _mixer_fwd(q, kv, gb, gc, xc, bias, sinks, conv_w, g_attn, g_conv):
    s = q.shape[0]
    nb = s // BLOCK

    def body(sink_ref, q_ref, kv_ref, gb_ref, gc_ref, xc_ref, gcp_ref, xcp_ref, bias_ref, cw_ref, ga_ref, gcv_ref,
             attn_ref, merged_ref, lse_ref):
        n = pl.program_id(0)
        ks, vs = _load_kv_window(kv_ref, n)
        lane = lax.broadcasted_iota(jnp.int32, (BLOCK, BLOCK), 1)
        low = lane < HEAD_DIM
        col = lax.broadcasted_iota(jnp.int32, (BLOCK, 2 * BLOCK), 1)
        no_prev = (col < BLOCK) & (n == 0)
        lse_all = jnp.zeros((BLOCK, BLOCK), F32)
        pairs = []
        for p in range(4):
            qp = q_ref[:, 128 * p:128 * (p + 1)].astype(F32)
            kvh = p // 2
            res = []
            for e in range(2):
                h = 2 * p + e
                qm = jnp.where(low if e == 0 else ~low, qp, 0.0).astype(BF16)
                sw = 0 if kvh == e else 1
                sc = _dot_nt(qm, ks[sw]) * SCALE + bias_ref[h]
                sc = jnp.where(no_prev, NEG_INF, sc)
                sink = sink_ref[h]
                m = jnp.maximum(jnp.max(sc, axis=-1, keepdims=True), sink)
                pe = jnp.exp(sc - m)
                den = jnp.sum(pe, axis=-1, keepdims=True) + jnp.exp(sink - m)
                res.append(_dot(pe.astype(BF16), vs[sw]) / den)
                lse_all = lse_all + jnp.where(lane == h, m + jnp.log(den), 0.0)
            pairs.append(jnp.where(low, res[0], res[1]))
        attn = jnp.concatenate(pairs, axis=1)
        attn_ref[...] = attn
        lse_ref[...] = lse_all
        u, u1, u2 = _conv_taps(gc_ref[...], xc_ref[...], gcp_ref[...], xcp_ref[...], n)
        cw = cw_ref[...]
        cv = gb_ref[...] * (cw[0:1, :] * u2 + cw[1:2, :] * u1 + cw[2:3, :] * u)
        an = attn * _rsqrt_mean_sq(attn) * ga_ref[...]
        cn = cv * _rsqrt_mean_sq(cv) * gcv_ref[...]
        merged_ref[...] = jnp.concatenate([an, cn], axis=1).astype(BF16)

    blk = lambda w: pl.BlockSpec((BLOCK, w), lambda n: (n, 0))
    prev8 = pl.BlockSpec((8, 512), lambda n: (jnp.maximum(n * (BLOCK // 8) - 1, 0), 0))
    return pl.pallas_call(
        body, name="mixer_fwd", grid=(nb,),
        in_specs=[pl.BlockSpec(memory_space=pltpu.SMEM), blk(512), _full((s, 256)), blk(512), blk(512), blk(512),
                  prev8, prev8, _full((N_Q_HEADS, BLOCK, 2 * BLOCK)), _full((3, 512)), _full((1, 512)),
                  _full((1, 512))],
        out_specs=[blk(512), blk(1024), blk(128)],
        out_shape=[jax.ShapeDtypeStruct((s, 512), F32), jax.ShapeDtypeStruct((s, 1024), BF16),
                   jax.ShapeDtypeStruct((s, 128), F32)],
        compiler_params=_params(("arbitrary",)),
    )(sinks, q, kv, gb, gc, xc, gc, xc, bias, conv_w, g_attn, g_conv)


def _out_proj(merged, x, mod, w_out, tm):
    s = x.shape[0]

    def body(m_ref, x_ref, mod_ref, w_ref, o_ref, x1_ref):
        o = _dot(m_ref[...], w_ref[...])
        o_ref[...] = o.astype(BF16)
        x1_ref[...] = x_ref[...] + mod_ref[G1:G1 + 1, :] * o

    return pl.pallas_call(
        body, name="out_proj", grid=(s // tm,),
        in_specs=[_rows(tm, D_MODEL), _rows(tm, D_MODEL), _full((8, D_MODEL)), _full((D_MODEL, D_MODEL))],
        out_specs=[_rows(tm, D_MODEL), _rows(tm, D_MODEL)],
        out_shape=[jax.ShapeDtypeStruct((s, D_MODEL), BF16), jax.ShapeDtypeStruct((s, D_MODEL), F32)],
        compiler_params=_params(("arbitrary",)),
    )(merged, x, mod, w_out)


def _ffn_up(x1, mod, g_norm2, w_gu, tm, tn):
    s = x1.shape[0]
    nj = D_FF // tn

    def body(x_ref, mod_ref, g_ref, wg_ref, wu_ref, h_ref, gate_ref, up_ref, act_ref):
        @pl.when(pl.program_id(1) == 0)
        def _():
            xf = x_ref[...]
            n = xf * _rsqrt_mean_sq(xf) * g_ref[...]
            h_ref[...] = (n * (1.0 + mod_ref[SC2:SC2 + 1, :]) + mod_ref[SH2:SH2 + 1, :]).astype(BF16)

        h = h_ref[...]
        gate = _dot_nt(h, wg_ref[...])
        up = _dot_nt(h, wu_ref[...])
        gate_ref[...] = gate.astype(BF16)
        up_ref[...] = up.astype(BF16)
        act_ref[...] = (gate * _sigmoid(gate) * up).astype(BF16)

    tile = pl.BlockSpec((tm, tn), lambda i, j: (i, j))
    return pl.pallas_call(
        body, name="ffn_up", grid=(s // tm, nj),
        in_specs=[_rows(tm, D_MODEL), _full((8, D_MODEL)), _full((1, D_MODEL)),
                  pl.BlockSpec((tn, D_MODEL), lambda i, j: (j, 0)),
                  pl.BlockSpec((tn, D_MODEL), lambda i, j: (j + nj, 0))],
        out_specs=[_rows(tm, D_MODEL), tile, tile, tile],
        out_shape=[jax.ShapeDtypeStruct((s, D_MODEL), BF16)] + [jax.ShapeDtypeStruct((s, D_FF), BF16)] * 3,
        compiler_params=_params(("arbitrary", "arbitrary"), VMEM_LIMIT_LARGE),
    )(x1, mod, g_norm2, w_gu, w_gu)


def _ffn_down_loss(act, x1, mod, w_down, g_final, target, tm):
    s = x1.shape[0]

    def body(a_ref, x1_ref, mod_ref, w_ref, gf_ref, t_ref, o_ref, dx2_ref, small_ref):
        @pl.when(pl.program_id(0) == 0)
        def _():
            small_ref[...] = jnp.zeros_like(small_ref)

        o = _dot(a_ref[...], w_ref[...])
        o_ref[...] = o.astype(BF16)
        x2 = x1_ref[...] + mod_ref[G2:G2 + 1, :] * o
        r = _rsqrt_mean_sq(x2)
        xn = x2 * r
        gf = gf_ref[...]
        err = xn * gf - t_ref[...]
        dy = err * (1.0 / D_MODEL)
        dxn = dy * gf
        dx2_ref[...] = r * (dxn - xn * jnp.mean(dxn * xn, axis=-1, keepdims=True))
        small_ref[0:1, :] += _colsum(dy * xn)
        small_ref[1:2, :] += _colsum(err * err)

        @pl.when(pl.program_id(0) == pl.num_programs(0) - 1)
        def _():
            total = jnp.sum(small_ref[1:2, :], axis=-1, keepdims=True) * (0.5 / D_MODEL)
            small_ref[2:3, :] = jnp.broadcast_to(total, (1, D_MODEL))

    return pl.pallas_call(
        body, name="ffn_down_loss", grid=(s // tm,),
        in_specs=[_rows(tm, D_FF), _rows(tm, D_MODEL), _full((8, D_MODEL)), _full((D_FF, D_MODEL)),
                  _full((1, D_MODEL)), _rows(tm, D_MODEL)],
        out_specs=[_rows(tm, D_MODEL), _rows(tm, D_MODEL), _full((8, D_MODEL))],
        out_shape=[jax.ShapeDtypeStruct((s, D_MODEL), BF16), jax.ShapeDtypeStruct((s, D_MODEL), F32),
                   jax.ShapeDtypeStruct((8, D_MODEL), F32)],
        compiler_params=_params(("arbitrary",), VMEM_LIMIT_LARGE),
    )(act, x1, mod, w_down, g_final, target)


def _ffn_down_bwd(dx2, o2, gate, up, mod, w_down, tm):
    s = dx2.shape[0]

    def body(dx_ref, o_ref, gate_ref, up_ref, mod_ref, w_ref, do_ref, dgu_ref, small_ref):
        @pl.when(pl.program_id(0) == 0)
        def _():
            small_ref[...] = jnp.zeros_like(small_ref)

        dx = dx_ref[...]
        small_ref[0:1, :] += _colsum(dx * o_ref[...].astype(F32))
        do = (dx * mod_ref[G2:G2 + 1, :]).astype(BF16)
        do_ref[...] = do
        dact = _dot_nt(do, w_ref[...])
        gate = gate_ref[...].astype(F32)
        sg = _sigmoid(gate)
        dgu_ref[:, 0:D_FF] = (dact * up_ref[...].astype(F32) * (sg * (1.0 + gate * (1.0 - sg)))).astype(BF16)
        dgu_ref[:, D_FF:2 * D_FF] = (dact * (gate * sg)).astype(BF16)

    return pl.pallas_call(
        body, name="ffn_down_bwd", grid=(s // tm,),
        in_specs=[_rows(tm, D_MODEL), _rows(tm, D_MODEL), _rows(tm, D_FF), _rows(tm, D_FF), _full((8, D_MODEL)),
                  _full((D_FF, D_MODEL))],
        out_specs=[_rows(tm, D_MODEL), _rows(tm, 2 * D_FF), _full((8, D_MODEL))],
        out_shape=[jax.ShapeDtypeStruct((s, D_MODEL), BF16), jax.ShapeDtypeStruct((s, 2 * D_FF), BF16),
                   jax.ShapeDtypeStruct((8, D_MODEL), F32)],
        compiler_params=_params(("arbitrary",), VMEM_LIMIT_LARGE),
    )(dx2, o2, gate, up, mod, w_down)


def _norm_mod_bwd(dh, xf, g, scale_row, small_ref):
    r = _rsqrt_mean_sq(xf)
    xn = xf * r
    small_ref[0:1, :] += _colsum(dh)
    small_ref[1:2, :] += _colsum(dh * (xn * g))
    dn = dh * (1.0 + scale_row)
    small_ref[2:3, :] += _colsum(dn * xn)
    dxn = dn * g
    return r * (dxn - xn * jnp.mean(dxn * xn, axis=-1, keepdims=True))


def _ffn_up_bwd(dgu, x1, dx2, mod, g_norm2, w_gu, tm, tk):
    s = x1.shape[0]
    nk = (2 * D_FF) // tk

    def body(dgu_ref, x_ref, dx2_ref, mod_ref, g_ref, w_ref, dx1_ref, small_ref, acc_ref):
        i, k = pl.program_id(0), pl.program_id(1)

        @pl.when((i == 0) & (k == 0))
        def _():
            small_ref[...] = jnp.zeros_like(small_ref)

        part = _dot(dgu_ref[...], w_ref[...])

        @pl.when(k == 0)
        def _():
            acc_ref[...] = part

        @pl.when(k > 0)
        def _():
            acc_ref[...] += part

        @pl.when(k == nk - 1)
        def _():
            dx1_ref[...] = dx2_ref[...] + _norm_mod_bwd(acc_ref[...], x_ref[...], g_ref[...],
                                                        mod_ref[SC2:SC2 + 1, :], small_ref)

    return pl.pallas_call(
        body, name="ffn_up_bwd", grid=(s // tm, nk),
        in_specs=[pl.BlockSpec((tm, tk), lambda i, k: (i, k)), _rows(tm, D_MODEL), _rows(tm, D_MODEL),
                  _full((8, D_MODEL)), _full((1, D_MODEL)), pl.BlockSpec((tk, D_MODEL), lambda i, k: (k, 0))],
        out_specs=[_rows(tm, D_MODEL), _full((8, D_MODEL))],
        out_shape=[jax.ShapeDtypeStruct((s, D_MODEL), F32), jax.ShapeDtypeStruct((8, D_MODEL), F32)],
        scratch_shapes=[pltpu.VMEM((tm, D_MODEL), F32)],
        compiler_params=_params(("arbitrary", "arbitrary"), VMEM_LIMIT_LARGE),
    )(dgu, x1, dx2, mod, g_norm2, w_gu)


def _out_proj_bwd(dx1, o1, mod, w_out, tm):
    s = dx1.shape[0]

    def body(dx_ref, o_ref, mod_ref, w_ref, do_ref, dm_ref, small_ref):
        @pl.when(pl.program_id(0) == 0)
        def _():
            small_ref[...] = jnp.zeros_like(small_ref)

        dx = dx_ref[...]
        small_ref[0:1, :] += _colsum(dx * o_ref[...].astype(F32))
        do = (dx * mod_ref[G1:G1 + 1, :]).astype(BF16)
        do_ref[...] = do
        dm_ref[...] = _dot_nt(do, w_ref[...])

    return pl.pallas_call(
        body, name="out_proj_bwd", grid=(s // tm,),
        in_specs=[_rows(tm, D_MODEL), _rows(tm, D_MODEL), _full((8, D_MODEL)), _full((D_MODEL, D_MODEL))],
        out_specs=[_rows(tm, D_MODEL), _rows(tm, D_MODEL), _full((8, D_MODEL))],
        out_shape=[jax.ShapeDtypeStruct((s, D_MODEL), BF16), jax.ShapeDtypeStruct((s, D_MODEL), F32),
                   jax.ShapeDtypeStruct((8, D_MODEL), F32)],
        compiler_params=_params(("arbitrary",)),
    )(dx1, o1, mod, w_out)


def _group_norm_bwd(dm, a, g):
    r = _rsqrt_mean_sq(a)
    an = a * r
    dan = dm * g
    return r * (dan - an * jnp.mean(dan * an, axis=-1, keepdims=True)), _colsum(dm * an)


def _mixer_bwd(q, kv, gb, gc, xc, bias, sinks, conv_w, g_attn, g_conv, attn, lse, dmerged):
    s = q.shape[0]
    nb = s // BLOCK

    def body(sink_ref, q_ref, kv_ref, gb_ref, gc_ref, xc_ref, gcp_ref, xcp_ref, bias_ref, cw_ref, ga_ref, gcv_ref,
             attn_ref, lse_ref, dm_ref,
             dq_ref, dkv_ref, dgb_ref, dgc_ref, dxc_ref, dbias_ref, dsink_ref, small_ref, carry_ref):
        step = pl.program_id(0)
        n = nb - 1 - step

        @pl.when(step == 0)
        def _():
            dkv_ref[...] = jnp.zeros_like(dkv_ref)
            dbias_ref[...] = jnp.zeros_like(dbias_ref)
            dsink_ref[...] = jnp.zeros_like(dsink_ref)
            small_ref[...] = jnp.zeros_like(small_ref)
            carry_ref[...] = jnp.zeros_like(carry_ref)

        dm = dm_ref[...]
        gbv, gcv_, xcv = gb_ref[...], gc_ref[...], xc_ref[...]
        u, u1, u2 = _conv_taps(gcv_, xcv, gcp_ref[...], xcp_ref[...], n)
        cw = cw_ref[...]
        yv = cw[0:1, :] * u2 + cw[1:2, :] * u1 + cw[2:3, :] * u
        dcv, dg_conv = _group_norm_bwd(dm[:, 512:1024], gbv * yv, gcv_ref[...])
        small_ref[1:2, :] += dg_conv
        dgb_ref[...] = (dcv * yv).astype(BF16)
        dy = dcv * gbv
        nxt = carry_ref[...]
        row = lax.broadcasted_iota(jnp.int32, dy.shape, 0)
        d1 = jnp.where(row == BLOCK - 1, nxt[0:1, :], pltpu.roll(dy, BLOCK - 1, 0))
        d2 = jnp.where(row == BLOCK - 2, nxt[0:1, :],
                       jnp.where(row == BLOCK - 1, nxt[1:2, :], pltpu.roll(dy, BLOCK - 2, 0)))
        du = cw[2:3, :] * dy + cw[1:2, :] * d1 + cw[0:1, :] * d2
        dgc_ref[...] = (du * xcv).astype(BF16)
        dxc_ref[...] = (du * gcv_).astype(BF16)
        small_ref[2:3, :] += _colsum(dy * u2)
        small_ref[3:4, :] += _colsum(dy * u1)
        small_ref[4:5, :] += _colsum(dy * u)
        carry_ref[...] = dy[0:8, :]

        attn_v = attn_ref[...]
        dout, dg_attn = _group_norm_bwd(dm[:, 0:512], attn_v, ga_ref[...])
        small_ref[0:1, :] += dg_attn
        ks, vs = _load_kv_window(kv_ref, n)
        lane = lax.broadcasted_iota(jnp.int32, (BLOCK, BLOCK), 1)
        low = lane < HEAD_DIM
        col = lax.broadcasted_iota(jnp.int32, (BLOCK, 2 * BLOCK), 1)
        no_prev = (col < BLOCK) & (n == 0)
        lse_all = lse_ref[...]
        dk = jnp.zeros((2 * BLOCK, BLOCK), F32)
        dv = jnp.zeros((2 * BLOCK, BLOCK), F32)
        dsink = jnp.zeros((BLOCK, BLOCK), F32)
        dq_pairs = []
        for p in range(4):
            qp = q_ref[:, 128 * p:128 * (p + 1)].astype(F32)
            do_p = dout[:, 128 * p:128 * (p + 1)]
            prod = do_p * attn_v[:, 128 * p:128 * (p + 1)]
            kvh = p // 2
            res = []
            for e in range(2):
                h = 2 * p + e
                half = low if e == 0 else ~low
                qm = jnp.where(half, qp, 0.0).astype(BF16)
                dom = jnp.where(half, do_p, 0.0).astype(BF16)
                delta = jnp.sum(jnp.where(half, prod, 0.0), axis=-1, keepdims=True)
                lse_h = jnp.sum(jnp.where(lane == h, lse_all, 0.0), axis=-1, keepdims=True)
                sw = 0 if kvh == e else 1
                sc = _dot_nt(qm, ks[sw]) * SCALE + bias_ref[h]
                sc = jnp.where(no_prev, NEG_INF, sc)
                pr = jnp.exp(sc - lse_h)
                dp = _dot_nt(dom, vs[sw])
                ds = pr * (dp - delta)
                dbias_ref[h] += ds
                dsink = dsink + jnp.where(lane == h, -jnp.exp(sink_ref[h] - lse_h) * delta, 0.0)
                dsb = ds.astype(BF16)
                res.append(_dot(dsb, ks[sw]) * SCALE)
                dk_h = _dot_tn(dsb, qm) * SCALE
                dv_h = _dot_tn(pr.astype(BF16), dom)
                if sw:
                    dk_h = pltpu.roll(dk_h, 64, 1)
                    dv_h = pltpu.roll(dv_h, 64, 1)
                dk = dk + dk_h
                dv = dv + dv_h
            dq_pairs.append(jnp.where(low, res[0], res[1]))
        dq_ref[...] = jnp.concatenate(dq_pairs, axis=1).astype(BF16)
        dsink_ref[...] += dsink
        dkv_win = jnp.concatenate([dk, dv], axis=1)
        prev = jnp.maximum(n - 1, 0)
        dkv_ref[pl.ds(pl.multiple_of(prev * BLOCK, BLOCK), BLOCK), :] += dkv_win[0:BLOCK, :]
        dkv_ref[pl.ds(pl.multiple_of(n * BLOCK, BLOCK), BLOCK), :] += dkv_win[BLOCK:2 * BLOCK, :]

        @pl.when(step == nb - 1)
        def _():
            small_ref[5:6, :] = jnp.concatenate([_colsum(dsink_ref[...]), jnp.zeros((1, 512 - BLOCK), F32)], axis=1)

    blk = lambda w: pl.BlockSpec((BLOCK, w), lambda t: (nb - 1 - t, 0))
    prev8 = pl.BlockSpec((8, 512), lambda t: (jnp.maximum((nb - 1 - t) * (BLOCK // 8) - 1, 0), 0))
    bf = lambda w: jax.ShapeDtypeStruct((s, w), BF16)
    return pl.pallas_call(
        body, name="mixer_bwd", grid=(nb,),
        in_specs=[pl.BlockSpec(memory_space=pltpu.SMEM), blk(512), _full((s, 256)), blk(512), blk(512), blk(512),
                  prev8, prev8, _full((N_Q_HEADS, BLOCK, 2 * BLOCK)), _full((3, 512)), _full((1, 512)),
                  _full((1, 512)), blk(512), blk(128), blk(1024)],
        out_specs=[blk(512), _full((s, 256)), blk(512), blk(512), blk(512), _full((N_Q_HEADS, BLOCK, 2 * BLOCK)),
                   _full((BLOCK, BLOCK)), _full((8, 512))],
        out_shape=[bf(512), jax.ShapeDtypeStruct((s, 256), F32), bf(512), bf(512), bf(512),
                   jax.ShapeDtypeStruct((N_Q_HEADS, BLOCK, 2 * BLOCK), F32), jax.ShapeDtypeStruct((BLOCK, BLOCK), F32),
                   jax.ShapeDtypeStruct((8, 512), F32)],
        scratch_shapes=[pltpu.VMEM((8, 512), F32)],
        compiler_params=_params(("arbitrary",), VMEM_LIMIT_LARGE),
    )(sinks, q, kv, gb, gc, xc, gc, xc, bias, conv_w, g_attn, g_conv, attn, lse, dmerged)


def _in_proj_bwd(dq, dkv, dgb, dgc, dxc, x, dx1, mod, g_norm1, w_in, tm):
    s = x.shape[0]

    def body(dq_ref, dkv_ref, dgb_ref, dgc_ref, dxc_ref, x_ref, dx1_ref, mod_ref, g_ref, w_ref,
             dproj_ref, dx_ref, small_ref):
        @pl.when(pl.program_id(0) == 0)
        def _():
            small_ref[...] = jnp.zeros_like(small_ref)

        dproj = jnp.concatenate([dq_ref[...], dkv_ref[...].astype(BF16), dgb_ref[...], dgc_ref[...], dxc_ref[...]],
                                axis=1)
        dproj_ref[...] = dproj
        dh = _dot(dproj, w_ref[...])
        dx_ref[...] = dx1_ref[...] + _norm_mod_bwd(dh, x_ref[...], g_ref[...], mod_ref[SC1:SC1 + 1, :], small_ref)

    return pl.pallas_call(
        body, name="in_proj_bwd", grid=(s // tm,),
        in_specs=[_rows(tm, 512), _rows(tm, 256), _rows(tm, 512), _rows(tm, 512), _rows(tm, 512),
                  _rows(tm, D_MODEL), _rows(tm, D_MODEL), _full((8, D_MODEL)), _full((1, D_MODEL)),
                  _full((IN_PROJ_WIDTH, D_MODEL))],
        out_specs=[_rows(tm, IN_PROJ_WIDTH), _rows(tm, D_MODEL), _full((8, D_MODEL))],
        out_shape=[jax.ShapeDtypeStruct((s, IN_PROJ_WIDTH), BF16), jax.ShapeDtypeStruct((s, D_MODEL), F32),
                   jax.ShapeDtypeStruct((8, D_MODEL), F32)],
        compiler_params=_params(("arbitrary",), VMEM_LIMIT_LARGE),
    )(dq, dkv, dgb, dgc, dxc, x, dx1, mod, g_norm1, w_in)


def _weight_grad(a, b, tk, ts, name, after=None):
    s, k = a.shape
    n = b.shape[1]
    nt = s // ts
    extra = [] if after is None else [after]

    def body(a_ref, b_ref, *rest):
        o_ref, acc_ref = rest[-2:]
        t = pl.program_id(1)
        part = _dot_tn(a_ref[...], b_ref[...])

        @pl.when(t == 0)
        def _():
            acc_ref[...] = part

        @pl.when(t > 0)
        def _():
            acc_ref[...] += part

        @pl.when(t == nt - 1)
        def _():
            o_ref[...] = acc_ref[...].astype(BF16)

    return pl.pallas_call(
        body, name=name, grid=(k // tk, nt),
        in_specs=[pl.BlockSpec((ts, tk), lambda i, t: (t, i)), pl.BlockSpec((ts, n), lambda i, t: (t, 0))]
        + [pl.BlockSpec(memory_space=pl.ANY)] * len(extra),
        out_specs=pl.BlockSpec((tk, n), lambda i, t: (i, 0)),
        out_shape=jax.ShapeDtypeStruct((k, n), BF16),
        scratch_shapes=[pltpu.VMEM((tk, n), F32)],
        compiler_params=_params(("arbitrary", "arbitrary"), VMEM_LIMIT_LARGE),
    )(a, b, *extra)


def _rel_bias_grad(dbias, bucket):
    def body(db_ref, bk_ref, o_ref, rows_ref):
        bk = bk_ref[...]
        for b in range(N_BUCKETS):
            sel = (bk == b).astype(F32)
            for h in range(N_Q_HEADS):
                rows_ref[8 * b + h:8 * b + h + 1, :] = _colsum(db_ref[h] * sel)
        o_ref[...] = jnp.sum(rows_ref[...], axis=-1, keepdims=True)

    return pl.pallas_call(
        body, name="rel_bias_grad",
        out_shape=jax.ShapeDtypeStruct((N_BUCKETS * N_Q_HEADS, 1), F32),
        scratch_shapes=[pltpu.VMEM((N_BUCKETS * N_Q_HEADS, 2 * BLOCK), F32)],
    )(dbias, bucket)


def _sum_slots(parts, after):
    def body(p_ref, after_ref, o_ref):
        acc = p_ref[0]
        for k in range(1, N_DEV):
            acc = acc + p_ref[k]
        o_ref[...] = acc

    return pl.pallas_call(body, name="sum_small_grads",
                          in_specs=[pl.BlockSpec(memory_space=pltpu.VMEM), pl.BlockSpec(memory_space=pl.ANY)],
                          out_shape=jax.ShapeDtypeStruct(parts.shape[1:], F32))(parts, after)


def _w_ada_grad(cond_t, dmod_cols):
    def body(c_ref, d_ref, o_ref):
        o_ref[...] = _dot(c_ref[...], d_ref[...])

    return pl.pallas_call(body, name="w_ada_grad",
                          out_shape=jax.ShapeDtypeStruct((cond_t.shape[0], dmod_cols.shape[1]), F32))(cond_t, dmod_cols)


def _adam_math(w, g, m, v):
    m = ADAM_B1 * m + (1.0 - ADAM_B1) * g
    v = ADAM_B2 * v + (1.0 - ADAM_B2) * (g * g)
    m_hat = m / (1.0 - ADAM_B1 ** ADAM_STEP)
    v_hat = v / (1.0 - ADAM_B2 ** ADAM_STEP)
    delta = -ADAM_LR * (m_hat / (jnp.sqrt(v_hat) + ADAM_EPS) + ADAM_WD * w)
    return delta, m, v


def _sum_parts(local, land, me, tr, name):
    r, c = local.shape[1:]

    def body(me_ref, own_ref, land_ref, o_ref):
        acc = own_ref[0].astype(F32)
        for k in range(N_DEV - 1):
            acc = acc + land_ref[k].astype(F32)
        o_ref[...] = acc

    return pl.pallas_call(
        body, name=name,
        grid_spec=pltpu.PrefetchScalarGridSpec(
            num_scalar_prefetch=1, grid=(r // tr,),
            in_specs=[pl.BlockSpec((1, tr, c), lambda i, me_ref: (me_ref[0], i, 0)),
                      pl.BlockSpec((N_DEV - 1, tr, c), lambda i, me_ref: (0, i, 0))],
            out_specs=pl.BlockSpec((tr, c), lambda i, me_ref: (i, 0))),
        out_shape=jax.ShapeDtypeStruct((r, c), F32),
        compiler_params=_params(("arbitrary",)),
    )(me, local, land)


def _adamw(w, m, v, g, tr, name):
    r, c = w.shape

    def body(w_ref, m_ref, v_ref, g_ref, d_ref, mo_ref, vo_ref):
        d_ref[...], mo_ref[...], vo_ref[...] = _adam_math(w_ref[...], g_ref[...], m_ref[...], v_ref[...])

    tile = pl.BlockSpec((tr, c), lambda i: (i, 0))
    return pl.pallas_call(
        body, name=name, grid=(r // tr,),
        in_specs=[tile] * 4, out_specs=[tile] * 3,
        out_shape=[jax.ShapeDtypeStruct((r, c), F32)] * 3,
        compiler_params=_params(("arbitrary",)),
    )(w, m, v, g)


def _behind(a, token):
    return a + token[0:a.shape[0], 0:1]


def _local_step(x, target, mod, w_in_t, w_out, w_gu_t, w_down, rel_bias, g_norm1, sinks, conv_w, g_attn, g_conv,
                g_norm2, g_final, exchange):
    s = x.shape[0]
    tm = min(512, s)
    tm_small = min(256, s)
    bucket = _bucket_table()
    bias = _bias_table(rel_bias, bucket)

    h, q, kv, gb, gc, xc = _in_proj(x, mod, g_norm1, w_in_t, tm)
    attn, merged, lse = _mixer_fwd(q, kv, gb, gc, xc, bias, sinks, conv_w, g_attn, g_conv)
    o1, x1 = _out_proj(merged, x, mod, w_out, tm)
    h2, gate, up, act = _ffn_up(x1, mod, g_norm2, w_gu_t, tm, D_FF // 2)
    o2, dx2, fin = _ffn_down_loss(act, x1, mod, w_down, g_final, target, tm)

    do2, dgu, sm_g2 = _ffn_down_bwd(dx2, o2, gate, up, mod, w_down, tm_small)
    mod = _behind(mod, exchange("w_down", _weight_grad(act, do2, D_FF // 2, tm, "w_down_grad")))
    dx1, sm_2 = _ffn_up_bwd(dgu, x1, dx2, mod, g_norm2, w_gu_t, tm, D_FF // 2)
    mod = _behind(mod, exchange("w_gu", _weight_grad(dgu, h2, D_FF // 2, tm, "w_gu_grad")))
    do1, dmerged, sm_g1 = _out_proj_bwd(dx1, o1, mod, w_out, tm)
    g_attn_b = _behind(g_attn, exchange("w_out", _weight_grad(merged, do1, D_MODEL, tm, "w_out_grad")))
    dq, dkv, dgb, dgc, dxc, dbias, dsink, sm_mix = _mixer_bwd(
        q, kv, gb, gc, xc, bias, sinks, conv_w, g_attn_b, g_conv, attn, lse, dmerged)
    dproj, dx, sm_1 = _in_proj_bwd(dq, dkv, dgb, dgc, dxc, x, dx1, mod, g_norm1, w_in_t, tm)
    d_rel = _rel_bias_grad(dbias, bucket)

    packed = jnp.concatenate([
        sm_1[0], sm_1[1], sm_g1[0], sm_2[0], sm_2[1], sm_g2[0],
        d_rel[:, 0],
        sm_1[2],
        sm_mix[5, 0:128],
        sm_mix[0], sm_mix[1],
        sm_2[2],
        fin[0],
        sm_mix[2], sm_mix[3], sm_mix[4],
        fin[2, 0:128],
    ])[None, :]
    return dx, dproj, h, packed


def kernel(x, c, rel_bias, w_ada, b_ada, g_norm1, w_in, sinks, conv_w, g_attn_out, g_conv_out, w_out, g_norm2, w_gu, w_down, g_final, loss_target, m_rel_bias, m_w_ada, m_b_ada, m_g_norm1, m_w_in, m_sinks, m_conv_w, m_g_attn_out, m_g_conv_out, m_w_out, m_g_norm2, m_w_gu, m_w_down, m_g_final, v_rel_bias, v_w_ada, v_b_ada, v_g_norm1, v_w_in, v_sinks, v_conv_w, v_g_attn_out, v_g_conv_out, v_w_out, v_g_norm2, v_w_gu, v_w_down, v_g_final):
    me = _linear(_mesh_position())
    me_arr = jnp.reshape(me, (1,)).astype(jnp.int32)
    ada_cols = w_ada.shape[2]
    tm = min(512, x.shape[1])

    cond = _silu_rows(c)
    cond_all, conv_w_all = _all_gather([cond, conv_w[0]], "gather_cond", to_bf16=False, big=False)
    cond_all = cond_all[:, 0, :]
    conv_cols = conv_w.shape[2]
    conv_w_full = conv_w_all.transpose(1, 0, 2).reshape(3, CONV_WIDTH)
    b_cols = lax.dynamic_slice_in_dim(b_ada, me * ada_cols, ada_cols, axis=1)
    mod_cols = _mod_columns(cond_all, w_ada[0], b_cols)
    mod_all = _all_gather([mod_cols], "gather_mod", to_bf16=False, big=False)[0]
    mod = lax.dynamic_index_in_dim(mod_all, me, axis=1, keepdims=False).reshape(N_MOD, D_MODEL)
    mod = jnp.concatenate([mod, jnp.zeros((2, D_MODEL), F32)], axis=0)

    gw_in, gw_out, gw_gu, gw_down = _all_gather([w_in[0].T, w_out[0], w_gu[0].T, w_down[0]], "gather_weights",
                                                to_bf16=True, big=True)
    w_in_t = gw_in.reshape(IN_PROJ_WIDTH, D_MODEL)
    w_gu_t = gw_gu.reshape(2 * D_FF, D_MODEL)
    w_out_f = gw_out.reshape(D_MODEL, D_MODEL)
    w_down_f = gw_down.reshape(D_FF, D_MODEL)

    started = {}

    def exchange(name, dw):
        st = _exchange_start(dw.reshape(N_DEV, dw.shape[0] // N_DEV, dw.shape[1]), "exchange_start_" + name)
        started[name] = st
        return st[4]

    dx, dproj, h, packed = _local_step(
        x[0], loss_target[0], mod, w_in_t, w_out_f, w_gu_t, w_down_f, rel_bias, g_norm1, sinks[0], conv_w_full,
        g_attn_out, g_conv_out, g_norm2, g_final[None, :], exchange)

    packed_all = _all_gather([packed], "gather_small_grads", to_bf16=False, big=False)[0]
    tok_in = exchange("w_in", _weight_grad(dproj, h, IN_PROJ_WIDTH // 2, tm, "w_in_grad", after=packed_all))
    small = _sum_slots(packed_all, tok_in)[0]
    dmod_all = packed_all[:, 0, OFF_DMOD:OFF_DMOD + N_MOD * D_MODEL]
    dmod_cols = lax.dynamic_slice_in_dim(dmod_all, me * ada_cols, ada_cols, axis=1)
    cond_t = jnp.zeros((D_MODEL, 128), F32).at[:, 0:N_DEV].set(cond_all.T)
    dmod_pad = jnp.zeros((128, ada_cols), F32).at[0:N_DEV, :].set(dmod_cols)
    g_ada = _w_ada_grad(cond_t, dmod_pad)
    d_ada, nm_ada, nv_ada = _adamw(w_ada[0], m_w_ada[0], v_w_ada[0], g_ada, 256, "adamw_w_ada")

    loss = small[OFF_LOSS]
    seg = lambda off, n: small[off:off + n]
    conv_g_full = seg(OFF_CONVW, 3 * CONV_WIDTH).reshape(3, CONV_WIDTH)
    small_grads = {
        "rel_bias": seg(OFF_RELB, 256).reshape(N_BUCKETS, N_Q_HEADS),
        "b_ada": seg(OFF_DMOD, N_MOD * D_MODEL).reshape(1, N_MOD * D_MODEL),
        "g_norm1": seg(OFF_GN1, D_MODEL).reshape(1, D_MODEL),
        "sinks": seg(OFF_SINK, N_Q_HEADS).reshape(1, N_Q_HEADS),
        "conv_w": lax.dynamic_slice_in_dim(conv_g_full, me * conv_cols, conv_cols, axis=1)[None],
        "g_attn_out": seg(OFF_GATT, ATTN_WIDTH).reshape(1, ATTN_WIDTH),
        "g_conv_out": seg(OFF_GCV, CONV_WIDTH).reshape(1, CONV_WIDTH),
        "g_norm2": seg(OFF_GN2, D_MODEL).reshape(1, D_MODEL),
        "g_final": seg(OFF_GFIN, D_MODEL),
    }
    small_state = {
        "rel_bias": (rel_bias, m_rel_bias, v_rel_bias), "b_ada": (b_ada, m_b_ada, v_b_ada),
        "g_norm1": (g_norm1, m_g_norm1, v_g_norm1), "sinks": (sinks, m_sinks, v_sinks),
        "conv_w": (conv_w, m_conv_w, v_conv_w), "g_attn_out": (g_attn_out, m_g_attn_out, v_g_attn_out),
        "g_conv_out": (g_conv_out, m_g_conv_out, v_g_conv_out), "g_norm2": (g_norm2, m_g_norm2, v_g_norm2),
        "g_final": (g_final, m_g_final, v_g_final),
    }
    names = list(small_grads)
    sizes = [small_grads[k].size for k in names]
    total = sum(sizes)
    padded = -(-total // 1024) * 1024

    def pack(arrs):
        flat = jnp.concatenate([a.reshape(-1) for a in arrs] + [jnp.ones((padded - total,), F32)])
        return flat.reshape(padded // 128, 128)

    sw = pack([small_state[k][0] for k in names])
    sm = pack([small_state[k][1] for k in names])
    sv = pack([small_state[k][2] for k in names])
    sg = pack([small_grads[k] for k in names])
    sd, snm, snv = _adamw(sw, sm, sv, sg, padded // 128, "adamw_small")

    def unpack(flat2d):
        flat = flat2d.reshape(-1)
        out, off = {}, 0
        for k, n in zip(names, sizes):
            out[k] = flat[off:off + n].reshape(small_grads[k].shape)
            off += n
        return out

    sd_all = sd
    sd, snm, snv = unpack(sd), unpack(snm), unpack(snv)

    def finish(name, after, tr):
        src, land = _exchange_wait(started[name], after, "exchange_wait_" + name)
        return _sum_parts(src, land, me_arr, tr, "sum_parts_" + name)

    g_down = finish("w_down", [sd_all], 176)
    d_down, nm_down, nv_down = _adamw(w_down[0], m_w_down[0], v_w_down[0], g_down, 176, "adamw_w_down")
    g_gu = finish("w_gu", [nv_down], 352)
    d_gu, nm_gu, nv_gu = _adamw(w_gu[0].T, m_w_gu[0].T, v_w_gu[0].T, g_gu, 352, "adamw_w_gu")
    g_out = finish("w_out", [nv_gu], 128)
    d_out, nm_out, nv_out = _adamw(w_out[0], m_w_out[0], v_w_out[0], g_out, 128, "adamw_w_out")
    g_in = finish("w_in", [nv_out, nv_ada], 144)
    d_in, nm_in, nv_in = _adamw(w_in[0].T, m_w_in[0].T, v_w_in[0].T, g_in, 144, "adamw_w_in")

    big = {
        "w_ada": (g_ada[None], d_ada[None], nm_ada[None], nv_ada[None]),
        "w_in": (g_in.T[None], d_in.T[None], nm_in.T[None], nv_in.T[None]),
        "w_out": (g_out[None], d_out[None], nm_out[None], nv_out[None]),
        "w_gu": (g_gu.T[None], d_gu.T[None], nm_gu.T[None], nv_gu.T[None]),
        "w_down": (g_down[None], d_down[None], nm_down[None], nv_down[None]),
    }
    order = ["rel_bias", "w_ada", "b_ada", "g_norm1", "w_in", "sinks", "conv_w", "g_attn_out", "g_conv_out", "w_out",
             "g_norm2", "w_gu", "w_down", "g_final"]
    grads = [big[k][0] if k in big else small_grads[k] for k in order]
    deltas = [big[k][1] if k in big else sd[k] for k in order]
    new_m = [big[k][2] if k in big else snm[k] for k in order]
    new_v = [big[k][3] if k in big else snv[k] for k in order]
    return (loss, dx[None], *grads, *deltas, *new_m, *new_v)
```

```python
import functools
import math

import jax
import jax.numpy as jnp
from jax import lax
from jax.experimental import pallas as pl
from jax.experimental.pallas import tpu as pltpu

F32 = jnp.float32
BF16 = jnp.bfloat16

D_MODEL = 1024
HEAD_DIM = 64
N_Q_HEADS = 8
ATTN_WIDTH = 512
KV_WIDTH = 128
CONV_WIDTH = 512
IN_PROJ_WIDTH = 2304
D_FF = 2816
N_MOD = 6
N_BUCKETS = 32
MAX_DISTANCE = 128
BLOCK = 128
EPS = 1e-6
NEG_INF = -1e30
SCALE = HEAD_DIM ** -0.5
N_DEV = 8

ADAM_LR = 0.001
ADAM_B1 = 0.9
ADAM_B2 = 0.999
ADAM_EPS = 1e-08
ADAM_WD = 0.01
ADAM_STEP = 10

SH1, SC1, G1, SH2, SC2, G2 = range(6)

VMEM_LIMIT_LARGE = 56 * 1024 * 1024
WEIGHT_GRAD_ROWS = 2048
MESH_ID = pl.DeviceIdType.MESH

OFF_DMOD = 0
OFF_RELB = OFF_DMOD + N_MOD * D_MODEL
OFF_GN1 = OFF_RELB + N_BUCKETS * N_Q_HEADS
OFF_SINK = OFF_GN1 + D_MODEL
OFF_GATT = OFF_SINK + 128
OFF_GCV = OFF_GATT + ATTN_WIDTH
OFF_GN2 = OFF_GCV + CONV_WIDTH
OFF_GFIN = OFF_GN2 + D_MODEL
OFF_CONVW = OFF_GFIN + D_MODEL
OFF_LOSS = OFF_CONVW + 3 * CONV_WIDTH
PACKED = OFF_LOSS + 128


def _params(sem=None, vmem=None):
    return pltpu.CompilerParams(dimension_semantics=sem, vmem_limit_bytes=vmem)


def _full(shape):
    nd = len(shape)
    return pl.BlockSpec(shape, lambda *_: (0,) * nd)


def _rows(tm, width):
    return pl.BlockSpec((tm, width), lambda i, *_: (i, 0))


def _sigmoid(x):
    return 1.0 / (1.0 + jnp.exp(-x))


def _rsqrt_mean_sq(x):
    return lax.rsqrt(jnp.mean(x * x, axis=-1, keepdims=True) + EPS)


def _colsum(x):
    return jnp.sum(x, axis=0, keepdims=True)


def _dot(a, b):
    return jnp.dot(a, b, preferred_element_type=F32)


def _dot_nt(a, b):
    return lax.dot_general(a, b, (((1,), (1,)), ((), ())), preferred_element_type=F32)


def _dot_tn(a, b):
    return lax.dot_general(a, b, (((0,), (0,)), ((), ())), preferred_element_type=F32)


def _mesh_position():
    return lax.axis_index("x"), lax.axis_index("y"), lax.axis_index("c")


def _linear(p):
    return 4 * p[0] + 2 * p[1] + p[2]


def _all_gather(arrs, name, to_bf16, big):
    n = len(arrs)
    out_dtype = BF16 if to_bf16 else F32

    def body(*refs):
        in_refs, out_refs = refs[:n], refs[n:2 * n]
        rest = refs[2 * n:]
        if to_bf16:
            stage, rest = rest[:n], rest[n:]
            for a in range(n):
                stage[a][...] = in_refs[a][...].astype(BF16)
            srcs = stage
        else:
            srcs = in_refs
        send_sems, recv_sems, local_sems = rest
        x, y, c = _mesh_position()
        me, sibling = (x, y, c), (x, y, 1 - c)
        chips = [(1 - x, y), (x, 1 - y), (1 - x, 1 - y)]

        def slot(a, p):
            return out_refs[a].at[_linear(p)]

        def copy(k, a, block, to, src=None):
            return pltpu.make_async_remote_copy(
                src_ref=slot(a, block) if src is None else src,
                dst_ref=slot(a, block),
                send_sem=send_sems.at[k * n + a],
                recv_sem=recv_sems.at[k * n + a],
                device_id=to,
                device_id_type=MESH_ID,
            )

        mine = [pltpu.make_async_copy(srcs[a], slot(a, me), local_sems.at[a]) for a in range(n)]
        for cp in mine:
            cp.start()
        first = [copy(0, a, me, sibling, src=srcs[a]) for a in range(n)]
        for j, chip in enumerate(chips):
            first += [copy(1 + j, a, me, (*chip, c), src=srcs[a]) for a in range(n)]
        for cp in first:
            cp.start()
        passed = []
        for j, chip in enumerate(chips):
            for a in range(n):
                copy(1 + j, a, (*chip, c), me).wait_recv()
                fwd = copy(4 + j, a, (*chip, c), sibling)
                fwd.start()
                passed.append(fwd)
        for a in range(n):
            copy(0, a, sibling, me).wait_recv()
        for j, chip in enumerate(chips):
            for a in range(n):
                copy(4 + j, a, (*chip, 1 - c), me).wait_recv()
        for cp in first + passed:
            cp.wait_send()
        for cp in mine:
            cp.wait()

    vmem = pl.BlockSpec(memory_space=pltpu.VMEM)
    out_space = pl.BlockSpec(memory_space=pl.ANY) if big else vmem
    scratch = [pltpu.VMEM(a.shape, BF16) for a in arrs] if to_bf16 else []
    scratch += [pltpu.SemaphoreType.DMA((7 * n,)), pltpu.SemaphoreType.DMA((7 * n,)),
                pltpu.SemaphoreType.DMA((n,))]
    outs = pl.pallas_call(
        body, name=name,
        out_shape=[jax.ShapeDtypeStruct((N_DEV,) + a.shape, out_dtype) for a in arrs],
        in_specs=[vmem] * n, out_specs=[out_space] * n,
        scratch_shapes=scratch,
        compiler_params=_params(vmem=VMEM_LIMIT_LARGE if big else None),
    )(*arrs)
    return list(outs)


def _peer(k):
    x, y, c = _mesh_position()
    return (1 - x if k & 4 else x, 1 - y if k & 2 else y, 1 - c if k & 1 else c)


HBM_SPEC = pl.BlockSpec(memory_space=pltpu.HBM)
SEM_SPEC = pl.BlockSpec(memory_space=pltpu.SEMAPHORE)
DATAFLOW = pltpu.SideEffectType.DATAFLOW_SIDE_EFFECTING


def _exchange_start(src, name):
    r, c = src.shape[1:]

    def body(src_ref, land_ref, send_sems, recv_sems, src_thru, land_thru, token):
        for k in range(1, N_DEV):
            peer = _peer(k)
            pltpu.make_async_remote_copy(
                src_ref=src_ref.at[_linear(peer)], dst_ref=land_ref.at[k - 1],
                send_sem=send_sems.at[k - 1], recv_sem=recv_sems.at[k - 1],
                device_id=peer, device_id_type=MESH_ID).start()
        token[...] = jnp.zeros_like(token)

    land = lax.empty((N_DEV - 1, r, c), src.dtype)
    return pl.pallas_call(
        body, name=name,
        out_shape=(pltpu.SemaphoreType.DMA((N_DEV - 1,)), pltpu.SemaphoreType.DMA((N_DEV - 1,)),
                   pltpu.HBM(src.shape, src.dtype), pltpu.HBM(land.shape, land.dtype),
                   jax.ShapeDtypeStruct((8, 128), F32)),
        in_specs=(HBM_SPEC, HBM_SPEC),
        out_specs=(SEM_SPEC, SEM_SPEC, HBM_SPEC, HBM_SPEC, pl.BlockSpec(memory_space=pltpu.VMEM)),
        input_output_aliases={0: 2, 1: 3},
        compiler_params=pltpu.CompilerParams(has_side_effects=DATAFLOW),
    )(pltpu.with_memory_space_constraint(src, pltpu.HBM), pltpu.with_memory_space_constraint(land, pltpu.HBM))


def _exchange_wait(started, after, name):
    send_sems, recv_sems, src_thru, land_thru, _ = started

    def body(src_ref, land_ref, send_sems, recv_sems, *rest):
        for k in range(1, N_DEV):
            cp = pltpu.make_async_remote_copy(
                src_ref=src_ref.at[0], dst_ref=land_ref.at[k - 1],
                send_sem=send_sems.at[k - 1], recv_sem=recv_sems.at[k - 1],
                device_id=_peer(k), device_id_type=MESH_ID)
            cp.wait_send()
            cp.wait_recv()

    return pl.pallas_call(
        body, name=name,
        out_shape=(pltpu.HBM(src_thru.shape, src_thru.dtype), pltpu.HBM(land_thru.shape, land_thru.dtype)),
        in_specs=(HBM_SPEC, HBM_SPEC, SEM_SPEC, SEM_SPEC) + (pl.BlockSpec(memory_space=pl.ANY),) * len(after),
        out_specs=(HBM_SPEC, HBM_SPEC), input_output_aliases={0: 0, 1: 1},
        compiler_params=pltpu.CompilerParams(has_side_effects=DATAFLOW),
    )(src_thru, land_thru, send_sems, recv_sems, *after)


def _stage_blocks(arrs, name):
    n = len(arrs)

    def body(*refs):
        in_refs, out_refs, stage, sems = refs[:n], refs[n:2 * n], refs[2 * n:3 * n], refs[3 * n]
        me = _linear(_mesh_position())
        copies = []
        for a in range(n):
            stage[a][...] = in_refs[a][...].astype(BF16)
            copies.append(pltpu.make_async_copy(stage[a], out_refs[a].at[me], sems.at[a]))
            copies[-1].start()
        for cp in copies:
            cp.wait()

    return list(pl.pallas_call(
        body, name=name,
        out_shape=[jax.ShapeDtypeStruct((N_DEV,) + a.shape, BF16) for a in arrs],
        in_specs=[pl.BlockSpec(memory_space=pltpu.VMEM)] * n, out_specs=[pl.BlockSpec(memory_space=pl.ANY)] * n,
        scratch_shapes=[pltpu.VMEM(a.shape, BF16) for a in arrs] + [pltpu.SemaphoreType.DMA((n,))],
        compiler_params=_params(vmem=VMEM_LIMIT_LARGE),
    )(*arrs))


def _same_core_peers():
    x, y, c = _mesh_position()
    return [(x, y, 1 - c), (1 - x, y, c), (x, 1 - y, c), (1 - x, 1 - y, c)]


def _gather_start(bufs, name):
    n = len(bufs)

    def body(*refs):
        buf_refs, rest = refs[:n], refs[n:]
        sems, token = rest[:2 * n], rest[-1]
        me = _linear(_mesh_position())
        for a in range(n):
            for k, peer in enumerate(_same_core_peers()):
                pltpu.make_async_remote_copy(
                    src_ref=buf_refs[a].at[me], dst_ref=buf_refs[a].at[me],
                    send_sem=sems[2 * a].at[k], recv_sem=sems[2 * a + 1].at[k],
                    device_id=peer, device_id_type=MESH_ID).start()
        token[...] = jnp.zeros_like(token)

    outs = pl.pallas_call(
        body, name=name,
        out_shape=tuple(pltpu.SemaphoreType.DMA((4,)) for _ in range(2 * n))
        + tuple(pltpu.HBM(b.shape, b.dtype) for b in bufs) + (jax.ShapeDtypeStruct((8, 128), F32),),
        in_specs=(HBM_SPEC,) * n,
        out_specs=(SEM_SPEC,) * (2 * n) + (HBM_SPEC,) * n + (pl.BlockSpec(memory_space=pltpu.VMEM),),
        input_output_aliases={a: 2 * n + a for a in range(n)},
        compiler_params=pltpu.CompilerParams(has_side_effects=DATAFLOW),
    )(*[pltpu.with_memory_space_constraint(b, pltpu.HBM) for b in bufs])
    return outs[:2 * n], outs[2 * n:3 * n], outs[3 * n]


def _gather_wait(sems, bufs, after, name):
    n = len(bufs)

    def body(*refs):
        buf_refs, sem_refs = refs[:n], refs[n:3 * n]
        x, y, c = _mesh_position()
        me = _linear((x, y, c))
        for a in range(n):
            for k, peer in enumerate(_same_core_peers()):
                cp = pltpu.make_async_remote_copy(
                    src_ref=buf_refs[a].at[me], dst_ref=buf_refs[a].at[_linear(peer)],
                    send_sem=sem_refs[2 * a].at[k], recv_sem=sem_refs[2 * a + 1].at[k],
                    device_id=peer, device_id_type=MESH_ID)
                cp.wait_send()
                cp.wait_recv()

    return list(pl.pallas_call(
        body, name=name,
        out_shape=tuple(pltpu.HBM(b.shape, b.dtype) for b in bufs),
        in_specs=(HBM_SPEC,) * n + (SEM_SPEC,) * (2 * n) + (pl.BlockSpec(memory_space=pl.ANY),) * len(after),
        out_specs=(HBM_SPEC,) * n, input_output_aliases={a: a for a in range(n)},
        compiler_params=pltpu.CompilerParams(has_side_effects=DATAFLOW),
    )(*bufs, *sems, *after))


def _gather_pass_on(bufs, name):
    n = len(bufs)

    def body(*refs):
        out_refs = refs[n:2 * n]
        send_sems, recv_sems = refs[2 * n:]
        x, y, c = _mesh_position()
        sibling = (x, y, 1 - c)
        chips = [(1 - x, y), (x, 1 - y), (1 - x, 1 - y)]
        copies = []
        for a in range(n):
            for j, chip in enumerate(chips):
                block = out_refs[a].at[_linear((*chip, c))]
                copies.append(pltpu.make_async_remote_copy(
                    src_ref=block, dst_ref=block, send_sem=send_sems.at[3 * a + j], recv_sem=recv_sems.at[3 * a + j],
                    device_id=sibling, device_id_type=MESH_ID))
                copies[-1].start()
        for a in range(n):
            for j, chip in enumerate(chips):
                copies[3 * a + j].wait_send()
                theirs = out_refs[a].at[_linear((*chip, 1 - c))]
                pltpu.make_async_remote_copy(
                    src_ref=theirs, dst_ref=theirs, send_sem=send_sems.at[3 * a + j], recv_sem=recv_sems.at[3 * a + j],
                    device_id=sibling, device_id_type=MESH_ID).wait_recv()

    hbm = pl.BlockSpec(memory_space=pl.ANY)
    return list(pl.pallas_call(
        body, name=name,
        out_shape=[jax.ShapeDtypeStruct(b.shape, b.dtype) for b in bufs],
        in_specs=[hbm] * n, out_specs=[hbm] * n, input_output_aliases={a: a for a in range(n)},
        scratch_shapes=[pltpu.SemaphoreType.DMA((3 * n,)), pltpu.SemaphoreType.DMA((3 * n,))],
    )(*bufs))


def _silu_rows(c):
    def body(c_ref, o_ref):
        v = c_ref[...]
        o_ref[...] = v * _sigmoid(v)

    return pl.pallas_call(body, name="cond_silu", out_shape=jax.ShapeDtypeStruct(c.shape, F32))(c)


def _mod_columns(cond_all, w_ada, b_cols):
    def body(c_ref, w_ref, b_ref, o_ref):
        o_ref[...] = _dot(c_ref[...], w_ref[...]) + b_ref[...]

    return pl.pallas_call(body, name="mod_columns",
                          out_shape=jax.ShapeDtypeStruct((N_DEV, w_ada.shape[1]), F32))(cond_all, w_ada, b_cols)


def _in_proj(x, mod, g_norm1, w_in, tm):
    s = x.shape[0]

    def body(x_ref, mod_ref, g_ref, w_ref, h_ref, q_ref, kv_ref, gb_ref, gc_ref, xc_ref):
        xf = x_ref[...]
        n = xf * _rsqrt_mean_sq(xf) * g_ref[...]
        h = (n * (1.0 + mod_ref[SC1:SC1 + 1, :]) + mod_ref[SH1:SH1 + 1, :]).astype(BF16)
        h_ref[...] = h
        p = _dot_nt(h, w_ref[...])
        q_ref[...] = p[:, 0:512].astype(BF16)
        kv_ref[...] = p[:, 512:768].astype(BF16)
        gb_ref[...] = p[:, 768:1280]
        gc_ref[...] = p[:, 1280:1792]
        xc_ref[...] = p[:, 1792:2304]

    return pl.pallas_call(
        body, name="in_proj", grid=(s // tm,),
        in_specs=[_rows(tm, D_MODEL), _full((8, D_MODEL)), _full((1, D_MODEL)), _full((IN_PROJ_WIDTH, D_MODEL))],
        out_specs=[_rows(tm, D_MODEL), _rows(tm, 512), _rows(tm, 256), _rows(tm, 512), _rows(tm, 512), _rows(tm, 512)],
        out_shape=[jax.ShapeDtypeStruct((s, D_MODEL), BF16), jax.ShapeDtypeStruct((s, 512), BF16),
                   jax.ShapeDtypeStruct((s, 256), BF16), jax.ShapeDtypeStruct((s, 512), F32),
                   jax.ShapeDtypeStruct((s, 512), F32), jax.ShapeDtypeStruct((s, 512), F32)],
        compiler_params=_params(("arbitrary",), VMEM_LIMIT_LARGE),
    )(x, mod, g_norm1, w_in)


def _t5_bucket(dist):
    max_exact = N_BUCKETS // 2
    is_small = dist < max_exact
    d = jnp.maximum(dist, 1).astype(F32)
    large = max_exact + (jnp.log(d / max_exact) / math.log(MAX_DISTANCE / max_exact)
                         * (N_BUCKETS - max_exact)).astype(jnp.int32)
    large = jnp.minimum(large, N_BUCKETS - 1)
    return jnp.where(is_small, dist, large)


def _bucket_table():
    qi = jnp.arange(BLOCK, dtype=jnp.int32)[:, None]
    sj = jnp.arange(2 * BLOCK, dtype=jnp.int32)[None, :]
    return _t5_bucket(jnp.maximum(qi + BLOCK - sj, 0))


def _window_mask():
    qi = lax.broadcasted_iota(jnp.int32, (BLOCK, 2 * BLOCK), 0)
    sj = lax.broadcasted_iota(jnp.int32, (BLOCK, 2 * BLOCK), 1)
    dist = qi + BLOCK - sj
    return (dist >= 0) & (dist < BLOCK)


def _bias_table(rel_bias, bucket):
    def body(rb_ref, bk_ref, o_ref):
        bk = bk_ref[...]
        inside = _window_mask()
        for h in range(N_Q_HEADS):
            acc = jnp.zeros((BLOCK, 2 * BLOCK), F32)
            for b in range(N_BUCKETS):
                acc = jnp.where(bk == b, rb_ref[b, h], acc)
            o_ref[h] = jnp.where(inside, acc, NEG_INF)

    return pl.pallas_call(
        body, name="bias_table",
        in_specs=[pl.BlockSpec(memory_space=pltpu.SMEM), pl.BlockSpec(memory_space=pltpu.VMEM)],
        out_shape=jax.ShapeDtypeStruct((N_Q_HEADS, BLOCK, 2 * BLOCK), F32),
    )(rel_bias, bucket)


def _load_kv_window(kv_ref, n):
    prev = jnp.maximum(n - 1, 0)
    kvw = jnp.concatenate([kv_ref[pl.ds(pl.multiple_of(prev * BLOCK, BLOCK), BLOCK), :],
                           kv_ref[pl.ds(pl.multiple_of(n * BLOCK, BLOCK), BLOCK), :]], axis=0)
    k, v = kvw[:, 0:128], kvw[:, 128:256]
    k_sw = pltpu.roll(k.astype(F32), 64, 1).astype(BF16)
    v_sw = pltpu.roll(v.astype(F32), 64, 1).astype(BF16)
    return (k, k_sw), (v, v_sw)


def _conv_taps(gc, xc, gc_prev, xc_prev, n):
    u = gc * xc
    before = jnp.where(n > 0, gc_prev * xc_prev, 0.0)
    row = lax.broadcasted_iota(jnp.int32, u.shape, 0)
    u1 = jnp.where(row == 0, before[7:8, :], pltpu.roll(u, 1, 0))
    u2 = jnp.where(row == 0, before[6:7, :], jnp.where(row == 1, before[7:8, :], pltpu.roll(u, 2, 0)))
    return u, u1, u2


def _mixer_fwd(q, kv, gb, gc, xc, bias, sinks, conv_w, g_attn, g_conv):
    s = q.shape[0]
    nb = s // BLOCK

    def body(sink_ref, q_ref, kv_ref, gb_ref, gc_ref, xc_ref, gcp_ref, xcp_ref, bias_ref, cw_ref, ga_ref, gcv_ref,
             attn_ref, merged_ref, lse_ref):
        n = pl.program_id(0)
        ks, vs = _load_kv_window(kv_ref, n)
        lane = lax.broadcasted_iota(jnp.int32, (BLOCK, BLOCK), 1)
        low = lane < HEAD_DIM
        col = lax.broadcasted_iota(jnp.int32, (BLOCK, 2 * BLOCK), 1)
        no_prev = (col < BLOCK) & (n == 0)
        lse_all = jnp.zeros((BLOCK, BLOCK), F32)
        pairs = []
        for p in range(4):
            qp = q_ref[:, 128 * p:128 * (p + 1)].astype(F32)
            kvh = p // 2
            res = []
            for e in range(2):
                h = 2 * p + e
                qm = jnp.where(low if e == 0 else ~low, qp, 0.0).astype(BF16)
                sw = 0 if kvh == e else 1
                sc = _dot_nt(qm, ks[sw]) * SCALE + bias_ref[h]
                sc = jnp.where(no_prev, NEG_INF, sc)
                sink = sink_ref[h]
                m = jnp.maximum(jnp.max(sc, axis=-1, keepdims=True), sink)
                pe = jnp.exp(sc - m)
                den = jnp.sum(pe, axis=-1, keepdims=True) + jnp.exp(sink - m)
                res.append(_dot(pe.astype(BF16), vs[sw]) / den)
                lse_all = lse_all + jnp.where(lane == h, m + jnp.log(den), 0.0)
            pairs.append(jnp.where(low, res[0], res[1]))
        attn = jnp.concatenate(pairs, axis=1)
        attn_ref[...] = attn
        lse_ref[...] = lse_all
        u, u1, u2 = _conv_taps(gc_ref[...], xc_ref[...], gcp_ref[...], xcp_ref[...], n)
        cw = cw_ref[...]
        cv = gb_ref[...] * (cw[0:1, :] * u2 + cw[1:2, :] * u1 + cw[2:3, :] * u)
        an = attn * _rsqrt_mean_sq(attn) * ga_ref[...]
        cn = cv * _rsqrt_mean_sq(cv) * gcv_ref[...]
        merged_ref[...] = jnp.concatenate([an, cn], axis=1).astype(BF16)

    blk = lambda w: pl.BlockSpec((BLOCK, w), lambda n: (n, 0))
    prev8 = pl.BlockSpec((8, 512), lambda n: (jnp.maximum(n * (BLOCK // 8) - 1, 0), 0))
    return pl.pallas_call(
        body, name="mixer_fwd", grid=(nb,),
        in_specs=[pl.BlockSpec(memory_space=pltpu.SMEM), blk(512), _full((s, 256)), blk(512), blk(512), blk(512),
                  prev8, prev8, _full((N_Q_HEADS, BLOCK, 2 * BLOCK)), _full((3, 512)), _full((1, 512)),
                  _full((1, 512))],
        out_specs=[blk(512), blk(1024), blk(128)],
        out_shape=[jax.ShapeDtypeStruct((s, 512), F32), jax.ShapeDtypeStruct((s, 1024), BF16),
                   jax.ShapeDtypeStruct((s, 128), F32)],
        compiler_params=_params(("arbitrary",)),
    )(sinks, q, kv, gb, gc, xc, gc, xc, bias, conv_w, g_attn, g_conv)


def _out_proj(merged, x, mod, w_out, tm):
    s = x.shape[0]

    def body(m_ref, x_ref, mod_ref, w_ref, o_ref, x1_ref):
        o = _dot(m_ref[...], w_ref[...])
        o_ref[...] = o.astype(BF16)
        x1_ref[...] = x_ref[...] + mod_ref[G1:G1 + 1, :] * o

    return pl.pallas_call(
        body, name="out_proj", grid=(s // tm,),
        in_specs=[_rows(tm, D_MODEL), _rows(tm, D_MODEL), _full((8, D_MODEL)), _full((D_MODEL, D_MODEL))],
        out_specs=[_rows(tm, D_MODEL), _rows(tm, D_MODEL)],
        out_shape=[jax.ShapeDtypeStruct((s, D_MODEL), BF16), jax.ShapeDtypeStruct((s, D_MODEL), F32)],
        compiler_params=_params(("arbitrary",)),
    )(merged, x, mod, w_out)


def _ffn_up(x1, mod, g_norm2, w_gu, tm, tn):
    s = x1.shape[0]
    nj = D_FF // tn

    def body(x_ref, mod_ref, g_ref, wg_ref, wu_ref, h_ref, gate_ref, up_ref, act_ref):
        @pl.when(pl.program_id(1) == 0)
        def _():
            xf = x_ref[...]
            n = xf * _rsqrt_mean_sq(xf) * g_ref[...]
            h_ref[...] = (n * (1.0 + mod_ref[SC2:SC2 + 1, :]) + mod_ref[SH2:SH2 + 1, :]).astype(BF16)

        h = h_ref[...]
        gate = _dot_nt(h, wg_ref[...])
        up = _dot_nt(h, wu_ref[...])
        gate_ref[...] = gate.astype(BF16)
        up_ref[...] = up.astype(BF16)
        act_ref[...] = (gate * _sigmoid(gate) * up).astype(BF16)

    tile = pl.BlockSpec((tm, tn), lambda i, j: (i, j))
    return pl.pallas_call(
        body, name="ffn_up", grid=(s // tm, nj),
        in_specs=[_rows(tm, D_MODEL), _full((8, D_MODEL)), _full((1, D_MODEL)),
                  pl.BlockSpec((tn, D_MODEL), lambda i, j: (j, 0)),
                  pl.BlockSpec((tn, D_MODEL), lambda i, j: (j + nj, 0))],
        out_specs=[_rows(tm, D_MODEL), tile, tile, tile],
        out_shape=[jax.ShapeDtypeStruct((s, D_MODEL), BF16)] + [jax.ShapeDtypeStruct((s, D_FF), BF16)] * 3,
        compiler_params=_params(("arbitrary", "arbitrary"), VMEM_LIMIT_LARGE),
    )(x1, mod, g_norm2, w_gu, w_gu)


def _ffn_down_loss(act, x1, mod, w_down, g_final, target, tm):
    s = x1.shape[0]

    def body(a_ref, x1_ref, mod_ref, w_ref, gf_ref, t_ref, o_ref, dx2_ref, small_ref):
        @pl.when(pl.program_id(0) == 0)
        def _():
            small_ref[...] = jnp.zeros_like(small_ref)

        o = _dot(a_ref[...], w_ref[...])
        o_ref[...] = o.astype(BF16)
        x2 = x1_ref[...] + mod_ref[G2:G2 + 1, :] * o
        r = _rsqrt_mean_sq(x2)
        xn = x2 * r
        gf = gf_ref[...]
        err = xn * gf - t_ref[...]
        dy = err * (1.0 / D_MODEL)
        dxn = dy * gf
        dx2_ref[...] = r * (dxn - xn * jnp.mean(dxn * xn, axis=-1, keepdims=True))
        small_ref[0:1, :] += _colsum(dy * xn)
        small_ref[1:2, :] += _colsum(err * err)

        @pl.when(pl.program_id(0) == pl.num_programs(0) - 1)
        def _():
            total = jnp.sum(small_ref[1:2, :], axis=-1, keepdims=True) * (0.5 / D_MODEL)
            small_ref[2:3, :] = jnp.broadcast_to(total, (1, D_MODEL))

    return pl.pallas_call(
        body, name="ffn_down_loss", grid=(s // tm,),
        in_specs=[_rows(tm, D_FF), _rows(tm, D_MODEL), _full((8, D_MODEL)), _full((D_FF, D_MODEL)),
                  _full((1, D_MODEL)), _rows(tm, D_MODEL)],
        out_specs=[_rows(tm, D_MODEL), _rows(tm, D_MODEL), _full((8, D_MODEL))],
        out_shape=[jax.ShapeDtypeStruct((s, D_MODEL), BF16), jax.ShapeDtypeStruct((s, D_MODEL), F32),
                   jax.ShapeDtypeStruct((8, D_MODEL), F32)],
        compiler_params=_params(("arbitrary",), VMEM_LIMIT_LARGE),
    )(act, x1, mod, w_down, g_final, target)


def _ffn_down_bwd(dx2, o2, gate, up, mod, w_down, tm):
    s = dx2.shape[0]

    def body(dx_ref, o_ref, gate_ref, up_ref, mod_ref, w_ref, do_ref, dgu_ref, small_ref):
        @pl.when(pl.program_id(0) == 0)
        def _():
            small_ref[...] = jnp.zeros_like(small_ref)

        dx = dx_ref[...]
        small_ref[0:1, :] += _colsum(dx * o_ref[...].astype(F32))
        do = (dx * mod_ref[G2:G2 + 1, :]).astype(BF16)
        do_ref[...] = do
        dact = _dot_nt(do, w_ref[...])
        gate = gate_ref[...].astype(F32)
        sg = _sigmoid(gate)
        dgu_ref[:, 0:D_FF] = (dact * up_ref[...].astype(F32) * (sg * (1.0 + gate * (1.0 - sg)))).astype(BF16)
        dgu_ref[:, D_FF:2 * D_FF] = (dact * (gate * sg)).astype(BF16)

    return pl.pallas_call(
        body, name="ffn_down_bwd", grid=(s // tm,),
        in_specs=[_rows(tm, D_MODEL), _rows(tm, D_MODEL), _rows(tm, D_FF), _rows(tm, D_FF), _full((8, D_MODEL)),
                  _full((D_FF, D_MODEL))],
        out_specs=[_rows(tm, D_MODEL), _rows(tm, 2 * D_FF), _full((8, D_MODEL))],
        out_shape=[jax.ShapeDtypeStruct((s, D_MODEL), BF16), jax.ShapeDtypeStruct((s, 2 * D_FF), BF16),
                   jax.ShapeDtypeStruct((8, D_MODEL), F32)],
        compiler_params=_params(("arbitrary",), VMEM_LIMIT_LARGE),
    )(dx2, o2, gate, up, mod, w_down)


def _norm_mod_bwd(dh, xf, g, scale_row, small_ref):
    r = _rsqrt_mean_sq(xf)
    xn = xf * r
    small_ref[0:1, :] += _colsum(dh)
    small_ref[1:2, :] += _colsum(dh * (xn * g))
    dn = dh * (1.0 + scale_row)
    small_ref[2:3, :] += _colsum(dn * xn)
    dxn = dn * g
    return r * (dxn - xn * jnp.mean(dxn * xn, axis=-1, keepdims=True))


def _ffn_up_bwd(dgu, x1, dx2, mod, g_norm2, w_gu, tm, tk):
    s = x1.shape[0]
    nk = (2 * D_FF) // tk

    def body(dgu_ref, x_ref, dx2_ref, mod_ref, g_ref, w_ref, dx1_ref, small_ref, acc_ref):
        i, k = pl.program_id(0), pl.program_id(1)

        @pl.when((i == 0) & (k == 0))
        def _():
            small_ref[...] = jnp.zeros_like(small_ref)

        part = _dot(dgu_ref[...], w_ref[...])

        @pl.when(k == 0)
        def _():
            acc_ref[...] = part

        @pl.when(k > 0)
        def _():
            acc_ref[...] += part

        @pl.when(k == nk - 1)
        def _():
            dx1_ref[...] = dx2_ref[...] + _norm_mod_bwd(acc_ref[...], x_ref[...], g_ref[...],
                                                        mod_ref[SC2:SC2 + 1, :], small_ref)

    return pl.pallas_call(
        body, name="ffn_up_bwd", grid=(s // tm, nk),
        in_specs=[pl.BlockSpec((tm, tk), lambda i, k: (i, k)), _rows(tm, D_MODEL), _rows(tm, D_MODEL),
                  _full((8, D_MODEL)), _full((1, D_MODEL)), pl.BlockSpec((tk, D_MODEL), lambda i, k: (k, 0))],
        out_specs=[_rows(tm, D_MODEL), _full((8, D_MODEL))],
        out_shape=[jax.ShapeDtypeStruct((s, D_MODEL), F32), jax.ShapeDtypeStruct((8, D_MODEL), F32)],
        scratch_shapes=[pltpu.VMEM((tm, D_MODEL), F32)],
        compiler_params=_params(("arbitrary", "arbitrary"), VMEM_LIMIT_LARGE),
    )(dgu, x1, dx2, mod, g_norm2, w_gu)


def _out_proj_bwd(dx1, o1, mod, w_out, tm):
    s = dx1.shape[0]

    def body(dx_ref, o_ref, mod_ref, w_ref, do_ref, dm_ref, small_ref):
        @pl.when(pl.program_id(0) == 0)
        def _():
            small_ref[...] = jnp.zeros_like(small_ref)

        dx = dx_ref[...]
        small_ref[0:1, :] += _colsum(dx * o_ref[...].astype(F32))
        do = (dx * mod_ref[G1:G1 + 1, :]).astype(BF16)
        do_ref[...] = do
        dm_ref[...] = _dot_nt(do, w_ref[...])

    return pl.pallas_call(
        body, name="out_proj_bwd", grid=(s // tm,),
        in_specs=[_rows(tm, D_MODEL), _rows(tm, D_MODEL), _full((8, D_MODEL)), _full((D_MODEL, D_MODEL))],
        out_specs=[_rows(tm, D_MODEL), _rows(tm, D_MODEL), _full((8, D_MODEL))],
        out_shape=[jax.ShapeDtypeStruct((s, D_MODEL), BF16), jax.ShapeDtypeStruct((s, D_MODEL), F32),
                   jax.ShapeDtypeStruct((8, D_MODEL), F32)],
        compiler_params=_params(("arbitrary",)),
    )(dx1, o1, mod, w_out)


def _group_norm_bwd(dm, a, g):
    r = _rsqrt_mean_sq(a)
    an = a * r
    dan = dm * g
    return r * (dan - an * jnp.mean(dan * an, axis=-1, keepdims=True)), _colsum(dm * an)


def _mixer_bwd(q, kv, gb, gc, xc, bias, sinks, conv_w, g_attn, g_conv, attn, lse, dmerged):
    s = q.shape[0]
    nb = s // BLOCK

    def body(sink_ref, q_ref, kv_ref, gb_ref, gc_ref, xc_ref, gcp_ref, xcp_ref, bias_ref, cw_ref, ga_ref, gcv_ref,
             attn_ref, lse_ref, dm_ref,
             dq_ref, dkv_ref, dgb_ref, dgc_ref, dxc_ref, dbias_ref, dsink_ref, small_ref, carry_ref):
        step = pl.program_id(0)
        n = nb - 1 - step

        @pl.when(step == 0)
        def _():
            dkv_ref[...] = jnp.zeros_like(dkv_ref)
            dbias_ref[...] = jnp.zeros_like(dbias_ref)
            dsink_ref[...] = jnp.zeros_like(dsink_ref)
            small_ref[...] = jnp.zeros_like(small_ref)
            carry_ref[...] = jnp.zeros_like(carry_ref)

        dm = dm_ref[...]
        gbv, gcv_, xcv = gb_ref[...], gc_ref[...], xc_ref[...]
        u, u1, u2 = _conv_taps(gcv_, xcv, gcp_ref[...], xcp_ref[...], n)
        cw = cw_ref[...]
        yv = cw[0:1, :] * u2 + cw[1:2, :] * u1 + cw[2:3, :] * u
        dcv, dg_conv = _group_norm_bwd(dm[:, 512:1024], gbv * yv, gcv_ref[...])
        small_ref[1:2, :] += dg_conv
        dgb_ref[...] = (dcv * yv).astype(BF16)
        dy = dcv * gbv
        nxt = carry_ref[...]
        row = lax.broadcasted_iota(jnp.int32, dy.shape, 0)
        d1 = jnp.where(row == BLOCK - 1, nxt[0:1, :], pltpu.roll(dy, BLOCK - 1, 0))
        d2 = jnp.where(row == BLOCK - 2, nxt[0:1, :],
                       jnp.where(row == BLOCK - 1, nxt[1:2, :], pltpu.roll(dy, BLOCK - 2, 0)))
        du = cw[2:3, :] * dy + cw[1:2, :] * d1 + cw[0:1, :] * d2
        dgc_ref[...] = (du * xcv).astype(BF16)
        dxc_ref[...] = (du * gcv_).astype(BF16)
        small_ref[2:3, :] += _colsum(dy * u2)
        small_ref[3:4, :] += _colsum(dy * u1)
        small_ref[4:5, :] += _colsum(dy * u)
        carry_ref[...] = dy[0:8, :]

        attn_v = attn_ref[...]
        dout, dg_attn = _group_norm_bwd(dm[:, 0:512], attn_v, ga_ref[...])
        small_ref[0:1, :] += dg_attn
        ks, vs = _load_kv_window(kv_ref, n)
        lane = lax.broadcasted_iota(jnp.int32, (BLOCK, BLOCK), 1)
        low = lane < HEAD_DIM
        col = lax.broadcasted_iota(jnp.int32, (BLOCK, 2 * BLOCK), 1)
        no_prev = (col < BLOCK) & (n == 0)
        lse_all = lse_ref[...]
        dk = jnp.zeros((2 * BLOCK, BLOCK), F32)
        dv = jnp.zeros((2 * BLOCK, BLOCK), F32)
        dsink = jnp.zeros((BLOCK, BLOCK), F32)
        dq_pairs = []
        for p in range(4):
            qp = q_ref[:, 128 * p:128 * (p + 1)].astype(F32)
            do_p = dout[:, 128 * p:128 * (p + 1)]
            prod = do_p * attn_v[:, 128 * p:128 * (p + 1)]
            kvh = p // 2
            res = []
            for e in range(2):
                h = 2 * p + e
                half = low if e == 0 else ~low
                qm = jnp.where(half, qp, 0.0).astype(BF16)
                dom = jnp.where(half, do_p, 0.0).astype(BF16)
                delta = jnp.sum(jnp.where(half, prod, 0.0), axis=-1, keepdims=True)
                lse_h = jnp.sum(jnp.where(lane == h, lse_all, 0.0), axis=-1, keepdims=True)
                sw = 0 if kvh == e else 1
                sc = _dot_nt(qm, ks[sw]) * SCALE + bias_ref[h]
                sc = jnp.where(no_prev, NEG_INF, sc)
                pr = jnp.exp(sc - lse_h)
                dp = _dot_nt(dom, vs[sw])
                ds = pr * (dp - delta)
                dbias_ref[h] += ds
                dsink = dsink + jnp.where(lane == h, -jnp.exp(sink_ref[h] - lse_h) * delta, 0.0)
                dsb = ds.astype(BF16)
                res.append(_dot(dsb, ks[sw]) * SCALE)
                dk_h = _dot_tn(dsb, qm) * SCALE
                dv_h = _dot_tn(pr.astype(BF16), dom)
                if sw:
                    dk_h = pltpu.roll(dk_h, 64, 1)
                    dv_h = pltpu.roll(dv_h, 64, 1)
                dk = dk + dk_h
                dv = dv + dv_h
            dq_pairs.append(jnp.where(low, res[0], res[1]))
        dq_ref[...] = jnp.concatenate(dq_pairs, axis=1).astype(BF16)
        dsink_ref[...] += dsink
        dkv_win = jnp.concatenate([dk, dv], axis=1)
        prev = jnp.maximum(n - 1, 0)
        dkv_ref[pl.ds(pl.multiple_of(prev * BLOCK, BLOCK), BLOCK), :] += dkv_win[0:BLOCK, :]
        dkv_ref[pl.ds(pl.multiple_of(n * BLOCK, BLOCK), BLOCK), :] += dkv_win[BLOCK:2 * BLOCK, :]

        @pl.when(step == nb - 1)
        def _():
            small_ref[5:6, :] = jnp.concatenate([_colsum(dsink_ref[...]), jnp.zeros((1, 512 - BLOCK), F32)], axis=1)

    blk = lambda w: pl.BlockSpec((BLOCK, w), lambda t: (nb - 1 - t, 0))
    prev8 = pl.BlockSpec((8, 512), lambda t: (jnp.maximum((nb - 1 - t) * (BLOCK // 8) - 1, 0), 0))
    bf = lambda w: jax.ShapeDtypeStruct((s, w), BF16)
    return pl.pallas_call(
        body, name="mixer_bwd", grid=(nb,),
        in_specs=[pl.BlockSpec(memory_space=pltpu.SMEM), blk(512), _full((s, 256)), blk(512), blk(512), blk(512),
                  prev8, prev8, _full((N_Q_HEADS, BLOCK, 2 * BLOCK)), _full((3, 512)), _full((1, 512)),
                  _full((1, 512)), blk(512), blk(128), blk(1024)],
        out_specs=[blk(512), _full((s, 256)), blk(512), blk(512), blk(512), _full((N_Q_HEADS, BLOCK, 2 * BLOCK)),
                   _full((BLOCK, BLOCK)), _full((8, 512))],
        out_shape=[bf(512), jax.ShapeDtypeStruct((s, 256), F32), bf(512), bf(512), bf(512),
                   jax.ShapeDtypeStruct((N_Q_HEADS, BLOCK, 2 * BLOCK), F32), jax.ShapeDtypeStruct((BLOCK, BLOCK), F32),
                   jax.ShapeDtypeStruct((8, 512), F32)],
        scratch_shapes=[pltpu.VMEM((8, 512), F32)],
        compiler_params=_params(("arbitrary",), VMEM_LIMIT_LARGE),
    )(sinks, q, kv, gb, gc, xc, gc, xc, bias, conv_w, g_attn, g_conv, attn, lse, dmerged)


def _in_proj_bwd(dq, dkv, dgb, dgc, dxc, x, dx1, mod, g_norm1, w_in, tm):
    s = x.shape[0]

    def body(dq_ref, dkv_ref, dgb_ref, dgc_ref, dxc_ref, x_ref, dx1_ref, mod_ref, g_ref, w_ref,
             dproj_ref, dx_ref, small_ref):
        @pl.when(pl.program_id(0) == 0)
        def _():
            small_ref[...] = jnp.zeros_like(small_ref)

        dproj = jnp.concatenate([dq_ref[...], dkv_ref[...].astype(BF16), dgb_ref[...], dgc_ref[...], dxc_ref[...]],
                                axis=1)
        dproj_ref[...] = dproj
        dh = _dot(dproj, w_ref[...])
        dx_ref[...] = dx1_ref[...] + _norm_mod_bwd(dh, x_ref[...], g_ref[...], mod_ref[SC1:SC1 + 1, :], small_ref)

    return pl.pallas_call(
        body, name="in_proj_bwd", grid=(s // tm,),
        in_specs=[_rows(tm, 512), _rows(tm, 256), _rows(tm, 512), _rows(tm, 512), _rows(tm, 512),
                  _rows(tm, D_MODEL), _rows(tm, D_MODEL), _full((8, D_MODEL)), _full((1, D_MODEL)),
                  _full((IN_PROJ_WIDTH, D_MODEL))],
        out_specs=[_rows(tm, IN_PROJ_WIDTH), _rows(tm, D_MODEL), _full((8, D_MODEL))],
        out_shape=[jax.ShapeDtypeStruct((s, IN_PROJ_WIDTH), BF16), jax.ShapeDtypeStruct((s, D_MODEL), F32),
                   jax.ShapeDtypeStruct((8, D_MODEL), F32)],
        compiler_params=_params(("arbitrary",), VMEM_LIMIT_LARGE),
    )(dq, dkv, dgb, dgc, dxc, x, dx1, mod, g_norm1, w_in)


def _weight_grad(a, b, tk, ts, name, after=None):
    s, k = a.shape
    n = b.shape[1]
    nt = s // ts
    extra = [] if after is None else [after]

    def body(a_ref, b_ref, *rest):
        o_ref, acc_ref = rest[-2:]
        t = pl.program_id(1)
        part = _dot_tn(a_ref[...], b_ref[...])

        @pl.when(t == 0)
        def _():
            acc_ref[...] = part

        @pl.when(t > 0)
        def _():
            acc_ref[...] += part

        @pl.when(t == nt - 1)
        def _():
            o_ref[...] = acc_ref[...].astype(BF16)

    return pl.pallas_call(
        body, name=name, grid=(k // tk, nt),
        in_specs=[pl.BlockSpec((ts, tk), lambda i, t: (t, i)), pl.BlockSpec((ts, n), lambda i, t: (t, 0))]
        + [pl.BlockSpec(memory_space=pl.ANY)] * len(extra),
        out_specs=pl.BlockSpec((tk, n), lambda i, t: (i, 0)),
        out_shape=jax.ShapeDtypeStruct((k, n), BF16),
        scratch_shapes=[pltpu.VMEM((tk, n), F32)],
        compiler_params=_params(("arbitrary", "arbitrary"), VMEM_LIMIT_LARGE),
    )(a, b, *extra)


def _rel_bias_grad(dbias, bucket):
    def body(db_ref, bk_ref, o_ref, rows_ref):
        bk = bk_ref[...]
        for b in range(N_BUCKETS):
            sel = (bk == b).astype(F32)
            for h in range(N_Q_HEADS):
                rows_ref[8 * b + h:8 * b + h + 1, :] = _colsum(db_ref[h] * sel)
        o_ref[...] = jnp.sum(rows_ref[...], axis=-1, keepdims=True)

    return pl.pallas_call(
        body, name="rel_bias_grad",
        out_shape=jax.ShapeDtypeStruct((N_BUCKETS * N_Q_HEADS, 1), F32),
        scratch_shapes=[pltpu.VMEM((N_BUCKETS * N_Q_HEADS, 2 * BLOCK), F32)],
    )(dbias, bucket)


def _sum_slots(parts, after):
    def body(p_ref, after_ref, o_ref):
        acc = p_ref[0]
        for k in range(1, N_DEV):
            acc = acc + p_ref[k]
        o_ref[...] = acc

    return pl.pallas_call(body, name="sum_small_grads",
                          in_specs=[pl.BlockSpec(memory_space=pltpu.VMEM), pl.BlockSpec(memory_space=pl.ANY)],
                          out_shape=jax.ShapeDtypeStruct(parts.shape[1:], F32))(parts, after)


def _w_ada_grad(cond_t, dmod_cols):
    def body(c_ref, d_ref, o_ref):
        o_ref[...] = _dot(c_ref[...], d_ref[...])

    return pl.pallas_call(body, name="w_ada_grad",
                          out_shape=jax.ShapeDtypeStruct((cond_t.shape[0], dmod_cols.shape[1]), F32))(cond_t, dmod_cols)


def _adam_math(w, g, m, v):
    m = ADAM_B1 * m + (1.0 - ADAM_B1) * g
    v = ADAM_B2 * v + (1.0 - ADAM_B2) * (g * g)
    m_hat = m / (1.0 - ADAM_B1 ** ADAM_STEP)
    v_hat = v / (1.0 - ADAM_B2 ** ADAM_STEP)
    delta = -ADAM_LR * (m_hat / (jnp.sqrt(v_hat) + ADAM_EPS) + ADAM_WD * w)
    return delta, m, v


def _sum_parts(local, land, me, tr, name):
    r, c = local.shape[1:]

    def body(me_ref, own_ref, land_ref, o_ref):
        acc = own_ref[0].astype(F32)
        for k in range(N_DEV - 1):
            acc = acc + land_ref[k].astype(F32)
        o_ref[...] = acc

    return pl.pallas_call(
        body, name=name,
        grid_spec=pltpu.PrefetchScalarGridSpec(
            num_scalar_prefetch=1, grid=(r // tr,),
            in_specs=[pl.BlockSpec((1, tr, c), lambda i, me_ref: (me_ref[0], i, 0)),
                      pl.BlockSpec((N_DEV - 1, tr, c), lambda i, me_ref: (0, i, 0))],
            out_specs=pl.BlockSpec((tr, c), lambda i, me_ref: (i, 0))),
        out_shape=jax.ShapeDtypeStruct((r, c), F32),
        compiler_params=_params(("arbitrary",)),
    )(me, local, land)


def _adamw(w, m, v, g, tr, name):
    r, c = w.shape

    def body(w_ref, m_ref, v_ref, g_ref, d_ref, mo_ref, vo_ref):
        d_ref[...], mo_ref[...], vo_ref[...] = _adam_math(w_ref[...], g_ref[...], m_ref[...], v_ref[...])

    tile = pl.BlockSpec((tr, c), lambda i: (i, 0))
    return pl.pallas_call(
        body, name=name, grid=(r // tr,),
        in_specs=[tile] * 4, out_specs=[tile] * 3,
        out_shape=[jax.ShapeDtypeStruct((r, c), F32)] * 3,
        compiler_params=_params(("arbitrary",)),
    )(w, m, v, g)


def _behind(a, token):
    return a + token[0:a.shape[0], 0:1]


def _local_step(x, target, mod, w_in_t, weights_out_gu, weights_down, rel_bias, g_norm1, sinks, conv_w, g_attn,
                g_conv, g_norm2, g_final, exchange):
    s = x.shape[0]
    tm = min(512, s)
    tm_small = min(256, s)
    bucket = _bucket_table()
    bias = _bias_table(rel_bias, bucket)

    h, q, kv, gb, gc, xc = _in_proj(x, mod, g_norm1, w_in_t, tm)
    attn, merged, lse = _mixer_fwd(q, kv, gb, gc, xc, bias, sinks, conv_w, g_attn, g_conv)
    w_out, w_gu_t = weights_out_gu(merged)
    o1, x1 = _out_proj(merged, x, mod, w_out, tm)
    h2, gate, up, act = _ffn_up(x1, mod, g_norm2, w_gu_t, tm, D_FF // 2)
    w_down = weights_down(act)
    o2, dx2, fin = _ffn_down_loss(act, x1, mod, w_down, g_final, target, tm)

    do2, dgu, sm_g2 = _ffn_down_bwd(dx2, o2, gate, up, mod, w_down, tm_small)
    ts = min(WEIGHT_GRAD_ROWS, s)
    mod = _behind(mod, exchange("w_down", _weight_grad(act, do2, D_FF // 2, ts, "w_down_grad")))
    dx1, sm_2 = _ffn_up_bwd(dgu, x1, dx2, mod, g_norm2, w_gu_t, tm, D_FF // 2)
    mod = _behind(mod, exchange("w_gu", _weight_grad(dgu, h2, D_FF // 2, ts, "w_gu_grad")))
    do1, dmerged, sm_g1 = _out_proj_bwd(dx1, o1, mod, w_out, tm)
    g_attn_b = _behind(g_attn, exchange("w_out", _weight_grad(merged, do1, D_MODEL, ts, "w_out_grad")))
    dq, dkv, dgb, dgc, dxc, dbias, dsink, sm_mix = _mixer_bwd(
        q, kv, gb, gc, xc, bias, sinks, conv_w, g_attn_b, g_conv, attn, lse, dmerged)
    dproj, dx, sm_1 = _in_proj_bwd(dq, dkv, dgb, dgc, dxc, x, dx1, mod, g_norm1, w_in_t, tm)
    d_rel = _rel_bias_grad(dbias, bucket)

    packed = jnp.concatenate([
        sm_1[0], sm_1[1], sm_g1[0], sm_2[0], sm_2[1], sm_g2[0],
        d_rel[:, 0],
        sm_1[2],
        sm_mix[5, 0:128],
        sm_mix[0], sm_mix[1],
        sm_2[2],
        fin[0],
        sm_mix[2], sm_mix[3], sm_mix[4],
        fin[2, 0:128],
    ])[None, :]
    return dx, dproj, h, packed


def kernel(x, c, rel_bias, w_ada, b_ada, g_norm1, w_in, sinks, conv_w, g_attn_out, g_conv_out, w_out, g_norm2, w_gu, w_down, g_final, loss_target, m_rel_bias, m_w_ada, m_b_ada, m_g_norm1, m_w_in, m_sinks, m_conv_w, m_g_attn_out, m_g_conv_out, m_w_out, m_g_norm2, m_w_gu, m_w_down, m_g_final, v_rel_bias, v_w_ada, v_b_ada, v_g_norm1, v_w_in, v_sinks, v_conv_w, v_g_attn_out, v_g_conv_out, v_w_out, v_g_norm2, v_w_gu, v_w_down, v_g_final):
    me = _linear(_mesh_position())
    me_arr = jnp.reshape(me, (1,)).astype(jnp.int32)
    ada_cols = w_ada.shape[2]
    tm = min(512, x.shape[1])

    cond = _silu_rows(c)
    cond_all, conv_w_all = _all_gather([cond, conv_w[0]], "gather_cond", to_bf16=False, big=False)
    cond_all = cond_all[:, 0, :]
    conv_cols = conv_w.shape[2]
    conv_w_full = conv_w_all.transpose(1, 0, 2).reshape(3, CONV_WIDTH)
    b_cols = lax.dynamic_slice_in_dim(b_ada, me * ada_cols, ada_cols, axis=1)
    mod_cols = _mod_columns(cond_all, w_ada[0], b_cols)
    mod_all = _all_gather([mod_cols], "gather_mod", to_bf16=False, big=False)[0]
    mod = lax.dynamic_index_in_dim(mod_all, me, axis=1, keepdims=False).reshape(N_MOD, D_MODEL)
    mod = jnp.concatenate([mod, jnp.zeros((2, D_MODEL), F32)], axis=0)

    w_in_t = _all_gather([w_in[0].T], "gather_w_in", to_bf16=True, big=True)[0].reshape(IN_PROJ_WIDTH, D_MODEL)
    gather_sems, staged, gather_token = _gather_start(
        _stage_blocks([w_out[0], w_gu[0].T, w_down[0]], "stage_weights"), "gather_start_weights")
    mod = _behind(mod, gather_token)

    def weights_out_gu(after):
        got = _gather_pass_on(_gather_wait(gather_sems[0:4], staged[0:2], [after], "gather_wait_out_gu"),
                              "gather_pass_on_out_gu")
        return got[0].reshape(D_MODEL, D_MODEL), got[1].reshape(2 * D_FF, D_MODEL)

    def weights_down(after):
        got = _gather_pass_on(_gather_wait(gather_sems[4:6], staged[2:3], [after], "gather_wait_down"),
                              "gather_pass_on_down")
        return got[0].reshape(D_FF, D_MODEL)

    started = {}

    def exchange(name, dw):
        st = _exchange_start(dw.reshape(N_DEV, dw.shape[0] // N_DEV, dw.shape[1]), "exchange_start_" + name)
        started[name] = st
        return st[4]

    dx, dproj, h, packed = _local_step(
        x[0], loss_target[0], mod, w_in_t, weights_out_gu, weights_down, rel_bias, g_norm1, sinks[0], conv_w_full,
        g_attn_out, g_conv_out, g_norm2, g_final[None, :], exchange)

    packed_all = _all_gather([packed], "gather_small_grads", to_bf16=False, big=False)[0]
    tok_in = exchange("w_in", _weight_grad(dproj, h, IN_PROJ_WIDTH // 2, min(WEIGHT_GRAD_ROWS, x.shape[1]),
                                           "w_in_grad", after=packed_all))
    small = _sum_slots(packed_all, tok_in)[0]
    dmod_all = packed_all[:, 0, OFF_DMOD:OFF_DMOD + N_MOD * D_MODEL]
    dmod_cols = lax.dynamic_slice_in_dim(dmod_all, me * ada_cols, ada_cols, axis=1)
    cond_t = jnp.zeros((D_MODEL, 128), F32).at[:, 0:N_DEV].set(cond_all.T)
    dmod_pad = jnp.zeros((128, ada_cols), F32).at[0:N_DEV, :].set(dmod_cols)
    g_ada = _w_ada_grad(cond_t, dmod_pad)
    d_ada, nm_ada, nv_ada = _adamw(w_ada[0], m_w_ada[0], v_w_ada[0], g_ada, 256, "adamw_w_ada")

    loss = small[OFF_LOSS]
    seg = lambda off, n: small[off:off + n]
    conv_g_full = seg(OFF_CONVW, 3 * CONV_WIDTH).reshape(3, CONV_WIDTH)
    small_grads = {
        "rel_bias": seg(OFF_RELB, 256).reshape(N_BUCKETS, N_Q_HEADS),
        "b_ada": seg(OFF_DMOD, N_MOD * D_MODEL).reshape(1, N_MOD * D_MODEL),
        "g_norm1": seg(OFF_GN1, D_MODEL).reshape(1, D_MODEL),
        "sinks": seg(OFF_SINK, N_Q_HEADS).reshape(1, N_Q_HEADS),
        "conv_w": lax.dynamic_slice_in_dim(conv_g_full, me * conv_cols, conv_cols, axis=1)[None],
        "g_attn_out": seg(OFF_GATT, ATTN_WIDTH).reshape(1, ATTN_WIDTH),
        "g_conv_out": seg(OFF_GCV, CONV_WIDTH).reshape(1, CONV_WIDTH),
        "g_norm2": seg(OFF_GN2, D_MODEL).reshape(1, D_MODEL),
        "g_final": seg(OFF_GFIN, D_MODEL),
    }
    small_state = {
        "rel_bias": (rel_bias, m_rel_bias, v_rel_bias), "b_ada": (b_ada, m_b_ada, v_b_ada),
        "g_norm1": (g_norm1, m_g_norm1, v_g_norm1), "sinks": (sinks, m_sinks, v_sinks),
        "conv_w": (conv_w, m_conv_w, v_conv_w), "g_attn_out": (g_attn_out, m_g_attn_out, v_g_attn_out),
        "g_conv_out": (g_conv_out, m_g_conv_out, v_g_conv_out), "g_norm2": (g_norm2, m_g_norm2, v_g_norm2),
        "g_final": (g_final, m_g_final, v_g_final),
    }
    names = list(small_grads)
    sizes = [small_grads[k].size for k in names]
    total = sum(sizes)
    padded = -(-total // 1024) * 1024

    def pack(arrs):
        flat = jnp.concatenate([a.reshape(-1) for a in arrs] + [jnp.ones((padded - total,), F32)])
        return flat.reshape(padded // 128, 128)

    sw = pack([small_state[k][0] for k in names])
    sm = pack([small_state[k][1] for k in names])
    sv = pack([small_state[k][2] for k in names])
    sg = pack([small_grads[k] for k in names])
    sd, snm, snv = _adamw(sw, sm, sv, sg, padded // 128, "adamw_small")

    def unpack(flat2d):
        flat = flat2d.reshape(-1)
        out, off = {}, 0
        for k, n in zip(names, sizes):
            out[k] = flat[off:off + n].reshape(small_grads[k].shape)
            off += n
        return out

    sd_all = sd
    sd, snm, snv = unpack(sd), unpack(snm), unpack(snv)

    def finish(name, after, tr):
        src, land = _exchange_wait(started[name], after, "exchange_wait_" + name)
        return _sum_parts(src, land, me_arr, tr, "sum_parts_" + name)

    g_down = finish("w_down", [sd_all], 176)
    d_down, nm_down, nv_down = _adamw(w_down[0], m_w_down[0], v_w_down[0], g_down, 176, "adamw_w_down")
    g_gu = finish("w_gu", [nv_down], 352)
    d_gu, nm_gu, nv_gu = _adamw(w_gu[0].T, m_w_gu[0].T, v_w_gu[0].T, g_gu, 352, "adamw_w_gu")
    g_out = finish("w_out", [nv_gu], 128)
    d_out, nm_out, nv_out = _adamw(w_out[0], m_w_out[0], v_w_out[0], g_out, 128, "adamw_w_out")
    g_in = finish("w_in", [nv_out, nv_ada], 144)
    d_in, nm_in, nv_in = _adamw(w_in[0].T, m_w_in[0].T, v_w_in[0].T, g_in, 144, "adamw_w_in")

    big = {
        "w_ada": (g_ada[None], d_ada[None], nm_ada[None], nv_ada[None]),
        "w_in": (g_in.T[None], d_in.T[None], nm_in.T[None], nv_in.T[None]),
        "w_out": (g_out[None], d_out[None], nm_out[None], nv_out[None]),
        "w_gu": (g_gu.T[None], d_gu.T[None], nm_gu.T[None], nv_gu.T[None]),
        "w_down": (g_down[None], d_down[None], nm_down[None], nv_down[None]),
    }
    order = ["rel_bias", "w_ada", "b_ada", "g_norm1", "w_in", "sinks", "conv_w", "g_attn_out", "g_conv_out", "w_out",
             "g_norm2", "w_gu", "w_down", "g_final"]
    grads = [big[k][0] if k in big else small_grads[k] for k in order]
    deltas = [big[k][1] if k in big else sd[k] for k in order]
    new_m = [big[k][2] if k in big else snm[k] for k in order]
    new_v = [big[k][3] if k in big else snv[k] for k in order]
    return (loss, dx[None], *grads, *deltas, *new_m, *new_v)
```

```python
import functools
import math

import jax
import jax.numpy as jnp
from jax import lax
from jax.experimental import pallas as pl
from jax.experimental.pallas import tpu as pltpu

F32 = jnp.float32
BF16 = jnp.bfloat16

D_MODEL = 1024
HEAD_DIM = 64
N_Q_HEADS = 8
ATTN_WIDTH = 512
KV_WIDTH = 128
CONV_WIDTH = 512
IN_PROJ_WIDTH = 2304
D_FF = 2816
N_MOD = 6
N_BUCKETS = 32
MAX_DISTANCE = 128
BLOCK = 128
EPS = 1e-6
NEG_INF = -1e30
SCALE = HEAD_DIM ** -0.5
N_DEV = 8

ADAM_LR = 0.001
ADAM_B1 = 0.9
ADAM_B2 = 0.999
ADAM_EPS = 1e-08
ADAM_WD = 0.01
ADAM_STEP = 10

SH1, SC1, G1, SH2, SC2, G2 = range(6)

VMEM_LIMIT_LARGE = 56 * 1024 * 1024
WEIGHT_GRAD_ROWS = 2048
MESH_ID = pl.DeviceIdType.MESH

OFF_DMOD = 0
OFF_RELB = OFF_DMOD + N_MOD * D_MODEL
OFF_GN1 = OFF_RELB + N_BUCKETS * N_Q_HEADS
OFF_SINK = OFF_GN1 + D_MODEL
OFF_GATT = OFF_SINK + 128
OFF_GCV = OFF_GATT + ATTN_WIDTH
OFF_GN2 = OFF_GCV + CONV_WIDTH
OFF_GFIN = OFF_GN2 + D_MODEL
OFF_CONVW = OFF_GFIN + D_MODEL
OFF_LOSS = OFF_CONVW + 3 * CONV_WIDTH
PACKED = OFF_LOSS + 128


def _params(sem=None, vmem=None):
    return pltpu.CompilerParams(dimension_semantics=sem, vmem_limit_bytes=vmem)


def _full(shape):
    nd = len(shape)
    return pl.BlockSpec(shape, lambda *_: (0,) * nd)


def _rows(tm, width):
    return pl.BlockSpec((tm, width), lambda i, *_: (i, 0))


def _sigmoid(x):
    return 1.0 / (1.0 + jnp.exp(-x))


def _rsqrt_mean_sq(x):
    return lax.rsqrt(jnp.mean(x * x, axis=-1, keepdims=True) + EPS)


def _colsum(x):
    return jnp.sum(x, axis=0, keepdims=True)


def _dot(a, b):
    return jnp.dot(a, b, preferred_element_type=F32)


def _dot_nt(a, b):
    return lax.dot_general(a, b, (((1,), (1,)), ((), ())), preferred_element_type=F32)


def _dot_tn(a, b):
    return lax.dot_general(a, b, (((0,), (0,)), ((), ())), preferred_element_type=F32)


def _mesh_position():
    return lax.axis_index("x"), lax.axis_index("y"), lax.axis_index("c")


def _linear(p):
    return 4 * p[0] + 2 * p[1] + p[2]


def _all_gather(arrs, name, to_bf16, big):
    n = len(arrs)
    out_dtype = BF16 if to_bf16 else F32

    def body(*refs):
        in_refs, out_refs = refs[:n], refs[n:2 * n]
        rest = refs[2 * n:]
        if to_bf16:
            stage, rest = rest[:n], rest[n:]
            for a in range(n):
                stage[a][...] = in_refs[a][...].astype(BF16)
            srcs = stage
        else:
            srcs = in_refs
        send_sems, recv_sems, local_sems = rest
        x, y, c = _mesh_position()
        me, sibling = (x, y, c), (x, y, 1 - c)
        chips = [(1 - x, y), (x, 1 - y), (1 - x, 1 - y)]

        def slot(a, p):
            return out_refs[a].at[_linear(p)]

        def copy(k, a, block, to, src=None):
            return pltpu.make_async_remote_copy(
                src_ref=slot(a, block) if src is None else src,
                dst_ref=slot(a, block),
                send_sem=send_sems.at[k * n + a],
                recv_sem=recv_sems.at[k * n + a],
                device_id=to,
                device_id_type=MESH_ID,
            )

        mine = [pltpu.make_async_copy(srcs[a], slot(a, me), local_sems.at[a]) for a in range(n)]
        for cp in mine:
            cp.start()
        first = [copy(0, a, me, sibling, src=srcs[a]) for a in range(n)]
        for j, chip in enumerate(chips):
            first += [copy(1 + j, a, me, (*chip, c), src=srcs[a]) for a in range(n)]
        for cp in first:
            cp.start()
        passed = []
        for j, chip in enumerate(chips):
            for a in range(n):
                copy(1 + j, a, (*chip, c), me).wait_recv()
                fwd = copy(4 + j, a, (*chip, c), sibling)
                fwd.start()
                passed.append(fwd)
        for a in range(n):
            copy(0, a, sibling, me).wait_recv()
        for j, chip in enumerate(chips):
            for a in range(n):
                copy(4 + j, a, (*chip, 1 - c), me).wait_recv()
        for cp in first + passed:
            cp.wait_send()
        for cp in mine:
            cp.wait()

    vmem = pl.BlockSpec(memory_space=pltpu.VMEM)
    out_space = pl.BlockSpec(memory_space=pl.ANY) if big else vmem
    scratch = [pltpu.VMEM(a.shape, BF16) for a in arrs] if to_bf16 else []
    scratch += [pltpu.SemaphoreType.DMA((7 * n,)), pltpu.SemaphoreType.DMA((7 * n,)),
                pltpu.SemaphoreType.DMA((n,))]
    outs = pl.pallas_call(
        body, name=name,
        out_shape=[jax.ShapeDtypeStruct((N_DEV,) + a.shape, out_dtype) for a in arrs],
        in_specs=[vmem] * n, out_specs=[out_space] * n,
        scratch_shapes=scratch,
        compiler_params=_params(vmem=VMEM_LIMIT_LARGE if big else None),
    )(*arrs)
    return list(outs)


def _peer(k):
    x, y, c = _mesh_position()
    return (1 - x if k & 4 else x, 1 - y if k & 2 else y, 1 - c if k & 1 else c)


HBM_SPEC = pl.BlockSpec(memory_space=pltpu.HBM)
SEM_SPEC = pl.BlockSpec(memory_space=pltpu.SEMAPHORE)
DATAFLOW = pltpu.SideEffectType.DATAFLOW_SIDE_EFFECTING


def _exchange_start(src, name):
    r, c = src.shape[1:]

    def body(src_ref, land_ref, send_sems, recv_sems, src_thru, land_thru, token):
        for k in range(1, N_DEV):
            peer = _peer(k)
            pltpu.make_async_remote_copy(
                src_ref=src_ref.at[_linear(peer)], dst_ref=land_ref.at[k - 1],
                send_sem=send_sems.at[k - 1], recv_sem=recv_sems.at[k - 1],
                device_id=peer, device_id_type=MESH_ID).start()
        token[...] = jnp.zeros_like(token)

    land = lax.empty((N_DEV - 1, r, c), src.dtype)
    return pl.pallas_call(
        body, name=name,
        out_shape=(pltpu.SemaphoreType.DMA((N_DEV - 1,)), pltpu.SemaphoreType.DMA((N_DEV - 1,)),
                   pltpu.HBM(src.shape, src.dtype), pltpu.HBM(land.shape, land.dtype),
                   jax.ShapeDtypeStruct((8, 128), F32)),
        in_specs=(HBM_SPEC, HBM_SPEC),
        out_specs=(SEM_SPEC, SEM_SPEC, HBM_SPEC, HBM_SPEC, pl.BlockSpec(memory_space=pltpu.VMEM)),
        input_output_aliases={0: 2, 1: 3},
        compiler_params=pltpu.CompilerParams(has_side_effects=DATAFLOW),
    )(pltpu.with_memory_space_constraint(src, pltpu.HBM), pltpu.with_memory_space_constraint(land, pltpu.HBM))


def _exchange_wait(started, after, name):
    send_sems, recv_sems, src_thru, land_thru, _ = started

    def body(src_ref, land_ref, send_sems, recv_sems, *rest):
        for k in range(1, N_DEV):
            cp = pltpu.make_async_remote_copy(
                src_ref=src_ref.at[0], dst_ref=land_ref.at[k - 1],
                send_sem=send_sems.at[k - 1], recv_sem=recv_sems.at[k - 1],
                device_id=_peer(k), device_id_type=MESH_ID)
            cp.wait_send()
            cp.wait_recv()

    return pl.pallas_call(
        body, name=name,
        out_shape=(pltpu.HBM(src_thru.shape, src_thru.dtype), pltpu.HBM(land_thru.shape, land_thru.dtype)),
        in_specs=(HBM_SPEC, HBM_SPEC, SEM_SPEC, SEM_SPEC) + (pl.BlockSpec(memory_space=pl.ANY),) * len(after),
        out_specs=(HBM_SPEC, HBM_SPEC), input_output_aliases={0: 0, 1: 1},
        compiler_params=pltpu.CompilerParams(has_side_effects=DATAFLOW),
    )(src_thru, land_thru, send_sems, recv_sems, *after)


def _stage_blocks(arrs, after, name):
    n = len(arrs)

    def body(*refs):
        in_refs, out_refs, stage, sems = refs[:n], refs[n + 1:2 * n + 1], refs[2 * n + 1:3 * n + 1], refs[3 * n + 1]
        me = _linear(_mesh_position())
        copies = []
        for a in range(n):
            stage[a][...] = in_refs[a][...].astype(BF16)
            copies.append(pltpu.make_async_copy(stage[a], out_refs[a].at[me], sems.at[a]))
            copies[-1].start()
        for cp in copies:
            cp.wait()

    return list(pl.pallas_call(
        body, name=name,
        out_shape=[jax.ShapeDtypeStruct((N_DEV,) + a.shape, BF16) for a in arrs],
        in_specs=[pl.BlockSpec(memory_space=pltpu.VMEM)] * n + [pl.BlockSpec(memory_space=pl.ANY)],
        out_specs=[pl.BlockSpec(memory_space=pl.ANY)] * n,
        scratch_shapes=[pltpu.VMEM(a.shape, BF16) for a in arrs] + [pltpu.SemaphoreType.DMA((n,))],
        compiler_params=_params(vmem=VMEM_LIMIT_LARGE),
    )(*arrs, after))


def _same_core_peers():
    x, y, c = _mesh_position()
    return [(x, y, 1 - c), (1 - x, y, c), (x, 1 - y, c), (1 - x, 1 - y, c)]


def _gather_start(bufs, name):
    n = len(bufs)

    def body(*refs):
        buf_refs, rest = refs[:n], refs[n:]
        sems, token = rest[:2 * n], rest[-1]
        me = _linear(_mesh_position())
        for a in range(n):
            for k, peer in enumerate(_same_core_peers()):
                pltpu.make_async_remote_copy(
                    src_ref=buf_refs[a].at[me], dst_ref=buf_refs[a].at[me],
                    send_sem=sems[2 * a].at[k], recv_sem=sems[2 * a + 1].at[k],
                    device_id=peer, device_id_type=MESH_ID).start()
        token[...] = jnp.zeros_like(token)

    outs = pl.pallas_call(
        body, name=name,
        out_shape=tuple(pltpu.SemaphoreType.DMA((4,)) for _ in range(2 * n))
        + tuple(pltpu.HBM(b.shape, b.dtype) for b in bufs) + (jax.ShapeDtypeStruct((8, 128), F32),),
        in_specs=(HBM_SPEC,) * n,
        out_specs=(SEM_SPEC,) * (2 * n) + (HBM_SPEC,) * n + (pl.BlockSpec(memory_space=pltpu.VMEM),),
        input_output_aliases={a: 2 * n + a for a in range(n)},
        compiler_params=pltpu.CompilerParams(has_side_effects=DATAFLOW),
    )(*[pltpu.with_memory_space_constraint(b, pltpu.HBM) for b in bufs])
    return outs[:2 * n], outs[2 * n:3 * n], outs[3 * n]


def _gather_wait(sems, bufs, after, name):
    n = len(bufs)

    def body(*refs):
        buf_refs, sem_refs = refs[:n], refs[n:3 * n]
        x, y, c = _mesh_position()
        me = _linear((x, y, c))
        for a in range(n):
            for k, peer in enumerate(_same_core_peers()):
                cp = pltpu.make_async_remote_copy(
                    src_ref=buf_refs[a].at[me], dst_ref=buf_refs[a].at[_linear(peer)],
                    send_sem=sem_refs[2 * a].at[k], recv_sem=sem_refs[2 * a + 1].at[k],
                    device_id=peer, device_id_type=MESH_ID)
                cp.wait_send()
                cp.wait_recv()

    return list(pl.pallas_call(
        body, name=name,
        out_shape=tuple(pltpu.HBM(b.shape, b.dtype) for b in bufs),
        in_specs=(HBM_SPEC,) * n + (SEM_SPEC,) * (2 * n) + (pl.BlockSpec(memory_space=pl.ANY),) * len(after),
        out_specs=(HBM_SPEC,) * n, input_output_aliases={a: a for a in range(n)},
        compiler_params=pltpu.CompilerParams(has_side_effects=DATAFLOW),
    )(*bufs, *sems, *after))


def _gather_pass_on(bufs, name):
    n = len(bufs)

    def body(*refs):
        out_refs = refs[n:2 * n]
        send_sems, recv_sems = refs[2 * n:]
        x, y, c = _mesh_position()
        sibling = (x, y, 1 - c)
        chips = [(1 - x, y), (x, 1 - y), (1 - x, 1 - y)]
        copies = []
        for a in range(n):
            for j, chip in enumerate(chips):
                block = out_refs[a].at[_linear((*chip, c))]
                copies.append(pltpu.make_async_remote_copy(
                    src_ref=block, dst_ref=block, send_sem=send_sems.at[3 * a + j], recv_sem=recv_sems.at[3 * a + j],
                    device_id=sibling, device_id_type=MESH_ID))
                copies[-1].start()
        for a in range(n):
            for j, chip in enumerate(chips):
                copies[3 * a + j].wait_send()
                theirs = out_refs[a].at[_linear((*chip, 1 - c))]
                pltpu.make_async_remote_copy(
                    src_ref=theirs, dst_ref=theirs, send_sem=send_sems.at[3 * a + j], recv_sem=recv_sems.at[3 * a + j],
                    device_id=sibling, device_id_type=MESH_ID).wait_recv()

    hbm = pl.BlockSpec(memory_space=pl.ANY)
    return list(pl.pallas_call(
        body, name=name,
        out_shape=[jax.ShapeDtypeStruct(b.shape, b.dtype) for b in bufs],
        in_specs=[hbm] * n, out_specs=[hbm] * n, input_output_aliases={a: a for a in range(n)},
        scratch_shapes=[pltpu.SemaphoreType.DMA((3 * n,)), pltpu.SemaphoreType.DMA((3 * n,))],
    )(*bufs))


def _silu_rows(c):
    def body(c_ref, o_ref):
        v = c_ref[...]
        o_ref[...] = v * _sigmoid(v)

    return pl.pallas_call(body, name="cond_silu", out_shape=jax.ShapeDtypeStruct(c.shape, F32))(c)


def _mod_columns(cond_all, w_ada, b_cols):
    def body(c_ref, w_ref, b_ref, o_ref):
        o_ref[...] = _dot(c_ref[...], w_ref[...]) + b_ref[...]

    return pl.pallas_call(body, name="mod_columns",
                          out_shape=jax.ShapeDtypeStruct((N_DEV, w_ada.shape[1]), F32))(cond_all, w_ada, b_cols)


def _in_proj(x, mod, g_norm1, w_in, tm):
    s = x.shape[0]

    def body(x_ref, mod_ref, g_ref, w_ref, h_ref, q_ref, kv_ref, gb_ref, gc_ref, xc_ref):
        xf = x_ref[...]
        n = xf * _rsqrt_mean_sq(xf) * g_ref[...]
        h = (n * (1.0 + mod_ref[SC1:SC1 + 1, :]) + mod_ref[SH1:SH1 + 1, :]).astype(BF16)
        h_ref[...] = h
        p = _dot_nt(h, w_ref[...])
        q_ref[...] = p[:, 0:512].astype(BF16)
        kv_ref[...] = p[:, 512:768].astype(BF16)
        gb_ref[...] = p[:, 768:1280]
        gc_ref[...] = p[:, 1280:1792]
        xc_ref[...] = p[:, 1792:2304]

    return pl.pallas_call(
        body, name="in_proj", grid=(s // tm,),
        in_specs=[_rows(tm, D_MODEL), _full((8, D_MODEL)), _full((1, D_MODEL)), _full((IN_PROJ_WIDTH, D_MODEL))],
        out_specs=[_rows(tm, D_MODEL), _rows(tm, 512), _rows(tm, 256), _rows(tm, 512), _rows(tm, 512), _rows(tm, 512)],
        out_shape=[jax.ShapeDtypeStruct((s, D_MODEL), BF16), jax.ShapeDtypeStruct((s, 512), BF16),
                   jax.ShapeDtypeStruct((s, 256), BF16), jax.ShapeDtypeStruct((s, 512), F32),
                   jax.ShapeDtypeStruct((s, 512), F32), jax.ShapeDtypeStruct((s, 512), F32)],
        compiler_params=_params(("arbitrary",), VMEM_LIMIT_LARGE),
    )(x, mod, g_norm1, w_in)


def _t5_bucket(dist):
    max_exact = N_BUCKETS // 2
    is_small = dist < max_exact
    d = jnp.maximum(dist, 1).astype(F32)
    large = max_exact + (jnp.log(d / max_exact) / math.log(MAX_DISTANCE / max_exact)
                         * (N_BUCKETS - max_exact)).astype(jnp.int32)
    large = jnp.minimum(large, N_BUCKETS - 1)
    return jnp.where(is_small, dist, large)


def _bucket_table():
    qi = jnp.arange(BLOCK, dtype=jnp.int32)[:, None]
    sj = jnp.arange(2 * BLOCK, dtype=jnp.int32)[None, :]
    return _t5_bucket(jnp.maximum(qi + BLOCK - sj, 0))


def _window_mask():
    qi = lax.broadcasted_iota(jnp.int32, (BLOCK, 2 * BLOCK), 0)
    sj = lax.broadcasted_iota(jnp.int32, (BLOCK, 2 * BLOCK), 1)
    dist = qi + BLOCK - sj
    return (dist >= 0) & (dist < BLOCK)


def _bias_table(rel_bias, bucket):
    def body(rb_ref, bk_ref, o_ref):
        bk = bk_ref[...]
        inside = _window_mask()
        for h in range(N_Q_HEADS):
            acc = jnp.zeros((BLOCK, 2 * BLOCK), F32)
            for b in range(N_BUCKETS):
                acc = jnp.where(bk == b, rb_ref[b, h], acc)
            o_ref[h] = jnp.where(inside, acc, NEG_INF)

    return pl.pallas_call(
        body, name="bias_table",
        in_specs=[pl.BlockSpec(memory_space=pltpu.SMEM), pl.BlockSpec(memory_space=pltpu.VMEM)],
        out_shape=jax.ShapeDtypeStruct((N_Q_HEADS, BLOCK, 2 * BLOCK), F32),
    )(rel_bias, bucket)


def _load_kv_window(kv_ref, n):
    prev = jnp.maximum(n - 1, 0)
    kvw = jnp.concatenate([kv_ref[pl.ds(pl.multiple_of(prev * BLOCK, BLOCK), BLOCK), :],
                           kv_ref[pl.ds(pl.multiple_of(n * BLOCK, BLOCK), BLOCK), :]], axis=0)
    k, v = kvw[:, 0:128], kvw[:, 128:256]
    k_sw = pltpu.roll(k.astype(F32), 64, 1).astype(BF16)
    v_sw = pltpu.roll(v.astype(F32), 64, 1).astype(BF16)
    return (k, k_sw), (v, v_sw)


def _conv_taps(gc, xc, gc_prev, xc_prev, n):
    u = gc * xc
    before = jnp.where(n > 0, gc_prev * xc_prev, 0.0)
    row = lax.broadcasted_iota(jnp.int32, u.shape, 0)
    u1 = jnp.where(row == 0, before[7:8, :], pltpu.roll(u, 1, 0))
    u2 = jnp.where(row == 0, before[6:7, :], jnp.where(row == 1, before[7:8, :], pltpu.roll(u, 2, 0)))
    return u, u1, u2


def _mixer_fwd(q, kv, gb, gc, xc, bias, sinks, conv_w, g_attn, g_conv):
    s = q.shape[0]
    nb = s // BLOCK

    def body(sink_ref, q_ref, kv_ref, gb_ref, gc_ref, xc_ref, gcp_ref, xcp_ref, bias_ref, cw_ref, ga_ref, gcv_ref,
             attn_ref, merged_ref, lse_ref):
        n = pl.program_id(0)
        ks, vs = _load_kv_window(kv_ref, n)
        lane = lax.broadcasted_iota(jnp.int32, (BLOCK, BLOCK), 1)
        low = lane < HEAD_DIM
        col = lax.broadcasted_iota(jnp.int32, (BLOCK, 2 * BLOCK), 1)
        no_prev = (col < BLOCK) & (n == 0)
        lse_all = jnp.zeros((BLOCK, BLOCK), F32)
        pairs = []
        for p in range(4):
            qp = q_ref[:, 128 * p:128 * (p + 1)].astype(F32)
            kvh = p // 2
            res = []
            for e in range(2):
                h = 2 * p + e
                qm = jnp.where(low if e == 0 else ~low, qp, 0.0).astype(BF16)
                sw = 0 if kvh == e else 1
                sc = _dot_nt(qm, ks[sw]) * SCALE + bias_ref[h]
                sc = jnp.where(no_prev, NEG_INF, sc)
                sink = sink_ref[h]
                m = jnp.maximum(jnp.max(sc, axis=-1, keepdims=True), sink)
                pe = jnp.exp(sc - m)
                den = jnp.sum(pe, axis=-1, keepdims=True) + jnp.exp(sink - m)
                res.append(_dot(pe.astype(BF16), vs[sw]) / den)
                lse_all = lse_all + jnp.where(lane == h, m + jnp.log(den), 0.0)
            pairs.append(jnp.where(low, res[0], res[1]))
        attn = jnp.concatenate(pairs, axis=1)
        attn_ref[...] = attn
        lse_ref[...] = lse_all
        u, u1, u2 = _conv_taps(gc_ref[...], xc_ref[...], gcp_ref[...], xcp_ref[...], n)
        cw = cw_ref[...]
        cv = gb_ref[...] * (cw[0:1, :] * u2 + cw[1:2, :] * u1 + cw[2:3, :] * u)
        an = attn * _rsqrt_mean_sq(attn) * ga_ref[...]
        cn = cv * _rsqrt_mean_sq(cv) * gcv_ref[...]
        merged_ref[...] = jnp.concatenate([an, cn], axis=1).astype(BF16)

    blk = lambda w: pl.BlockSpec((BLOCK, w), lambda n: (n, 0))
    prev8 = pl.BlockSpec((8, 512), lambda n: (jnp.maximum(n * (BLOCK // 8) - 1, 0), 0))
    return pl.pallas_call(
        body, name="mixer_fwd", grid=(nb,),
        in_specs=[pl.BlockSpec(memory_space=pltpu.SMEM), blk(512), _full((s, 256)), blk(512), blk(512), blk(512),
                  prev8, prev8, _full((N_Q_HEADS, BLOCK, 2 * BLOCK)), _full((3, 512)), _full((1, 512)),
                  _full((1, 512))],
        out_specs=[blk(512), blk(1024), blk(128)],
        out_shape=[jax.ShapeDtypeStruct((s, 512), F32), jax.ShapeDtypeStruct((s, 1024), BF16),
                   jax.ShapeDtypeStruct((s, 128), F32)],
        compiler_params=_params(("arbitrary",)),
    )(sinks, q, kv, gb, gc, xc, gc, xc, bias, conv_w, g_attn, g_conv)


def _out_proj(merged, x, mod, w_out, tm):
    s = x.shape[0]

    def body(m_ref, x_ref, mod_ref, w_ref, o_ref, x1_ref):
        o = _dot(m_ref[...], w_ref[...])
        o_ref[...] = o.astype(BF16)
        x1_ref[...] = x_ref[...] + mod_ref[G1:G1 + 1, :] * o

    return pl.pallas_call(
        body, name="out_proj", grid=(s // tm,),
        in_specs=[_rows(tm, D_MODEL), _rows(tm, D_MODEL), _full((8, D_MODEL)), _full((D_MODEL, D_MODEL))],
        out_specs=[_rows(tm, D_MODEL), _rows(tm, D_MODEL)],
        out_shape=[jax.ShapeDtypeStruct((s, D_MODEL), BF16), jax.ShapeDtypeStruct((s, D_MODEL), F32)],
        compiler_params=_params(("arbitrary",)),
    )(merged, x, mod, w_out)


def _ffn_up(x1, mod, g_norm2, w_gu, tm, tn):
    s = x1.shape[0]
    nj = D_FF // tn

    def body(x_ref, mod_ref, g_ref, wg_ref, wu_ref, h_ref, gate_ref, up_ref, act_ref):
        xf = x_ref[...]
        n = xf * _rsqrt_mean_sq(xf) * g_ref[...]
        h = (n * (1.0 + mod_ref[SC2:SC2 + 1, :]) + mod_ref[SH2:SH2 + 1, :]).astype(BF16)

        @pl.when(pl.program_id(0) == 0)
        def _():
            h_ref[...] = h

        gate = _dot_nt(h, wg_ref[...])
        up = _dot_nt(h, wu_ref[...])
        gate_ref[...] = gate.astype(BF16)
        up_ref[...] = up.astype(BF16)
        act_ref[...] = (gate * _sigmoid(gate) * up).astype(BF16)

    ni = s // tm
    tile = pl.BlockSpec((tm, tn), lambda j, i: (i, j))
    h_rows = pl.BlockSpec((tm, D_MODEL), lambda j, i: (jnp.where(j == 0, i, ni - 1), 0))
    return pl.pallas_call(
        body, name="ffn_up", grid=(nj, ni),
        in_specs=[pl.BlockSpec((tm, D_MODEL), lambda j, i: (i, 0)), _full((8, D_MODEL)), _full((1, D_MODEL)),
                  pl.BlockSpec((tn, D_MODEL), lambda j, i: (j, 0)),
                  pl.BlockSpec((tn, D_MODEL), lambda j, i: (j + nj, 0))],
        out_specs=[h_rows, tile, tile, tile],
        out_shape=[jax.ShapeDtypeStruct((s, D_MODEL), BF16)] + [jax.ShapeDtypeStruct((s, D_FF), BF16)] * 3,
        compiler_params=_params(("arbitrary", "arbitrary"), VMEM_LIMIT_LARGE),
    )(x1, mod, g_norm2, w_gu, w_gu)


def _ffn_down_loss(act, x1, mod, w_down, g_final, target, tm):
    s = x1.shape[0]

    def body(a_ref, x1_ref, mod_ref, w_ref, gf_ref, t_ref, o_ref, dx2_ref, small_ref):
        @pl.when(pl.program_id(0) == 0)
        def _():
            small_ref[...] = jnp.zeros_like(small_ref)

        o = _dot(a_ref[...], w_ref[...])
        o_ref[...] = o.astype(BF16)
        x2 = x1_ref[...] + mod_ref[G2:G2 + 1, :] * o
        r = _rsqrt_mean_sq(x2)
        xn = x2 * r
        gf = gf_ref[...]
        err = xn * gf - t_ref[...]
        dy = err * (1.0 / D_MODEL)
        dxn = dy * gf
        dx2_ref[...] = r * (dxn - xn * jnp.mean(dxn * xn, axis=-1, keepdims=True))
        small_ref[0:1, :] += _colsum(dy * xn)
        small_ref[1:2, :] += _colsum(err * err)

        @pl.when(pl.program_id(0) == pl.num_programs(0) - 1)
        def _():
            total = jnp.sum(small_ref[1:2, :], axis=-1, keepdims=True) * (0.5 / D_MODEL)
            small_ref[2:3, :] = jnp.broadcast_to(total, (1, D_MODEL))

    return pl.pallas_call(
        body, name="ffn_down_loss", grid=(s // tm,),
        in_specs=[_rows(tm, D_FF), _rows(tm, D_MODEL), _full((8, D_MODEL)), _full((D_FF, D_MODEL)),
                  _full((1, D_MODEL)), _rows(tm, D_MODEL)],
        out_specs=[_rows(tm, D_MODEL), _rows(tm, D_MODEL), _full((8, D_MODEL))],
        out_shape=[jax.ShapeDtypeStruct((s, D_MODEL), BF16), jax.ShapeDtypeStruct((s, D_MODEL), F32),
                   jax.ShapeDtypeStruct((8, D_MODEL), F32)],
        compiler_params=_params(("arbitrary",), VMEM_LIMIT_LARGE),
    )(act, x1, mod, w_down, g_final, target)


def _ffn_down_bwd(dx2, o2, gate, up, mod, w_down, tm):
    s = dx2.shape[0]

    def body(dx_ref, o_ref, gate_ref, up_ref, mod_ref, w_ref, do_ref, dgu_ref, small_ref):
        @pl.when(pl.program_id(0) == 0)
        def _():
            small_ref[...] = jnp.zeros_like(small_ref)

        dx = dx_ref[...]
        small_ref[0:1, :] += _colsum(dx * o_ref[...].astype(F32))
        do = (dx * mod_ref[G2:G2 + 1, :]).astype(BF16)
        do_ref[...] = do
        dact = _dot_nt(do, w_ref[...])
        gate = gate_ref[...].astype(F32)
        sg = _sigmoid(gate)
        dgu_ref[:, 0:D_FF] = (dact * up_ref[...].astype(F32) * (sg * (1.0 + gate * (1.0 - sg)))).astype(BF16)
        dgu_ref[:, D_FF:2 * D_FF] = (dact * (gate * sg)).astype(BF16)

    return pl.pallas_call(
        body, name="ffn_down_bwd", grid=(s // tm,),
        in_specs=[_rows(tm, D_MODEL), _rows(tm, D_MODEL), _rows(tm, D_FF), _rows(tm, D_FF), _full((8, D_MODEL)),
                  _full((D_FF, D_MODEL))],
        out_specs=[_rows(tm, D_MODEL), _rows(tm, 2 * D_FF), _full((8, D_MODEL))],
        out_shape=[jax.ShapeDtypeStruct((s, D_MODEL), BF16), jax.ShapeDtypeStruct((s, 2 * D_FF), BF16),
                   jax.ShapeDtypeStruct((8, D_MODEL), F32)],
        compiler_params=_params(("arbitrary",), VMEM_LIMIT_LARGE),
    )(dx2, o2, gate, up, mod, w_down)


def _norm_mod_bwd(dh, xf, g, scale_row, small_ref):
    r = _rsqrt_mean_sq(xf)
    xn = xf * r
    small_ref[0:1, :] += _colsum(dh)
    small_ref[1:2, :] += _colsum(dh * (xn * g))
    dn = dh * (1.0 + scale_row)
    small_ref[2:3, :] += _colsum(dn * xn)
    dxn = dn * g
    return r * (dxn - xn * jnp.mean(dxn * xn, axis=-1, keepdims=True))


def _ffn_up_bwd(dgu, x1, dx2, mod, g_norm2, w_gu, tm):
    s = x1.shape[0]

    def body(dgu_ref, x_ref, dx2_ref, mod_ref, g_ref, w_ref, dx1_ref, small_ref):
        @pl.when(pl.program_id(0) == 0)
        def _():
            small_ref[...] = jnp.zeros_like(small_ref)

        dh = _dot(dgu_ref[...], w_ref[...])
        dx1_ref[...] = dx2_ref[...] + _norm_mod_bwd(dh, x_ref[...], g_ref[...], mod_ref[SC2:SC2 + 1, :], small_ref)

    return pl.pallas_call(
        body, name="ffn_up_bwd", grid=(s // tm,),
        in_specs=[_rows(tm, 2 * D_FF), _rows(tm, D_MODEL), _rows(tm, D_MODEL),
                  _full((8, D_MODEL)), _full((1, D_MODEL)),
                  pl.BlockSpec((2 * D_FF, D_MODEL), lambda i: (0, 0), pipeline_mode=pl.Buffered(1))],
        out_specs=[_rows(tm, D_MODEL), _full((8, D_MODEL))],
        out_shape=[jax.ShapeDtypeStruct((s, D_MODEL), F32), jax.ShapeDtypeStruct((8, D_MODEL), F32)],
        compiler_params=_params(("arbitrary",), VMEM_LIMIT_LARGE),
    )(dgu, x1, dx2, mod, g_norm2, w_gu)


def _out_proj_bwd(dx1, o1, mod, w_out, tm):
    s = dx1.shape[0]

    def body(dx_ref, o_ref, mod_ref, w_ref, do_ref, dm_ref, small_ref):
        @pl.when(pl.program_id(0) == 0)
        def _():
            small_ref[...] = jnp.zeros_like(small_ref)

        dx = dx_ref[...]
        small_ref[0:1, :] += _colsum(dx * o_ref[...].astype(F32))
        do = (dx * mod_ref[G1:G1 + 1, :]).astype(BF16)
        do_ref[...] = do
        dm_ref[...] = _dot_nt(do, w_ref[...])

    return pl.pallas_call(
        body, name="out_proj_bwd", grid=(s // tm,),
        in_specs=[_rows(tm, D_MODEL), _rows(tm, D_MODEL), _full((8, D_MODEL)), _full((D_MODEL, D_MODEL))],
        out_specs=[_rows(tm, D_MODEL), _rows(tm, D_MODEL), _full((8, D_MODEL))],
        out_shape=[jax.ShapeDtypeStruct((s, D_MODEL), BF16), jax.ShapeDtypeStruct((s, D_MODEL), F32),
                   jax.ShapeDtypeStruct((8, D_MODEL), F32)],
        compiler_params=_params(("arbitrary",)),
    )(dx1, o1, mod, w_out)


def _group_norm_bwd(dm, a, g):
    r = _rsqrt_mean_sq(a)
    an = a * r
    dan = dm * g
    return r * (dan - an * jnp.mean(dan * an, axis=-1, keepdims=True)), _colsum(dm * an)


def _mixer_bwd(q, kv, gb, gc, xc, bias, sinks, conv_w, g_attn, g_conv, attn, lse, dmerged):
    s = q.shape[0]
    nb = s // BLOCK

    def body(sink_ref, q_ref, kv_ref, gb_ref, gc_ref, xc_ref, gcp_ref, xcp_ref, bias_ref, cw_ref, ga_ref, gcv_ref,
             attn_ref, lse_ref, dm_ref,
             dq_ref, dkv_ref, dgb_ref, dgc_ref, dxc_ref, dbias_ref, dsink_ref, small_ref, carry_ref):
        step = pl.program_id(0)
        n = nb - 1 - step

        @pl.when(step == 0)
        def _():
            dkv_ref[...] = jnp.zeros_like(dkv_ref)
            dbias_ref[...] = jnp.zeros_like(dbias_ref)
            dsink_ref[...] = jnp.zeros_like(dsink_ref)
            small_ref[...] = jnp.zeros_like(small_ref)
            carry_ref[...] = jnp.zeros_like(carry_ref)

        dm = dm_ref[...]
        gbv, gcv_, xcv = gb_ref[...], gc_ref[...], xc_ref[...]
        u, u1, u2 = _conv_taps(gcv_, xcv, gcp_ref[...], xcp_ref[...], n)
        cw = cw_ref[...]
        yv = cw[0:1, :] * u2 + cw[1:2, :] * u1 + cw[2:3, :] * u
        dcv, dg_conv = _group_norm_bwd(dm[:, 512:1024], gbv * yv, gcv_ref[...])
        small_ref[1:2, :] += dg_conv
        dgb_ref[...] = (dcv * yv).astype(BF16)
        dy = dcv * gbv
        nxt = carry_ref[...]
        row = lax.broadcasted_iota(jnp.int32, dy.shape, 0)
        d1 = jnp.where(row == BLOCK - 1, nxt[0:1, :], pltpu.roll(dy, BLOCK - 1, 0))
        d2 = jnp.where(row == BLOCK - 2, nxt[0:1, :],
                       jnp.where(row == BLOCK - 1, nxt[1:2, :], pltpu.roll(dy, BLOCK - 2, 0)))
        du = cw[2:3, :] * dy + cw[1:2, :] * d1 + cw[0:1, :] * d2
        dgc_ref[...] = (du * xcv).astype(BF16)
        dxc_ref[...] = (du * gcv_).astype(BF16)
        small_ref[2:3, :] += _colsum(dy * u2)
        small_ref[3:4, :] += _colsum(dy * u1)
        small_ref[4:5, :] += _colsum(dy * u)
        carry_ref[...] = dy[0:8, :]

        attn_v = attn_ref[...]
        dout, dg_attn = _group_norm_bwd(dm[:, 0:512], attn_v, ga_ref[...])
        small_ref[0:1, :] += dg_attn
        ks, vs = _load_kv_window(kv_ref, n)
        lane = lax.broadcasted_iota(jnp.int32, (BLOCK, BLOCK), 1)
        low = lane < HEAD_DIM
        col = lax.broadcasted_iota(jnp.int32, (BLOCK, 2 * BLOCK), 1)
        no_prev = (col < BLOCK) & (n == 0)
        lse_all = lse_ref[...]
        dk = jnp.zeros((2 * BLOCK, BLOCK), F32)
        dv = jnp.zeros((2 * BLOCK, BLOCK), F32)
        dsink = jnp.zeros((BLOCK, BLOCK), F32)
        dq_pairs = []
        for p in range(4):
            qp = q_ref[:, 128 * p:128 * (p + 1)].astype(F32)
            do_p = dout[:, 128 * p:128 * (p + 1)]
            prod = do_p * attn_v[:, 128 * p:128 * (p + 1)]
            kvh = p // 2
            res = []
            for e in range(2):
                h = 2 * p + e
                half = low if e == 0 else ~low
                qm = jnp.where(half, qp, 0.0).astype(BF16)
                dom = jnp.where(half, do_p, 0.0).astype(BF16)
                delta = jnp.sum(jnp.where(half, prod, 0.0), axis=-1, keepdims=True)
                lse_h = jnp.sum(jnp.where(lane == h, lse_all, 0.0), axis=-1, keepdims=True)
                sw = 0 if kvh == e else 1
                sc = _dot_nt(qm, ks[sw]) * SCALE + bias_ref[h]
                sc = jnp.where(no_prev, NEG_INF, sc)
                pr = jnp.exp(sc - lse_h)
                dp = _dot_nt(dom, vs[sw])
                ds = pr * (dp - delta)
                dbias_ref[h] += ds
                dsink = dsink + jnp.where(lane == h, -jnp.exp(sink_ref[h] - lse_h) * delta, 0.0)
                dsb = ds.astype(BF16)
                res.append(_dot(dsb, ks[sw]) * SCALE)
                dk_h = _dot_tn(dsb, qm) * SCALE
                dv_h = _dot_tn(pr.astype(BF16), dom)
                if sw:
                    dk_h = pltpu.roll(dk_h, 64, 1)
                    dv_h = pltpu.roll(dv_h, 64, 1)
                dk = dk + dk_h
                dv = dv + dv_h
            dq_pairs.append(jnp.where(low, res[0], res[1]))
        dq_ref[...] = jnp.concatenate(dq_pairs, axis=1).astype(BF16)
        dsink_ref[...] += dsink
        dkv_win = jnp.concatenate([dk, dv], axis=1)
        prev = jnp.maximum(n - 1, 0)
        dkv_ref[pl.ds(pl.multiple_of(prev * BLOCK, BLOCK), BLOCK), :] += dkv_win[0:BLOCK, :]
        dkv_ref[pl.ds(pl.multiple_of(n * BLOCK, BLOCK), BLOCK), :] += dkv_win[BLOCK:2 * BLOCK, :]

        @pl.when(step == nb - 1)
        def _():
            small_ref[5:6, :] = jnp.concatenate([_colsum(dsink_ref[...]), jnp.zeros((1, 512 - BLOCK), F32)], axis=1)

    blk = lambda w: pl.BlockSpec((BLOCK, w), lambda t: (nb - 1 - t, 0))
    prev8 = pl.BlockSpec((8, 512), lambda t: (jnp.maximum((nb - 1 - t) * (BLOCK // 8) - 1, 0), 0))
    bf = lambda w: jax.ShapeDtypeStruct((s, w), BF16)
    return pl.pallas_call(
        body, name="mixer_bwd", grid=(nb,),
        in_specs=[pl.BlockSpec(memory_space=pltpu.SMEM), blk(512), _full((s, 256)), blk(512), blk(512), blk(512),
                  prev8, prev8, _full((N_Q_HEADS, BLOCK, 2 * BLOCK)), _full((3, 512)), _full((1, 512)),
                  _full((1, 512)), blk(512), blk(128), blk(1024)],
        out_specs=[blk(512), _full((s, 256)), blk(512), blk(512), blk(512), _full((N_Q_HEADS, BLOCK, 2 * BLOCK)),
                   _full((BLOCK, BLOCK)), _full((8, 512))],
        out_shape=[bf(512), jax.ShapeDtypeStruct((s, 256), F32), bf(512), bf(512), bf(512),
                   jax.ShapeDtypeStruct((N_Q_HEADS, BLOCK, 2 * BLOCK), F32), jax.ShapeDtypeStruct((BLOCK, BLOCK), F32),
                   jax.ShapeDtypeStruct((8, 512), F32)],
        scratch_shapes=[pltpu.VMEM((8, 512), F32)],
        compiler_params=_params(("arbitrary",), VMEM_LIMIT_LARGE),
    )(sinks, q, kv, gb, gc, xc, gc, xc, bias, conv_w, g_attn, g_conv, attn, lse, dmerged)


def _in_proj_bwd(dq, dkv, dgb, dgc, dxc, x, dx1, mod, g_norm1, w_in, tm):
    s = x.shape[0]

    def body(dq_ref, dkv_ref, dgb_ref, dgc_ref, dxc_ref, x_ref, dx1_ref, mod_ref, g_ref, w_ref,
             dproj_ref, dx_ref, small_ref):
        @pl.when(pl.program_id(0) == 0)
        def _():
            small_ref[...] = jnp.zeros_like(small_ref)

        dproj = jnp.concatenate([dq_ref[...], dkv_ref[...].astype(BF16), dgb_ref[...], dgc_ref[...], dxc_ref[...]],
                                axis=1)
        dproj_ref[...] = dproj
        dh = _dot(dproj, w_ref[...])
        dx_ref[...] = dx1_ref[...] + _norm_mod_bwd(dh, x_ref[...], g_ref[...], mod_ref[SC1:SC1 + 1, :], small_ref)

    return pl.pallas_call(
        body, name="in_proj_bwd", grid=(s // tm,),
        in_specs=[_rows(tm, 512), _rows(tm, 256), _rows(tm, 512), _rows(tm, 512), _rows(tm, 512),
                  _rows(tm, D_MODEL), _rows(tm, D_MODEL), _full((8, D_MODEL)), _full((1, D_MODEL)),
                  _full((IN_PROJ_WIDTH, D_MODEL))],
        out_specs=[_rows(tm, IN_PROJ_WIDTH), _rows(tm, D_MODEL), _full((8, D_MODEL))],
        out_shape=[jax.ShapeDtypeStruct((s, IN_PROJ_WIDTH), BF16), jax.ShapeDtypeStruct((s, D_MODEL), F32),
                   jax.ShapeDtypeStruct((8, D_MODEL), F32)],
        compiler_params=_params(("arbitrary",), VMEM_LIMIT_LARGE),
    )(dq, dkv, dgb, dgc, dxc, x, dx1, mod, g_norm1, w_in)


def _weight_grad(a, b, tk, ts, name, after=None):
    s, k = a.shape
    n = b.shape[1]
    nt = s // ts
    extra = [] if after is None else [after]

    def body(a_ref, b_ref, *rest):
        o_ref, acc_ref = rest[-2:]
        t = pl.program_id(1)
        part = _dot_tn(a_ref[...], b_ref[...])

        @pl.when(t == 0)
        def _():
            acc_ref[...] = part

        @pl.when(t > 0)
        def _():
            acc_ref[...] += part

        @pl.when(t == nt - 1)
        def _():
            o_ref[...] = acc_ref[...].astype(BF16)

    return pl.pallas_call(
        body, name=name, grid=(k // tk, nt),
        in_specs=[pl.BlockSpec((ts, tk), lambda i, t: (t, i)), pl.BlockSpec((ts, n), lambda i, t: (t, 0))]
        + [pl.BlockSpec(memory_space=pl.ANY)] * len(extra),
        out_specs=pl.BlockSpec((tk, n), lambda i, t: (i, 0)),
        out_shape=jax.ShapeDtypeStruct((k, n), BF16),
        scratch_shapes=[pltpu.VMEM((tk, n), F32)],
        compiler_params=_params(("arbitrary", "arbitrary"), VMEM_LIMIT_LARGE),
    )(a, b, *extra)


def _rel_bias_grad(dbias, bucket):
    def body(db_ref, bk_ref, o_ref, rows_ref):
        bk = bk_ref[...]
        for b in range(N_BUCKETS):
            sel = (bk == b).astype(F32)
            for h in range(N_Q_HEADS):
                rows_ref[8 * b + h:8 * b + h + 1, :] = _colsum(db_ref[h] * sel)
        o_ref[...] = jnp.sum(rows_ref[...], axis=-1, keepdims=True)

    return pl.pallas_call(
        body, name="rel_bias_grad",
        out_shape=jax.ShapeDtypeStruct((N_BUCKETS * N_Q_HEADS, 1), F32),
        scratch_shapes=[pltpu.VMEM((N_BUCKETS * N_Q_HEADS, 2 * BLOCK), F32)],
    )(dbias, bucket)


def _sum_slots(parts, after):
    def body(p_ref, after_ref, o_ref):
        acc = p_ref[0]
        for k in range(1, N_DEV):
            acc = acc + p_ref[k]
        o_ref[...] = acc

    return pl.pallas_call(body, name="sum_small_grads",
                          in_specs=[pl.BlockSpec(memory_space=pltpu.VMEM), pl.BlockSpec(memory_space=pl.ANY)],
                          out_shape=jax.ShapeDtypeStruct(parts.shape[1:], F32))(parts, after)


def _w_ada_grad(cond_t, dmod_cols):
    def body(c_ref, d_ref, o_ref):
        o_ref[...] = _dot(c_ref[...], d_ref[...])

    return pl.pallas_call(body, name="w_ada_grad",
                          out_shape=jax.ShapeDtypeStruct((cond_t.shape[0], dmod_cols.shape[1]), F32))(cond_t, dmod_cols)


def _adam_math(w, g, m, v):
    m = ADAM_B1 * m + (1.0 - ADAM_B1) * g
    v = ADAM_B2 * v + (1.0 - ADAM_B2) * (g * g)
    m_hat = m / (1.0 - ADAM_B1 ** ADAM_STEP)
    v_hat = v / (1.0 - ADAM_B2 ** ADAM_STEP)
    delta = -ADAM_LR * (m_hat / (jnp.sqrt(v_hat) + ADAM_EPS) + ADAM_WD * w)
    return delta, m, v


def _sum_parts(local, land, me, tr, name):
    r, c = local.shape[1:]

    def body(me_ref, own_ref, land_ref, o_ref):
        acc = own_ref[0].astype(F32)
        for k in range(N_DEV - 1):
            acc = acc + land_ref[k].astype(F32)
        o_ref[...] = acc

    return pl.pallas_call(
        body, name=name,
        grid_spec=pltpu.PrefetchScalarGridSpec(
            num_scalar_prefetch=1, grid=(r // tr,),
            in_specs=[pl.BlockSpec((1, tr, c), lambda i, me_ref: (me_ref[0], i, 0)),
                      pl.BlockSpec((N_DEV - 1, tr, c), lambda i, me_ref: (0, i, 0))],
            out_specs=pl.BlockSpec((tr, c), lambda i, me_ref: (i, 0))),
        out_shape=jax.ShapeDtypeStruct((r, c), F32),
        compiler_params=_params(("arbitrary",)),
    )(me, local, land)


def _adamw(w, m, v, g, tr, name):
    r, c = w.shape

    def body(w_ref, m_ref, v_ref, g_ref, d_ref, mo_ref, vo_ref):
        d_ref[...], mo_ref[...], vo_ref[...] = _adam_math(w_ref[...], g_ref[...], m_ref[...], v_ref[...])

    tile = pl.BlockSpec((tr, c), lambda i: (i, 0))
    return pl.pallas_call(
        body, name=name, grid=(r // tr,),
        in_specs=[tile] * 4, out_specs=[tile] * 3,
        out_shape=[jax.ShapeDtypeStruct((r, c), F32)] * 3,
        compiler_params=_params(("arbitrary",)),
    )(w, m, v, g)


def _behind(a, token):
    return a + token[0:a.shape[0], 0:1]


def _local_step(x, target, mod, w_in_t, weights_out_gu, weights_down, rel_bias, g_norm1, sinks, conv_w, g_attn,
                g_conv, g_norm2, g_final, exchange):
    s = x.shape[0]
    tm = min(512, s)
    tm_small = min(256, s)
    bucket = _bucket_table()
    bias = _bias_table(rel_bias, bucket)

    h, q, kv, gb, gc, xc = _in_proj(x, mod, g_norm1, w_in_t, tm)
    attn, merged, lse = _mixer_fwd(q, kv, gb, gc, xc, bias, sinks, conv_w, g_attn, g_conv)
    w_out, w_gu_t = weights_out_gu(merged)
    o1, x1 = _out_proj(merged, x, mod, w_out, tm)
    h2, gate, up, act = _ffn_up(x1, mod, g_norm2, w_gu_t, tm, D_FF // 2)
    w_down = weights_down(act)
    o2, dx2, fin = _ffn_down_loss(act, x1, mod, w_down, g_final, target, tm)

    do2, dgu, sm_g2 = _ffn_down_bwd(dx2, o2, gate, up, mod, w_down, tm_small)
    ts = min(WEIGHT_GRAD_ROWS, s)
    mod = _behind(mod, exchange("w_down", _weight_grad(act, do2, D_FF // 2, ts, "w_down_grad")))
    dx1, sm_2 = _ffn_up_bwd(dgu, x1, dx2, mod, g_norm2, w_gu_t, tm)
    mod = _behind(mod, exchange("w_gu", _weight_grad(dgu, h2, D_FF // 2, ts, "w_gu_grad")))
    do1, dmerged, sm_g1 = _out_proj_bwd(dx1, o1, mod, w_out, tm)
    g_attn_b = _behind(g_attn, exchange("w_out", _weight_grad(merged, do1, D_MODEL, ts, "w_out_grad")))
    dq, dkv, dgb, dgc, dxc, dbias, dsink, sm_mix = _mixer_bwd(
        q, kv, gb, gc, xc, bias, sinks, conv_w, g_attn_b, g_conv, attn, lse, dmerged)
    dproj, dx, sm_1 = _in_proj_bwd(dq, dkv, dgb, dgc, dxc, x, dx1, mod, g_norm1, w_in_t, tm)
    d_rel = _rel_bias_grad(dbias, bucket)

    packed = jnp.concatenate([
        sm_1[0], sm_1[1], sm_g1[0], sm_2[0], sm_2[1], sm_g2[0],
        d_rel[:, 0],
        sm_1[2],
        sm_mix[5, 0:128],
        sm_mix[0], sm_mix[1],
        sm_2[2],
        fin[0],
        sm_mix[2], sm_mix[3], sm_mix[4],
        fin[2, 0:128],
    ])[None, :]
    return dx, dproj, h, packed


def kernel(x, c, rel_bias, w_ada, b_ada, g_norm1, w_in, sinks, conv_w, g_attn_out, g_conv_out, w_out, g_norm2, w_gu, w_down, g_final, loss_target, m_rel_bias, m_w_ada, m_b_ada, m_g_norm1, m_w_in, m_sinks, m_conv_w, m_g_attn_out, m_g_conv_out, m_w_out, m_g_norm2, m_w_gu, m_w_down, m_g_final, v_rel_bias, v_w_ada, v_b_ada, v_g_norm1, v_w_in, v_sinks, v_conv_w, v_g_attn_out, v_g_conv_out, v_w_out, v_g_norm2, v_w_gu, v_w_down, v_g_final):
    me = _linear(_mesh_position())
    me_arr = jnp.reshape(me, (1,)).astype(jnp.int32)
    ada_cols = w_ada.shape[2]
    tm = min(512, x.shape[1])

    cond = _silu_rows(c)
    cond_all, conv_w_all = _all_gather([cond, conv_w[0]], "gather_cond", to_bf16=False, big=False)
    cond_all = cond_all[:, 0, :]
    conv_cols = conv_w.shape[2]
    conv_w_full = conv_w_all.transpose(1, 0, 2).reshape(3, CONV_WIDTH)
    b_cols = lax.dynamic_slice_in_dim(b_ada, me * ada_cols, ada_cols, axis=1)
    mod_cols = _mod_columns(cond_all, w_ada[0], b_cols)
    mod_all = _all_gather([mod_cols], "gather_mod", to_bf16=False, big=False)[0]
    mod = lax.dynamic_index_in_dim(mod_all, me, axis=1, keepdims=False).reshape(N_MOD, D_MODEL)
    mod = jnp.concatenate([mod, jnp.zeros((2, D_MODEL), F32)], axis=0)

    w_in_t = _all_gather([w_in[0].T], "gather_w_in", to_bf16=True, big=True)[0].reshape(IN_PROJ_WIDTH, D_MODEL)
    gather_sems, staged, gather_token = _gather_start(
        _stage_blocks([w_out[0], w_gu[0].T, w_down[0]], w_in_t, "stage_weights"), "gather_start_weights")
    mod = _behind(mod, gather_token)

    def weights_out_gu(after):
        got = _gather_pass_on(_gather_wait(gather_sems[0:4], staged[0:2], [after], "gather_wait_out_gu"),
                              "gather_pass_on_out_gu")
        return got[0].reshape(D_MODEL, D_MODEL), got[1].reshape(2 * D_FF, D_MODEL)

    def weights_down(after):
        got = _gather_pass_on(_gather_wait(gather_sems[4:6], staged[2:3], [after], "gather_wait_down"),
                              "gather_pass_on_down")
        return got[0].reshape(D_FF, D_MODEL)

    started = {}

    def exchange(name, dw):
        st = _exchange_start(dw.reshape(N_DEV, dw.shape[0] // N_DEV, dw.shape[1]), "exchange_start_" + name)
        started[name] = st
        return st[4]

    dx, dproj, h, packed = _local_step(
        x[0], loss_target[0], mod, w_in_t, weights_out_gu, weights_down, rel_bias, g_norm1, sinks[0], conv_w_full,
        g_attn_out, g_conv_out, g_norm2, g_final[None, :], exchange)

    packed_all = _all_gather([packed], "gather_small_grads", to_bf16=False, big=False)[0]
    tok_in = exchange("w_in", _weight_grad(dproj, h, IN_PROJ_WIDTH // 2, min(WEIGHT_GRAD_ROWS, x.shape[1]),
                                           "w_in_grad", after=packed_all))
    small = _sum_slots(packed_all, tok_in)[0]
    dmod_all = packed_all[:, 0, OFF_DMOD:OFF_DMOD + N_MOD * D_MODEL]
    dmod_cols = lax.dynamic_slice_in_dim(dmod_all, me * ada_cols, ada_cols, axis=1)
    cond_t = jnp.zeros((D_MODEL, 128), F32).at[:, 0:N_DEV].set(cond_all.T)
    dmod_pad = jnp.zeros((128, ada_cols), F32).at[0:N_DEV, :].set(dmod_cols)
    g_ada = _w_ada_grad(cond_t, dmod_pad)
    d_ada, nm_ada, nv_ada = _adamw(w_ada[0], m_w_ada[0], v_w_ada[0], g_ada, 256, "adamw_w_ada")

    loss = small[OFF_LOSS]
    seg = lambda off, n: small[off:off + n]
    conv_g_full = seg(OFF_CONVW, 3 * CONV_WIDTH).reshape(3, CONV_WIDTH)
    small_grads = {
        "rel_bias": seg(OFF_RELB, 256).reshape(N_BUCKETS, N_Q_HEADS),
        "b_ada": seg(OFF_DMOD, N_MOD * D_MODEL).reshape(1, N_MOD * D_MODEL),
        "g_norm1": seg(OFF_GN1, D_MODEL).reshape(1, D_MODEL),
        "sinks": seg(OFF_SINK, N_Q_HEADS).reshape(1, N_Q_HEADS),
        "conv_w": lax.dynamic_slice_in_dim(conv_g_full, me * conv_cols, conv_cols, axis=1)[None],
        "g_attn_out": seg(OFF_GATT, ATTN_WIDTH).reshape(1, ATTN_WIDTH),
        "g_conv_out": seg(OFF_GCV, CONV_WIDTH).reshape(1, CONV_WIDTH),
        "g_norm2": seg(OFF_GN2, D_MODEL).reshape(1, D_MODEL),
        "g_final": seg(OFF_GFIN, D_MODEL),
    }
    small_state = {
        "rel_bias": (rel_bias, m_rel_bias, v_rel_bias), "b_ada": (b_ada, m_b_ada, v_b_ada),
        "g_norm1": (g_norm1, m_g_norm1, v_g_norm1), "sinks": (sinks, m_sinks, v_sinks),
        "conv_w": (conv_w, m_conv_w, v_conv_w), "g_attn_out": (g_attn_out, m_g_attn_out, v_g_attn_out),
        "g_conv_out": (g_conv_out, m_g_conv_out, v_g_conv_out), "g_norm2": (g_norm2, m_g_norm2, v_g_norm2),
        "g_final": (g_final, m_g_final, v_g_final),
    }
    names = list(small_grads)
    sizes = [small_grads[k].size for k in names]
    total = sum(sizes)
    padded = -(-total // 1024) * 1024

    def pack(arrs):
        flat = jnp.concatenate([a.reshape(-1) for a in arrs] + [jnp.ones((padded - total,), F32)])
        return flat.reshape(padded // 128, 128)

    sw = pack([small_state[k][0] for k in names])
    sm = pack([small_state[k][1] for k in names])
    sv = pack([small_state[k][2] for k in names])
    sg = pack([small_grads[k] for k in names])
    sd, snm, snv = _adamw(sw, sm, sv, sg, padded // 128, "adamw_small")

    def unpack(flat2d):
        flat = flat2d.reshape(-1)
        out, off = {}, 0
        for k, n in zip(names, sizes):
            out[k] = flat[off:off + n].reshape(small_grads[k].shape)
            off += n
        return out

    sd_all = sd
    sd, snm, snv = unpack(sd), unpack(snm), unpack(snv)

    def finish(name, after, tr):
        src, land = _exchange_wait(started[name], after, "exchange_wait_" + name)
        return _sum_parts(src, land, me_arr, tr, "sum_parts_" + name)

    g_down = finish("w_down", [sd_all], 176)
    d_down, nm_down, nv_down = _adamw(w_down[0], m_w_down[0], v_w_down[0], g_down, 176, "adamw_w_down")
    g_gu = finish("w_gu", [nv_down], 352)
    d_gu, nm_gu, nv_gu = _adamw(w_gu[0].T, m_w_gu[0].T, v_w_gu[0].T, g_gu, 352, "adamw_w_gu")
    g_out = finish("w_out", [nv_gu], 128)
    d_out, nm_out, nv_out = _adamw(w_out[0], m_w_out[0], v_w_out[0], g_out, 128, "adamw_w_out")
    g_in = finish("w_in", [nv_out, nv_ada], 144)
    d_in, nm_in, nv_in = _adamw(w_in[0].T, m_w_in[0].T, v_w_in[0].T, g_in, 144, "adamw_w_in")

    big = {
        "w_ada": (g_ada[None], d_ada[None], nm_ada[None], nv_ada[None]),
        "w_in": (g_in.T[None], d_in.T[None], nm_in.T[None], nv_in.T[None]),
        "w_out": (g_out[None], d_out[None], nm_out[None], nv_out[None]),
        "w_gu": (g_gu.T[None], d_gu.T[None], nm_gu.T[None], nv_gu.T[None]),
        "w_down": (g_down[None], d_down[None], nm_down[None], nv_down[None]),
    }
    order = ["rel_bias", "w_ada", "b_ada", "g_norm1", "w_in", "sinks", "conv_w", "g_attn_out", "g_conv_out", "w_out",
             "g_norm2", "w_gu", "w_down", "g_final"]
    grads = [big[k][0] if k in big else small_grads[k] for k in order]
    deltas = [big[k][1] if k in big else sd[k] for k in order]
    new_m = [big[k][2] if k in big else snm[k] for k in order]
    new_v = [big[k][3] if k in big else snv[k] for k in order]
    return (loss, dx[None], *grads, *deltas, *new_m, *new_v)
```

```python
import functools
import math

import jax
import jax.numpy as jnp
from jax import lax
from jax.experimental import pallas as pl
from jax.experimental.pallas import tpu as pltpu

F32 = jnp.float32
BF16 = jnp.bfloat16

D_MODEL = 1024
HEAD_DIM = 64
N_Q_HEADS = 8
ATTN_WIDTH = 512
KV_WIDTH = 128
CONV_WIDTH = 512
IN_PROJ_WIDTH = 2304
D_FF = 2816
N_MOD = 6
N_BUCKETS = 32
MAX_DISTANCE = 128
BLOCK = 128
EPS = 1e-6
NEG_INF = -1e30
SCALE = HEAD_DIM ** -0.5
N_DEV = 8

ADAM_LR = 0.001
ADAM_B1 = 0.9
ADAM_B2 = 0.999
ADAM_EPS = 1e-08
ADAM_WD = 0.01
ADAM_STEP = 10

SH1, SC1, G1, SH2, SC2, G2 = range(6)

VMEM_LIMIT_LARGE = 56 * 1024 * 1024
WEIGHT_GRAD_ROWS = 2048
FFN_CHUNKS = 2
MESH_ID = pl.DeviceIdType.MESH

OFF_DMOD = 0
OFF_RELB = OFF_DMOD + N_MOD * D_MODEL
OFF_GN1 = OFF_RELB + N_BUCKETS * N_Q_HEADS
OFF_SINK = OFF_GN1 + D_MODEL
OFF_GATT = OFF_SINK + 128
OFF_GCV = OFF_GATT + ATTN_WIDTH
OFF_GN2 = OFF_GCV + CONV_WIDTH
OFF_GFIN = OFF_GN2 + D_MODEL
OFF_CONVW = OFF_GFIN + D_MODEL
OFF_LOSS = OFF_CONVW + 3 * CONV_WIDTH
PACKED = OFF_LOSS + 128


def _params(sem=None, vmem=None):
    return pltpu.CompilerParams(dimension_semantics=sem, vmem_limit_bytes=vmem)


def _full(shape):
    nd = len(shape)
    return pl.BlockSpec(shape, lambda *_: (0,) * nd)


def _rows(tm, width):
    return pl.BlockSpec((tm, width), lambda i, *_: (i, 0))


def _sigmoid(x):
    return 1.0 / (1.0 + jnp.exp(-x))


def _rsqrt_mean_sq(x):
    return lax.rsqrt(jnp.mean(x * x, axis=-1, keepdims=True) + EPS)


def _colsum(x):
    return jnp.sum(x, axis=0, keepdims=True)


def _dot(a, b):
    return jnp.dot(a, b, preferred_element_type=F32)


def _dot_nt(a, b):
    return lax.dot_general(a, b, (((1,), (1,)), ((), ())), preferred_element_type=F32)


def _dot_tn(a, b):
    return lax.dot_general(a, b, (((0,), (0,)), ((), ())), preferred_element_type=F32)


def _mesh_position():
    return lax.axis_index("x"), lax.axis_index("y"), lax.axis_index("c")


def _linear(p):
    return 4 * p[0] + 2 * p[1] + p[2]


def _all_gather(arrs, name, to_bf16, big):
    n = len(arrs)
    out_dtype = BF16 if to_bf16 else F32

    def body(*refs):
        in_refs, out_refs = refs[:n], refs[n:2 * n]
        rest = refs[2 * n:]
        if to_bf16:
            stage, rest = rest[:n], rest[n:]
            for a in range(n):
                stage[a][...] = in_refs[a][...].astype(BF16)
            srcs = stage
        else:
            srcs = in_refs
        send_sems, recv_sems, local_sems = rest
        x, y, c = _mesh_position()
        me, sibling = (x, y, c), (x, y, 1 - c)
        chips = [(1 - x, y), (x, 1 - y), (1 - x, 1 - y)]

        def slot(a, p):
            return out_refs[a].at[_linear(p)]

        def copy(k, a, block, to, src=None):
            return pltpu.make_async_remote_copy(
                src_ref=slot(a, block) if src is None else src,
                dst_ref=slot(a, block),
                send_sem=send_sems.at[k * n + a],
                recv_sem=recv_sems.at[k * n + a],
                device_id=to,
                device_id_type=MESH_ID,
            )

        mine = [pltpu.make_async_copy(srcs[a], slot(a, me), local_sems.at[a]) for a in range(n)]
        for cp in mine:
            cp.start()
        first = [copy(0, a, me, sibling, src=srcs[a]) for a in range(n)]
        for j, chip in enumerate(chips):
            first += [copy(1 + j, a, me, (*chip, c), src=srcs[a]) for a in range(n)]
        for cp in first:
            cp.start()
        passed = []
        for j, chip in enumerate(chips):
            for a in range(n):
                copy(1 + j, a, (*chip, c), me).wait_recv()
                fwd = copy(4 + j, a, (*chip, c), sibling)
                fwd.start()
                passed.append(fwd)
        for a in range(n):
            copy(0, a, sibling, me).wait_recv()
        for j, chip in enumerate(chips):
            for a in range(n):
                copy(4 + j, a, (*chip, 1 - c), me).wait_recv()
        for cp in first + passed:
            cp.wait_send()
        for cp in mine:
            cp.wait()

    vmem = pl.BlockSpec(memory_space=pltpu.VMEM)
    out_space = pl.BlockSpec(memory_space=pl.ANY) if big else vmem
    scratch = [pltpu.VMEM(a.shape, BF16) for a in arrs] if to_bf16 else []
    scratch += [pltpu.SemaphoreType.DMA((7 * n,)), pltpu.SemaphoreType.DMA((7 * n,)),
                pltpu.SemaphoreType.DMA((n,))]
    outs = pl.pallas_call(
        body, name=name,
        out_shape=[jax.ShapeDtypeStruct((N_DEV,) + a.shape, out_dtype) for a in arrs],
        in_specs=[vmem] * n, out_specs=[out_space] * n,
        scratch_shapes=scratch,
        compiler_params=_params(vmem=VMEM_LIMIT_LARGE if big else None),
    )(*arrs)
    return list(outs)


def _peer(k):
    x, y, c = _mesh_position()
    return (1 - x if k & 4 else x, 1 - y if k & 2 else y, 1 - c if k & 1 else c)


HBM_SPEC = pl.BlockSpec(memory_space=pltpu.HBM)
SEM_SPEC = pl.BlockSpec(memory_space=pltpu.SEMAPHORE)
DATAFLOW = pltpu.SideEffectType.DATAFLOW_SIDE_EFFECTING


def _exchange_start(src, name):
    r, c = src.shape[1:]

    def body(src_ref, land_ref, send_sems, recv_sems, src_thru, land_thru, token):
        for k in range(1, N_DEV):
            peer = _peer(k)
            pltpu.make_async_remote_copy(
                src_ref=src_ref.at[_linear(peer)], dst_ref=land_ref.at[k - 1],
                send_sem=send_sems.at[k - 1], recv_sem=recv_sems.at[k - 1],
                device_id=peer, device_id_type=MESH_ID).start()
        token[...] = jnp.zeros_like(token)

    land = lax.empty((N_DEV - 1, r, c), src.dtype)
    return pl.pallas_call(
        body, name=name,
        out_shape=(pltpu.SemaphoreType.DMA((N_DEV - 1,)), pltpu.SemaphoreType.DMA((N_DEV - 1,)),
                   pltpu.HBM(src.shape, src.dtype), pltpu.HBM(land.shape, land.dtype),
                   jax.ShapeDtypeStruct((8, 128), F32)),
        in_specs=(HBM_SPEC, HBM_SPEC),
        out_specs=(SEM_SPEC, SEM_SPEC, HBM_SPEC, HBM_SPEC, pl.BlockSpec(memory_space=pltpu.VMEM)),
        input_output_aliases={0: 2, 1: 3},
        compiler_params=pltpu.CompilerParams(has_side_effects=DATAFLOW),
    )(pltpu.with_memory_space_constraint(src, pltpu.HBM), pltpu.with_memory_space_constraint(land, pltpu.HBM))


def _exchange_wait(started, after, name):
    send_sems, recv_sems, src_thru, land_thru, _ = started

    def body(src_ref, land_ref, send_sems, recv_sems, *rest):
        for k in range(1, N_DEV):
            cp = pltpu.make_async_remote_copy(
                src_ref=src_ref.at[0], dst_ref=land_ref.at[k - 1],
                send_sem=send_sems.at[k - 1], recv_sem=recv_sems.at[k - 1],
                device_id=_peer(k), device_id_type=MESH_ID)
            cp.wait_send()
            cp.wait_recv()

    return pl.pallas_call(
        body, name=name,
        out_shape=(pltpu.HBM(src_thru.shape, src_thru.dtype), pltpu.HBM(land_thru.shape, land_thru.dtype)),
        in_specs=(HBM_SPEC, HBM_SPEC, SEM_SPEC, SEM_SPEC) + (pl.BlockSpec(memory_space=pl.ANY),) * len(after),
        out_specs=(HBM_SPEC, HBM_SPEC), input_output_aliases={0: 0, 1: 1},
        compiler_params=pltpu.CompilerParams(has_side_effects=DATAFLOW),
    )(src_thru, land_thru, send_sems, recv_sems, *after)


def _stage_blocks(arrs, after, name):
    n = len(arrs)

    def body(*refs):
        in_refs, out_refs, stage, sems = refs[:n], refs[n + 1:2 * n + 1], refs[2 * n + 1:3 * n + 1], refs[3 * n + 1]
        me = _linear(_mesh_position())
        copies = []
        for a in range(n):
            stage[a][...] = in_refs[a][...].astype(BF16)
            copies.append(pltpu.make_async_copy(stage[a], out_refs[a].at[me], sems.at[a]))
            copies[-1].start()
        for cp in copies:
            cp.wait()

    return list(pl.pallas_call(
        body, name=name,
        out_shape=[jax.ShapeDtypeStruct((N_DEV,) + a.shape, BF16) for a in arrs],
        in_specs=[pl.BlockSpec(memory_space=pltpu.VMEM)] * n + [pl.BlockSpec(memory_space=pl.ANY)],
        out_specs=[pl.BlockSpec(memory_space=pl.ANY)] * n,
        scratch_shapes=[pltpu.VMEM(a.shape, BF16) for a in arrs] + [pltpu.SemaphoreType.DMA((n,))],
        compiler_params=_params(vmem=VMEM_LIMIT_LARGE),
    )(*arrs, after))


def _same_core_peers():
    x, y, c = _mesh_position()
    return [(x, y, 1 - c), (1 - x, y, c), (x, 1 - y, c), (1 - x, 1 - y, c)]


def _gather_start(bufs, name):
    n = len(bufs)

    def body(*refs):
        buf_refs, rest = refs[:n], refs[n:]
        sems, token = rest[:2 * n], rest[-1]
        me = _linear(_mesh_position())
        for a in range(n):
            for k, peer in enumerate(_same_core_peers()):
                pltpu.make_async_remote_copy(
                    src_ref=buf_refs[a].at[me], dst_ref=buf_refs[a].at[me],
                    send_sem=sems[2 * a].at[k], recv_sem=sems[2 * a + 1].at[k],
                    device_id=peer, device_id_type=MESH_ID).start()
        token[...] = jnp.zeros_like(token)

    outs = pl.pallas_call(
        body, name=name,
        out_shape=tuple(pltpu.SemaphoreType.DMA((4,)) for _ in range(2 * n))
        + tuple(pltpu.HBM(b.shape, b.dtype) for b in bufs) + (jax.ShapeDtypeStruct((8, 128), F32),),
        in_specs=(HBM_SPEC,) * n,
        out_specs=(SEM_SPEC,) * (2 * n) + (HBM_SPEC,) * n + (pl.BlockSpec(memory_space=pltpu.VMEM),),
        input_output_aliases={a: 2 * n + a for a in range(n)},
        compiler_params=pltpu.CompilerParams(has_side_effects=DATAFLOW),
    )(*[pltpu.with_memory_space_constraint(b, pltpu.HBM) for b in bufs])
    return outs[:2 * n], outs[2 * n:3 * n], outs[3 * n]


def _gather_wait(sems, bufs, after, name):
    n = len(bufs)

    def body(*refs):
        buf_refs, sem_refs = refs[:n], refs[n:3 * n]
        x, y, c = _mesh_position()
        me = _linear((x, y, c))
        for a in range(n):
            for k, peer in enumerate(_same_core_peers()):
                cp = pltpu.make_async_remote_copy(
                    src_ref=buf_refs[a].at[me], dst_ref=buf_refs[a].at[_linear(peer)],
                    send_sem=sem_refs[2 * a].at[k], recv_sem=sem_refs[2 * a + 1].at[k],
                    device_id=peer, device_id_type=MESH_ID)
                cp.wait_send()
                cp.wait_recv()

    return list(pl.pallas_call(
        body, name=name,
        out_shape=tuple(pltpu.HBM(b.shape, b.dtype) for b in bufs),
        in_specs=(HBM_SPEC,) * n + (SEM_SPEC,) * (2 * n) + (pl.BlockSpec(memory_space=pl.ANY),) * len(after),
        out_specs=(HBM_SPEC,) * n, input_output_aliases={a: a for a in range(n)},
        compiler_params=pltpu.CompilerParams(has_side_effects=DATAFLOW),
    )(*bufs, *sems, *after))


def _gather_pass_on(bufs, name):
    n = len(bufs)

    def body(*refs):
        out_refs = refs[n:2 * n]
        send_sems, recv_sems = refs[2 * n:]
        x, y, c = _mesh_position()
        sibling = (x, y, 1 - c)
        chips = [(1 - x, y), (x, 1 - y), (1 - x, 1 - y)]
        copies = []
        for a in range(n):
            for j, chip in enumerate(chips):
                block = out_refs[a].at[_linear((*chip, c))]
                copies.append(pltpu.make_async_remote_copy(
                    src_ref=block, dst_ref=block, send_sem=send_sems.at[3 * a + j], recv_sem=recv_sems.at[3 * a + j],
                    device_id=sibling, device_id_type=MESH_ID))
                copies[-1].start()
        for a in range(n):
            for j, chip in enumerate(chips):
                copies[3 * a + j].wait_send()
                theirs = out_refs[a].at[_linear((*chip, 1 - c))]
                pltpu.make_async_remote_copy(
                    src_ref=theirs, dst_ref=theirs, send_sem=send_sems.at[3 * a + j], recv_sem=recv_sems.at[3 * a + j],
                    device_id=sibling, device_id_type=MESH_ID).wait_recv()

    hbm = pl.BlockSpec(memory_space=pl.ANY)
    return list(pl.pallas_call(
        body, name=name,
        out_shape=[jax.ShapeDtypeStruct(b.shape, b.dtype) for b in bufs],
        in_specs=[hbm] * n, out_specs=[hbm] * n, input_output_aliases={a: a for a in range(n)},
        scratch_shapes=[pltpu.SemaphoreType.DMA((3 * n,)), pltpu.SemaphoreType.DMA((3 * n,))],
    )(*bufs))


def _silu_rows(c):
    def body(c_ref, o_ref):
        v = c_ref[...]
        o_ref[...] = v * _sigmoid(v)

    return pl.pallas_call(body, name="cond_silu", out_shape=jax.ShapeDtypeStruct(c.shape, F32))(c)


def _mod_columns(cond_all, w_ada, b_cols):
    def body(c_ref, w_ref, b_ref, o_ref):
        o_ref[...] = _dot(c_ref[...], w_ref[...]) + b_ref[...]

    return pl.pallas_call(body, name="mod_columns",
                          out_shape=jax.ShapeDtypeStruct((N_DEV, w_ada.shape[1]), F32))(cond_all, w_ada, b_cols)


def _in_proj(x, mod, g_norm1, w_in, tm):
    s = x.shape[0]

    def body(x_ref, mod_ref, g_ref, w_ref, h_ref, q_ref, kv_ref, gb_ref, gc_ref, xc_ref):
        xf = x_ref[...]
        n = xf * _rsqrt_mean_sq(xf) * g_ref[...]
        h = (n * (1.0 + mod_ref[SC1:SC1 + 1, :]) + mod_ref[SH1:SH1 + 1, :]).astype(BF16)
        h_ref[...] = h
        p = _dot_nt(h, w_ref[...])
        q_ref[...] = p[:, 0:512].astype(BF16)
        kv_ref[...] = p[:, 512:768].astype(BF16)
        gb_ref[...] = p[:, 768:1280]
        gc_ref[...] = p[:, 1280:1792]
        xc_ref[...] = p[:, 1792:2304]

    return pl.pallas_call(
        body, name="in_proj", grid=(s // tm,),
        in_specs=[_rows(tm, D_MODEL), _full((8, D_MODEL)), _full((1, D_MODEL)), _full((IN_PROJ_WIDTH, D_MODEL))],
        out_specs=[_rows(tm, D_MODEL), _rows(tm, 512), _rows(tm, 256), _rows(tm, 512), _rows(tm, 512), _rows(tm, 512)],
        out_shape=[jax.ShapeDtypeStruct((s, D_MODEL), BF16), jax.ShapeDtypeStruct((s, 512), BF16),
                   jax.ShapeDtypeStruct((s, 256), BF16), jax.ShapeDtypeStruct((s, 512), F32),
                   jax.ShapeDtypeStruct((s, 512), F32), jax.ShapeDtypeStruct((s, 512), F32)],
        compiler_params=_params(("arbitrary",), VMEM_LIMIT_LARGE),
    )(x, mod, g_norm1, w_in)


def _t5_bucket(dist):
    max_exact = N_BUCKETS // 2
    is_small = dist < max_exact
    d = jnp.maximum(dist, 1).astype(F32)
    large = max_exact + (jnp.log(d / max_exact) / math.log(MAX_DISTANCE / max_exact)
                         * (N_BUCKETS - max_exact)).astype(jnp.int32)
    large = jnp.minimum(large, N_BUCKETS - 1)
    return jnp.where(is_small, dist, large)


def _bucket_table():
    qi = jnp.arange(BLOCK, dtype=jnp.int32)[:, None]
    sj = jnp.arange(2 * BLOCK, dtype=jnp.int32)[None, :]
    return _t5_bucket(jnp.maximum(qi + BLOCK - sj, 0))


def _window_mask():
    qi = lax.broadcasted_iota(jnp.int32, (BLOCK, 2 * BLOCK), 0)
    sj = lax.broadcasted_iota(jnp.int32, (BLOCK, 2 * BLOCK), 1)
    dist = qi + BLOCK - sj
    return (dist >= 0) & (dist < BLOCK)


def _bias_table(rel_bias, bucket):
    def body(rb_ref, bk_ref, o_ref):
        bk = bk_ref[...]
        inside = _window_mask()
        for h in range(N_Q_HEADS):
            acc = jnp.zeros((BLOCK, 2 * BLOCK), F32)
            for b in range(N_BUCKETS):
                acc = jnp.where(bk == b, rb_ref[b, h], acc)
            o_ref[h] = jnp.where(inside, acc, NEG_INF)

    return pl.pallas_call(
        body, name="bias_table",
        in_specs=[pl.BlockSpec(memory_space=pltpu.SMEM), pl.BlockSpec(memory_space=pltpu.VMEM)],
        out_shape=jax.ShapeDtypeStruct((N_Q_HEADS, BLOCK, 2 * BLOCK), F32),
    )(rel_bias, bucket)


def _load_kv_window(kv_ref, n):
    prev = jnp.maximum(n - 1, 0)
    kvw = jnp.concatenate([kv_ref[pl.ds(pl.multiple_of(prev * BLOCK, BLOCK), BLOCK), :],
                           kv_ref[pl.ds(pl.multiple_of(n * BLOCK, BLOCK), BLOCK), :]], axis=0)
    k, v = kvw[:, 0:128], kvw[:, 128:256]
    k_sw = pltpu.roll(k.astype(F32), 64, 1).astype(BF16)
    v_sw = pltpu.roll(v.astype(F32), 64, 1).astype(BF16)
    return (k, k_sw), (v, v_sw)


def _conv_taps(gc, xc, gc_prev, xc_prev, n):
    u = gc * xc
    before = jnp.where(n > 0, gc_prev * xc_prev, 0.0)
    row = lax.broadcasted_iota(jnp.int32, u.shape, 0)
    u1 = jnp.where(row == 0, before[7:8, :], pltpu.roll(u, 1, 0))
    u2 = jnp.where(row == 0, before[6:7, :], jnp.where(row == 1, before[7:8, :], pltpu.roll(u, 2, 0)))
    return u, u1, u2


def _mixer_fwd(q, kv, gb, gc, xc, bias, sinks, conv_w, g_attn, g_conv):
    s = q.shape[0]
    nb = s // BLOCK

    def body(sink_ref, q_ref, kv_ref, gb_ref, gc_ref, xc_ref, gcp_ref, xcp_ref, bias_ref, cw_ref, ga_ref, gcv_ref,
             attn_ref, merged_ref, lse_ref):
        n = pl.program_id(0)
        ks, vs = _load_kv_window(kv_ref, n)
        lane = lax.broadcasted_iota(jnp.int32, (BLOCK, BLOCK), 1)
        low = lane < HEAD_DIM
        col = lax.broadcasted_iota(jnp.int32, (BLOCK, 2 * BLOCK), 1)
        no_prev = (col < BLOCK) & (n == 0)
        lse_all = jnp.zeros((BLOCK, BLOCK), F32)
        pairs = []
        for p in range(4):
            qp = q_ref[:, 128 * p:128 * (p + 1)].astype(F32)
            kvh = p // 2
            res = []
            for e in range(2):
                h = 2 * p + e
                qm = jnp.where(low if e == 0 else ~low, qp, 0.0).astype(BF16)
                sw = 0 if kvh == e else 1
                sc = _dot_nt(qm, ks[sw]) * SCALE + bias_ref[h]
                sc = jnp.where(no_prev, NEG_INF, sc)
                sink = sink_ref[h]
                m = jnp.maximum(jnp.max(sc, axis=-1, keepdims=True), sink)
                pe = jnp.exp(sc - m)
                den = jnp.sum(pe, axis=-1, keepdims=True) + jnp.exp(sink - m)
                res.append(_dot(pe.astype(BF16), vs[sw]) / den)
                lse_all = lse_all + jnp.where(lane == h, m + jnp.log(den), 0.0)
            pairs.append(jnp.where(low, res[0], res[1]))
        attn = jnp.concatenate(pairs, axis=1)
        attn_ref[...] = attn
        lse_ref[...] = lse_all
        u, u1, u2 = _conv_taps(gc_ref[...], xc_ref[...], gcp_ref[...], xcp_ref[...], n)
        cw = cw_ref[...]
        cv = gb_ref[...] * (cw[0:1, :] * u2 + cw[1:2, :] * u1 + cw[2:3, :] * u)
        an = attn * _rsqrt_mean_sq(attn) * ga_ref[...]
        cn = cv * _rsqrt_mean_sq(cv) * gcv_ref[...]
        merged_ref[...] = jnp.concatenate([an, cn], axis=1).astype(BF16)

    blk = lambda w: pl.BlockSpec((BLOCK, w), lambda n: (n, 0))
    prev8 = pl.BlockSpec((8, 512), lambda n: (jnp.maximum(n * (BLOCK // 8) - 1, 0), 0))
    return pl.pallas_call(
        body, name="mixer_fwd", grid=(nb,),
        in_specs=[pl.BlockSpec(memory_space=pltpu.SMEM), blk(512), _full((s, 256)), blk(512), blk(512), blk(512),
                  prev8, prev8, _full((N_Q_HEADS, BLOCK, 2 * BLOCK)), _full((3, 512)), _full((1, 512)),
                  _full((1, 512))],
        out_specs=[blk(512), blk(1024), blk(128)],
        out_shape=[jax.ShapeDtypeStruct((s, 512), F32), jax.ShapeDtypeStruct((s, 1024), BF16),
                   jax.ShapeDtypeStruct((s, 128), F32)],
        compiler_params=_params(("arbitrary",)),
    )(sinks, q, kv, gb, gc, xc, gc, xc, bias, conv_w, g_attn, g_conv)


def _out_proj(merged, x, mod, w_out, tm):
    s = x.shape[0]

    def body(m_ref, x_ref, mod_ref, w_ref, o_ref, x1_ref):
        o = _dot(m_ref[...], w_ref[...])
        o_ref[...] = o.astype(BF16)
        x1_ref[...] = x_ref[...] + mod_ref[G1:G1 + 1, :] * o

    return pl.pallas_call(
        body, name="out_proj", grid=(s // tm,),
        in_specs=[_rows(tm, D_MODEL), _rows(tm, D_MODEL), _full((8, D_MODEL)), _full((D_MODEL, D_MODEL))],
        out_specs=[_rows(tm, D_MODEL), _rows(tm, D_MODEL)],
        out_shape=[jax.ShapeDtypeStruct((s, D_MODEL), BF16), jax.ShapeDtypeStruct((s, D_MODEL), F32)],
        compiler_params=_params(("arbitrary",)),
    )(merged, x, mod, w_out)


def _resident(shape):
    nd = len(shape)
    return pl.BlockSpec(shape, lambda *_: (0,) * nd, pipeline_mode=pl.Buffered(1))


def _ffn_fwd(x1, mod, g_norm2, w_gu, w_down, g_final, target, tm):
    s = x1.shape[0]
    chunk = D_FF // FFN_CHUNKS

    def body(x_ref, mod_ref, g_ref, wgu_ref, wd_ref, gf_ref, t_ref,
             h_ref, gate_ref, up_ref, act_ref, o_ref, dx2_ref, small_ref):
        @pl.when(pl.program_id(0) == 0)
        def _():
            small_ref[...] = jnp.zeros_like(small_ref)

        xf = x_ref[...]
        n = xf * _rsqrt_mean_sq(xf) * g_ref[...]
        h = (n * (1.0 + mod_ref[SC2:SC2 + 1, :]) + mod_ref[SH2:SH2 + 1, :]).astype(BF16)
        h_ref[...] = h
        o = None
        for j in range(FFN_CHUNKS):
            lo = j * chunk
            gate = _dot_nt(h, wgu_ref[lo:lo + chunk, :])
            up = _dot_nt(h, wgu_ref[D_FF + lo:D_FF + lo + chunk, :])
            gate_ref[:, lo:lo + chunk] = gate.astype(BF16)
            up_ref[:, lo:lo + chunk] = up.astype(BF16)
            act = (gate * _sigmoid(gate) * up).astype(BF16)
            act_ref[:, lo:lo + chunk] = act
            part = _dot(act, wd_ref[lo:lo + chunk, :])
            o = part if o is None else o + part
        o_ref[...] = o.astype(BF16)
        x2 = xf + mod_ref[G2:G2 + 1, :] * o
        r = _rsqrt_mean_sq(x2)
        xn = x2 * r
        gf = gf_ref[...]
        err = xn * gf - t_ref[...]
        dy = err * (1.0 / D_MODEL)
        dxn = dy * gf
        dx2_ref[...] = r * (dxn - xn * jnp.mean(dxn * xn, axis=-1, keepdims=True))
        small_ref[0:1, :] += _colsum(dy * xn)
        small_ref[1:2, :] += _colsum(err * err)

        @pl.when(pl.program_id(0) == pl.num_programs(0) - 1)
        def _():
            total = jnp.sum(small_ref[1:2, :], axis=-1, keepdims=True) * (0.5 / D_MODEL)
            small_ref[2:3, :] = jnp.broadcast_to(total, (1, D_MODEL))

    wide = jax.ShapeDtypeStruct((s, D_FF), BF16)
    return pl.pallas_call(
        body, name="ffn_fwd", grid=(s // tm,),
        in_specs=[_rows(tm, D_MODEL), _full((8, D_MODEL)), _full((1, D_MODEL)), _resident((2 * D_FF, D_MODEL)),
                  _resident((D_FF, D_MODEL)), _full((1, D_MODEL)), _rows(tm, D_MODEL)],
        out_specs=[_rows(tm, D_MODEL), _rows(tm, D_FF), _rows(tm, D_FF), _rows(tm, D_FF), _rows(tm, D_MODEL),
                   _rows(tm, D_MODEL), _full((8, D_MODEL))],
        out_shape=[jax.ShapeDtypeStruct((s, D_MODEL), BF16), wide, wide, wide,
                   jax.ShapeDtypeStruct((s, D_MODEL), BF16), jax.ShapeDtypeStruct((s, D_MODEL), F32),
                   jax.ShapeDtypeStruct((8, D_MODEL), F32)],
        compiler_params=_params(("arbitrary",), VMEM_LIMIT_LARGE),
    )(x1, mod, g_norm2, w_gu, w_down, g_final, target)


def _ffn_bwd(dx2, o2, gate, up, x1, mod, g_norm2, w_down, w_gu, tm):
    s = x1.shape[0]
    chunk = D_FF // FFN_CHUNKS

    def body(dx_ref, o_ref, gate_ref, up_ref, x_ref, mod_ref, g_ref, wd_ref, wgu_ref,
             do_ref, dgu_ref, dx1_ref, small_ref):
        @pl.when(pl.program_id(0) == 0)
        def _():
            small_ref[...] = jnp.zeros_like(small_ref)

        dx = dx_ref[...]
        small_ref[3:4, :] += _colsum(dx * o_ref[...].astype(F32))
        do = (dx * mod_ref[G2:G2 + 1, :]).astype(BF16)
        do_ref[...] = do
        dh = None
        for j in range(FFN_CHUNKS):
            lo = j * chunk
            dact = _dot_nt(do, wd_ref[lo:lo + chunk, :])
            gate = gate_ref[:, lo:lo + chunk].astype(F32)
            sg = _sigmoid(gate)
            dgate = (dact * up_ref[:, lo:lo + chunk].astype(F32) * (sg * (1.0 + gate * (1.0 - sg)))).astype(BF16)
            dup = (dact * (gate * sg)).astype(BF16)
            dgu_ref[:, lo:lo + chunk] = dgate
            dgu_ref[:, D_FF + lo:D_FF + lo + chunk] = dup
            part = _dot(dgate, wgu_ref[lo:lo + chunk, :]) + _dot(dup, wgu_ref[D_FF + lo:D_FF + lo + chunk, :])
            dh = part if dh is None else dh + part
        dx1_ref[...] = dx + _norm_mod_bwd(dh, x_ref[...], g_ref[...], mod_ref[SC2:SC2 + 1, :], small_ref)

    return pl.pallas_call(
        body, name="ffn_bwd", grid=(s // tm,),
        in_specs=[_rows(tm, D_MODEL), _rows(tm, D_MODEL), _rows(tm, D_FF), _rows(tm, D_FF), _rows(tm, D_MODEL),
                  _full((8, D_MODEL)), _full((1, D_MODEL)), _resident((D_FF, D_MODEL)),
                  _resident((2 * D_FF, D_MODEL))],
        out_specs=[_rows(tm, D_MODEL), _rows(tm, 2 * D_FF), _rows(tm, D_MODEL), _full((8, D_MODEL))],
        out_shape=[jax.ShapeDtypeStruct((s, D_MODEL), BF16), jax.ShapeDtypeStruct((s, 2 * D_FF), BF16),
                   jax.ShapeDtypeStruct((s, D_MODEL), F32), jax.ShapeDtypeStruct((8, D_MODEL), F32)],
        compiler_params=_params(("arbitrary",), VMEM_LIMIT_LARGE),
    )(dx2, o2, gate, up, x1, mod, g_norm2, w_down, w_gu)


def _norm_mod_bwd(dh, xf, g, scale_row, small_ref):
    r = _rsqrt_mean_sq(xf)
    xn = xf * r
    small_ref[0:1, :] += _colsum(dh)
    small_ref[1:2, :] += _colsum(dh * (xn * g))
    dn = dh * (1.0 + scale_row)
    small_ref[2:3, :] += _colsum(dn * xn)
    dxn = dn * g
    return r * (dxn - xn * jnp.mean(dxn * xn, axis=-1, keepdims=True))


def _out_proj_bwd(dx1, o1, mod, w_out, tm):
    s = dx1.shape[0]

    def body(dx_ref, o_ref, mod_ref, w_ref, do_ref, dm_ref, small_ref):
        @pl.when(pl.program_id(0) == 0)
        def _():
            small_ref[...] = jnp.zeros_like(small_ref)

        dx = dx_ref[...]
        small_ref[0:1, :] += _colsum(dx * o_ref[...].astype(F32))
        do = (dx * mod_ref[G1:G1 + 1, :]).astype(BF16)
        do_ref[...] = do
        dm_ref[...] = _dot_nt(do, w_ref[...])

    return pl.pallas_call(
        body, name="out_proj_bwd", grid=(s // tm,),
        in_specs=[_rows(tm, D_MODEL), _rows(tm, D_MODEL), _full((8, D_MODEL)), _full((D_MODEL, D_MODEL))],
        out_specs=[_rows(tm, D_MODEL), _rows(tm, D_MODEL), _full((8, D_MODEL))],
        out_shape=[jax.ShapeDtypeStruct((s, D_MODEL), BF16), jax.ShapeDtypeStruct((s, D_MODEL), F32),
                   jax.ShapeDtypeStruct((8, D_MODEL), F32)],
        compiler_params=_params(("arbitrary",)),
    )(dx1, o1, mod, w_out)


def _group_norm_bwd(dm, a, g):
    r = _rsqrt_mean_sq(a)
    an = a * r
    dan = dm * g
    return r * (dan - an * jnp.mean(dan * an, axis=-1, keepdims=True)), _colsum(dm * an)


def _mixer_bwd(q, kv, gb, gc, xc, bias, sinks, conv_w, g_attn, g_conv, attn, lse, dmerged):
    s = q.shape[0]
    nb = s // BLOCK

    def body(sink_ref, q_ref, kv_ref, gb_ref, gc_ref, xc_ref, gcp_ref, xcp_ref, bias_ref, cw_ref, ga_ref, gcv_ref,
             attn_ref, lse_ref, dm_ref,
             dq_ref, dkv_ref, dgb_ref, dgc_ref, dxc_ref, dbias_ref, dsink_ref, small_ref, carry_ref):
        step = pl.program_id(0)
        n = nb - 1 - step

        @pl.when(step == 0)
        def _():
            dkv_ref[...] = jnp.zeros_like(dkv_ref)
            dbias_ref[...] = jnp.zeros_like(dbias_ref)
            dsink_ref[...] = jnp.zeros_like(dsink_ref)
            small_ref[...] = jnp.zeros_like(small_ref)
            carry_ref[...] = jnp.zeros_like(carry_ref)

        dm = dm_ref[...]
        gbv, gcv_, xcv = gb_ref[...], gc_ref[...], xc_ref[...]
        u, u1, u2 = _conv_taps(gcv_, xcv, gcp_ref[...], xcp_ref[...], n)
        cw = cw_ref[...]
        yv = cw[0:1, :] * u2 + cw[1:2, :] * u1 + cw[2:3, :] * u
        dcv, dg_conv = _group_norm_bwd(dm[:, 512:1024], gbv * yv, gcv_ref[...])
        small_ref[1:2, :] += dg_conv
        dgb_ref[...] = (dcv * yv).astype(BF16)
        dy = dcv * gbv
        nxt = carry_ref[...]
        row = lax.broadcasted_iota(jnp.int32, dy.shape, 0)
        d1 = jnp.where(row == BLOCK - 1, nxt[0:1, :], pltpu.roll(dy, BLOCK - 1, 0))
        d2 = jnp.where(row == BLOCK - 2, nxt[0:1, :],
                       jnp.where(row == BLOCK - 1, nxt[1:2, :], pltpu.roll(dy, BLOCK - 2, 0)))
        du = cw[2:3, :] * dy + cw[1:2, :] * d1 + cw[0:1, :] * d2
        dgc_ref[...] = (du * xcv).astype(BF16)
        dxc_ref[...] = (du * gcv_).astype(BF16)
        small_ref[2:3, :] += _colsum(dy * u2)
        small_ref[3:4, :] += _colsum(dy * u1)
        small_ref[4:5, :] += _colsum(dy * u)
        carry_ref[...] = dy[0:8, :]

        attn_v = attn_ref[...]
        dout, dg_attn = _group_norm_bwd(dm[:, 0:512], attn_v, ga_ref[...])
        small_ref[0:1, :] += dg_attn
        ks, vs = _load_kv_window(kv_ref, n)
        lane = lax.broadcasted_iota(jnp.int32, (BLOCK, BLOCK), 1)
        low = lane < HEAD_DIM
        col = lax.broadcasted_iota(jnp.int32, (BLOCK, 2 * BLOCK), 1)
        no_prev = (col < BLOCK) & (n == 0)
        lse_all = lse_ref[...]
        dk = jnp.zeros((2 * BLOCK, BLOCK), F32)
        dv = jnp.zeros((2 * BLOCK, BLOCK), F32)
        dsink = jnp.zeros((BLOCK, BLOCK), F32)
        dq_pairs = []
        for p in range(4):
            qp = q_ref[:, 128 * p:128 * (p + 1)].astype(F32)
            do_p = dout[:, 128 * p:128 * (p + 1)]
            prod = do_p * attn_v[:, 128 * p:128 * (p + 1)]
            kvh = p // 2
            res = []
            for e in range(2):
                h = 2 * p + e
                half = low if e == 0 else ~low
                qm = jnp.where(half, qp, 0.0).astype(BF16)
                dom = jnp.where(half, do_p, 0.0).astype(BF16)
                delta = jnp.sum(jnp.where(half, prod, 0.0), axis=-1, keepdims=True)
                lse_h = jnp.sum(jnp.where(lane == h, lse_all, 0.0), axis=-1, keepdims=True)
                sw = 0 if kvh == e else 1
                sc = _dot_nt(qm, ks[sw]) * SCALE + bias_ref[h]
                sc = jnp.where(no_prev, NEG_INF, sc)
                pr = jnp.exp(sc - lse_h)
                dp = _dot_nt(dom, vs[sw])
                ds = pr * (dp - delta)
                dbias_ref[h] += ds
                dsink = dsink + jnp.where(lane == h, -jnp.exp(sink_ref[h] - lse_h) * delta, 0.0)
                dsb = ds.astype(BF16)
                res.append(_dot(dsb, ks[sw]) * SCALE)
                dk_h = _dot_tn(dsb, qm) * SCALE
                dv_h = _dot_tn(pr.astype(BF16), dom)
                if sw:
                    dk_h = pltpu.roll(dk_h, 64, 1)
                    dv_h = pltpu.roll(dv_h, 64, 1)
                dk = dk + dk_h
                dv = dv + dv_h
            dq_pairs.append(jnp.where(low, res[0], res[1]))
        dq_ref[...] = jnp.concatenate(dq_pairs, axis=1).astype(BF16)
        dsink_ref[...] += dsink
        dkv_win = jnp.concatenate([dk, dv], axis=1)
        prev = jnp.maximum(n - 1, 0)
        dkv_ref[pl.ds(pl.multiple_of(prev * BLOCK, BLOCK), BLOCK), :] += dkv_win[0:BLOCK, :]
        dkv_ref[pl.ds(pl.multiple_of(n * BLOCK, BLOCK), BLOCK), :] += dkv_win[BLOCK:2 * BLOCK, :]

        @pl.when(step == nb - 1)
        def _():
            small_ref[5:6, :] = jnp.concatenate([_colsum(dsink_ref[...]), jnp.zeros((1, 512 - BLOCK), F32)], axis=1)

    blk = lambda w: pl.BlockSpec((BLOCK, w), lambda t: (nb - 1 - t, 0))
    prev8 = pl.BlockSpec((8, 512), lambda t: (jnp.maximum((nb - 1 - t) * (BLOCK // 8) - 1, 0), 0))
    bf = lambda w: jax.ShapeDtypeStruct((s, w), BF16)
    return pl.pallas_call(
        body, name="mixer_bwd", grid=(nb,),
        in_specs=[pl.BlockSpec(memory_space=pltpu.SMEM), blk(512), _full((s, 256)), blk(512), blk(512), blk(512),
                  prev8, prev8, _full((N_Q_HEADS, BLOCK, 2 * BLOCK)), _full((3, 512)), _full((1, 512)),
                  _full((1, 512)), blk(512), blk(128), blk(1024)],
        out_specs=[blk(512), _full((s, 256)), blk(512), blk(512), blk(512), _full((N_Q_HEADS, BLOCK, 2 * BLOCK)),
                   _full((BLOCK, BLOCK)), _full((8, 512))],
        out_shape=[bf(512), jax.ShapeDtypeStruct((s, 256), F32), bf(512), bf(512), bf(512),
                   jax.ShapeDtypeStruct((N_Q_HEADS, BLOCK, 2 * BLOCK), F32), jax.ShapeDtypeStruct((BLOCK, BLOCK), F32),
                   jax.ShapeDtypeStruct((8, 512), F32)],
        scratch_shapes=[pltpu.VMEM((8, 512), F32)],
        compiler_params=_params(("arbitrary",), VMEM_LIMIT_LARGE),
    )(sinks, q, kv, gb, gc, xc, gc, xc, bias, conv_w, g_attn, g_conv, attn, lse, dmerged)


def _in_proj_bwd(dq, dkv, dgb, dgc, dxc, x, dx1, mod, g_norm1, w_in, tm):
    s = x.shape[0]

    def body(dq_ref, dkv_ref, dgb_ref, dgc_ref, dxc_ref, x_ref, dx1_ref, mod_ref, g_ref, w_ref,
             dproj_ref, dx_ref, small_ref):
        @pl.when(pl.program_id(0) == 0)
        def _():
            small_ref[...] = jnp.zeros_like(small_ref)

        dproj = jnp.concatenate([dq_ref[...], dkv_ref[...].astype(BF16), dgb_ref[...], dgc_ref[...], dxc_ref[...]],
                                axis=1)
        dproj_ref[...] = dproj
        dh = _dot(dproj, w_ref[...])
        dx_ref[...] = dx1_ref[...] + _norm_mod_bwd(dh, x_ref[...], g_ref[...], mod_ref[SC1:SC1 + 1, :], small_ref)

    return pl.pallas_call(
        body, name="in_proj_bwd", grid=(s // tm,),
        in_specs=[_rows(tm, 512), _rows(tm, 256), _rows(tm, 512), _rows(tm, 512), _rows(tm, 512),
                  _rows(tm, D_MODEL), _rows(tm, D_MODEL), _full((8, D_MODEL)), _full((1, D_MODEL)),
                  _full((IN_PROJ_WIDTH, D_MODEL))],
        out_specs=[_rows(tm, IN_PROJ_WIDTH), _rows(tm, D_MODEL), _full((8, D_MODEL))],
        out_shape=[jax.ShapeDtypeStruct((s, IN_PROJ_WIDTH), BF16), jax.ShapeDtypeStruct((s, D_MODEL), F32),
                   jax.ShapeDtypeStruct((8, D_MODEL), F32)],
        compiler_params=_params(("arbitrary",), VMEM_LIMIT_LARGE),
    )(dq, dkv, dgb, dgc, dxc, x, dx1, mod, g_norm1, w_in)


def _weight_grad(a, b, tk, ts, name, after=None):
    s, k = a.shape
    n = b.shape[1]
    nt = s // ts
    extra = [] if after is None else [after]

    def body(a_ref, b_ref, *rest):
        o_ref, acc_ref = rest[-2:]
        t = pl.program_id(1)
        part = _dot_tn(a_ref[...], b_ref[...])

        @pl.when(t == 0)
        def _():
            acc_ref[...] = part

        @pl.when(t > 0)
        def _():
            acc_ref[...] += part

        @pl.when(t == nt - 1)
        def _():
            o_ref[...] = acc_ref[...].astype(BF16)

    return pl.pallas_call(
        body, name=name, grid=(k // tk, nt),
        in_specs=[pl.BlockSpec((ts, tk), lambda i, t: (t, i)), pl.BlockSpec((ts, n), lambda i, t: (t, 0))]
        + [pl.BlockSpec(memory_space=pl.ANY)] * len(extra),
        out_specs=pl.BlockSpec((tk, n), lambda i, t: (i, 0)),
        out_shape=jax.ShapeDtypeStruct((k, n), BF16),
        scratch_shapes=[pltpu.VMEM((tk, n), F32)],
        compiler_params=_params(("arbitrary", "arbitrary"), VMEM_LIMIT_LARGE),
    )(a, b, *extra)


def _rel_bias_grad(dbias, bucket):
    def body(db_ref, bk_ref, o_ref, rows_ref):
        bk = bk_ref[...]
        for b in range(N_BUCKETS):
            sel = (bk == b).astype(F32)
            for h in range(N_Q_HEADS):
                rows_ref[8 * b + h:8 * b + h + 1, :] = _colsum(db_ref[h] * sel)
        o_ref[...] = jnp.sum(rows_ref[...], axis=-1, keepdims=True)

    return pl.pallas_call(
        body, name="rel_bias_grad",
        out_shape=jax.ShapeDtypeStruct((N_BUCKETS * N_Q_HEADS, 1), F32),
        scratch_shapes=[pltpu.VMEM((N_BUCKETS * N_Q_HEADS, 2 * BLOCK), F32)],
    )(dbias, bucket)


def _sum_slots(parts, after):
    def body(p_ref, after_ref, o_ref):
        acc = p_ref[0]
        for k in range(1, N_DEV):
            acc = acc + p_ref[k]
        o_ref[...] = acc

    return pl.pallas_call(body, name="sum_small_grads",
                          in_specs=[pl.BlockSpec(memory_space=pltpu.VMEM), pl.BlockSpec(memory_space=pl.ANY)],
                          out_shape=jax.ShapeDtypeStruct(parts.shape[1:], F32))(parts, after)


def _w_ada_grad(cond_t, dmod_cols):
    def body(c_ref, d_ref, o_ref):
        o_ref[...] = _dot(c_ref[...], d_ref[...])

    return pl.pallas_call(body, name="w_ada_grad",
                          out_shape=jax.ShapeDtypeStruct((cond_t.shape[0], dmod_cols.shape[1]), F32))(cond_t, dmod_cols)


def _adam_math(w, g, m, v):
    m = ADAM_B1 * m + (1.0 - ADAM_B1) * g
    v = ADAM_B2 * v + (1.0 - ADAM_B2) * (g * g)
    m_hat = m / (1.0 - ADAM_B1 ** ADAM_STEP)
    v_hat = v / (1.0 - ADAM_B2 ** ADAM_STEP)
    delta = -ADAM_LR * (m_hat / (jnp.sqrt(v_hat) + ADAM_EPS) + ADAM_WD * w)
    return delta, m, v


def _sum_parts(local, land, me, tr, name):
    r, c = local.shape[1:]

    def body(me_ref, own_ref, land_ref, o_ref):
        acc = own_ref[0].astype(F32)
        for k in range(N_DEV - 1):
            acc = acc + land_ref[k].astype(F32)
        o_ref[...] = acc

    return pl.pallas_call(
        body, name=name,
        grid_spec=pltpu.PrefetchScalarGridSpec(
            num_scalar_prefetch=1, grid=(r // tr,),
            in_specs=[pl.BlockSpec((1, tr, c), lambda i, me_ref: (me_ref[0], i, 0)),
                      pl.BlockSpec((N_DEV - 1, tr, c), lambda i, me_ref: (0, i, 0))],
            out_specs=pl.BlockSpec((tr, c), lambda i, me_ref: (i, 0))),
        out_shape=jax.ShapeDtypeStruct((r, c), F32),
        compiler_params=_params(("arbitrary",)),
    )(me, local, land)


def _adamw(w, m, v, g, tr, name):
    r, c = w.shape

    def body(w_ref, m_ref, v_ref, g_ref, d_ref, mo_ref, vo_ref):
        d_ref[...], mo_ref[...], vo_ref[...] = _adam_math(w_ref[...], g_ref[...], m_ref[...], v_ref[...])

    tile = pl.BlockSpec((tr, c), lambda i: (i, 0))
    return pl.pallas_call(
        body, name=name, grid=(r // tr,),
        in_specs=[tile] * 4, out_specs=[tile] * 3,
        out_shape=[jax.ShapeDtypeStruct((r, c), F32)] * 3,
        compiler_params=_params(("arbitrary",)),
    )(w, m, v, g)


def _behind(a, token):
    return a + token[0:a.shape[0], 0:1]


def _local_step(x, target, mod, w_in_t, weights_out_gu, weights_down, rel_bias, g_norm1, sinks, conv_w, g_attn,
                g_conv, g_norm2, g_final, exchange):
    s = x.shape[0]
    tm = min(512, s)
    tm_small = min(256, s)
    bucket = _bucket_table()
    bias = _bias_table(rel_bias, bucket)

    h, q, kv, gb, gc, xc = _in_proj(x, mod, g_norm1, w_in_t, tm)
    attn, merged, lse = _mixer_fwd(q, kv, gb, gc, xc, bias, sinks, conv_w, g_attn, g_conv)
    w_out, w_gu_t = weights_out_gu(merged)
    o1, x1 = _out_proj(merged, x, mod, w_out, tm)
    w_down = weights_down(x1)
    h2, gate, up, act, o2, dx2, fin = _ffn_fwd(x1, mod, g_norm2, w_gu_t, w_down, g_final, target, tm_small)

    do2, dgu, dx1, sm_2 = _ffn_bwd(dx2, o2, gate, up, x1, mod, g_norm2, w_down, w_gu_t, tm_small)
    ts = min(WEIGHT_GRAD_ROWS, s)
    tok_down = exchange("w_down", _weight_grad(act, do2, D_FF // 2, ts, "w_down_grad"))
    mod = _behind(mod, exchange("w_gu", _weight_grad(dgu, h2, D_FF // 2, ts, "w_gu_grad", after=tok_down)))
    do1, dmerged, sm_g1 = _out_proj_bwd(dx1, o1, mod, w_out, tm)
    g_attn_b = _behind(g_attn, exchange("w_out", _weight_grad(merged, do1, D_MODEL, ts, "w_out_grad")))
    dq, dkv, dgb, dgc, dxc, dbias, dsink, sm_mix = _mixer_bwd(
        q, kv, gb, gc, xc, bias, sinks, conv_w, g_attn_b, g_conv, attn, lse, dmerged)
    dproj, dx, sm_1 = _in_proj_bwd(dq, dkv, dgb, dgc, dxc, x, dx1, mod, g_norm1, w_in_t, tm)
    d_rel = _rel_bias_grad(dbias, bucket)

    packed = jnp.concatenate([
        sm_1[0], sm_1[1], sm_g1[0], sm_2[0], sm_2[1], sm_2[3],
        d_rel[:, 0],
        sm_1[2],
        sm_mix[5, 0:128],
        sm_mix[0], sm_mix[1],
        sm_2[2],
        fin[0],
        sm_mix[2], sm_mix[3], sm_mix[4],
        fin[2, 0:128],
    ])[None, :]
    return dx, dproj, h, packed


def kernel(x, c, rel_bias, w_ada, b_ada, g_norm1, w_in, sinks, conv_w, g_attn_out, g_conv_out, w_out, g_norm2, w_gu, w_down, g_final, loss_target, m_rel_bias, m_w_ada, m_b_ada, m_g_norm1, m_w_in, m_sinks, m_conv_w, m_g_attn_out, m_g_conv_out, m_w_out, m_g_norm2, m_w_gu, m_w_down, m_g_final, v_rel_bias, v_w_ada, v_b_ada, v_g_norm1, v_w_in, v_sinks, v_conv_w, v_g_attn_out, v_g_conv_out, v_w_out, v_g_norm2, v_w_gu, v_w_down, v_g_final):
    me = _linear(_mesh_position())
    me_arr = jnp.reshape(me, (1,)).astype(jnp.int32)
    ada_cols = w_ada.shape[2]
    tm = min(512, x.shape[1])

    cond = _silu_rows(c)
    cond_all, conv_w_all = _all_gather([cond, conv_w[0]], "gather_cond", to_bf16=False, big=False)
    cond_all = cond_all[:, 0, :]
    conv_cols = conv_w.shape[2]
    conv_w_full = conv_w_all.transpose(1, 0, 2).reshape(3, CONV_WIDTH)
    b_cols = lax.dynamic_slice_in_dim(b_ada, me * ada_cols, ada_cols, axis=1)
    mod_cols = _mod_columns(cond_all, w_ada[0], b_cols)
    mod_all = _all_gather([mod_cols], "gather_mod", to_bf16=False, big=False)[0]
    mod = lax.dynamic_index_in_dim(mod_all, me, axis=1, keepdims=False).reshape(N_MOD, D_MODEL)
    mod = jnp.concatenate([mod, jnp.zeros((2, D_MODEL), F32)], axis=0)

    w_in_t = _all_gather([w_in[0].T], "gather_w_in", to_bf16=True, big=True)[0].reshape(IN_PROJ_WIDTH, D_MODEL)
    gather_sems, staged, gather_token = _gather_start(
        _stage_blocks([w_out[0], w_gu[0].T, w_down[0]], w_in_t, "stage_weights"), "gather_start_weights")
    mod = _behind(mod, gather_token)

    def weights_out_gu(after):
        got = _gather_pass_on(_gather_wait(gather_sems[0:4], staged[0:2], [after], "gather_wait_out_gu"),
                              "gather_pass_on_out_gu")
        return got[0].reshape(D_MODEL, D_MODEL), got[1].reshape(2 * D_FF, D_MODEL)

    def weights_down(after):
        got = _gather_pass_on(_gather_wait(gather_sems[4:6], staged[2:3], [after], "gather_wait_down"),
                              "gather_pass_on_down")
        return got[0].reshape(D_FF, D_MODEL)

    started = {}

    def exchange(name, dw):
        st = _exchange_start(dw.reshape(N_DEV, dw.shape[0] // N_DEV, dw.shape[1]), "exchange_start_" + name)
        started[name] = st
        return st[4]

    dx, dproj, h, packed = _local_step(
        x[0], loss_target[0], mod, w_in_t, weights_out_gu, weights_down, rel_bias, g_norm1, sinks[0], conv_w_full,
        g_attn_out, g_conv_out, g_norm2, g_final[None, :], exchange)

    packed_all = _all_gather([packed], "gather_small_grads", to_bf16=False, big=False)[0]
    tok_in = exchange("w_in", _weight_grad(dproj, h, IN_PROJ_WIDTH // 2, min(WEIGHT_GRAD_ROWS, x.shape[1]),
                                           "w_in_grad", after=packed_all))
    small = _sum_slots(packed_all, tok_in)[0]
    dmod_all = packed_all[:, 0, OFF_DMOD:OFF_DMOD + N_MOD * D_MODEL]
    dmod_cols = lax.dynamic_slice_in_dim(dmod_all, me * ada_cols, ada_cols, axis=1)
    cond_t = jnp.zeros((D_MODEL, 128), F32).at[:, 0:N_DEV].set(cond_all.T)
    dmod_pad = jnp.zeros((128, ada_cols), F32).at[0:N_DEV, :].set(dmod_cols)
    g_ada = _w_ada_grad(cond_t, dmod_pad)
    d_ada, nm_ada, nv_ada = _adamw(w_ada[0], m_w_ada[0], v_w_ada[0], g_ada, 256, "adamw_w_ada")

    loss = small[OFF_LOSS]
    seg = lambda off, n: small[off:off + n]
    conv_g_full = seg(OFF_CONVW, 3 * CONV_WIDTH).reshape(3, CONV_WIDTH)
    small_grads = {
        "rel_bias": seg(OFF_RELB, 256).reshape(N_BUCKETS, N_Q_HEADS),
        "b_ada": seg(OFF_DMOD, N_MOD * D_MODEL).reshape(1, N_MOD * D_MODEL),
        "g_norm1": seg(OFF_GN1, D_MODEL).reshape(1, D_MODEL),
        "sinks": seg(OFF_SINK, N_Q_HEADS).reshape(1, N_Q_HEADS),
        "conv_w": lax.dynamic_slice_in_dim(conv_g_full, me * conv_cols, conv_cols, axis=1)[None],
        "g_attn_out": seg(OFF_GATT, ATTN_WIDTH).reshape(1, ATTN_WIDTH),
        "g_conv_out": seg(OFF_GCV, CONV_WIDTH).reshape(1, CONV_WIDTH),
        "g_norm2": seg(OFF_GN2, D_MODEL).reshape(1, D_MODEL),
        "g_final": seg(OFF_GFIN, D_MODEL),
    }
    small_state = {
        "rel_bias": (rel_bias, m_rel_bias, v_rel_bias), "b_ada": (b_ada, m_b_ada, v_b_ada),
        "g_norm1": (g_norm1, m_g_norm1, v_g_norm1), "sinks": (sinks, m_sinks, v_sinks),
        "conv_w": (conv_w, m_conv_w, v_conv_w), "g_attn_out": (g_attn_out, m_g_attn_out, v_g_attn_out),
        "g_conv_out": (g_conv_out, m_g_conv_out, v_g_conv_out), "g_norm2": (g_norm2, m_g_norm2, v_g_norm2),
        "g_final": (g_final, m_g_final, v_g_final),
    }
    names = list(small_grads)
    sizes = [small_grads[k].size for k in names]
    total = sum(sizes)
    padded = -(-total // 1024) * 1024

    def pack(arrs):
        flat = jnp.concatenate([a.reshape(-1) for a in arrs] + [jnp.ones((padded - total,), F32)])
        return flat.reshape(padded // 128, 128)

    sw = pack([small_state[k][0] for k in names])
    sm = pack([small_state[k][1] for k in names])
    sv = pack([small_state[k][2] for k in names])
    sg = pack([small_grads[k] for k in names])
    sd, snm, snv = _adamw(sw, sm, sv, sg, padded // 128, "adamw_small")

    def unpack(flat2d):
        flat = flat2d.reshape(-1)
        out, off = {}, 0
        for k, n in zip(names, sizes):
            out[k] = flat[off:off + n].reshape(small_grads[k].shape)
            off += n
        return out

    sd_all = sd
    sd, snm, snv = unpack(sd), unpack(snm), unpack(snv)

    def finish(name, after, tr):
        src, land = _exchange_wait(started[name], after, "exchange_wait_" + name)
        return _sum_parts(src, land, me_arr, tr, "sum_parts_" + name)

    g_down = finish("w_down", [sd_all], 176)
    d_down, nm_down, nv_down = _adamw(w_down[0], m_w_down[0], v_w_down[0], g_down, 176, "adamw_w_down")
    g_gu = finish("w_gu", [nv_down], 352)
    d_gu, nm_gu, nv_gu = _adamw(w_gu[0].T, m_w_gu[0].T, v_w_gu[0].T, g_gu, 352, "adamw_w_gu")
    g_out = finish("w_out", [nv_gu], 128)
    d_out, nm_out, nv_out = _adamw(w_out[0], m_w_out[0], v_w_out[0], g_out, 128, "adamw_w_out")
    g_in = finish("w_in", [nv_out, nv_ada], 144)
    d_in, nm_in, nv_in = _adamw(w_in[0].T, m_w_in[0].T, v_w_in[0].T, g_in, 144, "adamw_w_in")

    big = {
        "w_ada": (g_ada[None], d_ada[None], nm_ada[None], nv_ada[None]),
        "w_in": (g_in.T[None], d_in.T[None], nm_in.T[None], nv_in.T[None]),
        "w_out": (g_out[None], d_out[None], nm_out[None], nv_out[None]),
        "w_gu": (g_gu.T[None], d_gu.T[None], nm_gu.T[None], nv_gu.T[None]),
        "w_down": (g_down[None], d_down[None], nm_down[None], nv_down[None]),
    }
    order = ["rel_bias", "w_ada", "b_ada", "g_norm1", "w_in", "sinks", "conv_w", "g_attn_out", "g_conv_out", "w_out",
             "g_norm2", "w_gu", "w_down", "g_final"]
    grads = [big[k][0] if k in big else small_grads[k] for k in order]
    deltas = [big[k][1] if k in big else sd[k] for k in order]
    new_m = [big[k][2] if k in big else snm[k] for k in order]
    new_v = [big[k][3] if k in big else snv[k] for k in order]
    return (loss, dx[None], *grads, *deltas, *new_m, *new_v)
```

```python
import functools
import math

import jax
import jax.numpy as jnp
from jax import lax
from jax.experimental import pallas as pl
from jax.experimental.pallas import tpu as pltpu

F32 = jnp.float32
BF16 = jnp.bfloat16

D_MODEL = 1024
HEAD_DIM = 64
N_Q_HEADS = 8
ATTN_WIDTH = 512
KV_WIDTH = 128
CONV_WIDTH = 512
IN_PROJ_WIDTH = 2304
D_FF = 2816
N_MOD = 6
N_BUCKETS = 32
MAX_DISTANCE = 128
BLOCK = 128
EPS = 1e-6
NEG_INF = -1e30
SCALE = HEAD_DIM ** -0.5
N_DEV = 8

ADAM_LR = 0.001
ADAM_B1 = 0.9
ADAM_B2 = 0.999
ADAM_EPS = 1e-08
ADAM_WD = 0.01
ADAM_STEP = 10

SH1, SC1, G1, SH2, SC2, G2 = range(6)

VMEM_LIMIT_LARGE = 56 * 1024 * 1024
WEIGHT_GRAD_ROWS = 2048
FFN_CHUNKS = 2
PREV_ROWS = 16
MESH_ID = pl.DeviceIdType.MESH

OFF_DMOD = 0
OFF_RELB = OFF_DMOD + N_MOD * D_MODEL
OFF_GN1 = OFF_RELB + N_BUCKETS * N_Q_HEADS
OFF_SINK = OFF_GN1 + D_MODEL
OFF_GATT = OFF_SINK + 128
OFF_GCV = OFF_GATT + ATTN_WIDTH
OFF_GN2 = OFF_GCV + CONV_WIDTH
OFF_GFIN = OFF_GN2 + D_MODEL
OFF_CONVW = OFF_GFIN + D_MODEL
OFF_LOSS = OFF_CONVW + 3 * CONV_WIDTH
PACKED = OFF_LOSS + 128


def _params(sem=None, vmem=None):
    return pltpu.CompilerParams(dimension_semantics=sem, vmem_limit_bytes=vmem)


def _full(shape):
    nd = len(shape)
    return pl.BlockSpec(shape, lambda *_: (0,) * nd)


def _rows(tm, width):
    return pl.BlockSpec((tm, width), lambda i, *_: (i, 0))


def _sigmoid(x):
    return 1.0 / (1.0 + jnp.exp(-x))


def _rsqrt_mean_sq(x):
    return lax.rsqrt(jnp.mean(x * x, axis=-1, keepdims=True) + EPS)


def _colsum(x):
    return jnp.sum(x, axis=0, keepdims=True)


def _dot(a, b):
    return jnp.dot(a, b, preferred_element_type=F32)


def _dot_nt(a, b):
    return lax.dot_general(a, b, (((1,), (1,)), ((), ())), preferred_element_type=F32)


def _dot_tn(a, b):
    return lax.dot_general(a, b, (((0,), (0,)), ((), ())), preferred_element_type=F32)


def _mesh_position():
    return lax.axis_index("x"), lax.axis_index("y"), lax.axis_index("c")


def _linear(p):
    return 4 * p[0] + 2 * p[1] + p[2]


def _all_gather(arrs, name, to_bf16, big):
    n = len(arrs)
    out_dtype = BF16 if to_bf16 else F32

    def body(*refs):
        in_refs, out_refs = refs[:n], refs[n:2 * n]
        rest = refs[2 * n:]
        if to_bf16:
            stage, rest = rest[:n], rest[n:]
            for a in range(n):
                stage[a][...] = in_refs[a][...].astype(BF16)
            srcs = stage
        else:
            srcs = in_refs
        send_sems, recv_sems, local_sems = rest
        x, y, c = _mesh_position()
        me, sibling = (x, y, c), (x, y, 1 - c)
        chips = [(1 - x, y), (x, 1 - y), (1 - x, 1 - y)]

        def slot(a, p):
            return out_refs[a].at[_linear(p)]

        def copy(k, a, block, to, src=None):
            return pltpu.make_async_remote_copy(
                src_ref=slot(a, block) if src is None else src,
                dst_ref=slot(a, block),
                send_sem=send_sems.at[k * n + a],
                recv_sem=recv_sems.at[k * n + a],
                device_id=to,
                device_id_type=MESH_ID,
            )

        mine = [pltpu.make_async_copy(srcs[a], slot(a, me), local_sems.at[a]) for a in range(n)]
        for cp in mine:
            cp.start()
        first = [copy(0, a, me, sibling, src=srcs[a]) for a in range(n)]
        for j, chip in enumerate(chips):
            first += [copy(1 + j, a, me, (*chip, c), src=srcs[a]) for a in range(n)]
        for cp in first:
            cp.start()
        passed = []
        for j, chip in enumerate(chips):
            for a in range(n):
                copy(1 + j, a, (*chip, c), me).wait_recv()
                fwd = copy(4 + j, a, (*chip, c), sibling)
                fwd.start()
                passed.append(fwd)
        for a in range(n):
            copy(0, a, sibling, me).wait_recv()
        for j, chip in enumerate(chips):
            for a in range(n):
                copy(4 + j, a, (*chip, 1 - c), me).wait_recv()
        for cp in first + passed:
            cp.wait_send()
        for cp in mine:
            cp.wait()

    vmem = pl.BlockSpec(memory_space=pltpu.VMEM)
    out_space = pl.BlockSpec(memory_space=pl.ANY) if big else vmem
    scratch = [pltpu.VMEM(a.shape, BF16) for a in arrs] if to_bf16 else []
    scratch += [pltpu.SemaphoreType.DMA((7 * n,)), pltpu.SemaphoreType.DMA((7 * n,)),
                pltpu.SemaphoreType.DMA((n,))]
    outs = pl.pallas_call(
        body, name=name,
        out_shape=[jax.ShapeDtypeStruct((N_DEV,) + a.shape, out_dtype) for a in arrs],
        in_specs=[vmem] * n, out_specs=[out_space] * n,
        scratch_shapes=scratch,
        compiler_params=_params(vmem=VMEM_LIMIT_LARGE if big else None),
    )(*arrs)
    return list(outs)


def _peer(k):
    x, y, c = _mesh_position()
    return (1 - x if k & 4 else x, 1 - y if k & 2 else y, 1 - c if k & 1 else c)


def _all_gather_small(arrs, name):
    n = len(arrs)

    def body(*refs):
        in_refs, out_refs = refs[:n], refs[n:2 * n]
        send_sems, recv_sems, local_sems = refs[2 * n:]
        me = _linear(_mesh_position())
        mine = [pltpu.make_async_copy(in_refs[a], out_refs[a].at[me], local_sems.at[a]) for a in range(n)]
        for cp in mine:
            cp.start()
        sends = []
        for k in range(1, N_DEV):
            for a in range(n):
                sends.append(pltpu.make_async_remote_copy(
                    src_ref=in_refs[a], dst_ref=out_refs[a].at[me],
                    send_sem=send_sems.at[(k - 1) * n + a], recv_sem=recv_sems.at[(k - 1) * n + a],
                    device_id=_peer(k), device_id_type=MESH_ID))
                sends[-1].start()
        for k in range(1, N_DEV):
            for a in range(n):
                pltpu.make_async_remote_copy(
                    src_ref=in_refs[a], dst_ref=out_refs[a].at[_linear(_peer(k))],
                    send_sem=send_sems.at[(k - 1) * n + a], recv_sem=recv_sems.at[(k - 1) * n + a],
                    device_id=_peer(k), device_id_type=MESH_ID).wait_recv()
        for cp in sends:
            cp.wait_send()
        for cp in mine:
            cp.wait()

    vmem = pl.BlockSpec(memory_space=pltpu.VMEM)
    return list(pl.pallas_call(
        body, name=name,
        out_shape=[jax.ShapeDtypeStruct((N_DEV,) + a.shape, F32) for a in arrs],
        in_specs=[vmem] * n, out_specs=[vmem] * n,
        scratch_shapes=[pltpu.SemaphoreType.DMA((7 * n,)), pltpu.SemaphoreType.DMA((7 * n,)),
                        pltpu.SemaphoreType.DMA((n,))],
    )(*arrs))


HBM_SPEC = pl.BlockSpec(memory_space=pltpu.HBM)
SEM_SPEC = pl.BlockSpec(memory_space=pltpu.SEMAPHORE)
DATAFLOW = pltpu.SideEffectType.DATAFLOW_SIDE_EFFECTING


def _exchange_start(src, name):
    r, c = src.shape[1:]

    def body(src_ref, land_ref, send_sems, recv_sems, src_thru, land_thru, token):
        for k in range(1, N_DEV):
            peer = _peer(k)
            pltpu.make_async_remote_copy(
                src_ref=src_ref.at[_linear(peer)], dst_ref=land_ref.at[k - 1],
                send_sem=send_sems.at[k - 1], recv_sem=recv_sems.at[k - 1],
                device_id=peer, device_id_type=MESH_ID).start()
        token[...] = jnp.zeros_like(token)

    land = lax.empty((N_DEV - 1, r, c), src.dtype)
    return pl.pallas_call(
        body, name=name,
        out_shape=(pltpu.SemaphoreType.DMA((N_DEV - 1,)), pltpu.SemaphoreType.DMA((N_DEV - 1,)),
                   pltpu.HBM(src.shape, src.dtype), pltpu.HBM(land.shape, land.dtype),
                   jax.ShapeDtypeStruct((8, 128), F32)),
        in_specs=(HBM_SPEC, HBM_SPEC),
        out_specs=(SEM_SPEC, SEM_SPEC, HBM_SPEC, HBM_SPEC, pl.BlockSpec(memory_space=pltpu.VMEM)),
        input_output_aliases={0: 2, 1: 3},
        compiler_params=pltpu.CompilerParams(has_side_effects=DATAFLOW),
    )(pltpu.with_memory_space_constraint(src, pltpu.HBM), pltpu.with_memory_space_constraint(land, pltpu.HBM))


def _exchange_wait(started, after, name):
    send_sems, recv_sems, src_thru, land_thru, _ = started

    def body(src_ref, land_ref, send_sems, recv_sems, *rest):
        for k in range(1, N_DEV):
            cp = pltpu.make_async_remote_copy(
                src_ref=src_ref.at[0], dst_ref=land_ref.at[k - 1],
                send_sem=send_sems.at[k - 1], recv_sem=recv_sems.at[k - 1],
                device_id=_peer(k), device_id_type=MESH_ID)
            cp.wait_send()
            cp.wait_recv()

    return pl.pallas_call(
        body, name=name,
        out_shape=(pltpu.HBM(src_thru.shape, src_thru.dtype), pltpu.HBM(land_thru.shape, land_thru.dtype)),
        in_specs=(HBM_SPEC, HBM_SPEC, SEM_SPEC, SEM_SPEC) + (pl.BlockSpec(memory_space=pl.ANY),) * len(after),
        out_specs=(HBM_SPEC, HBM_SPEC), input_output_aliases={0: 0, 1: 1},
        compiler_params=pltpu.CompilerParams(has_side_effects=DATAFLOW),
    )(src_thru, land_thru, send_sems, recv_sems, *after)


def _stage_blocks(arrs, after, name):
    n = len(arrs)

    def body(*refs):
        in_refs, out_refs, stage, sems = refs[:n], refs[n + 1:2 * n + 1], refs[2 * n + 1:3 * n + 1], refs[3 * n + 1]
        me = _linear(_mesh_position())
        copies = []
        for a in range(n):
            stage[a][...] = in_refs[a][...].astype(BF16)
            copies.append(pltpu.make_async_copy(stage[a], out_refs[a].at[me], sems.at[a]))
            copies[-1].start()
        for cp in copies:
            cp.wait()

    return list(pl.pallas_call(
        body, name=name,
        out_shape=[jax.ShapeDtypeStruct((N_DEV,) + a.shape, BF16) for a in arrs],
        in_specs=[pl.BlockSpec(memory_space=pltpu.VMEM)] * n + [pl.BlockSpec(memory_space=pl.ANY)],
        out_specs=[pl.BlockSpec(memory_space=pl.ANY)] * n,
        scratch_shapes=[pltpu.VMEM(a.shape, BF16) for a in arrs] + [pltpu.SemaphoreType.DMA((n,))],
        compiler_params=_params(vmem=VMEM_LIMIT_LARGE),
    )(*arrs, after))


def _same_core_peers():
    x, y, c = _mesh_position()
    return [(x, y, 1 - c), (1 - x, y, c), (x, 1 - y, c), (1 - x, 1 - y, c)]


def _gather_start(bufs, name):
    n = len(bufs)

    def body(*refs):
        buf_refs, rest = refs[:n], refs[n:]
        sems, token = rest[:2 * n], rest[-1]
        me = _linear(_mesh_position())
        for a in range(n):
            for k, peer in enumerate(_same_core_peers()):
                pltpu.make_async_remote_copy(
                    src_ref=buf_refs[a].at[me], dst_ref=buf_refs[a].at[me],
                    send_sem=sems[2 * a].at[k], recv_sem=sems[2 * a + 1].at[k],
                    device_id=peer, device_id_type=MESH_ID).start()
        token[...] = jnp.zeros_like(token)

    outs = pl.pallas_call(
        body, name=name,
        out_shape=tuple(pltpu.SemaphoreType.DMA((4,)) for _ in range(2 * n))
        + tuple(pltpu.HBM(b.shape, b.dtype) for b in bufs) + (jax.ShapeDtypeStruct((8, 128), F32),),
        in_specs=(HBM_SPEC,) * n,
        out_specs=(SEM_SPEC,) * (2 * n) + (HBM_SPEC,) * n + (pl.BlockSpec(memory_space=pltpu.VMEM),),
        input_output_aliases={a: 2 * n + a for a in range(n)},
        compiler_params=pltpu.CompilerParams(has_side_effects=DATAFLOW),
    )(*[pltpu.with_memory_space_constraint(b, pltpu.HBM) for b in bufs])
    return outs[:2 * n], outs[2 * n:3 * n], outs[3 * n]


def _gather_wait(sems, bufs, after, name):
    n = len(bufs)

    def body(*refs):
        buf_refs, sem_refs = refs[:n], refs[n:3 * n]
        x, y, c = _mesh_position()
        me = _linear((x, y, c))
        for a in range(n):
            for k, peer in enumerate(_same_core_peers()):
                cp = pltpu.make_async_remote_copy(
                    src_ref=buf_refs[a].at[me], dst_ref=buf_refs[a].at[_linear(peer)],
                    send_sem=sem_refs[2 * a].at[k], recv_sem=sem_refs[2 * a + 1].at[k],
                    device_id=peer, device_id_type=MESH_ID)
                cp.wait_send()
                cp.wait_recv()

    return list(pl.pallas_call(
        body, name=name,
        out_shape=tuple(pltpu.HBM(b.shape, b.dtype) for b in bufs),
        in_specs=(HBM_SPEC,) * n + (SEM_SPEC,) * (2 * n) + (pl.BlockSpec(memory_space=pl.ANY),) * len(after),
        out_specs=(HBM_SPEC,) * n, input_output_aliases={a: a for a in range(n)},
        compiler_params=pltpu.CompilerParams(has_side_effects=DATAFLOW),
    )(*bufs, *sems, *after))


def _gather_pass_on(bufs, name):
    n = len(bufs)

    def body(*refs):
        out_refs = refs[n:2 * n]
        send_sems, recv_sems = refs[2 * n:]
        x, y, c = _mesh_position()
        sibling = (x, y, 1 - c)
        chips = [(1 - x, y), (x, 1 - y), (1 - x, 1 - y)]
        copies = []
        for a in range(n):
            for j, chip in enumerate(chips):
                block = out_refs[a].at[_linear((*chip, c))]
                copies.append(pltpu.make_async_remote_copy(
                    src_ref=block, dst_ref=block, send_sem=send_sems.at[3 * a + j], recv_sem=recv_sems.at[3 * a + j],
                    device_id=sibling, device_id_type=MESH_ID))
                copies[-1].start()
        for a in range(n):
            for j, chip in enumerate(chips):
                copies[3 * a + j].wait_send()
                theirs = out_refs[a].at[_linear((*chip, 1 - c))]
                pltpu.make_async_remote_copy(
                    src_ref=theirs, dst_ref=theirs, send_sem=send_sems.at[3 * a + j], recv_sem=recv_sems.at[3 * a + j],
                    device_id=sibling, device_id_type=MESH_ID).wait_recv()

    hbm = pl.BlockSpec(memory_space=pl.ANY)
    return list(pl.pallas_call(
        body, name=name,
        out_shape=[jax.ShapeDtypeStruct(b.shape, b.dtype) for b in bufs],
        in_specs=[hbm] * n, out_specs=[hbm] * n, input_output_aliases={a: a for a in range(n)},
        scratch_shapes=[pltpu.SemaphoreType.DMA((3 * n,)), pltpu.SemaphoreType.DMA((3 * n,))],
    )(*bufs))


def _silu_rows(c):
    def body(c_ref, o_ref):
        v = c_ref[...]
        o_ref[...] = v * _sigmoid(v)

    return pl.pallas_call(body, name="cond_silu", out_shape=jax.ShapeDtypeStruct(c.shape, F32))(c)


def _mod_columns(cond_all, w_ada, b_cols):
    def body(c_ref, w_ref, b_ref, o_ref):
        o_ref[...] = _dot(c_ref[...], w_ref[...]) + b_ref[...]

    return pl.pallas_call(body, name="mod_columns",
                          out_shape=jax.ShapeDtypeStruct((N_DEV, w_ada.shape[1]), F32))(cond_all, w_ada, b_cols)


def _in_proj(x, mod, g_norm1, w_in, tm):
    s = x.shape[0]

    def body(x_ref, mod_ref, g_ref, w_ref, h_ref, q_ref, kv_ref, gb_ref, gc_ref, xc_ref):
        xf = x_ref[...]
        n = xf * _rsqrt_mean_sq(xf) * g_ref[...]
        h = (n * (1.0 + mod_ref[SC1:SC1 + 1, :]) + mod_ref[SH1:SH1 + 1, :]).astype(BF16)
        h_ref[...] = h
        p = _dot_nt(h, w_ref[...])
        q_ref[...] = p[:, 0:512].astype(BF16)
        kv_ref[...] = p[:, 512:768].astype(BF16)
        gb_ref[...] = p[:, 768:1280].astype(BF16)
        gc_ref[...] = p[:, 1280:1792].astype(BF16)
        xc_ref[...] = p[:, 1792:2304].astype(BF16)

    return pl.pallas_call(
        body, name="in_proj", grid=(s // tm,),
        in_specs=[_rows(tm, D_MODEL), _full((8, D_MODEL)), _full((1, D_MODEL)), _full((IN_PROJ_WIDTH, D_MODEL))],
        out_specs=[_rows(tm, D_MODEL), _rows(tm, 512), _rows(tm, 256), _rows(tm, 512), _rows(tm, 512), _rows(tm, 512)],
        out_shape=[jax.ShapeDtypeStruct((s, D_MODEL), BF16), jax.ShapeDtypeStruct((s, 512), BF16),
                   jax.ShapeDtypeStruct((s, 256), BF16), jax.ShapeDtypeStruct((s, 512), BF16),
                   jax.ShapeDtypeStruct((s, 512), BF16), jax.ShapeDtypeStruct((s, 512), BF16)],
        compiler_params=_params(("arbitrary",), VMEM_LIMIT_LARGE),
    )(x, mod, g_norm1, w_in)


def _t5_bucket(dist):
    max_exact = N_BUCKETS // 2
    is_small = dist < max_exact
    d = jnp.maximum(dist, 1).astype(F32)
    large = max_exact + (jnp.log(d / max_exact) / math.log(MAX_DISTANCE / max_exact)
                         * (N_BUCKETS - max_exact)).astype(jnp.int32)
    large = jnp.minimum(large, N_BUCKETS - 1)
    return jnp.where(is_small, dist, large)


def _bucket_table():
    qi = jnp.arange(BLOCK, dtype=jnp.int32)[:, None]
    sj = jnp.arange(2 * BLOCK, dtype=jnp.int32)[None, :]
    return _t5_bucket(jnp.maximum(qi + BLOCK - sj, 0))


def _window_mask():
    qi = lax.broadcasted_iota(jnp.int32, (BLOCK, 2 * BLOCK), 0)
    sj = lax.broadcasted_iota(jnp.int32, (BLOCK, 2 * BLOCK), 1)
    dist = qi + BLOCK - sj
    return (dist >= 0) & (dist < BLOCK)


def _bias_table(rel_bias, bucket):
    def body(rb_ref, bk_ref, o_ref):
        bk = bk_ref[...]
        inside = _window_mask()
        for h in range(N_Q_HEADS):
            acc = jnp.zeros((BLOCK, 2 * BLOCK), F32)
            for b in range(N_BUCKETS):
                acc = jnp.where(bk == b, rb_ref[b, h], acc)
            o_ref[h] = jnp.where(inside, acc, NEG_INF)

    return pl.pallas_call(
        body, name="bias_table",
        in_specs=[pl.BlockSpec(memory_space=pltpu.SMEM), pl.BlockSpec(memory_space=pltpu.VMEM)],
        out_shape=jax.ShapeDtypeStruct((N_Q_HEADS, BLOCK, 2 * BLOCK), F32),
    )(rel_bias, bucket)


def _load_kv_window(kv_ref, n):
    prev = jnp.maximum(n - 1, 0)
    kvw = jnp.concatenate([kv_ref[pl.ds(pl.multiple_of(prev * BLOCK, BLOCK), BLOCK), :],
                           kv_ref[pl.ds(pl.multiple_of(n * BLOCK, BLOCK), BLOCK), :]], axis=0)
    k, v = kvw[:, 0:128], kvw[:, 128:256]
    k_sw = pltpu.roll(k.astype(F32), 64, 1).astype(BF16)
    v_sw = pltpu.roll(v.astype(F32), 64, 1).astype(BF16)
    return (k, k_sw), (v, v_sw)


def _conv_taps(gc, xc, gc_prev, xc_prev, n):
    u = gc * xc
    before = jnp.where(n > 0, gc_prev.astype(F32) * xc_prev.astype(F32), 0.0)
    last = before.shape[0] - 1
    row = lax.broadcasted_iota(jnp.int32, u.shape, 0)
    u1 = jnp.where(row == 0, before[last:last + 1, :], pltpu.roll(u, 1, 0))
    u2 = jnp.where(row == 0, before[last - 1:last, :],
                   jnp.where(row == 1, before[last:last + 1, :], pltpu.roll(u, 2, 0)))
    return u, u1, u2


def _mixer_fwd(q, kv, gb, gc, xc, bias, sinks, conv_w, g_attn, g_conv):
    s = q.shape[0]
    nb = s // BLOCK

    def body(sink_ref, q_ref, kv_ref, gb_ref, gc_ref, xc_ref, gcp_ref, xcp_ref, bias_ref, cw_ref, ga_ref, gcv_ref,
             attn_ref, merged_ref, lse_ref):
        n = pl.program_id(0)
        ks, vs = _load_kv_window(kv_ref, n)
        lane = lax.broadcasted_iota(jnp.int32, (BLOCK, BLOCK), 1)
        low = lane < HEAD_DIM
        col = lax.broadcasted_iota(jnp.int32, (BLOCK, 2 * BLOCK), 1)
        no_prev = (col < BLOCK) & (n == 0)
        lse_all = jnp.zeros((BLOCK, BLOCK), F32)
        pairs = []
        for p in range(4):
            qp = q_ref[:, 128 * p:128 * (p + 1)].astype(F32)
            kvh = p // 2
            res = []
            for e in range(2):
                h = 2 * p + e
                qm = jnp.where(low if e == 0 else ~low, qp, 0.0).astype(BF16)
                sw = 0 if kvh == e else 1
                sc = _dot_nt(qm, ks[sw]) * SCALE + bias_ref[h]
                sc = jnp.where(no_prev, NEG_INF, sc)
                sink = sink_ref[h]
                m = jnp.maximum(jnp.max(sc, axis=-1, keepdims=True), sink)
                pe = jnp.exp(sc - m)
                den = jnp.sum(pe, axis=-1, keepdims=True) + jnp.exp(sink - m)
                res.append(_dot(pe.astype(BF16), vs[sw]) / den)
                lse_all = lse_all + jnp.where(lane == h, m + jnp.log(den), 0.0)
            pairs.append(jnp.where(low, res[0], res[1]))
        attn = jnp.concatenate(pairs, axis=1)
        attn_ref[...] = attn
        lse_ref[...] = lse_all
        u, u1, u2 = _conv_taps(gc_ref[...].astype(F32), xc_ref[...].astype(F32), gcp_ref[...], xcp_ref[...], n)
        cw = cw_ref[...]
        cv = gb_ref[...].astype(F32) * (cw[0:1, :] * u2 + cw[1:2, :] * u1 + cw[2:3, :] * u)
        an = attn * _rsqrt_mean_sq(attn) * ga_ref[...]
        cn = cv * _rsqrt_mean_sq(cv) * gcv_ref[...]
        merged_ref[...] = jnp.concatenate([an, cn], axis=1).astype(BF16)

    blk = lambda w: pl.BlockSpec((BLOCK, w), lambda n: (n, 0))
    prev8 = pl.BlockSpec((PREV_ROWS, 512), lambda n: (jnp.maximum(n * (BLOCK // PREV_ROWS) - 1, 0), 0))
    return pl.pallas_call(
        body, name="mixer_fwd", grid=(nb,),
        in_specs=[pl.BlockSpec(memory_space=pltpu.SMEM), blk(512), _full((s, 256)), blk(512), blk(512), blk(512),
                  prev8, prev8, _full((N_Q_HEADS, BLOCK, 2 * BLOCK)), _full((3, 512)), _full((1, 512)),
                  _full((1, 512))],
        out_specs=[blk(512), blk(1024), blk(128)],
        out_shape=[jax.ShapeDtypeStruct((s, 512), F32), jax.ShapeDtypeStruct((s, 1024), BF16),
                   jax.ShapeDtypeStruct((s, 128), F32)],
        compiler_params=_params(("arbitrary",)),
    )(sinks, q, kv, gb, gc, xc, gc, xc, bias, conv_w, g_attn, g_conv)


def _out_proj(merged, x, mod, w_out, tm):
    s = x.shape[0]

    def body(m_ref, x_ref, mod_ref, w_ref, o_ref, x1_ref):
        o = _dot(m_ref[...], w_ref[...])
        o_ref[...] = o.astype(BF16)
        x1_ref[...] = x_ref[...] + mod_ref[G1:G1 + 1, :] * o

    return pl.pallas_call(
        body, name="out_proj", grid=(s // tm,),
        in_specs=[_rows(tm, D_MODEL), _rows(tm, D_MODEL), _full((8, D_MODEL)), _full((D_MODEL, D_MODEL))],
        out_specs=[_rows(tm, D_MODEL), _rows(tm, D_MODEL)],
        out_shape=[jax.ShapeDtypeStruct((s, D_MODEL), BF16), jax.ShapeDtypeStruct((s, D_MODEL), F32)],
        compiler_params=_params(("arbitrary",)),
    )(merged, x, mod, w_out)


def _resident(shape):
    nd = len(shape)
    return pl.BlockSpec(shape, lambda *_: (0,) * nd, pipeline_mode=pl.Buffered(1))


def _ffn_fwd(x1, mod, g_norm2, w_gu, w_down, g_final, target, tm):
    s = x1.shape[0]
    chunk = D_FF // FFN_CHUNKS

    def body(x_ref, mod_ref, g_ref, wgu_ref, wd_ref, gf_ref, t_ref,
             h_ref, gate_ref, up_ref, act_ref, o_ref, dx2_ref, small_ref):
        @pl.when(pl.program_id(0) == 0)
        def _():
            small_ref[...] = jnp.zeros_like(small_ref)

        xf = x_ref[...]
        n = xf * _rsqrt_mean_sq(xf) * g_ref[...]
        h = (n * (1.0 + mod_ref[SC2:SC2 + 1, :]) + mod_ref[SH2:SH2 + 1, :]).astype(BF16)
        h_ref[...] = h
        o = None
        for j in range(FFN_CHUNKS):
            lo = j * chunk
            gate = _dot_nt(h, wgu_ref[lo:lo + chunk, :])
            up = _dot_nt(h, wgu_ref[D_FF + lo:D_FF + lo + chunk, :])
            gate_ref[:, lo:lo + chunk] = gate.astype(BF16)
            up_ref[:, lo:lo + chunk] = up.astype(BF16)
            act = (gate * _sigmoid(gate) * up).astype(BF16)
            act_ref[:, lo:lo + chunk] = act
            part = _dot(act, wd_ref[lo:lo + chunk, :])
            o = part if o is None else o + part
        o_ref[...] = o.astype(BF16)
        x2 = xf + mod_ref[G2:G2 + 1, :] * o
        r = _rsqrt_mean_sq(x2)
        xn = x2 * r
        gf = gf_ref[...]
        err = xn * gf - t_ref[...]
        dy = err * (1.0 / D_MODEL)
        dxn = dy * gf
        dx2_ref[...] = (r * (dxn - xn * jnp.mean(dxn * xn, axis=-1, keepdims=True))).astype(BF16)
        small_ref[0:1, :] += _colsum(dy * xn)
        small_ref[1:2, :] += _colsum(err * err)

        @pl.when(pl.program_id(0) == pl.num_programs(0) - 1)
        def _():
            total = jnp.sum(small_ref[1:2, :], axis=-1, keepdims=True) * (0.5 / D_MODEL)
            small_ref[2:3, :] = jnp.broadcast_to(total, (1, D_MODEL))

    wide = jax.ShapeDtypeStruct((s, D_FF), BF16)
    return pl.pallas_call(
        body, name="ffn_fwd", grid=(s // tm,),
        in_specs=[_rows(tm, D_MODEL), _full((8, D_MODEL)), _full((1, D_MODEL)), _resident((2 * D_FF, D_MODEL)),
                  _resident((D_FF, D_MODEL)), _full((1, D_MODEL)), _rows(tm, D_MODEL)],
        out_specs=[_rows(tm, D_MODEL), _rows(tm, D_FF), _rows(tm, D_FF), _rows(tm, D_FF), _rows(tm, D_MODEL),
                   _rows(tm, D_MODEL), _full((8, D_MODEL))],
        out_shape=[jax.ShapeDtypeStruct((s, D_MODEL), BF16), wide, wide, wide,
                   jax.ShapeDtypeStruct((s, D_MODEL), BF16), jax.ShapeDtypeStruct((s, D_MODEL), BF16),
                   jax.ShapeDtypeStruct((8, D_MODEL), F32)],
        compiler_params=_params(("arbitrary",), VMEM_LIMIT_LARGE),
    )(x1, mod, g_norm2, w_gu, w_down, g_final, target)


def _ffn_bwd(dx2, o2, gate, up, x1, mod, g_norm2, w_down, w_gu, tm):
    s = x1.shape[0]
    chunk = D_FF // FFN_CHUNKS

    def body(dx_ref, o_ref, gate_ref, up_ref, x_ref, mod_ref, g_ref, wd_ref, wgu_ref,
             do_ref, dgu_ref, dx1_ref, small_ref):
        @pl.when(pl.program_id(0) == 0)
        def _():
            small_ref[...] = jnp.zeros_like(small_ref)

        dx = dx_ref[...].astype(F32)
        small_ref[3:4, :] += _colsum(dx * o_ref[...].astype(F32))
        do = (dx * mod_ref[G2:G2 + 1, :]).astype(BF16)
        do_ref[...] = do
        dh = None
        for j in range(FFN_CHUNKS):
            lo = j * chunk
            dact = _dot_nt(do, wd_ref[lo:lo + chunk, :])
            gate = gate_ref[:, lo:lo + chunk].astype(F32)
            sg = _sigmoid(gate)
            dgate = (dact * up_ref[:, lo:lo + chunk].astype(F32) * (sg * (1.0 + gate * (1.0 - sg)))).astype(BF16)
            dup = (dact * (gate * sg)).astype(BF16)
            dgu_ref[:, lo:lo + chunk] = dgate
            dgu_ref[:, D_FF + lo:D_FF + lo + chunk] = dup
            part = _dot(dgate, wgu_ref[lo:lo + chunk, :]) + _dot(dup, wgu_ref[D_FF + lo:D_FF + lo + chunk, :])
            dh = part if dh is None else dh + part
        dx1 = dx + _norm_mod_bwd(dh, x_ref[...], g_ref[...], mod_ref[SC2:SC2 + 1, :], small_ref)
        dx1_ref[...] = dx1.astype(BF16)

    return pl.pallas_call(
        body, name="ffn_bwd", grid=(s // tm,),
        in_specs=[_rows(tm, D_MODEL), _rows(tm, D_MODEL), _rows(tm, D_FF), _rows(tm, D_FF), _rows(tm, D_MODEL),
                  _full((8, D_MODEL)), _full((1, D_MODEL)), _resident((D_FF, D_MODEL)),
                  _resident((2 * D_FF, D_MODEL))],
        out_specs=[_rows(tm, D_MODEL), _rows(tm, 2 * D_FF), _rows(tm, D_MODEL), _full((8, D_MODEL))],
        out_shape=[jax.ShapeDtypeStruct((s, D_MODEL), BF16), jax.ShapeDtypeStruct((s, 2 * D_FF), BF16),
                   jax.ShapeDtypeStruct((s, D_MODEL), BF16), jax.ShapeDtypeStruct((8, D_MODEL), F32)],
        compiler_params=_params(("arbitrary",), VMEM_LIMIT_LARGE),
    )(dx2, o2, gate, up, x1, mod, g_norm2, w_down, w_gu)


def _norm_mod_bwd(dh, xf, g, scale_row, small_ref):
    r = _rsqrt_mean_sq(xf)
    xn = xf * r
    small_ref[0:1, :] += _colsum(dh)
    small_ref[1:2, :] += _colsum(dh * (xn * g))
    dn = dh * (1.0 + scale_row)
    small_ref[2:3, :] += _colsum(dn * xn)
    dxn = dn * g
    return r * (dxn - xn * jnp.mean(dxn * xn, axis=-1, keepdims=True))


def _out_proj_bwd(dx1, o1, mod, w_out, tm):
    s = dx1.shape[0]

    def body(dx_ref, o_ref, mod_ref, w_ref, do_ref, dm_ref, small_ref):
        @pl.when(pl.program_id(0) == 0)
        def _():
            small_ref[...] = jnp.zeros_like(small_ref)

        dx = dx_ref[...].astype(F32)
        small_ref[0:1, :] += _colsum(dx * o_ref[...].astype(F32))
        do = (dx * mod_ref[G1:G1 + 1, :]).astype(BF16)
        do_ref[...] = do
        dm_ref[...] = _dot_nt(do, w_ref[...]).astype(BF16)

    return pl.pallas_call(
        body, name="out_proj_bwd", grid=(s // tm,),
        in_specs=[_rows(tm, D_MODEL), _rows(tm, D_MODEL), _full((8, D_MODEL)), _full((D_MODEL, D_MODEL))],
        out_specs=[_rows(tm, D_MODEL), _rows(tm, D_MODEL), _full((8, D_MODEL))],
        out_shape=[jax.ShapeDtypeStruct((s, D_MODEL), BF16), jax.ShapeDtypeStruct((s, D_MODEL), BF16),
                   jax.ShapeDtypeStruct((8, D_MODEL), F32)],
        compiler_params=_params(("arbitrary",)),
    )(dx1, o1, mod, w_out)


def _group_norm_bwd(dm, a, g):
    r = _rsqrt_mean_sq(a)
    an = a * r
    dan = dm * g
    return r * (dan - an * jnp.mean(dan * an, axis=-1, keepdims=True)), _colsum(dm * an)


def _mixer_bwd(q, kv, gb, gc, xc, bias, sinks, conv_w, g_attn, g_conv, attn, lse, dmerged):
    s = q.shape[0]
    nb = s // BLOCK

    def body(sink_ref, q_ref, kv_ref, gb_ref, gc_ref, xc_ref, gcp_ref, xcp_ref, bias_ref, cw_ref, ga_ref, gcv_ref,
             attn_ref, lse_ref, dm_ref,
             dq_ref, dkv_ref, dgb_ref, dgc_ref, dxc_ref, dbias_ref, dsink_ref, small_ref, carry_ref):
        step = pl.program_id(0)
        n = nb - 1 - step

        @pl.when(step == 0)
        def _():
            dkv_ref[...] = jnp.zeros_like(dkv_ref)
            dbias_ref[...] = jnp.zeros_like(dbias_ref)
            dsink_ref[...] = jnp.zeros_like(dsink_ref)
            small_ref[...] = jnp.zeros_like(small_ref)
            carry_ref[...] = jnp.zeros_like(carry_ref)

        dm = dm_ref[...].astype(F32)
        gbv, gcv_, xcv = gb_ref[...].astype(F32), gc_ref[...].astype(F32), xc_ref[...].astype(F32)
        u, u1, u2 = _conv_taps(gcv_, xcv, gcp_ref[...], xcp_ref[...], n)
        cw = cw_ref[...]
        yv = cw[0:1, :] * u2 + cw[1:2, :] * u1 + cw[2:3, :] * u
        dcv, dg_conv = _group_norm_bwd(dm[:, 512:1024], gbv * yv, gcv_ref[...])
        small_ref[1:2, :] += dg_conv
        dgb_ref[...] = (dcv * yv).astype(BF16)
        dy = dcv * gbv
        nxt = carry_ref[...]
        row = lax.broadcasted_iota(jnp.int32, dy.shape, 0)
        d1 = jnp.where(row == BLOCK - 1, nxt[0:1, :], pltpu.roll(dy, BLOCK - 1, 0))
        d2 = jnp.where(row == BLOCK - 2, nxt[0:1, :],
                       jnp.where(row == BLOCK - 1, nxt[1:2, :], pltpu.roll(dy, BLOCK - 2, 0)))
        du = cw[2:3, :] * dy + cw[1:2, :] * d1 + cw[0:1, :] * d2
        dgc_ref[...] = (du * xcv).astype(BF16)
        dxc_ref[...] = (du * gcv_).astype(BF16)
        small_ref[2:3, :] += _colsum(dy * u2)
        small_ref[3:4, :] += _colsum(dy * u1)
        small_ref[4:5, :] += _colsum(dy * u)
        carry_ref[...] = dy[0:8, :]

        attn_v = attn_ref[...]
        dout, dg_attn = _group_norm_bwd(dm[:, 0:512], attn_v, ga_ref[...])
        small_ref[0:1, :] += dg_attn
        ks, vs = _load_kv_window(kv_ref, n)
        lane = lax.broadcasted_iota(jnp.int32, (BLOCK, BLOCK), 1)
        low = lane < HEAD_DIM
        col = lax.broadcasted_iota(jnp.int32, (BLOCK, 2 * BLOCK), 1)
        no_prev = (col < BLOCK) & (n == 0)
        lse_all = lse_ref[...]
        dk = jnp.zeros((2 * BLOCK, BLOCK), F32)
        dv = jnp.zeros((2 * BLOCK, BLOCK), F32)
        dsink = jnp.zeros((BLOCK, BLOCK), F32)
        dq_pairs = []
        for p in range(4):
            qp = q_ref[:, 128 * p:128 * (p + 1)].astype(F32)
            do_p = dout[:, 128 * p:128 * (p + 1)]
            prod = do_p * attn_v[:, 128 * p:128 * (p + 1)]
            kvh = p // 2
            res = []
            for e in range(2):
                h = 2 * p + e
                half = low if e == 0 else ~low
                qm = jnp.where(half, qp, 0.0).astype(BF16)
                dom = jnp.where(half, do_p, 0.0).astype(BF16)
                delta = jnp.sum(jnp.where(half, prod, 0.0), axis=-1, keepdims=True)
                lse_h = jnp.sum(jnp.where(lane == h, lse_all, 0.0), axis=-1, keepdims=True)
                sw = 0 if kvh == e else 1
                sc = _dot_nt(qm, ks[sw]) * SCALE + bias_ref[h]
                sc = jnp.where(no_prev, NEG_INF, sc)
                pr = jnp.exp(sc - lse_h)
                dp = _dot_nt(dom, vs[sw])
                ds = pr * (dp - delta)
                dbias_ref[h] += ds
                dsink = dsink + jnp.where(lane == h, -jnp.exp(sink_ref[h] - lse_h) * delta, 0.0)
                dsb = ds.astype(BF16)
                res.append(_dot(dsb, ks[sw]) * SCALE)
                dk_h = _dot_tn(dsb, qm) * SCALE
                dv_h = _dot_tn(pr.astype(BF16), dom)
                if sw:
                    dk_h = pltpu.roll(dk_h, 64, 1)
                    dv_h = pltpu.roll(dv_h, 64, 1)
                dk = dk + dk_h
                dv = dv + dv_h
            dq_pairs.append(jnp.where(low, res[0], res[1]))
        dq_ref[...] = jnp.concatenate(dq_pairs, axis=1).astype(BF16)
        dsink_ref[...] += dsink
        dkv_win = jnp.concatenate([dk, dv], axis=1)
        prev = jnp.maximum(n - 1, 0)
        dkv_ref[pl.ds(pl.multiple_of(prev * BLOCK, BLOCK), BLOCK), :] += dkv_win[0:BLOCK, :]
        dkv_ref[pl.ds(pl.multiple_of(n * BLOCK, BLOCK), BLOCK), :] += dkv_win[BLOCK:2 * BLOCK, :]

        @pl.when(step == nb - 1)
        def _():
            small_ref[5:6, :] = jnp.concatenate([_colsum(dsink_ref[...]), jnp.zeros((1, 512 - BLOCK), F32)], axis=1)

    blk = lambda w: pl.BlockSpec((BLOCK, w), lambda t: (nb - 1 - t, 0))
    prev8 = pl.BlockSpec((PREV_ROWS, 512),
                         lambda t: (jnp.maximum((nb - 1 - t) * (BLOCK // PREV_ROWS) - 1, 0), 0))
    bf = lambda w: jax.ShapeDtypeStruct((s, w), BF16)
    return pl.pallas_call(
        body, name="mixer_bwd", grid=(nb,),
        in_specs=[pl.BlockSpec(memory_space=pltpu.SMEM), blk(512), _full((s, 256)), blk(512), blk(512), blk(512),
                  prev8, prev8, _full((N_Q_HEADS, BLOCK, 2 * BLOCK)), _full((3, 512)), _full((1, 512)),
                  _full((1, 512)), blk(512), blk(128), blk(1024)],
        out_specs=[blk(512), _full((s, 256)), blk(512), blk(512), blk(512), _full((N_Q_HEADS, BLOCK, 2 * BLOCK)),
                   _full((BLOCK, BLOCK)), _full((8, 512))],
        out_shape=[bf(512), jax.ShapeDtypeStruct((s, 256), F32), bf(512), bf(512), bf(512),
                   jax.ShapeDtypeStruct((N_Q_HEADS, BLOCK, 2 * BLOCK), F32), jax.ShapeDtypeStruct((BLOCK, BLOCK), F32),
                   jax.ShapeDtypeStruct((8, 512), F32)],
        scratch_shapes=[pltpu.VMEM((8, 512), F32)],
        compiler_params=_params(("arbitrary",), VMEM_LIMIT_LARGE),
    )(sinks, q, kv, gb, gc, xc, gc, xc, bias, conv_w, g_attn, g_conv, attn, lse, dmerged)


def _in_proj_bwd(dq, dkv, dgb, dgc, dxc, x, dx1, mod, g_norm1, w_in, tm):
    s = x.shape[0]

    def body(dq_ref, dkv_ref, dgb_ref, dgc_ref, dxc_ref, x_ref, dx1_ref, mod_ref, g_ref, w_ref,
             dproj_ref, dx_ref, small_ref):
        @pl.when(pl.program_id(0) == 0)
        def _():
            small_ref[...] = jnp.zeros_like(small_ref)

        dproj = jnp.concatenate([dq_ref[...], dkv_ref[...].astype(BF16), dgb_ref[...], dgc_ref[...], dxc_ref[...]],
                                axis=1)
        dproj_ref[...] = dproj
        dh = _dot(dproj, w_ref[...])
        dx_ref[...] = dx1_ref[...].astype(F32) + _norm_mod_bwd(dh, x_ref[...], g_ref[...], mod_ref[SC1:SC1 + 1, :], small_ref)

    return pl.pallas_call(
        body, name="in_proj_bwd", grid=(s // tm,),
        in_specs=[_rows(tm, 512), _rows(tm, 256), _rows(tm, 512), _rows(tm, 512), _rows(tm, 512),
                  _rows(tm, D_MODEL), _rows(tm, D_MODEL), _full((8, D_MODEL)), _full((1, D_MODEL)),
                  _full((IN_PROJ_WIDTH, D_MODEL))],
        out_specs=[_rows(tm, IN_PROJ_WIDTH), _rows(tm, D_MODEL), _full((8, D_MODEL))],
        out_shape=[jax.ShapeDtypeStruct((s, IN_PROJ_WIDTH), BF16), jax.ShapeDtypeStruct((s, D_MODEL), F32),
                   jax.ShapeDtypeStruct((8, D_MODEL), F32)],
        compiler_params=_params(("arbitrary",), VMEM_LIMIT_LARGE),
    )(dq, dkv, dgb, dgc, dxc, x, dx1, mod, g_norm1, w_in)


def _weight_grad(a, b, tk, ts, name, after=None):
    s, k = a.shape
    n = b.shape[1]
    nt = s // ts
    extra = [] if after is None else [after]

    def body(a_ref, b_ref, *rest):
        o_ref, acc_ref = rest[-2:]
        t = pl.program_id(1)
        part = _dot_tn(a_ref[...], b_ref[...])

        @pl.when(t == 0)
        def _():
            acc_ref[...] = part

        @pl.when(t > 0)
        def _():
            acc_ref[...] += part

        @pl.when(t == nt - 1)
        def _():
            o_ref[...] = acc_ref[...].astype(BF16)

    return pl.pallas_call(
        body, name=name, grid=(k // tk, nt),
        in_specs=[pl.BlockSpec((ts, tk), lambda i, t: (t, i)), pl.BlockSpec((ts, n), lambda i, t: (t, 0))]
        + [pl.BlockSpec(memory_space=pl.ANY)] * len(extra),
        out_specs=pl.BlockSpec((tk, n), lambda i, t: (i, 0)),
        out_shape=jax.ShapeDtypeStruct((k, n), BF16),
        scratch_shapes=[pltpu.VMEM((tk, n), F32)],
        compiler_params=_params(("arbitrary", "arbitrary"), VMEM_LIMIT_LARGE),
    )(a, b, *extra)


def _rel_bias_grad(dbias, bucket):
    def body(db_ref, bk_ref, o_ref, rows_ref):
        bk = bk_ref[...]
        for b in range(N_BUCKETS):
            sel = (bk == b).astype(F32)
            for h in range(N_Q_HEADS):
                rows_ref[8 * b + h:8 * b + h + 1, :] = _colsum(db_ref[h] * sel)
        o_ref[...] = jnp.sum(rows_ref[...], axis=-1, keepdims=True)

    return pl.pallas_call(
        body, name="rel_bias_grad",
        out_shape=jax.ShapeDtypeStruct((N_BUCKETS * N_Q_HEADS, 1), F32),
        scratch_shapes=[pltpu.VMEM((N_BUCKETS * N_Q_HEADS, 2 * BLOCK), F32)],
    )(dbias, bucket)


def _sum_slots(parts, after):
    def body(p_ref, after_ref, o_ref):
        acc = p_ref[0]
        for k in range(1, N_DEV):
            acc = acc + p_ref[k]
        o_ref[...] = acc

    return pl.pallas_call(body, name="sum_small_grads",
                          in_specs=[pl.BlockSpec(memory_space=pltpu.VMEM), pl.BlockSpec(memory_space=pl.ANY)],
                          out_shape=jax.ShapeDtypeStruct(parts.shape[1:], F32))(parts, after)


def _w_ada_grad(cond_t, dmod_cols):
    def body(c_ref, d_ref, o_ref):
        o_ref[...] = _dot(c_ref[...], d_ref[...])

    return pl.pallas_call(body, name="w_ada_grad",
                          out_shape=jax.ShapeDtypeStruct((cond_t.shape[0], dmod_cols.shape[1]), F32))(cond_t, dmod_cols)


def _adam_math(w, g, m, v):
    m = ADAM_B1 * m + (1.0 - ADAM_B1) * g
    v = ADAM_B2 * v + (1.0 - ADAM_B2) * (g * g)
    m_hat = m / (1.0 - ADAM_B1 ** ADAM_STEP)
    v_hat = v / (1.0 - ADAM_B2 ** ADAM_STEP)
    delta = -ADAM_LR * (m_hat / (jnp.sqrt(v_hat) + ADAM_EPS) + ADAM_WD * w)
    return delta, m, v


def _adamw_parts(w, m, v, local, land, me, tr, name):
    r, c = w.shape

    def body(me_ref, w_ref, m_ref, v_ref, own_ref, land_ref, g_ref, d_ref, mo_ref, vo_ref):
        g = own_ref[0].astype(F32)
        for k in range(N_DEV - 1):
            g = g + land_ref[k].astype(F32)
        g_ref[...] = g
        d_ref[...], mo_ref[...], vo_ref[...] = _adam_math(w_ref[...], g, m_ref[...], v_ref[...])

    tile = pl.BlockSpec((tr, c), lambda i, me_ref: (i, 0))
    return pl.pallas_call(
        body, name=name,
        grid_spec=pltpu.PrefetchScalarGridSpec(
            num_scalar_prefetch=1, grid=(r // tr,),
            in_specs=[tile, tile, tile, pl.BlockSpec((1, tr, c), lambda i, me_ref: (me_ref[0], i, 0)),
                      pl.BlockSpec((N_DEV - 1, tr, c), lambda i, me_ref: (0, i, 0))],
            out_specs=[tile] * 4),
        out_shape=[jax.ShapeDtypeStruct((r, c), F32)] * 4,
        compiler_params=_params(("arbitrary",)),
    )(me, w, m, v, local, land)


def _adamw(w, m, v, g, tr, name):
    r, c = w.shape

    def body(w_ref, m_ref, v_ref, g_ref, d_ref, mo_ref, vo_ref):
        d_ref[...], mo_ref[...], vo_ref[...] = _adam_math(w_ref[...], g_ref[...], m_ref[...], v_ref[...])

    tile = pl.BlockSpec((tr, c), lambda i: (i, 0))
    return pl.pallas_call(
        body, name=name, grid=(r // tr,),
        in_specs=[tile] * 4, out_specs=[tile] * 3,
        out_shape=[jax.ShapeDtypeStruct((r, c), F32)] * 3,
        compiler_params=_params(("arbitrary",)),
    )(w, m, v, g)


def _behind(a, token):
    return a + token[0:a.shape[0], 0:1]


def _local_step(x, target, mod, w_in_t, weights_out_gu, weights_down, rel_bias, g_norm1, sinks, conv_w, g_attn,
                g_conv, g_norm2, g_final, exchange):
    s = x.shape[0]
    tm = min(512, s)
    tm_small = min(256, s)
    bucket = _bucket_table()
    bias = _bias_table(rel_bias, bucket)

    h, q, kv, gb, gc, xc = _in_proj(x, mod, g_norm1, w_in_t, tm)
    attn, merged, lse = _mixer_fwd(q, kv, gb, gc, xc, bias, sinks, conv_w, g_attn, g_conv)
    w_out, w_gu_t = weights_out_gu(merged)
    o1, x1 = _out_proj(merged, x, mod, w_out, tm)
    w_down = weights_down(x1)
    h2, gate, up, act, o2, dx2, fin = _ffn_fwd(x1, mod, g_norm2, w_gu_t, w_down, g_final, target, tm_small)

    do2, dgu, dx1, sm_2 = _ffn_bwd(dx2, o2, gate, up, x1, mod, g_norm2, w_down, w_gu_t, tm_small)
    ts = min(WEIGHT_GRAD_ROWS, s)
    tok_down = exchange("w_down", _weight_grad(act, do2, D_FF // 2, ts, "w_down_grad"))
    mod = _behind(mod, exchange("w_gu", _weight_grad(dgu, h2, D_FF // 2, ts, "w_gu_grad", after=tok_down)))
    do1, dmerged, sm_g1 = _out_proj_bwd(dx1, o1, mod, w_out, tm)
    g_attn_b = _behind(g_attn, exchange("w_out", _weight_grad(merged, do1, D_MODEL, ts, "w_out_grad")))
    dq, dkv, dgb, dgc, dxc, dbias, dsink, sm_mix = _mixer_bwd(
        q, kv, gb, gc, xc, bias, sinks, conv_w, g_attn_b, g_conv, attn, lse, dmerged)
    dproj, dx, sm_1 = _in_proj_bwd(dq, dkv, dgb, dgc, dxc, x, dx1, mod, g_norm1, w_in_t, tm)
    d_rel = _rel_bias_grad(dbias, bucket)

    packed = jnp.concatenate([
        sm_1[0], sm_1[1], sm_g1[0], sm_2[0], sm_2[1], sm_2[3],
        d_rel[:, 0],
        sm_1[2],
        sm_mix[5, 0:128],
        sm_mix[0], sm_mix[1],
        sm_2[2],
        fin[0],
        sm_mix[2], sm_mix[3], sm_mix[4],
        fin[2, 0:128],
    ])[None, :]
    return dx, dproj, h, packed


def kernel(x, c, rel_bias, w_ada, b_ada, g_norm1, w_in, sinks, conv_w, g_attn_out, g_conv_out, w_out, g_norm2, w_gu, w_down, g_final, loss_target, m_rel_bias, m_w_ada, m_b_ada, m_g_norm1, m_w_in, m_sinks, m_conv_w, m_g_attn_out, m_g_conv_out, m_w_out, m_g_norm2, m_w_gu, m_w_down, m_g_final, v_rel_bias, v_w_ada, v_b_ada, v_g_norm1, v_w_in, v_sinks, v_conv_w, v_g_attn_out, v_g_conv_out, v_w_out, v_g_norm2, v_w_gu, v_w_down, v_g_final):
    me = _linear(_mesh_position())
    me_arr = jnp.reshape(me, (1,)).astype(jnp.int32)
    ada_cols = w_ada.shape[2]
    tm = min(512, x.shape[1])

    cond = _silu_rows(c)
    cond_all, conv_w_all = _all_gather_small([cond, conv_w[0]], "gather_cond")
    cond_all = cond_all[:, 0, :]
    conv_cols = conv_w.shape[2]
    conv_w_full = conv_w_all.transpose(1, 0, 2).reshape(3, CONV_WIDTH)
    b_cols = lax.dynamic_slice_in_dim(b_ada, me * ada_cols, ada_cols, axis=1)
    mod_cols = _mod_columns(cond_all, w_ada[0], b_cols)
    mod_all = _all_gather_small([mod_cols], "gather_mod")[0]
    mod = lax.dynamic_index_in_dim(mod_all, me, axis=1, keepdims=False).reshape(N_MOD, D_MODEL)
    mod = jnp.concatenate([mod, jnp.zeros((2, D_MODEL), F32)], axis=0)

    w_in_t = _all_gather([w_in[0].T], "gather_w_in", to_bf16=True, big=True)[0].reshape(IN_PROJ_WIDTH, D_MODEL)
    gather_sems, staged, gather_token = _gather_start(
        _stage_blocks([w_out[0], w_gu[0].T, w_down[0]], w_in_t, "stage_weights"), "gather_start_weights")
    mod = _behind(mod, gather_token)

    def weights_out_gu(after):
        got = _gather_pass_on(_gather_wait(gather_sems[0:4], staged[0:2], [after], "gather_wait_out_gu"),
                              "gather_pass_on_out_gu")
        return got[0].reshape(D_MODEL, D_MODEL), got[1].reshape(2 * D_FF, D_MODEL)

    def weights_down(after):
        got = _gather_pass_on(_gather_wait(gather_sems[4:6], staged[2:3], [after], "gather_wait_down"),
                              "gather_pass_on_down")
        return got[0].reshape(D_FF, D_MODEL)

    started = {}

    def exchange(name, dw):
        st = _exchange_start(dw.reshape(N_DEV, dw.shape[0] // N_DEV, dw.shape[1]), "exchange_start_" + name)
        started[name] = st
        return st[4]

    dx, dproj, h, packed = _local_step(
        x[0], loss_target[0], mod, w_in_t, weights_out_gu, weights_down, rel_bias, g_norm1, sinks[0], conv_w_full,
        g_attn_out, g_conv_out, g_norm2, g_final[None, :], exchange)

    packed_all = _all_gather_small([packed], "gather_small_grads")[0]
    tok_in = exchange("w_in", _weight_grad(dproj, h, IN_PROJ_WIDTH // 2, min(WEIGHT_GRAD_ROWS, x.shape[1]),
                                           "w_in_grad", after=packed_all))
    small = _sum_slots(packed_all, tok_in)[0]
    dmod_all = packed_all[:, 0, OFF_DMOD:OFF_DMOD + N_MOD * D_MODEL]
    dmod_cols = lax.dynamic_slice_in_dim(dmod_all, me * ada_cols, ada_cols, axis=1)
    cond_t = jnp.zeros((D_MODEL, 128), F32).at[:, 0:N_DEV].set(cond_all.T)
    dmod_pad = jnp.zeros((128, ada_cols), F32).at[0:N_DEV, :].set(dmod_cols)
    g_ada = _w_ada_grad(cond_t, dmod_pad)
    d_ada, nm_ada, nv_ada = _adamw(w_ada[0], m_w_ada[0], v_w_ada[0], g_ada, 256, "adamw_w_ada")

    loss = small[OFF_LOSS]
    seg = lambda off, n: small[off:off + n]
    conv_g_full = seg(OFF_CONVW, 3 * CONV_WIDTH).reshape(3, CONV_WIDTH)
    small_grads = {
        "rel_bias": seg(OFF_RELB, 256).reshape(N_BUCKETS, N_Q_HEADS),
        "b_ada": seg(OFF_DMOD, N_MOD * D_MODEL).reshape(1, N_MOD * D_MODEL),
        "g_norm1": seg(OFF_GN1, D_MODEL).reshape(1, D_MODEL),
        "sinks": seg(OFF_SINK, N_Q_HEADS).reshape(1, N_Q_HEADS),
        "conv_w": lax.dynamic_slice_in_dim(conv_g_full, me * conv_cols, conv_cols, axis=1)[None],
        "g_attn_out": seg(OFF_GATT, ATTN_WIDTH).reshape(1, ATTN_WIDTH),
        "g_conv_out": seg(OFF_GCV, CONV_WIDTH).reshape(1, CONV_WIDTH),
        "g_norm2": seg(OFF_GN2, D_MODEL).reshape(1, D_MODEL),
        "g_final": seg(OFF_GFIN, D_MODEL),
    }
    small_state = {
        "rel_bias": (rel_bias, m_rel_bias, v_rel_bias), "b_ada": (b_ada, m_b_ada, v_b_ada),
        "g_norm1": (g_norm1, m_g_norm1, v_g_norm1), "sinks": (sinks, m_sinks, v_sinks),
        "conv_w": (conv_w, m_conv_w, v_conv_w), "g_attn_out": (g_attn_out, m_g_attn_out, v_g_attn_out),
        "g_conv_out": (g_conv_out, m_g_conv_out, v_g_conv_out), "g_norm2": (g_norm2, m_g_norm2, v_g_norm2),
        "g_final": (g_final, m_g_final, v_g_final),
    }
    names = list(small_grads)
    sizes = [small_grads[k].size for k in names]
    total = sum(sizes)
    padded = -(-total // 1024) * 1024

    def pack(arrs):
        flat = jnp.concatenate([a.reshape(-1) for a in arrs] + [jnp.ones((padded - total,), F32)])
        return flat.reshape(padded // 128, 128)

    sw = pack([small_state[k][0] for k in names])
    sm = pack([small_state[k][1] for k in names])
    sv = pack([small_state[k][2] for k in names])
    sg = pack([small_grads[k] for k in names])
    sd, snm, snv = _adamw(sw, sm, sv, sg, padded // 128, "adamw_small")

    def unpack(flat2d):
        flat = flat2d.reshape(-1)
        out, off = {}, 0
        for k, n in zip(names, sizes):
            out[k] = flat[off:off + n].reshape(small_grads[k].shape)
            off += n
        return out

    sd_all = sd
    sd, snm, snv = unpack(sd), unpack(snm), unpack(snv)

    def finish(name, after, w, m, v, tr):
        src, land = _exchange_wait(started[name], after, "exchange_wait_" + name)
        return _adamw_parts(w, m, v, src, land, me_arr, tr, "adamw_" + name)

    g_down, d_down, nm_down, nv_down = finish("w_down", [sd_all], w_down[0], m_w_down[0], v_w_down[0], 176)
    g_gu, d_gu, nm_gu, nv_gu = finish("w_gu", [nv_down], w_gu[0].T, m_w_gu[0].T, v_w_gu[0].T, 352)
    g_out, d_out, nm_out, nv_out = finish("w_out", [nv_gu], w_out[0], m_w_out[0], v_w_out[0], 128)
    g_in, d_in, nm_in, nv_in = finish("w_in", [nv_out, nv_ada], w_in[0].T, m_w_in[0].T, v_w_in[0].T, 144)

    big = {
        "w_ada": (g_ada[None], d_ada[None], nm_ada[None], nv_ada[None]),
        "w_in": (g_in.T[None], d_in.T[None], nm_in.T[None], nv_in.T[None]),
        "w_out": (g_out[None], d_out[None], nm_out[None], nv_out[None]),
        "w_gu": (g_gu.T[None], d_gu.T[None], nm_gu.T[None], nv_gu.T[None]),
        "w_down": (g_down[None], d_down[None], nm_down[None], nv_down[None]),
    }
    order = ["rel_bias", "w_ada", "b_ada", "g_norm1", "w_in", "sinks", "conv_w", "g_attn_out", "g_conv_out", "w_out",
             "g_norm2", "w_gu", "w_down", "g_final"]
    grads = [big[k][0] if k in big else small_grads[k] for k in order]
    deltas = [big[k][1] if k in big else sd[k] for k in order]
    new_m = [big[k][2] if k in big else snm[k] for k in order]
    new_v = [big[k][3] if k in big else snv[k] for k in order]
    return (loss, dx[None], *grads, *deltas, *new_m, *new_v)
```

```python
import functools
import math

import jax
import jax.numpy as jnp
from jax import lax
from jax.experimental import pallas as pl
from jax.experimental.pallas import tpu as pltpu

F32 = jnp.float32
BF16 = jnp.bfloat16

D_MODEL = 1024
HEAD_DIM = 64
N_Q_HEADS = 8
ATTN_WIDTH = 512
KV_WIDTH = 128
CONV_WIDTH = 512
IN_PROJ_WIDTH = 2304
D_FF = 2816
N_MOD = 6
N_BUCKETS = 32
MAX_DISTANCE = 128
BLOCK = 128
EPS = 1e-6
NEG_INF = -1e30
SCALE = HEAD_DIM ** -0.5
N_DEV = 8

ADAM_LR = 0.001
ADAM_B1 = 0.9
ADAM_B2 = 0.999
ADAM_EPS = 1e-08
ADAM_WD = 0.01
ADAM_STEP = 10

SH1, SC1, G1, SH2, SC2, G2 = range(6)

VMEM_LIMIT_LARGE = 56 * 1024 * 1024
WEIGHT_GRAD_ROWS = 2048
FFN_CHUNKS = 2
PREV_ROWS = 16
MIXER_BLOCKS = 4
MESH_ID = pl.DeviceIdType.MESH

OFF_DMOD = 0
OFF_RELB = OFF_DMOD + N_MOD * D_MODEL
OFF_GN1 = OFF_RELB + N_BUCKETS * N_Q_HEADS
OFF_SINK = OFF_GN1 + D_MODEL
OFF_GATT = OFF_SINK + 128
OFF_GCV = OFF_GATT + ATTN_WIDTH
OFF_GN2 = OFF_GCV + CONV_WIDTH
OFF_GFIN = OFF_GN2 + D_MODEL
OFF_CONVW = OFF_GFIN + D_MODEL
OFF_LOSS = OFF_CONVW + 3 * CONV_WIDTH
PACKED = OFF_LOSS + 128


def _params(sem=None, vmem=None):
    return pltpu.CompilerParams(dimension_semantics=sem, vmem_limit_bytes=vmem)


def _full(shape):
    nd = len(shape)
    return pl.BlockSpec(shape, lambda *_: (0,) * nd)


def _rows(tm, width):
    return pl.BlockSpec((tm, width), lambda i, *_: (i, 0))


def _sigmoid(x):
    return 1.0 / (1.0 + jnp.exp(-x))


def _rsqrt_mean_sq(x):
    return lax.rsqrt(jnp.mean(x * x, axis=-1, keepdims=True) + EPS)


def _colsum(x):
    return jnp.sum(x, axis=0, keepdims=True)


def _dot(a, b):
    return jnp.dot(a, b, preferred_element_type=F32)


def _dot_nt(a, b):
    return lax.dot_general(a, b, (((1,), (1,)), ((), ())), preferred_element_type=F32)


def _dot_tn(a, b):
    return lax.dot_general(a, b, (((0,), (0,)), ((), ())), preferred_element_type=F32)


def _mesh_position():
    return lax.axis_index("x"), lax.axis_index("y"), lax.axis_index("c")


def _linear(p):
    return 4 * p[0] + 2 * p[1] + p[2]


def _all_gather(arrs, name, to_bf16, big):
    n = len(arrs)
    out_dtype = BF16 if to_bf16 else F32

    def body(*refs):
        in_refs, out_refs = refs[:n], refs[n:2 * n]
        rest = refs[2 * n:]
        if to_bf16:
            stage, rest = rest[:n], rest[n:]
            for a in range(n):
                stage[a][...] = in_refs[a][...].astype(BF16)
            srcs = stage
        else:
            srcs = in_refs
        send_sems, recv_sems, local_sems = rest
        x, y, c = _mesh_position()
        me, sibling = (x, y, c), (x, y, 1 - c)
        chips = [(1 - x, y), (x, 1 - y), (1 - x, 1 - y)]

        def slot(a, p):
            return out_refs[a].at[_linear(p)]

        def copy(k, a, block, to, src=None):
            return pltpu.make_async_remote_copy(
                src_ref=slot(a, block) if src is None else src,
                dst_ref=slot(a, block),
                send_sem=send_sems.at[k * n + a],
                recv_sem=recv_sems.at[k * n + a],
                device_id=to,
                device_id_type=MESH_ID,
            )

        mine = [pltpu.make_async_copy(srcs[a], slot(a, me), local_sems.at[a]) for a in range(n)]
        for cp in mine:
            cp.start()
        first = [copy(0, a, me, sibling, src=srcs[a]) for a in range(n)]
        for j, chip in enumerate(chips):
            first += [copy(1 + j, a, me, (*chip, c), src=srcs[a]) for a in range(n)]
        for cp in first:
            cp.start()
        passed = []
        for j, chip in enumerate(chips):
            for a in range(n):
                copy(1 + j, a, (*chip, c), me).wait_recv()
                fwd = copy(4 + j, a, (*chip, c), sibling)
                fwd.start()
                passed.append(fwd)
        for a in range(n):
            copy(0, a, sibling, me).wait_recv()
        for j, chip in enumerate(chips):
            for a in range(n):
                copy(4 + j, a, (*chip, 1 - c), me).wait_recv()
        for cp in first + passed:
            cp.wait_send()
        for cp in mine:
            cp.wait()

    vmem = pl.BlockSpec(memory_space=pltpu.VMEM)
    out_space = pl.BlockSpec(memory_space=pl.ANY) if big else vmem
    scratch = [pltpu.VMEM(a.shape, BF16) for a in arrs] if to_bf16 else []
    scratch += [pltpu.SemaphoreType.DMA((7 * n,)), pltpu.SemaphoreType.DMA((7 * n,)),
                pltpu.SemaphoreType.DMA((n,))]
    outs = pl.pallas_call(
        body, name=name,
        out_shape=[jax.ShapeDtypeStruct((N_DEV,) + a.shape, out_dtype) for a in arrs],
        in_specs=[vmem] * n, out_specs=[out_space] * n,
        scratch_shapes=scratch,
        compiler_params=_params(vmem=VMEM_LIMIT_LARGE if big else None),
    )(*arrs)
    return list(outs)


def _peer(k):
    x, y, c = _mesh_position()
    return (1 - x if k & 4 else x, 1 - y if k & 2 else y, 1 - c if k & 1 else c)


def _all_gather_small(arrs, name):
    n = len(arrs)

    def body(*refs):
        in_refs, out_refs = refs[:n], refs[n:2 * n]
        send_sems, recv_sems, local_sems = refs[2 * n:]
        me = _linear(_mesh_position())
        mine = [pltpu.make_async_copy(in_refs[a], out_refs[a].at[me], local_sems.at[a]) for a in range(n)]
        for cp in mine:
            cp.start()
        sends = []
        for k in range(1, N_DEV):
            for a in range(n):
                sends.append(pltpu.make_async_remote_copy(
                    src_ref=in_refs[a], dst_ref=out_refs[a].at[me],
                    send_sem=send_sems.at[(k - 1) * n + a], recv_sem=recv_sems.at[(k - 1) * n + a],
                    device_id=_peer(k), device_id_type=MESH_ID))
                sends[-1].start()
        for k in range(1, N_DEV):
            for a in range(n):
                pltpu.make_async_remote_copy(
                    src_ref=in_refs[a], dst_ref=out_refs[a].at[_linear(_peer(k))],
                    send_sem=send_sems.at[(k - 1) * n + a], recv_sem=recv_sems.at[(k - 1) * n + a],
                    device_id=_peer(k), device_id_type=MESH_ID).wait_recv()
        for cp in sends:
            cp.wait_send()
        for cp in mine:
            cp.wait()

    vmem = pl.BlockSpec(memory_space=pltpu.VMEM)
    return list(pl.pallas_call(
        body, name=name,
        out_shape=[jax.ShapeDtypeStruct((N_DEV,) + a.shape, F32) for a in arrs],
        in_specs=[vmem] * n, out_specs=[vmem] * n,
        scratch_shapes=[pltpu.SemaphoreType.DMA((7 * n,)), pltpu.SemaphoreType.DMA((7 * n,)),
                        pltpu.SemaphoreType.DMA((n,))],
    )(*arrs))


HBM_SPEC = pl.BlockSpec(memory_space=pltpu.HBM)
SEM_SPEC = pl.BlockSpec(memory_space=pltpu.SEMAPHORE)
DATAFLOW = pltpu.SideEffectType.DATAFLOW_SIDE_EFFECTING


def _exchange_start(src, name):
    r, c = src.shape[1:]

    def body(src_ref, land_ref, send_sems, recv_sems, src_thru, land_thru, token):
        for k in range(1, N_DEV):
            peer = _peer(k)
            pltpu.make_async_remote_copy(
                src_ref=src_ref.at[_linear(peer)], dst_ref=land_ref.at[k - 1],
                send_sem=send_sems.at[k - 1], recv_sem=recv_sems.at[k - 1],
                device_id=peer, device_id_type=MESH_ID).start()
        token[...] = jnp.zeros_like(token)

    land = lax.empty((N_DEV - 1, r, c), src.dtype)
    return pl.pallas_call(
        body, name=name,
        out_shape=(pltpu.SemaphoreType.DMA((N_DEV - 1,)), pltpu.SemaphoreType.DMA((N_DEV - 1,)),
                   pltpu.HBM(src.shape, src.dtype), pltpu.HBM(land.shape, land.dtype),
                   jax.ShapeDtypeStruct((8, 128), F32)),
        in_specs=(HBM_SPEC, HBM_SPEC),
        out_specs=(SEM_SPEC, SEM_SPEC, HBM_SPEC, HBM_SPEC, pl.BlockSpec(memory_space=pltpu.VMEM)),
        input_output_aliases={0: 2, 1: 3},
        compiler_params=pltpu.CompilerParams(has_side_effects=DATAFLOW),
    )(pltpu.with_memory_space_constraint(src, pltpu.HBM), pltpu.with_memory_space_constraint(land, pltpu.HBM))


def _exchange_wait(started, after, name):
    send_sems, recv_sems, src_thru, land_thru, _ = started

    def body(src_ref, land_ref, send_sems, recv_sems, *rest):
        for k in range(1, N_DEV):
            cp = pltpu.make_async_remote_copy(
                src_ref=src_ref.at[0], dst_ref=land_ref.at[k - 1],
                send_sem=send_sems.at[k - 1], recv_sem=recv_sems.at[k - 1],
                device_id=_peer(k), device_id_type=MESH_ID)
            cp.wait_send()
            cp.wait_recv()

    return pl.pallas_call(
        body, name=name,
        out_shape=(pltpu.HBM(src_thru.shape, src_thru.dtype), pltpu.HBM(land_thru.shape, land_thru.dtype)),
        in_specs=(HBM_SPEC, HBM_SPEC, SEM_SPEC, SEM_SPEC) + (pl.BlockSpec(memory_space=pl.ANY),) * len(after),
        out_specs=(HBM_SPEC, HBM_SPEC), input_output_aliases={0: 0, 1: 1},
        compiler_params=pltpu.CompilerParams(has_side_effects=DATAFLOW),
    )(src_thru, land_thru, send_sems, recv_sems, *after)


def _stage_blocks(arrs, after, name):
    n = len(arrs)

    def body(*refs):
        in_refs, out_refs, stage, sems = refs[:n], refs[n + 1:2 * n + 1], refs[2 * n + 1:3 * n + 1], refs[3 * n + 1]
        me = _linear(_mesh_position())
        copies = []
        for a in range(n):
            stage[a][...] = in_refs[a][...].astype(BF16)
            copies.append(pltpu.make_async_copy(stage[a], out_refs[a].at[me], sems.at[a]))
            copies[-1].start()
        for cp in copies:
            cp.wait()

    return list(pl.pallas_call(
        body, name=name,
        out_shape=[jax.ShapeDtypeStruct((N_DEV,) + a.shape, BF16) for a in arrs],
        in_specs=[pl.BlockSpec(memory_space=pltpu.VMEM)] * n + [pl.BlockSpec(memory_space=pl.ANY)],
        out_specs=[pl.BlockSpec(memory_space=pl.ANY)] * n,
        scratch_shapes=[pltpu.VMEM(a.shape, BF16) for a in arrs] + [pltpu.SemaphoreType.DMA((n,))],
        compiler_params=_params(vmem=VMEM_LIMIT_LARGE),
    )(*arrs, after))


def _same_core_peers():
    x, y, c = _mesh_position()
    return [(x, y, 1 - c), (1 - x, y, c), (x, 1 - y, c), (1 - x, 1 - y, c)]


def _gather_start(bufs, name):
    n = len(bufs)

    def body(*refs):
        buf_refs, rest = refs[:n], refs[n:]
        sems, token = rest[:2 * n], rest[-1]
        me = _linear(_mesh_position())
        for a in range(n):
            for k, peer in enumerate(_same_core_peers()):
                pltpu.make_async_remote_copy(
                    src_ref=buf_refs[a].at[me], dst_ref=buf_refs[a].at[me],
                    send_sem=sems[2 * a].at[k], recv_sem=sems[2 * a + 1].at[k],
                    device_id=peer, device_id_type=MESH_ID).start()
        token[...] = jnp.zeros_like(token)

    outs = pl.pallas_call(
        body, name=name,
        out_shape=tuple(pltpu.SemaphoreType.DMA((4,)) for _ in range(2 * n))
        + tuple(pltpu.HBM(b.shape, b.dtype) for b in bufs) + (jax.ShapeDtypeStruct((8, 128), F32),),
        in_specs=(HBM_SPEC,) * n,
        out_specs=(SEM_SPEC,) * (2 * n) + (HBM_SPEC,) * n + (pl.BlockSpec(memory_space=pltpu.VMEM),),
        input_output_aliases={a: 2 * n + a for a in range(n)},
        compiler_params=pltpu.CompilerParams(has_side_effects=DATAFLOW),
    )(*[pltpu.with_memory_space_constraint(b, pltpu.HBM) for b in bufs])
    return outs[:2 * n], outs[2 * n:3 * n], outs[3 * n]


def _gather_wait(sems, bufs, after, name):
    n = len(bufs)

    def body(*refs):
        buf_refs, sem_refs = refs[:n], refs[n:3 * n]
        x, y, c = _mesh_position()
        me = _linear((x, y, c))
        for a in range(n):
            for k, peer in enumerate(_same_core_peers()):
                cp = pltpu.make_async_remote_copy(
                    src_ref=buf_refs[a].at[me], dst_ref=buf_refs[a].at[_linear(peer)],
                    send_sem=sem_refs[2 * a].at[k], recv_sem=sem_refs[2 * a + 1].at[k],
                    device_id=peer, device_id_type=MESH_ID)
                cp.wait_send()
                cp.wait_recv()

    return list(pl.pallas_call(
        body, name=name,
        out_shape=tuple(pltpu.HBM(b.shape, b.dtype) for b in bufs),
        in_specs=(HBM_SPEC,) * n + (SEM_SPEC,) * (2 * n) + (pl.BlockSpec(memory_space=pl.ANY),) * len(after),
        out_specs=(HBM_SPEC,) * n, input_output_aliases={a: a for a in range(n)},
        compiler_params=pltpu.CompilerParams(has_side_effects=DATAFLOW),
    )(*bufs, *sems, *after))


def _gather_pass_on(bufs, name):
    n = len(bufs)

    def body(*refs):
        out_refs = refs[n:2 * n]
        send_sems, recv_sems = refs[2 * n:]
        x, y, c = _mesh_position()
        sibling = (x, y, 1 - c)
        chips = [(1 - x, y), (x, 1 - y), (1 - x, 1 - y)]
        copies = []
        for a in range(n):
            for j, chip in enumerate(chips):
                block = out_refs[a].at[_linear((*chip, c))]
                copies.append(pltpu.make_async_remote_copy(
                    src_ref=block, dst_ref=block, send_sem=send_sems.at[3 * a + j], recv_sem=recv_sems.at[3 * a + j],
                    device_id=sibling, device_id_type=MESH_ID))
                copies[-1].start()
        for a in range(n):
            for j, chip in enumerate(chips):
                copies[3 * a + j].wait_send()
                theirs = out_refs[a].at[_linear((*chip, 1 - c))]
                pltpu.make_async_remote_copy(
                    src_ref=theirs, dst_ref=theirs, send_sem=send_sems.at[3 * a + j], recv_sem=recv_sems.at[3 * a + j],
                    device_id=sibling, device_id_type=MESH_ID).wait_recv()

    hbm = pl.BlockSpec(memory_space=pl.ANY)
    return list(pl.pallas_call(
        body, name=name,
        out_shape=[jax.ShapeDtypeStruct(b.shape, b.dtype) for b in bufs],
        in_specs=[hbm] * n, out_specs=[hbm] * n, input_output_aliases={a: a for a in range(n)},
        scratch_shapes=[pltpu.SemaphoreType.DMA((3 * n,)), pltpu.SemaphoreType.DMA((3 * n,))],
    )(*bufs))


def _silu_rows(c):
    def body(c_ref, o_ref):
        v = c_ref[...]
        o_ref[...] = v * _sigmoid(v)

    return pl.pallas_call(body, name="cond_silu", out_shape=jax.ShapeDtypeStruct(c.shape, F32))(c)


def _mod_columns(cond_all, w_ada, b_cols):
    def body(c_ref, w_ref, b_ref, o_ref):
        o_ref[...] = _dot(c_ref[...], w_ref[...]) + b_ref[...]

    return pl.pallas_call(body, name="mod_columns",
                          out_shape=jax.ShapeDtypeStruct((N_DEV, w_ada.shape[1]), F32))(cond_all, w_ada, b_cols)


def _in_proj(x, mod, g_norm1, w_in, tm):
    s = x.shape[0]

    def body(x_ref, mod_ref, g_ref, w_ref, h_ref, q_ref, kv_ref, gb_ref, gc_ref, xc_ref):
        xf = x_ref[...]
        n = xf * _rsqrt_mean_sq(xf) * g_ref[...]
        h = (n * (1.0 + mod_ref[SC1:SC1 + 1, :]) + mod_ref[SH1:SH1 + 1, :]).astype(BF16)
        h_ref[...] = h
        p = _dot_nt(h, w_ref[...])
        q_ref[...] = p[:, 0:512].astype(BF16)
        kv_ref[...] = p[:, 512:768].astype(BF16)
        gb_ref[...] = p[:, 768:1280].astype(BF16)
        gc_ref[...] = p[:, 1280:1792].astype(BF16)
        xc_ref[...] = p[:, 1792:2304].astype(BF16)

    return pl.pallas_call(
        body, name="in_proj", grid=(s // tm,),
        in_specs=[_rows(tm, D_MODEL), _full((8, D_MODEL)), _full((1, D_MODEL)), _full((IN_PROJ_WIDTH, D_MODEL))],
        out_specs=[_rows(tm, D_MODEL), _rows(tm, 512), _rows(tm, 256), _rows(tm, 512), _rows(tm, 512), _rows(tm, 512)],
        out_shape=[jax.ShapeDtypeStruct((s, D_MODEL), BF16), jax.ShapeDtypeStruct((s, 512), BF16),
                   jax.ShapeDtypeStruct((s, 256), BF16), jax.ShapeDtypeStruct((s, 512), BF16),
                   jax.ShapeDtypeStruct((s, 512), BF16), jax.ShapeDtypeStruct((s, 512), BF16)],
        compiler_params=_params(("arbitrary",), VMEM_LIMIT_LARGE),
    )(x, mod, g_norm1, w_in)


def _t5_bucket(dist):
    max_exact = N_BUCKETS // 2
    is_small = dist < max_exact
    d = jnp.maximum(dist, 1).astype(F32)
    large = max_exact + (jnp.log(d / max_exact) / math.log(MAX_DISTANCE / max_exact)
                         * (N_BUCKETS - max_exact)).astype(jnp.int32)
    large = jnp.minimum(large, N_BUCKETS - 1)
    return jnp.where(is_small, dist, large)


def _bucket_table():
    qi = jnp.arange(BLOCK, dtype=jnp.int32)[:, None]
    sj = jnp.arange(2 * BLOCK, dtype=jnp.int32)[None, :]
    return _t5_bucket(jnp.maximum(qi + BLOCK - sj, 0))


def _window_mask():
    qi = lax.broadcasted_iota(jnp.int32, (BLOCK, 2 * BLOCK), 0)
    sj = lax.broadcasted_iota(jnp.int32, (BLOCK, 2 * BLOCK), 1)
    dist = qi + BLOCK - sj
    return (dist >= 0) & (dist < BLOCK)


def _bias_table(rel_bias, bucket):
    def body(rb_ref, bk_ref, o_ref):
        bk = bk_ref[...]
        inside = _window_mask()
        for h in range(N_Q_HEADS):
            acc = jnp.zeros((BLOCK, 2 * BLOCK), F32)
            for b in range(N_BUCKETS):
                acc = jnp.where(bk == b, rb_ref[b, h], acc)
            o_ref[h] = jnp.where(inside, acc, NEG_INF)

    return pl.pallas_call(
        body, name="bias_table",
        in_specs=[pl.BlockSpec(memory_space=pltpu.SMEM), pl.BlockSpec(memory_space=pltpu.VMEM)],
        out_shape=jax.ShapeDtypeStruct((N_Q_HEADS, BLOCK, 2 * BLOCK), F32),
    )(rel_bias, bucket)


def _load_kv_window(kv_ref, n):
    prev = jnp.maximum(n - 1, 0)
    kvw = jnp.concatenate([kv_ref[pl.ds(pl.multiple_of(prev * BLOCK, BLOCK), BLOCK), :],
                           kv_ref[pl.ds(pl.multiple_of(n * BLOCK, BLOCK), BLOCK), :]], axis=0)
    k, v = kvw[:, 0:128], kvw[:, 128:256]
    k_sw = pltpu.roll(k.astype(F32), 64, 1).astype(BF16)
    v_sw = pltpu.roll(v.astype(F32), 64, 1).astype(BF16)
    return (k, k_sw), (v, v_sw)


def _conv_taps(gc, xc, gc_prev, xc_prev, n):
    u = gc * xc
    before = jnp.where(n > 0, gc_prev.astype(F32) * xc_prev.astype(F32), 0.0)
    last = before.shape[0] - 1
    row = lax.broadcasted_iota(jnp.int32, u.shape, 0)
    u1 = jnp.where(row == 0, before[last:last + 1, :], pltpu.roll(u, 1, 0))
    u2 = jnp.where(row == 0, before[last - 1:last, :],
                   jnp.where(row == 1, before[last:last + 1, :], pltpu.roll(u, 2, 0)))
    return u, u1, u2


def _mixer_fwd(q, kv, gb, gc, xc, bias, sinks, conv_w, g_attn, g_conv):
    s = q.shape[0]
    nb = s // BLOCK

    per_step = min(MIXER_BLOCKS, nb)
    tile = per_step * BLOCK

    def one_block(n, rows, before, sink_ref, q_ref, kv_ref, gb_ref, gc_ref, xc_ref, bias_ref, cw_ref, ga_ref,
                  gcv_ref, attn_ref, merged_ref, lse_ref):
        ks, vs = _load_kv_window(kv_ref, n)
        lane = lax.broadcasted_iota(jnp.int32, (BLOCK, BLOCK), 1)
        low = lane < HEAD_DIM
        col = lax.broadcasted_iota(jnp.int32, (BLOCK, 2 * BLOCK), 1)
        no_prev = (col < BLOCK) & (n == 0)
        lse_all = jnp.zeros((BLOCK, BLOCK), F32)
        pairs = []
        for p in range(4):
            qp = q_ref[rows, 128 * p:128 * (p + 1)].astype(F32)
            kvh = p // 2
            res = []
            for e in range(2):
                h = 2 * p + e
                qm = jnp.where(low if e == 0 else ~low, qp, 0.0).astype(BF16)
                sw = 0 if kvh == e else 1
                sc = _dot_nt(qm, ks[sw]) * SCALE + bias_ref[h]
                sc = jnp.where(no_prev, NEG_INF, sc)
                sink = sink_ref[h]
                m = jnp.maximum(jnp.max(sc, axis=-1, keepdims=True), sink)
                pe = jnp.exp(sc - m)
                den = jnp.sum(pe, axis=-1, keepdims=True) + jnp.exp(sink - m)
                res.append(_dot(pe.astype(BF16), vs[sw]) / den)
                lse_all = lse_all + jnp.where(lane == h, m + jnp.log(den), 0.0)
            pairs.append(jnp.where(low, res[0], res[1]))
        attn = jnp.concatenate(pairs, axis=1)
        attn_ref[rows, :] = attn
        lse_ref[rows, :] = lse_all
        u, u1, u2 = _conv_taps(gc_ref[rows, :].astype(F32), xc_ref[rows, :].astype(F32), before[0], before[1], n)
        cw = cw_ref[...]
        cv = gb_ref[rows, :].astype(F32) * (cw[0:1, :] * u2 + cw[1:2, :] * u1 + cw[2:3, :] * u)
        an = attn * _rsqrt_mean_sq(attn) * ga_ref[...]
        cn = cv * _rsqrt_mean_sq(cv) * gcv_ref[...]
        merged_ref[rows, :] = jnp.concatenate([an, cn], axis=1).astype(BF16)

    def body(sink_ref, q_ref, kv_ref, gb_ref, gc_ref, xc_ref, gcp_ref, xcp_ref, *rest):
        step = pl.program_id(0)
        for sub in range(per_step):
            rows = slice(sub * BLOCK, (sub + 1) * BLOCK)
            ahead = slice(sub * BLOCK - PREV_ROWS, sub * BLOCK)
            before = (gcp_ref[...], xcp_ref[...]) if sub == 0 else (gc_ref[ahead, :], xc_ref[ahead, :])
            one_block(step * per_step + sub, rows, before, sink_ref, q_ref, kv_ref, gb_ref, gc_ref, xc_ref, *rest)

    blk = lambda w: pl.BlockSpec((tile, w), lambda n: (n, 0))
    prev8 = pl.BlockSpec((PREV_ROWS, 512), lambda n: (jnp.maximum(n * (tile // PREV_ROWS) - 1, 0), 0))
    return pl.pallas_call(
        body, name="mixer_fwd", grid=(nb // per_step,),
        in_specs=[pl.BlockSpec(memory_space=pltpu.SMEM), blk(512), _full((s, 256)), blk(512), blk(512), blk(512),
                  prev8, prev8, _full((N_Q_HEADS, BLOCK, 2 * BLOCK)), _full((3, 512)), _full((1, 512)),
                  _full((1, 512))],
        out_specs=[blk(512), blk(1024), blk(128)],
        out_shape=[jax.ShapeDtypeStruct((s, 512), F32), jax.ShapeDtypeStruct((s, 1024), BF16),
                   jax.ShapeDtypeStruct((s, 128), F32)],
        compiler_params=_params(("arbitrary",)),
    )(sinks, q, kv, gb, gc, xc, gc, xc, bias, conv_w, g_attn, g_conv)


def _out_proj(merged, x, mod, w_out, tm):
    s = x.shape[0]

    def body(m_ref, x_ref, mod_ref, w_ref, o_ref, x1_ref):
        o = _dot(m_ref[...], w_ref[...])
        o_ref[...] = o.astype(BF16)
        x1_ref[...] = x_ref[...] + mod_ref[G1:G1 + 1, :] * o

    return pl.pallas_call(
        body, name="out_proj", grid=(s // tm,),
        in_specs=[_rows(tm, D_MODEL), _rows(tm, D_MODEL), _full((8, D_MODEL)), _full((D_MODEL, D_MODEL))],
        out_specs=[_rows(tm, D_MODEL), _rows(tm, D_MODEL)],
        out_shape=[jax.ShapeDtypeStruct((s, D_MODEL), BF16), jax.ShapeDtypeStruct((s, D_MODEL), F32)],
        compiler_params=_params(("arbitrary",)),
    )(merged, x, mod, w_out)


def _resident(shape):
    nd = len(shape)
    return pl.BlockSpec(shape, lambda *_: (0,) * nd, pipeline_mode=pl.Buffered(1))


def _ffn_fwd(x1, mod, g_norm2, w_gu, w_down, g_final, target, tm):
    s = x1.shape[0]
    chunk = D_FF // FFN_CHUNKS

    def body(x_ref, mod_ref, g_ref, wgu_ref, wd_ref, gf_ref, t_ref,
             h_ref, gate_ref, up_ref, act_ref, o_ref, dx2_ref, small_ref):
        @pl.when(pl.program_id(0) == 0)
        def _():
            small_ref[...] = jnp.zeros_like(small_ref)

        xf = x_ref[...]
        n = xf * _rsqrt_mean_sq(xf) * g_ref[...]
        h = (n * (1.0 + mod_ref[SC2:SC2 + 1, :]) + mod_ref[SH2:SH2 + 1, :]).astype(BF16)
        h_ref[...] = h
        o = None
        for j in range(FFN_CHUNKS):
            lo = j * chunk
            gate = _dot_nt(h, wgu_ref[lo:lo + chunk, :])
            up = _dot_nt(h, wgu_ref[D_FF + lo:D_FF + lo + chunk, :])
            gate_ref[:, lo:lo + chunk] = gate.astype(BF16)
            up_ref[:, lo:lo + chunk] = up.astype(BF16)
            act = (gate * _sigmoid(gate) * up).astype(BF16)
            act_ref[:, lo:lo + chunk] = act
            part = _dot(act, wd_ref[lo:lo + chunk, :])
            o = part if o is None else o + part
        o_ref[...] = o.astype(BF16)
        x2 = xf + mod_ref[G2:G2 + 1, :] * o
        r = _rsqrt_mean_sq(x2)
        xn = x2 * r
        gf = gf_ref[...]
        err = xn * gf - t_ref[...]
        dy = err * (1.0 / D_MODEL)
        dxn = dy * gf
        dx2_ref[...] = (r * (dxn - xn * jnp.mean(dxn * xn, axis=-1, keepdims=True))).astype(BF16)
        small_ref[0:1, :] += _colsum(dy * xn)
        small_ref[1:2, :] += _colsum(err * err)

        @pl.when(pl.program_id(0) == pl.num_programs(0) - 1)
        def _():
            total = jnp.sum(small_ref[1:2, :], axis=-1, keepdims=True) * (0.5 / D_MODEL)
            small_ref[2:3, :] = jnp.broadcast_to(total, (1, D_MODEL))

    wide = jax.ShapeDtypeStruct((s, D_FF), BF16)
    return pl.pallas_call(
        body, name="ffn_fwd", grid=(s // tm,),
        in_specs=[_rows(tm, D_MODEL), _full((8, D_MODEL)), _full((1, D_MODEL)), _resident((2 * D_FF, D_MODEL)),
                  _resident((D_FF, D_MODEL)), _full((1, D_MODEL)), _rows(tm, D_MODEL)],
        out_specs=[_rows(tm, D_MODEL), _rows(tm, D_FF), _rows(tm, D_FF), _rows(tm, D_FF), _rows(tm, D_MODEL),
                   _rows(tm, D_MODEL), _full((8, D_MODEL))],
        out_shape=[jax.ShapeDtypeStruct((s, D_MODEL), BF16), wide, wide, wide,
                   jax.ShapeDtypeStruct((s, D_MODEL), BF16), jax.ShapeDtypeStruct((s, D_MODEL), BF16),
                   jax.ShapeDtypeStruct((8, D_MODEL), F32)],
        compiler_params=_params(("arbitrary",), VMEM_LIMIT_LARGE),
    )(x1, mod, g_norm2, w_gu, w_down, g_final, target)


def _ffn_bwd(dx2, o2, gate, up, x1, mod, g_norm2, w_down, w_gu, tm):
    s = x1.shape[0]
    chunk = D_FF // FFN_CHUNKS

    def body(dx_ref, o_ref, gate_ref, up_ref, x_ref, mod_ref, g_ref, wd_ref, wgu_ref,
             do_ref, dgu_ref, dx1_ref, small_ref):
        @pl.when(pl.program_id(0) == 0)
        def _():
            small_ref[...] = jnp.zeros_like(small_ref)

        dx = dx_ref[...].astype(F32)
        small_ref[3:4, :] += _colsum(dx * o_ref[...].astype(F32))
        do = (dx * mod_ref[G2:G2 + 1, :]).astype(BF16)
        do_ref[...] = do
        dh = None
        for j in range(FFN_CHUNKS):
            lo = j * chunk
            dact = _dot_nt(do, wd_ref[lo:lo + chunk, :])
            gate = gate_ref[:, lo:lo + chunk].astype(F32)
            sg = _sigmoid(gate)
            dgate = (dact * up_ref[:, lo:lo + chunk].astype(F32) * (sg * (1.0 + gate * (1.0 - sg)))).astype(BF16)
            dup = (dact * (gate * sg)).astype(BF16)
            dgu_ref[:, lo:lo + chunk] = dgate
            dgu_ref[:, D_FF + lo:D_FF + lo + chunk] = dup
            part = _dot(dgate, wgu_ref[lo:lo + chunk, :]) + _dot(dup, wgu_ref[D_FF + lo:D_FF + lo + chunk, :])
            dh = part if dh is None else dh + part
        dx1 = dx + _norm_mod_bwd(dh, x_ref[...], g_ref[...], mod_ref[SC2:SC2 + 1, :], small_ref)
        dx1_ref[...] = dx1.astype(BF16)

    return pl.pallas_call(
        body, name="ffn_bwd", grid=(s // tm,),
        in_specs=[_rows(tm, D_MODEL), _rows(tm, D_MODEL), _rows(tm, D_FF), _rows(tm, D_FF), _rows(tm, D_MODEL),
                  _full((8, D_MODEL)), _full((1, D_MODEL)), _resident((D_FF, D_MODEL)),
                  _resident((2 * D_FF, D_MODEL))],
        out_specs=[_rows(tm, D_MODEL), _rows(tm, 2 * D_FF), _rows(tm, D_MODEL), _full((8, D_MODEL))],
        out_shape=[jax.ShapeDtypeStruct((s, D_MODEL), BF16), jax.ShapeDtypeStruct((s, 2 * D_FF), BF16),
                   jax.ShapeDtypeStruct((s, D_MODEL), BF16), jax.ShapeDtypeStruct((8, D_MODEL), F32)],
        compiler_params=_params(("arbitrary",), VMEM_LIMIT_LARGE),
    )(dx2, o2, gate, up, x1, mod, g_norm2, w_down, w_gu)


def _norm_mod_bwd(dh, xf, g, scale_row, small_ref):
    r = _rsqrt_mean_sq(xf)
    xn = xf * r
    small_ref[0:1, :] += _colsum(dh)
    small_ref[1:2, :] += _colsum(dh * (xn * g))
    dn = dh * (1.0 + scale_row)
    small_ref[2:3, :] += _colsum(dn * xn)
    dxn = dn * g
    return r * (dxn - xn * jnp.mean(dxn * xn, axis=-1, keepdims=True))


def _out_proj_bwd(dx1, o1, mod, w_out, tm):
    s = dx1.shape[0]

    def body(dx_ref, o_ref, mod_ref, w_ref, do_ref, dm_ref, small_ref):
        @pl.when(pl.program_id(0) == 0)
        def _():
            small_ref[...] = jnp.zeros_like(small_ref)

        dx = dx_ref[...].astype(F32)
        small_ref[0:1, :] += _colsum(dx * o_ref[...].astype(F32))
        do = (dx * mod_ref[G1:G1 + 1, :]).astype(BF16)
        do_ref[...] = do
        dm_ref[...] = _dot_nt(do, w_ref[...]).astype(BF16)

    return pl.pallas_call(
        body, name="out_proj_bwd", grid=(s // tm,),
        in_specs=[_rows(tm, D_MODEL), _rows(tm, D_MODEL), _full((8, D_MODEL)), _full((D_MODEL, D_MODEL))],
        out_specs=[_rows(tm, D_MODEL), _rows(tm, D_MODEL), _full((8, D_MODEL))],
        out_shape=[jax.ShapeDtypeStruct((s, D_MODEL), BF16), jax.ShapeDtypeStruct((s, D_MODEL), BF16),
                   jax.ShapeDtypeStruct((8, D_MODEL), F32)],
        compiler_params=_params(("arbitrary",)),
    )(dx1, o1, mod, w_out)


def _group_norm_bwd(dm, a, g):
    r = _rsqrt_mean_sq(a)
    an = a * r
    dan = dm * g
    return r * (dan - an * jnp.mean(dan * an, axis=-1, keepdims=True)), _colsum(dm * an)


def _mixer_bwd(q, kv, gb, gc, xc, bias, sinks, conv_w, g_attn, g_conv, attn, lse, dmerged):
    s = q.shape[0]
    nb = s // BLOCK

    per_step = min(MIXER_BLOCKS, nb)
    tile = per_step * BLOCK
    steps = nb // per_step

    def one_block(n, rows, before, nxt, sink_ref, q_ref, kv_ref, gb_ref, gc_ref, xc_ref, bias_ref, cw_ref, ga_ref,
                  gcv_ref, attn_ref, lse_ref, dm_ref,
                  dq_ref, dkv_ref, dgb_ref, dgc_ref, dxc_ref, dbias_ref, dsink_ref, small_ref):
        dm = dm_ref[rows, :].astype(F32)
        gbv, gcv_, xcv = gb_ref[rows, :].astype(F32), gc_ref[rows, :].astype(F32), xc_ref[rows, :].astype(F32)
        u, u1, u2 = _conv_taps(gcv_, xcv, before[0], before[1], n)
        cw = cw_ref[...]
        yv = cw[0:1, :] * u2 + cw[1:2, :] * u1 + cw[2:3, :] * u
        dcv, dg_conv = _group_norm_bwd(dm[:, 512:1024], gbv * yv, gcv_ref[...])
        small_ref[1:2, :] += dg_conv
        dgb_ref[rows, :] = (dcv * yv).astype(BF16)
        dy = dcv * gbv
        row = lax.broadcasted_iota(jnp.int32, dy.shape, 0)
        d1 = jnp.where(row == BLOCK - 1, nxt[0:1, :], pltpu.roll(dy, BLOCK - 1, 0))
        d2 = jnp.where(row == BLOCK - 2, nxt[0:1, :],
                       jnp.where(row == BLOCK - 1, nxt[1:2, :], pltpu.roll(dy, BLOCK - 2, 0)))
        du = cw[2:3, :] * dy + cw[1:2, :] * d1 + cw[0:1, :] * d2
        dgc_ref[rows, :] = (du * xcv).astype(BF16)
        dxc_ref[rows, :] = (du * gcv_).astype(BF16)
        small_ref[2:3, :] += _colsum(dy * u2)
        small_ref[3:4, :] += _colsum(dy * u1)
        small_ref[4:5, :] += _colsum(dy * u)

        attn_v = attn_ref[rows, :]
        dout, dg_attn = _group_norm_bwd(dm[:, 0:512], attn_v, ga_ref[...])
        small_ref[0:1, :] += dg_attn
        ks, vs = _load_kv_window(kv_ref, n)
        lane = lax.broadcasted_iota(jnp.int32, (BLOCK, BLOCK), 1)
        low = lane < HEAD_DIM
        col = lax.broadcasted_iota(jnp.int32, (BLOCK, 2 * BLOCK), 1)
        no_prev = (col < BLOCK) & (n == 0)
        lse_all = lse_ref[rows, :]
        dsink = jnp.zeros((BLOCK, BLOCK), F32)
        dq_pairs = []
        dk_groups, dv_groups = [], []
        for kvh in range(2):
            ds_rows, pr_rows, q_rows, do_rows = [], [], [], []
            for p in (2 * kvh, 2 * kvh + 1):
                qp = q_ref[rows, 128 * p:128 * (p + 1)].astype(F32)
                do_p = dout[:, 128 * p:128 * (p + 1)]
                prod = do_p * attn_v[:, 128 * p:128 * (p + 1)]
                res = []
                for e in range(2):
                    h = 2 * p + e
                    half = low if e == 0 else ~low
                    qm = jnp.where(half, qp, 0.0).astype(BF16)
                    dom = jnp.where(half, do_p, 0.0).astype(BF16)
                    delta = jnp.sum(jnp.where(half, prod, 0.0), axis=-1, keepdims=True)
                    lse_h = jnp.sum(jnp.where(lane == h, lse_all, 0.0), axis=-1, keepdims=True)
                    sw = 0 if kvh == e else 1
                    sc = _dot_nt(qm, ks[sw]) * SCALE + bias_ref[h]
                    sc = jnp.where(no_prev, NEG_INF, sc)
                    pr = jnp.exp(sc - lse_h)
                    dp = _dot_nt(dom, vs[sw])
                    ds = pr * (dp - delta)
                    dbias_ref[h] += ds
                    dsink = dsink + jnp.where(lane == h, -jnp.exp(sink_ref[h] - lse_h) * delta, 0.0)
                    dsb = ds.astype(BF16)
                    res.append(_dot(dsb, ks[sw]) * SCALE)
                    ds_rows.append(dsb)
                    pr_rows.append(pr.astype(BF16))
                    q_rows.append(qm)
                    do_rows.append(dom)
                dq_pairs.append(jnp.where(low, res[0], res[1]))
            dk_g = _dot_tn(jnp.concatenate(ds_rows, axis=0), jnp.concatenate(q_rows, axis=0)) * SCALE
            dv_g = _dot_tn(jnp.concatenate(pr_rows, axis=0), jnp.concatenate(do_rows, axis=0))
            dk_groups.append(dk_g + pltpu.roll(dk_g, 64, 1))
            dv_groups.append(dv_g + pltpu.roll(dv_g, 64, 1))
        dq_ref[rows, :] = jnp.concatenate(dq_pairs, axis=1).astype(BF16)
        dsink_ref[...] += dsink
        low_kv = lax.broadcasted_iota(jnp.int32, (2 * BLOCK, BLOCK), 1) < HEAD_DIM
        dkv_win = jnp.concatenate([jnp.where(low_kv, dk_groups[0], dk_groups[1]),
                                   jnp.where(low_kv, dv_groups[0], dv_groups[1])], axis=1)
        prev = jnp.maximum(n - 1, 0)
        dkv_ref[pl.ds(pl.multiple_of(prev * BLOCK, BLOCK), BLOCK), :] += dkv_win[0:BLOCK, :]
        dkv_ref[pl.ds(pl.multiple_of(n * BLOCK, BLOCK), BLOCK), :] += dkv_win[BLOCK:2 * BLOCK, :]

        return dy[0:8, :]

    def body(sink_ref, q_ref, kv_ref, gb_ref, gc_ref, xc_ref, gcp_ref, xcp_ref, *rest):
        refs, carry_ref = rest[:-1], rest[-1]
        dkv_ref, dbias_ref, dsink_ref, small_ref = refs[8], refs[12], refs[13], refs[14]
        step = pl.program_id(0)

        @pl.when(step == 0)
        def _():
            dkv_ref[...] = jnp.zeros_like(dkv_ref)
            dbias_ref[...] = jnp.zeros_like(dbias_ref)
            dsink_ref[...] = jnp.zeros_like(dsink_ref)
            small_ref[...] = jnp.zeros_like(small_ref)
            carry_ref[...] = jnp.zeros_like(carry_ref)

        nxt = carry_ref[...]
        for sub in reversed(range(per_step)):
            rows = slice(sub * BLOCK, (sub + 1) * BLOCK)
            ahead = slice(sub * BLOCK - PREV_ROWS, sub * BLOCK)
            before = (gcp_ref[...], xcp_ref[...]) if sub == 0 else (gc_ref[ahead, :], xc_ref[ahead, :])
            nxt = one_block((steps - 1 - step) * per_step + sub, rows, before, nxt,
                            sink_ref, q_ref, kv_ref, gb_ref, gc_ref, xc_ref, *refs)
        carry_ref[...] = nxt

        @pl.when(step == steps - 1)
        def _():
            small_ref[5:6, :] = jnp.concatenate([_colsum(dsink_ref[...]), jnp.zeros((1, 512 - BLOCK), F32)], axis=1)

    blk = lambda w: pl.BlockSpec((tile, w), lambda t: (steps - 1 - t, 0))
    prev8 = pl.BlockSpec((PREV_ROWS, 512),
                         lambda t: (jnp.maximum((steps - 1 - t) * (tile // PREV_ROWS) - 1, 0), 0))
    bf = lambda w: jax.ShapeDtypeStruct((s, w), BF16)
    return pl.pallas_call(
        body, name="mixer_bwd", grid=(steps,),
        in_specs=[pl.BlockSpec(memory_space=pltpu.SMEM), blk(512), _full((s, 256)), blk(512), blk(512), blk(512),
                  prev8, prev8, _full((N_Q_HEADS, BLOCK, 2 * BLOCK)), _full((3, 512)), _full((1, 512)),
                  _full((1, 512)), blk(512), blk(128), blk(1024)],
        out_specs=[blk(512), _full((s, 256)), blk(512), blk(512), blk(512), _full((N_Q_HEADS, BLOCK, 2 * BLOCK)),
                   _full((BLOCK, BLOCK)), _full((8, 512))],
        out_shape=[bf(512), jax.ShapeDtypeStruct((s, 256), F32), bf(512), bf(512), bf(512),
                   jax.ShapeDtypeStruct((N_Q_HEADS, BLOCK, 2 * BLOCK), F32), jax.ShapeDtypeStruct((BLOCK, BLOCK), F32),
                   jax.ShapeDtypeStruct((8, 512), F32)],
        scratch_shapes=[pltpu.VMEM((8, 512), F32)],
        compiler_params=_params(("arbitrary",), VMEM_LIMIT_LARGE),
    )(sinks, q, kv, gb, gc, xc, gc, xc, bias, conv_w, g_attn, g_conv, attn, lse, dmerged)


def _in_proj_bwd(dq, dkv, dgb, dgc, dxc, x, dx1, mod, g_norm1, w_in, tm):
    s = x.shape[0]

    def body(dq_ref, dkv_ref, dgb_ref, dgc_ref, dxc_ref, x_ref, dx1_ref, mod_ref, g_ref, w_ref,
             dproj_ref, dx_ref, small_ref):
        @pl.when(pl.program_id(0) == 0)
        def _():
            small_ref[...] = jnp.zeros_like(small_ref)

        dproj = jnp.concatenate([dq_ref[...], dkv_ref[...].astype(BF16), dgb_ref[...], dgc_ref[...], dxc_ref[...]],
                                axis=1)
        dproj_ref[...] = dproj
        dh = _dot(dproj, w_ref[...])
        dx_ref[...] = dx1_ref[...].astype(F32) + _norm_mod_bwd(dh, x_ref[...], g_ref[...], mod_ref[SC1:SC1 + 1, :], small_ref)

    return pl.pallas_call(
        body, name="in_proj_bwd", grid=(s // tm,),
        in_specs=[_rows(tm, 512), _rows(tm, 256), _rows(tm, 512), _rows(tm, 512), _rows(tm, 512),
                  _rows(tm, D_MODEL), _rows(tm, D_MODEL), _full((8, D_MODEL)), _full((1, D_MODEL)),
                  _full((IN_PROJ_WIDTH, D_MODEL))],
        out_specs=[_rows(tm, IN_PROJ_WIDTH), _rows(tm, D_MODEL), _full((8, D_MODEL))],
        out_shape=[jax.ShapeDtypeStruct((s, IN_PROJ_WIDTH), BF16), jax.ShapeDtypeStruct((s, D_MODEL), F32),
                   jax.ShapeDtypeStruct((8, D_MODEL), F32)],
        compiler_params=_params(("arbitrary",), VMEM_LIMIT_LARGE),
    )(dq, dkv, dgb, dgc, dxc, x, dx1, mod, g_norm1, w_in)


def _weight_grad(a, b, tk, ts, name, after=None):
    s, k = a.shape
    n = b.shape[1]
    nt = s // ts
    extra = [] if after is None else [after]

    def body(a_ref, b_ref, *rest):
        o_ref, acc_ref = rest[-2:]
        t = pl.program_id(1)
        part = _dot_tn(a_ref[...], b_ref[...])

        @pl.when(t == 0)
        def _():
            acc_ref[...] = part

        @pl.when(t > 0)
        def _():
            acc_ref[...] += part

        @pl.when(t == nt - 1)
        def _():
            o_ref[...] = acc_ref[...].astype(BF16)

    return pl.pallas_call(
        body, name=name, grid=(k // tk, nt),
        in_specs=[pl.BlockSpec((ts, tk), lambda i, t: (t, i)), pl.BlockSpec((ts, n), lambda i, t: (t, 0))]
        + [pl.BlockSpec(memory_space=pl.ANY)] * len(extra),
        out_specs=pl.BlockSpec((tk, n), lambda i, t: (i, 0)),
        out_shape=jax.ShapeDtypeStruct((k, n), BF16),
        scratch_shapes=[pltpu.VMEM((tk, n), F32)],
        compiler_params=_params(("arbitrary", "arbitrary"), VMEM_LIMIT_LARGE),
    )(a, b, *extra)


def _rel_bias_grad(dbias, bucket):
    def body(db_ref, bk_ref, o_ref, rows_ref):
        bk = bk_ref[...]
        for b in range(N_BUCKETS):
            sel = (bk == b).astype(F32)
            for h in range(N_Q_HEADS):
                rows_ref[8 * b + h:8 * b + h + 1, :] = _colsum(db_ref[h] * sel)
        o_ref[...] = jnp.sum(rows_ref[...], axis=-1, keepdims=True)

    return pl.pallas_call(
        body, name="rel_bias_grad",
        out_shape=jax.ShapeDtypeStruct((N_BUCKETS * N_Q_HEADS, 1), F32),
        scratch_shapes=[pltpu.VMEM((N_BUCKETS * N_Q_HEADS, 2 * BLOCK), F32)],
    )(dbias, bucket)


def _sum_slots(parts, after):
    def body(p_ref, after_ref, o_ref):
        acc = p_ref[0]
        for k in range(1, N_DEV):
            acc = acc + p_ref[k]
        o_ref[...] = acc

    return pl.pallas_call(body, name="sum_small_grads",
                          in_specs=[pl.BlockSpec(memory_space=pltpu.VMEM), pl.BlockSpec(memory_space=pl.ANY)],
                          out_shape=jax.ShapeDtypeStruct(parts.shape[1:], F32))(parts, after)


def _w_ada_grad(cond_t, dmod_cols):
    def body(c_ref, d_ref, o_ref):
        o_ref[...] = _dot(c_ref[...], d_ref[...])

    return pl.pallas_call(body, name="w_ada_grad",
                          out_shape=jax.ShapeDtypeStruct((cond_t.shape[0], dmod_cols.shape[1]), F32))(cond_t, dmod_cols)


def _adam_math(w, g, m, v):
    m = ADAM_B1 * m + (1.0 - ADAM_B1) * g
    v = ADAM_B2 * v + (1.0 - ADAM_B2) * (g * g)
    m_hat = m / (1.0 - ADAM_B1 ** ADAM_STEP)
    v_hat = v / (1.0 - ADAM_B2 ** ADAM_STEP)
    delta = -ADAM_LR * (m_hat / (jnp.sqrt(v_hat) + ADAM_EPS) + ADAM_WD * w)
    return delta, m, v


def _adamw_parts(w, m, v, local, land, me, tr, name):
    r, c = w.shape

    def body(me_ref, w_ref, m_ref, v_ref, own_ref, land_ref, g_ref, d_ref, mo_ref, vo_ref):
        g = own_ref[0].astype(F32)
        for k in range(N_DEV - 1):
            g = g + land_ref[k].astype(F32)
        g_ref[...] = g
        d_ref[...], mo_ref[...], vo_ref[...] = _adam_math(w_ref[...], g, m_ref[...], v_ref[...])

    tile = pl.BlockSpec((tr, c), lambda i, me_ref: (i, 0))
    return pl.pallas_call(
        body, name=name,
        grid_spec=pltpu.PrefetchScalarGridSpec(
            num_scalar_prefetch=1, grid=(r // tr,),
            in_specs=[tile, tile, tile, pl.BlockSpec((1, tr, c), lambda i, me_ref: (me_ref[0], i, 0)),
                      pl.BlockSpec((N_DEV - 1, tr, c), lambda i, me_ref: (0, i, 0))],
            out_specs=[tile] * 4),
        out_shape=[jax.ShapeDtypeStruct((r, c), F32)] * 4,
        compiler_params=_params(("arbitrary",)),
    )(me, w, m, v, local, land)


def _adamw(w, m, v, g, tr, name):
    r, c = w.shape

    def body(w_ref, m_ref, v_ref, g_ref, d_ref, mo_ref, vo_ref):
        d_ref[...], mo_ref[...], vo_ref[...] = _adam_math(w_ref[...], g_ref[...], m_ref[...], v_ref[...])

    tile = pl.BlockSpec((tr, c), lambda i: (i, 0))
    return pl.pallas_call(
        body, name=name, grid=(r // tr,),
        in_specs=[tile] * 4, out_specs=[tile] * 3,
        out_shape=[jax.ShapeDtypeStruct((r, c), F32)] * 3,
        compiler_params=_params(("arbitrary",)),
    )(w, m, v, g)


def _behind(a, token):
    return a + token[0:a.shape[0], 0:1]


def _local_step(x, target, mod, w_in_t, weights_out_gu, weights_down, rel_bias, g_norm1, sinks, conv_w, g_attn,
                g_conv, g_norm2, g_final, exchange):
    s = x.shape[0]
    tm = min(512, s)
    tm_small = min(256, s)
    bucket = _bucket_table()
    bias = _bias_table(rel_bias, bucket)

    h, q, kv, gb, gc, xc = _in_proj(x, mod, g_norm1, w_in_t, tm)
    attn, merged, lse = _mixer_fwd(q, kv, gb, gc, xc, bias, sinks, conv_w, g_attn, g_conv)
    w_out, w_gu_t = weights_out_gu(merged)
    o1, x1 = _out_proj(merged, x, mod, w_out, tm)
    w_down = weights_down(x1)
    h2, gate, up, act, o2, dx2, fin = _ffn_fwd(x1, mod, g_norm2, w_gu_t, w_down, g_final, target, tm_small)

    do2, dgu, dx1, sm_2 = _ffn_bwd(dx2, o2, gate, up, x1, mod, g_norm2, w_down, w_gu_t, tm_small)
    ts = min(WEIGHT_GRAD_ROWS, s)
    tok_down = exchange("w_down", _weight_grad(act, do2, D_FF // 2, ts, "w_down_grad"))
    mod = _behind(mod, exchange("w_gu", _weight_grad(dgu, h2, D_FF // 2, ts, "w_gu_grad", after=tok_down)))
    do1, dmerged, sm_g1 = _out_proj_bwd(dx1, o1, mod, w_out, tm)
    g_attn_b = _behind(g_attn, exchange("w_out", _weight_grad(merged, do1, D_MODEL, ts, "w_out_grad")))
    dq, dkv, dgb, dgc, dxc, dbias, dsink, sm_mix = _mixer_bwd(
        q, kv, gb, gc, xc, bias, sinks, conv_w, g_attn_b, g_conv, attn, lse, dmerged)
    dproj, dx, sm_1 = _in_proj_bwd(dq, dkv, dgb, dgc, dxc, x, dx1, mod, g_norm1, w_in_t, tm)
    d_rel = _rel_bias_grad(dbias, bucket)

    packed = jnp.concatenate([
        sm_1[0], sm_1[1], sm_g1[0], sm_2[0], sm_2[1], sm_2[3],
        d_rel[:, 0],
        sm_1[2],
        sm_mix[5, 0:128],
        sm_mix[0], sm_mix[1],
        sm_2[2],
        fin[0],
        sm_mix[2], sm_mix[3], sm_mix[4],
        fin[2, 0:128],
    ])[None, :]
    return dx, dproj, h, packed


def kernel(x, c, rel_bias, w_ada, b_ada, g_norm1, w_in, sinks, conv_w, g_attn_out, g_conv_out, w_out, g_norm2, w_gu, w_down, g_final, loss_target, m_rel_bias, m_w_ada, m_b_ada, m_g_norm1, m_w_in, m_sinks, m_conv_w, m_g_attn_out, m_g_conv_out, m_w_out, m_g_norm2, m_w_gu, m_w_down, m_g_final, v_rel_bias, v_w_ada, v_b_ada, v_g_norm1, v_w_in, v_sinks, v_conv_w, v_g_attn_out, v_g_conv_out, v_w_out, v_g_norm2, v_w_gu, v_w_down, v_g_final):
    me = _linear(_mesh_position())
    me_arr = jnp.reshape(me, (1,)).astype(jnp.int32)
    ada_cols = w_ada.shape[2]
    tm = min(512, x.shape[1])

    cond = _silu_rows(c)
    cond_all, conv_w_all = _all_gather_small([cond, conv_w[0]], "gather_cond")
    cond_all = cond_all[:, 0, :]
    conv_cols = conv_w.shape[2]
    conv_w_full = conv_w_all.transpose(1, 0, 2).reshape(3, CONV_WIDTH)
    b_cols = lax.dynamic_slice_in_dim(b_ada, me * ada_cols, ada_cols, axis=1)
    mod_cols = _mod_columns(cond_all, w_ada[0], b_cols)
    mod_all = _all_gather_small([mod_cols], "gather_mod")[0]
    mod = lax.dynamic_index_in_dim(mod_all, me, axis=1, keepdims=False).reshape(N_MOD, D_MODEL)
    mod = jnp.concatenate([mod, jnp.zeros((2, D_MODEL), F32)], axis=0)

    w_in_t = _all_gather([w_in[0].T], "gather_w_in", to_bf16=True, big=True)[0].reshape(IN_PROJ_WIDTH, D_MODEL)
    gather_sems, staged, gather_token = _gather_start(
        _stage_blocks([w_out[0], w_gu[0].T, w_down[0]], w_in_t, "stage_weights"), "gather_start_weights")
    mod = _behind(mod, gather_token)

    def weights_out_gu(after):
        got = _gather_pass_on(_gather_wait(gather_sems[0:4], staged[0:2], [after], "gather_wait_out_gu"),
                              "gather_pass_on_out_gu")
        return got[0].reshape(D_MODEL, D_MODEL), got[1].reshape(2 * D_FF, D_MODEL)

    def weights_down(after):
        got = _gather_pass_on(_gather_wait(gather_sems[4:6], staged[2:3], [after], "gather_wait_down"),
                              "gather_pass_on_down")
        return got[0].reshape(D_FF, D_MODEL)

    started = {}

    def exchange(name, dw):
        st = _exchange_start(dw.reshape(N_DEV, dw.shape[0] // N_DEV, dw.shape[1]), "exchange_start_" + name)
        started[name] = st
        return st[4]

    dx, dproj, h, packed = _local_step(
        x[0], loss_target[0], mod, w_in_t, weights_out_gu, weights_down, rel_bias, g_norm1, sinks[0], conv_w_full,
        g_attn_out, g_conv_out, g_norm2, g_final[None, :], exchange)

    packed_all = _all_gather_small([packed], "gather_small_grads")[0]
    tok_in = exchange("w_in", _weight_grad(dproj, h, IN_PROJ_WIDTH // 2, min(WEIGHT_GRAD_ROWS, x.shape[1]),
                                           "w_in_grad", after=packed_all))
    small = _sum_slots(packed_all, tok_in)[0]
    dmod_all = packed_all[:, 0, OFF_DMOD:OFF_DMOD + N_MOD * D_MODEL]
    dmod_cols = lax.dynamic_slice_in_dim(dmod_all, me * ada_cols, ada_cols, axis=1)
    cond_t = jnp.zeros((D_MODEL, 128), F32).at[:, 0:N_DEV].set(cond_all.T)
    dmod_pad = jnp.zeros((128, ada_cols), F32).at[0:N_DEV, :].set(dmod_cols)
    g_ada = _w_ada_grad(cond_t, dmod_pad)
    d_ada, nm_ada, nv_ada = _adamw(w_ada[0], m_w_ada[0], v_w_ada[0], g_ada, 256, "adamw_w_ada")

    loss = small[OFF_LOSS]
    seg = lambda off, n: small[off:off + n]
    conv_g_full = seg(OFF_CONVW, 3 * CONV_WIDTH).reshape(3, CONV_WIDTH)
    small_grads = {
        "rel_bias": seg(OFF_RELB, 256).reshape(N_BUCKETS, N_Q_HEADS),
        "b_ada": seg(OFF_DMOD, N_MOD * D_MODEL).reshape(1, N_MOD * D_MODEL),
        "g_norm1": seg(OFF_GN1, D_MODEL).reshape(1, D_MODEL),
        "sinks": seg(OFF_SINK, N_Q_HEADS).reshape(1, N_Q_HEADS),
        "conv_w": lax.dynamic_slice_in_dim(conv_g_full, me * conv_cols, conv_cols, axis=1)[None],
        "g_attn_out": seg(OFF_GATT, ATTN_WIDTH).reshape(1, ATTN_WIDTH),
        "g_conv_out": seg(OFF_GCV, CONV_WIDTH).reshape(1, CONV_WIDTH),
        "g_norm2": seg(OFF_GN2, D_MODEL).reshape(1, D_MODEL),
        "g_final": seg(OFF_GFIN, D_MODEL),
    }
    small_state = {
        "rel_bias": (rel_bias, m_rel_bias, v_rel_bias), "b_ada": (b_ada, m_b_ada, v_b_ada),
        "g_norm1": (g_norm1, m_g_norm1, v_g_norm1), "sinks": (sinks, m_sinks, v_sinks),
        "conv_w": (conv_w, m_conv_w, v_conv_w), "g_attn_out": (g_attn_out, m_g_attn_out, v_g_attn_out),
        "g_conv_out": (g_conv_out, m_g_conv_out, v_g_conv_out), "g_norm2": (g_norm2, m_g_norm2, v_g_norm2),
        "g_final": (g_final, m_g_final, v_g_final),
    }
    names = list(small_grads)
    sizes = [small_grads[k].size for k in names]
    total = sum(sizes)
    padded = -(-total // 1024) * 1024

    def pack(arrs):
        flat = jnp.concatenate([a.reshape(-1) for a in arrs] + [jnp.ones((padded - total,), F32)])
        return flat.reshape(padded // 128, 128)

    sw = pack([small_state[k][0] for k in names])
    sm = pack([small_state[k][1] for k in names])
    sv = pack([small_state[k][2] for k in names])
    sg = pack([small_grads[k] for k in names])
    sd, snm, snv = _adamw(sw, sm, sv, sg, padded // 128, "adamw_small")

    def unpack(flat2d):
        flat = flat2d.reshape(-1)
        out, off = {}, 0
        for k, n in zip(names, sizes):
            out[k] = flat[off:off + n].reshape(small_grads[k].shape)
            off += n
        return out

    sd_all = sd
    sd, snm, snv = unpack(sd), unpack(snm), unpack(snv)

    def finish(name, after, w, m, v, tr):
        src, land = _exchange_wait(started[name], after, "exchange_wait_" + name)
        return _adamw_parts(w, m, v, src, land, me_arr, tr, "adamw_" + name)

    g_down, d_down, nm_down, nv_down = finish("w_down", [sd_all], w_down[0], m_w_down[0], v_w_down[0], 176)
    g_gu, d_gu, nm_gu, nv_gu = finish("w_gu", [nv_down], w_gu[0].T, m_w_gu[0].T, v_w_gu[0].T, 352)
    g_out, d_out, nm_out, nv_out = finish("w_out", [nv_gu], w_out[0], m_w_out[0], v_w_out[0], 128)
    g_in, d_in, nm_in, nv_in = finish("w_in", [nv_out, nv_ada], w_in[0].T, m_w_in[0].T, v_w_in[0].T, 144)

    big = {
        "w_ada": (g_ada[None], d_ada[None], nm_ada[None], nv_ada[None]),
        "w_in": (g_in.T[None], d_in.T[None], nm_in.T[None], nv_in.T[None]),
        "w_out": (g_out[None], d_out[None], nm_out[None], nv_out[None]),
        "w_gu": (g_gu.T[None], d_gu.T[None], nm_gu.T[None], nv_gu.T[None]),
        "w_down": (g_down[None], d_down[None], nm_down[None], nv_down[None]),
    }
    order = ["rel_bias", "w_ada", "b_ada", "g_norm1", "w_in", "sinks", "conv_w", "g_attn_out", "g_conv_out", "w_out",
             "g_norm2", "w_gu", "w_down", "g_final"]
    grads = [big[k][0] if k in big else small_grads[k] for k in order]
    deltas = [big[k][1] if k in big else sd[k] for k in order]
    new_m = [big[k][2] if k in big else snm[k] for k in order]
    new_v = [big[k][3] if k in big else snv[k] for k in order]
    return (loss, dx[None], *grads, *deltas, *new_m, *new_v)
```

```python
import functools
import math

import jax
import jax.numpy as jnp
from jax import lax
from jax.experimental import pallas as pl
from jax.experimental.pallas import tpu as pltpu

F32 = jnp.float32
BF16 = jnp.bfloat16

D_MODEL = 1024
HEAD_DIM = 64
N_Q_HEADS = 8
ATTN_WIDTH = 512
KV_WIDTH = 128
CONV_WIDTH = 512
IN_PROJ_WIDTH = 2304
D_FF = 2816
N_MOD = 6
N_BUCKETS = 32
MAX_DISTANCE = 128
BLOCK = 128
EPS = 1e-6
NEG_INF = -1e30
SCALE = HEAD_DIM ** -0.5
N_DEV = 8

ADAM_LR = 0.001
ADAM_B1 = 0.9
ADAM_B2 = 0.999
ADAM_EPS = 1e-08
ADAM_WD = 0.01
ADAM_STEP = 10

SH1, SC1, G1, SH2, SC2, G2 = range(6)

VMEM_LIMIT_LARGE = 56 * 1024 * 1024
WEIGHT_GRAD_ROWS = 2048
FFN_CHUNKS = 2
PREV_ROWS = 16
MIXER_BLOCKS = 4
MESH_ID = pl.DeviceIdType.MESH

OFF_DMOD = 0
OFF_GN1 = OFF_DMOD + N_MOD * D_MODEL
OFF_SINK = OFF_GN1 + D_MODEL
OFF_GATT = OFF_SINK + 128
OFF_GCV = OFF_GATT + ATTN_WIDTH
OFF_GN2 = OFF_GCV + CONV_WIDTH
OFF_GFIN = OFF_GN2 + D_MODEL
OFF_CONVW = OFF_GFIN + D_MODEL
OFF_LOSS = OFF_CONVW + 3 * CONV_WIDTH
PACKED = OFF_LOSS + 128


def _params(sem=None, vmem=None):
    return pltpu.CompilerParams(dimension_semantics=sem, vmem_limit_bytes=vmem)


def _full(shape):
    nd = len(shape)
    return pl.BlockSpec(shape, lambda *_: (0,) * nd)


def _rows(tm, width):
    return pl.BlockSpec((tm, width), lambda i, *_: (i, 0))


def _sigmoid(x):
    return 1.0 / (1.0 + jnp.exp(-x))


def _rsqrt_mean_sq(x):
    return lax.rsqrt(jnp.mean(x * x, axis=-1, keepdims=True) + EPS)


def _colsum(x):
    return jnp.sum(x, axis=0, keepdims=True)


def _dot(a, b):
    return jnp.dot(a, b, preferred_element_type=F32)


def _dot_nt(a, b):
    return lax.dot_general(a, b, (((1,), (1,)), ((), ())), preferred_element_type=F32)


def _dot_tn(a, b):
    return lax.dot_general(a, b, (((0,), (0,)), ((), ())), preferred_element_type=F32)


def _mesh_position():
    return lax.axis_index("x"), lax.axis_index("y"), lax.axis_index("c")


def _linear(p):
    return 4 * p[0] + 2 * p[1] + p[2]


def _all_gather(arrs, name, to_bf16, big):
    n = len(arrs)
    out_dtype = BF16 if to_bf16 else F32

    def body(*refs):
        in_refs, out_refs = refs[:n], refs[n:2 * n]
        rest = refs[2 * n:]
        if to_bf16:
            stage, rest = rest[:n], rest[n:]
            for a in range(n):
                stage[a][...] = in_refs[a][...].astype(BF16)
            srcs = stage
        else:
            srcs = in_refs
        send_sems, recv_sems, local_sems = rest
        x, y, c = _mesh_position()
        me, sibling = (x, y, c), (x, y, 1 - c)
        chips = [(1 - x, y), (x, 1 - y), (1 - x, 1 - y)]

        def slot(a, p):
            return out_refs[a].at[_linear(p)]

        def copy(k, a, block, to, src=None):
            return pltpu.make_async_remote_copy(
                src_ref=slot(a, block) if src is None else src,
                dst_ref=slot(a, block),
                send_sem=send_sems.at[k * n + a],
                recv_sem=recv_sems.at[k * n + a],
                device_id=to,
                device_id_type=MESH_ID,
            )

        mine = [pltpu.make_async_copy(srcs[a], slot(a, me), local_sems.at[a]) for a in range(n)]
        for cp in mine:
            cp.start()
        first = [copy(0, a, me, sibling, src=srcs[a]) for a in range(n)]
        for j, chip in enumerate(chips):
            first += [copy(1 + j, a, me, (*chip, c), src=srcs[a]) for a in range(n)]
        for cp in first:
            cp.start()
        passed = []
        for j, chip in enumerate(chips):
            for a in range(n):
                copy(1 + j, a, (*chip, c), me).wait_recv()
                fwd = copy(4 + j, a, (*chip, c), sibling)
                fwd.start()
                passed.append(fwd)
        for a in range(n):
            copy(0, a, sibling, me).wait_recv()
        for j, chip in enumerate(chips):
            for a in range(n):
                copy(4 + j, a, (*chip, 1 - c), me).wait_recv()
        for cp in first + passed:
            cp.wait_send()
        for cp in mine:
            cp.wait()

    vmem = pl.BlockSpec(memory_space=pltpu.VMEM)
    out_space = pl.BlockSpec(memory_space=pl.ANY) if big else vmem
    scratch = [pltpu.VMEM(a.shape, BF16) for a in arrs] if to_bf16 else []
    scratch += [pltpu.SemaphoreType.DMA((7 * n,)), pltpu.SemaphoreType.DMA((7 * n,)),
                pltpu.SemaphoreType.DMA((n,))]
    outs = pl.pallas_call(
        body, name=name,
        out_shape=[jax.ShapeDtypeStruct((N_DEV,) + a.shape, out_dtype) for a in arrs],
        in_specs=[vmem] * n, out_specs=[out_space] * n,
        scratch_shapes=scratch,
        compiler_params=_params(vmem=VMEM_LIMIT_LARGE if big else None),
    )(*arrs)
    return list(outs)


def _peer(k):
    x, y, c = _mesh_position()
    return (1 - x if k & 4 else x, 1 - y if k & 2 else y, 1 - c if k & 1 else c)


def _all_gather_small(arrs, name):
    n = len(arrs)

    def body(*refs):
        in_refs, out_refs = refs[:n], refs[n:2 * n]
        send_sems, recv_sems, local_sems = refs[2 * n:]
        me = _linear(_mesh_position())
        mine = [pltpu.make_async_copy(in_refs[a], out_refs[a].at[me], local_sems.at[a]) for a in range(n)]
        for cp in mine:
            cp.start()
        sends = []
        for k in range(1, N_DEV):
            for a in range(n):
                sends.append(pltpu.make_async_remote_copy(
                    src_ref=in_refs[a], dst_ref=out_refs[a].at[me],
                    send_sem=send_sems.at[(k - 1) * n + a], recv_sem=recv_sems.at[(k - 1) * n + a],
                    device_id=_peer(k), device_id_type=MESH_ID))
                sends[-1].start()
        for k in range(1, N_DEV):
            for a in range(n):
                pltpu.make_async_remote_copy(
                    src_ref=in_refs[a], dst_ref=out_refs[a].at[_linear(_peer(k))],
                    send_sem=send_sems.at[(k - 1) * n + a], recv_sem=recv_sems.at[(k - 1) * n + a],
                    device_id=_peer(k), device_id_type=MESH_ID).wait_recv()
        for cp in sends:
            cp.wait_send()
        for cp in mine:
            cp.wait()

    vmem = pl.BlockSpec(memory_space=pltpu.VMEM)
    return list(pl.pallas_call(
        body, name=name,
        out_shape=[jax.ShapeDtypeStruct((N_DEV,) + a.shape, F32) for a in arrs],
        in_specs=[vmem] * n, out_specs=[vmem] * n,
        scratch_shapes=[pltpu.SemaphoreType.DMA((7 * n,)), pltpu.SemaphoreType.DMA((7 * n,)),
                        pltpu.SemaphoreType.DMA((n,))],
    )(*arrs))


HBM_SPEC = pl.BlockSpec(memory_space=pltpu.HBM)
SEM_SPEC = pl.BlockSpec(memory_space=pltpu.SEMAPHORE)
DATAFLOW = pltpu.SideEffectType.DATAFLOW_SIDE_EFFECTING


def _exchange_start(src, name):
    r, c = src.shape[1:]

    def body(src_ref, land_ref, send_sems, recv_sems, src_thru, land_thru, token):
        for k in range(1, N_DEV):
            peer = _peer(k)
            pltpu.make_async_remote_copy(
                src_ref=src_ref.at[_linear(peer)], dst_ref=land_ref.at[k - 1],
                send_sem=send_sems.at[k - 1], recv_sem=recv_sems.at[k - 1],
                device_id=peer, device_id_type=MESH_ID).start()
        token[...] = jnp.zeros_like(token)

    land = lax.empty((N_DEV - 1, r, c), src.dtype)
    return pl.pallas_call(
        body, name=name,
        out_shape=(pltpu.SemaphoreType.DMA((N_DEV - 1,)), pltpu.SemaphoreType.DMA((N_DEV - 1,)),
                   pltpu.HBM(src.shape, src.dtype), pltpu.HBM(land.shape, land.dtype),
                   jax.ShapeDtypeStruct((8, 128), F32)),
        in_specs=(HBM_SPEC, HBM_SPEC),
        out_specs=(SEM_SPEC, SEM_SPEC, HBM_SPEC, HBM_SPEC, pl.BlockSpec(memory_space=pltpu.VMEM)),
        input_output_aliases={0: 2, 1: 3},
        compiler_params=pltpu.CompilerParams(has_side_effects=DATAFLOW),
    )(pltpu.with_memory_space_constraint(src, pltpu.HBM), pltpu.with_memory_space_constraint(land, pltpu.HBM))


def _exchange_wait(started, after, name):
    send_sems, recv_sems, src_thru, land_thru, _ = started

    def body(src_ref, land_ref, send_sems, recv_sems, *rest):
        for k in range(1, N_DEV):
            cp = pltpu.make_async_remote_copy(
                src_ref=src_ref.at[0], dst_ref=land_ref.at[k - 1],
                send_sem=send_sems.at[k - 1], recv_sem=recv_sems.at[k - 1],
                device_id=_peer(k), device_id_type=MESH_ID)
            cp.wait_send()
            cp.wait_recv()

    return pl.pallas_call(
        body, name=name,
        out_shape=(pltpu.HBM(src_thru.shape, src_thru.dtype), pltpu.HBM(land_thru.shape, land_thru.dtype)),
        in_specs=(HBM_SPEC, HBM_SPEC, SEM_SPEC, SEM_SPEC) + (pl.BlockSpec(memory_space=pl.ANY),) * len(after),
        out_specs=(HBM_SPEC, HBM_SPEC), input_output_aliases={0: 0, 1: 1},
        compiler_params=pltpu.CompilerParams(has_side_effects=DATAFLOW),
    )(src_thru, land_thru, send_sems, recv_sems, *after)


def _stage_blocks(arrs, after, name):
    n = len(arrs)

    def body(*refs):
        in_refs, out_refs, stage, sems = refs[:n], refs[n + 1:2 * n + 1], refs[2 * n + 1:3 * n + 1], refs[3 * n + 1]
        me = _linear(_mesh_position())
        copies = []
        for a in range(n):
            stage[a][...] = in_refs[a][...].astype(BF16)
            copies.append(pltpu.make_async_copy(stage[a], out_refs[a].at[me], sems.at[a]))
            copies[-1].start()
        for cp in copies:
            cp.wait()

    return list(pl.pallas_call(
        body, name=name,
        out_shape=[jax.ShapeDtypeStruct((N_DEV,) + a.shape, BF16) for a in arrs],
        in_specs=[pl.BlockSpec(memory_space=pltpu.VMEM)] * n + [pl.BlockSpec(memory_space=pl.ANY)],
        out_specs=[pl.BlockSpec(memory_space=pl.ANY)] * n,
        scratch_shapes=[pltpu.VMEM(a.shape, BF16) for a in arrs] + [pltpu.SemaphoreType.DMA((n,))],
        compiler_params=_params(vmem=VMEM_LIMIT_LARGE),
    )(*arrs, after))


def _same_core_peers():
    x, y, c = _mesh_position()
    return [(x, y, 1 - c), (1 - x, y, c), (x, 1 - y, c), (1 - x, 1 - y, c)]


def _gather_start(bufs, name):
    n = len(bufs)

    def body(*refs):
        buf_refs, rest = refs[:n], refs[n:]
        sems, token = rest[:2 * n], rest[-1]
        me = _linear(_mesh_position())
        for a in range(n):
            for k, peer in enumerate(_same_core_peers()):
                pltpu.make_async_remote_copy(
                    src_ref=buf_refs[a].at[me], dst_ref=buf_refs[a].at[me],
                    send_sem=sems[2 * a].at[k], recv_sem=sems[2 * a + 1].at[k],
                    device_id=peer, device_id_type=MESH_ID).start()
        token[...] = jnp.zeros_like(token)

    outs = pl.pallas_call(
        body, name=name,
        out_shape=tuple(pltpu.SemaphoreType.DMA((4,)) for _ in range(2 * n))
        + tuple(pltpu.HBM(b.shape, b.dtype) for b in bufs) + (jax.ShapeDtypeStruct((8, 128), F32),),
        in_specs=(HBM_SPEC,) * n,
        out_specs=(SEM_SPEC,) * (2 * n) + (HBM_SPEC,) * n + (pl.BlockSpec(memory_space=pltpu.VMEM),),
        input_output_aliases={a: 2 * n + a for a in range(n)},
        compiler_params=pltpu.CompilerParams(has_side_effects=DATAFLOW),
    )(*[pltpu.with_memory_space_constraint(b, pltpu.HBM) for b in bufs])
    return outs[:2 * n], outs[2 * n:3 * n], outs[3 * n]


def _gather_wait(sems, bufs, after, name):
    n = len(bufs)

    def body(*refs):
        buf_refs, sem_refs = refs[:n], refs[n:3 * n]
        x, y, c = _mesh_position()
        me = _linear((x, y, c))
        for a in range(n):
            for k, peer in enumerate(_same_core_peers()):
                cp = pltpu.make_async_remote_copy(
                    src_ref=buf_refs[a].at[me], dst_ref=buf_refs[a].at[_linear(peer)],
                    send_sem=sem_refs[2 * a].at[k], recv_sem=sem_refs[2 * a + 1].at[k],
                    device_id=peer, device_id_type=MESH_ID)
                cp.wait_send()
                cp.wait_recv()

    return list(pl.pallas_call(
        body, name=name,
        out_shape=tuple(pltpu.HBM(b.shape, b.dtype) for b in bufs),
        in_specs=(HBM_SPEC,) * n + (SEM_SPEC,) * (2 * n) + (pl.BlockSpec(memory_space=pl.ANY),) * len(after),
        out_specs=(HBM_SPEC,) * n, input_output_aliases={a: a for a in range(n)},
        compiler_params=pltpu.CompilerParams(has_side_effects=DATAFLOW),
    )(*bufs, *sems, *after))


def _gather_pass_on(bufs, name):
    n = len(bufs)

    def body(*refs):
        out_refs = refs[n:2 * n]
        send_sems, recv_sems = refs[2 * n:]
        x, y, c = _mesh_position()
        sibling = (x, y, 1 - c)
        chips = [(1 - x, y), (x, 1 - y), (1 - x, 1 - y)]
        copies = []
        for a in range(n):
            for j, chip in enumerate(chips):
                block = out_refs[a].at[_linear((*chip, c))]
                copies.append(pltpu.make_async_remote_copy(
                    src_ref=block, dst_ref=block, send_sem=send_sems.at[3 * a + j], recv_sem=recv_sems.at[3 * a + j],
                    device_id=sibling, device_id_type=MESH_ID))
                copies[-1].start()
        for a in range(n):
            for j, chip in enumerate(chips):
                copies[3 * a + j].wait_send()
                theirs = out_refs[a].at[_linear((*chip, 1 - c))]
                pltpu.make_async_remote_copy(
                    src_ref=theirs, dst_ref=theirs, send_sem=send_sems.at[3 * a + j], recv_sem=recv_sems.at[3 * a + j],
                    device_id=sibling, device_id_type=MESH_ID).wait_recv()

    hbm = pl.BlockSpec(memory_space=pl.ANY)
    return list(pl.pallas_call(
        body, name=name,
        out_shape=[jax.ShapeDtypeStruct(b.shape, b.dtype) for b in bufs],
        in_specs=[hbm] * n, out_specs=[hbm] * n, input_output_aliases={a: a for a in range(n)},
        scratch_shapes=[pltpu.SemaphoreType.DMA((3 * n,)), pltpu.SemaphoreType.DMA((3 * n,))],
    )(*bufs))


def _silu_rows(c):
    def body(c_ref, o_ref):
        v = c_ref[...]
        o_ref[...] = v * _sigmoid(v)

    return pl.pallas_call(body, name="cond_silu", out_shape=jax.ShapeDtypeStruct(c.shape, F32))(c)


def _mod_columns(cond_all, w_ada, b_cols):
    def body(c_ref, w_ref, b_ref, o_ref):
        o_ref[...] = _dot(c_ref[...], w_ref[...]) + b_ref[...]

    return pl.pallas_call(body, name="mod_columns",
                          out_shape=jax.ShapeDtypeStruct((N_DEV, w_ada.shape[1]), F32))(cond_all, w_ada, b_cols)


def _in_proj(x, mod, g_norm1, w_in, tm):
    s = x.shape[0]

    def body(x_ref, mod_ref, g_ref, w_ref, h_ref, q_ref, kv_ref, gb_ref, gc_ref, xc_ref):
        xf = x_ref[...]
        n = xf * _rsqrt_mean_sq(xf) * g_ref[...]
        h = (n * (1.0 + mod_ref[SC1:SC1 + 1, :]) + mod_ref[SH1:SH1 + 1, :]).astype(BF16)
        h_ref[...] = h
        p = _dot_nt(h, w_ref[...])
        q_ref[...] = p[:, 0:512].astype(BF16)
        kv_ref[...] = p[:, 512:768].astype(BF16)
        gb_ref[...] = p[:, 768:1280].astype(BF16)
        gc_ref[...] = p[:, 1280:1792].astype(BF16)
        xc_ref[...] = p[:, 1792:2304].astype(BF16)

    return pl.pallas_call(
        body, name="in_proj", grid=(s // tm,),
        in_specs=[_rows(tm, D_MODEL), _full((8, D_MODEL)), _full((1, D_MODEL)), _full((IN_PROJ_WIDTH, D_MODEL))],
        out_specs=[_rows(tm, D_MODEL), _rows(tm, 512), _rows(tm, 256), _rows(tm, 512), _rows(tm, 512), _rows(tm, 512)],
        out_shape=[jax.ShapeDtypeStruct((s, D_MODEL), BF16), jax.ShapeDtypeStruct((s, 512), BF16),
                   jax.ShapeDtypeStruct((s, 256), BF16), jax.ShapeDtypeStruct((s, 512), BF16),
                   jax.ShapeDtypeStruct((s, 512), BF16), jax.ShapeDtypeStruct((s, 512), BF16)],
        compiler_params=_params(("arbitrary",), VMEM_LIMIT_LARGE),
    )(x, mod, g_norm1, w_in)


def _t5_bucket(dist):
    max_exact = N_BUCKETS // 2
    is_small = dist < max_exact
    d = jnp.maximum(dist, 1).astype(F32)
    large = max_exact + (jnp.log(d / max_exact) / math.log(MAX_DISTANCE / max_exact)
                         * (N_BUCKETS - max_exact)).astype(jnp.int32)
    large = jnp.minimum(large, N_BUCKETS - 1)
    return jnp.where(is_small, dist, large)


def _bucket_table():
    qi = jnp.arange(BLOCK, dtype=jnp.int32)[:, None]
    sj = jnp.arange(2 * BLOCK, dtype=jnp.int32)[None, :]
    return _t5_bucket(jnp.maximum(qi + BLOCK - sj, 0))


def _window_mask():
    qi = lax.broadcasted_iota(jnp.int32, (BLOCK, 2 * BLOCK), 0)
    sj = lax.broadcasted_iota(jnp.int32, (BLOCK, 2 * BLOCK), 1)
    dist = qi + BLOCK - sj
    return (dist >= 0) & (dist < BLOCK)


def _bias_table(rel_bias, bucket):
    def body(rb_ref, bk_ref, o_ref):
        bk = bk_ref[...]
        inside = _window_mask()
        for h in range(N_Q_HEADS):
            acc = jnp.zeros((BLOCK, 2 * BLOCK), F32)
            for b in range(N_BUCKETS):
                acc = jnp.where(bk == b, rb_ref[b, h], acc)
            o_ref[h] = jnp.where(inside, acc, NEG_INF)

    return pl.pallas_call(
        body, name="bias_table",
        in_specs=[pl.BlockSpec(memory_space=pltpu.SMEM), pl.BlockSpec(memory_space=pltpu.VMEM)],
        out_shape=jax.ShapeDtypeStruct((N_Q_HEADS, BLOCK, 2 * BLOCK), F32),
    )(rel_bias, bucket)


def _load_kv_window(kv_ref, n):
    prev = jnp.maximum(n - 1, 0)
    kvw = jnp.concatenate([kv_ref[pl.ds(pl.multiple_of(prev * BLOCK, BLOCK), BLOCK), :],
                           kv_ref[pl.ds(pl.multiple_of(n * BLOCK, BLOCK), BLOCK), :]], axis=0)
    k, v = kvw[:, 0:128], kvw[:, 128:256]
    k_sw = pltpu.roll(k.astype(F32), 64, 1).astype(BF16)
    v_sw = pltpu.roll(v.astype(F32), 64, 1).astype(BF16)
    return (k, k_sw), (v, v_sw)


def _conv_taps(gc, xc, gc_prev, xc_prev, n):
    u = gc * xc
    before = jnp.where(n > 0, gc_prev.astype(F32) * xc_prev.astype(F32), 0.0)
    last = before.shape[0] - 1
    row = lax.broadcasted_iota(jnp.int32, u.shape, 0)
    u1 = jnp.where(row == 0, before[last:last + 1, :], pltpu.roll(u, 1, 0))
    u2 = jnp.where(row == 0, before[last - 1:last, :],
                   jnp.where(row == 1, before[last:last + 1, :], pltpu.roll(u, 2, 0)))
    return u, u1, u2


def _mixer_fwd(q, kv, gb, gc, xc, bias, sinks, conv_w, g_attn, g_conv):
    s = q.shape[0]
    nb = s // BLOCK

    per_step = min(MIXER_BLOCKS, nb)
    tile = per_step * BLOCK

    def one_block(n, rows, before, sink_ref, q_ref, kv_ref, gb_ref, gc_ref, xc_ref, bias_ref, cw_ref, ga_ref,
                  gcv_ref, attn_ref, merged_ref, lse_ref):
        ks, vs = _load_kv_window(kv_ref, n)
        lane = lax.broadcasted_iota(jnp.int32, (BLOCK, BLOCK), 1)
        low = lane < HEAD_DIM
        col = lax.broadcasted_iota(jnp.int32, (BLOCK, 2 * BLOCK), 1)
        no_prev = (col < BLOCK) & (n == 0)
        lse_all = jnp.zeros((BLOCK, BLOCK), F32)
        pairs = []
        for p in range(4):
            qp = q_ref[rows, 128 * p:128 * (p + 1)].astype(F32)
            kvh = p // 2
            res = []
            for e in range(2):
                h = 2 * p + e
                qm = jnp.where(low if e == 0 else ~low, qp, 0.0).astype(BF16)
                sw = 0 if kvh == e else 1
                sc = _dot_nt(qm, ks[sw]) * SCALE + bias_ref[h]
                sc = jnp.where(no_prev, NEG_INF, sc)
                sink = sink_ref[h]
                m = jnp.maximum(jnp.max(sc, axis=-1, keepdims=True), sink)
                pe = jnp.exp(sc - m)
                den = jnp.sum(pe, axis=-1, keepdims=True) + jnp.exp(sink - m)
                res.append(_dot(pe.astype(BF16), vs[sw]) / den)
                lse_all = lse_all + jnp.where(lane == h, m + jnp.log(den), 0.0)
            pairs.append(jnp.where(low, res[0], res[1]))
        attn = jnp.concatenate(pairs, axis=1)
        attn_ref[rows, :] = attn
        lse_ref[rows, :] = lse_all
        u, u1, u2 = _conv_taps(gc_ref[rows, :].astype(F32), xc_ref[rows, :].astype(F32), before[0], before[1], n)
        cw = cw_ref[...]
        cv = gb_ref[rows, :].astype(F32) * (cw[0:1, :] * u2 + cw[1:2, :] * u1 + cw[2:3, :] * u)
        an = attn * _rsqrt_mean_sq(attn) * ga_ref[...]
        cn = cv * _rsqrt_mean_sq(cv) * gcv_ref[...]
        merged_ref[rows, :] = jnp.concatenate([an, cn], axis=1).astype(BF16)

    def body(sink_ref, q_ref, kv_ref, gb_ref, gc_ref, xc_ref, gcp_ref, xcp_ref, *rest):
        step = pl.program_id(0)
        for sub in range(per_step):
            rows = slice(sub * BLOCK, (sub + 1) * BLOCK)
            ahead = slice(sub * BLOCK - PREV_ROWS, sub * BLOCK)
            before = (gcp_ref[...], xcp_ref[...]) if sub == 0 else (gc_ref[ahead, :], xc_ref[ahead, :])
            one_block(step * per_step + sub, rows, before, sink_ref, q_ref, kv_ref, gb_ref, gc_ref, xc_ref, *rest)

    blk = lambda w: pl.BlockSpec((tile, w), lambda n: (n, 0))
    prev8 = pl.BlockSpec((PREV_ROWS, 512), lambda n: (jnp.maximum(n * (tile // PREV_ROWS) - 1, 0), 0))
    return pl.pallas_call(
        body, name="mixer_fwd", grid=(nb // per_step,),
        in_specs=[pl.BlockSpec(memory_space=pltpu.SMEM), blk(512), _full((s, 256)), blk(512), blk(512), blk(512),
                  prev8, prev8, _full((N_Q_HEADS, BLOCK, 2 * BLOCK)), _full((3, 512)), _full((1, 512)),
                  _full((1, 512))],
        out_specs=[blk(512), blk(1024), blk(128)],
        out_shape=[jax.ShapeDtypeStruct((s, 512), F32), jax.ShapeDtypeStruct((s, 1024), BF16),
                   jax.ShapeDtypeStruct((s, 128), F32)],
        compiler_params=_params(("arbitrary",)),
    )(sinks, q, kv, gb, gc, xc, gc, xc, bias, conv_w, g_attn, g_conv)


def _out_proj(merged, x, mod, w_out, tm):
    s = x.shape[0]

    def body(m_ref, x_ref, mod_ref, w_ref, o_ref, x1_ref):
        o = _dot(m_ref[...], w_ref[...])
        o_ref[...] = o.astype(BF16)
        x1_ref[...] = x_ref[...] + mod_ref[G1:G1 + 1, :] * o

    return pl.pallas_call(
        body, name="out_proj", grid=(s // tm,),
        in_specs=[_rows(tm, D_MODEL), _rows(tm, D_MODEL), _full((8, D_MODEL)), _full((D_MODEL, D_MODEL))],
        out_specs=[_rows(tm, D_MODEL), _rows(tm, D_MODEL)],
        out_shape=[jax.ShapeDtypeStruct((s, D_MODEL), BF16), jax.ShapeDtypeStruct((s, D_MODEL), F32)],
        compiler_params=_params(("arbitrary",)),
    )(merged, x, mod, w_out)


def _resident(shape):
    nd = len(shape)
    return pl.BlockSpec(shape, lambda *_: (0,) * nd, pipeline_mode=pl.Buffered(1))


def _ffn_fwd(x1, mod, g_norm2, w_gu, w_down, g_final, target, tm):
    s = x1.shape[0]
    chunk = D_FF // FFN_CHUNKS

    def body(x_ref, mod_ref, g_ref, wgu_ref, wd_ref, gf_ref, t_ref,
             h_ref, gate_ref, up_ref, act_ref, o_ref, dx2_ref, small_ref):
        @pl.when(pl.program_id(0) == 0)
        def _():
            small_ref[...] = jnp.zeros_like(small_ref)

        xf = x_ref[...]
        n = xf * _rsqrt_mean_sq(xf) * g_ref[...]
        h = (n * (1.0 + mod_ref[SC2:SC2 + 1, :]) + mod_ref[SH2:SH2 + 1, :]).astype(BF16)
        h_ref[...] = h
        o = None
        for j in range(FFN_CHUNKS):
            lo = j * chunk
            gate = _dot_nt(h, wgu_ref[lo:lo + chunk, :])
            up = _dot_nt(h, wgu_ref[D_FF + lo:D_FF + lo + chunk, :])
            gate_ref[:, lo:lo + chunk] = gate.astype(BF16)
            up_ref[:, lo:lo + chunk] = up.astype(BF16)
            act = (gate * _sigmoid(gate) * up).astype(BF16)
            act_ref[:, lo:lo + chunk] = act
            part = _dot(act, wd_ref[lo:lo + chunk, :])
            o = part if o is None else o + part
        o_ref[...] = o.astype(BF16)
        x2 = xf + mod_ref[G2:G2 + 1, :] * o
        r = _rsqrt_mean_sq(x2)
        xn = x2 * r
        gf = gf_ref[...]
        err = xn * gf - t_ref[...]
        dy = err * (1.0 / D_MODEL)
        dxn = dy * gf
        dx2_ref[...] = (r * (dxn - xn * jnp.mean(dxn * xn, axis=-1, keepdims=True))).astype(BF16)
        small_ref[0:1, :] += _colsum(dy * xn)
        small_ref[1:2, :] += _colsum(err * err)

        @pl.when(pl.program_id(0) == pl.num_programs(0) - 1)
        def _():
            total = jnp.sum(small_ref[1:2, :], axis=-1, keepdims=True) * (0.5 / D_MODEL)
            small_ref[2:3, :] = jnp.broadcast_to(total, (1, D_MODEL))

    wide = jax.ShapeDtypeStruct((s, D_FF), BF16)
    return pl.pallas_call(
        body, name="ffn_fwd", grid=(s // tm,),
        in_specs=[_rows(tm, D_MODEL), _full((8, D_MODEL)), _full((1, D_MODEL)), _resident((2 * D_FF, D_MODEL)),
                  _resident((D_FF, D_MODEL)), _full((1, D_MODEL)), _rows(tm, D_MODEL)],
        out_specs=[_rows(tm, D_MODEL), _rows(tm, D_FF), _rows(tm, D_FF), _rows(tm, D_FF), _rows(tm, D_MODEL),
                   _rows(tm, D_MODEL), _full((8, D_MODEL))],
        out_shape=[jax.ShapeDtypeStruct((s, D_MODEL), BF16), wide, wide, wide,
                   jax.ShapeDtypeStruct((s, D_MODEL), BF16), jax.ShapeDtypeStruct((s, D_MODEL), BF16),
                   jax.ShapeDtypeStruct((8, D_MODEL), F32)],
        compiler_params=_params(("arbitrary",), VMEM_LIMIT_LARGE),
    )(x1, mod, g_norm2, w_gu, w_down, g_final, target)


def _ffn_bwd(dx2, o2, gate, up, x1, mod, g_norm2, w_down, w_gu, tm):
    s = x1.shape[0]
    chunk = D_FF // FFN_CHUNKS

    def body(dx_ref, o_ref, gate_ref, up_ref, x_ref, mod_ref, g_ref, wd_ref, wgu_ref,
             do_ref, dgu_ref, dx1_ref, small_ref):
        @pl.when(pl.program_id(0) == 0)
        def _():
            small_ref[...] = jnp.zeros_like(small_ref)

        dx = dx_ref[...].astype(F32)
        small_ref[3:4, :] += _colsum(dx * o_ref[...].astype(F32))
        do = (dx * mod_ref[G2:G2 + 1, :]).astype(BF16)
        do_ref[...] = do
        dh = None
        for j in range(FFN_CHUNKS):
            lo = j * chunk
            dact = _dot_nt(do, wd_ref[lo:lo + chunk, :])
            gate = gate_ref[:, lo:lo + chunk].astype(F32)
            sg = _sigmoid(gate)
            dgate = (dact * up_ref[:, lo:lo + chunk].astype(F32) * (sg * (1.0 + gate * (1.0 - sg)))).astype(BF16)
            dup = (dact * (gate * sg)).astype(BF16)
            dgu_ref[:, lo:lo + chunk] = dgate
            dgu_ref[:, D_FF + lo:D_FF + lo + chunk] = dup
            part = _dot(dgate, wgu_ref[lo:lo + chunk, :]) + _dot(dup, wgu_ref[D_FF + lo:D_FF + lo + chunk, :])
            dh = part if dh is None else dh + part
        dx1 = dx + _norm_mod_bwd(dh, x_ref[...], g_ref[...], mod_ref[SC2:SC2 + 1, :], small_ref)
        dx1_ref[...] = dx1.astype(BF16)

    return pl.pallas_call(
        body, name="ffn_bwd", grid=(s // tm,),
        in_specs=[_rows(tm, D_MODEL), _rows(tm, D_MODEL), _rows(tm, D_FF), _rows(tm, D_FF), _rows(tm, D_MODEL),
                  _full((8, D_MODEL)), _full((1, D_MODEL)), _resident((D_FF, D_MODEL)),
                  _resident((2 * D_FF, D_MODEL))],
        out_specs=[_rows(tm, D_MODEL), _rows(tm, 2 * D_FF), _rows(tm, D_MODEL), _full((8, D_MODEL))],
        out_shape=[jax.ShapeDtypeStruct((s, D_MODEL), BF16), jax.ShapeDtypeStruct((s, 2 * D_FF), BF16),
                   jax.ShapeDtypeStruct((s, D_MODEL), BF16), jax.ShapeDtypeStruct((8, D_MODEL), F32)],
        compiler_params=_params(("arbitrary",), VMEM_LIMIT_LARGE),
    )(dx2, o2, gate, up, x1, mod, g_norm2, w_down, w_gu)


def _norm_mod_bwd(dh, xf, g, scale_row, small_ref):
    r = _rsqrt_mean_sq(xf)
    xn = xf * r
    small_ref[0:1, :] += _colsum(dh)
    small_ref[1:2, :] += _colsum(dh * (xn * g))
    dn = dh * (1.0 + scale_row)
    small_ref[2:3, :] += _colsum(dn * xn)
    dxn = dn * g
    return r * (dxn - xn * jnp.mean(dxn * xn, axis=-1, keepdims=True))


def _out_proj_bwd(dx1, o1, mod, w_out, tm):
    s = dx1.shape[0]

    def body(dx_ref, o_ref, mod_ref, w_ref, do_ref, dm_ref, small_ref):
        @pl.when(pl.program_id(0) == 0)
        def _():
            small_ref[...] = jnp.zeros_like(small_ref)

        dx = dx_ref[...].astype(F32)
        small_ref[0:1, :] += _colsum(dx * o_ref[...].astype(F32))
        do = (dx * mod_ref[G1:G1 + 1, :]).astype(BF16)
        do_ref[...] = do
        dm_ref[...] = _dot_nt(do, w_ref[...]).astype(BF16)

    return pl.pallas_call(
        body, name="out_proj_bwd", grid=(s // tm,),
        in_specs=[_rows(tm, D_MODEL), _rows(tm, D_MODEL), _full((8, D_MODEL)), _full((D_MODEL, D_MODEL))],
        out_specs=[_rows(tm, D_MODEL), _rows(tm, D_MODEL), _full((8, D_MODEL))],
        out_shape=[jax.ShapeDtypeStruct((s, D_MODEL), BF16), jax.ShapeDtypeStruct((s, D_MODEL), BF16),
                   jax.ShapeDtypeStruct((8, D_MODEL), F32)],
        compiler_params=_params(("arbitrary",)),
    )(dx1, o1, mod, w_out)


def _group_norm_bwd(dm, a, g):
    r = _rsqrt_mean_sq(a)
    an = a * r
    dan = dm * g
    return r * (dan - an * jnp.mean(dan * an, axis=-1, keepdims=True)), _colsum(dm * an)


def _mixer_bwd(q, kv, gb, gc, xc, bias, sinks, conv_w, g_attn, g_conv, attn, lse, dmerged):
    s = q.shape[0]
    nb = s // BLOCK

    per_step = min(MIXER_BLOCKS, nb)
    tile = per_step * BLOCK
    steps = nb // per_step

    def one_block(n, rows, before, nxt, sink_ref, q_ref, kv_ref, gb_ref, gc_ref, xc_ref, bias_ref, cw_ref, ga_ref,
                  gcv_ref, attn_ref, lse_ref, dm_ref,
                  dq_ref, dkv_ref, dgb_ref, dgc_ref, dxc_ref, dbias_ref, dsink_ref, small_ref):
        dm = dm_ref[rows, :].astype(F32)
        gbv, gcv_, xcv = gb_ref[rows, :].astype(F32), gc_ref[rows, :].astype(F32), xc_ref[rows, :].astype(F32)
        u, u1, u2 = _conv_taps(gcv_, xcv, before[0], before[1], n)
        cw = cw_ref[...]
        yv = cw[0:1, :] * u2 + cw[1:2, :] * u1 + cw[2:3, :] * u
        dcv, dg_conv = _group_norm_bwd(dm[:, 512:1024], gbv * yv, gcv_ref[...])
        small_ref[1:2, :] += dg_conv
        dgb_ref[rows, :] = (dcv * yv).astype(BF16)
        dy = dcv * gbv
        row = lax.broadcasted_iota(jnp.int32, dy.shape, 0)
        d1 = jnp.where(row == BLOCK - 1, nxt[0:1, :], pltpu.roll(dy, BLOCK - 1, 0))
        d2 = jnp.where(row == BLOCK - 2, nxt[0:1, :],
                       jnp.where(row == BLOCK - 1, nxt[1:2, :], pltpu.roll(dy, BLOCK - 2, 0)))
        du = cw[2:3, :] * dy + cw[1:2, :] * d1 + cw[0:1, :] * d2
        dgc_ref[rows, :] = (du * xcv).astype(BF16)
        dxc_ref[rows, :] = (du * gcv_).astype(BF16)
        small_ref[2:3, :] += _colsum(dy * u2)
        small_ref[3:4, :] += _colsum(dy * u1)
        small_ref[4:5, :] += _colsum(dy * u)

        attn_v = attn_ref[rows, :]
        dout, dg_attn = _group_norm_bwd(dm[:, 0:512], attn_v, ga_ref[...])
        small_ref[0:1, :] += dg_attn
        ks, vs = _load_kv_window(kv_ref, n)
        lane = lax.broadcasted_iota(jnp.int32, (BLOCK, BLOCK), 1)
        low = lane < HEAD_DIM
        col = lax.broadcasted_iota(jnp.int32, (BLOCK, 2 * BLOCK), 1)
        no_prev = (col < BLOCK) & (n == 0)
        lse_all = lse_ref[rows, :]
        dsink = jnp.zeros((BLOCK, BLOCK), F32)
        dq_pairs = []
        dk_groups, dv_groups = [], []
        for kvh in range(2):
            ds_rows, pr_rows, q_rows, do_rows = [], [], [], []
            for p in (2 * kvh, 2 * kvh + 1):
                qp = q_ref[rows, 128 * p:128 * (p + 1)].astype(F32)
                do_p = dout[:, 128 * p:128 * (p + 1)]
                prod = do_p * attn_v[:, 128 * p:128 * (p + 1)]
                res = []
                for e in range(2):
                    h = 2 * p + e
                    half = low if e == 0 else ~low
                    qm = jnp.where(half, qp, 0.0).astype(BF16)
                    dom = jnp.where(half, do_p, 0.0).astype(BF16)
                    delta = jnp.sum(jnp.where(half, prod, 0.0), axis=-1, keepdims=True)
                    lse_h = jnp.sum(jnp.where(lane == h, lse_all, 0.0), axis=-1, keepdims=True)
                    sw = 0 if kvh == e else 1
                    sc = _dot_nt(qm, ks[sw]) * SCALE + bias_ref[h]
                    sc = jnp.where(no_prev, NEG_INF, sc)
                    pr = jnp.exp(sc - lse_h)
                    dp = _dot_nt(dom, vs[sw])
                    ds = pr * (dp - delta)
                    dbias_ref[h] += ds
                    dsink = dsink + jnp.where(lane == h, -jnp.exp(sink_ref[h] - lse_h) * delta, 0.0)
                    dsb = ds.astype(BF16)
                    res.append(_dot(dsb, ks[sw]) * SCALE)
                    ds_rows.append(dsb)
                    pr_rows.append(pr.astype(BF16))
                    q_rows.append(qm)
                    do_rows.append(dom)
                dq_pairs.append(jnp.where(low, res[0], res[1]))
            dk_g = _dot_tn(jnp.concatenate(ds_rows, axis=0), jnp.concatenate(q_rows, axis=0)) * SCALE
            dv_g = _dot_tn(jnp.concatenate(pr_rows, axis=0), jnp.concatenate(do_rows, axis=0))
            dk_groups.append(dk_g + pltpu.roll(dk_g, 64, 1))
            dv_groups.append(dv_g + pltpu.roll(dv_g, 64, 1))
        dq_ref[rows, :] = jnp.concatenate(dq_pairs, axis=1).astype(BF16)
        dsink_ref[...] += dsink
        low_kv = lax.broadcasted_iota(jnp.int32, (2 * BLOCK, BLOCK), 1) < HEAD_DIM
        dkv_win = jnp.concatenate([jnp.where(low_kv, dk_groups[0], dk_groups[1]),
                                   jnp.where(low_kv, dv_groups[0], dv_groups[1])], axis=1)
        prev = jnp.maximum(n - 1, 0)
        dkv_ref[pl.ds(pl.multiple_of(prev * BLOCK, BLOCK), BLOCK), :] += dkv_win[0:BLOCK, :]
        dkv_ref[pl.ds(pl.multiple_of(n * BLOCK, BLOCK), BLOCK), :] += dkv_win[BLOCK:2 * BLOCK, :]

        return dy[0:8, :]

    def body(sink_ref, q_ref, kv_ref, gb_ref, gc_ref, xc_ref, gcp_ref, xcp_ref, *rest):
        refs, carry_ref = rest[:-1], rest[-1]
        dkv_ref, dbias_ref, dsink_ref, small_ref = refs[8], refs[12], refs[13], refs[14]
        step = pl.program_id(0)

        @pl.when(step == 0)
        def _():
            dkv_ref[...] = jnp.zeros_like(dkv_ref)
            dbias_ref[...] = jnp.zeros_like(dbias_ref)
            dsink_ref[...] = jnp.zeros_like(dsink_ref)
            small_ref[...] = jnp.zeros_like(small_ref)
            carry_ref[...] = jnp.zeros_like(carry_ref)

        nxt = carry_ref[...]
        for sub in reversed(range(per_step)):
            rows = slice(sub * BLOCK, (sub + 1) * BLOCK)
            ahead = slice(sub * BLOCK - PREV_ROWS, sub * BLOCK)
            before = (gcp_ref[...], xcp_ref[...]) if sub == 0 else (gc_ref[ahead, :], xc_ref[ahead, :])
            nxt = one_block((steps - 1 - step) * per_step + sub, rows, before, nxt,
                            sink_ref, q_ref, kv_ref, gb_ref, gc_ref, xc_ref, *refs)
        carry_ref[...] = nxt

        @pl.when(step == steps - 1)
        def _():
            small_ref[5:6, :] = jnp.concatenate([_colsum(dsink_ref[...]), jnp.zeros((1, 512 - BLOCK), F32)], axis=1)

    blk = lambda w: pl.BlockSpec((tile, w), lambda t: (steps - 1 - t, 0))
    prev8 = pl.BlockSpec((PREV_ROWS, 512),
                         lambda t: (jnp.maximum((steps - 1 - t) * (tile // PREV_ROWS) - 1, 0), 0))
    bf = lambda w: jax.ShapeDtypeStruct((s, w), BF16)
    return pl.pallas_call(
        body, name="mixer_bwd", grid=(steps,),
        in_specs=[pl.BlockSpec(memory_space=pltpu.SMEM), blk(512), _full((s, 256)), blk(512), blk(512), blk(512),
                  prev8, prev8, _full((N_Q_HEADS, BLOCK, 2 * BLOCK)), _full((3, 512)), _full((1, 512)),
                  _full((1, 512)), blk(512), blk(128), blk(1024)],
        out_specs=[blk(512), _full((s, 256)), blk(512), blk(512), blk(512), _full((N_Q_HEADS, BLOCK, 2 * BLOCK)),
                   _full((BLOCK, BLOCK)), _full((8, 512))],
        out_shape=[bf(512), jax.ShapeDtypeStruct((s, 256), F32), bf(512), bf(512), bf(512),
                   jax.ShapeDtypeStruct((N_Q_HEADS, BLOCK, 2 * BLOCK), F32), jax.ShapeDtypeStruct((BLOCK, BLOCK), F32),
                   jax.ShapeDtypeStruct((8, 512), F32)],
        scratch_shapes=[pltpu.VMEM((8, 512), F32)],
        compiler_params=_params(("arbitrary",), VMEM_LIMIT_LARGE),
    )(sinks, q, kv, gb, gc, xc, gc, xc, bias, conv_w, g_attn, g_conv, attn, lse, dmerged)


def _in_proj_bwd(dq, dkv, dgb, dgc, dxc, x, dx1, mod, g_norm1, w_in, tm):
    s = x.shape[0]

    def body(dq_ref, dkv_ref, dgb_ref, dgc_ref, dxc_ref, x_ref, dx1_ref, mod_ref, g_ref, w_ref,
             dproj_ref, dx_ref, small_ref):
        @pl.when(pl.program_id(0) == 0)
        def _():
            small_ref[...] = jnp.zeros_like(small_ref)

        dproj = jnp.concatenate([dq_ref[...], dkv_ref[...].astype(BF16), dgb_ref[...], dgc_ref[...], dxc_ref[...]],
                                axis=1)
        dproj_ref[...] = dproj
        dh = _dot(dproj, w_ref[...])
        dx_ref[...] = dx1_ref[...].astype(F32) + _norm_mod_bwd(dh, x_ref[...], g_ref[...], mod_ref[SC1:SC1 + 1, :], small_ref)

    return pl.pallas_call(
        body, name="in_proj_bwd", grid=(s // tm,),
        in_specs=[_rows(tm, 512), _rows(tm, 256), _rows(tm, 512), _rows(tm, 512), _rows(tm, 512),
                  _rows(tm, D_MODEL), _rows(tm, D_MODEL), _full((8, D_MODEL)), _full((1, D_MODEL)),
                  _full((IN_PROJ_WIDTH, D_MODEL))],
        out_specs=[_rows(tm, IN_PROJ_WIDTH), _rows(tm, D_MODEL), _full((8, D_MODEL))],
        out_shape=[jax.ShapeDtypeStruct((s, IN_PROJ_WIDTH), BF16), jax.ShapeDtypeStruct((s, D_MODEL), F32),
                   jax.ShapeDtypeStruct((8, D_MODEL), F32)],
        compiler_params=_params(("arbitrary",), VMEM_LIMIT_LARGE),
    )(dq, dkv, dgb, dgc, dxc, x, dx1, mod, g_norm1, w_in)


def _weight_grad(a, b, tk, ts, name, after=None):
    s, k = a.shape
    n = b.shape[1]
    nt = s // ts
    extra = [] if after is None else [after]

    def body(a_ref, b_ref, *rest):
        o_ref, acc_ref = rest[-2:]
        t = pl.program_id(1)
        part = _dot_tn(a_ref[...], b_ref[...])

        @pl.when(t == 0)
        def _():
            acc_ref[...] = part

        @pl.when(t > 0)
        def _():
            acc_ref[...] += part

        @pl.when(t == nt - 1)
        def _():
            o_ref[...] = acc_ref[...].astype(BF16)

    return pl.pallas_call(
        body, name=name, grid=(k // tk, nt),
        in_specs=[pl.BlockSpec((ts, tk), lambda i, t: (t, i)), pl.BlockSpec((ts, n), lambda i, t: (t, 0))]
        + [pl.BlockSpec(memory_space=pl.ANY)] * len(extra),
        out_specs=pl.BlockSpec((tk, n), lambda i, t: (i, 0)),
        out_shape=jax.ShapeDtypeStruct((k, n), BF16),
        scratch_shapes=[pltpu.VMEM((tk, n), F32)],
        compiler_params=_params(("arbitrary", "arbitrary"), VMEM_LIMIT_LARGE),
    )(a, b, *extra)


def _rel_bias_grad(dbias, bucket):
    def body(db_ref, bk_ref, o_ref, rows_ref):
        bk = bk_ref[...]
        for b in range(N_BUCKETS):
            sel = (bk == b).astype(F32)
            for h in range(N_Q_HEADS):
                rows_ref[N_BUCKETS * h + b:N_BUCKETS * h + b + 1, :] = _colsum(db_ref[h] * sel)
        head = lax.broadcasted_iota(jnp.int32, (N_BUCKETS, N_Q_HEADS), 1)
        out = jnp.zeros((N_BUCKETS, N_Q_HEADS), F32)
        for h in range(N_Q_HEADS):
            per_bucket = jnp.sum(rows_ref[N_BUCKETS * h:N_BUCKETS * (h + 1), :], axis=-1, keepdims=True)
            out = out + jnp.where(head == h, per_bucket, 0.0)
        o_ref[...] = out

    return pl.pallas_call(
        body, name="rel_bias_grad",
        out_shape=jax.ShapeDtypeStruct((N_BUCKETS, N_Q_HEADS), F32),
        scratch_shapes=[pltpu.VMEM((N_BUCKETS * N_Q_HEADS, 2 * BLOCK), F32)],
    )(dbias, bucket)


def _lanes_from(x, start, width):
    n = x.shape[1]
    return pltpu.roll(x, (n - start) % n, 1)[:, 0:width]


def _w_ada_grad(me, cond_all, packed_all, cols):
    def body(me_ref, c_ref, p_ref, o_ref):
        dmod = jnp.concatenate([p_ref[k][:, OFF_DMOD:OFF_DMOD + N_MOD * D_MODEL] for k in range(N_DEV)], axis=0)
        mine = _lanes_from(dmod, me_ref[0] * cols, cols)
        pad = lambda a: jnp.concatenate([a, jnp.zeros((128 - N_DEV, a.shape[1]), F32)], axis=0)
        o_ref[...] = _dot_tn(pad(c_ref[...]), pad(mine))

    vmem = pl.BlockSpec(memory_space=pltpu.VMEM)
    return pl.pallas_call(body, name="w_ada_grad",
                          in_specs=[pl.BlockSpec(memory_space=pltpu.SMEM), vmem, vmem],
                          out_shape=jax.ShapeDtypeStruct((cond_all.shape[1], cols), F32))(me, cond_all, packed_all)


SMALL_PARAMS = (("rel_bias", None), ("b_ada", (OFF_DMOD, N_MOD * D_MODEL)), ("g_norm1", (OFF_GN1, D_MODEL)),
                ("sinks", (OFF_SINK, N_Q_HEADS)), ("conv_w", None), ("g_attn_out", (OFF_GATT, ATTN_WIDTH)),
                ("g_conv_out", (OFF_GCV, CONV_WIDTH)), ("g_norm2", (OFF_GN2, D_MODEL)),
                ("g_final", (OFF_GFIN, D_MODEL)))


def _small_update(me, packed_all, rel_all, state, after):
    n_p = len(SMALL_PARAMS)
    flat = [a for triple in state for a in triple]
    conv_cols = state[4][0].shape[1]

    def body(me_ref, p_ref, r_ref, *refs):
        ins = refs[:3 * n_p]
        loss_ref, outs = refs[3 * n_p + len(after)], refs[3 * n_p + len(after) + 1:]
        small, rel = p_ref[0], r_ref[0]
        for k in range(1, N_DEV):
            small = small + p_ref[k]
            rel = rel + r_ref[k]
        loss_ref[...] = small[:, OFF_LOSS:OFF_LOSS + 128]
        taps = jnp.concatenate([small[:, OFF_CONVW + CONV_WIDTH * j:OFF_CONVW + CONV_WIDTH * (j + 1)]
                                for j in range(3)] + [jnp.zeros((5, CONV_WIDTH), F32)], axis=0)
        conv_g = _lanes_from(taps, me_ref[0] * conv_cols, conv_cols)[0:3, :]
        for i, (name, lanes) in enumerate(SMALL_PARAMS):
            g = rel if name == "rel_bias" else conv_g if name == "conv_w" else small[:, lanes[0]:lanes[0] + lanes[1]]
            w_ref, m_ref, v_ref = ins[3 * i:3 * i + 3]
            outs[4 * i][...] = g
            outs[4 * i + 1][...], outs[4 * i + 2][...], outs[4 * i + 3][...] = _adam_math(
                w_ref[...], g, m_ref[...], v_ref[...])

    vmem = pl.BlockSpec(memory_space=pltpu.VMEM)
    out_shape = [jax.ShapeDtypeStruct((1, 128), F32)]
    for w, _, _ in state:
        out_shape += [jax.ShapeDtypeStruct(w.shape, F32)] * 4
    outs = pl.pallas_call(
        body, name="small_update",
        in_specs=[pl.BlockSpec(memory_space=pltpu.SMEM), vmem, vmem] + [vmem] * len(flat)
        + [pl.BlockSpec(memory_space=pl.ANY)] * len(after),
        out_shape=out_shape,
    )(me, packed_all, rel_all, *flat, *after)
    return outs[0], [tuple(outs[1 + 4 * i:5 + 4 * i]) for i in range(n_p)]


def _adam_math(w, g, m, v):
    m = ADAM_B1 * m + (1.0 - ADAM_B1) * g
    v = ADAM_B2 * v + (1.0 - ADAM_B2) * (g * g)
    m_hat = m / (1.0 - ADAM_B1 ** ADAM_STEP)
    v_hat = v / (1.0 - ADAM_B2 ** ADAM_STEP)
    delta = -ADAM_LR * (m_hat / (jnp.sqrt(v_hat) + ADAM_EPS) + ADAM_WD * w)
    return delta, m, v


def _adamw_parts(w, m, v, local, land, me, tr, name):
    r, c = w.shape

    def body(me_ref, w_ref, m_ref, v_ref, own_ref, land_ref, g_ref, d_ref, mo_ref, vo_ref):
        g = own_ref[0].astype(F32)
        for k in range(N_DEV - 1):
            g = g + land_ref[k].astype(F32)
        g_ref[...] = g
        d_ref[...], mo_ref[...], vo_ref[...] = _adam_math(w_ref[...], g, m_ref[...], v_ref[...])

    tile = pl.BlockSpec((tr, c), lambda i, me_ref: (i, 0))
    return pl.pallas_call(
        body, name=name,
        grid_spec=pltpu.PrefetchScalarGridSpec(
            num_scalar_prefetch=1, grid=(r // tr,),
            in_specs=[tile, tile, tile, pl.BlockSpec((1, tr, c), lambda i, me_ref: (me_ref[0], i, 0)),
                      pl.BlockSpec((N_DEV - 1, tr, c), lambda i, me_ref: (0, i, 0))],
            out_specs=[tile] * 4),
        out_shape=[jax.ShapeDtypeStruct((r, c), F32)] * 4,
        compiler_params=_params(("arbitrary",)),
    )(me, w, m, v, local, land)


def _adamw(w, m, v, g, tr, name):
    r, c = w.shape

    def body(w_ref, m_ref, v_ref, g_ref, d_ref, mo_ref, vo_ref):
        d_ref[...], mo_ref[...], vo_ref[...] = _adam_math(w_ref[...], g_ref[...], m_ref[...], v_ref[...])

    tile = pl.BlockSpec((tr, c), lambda i: (i, 0))
    return pl.pallas_call(
        body, name=name, grid=(r // tr,),
        in_specs=[tile] * 4, out_specs=[tile] * 3,
        out_shape=[jax.ShapeDtypeStruct((r, c), F32)] * 3,
        compiler_params=_params(("arbitrary",)),
    )(w, m, v, g)


def _behind(a, token):
    return a + token[0:a.shape[0], 0:1]


def _local_step(x, target, mod, w_in_t, weights_out_gu, weights_down, rel_bias, g_norm1, sinks, conv_w, g_attn,
                g_conv, g_norm2, g_final, exchange):
    s = x.shape[0]
    tm = min(512, s)
    tm_small = min(256, s)
    bucket = _bucket_table()
    bias = _bias_table(rel_bias, bucket)

    h, q, kv, gb, gc, xc = _in_proj(x, mod, g_norm1, w_in_t, tm)
    attn, merged, lse = _mixer_fwd(q, kv, gb, gc, xc, bias, sinks, conv_w, g_attn, g_conv)
    w_out, w_gu_t = weights_out_gu(merged)
    o1, x1 = _out_proj(merged, x, mod, w_out, tm)
    w_down = weights_down(x1)
    h2, gate, up, act, o2, dx2, fin = _ffn_fwd(x1, mod, g_norm2, w_gu_t, w_down, g_final, target, tm_small)

    do2, dgu, dx1, sm_2 = _ffn_bwd(dx2, o2, gate, up, x1, mod, g_norm2, w_down, w_gu_t, tm_small)
    ts = min(WEIGHT_GRAD_ROWS, s)
    tok_down = exchange("w_down", _weight_grad(act, do2, D_FF // 2, ts, "w_down_grad"))
    mod = _behind(mod, exchange("w_gu", _weight_grad(dgu, h2, D_FF // 2, ts, "w_gu_grad", after=tok_down)))
    do1, dmerged, sm_g1 = _out_proj_bwd(dx1, o1, mod, w_out, tm)
    g_attn_b = _behind(g_attn, exchange("w_out", _weight_grad(merged, do1, D_MODEL, ts, "w_out_grad")))
    dq, dkv, dgb, dgc, dxc, dbias, dsink, sm_mix = _mixer_bwd(
        q, kv, gb, gc, xc, bias, sinks, conv_w, g_attn_b, g_conv, attn, lse, dmerged)
    dproj, dx, sm_1 = _in_proj_bwd(dq, dkv, dgb, dgc, dxc, x, dx1, mod, g_norm1, w_in_t, tm)
    d_rel = _rel_bias_grad(dbias, bucket)

    packed = jnp.concatenate([
        sm_1[0], sm_1[1], sm_g1[0], sm_2[0], sm_2[1], sm_2[3],
        sm_1[2],
        sm_mix[5, 0:128],
        sm_mix[0], sm_mix[1],
        sm_2[2],
        fin[0],
        sm_mix[2], sm_mix[3], sm_mix[4],
        fin[2, 0:128],
    ])[None, :]
    return dx, dproj, h, packed, d_rel


def kernel(x, c, rel_bias, w_ada, b_ada, g_norm1, w_in, sinks, conv_w, g_attn_out, g_conv_out, w_out, g_norm2, w_gu, w_down, g_final, loss_target, m_rel_bias, m_w_ada, m_b_ada, m_g_norm1, m_w_in, m_sinks, m_conv_w, m_g_attn_out, m_g_conv_out, m_w_out, m_g_norm2, m_w_gu, m_w_down, m_g_final, v_rel_bias, v_w_ada, v_b_ada, v_g_norm1, v_w_in, v_sinks, v_conv_w, v_g_attn_out, v_g_conv_out, v_w_out, v_g_norm2, v_w_gu, v_w_down, v_g_final):
    me = _linear(_mesh_position())
    me_arr = jnp.reshape(me, (1,)).astype(jnp.int32)
    ada_cols = w_ada.shape[2]
    tm = min(512, x.shape[1])

    cond = _silu_rows(c)
    cond_all, conv_w_all = _all_gather_small([cond, conv_w[0]], "gather_cond")
    cond_all = cond_all[:, 0, :]
    conv_cols = conv_w.shape[2]
    conv_w_full = conv_w_all.transpose(1, 0, 2).reshape(3, CONV_WIDTH)
    b_cols = lax.dynamic_slice_in_dim(b_ada, me * ada_cols, ada_cols, axis=1)
    mod_cols = _mod_columns(cond_all, w_ada[0], b_cols)
    mod_all = _all_gather_small([mod_cols], "gather_mod")[0]
    mod = lax.dynamic_index_in_dim(mod_all, me, axis=1, keepdims=False).reshape(N_MOD, D_MODEL)
    mod = jnp.concatenate([mod, jnp.zeros((2, D_MODEL), F32)], axis=0)

    w_in_t = _all_gather([w_in[0].T], "gather_w_in", to_bf16=True, big=True)[0].reshape(IN_PROJ_WIDTH, D_MODEL)
    gather_sems, staged, gather_token = _gather_start(
        _stage_blocks([w_out[0], w_gu[0].T, w_down[0]], w_in_t, "stage_weights"), "gather_start_weights")
    mod = _behind(mod, gather_token)

    def weights_out_gu(after):
        got = _gather_pass_on(_gather_wait(gather_sems[0:4], staged[0:2], [after], "gather_wait_out_gu"),
                              "gather_pass_on_out_gu")
        return got[0].reshape(D_MODEL, D_MODEL), got[1].reshape(2 * D_FF, D_MODEL)

    def weights_down(after):
        got = _gather_pass_on(_gather_wait(gather_sems[4:6], staged[2:3], [after], "gather_wait_down"),
                              "gather_pass_on_down")
        return got[0].reshape(D_FF, D_MODEL)

    started = {}

    def exchange(name, dw):
        st = _exchange_start(dw.reshape(N_DEV, dw.shape[0] // N_DEV, dw.shape[1]), "exchange_start_" + name)
        started[name] = st
        return st[4]

    dx, dproj, h, packed, d_rel = _local_step(
        x[0], loss_target[0], mod, w_in_t, weights_out_gu, weights_down, rel_bias, g_norm1, sinks[0], conv_w_full,
        g_attn_out, g_conv_out, g_norm2, g_final[None, :], exchange)

    packed_all, rel_all = _all_gather_small([packed, d_rel], "gather_small_grads")
    tok_in = exchange("w_in", _weight_grad(dproj, h, IN_PROJ_WIDTH // 2, min(WEIGHT_GRAD_ROWS, x.shape[1]),
                                           "w_in_grad", after=packed_all))
    g_ada = _w_ada_grad(me_arr, cond_all, packed_all, ada_cols)
    d_ada, nm_ada, nv_ada = _adamw(w_ada[0], m_w_ada[0], v_w_ada[0], g_ada, 256, "adamw_w_ada")
    as_rows = {"conv_w": lambda a: a[0], "g_final": lambda a: a[None, :]}
    small_state = {
        "rel_bias": (rel_bias, m_rel_bias, v_rel_bias), "b_ada": (b_ada, m_b_ada, v_b_ada),
        "g_norm1": (g_norm1, m_g_norm1, v_g_norm1), "sinks": (sinks, m_sinks, v_sinks),
        "conv_w": (conv_w, m_conv_w, v_conv_w), "g_attn_out": (g_attn_out, m_g_attn_out, v_g_attn_out),
        "g_conv_out": (g_conv_out, m_g_conv_out, v_g_conv_out), "g_norm2": (g_norm2, m_g_norm2, v_g_norm2),
        "g_final": (g_final, m_g_final, v_g_final),
    }
    state = [tuple(as_rows.get(name, lambda a: a)(a) for a in small_state[name]) for name, _ in SMALL_PARAMS]
    loss_row, small_out = _small_update(me_arr, packed_all, rel_all, state, [tok_in])
    loss = loss_row[0, 0]
    small_res = {name: tuple(a.reshape(small_state[name][0].shape) for a in res)
                 for (name, _), res in zip(SMALL_PARAMS, small_out)}

    def finish(name, after, w, m, v, tr):
        src, land = _exchange_wait(started[name], after, "exchange_wait_" + name)
        return _adamw_parts(w, m, v, src, land, me_arr, tr, "adamw_" + name)

    g_down, d_down, nm_down, nv_down = finish("w_down", [loss_row], w_down[0], m_w_down[0], v_w_down[0], 176)
    g_gu, d_gu, nm_gu, nv_gu = finish("w_gu", [nv_down], w_gu[0].T, m_w_gu[0].T, v_w_gu[0].T, 352)
    g_out, d_out, nm_out, nv_out = finish("w_out", [nv_gu], w_out[0], m_w_out[0], v_w_out[0], 128)
    g_in, d_in, nm_in, nv_in = finish("w_in", [nv_out, nv_ada], w_in[0].T, m_w_in[0].T, v_w_in[0].T, 144)

    big = {
        "w_ada": (g_ada[None], d_ada[None], nm_ada[None], nv_ada[None]),
        "w_in": (g_in.T[None], d_in.T[None], nm_in.T[None], nv_in.T[None]),
        "w_out": (g_out[None], d_out[None], nm_out[None], nv_out[None]),
        "w_gu": (g_gu.T[None], d_gu.T[None], nm_gu.T[None], nv_gu.T[None]),
        "w_down": (g_down[None], d_down[None], nm_down[None], nv_down[None]),
    }
    order = ["rel_bias", "w_ada", "b_ada", "g_norm1", "w_in", "sinks", "conv_w", "g_attn_out", "g_conv_out", "w_out",
             "g_norm2", "w_gu", "w_down", "g_final"]
    results = [big[k] if k in big else small_res[k] for k in order]
    return (loss, dx[None], *[r[0] for r in results], *[r[1] for r in results], *[r[2] for r in results],
            *[r[3] for r in results])
```

```python
import functools
import math

import jax
import jax.numpy as jnp
from jax import lax
from jax.experimental import pallas as pl
from jax.experimental.pallas import tpu as pltpu

F32 = jnp.float32
BF16 = jnp.bfloat16

D_MODEL = 1024
HEAD_DIM = 64
N_Q_HEADS = 8
ATTN_WIDTH = 512
KV_WIDTH = 128
CONV_WIDTH = 512
IN_PROJ_WIDTH = 2304
D_FF = 2816
N_MOD = 6
N_BUCKETS = 32
MAX_DISTANCE = 128
BLOCK = 128
EPS = 1e-6
NEG_INF = -1e30
SCALE = HEAD_DIM ** -0.5
N_DEV = 8

ADAM_LR = 0.001
ADAM_B1 = 0.9
ADAM_B2 = 0.999
ADAM_EPS = 1e-08
ADAM_WD = 0.01
ADAM_STEP = 10

SH1, SC1, G1, SH2, SC2, G2 = range(6)

VMEM_LIMIT_LARGE = 56 * 1024 * 1024
WEIGHT_GRAD_ROWS = 2048
FFN_CHUNKS = 2
PREV_ROWS = 16
MIXER_BLOCKS = 4
MESH_ID = pl.DeviceIdType.MESH

OFF_DMOD = 0
OFF_GN1 = OFF_DMOD + N_MOD * D_MODEL
OFF_SINK = OFF_GN1 + D_MODEL
OFF_GATT = OFF_SINK + 128
OFF_GCV = OFF_GATT + ATTN_WIDTH
OFF_GN2 = OFF_GCV + CONV_WIDTH
OFF_GFIN = OFF_GN2 + D_MODEL
OFF_CONVW = OFF_GFIN + D_MODEL
OFF_LOSS = OFF_CONVW + 3 * CONV_WIDTH
PACKED = OFF_LOSS + 128


def _params(sem=None, vmem=None):
    return pltpu.CompilerParams(dimension_semantics=sem, vmem_limit_bytes=vmem)


def _full(shape):
    nd = len(shape)
    return pl.BlockSpec(shape, lambda *_: (0,) * nd)


def _rows(tm, width):
    return pl.BlockSpec((tm, width), lambda i, *_: (i, 0))


def _sigmoid(x):
    return 1.0 / (1.0 + jnp.exp(-x))


def _rsqrt_mean_sq(x):
    return lax.rsqrt(jnp.mean(x * x, axis=-1, keepdims=True) + EPS)


def _colsum(x):
    return jnp.sum(x, axis=0, keepdims=True)


def _dot(a, b):
    return jnp.dot(a, b, preferred_element_type=F32)


def _dot_nt(a, b):
    return lax.dot_general(a, b, (((1,), (1,)), ((), ())), preferred_element_type=F32)


def _dot_tn(a, b):
    return lax.dot_general(a, b, (((0,), (0,)), ((), ())), preferred_element_type=F32)


def _mesh_position():
    return lax.axis_index("x"), lax.axis_index("y"), lax.axis_index("c")


def _linear(p):
    return 4 * p[0] + 2 * p[1] + p[2]


def _all_gather(arrs, name, to_bf16, big):
    n = len(arrs)
    out_dtype = BF16 if to_bf16 else F32

    def body(*refs):
        in_refs, out_refs = refs[:n], refs[n:2 * n]
        rest = refs[2 * n:]
        if to_bf16:
            stage, rest = rest[:n], rest[n:]
            for a in range(n):
                stage[a][...] = in_refs[a][...].astype(BF16)
            srcs = stage
        else:
            srcs = in_refs
        send_sems, recv_sems, local_sems = rest
        x, y, c = _mesh_position()
        me, sibling = (x, y, c), (x, y, 1 - c)
        chips = [(1 - x, y), (x, 1 - y), (1 - x, 1 - y)]

        def slot(a, p):
            return out_refs[a].at[_linear(p)]

        def copy(k, a, block, to, src=None):
            return pltpu.make_async_remote_copy(
                src_ref=slot(a, block) if src is None else src,
                dst_ref=slot(a, block),
                send_sem=send_sems.at[k * n + a],
                recv_sem=recv_sems.at[k * n + a],
                device_id=to,
                device_id_type=MESH_ID,
            )

        mine = [pltpu.make_async_copy(srcs[a], slot(a, me), local_sems.at[a]) for a in range(n)]
        for cp in mine:
            cp.start()
        first = [copy(0, a, me, sibling, src=srcs[a]) for a in range(n)]
        for j, chip in enumerate(chips):
            first += [copy(1 + j, a, me, (*chip, c), src=srcs[a]) for a in range(n)]
        for cp in first:
            cp.start()
        passed = []
        for j, chip in enumerate(chips):
            for a in range(n):
                copy(1 + j, a, (*chip, c), me).wait_recv()
                fwd = copy(4 + j, a, (*chip, c), sibling)
                fwd.start()
                passed.append(fwd)
        for a in range(n):
            copy(0, a, sibling, me).wait_recv()
        for j, chip in enumerate(chips):
            for a in range(n):
                copy(4 + j, a, (*chip, 1 - c), me).wait_recv()
        for cp in first + passed:
            cp.wait_send()
        for cp in mine:
            cp.wait()

    vmem = pl.BlockSpec(memory_space=pltpu.VMEM)
    out_space = pl.BlockSpec(memory_space=pl.ANY) if big else vmem
    scratch = [pltpu.VMEM(a.shape, BF16) for a in arrs] if to_bf16 else []
    scratch += [pltpu.SemaphoreType.DMA((7 * n,)), pltpu.SemaphoreType.DMA((7 * n,)),
                pltpu.SemaphoreType.DMA((n,))]
    outs = pl.pallas_call(
        body, name=name,
        out_shape=[jax.ShapeDtypeStruct((N_DEV,) + a.shape, out_dtype) for a in arrs],
        in_specs=[vmem] * n, out_specs=[out_space] * n,
        scratch_shapes=scratch,
        compiler_params=_params(vmem=VMEM_LIMIT_LARGE if big else None),
    )(*arrs)
    return list(outs)


def _peer(k):
    x, y, c = _mesh_position()
    return (1 - x if k & 4 else x, 1 - y if k & 2 else y, 1 - c if k & 1 else c)


def _all_gather_small(arrs, name):
    n = len(arrs)

    def body(*refs):
        in_refs, out_refs = refs[:n], refs[n:2 * n]
        send_sems, recv_sems, local_sems = refs[2 * n:]
        me = _linear(_mesh_position())
        mine = [pltpu.make_async_copy(in_refs[a], out_refs[a].at[me], local_sems.at[a]) for a in range(n)]
        for cp in mine:
            cp.start()
        sends = []
        for k in range(1, N_DEV):
            for a in range(n):
                sends.append(pltpu.make_async_remote_copy(
                    src_ref=in_refs[a], dst_ref=out_refs[a].at[me],
                    send_sem=send_sems.at[(k - 1) * n + a], recv_sem=recv_sems.at[(k - 1) * n + a],
                    device_id=_peer(k), device_id_type=MESH_ID))
                sends[-1].start()
        for k in range(1, N_DEV):
            for a in range(n):
                pltpu.make_async_remote_copy(
                    src_ref=in_refs[a], dst_ref=out_refs[a].at[_linear(_peer(k))],
                    send_sem=send_sems.at[(k - 1) * n + a], recv_sem=recv_sems.at[(k - 1) * n + a],
                    device_id=_peer(k), device_id_type=MESH_ID).wait_recv()
        for cp in sends:
            cp.wait_send()
        for cp in mine:
            cp.wait()

    vmem = pl.BlockSpec(memory_space=pltpu.VMEM)
    return list(pl.pallas_call(
        body, name=name,
        out_shape=[jax.ShapeDtypeStruct((N_DEV,) + a.shape, F32) for a in arrs],
        in_specs=[vmem] * n, out_specs=[vmem] * n,
        scratch_shapes=[pltpu.SemaphoreType.DMA((7 * n,)), pltpu.SemaphoreType.DMA((7 * n,)),
                        pltpu.SemaphoreType.DMA((n,))],
    )(*arrs))


HBM_SPEC = pl.BlockSpec(memory_space=pltpu.HBM)
SEM_SPEC = pl.BlockSpec(memory_space=pltpu.SEMAPHORE)
DATAFLOW = pltpu.SideEffectType.DATAFLOW_SIDE_EFFECTING


def _exchange_start(src, name):
    r, c = src.shape[1:]

    def body(src_ref, land_ref, send_sems, recv_sems, src_thru, land_thru, token):
        for k in range(1, N_DEV):
            peer = _peer(k)
            pltpu.make_async_remote_copy(
                src_ref=src_ref.at[_linear(peer)], dst_ref=land_ref.at[k - 1],
                send_sem=send_sems.at[k - 1], recv_sem=recv_sems.at[k - 1],
                device_id=peer, device_id_type=MESH_ID).start()
        token[...] = jnp.zeros_like(token)

    land = lax.empty((N_DEV - 1, r, c), src.dtype)
    return pl.pallas_call(
        body, name=name,
        out_shape=(pltpu.SemaphoreType.DMA((N_DEV - 1,)), pltpu.SemaphoreType.DMA((N_DEV - 1,)),
                   pltpu.HBM(src.shape, src.dtype), pltpu.HBM(land.shape, land.dtype),
                   jax.ShapeDtypeStruct((8, 128), F32)),
        in_specs=(HBM_SPEC, HBM_SPEC),
        out_specs=(SEM_SPEC, SEM_SPEC, HBM_SPEC, HBM_SPEC, pl.BlockSpec(memory_space=pltpu.VMEM)),
        input_output_aliases={0: 2, 1: 3},
        compiler_params=pltpu.CompilerParams(has_side_effects=DATAFLOW),
    )(pltpu.with_memory_space_constraint(src, pltpu.HBM), pltpu.with_memory_space_constraint(land, pltpu.HBM))


def _exchange_wait(started, after, name):
    send_sems, recv_sems, src_thru, land_thru, _ = started

    def body(src_ref, land_ref, send_sems, recv_sems, *rest):
        for k in range(1, N_DEV):
            cp = pltpu.make_async_remote_copy(
                src_ref=src_ref.at[0], dst_ref=land_ref.at[k - 1],
                send_sem=send_sems.at[k - 1], recv_sem=recv_sems.at[k - 1],
                device_id=_peer(k), device_id_type=MESH_ID)
            cp.wait_send()
            cp.wait_recv()

    return pl.pallas_call(
        body, name=name,
        out_shape=(pltpu.HBM(src_thru.shape, src_thru.dtype), pltpu.HBM(land_thru.shape, land_thru.dtype)),
        in_specs=(HBM_SPEC, HBM_SPEC, SEM_SPEC, SEM_SPEC) + (pl.BlockSpec(memory_space=pl.ANY),) * len(after),
        out_specs=(HBM_SPEC, HBM_SPEC), input_output_aliases={0: 0, 1: 1},
        compiler_params=pltpu.CompilerParams(has_side_effects=DATAFLOW),
    )(src_thru, land_thru, send_sems, recv_sems, *after)


def _stage_blocks(arrs, after, name):
    n = len(arrs)

    def body(*refs):
        in_refs, out_refs, stage, sems = refs[:n], refs[n + 1:2 * n + 1], refs[2 * n + 1:3 * n + 1], refs[3 * n + 1]
        me = _linear(_mesh_position())
        copies = []
        for a in range(n):
            stage[a][...] = in_refs[a][...].astype(BF16)
            copies.append(pltpu.make_async_copy(stage[a], out_refs[a].at[me], sems.at[a]))
            copies[-1].start()
        for cp in copies:
            cp.wait()

    return list(pl.pallas_call(
        body, name=name,
        out_shape=[jax.ShapeDtypeStruct((N_DEV,) + a.shape, BF16) for a in arrs],
        in_specs=[pl.BlockSpec(memory_space=pltpu.VMEM)] * n + [pl.BlockSpec(memory_space=pl.ANY)],
        out_specs=[pl.BlockSpec(memory_space=pl.ANY)] * n,
        scratch_shapes=[pltpu.VMEM(a.shape, BF16) for a in arrs] + [pltpu.SemaphoreType.DMA((n,))],
        compiler_params=_params(vmem=VMEM_LIMIT_LARGE),
    )(*arrs, after))


def _same_core_peers():
    x, y, c = _mesh_position()
    return [(x, y, 1 - c), (1 - x, y, c), (x, 1 - y, c), (1 - x, 1 - y, c)]


def _gather_start(bufs, name):
    n = len(bufs)

    def body(*refs):
        buf_refs, rest = refs[:n], refs[n:]
        sems, token = rest[:2 * n], rest[-1]
        me = _linear(_mesh_position())
        for a in range(n):
            for k, peer in enumerate(_same_core_peers()):
                pltpu.make_async_remote_copy(
                    src_ref=buf_refs[a].at[me], dst_ref=buf_refs[a].at[me],
                    send_sem=sems[2 * a].at[k], recv_sem=sems[2 * a + 1].at[k],
                    device_id=peer, device_id_type=MESH_ID).start()
        token[...] = jnp.zeros_like(token)

    outs = pl.pallas_call(
        body, name=name,
        out_shape=tuple(pltpu.SemaphoreType.DMA((4,)) for _ in range(2 * n))
        + tuple(pltpu.HBM(b.shape, b.dtype) for b in bufs) + (jax.ShapeDtypeStruct((8, 128), F32),),
        in_specs=(HBM_SPEC,) * n,
        out_specs=(SEM_SPEC,) * (2 * n) + (HBM_SPEC,) * n + (pl.BlockSpec(memory_space=pltpu.VMEM),),
        input_output_aliases={a: 2 * n + a for a in range(n)},
        compiler_params=pltpu.CompilerParams(has_side_effects=DATAFLOW),
    )(*[pltpu.with_memory_space_constraint(b, pltpu.HBM) for b in bufs])
    return outs[:2 * n], outs[2 * n:3 * n], outs[3 * n]


def _gather_wait(sems, bufs, after, name):
    n = len(bufs)

    def body(*refs):
        buf_refs, sem_refs = refs[:n], refs[n:3 * n]
        x, y, c = _mesh_position()
        me = _linear((x, y, c))
        for a in range(n):
            for k, peer in enumerate(_same_core_peers()):
                cp = pltpu.make_async_remote_copy(
                    src_ref=buf_refs[a].at[me], dst_ref=buf_refs[a].at[_linear(peer)],
                    send_sem=sem_refs[2 * a].at[k], recv_sem=sem_refs[2 * a + 1].at[k],
                    device_id=peer, device_id_type=MESH_ID)
                cp.wait_send()
                cp.wait_recv()

    return list(pl.pallas_call(
        body, name=name,
        out_shape=tuple(pltpu.HBM(b.shape, b.dtype) for b in bufs),
        in_specs=(HBM_SPEC,) * n + (SEM_SPEC,) * (2 * n) + (pl.BlockSpec(memory_space=pl.ANY),) * len(after),
        out_specs=(HBM_SPEC,) * n, input_output_aliases={a: a for a in range(n)},
        compiler_params=pltpu.CompilerParams(has_side_effects=DATAFLOW),
    )(*bufs, *sems, *after))


def _gather_pass_on(bufs, name):
    n = len(bufs)

    def body(*refs):
        out_refs = refs[n:2 * n]
        send_sems, recv_sems = refs[2 * n:]
        x, y, c = _mesh_position()
        sibling = (x, y, 1 - c)
        chips = [(1 - x, y), (x, 1 - y), (1 - x, 1 - y)]
        copies = []
        for a in range(n):
            for j, chip in enumerate(chips):
                block = out_refs[a].at[_linear((*chip, c))]
                copies.append(pltpu.make_async_remote_copy(
                    src_ref=block, dst_ref=block, send_sem=send_sems.at[3 * a + j], recv_sem=recv_sems.at[3 * a + j],
                    device_id=sibling, device_id_type=MESH_ID))
                copies[-1].start()
        for a in range(n):
            for j, chip in enumerate(chips):
                copies[3 * a + j].wait_send()
                theirs = out_refs[a].at[_linear((*chip, 1 - c))]
                pltpu.make_async_remote_copy(
                    src_ref=theirs, dst_ref=theirs, send_sem=send_sems.at[3 * a + j], recv_sem=recv_sems.at[3 * a + j],
                    device_id=sibling, device_id_type=MESH_ID).wait_recv()

    hbm = pl.BlockSpec(memory_space=pl.ANY)
    return list(pl.pallas_call(
        body, name=name,
        out_shape=[jax.ShapeDtypeStruct(b.shape, b.dtype) for b in bufs],
        in_specs=[hbm] * n, out_specs=[hbm] * n, input_output_aliases={a: a for a in range(n)},
        scratch_shapes=[pltpu.SemaphoreType.DMA((3 * n,)), pltpu.SemaphoreType.DMA((3 * n,))],
    )(*bufs))


def _silu_rows(c):
    def body(c_ref, o_ref):
        v = c_ref[...]
        o_ref[...] = v * _sigmoid(v)

    return pl.pallas_call(body, name="cond_silu", out_shape=jax.ShapeDtypeStruct(c.shape, F32))(c)


def _mod_columns(cond_all, w_ada, b_cols):
    def body(c_ref, w_ref, b_ref, o_ref):
        o_ref[...] = _dot(c_ref[...], w_ref[...]) + b_ref[...]

    return pl.pallas_call(body, name="mod_columns",
                          out_shape=jax.ShapeDtypeStruct((N_DEV, w_ada.shape[1]), F32))(cond_all, w_ada, b_cols)


def _in_proj(x, mod, g_norm1, w_in, tm):
    s = x.shape[0]

    def body(x_ref, mod_ref, g_ref, w_ref, h_ref, q_ref, kv_ref, gb_ref, gc_ref, xc_ref):
        xf = x_ref[...]
        n = xf * _rsqrt_mean_sq(xf) * g_ref[...]
        h = (n * (1.0 + mod_ref[SC1:SC1 + 1, :]) + mod_ref[SH1:SH1 + 1, :]).astype(BF16)
        h_ref[...] = h
        p = _dot_nt(h, w_ref[...])
        q_ref[...] = p[:, 0:512].astype(BF16)
        kv_ref[...] = p[:, 512:768].astype(BF16)
        gb_ref[...] = p[:, 768:1280].astype(BF16)
        gc_ref[...] = p[:, 1280:1792].astype(BF16)
        xc_ref[...] = p[:, 1792:2304].astype(BF16)

    return pl.pallas_call(
        body, name="in_proj", grid=(s // tm,),
        in_specs=[_rows(tm, D_MODEL), _full((8, D_MODEL)), _full((1, D_MODEL)), _full((IN_PROJ_WIDTH, D_MODEL))],
        out_specs=[_rows(tm, D_MODEL), _rows(tm, 512), _rows(tm, 256), _rows(tm, 512), _rows(tm, 512), _rows(tm, 512)],
        out_shape=[jax.ShapeDtypeStruct((s, D_MODEL), BF16), jax.ShapeDtypeStruct((s, 512), BF16),
                   jax.ShapeDtypeStruct((s, 256), BF16), jax.ShapeDtypeStruct((s, 512), BF16),
                   jax.ShapeDtypeStruct((s, 512), BF16), jax.ShapeDtypeStruct((s, 512), BF16)],
        compiler_params=_params(("arbitrary",), VMEM_LIMIT_LARGE),
    )(x, mod, g_norm1, w_in)


def _t5_bucket(dist):
    max_exact = N_BUCKETS // 2
    is_small = dist < max_exact
    d = jnp.maximum(dist, 1).astype(F32)
    large = max_exact + (jnp.log(d / max_exact) / math.log(MAX_DISTANCE / max_exact)
                         * (N_BUCKETS - max_exact)).astype(jnp.int32)
    large = jnp.minimum(large, N_BUCKETS - 1)
    return jnp.where(is_small, dist, large)


def _bucket_table():
    qi = jnp.arange(BLOCK, dtype=jnp.int32)[:, None]
    sj = jnp.arange(2 * BLOCK, dtype=jnp.int32)[None, :]
    return _t5_bucket(jnp.maximum(qi + BLOCK - sj, 0))


def _window_mask():
    qi = lax.broadcasted_iota(jnp.int32, (BLOCK, 2 * BLOCK), 0)
    sj = lax.broadcasted_iota(jnp.int32, (BLOCK, 2 * BLOCK), 1)
    dist = qi + BLOCK - sj
    return (dist >= 0) & (dist < BLOCK)


def _bias_table(rel_bias, bucket):
    def body(rb_ref, bk_ref, o_ref):
        bk = bk_ref[...]
        inside = _window_mask()
        for h in range(N_Q_HEADS):
            acc = jnp.zeros((BLOCK, 2 * BLOCK), F32)
            for b in range(N_BUCKETS):
                acc = jnp.where(bk == b, rb_ref[b, h], acc)
            o_ref[h] = jnp.where(inside, acc, NEG_INF)

    return pl.pallas_call(
        body, name="bias_table",
        in_specs=[pl.BlockSpec(memory_space=pltpu.SMEM), pl.BlockSpec(memory_space=pltpu.VMEM)],
        out_shape=jax.ShapeDtypeStruct((N_Q_HEADS, BLOCK, 2 * BLOCK), F32),
    )(rel_bias, bucket)


def _load_kv_window(kv_ref, n):
    prev = jnp.maximum(n - 1, 0)
    kvw = jnp.concatenate([kv_ref[pl.ds(pl.multiple_of(prev * BLOCK, BLOCK), BLOCK), :],
                           kv_ref[pl.ds(pl.multiple_of(n * BLOCK, BLOCK), BLOCK), :]], axis=0)
    k, v = kvw[:, 0:128], kvw[:, 128:256]
    k_sw = pltpu.roll(k.astype(F32), 64, 1).astype(BF16)
    v_sw = pltpu.roll(v.astype(F32), 64, 1).astype(BF16)
    return (k, k_sw), (v, v_sw)


def _conv_taps(gc, xc, gc_prev, xc_prev, n):
    u = gc * xc
    before = jnp.where(n > 0, gc_prev.astype(F32) * xc_prev.astype(F32), 0.0)
    last = before.shape[0] - 1
    row = lax.broadcasted_iota(jnp.int32, u.shape, 0)
    u1 = jnp.where(row == 0, before[last:last + 1, :], pltpu.roll(u, 1, 0))
    u2 = jnp.where(row == 0, before[last - 1:last, :],
                   jnp.where(row == 1, before[last:last + 1, :], pltpu.roll(u, 2, 0)))
    return u, u1, u2


def _mixer_fwd(q, kv, gb, gc, xc, bias, sinks, conv_w, g_attn, g_conv):
    s = q.shape[0]
    nb = s // BLOCK

    per_step = min(MIXER_BLOCKS, nb)
    tile = per_step * BLOCK

    def one_block(n, rows, before, sink_ref, q_ref, kv_ref, gb_ref, gc_ref, xc_ref, bias_ref, cw_ref, ga_ref,
                  gcv_ref, attn_ref, merged_ref, lse_ref):
        ks, vs = _load_kv_window(kv_ref, n)
        lane = lax.broadcasted_iota(jnp.int32, (BLOCK, BLOCK), 1)
        low = lane < HEAD_DIM
        col = lax.broadcasted_iota(jnp.int32, (BLOCK, 2 * BLOCK), 1)
        no_prev = (col < BLOCK) & (n == 0)
        lse_all = jnp.zeros((BLOCK, BLOCK), F32)
        pairs = []
        for p in range(4):
            qp = q_ref[rows, 128 * p:128 * (p + 1)].astype(F32)
            kvh = p // 2
            res = []
            for e in range(2):
                h = 2 * p + e
                qm = jnp.where(low if e == 0 else ~low, qp, 0.0).astype(BF16)
                sw = 0 if kvh == e else 1
                sc = _dot_nt(qm, ks[sw]) * SCALE + bias_ref[h]
                sc = jnp.where(no_prev, NEG_INF, sc)
                sink = sink_ref[h]
                m = jnp.maximum(jnp.max(sc, axis=-1, keepdims=True), sink)
                pe = jnp.exp(sc - m)
                den = jnp.sum(pe, axis=-1, keepdims=True) + jnp.exp(sink - m)
                res.append(_dot(pe.astype(BF16), vs[sw]) / den)
                lse_all = lse_all + jnp.where(lane == h, m + jnp.log(den), 0.0)
            pairs.append(jnp.where(low, res[0], res[1]))
        attn = jnp.concatenate(pairs, axis=1)
        attn_ref[rows, :] = attn
        lse_ref[rows, :] = lse_all
        u, u1, u2 = _conv_taps(gc_ref[rows, :].astype(F32), xc_ref[rows, :].astype(F32), before[0], before[1], n)
        cw = cw_ref[...]
        cv = gb_ref[rows, :].astype(F32) * (cw[0:1, :] * u2 + cw[1:2, :] * u1 + cw[2:3, :] * u)
        an = attn * _rsqrt_mean_sq(attn) * ga_ref[...]
        cn = cv * _rsqrt_mean_sq(cv) * gcv_ref[...]
        merged_ref[rows, :] = jnp.concatenate([an, cn], axis=1).astype(BF16)

    def body(sink_ref, q_ref, kv_ref, gb_ref, gc_ref, xc_ref, gcp_ref, xcp_ref, *rest):
        step = pl.program_id(0)
        for sub in range(per_step):
            rows = slice(sub * BLOCK, (sub + 1) * BLOCK)
            ahead = slice(sub * BLOCK - PREV_ROWS, sub * BLOCK)
            before = (gcp_ref[...], xcp_ref[...]) if sub == 0 else (gc_ref[ahead, :], xc_ref[ahead, :])
            one_block(step * per_step + sub, rows, before, sink_ref, q_ref, kv_ref, gb_ref, gc_ref, xc_ref, *rest)

    blk = lambda w: pl.BlockSpec((tile, w), lambda n: (n, 0))
    prev8 = pl.BlockSpec((PREV_ROWS, 512), lambda n: (jnp.maximum(n * (tile // PREV_ROWS) - 1, 0), 0))
    return pl.pallas_call(
        body, name="mixer_fwd", grid=(nb // per_step,),
        in_specs=[pl.BlockSpec(memory_space=pltpu.SMEM), blk(512), _full((s, 256)), blk(512), blk(512), blk(512),
                  prev8, prev8, _full((N_Q_HEADS, BLOCK, 2 * BLOCK)), _full((3, 512)), _full((1, 512)),
                  _full((1, 512))],
        out_specs=[blk(512), blk(1024), blk(128)],
        out_shape=[jax.ShapeDtypeStruct((s, 512), F32), jax.ShapeDtypeStruct((s, 1024), BF16),
                   jax.ShapeDtypeStruct((s, 128), F32)],
        compiler_params=_params(("arbitrary",)),
    )(sinks, q, kv, gb, gc, xc, gc, xc, bias, conv_w, g_attn, g_conv)


def _out_proj(merged, x, mod, w_out, tm):
    s = x.shape[0]

    def body(m_ref, x_ref, mod_ref, w_ref, o_ref, x1_ref):
        o = _dot(m_ref[...], w_ref[...])
        o_ref[...] = o.astype(BF16)
        x1_ref[...] = x_ref[...] + mod_ref[G1:G1 + 1, :] * o

    return pl.pallas_call(
        body, name="out_proj", grid=(s // tm,),
        in_specs=[_rows(tm, D_MODEL), _rows(tm, D_MODEL), _full((8, D_MODEL)), _full((D_MODEL, D_MODEL))],
        out_specs=[_rows(tm, D_MODEL), _rows(tm, D_MODEL)],
        out_shape=[jax.ShapeDtypeStruct((s, D_MODEL), BF16), jax.ShapeDtypeStruct((s, D_MODEL), F32)],
        compiler_params=_params(("arbitrary",)),
    )(merged, x, mod, w_out)


def _resident(shape):
    nd = len(shape)
    return pl.BlockSpec(shape, lambda *_: (0,) * nd, pipeline_mode=pl.Buffered(1))


def _ffn_fwd(x1, mod, g_norm2, w_gu, w_down, g_final, target, tm):
    s = x1.shape[0]
    chunk = D_FF // FFN_CHUNKS

    def body(x_ref, mod_ref, g_ref, wgu_ref, wd_ref, gf_ref, t_ref,
             h_ref, gate_ref, up_ref, act_ref, o_ref, dx2_ref, small_ref):
        @pl.when(pl.program_id(0) == 0)
        def _():
            small_ref[...] = jnp.zeros_like(small_ref)

        xf = x_ref[...]
        n = xf * _rsqrt_mean_sq(xf) * g_ref[...]
        h = (n * (1.0 + mod_ref[SC2:SC2 + 1, :]) + mod_ref[SH2:SH2 + 1, :]).astype(BF16)
        h_ref[...] = h
        o = None
        for j in range(FFN_CHUNKS):
            lo = j * chunk
            gate = _dot_nt(h, wgu_ref[lo:lo + chunk, :])
            up = _dot_nt(h, wgu_ref[D_FF + lo:D_FF + lo + chunk, :])
            gate_ref[:, lo:lo + chunk] = gate.astype(BF16)
            up_ref[:, lo:lo + chunk] = up.astype(BF16)
            act = (gate * _sigmoid(gate) * up).astype(BF16)
            act_ref[:, lo:lo + chunk] = act
            part = _dot(act, wd_ref[lo:lo + chunk, :])
            o = part if o is None else o + part
        o_ref[...] = o.astype(BF16)
        x2 = xf + mod_ref[G2:G2 + 1, :] * o
        r = _rsqrt_mean_sq(x2)
        xn = x2 * r
        gf = gf_ref[...]
        err = xn * gf - t_ref[...]
        dy = err * (1.0 / D_MODEL)
        dxn = dy * gf
        dx2_ref[...] = (r * (dxn - xn * jnp.mean(dxn * xn, axis=-1, keepdims=True))).astype(BF16)
        small_ref[0:1, :] += _colsum(dy * xn)
        small_ref[1:2, :] += _colsum(err * err)

        @pl.when(pl.program_id(0) == pl.num_programs(0) - 1)
        def _():
            total = jnp.sum(small_ref[1:2, :], axis=-1, keepdims=True) * (0.5 / D_MODEL)
            small_ref[2:3, :] = jnp.broadcast_to(total, (1, D_MODEL))

    wide = jax.ShapeDtypeStruct((s, D_FF), BF16)
    return pl.pallas_call(
        body, name="ffn_fwd", grid=(s // tm,),
        in_specs=[_rows(tm, D_MODEL), _full((8, D_MODEL)), _full((1, D_MODEL)), _resident((2 * D_FF, D_MODEL)),
                  _resident((D_FF, D_MODEL)), _full((1, D_MODEL)), _rows(tm, D_MODEL)],
        out_specs=[_rows(tm, D_MODEL), _rows(tm, D_FF), _rows(tm, D_FF), _rows(tm, D_FF), _rows(tm, D_MODEL),
                   _rows(tm, D_MODEL), _full((8, D_MODEL))],
        out_shape=[jax.ShapeDtypeStruct((s, D_MODEL), BF16), wide, wide, wide,
                   jax.ShapeDtypeStruct((s, D_MODEL), BF16), jax.ShapeDtypeStruct((s, D_MODEL), BF16),
                   jax.ShapeDtypeStruct((8, D_MODEL), F32)],
        compiler_params=_params(("arbitrary",), VMEM_LIMIT_LARGE),
    )(x1, mod, g_norm2, w_gu, w_down, g_final, target)


def _ffn_bwd(dx2, o2, gate, up, x1, mod, g_norm2, w_down, w_gu, tm):
    s = x1.shape[0]
    chunk = D_FF // FFN_CHUNKS

    def body(dx_ref, o_ref, gate_ref, up_ref, x_ref, mod_ref, g_ref, wd_ref, wgu_ref,
             do_ref, dgu_ref, dx1_ref, small_ref):
        @pl.when(pl.program_id(0) == 0)
        def _():
            small_ref[...] = jnp.zeros_like(small_ref)

        dx = dx_ref[...].astype(F32)
        small_ref[3:4, :] += _colsum(dx * o_ref[...].astype(F32))
        do = (dx * mod_ref[G2:G2 + 1, :]).astype(BF16)
        do_ref[...] = do
        dh = None
        for j in range(FFN_CHUNKS):
            lo = j * chunk
            dact = _dot_nt(do, wd_ref[lo:lo + chunk, :])
            gate = gate_ref[:, lo:lo + chunk].astype(F32)
            sg = _sigmoid(gate)
            dgate = (dact * up_ref[:, lo:lo + chunk].astype(F32) * (sg * (1.0 + gate * (1.0 - sg)))).astype(BF16)
            dup = (dact * (gate * sg)).astype(BF16)
            dgu_ref[:, lo:lo + chunk] = dgate
            dgu_ref[:, D_FF + lo:D_FF + lo + chunk] = dup
            part = _dot(dgate, wgu_ref[lo:lo + chunk, :]) + _dot(dup, wgu_ref[D_FF + lo:D_FF + lo + chunk, :])
            dh = part if dh is None else dh + part
        dx1 = dx + _norm_mod_bwd(dh, x_ref[...], g_ref[...], mod_ref[SC2:SC2 + 1, :], small_ref)
        dx1_ref[...] = dx1.astype(BF16)

    return pl.pallas_call(
        body, name="ffn_bwd", grid=(s // tm,),
        in_specs=[_rows(tm, D_MODEL), _rows(tm, D_MODEL), _rows(tm, D_FF), _rows(tm, D_FF), _rows(tm, D_MODEL),
                  _full((8, D_MODEL)), _full((1, D_MODEL)), _resident((D_FF, D_MODEL)),
                  _resident((2 * D_FF, D_MODEL))],
        out_specs=[_rows(tm, D_MODEL), _rows(tm, 2 * D_FF), _rows(tm, D_MODEL), _full((8, D_MODEL))],
        out_shape=[jax.ShapeDtypeStruct((s, D_MODEL), BF16), jax.ShapeDtypeStruct((s, 2 * D_FF), BF16),
                   jax.ShapeDtypeStruct((s, D_MODEL), BF16), jax.ShapeDtypeStruct((8, D_MODEL), F32)],
        compiler_params=_params(("arbitrary",), VMEM_LIMIT_LARGE),
    )(dx2, o2, gate, up, x1, mod, g_norm2, w_down, w_gu)


def _norm_mod_bwd(dh, xf, g, scale_row, small_ref):
    r = _rsqrt_mean_sq(xf)
    xn = xf * r
    small_ref[0:1, :] += _colsum(dh)
    small_ref[1:2, :] += _colsum(dh * (xn * g))
    dn = dh * (1.0 + scale_row)
    small_ref[2:3, :] += _colsum(dn * xn)
    dxn = dn * g
    return r * (dxn - xn * jnp.mean(dxn * xn, axis=-1, keepdims=True))


def _out_proj_bwd(dx1, o1, mod, w_out, tm):
    s = dx1.shape[0]

    def body(dx_ref, o_ref, mod_ref, w_ref, do_ref, dm_ref, small_ref):
        @pl.when(pl.program_id(0) == 0)
        def _():
            small_ref[...] = jnp.zeros_like(small_ref)

        dx = dx_ref[...].astype(F32)
        small_ref[0:1, :] += _colsum(dx * o_ref[...].astype(F32))
        do = (dx * mod_ref[G1:G1 + 1, :]).astype(BF16)
        do_ref[...] = do
        dm_ref[...] = _dot_nt(do, w_ref[...]).astype(BF16)

    return pl.pallas_call(
        body, name="out_proj_bwd", grid=(s // tm,),
        in_specs=[_rows(tm, D_MODEL), _rows(tm, D_MODEL), _full((8, D_MODEL)), _full((D_MODEL, D_MODEL))],
        out_specs=[_rows(tm, D_MODEL), _rows(tm, D_MODEL), _full((8, D_MODEL))],
        out_shape=[jax.ShapeDtypeStruct((s, D_MODEL), BF16), jax.ShapeDtypeStruct((s, D_MODEL), BF16),
                   jax.ShapeDtypeStruct((8, D_MODEL), F32)],
        compiler_params=_params(("arbitrary",)),
    )(dx1, o1, mod, w_out)


def _group_norm_bwd(dm, a, g):
    r = _rsqrt_mean_sq(a)
    an = a * r
    dan = dm * g
    return r * (dan - an * jnp.mean(dan * an, axis=-1, keepdims=True)), _colsum(dm * an)


def _mixer_bwd(q, kv, gb, gc, xc, bias, sinks, conv_w, g_attn, g_conv, attn, lse, dmerged):
    s = q.shape[0]
    nb = s // BLOCK

    per_step = min(MIXER_BLOCKS, nb)
    tile = per_step * BLOCK
    steps = nb // per_step

    def one_block(n, rows, before, nxt, sink_ref, q_ref, kv_ref, gb_ref, gc_ref, xc_ref, bias_ref, cw_ref, ga_ref,
                  gcv_ref, attn_ref, lse_ref, dm_ref, dproj_ref, dbias_ref, dsink_ref, small_ref):
        next_dy, next_dkv = nxt
        dm = dm_ref[rows, :].astype(F32)
        gbv, gcv_, xcv = gb_ref[rows, :].astype(F32), gc_ref[rows, :].astype(F32), xc_ref[rows, :].astype(F32)
        u, u1, u2 = _conv_taps(gcv_, xcv, before[0], before[1], n)
        cw = cw_ref[...]
        yv = cw[0:1, :] * u2 + cw[1:2, :] * u1 + cw[2:3, :] * u
        dcv, dg_conv = _group_norm_bwd(dm[:, 512:1024], gbv * yv, gcv_ref[...])
        small_ref[1:2, :] += dg_conv
        dproj_ref[rows, 768:1280] = (dcv * yv).astype(BF16)
        dy = dcv * gbv
        row = lax.broadcasted_iota(jnp.int32, dy.shape, 0)
        d1 = jnp.where(row == BLOCK - 1, next_dy[0:1, :], pltpu.roll(dy, BLOCK - 1, 0))
        d2 = jnp.where(row == BLOCK - 2, next_dy[0:1, :],
                       jnp.where(row == BLOCK - 1, next_dy[1:2, :], pltpu.roll(dy, BLOCK - 2, 0)))
        du = cw[2:3, :] * dy + cw[1:2, :] * d1 + cw[0:1, :] * d2
        dproj_ref[rows, 1280:1792] = (du * xcv).astype(BF16)
        dproj_ref[rows, 1792:2304] = (du * gcv_).astype(BF16)
        small_ref[2:3, :] += _colsum(dy * u2)
        small_ref[3:4, :] += _colsum(dy * u1)
        small_ref[4:5, :] += _colsum(dy * u)

        attn_v = attn_ref[rows, :]
        dout, dg_attn = _group_norm_bwd(dm[:, 0:512], attn_v, ga_ref[...])
        small_ref[0:1, :] += dg_attn
        ks, vs = _load_kv_window(kv_ref, n)
        lane = lax.broadcasted_iota(jnp.int32, (BLOCK, BLOCK), 1)
        low = lane < HEAD_DIM
        col = lax.broadcasted_iota(jnp.int32, (BLOCK, 2 * BLOCK), 1)
        no_prev = (col < BLOCK) & (n == 0)
        lse_all = lse_ref[rows, :]
        dsink = jnp.zeros((BLOCK, BLOCK), F32)
        dq_pairs = []
        dk_groups, dv_groups = [], []
        for kvh in range(2):
            ds_rows, pr_rows, q_rows, do_rows = [], [], [], []
            for p in (2 * kvh, 2 * kvh + 1):
                qp = q_ref[rows, 128 * p:128 * (p + 1)].astype(F32)
                do_p = dout[:, 128 * p:128 * (p + 1)]
                prod = do_p * attn_v[:, 128 * p:128 * (p + 1)]
                res = []
                for e in range(2):
                    h = 2 * p + e
                    half = low if e == 0 else ~low
                    qm = jnp.where(half, qp, 0.0).astype(BF16)
                    dom = jnp.where(half, do_p, 0.0).astype(BF16)
                    delta = jnp.sum(jnp.where(half, prod, 0.0), axis=-1, keepdims=True)
                    lse_h = jnp.sum(jnp.where(lane == h, lse_all, 0.0), axis=-1, keepdims=True)
                    sw = 0 if kvh == e else 1
                    sc = _dot_nt(qm, ks[sw]) * SCALE + bias_ref[h]
                    sc = jnp.where(no_prev, NEG_INF, sc)
                    pr = jnp.exp(sc - lse_h)
                    dp = _dot_nt(dom, vs[sw])
                    ds = pr * (dp - delta)
                    dbias_ref[h] += ds
                    dsink = dsink + jnp.where(lane == h, -jnp.exp(sink_ref[h] - lse_h) * delta, 0.0)
                    dsb = ds.astype(BF16)
                    res.append(_dot(dsb, ks[sw]) * SCALE)
                    ds_rows.append(dsb)
                    pr_rows.append(pr.astype(BF16))
                    q_rows.append(qm)
                    do_rows.append(dom)
                dq_pairs.append(jnp.where(low, res[0], res[1]))
            dk_g = _dot_tn(jnp.concatenate(ds_rows, axis=0), jnp.concatenate(q_rows, axis=0)) * SCALE
            dv_g = _dot_tn(jnp.concatenate(pr_rows, axis=0), jnp.concatenate(do_rows, axis=0))
            dk_groups.append(dk_g + pltpu.roll(dk_g, 64, 1))
            dv_groups.append(dv_g + pltpu.roll(dv_g, 64, 1))
        dproj_ref[rows, 0:512] = jnp.concatenate(dq_pairs, axis=1).astype(BF16)
        dsink_ref[...] += dsink
        low_kv = lax.broadcasted_iota(jnp.int32, (2 * BLOCK, BLOCK), 1) < HEAD_DIM
        dkv_win = jnp.concatenate([jnp.where(low_kv, dk_groups[0], dk_groups[1]),
                                   jnp.where(low_kv, dv_groups[0], dv_groups[1])], axis=1)
        dproj_ref[rows, 512:768] = (dkv_win[BLOCK:2 * BLOCK, :] + next_dkv).astype(BF16)
        return dy[0:8, :], dkv_win[0:BLOCK, :]

    def body(sink_ref, q_ref, kv_ref, gb_ref, gc_ref, xc_ref, gcp_ref, xcp_ref, *rest):
        refs, dy_ref, dkv_ref = rest[:-2], rest[-2], rest[-1]
        dbias_ref, dsink_ref, small_ref = refs[8], refs[9], refs[10]
        step = pl.program_id(0)

        @pl.when(step == 0)
        def _():
            dbias_ref[...] = jnp.zeros_like(dbias_ref)
            dsink_ref[...] = jnp.zeros_like(dsink_ref)
            small_ref[...] = jnp.zeros_like(small_ref)
            dy_ref[...] = jnp.zeros_like(dy_ref)
            dkv_ref[...] = jnp.zeros_like(dkv_ref)

        nxt = (dy_ref[...], dkv_ref[...])
        for sub in reversed(range(per_step)):
            rows = slice(sub * BLOCK, (sub + 1) * BLOCK)
            ahead = slice(sub * BLOCK - PREV_ROWS, sub * BLOCK)
            before = (gcp_ref[...], xcp_ref[...]) if sub == 0 else (gc_ref[ahead, :], xc_ref[ahead, :])
            nxt = one_block((steps - 1 - step) * per_step + sub, rows, before, nxt,
                            sink_ref, q_ref, kv_ref, gb_ref, gc_ref, xc_ref, *refs)
        dy_ref[...], dkv_ref[...] = nxt

        @pl.when(step == steps - 1)
        def _():
            small_ref[5:6, :] = jnp.concatenate([_colsum(dsink_ref[...]), jnp.zeros((1, 512 - BLOCK), F32)], axis=1)

    blk = lambda w: pl.BlockSpec((tile, w), lambda t: (steps - 1 - t, 0))
    prev8 = pl.BlockSpec((PREV_ROWS, 512),
                         lambda t: (jnp.maximum((steps - 1 - t) * (tile // PREV_ROWS) - 1, 0), 0))
    bf = lambda w: jax.ShapeDtypeStruct((s, w), BF16)
    return pl.pallas_call(
        body, name="mixer_bwd", grid=(steps,),
        in_specs=[pl.BlockSpec(memory_space=pltpu.SMEM), blk(512), _full((s, 256)), blk(512), blk(512), blk(512),
                  prev8, prev8, _full((N_Q_HEADS, BLOCK, 2 * BLOCK)), _full((3, 512)), _full((1, 512)),
                  _full((1, 512)), blk(512), blk(128), blk(1024)],
        out_specs=[blk(IN_PROJ_WIDTH), _full((N_Q_HEADS, BLOCK, 2 * BLOCK)), _full((BLOCK, BLOCK)), _full((8, 512))],
        out_shape=[bf(IN_PROJ_WIDTH), jax.ShapeDtypeStruct((N_Q_HEADS, BLOCK, 2 * BLOCK), F32),
                   jax.ShapeDtypeStruct((BLOCK, BLOCK), F32), jax.ShapeDtypeStruct((8, 512), F32)],
        scratch_shapes=[pltpu.VMEM((8, 512), F32), pltpu.VMEM((BLOCK, 2 * KV_WIDTH), F32)],
        compiler_params=_params(("arbitrary",), VMEM_LIMIT_LARGE),
    )(sinks, q, kv, gb, gc, xc, gc, xc, bias, conv_w, g_attn, g_conv, attn, lse, dmerged)


def _in_proj_bwd(dproj, x, dx1, mod, g_norm1, w_in, tm):
    s = x.shape[0]

    def body(dproj_ref, x_ref, dx1_ref, mod_ref, g_ref, w_ref, dx_ref, small_ref):
        @pl.when(pl.program_id(0) == 0)
        def _():
            small_ref[...] = jnp.zeros_like(small_ref)

        dh = _dot(dproj_ref[...], w_ref[...])
        dx_ref[...] = dx1_ref[...].astype(F32) + _norm_mod_bwd(dh, x_ref[...], g_ref[...], mod_ref[SC1:SC1 + 1, :],
                                                               small_ref)

    return pl.pallas_call(
        body, name="in_proj_bwd", grid=(s // tm,),
        in_specs=[_rows(tm, IN_PROJ_WIDTH), _rows(tm, D_MODEL), _rows(tm, D_MODEL), _full((8, D_MODEL)),
                  _full((1, D_MODEL)), _full((IN_PROJ_WIDTH, D_MODEL))],
        out_specs=[_rows(tm, D_MODEL), _full((8, D_MODEL))],
        out_shape=[jax.ShapeDtypeStruct((s, D_MODEL), F32), jax.ShapeDtypeStruct((8, D_MODEL), F32)],
        compiler_params=_params(("arbitrary",), VMEM_LIMIT_LARGE),
    )(dproj, x, dx1, mod, g_norm1, w_in)


def _weight_grad(a, b, tk, ts, name, after=None):
    s, k = a.shape
    n = b.shape[1]
    nt = s // ts
    extra = [] if after is None else [after]

    def body(a_ref, b_ref, *rest):
        o_ref, acc_ref = rest[-2:]
        t = pl.program_id(1)
        part = _dot_tn(a_ref[...], b_ref[...])

        @pl.when(t == 0)
        def _():
            acc_ref[...] = part

        @pl.when(t > 0)
        def _():
            acc_ref[...] += part

        @pl.when(t == nt - 1)
        def _():
            o_ref[...] = acc_ref[...].astype(BF16)

    return pl.pallas_call(
        body, name=name, grid=(k // tk, nt),
        in_specs=[pl.BlockSpec((ts, tk), lambda i, t: (t, i)), pl.BlockSpec((ts, n), lambda i, t: (t, 0))]
        + [pl.BlockSpec(memory_space=pl.ANY)] * len(extra),
        out_specs=pl.BlockSpec((tk, n), lambda i, t: (i, 0)),
        out_shape=jax.ShapeDtypeStruct((k, n), BF16),
        scratch_shapes=[pltpu.VMEM((tk, n), F32)],
        compiler_params=_params(("arbitrary", "arbitrary"), VMEM_LIMIT_LARGE),
    )(a, b, *extra)


def _rel_bias_grad(dbias, bucket):
    def body(db_ref, bk_ref, o_ref, rows_ref):
        bk = bk_ref[...]
        for b in range(N_BUCKETS):
            sel = (bk == b).astype(F32)
            for h in range(N_Q_HEADS):
                rows_ref[N_BUCKETS * h + b:N_BUCKETS * h + b + 1, :] = _colsum(db_ref[h] * sel)
        head = lax.broadcasted_iota(jnp.int32, (N_BUCKETS, N_Q_HEADS), 1)
        out = jnp.zeros((N_BUCKETS, N_Q_HEADS), F32)
        for h in range(N_Q_HEADS):
            per_bucket = jnp.sum(rows_ref[N_BUCKETS * h:N_BUCKETS * (h + 1), :], axis=-1, keepdims=True)
            out = out + jnp.where(head == h, per_bucket, 0.0)
        o_ref[...] = out

    return pl.pallas_call(
        body, name="rel_bias_grad",
        out_shape=jax.ShapeDtypeStruct((N_BUCKETS, N_Q_HEADS), F32),
        scratch_shapes=[pltpu.VMEM((N_BUCKETS * N_Q_HEADS, 2 * BLOCK), F32)],
    )(dbias, bucket)


def _lanes_from(x, start, width):
    n = x.shape[1]
    return pltpu.roll(x, (n - start) % n, 1)[:, 0:width]


def _w_ada_grad(me, cond_all, packed_all, cols):
    def body(me_ref, c_ref, p_ref, o_ref):
        dmod = jnp.concatenate([p_ref[k][:, OFF_DMOD:OFF_DMOD + N_MOD * D_MODEL] for k in range(N_DEV)], axis=0)
        mine = _lanes_from(dmod, me_ref[0] * cols, cols)
        pad = lambda a: jnp.concatenate([a, jnp.zeros((128 - N_DEV, a.shape[1]), F32)], axis=0)
        o_ref[...] = _dot_tn(pad(c_ref[...]), pad(mine))

    vmem = pl.BlockSpec(memory_space=pltpu.VMEM)
    return pl.pallas_call(body, name="w_ada_grad",
                          in_specs=[pl.BlockSpec(memory_space=pltpu.SMEM), vmem, vmem],
                          out_shape=jax.ShapeDtypeStruct((cond_all.shape[1], cols), F32))(me, cond_all, packed_all)


SMALL_PARAMS = (("rel_bias", None), ("b_ada", (OFF_DMOD, N_MOD * D_MODEL)), ("g_norm1", (OFF_GN1, D_MODEL)),
                ("sinks", (OFF_SINK, N_Q_HEADS)), ("conv_w", None), ("g_attn_out", (OFF_GATT, ATTN_WIDTH)),
                ("g_conv_out", (OFF_GCV, CONV_WIDTH)), ("g_norm2", (OFF_GN2, D_MODEL)),
                ("g_final", (OFF_GFIN, D_MODEL)))


def _small_update(me, packed_all, rel_all, state, after):
    n_p = len(SMALL_PARAMS)
    flat = [a for triple in state for a in triple]
    conv_cols = state[4][0].shape[1]

    def body(me_ref, p_ref, r_ref, *refs):
        ins = refs[:3 * n_p]
        loss_ref, outs = refs[3 * n_p + len(after)], refs[3 * n_p + len(after) + 1:]
        small, rel = p_ref[0], r_ref[0]
        for k in range(1, N_DEV):
            small = small + p_ref[k]
            rel = rel + r_ref[k]
        loss_ref[...] = small[:, OFF_LOSS:OFF_LOSS + 128]
        taps = jnp.concatenate([small[:, OFF_CONVW + CONV_WIDTH * j:OFF_CONVW + CONV_WIDTH * (j + 1)]
                                for j in range(3)] + [jnp.zeros((5, CONV_WIDTH), F32)], axis=0)
        conv_g = _lanes_from(taps, me_ref[0] * conv_cols, conv_cols)[0:3, :]
        for i, (name, lanes) in enumerate(SMALL_PARAMS):
            g = rel if name == "rel_bias" else conv_g if name == "conv_w" else small[:, lanes[0]:lanes[0] + lanes[1]]
            w_ref, m_ref, v_ref = ins[3 * i:3 * i + 3]
            outs[4 * i][...] = g
            outs[4 * i + 1][...], outs[4 * i + 2][...], outs[4 * i + 3][...] = _adam_math(
                w_ref[...], g, m_ref[...], v_ref[...])

    vmem = pl.BlockSpec(memory_space=pltpu.VMEM)
    out_shape = [jax.ShapeDtypeStruct((1, 128), F32)]
    for w, _, _ in state:
        out_shape += [jax.ShapeDtypeStruct(w.shape, F32)] * 4
    outs = pl.pallas_call(
        body, name="small_update",
        in_specs=[pl.BlockSpec(memory_space=pltpu.SMEM), vmem, vmem] + [vmem] * len(flat)
        + [pl.BlockSpec(memory_space=pl.ANY)] * len(after),
        out_shape=out_shape,
    )(me, packed_all, rel_all, *flat, *after)
    return outs[0], [tuple(outs[1 + 4 * i:5 + 4 * i]) for i in range(n_p)]


def _adam_math(w, g, m, v):
    m = ADAM_B1 * m + (1.0 - ADAM_B1) * g
    v = ADAM_B2 * v + (1.0 - ADAM_B2) * (g * g)
    m_hat = m / (1.0 - ADAM_B1 ** ADAM_STEP)
    v_hat = v / (1.0 - ADAM_B2 ** ADAM_STEP)
    delta = -ADAM_LR * (m_hat / (jnp.sqrt(v_hat) + ADAM_EPS) + ADAM_WD * w)
    return delta, m, v


def _adamw_parts(w, m, v, local, land, me, tr, name):
    r, c = w.shape

    def body(me_ref, w_ref, m_ref, v_ref, own_ref, land_ref, g_ref, d_ref, mo_ref, vo_ref):
        g = own_ref[0].astype(F32)
        for k in range(N_DEV - 1):
            g = g + land_ref[k].astype(F32)
        g_ref[...] = g
        d_ref[...], mo_ref[...], vo_ref[...] = _adam_math(w_ref[...], g, m_ref[...], v_ref[...])

    tile = pl.BlockSpec((tr, c), lambda i, me_ref: (i, 0))
    return pl.pallas_call(
        body, name=name,
        grid_spec=pltpu.PrefetchScalarGridSpec(
            num_scalar_prefetch=1, grid=(r // tr,),
            in_specs=[tile, tile, tile, pl.BlockSpec((1, tr, c), lambda i, me_ref: (me_ref[0], i, 0)),
                      pl.BlockSpec((N_DEV - 1, tr, c), lambda i, me_ref: (0, i, 0))],
            out_specs=[tile] * 4),
        out_shape=[jax.ShapeDtypeStruct((r, c), F32)] * 4,
        compiler_params=_params(("arbitrary",)),
    )(me, w, m, v, local, land)


def _adamw(w, m, v, g, tr, name):
    r, c = w.shape

    def body(w_ref, m_ref, v_ref, g_ref, d_ref, mo_ref, vo_ref):
        d_ref[...], mo_ref[...], vo_ref[...] = _adam_math(w_ref[...], g_ref[...], m_ref[...], v_ref[...])

    tile = pl.BlockSpec((tr, c), lambda i: (i, 0))
    return pl.pallas_call(
        body, name=name, grid=(r // tr,),
        in_specs=[tile] * 4, out_specs=[tile] * 3,
        out_shape=[jax.ShapeDtypeStruct((r, c), F32)] * 3,
        compiler_params=_params(("arbitrary",)),
    )(w, m, v, g)


def _behind(a, token):
    return a + token[0:a.shape[0], 0:1]


def _local_step(x, target, mod, w_in_t, weights_out_gu, weights_down, rel_bias, g_norm1, sinks, conv_w, g_attn,
                g_conv, g_norm2, g_final, exchange):
    s = x.shape[0]
    tm = min(512, s)
    tm_small = min(256, s)
    bucket = _bucket_table()
    bias = _bias_table(rel_bias, bucket)

    h, q, kv, gb, gc, xc = _in_proj(x, mod, g_norm1, w_in_t, tm)
    attn, merged, lse = _mixer_fwd(q, kv, gb, gc, xc, bias, sinks, conv_w, g_attn, g_conv)
    w_out, w_gu_t = weights_out_gu(merged)
    o1, x1 = _out_proj(merged, x, mod, w_out, tm)
    w_down = weights_down(x1)
    h2, gate, up, act, o2, dx2, fin = _ffn_fwd(x1, mod, g_norm2, w_gu_t, w_down, g_final, target, tm_small)

    do2, dgu, dx1, sm_2 = _ffn_bwd(dx2, o2, gate, up, x1, mod, g_norm2, w_down, w_gu_t, tm_small)
    ts = min(WEIGHT_GRAD_ROWS, s)
    tok_down = exchange("w_down", _weight_grad(act, do2, D_FF // 2, ts, "w_down_grad"))
    mod = _behind(mod, exchange("w_gu", _weight_grad(dgu, h2, D_FF // 2, ts, "w_gu_grad", after=tok_down)))
    do1, dmerged, sm_g1 = _out_proj_bwd(dx1, o1, mod, w_out, tm)
    g_attn_b = _behind(g_attn, exchange("w_out", _weight_grad(merged, do1, D_MODEL, ts, "w_out_grad")))
    dproj, dbias, dsink, sm_mix = _mixer_bwd(
        q, kv, gb, gc, xc, bias, sinks, conv_w, g_attn_b, g_conv, attn, lse, dmerged)
    mod = _behind(mod, exchange("w_in", _weight_grad(dproj, h, IN_PROJ_WIDTH // 2, ts, "w_in_grad")))
    dx, sm_1 = _in_proj_bwd(dproj, x, dx1, mod, g_norm1, w_in_t, tm)
    d_rel = _rel_bias_grad(dbias, bucket)

    packed = jnp.concatenate([
        sm_1[0], sm_1[1], sm_g1[0], sm_2[0], sm_2[1], sm_2[3],
        sm_1[2],
        sm_mix[5, 0:128],
        sm_mix[0], sm_mix[1],
        sm_2[2],
        fin[0],
        sm_mix[2], sm_mix[3], sm_mix[4],
        fin[2, 0:128],
    ])[None, :]
    return dx, packed, d_rel


def kernel(x, c, rel_bias, w_ada, b_ada, g_norm1, w_in, sinks, conv_w, g_attn_out, g_conv_out, w_out, g_norm2, w_gu, w_down, g_final, loss_target, m_rel_bias, m_w_ada, m_b_ada, m_g_norm1, m_w_in, m_sinks, m_conv_w, m_g_attn_out, m_g_conv_out, m_w_out, m_g_norm2, m_w_gu, m_w_down, m_g_final, v_rel_bias, v_w_ada, v_b_ada, v_g_norm1, v_w_in, v_sinks, v_conv_w, v_g_attn_out, v_g_conv_out, v_w_out, v_g_norm2, v_w_gu, v_w_down, v_g_final):
    me = _linear(_mesh_position())
    me_arr = jnp.reshape(me, (1,)).astype(jnp.int32)
    ada_cols = w_ada.shape[2]
    tm = min(512, x.shape[1])

    cond = _silu_rows(c)
    cond_all, conv_w_all = _all_gather_small([cond, conv_w[0]], "gather_cond")
    cond_all = cond_all[:, 0, :]
    conv_cols = conv_w.shape[2]
    conv_w_full = conv_w_all.transpose(1, 0, 2).reshape(3, CONV_WIDTH)
    b_cols = lax.dynamic_slice_in_dim(b_ada, me * ada_cols, ada_cols, axis=1)
    mod_cols = _mod_columns(cond_all, w_ada[0], b_cols)
    mod_all = _all_gather_small([mod_cols], "gather_mod")[0]
    mod = lax.dynamic_index_in_dim(mod_all, me, axis=1, keepdims=False).reshape(N_MOD, D_MODEL)
    mod = jnp.concatenate([mod, jnp.zeros((2, D_MODEL), F32)], axis=0)

    w_in_t = _all_gather([w_in[0].T], "gather_w_in", to_bf16=True, big=True)[0].reshape(IN_PROJ_WIDTH, D_MODEL)
    gather_sems, staged, gather_token = _gather_start(
        _stage_blocks([w_out[0], w_gu[0].T, w_down[0]], w_in_t, "stage_weights"), "gather_start_weights")
    mod = _behind(mod, gather_token)

    def weights_out_gu(after):
        got = _gather_pass_on(_gather_wait(gather_sems[0:4], staged[0:2], [after], "gather_wait_out_gu"),
                              "gather_pass_on_out_gu")
        return got[0].reshape(D_MODEL, D_MODEL), got[1].reshape(2 * D_FF, D_MODEL)

    def weights_down(after):
        got = _gather_pass_on(_gather_wait(gather_sems[4:6], staged[2:3], [after], "gather_wait_down"),
                              "gather_pass_on_down")
        return got[0].reshape(D_FF, D_MODEL)

    started = {}

    def exchange(name, dw):
        st = _exchange_start(dw.reshape(N_DEV, dw.shape[0] // N_DEV, dw.shape[1]), "exchange_start_" + name)
        started[name] = st
        return st[4]

    dx, packed, d_rel = _local_step(
        x[0], loss_target[0], mod, w_in_t, weights_out_gu, weights_down, rel_bias, g_norm1, sinks[0], conv_w_full,
        g_attn_out, g_conv_out, g_norm2, g_final[None, :], exchange)

    packed_all, rel_all = _all_gather_small([packed, d_rel], "gather_small_grads")
    g_ada = _w_ada_grad(me_arr, cond_all, packed_all, ada_cols)
    d_ada, nm_ada, nv_ada = _adamw(w_ada[0], m_w_ada[0], v_w_ada[0], g_ada, 256, "adamw_w_ada")
    as_rows = {"conv_w": lambda a: a[0], "g_final": lambda a: a[None, :]}
    small_state = {
        "rel_bias": (rel_bias, m_rel_bias, v_rel_bias), "b_ada": (b_ada, m_b_ada, v_b_ada),
        "g_norm1": (g_norm1, m_g_norm1, v_g_norm1), "sinks": (sinks, m_sinks, v_sinks),
        "conv_w": (conv_w, m_conv_w, v_conv_w), "g_attn_out": (g_attn_out, m_g_attn_out, v_g_attn_out),
        "g_conv_out": (g_conv_out, m_g_conv_out, v_g_conv_out), "g_norm2": (g_norm2, m_g_norm2, v_g_norm2),
        "g_final": (g_final, m_g_final, v_g_final),
    }
    state = [tuple(as_rows.get(name, lambda a: a)(a) for a in small_state[name]) for name, _ in SMALL_PARAMS]
    loss_row, small_out = _small_update(me_arr, packed_all, rel_all, state, [])
    loss = loss_row[0, 0]
    small_res = {name: tuple(a.reshape(small_state[name][0].shape) for a in res)
                 for (name, _), res in zip(SMALL_PARAMS, small_out)}

    def finish(name, after, w, m, v, tr):
        src, land = _exchange_wait(started[name], after, "exchange_wait_" + name)
        return _adamw_parts(w, m, v, src, land, me_arr, tr, "adamw_" + name)

    g_down, d_down, nm_down, nv_down = finish("w_down", [loss_row], w_down[0], m_w_down[0], v_w_down[0], 176)
    g_gu, d_gu, nm_gu, nv_gu = finish("w_gu", [nv_down], w_gu[0].T, m_w_gu[0].T, v_w_gu[0].T, 352)
    g_out, d_out, nm_out, nv_out = finish("w_out", [nv_gu], w_out[0], m_w_out[0], v_w_out[0], 128)
    g_in, d_in, nm_in, nv_in = finish("w_in", [nv_out, nv_ada], w_in[0].T, m_w_in[0].T, v_w_in[0].T, 144)

    big = {
        "w_ada": (g_ada[None], d_ada[None], nm_ada[None], nv_ada[None]),
        "w_in": (g_in.T[None], d_in.T[None], nm_in.T[None], nv_in.T[None]),
        "w_out": (g_out[None], d_out[None], nm_out[None], nv_out[None]),
        "w_gu": (g_gu.T[None], d_gu.T[None], nm_gu.T[None], nv_gu.T[None]),
        "w_down": (g_down[None], d_down[None], nm_down[None], nv_down[None]),
    }
    order = ["rel_bias", "w_ada", "b_ada", "g_norm1", "w_in", "sinks", "conv_w", "g_attn_out", "g_conv_out", "w_out",
             "g_norm2", "w_gu", "w_down", "g_final"]
    results = [big[k] if k in big else small_res[k] for k in order]
    return (loss, dx[None], *[r[0] for r in results], *[r[1] for r in results], *[r[2] for r in results],
            *[r[3] for r in results])
```

```python
import functools
import math

import jax
import jax.numpy as jnp
from jax import lax
from jax.experimental import pallas as pl
from jax.experimental.pallas import tpu as pltpu

F32 = jnp.float32
BF16 = jnp.bfloat16

D_MODEL = 1024
HEAD_DIM = 64
N_Q_HEADS = 8
ATTN_WIDTH = 512
KV_WIDTH = 128
CONV_WIDTH = 512
IN_PROJ_WIDTH = 2304
D_FF = 2816
N_MOD = 6
N_BUCKETS = 32
MAX_DISTANCE = 128
BLOCK = 128
EPS = 1e-6
NEG_INF = -1e30
SCALE = HEAD_DIM ** -0.5
N_DEV = 8

ADAM_LR = 0.001
ADAM_B1 = 0.9
ADAM_B2 = 0.999
ADAM_EPS = 1e-08
ADAM_WD = 0.01
ADAM_STEP = 10

SH1, SC1, G1, SH2, SC2, G2 = range(6)

VMEM_LIMIT_LARGE = 60 * 1024 * 1024
WEIGHT_GRAD_ROWS = 2048
FFN_CHUNKS = 2
PREV_ROWS = 16
MIXER_BLOCKS = 4
MESH_ID = pl.DeviceIdType.MESH

OFF_DMOD = 0
OFF_GN1 = OFF_DMOD + N_MOD * D_MODEL
OFF_SINK = OFF_GN1 + D_MODEL
OFF_GATT = OFF_SINK + 128
OFF_GCV = OFF_GATT + ATTN_WIDTH
OFF_GN2 = OFF_GCV + CONV_WIDTH
OFF_GFIN = OFF_GN2 + D_MODEL
OFF_CONVW = OFF_GFIN + D_MODEL
OFF_LOSS = OFF_CONVW + 3 * CONV_WIDTH
PACKED = OFF_LOSS + 128


def _params(sem=None, vmem=None):
    return pltpu.CompilerParams(dimension_semantics=sem, vmem_limit_bytes=vmem)


def _full(shape):
    nd = len(shape)
    return pl.BlockSpec(shape, lambda *_: (0,) * nd)


def _rows(tm, width):
    return pl.BlockSpec((tm, width), lambda i, *_: (i, 0))


def _sigmoid(x):
    return 1.0 / (1.0 + jnp.exp(-x))


def _rsqrt_mean_sq(x):
    return lax.rsqrt(jnp.mean(x * x, axis=-1, keepdims=True) + EPS)


def _colsum(x):
    return jnp.sum(x, axis=0, keepdims=True)


def _dot(a, b):
    return jnp.dot(a, b, preferred_element_type=F32)


def _dot_nt(a, b):
    return lax.dot_general(a, b, (((1,), (1,)), ((), ())), preferred_element_type=F32)


def _dot_tn(a, b):
    return lax.dot_general(a, b, (((0,), (0,)), ((), ())), preferred_element_type=F32)


def _mesh_position():
    return lax.axis_index("x"), lax.axis_index("y"), lax.axis_index("c")


def _linear(p):
    return 4 * p[0] + 2 * p[1] + p[2]


def _all_gather(arrs, name, to_bf16, big):
    n = len(arrs)
    out_dtype = BF16 if to_bf16 else F32

    def body(*refs):
        in_refs, out_refs = refs[:n], refs[n:2 * n]
        rest = refs[2 * n:]
        if to_bf16:
            stage, rest = rest[:n], rest[n:]
            for a in range(n):
                stage[a][...] = in_refs[a][...].astype(BF16)
            srcs = stage
        else:
            srcs = in_refs
        send_sems, recv_sems, local_sems = rest
        x, y, c = _mesh_position()
        me, sibling = (x, y, c), (x, y, 1 - c)
        chips = [(1 - x, y), (x, 1 - y), (1 - x, 1 - y)]

        def slot(a, p):
            return out_refs[a].at[_linear(p)]

        def copy(k, a, block, to, src=None):
            return pltpu.make_async_remote_copy(
                src_ref=slot(a, block) if src is None else src,
                dst_ref=slot(a, block),
                send_sem=send_sems.at[k * n + a],
                recv_sem=recv_sems.at[k * n + a],
                device_id=to,
                device_id_type=MESH_ID,
            )

        mine = [pltpu.make_async_copy(srcs[a], slot(a, me), local_sems.at[a]) for a in range(n)]
        for cp in mine:
            cp.start()
        first = [copy(0, a, me, sibling, src=srcs[a]) for a in range(n)]
        for j, chip in enumerate(chips):
            first += [copy(1 + j, a, me, (*chip, c), src=srcs[a]) for a in range(n)]
        for cp in first:
            cp.start()
        passed = []
        for j, chip in enumerate(chips):
            for a in range(n):
                copy(1 + j, a, (*chip, c), me).wait_recv()
                fwd = copy(4 + j, a, (*chip, c), sibling)
                fwd.start()
                passed.append(fwd)
        for a in range(n):
            copy(0, a, sibling, me).wait_recv()
        for j, chip in enumerate(chips):
            for a in range(n):
                copy(4 + j, a, (*chip, 1 - c), me).wait_recv()
        for cp in first + passed:
            cp.wait_send()
        for cp in mine:
            cp.wait()

    vmem = pl.BlockSpec(memory_space=pltpu.VMEM)
    out_space = pl.BlockSpec(memory_space=pl.ANY) if big else vmem
    scratch = [pltpu.VMEM(a.shape, BF16) for a in arrs] if to_bf16 else []
    scratch += [pltpu.SemaphoreType.DMA((7 * n,)), pltpu.SemaphoreType.DMA((7 * n,)),
                pltpu.SemaphoreType.DMA((n,))]
    outs = pl.pallas_call(
        body, name=name,
        out_shape=[jax.ShapeDtypeStruct((N_DEV,) + a.shape, out_dtype) for a in arrs],
        in_specs=[vmem] * n, out_specs=[out_space] * n,
        scratch_shapes=scratch,
        compiler_params=_params(vmem=VMEM_LIMIT_LARGE if big else None),
    )(*arrs)
    return list(outs)


def _peer(k):
    x, y, c = _mesh_position()
    return (1 - x if k & 4 else x, 1 - y if k & 2 else y, 1 - c if k & 1 else c)


def _all_gather_small(arrs, name):
    n = len(arrs)

    def body(*refs):
        in_refs, out_refs = refs[:n], refs[n:2 * n]
        send_sems, recv_sems, local_sems = refs[2 * n:]
        me = _linear(_mesh_position())
        mine = [pltpu.make_async_copy(in_refs[a], out_refs[a].at[me], local_sems.at[a]) for a in range(n)]
        for cp in mine:
            cp.start()
        sends = []
        for k in range(1, N_DEV):
            for a in range(n):
                sends.append(pltpu.make_async_remote_copy(
                    src_ref=in_refs[a], dst_ref=out_refs[a].at[me],
                    send_sem=send_sems.at[(k - 1) * n + a], recv_sem=recv_sems.at[(k - 1) * n + a],
                    device_id=_peer(k), device_id_type=MESH_ID))
                sends[-1].start()
        for k in range(1, N_DEV):
            for a in range(n):
                pltpu.make_async_remote_copy(
                    src_ref=in_refs[a], dst_ref=out_refs[a].at[_linear(_peer(k))],
                    send_sem=send_sems.at[(k - 1) * n + a], recv_sem=recv_sems.at[(k - 1) * n + a],
                    device_id=_peer(k), device_id_type=MESH_ID).wait_recv()
        for cp in sends:
            cp.wait_send()
        for cp in mine:
            cp.wait()

    vmem = pl.BlockSpec(memory_space=pltpu.VMEM)
    return list(pl.pallas_call(
        body, name=name,
        out_shape=[jax.ShapeDtypeStruct((N_DEV,) + a.shape, F32) for a in arrs],
        in_specs=[vmem] * n, out_specs=[vmem] * n,
        scratch_shapes=[pltpu.SemaphoreType.DMA((7 * n,)), pltpu.SemaphoreType.DMA((7 * n,)),
                        pltpu.SemaphoreType.DMA((n,))],
    )(*arrs))


HBM_SPEC = pl.BlockSpec(memory_space=pltpu.HBM)
SEM_SPEC = pl.BlockSpec(memory_space=pltpu.SEMAPHORE)
DATAFLOW = pltpu.SideEffectType.DATAFLOW_SIDE_EFFECTING


def _exchange_start(src, name):
    r, c = src.shape[1:]

    def body(src_ref, land_ref, send_sems, recv_sems, src_thru, land_thru, token):
        for k in range(1, N_DEV):
            peer = _peer(k)
            pltpu.make_async_remote_copy(
                src_ref=src_ref.at[_linear(peer)], dst_ref=land_ref.at[k - 1],
                send_sem=send_sems.at[k - 1], recv_sem=recv_sems.at[k - 1],
                device_id=peer, device_id_type=MESH_ID).start()
        token[...] = jnp.zeros_like(token)

    land = lax.empty((N_DEV - 1, r, c), src.dtype)
    return pl.pallas_call(
        body, name=name,
        out_shape=(pltpu.SemaphoreType.DMA((N_DEV - 1,)), pltpu.SemaphoreType.DMA((N_DEV - 1,)),
                   pltpu.HBM(src.shape, src.dtype), pltpu.HBM(land.shape, land.dtype),
                   jax.ShapeDtypeStruct((8, 128), F32)),
        in_specs=(HBM_SPEC, HBM_SPEC),
        out_specs=(SEM_SPEC, SEM_SPEC, HBM_SPEC, HBM_SPEC, pl.BlockSpec(memory_space=pltpu.VMEM)),
        input_output_aliases={0: 2, 1: 3},
        compiler_params=pltpu.CompilerParams(has_side_effects=DATAFLOW),
    )(pltpu.with_memory_space_constraint(src, pltpu.HBM), pltpu.with_memory_space_constraint(land, pltpu.HBM))


def _exchange_wait(started, after, name):
    send_sems, recv_sems, src_thru, land_thru, _ = started

    def body(src_ref, land_ref, send_sems, recv_sems, *rest):
        for k in range(1, N_DEV):
            cp = pltpu.make_async_remote_copy(
                src_ref=src_ref.at[0], dst_ref=land_ref.at[k - 1],
                send_sem=send_sems.at[k - 1], recv_sem=recv_sems.at[k - 1],
                device_id=_peer(k), device_id_type=MESH_ID)
            cp.wait_send()
            cp.wait_recv()

    return pl.pallas_call(
        body, name=name,
        out_shape=(pltpu.HBM(src_thru.shape, src_thru.dtype), pltpu.HBM(land_thru.shape, land_thru.dtype)),
        in_specs=(HBM_SPEC, HBM_SPEC, SEM_SPEC, SEM_SPEC) + (pl.BlockSpec(memory_space=pl.ANY),) * len(after),
        out_specs=(HBM_SPEC, HBM_SPEC), input_output_aliases={0: 0, 1: 1},
        compiler_params=pltpu.CompilerParams(has_side_effects=DATAFLOW),
    )(src_thru, land_thru, send_sems, recv_sems, *after)


def _stage_blocks(arrs, after, name):
    n = len(arrs)

    def body(*refs):
        in_refs, out_refs, stage, sems = refs[:n], refs[n + 1:2 * n + 1], refs[2 * n + 1:3 * n + 1], refs[3 * n + 1]
        me = _linear(_mesh_position())
        copies = []
        for a in range(n):
            stage[a][...] = in_refs[a][...].astype(BF16)
            copies.append(pltpu.make_async_copy(stage[a], out_refs[a].at[me], sems.at[a]))
            copies[-1].start()
        for cp in copies:
            cp.wait()

    return list(pl.pallas_call(
        body, name=name,
        out_shape=[jax.ShapeDtypeStruct((N_DEV,) + a.shape, BF16) for a in arrs],
        in_specs=[pl.BlockSpec(memory_space=pltpu.VMEM)] * n + [pl.BlockSpec(memory_space=pl.ANY)],
        out_specs=[pl.BlockSpec(memory_space=pl.ANY)] * n,
        scratch_shapes=[pltpu.VMEM(a.shape, BF16) for a in arrs] + [pltpu.SemaphoreType.DMA((n,))],
        compiler_params=_params(vmem=VMEM_LIMIT_LARGE),
    )(*arrs, after))


def _same_core_peers():
    x, y, c = _mesh_position()
    return [(x, y, 1 - c), (1 - x, y, c), (x, 1 - y, c), (1 - x, 1 - y, c)]


def _gather_start(bufs, name):
    n = len(bufs)

    def body(*refs):
        buf_refs, rest = refs[:n], refs[n:]
        sems, token = rest[:2 * n], rest[-1]
        me = _linear(_mesh_position())
        for a in range(n):
            for k, peer in enumerate(_same_core_peers()):
                pltpu.make_async_remote_copy(
                    src_ref=buf_refs[a].at[me], dst_ref=buf_refs[a].at[me],
                    send_sem=sems[2 * a].at[k], recv_sem=sems[2 * a + 1].at[k],
                    device_id=peer, device_id_type=MESH_ID).start()
        token[...] = jnp.zeros_like(token)

    outs = pl.pallas_call(
        body, name=name,
        out_shape=tuple(pltpu.SemaphoreType.DMA((4,)) for _ in range(2 * n))
        + tuple(pltpu.HBM(b.shape, b.dtype) for b in bufs) + (jax.ShapeDtypeStruct((8, 128), F32),),
        in_specs=(HBM_SPEC,) * n,
        out_specs=(SEM_SPEC,) * (2 * n) + (HBM_SPEC,) * n + (pl.BlockSpec(memory_space=pltpu.VMEM),),
        input_output_aliases={a: 2 * n + a for a in range(n)},
        compiler_params=pltpu.CompilerParams(has_side_effects=DATAFLOW),
    )(*[pltpu.with_memory_space_constraint(b, pltpu.HBM) for b in bufs])
    return outs[:2 * n], outs[2 * n:3 * n], outs[3 * n]


def _gather_wait(sems, bufs, after, name):
    n = len(bufs)

    def body(*refs):
        buf_refs, sem_refs = refs[:n], refs[n:3 * n]
        x, y, c = _mesh_position()
        me = _linear((x, y, c))
        for a in range(n):
            for k, peer in enumerate(_same_core_peers()):
                cp = pltpu.make_async_remote_copy(
                    src_ref=buf_refs[a].at[me], dst_ref=buf_refs[a].at[_linear(peer)],
                    send_sem=sem_refs[2 * a].at[k], recv_sem=sem_refs[2 * a + 1].at[k],
                    device_id=peer, device_id_type=MESH_ID)
                cp.wait_send()
                cp.wait_recv()

    return list(pl.pallas_call(
        body, name=name,
        out_shape=tuple(pltpu.HBM(b.shape, b.dtype) for b in bufs),
        in_specs=(HBM_SPEC,) * n + (SEM_SPEC,) * (2 * n) + (pl.BlockSpec(memory_space=pl.ANY),) * len(after),
        out_specs=(HBM_SPEC,) * n, input_output_aliases={a: a for a in range(n)},
        compiler_params=pltpu.CompilerParams(has_side_effects=DATAFLOW),
    )(*bufs, *sems, *after))


def _gather_pass_on(bufs, name):
    n = len(bufs)

    def body(*refs):
        out_refs = refs[n:2 * n]
        send_sems, recv_sems = refs[2 * n:]
        x, y, c = _mesh_position()
        sibling = (x, y, 1 - c)
        chips = [(1 - x, y), (x, 1 - y), (1 - x, 1 - y)]
        copies = []
        for a in range(n):
            for j, chip in enumerate(chips):
                block = out_refs[a].at[_linear((*chip, c))]
                copies.append(pltpu.make_async_remote_copy(
                    src_ref=block, dst_ref=block, send_sem=send_sems.at[3 * a + j], recv_sem=recv_sems.at[3 * a + j],
                    device_id=sibling, device_id_type=MESH_ID))
                copies[-1].start()
        for a in range(n):
            for j, chip in enumerate(chips):
                copies[3 * a + j].wait_send()
                theirs = out_refs[a].at[_linear((*chip, 1 - c))]
                pltpu.make_async_remote_copy(
                    src_ref=theirs, dst_ref=theirs, send_sem=send_sems.at[3 * a + j], recv_sem=recv_sems.at[3 * a + j],
                    device_id=sibling, device_id_type=MESH_ID).wait_recv()

    hbm = pl.BlockSpec(memory_space=pl.ANY)
    return list(pl.pallas_call(
        body, name=name,
        out_shape=[jax.ShapeDtypeStruct(b.shape, b.dtype) for b in bufs],
        in_specs=[hbm] * n, out_specs=[hbm] * n, input_output_aliases={a: a for a in range(n)},
        scratch_shapes=[pltpu.SemaphoreType.DMA((3 * n,)), pltpu.SemaphoreType.DMA((3 * n,))],
    )(*bufs))


def _silu_rows(c):
    def body(c_ref, o_ref):
        v = c_ref[...]
        o_ref[...] = v * _sigmoid(v)

    return pl.pallas_call(body, name="cond_silu", out_shape=jax.ShapeDtypeStruct(c.shape, F32))(c)


def _mod_columns(cond_all, w_ada, b_cols):
    def body(c_ref, w_ref, b_ref, o_ref):
        o_ref[...] = _dot(c_ref[...], w_ref[...]) + b_ref[...]

    return pl.pallas_call(body, name="mod_columns",
                          out_shape=jax.ShapeDtypeStruct((N_DEV, w_ada.shape[1]), F32))(cond_all, w_ada, b_cols)


def _in_proj(x, mod, g_norm1, w_in, tm):
    s = x.shape[0]

    def body(x_ref, mod_ref, g_ref, w_ref, h_ref, q_ref, kv_ref, gb_ref, gc_ref, xc_ref):
        xf = x_ref[...]
        n = xf * _rsqrt_mean_sq(xf) * g_ref[...]
        h = (n * (1.0 + mod_ref[SC1:SC1 + 1, :]) + mod_ref[SH1:SH1 + 1, :]).astype(BF16)
        h_ref[...] = h
        p = _dot_nt(h, w_ref[...])
        q_ref[...] = p[:, 0:512].astype(BF16)
        kv_ref[...] = p[:, 512:768].astype(BF16)
        gb_ref[...] = p[:, 768:1280].astype(BF16)
        gc_ref[...] = p[:, 1280:1792].astype(BF16)
        xc_ref[...] = p[:, 1792:2304].astype(BF16)

    return pl.pallas_call(
        body, name="in_proj", grid=(s // tm,),
        in_specs=[_rows(tm, D_MODEL), _full((8, D_MODEL)), _full((1, D_MODEL)), _full((IN_PROJ_WIDTH, D_MODEL))],
        out_specs=[_rows(tm, D_MODEL), _rows(tm, 512), _rows(tm, 256), _rows(tm, 512), _rows(tm, 512), _rows(tm, 512)],
        out_shape=[jax.ShapeDtypeStruct((s, D_MODEL), BF16), jax.ShapeDtypeStruct((s, 512), BF16),
                   jax.ShapeDtypeStruct((s, 256), BF16), jax.ShapeDtypeStruct((s, 512), BF16),
                   jax.ShapeDtypeStruct((s, 512), BF16), jax.ShapeDtypeStruct((s, 512), BF16)],
        compiler_params=_params(("arbitrary",), VMEM_LIMIT_LARGE),
    )(x, mod, g_norm1, w_in)


def _t5_bucket(dist):
    max_exact = N_BUCKETS // 2
    is_small = dist < max_exact
    d = jnp.maximum(dist, 1).astype(F32)
    large = max_exact + (jnp.log(d / max_exact) / math.log(MAX_DISTANCE / max_exact)
                         * (N_BUCKETS - max_exact)).astype(jnp.int32)
    large = jnp.minimum(large, N_BUCKETS - 1)
    return jnp.where(is_small, dist, large)


def _bucket_table():
    qi = jnp.arange(BLOCK, dtype=jnp.int32)[:, None]
    sj = jnp.arange(2 * BLOCK, dtype=jnp.int32)[None, :]
    return _t5_bucket(jnp.maximum(qi + BLOCK - sj, 0))


def _window_mask():
    qi = lax.broadcasted_iota(jnp.int32, (BLOCK, 2 * BLOCK), 0)
    sj = lax.broadcasted_iota(jnp.int32, (BLOCK, 2 * BLOCK), 1)
    dist = qi + BLOCK - sj
    return (dist >= 0) & (dist < BLOCK)


def _bias_table(rel_bias, bucket):
    def body(rb_ref, bk_ref, o_ref):
        bk = bk_ref[...]
        inside = _window_mask()
        for h in range(N_Q_HEADS):
            acc = jnp.zeros((BLOCK, 2 * BLOCK), F32)
            for b in range(N_BUCKETS):
                acc = jnp.where(bk == b, rb_ref[b, h], acc)
            o_ref[h] = jnp.where(inside, acc, NEG_INF)

    return pl.pallas_call(
        body, name="bias_table",
        in_specs=[pl.BlockSpec(memory_space=pltpu.SMEM), pl.BlockSpec(memory_space=pltpu.VMEM)],
        out_shape=jax.ShapeDtypeStruct((N_Q_HEADS, BLOCK, 2 * BLOCK), F32),
    )(rel_bias, bucket)


def _load_kv_window(kv_ref, n):
    prev = jnp.maximum(n - 1, 0)
    kvw = jnp.concatenate([kv_ref[pl.ds(pl.multiple_of(prev * BLOCK, BLOCK), BLOCK), :],
                           kv_ref[pl.ds(pl.multiple_of(n * BLOCK, BLOCK), BLOCK), :]], axis=0)
    k, v = kvw[:, 0:128], kvw[:, 128:256]
    k_sw = pltpu.roll(k.astype(F32), 64, 1).astype(BF16)
    v_sw = pltpu.roll(v.astype(F32), 64, 1).astype(BF16)
    return (k, k_sw), (v, v_sw)


def _conv_taps(gc, xc, gc_prev, xc_prev, n):
    u = gc * xc
    before = jnp.where(n > 0, gc_prev.astype(F32) * xc_prev.astype(F32), 0.0)
    last = before.shape[0] - 1
    row = lax.broadcasted_iota(jnp.int32, u.shape, 0)
    u1 = jnp.where(row == 0, before[last:last + 1, :], pltpu.roll(u, 1, 0))
    u2 = jnp.where(row == 0, before[last - 1:last, :],
                   jnp.where(row == 1, before[last:last + 1, :], pltpu.roll(u, 2, 0)))
    return u, u1, u2


def _mixer_fwd(q, kv, gb, gc, xc, bias, sinks, conv_w, g_attn, g_conv):
    s = q.shape[0]
    nb = s // BLOCK

    per_step = min(MIXER_BLOCKS, nb)
    tile = per_step * BLOCK

    def one_block(n, rows, before, sink_ref, q_ref, kv_ref, gb_ref, gc_ref, xc_ref, bias_ref, cw_ref, ga_ref,
                  gcv_ref, attn_ref, merged_ref, lse_ref):
        ks, vs = _load_kv_window(kv_ref, n)
        lane = lax.broadcasted_iota(jnp.int32, (BLOCK, BLOCK), 1)
        low = lane < HEAD_DIM
        col = lax.broadcasted_iota(jnp.int32, (BLOCK, 2 * BLOCK), 1)
        no_prev = (col < BLOCK) & (n == 0)
        lse_all = jnp.zeros((BLOCK, BLOCK), F32)
        pairs = []
        for p in range(4):
            qp = q_ref[rows, 128 * p:128 * (p + 1)].astype(F32)
            kvh = p // 2
            res = []
            for e in range(2):
                h = 2 * p + e
                qm = jnp.where(low if e == 0 else ~low, qp, 0.0).astype(BF16)
                sw = 0 if kvh == e else 1
                sc = _dot_nt(qm, ks[sw]) * SCALE + bias_ref[h]
                sc = jnp.where(no_prev, NEG_INF, sc)
                sink = sink_ref[h]
                m = jnp.maximum(jnp.max(sc, axis=-1, keepdims=True), sink)
                pe = jnp.exp(sc - m)
                den = jnp.sum(pe, axis=-1, keepdims=True) + jnp.exp(sink - m)
                res.append(_dot(pe.astype(BF16), vs[sw]) / den)
                lse_all = lse_all + jnp.where(lane == h, m + jnp.log(den), 0.0)
            pairs.append(jnp.where(low, res[0], res[1]))
        attn = jnp.concatenate(pairs, axis=1)
        attn_ref[rows, :] = attn
        lse_ref[rows, :] = lse_all
        u, u1, u2 = _conv_taps(gc_ref[rows, :].astype(F32), xc_ref[rows, :].astype(F32), before[0], before[1], n)
        cw = cw_ref[...]
        cv = gb_ref[rows, :].astype(F32) * (cw[0:1, :] * u2 + cw[1:2, :] * u1 + cw[2:3, :] * u)
        an = attn * _rsqrt_mean_sq(attn) * ga_ref[...]
        cn = cv * _rsqrt_mean_sq(cv) * gcv_ref[...]
        merged_ref[rows, :] = jnp.concatenate([an, cn], axis=1).astype(BF16)

    def body(sink_ref, q_ref, kv_ref, gb_ref, gc_ref, xc_ref, gcp_ref, xcp_ref, *rest):
        step = pl.program_id(0)
        for sub in range(per_step):
            rows = slice(sub * BLOCK, (sub + 1) * BLOCK)
            ahead = slice(sub * BLOCK - PREV_ROWS, sub * BLOCK)
            before = (gcp_ref[...], xcp_ref[...]) if sub == 0 else (gc_ref[ahead, :], xc_ref[ahead, :])
            one_block(step * per_step + sub, rows, before, sink_ref, q_ref, kv_ref, gb_ref, gc_ref, xc_ref, *rest)

    blk = lambda w: pl.BlockSpec((tile, w), lambda n: (n, 0))
    prev8 = pl.BlockSpec((PREV_ROWS, 512), lambda n: (jnp.maximum(n * (tile // PREV_ROWS) - 1, 0), 0))
    return pl.pallas_call(
        body, name="mixer_fwd", grid=(nb // per_step,),
        in_specs=[pl.BlockSpec(memory_space=pltpu.SMEM), blk(512), _full((s, 256)), blk(512), blk(512), blk(512),
                  prev8, prev8, _full((N_Q_HEADS, BLOCK, 2 * BLOCK)), _full((3, 512)), _full((1, 512)),
                  _full((1, 512))],
        out_specs=[blk(512), blk(1024), blk(128)],
        out_shape=[jax.ShapeDtypeStruct((s, 512), F32), jax.ShapeDtypeStruct((s, 1024), BF16),
                   jax.ShapeDtypeStruct((s, 128), F32)],
        compiler_params=_params(("arbitrary",)),
    )(sinks, q, kv, gb, gc, xc, gc, xc, bias, conv_w, g_attn, g_conv)


def _out_proj(merged, x, mod, w_out, tm):
    s = x.shape[0]

    def body(m_ref, x_ref, mod_ref, w_ref, o_ref, x1_ref):
        o = _dot(m_ref[...], w_ref[...])
        o_ref[...] = o.astype(BF16)
        x1_ref[...] = x_ref[...] + mod_ref[G1:G1 + 1, :] * o

    return pl.pallas_call(
        body, name="out_proj", grid=(s // tm,),
        in_specs=[_rows(tm, D_MODEL), _rows(tm, D_MODEL), _full((8, D_MODEL)), _full((D_MODEL, D_MODEL))],
        out_specs=[_rows(tm, D_MODEL), _rows(tm, D_MODEL)],
        out_shape=[jax.ShapeDtypeStruct((s, D_MODEL), BF16), jax.ShapeDtypeStruct((s, D_MODEL), F32)],
        compiler_params=_params(("arbitrary",)),
    )(merged, x, mod, w_out)


def _resident(shape):
    nd = len(shape)
    return pl.BlockSpec(shape, lambda *_: (0,) * nd, pipeline_mode=pl.Buffered(1))


def _ffn_fwd(x1, mod, g_norm2, w_gu, w_down, g_final, target, tm):
    s = x1.shape[0]
    chunk = D_FF // FFN_CHUNKS

    def body(x_ref, mod_ref, g_ref, wgu_ref, wd_ref, gf_ref, t_ref,
             h_ref, gate_ref, up_ref, act_ref, o_ref, dx2_ref, small_ref):
        @pl.when(pl.program_id(0) == 0)
        def _():
            small_ref[...] = jnp.zeros_like(small_ref)

        xf = x_ref[...]
        n = xf * _rsqrt_mean_sq(xf) * g_ref[...]
        h = (n * (1.0 + mod_ref[SC2:SC2 + 1, :]) + mod_ref[SH2:SH2 + 1, :]).astype(BF16)
        h_ref[...] = h
        o = None
        for j in range(FFN_CHUNKS):
            lo = j * chunk
            gate = _dot_nt(h, wgu_ref[lo:lo + chunk, :])
            up = _dot_nt(h, wgu_ref[D_FF + lo:D_FF + lo + chunk, :])
            gate_ref[:, lo:lo + chunk] = gate.astype(BF16)
            up_ref[:, lo:lo + chunk] = up.astype(BF16)
            act = (gate * _sigmoid(gate) * up).astype(BF16)
            act_ref[:, lo:lo + chunk] = act
            part = _dot(act, wd_ref[lo:lo + chunk, :])
            o = part if o is None else o + part
        o_ref[...] = o.astype(BF16)
        x2 = xf + mod_ref[G2:G2 + 1, :] * o
        r = _rsqrt_mean_sq(x2)
        xn = x2 * r
        gf = gf_ref[...]
        err = xn * gf - t_ref[...]
        dy = err * (1.0 / D_MODEL)
        dxn = dy * gf
        dx2_ref[...] = (r * (dxn - xn * jnp.mean(dxn * xn, axis=-1, keepdims=True))).astype(BF16)
        small_ref[0:1, :] += _colsum(dy * xn)
        small_ref[1:2, :] += _colsum(err * err)

        @pl.when(pl.program_id(0) == pl.num_programs(0) - 1)
        def _():
            total = jnp.sum(small_ref[1:2, :], axis=-1, keepdims=True) * (0.5 / D_MODEL)
            small_ref[2:3, :] = jnp.broadcast_to(total, (1, D_MODEL))

    wide = jax.ShapeDtypeStruct((s, D_FF), BF16)
    return pl.pallas_call(
        body, name="ffn_fwd", grid=(s // tm,),
        in_specs=[_rows(tm, D_MODEL), _full((8, D_MODEL)), _full((1, D_MODEL)), _resident((2 * D_FF, D_MODEL)),
                  _resident((D_FF, D_MODEL)), _full((1, D_MODEL)), _rows(tm, D_MODEL)],
        out_specs=[_rows(tm, D_MODEL), _rows(tm, D_FF), _rows(tm, D_FF), _rows(tm, D_FF), _rows(tm, D_MODEL),
                   _rows(tm, D_MODEL), _full((8, D_MODEL))],
        out_shape=[jax.ShapeDtypeStruct((s, D_MODEL), BF16), wide, wide, wide,
                   jax.ShapeDtypeStruct((s, D_MODEL), BF16), jax.ShapeDtypeStruct((s, D_MODEL), BF16),
                   jax.ShapeDtypeStruct((8, D_MODEL), F32)],
        compiler_params=_params(("arbitrary",), VMEM_LIMIT_LARGE),
    )(x1, mod, g_norm2, w_gu, w_down, g_final, target)


def _ffn_bwd(dx2, o2, gate, up, x1, mod, g_norm2, w_down, w_gu, tm):
    s = x1.shape[0]
    chunk = D_FF // FFN_CHUNKS

    def body(dx_ref, o_ref, gate_ref, up_ref, x_ref, mod_ref, g_ref, wd_ref, wgu_ref,
             do_ref, dgu_ref, dx1_ref, small_ref):
        @pl.when(pl.program_id(0) == 0)
        def _():
            small_ref[...] = jnp.zeros_like(small_ref)

        dx = dx_ref[...].astype(F32)
        small_ref[3:4, :] += _colsum(dx * o_ref[...].astype(F32))
        do = (dx * mod_ref[G2:G2 + 1, :]).astype(BF16)
        do_ref[...] = do
        dh = None
        for j in range(FFN_CHUNKS):
            lo = j * chunk
            dact = _dot_nt(do, wd_ref[lo:lo + chunk, :])
            gate = gate_ref[:, lo:lo + chunk].astype(F32)
            sg = _sigmoid(gate)
            dgate = (dact * up_ref[:, lo:lo + chunk].astype(F32) * (sg * (1.0 + gate * (1.0 - sg)))).astype(BF16)
            dup = (dact * (gate * sg)).astype(BF16)
            dgu_ref[:, lo:lo + chunk] = dgate
            dgu_ref[:, D_FF + lo:D_FF + lo + chunk] = dup
            part = _dot(dgate, wgu_ref[lo:lo + chunk, :]) + _dot(dup, wgu_ref[D_FF + lo:D_FF + lo + chunk, :])
            dh = part if dh is None else dh + part
        dx1 = dx + _norm_mod_bwd(dh, x_ref[...], g_ref[...], mod_ref[SC2:SC2 + 1, :], small_ref)
        dx1_ref[...] = dx1.astype(BF16)

    return pl.pallas_call(
        body, name="ffn_bwd", grid=(s // tm,),
        in_specs=[_rows(tm, D_MODEL), _rows(tm, D_MODEL), _rows(tm, D_FF), _rows(tm, D_FF), _rows(tm, D_MODEL),
                  _full((8, D_MODEL)), _full((1, D_MODEL)), _resident((D_FF, D_MODEL)),
                  _resident((2 * D_FF, D_MODEL))],
        out_specs=[_rows(tm, D_MODEL), _rows(tm, 2 * D_FF), _rows(tm, D_MODEL), _full((8, D_MODEL))],
        out_shape=[jax.ShapeDtypeStruct((s, D_MODEL), BF16), jax.ShapeDtypeStruct((s, 2 * D_FF), BF16),
                   jax.ShapeDtypeStruct((s, D_MODEL), BF16), jax.ShapeDtypeStruct((8, D_MODEL), F32)],
        compiler_params=_params(("arbitrary",), VMEM_LIMIT_LARGE),
    )(dx2, o2, gate, up, x1, mod, g_norm2, w_down, w_gu)


def _norm_mod_bwd(dh, xf, g, scale_row, small_ref):
    r = _rsqrt_mean_sq(xf)
    xn = xf * r
    small_ref[0:1, :] += _colsum(dh)
    small_ref[1:2, :] += _colsum(dh * (xn * g))
    dn = dh * (1.0 + scale_row)
    small_ref[2:3, :] += _colsum(dn * xn)
    dxn = dn * g
    return r * (dxn - xn * jnp.mean(dxn * xn, axis=-1, keepdims=True))


def _out_proj_bwd(dx1, o1, mod, w_out, tm):
    s = dx1.shape[0]

    def body(dx_ref, o_ref, mod_ref, w_ref, do_ref, dm_ref, small_ref):
        @pl.when(pl.program_id(0) == 0)
        def _():
            small_ref[...] = jnp.zeros_like(small_ref)

        dx = dx_ref[...].astype(F32)
        small_ref[0:1, :] += _colsum(dx * o_ref[...].astype(F32))
        do = (dx * mod_ref[G1:G1 + 1, :]).astype(BF16)
        do_ref[...] = do
        dm_ref[...] = _dot_nt(do, w_ref[...]).astype(BF16)

    return pl.pallas_call(
        body, name="out_proj_bwd", grid=(s // tm,),
        in_specs=[_rows(tm, D_MODEL), _rows(tm, D_MODEL), _full((8, D_MODEL)), _full((D_MODEL, D_MODEL))],
        out_specs=[_rows(tm, D_MODEL), _rows(tm, D_MODEL), _full((8, D_MODEL))],
        out_shape=[jax.ShapeDtypeStruct((s, D_MODEL), BF16), jax.ShapeDtypeStruct((s, D_MODEL), BF16),
                   jax.ShapeDtypeStruct((8, D_MODEL), F32)],
        compiler_params=_params(("arbitrary",)),
    )(dx1, o1, mod, w_out)


def _group_norm_bwd(dm, a, g):
    r = _rsqrt_mean_sq(a)
    an = a * r
    dan = dm * g
    return r * (dan - an * jnp.mean(dan * an, axis=-1, keepdims=True)), _colsum(dm * an)


def _mixer_bwd(q, kv, gb, gc, xc, bias, sinks, conv_w, g_attn, g_conv, attn, lse, dmerged):
    s = q.shape[0]
    nb = s // BLOCK

    per_step = min(MIXER_BLOCKS, nb)
    tile = per_step * BLOCK
    steps = nb // per_step

    def one_block(n, rows, before, nxt, sink_ref, q_ref, kv_ref, gb_ref, gc_ref, xc_ref, bias_ref, cw_ref, ga_ref,
                  gcv_ref, attn_ref, lse_ref, dm_ref, dproj_ref, dbias_ref, dsink_ref, small_ref):
        next_dy, next_dkv = nxt
        dm = dm_ref[rows, :].astype(F32)
        gbv, gcv_, xcv = gb_ref[rows, :].astype(F32), gc_ref[rows, :].astype(F32), xc_ref[rows, :].astype(F32)
        u, u1, u2 = _conv_taps(gcv_, xcv, before[0], before[1], n)
        cw = cw_ref[...]
        yv = cw[0:1, :] * u2 + cw[1:2, :] * u1 + cw[2:3, :] * u
        dcv, dg_conv = _group_norm_bwd(dm[:, 512:1024], gbv * yv, gcv_ref[...])
        small_ref[1:2, :] += dg_conv
        dproj_ref[rows, 768:1280] = (dcv * yv).astype(BF16)
        dy = dcv * gbv
        row = lax.broadcasted_iota(jnp.int32, dy.shape, 0)
        d1 = jnp.where(row == BLOCK - 1, next_dy[0:1, :], pltpu.roll(dy, BLOCK - 1, 0))
        d2 = jnp.where(row == BLOCK - 2, next_dy[0:1, :],
                       jnp.where(row == BLOCK - 1, next_dy[1:2, :], pltpu.roll(dy, BLOCK - 2, 0)))
        du = cw[2:3, :] * dy + cw[1:2, :] * d1 + cw[0:1, :] * d2
        dproj_ref[rows, 1280:1792] = (du * xcv).astype(BF16)
        dproj_ref[rows, 1792:2304] = (du * gcv_).astype(BF16)
        small_ref[2:3, :] += _colsum(dy * u2)
        small_ref[3:4, :] += _colsum(dy * u1)
        small_ref[4:5, :] += _colsum(dy * u)

        attn_v = attn_ref[rows, :]
        dout, dg_attn = _group_norm_bwd(dm[:, 0:512], attn_v, ga_ref[...])
        small_ref[0:1, :] += dg_attn
        ks, vs = _load_kv_window(kv_ref, n)
        lane = lax.broadcasted_iota(jnp.int32, (BLOCK, BLOCK), 1)
        low = lane < HEAD_DIM
        col = lax.broadcasted_iota(jnp.int32, (BLOCK, 2 * BLOCK), 1)
        no_prev = (col < BLOCK) & (n == 0)
        lse_all = lse_ref[rows, :]
        dsink = jnp.zeros((BLOCK, BLOCK), F32)
        dq_pairs = []
        dk_groups, dv_groups = [], []
        for kvh in range(2):
            ds_rows, pr_rows, q_rows, do_rows = [], [], [], []
            for p in (2 * kvh, 2 * kvh + 1):
                qp = q_ref[rows, 128 * p:128 * (p + 1)].astype(F32)
                do_p = dout[:, 128 * p:128 * (p + 1)]
                prod = do_p * attn_v[:, 128 * p:128 * (p + 1)]
                res = []
                for e in range(2):
                    h = 2 * p + e
                    half = low if e == 0 else ~low
                    qm = jnp.where(half, qp, 0.0).astype(BF16)
                    dom = jnp.where(half, do_p, 0.0).astype(BF16)
                    delta = jnp.sum(jnp.where(half, prod, 0.0), axis=-1, keepdims=True)
                    lse_h = jnp.sum(jnp.where(lane == h, lse_all, 0.0), axis=-1, keepdims=True)
                    sw = 0 if kvh == e else 1
                    sc = _dot_nt(qm, ks[sw]) * SCALE + bias_ref[h]
                    sc = jnp.where(no_prev, NEG_INF, sc)
                    pr = jnp.exp(sc - lse_h)
                    dp = _dot_nt(dom, vs[sw])
                    ds = pr * (dp - delta)
                    dbias_ref[h] += ds
                    dsink = dsink + jnp.where(lane == h, -jnp.exp(sink_ref[h] - lse_h) * delta, 0.0)
                    dsb = ds.astype(BF16)
                    res.append(_dot(dsb, ks[sw]) * SCALE)
                    ds_rows.append(dsb)
                    pr_rows.append(pr.astype(BF16))
                    q_rows.append(qm)
                    do_rows.append(dom)
                dq_pairs.append(jnp.where(low, res[0], res[1]))
            dk_g = _dot_tn(jnp.concatenate(ds_rows, axis=0), jnp.concatenate(q_rows, axis=0)) * SCALE
            dv_g = _dot_tn(jnp.concatenate(pr_rows, axis=0), jnp.concatenate(do_rows, axis=0))
            dk_groups.append(dk_g + pltpu.roll(dk_g, 64, 1))
            dv_groups.append(dv_g + pltpu.roll(dv_g, 64, 1))
        dproj_ref[rows, 0:512] = jnp.concatenate(dq_pairs, axis=1).astype(BF16)
        dsink_ref[...] += dsink
        low_kv = lax.broadcasted_iota(jnp.int32, (2 * BLOCK, BLOCK), 1) < HEAD_DIM
        dkv_win = jnp.concatenate([jnp.where(low_kv, dk_groups[0], dk_groups[1]),
                                   jnp.where(low_kv, dv_groups[0], dv_groups[1])], axis=1)
        dproj_ref[rows, 512:768] = (dkv_win[BLOCK:2 * BLOCK, :] + next_dkv).astype(BF16)
        return dy[0:8, :], dkv_win[0:BLOCK, :]

    def body(sink_ref, q_ref, kv_ref, gb_ref, gc_ref, xc_ref, gcp_ref, xcp_ref, *rest):
        refs, dy_ref, dkv_ref = rest[:-2], rest[-2], rest[-1]
        dbias_ref, dsink_ref, small_ref = refs[8], refs[9], refs[10]
        step = pl.program_id(0)

        @pl.when(step == 0)
        def _():
            dbias_ref[...] = jnp.zeros_like(dbias_ref)
            dsink_ref[...] = jnp.zeros_like(dsink_ref)
            small_ref[...] = jnp.zeros_like(small_ref)
            dy_ref[...] = jnp.zeros_like(dy_ref)
            dkv_ref[...] = jnp.zeros_like(dkv_ref)

        nxt = (dy_ref[...], dkv_ref[...])
        for sub in reversed(range(per_step)):
            rows = slice(sub * BLOCK, (sub + 1) * BLOCK)
            ahead = slice(sub * BLOCK - PREV_ROWS, sub * BLOCK)
            before = (gcp_ref[...], xcp_ref[...]) if sub == 0 else (gc_ref[ahead, :], xc_ref[ahead, :])
            nxt = one_block((steps - 1 - step) * per_step + sub, rows, before, nxt,
                            sink_ref, q_ref, kv_ref, gb_ref, gc_ref, xc_ref, *refs)
        dy_ref[...], dkv_ref[...] = nxt

        @pl.when(step == steps - 1)
        def _():
            small_ref[5:6, :] = jnp.concatenate([_colsum(dsink_ref[...]), jnp.zeros((1, 512 - BLOCK), F32)], axis=1)

    blk = lambda w: pl.BlockSpec((tile, w), lambda t: (steps - 1 - t, 0))
    prev8 = pl.BlockSpec((PREV_ROWS, 512),
                         lambda t: (jnp.maximum((steps - 1 - t) * (tile // PREV_ROWS) - 1, 0), 0))
    bf = lambda w: jax.ShapeDtypeStruct((s, w), BF16)
    return pl.pallas_call(
        body, name="mixer_bwd", grid=(steps,),
        in_specs=[pl.BlockSpec(memory_space=pltpu.SMEM), blk(512), _full((s, 256)), blk(512), blk(512), blk(512),
                  prev8, prev8, _full((N_Q_HEADS, BLOCK, 2 * BLOCK)), _full((3, 512)), _full((1, 512)),
                  _full((1, 512)), blk(512), blk(128), blk(1024)],
        out_specs=[blk(IN_PROJ_WIDTH), _full((N_Q_HEADS, BLOCK, 2 * BLOCK)), _full((BLOCK, BLOCK)), _full((8, 512))],
        out_shape=[bf(IN_PROJ_WIDTH), jax.ShapeDtypeStruct((N_Q_HEADS, BLOCK, 2 * BLOCK), F32),
                   jax.ShapeDtypeStruct((BLOCK, BLOCK), F32), jax.ShapeDtypeStruct((8, 512), F32)],
        scratch_shapes=[pltpu.VMEM((8, 512), F32), pltpu.VMEM((BLOCK, 2 * KV_WIDTH), F32)],
        compiler_params=_params(("arbitrary",), VMEM_LIMIT_LARGE),
    )(sinks, q, kv, gb, gc, xc, gc, xc, bias, conv_w, g_attn, g_conv, attn, lse, dmerged)


def _in_proj_bwd(dproj, x, dx1, mod, g_norm1, w_in, tm):
    s = x.shape[0]

    def body(dproj_ref, x_ref, dx1_ref, mod_ref, g_ref, w_ref, dx_ref, small_ref):
        @pl.when(pl.program_id(0) == 0)
        def _():
            small_ref[...] = jnp.zeros_like(small_ref)

        dh = _dot(dproj_ref[...], w_ref[...])
        dx_ref[...] = dx1_ref[...].astype(F32) + _norm_mod_bwd(dh, x_ref[...], g_ref[...], mod_ref[SC1:SC1 + 1, :],
                                                               small_ref)

    return pl.pallas_call(
        body, name="in_proj_bwd", grid=(s // tm,),
        in_specs=[_rows(tm, IN_PROJ_WIDTH), _rows(tm, D_MODEL), _rows(tm, D_MODEL), _full((8, D_MODEL)),
                  _full((1, D_MODEL)), _full((IN_PROJ_WIDTH, D_MODEL))],
        out_specs=[_rows(tm, D_MODEL), _full((8, D_MODEL))],
        out_shape=[jax.ShapeDtypeStruct((s, D_MODEL), F32), jax.ShapeDtypeStruct((8, D_MODEL), F32)],
        compiler_params=_params(("arbitrary",), VMEM_LIMIT_LARGE),
    )(dproj, x, dx1, mod, g_norm1, w_in)


def _weight_grad(a, b, tk, ts, name, after=None):
    s, k = a.shape
    n = b.shape[1]
    nt = s // ts
    extra = [] if after is None else [after]

    def body(a_ref, b_ref, *rest):
        o_ref, acc_ref = rest[-2:]
        t = pl.program_id(1)
        part = _dot_tn(a_ref[...], b_ref[...])

        @pl.when(t == 0)
        def _():
            acc_ref[...] = part

        @pl.when(t > 0)
        def _():
            acc_ref[...] += part

        @pl.when(t == nt - 1)
        def _():
            o_ref[...] = acc_ref[...].astype(BF16)

    return pl.pallas_call(
        body, name=name, grid=(k // tk, nt),
        in_specs=[pl.BlockSpec((ts, tk), lambda i, t: (t, i)), pl.BlockSpec((ts, n), lambda i, t: (t, 0))]
        + [pl.BlockSpec(memory_space=pl.ANY)] * len(extra),
        out_specs=pl.BlockSpec((tk, n), lambda i, t: (i, 0)),
        out_shape=jax.ShapeDtypeStruct((k, n), BF16),
        scratch_shapes=[pltpu.VMEM((tk, n), F32)],
        compiler_params=_params(("arbitrary", "arbitrary"), VMEM_LIMIT_LARGE),
    )(a, b, *extra)


def _rel_bias_grad(dbias, bucket):
    def body(db_ref, bk_ref, o_ref, rows_ref):
        bk = bk_ref[...]
        for b in range(N_BUCKETS):
            sel = (bk == b).astype(F32)
            for h in range(N_Q_HEADS):
                rows_ref[N_BUCKETS * h + b:N_BUCKETS * h + b + 1, :] = _colsum(db_ref[h] * sel)
        head = lax.broadcasted_iota(jnp.int32, (N_BUCKETS, N_Q_HEADS), 1)
        out = jnp.zeros((N_BUCKETS, N_Q_HEADS), F32)
        for h in range(N_Q_HEADS):
            per_bucket = jnp.sum(rows_ref[N_BUCKETS * h:N_BUCKETS * (h + 1), :], axis=-1, keepdims=True)
            out = out + jnp.where(head == h, per_bucket, 0.0)
        o_ref[...] = out

    return pl.pallas_call(
        body, name="rel_bias_grad",
        out_shape=jax.ShapeDtypeStruct((N_BUCKETS, N_Q_HEADS), F32),
        scratch_shapes=[pltpu.VMEM((N_BUCKETS * N_Q_HEADS, 2 * BLOCK), F32)],
    )(dbias, bucket)


def _lanes_from(x, start, width):
    n = x.shape[1]
    return pltpu.roll(x, (n - start) % n, 1)[:, 0:width]


def _w_ada_grad(me, cond_all, packed_all, cols):
    def body(me_ref, c_ref, p_ref, o_ref):
        dmod = jnp.concatenate([p_ref[k][:, OFF_DMOD:OFF_DMOD + N_MOD * D_MODEL] for k in range(N_DEV)], axis=0)
        mine = _lanes_from(dmod, me_ref[0] * cols, cols)
        pad = lambda a: jnp.concatenate([a, jnp.zeros((128 - N_DEV, a.shape[1]), F32)], axis=0)
        o_ref[...] = _dot_tn(pad(c_ref[...]), pad(mine))

    vmem = pl.BlockSpec(memory_space=pltpu.VMEM)
    return pl.pallas_call(body, name="w_ada_grad",
                          in_specs=[pl.BlockSpec(memory_space=pltpu.SMEM), vmem, vmem],
                          out_shape=jax.ShapeDtypeStruct((cond_all.shape[1], cols), F32))(me, cond_all, packed_all)


SMALL_PARAMS = (("rel_bias", None), ("b_ada", (OFF_DMOD, N_MOD * D_MODEL)), ("g_norm1", (OFF_GN1, D_MODEL)),
                ("sinks", (OFF_SINK, N_Q_HEADS)), ("conv_w", None), ("g_attn_out", (OFF_GATT, ATTN_WIDTH)),
                ("g_conv_out", (OFF_GCV, CONV_WIDTH)), ("g_norm2", (OFF_GN2, D_MODEL)),
                ("g_final", (OFF_GFIN, D_MODEL)))


def _small_update(me, packed_all, rel_all, state, after):
    n_p = len(SMALL_PARAMS)
    flat = [a for triple in state for a in triple]
    conv_cols = state[4][0].shape[1]

    def body(me_ref, p_ref, r_ref, *refs):
        ins = refs[:3 * n_p]
        loss_ref, outs = refs[3 * n_p + len(after)], refs[3 * n_p + len(after) + 1:]
        small, rel = p_ref[0], r_ref[0]
        for k in range(1, N_DEV):
            small = small + p_ref[k]
            rel = rel + r_ref[k]
        loss_ref[...] = small[:, OFF_LOSS:OFF_LOSS + 128]
        taps = jnp.concatenate([small[:, OFF_CONVW + CONV_WIDTH * j:OFF_CONVW + CONV_WIDTH * (j + 1)]
                                for j in range(3)] + [jnp.zeros((5, CONV_WIDTH), F32)], axis=0)
        conv_g = _lanes_from(taps, me_ref[0] * conv_cols, conv_cols)[0:3, :]
        for i, (name, lanes) in enumerate(SMALL_PARAMS):
            g = rel if name == "rel_bias" else conv_g if name == "conv_w" else small[:, lanes[0]:lanes[0] + lanes[1]]
            w_ref, m_ref, v_ref = ins[3 * i:3 * i + 3]
            outs[4 * i][...] = g
            outs[4 * i + 1][...], outs[4 * i + 2][...], outs[4 * i + 3][...] = _adam_math(
                w_ref[...], g, m_ref[...], v_ref[...])

    vmem = pl.BlockSpec(memory_space=pltpu.VMEM)
    out_shape = [jax.ShapeDtypeStruct((1, 128), F32)]
    for w, _, _ in state:
        out_shape += [jax.ShapeDtypeStruct(w.shape, F32)] * 4
    outs = pl.pallas_call(
        body, name="small_update",
        in_specs=[pl.BlockSpec(memory_space=pltpu.SMEM), vmem, vmem] + [vmem] * len(flat)
        + [pl.BlockSpec(memory_space=pl.ANY)] * len(after),
        out_shape=out_shape,
    )(me, packed_all, rel_all, *flat, *after)
    return outs[0], [tuple(outs[1 + 4 * i:5 + 4 * i]) for i in range(n_p)]


def _adam_math(w, g, m, v):
    m = ADAM_B1 * m + (1.0 - ADAM_B1) * g
    v = ADAM_B2 * v + (1.0 - ADAM_B2) * (g * g)
    m_hat = m / (1.0 - ADAM_B1 ** ADAM_STEP)
    v_hat = v / (1.0 - ADAM_B2 ** ADAM_STEP)
    delta = -ADAM_LR * (m_hat / (jnp.sqrt(v_hat) + ADAM_EPS) + ADAM_WD * w)
    return delta, m, v


def _adamw_parts(w, m, v, local, land, me, tr, name):
    r, c = w.shape

    def body(me_ref, w_ref, m_ref, v_ref, own_ref, land_ref, g_ref, d_ref, mo_ref, vo_ref):
        g = own_ref[0].astype(F32)
        for k in range(N_DEV - 1):
            g = g + land_ref[k].astype(F32)
        g_ref[...] = g
        d_ref[...], mo_ref[...], vo_ref[...] = _adam_math(w_ref[...], g, m_ref[...], v_ref[...])

    tile = pl.BlockSpec((tr, c), lambda i, me_ref: (i, 0))
    return pl.pallas_call(
        body, name=name,
        grid_spec=pltpu.PrefetchScalarGridSpec(
            num_scalar_prefetch=1, grid=(r // tr,),
            in_specs=[tile, tile, tile, pl.BlockSpec((1, tr, c), lambda i, me_ref: (me_ref[0], i, 0)),
                      pl.BlockSpec((N_DEV - 1, tr, c), lambda i, me_ref: (0, i, 0))],
            out_specs=[tile] * 4),
        out_shape=[jax.ShapeDtypeStruct((r, c), F32)] * 4,
        compiler_params=_params(("arbitrary",)),
    )(me, w, m, v, local, land)


def _adamw(w, m, v, g, tr, name):
    r, c = w.shape

    def body(w_ref, m_ref, v_ref, g_ref, d_ref, mo_ref, vo_ref):
        d_ref[...], mo_ref[...], vo_ref[...] = _adam_math(w_ref[...], g_ref[...], m_ref[...], v_ref[...])

    tile = pl.BlockSpec((tr, c), lambda i: (i, 0))
    return pl.pallas_call(
        body, name=name, grid=(r // tr,),
        in_specs=[tile] * 4, out_specs=[tile] * 3,
        out_shape=[jax.ShapeDtypeStruct((r, c), F32)] * 3,
        compiler_params=_params(("arbitrary",)),
    )(w, m, v, g)


def _behind(a, token):
    return a + token[0:a.shape[0], 0:1]


def _local_step(x, target, mod, w_in_t, weights_out_gu, weights_down, rel_bias, g_norm1, sinks, conv_w, g_attn,
                g_conv, g_norm2, g_final, exchange):
    s = x.shape[0]
    tm = min(512, s)
    tm_small = min(256, s)
    bucket = _bucket_table()
    bias = _bias_table(rel_bias, bucket)

    h, q, kv, gb, gc, xc = _in_proj(x, mod, g_norm1, w_in_t, tm)
    attn, merged, lse = _mixer_fwd(q, kv, gb, gc, xc, bias, sinks, conv_w, g_attn, g_conv)
    w_out, w_gu_t = weights_out_gu(merged)
    o1, x1 = _out_proj(merged, x, mod, w_out, tm)
    w_down = weights_down(x1)
    h2, gate, up, act, o2, dx2, fin = _ffn_fwd(x1, mod, g_norm2, w_gu_t, w_down, g_final, target, tm)

    do2, dgu, dx1, sm_2 = _ffn_bwd(dx2, o2, gate, up, x1, mod, g_norm2, w_down, w_gu_t, tm)
    ts = min(WEIGHT_GRAD_ROWS, s)
    tok_down = exchange("w_down", _weight_grad(act, do2, D_FF // 2, ts, "w_down_grad"))
    mod = _behind(mod, exchange("w_gu", _weight_grad(dgu, h2, D_FF // 2, ts, "w_gu_grad", after=tok_down)))
    do1, dmerged, sm_g1 = _out_proj_bwd(dx1, o1, mod, w_out, tm)
    g_attn_b = _behind(g_attn, exchange("w_out", _weight_grad(merged, do1, D_MODEL, ts, "w_out_grad")))
    dproj, dbias, dsink, sm_mix = _mixer_bwd(
        q, kv, gb, gc, xc, bias, sinks, conv_w, g_attn_b, g_conv, attn, lse, dmerged)
    mod = _behind(mod, exchange("w_in", _weight_grad(dproj, h, IN_PROJ_WIDTH // 2, ts, "w_in_grad")))
    dx, sm_1 = _in_proj_bwd(dproj, x, dx1, mod, g_norm1, w_in_t, tm)
    d_rel = _rel_bias_grad(dbias, bucket)

    packed = jnp.concatenate([
        sm_1[0], sm_1[1], sm_g1[0], sm_2[0], sm_2[1], sm_2[3],
        sm_1[2],
        sm_mix[5, 0:128],
        sm_mix[0], sm_mix[1],
        sm_2[2],
        fin[0],
        sm_mix[2], sm_mix[3], sm_mix[4],
        fin[2, 0:128],
    ])[None, :]
    return dx, packed, d_rel


def kernel(x, c, rel_bias, w_ada, b_ada, g_norm1, w_in, sinks, conv_w, g_attn_out, g_conv_out, w_out, g_norm2, w_gu, w_down, g_final, loss_target, m_rel_bias, m_w_ada, m_b_ada, m_g_norm1, m_w_in, m_sinks, m_conv_w, m_g_attn_out, m_g_conv_out, m_w_out, m_g_norm2, m_w_gu, m_w_down, m_g_final, v_rel_bias, v_w_ada, v_b_ada, v_g_norm1, v_w_in, v_sinks, v_conv_w, v_g_attn_out, v_g_conv_out, v_w_out, v_g_norm2, v_w_gu, v_w_down, v_g_final):
    me = _linear(_mesh_position())
    me_arr = jnp.reshape(me, (1,)).astype(jnp.int32)
    ada_cols = w_ada.shape[2]
    tm = min(512, x.shape[1])

    cond = _silu_rows(c)
    cond_all, conv_w_all = _all_gather_small([cond, conv_w[0]], "gather_cond")
    cond_all = cond_all[:, 0, :]
    conv_cols = conv_w.shape[2]
    conv_w_full = conv_w_all.transpose(1, 0, 2).reshape(3, CONV_WIDTH)
    b_cols = lax.dynamic_slice_in_dim(b_ada, me * ada_cols, ada_cols, axis=1)
    mod_cols = _mod_columns(cond_all, w_ada[0], b_cols)
    mod_all = _all_gather_small([mod_cols], "gather_mod")[0]
    mod = lax.dynamic_index_in_dim(mod_all, me, axis=1, keepdims=False).reshape(N_MOD, D_MODEL)
    mod = jnp.concatenate([mod, jnp.zeros((2, D_MODEL), F32)], axis=0)

    w_in_t = _all_gather([w_in[0].T], "gather_w_in", to_bf16=True, big=True)[0].reshape(IN_PROJ_WIDTH, D_MODEL)
    gather_sems, staged, gather_token = _gather_start(
        _stage_blocks([w_out[0], w_gu[0].T, w_down[0]], w_in_t, "stage_weights"), "gather_start_weights")
    mod = _behind(mod, gather_token)

    def weights_out_gu(after):
        got = _gather_pass_on(_gather_wait(gather_sems[0:4], staged[0:2], [after], "gather_wait_out_gu"),
                              "gather_pass_on_out_gu")
        return got[0].reshape(D_MODEL, D_MODEL), got[1].reshape(2 * D_FF, D_MODEL)

    def weights_down(after):
        got = _gather_pass_on(_gather_wait(gather_sems[4:6], staged[2:3], [after], "gather_wait_down"),
                              "gather_pass_on_down")
        return got[0].reshape(D_FF, D_MODEL)

    started = {}

    def exchange(name, dw):
        st = _exchange_start(dw.reshape(N_DEV, dw.shape[0] // N_DEV, dw.shape[1]), "exchange_start_" + name)
        started[name] = st
        return st[4]

    dx, packed, d_rel = _local_step(
        x[0], loss_target[0], mod, w_in_t, weights_out_gu, weights_down, rel_bias, g_norm1, sinks[0], conv_w_full,
        g_attn_out, g_conv_out, g_norm2, g_final[None, :], exchange)

    packed_all, rel_all = _all_gather_small([packed, d_rel], "gather_small_grads")
    g_ada = _w_ada_grad(me_arr, cond_all, packed_all, ada_cols)
    d_ada, nm_ada, nv_ada = _adamw(w_ada[0], m_w_ada[0], v_w_ada[0], g_ada, 256, "adamw_w_ada")
    as_rows = {"conv_w": lambda a: a[0], "g_final": lambda a: a[None, :]}
    small_state = {
        "rel_bias": (rel_bias, m_rel_bias, v_rel_bias), "b_ada": (b_ada, m_b_ada, v_b_ada),
        "g_norm1": (g_norm1, m_g_norm1, v_g_norm1), "sinks": (sinks, m_sinks, v_sinks),
        "conv_w": (conv_w, m_conv_w, v_conv_w), "g_attn_out": (g_attn_out, m_g_attn_out, v_g_attn_out),
        "g_conv_out": (g_conv_out, m_g_conv_out, v_g_conv_out), "g_norm2": (g_norm2, m_g_norm2, v_g_norm2),
        "g_final": (g_final, m_g_final, v_g_final),
    }
    state = [tuple(as_rows.get(name, lambda a: a)(a) for a in small_state[name]) for name, _ in SMALL_PARAMS]
    loss_row, small_out = _small_update(me_arr, packed_all, rel_all, state, [])
    loss = loss_row[0, 0]
    small_res = {name: tuple(a.reshape(small_state[name][0].shape) for a in res)
                 for (name, _), res in zip(SMALL_PARAMS, small_out)}

    def finish(name, after, w, m, v, tr):
        src, land = _exchange_wait(started[name], after, "exchange_wait_" + name)
        return _adamw_parts(w, m, v, src, land, me_arr, tr, "adamw_" + name)

    g_down, d_down, nm_down, nv_down = finish("w_down", [loss_row], w_down[0], m_w_down[0], v_w_down[0], 176)
    g_gu, d_gu, nm_gu, nv_gu = finish("w_gu", [nv_down], w_gu[0].T, m_w_gu[0].T, v_w_gu[0].T, 352)
    g_out, d_out, nm_out, nv_out = finish("w_out", [nv_gu], w_out[0], m_w_out[0], v_w_out[0], 128)
    g_in, d_in, nm_in, nv_in = finish("w_in", [nv_out, nv_ada], w_in[0].T, m_w_in[0].T, v_w_in[0].T, 144)

    big = {
        "w_ada": (g_ada[None], d_ada[None], nm_ada[None], nv_ada[None]),
        "w_in": (g_in.T[None], d_in.T[None], nm_in.T[None], nv_in.T[None]),
        "w_out": (g_out[None], d_out[None], nm_out[None], nv_out[None]),
        "w_gu": (g_gu.T[None], d_gu.T[None], nm_gu.T[None], nv_gu.T[None]),
        "w_down": (g_down[None], d_down[None], nm_down[None], nv_down[None]),
    }
    order = ["rel_bias", "w_ada", "b_ada", "g_norm1", "w_in", "sinks", "conv_w", "g_attn_out", "g_conv_out", "w_out",
             "g_norm2", "w_gu", "w_down", "g_final"]
    results = [big[k] if k in big else small_res[k] for k in order]
    return (loss, dx[None], *[r[0] for r in results], *[r[1] for r in results], *[r[2] for r in results],
            *[r[3] for r in results])
```

```python
import functools
import math

import jax
import jax.numpy as jnp
from jax import lax
from jax.experimental import pallas as pl
from jax.experimental.pallas import tpu as pltpu

F32 = jnp.float32
BF16 = jnp.bfloat16

D_MODEL = 1024
HEAD_DIM = 64
N_Q_HEADS = 8
ATTN_WIDTH = 512
KV_WIDTH = 128
CONV_WIDTH = 512
IN_PROJ_WIDTH = 2304
D_FF = 2816
N_MOD = 6
N_BUCKETS = 32
MAX_DISTANCE = 128
BLOCK = 128
EPS = 1e-6
NEG_INF = -1e30
SCALE = HEAD_DIM ** -0.5
N_DEV = 8

ADAM_LR = 0.001
ADAM_B1 = 0.9
ADAM_B2 = 0.999
ADAM_EPS = 1e-08
ADAM_WD = 0.01
ADAM_STEP = 10

SH1, SC1, G1, SH2, SC2, G2 = range(6)

VMEM_LIMIT_LARGE = 60 * 1024 * 1024
WEIGHT_GRAD_ROWS = 2048
FFN_CHUNKS = 2
PREV_ROWS = 16
MIXER_BLOCKS = 4
MESH_ID = pl.DeviceIdType.MESH

OFF_DMOD = 0
OFF_GN1 = OFF_DMOD + N_MOD * D_MODEL
OFF_SINK = OFF_GN1 + D_MODEL
OFF_GATT = OFF_SINK + 128
OFF_GCV = OFF_GATT + ATTN_WIDTH
OFF_GN2 = OFF_GCV + CONV_WIDTH
OFF_GFIN = OFF_GN2 + D_MODEL
OFF_CONVW = OFF_GFIN + D_MODEL
OFF_LOSS = OFF_CONVW + 3 * CONV_WIDTH
PACKED = OFF_LOSS + 128


def _params(sem=None, vmem=None):
    return pltpu.CompilerParams(dimension_semantics=sem, vmem_limit_bytes=vmem)


def _full(shape):
    nd = len(shape)
    return pl.BlockSpec(shape, lambda *_: (0,) * nd)


def _rows(tm, width):
    return pl.BlockSpec((tm, width), lambda i, *_: (i, 0))


def _sigmoid(x):
    return 1.0 / (1.0 + jnp.exp(-x))


def _rsqrt_mean_sq(x):
    return lax.rsqrt(jnp.mean(x * x, axis=-1, keepdims=True) + EPS)


def _colsum(x):
    return jnp.sum(x, axis=0, keepdims=True)


def _dot(a, b):
    return jnp.dot(a, b, preferred_element_type=F32)


def _dot_nt(a, b):
    return lax.dot_general(a, b, (((1,), (1,)), ((), ())), preferred_element_type=F32)


def _dot_tn(a, b):
    return lax.dot_general(a, b, (((0,), (0,)), ((), ())), preferred_element_type=F32)


def _mesh_position():
    return lax.axis_index("x"), lax.axis_index("y"), lax.axis_index("c")


def _linear(p):
    return 4 * p[0] + 2 * p[1] + p[2]


def _all_gather(arrs, name, to_bf16, big):
    n = len(arrs)
    out_dtype = BF16 if to_bf16 else F32

    def body(*refs):
        in_refs, out_refs = refs[:n], refs[n:2 * n]
        rest = refs[2 * n:]
        if to_bf16:
            stage, rest = rest[:n], rest[n:]
            for a in range(n):
                stage[a][...] = in_refs[a][...].astype(BF16)
            srcs = stage
        else:
            srcs = in_refs
        send_sems, recv_sems, local_sems = rest
        x, y, c = _mesh_position()
        me, sibling = (x, y, c), (x, y, 1 - c)
        chips = [(1 - x, y), (x, 1 - y), (1 - x, 1 - y)]

        def slot(a, p):
            return out_refs[a].at[_linear(p)]

        def copy(k, a, block, to, src=None):
            return pltpu.make_async_remote_copy(
                src_ref=slot(a, block) if src is None else src,
                dst_ref=slot(a, block),
                send_sem=send_sems.at[k * n + a],
                recv_sem=recv_sems.at[k * n + a],
                device_id=to,
                device_id_type=MESH_ID,
            )

        mine = [pltpu.make_async_copy(srcs[a], slot(a, me), local_sems.at[a]) for a in range(n)]
        for cp in mine:
            cp.start()
        first = [copy(0, a, me, sibling, src=srcs[a]) for a in range(n)]
        for j, chip in enumerate(chips):
            first += [copy(1 + j, a, me, (*chip, c), src=srcs[a]) for a in range(n)]
        for cp in first:
            cp.start()
        passed = []
        for j, chip in enumerate(chips):
            for a in range(n):
                copy(1 + j, a, (*chip, c), me).wait_recv()
                fwd = copy(4 + j, a, (*chip, c), sibling)
                fwd.start()
                passed.append(fwd)
        for a in range(n):
            copy(0, a, sibling, me).wait_recv()
        for j, chip in enumerate(chips):
            for a in range(n):
                copy(4 + j, a, (*chip, 1 - c), me).wait_recv()
        for cp in first + passed:
            cp.wait_send()
        for cp in mine:
            cp.wait()

    vmem = pl.BlockSpec(memory_space=pltpu.VMEM)
    out_space = pl.BlockSpec(memory_space=pl.ANY) if big else vmem
    scratch = [pltpu.VMEM(a.shape, BF16) for a in arrs] if to_bf16 else []
    scratch += [pltpu.SemaphoreType.DMA((7 * n,)), pltpu.SemaphoreType.DMA((7 * n,)),
                pltpu.SemaphoreType.DMA((n,))]
    outs = pl.pallas_call(
        body, name=name,
        out_shape=[jax.ShapeDtypeStruct((N_DEV,) + a.shape, out_dtype) for a in arrs],
        in_specs=[vmem] * n, out_specs=[out_space] * n,
        scratch_shapes=scratch,
        compiler_params=_params(vmem=VMEM_LIMIT_LARGE if big else None),
    )(*arrs)
    return list(outs)


def _peer(k):
    x, y, c = _mesh_position()
    return (1 - x if k & 4 else x, 1 - y if k & 2 else y, 1 - c if k & 1 else c)


def _all_gather_small(arrs, name):
    n = len(arrs)

    def body(*refs):
        in_refs, out_refs = refs[:n], refs[n:2 * n]
        send_sems, recv_sems, local_sems = refs[2 * n:]
        me = _linear(_mesh_position())
        mine = [pltpu.make_async_copy(in_refs[a], out_refs[a].at[me], local_sems.at[a]) for a in range(n)]
        for cp in mine:
            cp.start()
        sends = []
        for k in range(1, N_DEV):
            for a in range(n):
                sends.append(pltpu.make_async_remote_copy(
                    src_ref=in_refs[a], dst_ref=out_refs[a].at[me],
                    send_sem=send_sems.at[(k - 1) * n + a], recv_sem=recv_sems.at[(k - 1) * n + a],
                    device_id=_peer(k), device_id_type=MESH_ID))
                sends[-1].start()
        for k in range(1, N_DEV):
            for a in range(n):
                pltpu.make_async_remote_copy(
                    src_ref=in_refs[a], dst_ref=out_refs[a].at[_linear(_peer(k))],
                    send_sem=send_sems.at[(k - 1) * n + a], recv_sem=recv_sems.at[(k - 1) * n + a],
                    device_id=_peer(k), device_id_type=MESH_ID).wait_recv()
        for cp in sends:
            cp.wait_send()
        for cp in mine:
            cp.wait()

    vmem = pl.BlockSpec(memory_space=pltpu.VMEM)
    return list(pl.pallas_call(
        body, name=name,
        out_shape=[jax.ShapeDtypeStruct((N_DEV,) + a.shape, F32) for a in arrs],
        in_specs=[vmem] * n, out_specs=[vmem] * n,
        scratch_shapes=[pltpu.SemaphoreType.DMA((7 * n,)), pltpu.SemaphoreType.DMA((7 * n,)),
                        pltpu.SemaphoreType.DMA((n,))],
    )(*arrs))


HBM_SPEC = pl.BlockSpec(memory_space=pltpu.HBM)
SEM_SPEC = pl.BlockSpec(memory_space=pltpu.SEMAPHORE)
DATAFLOW = pltpu.SideEffectType.DATAFLOW_SIDE_EFFECTING


def _exchange_start(src, name):
    r, c = src.shape[1:]

    def body(src_ref, land_ref, send_sems, recv_sems, src_thru, land_thru, token):
        for k in range(1, N_DEV):
            peer = _peer(k)
            pltpu.make_async_remote_copy(
                src_ref=src_ref.at[_linear(peer)], dst_ref=land_ref.at[k - 1],
                send_sem=send_sems.at[k - 1], recv_sem=recv_sems.at[k - 1],
                device_id=peer, device_id_type=MESH_ID).start()
        token[...] = jnp.zeros_like(token)

    land = lax.empty((N_DEV - 1, r, c), src.dtype)
    return pl.pallas_call(
        body, name=name,
        out_shape=(pltpu.SemaphoreType.DMA((N_DEV - 1,)), pltpu.SemaphoreType.DMA((N_DEV - 1,)),
                   pltpu.HBM(src.shape, src.dtype), pltpu.HBM(land.shape, land.dtype),
                   jax.ShapeDtypeStruct((8, 128), F32)),
        in_specs=(HBM_SPEC, HBM_SPEC),
        out_specs=(SEM_SPEC, SEM_SPEC, HBM_SPEC, HBM_SPEC, pl.BlockSpec(memory_space=pltpu.VMEM)),
        input_output_aliases={0: 2, 1: 3},
        compiler_params=pltpu.CompilerParams(has_side_effects=DATAFLOW),
    )(pltpu.with_memory_space_constraint(src, pltpu.HBM), pltpu.with_memory_space_constraint(land, pltpu.HBM))


def _exchange_wait(started, after, name):
    send_sems, recv_sems, src_thru, land_thru, _ = started

    def body(src_ref, land_ref, send_sems, recv_sems, *rest):
        for k in range(1, N_DEV):
            cp = pltpu.make_async_remote_copy(
                src_ref=src_ref.at[0], dst_ref=land_ref.at[k - 1],
                send_sem=send_sems.at[k - 1], recv_sem=recv_sems.at[k - 1],
                device_id=_peer(k), device_id_type=MESH_ID)
            cp.wait_send()
            cp.wait_recv()

    return pl.pallas_call(
        body, name=name,
        out_shape=(pltpu.HBM(src_thru.shape, src_thru.dtype), pltpu.HBM(land_thru.shape, land_thru.dtype)),
        in_specs=(HBM_SPEC, HBM_SPEC, SEM_SPEC, SEM_SPEC) + (pl.BlockSpec(memory_space=pl.ANY),) * len(after),
        out_specs=(HBM_SPEC, HBM_SPEC), input_output_aliases={0: 0, 1: 1},
        compiler_params=pltpu.CompilerParams(has_side_effects=DATAFLOW),
    )(src_thru, land_thru, send_sems, recv_sems, *after)


def _share_start(arrs, zones, name):
    n = len(arrs)

    def body(*refs):
        src_refs, zone_refs, sems = refs[:n], refs[n:2 * n], refs[2 * n:4 * n]
        me = _linear(_mesh_position())
        for a in range(n):
            for k in range(1, N_DEV):
                pltpu.make_async_remote_copy(
                    src_ref=src_refs[a], dst_ref=zone_refs[a].at[me],
                    send_sem=sems[2 * a].at[k - 1], recv_sem=sems[2 * a + 1].at[k - 1],
                    device_id=_peer(k), device_id_type=MESH_ID).start()

    outs = pl.pallas_call(
        body, name=name,
        out_shape=tuple(pltpu.SemaphoreType.DMA((N_DEV - 1,)) for _ in range(2 * n))
        + tuple(pltpu.HBM(a.shape, a.dtype) for a in arrs) + tuple(pltpu.HBM(z.shape, z.dtype) for z in zones),
        in_specs=(HBM_SPEC,) * (2 * n),
        out_specs=(SEM_SPEC,) * (2 * n) + (HBM_SPEC,) * (2 * n),
        input_output_aliases={i: 2 * n + i for i in range(2 * n)},
        compiler_params=pltpu.CompilerParams(has_side_effects=DATAFLOW),
    )(*[pltpu.with_memory_space_constraint(a, pltpu.HBM) for a in list(arrs) + list(zones)])
    return outs[:2 * n], outs[2 * n:3 * n], outs[3 * n:]


def _share_wait(started, after, name):
    sems, arrs, zones = started
    n = len(arrs)

    def body(*refs):
        src_refs, zone_refs, sem_refs = refs[:n], refs[n:2 * n], refs[2 * n:4 * n]
        for a in range(n):
            for k in range(1, N_DEV):
                cp = pltpu.make_async_remote_copy(
                    src_ref=src_refs[a], dst_ref=zone_refs[a].at[_linear(_peer(k))],
                    send_sem=sem_refs[2 * a].at[k - 1], recv_sem=sem_refs[2 * a + 1].at[k - 1],
                    device_id=_peer(k), device_id_type=MESH_ID)
                cp.wait_send()
                cp.wait_recv()

    outs = pl.pallas_call(
        body, name=name,
        out_shape=tuple(pltpu.HBM(a.shape, a.dtype) for a in arrs) + tuple(pltpu.HBM(z.shape, z.dtype) for z in zones),
        in_specs=(HBM_SPEC,) * (2 * n) + (SEM_SPEC,) * (2 * n) + (pl.BlockSpec(memory_space=pl.ANY),) * len(after),
        out_specs=(HBM_SPEC,) * (2 * n), input_output_aliases={i: i for i in range(2 * n)},
        compiler_params=pltpu.CompilerParams(has_side_effects=DATAFLOW),
    )(*arrs, *zones, *sems, *after)
    return list(outs[n:])


def _stage_blocks(arrs, after, name):
    n = len(arrs)

    def body(*refs):
        in_refs, out_refs, stage, sems = refs[:n], refs[n + 1:2 * n + 1], refs[2 * n + 1:3 * n + 1], refs[3 * n + 1]
        me = _linear(_mesh_position())
        copies = []
        for a in range(n):
            stage[a][...] = in_refs[a][...].astype(BF16)
            copies.append(pltpu.make_async_copy(stage[a], out_refs[a].at[me], sems.at[a]))
            copies[-1].start()
        for cp in copies:
            cp.wait()

    return list(pl.pallas_call(
        body, name=name,
        out_shape=[jax.ShapeDtypeStruct((N_DEV,) + a.shape, BF16) for a in arrs],
        in_specs=[pl.BlockSpec(memory_space=pltpu.VMEM)] * n + [pl.BlockSpec(memory_space=pl.ANY)],
        out_specs=[pl.BlockSpec(memory_space=pl.ANY)] * n,
        scratch_shapes=[pltpu.VMEM(a.shape, BF16) for a in arrs] + [pltpu.SemaphoreType.DMA((n,))],
        compiler_params=_params(vmem=VMEM_LIMIT_LARGE),
    )(*arrs, after))


def _same_core_peers():
    x, y, c = _mesh_position()
    return [(x, y, 1 - c), (1 - x, y, c), (x, 1 - y, c), (1 - x, 1 - y, c)]


def _gather_start(bufs, name):
    n = len(bufs)

    def body(*refs):
        buf_refs, rest = refs[:n], refs[n:]
        sems, token = rest[:2 * n], rest[-1]
        me = _linear(_mesh_position())
        for a in range(n):
            for k, peer in enumerate(_same_core_peers()):
                pltpu.make_async_remote_copy(
                    src_ref=buf_refs[a].at[me], dst_ref=buf_refs[a].at[me],
                    send_sem=sems[2 * a].at[k], recv_sem=sems[2 * a + 1].at[k],
                    device_id=peer, device_id_type=MESH_ID).start()
        token[...] = jnp.zeros_like(token)

    outs = pl.pallas_call(
        body, name=name,
        out_shape=tuple(pltpu.SemaphoreType.DMA((4,)) for _ in range(2 * n))
        + tuple(pltpu.HBM(b.shape, b.dtype) for b in bufs) + (jax.ShapeDtypeStruct((8, 128), F32),),
        in_specs=(HBM_SPEC,) * n,
        out_specs=(SEM_SPEC,) * (2 * n) + (HBM_SPEC,) * n + (pl.BlockSpec(memory_space=pltpu.VMEM),),
        input_output_aliases={a: 2 * n + a for a in range(n)},
        compiler_params=pltpu.CompilerParams(has_side_effects=DATAFLOW),
    )(*[pltpu.with_memory_space_constraint(b, pltpu.HBM) for b in bufs])
    return outs[:2 * n], outs[2 * n:3 * n], outs[3 * n]


def _gather_wait(sems, bufs, after, name):
    n = len(bufs)

    def body(*refs):
        buf_refs, sem_refs = refs[:n], refs[n:3 * n]
        x, y, c = _mesh_position()
        me = _linear((x, y, c))
        for a in range(n):
            for k, peer in enumerate(_same_core_peers()):
                cp = pltpu.make_async_remote_copy(
                    src_ref=buf_refs[a].at[me], dst_ref=buf_refs[a].at[_linear(peer)],
                    send_sem=sem_refs[2 * a].at[k], recv_sem=sem_refs[2 * a + 1].at[k],
                    device_id=peer, device_id_type=MESH_ID)
                cp.wait_send()
                cp.wait_recv()

    return list(pl.pallas_call(
        body, name=name,
        out_shape=tuple(pltpu.HBM(b.shape, b.dtype) for b in bufs),
        in_specs=(HBM_SPEC,) * n + (SEM_SPEC,) * (2 * n) + (pl.BlockSpec(memory_space=pl.ANY),) * len(after),
        out_specs=(HBM_SPEC,) * n, input_output_aliases={a: a for a in range(n)},
        compiler_params=pltpu.CompilerParams(has_side_effects=DATAFLOW),
    )(*bufs, *sems, *after))


def _gather_pass_on(bufs, name):
    n = len(bufs)

    def body(*refs):
        out_refs = refs[n:2 * n]
        send_sems, recv_sems = refs[2 * n:]
        x, y, c = _mesh_position()
        sibling = (x, y, 1 - c)
        chips = [(1 - x, y), (x, 1 - y), (1 - x, 1 - y)]
        copies = []
        for a in range(n):
            for j, chip in enumerate(chips):
                block = out_refs[a].at[_linear((*chip, c))]
                copies.append(pltpu.make_async_remote_copy(
                    src_ref=block, dst_ref=block, send_sem=send_sems.at[3 * a + j], recv_sem=recv_sems.at[3 * a + j],
                    device_id=sibling, device_id_type=MESH_ID))
                copies[-1].start()
        for a in range(n):
            for j, chip in enumerate(chips):
                copies[3 * a + j].wait_send()
                theirs = out_refs[a].at[_linear((*chip, 1 - c))]
                pltpu.make_async_remote_copy(
                    src_ref=theirs, dst_ref=theirs, send_sem=send_sems.at[3 * a + j], recv_sem=recv_sems.at[3 * a + j],
                    device_id=sibling, device_id_type=MESH_ID).wait_recv()

    hbm = pl.BlockSpec(memory_space=pl.ANY)
    return list(pl.pallas_call(
        body, name=name,
        out_shape=[jax.ShapeDtypeStruct(b.shape, b.dtype) for b in bufs],
        in_specs=[hbm] * n, out_specs=[hbm] * n, input_output_aliases={a: a for a in range(n)},
        scratch_shapes=[pltpu.SemaphoreType.DMA((3 * n,)), pltpu.SemaphoreType.DMA((3 * n,))],
    )(*bufs))


def _silu_rows(c):
    def body(c_ref, o_ref):
        v = c_ref[...]
        o_ref[...] = v * _sigmoid(v)

    return pl.pallas_call(body, name="cond_silu", out_shape=jax.ShapeDtypeStruct(c.shape, F32))(c)


def _mod_columns(cond_all, w_ada, b_cols):
    def body(c_ref, w_ref, b_ref, o_ref):
        o_ref[...] = _dot(c_ref[...], w_ref[...]) + b_ref[...]

    return pl.pallas_call(body, name="mod_columns",
                          out_shape=jax.ShapeDtypeStruct((N_DEV, w_ada.shape[1]), F32))(cond_all, w_ada, b_cols)


def _in_proj(x, mod, g_norm1, w_in, tm):
    s = x.shape[0]

    def body(x_ref, mod_ref, g_ref, w_ref, h_ref, q_ref, kv_ref, gb_ref, gc_ref, xc_ref):
        xf = x_ref[...]
        n = xf * _rsqrt_mean_sq(xf) * g_ref[...]
        h = (n * (1.0 + mod_ref[SC1:SC1 + 1, :]) + mod_ref[SH1:SH1 + 1, :]).astype(BF16)
        h_ref[...] = h
        p = _dot_nt(h, w_ref[...])
        q_ref[...] = p[:, 0:512].astype(BF16)
        kv_ref[...] = p[:, 512:768].astype(BF16)
        gb_ref[...] = p[:, 768:1280].astype(BF16)
        gc_ref[...] = p[:, 1280:1792].astype(BF16)
        xc_ref[...] = p[:, 1792:2304].astype(BF16)

    return pl.pallas_call(
        body, name="in_proj", grid=(s // tm,),
        in_specs=[_rows(tm, D_MODEL), _full((8, D_MODEL)), _full((1, D_MODEL)), _full((IN_PROJ_WIDTH, D_MODEL))],
        out_specs=[_rows(tm, D_MODEL), _rows(tm, 512), _rows(tm, 256), _rows(tm, 512), _rows(tm, 512), _rows(tm, 512)],
        out_shape=[jax.ShapeDtypeStruct((s, D_MODEL), BF16), jax.ShapeDtypeStruct((s, 512), BF16),
                   jax.ShapeDtypeStruct((s, 256), BF16), jax.ShapeDtypeStruct((s, 512), BF16),
                   jax.ShapeDtypeStruct((s, 512), BF16), jax.ShapeDtypeStruct((s, 512), BF16)],
        compiler_params=_params(("arbitrary",), VMEM_LIMIT_LARGE),
    )(x, mod, g_norm1, w_in)


def _t5_bucket(dist):
    max_exact = N_BUCKETS // 2
    is_small = dist < max_exact
    d = jnp.maximum(dist, 1).astype(F32)
    large = max_exact + (jnp.log(d / max_exact) / math.log(MAX_DISTANCE / max_exact)
                         * (N_BUCKETS - max_exact)).astype(jnp.int32)
    large = jnp.minimum(large, N_BUCKETS - 1)
    return jnp.where(is_small, dist, large)


def _bucket_table():
    qi = jnp.arange(BLOCK, dtype=jnp.int32)[:, None]
    sj = jnp.arange(2 * BLOCK, dtype=jnp.int32)[None, :]
    return _t5_bucket(jnp.maximum(qi + BLOCK - sj, 0))


def _window_mask():
    qi = lax.broadcasted_iota(jnp.int32, (BLOCK, 2 * BLOCK), 0)
    sj = lax.broadcasted_iota(jnp.int32, (BLOCK, 2 * BLOCK), 1)
    dist = qi + BLOCK - sj
    return (dist >= 0) & (dist < BLOCK)


def _bias_table(rel_bias, bucket):
    def body(rb_ref, bk_ref, o_ref):
        bk = bk_ref[...]
        inside = _window_mask()
        for h in range(N_Q_HEADS):
            acc = jnp.zeros((BLOCK, 2 * BLOCK), F32)
            for b in range(N_BUCKETS):
                acc = jnp.where(bk == b, rb_ref[b, h], acc)
            o_ref[h] = jnp.where(inside, acc, NEG_INF)

    return pl.pallas_call(
        body, name="bias_table",
        in_specs=[pl.BlockSpec(memory_space=pltpu.SMEM), pl.BlockSpec(memory_space=pltpu.VMEM)],
        out_shape=jax.ShapeDtypeStruct((N_Q_HEADS, BLOCK, 2 * BLOCK), F32),
    )(rel_bias, bucket)


def _load_kv_window(kv_ref, n):
    prev = jnp.maximum(n - 1, 0)
    kvw = jnp.concatenate([kv_ref[pl.ds(pl.multiple_of(prev * BLOCK, BLOCK), BLOCK), :],
                           kv_ref[pl.ds(pl.multiple_of(n * BLOCK, BLOCK), BLOCK), :]], axis=0)
    k, v = kvw[:, 0:128], kvw[:, 128:256]
    k_sw = pltpu.roll(k.astype(F32), 64, 1).astype(BF16)
    v_sw = pltpu.roll(v.astype(F32), 64, 1).astype(BF16)
    return (k, k_sw), (v, v_sw)


def _conv_taps(gc, xc, gc_prev, xc_prev, n):
    u = gc * xc
    before = jnp.where(n > 0, gc_prev.astype(F32) * xc_prev.astype(F32), 0.0)
    last = before.shape[0] - 1
    row = lax.broadcasted_iota(jnp.int32, u.shape, 0)
    u1 = jnp.where(row == 0, before[last:last + 1, :], pltpu.roll(u, 1, 0))
    u2 = jnp.where(row == 0, before[last - 1:last, :],
                   jnp.where(row == 1, before[last:last + 1, :], pltpu.roll(u, 2, 0)))
    return u, u1, u2


def _mixer_fwd(q, kv, gb, gc, xc, bias, sinks, conv_w, g_attn, g_conv):
    s = q.shape[0]
    nb = s // BLOCK

    per_step = min(MIXER_BLOCKS, nb)
    tile = per_step * BLOCK

    def one_block(n, rows, before, sink_ref, q_ref, kv_ref, gb_ref, gc_ref, xc_ref, bias_ref, cw_ref, ga_ref,
                  gcv_ref, attn_ref, merged_ref, lse_ref):
        ks, vs = _load_kv_window(kv_ref, n)
        lane = lax.broadcasted_iota(jnp.int32, (BLOCK, BLOCK), 1)
        low = lane < HEAD_DIM
        col = lax.broadcasted_iota(jnp.int32, (BLOCK, 2 * BLOCK), 1)
        no_prev = (col < BLOCK) & (n == 0)
        lse_all = jnp.zeros((BLOCK, BLOCK), F32)
        pairs = []
        for p in range(4):
            qp = q_ref[rows, 128 * p:128 * (p + 1)].astype(F32)
            kvh = p // 2
            res = []
            for e in range(2):
                h = 2 * p + e
                qm = jnp.where(low if e == 0 else ~low, qp, 0.0).astype(BF16)
                sw = 0 if kvh == e else 1
                sc = _dot_nt(qm, ks[sw]) * SCALE + bias_ref[h]
                sc = jnp.where(no_prev, NEG_INF, sc)
                sink = sink_ref[h]
                m = jnp.maximum(jnp.max(sc, axis=-1, keepdims=True), sink)
                pe = jnp.exp(sc - m)
                den = jnp.sum(pe, axis=-1, keepdims=True) + jnp.exp(sink - m)
                res.append(_dot(pe.astype(BF16), vs[sw]) / den)
                lse_all = lse_all + jnp.where(lane == h, m + jnp.log(den), 0.0)
            pairs.append(jnp.where(low, res[0], res[1]))
        attn = jnp.concatenate(pairs, axis=1)
        attn_ref[rows, :] = attn
        lse_ref[rows, :] = lse_all
        u, u1, u2 = _conv_taps(gc_ref[rows, :].astype(F32), xc_ref[rows, :].astype(F32), before[0], before[1], n)
        cw = cw_ref[...]
        cv = gb_ref[rows, :].astype(F32) * (cw[0:1, :] * u2 + cw[1:2, :] * u1 + cw[2:3, :] * u)
        an = attn * _rsqrt_mean_sq(attn) * ga_ref[...]
        cn = cv * _rsqrt_mean_sq(cv) * gcv_ref[...]
        merged_ref[rows, :] = jnp.concatenate([an, cn], axis=1).astype(BF16)

    def body(sink_ref, q_ref, kv_ref, gb_ref, gc_ref, xc_ref, gcp_ref, xcp_ref, *rest):
        step = pl.program_id(0)
        for sub in range(per_step):
            rows = slice(sub * BLOCK, (sub + 1) * BLOCK)
            ahead = slice(sub * BLOCK - PREV_ROWS, sub * BLOCK)
            before = (gcp_ref[...], xcp_ref[...]) if sub == 0 else (gc_ref[ahead, :], xc_ref[ahead, :])
            one_block(step * per_step + sub, rows, before, sink_ref, q_ref, kv_ref, gb_ref, gc_ref, xc_ref, *rest)

    blk = lambda w: pl.BlockSpec((tile, w), lambda n: (n, 0))
    prev8 = pl.BlockSpec((PREV_ROWS, 512), lambda n: (jnp.maximum(n * (tile // PREV_ROWS) - 1, 0), 0))
    return pl.pallas_call(
        body, name="mixer_fwd", grid=(nb // per_step,),
        in_specs=[pl.BlockSpec(memory_space=pltpu.SMEM), blk(512), _full((s, 256)), blk(512), blk(512), blk(512),
                  prev8, prev8, _full((N_Q_HEADS, BLOCK, 2 * BLOCK)), _full((3, 512)), _full((1, 512)),
                  _full((1, 512))],
        out_specs=[blk(512), blk(1024), blk(128)],
        out_shape=[jax.ShapeDtypeStruct((s, 512), F32), jax.ShapeDtypeStruct((s, 1024), BF16),
                   jax.ShapeDtypeStruct((s, 128), F32)],
        compiler_params=_params(("arbitrary",)),
    )(sinks, q, kv, gb, gc, xc, gc, xc, bias, conv_w, g_attn, g_conv)


def _out_proj(merged, x, mod, w_out, tm):
    s = x.shape[0]

    def body(m_ref, x_ref, mod_ref, w_ref, o_ref, x1_ref):
        o = _dot(m_ref[...], w_ref[...])
        o_ref[...] = o.astype(BF16)
        x1_ref[...] = x_ref[...] + mod_ref[G1:G1 + 1, :] * o

    return pl.pallas_call(
        body, name="out_proj", grid=(s // tm,),
        in_specs=[_rows(tm, D_MODEL), _rows(tm, D_MODEL), _full((8, D_MODEL)), _full((D_MODEL, D_MODEL))],
        out_specs=[_rows(tm, D_MODEL), _rows(tm, D_MODEL)],
        out_shape=[jax.ShapeDtypeStruct((s, D_MODEL), BF16), jax.ShapeDtypeStruct((s, D_MODEL), F32)],
        compiler_params=_params(("arbitrary",)),
    )(merged, x, mod, w_out)


def _resident(shape):
    nd = len(shape)
    return pl.BlockSpec(shape, lambda *_: (0,) * nd, pipeline_mode=pl.Buffered(1))


def _ffn_fwd(x1, mod, g_norm2, w_gu, w_down, g_final, target, tm):
    s = x1.shape[0]
    chunk = D_FF // FFN_CHUNKS

    def body(x_ref, mod_ref, g_ref, wgu_ref, wd_ref, gf_ref, t_ref,
             h_ref, gate_ref, up_ref, act_ref, o_ref, dx2_ref, small_ref):
        @pl.when(pl.program_id(0) == 0)
        def _():
            small_ref[...] = jnp.zeros_like(small_ref)

        xf = x_ref[...]
        n = xf * _rsqrt_mean_sq(xf) * g_ref[...]
        h = (n * (1.0 + mod_ref[SC2:SC2 + 1, :]) + mod_ref[SH2:SH2 + 1, :]).astype(BF16)
        h_ref[...] = h
        o = None
        for j in range(FFN_CHUNKS):
            lo = j * chunk
            gate = _dot_nt(h, wgu_ref[lo:lo + chunk, :])
            up = _dot_nt(h, wgu_ref[D_FF + lo:D_FF + lo + chunk, :])
            gate_ref[:, lo:lo + chunk] = gate.astype(BF16)
            up_ref[:, lo:lo + chunk] = up.astype(BF16)
            act = (gate * _sigmoid(gate) * up).astype(BF16)
            act_ref[:, lo:lo + chunk] = act
            part = _dot(act, wd_ref[lo:lo + chunk, :])
            o = part if o is None else o + part
        o_ref[...] = o.astype(BF16)
        x2 = xf + mod_ref[G2:G2 + 1, :] * o
        r = _rsqrt_mean_sq(x2)
        xn = x2 * r
        gf = gf_ref[...]
        err = xn * gf - t_ref[...]
        dy = err * (1.0 / D_MODEL)
        dxn = dy * gf
        dx2_ref[...] = (r * (dxn - xn * jnp.mean(dxn * xn, axis=-1, keepdims=True))).astype(BF16)
        small_ref[0:1, :] += _colsum(dy * xn)
        small_ref[1:2, :] += _colsum(err * err)

        @pl.when(pl.program_id(0) == pl.num_programs(0) - 1)
        def _():
            total = jnp.sum(small_ref[1:2, :], axis=-1, keepdims=True) * (0.5 / D_MODEL)
            small_ref[2:3, :] = jnp.broadcast_to(total, (1, D_MODEL))

    wide = jax.ShapeDtypeStruct((s, D_FF), BF16)
    return pl.pallas_call(
        body, name="ffn_fwd", grid=(s // tm,),
        in_specs=[_rows(tm, D_MODEL), _full((8, D_MODEL)), _full((1, D_MODEL)), _resident((2 * D_FF, D_MODEL)),
                  _resident((D_FF, D_MODEL)), _full((1, D_MODEL)), _rows(tm, D_MODEL)],
        out_specs=[_rows(tm, D_MODEL), _rows(tm, D_FF), _rows(tm, D_FF), _rows(tm, D_FF), _rows(tm, D_MODEL),
                   _rows(tm, D_MODEL), _full((8, D_MODEL))],
        out_shape=[jax.ShapeDtypeStruct((s, D_MODEL), BF16), wide, wide, wide,
                   jax.ShapeDtypeStruct((s, D_MODEL), BF16), jax.ShapeDtypeStruct((s, D_MODEL), BF16),
                   jax.ShapeDtypeStruct((8, D_MODEL), F32)],
        compiler_params=_params(("arbitrary",), VMEM_LIMIT_LARGE),
    )(x1, mod, g_norm2, w_gu, w_down, g_final, target)


def _ffn_bwd(dx2, o2, gate, up, x1, mod, g_norm2, w_down, w_gu, tm):
    s = x1.shape[0]
    chunk = D_FF // FFN_CHUNKS

    def body(dx_ref, o_ref, gate_ref, up_ref, x_ref, mod_ref, g_ref, wd_ref, wgu_ref,
             do_ref, dgu_ref, dx1_ref, small_ref):
        @pl.when(pl.program_id(0) == 0)
        def _():
            small_ref[...] = jnp.zeros_like(small_ref)

        dx = dx_ref[...].astype(F32)
        small_ref[3:4, :] += _colsum(dx * o_ref[...].astype(F32))
        do = (dx * mod_ref[G2:G2 + 1, :]).astype(BF16)
        do_ref[...] = do
        dh = None
        for j in range(FFN_CHUNKS):
            lo = j * chunk
            dact = _dot_nt(do, wd_ref[lo:lo + chunk, :])
            gate = gate_ref[:, lo:lo + chunk].astype(F32)
            sg = _sigmoid(gate)
            dgate = (dact * up_ref[:, lo:lo + chunk].astype(F32) * (sg * (1.0 + gate * (1.0 - sg)))).astype(BF16)
            dup = (dact * (gate * sg)).astype(BF16)
            dgu_ref[:, lo:lo + chunk] = dgate
            dgu_ref[:, D_FF + lo:D_FF + lo + chunk] = dup
            part = _dot(dgate, wgu_ref[lo:lo + chunk, :]) + _dot(dup, wgu_ref[D_FF + lo:D_FF + lo + chunk, :])
            dh = part if dh is None else dh + part
        dx1 = dx + _norm_mod_bwd(dh, x_ref[...], g_ref[...], mod_ref[SC2:SC2 + 1, :], small_ref)
        dx1_ref[...] = dx1.astype(BF16)

    return pl.pallas_call(
        body, name="ffn_bwd", grid=(s // tm,),
        in_specs=[_rows(tm, D_MODEL), _rows(tm, D_MODEL), _rows(tm, D_FF), _rows(tm, D_FF), _rows(tm, D_MODEL),
                  _full((8, D_MODEL)), _full((1, D_MODEL)), _resident((D_FF, D_MODEL)),
                  _resident((2 * D_FF, D_MODEL))],
        out_specs=[_rows(tm, D_MODEL), _rows(tm, 2 * D_FF), _rows(tm, D_MODEL), _full((8, D_MODEL))],
        out_shape=[jax.ShapeDtypeStruct((s, D_MODEL), BF16), jax.ShapeDtypeStruct((s, 2 * D_FF), BF16),
                   jax.ShapeDtypeStruct((s, D_MODEL), BF16), jax.ShapeDtypeStruct((8, D_MODEL), F32)],
        compiler_params=_params(("arbitrary",), VMEM_LIMIT_LARGE),
    )(dx2, o2, gate, up, x1, mod, g_norm2, w_down, w_gu)


def _norm_mod_bwd(dh, xf, g, scale_row, small_ref):
    r = _rsqrt_mean_sq(xf)
    xn = xf * r
    small_ref[0:1, :] += _colsum(dh)
    small_ref[1:2, :] += _colsum(dh * (xn * g))
    dn = dh * (1.0 + scale_row)
    small_ref[2:3, :] += _colsum(dn * xn)
    dxn = dn * g
    return r * (dxn - xn * jnp.mean(dxn * xn, axis=-1, keepdims=True))


def _out_proj_bwd(dx1, o1, mod, w_out, tm):
    s = dx1.shape[0]

    def body(dx_ref, o_ref, mod_ref, w_ref, do_ref, dm_ref, small_ref):
        @pl.when(pl.program_id(0) == 0)
        def _():
            small_ref[...] = jnp.zeros_like(small_ref)

        dx = dx_ref[...].astype(F32)
        small_ref[0:1, :] += _colsum(dx * o_ref[...].astype(F32))
        do = (dx * mod_ref[G1:G1 + 1, :]).astype(BF16)
        do_ref[...] = do
        dm_ref[...] = _dot_nt(do, w_ref[...]).astype(BF16)

    return pl.pallas_call(
        body, name="out_proj_bwd", grid=(s // tm,),
        in_specs=[_rows(tm, D_MODEL), _rows(tm, D_MODEL), _full((8, D_MODEL)), _full((D_MODEL, D_MODEL))],
        out_specs=[_rows(tm, D_MODEL), _rows(tm, D_MODEL), _full((8, D_MODEL))],
        out_shape=[jax.ShapeDtypeStruct((s, D_MODEL), BF16), jax.ShapeDtypeStruct((s, D_MODEL), BF16),
                   jax.ShapeDtypeStruct((8, D_MODEL), F32)],
        compiler_params=_params(("arbitrary",)),
    )(dx1, o1, mod, w_out)


def _group_norm_bwd(dm, a, g):
    r = _rsqrt_mean_sq(a)
    an = a * r
    dan = dm * g
    return r * (dan - an * jnp.mean(dan * an, axis=-1, keepdims=True)), _colsum(dm * an)


def _mixer_bwd(q, kv, gb, gc, xc, bias, sinks, conv_w, g_attn, g_conv, attn, lse, dmerged):
    s = q.shape[0]
    nb = s // BLOCK

    per_step = min(MIXER_BLOCKS, nb)
    tile = per_step * BLOCK
    steps = nb // per_step

    def one_block(n, rows, before, nxt, sink_ref, q_ref, kv_ref, gb_ref, gc_ref, xc_ref, bias_ref, cw_ref, ga_ref,
                  gcv_ref, attn_ref, lse_ref, dm_ref, dproj_ref, dbias_ref, dsink_ref, small_ref):
        next_dy, next_dkv = nxt
        dm = dm_ref[rows, :].astype(F32)
        gbv, gcv_, xcv = gb_ref[rows, :].astype(F32), gc_ref[rows, :].astype(F32), xc_ref[rows, :].astype(F32)
        u, u1, u2 = _conv_taps(gcv_, xcv, before[0], before[1], n)
        cw = cw_ref[...]
        yv = cw[0:1, :] * u2 + cw[1:2, :] * u1 + cw[2:3, :] * u
        dcv, dg_conv = _group_norm_bwd(dm[:, 512:1024], gbv * yv, gcv_ref[...])
        small_ref[1:2, :] += dg_conv
        dproj_ref[rows, 768:1280] = (dcv * yv).astype(BF16)
        dy = dcv * gbv
        row = lax.broadcasted_iota(jnp.int32, dy.shape, 0)
        d1 = jnp.where(row == BLOCK - 1, next_dy[0:1, :], pltpu.roll(dy, BLOCK - 1, 0))
        d2 = jnp.where(row == BLOCK - 2, next_dy[0:1, :],
                       jnp.where(row == BLOCK - 1, next_dy[1:2, :], pltpu.roll(dy, BLOCK - 2, 0)))
        du = cw[2:3, :] * dy + cw[1:2, :] * d1 + cw[0:1, :] * d2
        dproj_ref[rows, 1280:1792] = (du * xcv).astype(BF16)
        dproj_ref[rows, 1792:2304] = (du * gcv_).astype(BF16)
        small_ref[2:3, :] += _colsum(dy * u2)
        small_ref[3:4, :] += _colsum(dy * u1)
        small_ref[4:5, :] += _colsum(dy * u)

        attn_v = attn_ref[rows, :]
        dout, dg_attn = _group_norm_bwd(dm[:, 0:512], attn_v, ga_ref[...])
        small_ref[0:1, :] += dg_attn
        ks, vs = _load_kv_window(kv_ref, n)
        lane = lax.broadcasted_iota(jnp.int32, (BLOCK, BLOCK), 1)
        low = lane < HEAD_DIM
        col = lax.broadcasted_iota(jnp.int32, (BLOCK, 2 * BLOCK), 1)
        no_prev = (col < BLOCK) & (n == 0)
        lse_all = lse_ref[rows, :]
        dsink = jnp.zeros((BLOCK, BLOCK), F32)
        dq_pairs = []
        dk_groups, dv_groups = [], []
        for kvh in range(2):
            ds_rows, pr_rows, q_rows, do_rows = [], [], [], []
            for p in (2 * kvh, 2 * kvh + 1):
                qp = q_ref[rows, 128 * p:128 * (p + 1)].astype(F32)
                do_p = dout[:, 128 * p:128 * (p + 1)]
                prod = do_p * attn_v[:, 128 * p:128 * (p + 1)]
                res = []
                for e in range(2):
                    h = 2 * p + e
                    half = low if e == 0 else ~low
                    qm = jnp.where(half, qp, 0.0).astype(BF16)
                    dom = jnp.where(half, do_p, 0.0).astype(BF16)
                    delta = jnp.sum(jnp.where(half, prod, 0.0), axis=-1, keepdims=True)
                    lse_h = jnp.sum(jnp.where(lane == h, lse_all, 0.0), axis=-1, keepdims=True)
                    sw = 0 if kvh == e else 1
                    sc = _dot_nt(qm, ks[sw]) * SCALE + bias_ref[h]
                    sc = jnp.where(no_prev, NEG_INF, sc)
                    pr = jnp.exp(sc - lse_h)
                    dp = _dot_nt(dom, vs[sw])
                    ds = pr * (dp - delta)
                    dbias_ref[h] += ds
                    dsink = dsink + jnp.where(lane == h, -jnp.exp(sink_ref[h] - lse_h) * delta, 0.0)
                    dsb = ds.astype(BF16)
                    res.append(_dot(dsb, ks[sw]) * SCALE)
                    ds_rows.append(dsb)
                    pr_rows.append(pr.astype(BF16))
                    q_rows.append(qm)
                    do_rows.append(dom)
                dq_pairs.append(jnp.where(low, res[0], res[1]))
            dk_g = _dot_tn(jnp.concatenate(ds_rows, axis=0), jnp.concatenate(q_rows, axis=0)) * SCALE
            dv_g = _dot_tn(jnp.concatenate(pr_rows, axis=0), jnp.concatenate(do_rows, axis=0))
            dk_groups.append(dk_g + pltpu.roll(dk_g, 64, 1))
            dv_groups.append(dv_g + pltpu.roll(dv_g, 64, 1))
        dproj_ref[rows, 0:512] = jnp.concatenate(dq_pairs, axis=1).astype(BF16)
        dsink_ref[...] += dsink
        low_kv = lax.broadcasted_iota(jnp.int32, (2 * BLOCK, BLOCK), 1) < HEAD_DIM
        dkv_win = jnp.concatenate([jnp.where(low_kv, dk_groups[0], dk_groups[1]),
                                   jnp.where(low_kv, dv_groups[0], dv_groups[1])], axis=1)
        dproj_ref[rows, 512:768] = (dkv_win[BLOCK:2 * BLOCK, :] + next_dkv).astype(BF16)
        return dy[0:8, :], dkv_win[0:BLOCK, :]

    def body(sink_ref, q_ref, kv_ref, gb_ref, gc_ref, xc_ref, gcp_ref, xcp_ref, *rest):
        refs, dy_ref, dkv_ref = rest[:-2], rest[-2], rest[-1]
        dbias_ref, dsink_ref, small_ref = refs[8], refs[9], refs[10]
        step = pl.program_id(0)

        @pl.when(step == 0)
        def _():
            dbias_ref[...] = jnp.zeros_like(dbias_ref)
            dsink_ref[...] = jnp.zeros_like(dsink_ref)
            small_ref[...] = jnp.zeros_like(small_ref)
            dy_ref[...] = jnp.zeros_like(dy_ref)
            dkv_ref[...] = jnp.zeros_like(dkv_ref)

        nxt = (dy_ref[...], dkv_ref[...])
        for sub in reversed(range(per_step)):
            rows = slice(sub * BLOCK, (sub + 1) * BLOCK)
            ahead = slice(sub * BLOCK - PREV_ROWS, sub * BLOCK)
            before = (gcp_ref[...], xcp_ref[...]) if sub == 0 else (gc_ref[ahead, :], xc_ref[ahead, :])
            nxt = one_block((steps - 1 - step) * per_step + sub, rows, before, nxt,
                            sink_ref, q_ref, kv_ref, gb_ref, gc_ref, xc_ref, *refs)
        dy_ref[...], dkv_ref[...] = nxt

        @pl.when(step == steps - 1)
        def _():
            small_ref[5:6, :] = jnp.concatenate([_colsum(dsink_ref[...]), jnp.zeros((1, 512 - BLOCK), F32)], axis=1)

    blk = lambda w: pl.BlockSpec((tile, w), lambda t: (steps - 1 - t, 0))
    prev8 = pl.BlockSpec((PREV_ROWS, 512),
                         lambda t: (jnp.maximum((steps - 1 - t) * (tile // PREV_ROWS) - 1, 0), 0))
    bf = lambda w: jax.ShapeDtypeStruct((s, w), BF16)
    return pl.pallas_call(
        body, name="mixer_bwd", grid=(steps,),
        in_specs=[pl.BlockSpec(memory_space=pltpu.SMEM), blk(512), _full((s, 256)), blk(512), blk(512), blk(512),
                  prev8, prev8, _full((N_Q_HEADS, BLOCK, 2 * BLOCK)), _full((3, 512)), _full((1, 512)),
                  _full((1, 512)), blk(512), blk(128), blk(1024)],
        out_specs=[blk(IN_PROJ_WIDTH), _full((N_Q_HEADS, BLOCK, 2 * BLOCK)), _full((BLOCK, BLOCK)), _full((8, 512))],
        out_shape=[bf(IN_PROJ_WIDTH), jax.ShapeDtypeStruct((N_Q_HEADS, BLOCK, 2 * BLOCK), F32),
                   jax.ShapeDtypeStruct((BLOCK, BLOCK), F32), jax.ShapeDtypeStruct((8, 512), F32)],
        scratch_shapes=[pltpu.VMEM((8, 512), F32), pltpu.VMEM((BLOCK, 2 * KV_WIDTH), F32)],
        compiler_params=_params(("arbitrary",), VMEM_LIMIT_LARGE),
    )(sinks, q, kv, gb, gc, xc, gc, xc, bias, conv_w, g_attn, g_conv, attn, lse, dmerged)


def _in_proj_bwd(dproj, x, dx1, mod, g_norm1, w_in, tm):
    s = x.shape[0]

    def body(dproj_ref, x_ref, dx1_ref, mod_ref, g_ref, w_ref, dx_ref, small_ref):
        @pl.when(pl.program_id(0) == 0)
        def _():
            small_ref[...] = jnp.zeros_like(small_ref)

        dh = _dot(dproj_ref[...], w_ref[...])
        dx_ref[...] = dx1_ref[...].astype(F32) + _norm_mod_bwd(dh, x_ref[...], g_ref[...], mod_ref[SC1:SC1 + 1, :],
                                                               small_ref)

    return pl.pallas_call(
        body, name="in_proj_bwd", grid=(s // tm,),
        in_specs=[_rows(tm, IN_PROJ_WIDTH), _rows(tm, D_MODEL), _rows(tm, D_MODEL), _full((8, D_MODEL)),
                  _full((1, D_MODEL)), _full((IN_PROJ_WIDTH, D_MODEL))],
        out_specs=[_rows(tm, D_MODEL), _full((8, D_MODEL))],
        out_shape=[jax.ShapeDtypeStruct((s, D_MODEL), F32), jax.ShapeDtypeStruct((8, D_MODEL), F32)],
        compiler_params=_params(("arbitrary",), VMEM_LIMIT_LARGE),
    )(dproj, x, dx1, mod, g_norm1, w_in)


def _weight_grad(a, b, tk, ts, name, after=None):
    s, k = a.shape
    n = b.shape[1]
    nt = s // ts
    extra = [] if after is None else [after]

    def body(a_ref, b_ref, *rest):
        o_ref, acc_ref = rest[-2:]
        t = pl.program_id(1)
        part = _dot_tn(a_ref[...], b_ref[...])

        @pl.when(t == 0)
        def _():
            acc_ref[...] = part

        @pl.when(t > 0)
        def _():
            acc_ref[...] += part

        @pl.when(t == nt - 1)
        def _():
            o_ref[...] = acc_ref[...].astype(BF16)

    return pl.pallas_call(
        body, name=name, grid=(k // tk, nt),
        in_specs=[pl.BlockSpec((ts, tk), lambda i, t: (t, i)), pl.BlockSpec((ts, n), lambda i, t: (t, 0))]
        + [pl.BlockSpec(memory_space=pl.ANY)] * len(extra),
        out_specs=pl.BlockSpec((tk, n), lambda i, t: (i, 0)),
        out_shape=jax.ShapeDtypeStruct((k, n), BF16),
        scratch_shapes=[pltpu.VMEM((tk, n), F32)],
        compiler_params=_params(("arbitrary", "arbitrary"), VMEM_LIMIT_LARGE),
    )(a, b, *extra)


def _rel_bias_grad(dbias, bucket):
    def body(db_ref, bk_ref, o_ref, rows_ref):
        bk = bk_ref[...]
        for b in range(N_BUCKETS):
            sel = (bk == b).astype(F32)
            for h in range(N_Q_HEADS):
                rows_ref[N_BUCKETS * h + b:N_BUCKETS * h + b + 1, :] = _colsum(db_ref[h] * sel)
        head = lax.broadcasted_iota(jnp.int32, (N_BUCKETS, N_Q_HEADS), 1)
        out = jnp.zeros((N_BUCKETS, N_Q_HEADS), F32)
        for h in range(N_Q_HEADS):
            per_bucket = jnp.sum(rows_ref[N_BUCKETS * h:N_BUCKETS * (h + 1), :], axis=-1, keepdims=True)
            out = out + jnp.where(head == h, per_bucket, 0.0)
        o_ref[...] = out

    return pl.pallas_call(
        body, name="rel_bias_grad",
        out_shape=jax.ShapeDtypeStruct((N_BUCKETS, N_Q_HEADS), F32),
        scratch_shapes=[pltpu.VMEM((N_BUCKETS * N_Q_HEADS, 2 * BLOCK), F32)],
    )(dbias, bucket)


def _lanes_from(x, start, width):
    n = x.shape[1]
    return pltpu.roll(x, (n - start) % n, 1)[:, 0:width]


def _w_ada_grad(me, cond_all, packed_all, cols):
    def body(me_ref, c_ref, p_ref, o_ref):
        dmod = jnp.concatenate([p_ref[k][:, OFF_DMOD:OFF_DMOD + N_MOD * D_MODEL] for k in range(N_DEV)], axis=0)
        mine = _lanes_from(dmod, me_ref[0] * cols, cols)
        pad = lambda a: jnp.concatenate([a, jnp.zeros((128 - N_DEV, a.shape[1]), F32)], axis=0)
        o_ref[...] = _dot_tn(pad(c_ref[...]), pad(mine))

    vmem = pl.BlockSpec(memory_space=pltpu.VMEM)
    return pl.pallas_call(body, name="w_ada_grad",
                          in_specs=[pl.BlockSpec(memory_space=pltpu.SMEM), vmem, vmem],
                          out_shape=jax.ShapeDtypeStruct((cond_all.shape[1], cols), F32))(me, cond_all, packed_all)


SMALL_PARAMS = (("rel_bias", None), ("b_ada", (OFF_DMOD, N_MOD * D_MODEL)), ("g_norm1", (OFF_GN1, D_MODEL)),
                ("sinks", (OFF_SINK, N_Q_HEADS)), ("conv_w", None), ("g_attn_out", (OFF_GATT, ATTN_WIDTH)),
                ("g_conv_out", (OFF_GCV, CONV_WIDTH)), ("g_norm2", (OFF_GN2, D_MODEL)),
                ("g_final", (OFF_GFIN, D_MODEL)))


def _small_update(me, packed_all, rel_all, state, after):
    n_p = len(SMALL_PARAMS)
    flat = [a for triple in state for a in triple]
    conv_cols = state[4][0].shape[1]

    def body(me_ref, p_ref, r_ref, *refs):
        ins = refs[:3 * n_p]
        loss_ref, outs = refs[3 * n_p + len(after)], refs[3 * n_p + len(after) + 1:]
        small, rel = p_ref[0], r_ref[0]
        for k in range(1, N_DEV):
            small = small + p_ref[k]
            rel = rel + r_ref[k]
        loss_ref[...] = small[:, OFF_LOSS:OFF_LOSS + 128]
        taps = jnp.concatenate([small[:, OFF_CONVW + CONV_WIDTH * j:OFF_CONVW + CONV_WIDTH * (j + 1)]
                                for j in range(3)] + [jnp.zeros((5, CONV_WIDTH), F32)], axis=0)
        conv_g = _lanes_from(taps, me_ref[0] * conv_cols, conv_cols)[0:3, :]
        for i, (name, lanes) in enumerate(SMALL_PARAMS):
            g = rel if name == "rel_bias" else conv_g if name == "conv_w" else small[:, lanes[0]:lanes[0] + lanes[1]]
            w_ref, m_ref, v_ref = ins[3 * i:3 * i + 3]
            outs[4 * i][...] = g
            outs[4 * i + 1][...], outs[4 * i + 2][...], outs[4 * i + 3][...] = _adam_math(
                w_ref[...], g, m_ref[...], v_ref[...])

    vmem = pl.BlockSpec(memory_space=pltpu.VMEM)
    out_shape = [jax.ShapeDtypeStruct((1, 128), F32)]
    for w, _, _ in state:
        out_shape += [jax.ShapeDtypeStruct(w.shape, F32)] * 4
    outs = pl.pallas_call(
        body, name="small_update",
        in_specs=[pl.BlockSpec(memory_space=pltpu.SMEM), vmem, vmem] + [vmem] * len(flat)
        + [pl.BlockSpec(memory_space=pl.ANY)] * len(after),
        out_shape=out_shape,
    )(me, packed_all, rel_all, *flat, *after)
    return outs[0], [tuple(outs[1 + 4 * i:5 + 4 * i]) for i in range(n_p)]


def _adam_math(w, g, m, v):
    m = ADAM_B1 * m + (1.0 - ADAM_B1) * g
    v = ADAM_B2 * v + (1.0 - ADAM_B2) * (g * g)
    m_hat = m / (1.0 - ADAM_B1 ** ADAM_STEP)
    v_hat = v / (1.0 - ADAM_B2 ** ADAM_STEP)
    delta = -ADAM_LR * (m_hat / (jnp.sqrt(v_hat) + ADAM_EPS) + ADAM_WD * w)
    return delta, m, v


def _adamw_parts(w, m, v, local, land, me, tr, name):
    r, c = w.shape

    def body(me_ref, w_ref, m_ref, v_ref, own_ref, land_ref, g_ref, d_ref, mo_ref, vo_ref):
        g = own_ref[0].astype(F32)
        for k in range(N_DEV - 1):
            g = g + land_ref[k].astype(F32)
        g_ref[...] = g
        d_ref[...], mo_ref[...], vo_ref[...] = _adam_math(w_ref[...], g, m_ref[...], v_ref[...])

    tile = pl.BlockSpec((tr, c), lambda i, me_ref: (i, 0))
    return pl.pallas_call(
        body, name=name,
        grid_spec=pltpu.PrefetchScalarGridSpec(
            num_scalar_prefetch=1, grid=(r // tr,),
            in_specs=[tile, tile, tile, pl.BlockSpec((1, tr, c), lambda i, me_ref: (me_ref[0], i, 0)),
                      pl.BlockSpec((N_DEV - 1, tr, c), lambda i, me_ref: (0, i, 0))],
            out_specs=[tile] * 4),
        out_shape=[jax.ShapeDtypeStruct((r, c), F32)] * 4,
        compiler_params=_params(("arbitrary",)),
    )(me, w, m, v, local, land)


def _adamw(w, m, v, g, tr, name):
    r, c = w.shape

    def body(w_ref, m_ref, v_ref, g_ref, d_ref, mo_ref, vo_ref):
        d_ref[...], mo_ref[...], vo_ref[...] = _adam_math(w_ref[...], g_ref[...], m_ref[...], v_ref[...])

    tile = pl.BlockSpec((tr, c), lambda i: (i, 0))
    return pl.pallas_call(
        body, name=name, grid=(r // tr,),
        in_specs=[tile] * 4, out_specs=[tile] * 3,
        out_shape=[jax.ShapeDtypeStruct((r, c), F32)] * 3,
        compiler_params=_params(("arbitrary",)),
    )(w, m, v, g)


def _behind(a, token):
    return a + token[0:a.shape[0], 0:1]


def _local_step(x, target, mod, w_in_t, weights_out_gu, weights_down, rel_bias, g_norm1, sinks, conv_w, g_attn,
                g_conv, g_norm2, g_final, exchange):
    s = x.shape[0]
    tm = min(512, s)
    tm_small = min(256, s)
    bucket = _bucket_table()
    bias = _bias_table(rel_bias, bucket)

    h, q, kv, gb, gc, xc = _in_proj(x, mod, g_norm1, w_in_t, tm)
    attn, merged, lse = _mixer_fwd(q, kv, gb, gc, xc, bias, sinks, conv_w, g_attn, g_conv)
    w_out, w_gu_t = weights_out_gu(merged)
    o1, x1 = _out_proj(merged, x, mod, w_out, tm)
    w_down = weights_down(x1)
    h2, gate, up, act, o2, dx2, fin = _ffn_fwd(x1, mod, g_norm2, w_gu_t, w_down, g_final, target, tm)

    do2, dgu, dx1, sm_2 = _ffn_bwd(dx2, o2, gate, up, x1, mod, g_norm2, w_down, w_gu_t, tm)
    ts = min(WEIGHT_GRAD_ROWS, s)
    tok_down = exchange("w_down", _weight_grad(act, do2, D_FF // 2, ts, "w_down_grad"))
    mod = _behind(mod, exchange("w_gu", _weight_grad(dgu, h2, D_FF // 2, ts, "w_gu_grad", after=tok_down)))
    do1, dmerged, sm_g1 = _out_proj_bwd(dx1, o1, mod, w_out, tm)
    g_attn_b = _behind(g_attn, exchange("w_out", _weight_grad(merged, do1, D_MODEL, ts, "w_out_grad")))
    dproj, dbias, dsink, sm_mix = _mixer_bwd(
        q, kv, gb, gc, xc, bias, sinks, conv_w, g_attn_b, g_conv, attn, lse, dmerged)
    mod = _behind(mod, exchange("w_in", _weight_grad(dproj, h, IN_PROJ_WIDTH // 2, ts, "w_in_grad")))
    dx, sm_1 = _in_proj_bwd(dproj, x, dx1, mod, g_norm1, w_in_t, tm)
    d_rel = _rel_bias_grad(dbias, bucket)

    packed = jnp.concatenate([
        sm_1[0], sm_1[1], sm_g1[0], sm_2[0], sm_2[1], sm_2[3],
        sm_1[2],
        sm_mix[5, 0:128],
        sm_mix[0], sm_mix[1],
        sm_2[2],
        fin[0],
        sm_mix[2], sm_mix[3], sm_mix[4],
        fin[2, 0:128],
    ])[None, :]
    return dx, packed, d_rel


def kernel(x, c, rel_bias, w_ada, b_ada, g_norm1, w_in, sinks, conv_w, g_attn_out, g_conv_out, w_out, g_norm2, w_gu, w_down, g_final, loss_target, m_rel_bias, m_w_ada, m_b_ada, m_g_norm1, m_w_in, m_sinks, m_conv_w, m_g_attn_out, m_g_conv_out, m_w_out, m_g_norm2, m_w_gu, m_w_down, m_g_final, v_rel_bias, v_w_ada, v_b_ada, v_g_norm1, v_w_in, v_sinks, v_conv_w, v_g_attn_out, v_g_conv_out, v_w_out, v_g_norm2, v_w_gu, v_w_down, v_g_final):
    me = _linear(_mesh_position())
    me_arr = jnp.reshape(me, (1,)).astype(jnp.int32)
    ada_cols = w_ada.shape[2]
    tm = min(512, x.shape[1])

    cond = _silu_rows(c)
    cond_all, conv_w_all = _all_gather_small([cond, conv_w[0]], "gather_cond")
    cond_all = cond_all[:, 0, :]
    conv_cols = conv_w.shape[2]
    conv_w_full = conv_w_all.transpose(1, 0, 2).reshape(3, CONV_WIDTH)
    b_cols = lax.dynamic_slice_in_dim(b_ada, me * ada_cols, ada_cols, axis=1)
    mod_cols = _mod_columns(cond_all, w_ada[0], b_cols)
    mod_all = _all_gather_small([mod_cols], "gather_mod")[0]
    mod = lax.dynamic_index_in_dim(mod_all, me, axis=1, keepdims=False).reshape(N_MOD, D_MODEL)
    mod = jnp.concatenate([mod, jnp.zeros((2, D_MODEL), F32)], axis=0)

    w_in_t = _all_gather([w_in[0].T], "gather_w_in", to_bf16=True, big=True)[0].reshape(IN_PROJ_WIDTH, D_MODEL)
    gather_sems, staged, gather_token = _gather_start(
        _stage_blocks([w_out[0], w_gu[0].T, w_down[0]], w_in_t, "stage_weights"), "gather_start_weights")
    mod = _behind(mod, gather_token)

    def weights_out_gu(after):
        got = _gather_pass_on(_gather_wait(gather_sems[0:4], staged[0:2], [after], "gather_wait_out_gu"),
                              "gather_pass_on_out_gu")
        return got[0].reshape(D_MODEL, D_MODEL), got[1].reshape(2 * D_FF, D_MODEL)

    def weights_down(after):
        got = _gather_pass_on(_gather_wait(gather_sems[4:6], staged[2:3], [after], "gather_wait_down"),
                              "gather_pass_on_down")
        return got[0].reshape(D_FF, D_MODEL)

    started = {}

    def exchange(name, dw):
        st = _exchange_start(dw.reshape(N_DEV, dw.shape[0] // N_DEV, dw.shape[1]), "exchange_start_" + name)
        started[name] = st
        return st[4]

    dx, packed, d_rel = _local_step(
        x[0], loss_target[0], mod, w_in_t, weights_out_gu, weights_down, rel_bias, g_norm1, sinks[0], conv_w_full,
        g_attn_out, g_conv_out, g_norm2, g_final[None, :], exchange)

    def zone(a):
        return lax.dynamic_update_slice(jnp.zeros((N_DEV,) + a.shape, F32), a[None], (me,) + (0,) * a.ndim)

    shared = _share_start([packed, d_rel], [zone(packed), zone(d_rel)], "share_small_start")

    def finish(name, after, w, m, v, tr):
        src, land = _exchange_wait(started[name], after, "exchange_wait_" + name)
        return _adamw_parts(w, m, v, src, land, me_arr, tr, "adamw_" + name)

    g_down, d_down, nm_down, nv_down = finish("w_down", [shared[2][0]], w_down[0], m_w_down[0], v_w_down[0], 176)
    g_gu, d_gu, nm_gu, nv_gu = finish("w_gu", [nv_down], w_gu[0].T, m_w_gu[0].T, v_w_gu[0].T, 352)
    g_out, d_out, nm_out, nv_out = finish("w_out", [nv_gu], w_out[0], m_w_out[0], v_w_out[0], 128)

    packed_all, rel_all = _share_wait(shared, [nv_out], "share_small_wait")
    g_ada = _w_ada_grad(me_arr, cond_all, packed_all, ada_cols)
    d_ada, nm_ada, nv_ada = _adamw(w_ada[0], m_w_ada[0], v_w_ada[0], g_ada, 256, "adamw_w_ada")
    as_rows = {"conv_w": lambda a: a[0], "g_final": lambda a: a[None, :]}
    small_state = {
        "rel_bias": (rel_bias, m_rel_bias, v_rel_bias), "b_ada": (b_ada, m_b_ada, v_b_ada),
        "g_norm1": (g_norm1, m_g_norm1, v_g_norm1), "sinks": (sinks, m_sinks, v_sinks),
        "conv_w": (conv_w, m_conv_w, v_conv_w), "g_attn_out": (g_attn_out, m_g_attn_out, v_g_attn_out),
        "g_conv_out": (g_conv_out, m_g_conv_out, v_g_conv_out), "g_norm2": (g_norm2, m_g_norm2, v_g_norm2),
        "g_final": (g_final, m_g_final, v_g_final),
    }
    state = [tuple(as_rows.get(name, lambda a: a)(a) for a in small_state[name]) for name, _ in SMALL_PARAMS]
    loss_row, small_out = _small_update(me_arr, packed_all, rel_all, state, [])
    loss = loss_row[0, 0]
    small_res = {name: tuple(a.reshape(small_state[name][0].shape) for a in res)
                 for (name, _), res in zip(SMALL_PARAMS, small_out)}

    g_in, d_in, nm_in, nv_in = finish("w_in", [loss_row, nv_ada], w_in[0].T, m_w_in[0].T, v_w_in[0].T, 144)

    big = {
        "w_ada": (g_ada[None], d_ada[None], nm_ada[None], nv_ada[None]),
        "w_in": (g_in.T[None], d_in.T[None], nm_in.T[None], nv_in.T[None]),
        "w_out": (g_out[None], d_out[None], nm_out[None], nv_out[None]),
        "w_gu": (g_gu.T[None], d_gu.T[None], nm_gu.T[None], nv_gu.T[None]),
        "w_down": (g_down[None], d_down[None], nm_down[None], nv_down[None]),
    }
    order = ["rel_bias", "w_ada", "b_ada", "g_norm1", "w_in", "sinks", "conv_w", "g_attn_out", "g_conv_out", "w_out",
             "g_norm2", "w_gu", "w_down", "g_final"]
    results = [big[k] if k in big else small_res[k] for k in order]
    return (loss, dx[None], *[r[0] for r in results], *[r[1] for r in results], *[r[2] for r in results],
            *[r[3] for r in results])
```

```python
import functools
import math

import jax
import jax.numpy as jnp
from jax import lax
from jax.experimental import pallas as pl
from jax.experimental.pallas import tpu as pltpu

F32 = jnp.float32
BF16 = jnp.bfloat16

D_MODEL = 1024
HEAD_DIM = 64
N_Q_HEADS = 8
ATTN_WIDTH = 512
KV_WIDTH = 128
CONV_WIDTH = 512
IN_PROJ_WIDTH = 2304
D_FF = 2816
N_MOD = 6
N_BUCKETS = 32
MAX_DISTANCE = 128
BLOCK = 128
EPS = 1e-6
NEG_INF = -1e30
SCALE = HEAD_DIM ** -0.5
N_DEV = 8

ADAM_LR = 0.001
ADAM_B1 = 0.9
ADAM_B2 = 0.999
ADAM_EPS = 1e-08
ADAM_WD = 0.01
ADAM_STEP = 10

SH1, SC1, G1, SH2, SC2, G2 = range(6)

VMEM_LIMIT_LARGE = 60 * 1024 * 1024
WEIGHT_GRAD_ROWS = 2048
FFN_CHUNKS = 2
PREV_ROWS = 16
MIXER_BLOCKS = 4
MESH_ID = pl.DeviceIdType.MESH

OFF_DMOD = 0
OFF_GN1 = OFF_DMOD + N_MOD * D_MODEL
OFF_SINK = OFF_GN1 + D_MODEL
OFF_GATT = OFF_SINK + 128
OFF_GCV = OFF_GATT + ATTN_WIDTH
OFF_GN2 = OFF_GCV + CONV_WIDTH
OFF_GFIN = OFF_GN2 + D_MODEL
OFF_CONVW = OFF_GFIN + D_MODEL
OFF_LOSS = OFF_CONVW + 3 * CONV_WIDTH
PACKED = OFF_LOSS + 128


def _params(sem=None, vmem=None):
    return pltpu.CompilerParams(dimension_semantics=sem, vmem_limit_bytes=vmem)


def _coming_behind(body):
    def skipping(after_ref, *refs):
        body(*refs)

    return skipping


ANY_SPEC = pl.BlockSpec(memory_space=pl.ANY)


def _full(shape):
    nd = len(shape)
    return pl.BlockSpec(shape, lambda *_: (0,) * nd)


def _rows(tm, width):
    return pl.BlockSpec((tm, width), lambda i, *_: (i, 0))


def _sigmoid(x):
    return 1.0 / (1.0 + jnp.exp(-x))


def _rsqrt_mean_sq(x):
    return lax.rsqrt(jnp.mean(x * x, axis=-1, keepdims=True) + EPS)


def _colsum(x):
    return jnp.sum(x, axis=0, keepdims=True)


def _dot(a, b):
    return jnp.dot(a, b, preferred_element_type=F32)


def _dot_nt(a, b):
    return lax.dot_general(a, b, (((1,), (1,)), ((), ())), preferred_element_type=F32)


def _dot_tn(a, b):
    return lax.dot_general(a, b, (((0,), (0,)), ((), ())), preferred_element_type=F32)


def _mesh_position():
    return lax.axis_index("x"), lax.axis_index("y"), lax.axis_index("c")


def _linear(p):
    return 4 * p[0] + 2 * p[1] + p[2]


def _all_gather(arrs, name, to_bf16, big):
    n = len(arrs)
    out_dtype = BF16 if to_bf16 else F32

    def body(*refs):
        in_refs, out_refs = refs[:n], refs[n:2 * n]
        rest = refs[2 * n:]
        if to_bf16:
            stage, rest = rest[:n], rest[n:]
            for a in range(n):
                stage[a][...] = in_refs[a][...].astype(BF16)
            srcs = stage
        else:
            srcs = in_refs
        send_sems, recv_sems, local_sems = rest
        x, y, c = _mesh_position()
        me, sibling = (x, y, c), (x, y, 1 - c)
        chips = [(1 - x, y), (x, 1 - y), (1 - x, 1 - y)]

        def slot(a, p):
            return out_refs[a].at[_linear(p)]

        def copy(k, a, block, to, src=None):
            return pltpu.make_async_remote_copy(
                src_ref=slot(a, block) if src is None else src,
                dst_ref=slot(a, block),
                send_sem=send_sems.at[k * n + a],
                recv_sem=recv_sems.at[k * n + a],
                device_id=to,
                device_id_type=MESH_ID,
            )

        mine = [pltpu.make_async_copy(srcs[a], slot(a, me), local_sems.at[a]) for a in range(n)]
        for cp in mine:
            cp.start()
        first = [copy(0, a, me, sibling, src=srcs[a]) for a in range(n)]
        for j, chip in enumerate(chips):
            first += [copy(1 + j, a, me, (*chip, c), src=srcs[a]) for a in range(n)]
        for cp in first:
            cp.start()
        passed = []
        for j, chip in enumerate(chips):
            for a in range(n):
                copy(1 + j, a, (*chip, c), me).wait_recv()
                fwd = copy(4 + j, a, (*chip, c), sibling)
                fwd.start()
                passed.append(fwd)
        for a in range(n):
            copy(0, a, sibling, me).wait_recv()
        for j, chip in enumerate(chips):
            for a in range(n):
                copy(4 + j, a, (*chip, 1 - c), me).wait_recv()
        for cp in first + passed:
            cp.wait_send()
        for cp in mine:
            cp.wait()

    vmem = pl.BlockSpec(memory_space=pltpu.VMEM)
    out_space = pl.BlockSpec(memory_space=pl.ANY) if big else vmem
    scratch = [pltpu.VMEM(a.shape, BF16) for a in arrs] if to_bf16 else []
    scratch += [pltpu.SemaphoreType.DMA((7 * n,)), pltpu.SemaphoreType.DMA((7 * n,)),
                pltpu.SemaphoreType.DMA((n,))]
    outs = pl.pallas_call(
        body, name=name,
        out_shape=[jax.ShapeDtypeStruct((N_DEV,) + a.shape, out_dtype) for a in arrs],
        in_specs=[vmem] * n, out_specs=[out_space] * n,
        scratch_shapes=scratch,
        compiler_params=_params(vmem=VMEM_LIMIT_LARGE if big else None),
    )(*arrs)
    return list(outs)


def _peer(k):
    x, y, c = _mesh_position()
    return (1 - x if k & 4 else x, 1 - y if k & 2 else y, 1 - c if k & 1 else c)


def _all_gather_small(arrs, name):
    n = len(arrs)

    def body(*refs):
        in_refs, out_refs = refs[:n], refs[n:2 * n]
        send_sems, recv_sems, local_sems = refs[2 * n:]
        me = _linear(_mesh_position())
        mine = [pltpu.make_async_copy(in_refs[a], out_refs[a].at[me], local_sems.at[a]) for a in range(n)]
        for cp in mine:
            cp.start()
        sends = []
        for k in range(1, N_DEV):
            for a in range(n):
                sends.append(pltpu.make_async_remote_copy(
                    src_ref=in_refs[a], dst_ref=out_refs[a].at[me],
                    send_sem=send_sems.at[(k - 1) * n + a], recv_sem=recv_sems.at[(k - 1) * n + a],
                    device_id=_peer(k), device_id_type=MESH_ID))
                sends[-1].start()
        for k in range(1, N_DEV):
            for a in range(n):
                pltpu.make_async_remote_copy(
                    src_ref=in_refs[a], dst_ref=out_refs[a].at[_linear(_peer(k))],
                    send_sem=send_sems.at[(k - 1) * n + a], recv_sem=recv_sems.at[(k - 1) * n + a],
                    device_id=_peer(k), device_id_type=MESH_ID).wait_recv()
        for cp in sends:
            cp.wait_send()
        for cp in mine:
            cp.wait()

    vmem = pl.BlockSpec(memory_space=pltpu.VMEM)
    return list(pl.pallas_call(
        body, name=name,
        out_shape=[jax.ShapeDtypeStruct((N_DEV,) + a.shape, F32) for a in arrs],
        in_specs=[vmem] * n, out_specs=[vmem] * n,
        scratch_shapes=[pltpu.SemaphoreType.DMA((7 * n,)), pltpu.SemaphoreType.DMA((7 * n,)),
                        pltpu.SemaphoreType.DMA((n,))],
    )(*arrs))


HBM_SPEC = pl.BlockSpec(memory_space=pltpu.HBM)
SEM_SPEC = pl.BlockSpec(memory_space=pltpu.SEMAPHORE)
DATAFLOW = pltpu.SideEffectType.DATAFLOW_SIDE_EFFECTING


def _exchange_start(src, name):
    r, c = src.shape[1:]

    def body(src_ref, land_ref, send_sems, recv_sems, src_thru, land_thru, token):
        for k in range(1, N_DEV):
            peer = _peer(k)
            pltpu.make_async_remote_copy(
                src_ref=src_ref.at[_linear(peer)], dst_ref=land_ref.at[k - 1],
                send_sem=send_sems.at[k - 1], recv_sem=recv_sems.at[k - 1],
                device_id=peer, device_id_type=MESH_ID).start()
        token[...] = jnp.zeros_like(token)

    land = lax.empty((N_DEV - 1, r, c), src.dtype)
    return pl.pallas_call(
        body, name=name,
        out_shape=(pltpu.SemaphoreType.DMA((N_DEV - 1,)), pltpu.SemaphoreType.DMA((N_DEV - 1,)),
                   pltpu.HBM(src.shape, src.dtype), pltpu.HBM(land.shape, land.dtype),
                   jax.ShapeDtypeStruct((8, 128), F32)),
        in_specs=(HBM_SPEC, HBM_SPEC),
        out_specs=(SEM_SPEC, SEM_SPEC, HBM_SPEC, HBM_SPEC, pl.BlockSpec(memory_space=pltpu.VMEM)),
        input_output_aliases={0: 2, 1: 3},
        compiler_params=pltpu.CompilerParams(has_side_effects=DATAFLOW),
    )(pltpu.with_memory_space_constraint(src, pltpu.HBM), pltpu.with_memory_space_constraint(land, pltpu.HBM))


def _exchange_wait(started, after, name):
    send_sems, recv_sems, src_thru, land_thru, _ = started

    def body(src_ref, land_ref, send_sems, recv_sems, *rest):
        for k in range(1, N_DEV):
            cp = pltpu.make_async_remote_copy(
                src_ref=src_ref.at[0], dst_ref=land_ref.at[k - 1],
                send_sem=send_sems.at[k - 1], recv_sem=recv_sems.at[k - 1],
                device_id=_peer(k), device_id_type=MESH_ID)
            cp.wait_send()
            cp.wait_recv()

    return pl.pallas_call(
        body, name=name,
        out_shape=(pltpu.HBM(src_thru.shape, src_thru.dtype), pltpu.HBM(land_thru.shape, land_thru.dtype)),
        in_specs=(HBM_SPEC, HBM_SPEC, SEM_SPEC, SEM_SPEC) + (pl.BlockSpec(memory_space=pl.ANY),) * len(after),
        out_specs=(HBM_SPEC, HBM_SPEC), input_output_aliases={0: 0, 1: 1},
        compiler_params=pltpu.CompilerParams(has_side_effects=DATAFLOW),
    )(src_thru, land_thru, send_sems, recv_sems, *after)


def _share_start(arrs, zones, name):
    n = len(arrs)

    def body(*refs):
        src_refs, zone_refs, sems = refs[:n], refs[n:2 * n], refs[2 * n:4 * n]
        me = _linear(_mesh_position())
        for a in range(n):
            for k in range(1, N_DEV):
                pltpu.make_async_remote_copy(
                    src_ref=src_refs[a], dst_ref=zone_refs[a].at[me],
                    send_sem=sems[2 * a].at[k - 1], recv_sem=sems[2 * a + 1].at[k - 1],
                    device_id=_peer(k), device_id_type=MESH_ID).start()

    outs = pl.pallas_call(
        body, name=name,
        out_shape=tuple(pltpu.SemaphoreType.DMA((N_DEV - 1,)) for _ in range(2 * n))
        + tuple(pltpu.HBM(a.shape, a.dtype) for a in arrs) + tuple(pltpu.HBM(z.shape, z.dtype) for z in zones),
        in_specs=(HBM_SPEC,) * (2 * n),
        out_specs=(SEM_SPEC,) * (2 * n) + (HBM_SPEC,) * (2 * n),
        input_output_aliases={i: 2 * n + i for i in range(2 * n)},
        compiler_params=pltpu.CompilerParams(has_side_effects=DATAFLOW),
    )(*[pltpu.with_memory_space_constraint(a, pltpu.HBM) for a in list(arrs) + list(zones)])
    return outs[:2 * n], outs[2 * n:3 * n], outs[3 * n:]


def _share_wait(started, after, name):
    sems, arrs, zones = started
    n = len(arrs)

    def body(*refs):
        src_refs, zone_refs, sem_refs = refs[:n], refs[n:2 * n], refs[2 * n:4 * n]
        for a in range(n):
            for k in range(1, N_DEV):
                cp = pltpu.make_async_remote_copy(
                    src_ref=src_refs[a], dst_ref=zone_refs[a].at[_linear(_peer(k))],
                    send_sem=sem_refs[2 * a].at[k - 1], recv_sem=sem_refs[2 * a + 1].at[k - 1],
                    device_id=_peer(k), device_id_type=MESH_ID)
                cp.wait_send()
                cp.wait_recv()

    outs = pl.pallas_call(
        body, name=name,
        out_shape=tuple(pltpu.HBM(a.shape, a.dtype) for a in arrs) + tuple(pltpu.HBM(z.shape, z.dtype) for z in zones),
        in_specs=(HBM_SPEC,) * (2 * n) + (SEM_SPEC,) * (2 * n) + (pl.BlockSpec(memory_space=pl.ANY),) * len(after),
        out_specs=(HBM_SPEC,) * (2 * n), input_output_aliases={i: i for i in range(2 * n)},
        compiler_params=pltpu.CompilerParams(has_side_effects=DATAFLOW),
    )(*arrs, *zones, *sems, *after)
    return list(outs[n:])


def _stage_blocks(arrs, after, name):
    n = len(arrs)

    def body(*refs):
        in_refs, out_refs, stage, sems = refs[:n], refs[n + 1:2 * n + 1], refs[2 * n + 1:3 * n + 1], refs[3 * n + 1]
        me = _linear(_mesh_position())
        copies = []
        for a in range(n):
            stage[a][...] = in_refs[a][...].astype(BF16)
            copies.append(pltpu.make_async_copy(stage[a], out_refs[a].at[me], sems.at[a]))
            copies[-1].start()
        for cp in copies:
            cp.wait()

    return list(pl.pallas_call(
        body, name=name,
        out_shape=[jax.ShapeDtypeStruct((N_DEV,) + a.shape, BF16) for a in arrs],
        in_specs=[pl.BlockSpec(memory_space=pltpu.VMEM)] * n + [pl.BlockSpec(memory_space=pl.ANY)],
        out_specs=[pl.BlockSpec(memory_space=pl.ANY)] * n,
        scratch_shapes=[pltpu.VMEM(a.shape, BF16) for a in arrs] + [pltpu.SemaphoreType.DMA((n,))],
        compiler_params=_params(vmem=VMEM_LIMIT_LARGE),
    )(*arrs, after))


def _same_core_peers():
    x, y, c = _mesh_position()
    return [(x, y, 1 - c), (1 - x, y, c), (x, 1 - y, c), (1 - x, 1 - y, c)]


def _gather_start(bufs, name):
    n = len(bufs)

    def body(*refs):
        buf_refs, rest = refs[:n], refs[n:]
        sems, token = rest[:2 * n], rest[-1]
        me = _linear(_mesh_position())
        for a in range(n):
            for k, peer in enumerate(_same_core_peers()):
                pltpu.make_async_remote_copy(
                    src_ref=buf_refs[a].at[me], dst_ref=buf_refs[a].at[me],
                    send_sem=sems[2 * a].at[k], recv_sem=sems[2 * a + 1].at[k],
                    device_id=peer, device_id_type=MESH_ID).start()
        token[...] = jnp.zeros_like(token)

    outs = pl.pallas_call(
        body, name=name,
        out_shape=tuple(pltpu.SemaphoreType.DMA((4,)) for _ in range(2 * n))
        + tuple(pltpu.HBM(b.shape, b.dtype) for b in bufs) + (jax.ShapeDtypeStruct((8, 128), F32),),
        in_specs=(HBM_SPEC,) * n,
        out_specs=(SEM_SPEC,) * (2 * n) + (HBM_SPEC,) * n + (pl.BlockSpec(memory_space=pltpu.VMEM),),
        input_output_aliases={a: 2 * n + a for a in range(n)},
        compiler_params=pltpu.CompilerParams(has_side_effects=DATAFLOW),
    )(*[pltpu.with_memory_space_constraint(b, pltpu.HBM) for b in bufs])
    return outs[:2 * n], outs[2 * n:3 * n], outs[3 * n]


def _gather_wait(sems, bufs, after, name):
    n = len(bufs)

    def body(*refs):
        buf_refs, sem_refs = refs[:n], refs[n:3 * n]
        x, y, c = _mesh_position()
        me = _linear((x, y, c))
        for a in range(n):
            for k, peer in enumerate(_same_core_peers()):
                cp = pltpu.make_async_remote_copy(
                    src_ref=buf_refs[a].at[me], dst_ref=buf_refs[a].at[_linear(peer)],
                    send_sem=sem_refs[2 * a].at[k], recv_sem=sem_refs[2 * a + 1].at[k],
                    device_id=peer, device_id_type=MESH_ID)
                cp.wait_send()
                cp.wait_recv()

    return list(pl.pallas_call(
        body, name=name,
        out_shape=tuple(pltpu.HBM(b.shape, b.dtype) for b in bufs),
        in_specs=(HBM_SPEC,) * n + (SEM_SPEC,) * (2 * n) + (pl.BlockSpec(memory_space=pl.ANY),) * len(after),
        out_specs=(HBM_SPEC,) * n, input_output_aliases={a: a for a in range(n)},
        compiler_params=pltpu.CompilerParams(has_side_effects=DATAFLOW),
    )(*bufs, *sems, *after))


def _gather_pass_on(bufs, name):
    n = len(bufs)

    def body(*refs):
        out_refs = refs[n:2 * n]
        send_sems, recv_sems = refs[2 * n:]
        x, y, c = _mesh_position()
        sibling = (x, y, 1 - c)
        chips = [(1 - x, y), (x, 1 - y), (1 - x, 1 - y)]
        copies = []
        for a in range(n):
            for j, chip in enumerate(chips):
                block = out_refs[a].at[_linear((*chip, c))]
                copies.append(pltpu.make_async_remote_copy(
                    src_ref=block, dst_ref=block, send_sem=send_sems.at[3 * a + j], recv_sem=recv_sems.at[3 * a + j],
                    device_id=sibling, device_id_type=MESH_ID))
                copies[-1].start()
        for a in range(n):
            for j, chip in enumerate(chips):
                copies[3 * a + j].wait_send()
                theirs = out_refs[a].at[_linear((*chip, 1 - c))]
                pltpu.make_async_remote_copy(
                    src_ref=theirs, dst_ref=theirs, send_sem=send_sems.at[3 * a + j], recv_sem=recv_sems.at[3 * a + j],
                    device_id=sibling, device_id_type=MESH_ID).wait_recv()

    hbm = pl.BlockSpec(memory_space=pl.ANY)
    return list(pl.pallas_call(
        body, name=name,
        out_shape=[jax.ShapeDtypeStruct(b.shape, b.dtype) for b in bufs],
        in_specs=[hbm] * n, out_specs=[hbm] * n, input_output_aliases={a: a for a in range(n)},
        scratch_shapes=[pltpu.SemaphoreType.DMA((3 * n,)), pltpu.SemaphoreType.DMA((3 * n,))],
    )(*bufs))


def _silu_rows(c):
    def body(c_ref, o_ref):
        v = c_ref[...]
        o_ref[...] = v * _sigmoid(v)

    return pl.pallas_call(body, name="cond_silu", out_shape=jax.ShapeDtypeStruct(c.shape, F32))(c)


def _mod_columns(cond_all, w_ada, b_cols):
    def body(c_ref, w_ref, b_ref, o_ref):
        o_ref[...] = _dot(c_ref[...], w_ref[...]) + b_ref[...]

    return pl.pallas_call(body, name="mod_columns",
                          out_shape=jax.ShapeDtypeStruct((N_DEV, w_ada.shape[1]), F32))(cond_all, w_ada, b_cols)


def _in_proj(x, mod, g_norm1, w_in, tm):
    s = x.shape[0]

    def body(x_ref, mod_ref, g_ref, w_ref, h_ref, q_ref, kv_ref, gb_ref, gc_ref, xc_ref):
        xf = x_ref[...]
        n = xf * _rsqrt_mean_sq(xf) * g_ref[...]
        h = (n * (1.0 + mod_ref[SC1:SC1 + 1, :]) + mod_ref[SH1:SH1 + 1, :]).astype(BF16)
        h_ref[...] = h
        p = _dot_nt(h, w_ref[...])
        q_ref[...] = p[:, 0:512].astype(BF16)
        kv_ref[...] = p[:, 512:768].astype(BF16)
        gb_ref[...] = p[:, 768:1280].astype(BF16)
        gc_ref[...] = p[:, 1280:1792].astype(BF16)
        xc_ref[...] = p[:, 1792:2304].astype(BF16)

    return pl.pallas_call(
        body, name="in_proj", grid=(s // tm,),
        in_specs=[_rows(tm, D_MODEL), _full((8, D_MODEL)), _full((1, D_MODEL)), _full((IN_PROJ_WIDTH, D_MODEL))],
        out_specs=[_rows(tm, D_MODEL), _rows(tm, 512), _rows(tm, 256), _rows(tm, 512), _rows(tm, 512), _rows(tm, 512)],
        out_shape=[jax.ShapeDtypeStruct((s, D_MODEL), BF16), jax.ShapeDtypeStruct((s, 512), BF16),
                   jax.ShapeDtypeStruct((s, 256), BF16), jax.ShapeDtypeStruct((s, 512), BF16),
                   jax.ShapeDtypeStruct((s, 512), BF16), jax.ShapeDtypeStruct((s, 512), BF16)],
        compiler_params=_params(("arbitrary",), VMEM_LIMIT_LARGE),
    )(x, mod, g_norm1, w_in)


def _t5_bucket(dist):
    max_exact = N_BUCKETS // 2
    is_small = dist < max_exact
    d = jnp.maximum(dist, 1).astype(F32)
    large = max_exact + (jnp.log(d / max_exact) / math.log(MAX_DISTANCE / max_exact)
                         * (N_BUCKETS - max_exact)).astype(jnp.int32)
    large = jnp.minimum(large, N_BUCKETS - 1)
    return jnp.where(is_small, dist, large)


def _bucket_table():
    qi = jnp.arange(BLOCK, dtype=jnp.int32)[:, None]
    sj = jnp.arange(2 * BLOCK, dtype=jnp.int32)[None, :]
    return _t5_bucket(jnp.maximum(qi + BLOCK - sj, 0))


def _window_mask():
    qi = lax.broadcasted_iota(jnp.int32, (BLOCK, 2 * BLOCK), 0)
    sj = lax.broadcasted_iota(jnp.int32, (BLOCK, 2 * BLOCK), 1)
    dist = qi + BLOCK - sj
    return (dist >= 0) & (dist < BLOCK)


def _bias_table(rel_bias, bucket):
    def body(rb_ref, bk_ref, o_ref):
        bk = bk_ref[...]
        inside = _window_mask()
        for h in range(N_Q_HEADS):
            acc = jnp.zeros((BLOCK, 2 * BLOCK), F32)
            for b in range(N_BUCKETS):
                acc = jnp.where(bk == b, rb_ref[b, h], acc)
            o_ref[h] = jnp.where(inside, acc, NEG_INF)

    return pl.pallas_call(
        body, name="bias_table",
        in_specs=[pl.BlockSpec(memory_space=pltpu.SMEM), pl.BlockSpec(memory_space=pltpu.VMEM)],
        out_shape=jax.ShapeDtypeStruct((N_Q_HEADS, BLOCK, 2 * BLOCK), F32),
    )(rel_bias, bucket)


def _load_kv_window(kv_ref, n):
    prev = jnp.maximum(n - 1, 0)
    kvw = jnp.concatenate([kv_ref[pl.ds(pl.multiple_of(prev * BLOCK, BLOCK), BLOCK), :],
                           kv_ref[pl.ds(pl.multiple_of(n * BLOCK, BLOCK), BLOCK), :]], axis=0)
    k, v = kvw[:, 0:128], kvw[:, 128:256]
    k_sw = pltpu.roll(k.astype(F32), 64, 1).astype(BF16)
    v_sw = pltpu.roll(v.astype(F32), 64, 1).astype(BF16)
    return (k, k_sw), (v, v_sw)


def _conv_taps(gc, xc, gc_prev, xc_prev, n):
    u = gc * xc
    before = jnp.where(n > 0, gc_prev.astype(F32) * xc_prev.astype(F32), 0.0)
    last = before.shape[0] - 1
    row = lax.broadcasted_iota(jnp.int32, u.shape, 0)
    u1 = jnp.where(row == 0, before[last:last + 1, :], pltpu.roll(u, 1, 0))
    u2 = jnp.where(row == 0, before[last - 1:last, :],
                   jnp.where(row == 1, before[last:last + 1, :], pltpu.roll(u, 2, 0)))
    return u, u1, u2


def _mixer_fwd(q, kv, gb, gc, xc, bias, sinks, conv_w, g_attn, g_conv):
    s = q.shape[0]
    nb = s // BLOCK

    per_step = min(MIXER_BLOCKS, nb)
    tile = per_step * BLOCK

    def one_block(n, rows, before, sink_ref, q_ref, kv_ref, gb_ref, gc_ref, xc_ref, bias_ref, cw_ref, ga_ref,
                  gcv_ref, attn_ref, merged_ref, lse_ref):
        ks, vs = _load_kv_window(kv_ref, n)
        lane = lax.broadcasted_iota(jnp.int32, (BLOCK, BLOCK), 1)
        low = lane < HEAD_DIM
        col = lax.broadcasted_iota(jnp.int32, (BLOCK, 2 * BLOCK), 1)
        no_prev = (col < BLOCK) & (n == 0)
        lse_all = jnp.zeros((BLOCK, BLOCK), F32)
        pairs = []
        for p in range(4):
            qp = q_ref[rows, 128 * p:128 * (p + 1)].astype(F32)
            kvh = p // 2
            res = []
            for e in range(2):
                h = 2 * p + e
                qm = jnp.where(low if e == 0 else ~low, qp, 0.0).astype(BF16)
                sw = 0 if kvh == e else 1
                sc = _dot_nt(qm, ks[sw]) * SCALE + bias_ref[h]
                sc = jnp.where(no_prev, NEG_INF, sc)
                sink = sink_ref[h]
                m = jnp.maximum(jnp.max(sc, axis=-1, keepdims=True), sink)
                pe = jnp.exp(sc - m)
                den = jnp.sum(pe, axis=-1, keepdims=True) + jnp.exp(sink - m)
                res.append(_dot(pe.astype(BF16), vs[sw]) / den)
                lse_all = lse_all + jnp.where(lane == h, m + jnp.log(den), 0.0)
            pairs.append(jnp.where(low, res[0], res[1]))
        attn = jnp.concatenate(pairs, axis=1)
        attn_ref[rows, :] = attn
        lse_ref[rows, :] = lse_all
        u, u1, u2 = _conv_taps(gc_ref[rows, :].astype(F32), xc_ref[rows, :].astype(F32), before[0], before[1], n)
        cw = cw_ref[...]
        cv = gb_ref[rows, :].astype(F32) * (cw[0:1, :] * u2 + cw[1:2, :] * u1 + cw[2:3, :] * u)
        an = attn * _rsqrt_mean_sq(attn) * ga_ref[...]
        cn = cv * _rsqrt_mean_sq(cv) * gcv_ref[...]
        merged_ref[rows, :] = jnp.concatenate([an, cn], axis=1).astype(BF16)

    def body(sink_ref, q_ref, kv_ref, gb_ref, gc_ref, xc_ref, gcp_ref, xcp_ref, *rest):
        step = pl.program_id(0)
        for sub in range(per_step):
            rows = slice(sub * BLOCK, (sub + 1) * BLOCK)
            ahead = slice(sub * BLOCK - PREV_ROWS, sub * BLOCK)
            before = (gcp_ref[...], xcp_ref[...]) if sub == 0 else (gc_ref[ahead, :], xc_ref[ahead, :])
            one_block(step * per_step + sub, rows, before, sink_ref, q_ref, kv_ref, gb_ref, gc_ref, xc_ref, *rest)

    blk = lambda w: pl.BlockSpec((tile, w), lambda n: (n, 0))
    prev8 = pl.BlockSpec((PREV_ROWS, 512), lambda n: (jnp.maximum(n * (tile // PREV_ROWS) - 1, 0), 0))
    return pl.pallas_call(
        body, name="mixer_fwd", grid=(nb // per_step,),
        in_specs=[pl.BlockSpec(memory_space=pltpu.SMEM), blk(512), _full((s, 256)), blk(512), blk(512), blk(512),
                  prev8, prev8, _full((N_Q_HEADS, BLOCK, 2 * BLOCK)), _full((3, 512)), _full((1, 512)),
                  _full((1, 512))],
        out_specs=[blk(512), blk(1024), blk(128)],
        out_shape=[jax.ShapeDtypeStruct((s, 512), F32), jax.ShapeDtypeStruct((s, 1024), BF16),
                   jax.ShapeDtypeStruct((s, 128), F32)],
        compiler_params=_params(("arbitrary",)),
    )(sinks, q, kv, gb, gc, xc, gc, xc, bias, conv_w, g_attn, g_conv)


def _out_proj(merged, x, mod, w_out, tm):
    s = x.shape[0]

    def body(m_ref, x_ref, mod_ref, w_ref, o_ref, x1_ref):
        o = _dot(m_ref[...], w_ref[...])
        o_ref[...] = o.astype(BF16)
        x1_ref[...] = x_ref[...] + mod_ref[G1:G1 + 1, :] * o

    return pl.pallas_call(
        body, name="out_proj", grid=(s // tm,),
        in_specs=[_rows(tm, D_MODEL), _rows(tm, D_MODEL), _full((8, D_MODEL)), _full((D_MODEL, D_MODEL))],
        out_specs=[_rows(tm, D_MODEL), _rows(tm, D_MODEL)],
        out_shape=[jax.ShapeDtypeStruct((s, D_MODEL), BF16), jax.ShapeDtypeStruct((s, D_MODEL), F32)],
        compiler_params=_params(("arbitrary",)),
    )(merged, x, mod, w_out)


def _resident(shape):
    nd = len(shape)
    return pl.BlockSpec(shape, lambda *_: (0,) * nd, pipeline_mode=pl.Buffered(1))


def _ffn_fwd(x1, mod, g_norm2, w_gu, w_down, g_final, target, tm):
    s = x1.shape[0]
    chunk = D_FF // FFN_CHUNKS

    def body(x_ref, mod_ref, g_ref, wgu_ref, wd_ref, gf_ref, t_ref,
             h_ref, gate_ref, up_ref, act_ref, o_ref, dx2_ref, small_ref):
        @pl.when(pl.program_id(0) == 0)
        def _():
            small_ref[...] = jnp.zeros_like(small_ref)

        xf = x_ref[...]
        n = xf * _rsqrt_mean_sq(xf) * g_ref[...]
        h = (n * (1.0 + mod_ref[SC2:SC2 + 1, :]) + mod_ref[SH2:SH2 + 1, :]).astype(BF16)
        h_ref[...] = h
        o = None
        for j in range(FFN_CHUNKS):
            lo = j * chunk
            gate = _dot_nt(h, wgu_ref[lo:lo + chunk, :])
            up = _dot_nt(h, wgu_ref[D_FF + lo:D_FF + lo + chunk, :])
            gate_ref[:, lo:lo + chunk] = gate.astype(BF16)
            up_ref[:, lo:lo + chunk] = up.astype(BF16)
            act = (gate * _sigmoid(gate) * up).astype(BF16)
            act_ref[:, lo:lo + chunk] = act
            part = _dot(act, wd_ref[lo:lo + chunk, :])
            o = part if o is None else o + part
        o_ref[...] = o.astype(BF16)
        x2 = xf + mod_ref[G2:G2 + 1, :] * o
        r = _rsqrt_mean_sq(x2)
        xn = x2 * r
        gf = gf_ref[...]
        err = xn * gf - t_ref[...]
        dy = err * (1.0 / D_MODEL)
        dxn = dy * gf
        dx2_ref[...] = (r * (dxn - xn * jnp.mean(dxn * xn, axis=-1, keepdims=True))).astype(BF16)
        small_ref[0:1, :] += _colsum(dy * xn)
        small_ref[1:2, :] += _colsum(err * err)

        @pl.when(pl.program_id(0) == pl.num_programs(0) - 1)
        def _():
            total = jnp.sum(small_ref[1:2, :], axis=-1, keepdims=True) * (0.5 / D_MODEL)
            small_ref[2:3, :] = jnp.broadcast_to(total, (1, D_MODEL))

    wide = jax.ShapeDtypeStruct((s, D_FF), BF16)
    return pl.pallas_call(
        body, name="ffn_fwd", grid=(s // tm,),
        in_specs=[_rows(tm, D_MODEL), _full((8, D_MODEL)), _full((1, D_MODEL)), _resident((2 * D_FF, D_MODEL)),
                  _resident((D_FF, D_MODEL)), _full((1, D_MODEL)), _rows(tm, D_MODEL)],
        out_specs=[_rows(tm, D_MODEL), _rows(tm, D_FF), _rows(tm, D_FF), _rows(tm, D_FF), _rows(tm, D_MODEL),
                   _rows(tm, D_MODEL), _full((8, D_MODEL))],
        out_shape=[jax.ShapeDtypeStruct((s, D_MODEL), BF16), wide, wide, wide,
                   jax.ShapeDtypeStruct((s, D_MODEL), BF16), jax.ShapeDtypeStruct((s, D_MODEL), BF16),
                   jax.ShapeDtypeStruct((8, D_MODEL), F32)],
        compiler_params=_params(("arbitrary",), VMEM_LIMIT_LARGE),
    )(x1, mod, g_norm2, w_gu, w_down, g_final, target)


def _ffn_bwd(dx2, o2, gate, up, x1, mod, g_norm2, w_down, w_gu, tm):
    s = x1.shape[0]
    chunk = D_FF // FFN_CHUNKS

    def body(dx_ref, o_ref, gate_ref, up_ref, x_ref, mod_ref, g_ref, wd_ref, wgu_ref,
             do_ref, dgu_ref, dx1_ref, small_ref):
        @pl.when(pl.program_id(0) == 0)
        def _():
            small_ref[...] = jnp.zeros_like(small_ref)

        dx = dx_ref[...].astype(F32)
        small_ref[3:4, :] += _colsum(dx * o_ref[...].astype(F32))
        do = (dx * mod_ref[G2:G2 + 1, :]).astype(BF16)
        do_ref[...] = do
        dh = None
        for j in range(FFN_CHUNKS):
            lo = j * chunk
            dact = _dot_nt(do, wd_ref[lo:lo + chunk, :])
            gate = gate_ref[:, lo:lo + chunk].astype(F32)
            sg = _sigmoid(gate)
            dgate = (dact * up_ref[:, lo:lo + chunk].astype(F32) * (sg * (1.0 + gate * (1.0 - sg)))).astype(BF16)
            dup = (dact * (gate * sg)).astype(BF16)
            dgu_ref[:, lo:lo + chunk] = dgate
            dgu_ref[:, D_FF + lo:D_FF + lo + chunk] = dup
            part = _dot(dgate, wgu_ref[lo:lo + chunk, :]) + _dot(dup, wgu_ref[D_FF + lo:D_FF + lo + chunk, :])
            dh = part if dh is None else dh + part
        dx1 = dx + _norm_mod_bwd(dh, x_ref[...], g_ref[...], mod_ref[SC2:SC2 + 1, :], small_ref)
        dx1_ref[...] = dx1.astype(BF16)

    return pl.pallas_call(
        body, name="ffn_bwd", grid=(s // tm,),
        in_specs=[_rows(tm, D_MODEL), _rows(tm, D_MODEL), _rows(tm, D_FF), _rows(tm, D_FF), _rows(tm, D_MODEL),
                  _full((8, D_MODEL)), _full((1, D_MODEL)), _resident((D_FF, D_MODEL)),
                  _resident((2 * D_FF, D_MODEL))],
        out_specs=[_rows(tm, D_MODEL), _rows(tm, 2 * D_FF), _rows(tm, D_MODEL), _full((8, D_MODEL))],
        out_shape=[jax.ShapeDtypeStruct((s, D_MODEL), BF16), jax.ShapeDtypeStruct((s, 2 * D_FF), BF16),
                   jax.ShapeDtypeStruct((s, D_MODEL), BF16), jax.ShapeDtypeStruct((8, D_MODEL), F32)],
        compiler_params=_params(("arbitrary",), VMEM_LIMIT_LARGE),
    )(dx2, o2, gate, up, x1, mod, g_norm2, w_down, w_gu)


def _norm_mod_bwd(dh, xf, g, scale_row, small_ref):
    r = _rsqrt_mean_sq(xf)
    xn = xf * r
    small_ref[0:1, :] += _colsum(dh)
    small_ref[1:2, :] += _colsum(dh * (xn * g))
    dn = dh * (1.0 + scale_row)
    small_ref[2:3, :] += _colsum(dn * xn)
    dxn = dn * g
    return r * (dxn - xn * jnp.mean(dxn * xn, axis=-1, keepdims=True))


def _out_proj_bwd(after, dx1, o1, mod, w_out, tm):
    s = dx1.shape[0]

    def body(dx_ref, o_ref, mod_ref, w_ref, do_ref, dm_ref, small_ref):
        @pl.when(pl.program_id(0) == 0)
        def _():
            small_ref[...] = jnp.zeros_like(small_ref)

        dx = dx_ref[...].astype(F32)
        small_ref[0:1, :] += _colsum(dx * o_ref[...].astype(F32))
        do = (dx * mod_ref[G1:G1 + 1, :]).astype(BF16)
        do_ref[...] = do
        dm_ref[...] = _dot_nt(do, w_ref[...]).astype(BF16)

    return pl.pallas_call(
        _coming_behind(body), name="out_proj_bwd", grid=(s // tm,),
        in_specs=[ANY_SPEC, _rows(tm, D_MODEL), _rows(tm, D_MODEL), _full((8, D_MODEL)), _full((D_MODEL, D_MODEL))],
        out_specs=[_rows(tm, D_MODEL), _rows(tm, D_MODEL), _full((8, D_MODEL))],
        out_shape=[jax.ShapeDtypeStruct((s, D_MODEL), BF16), jax.ShapeDtypeStruct((s, D_MODEL), BF16),
                   jax.ShapeDtypeStruct((8, D_MODEL), F32)],
        compiler_params=_params(("arbitrary",)),
    )(after, dx1, o1, mod, w_out)


def _group_norm_bwd(dm, a, g):
    r = _rsqrt_mean_sq(a)
    an = a * r
    dan = dm * g
    return r * (dan - an * jnp.mean(dan * an, axis=-1, keepdims=True)), _colsum(dm * an)


def _mixer_bwd(after, q, kv, gb, gc, xc, bias, sinks, conv_w, g_attn, g_conv, attn, lse, dmerged):
    s = q.shape[0]
    nb = s // BLOCK

    per_step = min(MIXER_BLOCKS, nb)
    tile = per_step * BLOCK
    steps = nb // per_step

    def one_block(n, rows, before, nxt, sink_ref, q_ref, kv_ref, gb_ref, gc_ref, xc_ref, bias_ref, cw_ref, ga_ref,
                  gcv_ref, attn_ref, lse_ref, dm_ref, dproj_ref, dbias_ref, dsink_ref, small_ref):
        next_dy, next_dkv = nxt
        dm = dm_ref[rows, :].astype(F32)
        gbv, gcv_, xcv = gb_ref[rows, :].astype(F32), gc_ref[rows, :].astype(F32), xc_ref[rows, :].astype(F32)
        u, u1, u2 = _conv_taps(gcv_, xcv, before[0], before[1], n)
        cw = cw_ref[...]
        yv = cw[0:1, :] * u2 + cw[1:2, :] * u1 + cw[2:3, :] * u
        dcv, dg_conv = _group_norm_bwd(dm[:, 512:1024], gbv * yv, gcv_ref[...])
        small_ref[1:2, :] += dg_conv
        dproj_ref[rows, 768:1280] = (dcv * yv).astype(BF16)
        dy = dcv * gbv
        row = lax.broadcasted_iota(jnp.int32, dy.shape, 0)
        d1 = jnp.where(row == BLOCK - 1, next_dy[0:1, :], pltpu.roll(dy, BLOCK - 1, 0))
        d2 = jnp.where(row == BLOCK - 2, next_dy[0:1, :],
                       jnp.where(row == BLOCK - 1, next_dy[1:2, :], pltpu.roll(dy, BLOCK - 2, 0)))
        du = cw[2:3, :] * dy + cw[1:2, :] * d1 + cw[0:1, :] * d2
        dproj_ref[rows, 1280:1792] = (du * xcv).astype(BF16)
        dproj_ref[rows, 1792:2304] = (du * gcv_).astype(BF16)
        small_ref[2:3, :] += _colsum(dy * u2)
        small_ref[3:4, :] += _colsum(dy * u1)
        small_ref[4:5, :] += _colsum(dy * u)

        attn_v = attn_ref[rows, :]
        dout, dg_attn = _group_norm_bwd(dm[:, 0:512], attn_v, ga_ref[...])
        small_ref[0:1, :] += dg_attn
        ks, vs = _load_kv_window(kv_ref, n)
        lane = lax.broadcasted_iota(jnp.int32, (BLOCK, BLOCK), 1)
        low = lane < HEAD_DIM
        col = lax.broadcasted_iota(jnp.int32, (BLOCK, 2 * BLOCK), 1)
        no_prev = (col < BLOCK) & (n == 0)
        lse_all = lse_ref[rows, :]
        dsink = jnp.zeros((BLOCK, BLOCK), F32)
        dq_pairs = []
        dk_groups, dv_groups = [], []
        for kvh in range(2):
            ds_rows, pr_rows, q_rows, do_rows = [], [], [], []
            for p in (2 * kvh, 2 * kvh + 1):
                qp = q_ref[rows, 128 * p:128 * (p + 1)].astype(F32)
                do_p = dout[:, 128 * p:128 * (p + 1)]
                prod = do_p * attn_v[:, 128 * p:128 * (p + 1)]
                res = []
                for e in range(2):
                    h = 2 * p + e
                    half = low if e == 0 else ~low
                    qm = jnp.where(half, qp, 0.0).astype(BF16)
                    dom = jnp.where(half, do_p, 0.0).astype(BF16)
                    delta = jnp.sum(jnp.where(half, prod, 0.0), axis=-1, keepdims=True)
                    lse_h = jnp.sum(jnp.where(lane == h, lse_all, 0.0), axis=-1, keepdims=True)
                    sw = 0 if kvh == e else 1
                    sc = _dot_nt(qm, ks[sw]) * SCALE + bias_ref[h]
                    sc = jnp.where(no_prev, NEG_INF, sc)
                    pr = jnp.exp(sc - lse_h)
                    dp = _dot_nt(dom, vs[sw])
                    ds = pr * (dp - delta)
                    dbias_ref[h] += ds
                    dsink = dsink + jnp.where(lane == h, -jnp.exp(sink_ref[h] - lse_h) * delta, 0.0)
                    dsb = ds.astype(BF16)
                    res.append(_dot(dsb, ks[sw]) * SCALE)
                    ds_rows.append(dsb)
                    pr_rows.append(pr.astype(BF16))
                    q_rows.append(qm)
                    do_rows.append(dom)
                dq_pairs.append(jnp.where(low, res[0], res[1]))
            dk_g = _dot_tn(jnp.concatenate(ds_rows, axis=0), jnp.concatenate(q_rows, axis=0)) * SCALE
            dv_g = _dot_tn(jnp.concatenate(pr_rows, axis=0), jnp.concatenate(do_rows, axis=0))
            dk_groups.append(dk_g + pltpu.roll(dk_g, 64, 1))
            dv_groups.append(dv_g + pltpu.roll(dv_g, 64, 1))
        dproj_ref[rows, 0:512] = jnp.concatenate(dq_pairs, axis=1).astype(BF16)
        dsink_ref[...] += dsink
        low_kv = lax.broadcasted_iota(jnp.int32, (2 * BLOCK, BLOCK), 1) < HEAD_DIM
        dkv_win = jnp.concatenate([jnp.where(low_kv, dk_groups[0], dk_groups[1]),
                                   jnp.where(low_kv, dv_groups[0], dv_groups[1])], axis=1)
        dproj_ref[rows, 512:768] = (dkv_win[BLOCK:2 * BLOCK, :] + next_dkv).astype(BF16)
        return dy[0:8, :], dkv_win[0:BLOCK, :]

    def body(sink_ref, q_ref, kv_ref, gb_ref, gc_ref, xc_ref, gcp_ref, xcp_ref, *rest):
        refs, dy_ref, dkv_ref = rest[:-2], rest[-2], rest[-1]
        dbias_ref, dsink_ref, small_ref = refs[8], refs[9], refs[10]
        step = pl.program_id(0)

        @pl.when(step == 0)
        def _():
            dbias_ref[...] = jnp.zeros_like(dbias_ref)
            dsink_ref[...] = jnp.zeros_like(dsink_ref)
            small_ref[...] = jnp.zeros_like(small_ref)
            dy_ref[...] = jnp.zeros_like(dy_ref)
            dkv_ref[...] = jnp.zeros_like(dkv_ref)

        nxt = (dy_ref[...], dkv_ref[...])
        for sub in reversed(range(per_step)):
            rows = slice(sub * BLOCK, (sub + 1) * BLOCK)
            ahead = slice(sub * BLOCK - PREV_ROWS, sub * BLOCK)
            before = (gcp_ref[...], xcp_ref[...]) if sub == 0 else (gc_ref[ahead, :], xc_ref[ahead, :])
            nxt = one_block((steps - 1 - step) * per_step + sub, rows, before, nxt,
                            sink_ref, q_ref, kv_ref, gb_ref, gc_ref, xc_ref, *refs)
        dy_ref[...], dkv_ref[...] = nxt

        @pl.when(step == steps - 1)
        def _():
            small_ref[5:6, :] = jnp.concatenate([_colsum(dsink_ref[...]), jnp.zeros((1, 512 - BLOCK), F32)], axis=1)

    blk = lambda w: pl.BlockSpec((tile, w), lambda t: (steps - 1 - t, 0))
    prev8 = pl.BlockSpec((PREV_ROWS, 512),
                         lambda t: (jnp.maximum((steps - 1 - t) * (tile // PREV_ROWS) - 1, 0), 0))
    bf = lambda w: jax.ShapeDtypeStruct((s, w), BF16)
    return pl.pallas_call(
        _coming_behind(body), name="mixer_bwd", grid=(steps,),
        in_specs=[ANY_SPEC, pl.BlockSpec(memory_space=pltpu.SMEM), blk(512), _full((s, 256)), blk(512), blk(512), blk(512),
                  prev8, prev8, _full((N_Q_HEADS, BLOCK, 2 * BLOCK)), _full((3, 512)), _full((1, 512)),
                  _full((1, 512)), blk(512), blk(128), blk(1024)],
        out_specs=[blk(IN_PROJ_WIDTH), _full((N_Q_HEADS, BLOCK, 2 * BLOCK)), _full((BLOCK, BLOCK)), _full((8, 512))],
        out_shape=[bf(IN_PROJ_WIDTH), jax.ShapeDtypeStruct((N_Q_HEADS, BLOCK, 2 * BLOCK), F32),
                   jax.ShapeDtypeStruct((BLOCK, BLOCK), F32), jax.ShapeDtypeStruct((8, 512), F32)],
        scratch_shapes=[pltpu.VMEM((8, 512), F32), pltpu.VMEM((BLOCK, 2 * KV_WIDTH), F32)],
        compiler_params=_params(("arbitrary",), VMEM_LIMIT_LARGE),
    )(after, sinks, q, kv, gb, gc, xc, gc, xc, bias, conv_w, g_attn, g_conv, attn, lse, dmerged)


def _in_proj_bwd(after, dproj, x, dx1, mod, g_norm1, w_in, tm):
    s = x.shape[0]

    def body(dproj_ref, x_ref, dx1_ref, mod_ref, g_ref, w_ref, dx_ref, small_ref):
        @pl.when(pl.program_id(0) == 0)
        def _():
            small_ref[...] = jnp.zeros_like(small_ref)

        dh = _dot(dproj_ref[...], w_ref[...])
        dx_ref[...] = dx1_ref[...].astype(F32) + _norm_mod_bwd(dh, x_ref[...], g_ref[...], mod_ref[SC1:SC1 + 1, :],
                                                               small_ref)

    return pl.pallas_call(
        _coming_behind(body), name="in_proj_bwd", grid=(s // tm,),
        in_specs=[ANY_SPEC, _rows(tm, IN_PROJ_WIDTH), _rows(tm, D_MODEL), _rows(tm, D_MODEL), _full((8, D_MODEL)),
                  _full((1, D_MODEL)), _full((IN_PROJ_WIDTH, D_MODEL))],
        out_specs=[_rows(tm, D_MODEL), _full((8, D_MODEL))],
        out_shape=[jax.ShapeDtypeStruct((s, D_MODEL), F32), jax.ShapeDtypeStruct((8, D_MODEL), F32)],
        compiler_params=_params(("arbitrary",), VMEM_LIMIT_LARGE),
    )(after, dproj, x, dx1, mod, g_norm1, w_in)


def _weight_grad(a, b, tk, ts, name, after=None):
    s, k = a.shape
    n = b.shape[1]
    nt = s // ts
    extra = [] if after is None else [after]

    def body(a_ref, b_ref, *rest):
        o_ref, acc_ref = rest[-2:]
        t = pl.program_id(1)
        part = _dot_tn(a_ref[...], b_ref[...])

        @pl.when(t == 0)
        def _():
            acc_ref[...] = part

        @pl.when(t > 0)
        def _():
            acc_ref[...] += part

        @pl.when(t == nt - 1)
        def _():
            o_ref[...] = acc_ref[...].astype(BF16)

    return pl.pallas_call(
        body, name=name, grid=(k // tk, nt),
        in_specs=[pl.BlockSpec((ts, tk), lambda i, t: (t, i)), pl.BlockSpec((ts, n), lambda i, t: (t, 0))]
        + [pl.BlockSpec(memory_space=pl.ANY)] * len(extra),
        out_specs=pl.BlockSpec((tk, n), lambda i, t: (i, 0)),
        out_shape=jax.ShapeDtypeStruct((k, n), BF16),
        scratch_shapes=[pltpu.VMEM((tk, n), F32)],
        compiler_params=_params(("arbitrary", "arbitrary"), VMEM_LIMIT_LARGE),
    )(a, b, *extra)


def _rel_bias_grad(dbias, bucket):
    def body(db_ref, bk_ref, o_ref, rows_ref):
        bk = bk_ref[...]
        for b in range(N_BUCKETS):
            sel = (bk == b).astype(F32)
            for h in range(N_Q_HEADS):
                rows_ref[N_BUCKETS * h + b:N_BUCKETS * h + b + 1, :] = _colsum(db_ref[h] * sel)
        head = lax.broadcasted_iota(jnp.int32, (N_BUCKETS, N_Q_HEADS), 1)
        out = jnp.zeros((N_BUCKETS, N_Q_HEADS), F32)
        for h in range(N_Q_HEADS):
            per_bucket = jnp.sum(rows_ref[N_BUCKETS * h:N_BUCKETS * (h + 1), :], axis=-1, keepdims=True)
            out = out + jnp.where(head == h, per_bucket, 0.0)
        o_ref[...] = out

    return pl.pallas_call(
        body, name="rel_bias_grad",
        out_shape=jax.ShapeDtypeStruct((N_BUCKETS, N_Q_HEADS), F32),
        scratch_shapes=[pltpu.VMEM((N_BUCKETS * N_Q_HEADS, 2 * BLOCK), F32)],
    )(dbias, bucket)


def _lanes_from(x, start, width):
    n = x.shape[1]
    return pltpu.roll(x, (n - start) % n, 1)[:, 0:width]


def _w_ada_grad(me, cond_all, packed_all, cols):
    def body(me_ref, c_ref, p_ref, o_ref):
        dmod = jnp.concatenate([p_ref[k][:, OFF_DMOD:OFF_DMOD + N_MOD * D_MODEL] for k in range(N_DEV)], axis=0)
        mine = _lanes_from(dmod, me_ref[0] * cols, cols)
        pad = lambda a: jnp.concatenate([a, jnp.zeros((128 - N_DEV, a.shape[1]), F32)], axis=0)
        o_ref[...] = _dot_tn(pad(c_ref[...]), pad(mine))

    vmem = pl.BlockSpec(memory_space=pltpu.VMEM)
    return pl.pallas_call(body, name="w_ada_grad",
                          in_specs=[pl.BlockSpec(memory_space=pltpu.SMEM), vmem, vmem],
                          out_shape=jax.ShapeDtypeStruct((cond_all.shape[1], cols), F32))(me, cond_all, packed_all)


SMALL_PARAMS = (("rel_bias", None), ("b_ada", (OFF_DMOD, N_MOD * D_MODEL)), ("g_norm1", (OFF_GN1, D_MODEL)),
                ("sinks", (OFF_SINK, N_Q_HEADS)), ("conv_w", None), ("g_attn_out", (OFF_GATT, ATTN_WIDTH)),
                ("g_conv_out", (OFF_GCV, CONV_WIDTH)), ("g_norm2", (OFF_GN2, D_MODEL)),
                ("g_final", (OFF_GFIN, D_MODEL)))


def _small_update(me, packed_all, rel_all, state, after):
    n_p = len(SMALL_PARAMS)
    flat = [a for triple in state for a in triple]
    conv_cols = state[4][0].shape[1]

    def body(me_ref, p_ref, r_ref, *refs):
        ins = refs[:3 * n_p]
        loss_ref, outs = refs[3 * n_p + len(after)], refs[3 * n_p + len(after) + 1:]
        small, rel = p_ref[0], r_ref[0]
        for k in range(1, N_DEV):
            small = small + p_ref[k]
            rel = rel + r_ref[k]
        loss_ref[...] = small[:, OFF_LOSS:OFF_LOSS + 128]
        taps = jnp.concatenate([small[:, OFF_CONVW + CONV_WIDTH * j:OFF_CONVW + CONV_WIDTH * (j + 1)]
                                for j in range(3)] + [jnp.zeros((5, CONV_WIDTH), F32)], axis=0)
        conv_g = _lanes_from(taps, me_ref[0] * conv_cols, conv_cols)[0:3, :]
        for i, (name, lanes) in enumerate(SMALL_PARAMS):
            g = rel if name == "rel_bias" else conv_g if name == "conv_w" else small[:, lanes[0]:lanes[0] + lanes[1]]
            w_ref, m_ref, v_ref = ins[3 * i:3 * i + 3]
            outs[4 * i][...] = g
            outs[4 * i + 1][...], outs[4 * i + 2][...], outs[4 * i + 3][...] = _adam_math(
                w_ref[...], g, m_ref[...], v_ref[...])

    vmem = pl.BlockSpec(memory_space=pltpu.VMEM)
    out_shape = [jax.ShapeDtypeStruct((1, 128), F32)]
    for w, _, _ in state:
        out_shape += [jax.ShapeDtypeStruct(w.shape, F32)] * 4
    outs = pl.pallas_call(
        body, name="small_update",
        in_specs=[pl.BlockSpec(memory_space=pltpu.SMEM), vmem, vmem] + [vmem] * len(flat)
        + [pl.BlockSpec(memory_space=pl.ANY)] * len(after),
        out_shape=out_shape,
    )(me, packed_all, rel_all, *flat, *after)
    return outs[0], [tuple(outs[1 + 4 * i:5 + 4 * i]) for i in range(n_p)]


def _adam_math(w, g, m, v):
    m = ADAM_B1 * m + (1.0 - ADAM_B1) * g
    v = ADAM_B2 * v + (1.0 - ADAM_B2) * (g * g)
    m_hat = m / (1.0 - ADAM_B1 ** ADAM_STEP)
    v_hat = v / (1.0 - ADAM_B2 ** ADAM_STEP)
    delta = -ADAM_LR * (m_hat / (jnp.sqrt(v_hat) + ADAM_EPS) + ADAM_WD * w)
    return delta, m, v


def _adamw_parts(w, m, v, local, land, me, tr, name):
    r, c = w.shape

    def body(me_ref, w_ref, m_ref, v_ref, own_ref, land_ref, g_ref, d_ref, mo_ref, vo_ref):
        g = own_ref[0].astype(F32)
        for k in range(N_DEV - 1):
            g = g + land_ref[k].astype(F32)
        g_ref[...] = g
        d_ref[...], mo_ref[...], vo_ref[...] = _adam_math(w_ref[...], g, m_ref[...], v_ref[...])

    tile = pl.BlockSpec((tr, c), lambda i, me_ref: (i, 0))
    return pl.pallas_call(
        body, name=name,
        grid_spec=pltpu.PrefetchScalarGridSpec(
            num_scalar_prefetch=1, grid=(r // tr,),
            in_specs=[tile, tile, tile, pl.BlockSpec((1, tr, c), lambda i, me_ref: (me_ref[0], i, 0)),
                      pl.BlockSpec((N_DEV - 1, tr, c), lambda i, me_ref: (0, i, 0))],
            out_specs=[tile] * 4),
        out_shape=[jax.ShapeDtypeStruct((r, c), F32)] * 4,
        compiler_params=_params(("arbitrary",)),
    )(me, w, m, v, local, land)


def _adamw(w, m, v, g, tr, name):
    r, c = w.shape

    def body(w_ref, m_ref, v_ref, g_ref, d_ref, mo_ref, vo_ref):
        d_ref[...], mo_ref[...], vo_ref[...] = _adam_math(w_ref[...], g_ref[...], m_ref[...], v_ref[...])

    tile = pl.BlockSpec((tr, c), lambda i: (i, 0))
    return pl.pallas_call(
        body, name=name, grid=(r // tr,),
        in_specs=[tile] * 4, out_specs=[tile] * 3,
        out_shape=[jax.ShapeDtypeStruct((r, c), F32)] * 3,
        compiler_params=_params(("arbitrary",)),
    )(w, m, v, g)


def _behind(a, token):
    return a + token[0:a.shape[0], 0:1]


def _local_step(x, target, mod, w_in_t, weights_out_gu, weights_down, rel_bias, g_norm1, sinks, conv_w, g_attn,
                g_conv, g_norm2, g_final, exchange):
    s = x.shape[0]
    tm = min(512, s)
    tm_small = min(256, s)
    bucket = _bucket_table()
    bias = _bias_table(rel_bias, bucket)

    h, q, kv, gb, gc, xc = _in_proj(x, mod, g_norm1, w_in_t, tm)
    attn, merged, lse = _mixer_fwd(q, kv, gb, gc, xc, bias, sinks, conv_w, g_attn, g_conv)
    w_out, w_gu_t = weights_out_gu(merged)
    o1, x1 = _out_proj(merged, x, mod, w_out, tm)
    w_down = weights_down(x1)
    h2, gate, up, act, o2, dx2, fin = _ffn_fwd(x1, mod, g_norm2, w_gu_t, w_down, g_final, target, tm)

    do2, dgu, dx1, sm_2 = _ffn_bwd(dx2, o2, gate, up, x1, mod, g_norm2, w_down, w_gu_t, tm)
    ts = min(WEIGHT_GRAD_ROWS, s)
    tok_down = exchange("w_down", _weight_grad(act, do2, D_FF // 2, ts, "w_down_grad"))
    tok_gu = exchange("w_gu", _weight_grad(dgu, h2, D_FF // 2, ts, "w_gu_grad", after=tok_down))
    do1, dmerged, sm_g1 = _out_proj_bwd(tok_gu, dx1, o1, mod, w_out, tm)
    tok_out = exchange("w_out", _weight_grad(merged, do1, D_MODEL, ts, "w_out_grad"))
    dproj, dbias, dsink, sm_mix = _mixer_bwd(
        tok_out, q, kv, gb, gc, xc, bias, sinks, conv_w, g_attn, g_conv, attn, lse, dmerged)
    tok_in = exchange("w_in", _weight_grad(dproj, h, IN_PROJ_WIDTH // 2, ts, "w_in_grad"))
    dx, sm_1 = _in_proj_bwd(tok_in, dproj, x, dx1, mod, g_norm1, w_in_t, tm)
    d_rel = _rel_bias_grad(dbias, bucket)

    packed = jnp.concatenate([
        sm_1[0], sm_1[1], sm_g1[0], sm_2[0], sm_2[1], sm_2[3],
        sm_1[2],
        sm_mix[5, 0:128],
        sm_mix[0], sm_mix[1],
        sm_2[2],
        fin[0],
        sm_mix[2], sm_mix[3], sm_mix[4],
        fin[2, 0:128],
    ])[None, :]
    return dx, packed, d_rel


def kernel(x, c, rel_bias, w_ada, b_ada, g_norm1, w_in, sinks, conv_w, g_attn_out, g_conv_out, w_out, g_norm2, w_gu, w_down, g_final, loss_target, m_rel_bias, m_w_ada, m_b_ada, m_g_norm1, m_w_in, m_sinks, m_conv_w, m_g_attn_out, m_g_conv_out, m_w_out, m_g_norm2, m_w_gu, m_w_down, m_g_final, v_rel_bias, v_w_ada, v_b_ada, v_g_norm1, v_w_in, v_sinks, v_conv_w, v_g_attn_out, v_g_conv_out, v_w_out, v_g_norm2, v_w_gu, v_w_down, v_g_final):
    me = _linear(_mesh_position())
    me_arr = jnp.reshape(me, (1,)).astype(jnp.int32)
    ada_cols = w_ada.shape[2]
    tm = min(512, x.shape[1])

    cond = _silu_rows(c)
    cond_all, conv_w_all = _all_gather_small([cond, conv_w[0]], "gather_cond")
    cond_all = cond_all[:, 0, :]
    conv_cols = conv_w.shape[2]
    conv_w_full = conv_w_all.transpose(1, 0, 2).reshape(3, CONV_WIDTH)
    b_cols = lax.dynamic_slice_in_dim(b_ada, me * ada_cols, ada_cols, axis=1)
    mod_cols = _mod_columns(cond_all, w_ada[0], b_cols)
    mod_all = _all_gather_small([mod_cols], "gather_mod")[0]
    mod = lax.dynamic_index_in_dim(mod_all, me, axis=1, keepdims=False).reshape(N_MOD, D_MODEL)
    mod = jnp.concatenate([mod, jnp.zeros((2, D_MODEL), F32)], axis=0)

    w_in_t = _all_gather([w_in[0].T], "gather_w_in", to_bf16=True, big=True)[0].reshape(IN_PROJ_WIDTH, D_MODEL)
    gather_sems, staged, gather_token = _gather_start(
        _stage_blocks([w_out[0], w_gu[0].T, w_down[0]], w_in_t, "stage_weights"), "gather_start_weights")
    mod = _behind(mod, gather_token)

    def weights_out_gu(after):
        got = _gather_pass_on(_gather_wait(gather_sems[0:4], staged[0:2], [after], "gather_wait_out_gu"),
                              "gather_pass_on_out_gu")
        return got[0].reshape(D_MODEL, D_MODEL), got[1].reshape(2 * D_FF, D_MODEL)

    def weights_down(after):
        got = _gather_pass_on(_gather_wait(gather_sems[4:6], staged[2:3], [after], "gather_wait_down"),
                              "gather_pass_on_down")
        return got[0].reshape(D_FF, D_MODEL)

    started = {}

    def exchange(name, dw):
        st = _exchange_start(dw.reshape(N_DEV, dw.shape[0] // N_DEV, dw.shape[1]), "exchange_start_" + name)
        started[name] = st
        return st[4]

    dx, packed, d_rel = _local_step(
        x[0], loss_target[0], mod, w_in_t, weights_out_gu, weights_down, rel_bias, g_norm1, sinks[0], conv_w_full,
        g_attn_out, g_conv_out, g_norm2, g_final[None, :], exchange)

    def zone(a):
        return lax.dynamic_update_slice(jnp.zeros((N_DEV,) + a.shape, F32), a[None], (me,) + (0,) * a.ndim)

    shared = _share_start([packed, d_rel], [zone(packed), zone(d_rel)], "share_small_start")

    def finish(name, after, w, m, v, tr):
        src, land = _exchange_wait(started[name], after, "exchange_wait_" + name)
        return _adamw_parts(w, m, v, src, land, me_arr, tr, "adamw_" + name)

    g_down, d_down, nm_down, nv_down = finish("w_down", [shared[2][0]], w_down[0], m_w_down[0], v_w_down[0], 176)
    g_gu, d_gu, nm_gu, nv_gu = finish("w_gu", [nv_down], w_gu[0].T, m_w_gu[0].T, v_w_gu[0].T, 352)
    g_out, d_out, nm_out, nv_out = finish("w_out", [nv_gu], w_out[0], m_w_out[0], v_w_out[0], 128)

    packed_all, rel_all = _share_wait(shared, [nv_out], "share_small_wait")
    g_ada = _w_ada_grad(me_arr, cond_all, packed_all, ada_cols)
    d_ada, nm_ada, nv_ada = _adamw(w_ada[0], m_w_ada[0], v_w_ada[0], g_ada, 256, "adamw_w_ada")
    as_rows = {"conv_w": lambda a: a[0], "g_final": lambda a: a[None, :]}
    small_state = {
        "rel_bias": (rel_bias, m_rel_bias, v_rel_bias), "b_ada": (b_ada, m_b_ada, v_b_ada),
        "g_norm1": (g_norm1, m_g_norm1, v_g_norm1), "sinks": (sinks, m_sinks, v_sinks),
        "conv_w": (conv_w, m_conv_w, v_conv_w), "g_attn_out": (g_attn_out, m_g_attn_out, v_g_attn_out),
        "g_conv_out": (g_conv_out, m_g_conv_out, v_g_conv_out), "g_norm2": (g_norm2, m_g_norm2, v_g_norm2),
        "g_final": (g_final, m_g_final, v_g_final),
    }
    state = [tuple(as_rows.get(name, lambda a: a)(a) for a in small_state[name]) for name, _ in SMALL_PARAMS]
    loss_row, small_out = _small_update(me_arr, packed_all, rel_all, state, [])
    loss = loss_row[0, 0]
    small_res = {name: tuple(a.reshape(small_state[name][0].shape) for a in res)
                 for (name, _), res in zip(SMALL_PARAMS, small_out)}

    g_in, d_in, nm_in, nv_in = finish("w_in", [loss_row, nv_ada], w_in[0].T, m_w_in[0].T, v_w_in[0].T, 144)

    big = {
        "w_ada": (g_ada[None], d_ada[None], nm_ada[None], nv_ada[None]),
        "w_in": (g_in.T[None], d_in.T[None], nm_in.T[None], nv_in.T[None]),
        "w_out": (g_out[None], d_out[None], nm_out[None], nv_out[None]),
        "w_gu": (g_gu.T[None], d_gu.T[None], nm_gu.T[None], nv_gu.T[None]),
        "w_down": (g_down[None], d_down[None], nm_down[None], nv_down[None]),
    }
    order = ["rel_bias", "w_ada", "b_ada", "g_norm1", "w_in", "sinks", "conv_w", "g_attn_out", "g_conv_out", "w_out",
             "g_norm2", "w_gu", "w_down", "g_final"]
    results = [big[k] if k in big else small_res[k] for k in order]
    return (loss, dx[None], *[r[0] for r in results], *[r[1] for r in results], *[r[2] for r in results],
            *[r[3] for r in results])
```

```python
import functools
import math

import jax
import jax.numpy as jnp
from jax import lax
from jax.experimental import pallas as pl
from jax.experimental.pallas import tpu as pltpu

F32 = jnp.float32
BF16 = jnp.bfloat16

D_MODEL = 1024
HEAD_DIM = 64
N_Q_HEADS = 8
ATTN_WIDTH = 512
KV_WIDTH = 128
CONV_WIDTH = 512
IN_PROJ_WIDTH = 2304
D_FF = 2816
N_MOD = 6
N_BUCKETS = 32
MAX_DISTANCE = 128
BLOCK = 128
EPS = 1e-6
NEG_INF = -1e30
SCALE = HEAD_DIM ** -0.5
N_DEV = 8

ADAM_LR = 0.001
ADAM_B1 = 0.9
ADAM_B2 = 0.999
ADAM_EPS = 1e-08
ADAM_WD = 0.01
ADAM_STEP = 10

SH1, SC1, G1, SH2, SC2, G2 = range(6)

VMEM_LIMIT_LARGE = 60 * 1024 * 1024
WEIGHT_GRAD_ROWS = 2048
FFN_CHUNKS = 2
PREV_ROWS = 16
MIXER_BLOCKS = 4
MESH_ID = pl.DeviceIdType.MESH

OFF_DMOD = 0
OFF_GN1 = OFF_DMOD + N_MOD * D_MODEL
OFF_SINK = OFF_GN1 + D_MODEL
OFF_GATT = OFF_SINK + 128
OFF_GCV = OFF_GATT + ATTN_WIDTH
OFF_GN2 = OFF_GCV + CONV_WIDTH
OFF_GFIN = OFF_GN2 + D_MODEL
OFF_CONVW = OFF_GFIN + D_MODEL
OFF_LOSS = OFF_CONVW + 3 * CONV_WIDTH
PACKED = OFF_LOSS + 128


def _params(sem=None, vmem=None):
    return pltpu.CompilerParams(dimension_semantics=sem, vmem_limit_bytes=vmem)


def _coming_behind(body):
    def skipping(after_ref, *refs):
        body(*refs)

    return skipping


ANY_SPEC = pl.BlockSpec(memory_space=pl.ANY)


def _full(shape):
    nd = len(shape)
    return pl.BlockSpec(shape, lambda *_: (0,) * nd)


def _rows(tm, width):
    return pl.BlockSpec((tm, width), lambda i, *_: (i, 0))


def _sigmoid(x):
    return 1.0 / (1.0 + jnp.exp(-x))


def _rsqrt_mean_sq(x):
    return lax.rsqrt(jnp.mean(x * x, axis=-1, keepdims=True) + EPS)


def _colsum(x):
    return jnp.sum(x, axis=0, keepdims=True)


def _dot(a, b):
    return jnp.dot(a, b, preferred_element_type=F32)


def _dot_nt(a, b):
    return lax.dot_general(a, b, (((1,), (1,)), ((), ())), preferred_element_type=F32)


def _dot_tn(a, b):
    return lax.dot_general(a, b, (((0,), (0,)), ((), ())), preferred_element_type=F32)


def _mesh_position():
    return lax.axis_index("x"), lax.axis_index("y"), lax.axis_index("c")


def _linear(p):
    return 4 * p[0] + 2 * p[1] + p[2]


def _all_gather(arrs, name, to_bf16, big):
    n = len(arrs)
    out_dtype = BF16 if to_bf16 else F32

    def body(*refs):
        in_refs, out_refs = refs[:n], refs[n:2 * n]
        rest = refs[2 * n:]
        if to_bf16:
            stage, rest = rest[:n], rest[n:]
            for a in range(n):
                stage[a][...] = in_refs[a][...].astype(BF16)
            srcs = stage
        else:
            srcs = in_refs
        send_sems, recv_sems, local_sems = rest
        x, y, c = _mesh_position()
        me, sibling = (x, y, c), (x, y, 1 - c)
        chips = [(1 - x, y), (x, 1 - y), (1 - x, 1 - y)]

        def slot(a, p):
            return out_refs[a].at[_linear(p)]

        def copy(k, a, block, to, src=None):
            return pltpu.make_async_remote_copy(
                src_ref=slot(a, block) if src is None else src,
                dst_ref=slot(a, block),
                send_sem=send_sems.at[k * n + a],
                recv_sem=recv_sems.at[k * n + a],
                device_id=to,
                device_id_type=MESH_ID,
            )

        mine = [pltpu.make_async_copy(srcs[a], slot(a, me), local_sems.at[a]) for a in range(n)]
        for cp in mine:
            cp.start()
        first = [copy(0, a, me, sibling, src=srcs[a]) for a in range(n)]
        for j, chip in enumerate(chips):
            first += [copy(1 + j, a, me, (*chip, c), src=srcs[a]) for a in range(n)]
        for cp in first:
            cp.start()
        passed = []
        for j, chip in enumerate(chips):
            for a in range(n):
                copy(1 + j, a, (*chip, c), me).wait_recv()
                fwd = copy(4 + j, a, (*chip, c), sibling)
                fwd.start()
                passed.append(fwd)
        for a in range(n):
            copy(0, a, sibling, me).wait_recv()
        for j, chip in enumerate(chips):
            for a in range(n):
                copy(4 + j, a, (*chip, 1 - c), me).wait_recv()
        for cp in first + passed:
            cp.wait_send()
        for cp in mine:
            cp.wait()

    vmem = pl.BlockSpec(memory_space=pltpu.VMEM)
    out_space = pl.BlockSpec(memory_space=pl.ANY) if big else vmem
    scratch = [pltpu.VMEM(a.shape, BF16) for a in arrs] if to_bf16 else []
    scratch += [pltpu.SemaphoreType.DMA((7 * n,)), pltpu.SemaphoreType.DMA((7 * n,)),
                pltpu.SemaphoreType.DMA((n,))]
    outs = pl.pallas_call(
        body, name=name,
        out_shape=[jax.ShapeDtypeStruct((N_DEV,) + a.shape, out_dtype) for a in arrs],
        in_specs=[vmem] * n, out_specs=[out_space] * n,
        scratch_shapes=scratch,
        compiler_params=_params(vmem=VMEM_LIMIT_LARGE if big else None),
    )(*arrs)
    return list(outs)


def _peer(k):
    x, y, c = _mesh_position()
    return (1 - x if k & 4 else x, 1 - y if k & 2 else y, 1 - c if k & 1 else c)


def _all_gather_small(arrs, name):
    n = len(arrs)

    def body(*refs):
        in_refs, out_refs = refs[:n], refs[n:2 * n]
        send_sems, recv_sems, local_sems = refs[2 * n:]
        me = _linear(_mesh_position())
        mine = [pltpu.make_async_copy(in_refs[a], out_refs[a].at[me], local_sems.at[a]) for a in range(n)]
        for cp in mine:
            cp.start()
        sends = []
        for k in range(1, N_DEV):
            for a in range(n):
                sends.append(pltpu.make_async_remote_copy(
                    src_ref=in_refs[a], dst_ref=out_refs[a].at[me],
                    send_sem=send_sems.at[(k - 1) * n + a], recv_sem=recv_sems.at[(k - 1) * n + a],
                    device_id=_peer(k), device_id_type=MESH_ID))
                sends[-1].start()
        for k in range(1, N_DEV):
            for a in range(n):
                pltpu.make_async_remote_copy(
                    src_ref=in_refs[a], dst_ref=out_refs[a].at[_linear(_peer(k))],
                    send_sem=send_sems.at[(k - 1) * n + a], recv_sem=recv_sems.at[(k - 1) * n + a],
                    device_id=_peer(k), device_id_type=MESH_ID).wait_recv()
        for cp in sends:
            cp.wait_send()
        for cp in mine:
            cp.wait()

    vmem = pl.BlockSpec(memory_space=pltpu.VMEM)
    return list(pl.pallas_call(
        body, name=name,
        out_shape=[jax.ShapeDtypeStruct((N_DEV,) + a.shape, F32) for a in arrs],
        in_specs=[vmem] * n, out_specs=[vmem] * n,
        scratch_shapes=[pltpu.SemaphoreType.DMA((7 * n,)), pltpu.SemaphoreType.DMA((7 * n,)),
                        pltpu.SemaphoreType.DMA((n,))],
    )(*arrs))


HBM_SPEC = pl.BlockSpec(memory_space=pltpu.HBM)
SEM_SPEC = pl.BlockSpec(memory_space=pltpu.SEMAPHORE)
DATAFLOW = pltpu.SideEffectType.DATAFLOW_SIDE_EFFECTING


def _exchange_start(src, name):
    r, c = src.shape[1:]

    def body(src_ref, land_ref, send_sems, recv_sems, src_thru, land_thru, token):
        for k in range(1, N_DEV):
            peer = _peer(k)
            pltpu.make_async_remote_copy(
                src_ref=src_ref.at[_linear(peer)], dst_ref=land_ref.at[k - 1],
                send_sem=send_sems.at[k - 1], recv_sem=recv_sems.at[k - 1],
                device_id=peer, device_id_type=MESH_ID).start()
        token[...] = jnp.zeros_like(token)

    land = lax.empty((N_DEV - 1, r, c), src.dtype)
    return pl.pallas_call(
        body, name=name,
        out_shape=(pltpu.SemaphoreType.DMA((N_DEV - 1,)), pltpu.SemaphoreType.DMA((N_DEV - 1,)),
                   pltpu.HBM(src.shape, src.dtype), pltpu.HBM(land.shape, land.dtype),
                   jax.ShapeDtypeStruct((8, 128), F32)),
        in_specs=(HBM_SPEC, HBM_SPEC),
        out_specs=(SEM_SPEC, SEM_SPEC, HBM_SPEC, HBM_SPEC, pl.BlockSpec(memory_space=pltpu.VMEM)),
        input_output_aliases={0: 2, 1: 3},
        compiler_params=pltpu.CompilerParams(has_side_effects=DATAFLOW),
    )(pltpu.with_memory_space_constraint(src, pltpu.HBM), pltpu.with_memory_space_constraint(land, pltpu.HBM))


def _exchange_wait(started, after, name):
    send_sems, recv_sems, src_thru, land_thru, _ = started

    def body(src_ref, land_ref, send_sems, recv_sems, *rest):
        for k in range(1, N_DEV):
            cp = pltpu.make_async_remote_copy(
                src_ref=src_ref.at[0], dst_ref=land_ref.at[k - 1],
                send_sem=send_sems.at[k - 1], recv_sem=recv_sems.at[k - 1],
                device_id=_peer(k), device_id_type=MESH_ID)
            cp.wait_send()
            cp.wait_recv()

    return pl.pallas_call(
        body, name=name,
        out_shape=(pltpu.HBM(src_thru.shape, src_thru.dtype), pltpu.HBM(land_thru.shape, land_thru.dtype)),
        in_specs=(HBM_SPEC, HBM_SPEC, SEM_SPEC, SEM_SPEC) + (pl.BlockSpec(memory_space=pl.ANY),) * len(after),
        out_specs=(HBM_SPEC, HBM_SPEC), input_output_aliases={0: 0, 1: 1},
        compiler_params=pltpu.CompilerParams(has_side_effects=DATAFLOW),
    )(src_thru, land_thru, send_sems, recv_sems, *after)


def _share_start(arrs, zones, name):
    n = len(arrs)

    def body(*refs):
        src_refs, zone_refs, sems = refs[:n], refs[n:2 * n], refs[2 * n:4 * n]
        me = _linear(_mesh_position())
        for a in range(n):
            for k in range(1, N_DEV):
                pltpu.make_async_remote_copy(
                    src_ref=src_refs[a], dst_ref=zone_refs[a].at[me],
                    send_sem=sems[2 * a].at[k - 1], recv_sem=sems[2 * a + 1].at[k - 1],
                    device_id=_peer(k), device_id_type=MESH_ID).start()

    outs = pl.pallas_call(
        body, name=name,
        out_shape=tuple(pltpu.SemaphoreType.DMA((N_DEV - 1,)) for _ in range(2 * n))
        + tuple(pltpu.HBM(a.shape, a.dtype) for a in arrs) + tuple(pltpu.HBM(z.shape, z.dtype) for z in zones),
        in_specs=(HBM_SPEC,) * (2 * n),
        out_specs=(SEM_SPEC,) * (2 * n) + (HBM_SPEC,) * (2 * n),
        input_output_aliases={i: 2 * n + i for i in range(2 * n)},
        compiler_params=pltpu.CompilerParams(has_side_effects=DATAFLOW),
    )(*[pltpu.with_memory_space_constraint(a, pltpu.HBM) for a in list(arrs) + list(zones)])
    return outs[:2 * n], outs[2 * n:3 * n], outs[3 * n:]


def _share_wait(started, after, name):
    sems, arrs, zones = started
    n = len(arrs)

    def body(*refs):
        src_refs, zone_refs, sem_refs = refs[:n], refs[n:2 * n], refs[2 * n:4 * n]
        for a in range(n):
            for k in range(1, N_DEV):
                cp = pltpu.make_async_remote_copy(
                    src_ref=src_refs[a], dst_ref=zone_refs[a].at[_linear(_peer(k))],
                    send_sem=sem_refs[2 * a].at[k - 1], recv_sem=sem_refs[2 * a + 1].at[k - 1],
                    device_id=_peer(k), device_id_type=MESH_ID)
                cp.wait_send()
                cp.wait_recv()

    outs = pl.pallas_call(
        body, name=name,
        out_shape=tuple(pltpu.HBM(a.shape, a.dtype) for a in arrs) + tuple(pltpu.HBM(z.shape, z.dtype) for z in zones),
        in_specs=(HBM_SPEC,) * (2 * n) + (SEM_SPEC,) * (2 * n) + (pl.BlockSpec(memory_space=pl.ANY),) * len(after),
        out_specs=(HBM_SPEC,) * (2 * n), input_output_aliases={i: i for i in range(2 * n)},
        compiler_params=pltpu.CompilerParams(has_side_effects=DATAFLOW),
    )(*arrs, *zones, *sems, *after)
    return list(outs[n:])


def _stage_blocks(arrs, after, name):
    n = len(arrs)

    def body(*refs):
        in_refs, out_refs, stage, sems = refs[:n], refs[n + 1:2 * n + 1], refs[2 * n + 1:3 * n + 1], refs[3 * n + 1]
        me = _linear(_mesh_position())
        copies = []
        for a in range(n):
            stage[a][...] = in_refs[a][...].astype(BF16)
            copies.append(pltpu.make_async_copy(stage[a], out_refs[a].at[me], sems.at[a]))
            copies[-1].start()
        for cp in copies:
            cp.wait()

    return list(pl.pallas_call(
        body, name=name,
        out_shape=[jax.ShapeDtypeStruct((N_DEV,) + a.shape, BF16) for a in arrs],
        in_specs=[pl.BlockSpec(memory_space=pltpu.VMEM)] * n + [pl.BlockSpec(memory_space=pl.ANY)],
        out_specs=[pl.BlockSpec(memory_space=pl.ANY)] * n,
        scratch_shapes=[pltpu.VMEM(a.shape, BF16) for a in arrs] + [pltpu.SemaphoreType.DMA((n,))],
        compiler_params=_params(vmem=VMEM_LIMIT_LARGE),
    )(*arrs, after))


def _same_core_peers():
    x, y, c = _mesh_position()
    return [(x, y, 1 - c), (1 - x, y, c), (x, 1 - y, c), (1 - x, 1 - y, c)]


def _gather_start(bufs, name):
    n = len(bufs)

    def body(*refs):
        buf_refs, rest = refs[:n], refs[n:]
        sems, token = rest[:2 * n], rest[-1]
        me = _linear(_mesh_position())
        for a in range(n):
            for k, peer in enumerate(_same_core_peers()):
                pltpu.make_async_remote_copy(
                    src_ref=buf_refs[a].at[me], dst_ref=buf_refs[a].at[me],
                    send_sem=sems[2 * a].at[k], recv_sem=sems[2 * a + 1].at[k],
                    device_id=peer, device_id_type=MESH_ID).start()
        token[...] = jnp.zeros_like(token)

    outs = pl.pallas_call(
        body, name=name,
        out_shape=tuple(pltpu.SemaphoreType.DMA((4,)) for _ in range(2 * n))
        + tuple(pltpu.HBM(b.shape, b.dtype) for b in bufs) + (jax.ShapeDtypeStruct((8, 128), F32),),
        in_specs=(HBM_SPEC,) * n,
        out_specs=(SEM_SPEC,) * (2 * n) + (HBM_SPEC,) * n + (pl.BlockSpec(memory_space=pltpu.VMEM),),
        input_output_aliases={a: 2 * n + a for a in range(n)},
        compiler_params=pltpu.CompilerParams(has_side_effects=DATAFLOW),
    )(*[pltpu.with_memory_space_constraint(b, pltpu.HBM) for b in bufs])
    return outs[:2 * n], outs[2 * n:3 * n], outs[3 * n]


def _gather_wait(sems, bufs, after, name):
    n = len(bufs)

    def body(*refs):
        buf_refs, sem_refs = refs[:n], refs[n:3 * n]
        x, y, c = _mesh_position()
        me = _linear((x, y, c))
        for a in range(n):
            for k, peer in enumerate(_same_core_peers()):
                cp = pltpu.make_async_remote_copy(
                    src_ref=buf_refs[a].at[me], dst_ref=buf_refs[a].at[_linear(peer)],
                    send_sem=sem_refs[2 * a].at[k], recv_sem=sem_refs[2 * a + 1].at[k],
                    device_id=peer, device_id_type=MESH_ID)
                cp.wait_send()
                cp.wait_recv()

    return list(pl.pallas_call(
        body, name=name,
        out_shape=tuple(pltpu.HBM(b.shape, b.dtype) for b in bufs),
        in_specs=(HBM_SPEC,) * n + (SEM_SPEC,) * (2 * n) + (pl.BlockSpec(memory_space=pl.ANY),) * len(after),
        out_specs=(HBM_SPEC,) * n, input_output_aliases={a: a for a in range(n)},
        compiler_params=pltpu.CompilerParams(has_side_effects=DATAFLOW),
    )(*bufs, *sems, *after))


def _gather_pass_on(bufs, name):
    n = len(bufs)

    def body(*refs):
        out_refs = refs[n:2 * n]
        send_sems, recv_sems = refs[2 * n:]
        x, y, c = _mesh_position()
        sibling = (x, y, 1 - c)
        chips = [(1 - x, y), (x, 1 - y), (1 - x, 1 - y)]
        copies = []
        for a in range(n):
            for j, chip in enumerate(chips):
                block = out_refs[a].at[_linear((*chip, c))]
                copies.append(pltpu.make_async_remote_copy(
                    src_ref=block, dst_ref=block, send_sem=send_sems.at[3 * a + j], recv_sem=recv_sems.at[3 * a + j],
                    device_id=sibling, device_id_type=MESH_ID))
                copies[-1].start()
        for a in range(n):
            for j, chip in enumerate(chips):
                copies[3 * a + j].wait_send()
                theirs = out_refs[a].at[_linear((*chip, 1 - c))]
                pltpu.make_async_remote_copy(
                    src_ref=theirs, dst_ref=theirs, send_sem=send_sems.at[3 * a + j], recv_sem=recv_sems.at[3 * a + j],
                    device_id=sibling, device_id_type=MESH_ID).wait_recv()

    hbm = pl.BlockSpec(memory_space=pl.ANY)
    return list(pl.pallas_call(
        body, name=name,
        out_shape=[jax.ShapeDtypeStruct(b.shape, b.dtype) for b in bufs],
        in_specs=[hbm] * n, out_specs=[hbm] * n, input_output_aliases={a: a for a in range(n)},
        scratch_shapes=[pltpu.SemaphoreType.DMA((3 * n,)), pltpu.SemaphoreType.DMA((3 * n,))],
    )(*bufs))


def _silu_rows(c):
    def body(c_ref, o_ref):
        v = c_ref[...]
        o_ref[...] = v * _sigmoid(v)

    return pl.pallas_call(body, name="cond_silu", out_shape=jax.ShapeDtypeStruct(c.shape, F32))(c)


def _mod_columns(cond_all, w_ada, b_cols):
    def body(c_ref, w_ref, b_ref, o_ref):
        o_ref[...] = _dot(c_ref[...], w_ref[...]) + b_ref[...]

    return pl.pallas_call(body, name="mod_columns",
                          out_shape=jax.ShapeDtypeStruct((N_DEV, w_ada.shape[1]), F32))(cond_all, w_ada, b_cols)


def _in_proj(x, mod, g_norm1, w_in, tm):
    s = x.shape[0]

    def body(x_ref, mod_ref, g_ref, w_ref, h_ref, q_ref, kv_ref, gb_ref, gc_ref, xc_ref):
        xf = x_ref[...]
        n = xf * _rsqrt_mean_sq(xf) * g_ref[...]
        h = (n * (1.0 + mod_ref[SC1:SC1 + 1, :]) + mod_ref[SH1:SH1 + 1, :]).astype(BF16)
        h_ref[...] = h
        p = _dot_nt(h, w_ref[...])
        q_ref[...] = p[:, 0:512].astype(BF16)
        kv_ref[...] = p[:, 512:768].astype(BF16)
        gb_ref[...] = p[:, 768:1280].astype(BF16)
        gc_ref[...] = p[:, 1280:1792].astype(BF16)
        xc_ref[...] = p[:, 1792:2304].astype(BF16)

    return pl.pallas_call(
        body, name="in_proj", grid=(s // tm,),
        in_specs=[_rows(tm, D_MODEL), _full((8, D_MODEL)), _full((1, D_MODEL)), _full((IN_PROJ_WIDTH, D_MODEL))],
        out_specs=[_rows(tm, D_MODEL), _rows(tm, 512), _rows(tm, 256), _rows(tm, 512), _rows(tm, 512), _rows(tm, 512)],
        out_shape=[jax.ShapeDtypeStruct((s, D_MODEL), BF16), jax.ShapeDtypeStruct((s, 512), BF16),
                   jax.ShapeDtypeStruct((s, 256), BF16), jax.ShapeDtypeStruct((s, 512), BF16),
                   jax.ShapeDtypeStruct((s, 512), BF16), jax.ShapeDtypeStruct((s, 512), BF16)],
        compiler_params=_params(("arbitrary",), VMEM_LIMIT_LARGE),
    )(x, mod, g_norm1, w_in)


def _t5_bucket(dist):
    max_exact = N_BUCKETS // 2
    is_small = dist < max_exact
    d = jnp.maximum(dist, 1).astype(F32)
    large = max_exact + (jnp.log(d / max_exact) / math.log(MAX_DISTANCE / max_exact)
                         * (N_BUCKETS - max_exact)).astype(jnp.int32)
    large = jnp.minimum(large, N_BUCKETS - 1)
    return jnp.where(is_small, dist, large)


def _bucket_table():
    qi = jnp.arange(BLOCK, dtype=jnp.int32)[:, None]
    sj = jnp.arange(2 * BLOCK, dtype=jnp.int32)[None, :]
    return _t5_bucket(jnp.maximum(qi + BLOCK - sj, 0))


def _window_mask():
    qi = lax.broadcasted_iota(jnp.int32, (BLOCK, 2 * BLOCK), 0)
    sj = lax.broadcasted_iota(jnp.int32, (BLOCK, 2 * BLOCK), 1)
    dist = qi + BLOCK - sj
    return (dist >= 0) & (dist < BLOCK)


def _bias_table(rel_bias, bucket):
    def body(rb_ref, bk_ref, o_ref):
        bk = bk_ref[...]
        inside = _window_mask()
        for h in range(N_Q_HEADS):
            acc = jnp.zeros((BLOCK, 2 * BLOCK), F32)
            for b in range(N_BUCKETS):
                acc = jnp.where(bk == b, rb_ref[b, h], acc)
            o_ref[h] = jnp.where(inside, acc, NEG_INF)

    return pl.pallas_call(
        body, name="bias_table",
        in_specs=[pl.BlockSpec(memory_space=pltpu.SMEM), pl.BlockSpec(memory_space=pltpu.VMEM)],
        out_shape=jax.ShapeDtypeStruct((N_Q_HEADS, BLOCK, 2 * BLOCK), F32),
    )(rel_bias, bucket)


def _load_kv_window(kv_ref, n):
    prev = jnp.maximum(n - 1, 0)
    kvw = jnp.concatenate([kv_ref[pl.ds(pl.multiple_of(prev * BLOCK, BLOCK), BLOCK), :],
                           kv_ref[pl.ds(pl.multiple_of(n * BLOCK, BLOCK), BLOCK), :]], axis=0)
    k, v = kvw[:, 0:128], kvw[:, 128:256]
    k_sw = pltpu.roll(k.astype(F32), 64, 1).astype(BF16)
    v_sw = pltpu.roll(v.astype(F32), 64, 1).astype(BF16)
    return (k, k_sw), (v, v_sw)


def _conv_taps(gc, xc, gc_prev, xc_prev, n):
    u = gc * xc
    before = jnp.where(n > 0, gc_prev.astype(F32) * xc_prev.astype(F32), 0.0)
    last = before.shape[0] - 1
    row = lax.broadcasted_iota(jnp.int32, u.shape, 0)
    u1 = jnp.where(row == 0, before[last:last + 1, :], pltpu.roll(u, 1, 0))
    u2 = jnp.where(row == 0, before[last - 1:last, :],
                   jnp.where(row == 1, before[last:last + 1, :], pltpu.roll(u, 2, 0)))
    return u, u1, u2


def _mixer_fwd(q, kv, gb, gc, xc, bias, sinks, conv_w, g_attn, g_conv):
    s = q.shape[0]
    nb = s // BLOCK

    per_step = min(MIXER_BLOCKS, nb)
    tile = per_step * BLOCK

    def one_block(n, rows, before, sink_ref, q_ref, kv_ref, gb_ref, gc_ref, xc_ref, bias_ref, cw_ref, ga_ref,
                  gcv_ref, attn_ref, merged_ref, lse_ref):
        ks, vs = _load_kv_window(kv_ref, n)
        lane = lax.broadcasted_iota(jnp.int32, (BLOCK, BLOCK), 1)
        low = lane < HEAD_DIM
        col = lax.broadcasted_iota(jnp.int32, (BLOCK, 2 * BLOCK), 1)
        no_prev = (col < BLOCK) & (n == 0)
        lse_all = jnp.zeros((BLOCK, BLOCK), F32)
        pairs = []
        for p in range(4):
            qp = q_ref[rows, 128 * p:128 * (p + 1)].astype(F32)
            kvh = p // 2
            res = []
            for e in range(2):
                h = 2 * p + e
                qm = jnp.where(low if e == 0 else ~low, qp, 0.0).astype(BF16)
                sw = 0 if kvh == e else 1
                sc = _dot_nt(qm, ks[sw]) * SCALE + bias_ref[h]
                sc = jnp.where(no_prev, NEG_INF, sc)
                sink = sink_ref[h]
                m = jnp.maximum(jnp.max(sc, axis=-1, keepdims=True), sink)
                pe = jnp.exp(sc - m)
                den = jnp.sum(pe, axis=-1, keepdims=True) + jnp.exp(sink - m)
                res.append(_dot(pe.astype(BF16), vs[sw]) / den)
                lse_all = lse_all + jnp.where(lane == h, m + jnp.log(den), 0.0)
            pairs.append(jnp.where(low, res[0], res[1]))
        attn = jnp.concatenate(pairs, axis=1)
        attn_ref[rows, :] = attn
        lse_ref[rows, :] = lse_all
        u, u1, u2 = _conv_taps(gc_ref[rows, :].astype(F32), xc_ref[rows, :].astype(F32), before[0], before[1], n)
        cw = cw_ref[...]
        cv = gb_ref[rows, :].astype(F32) * (cw[0:1, :] * u2 + cw[1:2, :] * u1 + cw[2:3, :] * u)
        an = attn * _rsqrt_mean_sq(attn) * ga_ref[...]
        cn = cv * _rsqrt_mean_sq(cv) * gcv_ref[...]
        merged_ref[rows, :] = jnp.concatenate([an, cn], axis=1).astype(BF16)

    def body(sink_ref, q_ref, kv_ref, gb_ref, gc_ref, xc_ref, gcp_ref, xcp_ref, *rest):
        step = pl.program_id(0)
        for sub in range(per_step):
            rows = slice(sub * BLOCK, (sub + 1) * BLOCK)
            ahead = slice(sub * BLOCK - PREV_ROWS, sub * BLOCK)
            before = (gcp_ref[...], xcp_ref[...]) if sub == 0 else (gc_ref[ahead, :], xc_ref[ahead, :])
            one_block(step * per_step + sub, rows, before, sink_ref, q_ref, kv_ref, gb_ref, gc_ref, xc_ref, *rest)

    blk = lambda w: pl.BlockSpec((tile, w), lambda n: (n, 0))
    prev8 = pl.BlockSpec((PREV_ROWS, 512), lambda n: (jnp.maximum(n * (tile // PREV_ROWS) - 1, 0), 0))
    return pl.pallas_call(
        body, name="mixer_fwd", grid=(nb // per_step,),
        in_specs=[pl.BlockSpec(memory_space=pltpu.SMEM), blk(512), _full((s, 256)), blk(512), blk(512), blk(512),
                  prev8, prev8, _full((N_Q_HEADS, BLOCK, 2 * BLOCK)), _full((3, 512)), _full((1, 512)),
                  _full((1, 512))],
        out_specs=[blk(512), blk(1024), blk(128)],
        out_shape=[jax.ShapeDtypeStruct((s, 512), F32), jax.ShapeDtypeStruct((s, 1024), BF16),
                   jax.ShapeDtypeStruct((s, 128), F32)],
        compiler_params=_params(("arbitrary",)),
    )(sinks, q, kv, gb, gc, xc, gc, xc, bias, conv_w, g_attn, g_conv)


def _out_proj(merged, x, mod, w_out, tm):
    s = x.shape[0]

    def body(m_ref, x_ref, mod_ref, w_ref, o_ref, x1_ref):
        o = _dot(m_ref[...], w_ref[...])
        o_ref[...] = o.astype(BF16)
        x1_ref[...] = x_ref[...] + mod_ref[G1:G1 + 1, :] * o

    return pl.pallas_call(
        body, name="out_proj", grid=(s // tm,),
        in_specs=[_rows(tm, D_MODEL), _rows(tm, D_MODEL), _full((8, D_MODEL)), _full((D_MODEL, D_MODEL))],
        out_specs=[_rows(tm, D_MODEL), _rows(tm, D_MODEL)],
        out_shape=[jax.ShapeDtypeStruct((s, D_MODEL), BF16), jax.ShapeDtypeStruct((s, D_MODEL), F32)],
        compiler_params=_params(("arbitrary",)),
    )(merged, x, mod, w_out)


def _resident(shape):
    nd = len(shape)
    return pl.BlockSpec(shape, lambda *_: (0,) * nd, pipeline_mode=pl.Buffered(1))


def _ffn_fwd(x1, mod, g_norm2, w_gu, w_down, g_final, target, tm):
    s = x1.shape[0]
    chunk = D_FF // FFN_CHUNKS

    def body(x_ref, mod_ref, g_ref, wgu_ref, wd_ref, gf_ref, t_ref,
             h_ref, gate_ref, up_ref, act_ref, o_ref, dx2_ref, small_ref):
        @pl.when(pl.program_id(0) == 0)
        def _():
            small_ref[...] = jnp.zeros_like(small_ref)

        xf = x_ref[...]
        n = xf * _rsqrt_mean_sq(xf) * g_ref[...]
        h = (n * (1.0 + mod_ref[SC2:SC2 + 1, :]) + mod_ref[SH2:SH2 + 1, :]).astype(BF16)
        h_ref[...] = h
        o = None
        for j in range(FFN_CHUNKS):
            lo = j * chunk
            gate = _dot_nt(h, wgu_ref[lo:lo + chunk, :])
            up = _dot_nt(h, wgu_ref[D_FF + lo:D_FF + lo + chunk, :])
            gate_ref[:, lo:lo + chunk] = gate.astype(BF16)
            up_ref[:, lo:lo + chunk] = up.astype(BF16)
            act = (gate * _sigmoid(gate) * up).astype(BF16)
            act_ref[:, lo:lo + chunk] = act
            part = _dot(act, wd_ref[lo:lo + chunk, :])
            o = part if o is None else o + part
        o_ref[...] = o.astype(BF16)
        x2 = xf + mod_ref[G2:G2 + 1, :] * o
        r = _rsqrt_mean_sq(x2)
        xn = x2 * r
        gf = gf_ref[...]
        err = xn * gf - t_ref[...]
        dy = err * (1.0 / D_MODEL)
        dxn = dy * gf
        dx2_ref[...] = (r * (dxn - xn * jnp.mean(dxn * xn, axis=-1, keepdims=True))).astype(BF16)
        small_ref[0:1, :] += _colsum(dy * xn)
        small_ref[1:2, :] += _colsum(err * err)

        @pl.when(pl.program_id(0) == pl.num_programs(0) - 1)
        def _():
            total = jnp.sum(small_ref[1:2, :], axis=-1, keepdims=True) * (0.5 / D_MODEL)
            small_ref[2:3, :] = jnp.broadcast_to(total, (1, D_MODEL))

    wide = jax.ShapeDtypeStruct((s, D_FF), BF16)
    return pl.pallas_call(
        body, name="ffn_fwd", grid=(s // tm,),
        in_specs=[_rows(tm, D_MODEL), _full((8, D_MODEL)), _full((1, D_MODEL)), _resident((2 * D_FF, D_MODEL)),
                  _resident((D_FF, D_MODEL)), _full((1, D_MODEL)), _rows(tm, D_MODEL)],
        out_specs=[_rows(tm, D_MODEL), _rows(tm, D_FF), _rows(tm, D_FF), _rows(tm, D_FF), _rows(tm, D_MODEL),
                   _rows(tm, D_MODEL), _full((8, D_MODEL))],
        out_shape=[jax.ShapeDtypeStruct((s, D_MODEL), BF16), wide, wide, wide,
                   jax.ShapeDtypeStruct((s, D_MODEL), BF16), jax.ShapeDtypeStruct((s, D_MODEL), BF16),
                   jax.ShapeDtypeStruct((8, D_MODEL), F32)],
        compiler_params=_params(("arbitrary",), VMEM_LIMIT_LARGE),
    )(x1, mod, g_norm2, w_gu, w_down, g_final, target)


def _ffn_bwd(dx2, o2, gate, up, x1, mod, g_norm2, w_down, w_gu, tm):
    s = x1.shape[0]
    chunk = D_FF // FFN_CHUNKS

    def body(dx_ref, o_ref, gate_ref, up_ref, x_ref, mod_ref, g_ref, wd_ref, wgu_ref,
             do_ref, dgu_ref, dx1_ref, small_ref):
        @pl.when(pl.program_id(0) == 0)
        def _():
            small_ref[...] = jnp.zeros_like(small_ref)

        dx = dx_ref[...].astype(F32)
        small_ref[3:4, :] += _colsum(dx * o_ref[...].astype(F32))
        do = (dx * mod_ref[G2:G2 + 1, :]).astype(BF16)
        do_ref[...] = do
        dh = None
        for j in range(FFN_CHUNKS):
            lo = j * chunk
            dact = _dot_nt(do, wd_ref[lo:lo + chunk, :])
            gate = gate_ref[:, lo:lo + chunk].astype(F32)
            sg = _sigmoid(gate)
            dgate = (dact * up_ref[:, lo:lo + chunk].astype(F32) * (sg * (1.0 + gate * (1.0 - sg)))).astype(BF16)
            dup = (dact * (gate * sg)).astype(BF16)
            dgu_ref[:, lo:lo + chunk] = dgate
            dgu_ref[:, D_FF + lo:D_FF + lo + chunk] = dup
            part = _dot(dgate, wgu_ref[lo:lo + chunk, :]) + _dot(dup, wgu_ref[D_FF + lo:D_FF + lo + chunk, :])
            dh = part if dh is None else dh + part
        dx1 = dx + _norm_mod_bwd(dh, x_ref[...], g_ref[...], mod_ref[SC2:SC2 + 1, :], small_ref)
        dx1_ref[...] = dx1.astype(BF16)

    return pl.pallas_call(
        body, name="ffn_bwd", grid=(s // tm,),
        in_specs=[_rows(tm, D_MODEL), _rows(tm, D_MODEL), _rows(tm, D_FF), _rows(tm, D_FF), _rows(tm, D_MODEL),
                  _full((8, D_MODEL)), _full((1, D_MODEL)), _resident((D_FF, D_MODEL)),
                  _resident((2 * D_FF, D_MODEL))],
        out_specs=[_rows(tm, D_MODEL), _rows(tm, 2 * D_FF), _rows(tm, D_MODEL), _full((8, D_MODEL))],
        out_shape=[jax.ShapeDtypeStruct((s, D_MODEL), BF16), jax.ShapeDtypeStruct((s, 2 * D_FF), BF16),
                   jax.ShapeDtypeStruct((s, D_MODEL), BF16), jax.ShapeDtypeStruct((8, D_MODEL), F32)],
        compiler_params=_params(("arbitrary",), VMEM_LIMIT_LARGE),
    )(dx2, o2, gate, up, x1, mod, g_norm2, w_down, w_gu)


def _ffn(x1, mod, g_norm2, w_gu, w_down, g_final, target, tm):
    s = x1.shape[0]
    chunk = D_FF // FFN_CHUNKS

    def body(x_ref, mod_ref, g_ref, wgu_ref, wd_ref, gf_ref, t_ref, h_ref, act_ref, do_ref, dgu_ref, dx1_ref,
             small_ref):
        @pl.when(pl.program_id(0) == 0)
        def _():
            small_ref[...] = jnp.zeros_like(small_ref)

        xf = x_ref[...]
        n = xf * _rsqrt_mean_sq(xf) * g_ref[...]
        h = (n * (1.0 + mod_ref[SC2:SC2 + 1, :]) + mod_ref[SH2:SH2 + 1, :]).astype(BF16)
        h_ref[...] = h
        gates, ups, o = [], [], None
        for j in range(FFN_CHUNKS):
            lo = j * chunk
            gate = _dot_nt(h, wgu_ref[lo:lo + chunk, :])
            up = _dot_nt(h, wgu_ref[D_FF + lo:D_FF + lo + chunk, :])
            sg = _sigmoid(gate)
            act = (gate * sg * up).astype(BF16)
            act_ref[:, lo:lo + chunk] = act
            gates.append((up * (sg * (1.0 + gate * (1.0 - sg)))).astype(BF16))
            ups.append((gate * sg).astype(BF16))
            part = _dot(act, wd_ref[lo:lo + chunk, :])
            o = part if o is None else o + part
        g2 = mod_ref[G2:G2 + 1, :]
        x2 = xf + g2 * o
        r = _rsqrt_mean_sq(x2)
        xn = x2 * r
        gf = gf_ref[...]
        err = xn * gf - t_ref[...]
        dy = err * (1.0 / D_MODEL)
        dxn = dy * gf
        dx2 = r * (dxn - xn * jnp.mean(dxn * xn, axis=-1, keepdims=True))
        small_ref[4:5, :] += _colsum(dy * xn)
        small_ref[5:6, :] += _colsum(err * err)
        small_ref[3:4, :] += _colsum(dx2 * o)
        do = (dx2 * g2).astype(BF16)
        do_ref[...] = do
        dh = None
        for j in range(FFN_CHUNKS):
            lo = j * chunk
            dact = _dot_nt(do, wd_ref[lo:lo + chunk, :])
            dgate = (dact * gates[j].astype(F32)).astype(BF16)
            dup = (dact * ups[j].astype(F32)).astype(BF16)
            dgu_ref[:, lo:lo + chunk] = dgate
            dgu_ref[:, D_FF + lo:D_FF + lo + chunk] = dup
            part = _dot(dgate, wgu_ref[lo:lo + chunk, :]) + _dot(dup, wgu_ref[D_FF + lo:D_FF + lo + chunk, :])
            dh = part if dh is None else dh + part
        dx1 = dx2 + _norm_mod_bwd(dh, xf, g_ref[...], mod_ref[SC2:SC2 + 1, :], small_ref)
        dx1_ref[...] = dx1.astype(BF16)

        @pl.when(pl.program_id(0) == pl.num_programs(0) - 1)
        def _():
            total = jnp.sum(small_ref[5:6, :], axis=-1, keepdims=True) * (0.5 / D_MODEL)
            small_ref[6:7, :] = jnp.broadcast_to(total, (1, D_MODEL))

    narrow = jax.ShapeDtypeStruct((s, D_MODEL), BF16)
    return pl.pallas_call(
        body, name="ffn", grid=(s // tm,),
        in_specs=[_rows(tm, D_MODEL), _full((8, D_MODEL)), _full((1, D_MODEL)), _resident((2 * D_FF, D_MODEL)),
                  _resident((D_FF, D_MODEL)), _full((1, D_MODEL)), _rows(tm, D_MODEL)],
        out_specs=[_rows(tm, D_MODEL), _rows(tm, D_FF), _rows(tm, D_MODEL), _rows(tm, 2 * D_FF), _rows(tm, D_MODEL),
                   _full((8, D_MODEL))],
        out_shape=[narrow, jax.ShapeDtypeStruct((s, D_FF), BF16), narrow, jax.ShapeDtypeStruct((s, 2 * D_FF), BF16),
                   narrow, jax.ShapeDtypeStruct((8, D_MODEL), F32)],
        compiler_params=_params(("arbitrary",), VMEM_LIMIT_LARGE),
    )(x1, mod, g_norm2, w_gu, w_down, g_final, target)


def _norm_mod_bwd(dh, xf, g, scale_row, small_ref):
    r = _rsqrt_mean_sq(xf)
    xn = xf * r
    small_ref[0:1, :] += _colsum(dh)
    small_ref[1:2, :] += _colsum(dh * (xn * g))
    dn = dh * (1.0 + scale_row)
    small_ref[2:3, :] += _colsum(dn * xn)
    dxn = dn * g
    return r * (dxn - xn * jnp.mean(dxn * xn, axis=-1, keepdims=True))


def _out_proj_bwd(after, dx1, o1, mod, w_out, tm):
    s = dx1.shape[0]

    def body(dx_ref, o_ref, mod_ref, w_ref, do_ref, dm_ref, small_ref):
        @pl.when(pl.program_id(0) == 0)
        def _():
            small_ref[...] = jnp.zeros_like(small_ref)

        dx = dx_ref[...].astype(F32)
        small_ref[0:1, :] += _colsum(dx * o_ref[...].astype(F32))
        do = (dx * mod_ref[G1:G1 + 1, :]).astype(BF16)
        do_ref[...] = do
        dm_ref[...] = _dot_nt(do, w_ref[...]).astype(BF16)

    return pl.pallas_call(
        _coming_behind(body), name="out_proj_bwd", grid=(s // tm,),
        in_specs=[ANY_SPEC, _rows(tm, D_MODEL), _rows(tm, D_MODEL), _full((8, D_MODEL)), _full((D_MODEL, D_MODEL))],
        out_specs=[_rows(tm, D_MODEL), _rows(tm, D_MODEL), _full((8, D_MODEL))],
        out_shape=[jax.ShapeDtypeStruct((s, D_MODEL), BF16), jax.ShapeDtypeStruct((s, D_MODEL), BF16),
                   jax.ShapeDtypeStruct((8, D_MODEL), F32)],
        compiler_params=_params(("arbitrary",)),
    )(after, dx1, o1, mod, w_out)


def _group_norm_bwd(dm, a, g):
    r = _rsqrt_mean_sq(a)
    an = a * r
    dan = dm * g
    return r * (dan - an * jnp.mean(dan * an, axis=-1, keepdims=True)), _colsum(dm * an)


def _mixer_bwd(after, q, kv, gb, gc, xc, bias, sinks, conv_w, g_attn, g_conv, attn, lse, dmerged):
    s = q.shape[0]
    nb = s // BLOCK

    per_step = min(MIXER_BLOCKS, nb)
    tile = per_step * BLOCK
    steps = nb // per_step

    def one_block(n, rows, before, nxt, sink_ref, q_ref, kv_ref, gb_ref, gc_ref, xc_ref, bias_ref, cw_ref, ga_ref,
                  gcv_ref, attn_ref, lse_ref, dm_ref, dproj_ref, dbias_ref, dsink_ref, small_ref):
        next_dy, next_dkv = nxt
        dm = dm_ref[rows, :].astype(F32)
        gbv, gcv_, xcv = gb_ref[rows, :].astype(F32), gc_ref[rows, :].astype(F32), xc_ref[rows, :].astype(F32)
        u, u1, u2 = _conv_taps(gcv_, xcv, before[0], before[1], n)
        cw = cw_ref[...]
        yv = cw[0:1, :] * u2 + cw[1:2, :] * u1 + cw[2:3, :] * u
        dcv, dg_conv = _group_norm_bwd(dm[:, 512:1024], gbv * yv, gcv_ref[...])
        small_ref[1:2, :] += dg_conv
        dproj_ref[rows, 768:1280] = (dcv * yv).astype(BF16)
        dy = dcv * gbv
        row = lax.broadcasted_iota(jnp.int32, dy.shape, 0)
        d1 = jnp.where(row == BLOCK - 1, next_dy[0:1, :], pltpu.roll(dy, BLOCK - 1, 0))
        d2 = jnp.where(row == BLOCK - 2, next_dy[0:1, :],
                       jnp.where(row == BLOCK - 1, next_dy[1:2, :], pltpu.roll(dy, BLOCK - 2, 0)))
        du = cw[2:3, :] * dy + cw[1:2, :] * d1 + cw[0:1, :] * d2
        dproj_ref[rows, 1280:1792] = (du * xcv).astype(BF16)
        dproj_ref[rows, 1792:2304] = (du * gcv_).astype(BF16)
        small_ref[2:3, :] += _colsum(dy * u2)
        small_ref[3:4, :] += _colsum(dy * u1)
        small_ref[4:5, :] += _colsum(dy * u)

        attn_v = attn_ref[rows, :]
        dout, dg_attn = _group_norm_bwd(dm[:, 0:512], attn_v, ga_ref[...])
        small_ref[0:1, :] += dg_attn
        ks, vs = _load_kv_window(kv_ref, n)
        lane = lax.broadcasted_iota(jnp.int32, (BLOCK, BLOCK), 1)
        low = lane < HEAD_DIM
        col = lax.broadcasted_iota(jnp.int32, (BLOCK, 2 * BLOCK), 1)
        no_prev = (col < BLOCK) & (n == 0)
        lse_all = lse_ref[rows, :]
        dsink = jnp.zeros((BLOCK, BLOCK), F32)
        dq_pairs = []
        dk_groups, dv_groups = [], []
        for kvh in range(2):
            ds_rows, pr_rows, q_rows, do_rows = [], [], [], []
            for p in (2 * kvh, 2 * kvh + 1):
                qp = q_ref[rows, 128 * p:128 * (p + 1)].astype(F32)
                do_p = dout[:, 128 * p:128 * (p + 1)]
                prod = do_p * attn_v[:, 128 * p:128 * (p + 1)]
                res = []
                for e in range(2):
                    h = 2 * p + e
                    half = low if e == 0 else ~low
                    qm = jnp.where(half, qp, 0.0).astype(BF16)
                    dom = jnp.where(half, do_p, 0.0).astype(BF16)
                    delta = jnp.sum(jnp.where(half, prod, 0.0), axis=-1, keepdims=True)
                    lse_h = jnp.sum(jnp.where(lane == h, lse_all, 0.0), axis=-1, keepdims=True)
                    sw = 0 if kvh == e else 1
                    sc = _dot_nt(qm, ks[sw]) * SCALE + bias_ref[h]
                    sc = jnp.where(no_prev, NEG_INF, sc)
                    pr = jnp.exp(sc - lse_h)
                    dp = _dot_nt(dom, vs[sw])
                    ds = pr * (dp - delta)
                    dbias_ref[h] += ds
                    dsink = dsink + jnp.where(lane == h, -jnp.exp(sink_ref[h] - lse_h) * delta, 0.0)
                    dsb = ds.astype(BF16)
                    res.append(_dot(dsb, ks[sw]) * SCALE)
                    ds_rows.append(dsb)
                    pr_rows.append(pr.astype(BF16))
                    q_rows.append(qm)
                    do_rows.append(dom)
                dq_pairs.append(jnp.where(low, res[0], res[1]))
            dk_g = _dot_tn(jnp.concatenate(ds_rows, axis=0), jnp.concatenate(q_rows, axis=0)) * SCALE
            dv_g = _dot_tn(jnp.concatenate(pr_rows, axis=0), jnp.concatenate(do_rows, axis=0))
            dk_groups.append(dk_g + pltpu.roll(dk_g, 64, 1))
            dv_groups.append(dv_g + pltpu.roll(dv_g, 64, 1))
        dproj_ref[rows, 0:512] = jnp.concatenate(dq_pairs, axis=1).astype(BF16)
        dsink_ref[...] += dsink
        low_kv = lax.broadcasted_iota(jnp.int32, (2 * BLOCK, BLOCK), 1) < HEAD_DIM
        dkv_win = jnp.concatenate([jnp.where(low_kv, dk_groups[0], dk_groups[1]),
                                   jnp.where(low_kv, dv_groups[0], dv_groups[1])], axis=1)
        dproj_ref[rows, 512:768] = (dkv_win[BLOCK:2 * BLOCK, :] + next_dkv).astype(BF16)
        return dy[0:8, :], dkv_win[0:BLOCK, :]

    def body(sink_ref, q_ref, kv_ref, gb_ref, gc_ref, xc_ref, gcp_ref, xcp_ref, *rest):
        refs, dy_ref, dkv_ref = rest[:-2], rest[-2], rest[-1]
        dbias_ref, dsink_ref, small_ref = refs[8], refs[9], refs[10]
        step = pl.program_id(0)

        @pl.when(step == 0)
        def _():
            dbias_ref[...] = jnp.zeros_like(dbias_ref)
            dsink_ref[...] = jnp.zeros_like(dsink_ref)
            small_ref[...] = jnp.zeros_like(small_ref)
            dy_ref[...] = jnp.zeros_like(dy_ref)
            dkv_ref[...] = jnp.zeros_like(dkv_ref)

        nxt = (dy_ref[...], dkv_ref[...])
        for sub in reversed(range(per_step)):
            rows = slice(sub * BLOCK, (sub + 1) * BLOCK)
            ahead = slice(sub * BLOCK - PREV_ROWS, sub * BLOCK)
            before = (gcp_ref[...], xcp_ref[...]) if sub == 0 else (gc_ref[ahead, :], xc_ref[ahead, :])
            nxt = one_block((steps - 1 - step) * per_step + sub, rows, before, nxt,
                            sink_ref, q_ref, kv_ref, gb_ref, gc_ref, xc_ref, *refs)
        dy_ref[...], dkv_ref[...] = nxt

        @pl.when(step == steps - 1)
        def _():
            small_ref[5:6, :] = jnp.concatenate([_colsum(dsink_ref[...]), jnp.zeros((1, 512 - BLOCK), F32)], axis=1)

    blk = lambda w: pl.BlockSpec((tile, w), lambda t: (steps - 1 - t, 0))
    prev8 = pl.BlockSpec((PREV_ROWS, 512),
                         lambda t: (jnp.maximum((steps - 1 - t) * (tile // PREV_ROWS) - 1, 0), 0))
    bf = lambda w: jax.ShapeDtypeStruct((s, w), BF16)
    return pl.pallas_call(
        _coming_behind(body), name="mixer_bwd", grid=(steps,),
        in_specs=[ANY_SPEC, pl.BlockSpec(memory_space=pltpu.SMEM), blk(512), _full((s, 256)), blk(512), blk(512), blk(512),
                  prev8, prev8, _full((N_Q_HEADS, BLOCK, 2 * BLOCK)), _full((3, 512)), _full((1, 512)),
                  _full((1, 512)), blk(512), blk(128), blk(1024)],
        out_specs=[blk(IN_PROJ_WIDTH), _full((N_Q_HEADS, BLOCK, 2 * BLOCK)), _full((BLOCK, BLOCK)), _full((8, 512))],
        out_shape=[bf(IN_PROJ_WIDTH), jax.ShapeDtypeStruct((N_Q_HEADS, BLOCK, 2 * BLOCK), F32),
                   jax.ShapeDtypeStruct((BLOCK, BLOCK), F32), jax.ShapeDtypeStruct((8, 512), F32)],
        scratch_shapes=[pltpu.VMEM((8, 512), F32), pltpu.VMEM((BLOCK, 2 * KV_WIDTH), F32)],
        compiler_params=_params(("arbitrary",), VMEM_LIMIT_LARGE),
    )(after, sinks, q, kv, gb, gc, xc, gc, xc, bias, conv_w, g_attn, g_conv, attn, lse, dmerged)


def _in_proj_bwd(after, dproj, x, dx1, mod, g_norm1, w_in, tm):
    s = x.shape[0]

    def body(dproj_ref, x_ref, dx1_ref, mod_ref, g_ref, w_ref, dx_ref, small_ref):
        @pl.when(pl.program_id(0) == 0)
        def _():
            small_ref[...] = jnp.zeros_like(small_ref)

        dh = _dot(dproj_ref[...], w_ref[...])
        dx_ref[...] = dx1_ref[...].astype(F32) + _norm_mod_bwd(dh, x_ref[...], g_ref[...], mod_ref[SC1:SC1 + 1, :],
                                                               small_ref)

    return pl.pallas_call(
        _coming_behind(body), name="in_proj_bwd", grid=(s // tm,),
        in_specs=[ANY_SPEC, _rows(tm, IN_PROJ_WIDTH), _rows(tm, D_MODEL), _rows(tm, D_MODEL), _full((8, D_MODEL)),
                  _full((1, D_MODEL)), _full((IN_PROJ_WIDTH, D_MODEL))],
        out_specs=[_rows(tm, D_MODEL), _full((8, D_MODEL))],
        out_shape=[jax.ShapeDtypeStruct((s, D_MODEL), F32), jax.ShapeDtypeStruct((8, D_MODEL), F32)],
        compiler_params=_params(("arbitrary",), VMEM_LIMIT_LARGE),
    )(after, dproj, x, dx1, mod, g_norm1, w_in)


def _weight_grad(a, b, tk, ts, name, after=None):
    s, k = a.shape
    n = b.shape[1]
    nt = s // ts
    extra = [] if after is None else [after]

    def body(a_ref, b_ref, *rest):
        o_ref, acc_ref = rest[-2:]
        t = pl.program_id(1)
        part = _dot_tn(a_ref[...], b_ref[...])

        @pl.when(t == 0)
        def _():
            acc_ref[...] = part

        @pl.when(t > 0)
        def _():
            acc_ref[...] += part

        @pl.when(t == nt - 1)
        def _():
            o_ref[...] = acc_ref[...].astype(BF16)

    return pl.pallas_call(
        body, name=name, grid=(k // tk, nt),
        in_specs=[pl.BlockSpec((ts, tk), lambda i, t: (t, i)), pl.BlockSpec((ts, n), lambda i, t: (t, 0))]
        + [pl.BlockSpec(memory_space=pl.ANY)] * len(extra),
        out_specs=pl.BlockSpec((tk, n), lambda i, t: (i, 0)),
        out_shape=jax.ShapeDtypeStruct((k, n), BF16),
        scratch_shapes=[pltpu.VMEM((tk, n), F32)],
        compiler_params=_params(("arbitrary", "arbitrary"), VMEM_LIMIT_LARGE),
    )(a, b, *extra)


def _rel_bias_grad(dbias, bucket):
    def body(db_ref, bk_ref, o_ref, rows_ref):
        bk = bk_ref[...]
        for b in range(N_BUCKETS):
            sel = (bk == b).astype(F32)
            for h in range(N_Q_HEADS):
                rows_ref[N_BUCKETS * h + b:N_BUCKETS * h + b + 1, :] = _colsum(db_ref[h] * sel)
        head = lax.broadcasted_iota(jnp.int32, (N_BUCKETS, N_Q_HEADS), 1)
        out = jnp.zeros((N_BUCKETS, N_Q_HEADS), F32)
        for h in range(N_Q_HEADS):
            per_bucket = jnp.sum(rows_ref[N_BUCKETS * h:N_BUCKETS * (h + 1), :], axis=-1, keepdims=True)
            out = out + jnp.where(head == h, per_bucket, 0.0)
        o_ref[...] = out

    return pl.pallas_call(
        body, name="rel_bias_grad",
        out_shape=jax.ShapeDtypeStruct((N_BUCKETS, N_Q_HEADS), F32),
        scratch_shapes=[pltpu.VMEM((N_BUCKETS * N_Q_HEADS, 2 * BLOCK), F32)],
    )(dbias, bucket)


def _lanes_from(x, start, width):
    n = x.shape[1]
    return pltpu.roll(x, (n - start) % n, 1)[:, 0:width]


def _w_ada_grad(me, cond_all, packed_all, cols):
    def body(me_ref, c_ref, p_ref, o_ref):
        dmod = jnp.concatenate([p_ref[k][:, OFF_DMOD:OFF_DMOD + N_MOD * D_MODEL] for k in range(N_DEV)], axis=0)
        mine = _lanes_from(dmod, me_ref[0] * cols, cols)
        pad = lambda a: jnp.concatenate([a, jnp.zeros((128 - N_DEV, a.shape[1]), F32)], axis=0)
        o_ref[...] = _dot_tn(pad(c_ref[...]), pad(mine))

    vmem = pl.BlockSpec(memory_space=pltpu.VMEM)
    return pl.pallas_call(body, name="w_ada_grad",
                          in_specs=[pl.BlockSpec(memory_space=pltpu.SMEM), vmem, vmem],
                          out_shape=jax.ShapeDtypeStruct((cond_all.shape[1], cols), F32))(me, cond_all, packed_all)


SMALL_PARAMS = (("rel_bias", None), ("b_ada", (OFF_DMOD, N_MOD * D_MODEL)), ("g_norm1", (OFF_GN1, D_MODEL)),
                ("sinks", (OFF_SINK, N_Q_HEADS)), ("conv_w", None), ("g_attn_out", (OFF_GATT, ATTN_WIDTH)),
                ("g_conv_out", (OFF_GCV, CONV_WIDTH)), ("g_norm2", (OFF_GN2, D_MODEL)),
                ("g_final", (OFF_GFIN, D_MODEL)))


def _small_update(me, packed_all, rel_all, state, after):
    n_p = len(SMALL_PARAMS)
    flat = [a for triple in state for a in triple]
    conv_cols = state[4][0].shape[1]

    def body(me_ref, p_ref, r_ref, *refs):
        ins = refs[:3 * n_p]
        loss_ref, outs = refs[3 * n_p + len(after)], refs[3 * n_p + len(after) + 1:]
        small, rel = p_ref[0], r_ref[0]
        for k in range(1, N_DEV):
            small = small + p_ref[k]
            rel = rel + r_ref[k]
        loss_ref[...] = small[:, OFF_LOSS:OFF_LOSS + 128]
        taps = jnp.concatenate([small[:, OFF_CONVW + CONV_WIDTH * j:OFF_CONVW + CONV_WIDTH * (j + 1)]
                                for j in range(3)] + [jnp.zeros((5, CONV_WIDTH), F32)], axis=0)
        conv_g = _lanes_from(taps, me_ref[0] * conv_cols, conv_cols)[0:3, :]
        for i, (name, lanes) in enumerate(SMALL_PARAMS):
            g = rel if name == "rel_bias" else conv_g if name == "conv_w" else small[:, lanes[0]:lanes[0] + lanes[1]]
            w_ref, m_ref, v_ref = ins[3 * i:3 * i + 3]
            outs[4 * i][...] = g
            outs[4 * i + 1][...], outs[4 * i + 2][...], outs[4 * i + 3][...] = _adam_math(
                w_ref[...], g, m_ref[...], v_ref[...])

    vmem = pl.BlockSpec(memory_space=pltpu.VMEM)
    out_shape = [jax.ShapeDtypeStruct((1, 128), F32)]
    for w, _, _ in state:
        out_shape += [jax.ShapeDtypeStruct(w.shape, F32)] * 4
    outs = pl.pallas_call(
        body, name="small_update",
        in_specs=[pl.BlockSpec(memory_space=pltpu.SMEM), vmem, vmem] + [vmem] * len(flat)
        + [pl.BlockSpec(memory_space=pl.ANY)] * len(after),
        out_shape=out_shape,
    )(me, packed_all, rel_all, *flat, *after)
    return outs[0], [tuple(outs[1 + 4 * i:5 + 4 * i]) for i in range(n_p)]


def _adam_math(w, g, m, v):
    m = ADAM_B1 * m + (1.0 - ADAM_B1) * g
    v = ADAM_B2 * v + (1.0 - ADAM_B2) * (g * g)
    m_hat = m / (1.0 - ADAM_B1 ** ADAM_STEP)
    v_hat = v / (1.0 - ADAM_B2 ** ADAM_STEP)
    delta = -ADAM_LR * (m_hat / (jnp.sqrt(v_hat) + ADAM_EPS) + ADAM_WD * w)
    return delta, m, v


def _adamw_parts(w, m, v, local, land, me, tr, name):
    r, c = w.shape

    def body(me_ref, w_ref, m_ref, v_ref, own_ref, land_ref, g_ref, d_ref, mo_ref, vo_ref):
        g = own_ref[0].astype(F32)
        for k in range(N_DEV - 1):
            g = g + land_ref[k].astype(F32)
        g_ref[...] = g
        d_ref[...], mo_ref[...], vo_ref[...] = _adam_math(w_ref[...], g, m_ref[...], v_ref[...])

    tile = pl.BlockSpec((tr, c), lambda i, me_ref: (i, 0))
    return pl.pallas_call(
        body, name=name,
        grid_spec=pltpu.PrefetchScalarGridSpec(
            num_scalar_prefetch=1, grid=(r // tr,),
            in_specs=[tile, tile, tile, pl.BlockSpec((1, tr, c), lambda i, me_ref: (me_ref[0], i, 0)),
                      pl.BlockSpec((N_DEV - 1, tr, c), lambda i, me_ref: (0, i, 0))],
            out_specs=[tile] * 4),
        out_shape=[jax.ShapeDtypeStruct((r, c), F32)] * 4,
        compiler_params=_params(("arbitrary",)),
    )(me, w, m, v, local, land)


def _adamw(w, m, v, g, tr, name):
    r, c = w.shape

    def body(w_ref, m_ref, v_ref, g_ref, d_ref, mo_ref, vo_ref):
        d_ref[...], mo_ref[...], vo_ref[...] = _adam_math(w_ref[...], g_ref[...], m_ref[...], v_ref[...])

    tile = pl.BlockSpec((tr, c), lambda i: (i, 0))
    return pl.pallas_call(
        body, name=name, grid=(r // tr,),
        in_specs=[tile] * 4, out_specs=[tile] * 3,
        out_shape=[jax.ShapeDtypeStruct((r, c), F32)] * 3,
        compiler_params=_params(("arbitrary",)),
    )(w, m, v, g)


def _behind(a, token):
    return a + token[0:a.shape[0], 0:1]


def _local_step(x, target, mod, w_in_t, weights_out_gu, weights_down, rel_bias, g_norm1, sinks, conv_w, g_attn,
                g_conv, g_norm2, g_final, exchange):
    s = x.shape[0]
    tm = min(512, s)
    tm_small = min(256, s)
    bucket = _bucket_table()
    bias = _bias_table(rel_bias, bucket)

    h, q, kv, gb, gc, xc = _in_proj(x, mod, g_norm1, w_in_t, tm)
    attn, merged, lse = _mixer_fwd(q, kv, gb, gc, xc, bias, sinks, conv_w, g_attn, g_conv)
    w_out, w_gu_t = weights_out_gu(merged)
    o1, x1 = _out_proj(merged, x, mod, w_out, tm)
    w_down = weights_down(x1)
    h2, act, do2, dgu, dx1, sm_2 = _ffn(x1, mod, g_norm2, w_gu_t, w_down, g_final, target, tm_small)
    ts = min(WEIGHT_GRAD_ROWS, s)
    tok_down = exchange("w_down", _weight_grad(act, do2, D_FF // 2, ts, "w_down_grad"))
    tok_gu = exchange("w_gu", _weight_grad(dgu, h2, D_FF // 2, ts, "w_gu_grad", after=tok_down))
    do1, dmerged, sm_g1 = _out_proj_bwd(tok_gu, dx1, o1, mod, w_out, tm)
    tok_out = exchange("w_out", _weight_grad(merged, do1, D_MODEL, ts, "w_out_grad"))
    dproj, dbias, dsink, sm_mix = _mixer_bwd(
        tok_out, q, kv, gb, gc, xc, bias, sinks, conv_w, g_attn, g_conv, attn, lse, dmerged)
    tok_in = exchange("w_in", _weight_grad(dproj, h, IN_PROJ_WIDTH // 2, ts, "w_in_grad"))
    dx, sm_1 = _in_proj_bwd(tok_in, dproj, x, dx1, mod, g_norm1, w_in_t, tm)
    d_rel = _rel_bias_grad(dbias, bucket)

    packed = jnp.concatenate([
        sm_1[0], sm_1[1], sm_g1[0], sm_2[0], sm_2[1], sm_2[3],
        sm_1[2],
        sm_mix[5, 0:128],
        sm_mix[0], sm_mix[1],
        sm_2[2],
        sm_2[4],
        sm_mix[2], sm_mix[3], sm_mix[4],
        sm_2[6, 0:128],
    ])[None, :]
    return dx, packed, d_rel


def kernel(x, c, rel_bias, w_ada, b_ada, g_norm1, w_in, sinks, conv_w, g_attn_out, g_conv_out, w_out, g_norm2, w_gu, w_down, g_final, loss_target, m_rel_bias, m_w_ada, m_b_ada, m_g_norm1, m_w_in, m_sinks, m_conv_w, m_g_attn_out, m_g_conv_out, m_w_out, m_g_norm2, m_w_gu, m_w_down, m_g_final, v_rel_bias, v_w_ada, v_b_ada, v_g_norm1, v_w_in, v_sinks, v_conv_w, v_g_attn_out, v_g_conv_out, v_w_out, v_g_norm2, v_w_gu, v_w_down, v_g_final):
    me = _linear(_mesh_position())
    me_arr = jnp.reshape(me, (1,)).astype(jnp.int32)
    ada_cols = w_ada.shape[2]
    tm = min(512, x.shape[1])

    cond = _silu_rows(c)
    cond_all, conv_w_all = _all_gather_small([cond, conv_w[0]], "gather_cond")
    cond_all = cond_all[:, 0, :]
    conv_cols = conv_w.shape[2]
    conv_w_full = conv_w_all.transpose(1, 0, 2).reshape(3, CONV_WIDTH)
    b_cols = lax.dynamic_slice_in_dim(b_ada, me * ada_cols, ada_cols, axis=1)
    mod_cols = _mod_columns(cond_all, w_ada[0], b_cols)
    mod_all = _all_gather_small([mod_cols], "gather_mod")[0]
    mod = lax.dynamic_index_in_dim(mod_all, me, axis=1, keepdims=False).reshape(N_MOD, D_MODEL)
    mod = jnp.concatenate([mod, jnp.zeros((2, D_MODEL), F32)], axis=0)

    w_in_t = _all_gather([w_in[0].T], "gather_w_in", to_bf16=True, big=True)[0].reshape(IN_PROJ_WIDTH, D_MODEL)
    gather_sems, staged, gather_token = _gather_start(
        _stage_blocks([w_out[0], w_gu[0].T, w_down[0]], w_in_t, "stage_weights"), "gather_start_weights")
    mod = _behind(mod, gather_token)

    def weights_out_gu(after):
        got = _gather_pass_on(_gather_wait(gather_sems[0:4], staged[0:2], [after], "gather_wait_out_gu"),
                              "gather_pass_on_out_gu")
        return got[0].reshape(D_MODEL, D_MODEL), got[1].reshape(2 * D_FF, D_MODEL)

    def weights_down(after):
        got = _gather_pass_on(_gather_wait(gather_sems[4:6], staged[2:3], [after], "gather_wait_down"),
                              "gather_pass_on_down")
        return got[0].reshape(D_FF, D_MODEL)

    started = {}

    def exchange(name, dw):
        st = _exchange_start(dw.reshape(N_DEV, dw.shape[0] // N_DEV, dw.shape[1]), "exchange_start_" + name)
        started[name] = st
        return st[4]

    dx, packed, d_rel = _local_step(
        x[0], loss_target[0], mod, w_in_t, weights_out_gu, weights_down, rel_bias, g_norm1, sinks[0], conv_w_full,
        g_attn_out, g_conv_out, g_norm2, g_final[None, :], exchange)

    def zone(a):
        return lax.dynamic_update_slice(jnp.zeros((N_DEV,) + a.shape, F32), a[None], (me,) + (0,) * a.ndim)

    shared = _share_start([packed, d_rel], [zone(packed), zone(d_rel)], "share_small_start")

    def finish(name, after, w, m, v, tr):
        src, land = _exchange_wait(started[name], after, "exchange_wait_" + name)
        return _adamw_parts(w, m, v, src, land, me_arr, tr, "adamw_" + name)

    g_down, d_down, nm_down, nv_down = finish("w_down", [shared[2][0]], w_down[0], m_w_down[0], v_w_down[0], 176)
    g_gu, d_gu, nm_gu, nv_gu = finish("w_gu", [nv_down], w_gu[0].T, m_w_gu[0].T, v_w_gu[0].T, 352)
    g_out, d_out, nm_out, nv_out = finish("w_out", [nv_gu], w_out[0], m_w_out[0], v_w_out[0], 128)

    packed_all, rel_all = _share_wait(shared, [nv_out], "share_small_wait")
    g_ada = _w_ada_grad(me_arr, cond_all, packed_all, ada_cols)
    d_ada, nm_ada, nv_ada = _adamw(w_ada[0], m_w_ada[0], v_w_ada[0], g_ada, 256, "adamw_w_ada")
    as_rows = {"conv_w": lambda a: a[0], "g_final": lambda a: a[None, :]}
    small_state = {
        "rel_bias": (rel_bias, m_rel_bias, v_rel_bias), "b_ada": (b_ada, m_b_ada, v_b_ada),
        "g_norm1": (g_norm1, m_g_norm1, v_g_norm1), "sinks": (sinks, m_sinks, v_sinks),
        "conv_w": (conv_w, m_conv_w, v_conv_w), "g_attn_out": (g_attn_out, m_g_attn_out, v_g_attn_out),
        "g_conv_out": (g_conv_out, m_g_conv_out, v_g_conv_out), "g_norm2": (g_norm2, m_g_norm2, v_g_norm2),
        "g_final": (g_final, m_g_final, v_g_final),
    }
    state = [tuple(as_rows.get(name, lambda a: a)(a) for a in small_state[name]) for name, _ in SMALL_PARAMS]
    loss_row, small_out = _small_update(me_arr, packed_all, rel_all, state, [])
    loss = loss_row[0, 0]
    small_res = {name: tuple(a.reshape(small_state[name][0].shape) for a in res)
                 for (name, _), res in zip(SMALL_PARAMS, small_out)}

    g_in, d_in, nm_in, nv_in = finish("w_in", [loss_row, nv_ada], w_in[0].T, m_w_in[0].T, v_w_in[0].T, 144)

    big = {
        "w_ada": (g_ada[None], d_ada[None], nm_ada[None], nv_ada[None]),
        "w_in": (g_in.T[None], d_in.T[None], nm_in.T[None], nv_in.T[None]),
        "w_out": (g_out[None], d_out[None], nm_out[None], nv_out[None]),
        "w_gu": (g_gu.T[None], d_gu.T[None], nm_gu.T[None], nv_gu.T[None]),
        "w_down": (g_down[None], d_down[None], nm_down[None], nv_down[None]),
    }
    order = ["rel_bias", "w_ada", "b_ada", "g_norm1", "w_in", "sinks", "conv_w", "g_attn_out", "g_conv_out", "w_out",
             "g_norm2", "w_gu", "w_down", "g_final"]
    results = [big[k] if k in big else small_res[k] for k in order]
    return (loss, dx[None], *[r[0] for r in results], *[r[1] for r in results], *[r[2] for r in results],
            *[r[3] for r in results])
```

```python
import functools
import math

import jax
import jax.numpy as jnp
from jax import lax
from jax.experimental import pallas as pl
from jax.experimental.pallas import tpu as pltpu

F32 = jnp.float32
BF16 = jnp.bfloat16

D_MODEL = 1024
HEAD_DIM = 64
N_Q_HEADS = 8
ATTN_WIDTH = 512
KV_WIDTH = 128
CONV_WIDTH = 512
IN_PROJ_WIDTH = 2304
D_FF = 2816
N_MOD = 6
N_BUCKETS = 32
MAX_DISTANCE = 128
BLOCK = 128
EPS = 1e-6
NEG_INF = -1e30
SCALE = HEAD_DIM ** -0.5
N_DEV = 8

ADAM_LR = 0.001
ADAM_B1 = 0.9
ADAM_B2 = 0.999
ADAM_EPS = 1e-08
ADAM_WD = 0.01
ADAM_STEP = 10

SH1, SC1, G1, SH2, SC2, G2 = range(6)

VMEM_LIMIT_LARGE = 60 * 1024 * 1024
WEIGHT_GRAD_ROWS = 2048
FFN_CHUNKS = 2
PREV_ROWS = 16
MIXER_BLOCKS = 4
MESH_ID = pl.DeviceIdType.MESH

OFF_DMOD = 0
OFF_GN1 = OFF_DMOD + N_MOD * D_MODEL
OFF_SINK = OFF_GN1 + D_MODEL
OFF_GATT = OFF_SINK + 128
OFF_GCV = OFF_GATT + ATTN_WIDTH
OFF_GN2 = OFF_GCV + CONV_WIDTH
OFF_GFIN = OFF_GN2 + D_MODEL
OFF_CONVW = OFF_GFIN + D_MODEL
OFF_LOSS = OFF_CONVW + 3 * CONV_WIDTH
PACKED = OFF_LOSS + 128


def _params(sem=None, vmem=None):
    return pltpu.CompilerParams(dimension_semantics=sem, vmem_limit_bytes=vmem)


def _coming_behind(body):
    def skipping(after_ref, *refs):
        body(*refs)

    return skipping


ANY_SPEC = pl.BlockSpec(memory_space=pl.ANY)


def _full(shape):
    nd = len(shape)
    return pl.BlockSpec(shape, lambda *_: (0,) * nd)


def _rows(tm, width):
    return pl.BlockSpec((tm, width), lambda i, *_: (i, 0))


def _sigmoid(x):
    return 1.0 / (1.0 + jnp.exp(-x))


def _rsqrt_mean_sq(x):
    return lax.rsqrt(jnp.mean(x * x, axis=-1, keepdims=True) + EPS)


def _colsum(x):
    return jnp.sum(x, axis=0, keepdims=True)


def _dot(a, b):
    return jnp.dot(a, b, preferred_element_type=F32)


def _dot_nt(a, b):
    return lax.dot_general(a, b, (((1,), (1,)), ((), ())), preferred_element_type=F32)


def _dot_tn(a, b):
    return lax.dot_general(a, b, (((0,), (0,)), ((), ())), preferred_element_type=F32)


def _mesh_position():
    return lax.axis_index("x"), lax.axis_index("y"), lax.axis_index("c")


def _linear(p):
    return 4 * p[0] + 2 * p[1] + p[2]


def _all_gather(arrs, name, to_bf16, big):
    n = len(arrs)
    out_dtype = BF16 if to_bf16 else F32

    def body(*refs):
        in_refs, out_refs = refs[:n], refs[n:2 * n]
        rest = refs[2 * n:]
        if to_bf16:
            stage, rest = rest[:n], rest[n:]
            for a in range(n):
                stage[a][...] = in_refs[a][...].astype(BF16)
            srcs = stage
        else:
            srcs = in_refs
        send_sems, recv_sems, local_sems = rest
        x, y, c = _mesh_position()
        me, sibling = (x, y, c), (x, y, 1 - c)
        chips = [(1 - x, y), (x, 1 - y), (1 - x, 1 - y)]

        def slot(a, p):
            return out_refs[a].at[_linear(p)]

        def copy(k, a, block, to, src=None):
            return pltpu.make_async_remote_copy(
                src_ref=slot(a, block) if src is None else src,
                dst_ref=slot(a, block),
                send_sem=send_sems.at[k * n + a],
                recv_sem=recv_sems.at[k * n + a],
                device_id=to,
                device_id_type=MESH_ID,
            )

        mine = [pltpu.make_async_copy(srcs[a], slot(a, me), local_sems.at[a]) for a in range(n)]
        for cp in mine:
            cp.start()
        first = [copy(0, a, me, sibling, src=srcs[a]) for a in range(n)]
        for j, chip in enumerate(chips):
            first += [copy(1 + j, a, me, (*chip, c), src=srcs[a]) for a in range(n)]
        for cp in first:
            cp.start()
        passed = []
        for j, chip in enumerate(chips):
            for a in range(n):
                copy(1 + j, a, (*chip, c), me).wait_recv()
                fwd = copy(4 + j, a, (*chip, c), sibling)
                fwd.start()
                passed.append(fwd)
        for a in range(n):
            copy(0, a, sibling, me).wait_recv()
        for j, chip in enumerate(chips):
            for a in range(n):
                copy(4 + j, a, (*chip, 1 - c), me).wait_recv()
        for cp in first + passed:
            cp.wait_send()
        for cp in mine:
            cp.wait()

    vmem = pl.BlockSpec(memory_space=pltpu.VMEM)
    out_space = pl.BlockSpec(memory_space=pl.ANY) if big else vmem
    scratch = [pltpu.VMEM(a.shape, BF16) for a in arrs] if to_bf16 else []
    scratch += [pltpu.SemaphoreType.DMA((7 * n,)), pltpu.SemaphoreType.DMA((7 * n,)),
                pltpu.SemaphoreType.DMA((n,))]
    outs = pl.pallas_call(
        body, name=name,
        out_shape=[jax.ShapeDtypeStruct((N_DEV,) + a.shape, out_dtype) for a in arrs],
        in_specs=[vmem] * n, out_specs=[out_space] * n,
        scratch_shapes=scratch,
        compiler_params=_params(vmem=VMEM_LIMIT_LARGE if big else None),
    )(*arrs)
    return list(outs)


def _peer(k):
    x, y, c = _mesh_position()
    return (1 - x if k & 4 else x, 1 - y if k & 2 else y, 1 - c if k & 1 else c)


def _all_gather_small(arrs, name):
    n = len(arrs)

    def body(*refs):
        in_refs, out_refs = refs[:n], refs[n:2 * n]
        send_sems, recv_sems, local_sems = refs[2 * n:]
        me = _linear(_mesh_position())
        mine = [pltpu.make_async_copy(in_refs[a], out_refs[a].at[me], local_sems.at[a]) for a in range(n)]
        for cp in mine:
            cp.start()
        sends = []
        for k in range(1, N_DEV):
            for a in range(n):
                sends.append(pltpu.make_async_remote_copy(
                    src_ref=in_refs[a], dst_ref=out_refs[a].at[me],
                    send_sem=send_sems.at[(k - 1) * n + a], recv_sem=recv_sems.at[(k - 1) * n + a],
                    device_id=_peer(k), device_id_type=MESH_ID))
                sends[-1].start()
        for k in range(1, N_DEV):
            for a in range(n):
                pltpu.make_async_remote_copy(
                    src_ref=in_refs[a], dst_ref=out_refs[a].at[_linear(_peer(k))],
                    send_sem=send_sems.at[(k - 1) * n + a], recv_sem=recv_sems.at[(k - 1) * n + a],
                    device_id=_peer(k), device_id_type=MESH_ID).wait_recv()
        for cp in sends:
            cp.wait_send()
        for cp in mine:
            cp.wait()

    vmem = pl.BlockSpec(memory_space=pltpu.VMEM)
    return list(pl.pallas_call(
        body, name=name,
        out_shape=[jax.ShapeDtypeStruct((N_DEV,) + a.shape, F32) for a in arrs],
        in_specs=[vmem] * n, out_specs=[vmem] * n,
        scratch_shapes=[pltpu.SemaphoreType.DMA((7 * n,)), pltpu.SemaphoreType.DMA((7 * n,)),
                        pltpu.SemaphoreType.DMA((n,))],
    )(*arrs))


HBM_SPEC = pl.BlockSpec(memory_space=pltpu.HBM)
SEM_SPEC = pl.BlockSpec(memory_space=pltpu.SEMAPHORE)
DATAFLOW = pltpu.SideEffectType.DATAFLOW_SIDE_EFFECTING


def _exchange_start(src, name):
    r, c = src.shape[1:]

    def body(src_ref, land_ref, send_sems, recv_sems, src_thru, land_thru, token):
        for k in range(1, N_DEV):
            peer = _peer(k)
            pltpu.make_async_remote_copy(
                src_ref=src_ref.at[_linear(peer)], dst_ref=land_ref.at[k - 1],
                send_sem=send_sems.at[k - 1], recv_sem=recv_sems.at[k - 1],
                device_id=peer, device_id_type=MESH_ID).start()
        token[...] = jnp.zeros_like(token)

    land = lax.empty((N_DEV - 1, r, c), src.dtype)
    return pl.pallas_call(
        body, name=name,
        out_shape=(pltpu.SemaphoreType.DMA((N_DEV - 1,)), pltpu.SemaphoreType.DMA((N_DEV - 1,)),
                   pltpu.HBM(src.shape, src.dtype), pltpu.HBM(land.shape, land.dtype),
                   jax.ShapeDtypeStruct((8, 128), F32)),
        in_specs=(HBM_SPEC, HBM_SPEC),
        out_specs=(SEM_SPEC, SEM_SPEC, HBM_SPEC, HBM_SPEC, pl.BlockSpec(memory_space=pltpu.VMEM)),
        input_output_aliases={0: 2, 1: 3},
        compiler_params=pltpu.CompilerParams(has_side_effects=DATAFLOW),
    )(pltpu.with_memory_space_constraint(src, pltpu.HBM), pltpu.with_memory_space_constraint(land, pltpu.HBM))


def _exchange_wait(started, after, name):
    send_sems, recv_sems, src_thru, land_thru, _ = started

    def body(src_ref, land_ref, send_sems, recv_sems, *rest):
        for k in range(1, N_DEV):
            cp = pltpu.make_async_remote_copy(
                src_ref=src_ref.at[0], dst_ref=land_ref.at[k - 1],
                send_sem=send_sems.at[k - 1], recv_sem=recv_sems.at[k - 1],
                device_id=_peer(k), device_id_type=MESH_ID)
            cp.wait_send()
            cp.wait_recv()

    return pl.pallas_call(
        body, name=name,
        out_shape=(pltpu.HBM(src_thru.shape, src_thru.dtype), pltpu.HBM(land_thru.shape, land_thru.dtype)),
        in_specs=(HBM_SPEC, HBM_SPEC, SEM_SPEC, SEM_SPEC) + (pl.BlockSpec(memory_space=pl.ANY),) * len(after),
        out_specs=(HBM_SPEC, HBM_SPEC), input_output_aliases={0: 0, 1: 1},
        compiler_params=pltpu.CompilerParams(has_side_effects=DATAFLOW),
    )(src_thru, land_thru, send_sems, recv_sems, *after)


def _share_start(arrs, zones, name):
    n = len(arrs)

    def body(*refs):
        src_refs, zone_refs, sems = refs[:n], refs[n:2 * n], refs[2 * n:4 * n]
        me = _linear(_mesh_position())
        for a in range(n):
            for k in range(1, N_DEV):
                pltpu.make_async_remote_copy(
                    src_ref=src_refs[a], dst_ref=zone_refs[a].at[me],
                    send_sem=sems[2 * a].at[k - 1], recv_sem=sems[2 * a + 1].at[k - 1],
                    device_id=_peer(k), device_id_type=MESH_ID).start()

    outs = pl.pallas_call(
        body, name=name,
        out_shape=tuple(pltpu.SemaphoreType.DMA((N_DEV - 1,)) for _ in range(2 * n))
        + tuple(pltpu.HBM(a.shape, a.dtype) for a in arrs) + tuple(pltpu.HBM(z.shape, z.dtype) for z in zones),
        in_specs=(HBM_SPEC,) * (2 * n),
        out_specs=(SEM_SPEC,) * (2 * n) + (HBM_SPEC,) * (2 * n),
        input_output_aliases={i: 2 * n + i for i in range(2 * n)},
        compiler_params=pltpu.CompilerParams(has_side_effects=DATAFLOW),
    )(*[pltpu.with_memory_space_constraint(a, pltpu.HBM) for a in list(arrs) + list(zones)])
    return outs[:2 * n], outs[2 * n:3 * n], outs[3 * n:]


def _share_wait(started, after, name):
    sems, arrs, zones = started
    n = len(arrs)

    def body(*refs):
        src_refs, zone_refs, sem_refs = refs[:n], refs[n:2 * n], refs[2 * n:4 * n]
        for a in range(n):
            for k in range(1, N_DEV):
                cp = pltpu.make_async_remote_copy(
                    src_ref=src_refs[a], dst_ref=zone_refs[a].at[_linear(_peer(k))],
                    send_sem=sem_refs[2 * a].at[k - 1], recv_sem=sem_refs[2 * a + 1].at[k - 1],
                    device_id=_peer(k), device_id_type=MESH_ID)
                cp.wait_send()
                cp.wait_recv()

    outs = pl.pallas_call(
        body, name=name,
        out_shape=tuple(pltpu.HBM(a.shape, a.dtype) for a in arrs) + tuple(pltpu.HBM(z.shape, z.dtype) for z in zones),
        in_specs=(HBM_SPEC,) * (2 * n) + (SEM_SPEC,) * (2 * n) + (pl.BlockSpec(memory_space=pl.ANY),) * len(after),
        out_specs=(HBM_SPEC,) * (2 * n), input_output_aliases={i: i for i in range(2 * n)},
        compiler_params=pltpu.CompilerParams(has_side_effects=DATAFLOW),
    )(*arrs, *zones, *sems, *after)
    return list(outs[n:])


def _stage_blocks(arrs, after, name):
    n = len(arrs)

    def body(*refs):
        in_refs, out_refs, stage, sems = refs[:n], refs[n + 1:2 * n + 1], refs[2 * n + 1:3 * n + 1], refs[3 * n + 1]
        me = _linear(_mesh_position())
        copies = []
        for a in range(n):
            stage[a][...] = in_refs[a][...].astype(BF16)
            copies.append(pltpu.make_async_copy(stage[a], out_refs[a].at[me], sems.at[a]))
            copies[-1].start()
        for cp in copies:
            cp.wait()

    return list(pl.pallas_call(
        body, name=name,
        out_shape=[jax.ShapeDtypeStruct((N_DEV,) + a.shape, BF16) for a in arrs],
        in_specs=[pl.BlockSpec(memory_space=pltpu.VMEM)] * n + [pl.BlockSpec(memory_space=pl.ANY)],
        out_specs=[pl.BlockSpec(memory_space=pl.ANY)] * n,
        scratch_shapes=[pltpu.VMEM(a.shape, BF16) for a in arrs] + [pltpu.SemaphoreType.DMA((n,))],
        compiler_params=_params(vmem=VMEM_LIMIT_LARGE),
    )(*arrs, after))


def _same_core_peers():
    x, y, c = _mesh_position()
    return [(x, y, 1 - c), (1 - x, y, c), (x, 1 - y, c), (1 - x, 1 - y, c)]


def _gather_start(bufs, name):
    n = len(bufs)

    def body(*refs):
        buf_refs, rest = refs[:n], refs[n:]
        sems, token = rest[:2 * n], rest[-1]
        me = _linear(_mesh_position())
        for a in range(n):
            for k, peer in enumerate(_same_core_peers()):
                pltpu.make_async_remote_copy(
                    src_ref=buf_refs[a].at[me], dst_ref=buf_refs[a].at[me],
                    send_sem=sems[2 * a].at[k], recv_sem=sems[2 * a + 1].at[k],
                    device_id=peer, device_id_type=MESH_ID).start()
        token[...] = jnp.zeros_like(token)

    outs = pl.pallas_call(
        body, name=name,
        out_shape=tuple(pltpu.SemaphoreType.DMA((4,)) for _ in range(2 * n))
        + tuple(pltpu.HBM(b.shape, b.dtype) for b in bufs) + (jax.ShapeDtypeStruct((8, 128), F32),),
        in_specs=(HBM_SPEC,) * n,
        out_specs=(SEM_SPEC,) * (2 * n) + (HBM_SPEC,) * n + (pl.BlockSpec(memory_space=pltpu.VMEM),),
        input_output_aliases={a: 2 * n + a for a in range(n)},
        compiler_params=pltpu.CompilerParams(has_side_effects=DATAFLOW),
    )(*[pltpu.with_memory_space_constraint(b, pltpu.HBM) for b in bufs])
    return outs[:2 * n], outs[2 * n:3 * n], outs[3 * n]


def _gather_wait(sems, bufs, after, name):
    n = len(bufs)

    def body(*refs):
        buf_refs, sem_refs = refs[:n], refs[n:3 * n]
        x, y, c = _mesh_position()
        me = _linear((x, y, c))
        for a in range(n):
            for k, peer in enumerate(_same_core_peers()):
                cp = pltpu.make_async_remote_copy(
                    src_ref=buf_refs[a].at[me], dst_ref=buf_refs[a].at[_linear(peer)],
                    send_sem=sem_refs[2 * a].at[k], recv_sem=sem_refs[2 * a + 1].at[k],
                    device_id=peer, device_id_type=MESH_ID)
                cp.wait_send()
                cp.wait_recv()

    return list(pl.pallas_call(
        body, name=name,
        out_shape=tuple(pltpu.HBM(b.shape, b.dtype) for b in bufs),
        in_specs=(HBM_SPEC,) * n + (SEM_SPEC,) * (2 * n) + (pl.BlockSpec(memory_space=pl.ANY),) * len(after),
        out_specs=(HBM_SPEC,) * n, input_output_aliases={a: a for a in range(n)},
        compiler_params=pltpu.CompilerParams(has_side_effects=DATAFLOW),
    )(*bufs, *sems, *after))


def _gather_pass_on(bufs, name):
    n = len(bufs)

    def body(*refs):
        out_refs = refs[n:2 * n]
        send_sems, recv_sems = refs[2 * n:]
        x, y, c = _mesh_position()
        sibling = (x, y, 1 - c)
        chips = [(1 - x, y), (x, 1 - y), (1 - x, 1 - y)]
        copies = []
        for a in range(n):
            for j, chip in enumerate(chips):
                block = out_refs[a].at[_linear((*chip, c))]
                copies.append(pltpu.make_async_remote_copy(
                    src_ref=block, dst_ref=block, send_sem=send_sems.at[3 * a + j], recv_sem=recv_sems.at[3 * a + j],
                    device_id=sibling, device_id_type=MESH_ID))
                copies[-1].start()
        for a in range(n):
            for j, chip in enumerate(chips):
                copies[3 * a + j].wait_send()
                theirs = out_refs[a].at[_linear((*chip, 1 - c))]
                pltpu.make_async_remote_copy(
                    src_ref=theirs, dst_ref=theirs, send_sem=send_sems.at[3 * a + j], recv_sem=recv_sems.at[3 * a + j],
                    device_id=sibling, device_id_type=MESH_ID).wait_recv()

    hbm = pl.BlockSpec(memory_space=pl.ANY)
    return list(pl.pallas_call(
        body, name=name,
        out_shape=[jax.ShapeDtypeStruct(b.shape, b.dtype) for b in bufs],
        in_specs=[hbm] * n, out_specs=[hbm] * n, input_output_aliases={a: a for a in range(n)},
        scratch_shapes=[pltpu.SemaphoreType.DMA((3 * n,)), pltpu.SemaphoreType.DMA((3 * n,))],
    )(*bufs))


def _silu_rows(c):
    def body(c_ref, o_ref):
        v = c_ref[...]
        o_ref[...] = v * _sigmoid(v)

    return pl.pallas_call(body, name="cond_silu", out_shape=jax.ShapeDtypeStruct(c.shape, F32))(c)


def _mod_columns(cond_all, w_ada, b_cols):
    def body(c_ref, w_ref, b_ref, o_ref):
        o_ref[...] = _dot(c_ref[...], w_ref[...]) + b_ref[...]

    return pl.pallas_call(body, name="mod_columns",
                          out_shape=jax.ShapeDtypeStruct((N_DEV, w_ada.shape[1]), F32))(cond_all, w_ada, b_cols)


def _in_proj(x, mod, g_norm1, w_in, tm):
    s = x.shape[0]

    def body(x_ref, mod_ref, g_ref, w_ref, h_ref, q_ref, kv_ref, gb_ref, gc_ref, xc_ref):
        xf = x_ref[...]
        n = xf * _rsqrt_mean_sq(xf) * g_ref[...]
        h = (n * (1.0 + mod_ref[SC1:SC1 + 1, :]) + mod_ref[SH1:SH1 + 1, :]).astype(BF16)
        h_ref[...] = h
        p = _dot_nt(h, w_ref[...])
        q_ref[...] = p[:, 0:512].astype(BF16)
        kv_ref[...] = p[:, 512:768].astype(BF16)
        gb_ref[...] = p[:, 768:1280].astype(BF16)
        gc_ref[...] = p[:, 1280:1792].astype(BF16)
        xc_ref[...] = p[:, 1792:2304].astype(BF16)

    return pl.pallas_call(
        body, name="in_proj", grid=(s // tm,),
        in_specs=[_rows(tm, D_MODEL), _full((8, D_MODEL)), _full((1, D_MODEL)), _full((IN_PROJ_WIDTH, D_MODEL))],
        out_specs=[_rows(tm, D_MODEL), _rows(tm, 512), _rows(tm, 256), _rows(tm, 512), _rows(tm, 512), _rows(tm, 512)],
        out_shape=[jax.ShapeDtypeStruct((s, D_MODEL), BF16), jax.ShapeDtypeStruct((s, 512), BF16),
                   jax.ShapeDtypeStruct((s, 256), BF16), jax.ShapeDtypeStruct((s, 512), BF16),
                   jax.ShapeDtypeStruct((s, 512), BF16), jax.ShapeDtypeStruct((s, 512), BF16)],
        compiler_params=_params(("arbitrary",), VMEM_LIMIT_LARGE),
    )(x, mod, g_norm1, w_in)


def _t5_bucket(dist):
    max_exact = N_BUCKETS // 2
    is_small = dist < max_exact
    d = jnp.maximum(dist, 1).astype(F32)
    large = max_exact + (jnp.log(d / max_exact) / math.log(MAX_DISTANCE / max_exact)
                         * (N_BUCKETS - max_exact)).astype(jnp.int32)
    large = jnp.minimum(large, N_BUCKETS - 1)
    return jnp.where(is_small, dist, large)


def _bucket_table():
    qi = jnp.arange(BLOCK, dtype=jnp.int32)[:, None]
    sj = jnp.arange(2 * BLOCK, dtype=jnp.int32)[None, :]
    return _t5_bucket(jnp.maximum(qi + BLOCK - sj, 0))


def _window_mask():
    qi = lax.broadcasted_iota(jnp.int32, (BLOCK, 2 * BLOCK), 0)
    sj = lax.broadcasted_iota(jnp.int32, (BLOCK, 2 * BLOCK), 1)
    dist = qi + BLOCK - sj
    return (dist >= 0) & (dist < BLOCK)


def _bias_table(rel_bias, bucket):
    def body(rb_ref, bk_ref, o_ref):
        bk = bk_ref[...]
        inside = _window_mask()
        for h in range(N_Q_HEADS):
            acc = jnp.zeros((BLOCK, 2 * BLOCK), F32)
            for b in range(N_BUCKETS):
                acc = jnp.where(bk == b, rb_ref[b, h], acc)
            o_ref[h] = jnp.where(inside, acc, NEG_INF)

    return pl.pallas_call(
        body, name="bias_table",
        in_specs=[pl.BlockSpec(memory_space=pltpu.SMEM), pl.BlockSpec(memory_space=pltpu.VMEM)],
        out_shape=jax.ShapeDtypeStruct((N_Q_HEADS, BLOCK, 2 * BLOCK), F32),
    )(rel_bias, bucket)


def _load_kv_window(kv_ref, n):
    prev = jnp.maximum(n - 1, 0)
    kvw = jnp.concatenate([kv_ref[pl.ds(pl.multiple_of(prev * BLOCK, BLOCK), BLOCK), :],
                           kv_ref[pl.ds(pl.multiple_of(n * BLOCK, BLOCK), BLOCK), :]], axis=0)
    k, v = kvw[:, 0:128], kvw[:, 128:256]
    k_sw = pltpu.roll(k.astype(F32), 64, 1).astype(BF16)
    v_sw = pltpu.roll(v.astype(F32), 64, 1).astype(BF16)
    return (k, k_sw), (v, v_sw)


def _conv_taps(gc, xc, gc_prev, xc_prev, n):
    u = gc * xc
    before = jnp.where(n > 0, gc_prev.astype(F32) * xc_prev.astype(F32), 0.0)
    last = before.shape[0] - 1
    row = lax.broadcasted_iota(jnp.int32, u.shape, 0)
    u1 = jnp.where(row == 0, before[last:last + 1, :], pltpu.roll(u, 1, 0))
    u2 = jnp.where(row == 0, before[last - 1:last, :],
                   jnp.where(row == 1, before[last:last + 1, :], pltpu.roll(u, 2, 0)))
    return u, u1, u2


def _mixer_fwd(q, kv, gb, gc, xc, bias, sinks, conv_w, g_attn, g_conv):
    s = q.shape[0]
    nb = s // BLOCK

    per_step = min(MIXER_BLOCKS, nb)
    tile = per_step * BLOCK

    def one_block(n, rows, before, sink_ref, q_ref, kv_ref, gb_ref, gc_ref, xc_ref, bias_ref, cw_ref, ga_ref,
                  gcv_ref, attn_ref, merged_ref, lse_ref):
        ks, vs = _load_kv_window(kv_ref, n)
        lane = lax.broadcasted_iota(jnp.int32, (BLOCK, BLOCK), 1)
        low = lane < HEAD_DIM
        col = lax.broadcasted_iota(jnp.int32, (BLOCK, 2 * BLOCK), 1)
        no_prev = (col < BLOCK) & (n == 0)
        lse_all = jnp.zeros((BLOCK, BLOCK), F32)
        pairs = []
        for p in range(4):
            qp = q_ref[rows, 128 * p:128 * (p + 1)].astype(F32)
            kvh = p // 2
            res = []
            for e in range(2):
                h = 2 * p + e
                qm = jnp.where(low if e == 0 else ~low, qp, 0.0).astype(BF16)
                sw = 0 if kvh == e else 1
                sc = _dot_nt(qm, ks[sw]) * SCALE + bias_ref[h]
                sc = jnp.where(no_prev, NEG_INF, sc)
                sink = sink_ref[h]
                m = jnp.maximum(jnp.max(sc, axis=-1, keepdims=True), sink)
                pe = jnp.exp(sc - m)
                den = jnp.sum(pe, axis=-1, keepdims=True) + jnp.exp(sink - m)
                res.append(_dot(pe.astype(BF16), vs[sw]) / den)
                lse_all = lse_all + jnp.where(lane == h, m + jnp.log(den), 0.0)
            pairs.append(jnp.where(low, res[0], res[1]))
        attn = jnp.concatenate(pairs, axis=1)
        attn_ref[rows, :] = attn
        lse_ref[rows, :] = lse_all
        u, u1, u2 = _conv_taps(gc_ref[rows, :].astype(F32), xc_ref[rows, :].astype(F32), before[0], before[1], n)
        cw = cw_ref[...]
        cv = gb_ref[rows, :].astype(F32) * (cw[0:1, :] * u2 + cw[1:2, :] * u1 + cw[2:3, :] * u)
        an = attn * _rsqrt_mean_sq(attn) * ga_ref[...]
        cn = cv * _rsqrt_mean_sq(cv) * gcv_ref[...]
        merged_ref[rows, :] = jnp.concatenate([an, cn], axis=1).astype(BF16)

    def body(sink_ref, q_ref, kv_ref, gb_ref, gc_ref, xc_ref, gcp_ref, xcp_ref, *rest):
        step = pl.program_id(0)
        for sub in range(per_step):
            rows = slice(sub * BLOCK, (sub + 1) * BLOCK)
            ahead = slice(sub * BLOCK - PREV_ROWS, sub * BLOCK)
            before = (gcp_ref[...], xcp_ref[...]) if sub == 0 else (gc_ref[ahead, :], xc_ref[ahead, :])
            one_block(step * per_step + sub, rows, before, sink_ref, q_ref, kv_ref, gb_ref, gc_ref, xc_ref, *rest)

    blk = lambda w: pl.BlockSpec((tile, w), lambda n: (n, 0))
    prev8 = pl.BlockSpec((PREV_ROWS, 512), lambda n: (jnp.maximum(n * (tile // PREV_ROWS) - 1, 0), 0))
    return pl.pallas_call(
        body, name="mixer_fwd", grid=(nb // per_step,),
        in_specs=[pl.BlockSpec(memory_space=pltpu.SMEM), blk(512), _full((s, 256)), blk(512), blk(512), blk(512),
                  prev8, prev8, _full((N_Q_HEADS, BLOCK, 2 * BLOCK)), _full((3, 512)), _full((1, 512)),
                  _full((1, 512))],
        out_specs=[blk(512), blk(1024), blk(128)],
        out_shape=[jax.ShapeDtypeStruct((s, 512), F32), jax.ShapeDtypeStruct((s, 1024), BF16),
                   jax.ShapeDtypeStruct((s, 128), F32)],
        compiler_params=_params(("arbitrary",)),
    )(sinks, q, kv, gb, gc, xc, gc, xc, bias, conv_w, g_attn, g_conv)


def _out_proj(merged, x, mod, w_out, tm):
    s = x.shape[0]

    def body(m_ref, x_ref, mod_ref, w_ref, o_ref, x1_ref):
        o = _dot(m_ref[...], w_ref[...])
        o_ref[...] = o.astype(BF16)
        x1_ref[...] = x_ref[...] + mod_ref[G1:G1 + 1, :] * o

    return pl.pallas_call(
        body, name="out_proj", grid=(s // tm,),
        in_specs=[_rows(tm, D_MODEL), _rows(tm, D_MODEL), _full((8, D_MODEL)), _full((D_MODEL, D_MODEL))],
        out_specs=[_rows(tm, D_MODEL), _rows(tm, D_MODEL)],
        out_shape=[jax.ShapeDtypeStruct((s, D_MODEL), BF16), jax.ShapeDtypeStruct((s, D_MODEL), F32)],
        compiler_params=_params(("arbitrary",)),
    )(merged, x, mod, w_out)


def _resident(shape):
    nd = len(shape)
    return pl.BlockSpec(shape, lambda *_: (0,) * nd, pipeline_mode=pl.Buffered(1))


def _ffn(x1, o1, mod, g_norm2, w_gu, w_down, w_out, g_final, target, tm):
    s = x1.shape[0]
    chunk = D_FF // FFN_CHUNKS

    def body(x_ref, o1_ref, mod_ref, g_ref, wgu_ref, wd_ref, wo_ref, gf_ref, t_ref,
             h_ref, act_ref, do_ref, dgu_ref, dx1_ref, do1_ref, dm_ref, small_ref):
        @pl.when(pl.program_id(0) == 0)
        def _():
            small_ref[...] = jnp.zeros_like(small_ref)

        xf = x_ref[...]
        n = xf * _rsqrt_mean_sq(xf) * g_ref[...]
        h = (n * (1.0 + mod_ref[SC2:SC2 + 1, :]) + mod_ref[SH2:SH2 + 1, :]).astype(BF16)
        h_ref[...] = h
        gates, ups, o = [], [], None
        for j in range(FFN_CHUNKS):
            lo = j * chunk
            gate = _dot_nt(h, wgu_ref[lo:lo + chunk, :])
            up = _dot_nt(h, wgu_ref[D_FF + lo:D_FF + lo + chunk, :])
            sg = _sigmoid(gate)
            act = (gate * sg * up).astype(BF16)
            act_ref[:, lo:lo + chunk] = act
            gates.append((up * (sg * (1.0 + gate * (1.0 - sg)))).astype(BF16))
            ups.append((gate * sg).astype(BF16))
            part = _dot(act, wd_ref[lo:lo + chunk, :])
            o = part if o is None else o + part
        g2 = mod_ref[G2:G2 + 1, :]
        x2 = xf + g2 * o
        r = _rsqrt_mean_sq(x2)
        xn = x2 * r
        gf = gf_ref[...]
        err = xn * gf - t_ref[...]
        dy = err * (1.0 / D_MODEL)
        dxn = dy * gf
        dx2 = r * (dxn - xn * jnp.mean(dxn * xn, axis=-1, keepdims=True))
        small_ref[4:5, :] += _colsum(dy * xn)
        small_ref[5:6, :] += _colsum(err * err)
        small_ref[3:4, :] += _colsum(dx2 * o)
        do = (dx2 * g2).astype(BF16)
        do_ref[...] = do
        dh = None
        for j in range(FFN_CHUNKS):
            lo = j * chunk
            dact = _dot_nt(do, wd_ref[lo:lo + chunk, :])
            dgate = (dact * gates[j].astype(F32)).astype(BF16)
            dup = (dact * ups[j].astype(F32)).astype(BF16)
            dgu_ref[:, lo:lo + chunk] = dgate
            dgu_ref[:, D_FF + lo:D_FF + lo + chunk] = dup
            part = _dot(dgate, wgu_ref[lo:lo + chunk, :]) + _dot(dup, wgu_ref[D_FF + lo:D_FF + lo + chunk, :])
            dh = part if dh is None else dh + part
        dx1 = dx2 + _norm_mod_bwd(dh, xf, g_ref[...], mod_ref[SC2:SC2 + 1, :], small_ref)
        dx1_ref[...] = dx1.astype(BF16)
        small_ref[7:8, :] += _colsum(dx1 * o1_ref[...].astype(F32))
        do1 = (dx1 * mod_ref[G1:G1 + 1, :]).astype(BF16)
        do1_ref[...] = do1
        dm_ref[...] = _dot_nt(do1, wo_ref[...]).astype(BF16)

        @pl.when(pl.program_id(0) == pl.num_programs(0) - 1)
        def _():
            total = jnp.sum(small_ref[5:6, :], axis=-1, keepdims=True) * (0.5 / D_MODEL)
            small_ref[6:7, :] = jnp.broadcast_to(total, (1, D_MODEL))

    narrow = jax.ShapeDtypeStruct((s, D_MODEL), BF16)
    return pl.pallas_call(
        body, name="ffn", grid=(s // tm,),
        in_specs=[_rows(tm, D_MODEL), _rows(tm, D_MODEL), _full((8, D_MODEL)), _full((1, D_MODEL)),
                  _resident((2 * D_FF, D_MODEL)), _resident((D_FF, D_MODEL)), _resident((D_MODEL, D_MODEL)),
                  _full((1, D_MODEL)), _rows(tm, D_MODEL)],
        out_specs=[_rows(tm, D_MODEL), _rows(tm, D_FF), _rows(tm, D_MODEL), _rows(tm, 2 * D_FF), _rows(tm, D_MODEL),
                   _rows(tm, D_MODEL), _rows(tm, D_MODEL), _full((8, D_MODEL))],
        out_shape=[narrow, jax.ShapeDtypeStruct((s, D_FF), BF16), narrow, jax.ShapeDtypeStruct((s, 2 * D_FF), BF16),
                   narrow, narrow, narrow, jax.ShapeDtypeStruct((8, D_MODEL), F32)],
        compiler_params=_params(("arbitrary",), VMEM_LIMIT_LARGE),
    )(x1, o1, mod, g_norm2, w_gu, w_down, w_out, g_final, target)


def _norm_mod_bwd(dh, xf, g, scale_row, small_ref):
    r = _rsqrt_mean_sq(xf)
    xn = xf * r
    small_ref[0:1, :] += _colsum(dh)
    small_ref[1:2, :] += _colsum(dh * (xn * g))
    dn = dh * (1.0 + scale_row)
    small_ref[2:3, :] += _colsum(dn * xn)
    dxn = dn * g
    return r * (dxn - xn * jnp.mean(dxn * xn, axis=-1, keepdims=True))


def _group_norm_bwd(dm, a, g):
    r = _rsqrt_mean_sq(a)
    an = a * r
    dan = dm * g
    return r * (dan - an * jnp.mean(dan * an, axis=-1, keepdims=True)), _colsum(dm * an)


def _mixer_bwd(after, q, kv, gb, gc, xc, bias, sinks, conv_w, g_attn, g_conv, attn, lse, dmerged):
    s = q.shape[0]
    nb = s // BLOCK

    per_step = min(MIXER_BLOCKS, nb)
    tile = per_step * BLOCK
    steps = nb // per_step

    def one_block(n, rows, before, nxt, sink_ref, q_ref, kv_ref, gb_ref, gc_ref, xc_ref, bias_ref, cw_ref, ga_ref,
                  gcv_ref, attn_ref, lse_ref, dm_ref, dproj_ref, dbias_ref, dsink_ref, small_ref):
        next_dy, next_dkv = nxt
        dm = dm_ref[rows, :].astype(F32)
        gbv, gcv_, xcv = gb_ref[rows, :].astype(F32), gc_ref[rows, :].astype(F32), xc_ref[rows, :].astype(F32)
        u, u1, u2 = _conv_taps(gcv_, xcv, before[0], before[1], n)
        cw = cw_ref[...]
        yv = cw[0:1, :] * u2 + cw[1:2, :] * u1 + cw[2:3, :] * u
        dcv, dg_conv = _group_norm_bwd(dm[:, 512:1024], gbv * yv, gcv_ref[...])
        small_ref[1:2, :] += dg_conv
        dproj_ref[rows, 768:1280] = (dcv * yv).astype(BF16)
        dy = dcv * gbv
        row = lax.broadcasted_iota(jnp.int32, dy.shape, 0)
        d1 = jnp.where(row == BLOCK - 1, next_dy[0:1, :], pltpu.roll(dy, BLOCK - 1, 0))
        d2 = jnp.where(row == BLOCK - 2, next_dy[0:1, :],
                       jnp.where(row == BLOCK - 1, next_dy[1:2, :], pltpu.roll(dy, BLOCK - 2, 0)))
        du = cw[2:3, :] * dy + cw[1:2, :] * d1 + cw[0:1, :] * d2
        dproj_ref[rows, 1280:1792] = (du * xcv).astype(BF16)
        dproj_ref[rows, 1792:2304] = (du * gcv_).astype(BF16)
        small_ref[2:3, :] += _colsum(dy * u2)
        small_ref[3:4, :] += _colsum(dy * u1)
        small_ref[4:5, :] += _colsum(dy * u)

        attn_v = attn_ref[rows, :]
        dout, dg_attn = _group_norm_bwd(dm[:, 0:512], attn_v, ga_ref[...])
        small_ref[0:1, :] += dg_attn
        ks, vs = _load_kv_window(kv_ref, n)
        lane = lax.broadcasted_iota(jnp.int32, (BLOCK, BLOCK), 1)
        low = lane < HEAD_DIM
        col = lax.broadcasted_iota(jnp.int32, (BLOCK, 2 * BLOCK), 1)
        no_prev = (col < BLOCK) & (n == 0)
        lse_all = lse_ref[rows, :]
        dsink = jnp.zeros((BLOCK, BLOCK), F32)
        dq_pairs = []
        dk_groups, dv_groups = [], []
        for kvh in range(2):
            ds_rows, pr_rows, q_rows, do_rows = [], [], [], []
            for p in (2 * kvh, 2 * kvh + 1):
                qp = q_ref[rows, 128 * p:128 * (p + 1)].astype(F32)
                do_p = dout[:, 128 * p:128 * (p + 1)]
                prod = do_p * attn_v[:, 128 * p:128 * (p + 1)]
                res = []
                for e in range(2):
                    h = 2 * p + e
                    half = low if e == 0 else ~low
                    qm = jnp.where(half, qp, 0.0).astype(BF16)
                    dom = jnp.where(half, do_p, 0.0).astype(BF16)
                    delta = jnp.sum(jnp.where(half, prod, 0.0), axis=-1, keepdims=True)
                    lse_h = jnp.sum(jnp.where(lane == h, lse_all, 0.0), axis=-1, keepdims=True)
                    sw = 0 if kvh == e else 1
                    sc = _dot_nt(qm, ks[sw]) * SCALE + bias_ref[h]
                    sc = jnp.where(no_prev, NEG_INF, sc)
                    pr = jnp.exp(sc - lse_h)
                    dp = _dot_nt(dom, vs[sw])
                    ds = pr * (dp - delta)
                    dbias_ref[h] += ds
                    dsink = dsink + jnp.where(lane == h, -jnp.exp(sink_ref[h] - lse_h) * delta, 0.0)
                    dsb = ds.astype(BF16)
                    res.append(_dot(dsb, ks[sw]) * SCALE)
                    ds_rows.append(dsb)
                    pr_rows.append(pr.astype(BF16))
                    q_rows.append(qm)
                    do_rows.append(dom)
                dq_pairs.append(jnp.where(low, res[0], res[1]))
            dk_g = _dot_tn(jnp.concatenate(ds_rows, axis=0), jnp.concatenate(q_rows, axis=0)) * SCALE
            dv_g = _dot_tn(jnp.concatenate(pr_rows, axis=0), jnp.concatenate(do_rows, axis=0))
            dk_groups.append(dk_g + pltpu.roll(dk_g, 64, 1))
            dv_groups.append(dv_g + pltpu.roll(dv_g, 64, 1))
        dproj_ref[rows, 0:512] = jnp.concatenate(dq_pairs, axis=1).astype(BF16)
        dsink_ref[...] += dsink
        low_kv = lax.broadcasted_iota(jnp.int32, (2 * BLOCK, BLOCK), 1) < HEAD_DIM
        dkv_win = jnp.concatenate([jnp.where(low_kv, dk_groups[0], dk_groups[1]),
                                   jnp.where(low_kv, dv_groups[0], dv_groups[1])], axis=1)
        dproj_ref[rows, 512:768] = (dkv_win[BLOCK:2 * BLOCK, :] + next_dkv).astype(BF16)
        return dy[0:8, :], dkv_win[0:BLOCK, :]

    def body(sink_ref, q_ref, kv_ref, gb_ref, gc_ref, xc_ref, gcp_ref, xcp_ref, *rest):
        refs, dy_ref, dkv_ref = rest[:-2], rest[-2], rest[-1]
        dbias_ref, dsink_ref, small_ref = refs[8], refs[9], refs[10]
        step = pl.program_id(0)

        @pl.when(step == 0)
        def _():
            dbias_ref[...] = jnp.zeros_like(dbias_ref)
            dsink_ref[...] = jnp.zeros_like(dsink_ref)
            small_ref[...] = jnp.zeros_like(small_ref)
            dy_ref[...] = jnp.zeros_like(dy_ref)
            dkv_ref[...] = jnp.zeros_like(dkv_ref)

        nxt = (dy_ref[...], dkv_ref[...])
        for sub in reversed(range(per_step)):
            rows = slice(sub * BLOCK, (sub + 1) * BLOCK)
            ahead = slice(sub * BLOCK - PREV_ROWS, sub * BLOCK)
            before = (gcp_ref[...], xcp_ref[...]) if sub == 0 else (gc_ref[ahead, :], xc_ref[ahead, :])
            nxt = one_block((steps - 1 - step) * per_step + sub, rows, before, nxt,
                            sink_ref, q_ref, kv_ref, gb_ref, gc_ref, xc_ref, *refs)
        dy_ref[...], dkv_ref[...] = nxt

        @pl.when(step == steps - 1)
        def _():
            small_ref[5:6, :] = jnp.concatenate([_colsum(dsink_ref[...]), jnp.zeros((1, 512 - BLOCK), F32)], axis=1)

    blk = lambda w: pl.BlockSpec((tile, w), lambda t: (steps - 1 - t, 0))
    prev8 = pl.BlockSpec((PREV_ROWS, 512),
                         lambda t: (jnp.maximum((steps - 1 - t) * (tile // PREV_ROWS) - 1, 0), 0))
    bf = lambda w: jax.ShapeDtypeStruct((s, w), BF16)
    return pl.pallas_call(
        _coming_behind(body), name="mixer_bwd", grid=(steps,),
        in_specs=[ANY_SPEC, pl.BlockSpec(memory_space=pltpu.SMEM), blk(512), _full((s, 256)), blk(512), blk(512), blk(512),
                  prev8, prev8, _full((N_Q_HEADS, BLOCK, 2 * BLOCK)), _full((3, 512)), _full((1, 512)),
                  _full((1, 512)), blk(512), blk(128), blk(1024)],
        out_specs=[blk(IN_PROJ_WIDTH), _full((N_Q_HEADS, BLOCK, 2 * BLOCK)), _full((BLOCK, BLOCK)), _full((8, 512))],
        out_shape=[bf(IN_PROJ_WIDTH), jax.ShapeDtypeStruct((N_Q_HEADS, BLOCK, 2 * BLOCK), F32),
                   jax.ShapeDtypeStruct((BLOCK, BLOCK), F32), jax.ShapeDtypeStruct((8, 512), F32)],
        scratch_shapes=[pltpu.VMEM((8, 512), F32), pltpu.VMEM((BLOCK, 2 * KV_WIDTH), F32)],
        compiler_params=_params(("arbitrary",), VMEM_LIMIT_LARGE),
    )(after, sinks, q, kv, gb, gc, xc, gc, xc, bias, conv_w, g_attn, g_conv, attn, lse, dmerged)


def _in_proj_bwd(after, dproj, x, dx1, mod, g_norm1, w_in, tm):
    s = x.shape[0]

    def body(dproj_ref, x_ref, dx1_ref, mod_ref, g_ref, w_ref, dx_ref, small_ref):
        @pl.when(pl.program_id(0) == 0)
        def _():
            small_ref[...] = jnp.zeros_like(small_ref)

        dh = _dot(dproj_ref[...], w_ref[...])
        dx_ref[...] = dx1_ref[...].astype(F32) + _norm_mod_bwd(dh, x_ref[...], g_ref[...], mod_ref[SC1:SC1 + 1, :],
                                                               small_ref)

    return pl.pallas_call(
        _coming_behind(body), name="in_proj_bwd", grid=(s // tm,),
        in_specs=[ANY_SPEC, _rows(tm, IN_PROJ_WIDTH), _rows(tm, D_MODEL), _rows(tm, D_MODEL), _full((8, D_MODEL)),
                  _full((1, D_MODEL)), _full((IN_PROJ_WIDTH, D_MODEL))],
        out_specs=[_rows(tm, D_MODEL), _full((8, D_MODEL))],
        out_shape=[jax.ShapeDtypeStruct((s, D_MODEL), F32), jax.ShapeDtypeStruct((8, D_MODEL), F32)],
        compiler_params=_params(("arbitrary",), VMEM_LIMIT_LARGE),
    )(after, dproj, x, dx1, mod, g_norm1, w_in)


def _weight_grad(a, b, tk, ts, name, after=None):
    s, k = a.shape
    n = b.shape[1]
    nt = s // ts
    extra = [] if after is None else [after]

    def body(a_ref, b_ref, *rest):
        o_ref, acc_ref = rest[-2:]
        t = pl.program_id(1)
        part = _dot_tn(a_ref[...], b_ref[...])

        @pl.when(t == 0)
        def _():
            acc_ref[...] = part

        @pl.when(t > 0)
        def _():
            acc_ref[...] += part

        @pl.when(t == nt - 1)
        def _():
            o_ref[...] = acc_ref[...].astype(BF16)

    return pl.pallas_call(
        body, name=name, grid=(k // tk, nt),
        in_specs=[pl.BlockSpec((ts, tk), lambda i, t: (t, i)), pl.BlockSpec((ts, n), lambda i, t: (t, 0))]
        + [pl.BlockSpec(memory_space=pl.ANY)] * len(extra),
        out_specs=pl.BlockSpec((tk, n), lambda i, t: (i, 0)),
        out_shape=jax.ShapeDtypeStruct((k, n), BF16),
        scratch_shapes=[pltpu.VMEM((tk, n), F32)],
        compiler_params=_params(("arbitrary", "arbitrary"), VMEM_LIMIT_LARGE),
    )(a, b, *extra)


def _rel_bias_grad(dbias, bucket):
    def body(db_ref, bk_ref, o_ref, rows_ref):
        bk = bk_ref[...]
        for b in range(N_BUCKETS):
            sel = (bk == b).astype(F32)
            for h in range(N_Q_HEADS):
                rows_ref[N_BUCKETS * h + b:N_BUCKETS * h + b + 1, :] = _colsum(db_ref[h] * sel)
        head = lax.broadcasted_iota(jnp.int32, (N_BUCKETS, N_Q_HEADS), 1)
        out = jnp.zeros((N_BUCKETS, N_Q_HEADS), F32)
        for h in range(N_Q_HEADS):
            per_bucket = jnp.sum(rows_ref[N_BUCKETS * h:N_BUCKETS * (h + 1), :], axis=-1, keepdims=True)
            out = out + jnp.where(head == h, per_bucket, 0.0)
        o_ref[...] = out

    return pl.pallas_call(
        body, name="rel_bias_grad",
        out_shape=jax.ShapeDtypeStruct((N_BUCKETS, N_Q_HEADS), F32),
        scratch_shapes=[pltpu.VMEM((N_BUCKETS * N_Q_HEADS, 2 * BLOCK), F32)],
    )(dbias, bucket)


def _lanes_from(x, start, width):
    n = x.shape[1]
    return pltpu.roll(x, (n - start) % n, 1)[:, 0:width]


def _w_ada_grad(me, cond_all, packed_all, cols):
    def body(me_ref, c_ref, p_ref, o_ref):
        dmod = jnp.concatenate([p_ref[k][:, OFF_DMOD:OFF_DMOD + N_MOD * D_MODEL] for k in range(N_DEV)], axis=0)
        mine = _lanes_from(dmod, me_ref[0] * cols, cols)
        pad = lambda a: jnp.concatenate([a, jnp.zeros((128 - N_DEV, a.shape[1]), F32)], axis=0)
        o_ref[...] = _dot_tn(pad(c_ref[...]), pad(mine))

    vmem = pl.BlockSpec(memory_space=pltpu.VMEM)
    return pl.pallas_call(body, name="w_ada_grad",
                          in_specs=[pl.BlockSpec(memory_space=pltpu.SMEM), vmem, vmem],
                          out_shape=jax.ShapeDtypeStruct((cond_all.shape[1], cols), F32))(me, cond_all, packed_all)


SMALL_PARAMS = (("rel_bias", None), ("b_ada", (OFF_DMOD, N_MOD * D_MODEL)), ("g_norm1", (OFF_GN1, D_MODEL)),
                ("sinks", (OFF_SINK, N_Q_HEADS)), ("conv_w", None), ("g_attn_out", (OFF_GATT, ATTN_WIDTH)),
                ("g_conv_out", (OFF_GCV, CONV_WIDTH)), ("g_norm2", (OFF_GN2, D_MODEL)),
                ("g_final", (OFF_GFIN, D_MODEL)))


def _small_update(me, packed_all, rel_all, state, after):
    n_p = len(SMALL_PARAMS)
    flat = [a for triple in state for a in triple]
    conv_cols = state[4][0].shape[1]

    def body(me_ref, p_ref, r_ref, *refs):
        ins = refs[:3 * n_p]
        loss_ref, outs = refs[3 * n_p + len(after)], refs[3 * n_p + len(after) + 1:]
        small, rel = p_ref[0], r_ref[0]
        for k in range(1, N_DEV):
            small = small + p_ref[k]
            rel = rel + r_ref[k]
        loss_ref[...] = small[:, OFF_LOSS:OFF_LOSS + 128]
        taps = jnp.concatenate([small[:, OFF_CONVW + CONV_WIDTH * j:OFF_CONVW + CONV_WIDTH * (j + 1)]
                                for j in range(3)] + [jnp.zeros((5, CONV_WIDTH), F32)], axis=0)
        conv_g = _lanes_from(taps, me_ref[0] * conv_cols, conv_cols)[0:3, :]
        for i, (name, lanes) in enumerate(SMALL_PARAMS):
            g = rel if name == "rel_bias" else conv_g if name == "conv_w" else small[:, lanes[0]:lanes[0] + lanes[1]]
            w_ref, m_ref, v_ref = ins[3 * i:3 * i + 3]
            outs[4 * i][...] = g
            outs[4 * i + 1][...], outs[4 * i + 2][...], outs[4 * i + 3][...] = _adam_math(
                w_ref[...], g, m_ref[...], v_ref[...])

    vmem = pl.BlockSpec(memory_space=pltpu.VMEM)
    out_shape = [jax.ShapeDtypeStruct((1, 128), F32)]
    for w, _, _ in state:
        out_shape += [jax.ShapeDtypeStruct(w.shape, F32)] * 4
    outs = pl.pallas_call(
        body, name="small_update",
        in_specs=[pl.BlockSpec(memory_space=pltpu.SMEM), vmem, vmem] + [vmem] * len(flat)
        + [pl.BlockSpec(memory_space=pl.ANY)] * len(after),
        out_shape=out_shape,
    )(me, packed_all, rel_all, *flat, *after)
    return outs[0], [tuple(outs[1 + 4 * i:5 + 4 * i]) for i in range(n_p)]


def _adam_math(w, g, m, v):
    m = ADAM_B1 * m + (1.0 - ADAM_B1) * g
    v = ADAM_B2 * v + (1.0 - ADAM_B2) * (g * g)
    m_hat = m / (1.0 - ADAM_B1 ** ADAM_STEP)
    v_hat = v / (1.0 - ADAM_B2 ** ADAM_STEP)
    delta = -ADAM_LR * (m_hat / (jnp.sqrt(v_hat) + ADAM_EPS) + ADAM_WD * w)
    return delta, m, v


def _adamw_parts(w, m, v, local, land, me, tr, name):
    r, c = w.shape

    def body(me_ref, w_ref, m_ref, v_ref, own_ref, land_ref, g_ref, d_ref, mo_ref, vo_ref):
        g = own_ref[0].astype(F32)
        for k in range(N_DEV - 1):
            g = g + land_ref[k].astype(F32)
        g_ref[...] = g
        d_ref[...], mo_ref[...], vo_ref[...] = _adam_math(w_ref[...], g, m_ref[...], v_ref[...])

    tile = pl.BlockSpec((tr, c), lambda i, me_ref: (i, 0))
    return pl.pallas_call(
        body, name=name,
        grid_spec=pltpu.PrefetchScalarGridSpec(
            num_scalar_prefetch=1, grid=(r // tr,),
            in_specs=[tile, tile, tile, pl.BlockSpec((1, tr, c), lambda i, me_ref: (me_ref[0], i, 0)),
                      pl.BlockSpec((N_DEV - 1, tr, c), lambda i, me_ref: (0, i, 0))],
            out_specs=[tile] * 4),
        out_shape=[jax.ShapeDtypeStruct((r, c), F32)] * 4,
        compiler_params=_params(("arbitrary",)),
    )(me, w, m, v, local, land)


def _adamw(w, m, v, g, tr, name):
    r, c = w.shape

    def body(w_ref, m_ref, v_ref, g_ref, d_ref, mo_ref, vo_ref):
        d_ref[...], mo_ref[...], vo_ref[...] = _adam_math(w_ref[...], g_ref[...], m_ref[...], v_ref[...])

    tile = pl.BlockSpec((tr, c), lambda i: (i, 0))
    return pl.pallas_call(
        body, name=name, grid=(r // tr,),
        in_specs=[tile] * 4, out_specs=[tile] * 3,
        out_shape=[jax.ShapeDtypeStruct((r, c), F32)] * 3,
        compiler_params=_params(("arbitrary",)),
    )(w, m, v, g)


def _behind(a, token):
    return a + token[0:a.shape[0], 0:1]


def _local_step(x, target, mod, w_in_t, weights_out_gu, weights_down, rel_bias, g_norm1, sinks, conv_w, g_attn,
                g_conv, g_norm2, g_final, exchange):
    s = x.shape[0]
    tm = min(512, s)
    tm_small = min(256, s)
    bucket = _bucket_table()
    bias = _bias_table(rel_bias, bucket)

    h, q, kv, gb, gc, xc = _in_proj(x, mod, g_norm1, w_in_t, tm)
    attn, merged, lse = _mixer_fwd(q, kv, gb, gc, xc, bias, sinks, conv_w, g_attn, g_conv)
    w_out, w_gu_t = weights_out_gu(merged)
    o1, x1 = _out_proj(merged, x, mod, w_out, tm)
    w_down = weights_down(x1)
    h2, act, do2, dgu, dx1, do1, dmerged, sm_2 = _ffn(x1, o1, mod, g_norm2, w_gu_t, w_down, w_out, g_final, target,
                                                      tm_small)
    ts = min(WEIGHT_GRAD_ROWS, s)
    tok_down = exchange("w_down", _weight_grad(act, do2, D_FF // 2, ts, "w_down_grad"))
    tok_gu = exchange("w_gu", _weight_grad(dgu, h2, D_FF // 2, ts, "w_gu_grad", after=tok_down))
    tok_out = exchange("w_out", _weight_grad(merged, do1, D_MODEL, ts, "w_out_grad", after=tok_gu))
    dproj, dbias, dsink, sm_mix = _mixer_bwd(
        tok_out, q, kv, gb, gc, xc, bias, sinks, conv_w, g_attn, g_conv, attn, lse, dmerged)
    tok_in = exchange("w_in", _weight_grad(dproj, h, IN_PROJ_WIDTH // 2, ts, "w_in_grad"))
    dx, sm_1 = _in_proj_bwd(tok_in, dproj, x, dx1, mod, g_norm1, w_in_t, tm)
    d_rel = _rel_bias_grad(dbias, bucket)

    packed = jnp.concatenate([
        sm_1[0], sm_1[1], sm_2[7], sm_2[0], sm_2[1], sm_2[3],
        sm_1[2],
        sm_mix[5, 0:128],
        sm_mix[0], sm_mix[1],
        sm_2[2],
        sm_2[4],
        sm_mix[2], sm_mix[3], sm_mix[4],
        sm_2[6, 0:128],
    ])[None, :]
    return dx, packed, d_rel


def kernel(x, c, rel_bias, w_ada, b_ada, g_norm1, w_in, sinks, conv_w, g_attn_out, g_conv_out, w_out, g_norm2, w_gu, w_down, g_final, loss_target, m_rel_bias, m_w_ada, m_b_ada, m_g_norm1, m_w_in, m_sinks, m_conv_w, m_g_attn_out, m_g_conv_out, m_w_out, m_g_norm2, m_w_gu, m_w_down, m_g_final, v_rel_bias, v_w_ada, v_b_ada, v_g_norm1, v_w_in, v_sinks, v_conv_w, v_g_attn_out, v_g_conv_out, v_w_out, v_g_norm2, v_w_gu, v_w_down, v_g_final):
    me = _linear(_mesh_position())
    me_arr = jnp.reshape(me, (1,)).astype(jnp.int32)
    ada_cols = w_ada.shape[2]
    tm = min(512, x.shape[1])

    cond = _silu_rows(c)
    cond_all, conv_w_all = _all_gather_small([cond, conv_w[0]], "gather_cond")
    cond_all = cond_all[:, 0, :]
    conv_cols = conv_w.shape[2]
    conv_w_full = conv_w_all.transpose(1, 0, 2).reshape(3, CONV_WIDTH)
    b_cols = lax.dynamic_slice_in_dim(b_ada, me * ada_cols, ada_cols, axis=1)
    mod_cols = _mod_columns(cond_all, w_ada[0], b_cols)
    mod_all = _all_gather_small([mod_cols], "gather_mod")[0]
    mod = lax.dynamic_index_in_dim(mod_all, me, axis=1, keepdims=False).reshape(N_MOD, D_MODEL)
    mod = jnp.concatenate([mod, jnp.zeros((2, D_MODEL), F32)], axis=0)

    w_in_t = _all_gather([w_in[0].T], "gather_w_in", to_bf16=True, big=True)[0].reshape(IN_PROJ_WIDTH, D_MODEL)
    gather_sems, staged, gather_token = _gather_start(
        _stage_blocks([w_out[0], w_gu[0].T, w_down[0]], w_in_t, "stage_weights"), "gather_start_weights")
    mod = _behind(mod, gather_token)

    def weights_out_gu(after):
        got = _gather_pass_on(_gather_wait(gather_sems[0:4], staged[0:2], [after], "gather_wait_out_gu"),
                              "gather_pass_on_out_gu")
        return got[0].reshape(D_MODEL, D_MODEL), got[1].reshape(2 * D_FF, D_MODEL)

    def weights_down(after):
        got = _gather_pass_on(_gather_wait(gather_sems[4:6], staged[2:3], [after], "gather_wait_down"),
                              "gather_pass_on_down")
        return got[0].reshape(D_FF, D_MODEL)

    started = {}

    def exchange(name, dw):
        st = _exchange_start(dw.reshape(N_DEV, dw.shape[0] // N_DEV, dw.shape[1]), "exchange_start_" + name)
        started[name] = st
        return st[4]

    dx, packed, d_rel = _local_step(
        x[0], loss_target[0], mod, w_in_t, weights_out_gu, weights_down, rel_bias, g_norm1, sinks[0], conv_w_full,
        g_attn_out, g_conv_out, g_norm2, g_final[None, :], exchange)

    def zone(a):
        return lax.dynamic_update_slice(jnp.zeros((N_DEV,) + a.shape, F32), a[None], (me,) + (0,) * a.ndim)

    shared = _share_start([packed, d_rel], [zone(packed), zone(d_rel)], "share_small_start")

    def finish(name, after, w, m, v, tr):
        src, land = _exchange_wait(started[name], after, "exchange_wait_" + name)
        return _adamw_parts(w, m, v, src, land, me_arr, tr, "adamw_" + name)

    g_down, d_down, nm_down, nv_down = finish("w_down", [shared[2][0]], w_down[0], m_w_down[0], v_w_down[0], 176)
    g_gu, d_gu, nm_gu, nv_gu = finish("w_gu", [nv_down], w_gu[0].T, m_w_gu[0].T, v_w_gu[0].T, 352)
    g_out, d_out, nm_out, nv_out = finish("w_out", [nv_gu], w_out[0], m_w_out[0], v_w_out[0], 128)

    packed_all, rel_all = _share_wait(shared, [nv_out], "share_small_wait")
    g_ada = _w_ada_grad(me_arr, cond_all, packed_all, ada_cols)
    d_ada, nm_ada, nv_ada = _adamw(w_ada[0], m_w_ada[0], v_w_ada[0], g_ada, 256, "adamw_w_ada")
    as_rows = {"conv_w": lambda a: a[0], "g_final": lambda a: a[None, :]}
    small_state = {
        "rel_bias": (rel_bias, m_rel_bias, v_rel_bias), "b_ada": (b_ada, m_b_ada, v_b_ada),
        "g_norm1": (g_norm1, m_g_norm1, v_g_norm1), "sinks": (sinks, m_sinks, v_sinks),
        "conv_w": (conv_w, m_conv_w, v_conv_w), "g_attn_out": (g_attn_out, m_g_attn_out, v_g_attn_out),
        "g_conv_out": (g_conv_out, m_g_conv_out, v_g_conv_out), "g_norm2": (g_norm2, m_g_norm2, v_g_norm2),
        "g_final": (g_final, m_g_final, v_g_final),
    }
    state = [tuple(as_rows.get(name, lambda a: a)(a) for a in small_state[name]) for name, _ in SMALL_PARAMS]
    loss_row, small_out = _small_update(me_arr, packed_all, rel_all, state, [])
    loss = loss_row[0, 0]
    small_res = {name: tuple(a.reshape(small_state[name][0].shape) for a in res)
                 for (name, _), res in zip(SMALL_PARAMS, small_out)}

    g_in, d_in, nm_in, nv_in = finish("w_in", [loss_row, nv_ada], w_in[0].T, m_w_in[0].T, v_w_in[0].T, 144)

    big = {
        "w_ada": (g_ada[None], d_ada[None], nm_ada[None], nv_ada[None]),
        "w_in": (g_in.T[None], d_in.T[None], nm_in.T[None], nv_in.T[None]),
        "w_out": (g_out[None], d_out[None], nm_out[None], nv_out[None]),
        "w_gu": (g_gu.T[None], d_gu.T[None], nm_gu.T[None], nv_gu.T[None]),
        "w_down": (g_down[None], d_down[None], nm_down[None], nv_down[None]),
    }
    order = ["rel_bias", "w_ada", "b_ada", "g_norm1", "w_in", "sinks", "conv_w", "g_attn_out", "g_conv_out", "w_out",
             "g_norm2", "w_gu", "w_down", "g_final"]
    results = [big[k] if k in big else small_res[k] for k in order]
    return (loss, dx[None], *[r[0] for r in results], *[r[1] for r in results], *[r[2] for r in results],
            *[r[3] for r in results])
```

```python
import functools
import math

import jax
import jax.numpy as jnp
from jax import lax
from jax.experimental import pallas as pl
from jax.experimental.pallas import tpu as pltpu

F32 = jnp.float32
BF16 = jnp.bfloat16

D_MODEL = 1024
HEAD_DIM = 64
N_Q_HEADS = 8
ATTN_WIDTH = 512
KV_WIDTH = 128
CONV_WIDTH = 512
IN_PROJ_WIDTH = 2304
D_FF = 2816
N_MOD = 6
N_BUCKETS = 32
MAX_DISTANCE = 128
BLOCK = 128
EPS = 1e-6
NEG_INF = -1e30
SCALE = HEAD_DIM ** -0.5
N_DEV = 8

ADAM_LR = 0.001
ADAM_B1 = 0.9
ADAM_B2 = 0.999
ADAM_EPS = 1e-08
ADAM_WD = 0.01
ADAM_STEP = 10

SH1, SC1, G1, SH2, SC2, G2 = range(6)

VMEM_LIMIT_LARGE = 60 * 1024 * 1024
WEIGHT_GRAD_ROWS = 2048
FFN_CHUNKS = 2
PREV_ROWS = 16
MIXER_BLOCKS = 4
MESH_ID = pl.DeviceIdType.MESH

OFF_DMOD = 0
OFF_GN1 = OFF_DMOD + N_MOD * D_MODEL
OFF_SINK = OFF_GN1 + D_MODEL
OFF_GATT = OFF_SINK + 128
OFF_GCV = OFF_GATT + ATTN_WIDTH
OFF_GN2 = OFF_GCV + CONV_WIDTH
OFF_GFIN = OFF_GN2 + D_MODEL
OFF_CONVW = OFF_GFIN + D_MODEL
OFF_LOSS = OFF_CONVW + 3 * CONV_WIDTH
PACKED = OFF_LOSS + 128


def _params(sem=None, vmem=None):
    return pltpu.CompilerParams(dimension_semantics=sem, vmem_limit_bytes=vmem)


def _coming_behind(body):
    def skipping(after_ref, *refs):
        body(*refs)

    return skipping


ANY_SPEC = pl.BlockSpec(memory_space=pl.ANY)


def _full(shape):
    nd = len(shape)
    return pl.BlockSpec(shape, lambda *_: (0,) * nd)


def _rows(tm, width):
    return pl.BlockSpec((tm, width), lambda i, *_: (i, 0))


def _sigmoid(x):
    return 1.0 / (1.0 + jnp.exp(-x))


def _rsqrt_mean_sq(x):
    return lax.rsqrt(jnp.mean(x * x, axis=-1, keepdims=True) + EPS)


def _colsum(x):
    return jnp.sum(x, axis=0, keepdims=True)


def _dot(a, b):
    return jnp.dot(a, b, preferred_element_type=F32)


def _dot_nt(a, b):
    return lax.dot_general(a, b, (((1,), (1,)), ((), ())), preferred_element_type=F32)


def _dot_tn(a, b):
    return lax.dot_general(a, b, (((0,), (0,)), ((), ())), preferred_element_type=F32)


def _mesh_position():
    return lax.axis_index("x"), lax.axis_index("y"), lax.axis_index("c")


def _linear(p):
    return 4 * p[0] + 2 * p[1] + p[2]


def _all_gather(arrs, name, to_bf16, big):
    n = len(arrs)
    out_dtype = BF16 if to_bf16 else F32

    def body(*refs):
        in_refs, out_refs = refs[:n], refs[n:2 * n]
        rest = refs[2 * n:]
        if to_bf16:
            stage, rest = rest[:n], rest[n:]
            for a in range(n):
                stage[a][...] = in_refs[a][...].astype(BF16)
            srcs = stage
        else:
            srcs = in_refs
        send_sems, recv_sems, local_sems = rest
        x, y, c = _mesh_position()
        me, sibling = (x, y, c), (x, y, 1 - c)
        chips = [(1 - x, y), (x, 1 - y), (1 - x, 1 - y)]

        def slot(a, p):
            return out_refs[a].at[_linear(p)]

        def copy(k, a, block, to, src=None):
            return pltpu.make_async_remote_copy(
                src_ref=slot(a, block) if src is None else src,
                dst_ref=slot(a, block),
                send_sem=send_sems.at[k * n + a],
                recv_sem=recv_sems.at[k * n + a],
                device_id=to,
                device_id_type=MESH_ID,
            )

        mine = [pltpu.make_async_copy(srcs[a], slot(a, me), local_sems.at[a]) for a in range(n)]
        for cp in mine:
            cp.start()
        first = [copy(0, a, me, sibling, src=srcs[a]) for a in range(n)]
        for j, chip in enumerate(chips):
            first += [copy(1 + j, a, me, (*chip, c), src=srcs[a]) for a in range(n)]
        for cp in first:
            cp.start()
        passed = []
        for j, chip in enumerate(chips):
            for a in range(n):
                copy(1 + j, a, (*chip, c), me).wait_recv()
                fwd = copy(4 + j, a, (*chip, c), sibling)
                fwd.start()
                passed.append(fwd)
        for a in range(n):
            copy(0, a, sibling, me).wait_recv()
        for j, chip in enumerate(chips):
            for a in range(n):
                copy(4 + j, a, (*chip, 1 - c), me).wait_recv()
        for cp in first + passed:
            cp.wait_send()
        for cp in mine:
            cp.wait()

    vmem = pl.BlockSpec(memory_space=pltpu.VMEM)
    out_space = pl.BlockSpec(memory_space=pl.ANY) if big else vmem
    scratch = [pltpu.VMEM(a.shape, BF16) for a in arrs] if to_bf16 else []
    scratch += [pltpu.SemaphoreType.DMA((7 * n,)), pltpu.SemaphoreType.DMA((7 * n,)),
                pltpu.SemaphoreType.DMA((n,))]
    outs = pl.pallas_call(
        body, name=name,
        out_shape=[jax.ShapeDtypeStruct((N_DEV,) + a.shape, out_dtype) for a in arrs],
        in_specs=[vmem] * n, out_specs=[out_space] * n,
        scratch_shapes=scratch,
        compiler_params=_params(vmem=VMEM_LIMIT_LARGE if big else None),
    )(*arrs)
    return list(outs)


def _peer(k):
    x, y, c = _mesh_position()
    return (1 - x if k & 4 else x, 1 - y if k & 2 else y, 1 - c if k & 1 else c)


def _all_gather_small(arrs, name):
    n = len(arrs)

    def body(*refs):
        in_refs, out_refs = refs[:n], refs[n:2 * n]
        send_sems, recv_sems, local_sems = refs[2 * n:]
        me = _linear(_mesh_position())
        mine = [pltpu.make_async_copy(in_refs[a], out_refs[a].at[me], local_sems.at[a]) for a in range(n)]
        for cp in mine:
            cp.start()
        sends = []
        for k in range(1, N_DEV):
            for a in range(n):
                sends.append(pltpu.make_async_remote_copy(
                    src_ref=in_refs[a], dst_ref=out_refs[a].at[me],
                    send_sem=send_sems.at[(k - 1) * n + a], recv_sem=recv_sems.at[(k - 1) * n + a],
                    device_id=_peer(k), device_id_type=MESH_ID))
                sends[-1].start()
        for k in range(1, N_DEV):
            for a in range(n):
                pltpu.make_async_remote_copy(
                    src_ref=in_refs[a], dst_ref=out_refs[a].at[_linear(_peer(k))],
                    send_sem=send_sems.at[(k - 1) * n + a], recv_sem=recv_sems.at[(k - 1) * n + a],
                    device_id=_peer(k), device_id_type=MESH_ID).wait_recv()
        for cp in sends:
            cp.wait_send()
        for cp in mine:
            cp.wait()

    vmem = pl.BlockSpec(memory_space=pltpu.VMEM)
    return list(pl.pallas_call(
        body, name=name,
        out_shape=[jax.ShapeDtypeStruct((N_DEV,) + a.shape, F32) for a in arrs],
        in_specs=[vmem] * n, out_specs=[vmem] * n,
        scratch_shapes=[pltpu.SemaphoreType.DMA((7 * n,)), pltpu.SemaphoreType.DMA((7 * n,)),
                        pltpu.SemaphoreType.DMA((n,))],
    )(*arrs))


HBM_SPEC = pl.BlockSpec(memory_space=pltpu.HBM)
SEM_SPEC = pl.BlockSpec(memory_space=pltpu.SEMAPHORE)
DATAFLOW = pltpu.SideEffectType.DATAFLOW_SIDE_EFFECTING


def _exchange_start(src, name):
    r, c = src.shape[1:]

    def body(src_ref, land_ref, send_sems, recv_sems, src_thru, land_thru, token):
        for k in range(1, N_DEV):
            peer = _peer(k)
            pltpu.make_async_remote_copy(
                src_ref=src_ref.at[_linear(peer)], dst_ref=land_ref.at[k - 1],
                send_sem=send_sems.at[k - 1], recv_sem=recv_sems.at[k - 1],
                device_id=peer, device_id_type=MESH_ID).start()
        token[...] = jnp.zeros_like(token)

    land = lax.empty((N_DEV - 1, r, c), src.dtype)
    return pl.pallas_call(
        body, name=name,
        out_shape=(pltpu.SemaphoreType.DMA((N_DEV - 1,)), pltpu.SemaphoreType.DMA((N_DEV - 1,)),
                   pltpu.HBM(src.shape, src.dtype), pltpu.HBM(land.shape, land.dtype),
                   jax.ShapeDtypeStruct((8, 128), F32)),
        in_specs=(HBM_SPEC, HBM_SPEC),
        out_specs=(SEM_SPEC, SEM_SPEC, HBM_SPEC, HBM_SPEC, pl.BlockSpec(memory_space=pltpu.VMEM)),
        input_output_aliases={0: 2, 1: 3},
        compiler_params=pltpu.CompilerParams(has_side_effects=DATAFLOW),
    )(pltpu.with_memory_space_constraint(src, pltpu.HBM), pltpu.with_memory_space_constraint(land, pltpu.HBM))


def _exchange_wait(started, after, name):
    send_sems, recv_sems, src_thru, land_thru, _ = started

    def body(src_ref, land_ref, send_sems, recv_sems, *rest):
        for k in range(1, N_DEV):
            cp = pltpu.make_async_remote_copy(
                src_ref=src_ref.at[0], dst_ref=land_ref.at[k - 1],
                send_sem=send_sems.at[k - 1], recv_sem=recv_sems.at[k - 1],
                device_id=_peer(k), device_id_type=MESH_ID)
            cp.wait_send()
            cp.wait_recv()

    return pl.pallas_call(
        body, name=name,
        out_shape=(pltpu.HBM(src_thru.shape, src_thru.dtype), pltpu.HBM(land_thru.shape, land_thru.dtype)),
        in_specs=(HBM_SPEC, HBM_SPEC, SEM_SPEC, SEM_SPEC) + (pl.BlockSpec(memory_space=pl.ANY),) * len(after),
        out_specs=(HBM_SPEC, HBM_SPEC), input_output_aliases={0: 0, 1: 1},
        compiler_params=pltpu.CompilerParams(has_side_effects=DATAFLOW),
    )(src_thru, land_thru, send_sems, recv_sems, *after)


def _share_start(arrs, zones, name):
    n = len(arrs)

    def body(*refs):
        src_refs, zone_refs, sems = refs[:n], refs[n:2 * n], refs[2 * n:4 * n]
        me = _linear(_mesh_position())
        for a in range(n):
            for k in range(1, N_DEV):
                pltpu.make_async_remote_copy(
                    src_ref=src_refs[a], dst_ref=zone_refs[a].at[me],
                    send_sem=sems[2 * a].at[k - 1], recv_sem=sems[2 * a + 1].at[k - 1],
                    device_id=_peer(k), device_id_type=MESH_ID).start()

    outs = pl.pallas_call(
        body, name=name,
        out_shape=tuple(pltpu.SemaphoreType.DMA((N_DEV - 1,)) for _ in range(2 * n))
        + tuple(pltpu.HBM(a.shape, a.dtype) for a in arrs) + tuple(pltpu.HBM(z.shape, z.dtype) for z in zones),
        in_specs=(HBM_SPEC,) * (2 * n),
        out_specs=(SEM_SPEC,) * (2 * n) + (HBM_SPEC,) * (2 * n),
        input_output_aliases={i: 2 * n + i for i in range(2 * n)},
        compiler_params=pltpu.CompilerParams(has_side_effects=DATAFLOW),
    )(*[pltpu.with_memory_space_constraint(a, pltpu.HBM) for a in list(arrs) + list(zones)])
    return outs[:2 * n], outs[2 * n:3 * n], outs[3 * n:]


def _share_wait(started, after, name):
    sems, arrs, zones = started
    n = len(arrs)

    def body(*refs):
        src_refs, zone_refs, sem_refs = refs[:n], refs[n:2 * n], refs[2 * n:4 * n]
        for a in range(n):
            for k in range(1, N_DEV):
                cp = pltpu.make_async_remote_copy(
                    src_ref=src_refs[a], dst_ref=zone_refs[a].at[_linear(_peer(k))],
                    send_sem=sem_refs[2 * a].at[k - 1], recv_sem=sem_refs[2 * a + 1].at[k - 1],
                    device_id=_peer(k), device_id_type=MESH_ID)
                cp.wait_send()
                cp.wait_recv()

    outs = pl.pallas_call(
        body, name=name,
        out_shape=tuple(pltpu.HBM(a.shape, a.dtype) for a in arrs) + tuple(pltpu.HBM(z.shape, z.dtype) for z in zones),
        in_specs=(HBM_SPEC,) * (2 * n) + (SEM_SPEC,) * (2 * n) + (pl.BlockSpec(memory_space=pl.ANY),) * len(after),
        out_specs=(HBM_SPEC,) * (2 * n), input_output_aliases={i: i for i in range(2 * n)},
        compiler_params=pltpu.CompilerParams(has_side_effects=DATAFLOW),
    )(*arrs, *zones, *sems, *after)
    return list(outs[n:])


def _stage_blocks(arrs, after, name):
    n = len(arrs)

    def body(*refs):
        in_refs, out_refs, stage, sems = refs[:n], refs[n + 1:2 * n + 1], refs[2 * n + 1:3 * n + 1], refs[3 * n + 1]
        me = _linear(_mesh_position())
        copies = []
        for a in range(n):
            stage[a][...] = in_refs[a][...].astype(BF16)
            copies.append(pltpu.make_async_copy(stage[a], out_refs[a].at[me], sems.at[a]))
            copies[-1].start()
        for cp in copies:
            cp.wait()

    return list(pl.pallas_call(
        body, name=name,
        out_shape=[jax.ShapeDtypeStruct((N_DEV,) + a.shape, BF16) for a in arrs],
        in_specs=[pl.BlockSpec(memory_space=pltpu.VMEM)] * n + [pl.BlockSpec(memory_space=pl.ANY)],
        out_specs=[pl.BlockSpec(memory_space=pl.ANY)] * n,
        scratch_shapes=[pltpu.VMEM(a.shape, BF16) for a in arrs] + [pltpu.SemaphoreType.DMA((n,))],
        compiler_params=_params(vmem=VMEM_LIMIT_LARGE),
    )(*arrs, after))


def _same_core_peers():
    x, y, c = _mesh_position()
    return [(x, y, 1 - c), (1 - x, y, c), (x, 1 - y, c), (1 - x, 1 - y, c)]


def _gather_start(bufs, name):
    n = len(bufs)

    def body(*refs):
        buf_refs, rest = refs[:n], refs[n:]
        sems, token = rest[:2 * n], rest[-1]
        me = _linear(_mesh_position())
        for a in range(n):
            for k, peer in enumerate(_same_core_peers()):
                pltpu.make_async_remote_copy(
                    src_ref=buf_refs[a].at[me], dst_ref=buf_refs[a].at[me],
                    send_sem=sems[2 * a].at[k], recv_sem=sems[2 * a + 1].at[k],
                    device_id=peer, device_id_type=MESH_ID).start()
        token[...] = jnp.zeros_like(token)

    outs = pl.pallas_call(
        body, name=name,
        out_shape=tuple(pltpu.SemaphoreType.DMA((4,)) for _ in range(2 * n))
        + tuple(pltpu.HBM(b.shape, b.dtype) for b in bufs) + (jax.ShapeDtypeStruct((8, 128), F32),),
        in_specs=(HBM_SPEC,) * n,
        out_specs=(SEM_SPEC,) * (2 * n) + (HBM_SPEC,) * n + (pl.BlockSpec(memory_space=pltpu.VMEM),),
        input_output_aliases={a: 2 * n + a for a in range(n)},
        compiler_params=pltpu.CompilerParams(has_side_effects=DATAFLOW),
    )(*[pltpu.with_memory_space_constraint(b, pltpu.HBM) for b in bufs])
    return outs[:2 * n], outs[2 * n:3 * n], outs[3 * n]


def _gather_wait(sems, bufs, after, name):
    n = len(bufs)

    def body(*refs):
        buf_refs, sem_refs = refs[:n], refs[n:3 * n]
        x, y, c = _mesh_position()
        me = _linear((x, y, c))
        for a in range(n):
            for k, peer in enumerate(_same_core_peers()):
                cp = pltpu.make_async_remote_copy(
                    src_ref=buf_refs[a].at[me], dst_ref=buf_refs[a].at[_linear(peer)],
                    send_sem=sem_refs[2 * a].at[k], recv_sem=sem_refs[2 * a + 1].at[k],
                    device_id=peer, device_id_type=MESH_ID)
                cp.wait_send()
                cp.wait_recv()

    return list(pl.pallas_call(
        body, name=name,
        out_shape=tuple(pltpu.HBM(b.shape, b.dtype) for b in bufs),
        in_specs=(HBM_SPEC,) * n + (SEM_SPEC,) * (2 * n) + (pl.BlockSpec(memory_space=pl.ANY),) * len(after),
        out_specs=(HBM_SPEC,) * n, input_output_aliases={a: a for a in range(n)},
        compiler_params=pltpu.CompilerParams(has_side_effects=DATAFLOW),
    )(*bufs, *sems, *after))


def _gather_pass_on(bufs, name):
    n = len(bufs)

    def body(*refs):
        out_refs = refs[n:2 * n]
        send_sems, recv_sems = refs[2 * n:]
        x, y, c = _mesh_position()
        sibling = (x, y, 1 - c)
        chips = [(1 - x, y), (x, 1 - y), (1 - x, 1 - y)]
        copies = []
        for a in range(n):
            for j, chip in enumerate(chips):
                block = out_refs[a].at[_linear((*chip, c))]
                copies.append(pltpu.make_async_remote_copy(
                    src_ref=block, dst_ref=block, send_sem=send_sems.at[3 * a + j], recv_sem=recv_sems.at[3 * a + j],
                    device_id=sibling, device_id_type=MESH_ID))
                copies[-1].start()
        for a in range(n):
            for j, chip in enumerate(chips):
                copies[3 * a + j].wait_send()
                theirs = out_refs[a].at[_linear((*chip, 1 - c))]
                pltpu.make_async_remote_copy(
                    src_ref=theirs, dst_ref=theirs, send_sem=send_sems.at[3 * a + j], recv_sem=recv_sems.at[3 * a + j],
                    device_id=sibling, device_id_type=MESH_ID).wait_recv()

    hbm = pl.BlockSpec(memory_space=pl.ANY)
    return list(pl.pallas_call(
        body, name=name,
        out_shape=[jax.ShapeDtypeStruct(b.shape, b.dtype) for b in bufs],
        in_specs=[hbm] * n, out_specs=[hbm] * n, input_output_aliases={a: a for a in range(n)},
        scratch_shapes=[pltpu.SemaphoreType.DMA((3 * n,)), pltpu.SemaphoreType.DMA((3 * n,))],
    )(*bufs))


def _silu_rows(c):
    def body(c_ref, o_ref):
        v = c_ref[...]
        o_ref[...] = v * _sigmoid(v)

    return pl.pallas_call(body, name="cond_silu", out_shape=jax.ShapeDtypeStruct(c.shape, F32))(c)


def _mod_columns(cond_all, w_ada, b_cols):
    def body(c_ref, w_ref, b_ref, o_ref):
        o_ref[...] = _dot(c_ref[...], w_ref[...]) + b_ref[...]

    return pl.pallas_call(body, name="mod_columns",
                          out_shape=jax.ShapeDtypeStruct((N_DEV, w_ada.shape[1]), F32))(cond_all, w_ada, b_cols)


def _in_proj(x, mod, g_norm1, w_in, tm):
    s = x.shape[0]

    def body(x_ref, mod_ref, g_ref, w_ref, h_ref, q_ref, kv_ref, gb_ref, gc_ref, xc_ref):
        xf = x_ref[...]
        n = xf * _rsqrt_mean_sq(xf) * g_ref[...]
        h = (n * (1.0 + mod_ref[SC1:SC1 + 1, :]) + mod_ref[SH1:SH1 + 1, :]).astype(BF16)
        h_ref[...] = h
        p = _dot_nt(h, w_ref[...])
        q_ref[...] = p[:, 0:512].astype(BF16)
        kv_ref[...] = p[:, 512:768].astype(BF16)
        gb_ref[...] = p[:, 768:1280].astype(BF16)
        gc_ref[...] = p[:, 1280:1792].astype(BF16)
        xc_ref[...] = p[:, 1792:2304].astype(BF16)

    return pl.pallas_call(
        body, name="in_proj", grid=(s // tm,),
        in_specs=[_rows(tm, D_MODEL), _full((8, D_MODEL)), _full((1, D_MODEL)), _full((IN_PROJ_WIDTH, D_MODEL))],
        out_specs=[_rows(tm, D_MODEL), _rows(tm, 512), _rows(tm, 256), _rows(tm, 512), _rows(tm, 512), _rows(tm, 512)],
        out_shape=[jax.ShapeDtypeStruct((s, D_MODEL), BF16), jax.ShapeDtypeStruct((s, 512), BF16),
                   jax.ShapeDtypeStruct((s, 256), BF16), jax.ShapeDtypeStruct((s, 512), BF16),
                   jax.ShapeDtypeStruct((s, 512), BF16), jax.ShapeDtypeStruct((s, 512), BF16)],
        compiler_params=_params(("arbitrary",), VMEM_LIMIT_LARGE),
    )(x, mod, g_norm1, w_in)


def _t5_bucket(dist):
    max_exact = N_BUCKETS // 2
    is_small = dist < max_exact
    d = jnp.maximum(dist, 1).astype(F32)
    large = max_exact + (jnp.log(d / max_exact) / math.log(MAX_DISTANCE / max_exact)
                         * (N_BUCKETS - max_exact)).astype(jnp.int32)
    large = jnp.minimum(large, N_BUCKETS - 1)
    return jnp.where(is_small, dist, large)


def _bucket_table():
    qi = jnp.arange(BLOCK, dtype=jnp.int32)[:, None]
    sj = jnp.arange(2 * BLOCK, dtype=jnp.int32)[None, :]
    return _t5_bucket(jnp.maximum(qi + BLOCK - sj, 0))


def _window_mask():
    qi = lax.broadcasted_iota(jnp.int32, (BLOCK, 2 * BLOCK), 0)
    sj = lax.broadcasted_iota(jnp.int32, (BLOCK, 2 * BLOCK), 1)
    dist = qi + BLOCK - sj
    return (dist >= 0) & (dist < BLOCK)


def _bias_table(rel_bias, bucket):
    def body(rb_ref, bk_ref, o_ref):
        bk = bk_ref[...]
        inside = _window_mask()
        for h in range(N_Q_HEADS):
            acc = jnp.zeros((BLOCK, 2 * BLOCK), F32)
            for b in range(N_BUCKETS):
                acc = jnp.where(bk == b, rb_ref[b, h], acc)
            o_ref[h] = jnp.where(inside, acc, NEG_INF)

    return pl.pallas_call(
        body, name="bias_table",
        in_specs=[pl.BlockSpec(memory_space=pltpu.SMEM), pl.BlockSpec(memory_space=pltpu.VMEM)],
        out_shape=jax.ShapeDtypeStruct((N_Q_HEADS, BLOCK, 2 * BLOCK), F32),
    )(rel_bias, bucket)


def _load_kv_window(kv_ref, n):
    prev = jnp.maximum(n - 1, 0)
    kvw = jnp.concatenate([kv_ref[pl.ds(pl.multiple_of(prev * BLOCK, BLOCK), BLOCK), :],
                           kv_ref[pl.ds(pl.multiple_of(n * BLOCK, BLOCK), BLOCK), :]], axis=0)
    k, v = kvw[:, 0:128], kvw[:, 128:256]
    k_sw = pltpu.roll(k.astype(F32), 64, 1).astype(BF16)
    v_sw = pltpu.roll(v.astype(F32), 64, 1).astype(BF16)
    return (k, k_sw), (v, v_sw)


def _conv_taps(gc, xc, gc_prev, xc_prev, n):
    u = gc * xc
    before = jnp.where(n > 0, gc_prev.astype(F32) * xc_prev.astype(F32), 0.0)
    last = before.shape[0] - 1
    row = lax.broadcasted_iota(jnp.int32, u.shape, 0)
    u1 = jnp.where(row == 0, before[last:last + 1, :], pltpu.roll(u, 1, 0))
    u2 = jnp.where(row == 0, before[last - 1:last, :],
                   jnp.where(row == 1, before[last:last + 1, :], pltpu.roll(u, 2, 0)))
    return u, u1, u2


def _mixer_fwd(q, kv, gb, gc, xc, bias, sinks, conv_w, g_attn, g_conv):
    s = q.shape[0]
    nb = s // BLOCK

    per_step = min(MIXER_BLOCKS, nb)
    tile = per_step * BLOCK

    def one_block(n, rows, before, sink_ref, q_ref, kv_ref, gb_ref, gc_ref, xc_ref, bias_ref, cw_ref, ga_ref,
                  gcv_ref, attn_ref, merged_ref, lse_ref):
        ks, vs = _load_kv_window(kv_ref, n)
        lane = lax.broadcasted_iota(jnp.int32, (BLOCK, BLOCK), 1)
        low = lane < HEAD_DIM
        col = lax.broadcasted_iota(jnp.int32, (BLOCK, 2 * BLOCK), 1)
        no_prev = (col < BLOCK) & (n == 0)
        lse_all = jnp.zeros((BLOCK, BLOCK), F32)
        pairs = []
        for p in range(4):
            qp = q_ref[rows, 128 * p:128 * (p + 1)].astype(F32)
            kvh = p // 2
            res = []
            for e in range(2):
                h = 2 * p + e
                qm = jnp.where(low if e == 0 else ~low, qp, 0.0).astype(BF16)
                sw = 0 if kvh == e else 1
                sc = _dot_nt(qm, ks[sw]) * SCALE + bias_ref[h]
                sc = jnp.where(no_prev, NEG_INF, sc)
                sink = sink_ref[h]
                m = jnp.maximum(jnp.max(sc, axis=-1, keepdims=True), sink)
                pe = jnp.exp(sc - m)
                den = jnp.sum(pe, axis=-1, keepdims=True) + jnp.exp(sink - m)
                res.append(_dot(pe.astype(BF16), vs[sw]) / den)
                lse_all = lse_all + jnp.where(lane == h, m + jnp.log(den), 0.0)
            pairs.append(jnp.where(low, res[0], res[1]))
        attn = jnp.concatenate(pairs, axis=1)
        attn_ref[rows, :] = attn
        lse_ref[rows, :] = lse_all
        u, u1, u2 = _conv_taps(gc_ref[rows, :].astype(F32), xc_ref[rows, :].astype(F32), before[0], before[1], n)
        cw = cw_ref[...]
        cv = gb_ref[rows, :].astype(F32) * (cw[0:1, :] * u2 + cw[1:2, :] * u1 + cw[2:3, :] * u)
        an = attn * _rsqrt_mean_sq(attn) * ga_ref[...]
        cn = cv * _rsqrt_mean_sq(cv) * gcv_ref[...]
        merged_ref[rows, :] = jnp.concatenate([an, cn], axis=1).astype(BF16)

    def body(sink_ref, q_ref, kv_ref, gb_ref, gc_ref, xc_ref, gcp_ref, xcp_ref, *rest):
        step = pl.program_id(0)
        for sub in range(per_step):
            rows = slice(sub * BLOCK, (sub + 1) * BLOCK)
            ahead = slice(sub * BLOCK - PREV_ROWS, sub * BLOCK)
            before = (gcp_ref[...], xcp_ref[...]) if sub == 0 else (gc_ref[ahead, :], xc_ref[ahead, :])
            one_block(step * per_step + sub, rows, before, sink_ref, q_ref, kv_ref, gb_ref, gc_ref, xc_ref, *rest)

    blk = lambda w: pl.BlockSpec((tile, w), lambda n: (n, 0))
    prev8 = pl.BlockSpec((PREV_ROWS, 512), lambda n: (jnp.maximum(n * (tile // PREV_ROWS) - 1, 0), 0))
    return pl.pallas_call(
        body, name="mixer_fwd", grid=(nb // per_step,),
        in_specs=[pl.BlockSpec(memory_space=pltpu.SMEM), blk(512), _full((s, 256)), blk(512), blk(512), blk(512),
                  prev8, prev8, _full((N_Q_HEADS, BLOCK, 2 * BLOCK)), _full((3, 512)), _full((1, 512)),
                  _full((1, 512))],
        out_specs=[blk(512), blk(1024), blk(128)],
        out_shape=[jax.ShapeDtypeStruct((s, 512), F32), jax.ShapeDtypeStruct((s, 1024), BF16),
                   jax.ShapeDtypeStruct((s, 128), F32)],
        compiler_params=_params(("arbitrary",)),
    )(sinks, q, kv, gb, gc, xc, gc, xc, bias, conv_w, g_attn, g_conv)


def _out_proj(merged, x, mod, w_out, tm):
    s = x.shape[0]

    def body(m_ref, x_ref, mod_ref, w_ref, o_ref, x1_ref):
        o = _dot(m_ref[...], w_ref[...])
        o_ref[...] = o.astype(BF16)
        x1_ref[...] = x_ref[...] + mod_ref[G1:G1 + 1, :] * o

    return pl.pallas_call(
        body, name="out_proj", grid=(s // tm,),
        in_specs=[_rows(tm, D_MODEL), _rows(tm, D_MODEL), _full((8, D_MODEL)), _full((D_MODEL, D_MODEL))],
        out_specs=[_rows(tm, D_MODEL), _rows(tm, D_MODEL)],
        out_shape=[jax.ShapeDtypeStruct((s, D_MODEL), BF16), jax.ShapeDtypeStruct((s, D_MODEL), F32)],
        compiler_params=_params(("arbitrary",)),
    )(merged, x, mod, w_out)


def _resident(shape):
    nd = len(shape)
    return pl.BlockSpec(shape, lambda *_: (0,) * nd, pipeline_mode=pl.Buffered(1))


def _ffn(x1, o1, mod, g_norm2, w_gu, w_down, w_out, g_final, target, tm):
    s = x1.shape[0]
    chunk = D_FF // FFN_CHUNKS

    def body(x_ref, o1_ref, mod_ref, g_ref, wgu_ref, wd_ref, wo_ref, gf_ref, t_ref,
             h_ref, act_ref, do_ref, dgu_ref, dx1_ref, do1_ref, dm_ref, small_ref):
        @pl.when(pl.program_id(0) == 0)
        def _():
            small_ref[...] = jnp.zeros_like(small_ref)

        xf = x_ref[...]
        n = xf * _rsqrt_mean_sq(xf) * g_ref[...]
        h = (n * (1.0 + mod_ref[SC2:SC2 + 1, :]) + mod_ref[SH2:SH2 + 1, :]).astype(BF16)
        h_ref[...] = h
        gates, ups, o = [], [], None
        for j in range(FFN_CHUNKS):
            lo = j * chunk
            gate = _dot_nt(h, wgu_ref[lo:lo + chunk, :])
            up = _dot_nt(h, wgu_ref[D_FF + lo:D_FF + lo + chunk, :])
            sg = _sigmoid(gate)
            act = (gate * sg * up).astype(BF16)
            act_ref[:, lo:lo + chunk] = act
            gates.append((up * (sg * (1.0 + gate * (1.0 - sg)))).astype(BF16))
            ups.append((gate * sg).astype(BF16))
            part = _dot(act, wd_ref[lo:lo + chunk, :])
            o = part if o is None else o + part
        g2 = mod_ref[G2:G2 + 1, :]
        x2 = xf + g2 * o
        r = _rsqrt_mean_sq(x2)
        xn = x2 * r
        gf = gf_ref[...]
        err = xn * gf - t_ref[...]
        dy = err * (1.0 / D_MODEL)
        dxn = dy * gf
        dx2 = r * (dxn - xn * jnp.mean(dxn * xn, axis=-1, keepdims=True))
        small_ref[4:5, :] += _colsum(dy * xn)
        small_ref[5:6, :] += _colsum(err * err)
        small_ref[3:4, :] += _colsum(dx2 * o)
        do = (dx2 * g2).astype(BF16)
        do_ref[...] = do
        dh = None
        for j in range(FFN_CHUNKS):
            lo = j * chunk
            dact = _dot_nt(do, wd_ref[lo:lo + chunk, :])
            dgate = (dact * gates[j].astype(F32)).astype(BF16)
            dup = (dact * ups[j].astype(F32)).astype(BF16)
            dgu_ref[:, lo:lo + chunk] = dgate
            dgu_ref[:, D_FF + lo:D_FF + lo + chunk] = dup
            part = _dot(dgate, wgu_ref[lo:lo + chunk, :]) + _dot(dup, wgu_ref[D_FF + lo:D_FF + lo + chunk, :])
            dh = part if dh is None else dh + part
        dx1 = dx2 + _norm_mod_bwd(dh, xf, g_ref[...], mod_ref[SC2:SC2 + 1, :], small_ref)
        dx1_ref[...] = dx1.astype(BF16)
        small_ref[7:8, :] += _colsum(dx1 * o1_ref[...].astype(F32))
        do1 = (dx1 * mod_ref[G1:G1 + 1, :]).astype(BF16)
        do1_ref[...] = do1
        dm_ref[...] = _dot_nt(do1, wo_ref[...]).astype(BF16)

        @pl.when(pl.program_id(0) == pl.num_programs(0) - 1)
        def _():
            total = jnp.sum(small_ref[5:6, :], axis=-1, keepdims=True) * (0.5 / D_MODEL)
            small_ref[6:7, :] = jnp.broadcast_to(total, (1, D_MODEL))

    narrow = jax.ShapeDtypeStruct((s, D_MODEL), BF16)
    return pl.pallas_call(
        body, name="ffn", grid=(s // tm,),
        in_specs=[_rows(tm, D_MODEL), _rows(tm, D_MODEL), _full((8, D_MODEL)), _full((1, D_MODEL)),
                  _resident((2 * D_FF, D_MODEL)), _resident((D_FF, D_MODEL)), _resident((D_MODEL, D_MODEL)),
                  _full((1, D_MODEL)), _rows(tm, D_MODEL)],
        out_specs=[_rows(tm, D_MODEL), _rows(tm, D_FF), _rows(tm, D_MODEL), _rows(tm, 2 * D_FF), _rows(tm, D_MODEL),
                   _rows(tm, D_MODEL), _rows(tm, D_MODEL), _full((8, D_MODEL))],
        out_shape=[narrow, jax.ShapeDtypeStruct((s, D_FF), BF16), narrow, jax.ShapeDtypeStruct((s, 2 * D_FF), BF16),
                   narrow, narrow, narrow, jax.ShapeDtypeStruct((8, D_MODEL), F32)],
        compiler_params=_params(("arbitrary",), VMEM_LIMIT_LARGE),
    )(x1, o1, mod, g_norm2, w_gu, w_down, w_out, g_final, target)


def _norm_mod_bwd(dh, xf, g, scale_row, small_ref):
    r = _rsqrt_mean_sq(xf)
    xn = xf * r
    small_ref[0:1, :] += _colsum(dh)
    small_ref[1:2, :] += _colsum(dh * (xn * g))
    dn = dh * (1.0 + scale_row)
    small_ref[2:3, :] += _colsum(dn * xn)
    dxn = dn * g
    return r * (dxn - xn * jnp.mean(dxn * xn, axis=-1, keepdims=True))


def _group_norm_bwd(dm, a, g):
    r = _rsqrt_mean_sq(a)
    an = a * r
    dan = dm * g
    return r * (dan - an * jnp.mean(dan * an, axis=-1, keepdims=True)), _colsum(dm * an)


def _mixer_bwd(after, q, kv, gb, gc, xc, bias, sinks, conv_w, g_attn, g_conv, attn, lse, dmerged):
    s = q.shape[0]
    nb = s // BLOCK

    per_step = min(MIXER_BLOCKS, nb)
    tile = per_step * BLOCK
    steps = nb // per_step

    def one_block(n, rows, before, nxt, sink_ref, q_ref, kv_ref, gb_ref, gc_ref, xc_ref, bias_ref, cw_ref, ga_ref,
                  gcv_ref, attn_ref, lse_ref, dm_ref, dproj_ref, dbias_ref, dsink_ref, small_ref):
        next_dy, next_dkv = nxt
        dm = dm_ref[rows, :].astype(F32)
        gbv, gcv_, xcv = gb_ref[rows, :].astype(F32), gc_ref[rows, :].astype(F32), xc_ref[rows, :].astype(F32)
        u, u1, u2 = _conv_taps(gcv_, xcv, before[0], before[1], n)
        cw = cw_ref[...]
        yv = cw[0:1, :] * u2 + cw[1:2, :] * u1 + cw[2:3, :] * u
        dcv, dg_conv = _group_norm_bwd(dm[:, 512:1024], gbv * yv, gcv_ref[...])
        small_ref[1:2, :] += dg_conv
        dproj_ref[rows, 768:1280] = (dcv * yv).astype(BF16)
        dy = dcv * gbv
        row = lax.broadcasted_iota(jnp.int32, dy.shape, 0)
        d1 = jnp.where(row == BLOCK - 1, next_dy[0:1, :], pltpu.roll(dy, BLOCK - 1, 0))
        d2 = jnp.where(row == BLOCK - 2, next_dy[0:1, :],
                       jnp.where(row == BLOCK - 1, next_dy[1:2, :], pltpu.roll(dy, BLOCK - 2, 0)))
        du = cw[2:3, :] * dy + cw[1:2, :] * d1 + cw[0:1, :] * d2
        dproj_ref[rows, 1280:1792] = (du * xcv).astype(BF16)
        dproj_ref[rows, 1792:2304] = (du * gcv_).astype(BF16)
        small_ref[2:3, :] += _colsum(dy * u2)
        small_ref[3:4, :] += _colsum(dy * u1)
        small_ref[4:5, :] += _colsum(dy * u)

        attn_v = attn_ref[rows, :]
        dout, dg_attn = _group_norm_bwd(dm[:, 0:512], attn_v, ga_ref[...])
        small_ref[0:1, :] += dg_attn
        ks, vs = _load_kv_window(kv_ref, n)
        lane = lax.broadcasted_iota(jnp.int32, (BLOCK, BLOCK), 1)
        low = lane < HEAD_DIM
        col = lax.broadcasted_iota(jnp.int32, (BLOCK, 2 * BLOCK), 1)
        no_prev = (col < BLOCK) & (n == 0)
        lse_all = lse_ref[rows, :]
        dsink = jnp.zeros((BLOCK, BLOCK), F32)
        dq_pairs = []
        dk_groups, dv_groups = [], []
        for kvh in range(2):
            ds_rows, pr_rows, q_rows, do_rows = [], [], [], []
            for p in (2 * kvh, 2 * kvh + 1):
                qp = q_ref[rows, 128 * p:128 * (p + 1)].astype(F32)
                do_p = dout[:, 128 * p:128 * (p + 1)]
                prod = do_p * attn_v[:, 128 * p:128 * (p + 1)]
                res = []
                for e in range(2):
                    h = 2 * p + e
                    half = low if e == 0 else ~low
                    qm = jnp.where(half, qp, 0.0).astype(BF16)
                    dom = jnp.where(half, do_p, 0.0).astype(BF16)
                    delta = jnp.sum(jnp.where(half, prod, 0.0), axis=-1, keepdims=True)
                    lse_h = jnp.sum(jnp.where(lane == h, lse_all, 0.0), axis=-1, keepdims=True)
                    sw = 0 if kvh == e else 1
                    sc = _dot_nt(qm, ks[sw]) * SCALE + bias_ref[h]
                    sc = jnp.where(no_prev, NEG_INF, sc)
                    pr = jnp.exp(sc - lse_h)
                    dp = _dot_nt(dom, vs[sw])
                    ds = pr * (dp - delta)
                    dbias_ref[h] += ds
                    dsink = dsink + jnp.where(lane == h, -jnp.exp(sink_ref[h] - lse_h) * delta, 0.0)
                    dsb = ds.astype(BF16)
                    res.append(_dot(dsb, ks[sw]) * SCALE)
                    ds_rows.append(dsb)
                    pr_rows.append(pr.astype(BF16))
                    q_rows.append(qm)
                    do_rows.append(dom)
                dq_pairs.append(jnp.where(low, res[0], res[1]))
            dk_g = _dot_tn(jnp.concatenate(ds_rows, axis=0), jnp.concatenate(q_rows, axis=0)) * SCALE
            dv_g = _dot_tn(jnp.concatenate(pr_rows, axis=0), jnp.concatenate(do_rows, axis=0))
            dk_groups.append(dk_g + pltpu.roll(dk_g, 64, 1))
            dv_groups.append(dv_g + pltpu.roll(dv_g, 64, 1))
        dproj_ref[rows, 0:512] = jnp.concatenate(dq_pairs, axis=1).astype(BF16)
        dsink_ref[...] += dsink
        low_kv = lax.broadcasted_iota(jnp.int32, (2 * BLOCK, BLOCK), 1) < HEAD_DIM
        dkv_win = jnp.concatenate([jnp.where(low_kv, dk_groups[0], dk_groups[1]),
                                   jnp.where(low_kv, dv_groups[0], dv_groups[1])], axis=1)
        dproj_ref[rows, 512:768] = (dkv_win[BLOCK:2 * BLOCK, :] + next_dkv).astype(BF16)
        return dy[0:8, :], dkv_win[0:BLOCK, :]

    def body(sink_ref, q_ref, kv_ref, gb_ref, gc_ref, xc_ref, gcp_ref, xcp_ref, *rest):
        refs, dy_ref, dkv_ref = rest[:-2], rest[-2], rest[-1]
        dbias_ref, dsink_ref, small_ref = refs[8], refs[9], refs[10]
        step = pl.program_id(0)

        @pl.when(step == 0)
        def _():
            dbias_ref[...] = jnp.zeros_like(dbias_ref)
            dsink_ref[...] = jnp.zeros_like(dsink_ref)
            small_ref[...] = jnp.zeros_like(small_ref)
            dy_ref[...] = jnp.zeros_like(dy_ref)
            dkv_ref[...] = jnp.zeros_like(dkv_ref)

        nxt = (dy_ref[...], dkv_ref[...])
        for sub in reversed(range(per_step)):
            rows = slice(sub * BLOCK, (sub + 1) * BLOCK)
            ahead = slice(sub * BLOCK - PREV_ROWS, sub * BLOCK)
            before = (gcp_ref[...], xcp_ref[...]) if sub == 0 else (gc_ref[ahead, :], xc_ref[ahead, :])
            nxt = one_block((steps - 1 - step) * per_step + sub, rows, before, nxt,
                            sink_ref, q_ref, kv_ref, gb_ref, gc_ref, xc_ref, *refs)
        dy_ref[...], dkv_ref[...] = nxt

        @pl.when(step == steps - 1)
        def _():
            small_ref[5:6, :] = jnp.concatenate([_colsum(dsink_ref[...]), jnp.zeros((1, 512 - BLOCK), F32)], axis=1)

    blk = lambda w: pl.BlockSpec((tile, w), lambda t: (steps - 1 - t, 0))
    prev8 = pl.BlockSpec((PREV_ROWS, 512),
                         lambda t: (jnp.maximum((steps - 1 - t) * (tile // PREV_ROWS) - 1, 0), 0))
    bf = lambda w: jax.ShapeDtypeStruct((s, w), BF16)
    return pl.pallas_call(
        _coming_behind(body), name="mixer_bwd", grid=(steps,),
        in_specs=[ANY_SPEC, pl.BlockSpec(memory_space=pltpu.SMEM), blk(512), _full((s, 256)), blk(512), blk(512), blk(512),
                  prev8, prev8, _full((N_Q_HEADS, BLOCK, 2 * BLOCK)), _full((3, 512)), _full((1, 512)),
                  _full((1, 512)), blk(512), blk(128), blk(1024)],
        out_specs=[blk(IN_PROJ_WIDTH), _full((N_Q_HEADS, BLOCK, 2 * BLOCK)), _full((BLOCK, BLOCK)), _full((8, 512))],
        out_shape=[bf(IN_PROJ_WIDTH), jax.ShapeDtypeStruct((N_Q_HEADS, BLOCK, 2 * BLOCK), F32),
                   jax.ShapeDtypeStruct((BLOCK, BLOCK), F32), jax.ShapeDtypeStruct((8, 512), F32)],
        scratch_shapes=[pltpu.VMEM((8, 512), F32), pltpu.VMEM((BLOCK, 2 * KV_WIDTH), F32)],
        compiler_params=_params(("arbitrary",), VMEM_LIMIT_LARGE),
    )(after, sinks, q, kv, gb, gc, xc, gc, xc, bias, conv_w, g_attn, g_conv, attn, lse, dmerged)


def _in_proj_bwd(after, dproj, x, dx1, mod, g_norm1, w_in, tm):
    s = x.shape[0]

    def body(dproj_ref, x_ref, dx1_ref, mod_ref, g_ref, w_ref, dx_ref, small_ref):
        @pl.when(pl.program_id(0) == 0)
        def _():
            small_ref[...] = jnp.zeros_like(small_ref)

        dh = _dot(dproj_ref[...], w_ref[...])
        dx_ref[...] = dx1_ref[...].astype(F32) + _norm_mod_bwd(dh, x_ref[...], g_ref[...], mod_ref[SC1:SC1 + 1, :],
                                                               small_ref)

    return pl.pallas_call(
        _coming_behind(body), name="in_proj_bwd", grid=(s // tm,),
        in_specs=[ANY_SPEC, _rows(tm, IN_PROJ_WIDTH), _rows(tm, D_MODEL), _rows(tm, D_MODEL), _full((8, D_MODEL)),
                  _full((1, D_MODEL)), _full((IN_PROJ_WIDTH, D_MODEL))],
        out_specs=[_rows(tm, D_MODEL), _full((8, D_MODEL))],
        out_shape=[jax.ShapeDtypeStruct((s, D_MODEL), F32), jax.ShapeDtypeStruct((8, D_MODEL), F32)],
        compiler_params=_params(("arbitrary",), VMEM_LIMIT_LARGE),
    )(after, dproj, x, dx1, mod, g_norm1, w_in)


def _weight_grad(a, b, tk, ts, name, after=None):
    s, k = a.shape
    n = b.shape[1]
    nt = s // ts
    extra = [] if after is None else [after]

    def body(a_ref, b_ref, *rest):
        o_ref, acc_ref = rest[-2:]
        t = pl.program_id(1)
        @pl.when(t == 0)
        def _():
            acc_ref[...] = jnp.zeros_like(acc_ref)

        acc = acc_ref[...] + _dot_tn(a_ref[...], b_ref[...])
        acc_ref[...] = acc
        o_ref[...] = acc.astype(BF16)

    return pl.pallas_call(
        body, name=name, grid=(k // tk, nt),
        in_specs=[pl.BlockSpec((ts, tk), lambda i, t: (t, i)), pl.BlockSpec((ts, n), lambda i, t: (t, 0))]
        + [ANY_SPEC] * len(extra),
        out_specs=pl.BlockSpec((tk, n), lambda i, t: (i, 0)),
        out_shape=jax.ShapeDtypeStruct((k, n), BF16),
        scratch_shapes=[pltpu.VMEM((tk, n), F32)],
        compiler_params=_params(("arbitrary", "arbitrary"), VMEM_LIMIT_LARGE),
    )(a, b, *extra)


def _rel_bias_grad(dbias, bucket):
    def body(db_ref, bk_ref, o_ref, rows_ref):
        bk = bk_ref[...]
        for b in range(N_BUCKETS):
            sel = (bk == b).astype(F32)
            for h in range(N_Q_HEADS):
                rows_ref[N_BUCKETS * h + b:N_BUCKETS * h + b + 1, :] = _colsum(db_ref[h] * sel)
        head = lax.broadcasted_iota(jnp.int32, (N_BUCKETS, N_Q_HEADS), 1)
        out = jnp.zeros((N_BUCKETS, N_Q_HEADS), F32)
        for h in range(N_Q_HEADS):
            per_bucket = jnp.sum(rows_ref[N_BUCKETS * h:N_BUCKETS * (h + 1), :], axis=-1, keepdims=True)
            out = out + jnp.where(head == h, per_bucket, 0.0)
        o_ref[...] = out

    return pl.pallas_call(
        body, name="rel_bias_grad",
        out_shape=jax.ShapeDtypeStruct((N_BUCKETS, N_Q_HEADS), F32),
        scratch_shapes=[pltpu.VMEM((N_BUCKETS * N_Q_HEADS, 2 * BLOCK), F32)],
    )(dbias, bucket)


def _lanes_from(x, start, width):
    n = x.shape[1]
    return pltpu.roll(x, (n - start) % n, 1)[:, 0:width]


def _w_ada_grad(me, cond_all, packed_all, cols):
    def body(me_ref, c_ref, p_ref, o_ref):
        dmod = jnp.concatenate([p_ref[k][:, OFF_DMOD:OFF_DMOD + N_MOD * D_MODEL] for k in range(N_DEV)], axis=0)
        mine = _lanes_from(dmod, me_ref[0] * cols, cols)
        pad = lambda a: jnp.concatenate([a, jnp.zeros((128 - N_DEV, a.shape[1]), F32)], axis=0)
        o_ref[...] = _dot_tn(pad(c_ref[...]), pad(mine))

    vmem = pl.BlockSpec(memory_space=pltpu.VMEM)
    return pl.pallas_call(body, name="w_ada_grad",
                          in_specs=[pl.BlockSpec(memory_space=pltpu.SMEM), vmem, vmem],
                          out_shape=jax.ShapeDtypeStruct((cond_all.shape[1], cols), F32))(me, cond_all, packed_all)


SMALL_PARAMS = (("rel_bias", None), ("b_ada", (OFF_DMOD, N_MOD * D_MODEL)), ("g_norm1", (OFF_GN1, D_MODEL)),
                ("sinks", (OFF_SINK, N_Q_HEADS)), ("conv_w", None), ("g_attn_out", (OFF_GATT, ATTN_WIDTH)),
                ("g_conv_out", (OFF_GCV, CONV_WIDTH)), ("g_norm2", (OFF_GN2, D_MODEL)),
                ("g_final", (OFF_GFIN, D_MODEL)))


def _small_update(me, packed_all, rel_all, state, after):
    n_p = len(SMALL_PARAMS)
    flat = [a for triple in state for a in triple]
    conv_cols = state[4][0].shape[1]

    def body(me_ref, p_ref, r_ref, *refs):
        ins = refs[:3 * n_p]
        loss_ref, outs = refs[3 * n_p + len(after)], refs[3 * n_p + len(after) + 1:]
        small, rel = p_ref[0], r_ref[0]
        for k in range(1, N_DEV):
            small = small + p_ref[k]
            rel = rel + r_ref[k]
        loss_ref[...] = small[:, OFF_LOSS:OFF_LOSS + 128]
        taps = jnp.concatenate([small[:, OFF_CONVW + CONV_WIDTH * j:OFF_CONVW + CONV_WIDTH * (j + 1)]
                                for j in range(3)] + [jnp.zeros((5, CONV_WIDTH), F32)], axis=0)
        conv_g = _lanes_from(taps, me_ref[0] * conv_cols, conv_cols)[0:3, :]
        for i, (name, lanes) in enumerate(SMALL_PARAMS):
            g = rel if name == "rel_bias" else conv_g if name == "conv_w" else small[:, lanes[0]:lanes[0] + lanes[1]]
            w_ref, m_ref, v_ref = ins[3 * i:3 * i + 3]
            outs[4 * i][...] = g
            outs[4 * i + 1][...], outs[4 * i + 2][...], outs[4 * i + 3][...] = _adam_math(
                w_ref[...], g, m_ref[...], v_ref[...])

    vmem = pl.BlockSpec(memory_space=pltpu.VMEM)
    out_shape = [jax.ShapeDtypeStruct((1, 128), F32)]
    for w, _, _ in state:
        out_shape += [jax.ShapeDtypeStruct(w.shape, F32)] * 4
    outs = pl.pallas_call(
        body, name="small_update",
        in_specs=[pl.BlockSpec(memory_space=pltpu.SMEM), vmem, vmem] + [vmem] * len(flat)
        + [pl.BlockSpec(memory_space=pl.ANY)] * len(after),
        out_shape=out_shape,
    )(me, packed_all, rel_all, *flat, *after)
    return outs[0], [tuple(outs[1 + 4 * i:5 + 4 * i]) for i in range(n_p)]


def _adam_math(w, g, m, v):
    m = ADAM_B1 * m + (1.0 - ADAM_B1) * g
    v = ADAM_B2 * v + (1.0 - ADAM_B2) * (g * g)
    m_hat = m / (1.0 - ADAM_B1 ** ADAM_STEP)
    v_hat = v / (1.0 - ADAM_B2 ** ADAM_STEP)
    delta = -ADAM_LR * (m_hat / (jnp.sqrt(v_hat) + ADAM_EPS) + ADAM_WD * w)
    return delta, m, v


def _adamw_parts(w, m, v, local, land, me, tr, name):
    r, c = w.shape

    def body(me_ref, w_ref, m_ref, v_ref, own_ref, land_ref, g_ref, d_ref, mo_ref, vo_ref):
        g = own_ref[0].astype(F32)
        for k in range(N_DEV - 1):
            g = g + land_ref[k].astype(F32)
        g_ref[...] = g
        d_ref[...], mo_ref[...], vo_ref[...] = _adam_math(w_ref[...], g, m_ref[...], v_ref[...])

    tile = pl.BlockSpec((tr, c), lambda i, me_ref: (i, 0))
    return pl.pallas_call(
        body, name=name,
        grid_spec=pltpu.PrefetchScalarGridSpec(
            num_scalar_prefetch=1, grid=(r // tr,),
            in_specs=[tile, tile, tile, pl.BlockSpec((1, tr, c), lambda i, me_ref: (me_ref[0], i, 0)),
                      pl.BlockSpec((N_DEV - 1, tr, c), lambda i, me_ref: (0, i, 0))],
            out_specs=[tile] * 4),
        out_shape=[jax.ShapeDtypeStruct((r, c), F32)] * 4,
        compiler_params=_params(("arbitrary",)),
    )(me, w, m, v, local, land)


def _adamw(w, m, v, g, tr, name):
    r, c = w.shape

    def body(w_ref, m_ref, v_ref, g_ref, d_ref, mo_ref, vo_ref):
        d_ref[...], mo_ref[...], vo_ref[...] = _adam_math(w_ref[...], g_ref[...], m_ref[...], v_ref[...])

    tile = pl.BlockSpec((tr, c), lambda i: (i, 0))
    return pl.pallas_call(
        body, name=name, grid=(r // tr,),
        in_specs=[tile] * 4, out_specs=[tile] * 3,
        out_shape=[jax.ShapeDtypeStruct((r, c), F32)] * 3,
        compiler_params=_params(("arbitrary",)),
    )(w, m, v, g)


def _behind(a, token):
    return a + token[0:a.shape[0], 0:1]


def _local_step(x, target, mod, w_in_t, weights_out_gu, weights_down, rel_bias, g_norm1, sinks, conv_w, g_attn,
                g_conv, g_norm2, g_final, exchange):
    s = x.shape[0]
    tm = min(512, s)
    tm_small = min(256, s)
    bucket = _bucket_table()
    bias = _bias_table(rel_bias, bucket)

    h, q, kv, gb, gc, xc = _in_proj(x, mod, g_norm1, w_in_t, tm)
    attn, merged, lse = _mixer_fwd(q, kv, gb, gc, xc, bias, sinks, conv_w, g_attn, g_conv)
    w_out, w_gu_t = weights_out_gu(merged)
    o1, x1 = _out_proj(merged, x, mod, w_out, tm)
    w_down = weights_down(x1)
    h2, act, do2, dgu, dx1, do1, dmerged, sm_2 = _ffn(x1, o1, mod, g_norm2, w_gu_t, w_down, w_out, g_final, target,
                                                      tm_small)
    ts = min(WEIGHT_GRAD_ROWS, s)
    tok_down = exchange("w_down", _weight_grad(act, do2, D_FF // 2, ts, "w_down_grad"))
    tok_gu = exchange("w_gu", _weight_grad(dgu, h2, D_FF // 2, ts, "w_gu_grad", after=tok_down))
    tok_out = exchange("w_out", _weight_grad(merged, do1, D_MODEL, ts, "w_out_grad", after=tok_gu))
    dproj, dbias, dsink, sm_mix = _mixer_bwd(
        tok_out, q, kv, gb, gc, xc, bias, sinks, conv_w, g_attn, g_conv, attn, lse, dmerged)
    tok_in = exchange("w_in", _weight_grad(dproj, h, IN_PROJ_WIDTH // 2, ts, "w_in_grad"))
    dx, sm_1 = _in_proj_bwd(tok_in, dproj, x, dx1, mod, g_norm1, w_in_t, tm)
    d_rel = _rel_bias_grad(dbias, bucket)

    packed = jnp.concatenate([
        sm_1[0], sm_1[1], sm_2[7], sm_2[0], sm_2[1], sm_2[3],
        sm_1[2],
        sm_mix[5, 0:128],
        sm_mix[0], sm_mix[1],
        sm_2[2],
        sm_2[4],
        sm_mix[2], sm_mix[3], sm_mix[4],
        sm_2[6, 0:128],
    ])[None, :]
    return dx, packed, d_rel


def kernel(x, c, rel_bias, w_ada, b_ada, g_norm1, w_in, sinks, conv_w, g_attn_out, g_conv_out, w_out, g_norm2, w_gu, w_down, g_final, loss_target, m_rel_bias, m_w_ada, m_b_ada, m_g_norm1, m_w_in, m_sinks, m_conv_w, m_g_attn_out, m_g_conv_out, m_w_out, m_g_norm2, m_w_gu, m_w_down, m_g_final, v_rel_bias, v_w_ada, v_b_ada, v_g_norm1, v_w_in, v_sinks, v_conv_w, v_g_attn_out, v_g_conv_out, v_w_out, v_g_norm2, v_w_gu, v_w_down, v_g_final):
    me = _linear(_mesh_position())
    me_arr = jnp.reshape(me, (1,)).astype(jnp.int32)
    ada_cols = w_ada.shape[2]
    tm = min(512, x.shape[1])

    cond = _silu_rows(c)
    cond_all, conv_w_all = _all_gather_small([cond, conv_w[0]], "gather_cond")
    cond_all = cond_all[:, 0, :]
    conv_cols = conv_w.shape[2]
    conv_w_full = conv_w_all.transpose(1, 0, 2).reshape(3, CONV_WIDTH)
    b_cols = lax.dynamic_slice_in_dim(b_ada, me * ada_cols, ada_cols, axis=1)
    mod_cols = _mod_columns(cond_all, w_ada[0], b_cols)
    mod_all = _all_gather_small([mod_cols], "gather_mod")[0]
    mod = lax.dynamic_index_in_dim(mod_all, me, axis=1, keepdims=False).reshape(N_MOD, D_MODEL)
    mod = jnp.concatenate([mod, jnp.zeros((2, D_MODEL), F32)], axis=0)

    w_in_t = _all_gather([w_in[0].T], "gather_w_in", to_bf16=True, big=True)[0].reshape(IN_PROJ_WIDTH, D_MODEL)
    gather_sems, staged, gather_token = _gather_start(
        _stage_blocks([w_out[0], w_gu[0].T, w_down[0]], w_in_t, "stage_weights"), "gather_start_weights")
    mod = _behind(mod, gather_token)

    def weights_out_gu(after):
        got = _gather_pass_on(_gather_wait(gather_sems[0:4], staged[0:2], [after], "gather_wait_out_gu"),
                              "gather_pass_on_out_gu")
        return got[0].reshape(D_MODEL, D_MODEL), got[1].reshape(2 * D_FF, D_MODEL)

    def weights_down(after):
        got = _gather_pass_on(_gather_wait(gather_sems[4:6], staged[2:3], [after], "gather_wait_down"),
                              "gather_pass_on_down")
        return got[0].reshape(D_FF, D_MODEL)

    started = {}

    def exchange(name, dw):
        st = _exchange_start(dw.reshape(N_DEV, dw.shape[0] // N_DEV, dw.shape[1]), "exchange_start_" + name)
        started[name] = st
        return st[4]

    dx, packed, d_rel = _local_step(
        x[0], loss_target[0], mod, w_in_t, weights_out_gu, weights_down, rel_bias, g_norm1, sinks[0], conv_w_full,
        g_attn_out, g_conv_out, g_norm2, g_final[None, :], exchange)

    def zone(a):
        return lax.dynamic_update_slice(jnp.zeros((N_DEV,) + a.shape, F32), a[None], (me,) + (0,) * a.ndim)

    shared = _share_start([packed, d_rel], [zone(packed), zone(d_rel)], "share_small_start")

    def finish(name, after, w, m, v, tr):
        src, land = _exchange_wait(started[name], after, "exchange_wait_" + name)
        return _adamw_parts(w, m, v, src, land, me_arr, tr, "adamw_" + name)

    g_down, d_down, nm_down, nv_down = finish("w_down", [shared[2][0]], w_down[0], m_w_down[0], v_w_down[0], 176)
    g_gu, d_gu, nm_gu, nv_gu = finish("w_gu", [nv_down], w_gu[0].T, m_w_gu[0].T, v_w_gu[0].T, 352)
    g_out, d_out, nm_out, nv_out = finish("w_out", [nv_gu], w_out[0], m_w_out[0], v_w_out[0], 128)

    packed_all, rel_all = _share_wait(shared, [nv_out], "share_small_wait")
    g_ada = _w_ada_grad(me_arr, cond_all, packed_all, ada_cols)
    d_ada, nm_ada, nv_ada = _adamw(w_ada[0], m_w_ada[0], v_w_ada[0], g_ada, 256, "adamw_w_ada")
    as_rows = {"conv_w": lambda a: a[0], "g_final": lambda a: a[None, :]}
    small_state = {
        "rel_bias": (rel_bias, m_rel_bias, v_rel_bias), "b_ada": (b_ada, m_b_ada, v_b_ada),
        "g_norm1": (g_norm1, m_g_norm1, v_g_norm1), "sinks": (sinks, m_sinks, v_sinks),
        "conv_w": (conv_w, m_conv_w, v_conv_w), "g_attn_out": (g_attn_out, m_g_attn_out, v_g_attn_out),
        "g_conv_out": (g_conv_out, m_g_conv_out, v_g_conv_out), "g_norm2": (g_norm2, m_g_norm2, v_g_norm2),
        "g_final": (g_final, m_g_final, v_g_final),
    }
    state = [tuple(as_rows.get(name, lambda a: a)(a) for a in small_state[name]) for name, _ in SMALL_PARAMS]
    loss_row, small_out = _small_update(me_arr, packed_all, rel_all, state, [])
    loss = loss_row[0, 0]
    small_res = {name: tuple(a.reshape(small_state[name][0].shape) for a in res)
                 for (name, _), res in zip(SMALL_PARAMS, small_out)}

    g_in, d_in, nm_in, nv_in = finish("w_in", [loss_row, nv_ada], w_in[0].T, m_w_in[0].T, v_w_in[0].T, 144)

    big = {
        "w_ada": (g_ada[None], d_ada[None], nm_ada[None], nv_ada[None]),
        "w_in": (g_in.T[None], d_in.T[None], nm_in.T[None], nv_in.T[None]),
        "w_out": (g_out[None], d_out[None], nm_out[None], nv_out[None]),
        "w_gu": (g_gu.T[None], d_gu.T[None], nm_gu.T[None], nv_gu.T[None]),
        "w_down": (g_down[None], d_down[None], nm_down[None], nv_down[None]),
    }
    order = ["rel_bias", "w_ada", "b_ada", "g_norm1", "w_in", "sinks", "conv_w", "g_attn_out", "g_conv_out", "w_out",
             "g_norm2", "w_gu", "w_down", "g_final"]
    results = [big[k] if k in big else small_res[k] for k in order]
    return (loss, dx[None], *[r[0] for r in results], *[r[1] for r in results], *[r[2] for r in results],
            *[r[3] for r in results])
```

```python
import functools
import math

import jax
import jax.numpy as jnp
from jax import lax
from jax.experimental import pallas as pl
from jax.experimental.pallas import tpu as pltpu

F32 = jnp.float32
BF16 = jnp.bfloat16

D_MODEL = 1024
HEAD_DIM = 64
N_Q_HEADS = 8
ATTN_WIDTH = 512
KV_WIDTH = 128
CONV_WIDTH = 512
IN_PROJ_WIDTH = 2304
D_FF = 2816
N_MOD = 6
N_BUCKETS = 32
MAX_DISTANCE = 128
BLOCK = 128
EPS = 1e-6
NEG_INF = -1e30
SCALE = HEAD_DIM ** -0.5
N_DEV = 8

ADAM_LR = 0.001
ADAM_B1 = 0.9
ADAM_B2 = 0.999
ADAM_EPS = 1e-08
ADAM_WD = 0.01
ADAM_STEP = 10

SH1, SC1, G1, SH2, SC2, G2 = range(6)

VMEM_LIMIT_LARGE = 60 * 1024 * 1024
WEIGHT_GRAD_ROWS = 2048
FFN_CHUNKS = 1
PREV_ROWS = 16
MIXER_BLOCKS = 4
MESH_ID = pl.DeviceIdType.MESH

OFF_DMOD = 0
OFF_GN1 = OFF_DMOD + N_MOD * D_MODEL
OFF_SINK = OFF_GN1 + D_MODEL
OFF_GATT = OFF_SINK + 128
OFF_GCV = OFF_GATT + ATTN_WIDTH
OFF_GN2 = OFF_GCV + CONV_WIDTH
OFF_GFIN = OFF_GN2 + D_MODEL
OFF_CONVW = OFF_GFIN + D_MODEL
OFF_LOSS = OFF_CONVW + 3 * CONV_WIDTH
PACKED = OFF_LOSS + 128


def _params(sem=None, vmem=None):
    return pltpu.CompilerParams(dimension_semantics=sem, vmem_limit_bytes=vmem)


def _coming_behind(body):
    def skipping(after_ref, *refs):
        body(*refs)

    return skipping


ANY_SPEC = pl.BlockSpec(memory_space=pl.ANY)


def _full(shape):
    nd = len(shape)
    return pl.BlockSpec(shape, lambda *_: (0,) * nd)


def _rows(tm, width):
    return pl.BlockSpec((tm, width), lambda i, *_: (i, 0))


def _sigmoid(x):
    return 1.0 / (1.0 + jnp.exp(-x))


def _rsqrt_mean_sq(x):
    return lax.rsqrt(jnp.mean(x * x, axis=-1, keepdims=True) + EPS)


def _colsum(x):
    return jnp.sum(x, axis=0, keepdims=True)


def _dot(a, b):
    return jnp.dot(a, b, preferred_element_type=F32)


def _dot_nt(a, b):
    return lax.dot_general(a, b, (((1,), (1,)), ((), ())), preferred_element_type=F32)


def _dot_tn(a, b):
    return lax.dot_general(a, b, (((0,), (0,)), ((), ())), preferred_element_type=F32)


def _mesh_position():
    return lax.axis_index("x"), lax.axis_index("y"), lax.axis_index("c")


def _linear(p):
    return 4 * p[0] + 2 * p[1] + p[2]


def _all_gather(arrs, name, to_bf16, big):
    n = len(arrs)
    out_dtype = BF16 if to_bf16 else F32

    def body(*refs):
        in_refs, out_refs = refs[:n], refs[n:2 * n]
        rest = refs[2 * n:]
        if to_bf16:
            stage, rest = rest[:n], rest[n:]
            for a in range(n):
                stage[a][...] = in_refs[a][...].astype(BF16)
            srcs = stage
        else:
            srcs = in_refs
        send_sems, recv_sems, local_sems = rest
        x, y, c = _mesh_position()
        me, sibling = (x, y, c), (x, y, 1 - c)
        chips = [(1 - x, y), (x, 1 - y), (1 - x, 1 - y)]

        def slot(a, p):
            return out_refs[a].at[_linear(p)]

        def copy(k, a, block, to, src=None):
            return pltpu.make_async_remote_copy(
                src_ref=slot(a, block) if src is None else src,
                dst_ref=slot(a, block),
                send_sem=send_sems.at[k * n + a],
                recv_sem=recv_sems.at[k * n + a],
                device_id=to,
                device_id_type=MESH_ID,
            )

        mine = [pltpu.make_async_copy(srcs[a], slot(a, me), local_sems.at[a]) for a in range(n)]
        for cp in mine:
            cp.start()
        first = [copy(0, a, me, sibling, src=srcs[a]) for a in range(n)]
        for j, chip in enumerate(chips):
            first += [copy(1 + j, a, me, (*chip, c), src=srcs[a]) for a in range(n)]
        for cp in first:
            cp.start()
        passed = []
        for j, chip in enumerate(chips):
            for a in range(n):
                copy(1 + j, a, (*chip, c), me).wait_recv()
                fwd = copy(4 + j, a, (*chip, c), sibling)
                fwd.start()
                passed.append(fwd)
        for a in range(n):
            copy(0, a, sibling, me).wait_recv()
        for j, chip in enumerate(chips):
            for a in range(n):
                copy(4 + j, a, (*chip, 1 - c), me).wait_recv()
        for cp in first + passed:
            cp.wait_send()
        for cp in mine:
            cp.wait()

    vmem = pl.BlockSpec(memory_space=pltpu.VMEM)
    out_space = pl.BlockSpec(memory_space=pl.ANY) if big else vmem
    scratch = [pltpu.VMEM(a.shape, BF16) for a in arrs] if to_bf16 else []
    scratch += [pltpu.SemaphoreType.DMA((7 * n,)), pltpu.SemaphoreType.DMA((7 * n,)),
                pltpu.SemaphoreType.DMA((n,))]
    outs = pl.pallas_call(
        body, name=name,
        out_shape=[jax.ShapeDtypeStruct((N_DEV,) + a.shape, out_dtype) for a in arrs],
        in_specs=[vmem] * n, out_specs=[out_space] * n,
        scratch_shapes=scratch,
        compiler_params=_params(vmem=VMEM_LIMIT_LARGE if big else None),
    )(*arrs)
    return list(outs)


def _peer(k):
    x, y, c = _mesh_position()
    return (1 - x if k & 4 else x, 1 - y if k & 2 else y, 1 - c if k & 1 else c)


def _all_gather_small(arrs, name):
    n = len(arrs)

    def body(*refs):
        in_refs, out_refs = refs[:n], refs[n:2 * n]
        send_sems, recv_sems, local_sems = refs[2 * n:]
        me = _linear(_mesh_position())
        mine = [pltpu.make_async_copy(in_refs[a], out_refs[a].at[me], local_sems.at[a]) for a in range(n)]
        for cp in mine:
            cp.start()
        sends = []
        for k in range(1, N_DEV):
            for a in range(n):
                sends.append(pltpu.make_async_remote_copy(
                    src_ref=in_refs[a], dst_ref=out_refs[a].at[me],
                    send_sem=send_sems.at[(k - 1) * n + a], recv_sem=recv_sems.at[(k - 1) * n + a],
                    device_id=_peer(k), device_id_type=MESH_ID))
                sends[-1].start()
        for k in range(1, N_DEV):
            for a in range(n):
                pltpu.make_async_remote_copy(
                    src_ref=in_refs[a], dst_ref=out_refs[a].at[_linear(_peer(k))],
                    send_sem=send_sems.at[(k - 1) * n + a], recv_sem=recv_sems.at[(k - 1) * n + a],
                    device_id=_peer(k), device_id_type=MESH_ID).wait_recv()
        for cp in sends:
            cp.wait_send()
        for cp in mine:
            cp.wait()

    vmem = pl.BlockSpec(memory_space=pltpu.VMEM)
    return list(pl.pallas_call(
        body, name=name,
        out_shape=[jax.ShapeDtypeStruct((N_DEV,) + a.shape, F32) for a in arrs],
        in_specs=[vmem] * n, out_specs=[vmem] * n,
        scratch_shapes=[pltpu.SemaphoreType.DMA((7 * n,)), pltpu.SemaphoreType.DMA((7 * n,)),
                        pltpu.SemaphoreType.DMA((n,))],
    )(*arrs))


HBM_SPEC = pl.BlockSpec(memory_space=pltpu.HBM)
SEM_SPEC = pl.BlockSpec(memory_space=pltpu.SEMAPHORE)
DATAFLOW = pltpu.SideEffectType.DATAFLOW_SIDE_EFFECTING


def _exchange_start(src, name):
    r, c = src.shape[1:]

    def body(src_ref, land_ref, send_sems, recv_sems, src_thru, land_thru, token):
        for k in range(1, N_DEV):
            peer = _peer(k)
            pltpu.make_async_remote_copy(
                src_ref=src_ref.at[_linear(peer)], dst_ref=land_ref.at[k - 1],
                send_sem=send_sems.at[k - 1], recv_sem=recv_sems.at[k - 1],
                device_id=peer, device_id_type=MESH_ID).start()
        token[...] = jnp.zeros_like(token)

    land = lax.empty((N_DEV - 1, r, c), src.dtype)
    return pl.pallas_call(
        body, name=name,
        out_shape=(pltpu.SemaphoreType.DMA((N_DEV - 1,)), pltpu.SemaphoreType.DMA((N_DEV - 1,)),
                   pltpu.HBM(src.shape, src.dtype), pltpu.HBM(land.shape, land.dtype),
                   jax.ShapeDtypeStruct((8, 128), F32)),
        in_specs=(HBM_SPEC, HBM_SPEC),
        out_specs=(SEM_SPEC, SEM_SPEC, HBM_SPEC, HBM_SPEC, pl.BlockSpec(memory_space=pltpu.VMEM)),
        input_output_aliases={0: 2, 1: 3},
        compiler_params=pltpu.CompilerParams(has_side_effects=DATAFLOW),
    )(pltpu.with_memory_space_constraint(src, pltpu.HBM), pltpu.with_memory_space_constraint(land, pltpu.HBM))


def _exchange_wait(started, after, name):
    send_sems, recv_sems, src_thru, land_thru, _ = started

    def body(src_ref, land_ref, send_sems, recv_sems, *rest):
        for k in range(1, N_DEV):
            cp = pltpu.make_async_remote_copy(
                src_ref=src_ref.at[0], dst_ref=land_ref.at[k - 1],
                send_sem=send_sems.at[k - 1], recv_sem=recv_sems.at[k - 1],
                device_id=_peer(k), device_id_type=MESH_ID)
            cp.wait_send()
            cp.wait_recv()

    return pl.pallas_call(
        body, name=name,
        out_shape=(pltpu.HBM(src_thru.shape, src_thru.dtype), pltpu.HBM(land_thru.shape, land_thru.dtype)),
        in_specs=(HBM_SPEC, HBM_SPEC, SEM_SPEC, SEM_SPEC) + (pl.BlockSpec(memory_space=pl.ANY),) * len(after),
        out_specs=(HBM_SPEC, HBM_SPEC), input_output_aliases={0: 0, 1: 1},
        compiler_params=pltpu.CompilerParams(has_side_effects=DATAFLOW),
    )(src_thru, land_thru, send_sems, recv_sems, *after)


def _share_start(arrs, zones, name):
    n = len(arrs)

    def body(*refs):
        src_refs, zone_refs, sems = refs[:n], refs[n:2 * n], refs[2 * n:4 * n]
        me = _linear(_mesh_position())
        for a in range(n):
            for k in range(1, N_DEV):
                pltpu.make_async_remote_copy(
                    src_ref=src_refs[a], dst_ref=zone_refs[a].at[me],
                    send_sem=sems[2 * a].at[k - 1], recv_sem=sems[2 * a + 1].at[k - 1],
                    device_id=_peer(k), device_id_type=MESH_ID).start()

    outs = pl.pallas_call(
        body, name=name,
        out_shape=tuple(pltpu.SemaphoreType.DMA((N_DEV - 1,)) for _ in range(2 * n))
        + tuple(pltpu.HBM(a.shape, a.dtype) for a in arrs) + tuple(pltpu.HBM(z.shape, z.dtype) for z in zones),
        in_specs=(HBM_SPEC,) * (2 * n),
        out_specs=(SEM_SPEC,) * (2 * n) + (HBM_SPEC,) * (2 * n),
        input_output_aliases={i: 2 * n + i for i in range(2 * n)},
        compiler_params=pltpu.CompilerParams(has_side_effects=DATAFLOW),
    )(*[pltpu.with_memory_space_constraint(a, pltpu.HBM) for a in list(arrs) + list(zones)])
    return outs[:2 * n], outs[2 * n:3 * n], outs[3 * n:]


def _share_wait(started, after, name):
    sems, arrs, zones = started
    n = len(arrs)

    def body(*refs):
        src_refs, zone_refs, sem_refs = refs[:n], refs[n:2 * n], refs[2 * n:4 * n]
        for a in range(n):
            for k in range(1, N_DEV):
                cp = pltpu.make_async_remote_copy(
                    src_ref=src_refs[a], dst_ref=zone_refs[a].at[_linear(_peer(k))],
                    send_sem=sem_refs[2 * a].at[k - 1], recv_sem=sem_refs[2 * a + 1].at[k - 1],
                    device_id=_peer(k), device_id_type=MESH_ID)
                cp.wait_send()
                cp.wait_recv()

    outs = pl.pallas_call(
        body, name=name,
        out_shape=tuple(pltpu.HBM(a.shape, a.dtype) for a in arrs) + tuple(pltpu.HBM(z.shape, z.dtype) for z in zones),
        in_specs=(HBM_SPEC,) * (2 * n) + (SEM_SPEC,) * (2 * n) + (pl.BlockSpec(memory_space=pl.ANY),) * len(after),
        out_specs=(HBM_SPEC,) * (2 * n), input_output_aliases={i: i for i in range(2 * n)},
        compiler_params=pltpu.CompilerParams(has_side_effects=DATAFLOW),
    )(*arrs, *zones, *sems, *after)
    return list(outs[n:])


def _stage_blocks(arrs, after, name):
    n = len(arrs)

    def body(*refs):
        in_refs, out_refs, stage, sems = refs[:n], refs[n + 1:2 * n + 1], refs[2 * n + 1:3 * n + 1], refs[3 * n + 1]
        me = _linear(_mesh_position())
        copies = []
        for a in range(n):
            stage[a][...] = in_refs[a][...].astype(BF16)
            copies.append(pltpu.make_async_copy(stage[a], out_refs[a].at[me], sems.at[a]))
            copies[-1].start()
        for cp in copies:
            cp.wait()

    return list(pl.pallas_call(
        body, name=name,
        out_shape=[jax.ShapeDtypeStruct((N_DEV,) + a.shape, BF16) for a in arrs],
        in_specs=[pl.BlockSpec(memory_space=pltpu.VMEM)] * n + [pl.BlockSpec(memory_space=pl.ANY)],
        out_specs=[pl.BlockSpec(memory_space=pl.ANY)] * n,
        scratch_shapes=[pltpu.VMEM(a.shape, BF16) for a in arrs] + [pltpu.SemaphoreType.DMA((n,))],
        compiler_params=_params(vmem=VMEM_LIMIT_LARGE),
    )(*arrs, after))


def _same_core_peers():
    x, y, c = _mesh_position()
    return [(x, y, 1 - c), (1 - x, y, c), (x, 1 - y, c), (1 - x, 1 - y, c)]


def _gather_start(bufs, name):
    n = len(bufs)

    def body(*refs):
        buf_refs, rest = refs[:n], refs[n:]
        sems, token = rest[:2 * n], rest[-1]
        me = _linear(_mesh_position())
        for a in range(n):
            for k, peer in enumerate(_same_core_peers()):
                pltpu.make_async_remote_copy(
                    src_ref=buf_refs[a].at[me], dst_ref=buf_refs[a].at[me],
                    send_sem=sems[2 * a].at[k], recv_sem=sems[2 * a + 1].at[k],
                    device_id=peer, device_id_type=MESH_ID).start()
        token[...] = jnp.zeros_like(token)

    outs = pl.pallas_call(
        body, name=name,
        out_shape=tuple(pltpu.SemaphoreType.DMA((4,)) for _ in range(2 * n))
        + tuple(pltpu.HBM(b.shape, b.dtype) for b in bufs) + (jax.ShapeDtypeStruct((8, 128), F32),),
        in_specs=(HBM_SPEC,) * n,
        out_specs=(SEM_SPEC,) * (2 * n) + (HBM_SPEC,) * n + (pl.BlockSpec(memory_space=pltpu.VMEM),),
        input_output_aliases={a: 2 * n + a for a in range(n)},
        compiler_params=pltpu.CompilerParams(has_side_effects=DATAFLOW),
    )(*[pltpu.with_memory_space_constraint(b, pltpu.HBM) for b in bufs])
    return outs[:2 * n], outs[2 * n:3 * n], outs[3 * n]


def _gather_wait(sems, bufs, after, name):
    n = len(bufs)

    def body(*refs):
        buf_refs, sem_refs = refs[:n], refs[n:3 * n]
        x, y, c = _mesh_position()
        me = _linear((x, y, c))
        for a in range(n):
            for k, peer in enumerate(_same_core_peers()):
                cp = pltpu.make_async_remote_copy(
                    src_ref=buf_refs[a].at[me], dst_ref=buf_refs[a].at[_linear(peer)],
                    send_sem=sem_refs[2 * a].at[k], recv_sem=sem_refs[2 * a + 1].at[k],
                    device_id=peer, device_id_type=MESH_ID)
                cp.wait_send()
                cp.wait_recv()

    return list(pl.pallas_call(
        body, name=name,
        out_shape=tuple(pltpu.HBM(b.shape, b.dtype) for b in bufs),
        in_specs=(HBM_SPEC,) * n + (SEM_SPEC,) * (2 * n) + (pl.BlockSpec(memory_space=pl.ANY),) * len(after),
        out_specs=(HBM_SPEC,) * n, input_output_aliases={a: a for a in range(n)},
        compiler_params=pltpu.CompilerParams(has_side_effects=DATAFLOW),
    )(*bufs, *sems, *after))


def _gather_pass_on(bufs, name):
    n = len(bufs)

    def body(*refs):
        out_refs = refs[n:2 * n]
        send_sems, recv_sems = refs[2 * n:]
        x, y, c = _mesh_position()
        sibling = (x, y, 1 - c)
        chips = [(1 - x, y), (x, 1 - y), (1 - x, 1 - y)]
        copies = []
        for a in range(n):
            for j, chip in enumerate(chips):
                block = out_refs[a].at[_linear((*chip, c))]
                copies.append(pltpu.make_async_remote_copy(
                    src_ref=block, dst_ref=block, send_sem=send_sems.at[3 * a + j], recv_sem=recv_sems.at[3 * a + j],
                    device_id=sibling, device_id_type=MESH_ID))
                copies[-1].start()
        for a in range(n):
            for j, chip in enumerate(chips):
                copies[3 * a + j].wait_send()
                theirs = out_refs[a].at[_linear((*chip, 1 - c))]
                pltpu.make_async_remote_copy(
                    src_ref=theirs, dst_ref=theirs, send_sem=send_sems.at[3 * a + j], recv_sem=recv_sems.at[3 * a + j],
                    device_id=sibling, device_id_type=MESH_ID).wait_recv()

    hbm = pl.BlockSpec(memory_space=pl.ANY)
    return list(pl.pallas_call(
        body, name=name,
        out_shape=[jax.ShapeDtypeStruct(b.shape, b.dtype) for b in bufs],
        in_specs=[hbm] * n, out_specs=[hbm] * n, input_output_aliases={a: a for a in range(n)},
        scratch_shapes=[pltpu.SemaphoreType.DMA((3 * n,)), pltpu.SemaphoreType.DMA((3 * n,))],
    )(*bufs))


def _silu_rows(c):
    def body(c_ref, o_ref):
        v = c_ref[...]
        o_ref[...] = v * _sigmoid(v)

    return pl.pallas_call(body, name="cond_silu", out_shape=jax.ShapeDtypeStruct(c.shape, F32))(c)


def _mod_columns(cond_all, w_ada, b_cols):
    def body(c_ref, w_ref, b_ref, o_ref):
        o_ref[...] = _dot(c_ref[...], w_ref[...]) + b_ref[...]

    return pl.pallas_call(body, name="mod_columns",
                          out_shape=jax.ShapeDtypeStruct((N_DEV, w_ada.shape[1]), F32))(cond_all, w_ada, b_cols)


def _in_proj(x, mod, g_norm1, w_in, tm):
    s = x.shape[0]

    def body(x_ref, mod_ref, g_ref, w_ref, h_ref, q_ref, kv_ref, gb_ref, gc_ref, xc_ref):
        xf = x_ref[...]
        n = xf * _rsqrt_mean_sq(xf) * g_ref[...]
        h = (n * (1.0 + mod_ref[SC1:SC1 + 1, :]) + mod_ref[SH1:SH1 + 1, :]).astype(BF16)
        h_ref[...] = h
        p = _dot_nt(h, w_ref[...])
        q_ref[...] = p[:, 0:512].astype(BF16)
        kv_ref[...] = p[:, 512:768].astype(BF16)
        gb_ref[...] = p[:, 768:1280].astype(BF16)
        gc_ref[...] = p[:, 1280:1792].astype(BF16)
        xc_ref[...] = p[:, 1792:2304].astype(BF16)

    return pl.pallas_call(
        body, name="in_proj", grid=(s // tm,),
        in_specs=[_rows(tm, D_MODEL), _full((8, D_MODEL)), _full((1, D_MODEL)), _full((IN_PROJ_WIDTH, D_MODEL))],
        out_specs=[_rows(tm, D_MODEL), _rows(tm, 512), _rows(tm, 256), _rows(tm, 512), _rows(tm, 512), _rows(tm, 512)],
        out_shape=[jax.ShapeDtypeStruct((s, D_MODEL), BF16), jax.ShapeDtypeStruct((s, 512), BF16),
                   jax.ShapeDtypeStruct((s, 256), BF16), jax.ShapeDtypeStruct((s, 512), BF16),
                   jax.ShapeDtypeStruct((s, 512), BF16), jax.ShapeDtypeStruct((s, 512), BF16)],
        compiler_params=_params(("arbitrary",), VMEM_LIMIT_LARGE),
    )(x, mod, g_norm1, w_in)


def _t5_bucket(dist):
    max_exact = N_BUCKETS // 2
    is_small = dist < max_exact
    d = jnp.maximum(dist, 1).astype(F32)
    large = max_exact + (jnp.log(d / max_exact) / math.log(MAX_DISTANCE / max_exact)
                         * (N_BUCKETS - max_exact)).astype(jnp.int32)
    large = jnp.minimum(large, N_BUCKETS - 1)
    return jnp.where(is_small, dist, large)


def _bucket_table():
    qi = jnp.arange(BLOCK, dtype=jnp.int32)[:, None]
    sj = jnp.arange(2 * BLOCK, dtype=jnp.int32)[None, :]
    return _t5_bucket(jnp.maximum(qi + BLOCK - sj, 0))


def _window_mask():
    qi = lax.broadcasted_iota(jnp.int32, (BLOCK, 2 * BLOCK), 0)
    sj = lax.broadcasted_iota(jnp.int32, (BLOCK, 2 * BLOCK), 1)
    dist = qi + BLOCK - sj
    return (dist >= 0) & (dist < BLOCK)


def _bias_table(rel_bias, bucket):
    def body(rb_ref, bk_ref, o_ref):
        bk = bk_ref[...]
        inside = _window_mask()
        for h in range(N_Q_HEADS):
            acc = jnp.zeros((BLOCK, 2 * BLOCK), F32)
            for b in range(N_BUCKETS):
                acc = jnp.where(bk == b, rb_ref[b, h], acc)
            o_ref[h] = jnp.where(inside, acc, NEG_INF)

    return pl.pallas_call(
        body, name="bias_table",
        in_specs=[pl.BlockSpec(memory_space=pltpu.SMEM), pl.BlockSpec(memory_space=pltpu.VMEM)],
        out_shape=jax.ShapeDtypeStruct((N_Q_HEADS, BLOCK, 2 * BLOCK), F32),
    )(rel_bias, bucket)


def _load_kv_window(kv_ref, n):
    prev = jnp.maximum(n - 1, 0)
    kvw = jnp.concatenate([kv_ref[pl.ds(pl.multiple_of(prev * BLOCK, BLOCK), BLOCK), :],
                           kv_ref[pl.ds(pl.multiple_of(n * BLOCK, BLOCK), BLOCK), :]], axis=0)
    k, v = kvw[:, 0:128], kvw[:, 128:256]
    k_sw = pltpu.roll(k.astype(F32), 64, 1).astype(BF16)
    v_sw = pltpu.roll(v.astype(F32), 64, 1).astype(BF16)
    return (k, k_sw), (v, v_sw)


def _conv_taps(gc, xc, gc_prev, xc_prev, n):
    u = gc * xc
    before = jnp.where(n > 0, gc_prev.astype(F32) * xc_prev.astype(F32), 0.0)
    last = before.shape[0] - 1
    row = lax.broadcasted_iota(jnp.int32, u.shape, 0)
    u1 = jnp.where(row == 0, before[last:last + 1, :], pltpu.roll(u, 1, 0))
    u2 = jnp.where(row == 0, before[last - 1:last, :],
                   jnp.where(row == 1, before[last:last + 1, :], pltpu.roll(u, 2, 0)))
    return u, u1, u2


def _mixer_fwd(q, kv, gb, gc, xc, bias, sinks, conv_w, g_attn, g_conv):
    s = q.shape[0]
    nb = s // BLOCK

    per_step = min(MIXER_BLOCKS, nb)
    tile = per_step * BLOCK

    def one_block(n, rows, before, sink_ref, q_ref, kv_ref, gb_ref, gc_ref, xc_ref, bias_ref, cw_ref, ga_ref,
                  gcv_ref, attn_ref, merged_ref, lse_ref):
        ks, vs = _load_kv_window(kv_ref, n)
        lane = lax.broadcasted_iota(jnp.int32, (BLOCK, BLOCK), 1)
        low = lane < HEAD_DIM
        col = lax.broadcasted_iota(jnp.int32, (BLOCK, 2 * BLOCK), 1)
        no_prev = (col < BLOCK) & (n == 0)
        lse_all = jnp.zeros((BLOCK, BLOCK), F32)
        pairs = []
        for p in range(4):
            qp = q_ref[rows, 128 * p:128 * (p + 1)].astype(F32)
            kvh = p // 2
            res = []
            for e in range(2):
                h = 2 * p + e
                qm = jnp.where(low if e == 0 else ~low, qp, 0.0).astype(BF16)
                sw = 0 if kvh == e else 1
                sc = _dot_nt(qm, ks[sw]) * SCALE + bias_ref[h]
                sc = jnp.where(no_prev, NEG_INF, sc)
                sink = sink_ref[h]
                m = jnp.maximum(jnp.max(sc, axis=-1, keepdims=True), sink)
                pe = jnp.exp(sc - m)
                den = jnp.sum(pe, axis=-1, keepdims=True) + jnp.exp(sink - m)
                res.append(_dot(pe.astype(BF16), vs[sw]) / den)
                lse_all = lse_all + jnp.where(lane == h, m + jnp.log(den), 0.0)
            pairs.append(jnp.where(low, res[0], res[1]))
        attn = jnp.concatenate(pairs, axis=1)
        attn_ref[rows, :] = attn
        lse_ref[rows, :] = lse_all
        u, u1, u2 = _conv_taps(gc_ref[rows, :].astype(F32), xc_ref[rows, :].astype(F32), before[0], before[1], n)
        cw = cw_ref[...]
        cv = gb_ref[rows, :].astype(F32) * (cw[0:1, :] * u2 + cw[1:2, :] * u1 + cw[2:3, :] * u)
        an = attn * _rsqrt_mean_sq(attn) * ga_ref[...]
        cn = cv * _rsqrt_mean_sq(cv) * gcv_ref[...]
        merged_ref[rows, :] = jnp.concatenate([an, cn], axis=1).astype(BF16)

    def body(sink_ref, q_ref, kv_ref, gb_ref, gc_ref, xc_ref, gcp_ref, xcp_ref, *rest):
        step = pl.program_id(0)
        for sub in range(per_step):
            rows = slice(sub * BLOCK, (sub + 1) * BLOCK)
            ahead = slice(sub * BLOCK - PREV_ROWS, sub * BLOCK)
            before = (gcp_ref[...], xcp_ref[...]) if sub == 0 else (gc_ref[ahead, :], xc_ref[ahead, :])
            one_block(step * per_step + sub, rows, before, sink_ref, q_ref, kv_ref, gb_ref, gc_ref, xc_ref, *rest)

    blk = lambda w: pl.BlockSpec((tile, w), lambda n: (n, 0))
    prev8 = pl.BlockSpec((PREV_ROWS, 512), lambda n: (jnp.maximum(n * (tile // PREV_ROWS) - 1, 0), 0))
    return pl.pallas_call(
        body, name="mixer_fwd", grid=(nb // per_step,),
        in_specs=[pl.BlockSpec(memory_space=pltpu.SMEM), blk(512), _full((s, 256)), blk(512), blk(512), blk(512),
                  prev8, prev8, _full((N_Q_HEADS, BLOCK, 2 * BLOCK)), _full((3, 512)), _full((1, 512)),
                  _full((1, 512))],
        out_specs=[blk(512), blk(1024), blk(128)],
        out_shape=[jax.ShapeDtypeStruct((s, 512), F32), jax.ShapeDtypeStruct((s, 1024), BF16),
                   jax.ShapeDtypeStruct((s, 128), F32)],
        compiler_params=_params(("arbitrary",)),
    )(sinks, q, kv, gb, gc, xc, gc, xc, bias, conv_w, g_attn, g_conv)


def _out_proj(merged, x, mod, w_out, tm):
    s = x.shape[0]

    def body(m_ref, x_ref, mod_ref, w_ref, o_ref, x1_ref):
        o = _dot(m_ref[...], w_ref[...])
        o_ref[...] = o.astype(BF16)
        x1_ref[...] = x_ref[...] + mod_ref[G1:G1 + 1, :] * o

    return pl.pallas_call(
        body, name="out_proj", grid=(s // tm,),
        in_specs=[_rows(tm, D_MODEL), _rows(tm, D_MODEL), _full((8, D_MODEL)), _full((D_MODEL, D_MODEL))],
        out_specs=[_rows(tm, D_MODEL), _rows(tm, D_MODEL)],
        out_shape=[jax.ShapeDtypeStruct((s, D_MODEL), BF16), jax.ShapeDtypeStruct((s, D_MODEL), F32)],
        compiler_params=_params(("arbitrary",)),
    )(merged, x, mod, w_out)


def _resident(shape):
    nd = len(shape)
    return pl.BlockSpec(shape, lambda *_: (0,) * nd, pipeline_mode=pl.Buffered(1))


def _ffn(x1, o1, mod, g_norm2, w_gu, w_down, w_out, g_final, target, tm):
    s = x1.shape[0]
    chunk = D_FF // FFN_CHUNKS

    def body(x_ref, o1_ref, mod_ref, g_ref, wgu_ref, wd_ref, wo_ref, gf_ref, t_ref,
             h_ref, act_ref, do_ref, dgu_ref, dx1_ref, do1_ref, dm_ref, small_ref):
        @pl.when(pl.program_id(0) == 0)
        def _():
            small_ref[...] = jnp.zeros_like(small_ref)

        xf = x_ref[...]
        n = xf * _rsqrt_mean_sq(xf) * g_ref[...]
        h = (n * (1.0 + mod_ref[SC2:SC2 + 1, :]) + mod_ref[SH2:SH2 + 1, :]).astype(BF16)
        h_ref[...] = h
        gates, ups, o = [], [], None
        for j in range(FFN_CHUNKS):
            lo = j * chunk
            gate = _dot_nt(h, wgu_ref[lo:lo + chunk, :])
            up = _dot_nt(h, wgu_ref[D_FF + lo:D_FF + lo + chunk, :])
            sg = _sigmoid(gate)
            act = (gate * sg * up).astype(BF16)
            act_ref[:, lo:lo + chunk] = act
            gates.append((up * (sg * (1.0 + gate * (1.0 - sg)))).astype(BF16))
            ups.append((gate * sg).astype(BF16))
            part = _dot(act, wd_ref[lo:lo + chunk, :])
            o = part if o is None else o + part
        g2 = mod_ref[G2:G2 + 1, :]
        x2 = xf + g2 * o
        r = _rsqrt_mean_sq(x2)
        xn = x2 * r
        gf = gf_ref[...]
        err = xn * gf - t_ref[...]
        dy = err * (1.0 / D_MODEL)
        dxn = dy * gf
        dx2 = r * (dxn - xn * jnp.mean(dxn * xn, axis=-1, keepdims=True))
        small_ref[4:5, :] += _colsum(dy * xn)
        small_ref[5:6, :] += _colsum(err * err)
        small_ref[3:4, :] += _colsum(dx2 * o)
        do = (dx2 * g2).astype(BF16)
        do_ref[...] = do
        dh = None
        for j in range(FFN_CHUNKS):
            lo = j * chunk
            dact = _dot_nt(do, wd_ref[lo:lo + chunk, :])
            dgate = (dact * gates[j].astype(F32)).astype(BF16)
            dup = (dact * ups[j].astype(F32)).astype(BF16)
            dgu_ref[:, lo:lo + chunk] = dgate
            dgu_ref[:, D_FF + lo:D_FF + lo + chunk] = dup
            part = _dot(dgate, wgu_ref[lo:lo + chunk, :]) + _dot(dup, wgu_ref[D_FF + lo:D_FF + lo + chunk, :])
            dh = part if dh is None else dh + part
        dx1 = dx2 + _norm_mod_bwd(dh, xf, g_ref[...], mod_ref[SC2:SC2 + 1, :], small_ref)
        dx1_ref[...] = dx1.astype(BF16)
        small_ref[7:8, :] += _colsum(dx1 * o1_ref[...].astype(F32))
        do1 = (dx1 * mod_ref[G1:G1 + 1, :]).astype(BF16)
        do1_ref[...] = do1
        dm_ref[...] = _dot_nt(do1, wo_ref[...]).astype(BF16)

        @pl.when(pl.program_id(0) == pl.num_programs(0) - 1)
        def _():
            total = jnp.sum(small_ref[5:6, :], axis=-1, keepdims=True) * (0.5 / D_MODEL)
            small_ref[6:7, :] = jnp.broadcast_to(total, (1, D_MODEL))

    narrow = jax.ShapeDtypeStruct((s, D_MODEL), BF16)
    return pl.pallas_call(
        body, name="ffn", grid=(s // tm,),
        in_specs=[_rows(tm, D_MODEL), _rows(tm, D_MODEL), _full((8, D_MODEL)), _full((1, D_MODEL)),
                  _resident((2 * D_FF, D_MODEL)), _resident((D_FF, D_MODEL)), _resident((D_MODEL, D_MODEL)),
                  _full((1, D_MODEL)), _rows(tm, D_MODEL)],
        out_specs=[_rows(tm, D_MODEL), _rows(tm, D_FF), _rows(tm, D_MODEL), _rows(tm, 2 * D_FF), _rows(tm, D_MODEL),
                   _rows(tm, D_MODEL), _rows(tm, D_MODEL), _full((8, D_MODEL))],
        out_shape=[narrow, jax.ShapeDtypeStruct((s, D_FF), BF16), narrow, jax.ShapeDtypeStruct((s, 2 * D_FF), BF16),
                   narrow, narrow, narrow, jax.ShapeDtypeStruct((8, D_MODEL), F32)],
        compiler_params=_params(("arbitrary",), VMEM_LIMIT_LARGE),
    )(x1, o1, mod, g_norm2, w_gu, w_down, w_out, g_final, target)


def _norm_mod_bwd(dh, xf, g, scale_row, small_ref):
    r = _rsqrt_mean_sq(xf)
    xn = xf * r
    small_ref[0:1, :] += _colsum(dh)
    small_ref[1:2, :] += _colsum(dh * (xn * g))
    dn = dh * (1.0 + scale_row)
    small_ref[2:3, :] += _colsum(dn * xn)
    dxn = dn * g
    return r * (dxn - xn * jnp.mean(dxn * xn, axis=-1, keepdims=True))


def _group_norm_bwd(dm, a, g):
    r = _rsqrt_mean_sq(a)
    an = a * r
    dan = dm * g
    return r * (dan - an * jnp.mean(dan * an, axis=-1, keepdims=True)), _colsum(dm * an)


def _mixer_bwd(after, q, kv, gb, gc, xc, bias, sinks, conv_w, g_attn, g_conv, attn, lse, dmerged):
    s = q.shape[0]
    nb = s // BLOCK

    per_step = min(MIXER_BLOCKS, nb)
    tile = per_step * BLOCK
    steps = nb // per_step

    def one_block(n, rows, before, nxt, sink_ref, q_ref, kv_ref, gb_ref, gc_ref, xc_ref, bias_ref, cw_ref, ga_ref,
                  gcv_ref, attn_ref, lse_ref, dm_ref, dproj_ref, dbias_ref, dsink_ref, small_ref):
        next_dy, next_dkv = nxt
        dm = dm_ref[rows, :].astype(F32)
        gbv, gcv_, xcv = gb_ref[rows, :].astype(F32), gc_ref[rows, :].astype(F32), xc_ref[rows, :].astype(F32)
        u, u1, u2 = _conv_taps(gcv_, xcv, before[0], before[1], n)
        cw = cw_ref[...]
        yv = cw[0:1, :] * u2 + cw[1:2, :] * u1 + cw[2:3, :] * u
        dcv, dg_conv = _group_norm_bwd(dm[:, 512:1024], gbv * yv, gcv_ref[...])
        small_ref[1:2, :] += dg_conv
        dproj_ref[rows, 768:1280] = (dcv * yv).astype(BF16)
        dy = dcv * gbv
        row = lax.broadcasted_iota(jnp.int32, dy.shape, 0)
        d1 = jnp.where(row == BLOCK - 1, next_dy[0:1, :], pltpu.roll(dy, BLOCK - 1, 0))
        d2 = jnp.where(row == BLOCK - 2, next_dy[0:1, :],
                       jnp.where(row == BLOCK - 1, next_dy[1:2, :], pltpu.roll(dy, BLOCK - 2, 0)))
        du = cw[2:3, :] * dy + cw[1:2, :] * d1 + cw[0:1, :] * d2
        dproj_ref[rows, 1280:1792] = (du * xcv).astype(BF16)
        dproj_ref[rows, 1792:2304] = (du * gcv_).astype(BF16)
        small_ref[2:3, :] += _colsum(dy * u2)
        small_ref[3:4, :] += _colsum(dy * u1)
        small_ref[4:5, :] += _colsum(dy * u)

        attn_v = attn_ref[rows, :]
        dout, dg_attn = _group_norm_bwd(dm[:, 0:512], attn_v, ga_ref[...])
        small_ref[0:1, :] += dg_attn
        ks, vs = _load_kv_window(kv_ref, n)
        lane = lax.broadcasted_iota(jnp.int32, (BLOCK, BLOCK), 1)
        low = lane < HEAD_DIM
        col = lax.broadcasted_iota(jnp.int32, (BLOCK, 2 * BLOCK), 1)
        no_prev = (col < BLOCK) & (n == 0)
        lse_all = lse_ref[rows, :]
        dsink = jnp.zeros((BLOCK, BLOCK), F32)
        dq_pairs = []
        dk_groups, dv_groups = [], []
        for kvh in range(2):
            ds_rows, pr_rows, q_rows, do_rows = [], [], [], []
            for p in (2 * kvh, 2 * kvh + 1):
                qp = q_ref[rows, 128 * p:128 * (p + 1)].astype(F32)
                do_p = dout[:, 128 * p:128 * (p + 1)]
                prod = do_p * attn_v[:, 128 * p:128 * (p + 1)]
                res = []
                for e in range(2):
                    h = 2 * p + e
                    half = low if e == 0 else ~low
                    qm = jnp.where(half, qp, 0.0).astype(BF16)
                    dom = jnp.where(half, do_p, 0.0).astype(BF16)
                    delta = jnp.sum(jnp.where(half, prod, 0.0), axis=-1, keepdims=True)
                    lse_h = jnp.sum(jnp.where(lane == h, lse_all, 0.0), axis=-1, keepdims=True)
                    sw = 0 if kvh == e else 1
                    sc = _dot_nt(qm, ks[sw]) * SCALE + bias_ref[h]
                    sc = jnp.where(no_prev, NEG_INF, sc)
                    pr = jnp.exp(sc - lse_h)
                    dp = _dot_nt(dom, vs[sw])
                    ds = pr * (dp - delta)
                    dbias_ref[h] += ds
                    dsink = dsink + jnp.where(lane == h, -jnp.exp(sink_ref[h] - lse_h) * delta, 0.0)
                    dsb = ds.astype(BF16)
                    res.append(_dot(dsb, ks[sw]) * SCALE)
                    ds_rows.append(dsb)
                    pr_rows.append(pr.astype(BF16))
                    q_rows.append(qm)
                    do_rows.append(dom)
                dq_pairs.append(jnp.where(low, res[0], res[1]))
            dk_g = _dot_tn(jnp.concatenate(ds_rows, axis=0), jnp.concatenate(q_rows, axis=0)) * SCALE
            dv_g = _dot_tn(jnp.concatenate(pr_rows, axis=0), jnp.concatenate(do_rows, axis=0))
            dk_groups.append(dk_g + pltpu.roll(dk_g, 64, 1))
            dv_groups.append(dv_g + pltpu.roll(dv_g, 64, 1))
        dproj_ref[rows, 0:512] = jnp.concatenate(dq_pairs, axis=1).astype(BF16)
        dsink_ref[...] += dsink
        low_kv = lax.broadcasted_iota(jnp.int32, (2 * BLOCK, BLOCK), 1) < HEAD_DIM
        dkv_win = jnp.concatenate([jnp.where(low_kv, dk_groups[0], dk_groups[1]),
                                   jnp.where(low_kv, dv_groups[0], dv_groups[1])], axis=1)
        dproj_ref[rows, 512:768] = (dkv_win[BLOCK:2 * BLOCK, :] + next_dkv).astype(BF16)
        return dy[0:8, :], dkv_win[0:BLOCK, :]

    def body(sink_ref, q_ref, kv_ref, gb_ref, gc_ref, xc_ref, gcp_ref, xcp_ref, *rest):
        refs, dy_ref, dkv_ref = rest[:-2], rest[-2], rest[-1]
        dbias_ref, dsink_ref, small_ref = refs[8], refs[9], refs[10]
        step = pl.program_id(0)

        @pl.when(step == 0)
        def _():
            dbias_ref[...] = jnp.zeros_like(dbias_ref)
            dsink_ref[...] = jnp.zeros_like(dsink_ref)
            small_ref[...] = jnp.zeros_like(small_ref)
            dy_ref[...] = jnp.zeros_like(dy_ref)
            dkv_ref[...] = jnp.zeros_like(dkv_ref)

        nxt = (dy_ref[...], dkv_ref[...])
        for sub in reversed(range(per_step)):
            rows = slice(sub * BLOCK, (sub + 1) * BLOCK)
            ahead = slice(sub * BLOCK - PREV_ROWS, sub * BLOCK)
            before = (gcp_ref[...], xcp_ref[...]) if sub == 0 else (gc_ref[ahead, :], xc_ref[ahead, :])
            nxt = one_block((steps - 1 - step) * per_step + sub, rows, before, nxt,
                            sink_ref, q_ref, kv_ref, gb_ref, gc_ref, xc_ref, *refs)
        dy_ref[...], dkv_ref[...] = nxt

        @pl.when(step == steps - 1)
        def _():
            small_ref[5:6, :] = jnp.concatenate([_colsum(dsink_ref[...]), jnp.zeros((1, 512 - BLOCK), F32)], axis=1)

    blk = lambda w: pl.BlockSpec((tile, w), lambda t: (steps - 1 - t, 0))
    prev8 = pl.BlockSpec((PREV_ROWS, 512),
                         lambda t: (jnp.maximum((steps - 1 - t) * (tile // PREV_ROWS) - 1, 0), 0))
    bf = lambda w: jax.ShapeDtypeStruct((s, w), BF16)
    return pl.pallas_call(
        _coming_behind(body), name="mixer_bwd", grid=(steps,),
        in_specs=[ANY_SPEC, pl.BlockSpec(memory_space=pltpu.SMEM), blk(512), _full((s, 256)), blk(512), blk(512), blk(512),
                  prev8, prev8, _full((N_Q_HEADS, BLOCK, 2 * BLOCK)), _full((3, 512)), _full((1, 512)),
                  _full((1, 512)), blk(512), blk(128), blk(1024)],
        out_specs=[blk(IN_PROJ_WIDTH), _full((N_Q_HEADS, BLOCK, 2 * BLOCK)), _full((BLOCK, BLOCK)), _full((8, 512))],
        out_shape=[bf(IN_PROJ_WIDTH), jax.ShapeDtypeStruct((N_Q_HEADS, BLOCK, 2 * BLOCK), F32),
                   jax.ShapeDtypeStruct((BLOCK, BLOCK), F32), jax.ShapeDtypeStruct((8, 512), F32)],
        scratch_shapes=[pltpu.VMEM((8, 512), F32), pltpu.VMEM((BLOCK, 2 * KV_WIDTH), F32)],
        compiler_params=_params(("arbitrary",), VMEM_LIMIT_LARGE),
    )(after, sinks, q, kv, gb, gc, xc, gc, xc, bias, conv_w, g_attn, g_conv, attn, lse, dmerged)


def _in_proj_bwd(after, dproj, x, dx1, mod, g_norm1, w_in, tm):
    s = x.shape[0]

    def body(dproj_ref, x_ref, dx1_ref, mod_ref, g_ref, w_ref, dx_ref, small_ref):
        @pl.when(pl.program_id(0) == 0)
        def _():
            small_ref[...] = jnp.zeros_like(small_ref)

        dh = _dot(dproj_ref[...], w_ref[...])
        dx_ref[...] = dx1_ref[...].astype(F32) + _norm_mod_bwd(dh, x_ref[...], g_ref[...], mod_ref[SC1:SC1 + 1, :],
                                                               small_ref)

    return pl.pallas_call(
        _coming_behind(body), name="in_proj_bwd", grid=(s // tm,),
        in_specs=[ANY_SPEC, _rows(tm, IN_PROJ_WIDTH), _rows(tm, D_MODEL), _rows(tm, D_MODEL), _full((8, D_MODEL)),
                  _full((1, D_MODEL)), _full((IN_PROJ_WIDTH, D_MODEL))],
        out_specs=[_rows(tm, D_MODEL), _full((8, D_MODEL))],
        out_shape=[jax.ShapeDtypeStruct((s, D_MODEL), F32), jax.ShapeDtypeStruct((8, D_MODEL), F32)],
        compiler_params=_params(("arbitrary",), VMEM_LIMIT_LARGE),
    )(after, dproj, x, dx1, mod, g_norm1, w_in)


def _weight_grad(a, b, tk, ts, name, after=None):
    s, k = a.shape
    n = b.shape[1]
    nt = s // ts
    extra = [] if after is None else [after]

    def body(a_ref, b_ref, *rest):
        o_ref, acc_ref = rest[-2:]
        t = pl.program_id(1)
        @pl.when(t == 0)
        def _():
            acc_ref[...] = jnp.zeros_like(acc_ref)

        acc = acc_ref[...] + _dot_tn(a_ref[...], b_ref[...])
        acc_ref[...] = acc
        o_ref[...] = acc.astype(BF16)

    return pl.pallas_call(
        body, name=name, grid=(k // tk, nt),
        in_specs=[pl.BlockSpec((ts, tk), lambda i, t: (t, i)), pl.BlockSpec((ts, n), lambda i, t: (t, 0))]
        + [ANY_SPEC] * len(extra),
        out_specs=pl.BlockSpec((tk, n), lambda i, t: (i, 0)),
        out_shape=jax.ShapeDtypeStruct((k, n), BF16),
        scratch_shapes=[pltpu.VMEM((tk, n), F32)],
        compiler_params=_params(("arbitrary", "arbitrary"), VMEM_LIMIT_LARGE),
    )(a, b, *extra)


def _rel_bias_grad(dbias, bucket):
    def body(db_ref, bk_ref, o_ref, rows_ref):
        bk = bk_ref[...]
        for b in range(N_BUCKETS):
            sel = (bk == b).astype(F32)
            for h in range(N_Q_HEADS):
                rows_ref[N_BUCKETS * h + b:N_BUCKETS * h + b + 1, :] = _colsum(db_ref[h] * sel)
        head = lax.broadcasted_iota(jnp.int32, (N_BUCKETS, N_Q_HEADS), 1)
        out = jnp.zeros((N_BUCKETS, N_Q_HEADS), F32)
        for h in range(N_Q_HEADS):
            per_bucket = jnp.sum(rows_ref[N_BUCKETS * h:N_BUCKETS * (h + 1), :], axis=-1, keepdims=True)
            out = out + jnp.where(head == h, per_bucket, 0.0)
        o_ref[...] = out

    return pl.pallas_call(
        body, name="rel_bias_grad",
        out_shape=jax.ShapeDtypeStruct((N_BUCKETS, N_Q_HEADS), F32),
        scratch_shapes=[pltpu.VMEM((N_BUCKETS * N_Q_HEADS, 2 * BLOCK), F32)],
    )(dbias, bucket)


def _lanes_from(x, start, width):
    n = x.shape[1]
    return pltpu.roll(x, (n - start) % n, 1)[:, 0:width]


def _w_ada_grad(me, cond_all, packed_all, cols):
    def body(me_ref, c_ref, p_ref, o_ref):
        dmod = jnp.concatenate([p_ref[k][:, OFF_DMOD:OFF_DMOD + N_MOD * D_MODEL] for k in range(N_DEV)], axis=0)
        mine = _lanes_from(dmod, me_ref[0] * cols, cols)
        pad = lambda a: jnp.concatenate([a, jnp.zeros((128 - N_DEV, a.shape[1]), F32)], axis=0)
        o_ref[...] = _dot_tn(pad(c_ref[...]), pad(mine))

    vmem = pl.BlockSpec(memory_space=pltpu.VMEM)
    return pl.pallas_call(body, name="w_ada_grad",
                          in_specs=[pl.BlockSpec(memory_space=pltpu.SMEM), vmem, vmem],
                          out_shape=jax.ShapeDtypeStruct((cond_all.shape[1], cols), F32))(me, cond_all, packed_all)


SMALL_PARAMS = (("rel_bias", None), ("b_ada", (OFF_DMOD, N_MOD * D_MODEL)), ("g_norm1", (OFF_GN1, D_MODEL)),
                ("sinks", (OFF_SINK, N_Q_HEADS)), ("conv_w", None), ("g_attn_out", (OFF_GATT, ATTN_WIDTH)),
                ("g_conv_out", (OFF_GCV, CONV_WIDTH)), ("g_norm2", (OFF_GN2, D_MODEL)),
                ("g_final", (OFF_GFIN, D_MODEL)))


def _small_update(me, packed_all, rel_all, state, after):
    n_p = len(SMALL_PARAMS)
    flat = [a for triple in state for a in triple]
    conv_cols = state[4][0].shape[1]

    def body(me_ref, p_ref, r_ref, *refs):
        ins = refs[:3 * n_p]
        loss_ref, outs = refs[3 * n_p + len(after)], refs[3 * n_p + len(after) + 1:]
        small, rel = p_ref[0], r_ref[0]
        for k in range(1, N_DEV):
            small = small + p_ref[k]
            rel = rel + r_ref[k]
        loss_ref[...] = small[:, OFF_LOSS:OFF_LOSS + 128]
        taps = jnp.concatenate([small[:, OFF_CONVW + CONV_WIDTH * j:OFF_CONVW + CONV_WIDTH * (j + 1)]
                                for j in range(3)] + [jnp.zeros((5, CONV_WIDTH), F32)], axis=0)
        conv_g = _lanes_from(taps, me_ref[0] * conv_cols, conv_cols)[0:3, :]
        for i, (name, lanes) in enumerate(SMALL_PARAMS):
            g = rel if name == "rel_bias" else conv_g if name == "conv_w" else small[:, lanes[0]:lanes[0] + lanes[1]]
            w_ref, m_ref, v_ref = ins[3 * i:3 * i + 3]
            outs[4 * i][...] = g
            outs[4 * i + 1][...], outs[4 * i + 2][...], outs[4 * i + 3][...] = _adam_math(
                w_ref[...], g, m_ref[...], v_ref[...])

    vmem = pl.BlockSpec(memory_space=pltpu.VMEM)
    out_shape = [jax.ShapeDtypeStruct((1, 128), F32)]
    for w, _, _ in state:
        out_shape += [jax.ShapeDtypeStruct(w.shape, F32)] * 4
    outs = pl.pallas_call(
        body, name="small_update",
        in_specs=[pl.BlockSpec(memory_space=pltpu.SMEM), vmem, vmem] + [vmem] * len(flat)
        + [pl.BlockSpec(memory_space=pl.ANY)] * len(after),
        out_shape=out_shape,
    )(me, packed_all, rel_all, *flat, *after)
    return outs[0], [tuple(outs[1 + 4 * i:5 + 4 * i]) for i in range(n_p)]


def _adam_math(w, g, m, v):
    m = ADAM_B1 * m + (1.0 - ADAM_B1) * g
    v = ADAM_B2 * v + (1.0 - ADAM_B2) * (g * g)
    m_hat = m / (1.0 - ADAM_B1 ** ADAM_STEP)
    v_hat = v / (1.0 - ADAM_B2 ** ADAM_STEP)
    delta = -ADAM_LR * (m_hat / (jnp.sqrt(v_hat) + ADAM_EPS) + ADAM_WD * w)
    return delta, m, v


def _adamw_parts(w, m, v, local, land, me, tr, name):
    r, c = w.shape

    def body(me_ref, w_ref, m_ref, v_ref, own_ref, land_ref, g_ref, d_ref, mo_ref, vo_ref):
        g = own_ref[0].astype(F32)
        for k in range(N_DEV - 1):
            g = g + land_ref[k].astype(F32)
        g_ref[...] = g
        d_ref[...], mo_ref[...], vo_ref[...] = _adam_math(w_ref[...], g, m_ref[...], v_ref[...])

    tile = pl.BlockSpec((tr, c), lambda i, me_ref: (i, 0))
    return pl.pallas_call(
        body, name=name,
        grid_spec=pltpu.PrefetchScalarGridSpec(
            num_scalar_prefetch=1, grid=(r // tr,),
            in_specs=[tile, tile, tile, pl.BlockSpec((1, tr, c), lambda i, me_ref: (me_ref[0], i, 0)),
                      pl.BlockSpec((N_DEV - 1, tr, c), lambda i, me_ref: (0, i, 0))],
            out_specs=[tile] * 4),
        out_shape=[jax.ShapeDtypeStruct((r, c), F32)] * 4,
        compiler_params=_params(("arbitrary",)),
    )(me, w, m, v, local, land)


def _adamw(w, m, v, g, tr, name):
    r, c = w.shape

    def body(w_ref, m_ref, v_ref, g_ref, d_ref, mo_ref, vo_ref):
        d_ref[...], mo_ref[...], vo_ref[...] = _adam_math(w_ref[...], g_ref[...], m_ref[...], v_ref[...])

    tile = pl.BlockSpec((tr, c), lambda i: (i, 0))
    return pl.pallas_call(
        body, name=name, grid=(r // tr,),
        in_specs=[tile] * 4, out_specs=[tile] * 3,
        out_shape=[jax.ShapeDtypeStruct((r, c), F32)] * 3,
        compiler_params=_params(("arbitrary",)),
    )(w, m, v, g)


def _behind(a, token):
    return a + token[0:a.shape[0], 0:1]


def _local_step(x, target, mod, w_in_t, weights_out_gu, weights_down, rel_bias, g_norm1, sinks, conv_w, g_attn,
                g_conv, g_norm2, g_final, exchange):
    s = x.shape[0]
    tm = min(512, s)
    tm_small = min(256, s)
    bucket = _bucket_table()
    bias = _bias_table(rel_bias, bucket)

    h, q, kv, gb, gc, xc = _in_proj(x, mod, g_norm1, w_in_t, tm)
    attn, merged, lse = _mixer_fwd(q, kv, gb, gc, xc, bias, sinks, conv_w, g_attn, g_conv)
    w_out, w_gu_t = weights_out_gu(merged)
    o1, x1 = _out_proj(merged, x, mod, w_out, tm)
    w_down = weights_down(x1)
    h2, act, do2, dgu, dx1, do1, dmerged, sm_2 = _ffn(x1, o1, mod, g_norm2, w_gu_t, w_down, w_out, g_final, target,
                                                      tm_small)
    ts = min(WEIGHT_GRAD_ROWS, s)
    tok_down = exchange("w_down", _weight_grad(act, do2, D_FF // 2, ts, "w_down_grad"))
    tok_gu = exchange("w_gu", _weight_grad(dgu, h2, D_FF // 2, ts, "w_gu_grad", after=tok_down))
    tok_out = exchange("w_out", _weight_grad(merged, do1, D_MODEL, ts, "w_out_grad", after=tok_gu))
    dproj, dbias, dsink, sm_mix = _mixer_bwd(
        tok_out, q, kv, gb, gc, xc, bias, sinks, conv_w, g_attn, g_conv, attn, lse, dmerged)
    tok_in = exchange("w_in", _weight_grad(dproj, h, IN_PROJ_WIDTH // 2, ts, "w_in_grad"))
    dx, sm_1 = _in_proj_bwd(tok_in, dproj, x, dx1, mod, g_norm1, w_in_t, tm)
    d_rel = _rel_bias_grad(dbias, bucket)

    packed = jnp.concatenate([
        sm_1[0], sm_1[1], sm_2[7], sm_2[0], sm_2[1], sm_2[3],
        sm_1[2],
        sm_mix[5, 0:128],
        sm_mix[0], sm_mix[1],
        sm_2[2],
        sm_2[4],
        sm_mix[2], sm_mix[3], sm_mix[4],
        sm_2[6, 0:128],
    ])[None, :]
    return dx, packed, d_rel


def kernel(x, c, rel_bias, w_ada, b_ada, g_norm1, w_in, sinks, conv_w, g_attn_out, g_conv_out, w_out, g_norm2, w_gu, w_down, g_final, loss_target, m_rel_bias, m_w_ada, m_b_ada, m_g_norm1, m_w_in, m_sinks, m_conv_w, m_g_attn_out, m_g_conv_out, m_w_out, m_g_norm2, m_w_gu, m_w_down, m_g_final, v_rel_bias, v_w_ada, v_b_ada, v_g_norm1, v_w_in, v_sinks, v_conv_w, v_g_attn_out, v_g_conv_out, v_w_out, v_g_norm2, v_w_gu, v_w_down, v_g_final):
    me = _linear(_mesh_position())
    me_arr = jnp.reshape(me, (1,)).astype(jnp.int32)
    ada_cols = w_ada.shape[2]
    tm = min(512, x.shape[1])

    cond = _silu_rows(c)
    cond_all, conv_w_all = _all_gather_small([cond, conv_w[0]], "gather_cond")
    cond_all = cond_all[:, 0, :]
    conv_cols = conv_w.shape[2]
    conv_w_full = conv_w_all.transpose(1, 0, 2).reshape(3, CONV_WIDTH)
    b_cols = lax.dynamic_slice_in_dim(b_ada, me * ada_cols, ada_cols, axis=1)
    mod_cols = _mod_columns(cond_all, w_ada[0], b_cols)
    mod_all = _all_gather_small([mod_cols], "gather_mod")[0]
    mod = lax.dynamic_index_in_dim(mod_all, me, axis=1, keepdims=False).reshape(N_MOD, D_MODEL)
    mod = jnp.concatenate([mod, jnp.zeros((2, D_MODEL), F32)], axis=0)

    w_in_t = _all_gather([w_in[0].T], "gather_w_in", to_bf16=True, big=True)[0].reshape(IN_PROJ_WIDTH, D_MODEL)
    gather_sems, staged, gather_token = _gather_start(
        _stage_blocks([w_out[0], w_gu[0].T, w_down[0]], w_in_t, "stage_weights"), "gather_start_weights")
    mod = _behind(mod, gather_token)

    def weights_out_gu(after):
        got = _gather_pass_on(_gather_wait(gather_sems[0:4], staged[0:2], [after], "gather_wait_out_gu"),
                              "gather_pass_on_out_gu")
        return got[0].reshape(D_MODEL, D_MODEL), got[1].reshape(2 * D_FF, D_MODEL)

    def weights_down(after):
        got = _gather_pass_on(_gather_wait(gather_sems[4:6], staged[2:3], [after], "gather_wait_down"),
                              "gather_pass_on_down")
        return got[0].reshape(D_FF, D_MODEL)

    started = {}

    def exchange(name, dw):
        st = _exchange_start(dw.reshape(N_DEV, dw.shape[0] // N_DEV, dw.shape[1]), "exchange_start_" + name)
        started[name] = st
        return st[4]

    dx, packed, d_rel = _local_step(
        x[0], loss_target[0], mod, w_in_t, weights_out_gu, weights_down, rel_bias, g_norm1, sinks[0], conv_w_full,
        g_attn_out, g_conv_out, g_norm2, g_final[None, :], exchange)

    def zone(a):
        return lax.dynamic_update_slice(jnp.zeros((N_DEV,) + a.shape, F32), a[None], (me,) + (0,) * a.ndim)

    shared = _share_start([packed, d_rel], [zone(packed), zone(d_rel)], "share_small_start")

    def finish(name, after, w, m, v, tr):
        src, land = _exchange_wait(started[name], after, "exchange_wait_" + name)
        return _adamw_parts(w, m, v, src, land, me_arr, tr, "adamw_" + name)

    g_down, d_down, nm_down, nv_down = finish("w_down", [shared[2][0]], w_down[0], m_w_down[0], v_w_down[0], 176)
    g_gu, d_gu, nm_gu, nv_gu = finish("w_gu", [nv_down], w_gu[0].T, m_w_gu[0].T, v_w_gu[0].T, 352)
    g_out, d_out, nm_out, nv_out = finish("w_out", [nv_gu], w_out[0], m_w_out[0], v_w_out[0], 128)

    packed_all, rel_all = _share_wait(shared, [nv_out], "share_small_wait")
    g_ada = _w_ada_grad(me_arr, cond_all, packed_all, ada_cols)
    d_ada, nm_ada, nv_ada = _adamw(w_ada[0], m_w_ada[0], v_w_ada[0], g_ada, 256, "adamw_w_ada")
    as_rows = {"conv_w": lambda a: a[0], "g_final": lambda a: a[None, :]}
    small_state = {
        "rel_bias": (rel_bias, m_rel_bias, v_rel_bias), "b_ada": (b_ada, m_b_ada, v_b_ada),
        "g_norm1": (g_norm1, m_g_norm1, v_g_norm1), "sinks": (sinks, m_sinks, v_sinks),
        "conv_w": (conv_w, m_conv_w, v_conv_w), "g_attn_out": (g_attn_out, m_g_attn_out, v_g_attn_out),
        "g_conv_out": (g_conv_out, m_g_conv_out, v_g_conv_out), "g_norm2": (g_norm2, m_g_norm2, v_g_norm2),
        "g_final": (g_final, m_g_final, v_g_final),
    }
    state = [tuple(as_rows.get(name, lambda a: a)(a) for a in small_state[name]) for name, _ in SMALL_PARAMS]
    loss_row, small_out = _small_update(me_arr, packed_all, rel_all, state, [])
    loss = loss_row[0, 0]
    small_res = {name: tuple(a.reshape(small_state[name][0].shape) for a in res)
                 for (name, _), res in zip(SMALL_PARAMS, small_out)}

    g_in, d_in, nm_in, nv_in = finish("w_in", [loss_row, nv_ada], w_in[0].T, m_w_in[0].T, v_w_in[0].T, 144)

    big = {
        "w_ada": (g_ada[None], d_ada[None], nm_ada[None], nv_ada[None]),
        "w_in": (g_in.T[None], d_in.T[None], nm_in.T[None], nv_in.T[None]),
        "w_out": (g_out[None], d_out[None], nm_out[None], nv_out[None]),
        "w_gu": (g_gu.T[None], d_gu.T[None], nm_gu.T[None], nv_gu.T[None]),
        "w_down": (g_down[None], d_down[None], nm_down[None], nv_down[None]),
    }
    order = ["rel_bias", "w_ada", "b_ada", "g_norm1", "w_in", "sinks", "conv_w", "g_attn_out", "g_conv_out", "w_out",
             "g_norm2", "w_gu", "w_down", "g_final"]
    results = [big[k] if k in big else small_res[k] for k in order]
    return (loss, dx[None], *[r[0] for r in results], *[r[1] for r in results], *[r[2] for r in results],
            *[r[3] for r in results])
```

```python
import functools
import math

import jax
import jax.numpy as jnp
from jax import lax
from jax.experimental import pallas as pl
from jax.experimental.pallas import tpu as pltpu

F32 = jnp.float32
BF16 = jnp.bfloat16

D_MODEL = 1024
HEAD_DIM = 64
N_Q_HEADS = 8
ATTN_WIDTH = 512
KV_WIDTH = 128
CONV_WIDTH = 512
IN_PROJ_WIDTH = 2304
D_FF = 2816
N_MOD = 6
N_BUCKETS = 32
MAX_DISTANCE = 128
BLOCK = 128
EPS = 1e-6
NEG_INF = -1e30
SCALE = HEAD_DIM ** -0.5
N_DEV = 8

ADAM_LR = 0.001
ADAM_B1 = 0.9
ADAM_B2 = 0.999
ADAM_EPS = 1e-08
ADAM_WD = 0.01
ADAM_STEP = 10

SH1, SC1, G1, SH2, SC2, G2 = range(6)

VMEM_LIMIT_LARGE = 60 * 1024 * 1024
WEIGHT_GRAD_ROWS = 2048
FFN_CHUNKS = 1
PREV_ROWS = 16
MIXER_BLOCKS = 4
MESH_ID = pl.DeviceIdType.MESH

OFF_DMOD = 0
OFF_GN1 = OFF_DMOD + N_MOD * D_MODEL
OFF_SINK = OFF_GN1 + D_MODEL
OFF_GATT = OFF_SINK + 128
OFF_GCV = OFF_GATT + ATTN_WIDTH
OFF_GN2 = OFF_GCV + CONV_WIDTH
OFF_GFIN = OFF_GN2 + D_MODEL
OFF_CONVW = OFF_GFIN + D_MODEL
OFF_LOSS = OFF_CONVW + 3 * CONV_WIDTH
PACKED = OFF_LOSS + 128


def _params(sem=None, vmem=None):
    return pltpu.CompilerParams(dimension_semantics=sem, vmem_limit_bytes=vmem)


def _coming_behind(body):
    def skipping(after_ref, *refs):
        body(*refs)

    return skipping


ANY_SPEC = pl.BlockSpec(memory_space=pl.ANY)


def _full(shape):
    nd = len(shape)
    return pl.BlockSpec(shape, lambda *_: (0,) * nd)


def _rows(tm, width):
    return pl.BlockSpec((tm, width), lambda i, *_: (i, 0))


def _sigmoid(x):
    return 1.0 / (1.0 + jnp.exp(-x))


def _rsqrt_mean_sq(x):
    return lax.rsqrt(jnp.mean(x * x, axis=-1, keepdims=True) + EPS)


def _colsum(x):
    return jnp.sum(x, axis=0, keepdims=True)


def _dot(a, b):
    return jnp.dot(a, b, preferred_element_type=F32)


def _dot_nt(a, b):
    return lax.dot_general(a, b, (((1,), (1,)), ((), ())), preferred_element_type=F32)


def _dot_tn(a, b):
    return lax.dot_general(a, b, (((0,), (0,)), ((), ())), preferred_element_type=F32)


def _mesh_position():
    return lax.axis_index("x"), lax.axis_index("y"), lax.axis_index("c")


def _linear(p):
    return 4 * p[0] + 2 * p[1] + p[2]


def _all_gather(arrs, name, to_bf16, big):
    n = len(arrs)
    out_dtype = BF16 if to_bf16 else F32

    def body(*refs):
        in_refs, out_refs = refs[:n], refs[n:2 * n]
        rest = refs[2 * n:]
        if to_bf16:
            stage, rest = rest[:n], rest[n:]
            for a in range(n):
                stage[a][...] = in_refs[a][...].astype(BF16)
            srcs = stage
        else:
            srcs = in_refs
        send_sems, recv_sems, local_sems = rest
        x, y, c = _mesh_position()
        me, sibling = (x, y, c), (x, y, 1 - c)
        chips = [(1 - x, y), (x, 1 - y), (1 - x, 1 - y)]

        def slot(a, p):
            return out_refs[a].at[_linear(p)]

        def copy(k, a, block, to, src=None):
            return pltpu.make_async_remote_copy(
                src_ref=slot(a, block) if src is None else src,
                dst_ref=slot(a, block),
                send_sem=send_sems.at[k * n + a],
                recv_sem=recv_sems.at[k * n + a],
                device_id=to,
                device_id_type=MESH_ID,
            )

        mine = [pltpu.make_async_copy(srcs[a], slot(a, me), local_sems.at[a]) for a in range(n)]
        for cp in mine:
            cp.start()
        first = [copy(0, a, me, sibling, src=srcs[a]) for a in range(n)]
        for j, chip in enumerate(chips):
            first += [copy(1 + j, a, me, (*chip, c), src=srcs[a]) for a in range(n)]
        for cp in first:
            cp.start()
        passed = []
        for j, chip in enumerate(chips):
            for a in range(n):
                copy(1 + j, a, (*chip, c), me).wait_recv()
                fwd = copy(4 + j, a, (*chip, c), sibling)
                fwd.start()
                passed.append(fwd)
        for a in range(n):
            copy(0, a, sibling, me).wait_recv()
        for j, chip in enumerate(chips):
            for a in range(n):
                copy(4 + j, a, (*chip, 1 - c), me).wait_recv()
        for cp in first + passed:
            cp.wait_send()
        for cp in mine:
            cp.wait()

    vmem = pl.BlockSpec(memory_space=pltpu.VMEM)
    out_space = pl.BlockSpec(memory_space=pl.ANY) if big else vmem
    scratch = [pltpu.VMEM(a.shape, BF16) for a in arrs] if to_bf16 else []
    scratch += [pltpu.SemaphoreType.DMA((7 * n,)), pltpu.SemaphoreType.DMA((7 * n,)),
                pltpu.SemaphoreType.DMA((n,))]
    outs = pl.pallas_call(
        body, name=name,
        out_shape=[jax.ShapeDtypeStruct((N_DEV,) + a.shape, out_dtype) for a in arrs],
        in_specs=[vmem] * n, out_specs=[out_space] * n,
        scratch_shapes=scratch,
        compiler_params=_params(vmem=VMEM_LIMIT_LARGE if big else None),
    )(*arrs)
    return list(outs)


def _peer(k):
    x, y, c = _mesh_position()
    return (1 - x if k & 4 else x, 1 - y if k & 2 else y, 1 - c if k & 1 else c)


def _all_gather_small(arrs, name):
    n = len(arrs)

    def body(*refs):
        in_refs, out_refs = refs[:n], refs[n:2 * n]
        send_sems, recv_sems, local_sems = refs[2 * n:]
        me = _linear(_mesh_position())
        mine = [pltpu.make_async_copy(in_refs[a], out_refs[a].at[me], local_sems.at[a]) for a in range(n)]
        for cp in mine:
            cp.start()
        sends = []
        for k in range(1, N_DEV):
            for a in range(n):
                sends.append(pltpu.make_async_remote_copy(
                    src_ref=in_refs[a], dst_ref=out_refs[a].at[me],
                    send_sem=send_sems.at[(k - 1) * n + a], recv_sem=recv_sems.at[(k - 1) * n + a],
                    device_id=_peer(k), device_id_type=MESH_ID))
                sends[-1].start()
        for k in range(1, N_DEV):
            for a in range(n):
                pltpu.make_async_remote_copy(
                    src_ref=in_refs[a], dst_ref=out_refs[a].at[_linear(_peer(k))],
                    send_sem=send_sems.at[(k - 1) * n + a], recv_sem=recv_sems.at[(k - 1) * n + a],
                    device_id=_peer(k), device_id_type=MESH_ID).wait_recv()
        for cp in sends:
            cp.wait_send()
        for cp in mine:
            cp.wait()

    vmem = pl.BlockSpec(memory_space=pltpu.VMEM)
    return list(pl.pallas_call(
        body, name=name,
        out_shape=[jax.ShapeDtypeStruct((N_DEV,) + a.shape, F32) for a in arrs],
        in_specs=[vmem] * n, out_specs=[vmem] * n,
        scratch_shapes=[pltpu.SemaphoreType.DMA((7 * n,)), pltpu.SemaphoreType.DMA((7 * n,)),
                        pltpu.SemaphoreType.DMA((n,))],
    )(*arrs))


HBM_SPEC = pl.BlockSpec(memory_space=pltpu.HBM)
SEM_SPEC = pl.BlockSpec(memory_space=pltpu.SEMAPHORE)
DATAFLOW = pltpu.SideEffectType.DATAFLOW_SIDE_EFFECTING


def _exchange_start(src, name):
    r, c = src.shape[1:]

    def body(src_ref, land_ref, send_sems, recv_sems, src_thru, land_thru, token):
        for k in range(1, N_DEV):
            peer = _peer(k)
            pltpu.make_async_remote_copy(
                src_ref=src_ref.at[_linear(peer)], dst_ref=land_ref.at[k - 1],
                send_sem=send_sems.at[k - 1], recv_sem=recv_sems.at[k - 1],
                device_id=peer, device_id_type=MESH_ID).start()
        token[...] = jnp.zeros_like(token)

    land = lax.empty((N_DEV - 1, r, c), src.dtype)
    return pl.pallas_call(
        body, name=name,
        out_shape=(pltpu.SemaphoreType.DMA((N_DEV - 1,)), pltpu.SemaphoreType.DMA((N_DEV - 1,)),
                   pltpu.HBM(src.shape, src.dtype), pltpu.HBM(land.shape, land.dtype),
                   jax.ShapeDtypeStruct((8, 128), F32)),
        in_specs=(HBM_SPEC, HBM_SPEC),
        out_specs=(SEM_SPEC, SEM_SPEC, HBM_SPEC, HBM_SPEC, pl.BlockSpec(memory_space=pltpu.VMEM)),
        input_output_aliases={0: 2, 1: 3},
        compiler_params=pltpu.CompilerParams(has_side_effects=DATAFLOW),
    )(pltpu.with_memory_space_constraint(src, pltpu.HBM), pltpu.with_memory_space_constraint(land, pltpu.HBM))


def _exchange_wait(started, after, name):
    send_sems, recv_sems, src_thru, land_thru, _ = started

    def body(src_ref, land_ref, send_sems, recv_sems, *rest):
        for k in range(1, N_DEV):
            cp = pltpu.make_async_remote_copy(
                src_ref=src_ref.at[0], dst_ref=land_ref.at[k - 1],
                send_sem=send_sems.at[k - 1], recv_sem=recv_sems.at[k - 1],
                device_id=_peer(k), device_id_type=MESH_ID)
            cp.wait_send()
            cp.wait_recv()

    return pl.pallas_call(
        body, name=name,
        out_shape=(pltpu.HBM(src_thru.shape, src_thru.dtype), pltpu.HBM(land_thru.shape, land_thru.dtype)),
        in_specs=(HBM_SPEC, HBM_SPEC, SEM_SPEC, SEM_SPEC) + (pl.BlockSpec(memory_space=pl.ANY),) * len(after),
        out_specs=(HBM_SPEC, HBM_SPEC), input_output_aliases={0: 0, 1: 1},
        compiler_params=pltpu.CompilerParams(has_side_effects=DATAFLOW),
    )(src_thru, land_thru, send_sems, recv_sems, *after)


def _share_start(arrs, zones, name):
    n = len(arrs)

    def body(*refs):
        src_refs, zone_refs, sems = refs[:n], refs[n:2 * n], refs[2 * n:4 * n]
        me = _linear(_mesh_position())
        for a in range(n):
            for k in range(1, N_DEV):
                pltpu.make_async_remote_copy(
                    src_ref=src_refs[a], dst_ref=zone_refs[a].at[me],
                    send_sem=sems[2 * a].at[k - 1], recv_sem=sems[2 * a + 1].at[k - 1],
                    device_id=_peer(k), device_id_type=MESH_ID).start()

    outs = pl.pallas_call(
        body, name=name,
        out_shape=tuple(pltpu.SemaphoreType.DMA((N_DEV - 1,)) for _ in range(2 * n))
        + tuple(pltpu.HBM(a.shape, a.dtype) for a in arrs) + tuple(pltpu.HBM(z.shape, z.dtype) for z in zones),
        in_specs=(HBM_SPEC,) * (2 * n),
        out_specs=(SEM_SPEC,) * (2 * n) + (HBM_SPEC,) * (2 * n),
        input_output_aliases={i: 2 * n + i for i in range(2 * n)},
        compiler_params=pltpu.CompilerParams(has_side_effects=DATAFLOW),
    )(*[pltpu.with_memory_space_constraint(a, pltpu.HBM) for a in list(arrs) + list(zones)])
    return outs[:2 * n], outs[2 * n:3 * n], outs[3 * n:]


def _share_wait(started, after, name):
    sems, arrs, zones = started
    n = len(arrs)

    def body(*refs):
        src_refs, zone_refs, sem_refs = refs[:n], refs[n:2 * n], refs[2 * n:4 * n]
        for a in range(n):
            for k in range(1, N_DEV):
                cp = pltpu.make_async_remote_copy(
                    src_ref=src_refs[a], dst_ref=zone_refs[a].at[_linear(_peer(k))],
                    send_sem=sem_refs[2 * a].at[k - 1], recv_sem=sem_refs[2 * a + 1].at[k - 1],
                    device_id=_peer(k), device_id_type=MESH_ID)
                cp.wait_send()
                cp.wait_recv()

    outs = pl.pallas_call(
        body, name=name,
        out_shape=tuple(pltpu.HBM(a.shape, a.dtype) for a in arrs) + tuple(pltpu.HBM(z.shape, z.dtype) for z in zones),
        in_specs=(HBM_SPEC,) * (2 * n) + (SEM_SPEC,) * (2 * n) + (pl.BlockSpec(memory_space=pl.ANY),) * len(after),
        out_specs=(HBM_SPEC,) * (2 * n), input_output_aliases={i: i for i in range(2 * n)},
        compiler_params=pltpu.CompilerParams(has_side_effects=DATAFLOW),
    )(*arrs, *zones, *sems, *after)
    return list(outs[n:])


def _stage_blocks(arrs, after, name):
    n = len(arrs)

    def body(*refs):
        in_refs, out_refs, stage, sems = refs[:n], refs[n + 1:2 * n + 1], refs[2 * n + 1:3 * n + 1], refs[3 * n + 1]
        me = _linear(_mesh_position())
        copies = []
        for a in range(n):
            stage[a][...] = in_refs[a][...].astype(BF16)
            copies.append(pltpu.make_async_copy(stage[a], out_refs[a].at[me], sems.at[a]))
            copies[-1].start()
        for cp in copies:
            cp.wait()

    return list(pl.pallas_call(
        body, name=name,
        out_shape=[jax.ShapeDtypeStruct((N_DEV,) + a.shape, BF16) for a in arrs],
        in_specs=[pl.BlockSpec(memory_space=pltpu.VMEM)] * n + [pl.BlockSpec(memory_space=pl.ANY)],
        out_specs=[pl.BlockSpec(memory_space=pl.ANY)] * n,
        scratch_shapes=[pltpu.VMEM(a.shape, BF16) for a in arrs] + [pltpu.SemaphoreType.DMA((n,))],
        compiler_params=_params(vmem=VMEM_LIMIT_LARGE),
    )(*arrs, after))


def _same_core_peers():
    x, y, c = _mesh_position()
    return [(x, y, 1 - c), (1 - x, y, c), (x, 1 - y, c), (1 - x, 1 - y, c)]


def _gather_start(bufs, name):
    n = len(bufs)

    def body(*refs):
        buf_refs, rest = refs[:n], refs[n:]
        sems, token = rest[:2 * n], rest[-1]
        me = _linear(_mesh_position())
        for a in range(n):
            for k, peer in enumerate(_same_core_peers()):
                pltpu.make_async_remote_copy(
                    src_ref=buf_refs[a].at[me], dst_ref=buf_refs[a].at[me],
                    send_sem=sems[2 * a].at[k], recv_sem=sems[2 * a + 1].at[k],
                    device_id=peer, device_id_type=MESH_ID).start()
        token[...] = jnp.zeros_like(token)

    outs = pl.pallas_call(
        body, name=name,
        out_shape=tuple(pltpu.SemaphoreType.DMA((4,)) for _ in range(2 * n))
        + tuple(pltpu.HBM(b.shape, b.dtype) for b in bufs) + (jax.ShapeDtypeStruct((8, 128), F32),),
        in_specs=(HBM_SPEC,) * n,
        out_specs=(SEM_SPEC,) * (2 * n) + (HBM_SPEC,) * n + (pl.BlockSpec(memory_space=pltpu.VMEM),),
        input_output_aliases={a: 2 * n + a for a in range(n)},
        compiler_params=pltpu.CompilerParams(has_side_effects=DATAFLOW),
    )(*[pltpu.with_memory_space_constraint(b, pltpu.HBM) for b in bufs])
    return outs[:2 * n], outs[2 * n:3 * n], outs[3 * n]


def _gather_wait(sems, bufs, after, name):
    n = len(bufs)

    def body(*refs):
        buf_refs, sem_refs = refs[:n], refs[n:3 * n]
        x, y, c = _mesh_position()
        me = _linear((x, y, c))
        for a in range(n):
            for k, peer in enumerate(_same_core_peers()):
                cp = pltpu.make_async_remote_copy(
                    src_ref=buf_refs[a].at[me], dst_ref=buf_refs[a].at[_linear(peer)],
                    send_sem=sem_refs[2 * a].at[k], recv_sem=sem_refs[2 * a + 1].at[k],
                    device_id=peer, device_id_type=MESH_ID)
                cp.wait_send()
                cp.wait_recv()

    return list(pl.pallas_call(
        body, name=name,
        out_shape=tuple(pltpu.HBM(b.shape, b.dtype) for b in bufs),
        in_specs=(HBM_SPEC,) * n + (SEM_SPEC,) * (2 * n) + (pl.BlockSpec(memory_space=pl.ANY),) * len(after),
        out_specs=(HBM_SPEC,) * n, input_output_aliases={a: a for a in range(n)},
        compiler_params=pltpu.CompilerParams(has_side_effects=DATAFLOW),
    )(*bufs, *sems, *after))


def _gather_pass_on(bufs, name):
    n = len(bufs)

    def body(*refs):
        out_refs = refs[n:2 * n]
        send_sems, recv_sems = refs[2 * n:]
        x, y, c = _mesh_position()
        sibling = (x, y, 1 - c)
        chips = [(1 - x, y), (x, 1 - y), (1 - x, 1 - y)]
        copies = []
        for a in range(n):
            for j, chip in enumerate(chips):
                block = out_refs[a].at[_linear((*chip, c))]
                copies.append(pltpu.make_async_remote_copy(
                    src_ref=block, dst_ref=block, send_sem=send_sems.at[3 * a + j], recv_sem=recv_sems.at[3 * a + j],
                    device_id=sibling, device_id_type=MESH_ID))
                copies[-1].start()
        for a in range(n):
            for j, chip in enumerate(chips):
                copies[3 * a + j].wait_send()
                theirs = out_refs[a].at[_linear((*chip, 1 - c))]
                pltpu.make_async_remote_copy(
                    src_ref=theirs, dst_ref=theirs, send_sem=send_sems.at[3 * a + j], recv_sem=recv_sems.at[3 * a + j],
                    device_id=sibling, device_id_type=MESH_ID).wait_recv()

    hbm = pl.BlockSpec(memory_space=pl.ANY)
    return list(pl.pallas_call(
        body, name=name,
        out_shape=[jax.ShapeDtypeStruct(b.shape, b.dtype) for b in bufs],
        in_specs=[hbm] * n, out_specs=[hbm] * n, input_output_aliases={a: a for a in range(n)},
        scratch_shapes=[pltpu.SemaphoreType.DMA((3 * n,)), pltpu.SemaphoreType.DMA((3 * n,))],
    )(*bufs))


def _open_step(c, conv_w, w_ada, b_cols, w_in_t):
    cols = w_ada.shape[1]

    def body(c_ref, cw_ref, wa_ref, b_ref, w_ref, cond_ref, conv_ref, mod_ref, win_ref,
             cond_own, mod_own, stage, s_send, s_recv, w_send, w_recv, local_sems):
        x, y, cc = _mesh_position()
        me = _linear((x, y, cc))
        sibling = (x, y, 1 - cc)
        chips = [(1 - x, y), (x, 1 - y), (1 - x, 1 - y)]
        v = c_ref[...]
        cond_own[...] = v * _sigmoid(v)
        stage[...] = w_ref[...].astype(BF16)

        def small(rnd, a, k, src, dst, slot):
            return pltpu.make_async_remote_copy(
                src_ref=src, dst_ref=dst.at[slot], send_sem=s_send.at[rnd, a, k - 1], recv_sem=s_recv.at[rnd, a, k - 1],
                device_id=_peer(k), device_id_type=MESH_ID)

        def block(p):
            return win_ref.at[_linear(p)]

        def big(k, blk, to, src=None):
            return pltpu.make_async_remote_copy(
                src_ref=block(blk) if src is None else src, dst_ref=block(blk),
                send_sem=w_send.at[k], recv_sem=w_recv.at[k], device_id=to, device_id_type=MESH_ID)

        mine = [pltpu.make_async_copy(cond_own, cond_ref.at[me], local_sems.at[0]),
                pltpu.make_async_copy(cw_ref, conv_ref.at[me], local_sems.at[1]),
                pltpu.make_async_copy(stage, block((x, y, cc)), local_sems.at[2])]
        for cp in mine:
            cp.start()
        sends = []
        for k in range(1, N_DEV):
            sends += [small(0, 0, k, cond_own, cond_ref, me), small(0, 1, k, cw_ref, conv_ref, me)]
        for cp in sends:
            cp.start()
        first = [big(0, (x, y, cc), sibling, src=stage)]
        first += [big(1 + j, (x, y, cc), (*chip, cc), src=stage) for j, chip in enumerate(chips)]
        for cp in first:
            cp.start()
        for k in range(1, N_DEV):
            small(0, 0, k, cond_own, cond_ref, _linear(_peer(k))).wait_recv()
            small(0, 1, k, cw_ref, conv_ref, _linear(_peer(k))).wait_recv()
        mine[0].wait()
        cond_all = jnp.concatenate([cond_ref[k] for k in range(N_DEV)], axis=0)
        mod_own[...] = _dot(cond_all, wa_ref[...]) + b_ref[...]
        mine.append(pltpu.make_async_copy(mod_own, mod_ref.at[me], local_sems.at[3]))
        mine[-1].start()
        second = [small(1, 0, k, mod_own, mod_ref, me) for k in range(1, N_DEV)]
        for cp in second:
            cp.start()
        passed = []
        for j, chip in enumerate(chips):
            big(1 + j, (*chip, cc), (x, y, cc)).wait_recv()
            fwd = big(4 + j, (*chip, cc), sibling)
            fwd.start()
            passed.append(fwd)
        big(0, sibling, (x, y, cc)).wait_recv()
        for j, chip in enumerate(chips):
            big(4 + j, (*chip, 1 - cc), (x, y, cc)).wait_recv()
        for k in range(1, N_DEV):
            small(1, 0, k, mod_own, mod_ref, _linear(_peer(k))).wait_recv()
        for cp in sends + first + second + passed:
            cp.wait_send()
        for cp in mine[1:]:
            cp.wait()

    vmem = pl.BlockSpec(memory_space=pltpu.VMEM)
    return pl.pallas_call(
        body, name="open_step",
        out_shape=[jax.ShapeDtypeStruct((N_DEV,) + c.shape, F32), jax.ShapeDtypeStruct((N_DEV,) + conv_w.shape, F32),
                   jax.ShapeDtypeStruct((N_DEV, N_DEV, cols), F32),
                   jax.ShapeDtypeStruct((N_DEV,) + w_in_t.shape, BF16)],
        in_specs=[vmem] * 5, out_specs=[vmem, vmem, vmem, ANY_SPEC],
        scratch_shapes=[pltpu.VMEM(c.shape, F32), pltpu.VMEM((N_DEV, cols), F32), pltpu.VMEM(w_in_t.shape, BF16),
                        pltpu.SemaphoreType.DMA((2, 2, N_DEV - 1)), pltpu.SemaphoreType.DMA((2, 2, N_DEV - 1)),
                        pltpu.SemaphoreType.DMA((7,)), pltpu.SemaphoreType.DMA((7,)),
                        pltpu.SemaphoreType.DMA((4,))],
        compiler_params=_params(vmem=VMEM_LIMIT_LARGE),
    )(c, conv_w, w_ada, b_cols, w_in_t)


def _in_proj(x, mod, g_norm1, w_in, tm):
    s = x.shape[0]

    def body(x_ref, mod_ref, g_ref, w_ref, h_ref, q_ref, kv_ref, gb_ref, gc_ref, xc_ref):
        xf = x_ref[...]
        n = xf * _rsqrt_mean_sq(xf) * g_ref[...]
        h = (n * (1.0 + mod_ref[SC1:SC1 + 1, :]) + mod_ref[SH1:SH1 + 1, :]).astype(BF16)
        h_ref[...] = h
        p = _dot_nt(h, w_ref[...])
        q_ref[...] = p[:, 0:512].astype(BF16)
        kv_ref[...] = p[:, 512:768].astype(BF16)
        gb_ref[...] = p[:, 768:1280].astype(BF16)
        gc_ref[...] = p[:, 1280:1792].astype(BF16)
        xc_ref[...] = p[:, 1792:2304].astype(BF16)

    return pl.pallas_call(
        body, name="in_proj", grid=(s // tm,),
        in_specs=[_rows(tm, D_MODEL), _full((8, D_MODEL)), _full((1, D_MODEL)), _full((IN_PROJ_WIDTH, D_MODEL))],
        out_specs=[_rows(tm, D_MODEL), _rows(tm, 512), _rows(tm, 256), _rows(tm, 512), _rows(tm, 512), _rows(tm, 512)],
        out_shape=[jax.ShapeDtypeStruct((s, D_MODEL), BF16), jax.ShapeDtypeStruct((s, 512), BF16),
                   jax.ShapeDtypeStruct((s, 256), BF16), jax.ShapeDtypeStruct((s, 512), BF16),
                   jax.ShapeDtypeStruct((s, 512), BF16), jax.ShapeDtypeStruct((s, 512), BF16)],
        compiler_params=_params(("arbitrary",), VMEM_LIMIT_LARGE),
    )(x, mod, g_norm1, w_in)


def _t5_bucket(dist):
    max_exact = N_BUCKETS // 2
    is_small = dist < max_exact
    d = jnp.maximum(dist, 1).astype(F32)
    large = max_exact + (jnp.log(d / max_exact) / math.log(MAX_DISTANCE / max_exact)
                         * (N_BUCKETS - max_exact)).astype(jnp.int32)
    large = jnp.minimum(large, N_BUCKETS - 1)
    return jnp.where(is_small, dist, large)


def _bucket_table():
    qi = jnp.arange(BLOCK, dtype=jnp.int32)[:, None]
    sj = jnp.arange(2 * BLOCK, dtype=jnp.int32)[None, :]
    return _t5_bucket(jnp.maximum(qi + BLOCK - sj, 0))


def _window_mask():
    qi = lax.broadcasted_iota(jnp.int32, (BLOCK, 2 * BLOCK), 0)
    sj = lax.broadcasted_iota(jnp.int32, (BLOCK, 2 * BLOCK), 1)
    dist = qi + BLOCK - sj
    return (dist >= 0) & (dist < BLOCK)


def _bias_table(rel_bias, bucket):
    def body(rb_ref, bk_ref, o_ref):
        bk = bk_ref[...]
        inside = _window_mask()
        for h in range(N_Q_HEADS):
            acc = jnp.zeros((BLOCK, 2 * BLOCK), F32)
            for b in range(N_BUCKETS):
                acc = jnp.where(bk == b, rb_ref[b, h], acc)
            o_ref[h] = jnp.where(inside, acc, NEG_INF)

    return pl.pallas_call(
        body, name="bias_table",
        in_specs=[pl.BlockSpec(memory_space=pltpu.SMEM), pl.BlockSpec(memory_space=pltpu.VMEM)],
        out_shape=jax.ShapeDtypeStruct((N_Q_HEADS, BLOCK, 2 * BLOCK), F32),
    )(rel_bias, bucket)


def _load_kv_window(kv_ref, n):
    prev = jnp.maximum(n - 1, 0)
    kvw = jnp.concatenate([kv_ref[pl.ds(pl.multiple_of(prev * BLOCK, BLOCK), BLOCK), :],
                           kv_ref[pl.ds(pl.multiple_of(n * BLOCK, BLOCK), BLOCK), :]], axis=0)
    k, v = kvw[:, 0:128], kvw[:, 128:256]
    k_sw = pltpu.roll(k.astype(F32), 64, 1).astype(BF16)
    v_sw = pltpu.roll(v.astype(F32), 64, 1).astype(BF16)
    return (k, k_sw), (v, v_sw)


def _conv_taps(gc, xc, gc_prev, xc_prev, n):
    u = gc * xc
    before = jnp.where(n > 0, gc_prev.astype(F32) * xc_prev.astype(F32), 0.0)
    last = before.shape[0] - 1
    row = lax.broadcasted_iota(jnp.int32, u.shape, 0)
    u1 = jnp.where(row == 0, before[last:last + 1, :], pltpu.roll(u, 1, 0))
    u2 = jnp.where(row == 0, before[last - 1:last, :],
                   jnp.where(row == 1, before[last:last + 1, :], pltpu.roll(u, 2, 0)))
    return u, u1, u2


def _mixer_fwd(q, kv, gb, gc, xc, bias, sinks, conv_w, g_attn, g_conv):
    s = q.shape[0]
    nb = s // BLOCK

    per_step = min(MIXER_BLOCKS, nb)
    tile = per_step * BLOCK

    def one_block(n, rows, before, sink_ref, q_ref, kv_ref, gb_ref, gc_ref, xc_ref, bias_ref, cw_ref, ga_ref,
                  gcv_ref, attn_ref, merged_ref, lse_ref):
        ks, vs = _load_kv_window(kv_ref, n)
        lane = lax.broadcasted_iota(jnp.int32, (BLOCK, BLOCK), 1)
        low = lane < HEAD_DIM
        col = lax.broadcasted_iota(jnp.int32, (BLOCK, 2 * BLOCK), 1)
        no_prev = (col < BLOCK) & (n == 0)
        lse_all = jnp.zeros((BLOCK, BLOCK), F32)
        pairs = []
        for p in range(4):
            qp = q_ref[rows, 128 * p:128 * (p + 1)].astype(F32)
            kvh = p // 2
            res = []
            for e in range(2):
                h = 2 * p + e
                qm = jnp.where(low if e == 0 else ~low, qp, 0.0).astype(BF16)
                sw = 0 if kvh == e else 1
                sc = _dot_nt(qm, ks[sw]) * SCALE + bias_ref[h]
                sc = jnp.where(no_prev, NEG_INF, sc)
                sink = sink_ref[h]
                m = jnp.maximum(jnp.max(sc, axis=-1, keepdims=True), sink)
                pe = jnp.exp(sc - m)
                den = jnp.sum(pe, axis=-1, keepdims=True) + jnp.exp(sink - m)
                res.append(_dot(pe.astype(BF16), vs[sw]) / den)
                lse_all = lse_all + jnp.where(lane == h, m + jnp.log(den), 0.0)
            pairs.append(jnp.where(low, res[0], res[1]))
        attn = jnp.concatenate(pairs, axis=1)
        attn_ref[rows, :] = attn
        lse_ref[rows, :] = lse_all
        u, u1, u2 = _conv_taps(gc_ref[rows, :].astype(F32), xc_ref[rows, :].astype(F32), before[0], before[1], n)
        cw = cw_ref[...]
        cv = gb_ref[rows, :].astype(F32) * (cw[0:1, :] * u2 + cw[1:2, :] * u1 + cw[2:3, :] * u)
        an = attn * _rsqrt_mean_sq(attn) * ga_ref[...]
        cn = cv * _rsqrt_mean_sq(cv) * gcv_ref[...]
        merged_ref[rows, :] = jnp.concatenate([an, cn], axis=1).astype(BF16)

    def body(sink_ref, q_ref, kv_ref, gb_ref, gc_ref, xc_ref, gcp_ref, xcp_ref, *rest):
        step = pl.program_id(0)
        for sub in range(per_step):
            rows = slice(sub * BLOCK, (sub + 1) * BLOCK)
            ahead = slice(sub * BLOCK - PREV_ROWS, sub * BLOCK)
            before = (gcp_ref[...], xcp_ref[...]) if sub == 0 else (gc_ref[ahead, :], xc_ref[ahead, :])
            one_block(step * per_step + sub, rows, before, sink_ref, q_ref, kv_ref, gb_ref, gc_ref, xc_ref, *rest)

    blk = lambda w: pl.BlockSpec((tile, w), lambda n: (n, 0))
    prev8 = pl.BlockSpec((PREV_ROWS, 512), lambda n: (jnp.maximum(n * (tile // PREV_ROWS) - 1, 0), 0))
    return pl.pallas_call(
        body, name="mixer_fwd", grid=(nb // per_step,),
        in_specs=[pl.BlockSpec(memory_space=pltpu.SMEM), blk(512), _full((s, 256)), blk(512), blk(512), blk(512),
                  prev8, prev8, _full((N_Q_HEADS, BLOCK, 2 * BLOCK)), _full((3, 512)), _full((1, 512)),
                  _full((1, 512))],
        out_specs=[blk(512), blk(1024), blk(128)],
        out_shape=[jax.ShapeDtypeStruct((s, 512), F32), jax.ShapeDtypeStruct((s, 1024), BF16),
                   jax.ShapeDtypeStruct((s, 128), F32)],
        compiler_params=_params(("arbitrary",)),
    )(sinks, q, kv, gb, gc, xc, gc, xc, bias, conv_w, g_attn, g_conv)


def _out_proj(merged, x, mod, w_out, tm):
    s = x.shape[0]

    def body(m_ref, x_ref, mod_ref, w_ref, o_ref, x1_ref):
        o = _dot(m_ref[...], w_ref[...])
        o_ref[...] = o.astype(BF16)
        x1_ref[...] = x_ref[...] + mod_ref[G1:G1 + 1, :] * o

    return pl.pallas_call(
        body, name="out_proj", grid=(s // tm,),
        in_specs=[_rows(tm, D_MODEL), _rows(tm, D_MODEL), _full((8, D_MODEL)), _full((D_MODEL, D_MODEL))],
        out_specs=[_rows(tm, D_MODEL), _rows(tm, D_MODEL)],
        out_shape=[jax.ShapeDtypeStruct((s, D_MODEL), BF16), jax.ShapeDtypeStruct((s, D_MODEL), F32)],
        compiler_params=_params(("arbitrary",)),
    )(merged, x, mod, w_out)


def _resident(shape):
    nd = len(shape)
    return pl.BlockSpec(shape, lambda *_: (0,) * nd, pipeline_mode=pl.Buffered(1))


def _ffn(x1, o1, mod, g_norm2, w_gu, w_down, w_out, g_final, target, tm):
    s = x1.shape[0]
    chunk = D_FF // FFN_CHUNKS

    def body(x_ref, o1_ref, mod_ref, g_ref, wgu_ref, wd_ref, wo_ref, gf_ref, t_ref,
             h_ref, act_ref, do_ref, dgu_ref, dx1_ref, do1_ref, dm_ref, small_ref):
        @pl.when(pl.program_id(0) == 0)
        def _():
            small_ref[...] = jnp.zeros_like(small_ref)

        xf = x_ref[...]
        n = xf * _rsqrt_mean_sq(xf) * g_ref[...]
        h = (n * (1.0 + mod_ref[SC2:SC2 + 1, :]) + mod_ref[SH2:SH2 + 1, :]).astype(BF16)
        h_ref[...] = h
        gates, ups, o = [], [], None
        for j in range(FFN_CHUNKS):
            lo = j * chunk
            gate = _dot_nt(h, wgu_ref[lo:lo + chunk, :])
            up = _dot_nt(h, wgu_ref[D_FF + lo:D_FF + lo + chunk, :])
            sg = _sigmoid(gate)
            act = (gate * sg * up).astype(BF16)
            act_ref[:, lo:lo + chunk] = act
            gates.append((up * (sg * (1.0 + gate * (1.0 - sg)))).astype(BF16))
            ups.append((gate * sg).astype(BF16))
            part = _dot(act, wd_ref[lo:lo + chunk, :])
            o = part if o is None else o + part
        g2 = mod_ref[G2:G2 + 1, :]
        x2 = xf + g2 * o
        r = _rsqrt_mean_sq(x2)
        xn = x2 * r
        gf = gf_ref[...]
        err = xn * gf - t_ref[...]
        dy = err * (1.0 / D_MODEL)
        dxn = dy * gf
        dx2 = r * (dxn - xn * jnp.mean(dxn * xn, axis=-1, keepdims=True))
        small_ref[4:5, :] += _colsum(dy * xn)
        small_ref[5:6, :] += _colsum(err * err)
        small_ref[3:4, :] += _colsum(dx2 * o)
        do = (dx2 * g2).astype(BF16)
        do_ref[...] = do
        dh = None
        for j in range(FFN_CHUNKS):
            lo = j * chunk
            dact = _dot_nt(do, wd_ref[lo:lo + chunk, :])
            dgate = (dact * gates[j].astype(F32)).astype(BF16)
            dup = (dact * ups[j].astype(F32)).astype(BF16)
            dgu_ref[:, lo:lo + chunk] = dgate
            dgu_ref[:, D_FF + lo:D_FF + lo + chunk] = dup
            part = _dot(dgate, wgu_ref[lo:lo + chunk, :]) + _dot(dup, wgu_ref[D_FF + lo:D_FF + lo + chunk, :])
            dh = part if dh is None else dh + part
        dx1 = dx2 + _norm_mod_bwd(dh, xf, g_ref[...], mod_ref[SC2:SC2 + 1, :], small_ref)
        dx1_ref[...] = dx1.astype(BF16)
        small_ref[7:8, :] += _colsum(dx1 * o1_ref[...].astype(F32))
        do1 = (dx1 * mod_ref[G1:G1 + 1, :]).astype(BF16)
        do1_ref[...] = do1
        dm_ref[...] = _dot_nt(do1, wo_ref[...]).astype(BF16)

        @pl.when(pl.program_id(0) == pl.num_programs(0) - 1)
        def _():
            total = jnp.sum(small_ref[5:6, :], axis=-1, keepdims=True) * (0.5 / D_MODEL)
            small_ref[6:7, :] = jnp.broadcast_to(total, (1, D_MODEL))

    narrow = jax.ShapeDtypeStruct((s, D_MODEL), BF16)
    return pl.pallas_call(
        body, name="ffn", grid=(s // tm,),
        in_specs=[_rows(tm, D_MODEL), _rows(tm, D_MODEL), _full((8, D_MODEL)), _full((1, D_MODEL)),
                  _resident((2 * D_FF, D_MODEL)), _resident((D_FF, D_MODEL)), _resident((D_MODEL, D_MODEL)),
                  _full((1, D_MODEL)), _rows(tm, D_MODEL)],
        out_specs=[_rows(tm, D_MODEL), _rows(tm, D_FF), _rows(tm, D_MODEL), _rows(tm, 2 * D_FF), _rows(tm, D_MODEL),
                   _rows(tm, D_MODEL), _rows(tm, D_MODEL), _full((8, D_MODEL))],
        out_shape=[narrow, jax.ShapeDtypeStruct((s, D_FF), BF16), narrow, jax.ShapeDtypeStruct((s, 2 * D_FF), BF16),
                   narrow, narrow, narrow, jax.ShapeDtypeStruct((8, D_MODEL), F32)],
        compiler_params=_params(("arbitrary",), VMEM_LIMIT_LARGE),
    )(x1, o1, mod, g_norm2, w_gu, w_down, w_out, g_final, target)


def _norm_mod_bwd(dh, xf, g, scale_row, small_ref):
    r = _rsqrt_mean_sq(xf)
    xn = xf * r
    small_ref[0:1, :] += _colsum(dh)
    small_ref[1:2, :] += _colsum(dh * (xn * g))
    dn = dh * (1.0 + scale_row)
    small_ref[2:3, :] += _colsum(dn * xn)
    dxn = dn * g
    return r * (dxn - xn * jnp.mean(dxn * xn, axis=-1, keepdims=True))


def _group_norm_bwd(dm, a, g):
    r = _rsqrt_mean_sq(a)
    an = a * r
    dan = dm * g
    return r * (dan - an * jnp.mean(dan * an, axis=-1, keepdims=True)), _colsum(dm * an)


def _mixer_bwd(after, q, kv, gb, gc, xc, bias, sinks, conv_w, g_attn, g_conv, attn, lse, dmerged):
    s = q.shape[0]
    nb = s // BLOCK

    per_step = min(MIXER_BLOCKS, nb)
    tile = per_step * BLOCK
    steps = nb // per_step

    def one_block(n, rows, before, nxt, sink_ref, q_ref, kv_ref, gb_ref, gc_ref, xc_ref, bias_ref, cw_ref, ga_ref,
                  gcv_ref, attn_ref, lse_ref, dm_ref, dproj_ref, dbias_ref, dsink_ref, small_ref):
        next_dy, next_dkv = nxt
        dm = dm_ref[rows, :].astype(F32)
        gbv, gcv_, xcv = gb_ref[rows, :].astype(F32), gc_ref[rows, :].astype(F32), xc_ref[rows, :].astype(F32)
        u, u1, u2 = _conv_taps(gcv_, xcv, before[0], before[1], n)
        cw = cw_ref[...]
        yv = cw[0:1, :] * u2 + cw[1:2, :] * u1 + cw[2:3, :] * u
        dcv, dg_conv = _group_norm_bwd(dm[:, 512:1024], gbv * yv, gcv_ref[...])
        small_ref[1:2, :] += dg_conv
        dproj_ref[rows, 768:1280] = (dcv * yv).astype(BF16)
        dy = dcv * gbv
        row = lax.broadcasted_iota(jnp.int32, dy.shape, 0)
        d1 = jnp.where(row == BLOCK - 1, next_dy[0:1, :], pltpu.roll(dy, BLOCK - 1, 0))
        d2 = jnp.where(row == BLOCK - 2, next_dy[0:1, :],
                       jnp.where(row == BLOCK - 1, next_dy[1:2, :], pltpu.roll(dy, BLOCK - 2, 0)))
        du = cw[2:3, :] * dy + cw[1:2, :] * d1 + cw[0:1, :] * d2
        dproj_ref[rows, 1280:1792] = (du * xcv).astype(BF16)
        dproj_ref[rows, 1792:2304] = (du * gcv_).astype(BF16)
        small_ref[2:3, :] += _colsum(dy * u2)
        small_ref[3:4, :] += _colsum(dy * u1)
        small_ref[4:5, :] += _colsum(dy * u)

        attn_v = attn_ref[rows, :]
        dout, dg_attn = _group_norm_bwd(dm[:, 0:512], attn_v, ga_ref[...])
        small_ref[0:1, :] += dg_attn
        ks, vs = _load_kv_window(kv_ref, n)
        lane = lax.broadcasted_iota(jnp.int32, (BLOCK, BLOCK), 1)
        low = lane < HEAD_DIM
        col = lax.broadcasted_iota(jnp.int32, (BLOCK, 2 * BLOCK), 1)
        no_prev = (col < BLOCK) & (n == 0)
        lse_all = lse_ref[rows, :]
        dsink = jnp.zeros((BLOCK, BLOCK), F32)
        dq_pairs = []
        dk_groups, dv_groups = [], []
        for kvh in range(2):
            ds_rows, pr_rows, q_rows, do_rows = [], [], [], []
            for p in (2 * kvh, 2 * kvh + 1):
                qp = q_ref[rows, 128 * p:128 * (p + 1)].astype(F32)
                do_p = dout[:, 128 * p:128 * (p + 1)]
                prod = do_p * attn_v[:, 128 * p:128 * (p + 1)]
                res = []
                for e in range(2):
                    h = 2 * p + e
                    half = low if e == 0 else ~low
                    qm = jnp.where(half, qp, 0.0).astype(BF16)
                    dom = jnp.where(half, do_p, 0.0).astype(BF16)
                    delta = jnp.sum(jnp.where(half, prod, 0.0), axis=-1, keepdims=True)
                    lse_h = jnp.sum(jnp.where(lane == h, lse_all, 0.0), axis=-1, keepdims=True)
                    sw = 0 if kvh == e else 1
                    sc = _dot_nt(qm, ks[sw]) * SCALE + bias_ref[h]
                    sc = jnp.where(no_prev, NEG_INF, sc)
                    pr = jnp.exp(sc - lse_h)
                    dp = _dot_nt(dom, vs[sw])
                    ds = pr * (dp - delta)
                    dbias_ref[h] += ds
                    dsink = dsink + jnp.where(lane == h, -jnp.exp(sink_ref[h] - lse_h) * delta, 0.0)
                    dsb = ds.astype(BF16)
                    res.append(_dot(dsb, ks[sw]) * SCALE)
                    ds_rows.append(dsb)
                    pr_rows.append(pr.astype(BF16))
                    q_rows.append(qm)
                    do_rows.append(dom)
                dq_pairs.append(jnp.where(low, res[0], res[1]))
            dk_g = _dot_tn(jnp.concatenate(ds_rows, axis=0), jnp.concatenate(q_rows, axis=0)) * SCALE
            dv_g = _dot_tn(jnp.concatenate(pr_rows, axis=0), jnp.concatenate(do_rows, axis=0))
            dk_groups.append(dk_g + pltpu.roll(dk_g, 64, 1))
            dv_groups.append(dv_g + pltpu.roll(dv_g, 64, 1))
        dproj_ref[rows, 0:512] = jnp.concatenate(dq_pairs, axis=1).astype(BF16)
        dsink_ref[...] += dsink
        low_kv = lax.broadcasted_iota(jnp.int32, (2 * BLOCK, BLOCK), 1) < HEAD_DIM
        dkv_win = jnp.concatenate([jnp.where(low_kv, dk_groups[0], dk_groups[1]),
                                   jnp.where(low_kv, dv_groups[0], dv_groups[1])], axis=1)
        dproj_ref[rows, 512:768] = (dkv_win[BLOCK:2 * BLOCK, :] + next_dkv).astype(BF16)
        return dy[0:8, :], dkv_win[0:BLOCK, :]

    def body(sink_ref, q_ref, kv_ref, gb_ref, gc_ref, xc_ref, gcp_ref, xcp_ref, *rest):
        refs, dy_ref, dkv_ref = rest[:-2], rest[-2], rest[-1]
        dbias_ref, dsink_ref, small_ref = refs[8], refs[9], refs[10]
        step = pl.program_id(0)

        @pl.when(step == 0)
        def _():
            dbias_ref[...] = jnp.zeros_like(dbias_ref)
            dsink_ref[...] = jnp.zeros_like(dsink_ref)
            small_ref[...] = jnp.zeros_like(small_ref)
            dy_ref[...] = jnp.zeros_like(dy_ref)
            dkv_ref[...] = jnp.zeros_like(dkv_ref)

        nxt = (dy_ref[...], dkv_ref[...])
        for sub in reversed(range(per_step)):
            rows = slice(sub * BLOCK, (sub + 1) * BLOCK)
            ahead = slice(sub * BLOCK - PREV_ROWS, sub * BLOCK)
            before = (gcp_ref[...], xcp_ref[...]) if sub == 0 else (gc_ref[ahead, :], xc_ref[ahead, :])
            nxt = one_block((steps - 1 - step) * per_step + sub, rows, before, nxt,
                            sink_ref, q_ref, kv_ref, gb_ref, gc_ref, xc_ref, *refs)
        dy_ref[...], dkv_ref[...] = nxt

        @pl.when(step == steps - 1)
        def _():
            small_ref[5:6, :] = jnp.concatenate([_colsum(dsink_ref[...]), jnp.zeros((1, 512 - BLOCK), F32)], axis=1)

    blk = lambda w: pl.BlockSpec((tile, w), lambda t: (steps - 1 - t, 0))
    prev8 = pl.BlockSpec((PREV_ROWS, 512),
                         lambda t: (jnp.maximum((steps - 1 - t) * (tile // PREV_ROWS) - 1, 0), 0))
    bf = lambda w: jax.ShapeDtypeStruct((s, w), BF16)
    return pl.pallas_call(
        _coming_behind(body), name="mixer_bwd", grid=(steps,),
        in_specs=[ANY_SPEC, pl.BlockSpec(memory_space=pltpu.SMEM), blk(512), _full((s, 256)), blk(512), blk(512), blk(512),
                  prev8, prev8, _full((N_Q_HEADS, BLOCK, 2 * BLOCK)), _full((3, 512)), _full((1, 512)),
                  _full((1, 512)), blk(512), blk(128), blk(1024)],
        out_specs=[blk(IN_PROJ_WIDTH), _full((N_Q_HEADS, BLOCK, 2 * BLOCK)), _full((BLOCK, BLOCK)), _full((8, 512))],
        out_shape=[bf(IN_PROJ_WIDTH), jax.ShapeDtypeStruct((N_Q_HEADS, BLOCK, 2 * BLOCK), F32),
                   jax.ShapeDtypeStruct((BLOCK, BLOCK), F32), jax.ShapeDtypeStruct((8, 512), F32)],
        scratch_shapes=[pltpu.VMEM((8, 512), F32), pltpu.VMEM((BLOCK, 2 * KV_WIDTH), F32)],
        compiler_params=_params(("arbitrary",), VMEM_LIMIT_LARGE),
    )(after, sinks, q, kv, gb, gc, xc, gc, xc, bias, conv_w, g_attn, g_conv, attn, lse, dmerged)


def _in_proj_bwd(after, dproj, x, dx1, mod, g_norm1, w_in, tm):
    s = x.shape[0]

    def body(dproj_ref, x_ref, dx1_ref, mod_ref, g_ref, w_ref, dx_ref, small_ref):
        @pl.when(pl.program_id(0) == 0)
        def _():
            small_ref[...] = jnp.zeros_like(small_ref)

        dh = _dot(dproj_ref[...], w_ref[...])
        dx_ref[...] = dx1_ref[...].astype(F32) + _norm_mod_bwd(dh, x_ref[...], g_ref[...], mod_ref[SC1:SC1 + 1, :],
                                                               small_ref)

    return pl.pallas_call(
        _coming_behind(body), name="in_proj_bwd", grid=(s // tm,),
        in_specs=[ANY_SPEC, _rows(tm, IN_PROJ_WIDTH), _rows(tm, D_MODEL), _rows(tm, D_MODEL), _full((8, D_MODEL)),
                  _full((1, D_MODEL)), _full((IN_PROJ_WIDTH, D_MODEL))],
        out_specs=[_rows(tm, D_MODEL), _full((8, D_MODEL))],
        out_shape=[jax.ShapeDtypeStruct((s, D_MODEL), F32), jax.ShapeDtypeStruct((8, D_MODEL), F32)],
        compiler_params=_params(("arbitrary",), VMEM_LIMIT_LARGE),
    )(after, dproj, x, dx1, mod, g_norm1, w_in)


def _weight_grad(a, b, tk, ts, name, after=None):
    s, k = a.shape
    n = b.shape[1]
    nt = s // ts
    extra = [] if after is None else [after]

    def body(a_ref, b_ref, *rest):
        o_ref, acc_ref = rest[-2:]
        t = pl.program_id(1)
        @pl.when(t == 0)
        def _():
            acc_ref[...] = jnp.zeros_like(acc_ref)

        acc = acc_ref[...] + _dot_tn(a_ref[...], b_ref[...])
        acc_ref[...] = acc
        o_ref[...] = acc.astype(BF16)

    return pl.pallas_call(
        body, name=name, grid=(k // tk, nt),
        in_specs=[pl.BlockSpec((ts, tk), lambda i, t: (t, i)), pl.BlockSpec((ts, n), lambda i, t: (t, 0))]
        + [ANY_SPEC] * len(extra),
        out_specs=pl.BlockSpec((tk, n), lambda i, t: (i, 0)),
        out_shape=jax.ShapeDtypeStruct((k, n), BF16),
        scratch_shapes=[pltpu.VMEM((tk, n), F32)],
        compiler_params=_params(("arbitrary", "arbitrary"), VMEM_LIMIT_LARGE),
    )(a, b, *extra)


def _rel_bias_grad(dbias, bucket):
    def body(db_ref, bk_ref, o_ref, rows_ref):
        bk = bk_ref[...]
        for b in range(N_BUCKETS):
            sel = (bk == b).astype(F32)
            for h in range(N_Q_HEADS):
                rows_ref[N_BUCKETS * h + b:N_BUCKETS * h + b + 1, :] = _colsum(db_ref[h] * sel)
        head = lax.broadcasted_iota(jnp.int32, (N_BUCKETS, N_Q_HEADS), 1)
        out = jnp.zeros((N_BUCKETS, N_Q_HEADS), F32)
        for h in range(N_Q_HEADS):
            per_bucket = jnp.sum(rows_ref[N_BUCKETS * h:N_BUCKETS * (h + 1), :], axis=-1, keepdims=True)
            out = out + jnp.where(head == h, per_bucket, 0.0)
        o_ref[...] = out

    return pl.pallas_call(
        body, name="rel_bias_grad",
        out_shape=jax.ShapeDtypeStruct((N_BUCKETS, N_Q_HEADS), F32),
        scratch_shapes=[pltpu.VMEM((N_BUCKETS * N_Q_HEADS, 2 * BLOCK), F32)],
    )(dbias, bucket)


def _lanes_from(x, start, width):
    n = x.shape[1]
    return pltpu.roll(x, (n - start) % n, 1)[:, 0:width]


def _w_ada_grad(me, cond_all, packed_all, cols):
    def body(me_ref, c_ref, p_ref, o_ref):
        dmod = jnp.concatenate([p_ref[k][:, OFF_DMOD:OFF_DMOD + N_MOD * D_MODEL] for k in range(N_DEV)], axis=0)
        mine = _lanes_from(dmod, me_ref[0] * cols, cols)
        pad = lambda a: jnp.concatenate([a, jnp.zeros((128 - N_DEV, a.shape[1]), F32)], axis=0)
        o_ref[...] = _dot_tn(pad(c_ref[...]), pad(mine))

    vmem = pl.BlockSpec(memory_space=pltpu.VMEM)
    return pl.pallas_call(body, name="w_ada_grad",
                          in_specs=[pl.BlockSpec(memory_space=pltpu.SMEM), vmem, vmem],
                          out_shape=jax.ShapeDtypeStruct((cond_all.shape[1], cols), F32))(me, cond_all, packed_all)


SMALL_PARAMS = (("rel_bias", None), ("b_ada", (OFF_DMOD, N_MOD * D_MODEL)), ("g_norm1", (OFF_GN1, D_MODEL)),
                ("sinks", (OFF_SINK, N_Q_HEADS)), ("conv_w", None), ("g_attn_out", (OFF_GATT, ATTN_WIDTH)),
                ("g_conv_out", (OFF_GCV, CONV_WIDTH)), ("g_norm2", (OFF_GN2, D_MODEL)),
                ("g_final", (OFF_GFIN, D_MODEL)))


def _small_update(me, packed_all, rel_all, state, after):
    n_p = len(SMALL_PARAMS)
    flat = [a for triple in state for a in triple]
    conv_cols = state[4][0].shape[1]

    def body(me_ref, p_ref, r_ref, *refs):
        ins = refs[:3 * n_p]
        loss_ref, outs = refs[3 * n_p + len(after)], refs[3 * n_p + len(after) + 1:]
        small, rel = p_ref[0], r_ref[0]
        for k in range(1, N_DEV):
            small = small + p_ref[k]
            rel = rel + r_ref[k]
        loss_ref[...] = small[:, OFF_LOSS:OFF_LOSS + 128]
        taps = jnp.concatenate([small[:, OFF_CONVW + CONV_WIDTH * j:OFF_CONVW + CONV_WIDTH * (j + 1)]
                                for j in range(3)] + [jnp.zeros((5, CONV_WIDTH), F32)], axis=0)
        conv_g = _lanes_from(taps, me_ref[0] * conv_cols, conv_cols)[0:3, :]
        for i, (name, lanes) in enumerate(SMALL_PARAMS):
            g = rel if name == "rel_bias" else conv_g if name == "conv_w" else small[:, lanes[0]:lanes[0] + lanes[1]]
            w_ref, m_ref, v_ref = ins[3 * i:3 * i + 3]
            outs[4 * i][...] = g
            outs[4 * i + 1][...], outs[4 * i + 2][...], outs[4 * i + 3][...] = _adam_math(
                w_ref[...], g, m_ref[...], v_ref[...])

    vmem = pl.BlockSpec(memory_space=pltpu.VMEM)
    out_shape = [jax.ShapeDtypeStruct((1, 128), F32)]
    for w, _, _ in state:
        out_shape += [jax.ShapeDtypeStruct(w.shape, F32)] * 4
    outs = pl.pallas_call(
        body, name="small_update",
        in_specs=[pl.BlockSpec(memory_space=pltpu.SMEM), vmem, vmem] + [vmem] * len(flat)
        + [pl.BlockSpec(memory_space=pl.ANY)] * len(after),
        out_shape=out_shape,
    )(me, packed_all, rel_all, *flat, *after)
    return outs[0], [tuple(outs[1 + 4 * i:5 + 4 * i]) for i in range(n_p)]


def _adam_math(w, g, m, v):
    m = ADAM_B1 * m + (1.0 - ADAM_B1) * g
    v = ADAM_B2 * v + (1.0 - ADAM_B2) * (g * g)
    m_hat = m / (1.0 - ADAM_B1 ** ADAM_STEP)
    v_hat = v / (1.0 - ADAM_B2 ** ADAM_STEP)
    delta = -ADAM_LR * (m_hat / (jnp.sqrt(v_hat) + ADAM_EPS) + ADAM_WD * w)
    return delta, m, v


def _adamw_parts(w, m, v, local, land, me, tr, name):
    r, c = w.shape

    def body(me_ref, w_ref, m_ref, v_ref, own_ref, land_ref, g_ref, d_ref, mo_ref, vo_ref):
        g = own_ref[0].astype(F32)
        for k in range(N_DEV - 1):
            g = g + land_ref[k].astype(F32)
        g_ref[...] = g
        d_ref[...], mo_ref[...], vo_ref[...] = _adam_math(w_ref[...], g, m_ref[...], v_ref[...])

    tile = pl.BlockSpec((tr, c), lambda i, me_ref: (i, 0))
    return pl.pallas_call(
        body, name=name,
        grid_spec=pltpu.PrefetchScalarGridSpec(
            num_scalar_prefetch=1, grid=(r // tr,),
            in_specs=[tile, tile, tile, pl.BlockSpec((1, tr, c), lambda i, me_ref: (me_ref[0], i, 0)),
                      pl.BlockSpec((N_DEV - 1, tr, c), lambda i, me_ref: (0, i, 0))],
            out_specs=[tile] * 4),
        out_shape=[jax.ShapeDtypeStruct((r, c), F32)] * 4,
        compiler_params=_params(("arbitrary",)),
    )(me, w, m, v, local, land)


def _adamw(w, m, v, g, tr, name):
    r, c = w.shape

    def body(w_ref, m_ref, v_ref, g_ref, d_ref, mo_ref, vo_ref):
        d_ref[...], mo_ref[...], vo_ref[...] = _adam_math(w_ref[...], g_ref[...], m_ref[...], v_ref[...])

    tile = pl.BlockSpec((tr, c), lambda i: (i, 0))
    return pl.pallas_call(
        body, name=name, grid=(r // tr,),
        in_specs=[tile] * 4, out_specs=[tile] * 3,
        out_shape=[jax.ShapeDtypeStruct((r, c), F32)] * 3,
        compiler_params=_params(("arbitrary",)),
    )(w, m, v, g)


def _behind(a, token):
    return a + token[0:a.shape[0], 0:1]


def _local_step(x, target, mod, w_in_t, weights_out_gu, weights_down, rel_bias, g_norm1, sinks, conv_w, g_attn,
                g_conv, g_norm2, g_final, exchange):
    s = x.shape[0]
    tm = min(512, s)
    tm_small = min(256, s)
    bucket = _bucket_table()
    bias = _bias_table(rel_bias, bucket)

    h, q, kv, gb, gc, xc = _in_proj(x, mod, g_norm1, w_in_t, tm)
    attn, merged, lse = _mixer_fwd(q, kv, gb, gc, xc, bias, sinks, conv_w, g_attn, g_conv)
    w_out, w_gu_t = weights_out_gu(merged)
    o1, x1 = _out_proj(merged, x, mod, w_out, tm)
    w_down = weights_down(x1)
    h2, act, do2, dgu, dx1, do1, dmerged, sm_2 = _ffn(x1, o1, mod, g_norm2, w_gu_t, w_down, w_out, g_final, target,
                                                      tm_small)
    ts = min(WEIGHT_GRAD_ROWS, s)
    tok_down = exchange("w_down", _weight_grad(act, do2, D_FF // 2, ts, "w_down_grad"))
    tok_gu = exchange("w_gu", _weight_grad(dgu, h2, D_FF // 2, ts, "w_gu_grad", after=tok_down))
    tok_out = exchange("w_out", _weight_grad(merged, do1, D_MODEL, ts, "w_out_grad", after=tok_gu))
    dproj, dbias, dsink, sm_mix = _mixer_bwd(
        tok_out, q, kv, gb, gc, xc, bias, sinks, conv_w, g_attn, g_conv, attn, lse, dmerged)
    tok_in = exchange("w_in", _weight_grad(dproj, h, IN_PROJ_WIDTH // 2, ts, "w_in_grad"))
    dx, sm_1 = _in_proj_bwd(tok_in, dproj, x, dx1, mod, g_norm1, w_in_t, tm)
    d_rel = _rel_bias_grad(dbias, bucket)

    packed = jnp.concatenate([
        sm_1[0], sm_1[1], sm_2[7], sm_2[0], sm_2[1], sm_2[3],
        sm_1[2],
        sm_mix[5, 0:128],
        sm_mix[0], sm_mix[1],
        sm_2[2],
        sm_2[4],
        sm_mix[2], sm_mix[3], sm_mix[4],
        sm_2[6, 0:128],
    ])[None, :]
    return dx, packed, d_rel


def kernel(x, c, rel_bias, w_ada, b_ada, g_norm1, w_in, sinks, conv_w, g_attn_out, g_conv_out, w_out, g_norm2, w_gu, w_down, g_final, loss_target, m_rel_bias, m_w_ada, m_b_ada, m_g_norm1, m_w_in, m_sinks, m_conv_w, m_g_attn_out, m_g_conv_out, m_w_out, m_g_norm2, m_w_gu, m_w_down, m_g_final, v_rel_bias, v_w_ada, v_b_ada, v_g_norm1, v_w_in, v_sinks, v_conv_w, v_g_attn_out, v_g_conv_out, v_w_out, v_g_norm2, v_w_gu, v_w_down, v_g_final):
    me = _linear(_mesh_position())
    me_arr = jnp.reshape(me, (1,)).astype(jnp.int32)
    ada_cols = w_ada.shape[2]
    tm = min(512, x.shape[1])

    b_cols = lax.dynamic_slice_in_dim(b_ada, me * ada_cols, ada_cols, axis=1)
    cond_all, conv_w_all, mod_all, w_in_blocks = _open_step(c, conv_w[0], w_ada[0], b_cols, w_in[0].T)
    cond_all = cond_all[:, 0, :]
    conv_w_full = conv_w_all.transpose(1, 0, 2).reshape(3, CONV_WIDTH)
    mod = lax.dynamic_index_in_dim(mod_all, me, axis=1, keepdims=False).reshape(N_MOD, D_MODEL)
    mod = jnp.concatenate([mod, jnp.zeros((2, D_MODEL), F32)], axis=0)
    w_in_t = w_in_blocks.reshape(IN_PROJ_WIDTH, D_MODEL)
    gather_sems, staged, gather_token = _gather_start(
        _stage_blocks([w_out[0], w_gu[0].T, w_down[0]], w_in_t, "stage_weights"), "gather_start_weights")
    mod = _behind(mod, gather_token)

    def weights_out_gu(after):
        got = _gather_pass_on(_gather_wait(gather_sems[0:4], staged[0:2], [after], "gather_wait_out_gu"),
                              "gather_pass_on_out_gu")
        return got[0].reshape(D_MODEL, D_MODEL), got[1].reshape(2 * D_FF, D_MODEL)

    def weights_down(after):
        got = _gather_pass_on(_gather_wait(gather_sems[4:6], staged[2:3], [after], "gather_wait_down"),
                              "gather_pass_on_down")
        return got[0].reshape(D_FF, D_MODEL)

    started = {}

    def exchange(name, dw):
        st = _exchange_start(dw.reshape(N_DEV, dw.shape[0] // N_DEV, dw.shape[1]), "exchange_start_" + name)
        started[name] = st
        return st[4]

    dx, packed, d_rel = _local_step(
        x[0], loss_target[0], mod, w_in_t, weights_out_gu, weights_down, rel_bias, g_norm1, sinks[0], conv_w_full,
        g_attn_out, g_conv_out, g_norm2, g_final[None, :], exchange)

    def zone(a):
        return lax.dynamic_update_slice(jnp.zeros((N_DEV,) + a.shape, F32), a[None], (me,) + (0,) * a.ndim)

    shared = _share_start([packed, d_rel], [zone(packed), zone(d_rel)], "share_small_start")

    def finish(name, after, w, m, v, tr):
        src, land = _exchange_wait(started[name], after, "exchange_wait_" + name)
        return _adamw_parts(w, m, v, src, land, me_arr, tr, "adamw_" + name)

    g_down, d_down, nm_down, nv_down = finish("w_down", [shared[2][0]], w_down[0], m_w_down[0], v_w_down[0], 176)
    g_gu, d_gu, nm_gu, nv_gu = finish("w_gu", [nv_down], w_gu[0].T, m_w_gu[0].T, v_w_gu[0].T, 352)
    g_out, d_out, nm_out, nv_out = finish("w_out", [nv_gu], w_out[0], m_w_out[0], v_w_out[0], 128)

    packed_all, rel_all = _share_wait(shared, [nv_out], "share_small_wait")
    g_ada = _w_ada_grad(me_arr, cond_all, packed_all, ada_cols)
    d_ada, nm_ada, nv_ada = _adamw(w_ada[0], m_w_ada[0], v_w_ada[0], g_ada, 256, "adamw_w_ada")
    as_rows = {"conv_w": lambda a: a[0], "g_final": lambda a: a[None, :]}
    small_state = {
        "rel_bias": (rel_bias, m_rel_bias, v_rel_bias), "b_ada": (b_ada, m_b_ada, v_b_ada),
        "g_norm1": (g_norm1, m_g_norm1, v_g_norm1), "sinks": (sinks, m_sinks, v_sinks),
        "conv_w": (conv_w, m_conv_w, v_conv_w), "g_attn_out": (g_attn_out, m_g_attn_out, v_g_attn_out),
        "g_conv_out": (g_conv_out, m_g_conv_out, v_g_conv_out), "g_norm2": (g_norm2, m_g_norm2, v_g_norm2),
        "g_final": (g_final, m_g_final, v_g_final),
    }
    state = [tuple(as_rows.get(name, lambda a: a)(a) for a in small_state[name]) for name, _ in SMALL_PARAMS]
    loss_row, small_out = _small_update(me_arr, packed_all, rel_all, state, [])
    loss = loss_row[0, 0]
    small_res = {name: tuple(a.reshape(small_state[name][0].shape) for a in res)
                 for (name, _), res in zip(SMALL_PARAMS, small_out)}

    g_in, d_in, nm_in, nv_in = finish("w_in", [loss_row, nv_ada], w_in[0].T, m_w_in[0].T, v_w_in[0].T, 144)

    big = {
        "w_ada": (g_ada[None], d_ada[None], nm_ada[None], nv_ada[None]),
        "w_in": (g_in.T[None], d_in.T[None], nm_in.T[None], nv_in.T[None]),
        "w_out": (g_out[None], d_out[None], nm_out[None], nv_out[None]),
        "w_gu": (g_gu.T[None], d_gu.T[None], nm_gu.T[None], nv_gu.T[None]),
        "w_down": (g_down[None], d_down[None], nm_down[None], nv_down[None]),
    }
    order = ["rel_bias", "w_ada", "b_ada", "g_norm1", "w_in", "sinks", "conv_w", "g_attn_out", "g_conv_out", "w_out",
             "g_norm2", "w_gu", "w_down", "g_final"]
    results = [big[k] if k in big else small_res[k] for k in order]
    return (loss, dx[None], *[r[0] for r in results], *[r[1] for r in results], *[r[2] for r in results],
            *[r[3] for r in results])
```

```python
import functools
import math

import jax
import jax.numpy as jnp
from jax import lax
from jax.experimental import pallas as pl
from jax.experimental.pallas import tpu as pltpu

F32 = jnp.float32
BF16 = jnp.bfloat16

D_MODEL = 1024
HEAD_DIM = 64
N_Q_HEADS = 8
ATTN_WIDTH = 512
KV_WIDTH = 128
CONV_WIDTH = 512
IN_PROJ_WIDTH = 2304
D_FF = 2816
N_MOD = 6
N_BUCKETS = 32
MAX_DISTANCE = 128
BLOCK = 128
EPS = 1e-6
NEG_INF = -1e30
SCALE = HEAD_DIM ** -0.5
N_DEV = 8

ADAM_LR = 0.001
ADAM_B1 = 0.9
ADAM_B2 = 0.999
ADAM_EPS = 1e-08
ADAM_WD = 0.01
ADAM_STEP = 10

SH1, SC1, G1, SH2, SC2, G2 = range(6)

VMEM_LIMIT_LARGE = 60 * 1024 * 1024
WEIGHT_GRAD_ROWS = 2048
FFN_CHUNKS = 1
PREV_ROWS = 16
MIXER_BLOCKS = 4
MESH_ID = pl.DeviceIdType.MESH

OFF_DMOD = 0
OFF_GN1 = OFF_DMOD + N_MOD * D_MODEL
OFF_SINK = OFF_GN1 + D_MODEL
OFF_GATT = OFF_SINK + 128
OFF_GCV = OFF_GATT + ATTN_WIDTH
OFF_GN2 = OFF_GCV + CONV_WIDTH
OFF_GFIN = OFF_GN2 + D_MODEL
OFF_CONVW = OFF_GFIN + D_MODEL
OFF_LOSS = OFF_CONVW + 3 * CONV_WIDTH
PACKED = OFF_LOSS + 128


def _params(sem=None, vmem=None):
    return pltpu.CompilerParams(dimension_semantics=sem, vmem_limit_bytes=vmem)


def _coming_behind(body):
    def skipping(after_ref, *refs):
        body(*refs)

    return skipping


ANY_SPEC = pl.BlockSpec(memory_space=pl.ANY)


def _full(shape):
    nd = len(shape)
    return pl.BlockSpec(shape, lambda *_: (0,) * nd)


def _rows(tm, width):
    return pl.BlockSpec((tm, width), lambda i, *_: (i, 0))


def _sigmoid(x):
    return 1.0 / (1.0 + jnp.exp(-x))


def _rsqrt_mean_sq(x):
    return lax.rsqrt(jnp.mean(x * x, axis=-1, keepdims=True) + EPS)


def _colsum(x):
    return jnp.sum(x, axis=0, keepdims=True)


def _dot(a, b):
    return jnp.dot(a, b, preferred_element_type=F32)


def _dot_nt(a, b):
    return lax.dot_general(a, b, (((1,), (1,)), ((), ())), preferred_element_type=F32)


def _dot_tn(a, b):
    return lax.dot_general(a, b, (((0,), (0,)), ((), ())), preferred_element_type=F32)


def _mesh_position():
    return lax.axis_index("x"), lax.axis_index("y"), lax.axis_index("c")


def _linear(p):
    return 4 * p[0] + 2 * p[1] + p[2]


def _peer(k):
    x, y, c = _mesh_position()
    return (1 - x if k & 4 else x, 1 - y if k & 2 else y, 1 - c if k & 1 else c)


HBM_SPEC = pl.BlockSpec(memory_space=pltpu.HBM)
SEM_SPEC = pl.BlockSpec(memory_space=pltpu.SEMAPHORE)
DATAFLOW = pltpu.SideEffectType.DATAFLOW_SIDE_EFFECTING


def _exchange_start(src, name):
    r, c = src.shape[1:]

    def body(src_ref, land_ref, send_sems, recv_sems, src_thru, land_thru, token):
        for k in range(1, N_DEV):
            peer = _peer(k)
            pltpu.make_async_remote_copy(
                src_ref=src_ref.at[_linear(peer)], dst_ref=land_ref.at[k - 1],
                send_sem=send_sems.at[k - 1], recv_sem=recv_sems.at[k - 1],
                device_id=peer, device_id_type=MESH_ID).start()
        token[...] = jnp.zeros_like(token)

    land = lax.empty((N_DEV - 1, r, c), src.dtype)
    return pl.pallas_call(
        body, name=name,
        out_shape=(pltpu.SemaphoreType.DMA((N_DEV - 1,)), pltpu.SemaphoreType.DMA((N_DEV - 1,)),
                   pltpu.HBM(src.shape, src.dtype), pltpu.HBM(land.shape, land.dtype),
                   jax.ShapeDtypeStruct((8, 128), F32)),
        in_specs=(HBM_SPEC, HBM_SPEC),
        out_specs=(SEM_SPEC, SEM_SPEC, HBM_SPEC, HBM_SPEC, pl.BlockSpec(memory_space=pltpu.VMEM)),
        input_output_aliases={0: 2, 1: 3},
        compiler_params=pltpu.CompilerParams(has_side_effects=DATAFLOW),
    )(pltpu.with_memory_space_constraint(src, pltpu.HBM), pltpu.with_memory_space_constraint(land, pltpu.HBM))


def _exchange_wait(started, after, name):
    send_sems, recv_sems, src_thru, land_thru, _ = started

    def body(src_ref, land_ref, send_sems, recv_sems, *rest):
        for k in range(1, N_DEV):
            cp = pltpu.make_async_remote_copy(
                src_ref=src_ref.at[0], dst_ref=land_ref.at[k - 1],
                send_sem=send_sems.at[k - 1], recv_sem=recv_sems.at[k - 1],
                device_id=_peer(k), device_id_type=MESH_ID)
            cp.wait_send()
            cp.wait_recv()

    return pl.pallas_call(
        body, name=name,
        out_shape=(pltpu.HBM(src_thru.shape, src_thru.dtype), pltpu.HBM(land_thru.shape, land_thru.dtype)),
        in_specs=(HBM_SPEC, HBM_SPEC, SEM_SPEC, SEM_SPEC) + (pl.BlockSpec(memory_space=pl.ANY),) * len(after),
        out_specs=(HBM_SPEC, HBM_SPEC), input_output_aliases={0: 0, 1: 1},
        compiler_params=pltpu.CompilerParams(has_side_effects=DATAFLOW),
    )(src_thru, land_thru, send_sems, recv_sems, *after)


def _share_start(arrs, zones, name):
    n = len(arrs)

    def body(*refs):
        src_refs, zone_refs, sems = refs[:n], refs[n:2 * n], refs[2 * n:4 * n]
        me = _linear(_mesh_position())
        for a in range(n):
            for k in range(1, N_DEV):
                pltpu.make_async_remote_copy(
                    src_ref=src_refs[a], dst_ref=zone_refs[a].at[me],
                    send_sem=sems[2 * a].at[k - 1], recv_sem=sems[2 * a + 1].at[k - 1],
                    device_id=_peer(k), device_id_type=MESH_ID).start()

    outs = pl.pallas_call(
        body, name=name,
        out_shape=tuple(pltpu.SemaphoreType.DMA((N_DEV - 1,)) for _ in range(2 * n))
        + tuple(pltpu.HBM(a.shape, a.dtype) for a in arrs) + tuple(pltpu.HBM(z.shape, z.dtype) for z in zones),
        in_specs=(HBM_SPEC,) * (2 * n),
        out_specs=(SEM_SPEC,) * (2 * n) + (HBM_SPEC,) * (2 * n),
        input_output_aliases={i: 2 * n + i for i in range(2 * n)},
        compiler_params=pltpu.CompilerParams(has_side_effects=DATAFLOW),
    )(*[pltpu.with_memory_space_constraint(a, pltpu.HBM) for a in list(arrs) + list(zones)])
    return outs[:2 * n], outs[2 * n:3 * n], outs[3 * n:]


def _share_wait(started, after, name):
    sems, arrs, zones = started
    n = len(arrs)

    def body(*refs):
        src_refs, zone_refs, sem_refs = refs[:n], refs[n:2 * n], refs[2 * n:4 * n]
        for a in range(n):
            for k in range(1, N_DEV):
                cp = pltpu.make_async_remote_copy(
                    src_ref=src_refs[a], dst_ref=zone_refs[a].at[_linear(_peer(k))],
                    send_sem=sem_refs[2 * a].at[k - 1], recv_sem=sem_refs[2 * a + 1].at[k - 1],
                    device_id=_peer(k), device_id_type=MESH_ID)
                cp.wait_send()
                cp.wait_recv()

    outs = pl.pallas_call(
        body, name=name,
        out_shape=tuple(pltpu.HBM(a.shape, a.dtype) for a in arrs) + tuple(pltpu.HBM(z.shape, z.dtype) for z in zones),
        in_specs=(HBM_SPEC,) * (2 * n) + (SEM_SPEC,) * (2 * n) + (pl.BlockSpec(memory_space=pl.ANY),) * len(after),
        out_specs=(HBM_SPEC,) * (2 * n), input_output_aliases={i: i for i in range(2 * n)},
        compiler_params=pltpu.CompilerParams(has_side_effects=DATAFLOW),
    )(*arrs, *zones, *sems, *after)
    return list(outs[n:])


def _same_core_peers():
    x, y, c = _mesh_position()
    return [(x, y, 1 - c), (1 - x, y, c), (x, 1 - y, c), (1 - x, 1 - y, c)]


def _gather_start(bufs, name):
    n = len(bufs)

    def body(*refs):
        buf_refs, rest = refs[:n], refs[n:]
        sems, token = rest[:2 * n], rest[-1]
        me = _linear(_mesh_position())
        for a in range(n):
            for k, peer in enumerate(_same_core_peers()):
                pltpu.make_async_remote_copy(
                    src_ref=buf_refs[a].at[me], dst_ref=buf_refs[a].at[me],
                    send_sem=sems[2 * a].at[k], recv_sem=sems[2 * a + 1].at[k],
                    device_id=peer, device_id_type=MESH_ID).start()
        token[...] = jnp.zeros_like(token)

    outs = pl.pallas_call(
        body, name=name,
        out_shape=tuple(pltpu.SemaphoreType.DMA((4,)) for _ in range(2 * n))
        + tuple(pltpu.HBM(b.shape, b.dtype) for b in bufs) + (jax.ShapeDtypeStruct((8, 128), F32),),
        in_specs=(HBM_SPEC,) * n,
        out_specs=(SEM_SPEC,) * (2 * n) + (HBM_SPEC,) * n + (pl.BlockSpec(memory_space=pltpu.VMEM),),
        input_output_aliases={a: 2 * n + a for a in range(n)},
        compiler_params=pltpu.CompilerParams(has_side_effects=DATAFLOW),
    )(*[pltpu.with_memory_space_constraint(b, pltpu.HBM) for b in bufs])
    return outs[:2 * n], outs[2 * n:3 * n], outs[3 * n]


def _gather_wait(sems, bufs, after, name):
    n = len(bufs)

    def body(*refs):
        buf_refs, sem_refs = refs[:n], refs[n:3 * n]
        x, y, c = _mesh_position()
        me = _linear((x, y, c))
        for a in range(n):
            for k, peer in enumerate(_same_core_peers()):
                cp = pltpu.make_async_remote_copy(
                    src_ref=buf_refs[a].at[me], dst_ref=buf_refs[a].at[_linear(peer)],
                    send_sem=sem_refs[2 * a].at[k], recv_sem=sem_refs[2 * a + 1].at[k],
                    device_id=peer, device_id_type=MESH_ID)
                cp.wait_send()
                cp.wait_recv()

    return list(pl.pallas_call(
        body, name=name,
        out_shape=tuple(pltpu.HBM(b.shape, b.dtype) for b in bufs),
        in_specs=(HBM_SPEC,) * n + (SEM_SPEC,) * (2 * n) + (pl.BlockSpec(memory_space=pl.ANY),) * len(after),
        out_specs=(HBM_SPEC,) * n, input_output_aliases={a: a for a in range(n)},
        compiler_params=pltpu.CompilerParams(has_side_effects=DATAFLOW),
    )(*bufs, *sems, *after))


def _gather_pass_on(bufs, name):
    n = len(bufs)

    def body(*refs):
        out_refs = refs[n:2 * n]
        send_sems, recv_sems = refs[2 * n:]
        x, y, c = _mesh_position()
        sibling = (x, y, 1 - c)
        chips = [(1 - x, y), (x, 1 - y), (1 - x, 1 - y)]
        copies = []
        for a in range(n):
            for j, chip in enumerate(chips):
                block = out_refs[a].at[_linear((*chip, c))]
                copies.append(pltpu.make_async_remote_copy(
                    src_ref=block, dst_ref=block, send_sem=send_sems.at[3 * a + j], recv_sem=recv_sems.at[3 * a + j],
                    device_id=sibling, device_id_type=MESH_ID))
                copies[-1].start()
        for a in range(n):
            for j, chip in enumerate(chips):
                copies[3 * a + j].wait_send()
                theirs = out_refs[a].at[_linear((*chip, 1 - c))]
                pltpu.make_async_remote_copy(
                    src_ref=theirs, dst_ref=theirs, send_sem=send_sems.at[3 * a + j], recv_sem=recv_sems.at[3 * a + j],
                    device_id=sibling, device_id_type=MESH_ID).wait_recv()

    hbm = pl.BlockSpec(memory_space=pl.ANY)
    return list(pl.pallas_call(
        body, name=name,
        out_shape=[jax.ShapeDtypeStruct(b.shape, b.dtype) for b in bufs],
        in_specs=[hbm] * n, out_specs=[hbm] * n, input_output_aliases={a: a for a in range(n)},
        scratch_shapes=[pltpu.SemaphoreType.DMA((3 * n,)), pltpu.SemaphoreType.DMA((3 * n,))],
    )(*bufs))


def _open_step(c, conv_w, w_ada, b_cols, w_in_t, later, rel_bias, bucket):
    cols = w_ada.shape[1]
    n_later = len(later)

    def body(c_ref, cw_ref, wa_ref, b_ref, w_ref, *rest):
        later_refs, rb_ref, bk_ref = rest[:n_later], rest[n_later], rest[n_later + 1]
        cond_ref, conv_ref, mod_ref, win_ref = rest[n_later + 2:n_later + 6]
        staged_refs, bias_ref = rest[n_later + 6:2 * n_later + 6], rest[2 * n_later + 6]
        cond_own, mod_own, stage = rest[2 * n_later + 7:2 * n_later + 10]
        later_stage = rest[2 * n_later + 10:3 * n_later + 10]
        s_send, s_recv, w_send, w_recv, local_sems = rest[3 * n_later + 10:]
        x, y, cc = _mesh_position()
        me = _linear((x, y, cc))
        sibling = (x, y, 1 - cc)
        chips = [(1 - x, y), (x, 1 - y), (1 - x, 1 - y)]
        v = c_ref[...]
        cond_own[...] = v * _sigmoid(v)
        stage[...] = w_ref[...].astype(BF16)

        def small(rnd, a, k, src, dst, slot):
            return pltpu.make_async_remote_copy(
                src_ref=src, dst_ref=dst.at[slot], send_sem=s_send.at[rnd, a, k - 1], recv_sem=s_recv.at[rnd, a, k - 1],
                device_id=_peer(k), device_id_type=MESH_ID)

        def block(p):
            return win_ref.at[_linear(p)]

        def big(k, blk, to, src=None):
            return pltpu.make_async_remote_copy(
                src_ref=block(blk) if src is None else src, dst_ref=block(blk),
                send_sem=w_send.at[k], recv_sem=w_recv.at[k], device_id=to, device_id_type=MESH_ID)

        mine = [pltpu.make_async_copy(cond_own, cond_ref.at[me], local_sems.at[0]),
                pltpu.make_async_copy(cw_ref, conv_ref.at[me], local_sems.at[1]),
                pltpu.make_async_copy(stage, block((x, y, cc)), local_sems.at[2])]
        for cp in mine:
            cp.start()
        sends = []
        for k in range(1, N_DEV):
            sends += [small(0, 0, k, cond_own, cond_ref, me), small(0, 1, k, cw_ref, conv_ref, me)]
        for cp in sends:
            cp.start()
        first = [big(0, (x, y, cc), sibling, src=stage)]
        first += [big(1 + j, (x, y, cc), (*chip, cc), src=stage) for j, chip in enumerate(chips)]
        for cp in first:
            cp.start()
        for a in range(n_later):
            later_stage[a][...] = later_refs[a][...].astype(BF16)
            mine.append(pltpu.make_async_copy(later_stage[a], staged_refs[a].at[me], local_sems.at[4 + a]))
            mine[-1].start()
        _fill_bias_table(rb_ref, bk_ref, bias_ref)
        for k in range(1, N_DEV):
            small(0, 0, k, cond_own, cond_ref, _linear(_peer(k))).wait_recv()
            small(0, 1, k, cw_ref, conv_ref, _linear(_peer(k))).wait_recv()
        mine[0].wait()
        cond_all = jnp.concatenate([cond_ref[k] for k in range(N_DEV)], axis=0)
        mod_own[...] = _dot(cond_all, wa_ref[...]) + b_ref[...]
        mine.append(pltpu.make_async_copy(mod_own, mod_ref.at[me], local_sems.at[3]))
        mine[-1].start()
        second = [small(1, 0, k, mod_own, mod_ref, me) for k in range(1, N_DEV)]
        for cp in second:
            cp.start()
        passed = []
        for j, chip in enumerate(chips):
            big(1 + j, (*chip, cc), (x, y, cc)).wait_recv()
            fwd = big(4 + j, (*chip, cc), sibling)
            fwd.start()
            passed.append(fwd)
        big(0, sibling, (x, y, cc)).wait_recv()
        for j, chip in enumerate(chips):
            big(4 + j, (*chip, 1 - cc), (x, y, cc)).wait_recv()
        for k in range(1, N_DEV):
            small(1, 0, k, mod_own, mod_ref, _linear(_peer(k))).wait_recv()
        for cp in sends + first + second + passed:
            cp.wait_send()
        for cp in mine[1:]:
            cp.wait()

    vmem = pl.BlockSpec(memory_space=pltpu.VMEM)
    outs = pl.pallas_call(
        body, name="open_step",
        out_shape=[jax.ShapeDtypeStruct((N_DEV,) + c.shape, F32), jax.ShapeDtypeStruct((N_DEV,) + conv_w.shape, F32),
                   jax.ShapeDtypeStruct((N_DEV, N_DEV, cols), F32),
                   jax.ShapeDtypeStruct((N_DEV,) + w_in_t.shape, BF16)]
        + [jax.ShapeDtypeStruct((N_DEV,) + a.shape, BF16) for a in later]
        + [jax.ShapeDtypeStruct((N_Q_HEADS, BLOCK, 2 * BLOCK), F32)],
        in_specs=[vmem] * (5 + n_later) + [pl.BlockSpec(memory_space=pltpu.SMEM), vmem],
        out_specs=[vmem, vmem, vmem, ANY_SPEC] + [ANY_SPEC] * n_later + [vmem],
        scratch_shapes=[pltpu.VMEM(c.shape, F32), pltpu.VMEM((N_DEV, cols), F32), pltpu.VMEM(w_in_t.shape, BF16)]
        + [pltpu.VMEM(a.shape, BF16) for a in later]
        + [pltpu.SemaphoreType.DMA((2, 2, N_DEV - 1)), pltpu.SemaphoreType.DMA((2, 2, N_DEV - 1)),
           pltpu.SemaphoreType.DMA((7,)), pltpu.SemaphoreType.DMA((7,)),
           pltpu.SemaphoreType.DMA((4 + n_later,))],
        compiler_params=_params(vmem=VMEM_LIMIT_LARGE),
    )(c, conv_w, w_ada, b_cols, w_in_t, *later, rel_bias, bucket)
    return outs[0], outs[1], outs[2], outs[3], list(outs[4:4 + n_later]), outs[4 + n_later]


def _in_proj(x, mod, g_norm1, w_in, tm):
    s = x.shape[0]

    def body(x_ref, mod_ref, g_ref, w_ref, h_ref, q_ref, kv_ref, gb_ref, gc_ref, xc_ref):
        xf = x_ref[...]
        n = xf * _rsqrt_mean_sq(xf) * g_ref[...]
        h = (n * (1.0 + mod_ref[SC1:SC1 + 1, :]) + mod_ref[SH1:SH1 + 1, :]).astype(BF16)
        h_ref[...] = h
        p = _dot_nt(h, w_ref[...])
        q_ref[...] = p[:, 0:512].astype(BF16)
        kv_ref[...] = p[:, 512:768].astype(BF16)
        gb_ref[...] = p[:, 768:1280].astype(BF16)
        gc_ref[...] = p[:, 1280:1792].astype(BF16)
        xc_ref[...] = p[:, 1792:2304].astype(BF16)

    return pl.pallas_call(
        body, name="in_proj", grid=(s // tm,),
        in_specs=[_rows(tm, D_MODEL), _full((8, D_MODEL)), _full((1, D_MODEL)), _full((IN_PROJ_WIDTH, D_MODEL))],
        out_specs=[_rows(tm, D_MODEL), _rows(tm, 512), _rows(tm, 256), _rows(tm, 512), _rows(tm, 512), _rows(tm, 512)],
        out_shape=[jax.ShapeDtypeStruct((s, D_MODEL), BF16), jax.ShapeDtypeStruct((s, 512), BF16),
                   jax.ShapeDtypeStruct((s, 256), BF16), jax.ShapeDtypeStruct((s, 512), BF16),
                   jax.ShapeDtypeStruct((s, 512), BF16), jax.ShapeDtypeStruct((s, 512), BF16)],
        compiler_params=_params(("arbitrary",), VMEM_LIMIT_LARGE),
    )(x, mod, g_norm1, w_in)


def _t5_bucket(dist):
    max_exact = N_BUCKETS // 2
    is_small = dist < max_exact
    d = jnp.maximum(dist, 1).astype(F32)
    large = max_exact + (jnp.log(d / max_exact) / math.log(MAX_DISTANCE / max_exact)
                         * (N_BUCKETS - max_exact)).astype(jnp.int32)
    large = jnp.minimum(large, N_BUCKETS - 1)
    return jnp.where(is_small, dist, large)


def _bucket_table():
    qi = jnp.arange(BLOCK, dtype=jnp.int32)[:, None]
    sj = jnp.arange(2 * BLOCK, dtype=jnp.int32)[None, :]
    return _t5_bucket(jnp.maximum(qi + BLOCK - sj, 0))


def _window_mask():
    qi = lax.broadcasted_iota(jnp.int32, (BLOCK, 2 * BLOCK), 0)
    sj = lax.broadcasted_iota(jnp.int32, (BLOCK, 2 * BLOCK), 1)
    dist = qi + BLOCK - sj
    return (dist >= 0) & (dist < BLOCK)


def _fill_bias_table(rb_ref, bk_ref, o_ref):
    bk = bk_ref[...]
    inside = _window_mask()
    for h in range(N_Q_HEADS):
        acc = jnp.zeros((BLOCK, 2 * BLOCK), F32)
        for b in range(N_BUCKETS):
            acc = jnp.where(bk == b, rb_ref[b, h], acc)
        o_ref[h] = jnp.where(inside, acc, NEG_INF)


def _load_kv_window(kv_ref, n):
    prev = jnp.maximum(n - 1, 0)
    kvw = jnp.concatenate([kv_ref[pl.ds(pl.multiple_of(prev * BLOCK, BLOCK), BLOCK), :],
                           kv_ref[pl.ds(pl.multiple_of(n * BLOCK, BLOCK), BLOCK), :]], axis=0)
    k, v = kvw[:, 0:128], kvw[:, 128:256]
    k_sw = pltpu.roll(k.astype(F32), 64, 1).astype(BF16)
    v_sw = pltpu.roll(v.astype(F32), 64, 1).astype(BF16)
    return (k, k_sw), (v, v_sw)


def _conv_taps(gc, xc, gc_prev, xc_prev, n):
    u = gc * xc
    before = jnp.where(n > 0, gc_prev.astype(F32) * xc_prev.astype(F32), 0.0)
    last = before.shape[0] - 1
    row = lax.broadcasted_iota(jnp.int32, u.shape, 0)
    u1 = jnp.where(row == 0, before[last:last + 1, :], pltpu.roll(u, 1, 0))
    u2 = jnp.where(row == 0, before[last - 1:last, :],
                   jnp.where(row == 1, before[last:last + 1, :], pltpu.roll(u, 2, 0)))
    return u, u1, u2


def _mixer_fwd(q, kv, gb, gc, xc, bias, sinks, conv_w, g_attn, g_conv):
    s = q.shape[0]
    nb = s // BLOCK

    per_step = min(MIXER_BLOCKS, nb)
    tile = per_step * BLOCK

    def one_block(n, rows, before, sink_ref, q_ref, kv_ref, gb_ref, gc_ref, xc_ref, bias_ref, cw_ref, ga_ref,
                  gcv_ref, attn_ref, merged_ref, lse_ref):
        ks, vs = _load_kv_window(kv_ref, n)
        lane = lax.broadcasted_iota(jnp.int32, (BLOCK, BLOCK), 1)
        low = lane < HEAD_DIM
        col = lax.broadcasted_iota(jnp.int32, (BLOCK, 2 * BLOCK), 1)
        no_prev = (col < BLOCK) & (n == 0)
        lse_all = jnp.zeros((BLOCK, BLOCK), F32)
        pairs = []
        for p in range(4):
            qp = q_ref[rows, 128 * p:128 * (p + 1)].astype(F32)
            kvh = p // 2
            res = []
            for e in range(2):
                h = 2 * p + e
                qm = jnp.where(low if e == 0 else ~low, qp, 0.0).astype(BF16)
                sw = 0 if kvh == e else 1
                sc = _dot_nt(qm, ks[sw]) * SCALE + bias_ref[h]
                sc = jnp.where(no_prev, NEG_INF, sc)
                sink = sink_ref[h]
                m = jnp.maximum(jnp.max(sc, axis=-1, keepdims=True), sink)
                pe = jnp.exp(sc - m)
                den = jnp.sum(pe, axis=-1, keepdims=True) + jnp.exp(sink - m)
                res.append(_dot(pe.astype(BF16), vs[sw]) / den)
                lse_all = lse_all + jnp.where(lane == h, m + jnp.log(den), 0.0)
            pairs.append(jnp.where(low, res[0], res[1]))
        attn = jnp.concatenate(pairs, axis=1)
        attn_ref[rows, :] = attn
        lse_ref[rows, :] = lse_all
        u, u1, u2 = _conv_taps(gc_ref[rows, :].astype(F32), xc_ref[rows, :].astype(F32), before[0], before[1], n)
        cw = cw_ref[...]
        cv = gb_ref[rows, :].astype(F32) * (cw[0:1, :] * u2 + cw[1:2, :] * u1 + cw[2:3, :] * u)
        an = attn * _rsqrt_mean_sq(attn) * ga_ref[...]
        cn = cv * _rsqrt_mean_sq(cv) * gcv_ref[...]
        merged_ref[rows, :] = jnp.concatenate([an, cn], axis=1).astype(BF16)

    def body(sink_ref, q_ref, kv_ref, gb_ref, gc_ref, xc_ref, gcp_ref, xcp_ref, *rest):
        step = pl.program_id(0)
        for sub in range(per_step):
            rows = slice(sub * BLOCK, (sub + 1) * BLOCK)
            ahead = slice(sub * BLOCK - PREV_ROWS, sub * BLOCK)
            before = (gcp_ref[...], xcp_ref[...]) if sub == 0 else (gc_ref[ahead, :], xc_ref[ahead, :])
            one_block(step * per_step + sub, rows, before, sink_ref, q_ref, kv_ref, gb_ref, gc_ref, xc_ref, *rest)

    blk = lambda w: pl.BlockSpec((tile, w), lambda n: (n, 0))
    prev8 = pl.BlockSpec((PREV_ROWS, 512), lambda n: (jnp.maximum(n * (tile // PREV_ROWS) - 1, 0), 0))
    return pl.pallas_call(
        body, name="mixer_fwd", grid=(nb // per_step,),
        in_specs=[pl.BlockSpec(memory_space=pltpu.SMEM), blk(512), _full((s, 256)), blk(512), blk(512), blk(512),
                  prev8, prev8, _full((N_Q_HEADS, BLOCK, 2 * BLOCK)), _full((3, 512)), _full((1, 512)),
                  _full((1, 512))],
        out_specs=[blk(512), blk(1024), blk(128)],
        out_shape=[jax.ShapeDtypeStruct((s, 512), F32), jax.ShapeDtypeStruct((s, 1024), BF16),
                   jax.ShapeDtypeStruct((s, 128), F32)],
        compiler_params=_params(("arbitrary",)),
    )(sinks, q, kv, gb, gc, xc, gc, xc, bias, conv_w, g_attn, g_conv)


def _out_proj(merged, x, mod, w_out, tm):
    s = x.shape[0]

    def body(m_ref, x_ref, mod_ref, w_ref, o_ref, x1_ref):
        o = _dot(m_ref[...], w_ref[...])
        o_ref[...] = o.astype(BF16)
        x1_ref[...] = x_ref[...] + mod_ref[G1:G1 + 1, :] * o

    return pl.pallas_call(
        body, name="out_proj", grid=(s // tm,),
        in_specs=[_rows(tm, D_MODEL), _rows(tm, D_MODEL), _full((8, D_MODEL)), _full((D_MODEL, D_MODEL))],
        out_specs=[_rows(tm, D_MODEL), _rows(tm, D_MODEL)],
        out_shape=[jax.ShapeDtypeStruct((s, D_MODEL), BF16), jax.ShapeDtypeStruct((s, D_MODEL), F32)],
        compiler_params=_params(("arbitrary",)),
    )(merged, x, mod, w_out)


def _resident(shape):
    nd = len(shape)
    return pl.BlockSpec(shape, lambda *_: (0,) * nd, pipeline_mode=pl.Buffered(1))


def _ffn(x1, o1, mod, g_norm2, w_gu, w_down, w_out, g_final, target, tm):
    s = x1.shape[0]
    chunk = D_FF // FFN_CHUNKS

    def body(x_ref, o1_ref, mod_ref, g_ref, wgu_ref, wd_ref, wo_ref, gf_ref, t_ref,
             h_ref, act_ref, do_ref, dgu_ref, dx1_ref, do1_ref, dm_ref, small_ref):
        @pl.when(pl.program_id(0) == 0)
        def _():
            small_ref[...] = jnp.zeros_like(small_ref)

        xf = x_ref[...]
        n = xf * _rsqrt_mean_sq(xf) * g_ref[...]
        h = (n * (1.0 + mod_ref[SC2:SC2 + 1, :]) + mod_ref[SH2:SH2 + 1, :]).astype(BF16)
        h_ref[...] = h
        gates, ups, o = [], [], None
        for j in range(FFN_CHUNKS):
            lo = j * chunk
            gate = _dot_nt(h, wgu_ref[lo:lo + chunk, :])
            up = _dot_nt(h, wgu_ref[D_FF + lo:D_FF + lo + chunk, :])
            sg = _sigmoid(gate)
            act = (gate * sg * up).astype(BF16)
            act_ref[:, lo:lo + chunk] = act
            gates.append((up * (sg * (1.0 + gate * (1.0 - sg)))).astype(BF16))
            ups.append((gate * sg).astype(BF16))
            part = _dot(act, wd_ref[lo:lo + chunk, :])
            o = part if o is None else o + part
        g2 = mod_ref[G2:G2 + 1, :]
        x2 = xf + g2 * o
        r = _rsqrt_mean_sq(x2)
        xn = x2 * r
        gf = gf_ref[...]
        err = xn * gf - t_ref[...]
        dy = err * (1.0 / D_MODEL)
        dxn = dy * gf
        dx2 = r * (dxn - xn * jnp.mean(dxn * xn, axis=-1, keepdims=True))
        small_ref[4:5, :] += _colsum(dy * xn)
        small_ref[5:6, :] += _colsum(err * err)
        small_ref[3:4, :] += _colsum(dx2 * o)
        do = (dx2 * g2).astype(BF16)
        do_ref[...] = do
        dh = None
        for j in range(FFN_CHUNKS):
            lo = j * chunk
            dact = _dot_nt(do, wd_ref[lo:lo + chunk, :])
            dgate = (dact * gates[j].astype(F32)).astype(BF16)
            dup = (dact * ups[j].astype(F32)).astype(BF16)
            dgu_ref[:, lo:lo + chunk] = dgate
            dgu_ref[:, D_FF + lo:D_FF + lo + chunk] = dup
            part = _dot(dgate, wgu_ref[lo:lo + chunk, :]) + _dot(dup, wgu_ref[D_FF + lo:D_FF + lo + chunk, :])
            dh = part if dh is None else dh + part
        dx1 = dx2 + _norm_mod_bwd(dh, xf, g_ref[...], mod_ref[SC2:SC2 + 1, :], small_ref)
        dx1_ref[...] = dx1.astype(BF16)
        small_ref[7:8, :] += _colsum(dx1 * o1_ref[...].astype(F32))
        do1 = (dx1 * mod_ref[G1:G1 + 1, :]).astype(BF16)
        do1_ref[...] = do1
        dm_ref[...] = _dot_nt(do1, wo_ref[...]).astype(BF16)

        @pl.when(pl.program_id(0) == pl.num_programs(0) - 1)
        def _():
            total = jnp.sum(small_ref[5:6, :], axis=-1, keepdims=True) * (0.5 / D_MODEL)
            small_ref[6:7, :] = jnp.broadcast_to(total, (1, D_MODEL))

    narrow = jax.ShapeDtypeStruct((s, D_MODEL), BF16)
    return pl.pallas_call(
        body, name="ffn", grid=(s // tm,),
        in_specs=[_rows(tm, D_MODEL), _rows(tm, D_MODEL), _full((8, D_MODEL)), _full((1, D_MODEL)),
                  _resident((2 * D_FF, D_MODEL)), _resident((D_FF, D_MODEL)), _resident((D_MODEL, D_MODEL)),
                  _full((1, D_MODEL)), _rows(tm, D_MODEL)],
        out_specs=[_rows(tm, D_MODEL), _rows(tm, D_FF), _rows(tm, D_MODEL), _rows(tm, 2 * D_FF), _rows(tm, D_MODEL),
                   _rows(tm, D_MODEL), _rows(tm, D_MODEL), _full((8, D_MODEL))],
        out_shape=[narrow, jax.ShapeDtypeStruct((s, D_FF), BF16), narrow, jax.ShapeDtypeStruct((s, 2 * D_FF), BF16),
                   narrow, narrow, narrow, jax.ShapeDtypeStruct((8, D_MODEL), F32)],
        compiler_params=_params(("arbitrary",), VMEM_LIMIT_LARGE),
    )(x1, o1, mod, g_norm2, w_gu, w_down, w_out, g_final, target)


def _norm_mod_bwd(dh, xf, g, scale_row, small_ref):
    r = _rsqrt_mean_sq(xf)
    xn = xf * r
    small_ref[0:1, :] += _colsum(dh)
    small_ref[1:2, :] += _colsum(dh * (xn * g))
    dn = dh * (1.0 + scale_row)
    small_ref[2:3, :] += _colsum(dn * xn)
    dxn = dn * g
    return r * (dxn - xn * jnp.mean(dxn * xn, axis=-1, keepdims=True))


def _group_norm_bwd(dm, a, g):
    r = _rsqrt_mean_sq(a)
    an = a * r
    dan = dm * g
    return r * (dan - an * jnp.mean(dan * an, axis=-1, keepdims=True)), _colsum(dm * an)


def _mixer_bwd(after, q, kv, gb, gc, xc, bias, sinks, conv_w, g_attn, g_conv, attn, lse, dmerged):
    s = q.shape[0]
    nb = s // BLOCK

    per_step = min(MIXER_BLOCKS, nb)
    tile = per_step * BLOCK
    steps = nb // per_step

    def one_block(n, rows, before, nxt, sink_ref, q_ref, kv_ref, gb_ref, gc_ref, xc_ref, bias_ref, cw_ref, ga_ref,
                  gcv_ref, attn_ref, lse_ref, dm_ref, dproj_ref, dbias_ref, dsink_ref, small_ref):
        next_dy, next_dkv = nxt
        dm = dm_ref[rows, :].astype(F32)
        gbv, gcv_, xcv = gb_ref[rows, :].astype(F32), gc_ref[rows, :].astype(F32), xc_ref[rows, :].astype(F32)
        u, u1, u2 = _conv_taps(gcv_, xcv, before[0], before[1], n)
        cw = cw_ref[...]
        yv = cw[0:1, :] * u2 + cw[1:2, :] * u1 + cw[2:3, :] * u
        dcv, dg_conv = _group_norm_bwd(dm[:, 512:1024], gbv * yv, gcv_ref[...])
        small_ref[1:2, :] += dg_conv
        dproj_ref[rows, 768:1280] = (dcv * yv).astype(BF16)
        dy = dcv * gbv
        row = lax.broadcasted_iota(jnp.int32, dy.shape, 0)
        d1 = jnp.where(row == BLOCK - 1, next_dy[0:1, :], pltpu.roll(dy, BLOCK - 1, 0))
        d2 = jnp.where(row == BLOCK - 2, next_dy[0:1, :],
                       jnp.where(row == BLOCK - 1, next_dy[1:2, :], pltpu.roll(dy, BLOCK - 2, 0)))
        du = cw[2:3, :] * dy + cw[1:2, :] * d1 + cw[0:1, :] * d2
        dproj_ref[rows, 1280:1792] = (du * xcv).astype(BF16)
        dproj_ref[rows, 1792:2304] = (du * gcv_).astype(BF16)
        small_ref[2:3, :] += _colsum(dy * u2)
        small_ref[3:4, :] += _colsum(dy * u1)
        small_ref[4:5, :] += _colsum(dy * u)

        attn_v = attn_ref[rows, :]
        dout, dg_attn = _group_norm_bwd(dm[:, 0:512], attn_v, ga_ref[...])
        small_ref[0:1, :] += dg_attn
        ks, vs = _load_kv_window(kv_ref, n)
        lane = lax.broadcasted_iota(jnp.int32, (BLOCK, BLOCK), 1)
        low = lane < HEAD_DIM
        col = lax.broadcasted_iota(jnp.int32, (BLOCK, 2 * BLOCK), 1)
        no_prev = (col < BLOCK) & (n == 0)
        lse_all = lse_ref[rows, :]
        dsink = jnp.zeros((BLOCK, BLOCK), F32)
        dq_pairs = []
        dk_groups, dv_groups = [], []
        for kvh in range(2):
            ds_rows, pr_rows, q_rows, do_rows = [], [], [], []
            for p in (2 * kvh, 2 * kvh + 1):
                qp = q_ref[rows, 128 * p:128 * (p + 1)].astype(F32)
                do_p = dout[:, 128 * p:128 * (p + 1)]
                prod = do_p * attn_v[:, 128 * p:128 * (p + 1)]
                res = []
                for e in range(2):
                    h = 2 * p + e
                    half = low if e == 0 else ~low
                    qm = jnp.where(half, qp, 0.0).astype(BF16)
                    dom = jnp.where(half, do_p, 0.0).astype(BF16)
                    delta = jnp.sum(jnp.where(half, prod, 0.0), axis=-1, keepdims=True)
                    lse_h = jnp.sum(jnp.where(lane == h, lse_all, 0.0), axis=-1, keepdims=True)
                    sw = 0 if kvh == e else 1
                    sc = _dot_nt(qm, ks[sw]) * SCALE + bias_ref[h]
                    sc = jnp.where(no_prev, NEG_INF, sc)
                    pr = jnp.exp(sc - lse_h)
                    dp = _dot_nt(dom, vs[sw])
                    ds = pr * (dp - delta)
                    dbias_ref[h] += ds
                    dsink = dsink + jnp.where(lane == h, -jnp.exp(sink_ref[h] - lse_h) * delta, 0.0)
                    dsb = ds.astype(BF16)
                    res.append(_dot(dsb, ks[sw]) * SCALE)
                    ds_rows.append(dsb)
                    pr_rows.append(pr.astype(BF16))
                    q_rows.append(qm)
                    do_rows.append(dom)
                dq_pairs.append(jnp.where(low, res[0], res[1]))
            dk_g = _dot_tn(jnp.concatenate(ds_rows, axis=0), jnp.concatenate(q_rows, axis=0)) * SCALE
            dv_g = _dot_tn(jnp.concatenate(pr_rows, axis=0), jnp.concatenate(do_rows, axis=0))
            dk_groups.append(dk_g + pltpu.roll(dk_g, 64, 1))
            dv_groups.append(dv_g + pltpu.roll(dv_g, 64, 1))
        dproj_ref[rows, 0:512] = jnp.concatenate(dq_pairs, axis=1).astype(BF16)
        dsink_ref[...] += dsink
        low_kv = lax.broadcasted_iota(jnp.int32, (2 * BLOCK, BLOCK), 1) < HEAD_DIM
        dkv_win = jnp.concatenate([jnp.where(low_kv, dk_groups[0], dk_groups[1]),
                                   jnp.where(low_kv, dv_groups[0], dv_groups[1])], axis=1)
        dproj_ref[rows, 512:768] = (dkv_win[BLOCK:2 * BLOCK, :] + next_dkv).astype(BF16)
        return dy[0:8, :], dkv_win[0:BLOCK, :]

    def body(sink_ref, q_ref, kv_ref, gb_ref, gc_ref, xc_ref, gcp_ref, xcp_ref, *rest):
        refs, dy_ref, dkv_ref = rest[:-2], rest[-2], rest[-1]
        dbias_ref, dsink_ref, small_ref = refs[8], refs[9], refs[10]
        step = pl.program_id(0)

        @pl.when(step == 0)
        def _():
            dbias_ref[...] = jnp.zeros_like(dbias_ref)
            dsink_ref[...] = jnp.zeros_like(dsink_ref)
            small_ref[...] = jnp.zeros_like(small_ref)
            dy_ref[...] = jnp.zeros_like(dy_ref)
            dkv_ref[...] = jnp.zeros_like(dkv_ref)

        nxt = (dy_ref[...], dkv_ref[...])
        for sub in reversed(range(per_step)):
            rows = slice(sub * BLOCK, (sub + 1) * BLOCK)
            ahead = slice(sub * BLOCK - PREV_ROWS, sub * BLOCK)
            before = (gcp_ref[...], xcp_ref[...]) if sub == 0 else (gc_ref[ahead, :], xc_ref[ahead, :])
            nxt = one_block((steps - 1 - step) * per_step + sub, rows, before, nxt,
                            sink_ref, q_ref, kv_ref, gb_ref, gc_ref, xc_ref, *refs)
        dy_ref[...], dkv_ref[...] = nxt

        @pl.when(step == steps - 1)
        def _():
            small_ref[5:6, :] = jnp.concatenate([_colsum(dsink_ref[...]), jnp.zeros((1, 512 - BLOCK), F32)], axis=1)

    blk = lambda w: pl.BlockSpec((tile, w), lambda t: (steps - 1 - t, 0))
    prev8 = pl.BlockSpec((PREV_ROWS, 512),
                         lambda t: (jnp.maximum((steps - 1 - t) * (tile // PREV_ROWS) - 1, 0), 0))
    bf = lambda w: jax.ShapeDtypeStruct((s, w), BF16)
    return pl.pallas_call(
        _coming_behind(body), name="mixer_bwd", grid=(steps,),
        in_specs=[ANY_SPEC, pl.BlockSpec(memory_space=pltpu.SMEM), blk(512), _full((s, 256)), blk(512), blk(512), blk(512),
                  prev8, prev8, _full((N_Q_HEADS, BLOCK, 2 * BLOCK)), _full((3, 512)), _full((1, 512)),
                  _full((1, 512)), blk(512), blk(128), blk(1024)],
        out_specs=[blk(IN_PROJ_WIDTH), _full((N_Q_HEADS, BLOCK, 2 * BLOCK)), _full((BLOCK, BLOCK)), _full((8, 512))],
        out_shape=[bf(IN_PROJ_WIDTH), jax.ShapeDtypeStruct((N_Q_HEADS, BLOCK, 2 * BLOCK), F32),
                   jax.ShapeDtypeStruct((BLOCK, BLOCK), F32), jax.ShapeDtypeStruct((8, 512), F32)],
        scratch_shapes=[pltpu.VMEM((8, 512), F32), pltpu.VMEM((BLOCK, 2 * KV_WIDTH), F32)],
        compiler_params=_params(("arbitrary",), VMEM_LIMIT_LARGE),
    )(after, sinks, q, kv, gb, gc, xc, gc, xc, bias, conv_w, g_attn, g_conv, attn, lse, dmerged)


def _in_proj_bwd(after, dproj, x, dx1, mod, g_norm1, w_in, tm):
    s = x.shape[0]

    def body(dproj_ref, x_ref, dx1_ref, mod_ref, g_ref, w_ref, dx_ref, small_ref):
        @pl.when(pl.program_id(0) == 0)
        def _():
            small_ref[...] = jnp.zeros_like(small_ref)

        dh = _dot(dproj_ref[...], w_ref[...])
        dx_ref[...] = dx1_ref[...].astype(F32) + _norm_mod_bwd(dh, x_ref[...], g_ref[...], mod_ref[SC1:SC1 + 1, :],
                                                               small_ref)

    return pl.pallas_call(
        _coming_behind(body), name="in_proj_bwd", grid=(s // tm,),
        in_specs=[ANY_SPEC, _rows(tm, IN_PROJ_WIDTH), _rows(tm, D_MODEL), _rows(tm, D_MODEL), _full((8, D_MODEL)),
                  _full((1, D_MODEL)), _full((IN_PROJ_WIDTH, D_MODEL))],
        out_specs=[_rows(tm, D_MODEL), _full((8, D_MODEL))],
        out_shape=[jax.ShapeDtypeStruct((s, D_MODEL), F32), jax.ShapeDtypeStruct((8, D_MODEL), F32)],
        compiler_params=_params(("arbitrary",), VMEM_LIMIT_LARGE),
    )(after, dproj, x, dx1, mod, g_norm1, w_in)


def _weight_grad(a, b, tk, ts, name, after=None):
    s, k = a.shape
    n = b.shape[1]
    nt = s // ts
    extra = [] if after is None else [after]

    def body(a_ref, b_ref, *rest):
        o_ref, acc_ref = rest[-2:]
        t = pl.program_id(1)
        @pl.when(t == 0)
        def _():
            acc_ref[...] = jnp.zeros_like(acc_ref)

        acc = acc_ref[...] + _dot_tn(a_ref[...], b_ref[...])
        acc_ref[...] = acc
        o_ref[...] = acc.astype(BF16)

    return pl.pallas_call(
        body, name=name, grid=(k // tk, nt),
        in_specs=[pl.BlockSpec((ts, tk), lambda i, t: (t, i)), pl.BlockSpec((ts, n), lambda i, t: (t, 0))]
        + [ANY_SPEC] * len(extra),
        out_specs=pl.BlockSpec((tk, n), lambda i, t: (i, 0)),
        out_shape=jax.ShapeDtypeStruct((k, n), BF16),
        scratch_shapes=[pltpu.VMEM((tk, n), F32)],
        compiler_params=_params(("arbitrary", "arbitrary"), VMEM_LIMIT_LARGE),
    )(a, b, *extra)


def _rel_bias_grad(dbias, bucket):
    def body(db_ref, bk_ref, o_ref, rows_ref):
        bk = bk_ref[...]
        for b in range(N_BUCKETS):
            sel = (bk == b).astype(F32)
            for h in range(N_Q_HEADS):
                rows_ref[N_BUCKETS * h + b:N_BUCKETS * h + b + 1, :] = _colsum(db_ref[h] * sel)
        head = lax.broadcasted_iota(jnp.int32, (N_BUCKETS, N_Q_HEADS), 1)
        out = jnp.zeros((N_BUCKETS, N_Q_HEADS), F32)
        for h in range(N_Q_HEADS):
            per_bucket = jnp.sum(rows_ref[N_BUCKETS * h:N_BUCKETS * (h + 1), :], axis=-1, keepdims=True)
            out = out + jnp.where(head == h, per_bucket, 0.0)
        o_ref[...] = out

    return pl.pallas_call(
        body, name="rel_bias_grad",
        out_shape=jax.ShapeDtypeStruct((N_BUCKETS, N_Q_HEADS), F32),
        scratch_shapes=[pltpu.VMEM((N_BUCKETS * N_Q_HEADS, 2 * BLOCK), F32)],
    )(dbias, bucket)


def _lanes_from(x, start, width):
    n = x.shape[1]
    return pltpu.roll(x, (n - start) % n, 1)[:, 0:width]


def _w_ada_grad(me, cond_all, packed_all, cols):
    def body(me_ref, c_ref, p_ref, o_ref):
        dmod = jnp.concatenate([p_ref[k][:, OFF_DMOD:OFF_DMOD + N_MOD * D_MODEL] for k in range(N_DEV)], axis=0)
        mine = _lanes_from(dmod, me_ref[0] * cols, cols)
        pad = lambda a: jnp.concatenate([a, jnp.zeros((128 - N_DEV, a.shape[1]), F32)], axis=0)
        o_ref[...] = _dot_tn(pad(c_ref[...]), pad(mine))

    vmem = pl.BlockSpec(memory_space=pltpu.VMEM)
    return pl.pallas_call(body, name="w_ada_grad",
                          in_specs=[pl.BlockSpec(memory_space=pltpu.SMEM), vmem, vmem],
                          out_shape=jax.ShapeDtypeStruct((cond_all.shape[1], cols), F32))(me, cond_all, packed_all)


SMALL_PARAMS = (("rel_bias", None), ("b_ada", (OFF_DMOD, N_MOD * D_MODEL)), ("g_norm1", (OFF_GN1, D_MODEL)),
                ("sinks", (OFF_SINK, N_Q_HEADS)), ("conv_w", None), ("g_attn_out", (OFF_GATT, ATTN_WIDTH)),
                ("g_conv_out", (OFF_GCV, CONV_WIDTH)), ("g_norm2", (OFF_GN2, D_MODEL)),
                ("g_final", (OFF_GFIN, D_MODEL)))


def _small_update(me, packed_all, rel_all, state, after):
    n_p = len(SMALL_PARAMS)
    flat = [a for triple in state for a in triple]
    conv_cols = state[4][0].shape[1]

    def body(me_ref, p_ref, r_ref, *refs):
        ins = refs[:3 * n_p]
        loss_ref, outs = refs[3 * n_p + len(after)], refs[3 * n_p + len(after) + 1:]
        small, rel = p_ref[0], r_ref[0]
        for k in range(1, N_DEV):
            small = small + p_ref[k]
            rel = rel + r_ref[k]
        loss_ref[...] = small[:, OFF_LOSS:OFF_LOSS + 128]
        taps = jnp.concatenate([small[:, OFF_CONVW + CONV_WIDTH * j:OFF_CONVW + CONV_WIDTH * (j + 1)]
                                for j in range(3)] + [jnp.zeros((5, CONV_WIDTH), F32)], axis=0)
        conv_g = _lanes_from(taps, me_ref[0] * conv_cols, conv_cols)[0:3, :]
        for i, (name, lanes) in enumerate(SMALL_PARAMS):
            g = rel if name == "rel_bias" else conv_g if name == "conv_w" else small[:, lanes[0]:lanes[0] + lanes[1]]
            w_ref, m_ref, v_ref = ins[3 * i:3 * i + 3]
            outs[4 * i][...] = g
            outs[4 * i + 1][...], outs[4 * i + 2][...], outs[4 * i + 3][...] = _adam_math(
                w_ref[...], g, m_ref[...], v_ref[...])

    vmem = pl.BlockSpec(memory_space=pltpu.VMEM)
    out_shape = [jax.ShapeDtypeStruct((1, 128), F32)]
    for w, _, _ in state:
        out_shape += [jax.ShapeDtypeStruct(w.shape, F32)] * 4
    outs = pl.pallas_call(
        body, name="small_update",
        in_specs=[pl.BlockSpec(memory_space=pltpu.SMEM), vmem, vmem] + [vmem] * len(flat)
        + [pl.BlockSpec(memory_space=pl.ANY)] * len(after),
        out_shape=out_shape,
    )(me, packed_all, rel_all, *flat, *after)
    return outs[0], [tuple(outs[1 + 4 * i:5 + 4 * i]) for i in range(n_p)]


def _adam_math(w, g, m, v):
    m = ADAM_B1 * m + (1.0 - ADAM_B1) * g
    v = ADAM_B2 * v + (1.0 - ADAM_B2) * (g * g)
    m_hat = m / (1.0 - ADAM_B1 ** ADAM_STEP)
    v_hat = v / (1.0 - ADAM_B2 ** ADAM_STEP)
    delta = -ADAM_LR * (m_hat / (jnp.sqrt(v_hat) + ADAM_EPS) + ADAM_WD * w)
    return delta, m, v


def _adamw_parts(w, m, v, local, land, me, tr, name):
    r, c = w.shape

    def body(me_ref, w_ref, m_ref, v_ref, own_ref, land_ref, g_ref, d_ref, mo_ref, vo_ref):
        g = own_ref[0].astype(F32)
        for k in range(N_DEV - 1):
            g = g + land_ref[k].astype(F32)
        g_ref[...] = g
        d_ref[...], mo_ref[...], vo_ref[...] = _adam_math(w_ref[...], g, m_ref[...], v_ref[...])

    tile = pl.BlockSpec((tr, c), lambda i, me_ref: (i, 0))
    return pl.pallas_call(
        body, name=name,
        grid_spec=pltpu.PrefetchScalarGridSpec(
            num_scalar_prefetch=1, grid=(r // tr,),
            in_specs=[tile, tile, tile, pl.BlockSpec((1, tr, c), lambda i, me_ref: (me_ref[0], i, 0)),
                      pl.BlockSpec((N_DEV - 1, tr, c), lambda i, me_ref: (0, i, 0))],
            out_specs=[tile] * 4),
        out_shape=[jax.ShapeDtypeStruct((r, c), F32)] * 4,
        compiler_params=_params(("arbitrary",)),
    )(me, w, m, v, local, land)


def _adamw(w, m, v, g, tr, name):
    r, c = w.shape

    def body(w_ref, m_ref, v_ref, g_ref, d_ref, mo_ref, vo_ref):
        d_ref[...], mo_ref[...], vo_ref[...] = _adam_math(w_ref[...], g_ref[...], m_ref[...], v_ref[...])

    tile = pl.BlockSpec((tr, c), lambda i: (i, 0))
    return pl.pallas_call(
        body, name=name, grid=(r // tr,),
        in_specs=[tile] * 4, out_specs=[tile] * 3,
        out_shape=[jax.ShapeDtypeStruct((r, c), F32)] * 3,
        compiler_params=_params(("arbitrary",)),
    )(w, m, v, g)


def _behind(a, token):
    return a + token[0:a.shape[0], 0:1]


def _local_step(x, target, mod, w_in_t, bias, weights_out_gu, weights_down, g_norm1, sinks, conv_w, g_attn,
                g_conv, g_norm2, g_final, exchange):
    s = x.shape[0]
    tm = min(512, s)
    tm_small = min(256, s)
    bucket = _bucket_table()

    h, q, kv, gb, gc, xc = _in_proj(x, mod, g_norm1, w_in_t, tm)
    attn, merged, lse = _mixer_fwd(q, kv, gb, gc, xc, bias, sinks, conv_w, g_attn, g_conv)
    w_out, w_gu_t = weights_out_gu(merged)
    o1, x1 = _out_proj(merged, x, mod, w_out, tm)
    w_down = weights_down(x1)
    h2, act, do2, dgu, dx1, do1, dmerged, sm_2 = _ffn(x1, o1, mod, g_norm2, w_gu_t, w_down, w_out, g_final, target,
                                                      tm_small)
    ts = min(WEIGHT_GRAD_ROWS, s)
    tok_down = exchange("w_down", _weight_grad(act, do2, D_FF // 2, ts, "w_down_grad"))
    tok_gu = exchange("w_gu", _weight_grad(dgu, h2, D_FF // 2, ts, "w_gu_grad", after=tok_down))
    tok_out = exchange("w_out", _weight_grad(merged, do1, D_MODEL, ts, "w_out_grad", after=tok_gu))
    dproj, dbias, dsink, sm_mix = _mixer_bwd(
        tok_out, q, kv, gb, gc, xc, bias, sinks, conv_w, g_attn, g_conv, attn, lse, dmerged)
    tok_in = exchange("w_in", _weight_grad(dproj, h, IN_PROJ_WIDTH // 2, ts, "w_in_grad"))
    dx, sm_1 = _in_proj_bwd(tok_in, dproj, x, dx1, mod, g_norm1, w_in_t, tm)
    d_rel = _rel_bias_grad(dbias, bucket)

    packed = jnp.concatenate([
        sm_1[0], sm_1[1], sm_2[7], sm_2[0], sm_2[1], sm_2[3],
        sm_1[2],
        sm_mix[5, 0:128],
        sm_mix[0], sm_mix[1],
        sm_2[2],
        sm_2[4],
        sm_mix[2], sm_mix[3], sm_mix[4],
        sm_2[6, 0:128],
    ])[None, :]
    return dx, packed, d_rel


def kernel(x, c, rel_bias, w_ada, b_ada, g_norm1, w_in, sinks, conv_w, g_attn_out, g_conv_out, w_out, g_norm2, w_gu, w_down, g_final, loss_target, m_rel_bias, m_w_ada, m_b_ada, m_g_norm1, m_w_in, m_sinks, m_conv_w, m_g_attn_out, m_g_conv_out, m_w_out, m_g_norm2, m_w_gu, m_w_down, m_g_final, v_rel_bias, v_w_ada, v_b_ada, v_g_norm1, v_w_in, v_sinks, v_conv_w, v_g_attn_out, v_g_conv_out, v_w_out, v_g_norm2, v_w_gu, v_w_down, v_g_final):
    me = _linear(_mesh_position())
    me_arr = jnp.reshape(me, (1,)).astype(jnp.int32)
    ada_cols = w_ada.shape[2]
    tm = min(512, x.shape[1])

    b_cols = lax.dynamic_slice_in_dim(b_ada, me * ada_cols, ada_cols, axis=1)
    cond_all, conv_w_all, mod_all, w_in_blocks, staged, bias = _open_step(
        c, conv_w[0], w_ada[0], b_cols, w_in[0].T, [w_out[0], w_gu[0].T, w_down[0]], rel_bias, _bucket_table())
    cond_all = cond_all[:, 0, :]
    conv_w_full = conv_w_all.transpose(1, 0, 2).reshape(3, CONV_WIDTH)
    mod = lax.dynamic_index_in_dim(mod_all, me, axis=1, keepdims=False).reshape(N_MOD, D_MODEL)
    mod = jnp.concatenate([mod, jnp.zeros((2, D_MODEL), F32)], axis=0)
    w_in_t = w_in_blocks.reshape(IN_PROJ_WIDTH, D_MODEL)
    gather_sems, staged, gather_token = _gather_start(staged, "gather_start_weights")
    mod = _behind(mod, gather_token)

    def weights_out_gu(after):
        got = _gather_pass_on(_gather_wait(gather_sems[0:4], staged[0:2], [after], "gather_wait_out_gu"),
                              "gather_pass_on_out_gu")
        return got[0].reshape(D_MODEL, D_MODEL), got[1].reshape(2 * D_FF, D_MODEL)

    def weights_down(after):
        got = _gather_pass_on(_gather_wait(gather_sems[4:6], staged[2:3], [after], "gather_wait_down"),
                              "gather_pass_on_down")
        return got[0].reshape(D_FF, D_MODEL)

    started = {}

    def exchange(name, dw):
        st = _exchange_start(dw.reshape(N_DEV, dw.shape[0] // N_DEV, dw.shape[1]), "exchange_start_" + name)
        started[name] = st
        return st[4]

    dx, packed, d_rel = _local_step(
        x[0], loss_target[0], mod, w_in_t, bias, weights_out_gu, weights_down, g_norm1, sinks[0], conv_w_full,
        g_attn_out, g_conv_out, g_norm2, g_final[None, :], exchange)

    def zone(a):
        return lax.dynamic_update_slice(jnp.zeros((N_DEV,) + a.shape, F32), a[None], (me,) + (0,) * a.ndim)

    shared = _share_start([packed, d_rel], [zone(packed), zone(d_rel)], "share_small_start")

    def finish(name, after, w, m, v, tr):
        src, land = _exchange_wait(started[name], after, "exchange_wait_" + name)
        return _adamw_parts(w, m, v, src, land, me_arr, tr, "adamw_" + name)

    g_down, d_down, nm_down, nv_down = finish("w_down", [shared[2][0]], w_down[0], m_w_down[0], v_w_down[0], 176)
    g_gu, d_gu, nm_gu, nv_gu = finish("w_gu", [nv_down], w_gu[0].T, m_w_gu[0].T, v_w_gu[0].T, 352)
    g_out, d_out, nm_out, nv_out = finish("w_out", [nv_gu], w_out[0], m_w_out[0], v_w_out[0], 128)

    packed_all, rel_all = _share_wait(shared, [nv_out], "share_small_wait")
    g_ada = _w_ada_grad(me_arr, cond_all, packed_all, ada_cols)
    d_ada, nm_ada, nv_ada = _adamw(w_ada[0], m_w_ada[0], v_w_ada[0], g_ada, 256, "adamw_w_ada")
    as_rows = {"conv_w": lambda a: a[0], "g_final": lambda a: a[None, :]}
    small_state = {
        "rel_bias": (rel_bias, m_rel_bias, v_rel_bias), "b_ada": (b_ada, m_b_ada, v_b_ada),
        "g_norm1": (g_norm1, m_g_norm1, v_g_norm1), "sinks": (sinks, m_sinks, v_sinks),
        "conv_w": (conv_w, m_conv_w, v_conv_w), "g_attn_out": (g_attn_out, m_g_attn_out, v_g_attn_out),
        "g_conv_out": (g_conv_out, m_g_conv_out, v_g_conv_out), "g_norm2": (g_norm2, m_g_norm2, v_g_norm2),
        "g_final": (g_final, m_g_final, v_g_final),
    }
    state = [tuple(as_rows.get(name, lambda a: a)(a) for a in small_state[name]) for name, _ in SMALL_PARAMS]
    loss_row, small_out = _small_update(me_arr, packed_all, rel_all, state, [])
    loss = loss_row[0, 0]
    small_res = {name: tuple(a.reshape(small_state[name][0].shape) for a in res)
                 for (name, _), res in zip(SMALL_PARAMS, small_out)}

    g_in, d_in, nm_in, nv_in = finish("w_in", [loss_row, nv_ada], w_in[0].T, m_w_in[0].T, v_w_in[0].T, 144)

    big = {
        "w_ada": (g_ada[None], d_ada[None], nm_ada[None], nv_ada[None]),
        "w_in": (g_in.T[None], d_in.T[None], nm_in.T[None], nv_in.T[None]),
        "w_out": (g_out[None], d_out[None], nm_out[None], nv_out[None]),
        "w_gu": (g_gu.T[None], d_gu.T[None], nm_gu.T[None], nv_gu.T[None]),
        "w_down": (g_down[None], d_down[None], nm_down[None], nv_down[None]),
    }
    order = ["rel_bias", "w_ada", "b_ada", "g_norm1", "w_in", "sinks", "conv_w", "g_attn_out", "g_conv_out", "w_out",
             "g_norm2", "w_gu", "w_down", "g_final"]
    results = [big[k] if k in big else small_res[k] for k in order]
    return (loss, dx[None], *[r[0] for r in results], *[r[1] for r in results], *[r[2] for r in results],
            *[r[3] for r in results])
```

```python
import math

import jax
import jax.numpy as jnp
from jax import lax
from jax.experimental import pallas as pl
from jax.experimental.pallas import tpu as pltpu

F32 = jnp.float32
BF16 = jnp.bfloat16

D_MODEL = 1024
HEAD_DIM = 64
N_Q_HEADS = 8
ATTN_WIDTH = 512
KV_WIDTH = 128
CONV_WIDTH = 512
IN_PROJ_WIDTH = 2304
D_FF = 2816
N_MOD = 6
N_BUCKETS = 32
MAX_DISTANCE = 128
BLOCK = 128
EPS = 1e-6
NEG_INF = -1e30
SCALE = HEAD_DIM ** -0.5
N_DEV = 8

ADAM_LR = 0.001
ADAM_B1 = 0.9
ADAM_B2 = 0.999
ADAM_EPS = 1e-08
ADAM_WD = 0.01
ADAM_STEP = 10

SH1, SC1, G1, SH2, SC2, G2 = range(6)

VMEM_LIMIT_LARGE = 60 * 1024 * 1024
WEIGHT_GRAD_ROWS = 2048
FFN_CHUNKS = 1
PREV_ROWS = 16
MIXER_BLOCKS = 4
MESH_ID = pl.DeviceIdType.MESH

OFF_DMOD = 0
OFF_GN1 = OFF_DMOD + N_MOD * D_MODEL
OFF_SINK = OFF_GN1 + D_MODEL
OFF_GATT = OFF_SINK + 128
OFF_GCV = OFF_GATT + ATTN_WIDTH
OFF_GN2 = OFF_GCV + CONV_WIDTH
OFF_GFIN = OFF_GN2 + D_MODEL
OFF_CONVW = OFF_GFIN + D_MODEL
OFF_LOSS = OFF_CONVW + 3 * CONV_WIDTH
PACKED = OFF_LOSS + 128


def _params(sem=None, vmem=None):
    return pltpu.CompilerParams(dimension_semantics=sem, vmem_limit_bytes=vmem)


def _coming_behind(body):
    def skipping(after_ref, *refs):
        body(*refs)

    return skipping


ANY_SPEC = pl.BlockSpec(memory_space=pl.ANY)


def _full(shape):
    nd = len(shape)
    return pl.BlockSpec(shape, lambda *_: (0,) * nd)


def _rows(tm, width):
    return pl.BlockSpec((tm, width), lambda i, *_: (i, 0))


def _sigmoid(x):
    return 1.0 / (1.0 + jnp.exp(-x))


def _rsqrt_mean_sq(x):
    return lax.rsqrt(jnp.mean(x * x, axis=-1, keepdims=True) + EPS)


def _colsum(x):
    return jnp.sum(x, axis=0, keepdims=True)


def _dot(a, b):
    return jnp.dot(a, b, preferred_element_type=F32)


def _dot_nt(a, b):
    return lax.dot_general(a, b, (((1,), (1,)), ((), ())), preferred_element_type=F32)


def _dot_tn(a, b):
    return lax.dot_general(a, b, (((0,), (0,)), ((), ())), preferred_element_type=F32)


def _mesh_position():
    return lax.axis_index("x"), lax.axis_index("y"), lax.axis_index("c")


def _linear(p):
    return 4 * p[0] + 2 * p[1] + p[2]


def _peer(k):
    x, y, c = _mesh_position()
    return (1 - x if k & 4 else x, 1 - y if k & 2 else y, 1 - c if k & 1 else c)


HBM_SPEC = pl.BlockSpec(memory_space=pltpu.HBM)
SEM_SPEC = pl.BlockSpec(memory_space=pltpu.SEMAPHORE)
DATAFLOW = pltpu.SideEffectType.DATAFLOW_SIDE_EFFECTING


def _exchange_start(src, name):
    r, c = src.shape[1:]

    def body(src_ref, land_ref, send_sems, recv_sems, src_thru, land_thru, token):
        for k in range(1, N_DEV):
            peer = _peer(k)
            pltpu.make_async_remote_copy(
                src_ref=src_ref.at[_linear(peer)], dst_ref=land_ref.at[k - 1],
                send_sem=send_sems.at[k - 1], recv_sem=recv_sems.at[k - 1],
                device_id=peer, device_id_type=MESH_ID).start()
        token[...] = jnp.zeros_like(token)

    land = lax.empty((N_DEV - 1, r, c), src.dtype)
    return pl.pallas_call(
        body, name=name,
        out_shape=(pltpu.SemaphoreType.DMA((N_DEV - 1,)), pltpu.SemaphoreType.DMA((N_DEV - 1,)),
                   pltpu.HBM(src.shape, src.dtype), pltpu.HBM(land.shape, land.dtype),
                   jax.ShapeDtypeStruct((8, 128), F32)),
        in_specs=(HBM_SPEC, HBM_SPEC),
        out_specs=(SEM_SPEC, SEM_SPEC, HBM_SPEC, HBM_SPEC, pl.BlockSpec(memory_space=pltpu.VMEM)),
        input_output_aliases={0: 2, 1: 3},
        compiler_params=pltpu.CompilerParams(has_side_effects=DATAFLOW),
    )(pltpu.with_memory_space_constraint(src, pltpu.HBM), pltpu.with_memory_space_constraint(land, pltpu.HBM))


def _exchange_wait(started, after, name):
    send_sems, recv_sems, src_thru, land_thru, _ = started

    def body(src_ref, land_ref, send_sems, recv_sems, *rest):
        for k in range(1, N_DEV):
            cp = pltpu.make_async_remote_copy(
                src_ref=src_ref.at[0], dst_ref=land_ref.at[k - 1],
                send_sem=send_sems.at[k - 1], recv_sem=recv_sems.at[k - 1],
                device_id=_peer(k), device_id_type=MESH_ID)
            cp.wait_send()
            cp.wait_recv()

    return pl.pallas_call(
        body, name=name,
        out_shape=(pltpu.HBM(src_thru.shape, src_thru.dtype), pltpu.HBM(land_thru.shape, land_thru.dtype)),
        in_specs=(HBM_SPEC, HBM_SPEC, SEM_SPEC, SEM_SPEC) + (pl.BlockSpec(memory_space=pl.ANY),) * len(after),
        out_specs=(HBM_SPEC, HBM_SPEC), input_output_aliases={0: 0, 1: 1},
        compiler_params=pltpu.CompilerParams(has_side_effects=DATAFLOW),
    )(src_thru, land_thru, send_sems, recv_sems, *after)


def _share_start(arrs, zones, name):
    n = len(arrs)

    def body(*refs):
        src_refs, zone_refs, sems = refs[:n], refs[n:2 * n], refs[2 * n:4 * n]
        me = _linear(_mesh_position())
        for a in range(n):
            for k in range(1, N_DEV):
                pltpu.make_async_remote_copy(
                    src_ref=src_refs[a], dst_ref=zone_refs[a].at[me],
                    send_sem=sems[2 * a].at[k - 1], recv_sem=sems[2 * a + 1].at[k - 1],
                    device_id=_peer(k), device_id_type=MESH_ID).start()

    outs = pl.pallas_call(
        body, name=name,
        out_shape=tuple(pltpu.SemaphoreType.DMA((N_DEV - 1,)) for _ in range(2 * n))
        + tuple(pltpu.HBM(a.shape, a.dtype) for a in arrs) + tuple(pltpu.HBM(z.shape, z.dtype) for z in zones),
        in_specs=(HBM_SPEC,) * (2 * n),
        out_specs=(SEM_SPEC,) * (2 * n) + (HBM_SPEC,) * (2 * n),
        input_output_aliases={i: 2 * n + i for i in range(2 * n)},
        compiler_params=pltpu.CompilerParams(has_side_effects=DATAFLOW),
    )(*[pltpu.with_memory_space_constraint(a, pltpu.HBM) for a in list(arrs) + list(zones)])
    return outs[:2 * n], outs[2 * n:3 * n], outs[3 * n:]


def _share_wait(started, after, name):
    sems, arrs, zones = started
    n = len(arrs)

    def body(*refs):
        src_refs, zone_refs, sem_refs = refs[:n], refs[n:2 * n], refs[2 * n:4 * n]
        for a in range(n):
            for k in range(1, N_DEV):
                cp = pltpu.make_async_remote_copy(
                    src_ref=src_refs[a], dst_ref=zone_refs[a].at[_linear(_peer(k))],
                    send_sem=sem_refs[2 * a].at[k - 1], recv_sem=sem_refs[2 * a + 1].at[k - 1],
                    device_id=_peer(k), device_id_type=MESH_ID)
                cp.wait_send()
                cp.wait_recv()

    outs = pl.pallas_call(
        body, name=name,
        out_shape=tuple(pltpu.HBM(a.shape, a.dtype) for a in arrs) + tuple(pltpu.HBM(z.shape, z.dtype) for z in zones),
        in_specs=(HBM_SPEC,) * (2 * n) + (SEM_SPEC,) * (2 * n) + (pl.BlockSpec(memory_space=pl.ANY),) * len(after),
        out_specs=(HBM_SPEC,) * (2 * n), input_output_aliases={i: i for i in range(2 * n)},
        compiler_params=pltpu.CompilerParams(has_side_effects=DATAFLOW),
    )(*arrs, *zones, *sems, *after)
    return list(outs[n:])


def _same_core_peers():
    x, y, c = _mesh_position()
    return [(x, y, 1 - c), (1 - x, y, c), (x, 1 - y, c), (1 - x, 1 - y, c)]


def _gather_start(bufs, name):
    n = len(bufs)

    def body(*refs):
        buf_refs, rest = refs[:n], refs[n:]
        sems, token = rest[:2 * n], rest[-1]
        me = _linear(_mesh_position())
        for a in range(n):
            for k, peer in enumerate(_same_core_peers()):
                pltpu.make_async_remote_copy(
                    src_ref=buf_refs[a].at[me], dst_ref=buf_refs[a].at[me],
                    send_sem=sems[2 * a].at[k], recv_sem=sems[2 * a + 1].at[k],
                    device_id=peer, device_id_type=MESH_ID).start()
        token[...] = jnp.zeros_like(token)

    outs = pl.pallas_call(
        body, name=name,
        out_shape=tuple(pltpu.SemaphoreType.DMA((4,)) for _ in range(2 * n))
        + tuple(pltpu.HBM(b.shape, b.dtype) for b in bufs) + (jax.ShapeDtypeStruct((8, 128), F32),),
        in_specs=(HBM_SPEC,) * n,
        out_specs=(SEM_SPEC,) * (2 * n) + (HBM_SPEC,) * n + (pl.BlockSpec(memory_space=pltpu.VMEM),),
        input_output_aliases={a: 2 * n + a for a in range(n)},
        compiler_params=pltpu.CompilerParams(has_side_effects=DATAFLOW),
    )(*[pltpu.with_memory_space_constraint(b, pltpu.HBM) for b in bufs])
    return outs[:2 * n], outs[2 * n:3 * n], outs[3 * n]


def _gather_wait(sems, bufs, after, name):
    n = len(bufs)

    def body(*refs):
        buf_refs, sem_refs = refs[:n], refs[n:3 * n]
        x, y, c = _mesh_position()
        me = _linear((x, y, c))
        for a in range(n):
            for k, peer in enumerate(_same_core_peers()):
                cp = pltpu.make_async_remote_copy(
                    src_ref=buf_refs[a].at[me], dst_ref=buf_refs[a].at[_linear(peer)],
                    send_sem=sem_refs[2 * a].at[k], recv_sem=sem_refs[2 * a + 1].at[k],
                    device_id=peer, device_id_type=MESH_ID)
                cp.wait_send()
                cp.wait_recv()

    return list(pl.pallas_call(
        body, name=name,
        out_shape=tuple(pltpu.HBM(b.shape, b.dtype) for b in bufs),
        in_specs=(HBM_SPEC,) * n + (SEM_SPEC,) * (2 * n) + (pl.BlockSpec(memory_space=pl.ANY),) * len(after),
        out_specs=(HBM_SPEC,) * n, input_output_aliases={a: a for a in range(n)},
        compiler_params=pltpu.CompilerParams(has_side_effects=DATAFLOW),
    )(*bufs, *sems, *after))


def _gather_pass_on(bufs, name):
    n = len(bufs)

    def body(*refs):
        out_refs = refs[n:2 * n]
        send_sems, recv_sems = refs[2 * n:]
        x, y, c = _mesh_position()
        sibling = (x, y, 1 - c)
        chips = [(1 - x, y), (x, 1 - y), (1 - x, 1 - y)]
        copies = []
        for a in range(n):
            for j, chip in enumerate(chips):
                block = out_refs[a].at[_linear((*chip, c))]
                copies.append(pltpu.make_async_remote_copy(
                    src_ref=block, dst_ref=block, send_sem=send_sems.at[3 * a + j], recv_sem=recv_sems.at[3 * a + j],
                    device_id=sibling, device_id_type=MESH_ID))
                copies[-1].start()
        for a in range(n):
            for j, chip in enumerate(chips):
                copies[3 * a + j].wait_send()
                theirs = out_refs[a].at[_linear((*chip, 1 - c))]
                pltpu.make_async_remote_copy(
                    src_ref=theirs, dst_ref=theirs, send_sem=send_sems.at[3 * a + j], recv_sem=recv_sems.at[3 * a + j],
                    device_id=sibling, device_id_type=MESH_ID).wait_recv()

    hbm = pl.BlockSpec(memory_space=pl.ANY)
    return list(pl.pallas_call(
        body, name=name,
        out_shape=[jax.ShapeDtypeStruct(b.shape, b.dtype) for b in bufs],
        in_specs=[hbm] * n, out_specs=[hbm] * n, input_output_aliases={a: a for a in range(n)},
        scratch_shapes=[pltpu.SemaphoreType.DMA((3 * n,)), pltpu.SemaphoreType.DMA((3 * n,))],
    )(*bufs))


def _open_step(c, conv_w, w_ada, b_cols, w_in_t, later, rel_bias, bucket):
    cols = w_ada.shape[1]
    n_later = len(later)

    def body(c_ref, cw_ref, wa_ref, b_ref, w_ref, *rest):
        later_refs, rb_ref, bk_ref = rest[:n_later], rest[n_later], rest[n_later + 1]
        cond_ref, conv_ref, mod_ref, win_ref = rest[n_later + 2:n_later + 6]
        staged_refs, bias_ref = rest[n_later + 6:2 * n_later + 6], rest[2 * n_later + 6]
        cond_own, mod_own, stage = rest[2 * n_later + 7:2 * n_later + 10]
        later_stage = rest[2 * n_later + 10:3 * n_later + 10]
        s_send, s_recv, w_send, w_recv, local_sems = rest[3 * n_later + 10:]
        x, y, cc = _mesh_position()
        me = _linear((x, y, cc))
        sibling = (x, y, 1 - cc)
        chips = [(1 - x, y), (x, 1 - y), (1 - x, 1 - y)]
        v = c_ref[...]
        cond_own[...] = v * _sigmoid(v)
        stage[...] = w_ref[...].astype(BF16)

        def small(rnd, a, k, src, dst, slot):
            return pltpu.make_async_remote_copy(
                src_ref=src, dst_ref=dst.at[slot], send_sem=s_send.at[rnd, a, k - 1], recv_sem=s_recv.at[rnd, a, k - 1],
                device_id=_peer(k), device_id_type=MESH_ID)

        def block(p):
            return win_ref.at[_linear(p)]

        def big(k, blk, to, src=None):
            return pltpu.make_async_remote_copy(
                src_ref=block(blk) if src is None else src, dst_ref=block(blk),
                send_sem=w_send.at[k], recv_sem=w_recv.at[k], device_id=to, device_id_type=MESH_ID)

        mine = [pltpu.make_async_copy(cond_own, cond_ref.at[me], local_sems.at[0]),
                pltpu.make_async_copy(cw_ref, conv_ref.at[me], local_sems.at[1]),
                pltpu.make_async_copy(stage, block((x, y, cc)), local_sems.at[2])]
        for cp in mine:
            cp.start()
        sends = []
        for k in range(1, N_DEV):
            sends += [small(0, 0, k, cond_own, cond_ref, me), small(0, 1, k, cw_ref, conv_ref, me)]
        for cp in sends:
            cp.start()
        first = [big(0, (x, y, cc), sibling, src=stage)]
        first += [big(1 + j, (x, y, cc), (*chip, cc), src=stage) for j, chip in enumerate(chips)]
        for cp in first:
            cp.start()
        for a in range(n_later):
            later_stage[a][...] = later_refs[a][...].astype(BF16)
            mine.append(pltpu.make_async_copy(later_stage[a], staged_refs[a].at[me], local_sems.at[4 + a]))
            mine[-1].start()
        _fill_bias_table(rb_ref, bk_ref, bias_ref)
        for k in range(1, N_DEV):
            small(0, 0, k, cond_own, cond_ref, _linear(_peer(k))).wait_recv()
            small(0, 1, k, cw_ref, conv_ref, _linear(_peer(k))).wait_recv()
        mine[0].wait()
        cond_all = jnp.concatenate([cond_ref[k] for k in range(N_DEV)], axis=0)
        mod_own[...] = _dot(cond_all, wa_ref[...]) + b_ref[...]
        mine.append(pltpu.make_async_copy(mod_own, mod_ref.at[me], local_sems.at[3]))
        mine[-1].start()
        second = [small(1, 0, k, mod_own, mod_ref, me) for k in range(1, N_DEV)]
        for cp in second:
            cp.start()
        passed = []
        for j, chip in enumerate(chips):
            big(1 + j, (*chip, cc), (x, y, cc)).wait_recv()
            fwd = big(4 + j, (*chip, cc), sibling)
            fwd.start()
            passed.append(fwd)
        big(0, sibling, (x, y, cc)).wait_recv()
        for j, chip in enumerate(chips):
            big(4 + j, (*chip, 1 - cc), (x, y, cc)).wait_recv()
        for k in range(1, N_DEV):
            small(1, 0, k, mod_own, mod_ref, _linear(_peer(k))).wait_recv()
        for cp in sends + first + second + passed:
            cp.wait_send()
        for cp in mine[1:]:
            cp.wait()

    vmem = pl.BlockSpec(memory_space=pltpu.VMEM)
    outs = pl.pallas_call(
        body, name="open_step",
        out_shape=[jax.ShapeDtypeStruct((N_DEV,) + c.shape, F32), jax.ShapeDtypeStruct((N_DEV,) + conv_w.shape, F32),
                   jax.ShapeDtypeStruct((N_DEV, N_DEV, cols), F32),
                   jax.ShapeDtypeStruct((N_DEV,) + w_in_t.shape, BF16)]
        + [jax.ShapeDtypeStruct((N_DEV,) + a.shape, BF16) for a in later]
        + [jax.ShapeDtypeStruct((N_Q_HEADS, BLOCK, 2 * BLOCK), F32)],
        in_specs=[vmem] * (5 + n_later) + [pl.BlockSpec(memory_space=pltpu.SMEM), vmem],
        out_specs=[vmem, vmem, vmem, ANY_SPEC] + [ANY_SPEC] * n_later + [vmem],
        scratch_shapes=[pltpu.VMEM(c.shape, F32), pltpu.VMEM((N_DEV, cols), F32), pltpu.VMEM(w_in_t.shape, BF16)]
        + [pltpu.VMEM(a.shape, BF16) for a in later]
        + [pltpu.SemaphoreType.DMA((2, 2, N_DEV - 1)), pltpu.SemaphoreType.DMA((2, 2, N_DEV - 1)),
           pltpu.SemaphoreType.DMA((7,)), pltpu.SemaphoreType.DMA((7,)),
           pltpu.SemaphoreType.DMA((4 + n_later,))],
        compiler_params=_params(vmem=VMEM_LIMIT_LARGE),
    )(c, conv_w, w_ada, b_cols, w_in_t, *later, rel_bias, bucket)
    return outs[0], outs[1], outs[2], outs[3], list(outs[4:4 + n_later]), outs[4 + n_later]


def _in_proj(x, mod, g_norm1, w_in, tm):
    s = x.shape[0]

    def body(x_ref, mod_ref, g_ref, w_ref, h_ref, q_ref, kv_ref, gb_ref, gc_ref, xc_ref):
        xf = x_ref[...]
        n = xf * _rsqrt_mean_sq(xf) * g_ref[...]
        h = (n * (1.0 + mod_ref[SC1:SC1 + 1, :]) + mod_ref[SH1:SH1 + 1, :]).astype(BF16)
        h_ref[...] = h
        p = _dot_nt(h, w_ref[...])
        q_ref[...] = p[:, 0:512].astype(BF16)
        kv_ref[...] = p[:, 512:768].astype(BF16)
        gb_ref[...] = p[:, 768:1280].astype(BF16)
        gc_ref[...] = p[:, 1280:1792].astype(BF16)
        xc_ref[...] = p[:, 1792:2304].astype(BF16)

    return pl.pallas_call(
        body, name="in_proj", grid=(s // tm,),
        in_specs=[_rows(tm, D_MODEL), _full((8, D_MODEL)), _full((1, D_MODEL)), _full((IN_PROJ_WIDTH, D_MODEL))],
        out_specs=[_rows(tm, D_MODEL), _rows(tm, 512), _rows(tm, 256), _rows(tm, 512), _rows(tm, 512), _rows(tm, 512)],
        out_shape=[jax.ShapeDtypeStruct((s, D_MODEL), BF16), jax.ShapeDtypeStruct((s, 512), BF16),
                   jax.ShapeDtypeStruct((s, 256), BF16), jax.ShapeDtypeStruct((s, 512), BF16),
                   jax.ShapeDtypeStruct((s, 512), BF16), jax.ShapeDtypeStruct((s, 512), BF16)],
        compiler_params=_params(("arbitrary",), VMEM_LIMIT_LARGE),
    )(x, mod, g_norm1, w_in)


def _t5_bucket(dist):
    max_exact = N_BUCKETS // 2
    is_small = dist < max_exact
    d = jnp.maximum(dist, 1).astype(F32)
    large = max_exact + (jnp.log(d / max_exact) / math.log(MAX_DISTANCE / max_exact)
                         * (N_BUCKETS - max_exact)).astype(jnp.int32)
    large = jnp.minimum(large, N_BUCKETS - 1)
    return jnp.where(is_small, dist, large)


def _bucket_table():
    qi = jnp.arange(BLOCK, dtype=jnp.int32)[:, None]
    sj = jnp.arange(2 * BLOCK, dtype=jnp.int32)[None, :]
    return _t5_bucket(jnp.maximum(qi + BLOCK - sj, 0))


def _window_mask():
    qi = lax.broadcasted_iota(jnp.int32, (BLOCK, 2 * BLOCK), 0)
    sj = lax.broadcasted_iota(jnp.int32, (BLOCK, 2 * BLOCK), 1)
    dist = qi + BLOCK - sj
    return (dist >= 0) & (dist < BLOCK)


def _fill_bias_table(rb_ref, bk_ref, o_ref):
    bk = bk_ref[...]
    inside = _window_mask()
    for h in range(N_Q_HEADS):
        acc = jnp.zeros((BLOCK, 2 * BLOCK), F32)
        for b in range(N_BUCKETS):
            acc = jnp.where(bk == b, rb_ref[b, h], acc)
        o_ref[h] = jnp.where(inside, acc, NEG_INF)


def _load_kv_window(kv_ref, n):
    prev = jnp.maximum(n - 1, 0)
    kvw = jnp.concatenate([kv_ref[pl.ds(pl.multiple_of(prev * BLOCK, BLOCK), BLOCK), :],
                           kv_ref[pl.ds(pl.multiple_of(n * BLOCK, BLOCK), BLOCK), :]], axis=0)
    k, v = kvw[:, 0:128], kvw[:, 128:256]
    k_sw = pltpu.roll(k.astype(F32), 64, 1).astype(BF16)
    v_sw = pltpu.roll(v.astype(F32), 64, 1).astype(BF16)
    return (k, k_sw), (v, v_sw)


def _conv_taps(gc, xc, gc_prev, xc_prev, n):
    u = gc * xc
    before = jnp.where(n > 0, gc_prev.astype(F32) * xc_prev.astype(F32), 0.0)
    last = before.shape[0] - 1
    row = lax.broadcasted_iota(jnp.int32, u.shape, 0)
    u1 = jnp.where(row == 0, before[last:last + 1, :], pltpu.roll(u, 1, 0))
    u2 = jnp.where(row == 0, before[last - 1:last, :],
                   jnp.where(row == 1, before[last:last + 1, :], pltpu.roll(u, 2, 0)))
    return u, u1, u2


def _mixer_fwd(q, kv, gb, gc, xc, bias, sinks, conv_w, g_attn, g_conv):
    s = q.shape[0]
    nb = s // BLOCK

    per_step = min(MIXER_BLOCKS, nb)
    tile = per_step * BLOCK

    def one_block(n, rows, before, sink_ref, q_ref, kv_ref, gb_ref, gc_ref, xc_ref, bias_ref, cw_ref, ga_ref,
                  gcv_ref, attn_ref, merged_ref, lse_ref):
        ks, vs = _load_kv_window(kv_ref, n)
        lane = lax.broadcasted_iota(jnp.int32, (BLOCK, BLOCK), 1)
        low = lane < HEAD_DIM
        col = lax.broadcasted_iota(jnp.int32, (BLOCK, 2 * BLOCK), 1)
        no_prev = (col < BLOCK) & (n == 0)
        lse_all = jnp.zeros((BLOCK, BLOCK), F32)
        pairs = []
        for p in range(4):
            qp = q_ref[rows, 128 * p:128 * (p + 1)].astype(F32)
            kvh = p // 2
            res = []
            for e in range(2):
                h = 2 * p + e
                qm = jnp.where(low if e == 0 else ~low, qp, 0.0).astype(BF16)
                sw = 0 if kvh == e else 1
                sc = _dot_nt(qm, ks[sw]) * SCALE + bias_ref[h]
                sc = jnp.where(no_prev, NEG_INF, sc)
                sink = sink_ref[h]
                m = jnp.maximum(jnp.max(sc, axis=-1, keepdims=True), sink)
                pe = jnp.exp(sc - m)
                den = jnp.sum(pe, axis=-1, keepdims=True) + jnp.exp(sink - m)
                res.append(_dot(pe.astype(BF16), vs[sw]) / den)
                lse_all = lse_all + jnp.where(lane == h, m + jnp.log(den), 0.0)
            pairs.append(jnp.where(low, res[0], res[1]))
        attn = jnp.concatenate(pairs, axis=1)
        attn_ref[rows, :] = attn
        lse_ref[rows, :] = lse_all
        u, u1, u2 = _conv_taps(gc_ref[rows, :].astype(F32), xc_ref[rows, :].astype(F32), before[0], before[1], n)
        cw = cw_ref[...]
        cv = gb_ref[rows, :].astype(F32) * (cw[0:1, :] * u2 + cw[1:2, :] * u1 + cw[2:3, :] * u)
        an = attn * _rsqrt_mean_sq(attn) * ga_ref[...]
        cn = cv * _rsqrt_mean_sq(cv) * gcv_ref[...]
        merged_ref[rows, :] = jnp.concatenate([an, cn], axis=1).astype(BF16)

    def body(sink_ref, q_ref, kv_ref, gb_ref, gc_ref, xc_ref, gcp_ref, xcp_ref, *rest):
        step = pl.program_id(0)
        for sub in range(per_step):
            rows = slice(sub * BLOCK, (sub + 1) * BLOCK)
            ahead = slice(sub * BLOCK - PREV_ROWS, sub * BLOCK)
            before = (gcp_ref[...], xcp_ref[...]) if sub == 0 else (gc_ref[ahead, :], xc_ref[ahead, :])
            one_block(step * per_step + sub, rows, before, sink_ref, q_ref, kv_ref, gb_ref, gc_ref, xc_ref, *rest)

    blk = lambda w: pl.BlockSpec((tile, w), lambda n: (n, 0))
    prev8 = pl.BlockSpec((PREV_ROWS, 512), lambda n: (jnp.maximum(n * (tile // PREV_ROWS) - 1, 0), 0))
    return pl.pallas_call(
        body, name="mixer_fwd", grid=(nb // per_step,),
        in_specs=[pl.BlockSpec(memory_space=pltpu.SMEM), blk(512), _full((s, 256)), blk(512), blk(512), blk(512),
                  prev8, prev8, _full((N_Q_HEADS, BLOCK, 2 * BLOCK)), _full((3, 512)), _full((1, 512)),
                  _full((1, 512))],
        out_specs=[blk(512), blk(1024), blk(128)],
        out_shape=[jax.ShapeDtypeStruct((s, 512), F32), jax.ShapeDtypeStruct((s, 1024), BF16),
                   jax.ShapeDtypeStruct((s, 128), F32)],
        compiler_params=_params(("arbitrary",)),
    )(sinks, q, kv, gb, gc, xc, gc, xc, bias, conv_w, g_attn, g_conv)


def _out_proj(merged, x, mod, w_out, tm):
    s = x.shape[0]

    def body(m_ref, x_ref, mod_ref, w_ref, o_ref, x1_ref):
        o = _dot(m_ref[...], w_ref[...])
        o_ref[...] = o.astype(BF16)
        x1_ref[...] = x_ref[...] + mod_ref[G1:G1 + 1, :] * o

    return pl.pallas_call(
        body, name="out_proj", grid=(s // tm,),
        in_specs=[_rows(tm, D_MODEL), _rows(tm, D_MODEL), _full((8, D_MODEL)), _full((D_MODEL, D_MODEL))],
        out_specs=[_rows(tm, D_MODEL), _rows(tm, D_MODEL)],
        out_shape=[jax.ShapeDtypeStruct((s, D_MODEL), BF16), jax.ShapeDtypeStruct((s, D_MODEL), F32)],
        compiler_params=_params(("arbitrary",)),
    )(merged, x, mod, w_out)


def _resident(shape):
    nd = len(shape)
    return pl.BlockSpec(shape, lambda *_: (0,) * nd, pipeline_mode=pl.Buffered(1))


def _ffn(x1, o1, mod, g_norm2, w_gu, w_down, w_out, g_final, target, tm):
    s = x1.shape[0]
    chunk = D_FF // FFN_CHUNKS

    def body(x_ref, o1_ref, mod_ref, g_ref, wgu_ref, wd_ref, wo_ref, gf_ref, t_ref,
             h_ref, act_ref, do_ref, dgu_ref, dx1_ref, do1_ref, dm_ref, small_ref):
        @pl.when(pl.program_id(0) == 0)
        def _():
            small_ref[...] = jnp.zeros_like(small_ref)

        xf = x_ref[...]
        n = xf * _rsqrt_mean_sq(xf) * g_ref[...]
        h = (n * (1.0 + mod_ref[SC2:SC2 + 1, :]) + mod_ref[SH2:SH2 + 1, :]).astype(BF16)
        h_ref[...] = h
        gates, ups, o = [], [], None
        for j in range(FFN_CHUNKS):
            lo = j * chunk
            gate = _dot_nt(h, wgu_ref[lo:lo + chunk, :])
            up = _dot_nt(h, wgu_ref[D_FF + lo:D_FF + lo + chunk, :])
            sg = _sigmoid(gate)
            act = (gate * sg * up).astype(BF16)
            act_ref[:, lo:lo + chunk] = act
            gates.append((up * (sg * (1.0 + gate * (1.0 - sg)))).astype(BF16))
            ups.append((gate * sg).astype(BF16))
            part = _dot(act, wd_ref[lo:lo + chunk, :])
            o = part if o is None else o + part
        g2 = mod_ref[G2:G2 + 1, :]
        x2 = xf + g2 * o
        r = _rsqrt_mean_sq(x2)
        xn = x2 * r
        gf = gf_ref[...]
        err = xn * gf - t_ref[...]
        dy = err * (1.0 / D_MODEL)
        dxn = dy * gf
        dx2 = r * (dxn - xn * jnp.mean(dxn * xn, axis=-1, keepdims=True))
        small_ref[4:5, :] += _colsum(dy * xn)
        small_ref[5:6, :] += _colsum(err * err)
        small_ref[3:4, :] += _colsum(dx2 * o)
        do = (dx2 * g2).astype(BF16)
        do_ref[...] = do
        dh = None
        for j in range(FFN_CHUNKS):
            lo = j * chunk
            dact = _dot_nt(do, wd_ref[lo:lo + chunk, :])
            dgate = (dact * gates[j].astype(F32)).astype(BF16)
            dup = (dact * ups[j].astype(F32)).astype(BF16)
            dgu_ref[:, lo:lo + chunk] = dgate
            dgu_ref[:, D_FF + lo:D_FF + lo + chunk] = dup
            part = _dot(dgate, wgu_ref[lo:lo + chunk, :]) + _dot(dup, wgu_ref[D_FF + lo:D_FF + lo + chunk, :])
            dh = part if dh is None else dh + part
        dx1 = dx2 + _norm_mod_bwd(dh, xf, g_ref[...], mod_ref[SC2:SC2 + 1, :], small_ref)
        dx1_ref[...] = dx1.astype(BF16)
        small_ref[7:8, :] += _colsum(dx1 * o1_ref[...].astype(F32))
        do1 = (dx1 * mod_ref[G1:G1 + 1, :]).astype(BF16)
        do1_ref[...] = do1
        dm_ref[...] = _dot_nt(do1, wo_ref[...]).astype(BF16)

        @pl.when(pl.program_id(0) == pl.num_programs(0) - 1)
        def _():
            total = jnp.sum(small_ref[5:6, :], axis=-1, keepdims=True) * (0.5 / D_MODEL)
            small_ref[6:7, :] = jnp.broadcast_to(total, (1, D_MODEL))

    narrow = jax.ShapeDtypeStruct((s, D_MODEL), BF16)
    return pl.pallas_call(
        body, name="ffn", grid=(s // tm,),
        in_specs=[_rows(tm, D_MODEL), _rows(tm, D_MODEL), _full((8, D_MODEL)), _full((1, D_MODEL)),
                  _resident((2 * D_FF, D_MODEL)), _resident((D_FF, D_MODEL)), _resident((D_MODEL, D_MODEL)),
                  _full((1, D_MODEL)), _rows(tm, D_MODEL)],
        out_specs=[_rows(tm, D_MODEL), _rows(tm, D_FF), _rows(tm, D_MODEL), _rows(tm, 2 * D_FF), _rows(tm, D_MODEL),
                   _rows(tm, D_MODEL), _rows(tm, D_MODEL), _full((8, D_MODEL))],
        out_shape=[narrow, jax.ShapeDtypeStruct((s, D_FF), BF16), narrow, jax.ShapeDtypeStruct((s, 2 * D_FF), BF16),
                   narrow, narrow, narrow, jax.ShapeDtypeStruct((8, D_MODEL), F32)],
        compiler_params=_params(("arbitrary",), VMEM_LIMIT_LARGE),
    )(x1, o1, mod, g_norm2, w_gu, w_down, w_out, g_final, target)


def _norm_mod_bwd(dh, xf, g, scale_row, small_ref):
    r = _rsqrt_mean_sq(xf)
    xn = xf * r
    small_ref[0:1, :] += _colsum(dh)
    small_ref[1:2, :] += _colsum(dh * (xn * g))
    dn = dh * (1.0 + scale_row)
    small_ref[2:3, :] += _colsum(dn * xn)
    dxn = dn * g
    return r * (dxn - xn * jnp.mean(dxn * xn, axis=-1, keepdims=True))


def _group_norm_bwd(dm, a, g):
    r = _rsqrt_mean_sq(a)
    an = a * r
    dan = dm * g
    return r * (dan - an * jnp.mean(dan * an, axis=-1, keepdims=True)), _colsum(dm * an)


def _mixer_bwd(after, q, kv, gb, gc, xc, bias, sinks, conv_w, g_attn, g_conv, attn, lse, dmerged):
    s = q.shape[0]
    nb = s // BLOCK

    per_step = min(MIXER_BLOCKS, nb)
    tile = per_step * BLOCK
    steps = nb // per_step

    def one_block(n, rows, before, nxt, sink_ref, q_ref, kv_ref, gb_ref, gc_ref, xc_ref, bias_ref, cw_ref, ga_ref,
                  gcv_ref, attn_ref, lse_ref, dm_ref, dproj_ref, dbias_ref, dsink_ref, small_ref):
        next_dy, next_dkv = nxt
        dm = dm_ref[rows, :].astype(F32)
        gbv, gcv_, xcv = gb_ref[rows, :].astype(F32), gc_ref[rows, :].astype(F32), xc_ref[rows, :].astype(F32)
        u, u1, u2 = _conv_taps(gcv_, xcv, before[0], before[1], n)
        cw = cw_ref[...]
        yv = cw[0:1, :] * u2 + cw[1:2, :] * u1 + cw[2:3, :] * u
        dcv, dg_conv = _group_norm_bwd(dm[:, 512:1024], gbv * yv, gcv_ref[...])
        small_ref[1:2, :] += dg_conv
        dproj_ref[rows, 768:1280] = (dcv * yv).astype(BF16)
        dy = dcv * gbv
        row = lax.broadcasted_iota(jnp.int32, dy.shape, 0)
        d1 = jnp.where(row == BLOCK - 1, next_dy[0:1, :], pltpu.roll(dy, BLOCK - 1, 0))
        d2 = jnp.where(row == BLOCK - 2, next_dy[0:1, :],
                       jnp.where(row == BLOCK - 1, next_dy[1:2, :], pltpu.roll(dy, BLOCK - 2, 0)))
        du = cw[2:3, :] * dy + cw[1:2, :] * d1 + cw[0:1, :] * d2
        dproj_ref[rows, 1280:1792] = (du * xcv).astype(BF16)
        dproj_ref[rows, 1792:2304] = (du * gcv_).astype(BF16)
        small_ref[2:3, :] += _colsum(dy * u2)
        small_ref[3:4, :] += _colsum(dy * u1)
        small_ref[4:5, :] += _colsum(dy * u)

        attn_v = attn_ref[rows, :]
        dout, dg_attn = _group_norm_bwd(dm[:, 0:512], attn_v, ga_ref[...])
        small_ref[0:1, :] += dg_attn
        ks, vs = _load_kv_window(kv_ref, n)
        lane = lax.broadcasted_iota(jnp.int32, (BLOCK, BLOCK), 1)
        low = lane < HEAD_DIM
        col = lax.broadcasted_iota(jnp.int32, (BLOCK, 2 * BLOCK), 1)
        no_prev = (col < BLOCK) & (n == 0)
        lse_all = lse_ref[rows, :]
        dsink = jnp.zeros((BLOCK, BLOCK), F32)
        dq_pairs = []
        dk_groups, dv_groups = [], []
        for kvh in range(2):
            ds_rows, pr_rows, q_rows, do_rows = [], [], [], []
            for p in (2 * kvh, 2 * kvh + 1):
                qp = q_ref[rows, 128 * p:128 * (p + 1)].astype(F32)
                do_p = dout[:, 128 * p:128 * (p + 1)]
                prod = do_p * attn_v[:, 128 * p:128 * (p + 1)]
                res = []
                for e in range(2):
                    h = 2 * p + e
                    half = low if e == 0 else ~low
                    qm = jnp.where(half, qp, 0.0).astype(BF16)
                    dom = jnp.where(half, do_p, 0.0).astype(BF16)
                    delta = jnp.sum(jnp.where(half, prod, 0.0), axis=-1, keepdims=True)
                    lse_h = jnp.sum(jnp.where(lane == h, lse_all, 0.0), axis=-1, keepdims=True)
                    sw = 0 if kvh == e else 1
                    sc = _dot_nt(qm, ks[sw]) * SCALE + bias_ref[h]
                    sc = jnp.where(no_prev, NEG_INF, sc)
                    pr = jnp.exp(sc - lse_h)
                    dp = _dot_nt(dom, vs[sw])
                    ds = pr * (dp - delta)
                    dbias_ref[h] += ds
                    dsink = dsink + jnp.where(lane == h, -jnp.exp(sink_ref[h] - lse_h) * delta, 0.0)
                    dsb = ds.astype(BF16)
                    res.append(_dot(dsb, ks[sw]) * SCALE)
                    ds_rows.append(dsb)
                    pr_rows.append(pr.astype(BF16))
                    q_rows.append(qm)
                    do_rows.append(dom)
                dq_pairs.append(jnp.where(low, res[0], res[1]))
            dk_g = _dot_tn(jnp.concatenate(ds_rows, axis=0), jnp.concatenate(q_rows, axis=0)) * SCALE
            dv_g = _dot_tn(jnp.concatenate(pr_rows, axis=0), jnp.concatenate(do_rows, axis=0))
            dk_groups.append(dk_g + pltpu.roll(dk_g, 64, 1))
            dv_groups.append(dv_g + pltpu.roll(dv_g, 64, 1))
        dproj_ref[rows, 0:512] = jnp.concatenate(dq_pairs, axis=1).astype(BF16)
        dsink_ref[...] += dsink
        low_kv = lax.broadcasted_iota(jnp.int32, (2 * BLOCK, BLOCK), 1) < HEAD_DIM
        dkv_win = jnp.concatenate([jnp.where(low_kv, dk_groups[0], dk_groups[1]),
                                   jnp.where(low_kv, dv_groups[0], dv_groups[1])], axis=1)
        dproj_ref[rows, 512:768] = (dkv_win[BLOCK:2 * BLOCK, :] + next_dkv).astype(BF16)
        return dy[0:8, :], dkv_win[0:BLOCK, :]

    def body(sink_ref, q_ref, kv_ref, gb_ref, gc_ref, xc_ref, gcp_ref, xcp_ref, *rest):
        refs, dy_ref, dkv_ref = rest[:-2], rest[-2], rest[-1]
        dbias_ref, dsink_ref, small_ref = refs[8], refs[9], refs[10]
        step = pl.program_id(0)

        @pl.when(step == 0)
        def _():
            dbias_ref[...] = jnp.zeros_like(dbias_ref)
            dsink_ref[...] = jnp.zeros_like(dsink_ref)
            small_ref[...] = jnp.zeros_like(small_ref)
            dy_ref[...] = jnp.zeros_like(dy_ref)
            dkv_ref[...] = jnp.zeros_like(dkv_ref)

        nxt = (dy_ref[...], dkv_ref[...])
        for sub in reversed(range(per_step)):
            rows = slice(sub * BLOCK, (sub + 1) * BLOCK)
            ahead = slice(sub * BLOCK - PREV_ROWS, sub * BLOCK)
            before = (gcp_ref[...], xcp_ref[...]) if sub == 0 else (gc_ref[ahead, :], xc_ref[ahead, :])
            nxt = one_block((steps - 1 - step) * per_step + sub, rows, before, nxt,
                            sink_ref, q_ref, kv_ref, gb_ref, gc_ref, xc_ref, *refs)
        dy_ref[...], dkv_ref[...] = nxt

        @pl.when(step == steps - 1)
        def _():
            small_ref[5:6, :] = jnp.concatenate([_colsum(dsink_ref[...]), jnp.zeros((1, 512 - BLOCK), F32)], axis=1)

    blk = lambda w: pl.BlockSpec((tile, w), lambda t: (steps - 1 - t, 0))
    prev8 = pl.BlockSpec((PREV_ROWS, 512),
                         lambda t: (jnp.maximum((steps - 1 - t) * (tile // PREV_ROWS) - 1, 0), 0))
    bf = lambda w: jax.ShapeDtypeStruct((s, w), BF16)
    return pl.pallas_call(
        _coming_behind(body), name="mixer_bwd", grid=(steps,),
        in_specs=[ANY_SPEC, pl.BlockSpec(memory_space=pltpu.SMEM), blk(512), _full((s, 256)), blk(512), blk(512), blk(512),
                  prev8, prev8, _full((N_Q_HEADS, BLOCK, 2 * BLOCK)), _full((3, 512)), _full((1, 512)),
                  _full((1, 512)), blk(512), blk(128), blk(1024)],
        out_specs=[blk(IN_PROJ_WIDTH), _full((N_Q_HEADS, BLOCK, 2 * BLOCK)), _full((BLOCK, BLOCK)), _full((8, 512))],
        out_shape=[bf(IN_PROJ_WIDTH), jax.ShapeDtypeStruct((N_Q_HEADS, BLOCK, 2 * BLOCK), F32),
                   jax.ShapeDtypeStruct((BLOCK, BLOCK), F32), jax.ShapeDtypeStruct((8, 512), F32)],
        scratch_shapes=[pltpu.VMEM((8, 512), F32), pltpu.VMEM((BLOCK, 2 * KV_WIDTH), F32)],
        compiler_params=_params(("arbitrary",), VMEM_LIMIT_LARGE),
    )(after, sinks, q, kv, gb, gc, xc, gc, xc, bias, conv_w, g_attn, g_conv, attn, lse, dmerged)


def _in_proj_bwd(after, dproj, x, dx1, mod, g_norm1, w_in, tm):
    s = x.shape[0]

    def body(dproj_ref, x_ref, dx1_ref, mod_ref, g_ref, w_ref, dx_ref, small_ref):
        @pl.when(pl.program_id(0) == 0)
        def _():
            small_ref[...] = jnp.zeros_like(small_ref)

        dh = _dot(dproj_ref[...], w_ref[...])
        dx_ref[...] = dx1_ref[...].astype(F32) + _norm_mod_bwd(dh, x_ref[...], g_ref[...], mod_ref[SC1:SC1 + 1, :],
                                                               small_ref)

    return pl.pallas_call(
        _coming_behind(body), name="in_proj_bwd", grid=(s // tm,),
        in_specs=[ANY_SPEC, _rows(tm, IN_PROJ_WIDTH), _rows(tm, D_MODEL), _rows(tm, D_MODEL), _full((8, D_MODEL)),
                  _full((1, D_MODEL)), _full((IN_PROJ_WIDTH, D_MODEL))],
        out_specs=[_rows(tm, D_MODEL), _full((8, D_MODEL))],
        out_shape=[jax.ShapeDtypeStruct((s, D_MODEL), F32), jax.ShapeDtypeStruct((8, D_MODEL), F32)],
        compiler_params=_params(("arbitrary",), VMEM_LIMIT_LARGE),
    )(after, dproj, x, dx1, mod, g_norm1, w_in)


def _weight_grad(a, b, tk, ts, name, after=None):
    s, k = a.shape
    n = b.shape[1]
    nt = s // ts
    extra = [] if after is None else [after]

    def body(a_ref, b_ref, *rest):
        o_ref, acc_ref = rest[-2:]
        t = pl.program_id(1)
        @pl.when(t == 0)
        def _():
            acc_ref[...] = jnp.zeros_like(acc_ref)

        acc = acc_ref[...] + _dot_tn(a_ref[...], b_ref[...])
        acc_ref[...] = acc
        o_ref[...] = acc.astype(BF16)

    return pl.pallas_call(
        body, name=name, grid=(k // tk, nt),
        in_specs=[pl.BlockSpec((ts, tk), lambda i, t: (t, i)), pl.BlockSpec((ts, n), lambda i, t: (t, 0))]
        + [ANY_SPEC] * len(extra),
        out_specs=pl.BlockSpec((tk, n), lambda i, t: (i, 0)),
        out_shape=jax.ShapeDtypeStruct((k, n), BF16),
        scratch_shapes=[pltpu.VMEM((tk, n), F32)],
        compiler_params=_params(("arbitrary", "arbitrary"), VMEM_LIMIT_LARGE),
    )(a, b, *extra)


def _rel_bias_grad(dbias, bucket):
    def body(db_ref, bk_ref, o_ref, rows_ref):
        bk = bk_ref[...]
        for b in range(N_BUCKETS):
            sel = (bk == b).astype(F32)
            for h in range(N_Q_HEADS):
                rows_ref[N_BUCKETS * h + b:N_BUCKETS * h + b + 1, :] = _colsum(db_ref[h] * sel)
        head = lax.broadcasted_iota(jnp.int32, (N_BUCKETS, N_Q_HEADS), 1)
        out = jnp.zeros((N_BUCKETS, N_Q_HEADS), F32)
        for h in range(N_Q_HEADS):
            per_bucket = jnp.sum(rows_ref[N_BUCKETS * h:N_BUCKETS * (h + 1), :], axis=-1, keepdims=True)
            out = out + jnp.where(head == h, per_bucket, 0.0)
        o_ref[...] = out

    return pl.pallas_call(
        body, name="rel_bias_grad",
        out_shape=jax.ShapeDtypeStruct((N_BUCKETS, N_Q_HEADS), F32),
        scratch_shapes=[pltpu.VMEM((N_BUCKETS * N_Q_HEADS, 2 * BLOCK), F32)],
    )(dbias, bucket)


def _lanes_from(x, start, width):
    n = x.shape[1]
    return pltpu.roll(x, (n - start) % n, 1)[:, 0:width]


def _adamw_w_ada(me, cond_all, packed_all, w, m, v, tr):
    r, cols = w.shape

    def body(me_ref, c_ref, p_ref, w_ref, m_ref, v_ref, g_ref, d_ref, mo_ref, vo_ref):
        dmod = jnp.concatenate([p_ref[k][:, OFF_DMOD:OFF_DMOD + N_MOD * D_MODEL] for k in range(N_DEV)], axis=0)
        mine = _lanes_from(dmod, me_ref[0] * cols, cols)
        pad = lambda a: jnp.concatenate([a, jnp.zeros((128 - N_DEV, a.shape[1]), F32)], axis=0)
        g = _dot_tn(pad(c_ref[...]), pad(mine))
        g_ref[...] = g
        d_ref[...], mo_ref[...], vo_ref[...] = _adam_math(w_ref[...], g, m_ref[...], v_ref[...])

    tile = pl.BlockSpec((tr, cols), lambda i, me_ref: (i, 0))
    return pl.pallas_call(
        body, name="adamw_w_ada",
        grid_spec=pltpu.PrefetchScalarGridSpec(
            num_scalar_prefetch=1, grid=(r // tr,),
            in_specs=[pl.BlockSpec((N_DEV, tr), lambda i, me_ref: (0, i)),
                      pl.BlockSpec(packed_all.shape, lambda i, me_ref: (0, 0, 0)), tile, tile, tile],
            out_specs=[tile] * 4),
        out_shape=[jax.ShapeDtypeStruct((r, cols), F32)] * 4,
        compiler_params=_params(("arbitrary",)),
    )(me, cond_all, packed_all, w, m, v)


SMALL_PARAMS = (("rel_bias", None), ("b_ada", (OFF_DMOD, N_MOD * D_MODEL)), ("g_norm1", (OFF_GN1, D_MODEL)),
                ("sinks", (OFF_SINK, N_Q_HEADS)), ("conv_w", None), ("g_attn_out", (OFF_GATT, ATTN_WIDTH)),
                ("g_conv_out", (OFF_GCV, CONV_WIDTH)), ("g_norm2", (OFF_GN2, D_MODEL)),
                ("g_final", (OFF_GFIN, D_MODEL)))


def _small_update(me, packed_all, rel_all, state, after):
    n_p = len(SMALL_PARAMS)
    flat = [a for triple in state for a in triple]
    conv_cols = state[4][0].shape[1]

    def body(me_ref, p_ref, r_ref, *refs):
        ins = refs[:3 * n_p]
        loss_ref, outs = refs[3 * n_p + len(after)], refs[3 * n_p + len(after) + 1:]
        small, rel = p_ref[0], r_ref[0]
        for k in range(1, N_DEV):
            small = small + p_ref[k]
            rel = rel + r_ref[k]
        loss_ref[...] = small[:, OFF_LOSS:OFF_LOSS + 128]
        taps = jnp.concatenate([small[:, OFF_CONVW + CONV_WIDTH * j:OFF_CONVW + CONV_WIDTH * (j + 1)]
                                for j in range(3)] + [jnp.zeros((5, CONV_WIDTH), F32)], axis=0)
        conv_g = _lanes_from(taps, me_ref[0] * conv_cols, conv_cols)[0:3, :]
        for i, (name, lanes) in enumerate(SMALL_PARAMS):
            g = rel if name == "rel_bias" else conv_g if name == "conv_w" else small[:, lanes[0]:lanes[0] + lanes[1]]
            w_ref, m_ref, v_ref = ins[3 * i:3 * i + 3]
            outs[4 * i][...] = g
            outs[4 * i + 1][...], outs[4 * i + 2][...], outs[4 * i + 3][...] = _adam_math(
                w_ref[...], g, m_ref[...], v_ref[...])

    vmem = pl.BlockSpec(memory_space=pltpu.VMEM)
    out_shape = [jax.ShapeDtypeStruct((1, 128), F32)]
    for w, _, _ in state:
        out_shape += [jax.ShapeDtypeStruct(w.shape, F32)] * 4
    outs = pl.pallas_call(
        body, name="small_update",
        in_specs=[pl.BlockSpec(memory_space=pltpu.SMEM), vmem, vmem] + [vmem] * len(flat)
        + [pl.BlockSpec(memory_space=pl.ANY)] * len(after),
        out_shape=out_shape,
    )(me, packed_all, rel_all, *flat, *after)
    return outs[0], [tuple(outs[1 + 4 * i:5 + 4 * i]) for i in range(n_p)]


def _adam_math(w, g, m, v):
    m = ADAM_B1 * m + (1.0 - ADAM_B1) * g
    v = ADAM_B2 * v + (1.0 - ADAM_B2) * (g * g)
    m_hat = m / (1.0 - ADAM_B1 ** ADAM_STEP)
    v_hat = v / (1.0 - ADAM_B2 ** ADAM_STEP)
    delta = -ADAM_LR * (m_hat / (jnp.sqrt(v_hat) + ADAM_EPS) + ADAM_WD * w)
    return delta, m, v


def _adamw_parts(w, m, v, local, land, me, tr, name):
    r, c = w.shape

    def body(me_ref, w_ref, m_ref, v_ref, own_ref, land_ref, g_ref, d_ref, mo_ref, vo_ref):
        g = own_ref[0].astype(F32)
        for k in range(N_DEV - 1):
            g = g + land_ref[k].astype(F32)
        g_ref[...] = g
        d_ref[...], mo_ref[...], vo_ref[...] = _adam_math(w_ref[...], g, m_ref[...], v_ref[...])

    tile = pl.BlockSpec((tr, c), lambda i, me_ref: (i, 0))
    return pl.pallas_call(
        body, name=name,
        grid_spec=pltpu.PrefetchScalarGridSpec(
            num_scalar_prefetch=1, grid=(r // tr,),
            in_specs=[tile, tile, tile, pl.BlockSpec((1, tr, c), lambda i, me_ref: (me_ref[0], i, 0)),
                      pl.BlockSpec((N_DEV - 1, tr, c), lambda i, me_ref: (0, i, 0))],
            out_specs=[tile] * 4),
        out_shape=[jax.ShapeDtypeStruct((r, c), F32)] * 4,
        compiler_params=_params(("arbitrary",)),
    )(me, w, m, v, local, land)


def _behind(a, token):
    return a + token[0:a.shape[0], 0:1]


def _local_step(x, target, mod, w_in_t, bias, weights_out_gu, weights_down, g_norm1, sinks, conv_w, g_attn,
                g_conv, g_norm2, g_final, exchange):
    s = x.shape[0]
    tm = min(512, s)
    tm_small = min(256, s)
    bucket = _bucket_table()

    h, q, kv, gb, gc, xc = _in_proj(x, mod, g_norm1, w_in_t, tm)
    attn, merged, lse = _mixer_fwd(q, kv, gb, gc, xc, bias, sinks, conv_w, g_attn, g_conv)
    w_out, w_gu_t = weights_out_gu(merged)
    o1, x1 = _out_proj(merged, x, mod, w_out, tm)
    w_down = weights_down(x1)
    h2, act, do2, dgu, dx1, do1, dmerged, sm_2 = _ffn(x1, o1, mod, g_norm2, w_gu_t, w_down, w_out, g_final, target,
                                                      tm_small)
    ts = min(WEIGHT_GRAD_ROWS, s)
    tok_down = exchange("w_down", _weight_grad(act, do2, D_FF // 2, ts, "w_down_grad"))
    tok_gu = exchange("w_gu", _weight_grad(dgu, h2, D_FF // 2, ts, "w_gu_grad", after=tok_down))
    tok_out = exchange("w_out", _weight_grad(merged, do1, D_MODEL, ts, "w_out_grad", after=tok_gu))
    dproj, dbias, dsink, sm_mix = _mixer_bwd(
        tok_out, q, kv, gb, gc, xc, bias, sinks, conv_w, g_attn, g_conv, attn, lse, dmerged)
    tok_in = exchange("w_in", _weight_grad(dproj, h, IN_PROJ_WIDTH // 2, ts, "w_in_grad"))
    dx, sm_1 = _in_proj_bwd(tok_in, dproj, x, dx1, mod, g_norm1, w_in_t, min(1024, s))
    d_rel = _rel_bias_grad(dbias, bucket)

    packed = jnp.concatenate([
        sm_1[0], sm_1[1], sm_2[7], sm_2[0], sm_2[1], sm_2[3],
        sm_1[2],
        sm_mix[5, 0:128],
        sm_mix[0], sm_mix[1],
        sm_2[2],
        sm_2[4],
        sm_mix[2], sm_mix[3], sm_mix[4],
        sm_2[6, 0:128],
    ])[None, :]
    return dx, packed, d_rel


def kernel(x, c, rel_bias, w_ada, b_ada, g_norm1, w_in, sinks, conv_w, g_attn_out, g_conv_out, w_out, g_norm2, w_gu, w_down, g_final, loss_target, m_rel_bias, m_w_ada, m_b_ada, m_g_norm1, m_w_in, m_sinks, m_conv_w, m_g_attn_out, m_g_conv_out, m_w_out, m_g_norm2, m_w_gu, m_w_down, m_g_final, v_rel_bias, v_w_ada, v_b_ada, v_g_norm1, v_w_in, v_sinks, v_conv_w, v_g_attn_out, v_g_conv_out, v_w_out, v_g_norm2, v_w_gu, v_w_down, v_g_final):
    me = _linear(_mesh_position())
    me_arr = jnp.reshape(me, (1,)).astype(jnp.int32)
    ada_cols = w_ada.shape[2]
    tm = min(512, x.shape[1])

    b_cols = lax.dynamic_slice_in_dim(b_ada, me * ada_cols, ada_cols, axis=1)
    cond_all, conv_w_all, mod_all, w_in_blocks, staged, bias = _open_step(
        c, conv_w[0], w_ada[0], b_cols, w_in[0].T, [w_out[0], w_gu[0].T, w_down[0]], rel_bias, _bucket_table())
    cond_all = cond_all[:, 0, :]
    conv_w_full = conv_w_all.transpose(1, 0, 2).reshape(3, CONV_WIDTH)
    mod = lax.dynamic_index_in_dim(mod_all, me, axis=1, keepdims=False).reshape(N_MOD, D_MODEL)
    mod = jnp.concatenate([mod, jnp.zeros((2, D_MODEL), F32)], axis=0)
    w_in_t = w_in_blocks.reshape(IN_PROJ_WIDTH, D_MODEL)
    gather_sems, staged, gather_token = _gather_start(staged, "gather_start_weights")
    mod = _behind(mod, gather_token)

    def weights_out_gu(after):
        got = _gather_pass_on(_gather_wait(gather_sems[0:4], staged[0:2], [after], "gather_wait_out_gu"),
                              "gather_pass_on_out_gu")
        return got[0].reshape(D_MODEL, D_MODEL), got[1].reshape(2 * D_FF, D_MODEL)

    def weights_down(after):
        got = _gather_pass_on(_gather_wait(gather_sems[4:6], staged[2:3], [after], "gather_wait_down"),
                              "gather_pass_on_down")
        return got[0].reshape(D_FF, D_MODEL)

    started = {}

    def exchange(name, dw):
        st = _exchange_start(dw.reshape(N_DEV, dw.shape[0] // N_DEV, dw.shape[1]), "exchange_start_" + name)
        started[name] = st
        return st[4]

    dx, packed, d_rel = _local_step(
        x[0], loss_target[0], mod, w_in_t, bias, weights_out_gu, weights_down, g_norm1, sinks[0], conv_w_full,
        g_attn_out, g_conv_out, g_norm2, g_final[None, :], exchange)

    def zone(a):
        return lax.dynamic_update_slice(jnp.zeros((N_DEV,) + a.shape, F32), a[None], (me,) + (0,) * a.ndim)

    shared = _share_start([packed, d_rel], [zone(packed), zone(d_rel)], "share_small_start")

    def finish(name, after, w, m, v, tr):
        src, land = _exchange_wait(started[name], after, "exchange_wait_" + name)
        return _adamw_parts(w, m, v, src, land, me_arr, tr, "adamw_" + name)

    g_down, d_down, nm_down, nv_down = finish("w_down", [shared[2][0]], w_down[0], m_w_down[0], v_w_down[0], 176)
    g_gu, d_gu, nm_gu, nv_gu = finish("w_gu", [nv_down], w_gu[0].T, m_w_gu[0].T, v_w_gu[0].T, 352)
    g_out, d_out, nm_out, nv_out = finish("w_out", [nv_gu], w_out[0], m_w_out[0], v_w_out[0], 128)

    packed_all, rel_all = _share_wait(shared, [nv_out], "share_small_wait")
    g_ada, d_ada, nm_ada, nv_ada = _adamw_w_ada(me_arr, cond_all, packed_all, w_ada[0], m_w_ada[0], v_w_ada[0], 256)
    as_rows = {"conv_w": lambda a: a[0], "g_final": lambda a: a[None, :]}
    small_state = {
        "rel_bias": (rel_bias, m_rel_bias, v_rel_bias), "b_ada": (b_ada, m_b_ada, v_b_ada),
        "g_norm1": (g_norm1, m_g_norm1, v_g_norm1), "sinks": (sinks, m_sinks, v_sinks),
        "conv_w": (conv_w, m_conv_w, v_conv_w), "g_attn_out": (g_attn_out, m_g_attn_out, v_g_attn_out),
        "g_conv_out": (g_conv_out, m_g_conv_out, v_g_conv_out), "g_norm2": (g_norm2, m_g_norm2, v_g_norm2),
        "g_final": (g_final, m_g_final, v_g_final),
    }
    state = [tuple(as_rows.get(name, lambda a: a)(a) for a in small_state[name]) for name, _ in SMALL_PARAMS]
    loss_row, small_out = _small_update(me_arr, packed_all, rel_all, state, [])
    loss = loss_row[0, 0]
    small_res = {name: tuple(a.reshape(small_state[name][0].shape) for a in res)
                 for (name, _), res in zip(SMALL_PARAMS, small_out)}

    g_in, d_in, nm_in, nv_in = finish("w_in", [loss_row, nv_ada], w_in[0].T, m_w_in[0].T, v_w_in[0].T, 144)

    big = {
        "w_ada": (g_ada[None], d_ada[None], nm_ada[None], nv_ada[None]),
        "w_in": (g_in.T[None], d_in.T[None], nm_in.T[None], nv_in.T[None]),
        "w_out": (g_out[None], d_out[None], nm_out[None], nv_out[None]),
        "w_gu": (g_gu.T[None], d_gu.T[None], nm_gu.T[None], nv_gu.T[None]),
        "w_down": (g_down[None], d_down[None], nm_down[None], nv_down[None]),
    }
    order = ["rel_bias", "w_ada", "b_ada", "g_norm1", "w_in", "sinks", "conv_w", "g_attn_out", "g_conv_out", "w_out",
             "g_norm2", "w_gu", "w_down", "g_final"]
    results = [big[k] if k in big else small_res[k] for k in order]
    return (loss, dx[None], *[r[0] for r in results], *[r[1] for r in results], *[r[2] for r in results],
            *[r[3] for r in results])
```

```python
import math

import jax
import jax.numpy as jnp
from jax import lax
from jax.experimental import pallas as pl
from jax.experimental.pallas import tpu as pltpu

F32 = jnp.float32
BF16 = jnp.bfloat16

D_MODEL = 1024
HEAD_DIM = 64
N_Q_HEADS = 8
ATTN_WIDTH = 512
KV_WIDTH = 128
CONV_WIDTH = 512
IN_PROJ_WIDTH = 2304
D_FF = 2816
N_MOD = 6
N_BUCKETS = 32
MAX_DISTANCE = 128
BLOCK = 128
EPS = 1e-6
NEG_INF = -1e30
SCALE = HEAD_DIM ** -0.5
N_DEV = 8

ADAM_LR = 0.001
ADAM_B1 = 0.9
ADAM_B2 = 0.999
ADAM_EPS = 1e-08
ADAM_WD = 0.01
ADAM_STEP = 10

SH1, SC1, G1, SH2, SC2, G2 = range(6)

VMEM_LIMIT_LARGE = 60 * 1024 * 1024
WEIGHT_GRAD_ROWS = 4096
FFN_CHUNKS = 1
PREV_ROWS = 16
MIXER_BLOCKS = 4
MESH_ID = pl.DeviceIdType.MESH

OFF_DMOD = 0
OFF_GN1 = OFF_DMOD + N_MOD * D_MODEL
OFF_SINK = OFF_GN1 + D_MODEL
OFF_GATT = OFF_SINK + 128
OFF_GCV = OFF_GATT + ATTN_WIDTH
OFF_GN2 = OFF_GCV + CONV_WIDTH
OFF_GFIN = OFF_GN2 + D_MODEL
OFF_CONVW = OFF_GFIN + D_MODEL
OFF_LOSS = OFF_CONVW + 3 * CONV_WIDTH
PACKED = OFF_LOSS + 128


def _params(sem=None, vmem=None):
    return pltpu.CompilerParams(dimension_semantics=sem, vmem_limit_bytes=vmem)


def _coming_behind(body):
    def skipping(after_ref, *refs):
        body(*refs)

    return skipping


ANY_SPEC = pl.BlockSpec(memory_space=pl.ANY)


def _full(shape):
    nd = len(shape)
    return pl.BlockSpec(shape, lambda *_: (0,) * nd)


def _rows(tm, width):
    return pl.BlockSpec((tm, width), lambda i, *_: (i, 0))


def _sigmoid(x):
    return 1.0 / (1.0 + jnp.exp(-x))


def _rsqrt_mean_sq(x):
    return lax.rsqrt(jnp.mean(x * x, axis=-1, keepdims=True) + EPS)


def _colsum(x):
    return jnp.sum(x, axis=0, keepdims=True)


def _dot(a, b):
    return jnp.dot(a, b, preferred_element_type=F32)


def _dot_nt(a, b):
    return lax.dot_general(a, b, (((1,), (1,)), ((), ())), preferred_element_type=F32)


def _dot_tn(a, b):
    return lax.dot_general(a, b, (((0,), (0,)), ((), ())), preferred_element_type=F32)


def _mesh_position():
    return lax.axis_index("x"), lax.axis_index("y"), lax.axis_index("c")


def _linear(p):
    return 4 * p[0] + 2 * p[1] + p[2]


def _peer(k):
    x, y, c = _mesh_position()
    return (1 - x if k & 4 else x, 1 - y if k & 2 else y, 1 - c if k & 1 else c)


HBM_SPEC = pl.BlockSpec(memory_space=pltpu.HBM)
SEM_SPEC = pl.BlockSpec(memory_space=pltpu.SEMAPHORE)
DATAFLOW = pltpu.SideEffectType.DATAFLOW_SIDE_EFFECTING


def _exchange_start(src, name):
    r, c = src.shape[1:]

    def body(src_ref, land_ref, send_sems, recv_sems, src_thru, land_thru, token):
        for k in range(1, N_DEV):
            peer = _peer(k)
            pltpu.make_async_remote_copy(
                src_ref=src_ref.at[_linear(peer)], dst_ref=land_ref.at[k - 1],
                send_sem=send_sems.at[k - 1], recv_sem=recv_sems.at[k - 1],
                device_id=peer, device_id_type=MESH_ID).start()
        token[...] = jnp.zeros_like(token)

    land = lax.empty((N_DEV - 1, r, c), src.dtype)
    return pl.pallas_call(
        body, name=name,
        out_shape=(pltpu.SemaphoreType.DMA((N_DEV - 1,)), pltpu.SemaphoreType.DMA((N_DEV - 1,)),
                   pltpu.HBM(src.shape, src.dtype), pltpu.HBM(land.shape, land.dtype),
                   jax.ShapeDtypeStruct((8, 128), F32)),
        in_specs=(HBM_SPEC, HBM_SPEC),
        out_specs=(SEM_SPEC, SEM_SPEC, HBM_SPEC, HBM_SPEC, pl.BlockSpec(memory_space=pltpu.VMEM)),
        input_output_aliases={0: 2, 1: 3},
        compiler_params=pltpu.CompilerParams(has_side_effects=DATAFLOW),
    )(pltpu.with_memory_space_constraint(src, pltpu.HBM), pltpu.with_memory_space_constraint(land, pltpu.HBM))


def _exchange_wait(started, after, name):
    send_sems, recv_sems, src_thru, land_thru, _ = started

    def body(src_ref, land_ref, send_sems, recv_sems, *rest):
        for k in range(1, N_DEV):
            cp = pltpu.make_async_remote_copy(
                src_ref=src_ref.at[0], dst_ref=land_ref.at[k - 1],
                send_sem=send_sems.at[k - 1], recv_sem=recv_sems.at[k - 1],
                device_id=_peer(k), device_id_type=MESH_ID)
            cp.wait_send()
            cp.wait_recv()

    return pl.pallas_call(
        body, name=name,
        out_shape=(pltpu.HBM(src_thru.shape, src_thru.dtype), pltpu.HBM(land_thru.shape, land_thru.dtype)),
        in_specs=(HBM_SPEC, HBM_SPEC, SEM_SPEC, SEM_SPEC) + (pl.BlockSpec(memory_space=pl.ANY),) * len(after),
        out_specs=(HBM_SPEC, HBM_SPEC), input_output_aliases={0: 0, 1: 1},
        compiler_params=pltpu.CompilerParams(has_side_effects=DATAFLOW),
    )(src_thru, land_thru, send_sems, recv_sems, *after)


def _share_start(arrs, zones, name):
    n = len(arrs)

    def body(*refs):
        src_refs, zone_refs, sems = refs[:n], refs[n:2 * n], refs[2 * n:4 * n]
        me = _linear(_mesh_position())
        for a in range(n):
            for k in range(1, N_DEV):
                pltpu.make_async_remote_copy(
                    src_ref=src_refs[a], dst_ref=zone_refs[a].at[me],
                    send_sem=sems[2 * a].at[k - 1], recv_sem=sems[2 * a + 1].at[k - 1],
                    device_id=_peer(k), device_id_type=MESH_ID).start()

    outs = pl.pallas_call(
        body, name=name,
        out_shape=tuple(pltpu.SemaphoreType.DMA((N_DEV - 1,)) for _ in range(2 * n))
        + tuple(pltpu.HBM(a.shape, a.dtype) for a in arrs) + tuple(pltpu.HBM(z.shape, z.dtype) for z in zones),
        in_specs=(HBM_SPEC,) * (2 * n),
        out_specs=(SEM_SPEC,) * (2 * n) + (HBM_SPEC,) * (2 * n),
        input_output_aliases={i: 2 * n + i for i in range(2 * n)},
        compiler_params=pltpu.CompilerParams(has_side_effects=DATAFLOW),
    )(*[pltpu.with_memory_space_constraint(a, pltpu.HBM) for a in list(arrs) + list(zones)])
    return outs[:2 * n], outs[2 * n:3 * n], outs[3 * n:]


def _share_wait(started, after, name):
    sems, arrs, zones = started
    n = len(arrs)

    def body(*refs):
        src_refs, zone_refs, sem_refs = refs[:n], refs[n:2 * n], refs[2 * n:4 * n]
        for a in range(n):
            for k in range(1, N_DEV):
                cp = pltpu.make_async_remote_copy(
                    src_ref=src_refs[a], dst_ref=zone_refs[a].at[_linear(_peer(k))],
                    send_sem=sem_refs[2 * a].at[k - 1], recv_sem=sem_refs[2 * a + 1].at[k - 1],
                    device_id=_peer(k), device_id_type=MESH_ID)
                cp.wait_send()
                cp.wait_recv()

    outs = pl.pallas_call(
        body, name=name,
        out_shape=tuple(pltpu.HBM(a.shape, a.dtype) for a in arrs) + tuple(pltpu.HBM(z.shape, z.dtype) for z in zones),
        in_specs=(HBM_SPEC,) * (2 * n) + (SEM_SPEC,) * (2 * n) + (pl.BlockSpec(memory_space=pl.ANY),) * len(after),
        out_specs=(HBM_SPEC,) * (2 * n), input_output_aliases={i: i for i in range(2 * n)},
        compiler_params=pltpu.CompilerParams(has_side_effects=DATAFLOW),
    )(*arrs, *zones, *sems, *after)
    return list(outs[n:])


def _same_core_peers():
    x, y, c = _mesh_position()
    return [(x, y, 1 - c), (1 - x, y, c), (x, 1 - y, c), (1 - x, 1 - y, c)]


def _gather_start(bufs, name):
    n = len(bufs)

    def body(*refs):
        buf_refs, rest = refs[:n], refs[n:]
        sems, token = rest[:2 * n], rest[-1]
        me = _linear(_mesh_position())
        for a in range(n):
            for k, peer in enumerate(_same_core_peers()):
                pltpu.make_async_remote_copy(
                    src_ref=buf_refs[a].at[me], dst_ref=buf_refs[a].at[me],
                    send_sem=sems[2 * a].at[k], recv_sem=sems[2 * a + 1].at[k],
                    device_id=peer, device_id_type=MESH_ID).start()
        token[...] = jnp.zeros_like(token)

    outs = pl.pallas_call(
        body, name=name,
        out_shape=tuple(pltpu.SemaphoreType.DMA((4,)) for _ in range(2 * n))
        + tuple(pltpu.HBM(b.shape, b.dtype) for b in bufs) + (jax.ShapeDtypeStruct((8, 128), F32),),
        in_specs=(HBM_SPEC,) * n,
        out_specs=(SEM_SPEC,) * (2 * n) + (HBM_SPEC,) * n + (pl.BlockSpec(memory_space=pltpu.VMEM),),
        input_output_aliases={a: 2 * n + a for a in range(n)},
        compiler_params=pltpu.CompilerParams(has_side_effects=DATAFLOW),
    )(*[pltpu.with_memory_space_constraint(b, pltpu.HBM) for b in bufs])
    return outs[:2 * n], outs[2 * n:3 * n], outs[3 * n]


def _gather_wait(sems, bufs, after, name):
    n = len(bufs)

    def body(*refs):
        buf_refs, sem_refs = refs[:n], refs[n:3 * n]
        x, y, c = _mesh_position()
        me = _linear((x, y, c))
        for a in range(n):
            for k, peer in enumerate(_same_core_peers()):
                cp = pltpu.make_async_remote_copy(
                    src_ref=buf_refs[a].at[me], dst_ref=buf_refs[a].at[_linear(peer)],
                    send_sem=sem_refs[2 * a].at[k], recv_sem=sem_refs[2 * a + 1].at[k],
                    device_id=peer, device_id_type=MESH_ID)
                cp.wait_send()
                cp.wait_recv()

    return list(pl.pallas_call(
        body, name=name,
        out_shape=tuple(pltpu.HBM(b.shape, b.dtype) for b in bufs),
        in_specs=(HBM_SPEC,) * n + (SEM_SPEC,) * (2 * n) + (pl.BlockSpec(memory_space=pl.ANY),) * len(after),
        out_specs=(HBM_SPEC,) * n, input_output_aliases={a: a for a in range(n)},
        compiler_params=pltpu.CompilerParams(has_side_effects=DATAFLOW),
    )(*bufs, *sems, *after))


def _gather_pass_on(bufs, name):
    n = len(bufs)

    def body(*refs):
        out_refs = refs[n:2 * n]
        send_sems, recv_sems = refs[2 * n:]
        x, y, c = _mesh_position()
        sibling = (x, y, 1 - c)
        chips = [(1 - x, y), (x, 1 - y), (1 - x, 1 - y)]
        copies = []
        for a in range(n):
            for j, chip in enumerate(chips):
                block = out_refs[a].at[_linear((*chip, c))]
                copies.append(pltpu.make_async_remote_copy(
                    src_ref=block, dst_ref=block, send_sem=send_sems.at[3 * a + j], recv_sem=recv_sems.at[3 * a + j],
                    device_id=sibling, device_id_type=MESH_ID))
                copies[-1].start()
        for a in range(n):
            for j, chip in enumerate(chips):
                copies[3 * a + j].wait_send()
                theirs = out_refs[a].at[_linear((*chip, 1 - c))]
                pltpu.make_async_remote_copy(
                    src_ref=theirs, dst_ref=theirs, send_sem=send_sems.at[3 * a + j], recv_sem=recv_sems.at[3 * a + j],
                    device_id=sibling, device_id_type=MESH_ID).wait_recv()

    hbm = pl.BlockSpec(memory_space=pl.ANY)
    return list(pl.pallas_call(
        body, name=name,
        out_shape=[jax.ShapeDtypeStruct(b.shape, b.dtype) for b in bufs],
        in_specs=[hbm] * n, out_specs=[hbm] * n, input_output_aliases={a: a for a in range(n)},
        scratch_shapes=[pltpu.SemaphoreType.DMA((3 * n,)), pltpu.SemaphoreType.DMA((3 * n,))],
    )(*bufs))


def _open_step(c, conv_w, w_ada, b_cols, w_in_t, later, rel_bias, bucket):
    cols = w_ada.shape[1]
    n_later = len(later)

    def body(c_ref, cw_ref, wa_ref, b_ref, w_ref, *rest):
        later_refs, rb_ref, bk_ref = rest[:n_later], rest[n_later], rest[n_later + 1]
        cond_ref, conv_ref, mod_ref, win_ref = rest[n_later + 2:n_later + 6]
        staged_refs, bias_ref = rest[n_later + 6:2 * n_later + 6], rest[2 * n_later + 6]
        cond_own, mod_own, stage = rest[2 * n_later + 7:2 * n_later + 10]
        later_stage = rest[2 * n_later + 10:3 * n_later + 10]
        s_send, s_recv, w_send, w_recv, local_sems = rest[3 * n_later + 10:]
        x, y, cc = _mesh_position()
        me = _linear((x, y, cc))
        sibling = (x, y, 1 - cc)
        chips = [(1 - x, y), (x, 1 - y), (1 - x, 1 - y)]
        v = c_ref[...]
        cond_own[...] = v * _sigmoid(v)
        stage[...] = w_ref[...].astype(BF16)

        def small(rnd, a, k, src, dst, slot):
            return pltpu.make_async_remote_copy(
                src_ref=src, dst_ref=dst.at[slot], send_sem=s_send.at[rnd, a, k - 1], recv_sem=s_recv.at[rnd, a, k - 1],
                device_id=_peer(k), device_id_type=MESH_ID)

        def block(p):
            return win_ref.at[_linear(p)]

        def big(k, blk, to, src=None):
            return pltpu.make_async_remote_copy(
                src_ref=block(blk) if src is None else src, dst_ref=block(blk),
                send_sem=w_send.at[k], recv_sem=w_recv.at[k], device_id=to, device_id_type=MESH_ID)

        mine = [pltpu.make_async_copy(cond_own, cond_ref.at[me], local_sems.at[0]),
                pltpu.make_async_copy(cw_ref, conv_ref.at[me], local_sems.at[1]),
                pltpu.make_async_copy(stage, block((x, y, cc)), local_sems.at[2])]
        for cp in mine:
            cp.start()
        sends = []
        for k in range(1, N_DEV):
            sends += [small(0, 0, k, cond_own, cond_ref, me), small(0, 1, k, cw_ref, conv_ref, me)]
        for cp in sends:
            cp.start()
        first = [big(0, (x, y, cc), sibling, src=stage)]
        first += [big(1 + j, (x, y, cc), (*chip, cc), src=stage) for j, chip in enumerate(chips)]
        for cp in first:
            cp.start()
        for a in range(n_later):
            later_stage[a][...] = later_refs[a][...].astype(BF16)
            mine.append(pltpu.make_async_copy(later_stage[a], staged_refs[a].at[me], local_sems.at[4 + a]))
            mine[-1].start()
        _fill_bias_table(rb_ref, bk_ref, bias_ref)
        for k in range(1, N_DEV):
            small(0, 0, k, cond_own, cond_ref, _linear(_peer(k))).wait_recv()
            small(0, 1, k, cw_ref, conv_ref, _linear(_peer(k))).wait_recv()
        mine[0].wait()
        cond_all = jnp.concatenate([cond_ref[k] for k in range(N_DEV)], axis=0)
        mod_own[...] = _dot(cond_all, wa_ref[...]) + b_ref[...]
        mine.append(pltpu.make_async_copy(mod_own, mod_ref.at[me], local_sems.at[3]))
        mine[-1].start()
        second = [small(1, 0, k, mod_own, mod_ref, me) for k in range(1, N_DEV)]
        for cp in second:
            cp.start()
        passed = []
        for j, chip in enumerate(chips):
            big(1 + j, (*chip, cc), (x, y, cc)).wait_recv()
            fwd = big(4 + j, (*chip, cc), sibling)
            fwd.start()
            passed.append(fwd)
        big(0, sibling, (x, y, cc)).wait_recv()
        for j, chip in enumerate(chips):
            big(4 + j, (*chip, 1 - cc), (x, y, cc)).wait_recv()
        for k in range(1, N_DEV):
            small(1, 0, k, mod_own, mod_ref, _linear(_peer(k))).wait_recv()
        for cp in sends + first + second + passed:
            cp.wait_send()
        for cp in mine[1:]:
            cp.wait()

    vmem = pl.BlockSpec(memory_space=pltpu.VMEM)
    outs = pl.pallas_call(
        body, name="open_step",
        out_shape=[jax.ShapeDtypeStruct((N_DEV,) + c.shape, F32), jax.ShapeDtypeStruct((N_DEV,) + conv_w.shape, F32),
                   jax.ShapeDtypeStruct((N_DEV, N_DEV, cols), F32),
                   jax.ShapeDtypeStruct((N_DEV,) + w_in_t.shape, BF16)]
        + [jax.ShapeDtypeStruct((N_DEV,) + a.shape, BF16) for a in later]
        + [jax.ShapeDtypeStruct((N_Q_HEADS, BLOCK, 2 * BLOCK), F32)],
        in_specs=[vmem] * (5 + n_later) + [pl.BlockSpec(memory_space=pltpu.SMEM), vmem],
        out_specs=[vmem, vmem, vmem, ANY_SPEC] + [ANY_SPEC] * n_later + [vmem],
        scratch_shapes=[pltpu.VMEM(c.shape, F32), pltpu.VMEM((N_DEV, cols), F32), pltpu.VMEM(w_in_t.shape, BF16)]
        + [pltpu.VMEM(a.shape, BF16) for a in later]
        + [pltpu.SemaphoreType.DMA((2, 2, N_DEV - 1)), pltpu.SemaphoreType.DMA((2, 2, N_DEV - 1)),
           pltpu.SemaphoreType.DMA((7,)), pltpu.SemaphoreType.DMA((7,)),
           pltpu.SemaphoreType.DMA((4 + n_later,))],
        compiler_params=_params(vmem=VMEM_LIMIT_LARGE),
    )(c, conv_w, w_ada, b_cols, w_in_t, *later, rel_bias, bucket)
    return outs[0], outs[1], outs[2], outs[3], list(outs[4:4 + n_later]), outs[4 + n_later]


def _in_proj(x, mod, g_norm1, w_in, tm):
    s = x.shape[0]

    def body(x_ref, mod_ref, g_ref, w_ref, h_ref, q_ref, kv_ref, gb_ref, gc_ref, xc_ref):
        xf = x_ref[...]
        n = xf * _rsqrt_mean_sq(xf) * g_ref[...]
        h = (n * (1.0 + mod_ref[SC1:SC1 + 1, :]) + mod_ref[SH1:SH1 + 1, :]).astype(BF16)
        h_ref[...] = h
        p = _dot_nt(h, w_ref[...])
        q_ref[...] = p[:, 0:512].astype(BF16)
        kv_ref[...] = p[:, 512:768].astype(BF16)
        gb_ref[...] = p[:, 768:1280].astype(BF16)
        gc_ref[...] = p[:, 1280:1792].astype(BF16)
        xc_ref[...] = p[:, 1792:2304].astype(BF16)

    return pl.pallas_call(
        body, name="in_proj", grid=(s // tm,),
        in_specs=[_rows(tm, D_MODEL), _full((8, D_MODEL)), _full((1, D_MODEL)), _full((IN_PROJ_WIDTH, D_MODEL))],
        out_specs=[_rows(tm, D_MODEL), _rows(tm, 512), _rows(tm, 256), _rows(tm, 512), _rows(tm, 512), _rows(tm, 512)],
        out_shape=[jax.ShapeDtypeStruct((s, D_MODEL), BF16), jax.ShapeDtypeStruct((s, 512), BF16),
                   jax.ShapeDtypeStruct((s, 256), BF16), jax.ShapeDtypeStruct((s, 512), BF16),
                   jax.ShapeDtypeStruct((s, 512), BF16), jax.ShapeDtypeStruct((s, 512), BF16)],
        compiler_params=_params(("arbitrary",), VMEM_LIMIT_LARGE),
    )(x, mod, g_norm1, w_in)


def _t5_bucket(dist):
    max_exact = N_BUCKETS // 2
    is_small = dist < max_exact
    d = jnp.maximum(dist, 1).astype(F32)
    large = max_exact + (jnp.log(d / max_exact) / math.log(MAX_DISTANCE / max_exact)
                         * (N_BUCKETS - max_exact)).astype(jnp.int32)
    large = jnp.minimum(large, N_BUCKETS - 1)
    return jnp.where(is_small, dist, large)


def _bucket_table():
    qi = jnp.arange(BLOCK, dtype=jnp.int32)[:, None]
    sj = jnp.arange(2 * BLOCK, dtype=jnp.int32)[None, :]
    return _t5_bucket(jnp.maximum(qi + BLOCK - sj, 0))


def _window_mask():
    qi = lax.broadcasted_iota(jnp.int32, (BLOCK, 2 * BLOCK), 0)
    sj = lax.broadcasted_iota(jnp.int32, (BLOCK, 2 * BLOCK), 1)
    dist = qi + BLOCK - sj
    return (dist >= 0) & (dist < BLOCK)


def _fill_bias_table(rb_ref, bk_ref, o_ref):
    bk = bk_ref[...]
    inside = _window_mask()
    for h in range(N_Q_HEADS):
        acc = jnp.zeros((BLOCK, 2 * BLOCK), F32)
        for b in range(N_BUCKETS):
            acc = jnp.where(bk == b, rb_ref[b, h], acc)
        o_ref[h] = jnp.where(inside, acc, NEG_INF)


def _load_kv_window(kv_ref, n):
    prev = jnp.maximum(n - 1, 0)
    kvw = jnp.concatenate([kv_ref[pl.ds(pl.multiple_of(prev * BLOCK, BLOCK), BLOCK), :],
                           kv_ref[pl.ds(pl.multiple_of(n * BLOCK, BLOCK), BLOCK), :]], axis=0)
    k, v = kvw[:, 0:128], kvw[:, 128:256]
    k_sw = pltpu.roll(k.astype(F32), 64, 1).astype(BF16)
    v_sw = pltpu.roll(v.astype(F32), 64, 1).astype(BF16)
    return (k, k_sw), (v, v_sw)


def _conv_taps(gc, xc, gc_prev, xc_prev, n):
    u = gc * xc
    before = jnp.where(n > 0, gc_prev.astype(F32) * xc_prev.astype(F32), 0.0)
    last = before.shape[0] - 1
    row = lax.broadcasted_iota(jnp.int32, u.shape, 0)
    u1 = jnp.where(row == 0, before[last:last + 1, :], pltpu.roll(u, 1, 0))
    u2 = jnp.where(row == 0, before[last - 1:last, :],
                   jnp.where(row == 1, before[last:last + 1, :], pltpu.roll(u, 2, 0)))
    return u, u1, u2


def _mixer_fwd(q, kv, gb, gc, xc, bias, sinks, conv_w, g_attn, g_conv):
    s = q.shape[0]
    nb = s // BLOCK

    per_step = min(MIXER_BLOCKS, nb)
    tile = per_step * BLOCK

    def one_block(n, rows, before, sink_ref, q_ref, kv_ref, gb_ref, gc_ref, xc_ref, bias_ref, cw_ref, ga_ref,
                  gcv_ref, attn_ref, merged_ref, lse_ref):
        ks, vs = _load_kv_window(kv_ref, n)
        lane = lax.broadcasted_iota(jnp.int32, (BLOCK, BLOCK), 1)
        low = lane < HEAD_DIM
        col = lax.broadcasted_iota(jnp.int32, (BLOCK, 2 * BLOCK), 1)
        no_prev = (col < BLOCK) & (n == 0)
        lse_all = jnp.zeros((BLOCK, BLOCK), F32)
        pairs = []
        for p in range(4):
            qp = q_ref[rows, 128 * p:128 * (p + 1)].astype(F32)
            kvh = p // 2
            res = []
            for e in range(2):
                h = 2 * p + e
                qm = jnp.where(low if e == 0 else ~low, qp, 0.0).astype(BF16)
                sw = 0 if kvh == e else 1
                sc = _dot_nt(qm, ks[sw]) * SCALE + bias_ref[h]
                sc = jnp.where(no_prev, NEG_INF, sc)
                sink = sink_ref[h]
                m = jnp.maximum(jnp.max(sc, axis=-1, keepdims=True), sink)
                pe = jnp.exp(sc - m)
                den = jnp.sum(pe, axis=-1, keepdims=True) + jnp.exp(sink - m)
                res.append(_dot(pe.astype(BF16), vs[sw]) / den)
                lse_all = lse_all + jnp.where(lane == h, m + jnp.log(den), 0.0)
            pairs.append(jnp.where(low, res[0], res[1]))
        attn = jnp.concatenate(pairs, axis=1)
        attn_ref[rows, :] = attn
        lse_ref[rows, :] = lse_all
        u, u1, u2 = _conv_taps(gc_ref[rows, :].astype(F32), xc_ref[rows, :].astype(F32), before[0], before[1], n)
        cw = cw_ref[...]
        cv = gb_ref[rows, :].astype(F32) * (cw[0:1, :] * u2 + cw[1:2, :] * u1 + cw[2:3, :] * u)
        an = attn * _rsqrt_mean_sq(attn) * ga_ref[...]
        cn = cv * _rsqrt_mean_sq(cv) * gcv_ref[...]
        merged_ref[rows, :] = jnp.concatenate([an, cn], axis=1).astype(BF16)

    def body(sink_ref, q_ref, kv_ref, gb_ref, gc_ref, xc_ref, gcp_ref, xcp_ref, *rest):
        step = pl.program_id(0)
        for sub in range(per_step):
            rows = slice(sub * BLOCK, (sub + 1) * BLOCK)
            ahead = slice(sub * BLOCK - PREV_ROWS, sub * BLOCK)
            before = (gcp_ref[...], xcp_ref[...]) if sub == 0 else (gc_ref[ahead, :], xc_ref[ahead, :])
            one_block(step * per_step + sub, rows, before, sink_ref, q_ref, kv_ref, gb_ref, gc_ref, xc_ref, *rest)

    blk = lambda w: pl.BlockSpec((tile, w), lambda n: (n, 0))
    prev8 = pl.BlockSpec((PREV_ROWS, 512), lambda n: (jnp.maximum(n * (tile // PREV_ROWS) - 1, 0), 0))
    return pl.pallas_call(
        body, name="mixer_fwd", grid=(nb // per_step,),
        in_specs=[pl.BlockSpec(memory_space=pltpu.SMEM), blk(512), _full((s, 256)), blk(512), blk(512), blk(512),
                  prev8, prev8, _full((N_Q_HEADS, BLOCK, 2 * BLOCK)), _full((3, 512)), _full((1, 512)),
                  _full((1, 512))],
        out_specs=[blk(512), blk(1024), blk(128)],
        out_shape=[jax.ShapeDtypeStruct((s, 512), F32), jax.ShapeDtypeStruct((s, 1024), BF16),
                   jax.ShapeDtypeStruct((s, 128), F32)],
        compiler_params=_params(("arbitrary",)),
    )(sinks, q, kv, gb, gc, xc, gc, xc, bias, conv_w, g_attn, g_conv)


def _out_proj(merged, x, mod, w_out, tm):
    s = x.shape[0]

    def body(m_ref, x_ref, mod_ref, w_ref, o_ref, x1_ref):
        o = _dot(m_ref[...], w_ref[...])
        o_ref[...] = o.astype(BF16)
        x1_ref[...] = x_ref[...] + mod_ref[G1:G1 + 1, :] * o

    return pl.pallas_call(
        body, name="out_proj", grid=(s // tm,),
        in_specs=[_rows(tm, D_MODEL), _rows(tm, D_MODEL), _full((8, D_MODEL)), _full((D_MODEL, D_MODEL))],
        out_specs=[_rows(tm, D_MODEL), _rows(tm, D_MODEL)],
        out_shape=[jax.ShapeDtypeStruct((s, D_MODEL), BF16), jax.ShapeDtypeStruct((s, D_MODEL), F32)],
        compiler_params=_params(("arbitrary",)),
    )(merged, x, mod, w_out)


def _resident(shape):
    nd = len(shape)
    return pl.BlockSpec(shape, lambda *_: (0,) * nd, pipeline_mode=pl.Buffered(1))


def _ffn(x1, o1, mod, g_norm2, w_gu, w_down, w_out, g_final, target, tm):
    s = x1.shape[0]
    chunk = D_FF // FFN_CHUNKS

    def body(x_ref, o1_ref, mod_ref, g_ref, wgu_ref, wd_ref, wo_ref, gf_ref, t_ref,
             h_ref, act_ref, do_ref, dgu_ref, dx1_ref, do1_ref, dm_ref, small_ref):
        @pl.when(pl.program_id(0) == 0)
        def _():
            small_ref[...] = jnp.zeros_like(small_ref)

        xf = x_ref[...]
        n = xf * _rsqrt_mean_sq(xf) * g_ref[...]
        h = (n * (1.0 + mod_ref[SC2:SC2 + 1, :]) + mod_ref[SH2:SH2 + 1, :]).astype(BF16)
        h_ref[...] = h
        gates, ups, o = [], [], None
        for j in range(FFN_CHUNKS):
            lo = j * chunk
            gate = _dot_nt(h, wgu_ref[lo:lo + chunk, :])
            up = _dot_nt(h, wgu_ref[D_FF + lo:D_FF + lo + chunk, :])
            sg = _sigmoid(gate)
            act = (gate * sg * up).astype(BF16)
            act_ref[:, lo:lo + chunk] = act
            gates.append((up * (sg * (1.0 + gate * (1.0 - sg)))).astype(BF16))
            ups.append((gate * sg).astype(BF16))
            part = _dot(act, wd_ref[lo:lo + chunk, :])
            o = part if o is None else o + part
        g2 = mod_ref[G2:G2 + 1, :]
        x2 = xf + g2 * o
        r = _rsqrt_mean_sq(x2)
        xn = x2 * r
        gf = gf_ref[...]
        err = xn * gf - t_ref[...]
        dy = err * (1.0 / D_MODEL)
        dxn = dy * gf
        dx2 = r * (dxn - xn * jnp.mean(dxn * xn, axis=-1, keepdims=True))
        small_ref[4:5, :] += _colsum(dy * xn)
        small_ref[5:6, :] += _colsum(err * err)
        small_ref[3:4, :] += _colsum(dx2 * o)
        do = (dx2 * g2).astype(BF16)
        do_ref[...] = do
        dh = None
        for j in range(FFN_CHUNKS):
            lo = j * chunk
            dact = _dot_nt(do, wd_ref[lo:lo + chunk, :])
            dgate = (dact * gates[j].astype(F32)).astype(BF16)
            dup = (dact * ups[j].astype(F32)).astype(BF16)
            dgu_ref[:, lo:lo + chunk] = dgate
            dgu_ref[:, D_FF + lo:D_FF + lo + chunk] = dup
            part = _dot(dgate, wgu_ref[lo:lo + chunk, :]) + _dot(dup, wgu_ref[D_FF + lo:D_FF + lo + chunk, :])
            dh = part if dh is None else dh + part
        dx1 = dx2 + _norm_mod_bwd(dh, xf, g_ref[...], mod_ref[SC2:SC2 + 1, :], small_ref)
        dx1_ref[...] = dx1.astype(BF16)
        small_ref[7:8, :] += _colsum(dx1 * o1_ref[...].astype(F32))
        do1 = (dx1 * mod_ref[G1:G1 + 1, :]).astype(BF16)
        do1_ref[...] = do1
        dm_ref[...] = _dot_nt(do1, wo_ref[...]).astype(BF16)

        @pl.when(pl.program_id(0) == pl.num_programs(0) - 1)
        def _():
            total = jnp.sum(small_ref[5:6, :], axis=-1, keepdims=True) * (0.5 / D_MODEL)
            small_ref[6:7, :] = jnp.broadcast_to(total, (1, D_MODEL))

    narrow = jax.ShapeDtypeStruct((s, D_MODEL), BF16)
    return pl.pallas_call(
        body, name="ffn", grid=(s // tm,),
        in_specs=[_rows(tm, D_MODEL), _rows(tm, D_MODEL), _full((8, D_MODEL)), _full((1, D_MODEL)),
                  _resident((2 * D_FF, D_MODEL)), _resident((D_FF, D_MODEL)), _resident((D_MODEL, D_MODEL)),
                  _full((1, D_MODEL)), _rows(tm, D_MODEL)],
        out_specs=[_rows(tm, D_MODEL), _rows(tm, D_FF), _rows(tm, D_MODEL), _rows(tm, 2 * D_FF), _rows(tm, D_MODEL),
                   _rows(tm, D_MODEL), _rows(tm, D_MODEL), _full((8, D_MODEL))],
        out_shape=[narrow, jax.ShapeDtypeStruct((s, D_FF), BF16), narrow, jax.ShapeDtypeStruct((s, 2 * D_FF), BF16),
                   narrow, narrow, narrow, jax.ShapeDtypeStruct((8, D_MODEL), F32)],
        compiler_params=_params(("arbitrary",), VMEM_LIMIT_LARGE),
    )(x1, o1, mod, g_norm2, w_gu, w_down, w_out, g_final, target)


def _norm_mod_bwd(dh, xf, g, scale_row, small_ref):
    r = _rsqrt_mean_sq(xf)
    xn = xf * r
    small_ref[0:1, :] += _colsum(dh)
    small_ref[1:2, :] += _colsum(dh * (xn * g))
    dn = dh * (1.0 + scale_row)
    small_ref[2:3, :] += _colsum(dn * xn)
    dxn = dn * g
    return r * (dxn - xn * jnp.mean(dxn * xn, axis=-1, keepdims=True))


def _group_norm_bwd(dm, a, g):
    r = _rsqrt_mean_sq(a)
    an = a * r
    dan = dm * g
    return r * (dan - an * jnp.mean(dan * an, axis=-1, keepdims=True)), _colsum(dm * an)


def _mixer_bwd(after, q, kv, gb, gc, xc, bias, sinks, conv_w, g_attn, g_conv, attn, lse, dmerged):
    s = q.shape[0]
    nb = s // BLOCK

    per_step = min(MIXER_BLOCKS, nb)
    tile = per_step * BLOCK
    steps = nb // per_step

    def one_block(n, rows, before, nxt, sink_ref, q_ref, kv_ref, gb_ref, gc_ref, xc_ref, bias_ref, cw_ref, ga_ref,
                  gcv_ref, attn_ref, lse_ref, dm_ref, dproj_ref, dbias_ref, dsink_ref, small_ref):
        next_dy, next_dkv = nxt
        dm = dm_ref[rows, :].astype(F32)
        gbv, gcv_, xcv = gb_ref[rows, :].astype(F32), gc_ref[rows, :].astype(F32), xc_ref[rows, :].astype(F32)
        u, u1, u2 = _conv_taps(gcv_, xcv, before[0], before[1], n)
        cw = cw_ref[...]
        yv = cw[0:1, :] * u2 + cw[1:2, :] * u1 + cw[2:3, :] * u
        dcv, dg_conv = _group_norm_bwd(dm[:, 512:1024], gbv * yv, gcv_ref[...])
        small_ref[1:2, :] += dg_conv
        dproj_ref[rows, 768:1280] = (dcv * yv).astype(BF16)
        dy = dcv * gbv
        row = lax.broadcasted_iota(jnp.int32, dy.shape, 0)
        d1 = jnp.where(row == BLOCK - 1, next_dy[0:1, :], pltpu.roll(dy, BLOCK - 1, 0))
        d2 = jnp.where(row == BLOCK - 2, next_dy[0:1, :],
                       jnp.where(row == BLOCK - 1, next_dy[1:2, :], pltpu.roll(dy, BLOCK - 2, 0)))
        du = cw[2:3, :] * dy + cw[1:2, :] * d1 + cw[0:1, :] * d2
        dproj_ref[rows, 1280:1792] = (du * xcv).astype(BF16)
        dproj_ref[rows, 1792:2304] = (du * gcv_).astype(BF16)
        small_ref[2:3, :] += _colsum(dy * u2)
        small_ref[3:4, :] += _colsum(dy * u1)
        small_ref[4:5, :] += _colsum(dy * u)

        attn_v = attn_ref[rows, :]
        dout, dg_attn = _group_norm_bwd(dm[:, 0:512], attn_v, ga_ref[...])
        small_ref[0:1, :] += dg_attn
        ks, vs = _load_kv_window(kv_ref, n)
        lane = lax.broadcasted_iota(jnp.int32, (BLOCK, BLOCK), 1)
        low = lane < HEAD_DIM
        col = lax.broadcasted_iota(jnp.int32, (BLOCK, 2 * BLOCK), 1)
        no_prev = (col < BLOCK) & (n == 0)
        lse_all = lse_ref[rows, :]
        dsink = jnp.zeros((BLOCK, BLOCK), F32)
        dq_pairs = []
        dk_groups, dv_groups = [], []
        for kvh in range(2):
            ds_rows, pr_rows, q_rows, do_rows = [], [], [], []
            for p in (2 * kvh, 2 * kvh + 1):
                qp = q_ref[rows, 128 * p:128 * (p + 1)].astype(F32)
                do_p = dout[:, 128 * p:128 * (p + 1)]
                prod = do_p * attn_v[:, 128 * p:128 * (p + 1)]
                res = []
                for e in range(2):
                    h = 2 * p + e
                    half = low if e == 0 else ~low
                    qm = jnp.where(half, qp, 0.0).astype(BF16)
                    dom = jnp.where(half, do_p, 0.0).astype(BF16)
                    delta = jnp.sum(jnp.where(half, prod, 0.0), axis=-1, keepdims=True)
                    lse_h = jnp.sum(jnp.where(lane == h, lse_all, 0.0), axis=-1, keepdims=True)
                    sw = 0 if kvh == e else 1
                    sc = _dot_nt(qm, ks[sw]) * SCALE + bias_ref[h]
                    sc = jnp.where(no_prev, NEG_INF, sc)
                    pr = jnp.exp(sc - lse_h)
                    dp = _dot_nt(dom, vs[sw])
                    ds = pr * (dp - delta)
                    dbias_ref[h] += ds
                    dsink = dsink + jnp.where(lane == h, -jnp.exp(sink_ref[h] - lse_h) * delta, 0.0)
                    dsb = ds.astype(BF16)
                    res.append(_dot(dsb, ks[sw]) * SCALE)
                    ds_rows.append(dsb)
                    pr_rows.append(pr.astype(BF16))
                    q_rows.append(qm)
                    do_rows.append(dom)
                dq_pairs.append(jnp.where(low, res[0], res[1]))
            dk_g = _dot_tn(jnp.concatenate(ds_rows, axis=0), jnp.concatenate(q_rows, axis=0)) * SCALE
            dv_g = _dot_tn(jnp.concatenate(pr_rows, axis=0), jnp.concatenate(do_rows, axis=0))
            dk_groups.append(dk_g + pltpu.roll(dk_g, 64, 1))
            dv_groups.append(dv_g + pltpu.roll(dv_g, 64, 1))
        dproj_ref[rows, 0:512] = jnp.concatenate(dq_pairs, axis=1).astype(BF16)
        dsink_ref[...] += dsink
        low_kv = lax.broadcasted_iota(jnp.int32, (2 * BLOCK, BLOCK), 1) < HEAD_DIM
        dkv_win = jnp.concatenate([jnp.where(low_kv, dk_groups[0], dk_groups[1]),
                                   jnp.where(low_kv, dv_groups[0], dv_groups[1])], axis=1)
        dproj_ref[rows, 512:768] = (dkv_win[BLOCK:2 * BLOCK, :] + next_dkv).astype(BF16)
        return dy[0:8, :], dkv_win[0:BLOCK, :]

    def body(sink_ref, q_ref, kv_ref, gb_ref, gc_ref, xc_ref, gcp_ref, xcp_ref, *rest):
        refs, dy_ref, dkv_ref = rest[:-2], rest[-2], rest[-1]
        dbias_ref, dsink_ref, small_ref = refs[8], refs[9], refs[10]
        step = pl.program_id(0)

        @pl.when(step == 0)
        def _():
            dbias_ref[...] = jnp.zeros_like(dbias_ref)
            dsink_ref[...] = jnp.zeros_like(dsink_ref)
            small_ref[...] = jnp.zeros_like(small_ref)
            dy_ref[...] = jnp.zeros_like(dy_ref)
            dkv_ref[...] = jnp.zeros_like(dkv_ref)

        nxt = (dy_ref[...], dkv_ref[...])
        for sub in reversed(range(per_step)):
            rows = slice(sub * BLOCK, (sub + 1) * BLOCK)
            ahead = slice(sub * BLOCK - PREV_ROWS, sub * BLOCK)
            before = (gcp_ref[...], xcp_ref[...]) if sub == 0 else (gc_ref[ahead, :], xc_ref[ahead, :])
            nxt = one_block((steps - 1 - step) * per_step + sub, rows, before, nxt,
                            sink_ref, q_ref, kv_ref, gb_ref, gc_ref, xc_ref, *refs)
        dy_ref[...], dkv_ref[...] = nxt

        @pl.when(step == steps - 1)
        def _():
            small_ref[5:6, :] = jnp.concatenate([_colsum(dsink_ref[...]), jnp.zeros((1, 512 - BLOCK), F32)], axis=1)

    blk = lambda w: pl.BlockSpec((tile, w), lambda t: (steps - 1 - t, 0))
    prev8 = pl.BlockSpec((PREV_ROWS, 512),
                         lambda t: (jnp.maximum((steps - 1 - t) * (tile // PREV_ROWS) - 1, 0), 0))
    bf = lambda w: jax.ShapeDtypeStruct((s, w), BF16)
    return pl.pallas_call(
        _coming_behind(body), name="mixer_bwd", grid=(steps,),
        in_specs=[ANY_SPEC, pl.BlockSpec(memory_space=pltpu.SMEM), blk(512), _full((s, 256)), blk(512), blk(512), blk(512),
                  prev8, prev8, _full((N_Q_HEADS, BLOCK, 2 * BLOCK)), _full((3, 512)), _full((1, 512)),
                  _full((1, 512)), blk(512), blk(128), blk(1024)],
        out_specs=[blk(IN_PROJ_WIDTH), _full((N_Q_HEADS, BLOCK, 2 * BLOCK)), _full((BLOCK, BLOCK)), _full((8, 512))],
        out_shape=[bf(IN_PROJ_WIDTH), jax.ShapeDtypeStruct((N_Q_HEADS, BLOCK, 2 * BLOCK), F32),
                   jax.ShapeDtypeStruct((BLOCK, BLOCK), F32), jax.ShapeDtypeStruct((8, 512), F32)],
        scratch_shapes=[pltpu.VMEM((8, 512), F32), pltpu.VMEM((BLOCK, 2 * KV_WIDTH), F32)],
        compiler_params=_params(("arbitrary",), VMEM_LIMIT_LARGE),
    )(after, sinks, q, kv, gb, gc, xc, gc, xc, bias, conv_w, g_attn, g_conv, attn, lse, dmerged)


def _in_proj_bwd(after, dproj, x, dx1, mod, g_norm1, w_in, tm):
    s = x.shape[0]

    def body(dproj_ref, x_ref, dx1_ref, mod_ref, g_ref, w_ref, dx_ref, small_ref):
        @pl.when(pl.program_id(0) == 0)
        def _():
            small_ref[...] = jnp.zeros_like(small_ref)

        dh = _dot(dproj_ref[...], w_ref[...])
        dx_ref[...] = dx1_ref[...].astype(F32) + _norm_mod_bwd(dh, x_ref[...], g_ref[...], mod_ref[SC1:SC1 + 1, :],
                                                               small_ref)

    return pl.pallas_call(
        _coming_behind(body), name="in_proj_bwd", grid=(s // tm,),
        in_specs=[ANY_SPEC, _rows(tm, IN_PROJ_WIDTH), _rows(tm, D_MODEL), _rows(tm, D_MODEL), _full((8, D_MODEL)),
                  _full((1, D_MODEL)), _full((IN_PROJ_WIDTH, D_MODEL))],
        out_specs=[_rows(tm, D_MODEL), _full((8, D_MODEL))],
        out_shape=[jax.ShapeDtypeStruct((s, D_MODEL), F32), jax.ShapeDtypeStruct((8, D_MODEL), F32)],
        compiler_params=_params(("arbitrary",), VMEM_LIMIT_LARGE),
    )(after, dproj, x, dx1, mod, g_norm1, w_in)


def _weight_grad(a, b, tk, ts, name, after=None):
    s, k = a.shape
    n = b.shape[1]
    nt = s // ts
    extra = [] if after is None else [after]

    def body(a_ref, b_ref, *rest):
        o_ref, acc_ref = rest[-2:]
        t = pl.program_id(1)
        @pl.when(t == 0)
        def _():
            acc_ref[...] = jnp.zeros_like(acc_ref)

        acc = acc_ref[...] + _dot_tn(a_ref[...], b_ref[...])
        acc_ref[...] = acc
        o_ref[...] = acc.astype(BF16)

    return pl.pallas_call(
        body, name=name, grid=(k // tk, nt),
        in_specs=[pl.BlockSpec((ts, tk), lambda i, t: (t, i)), pl.BlockSpec((ts, n), lambda i, t: (t, 0))]
        + [ANY_SPEC] * len(extra),
        out_specs=pl.BlockSpec((tk, n), lambda i, t: (i, 0)),
        out_shape=jax.ShapeDtypeStruct((k, n), BF16),
        scratch_shapes=[pltpu.VMEM((tk, n), F32)],
        compiler_params=_params(("arbitrary", "arbitrary"), VMEM_LIMIT_LARGE),
    )(a, b, *extra)


def _rel_bias_grad(dbias, bucket):
    def body(db_ref, bk_ref, o_ref, rows_ref):
        bk = bk_ref[...]
        for b in range(N_BUCKETS):
            sel = (bk == b).astype(F32)
            for h in range(N_Q_HEADS):
                rows_ref[N_BUCKETS * h + b:N_BUCKETS * h + b + 1, :] = _colsum(db_ref[h] * sel)
        head = lax.broadcasted_iota(jnp.int32, (N_BUCKETS, N_Q_HEADS), 1)
        out = jnp.zeros((N_BUCKETS, N_Q_HEADS), F32)
        for h in range(N_Q_HEADS):
            per_bucket = jnp.sum(rows_ref[N_BUCKETS * h:N_BUCKETS * (h + 1), :], axis=-1, keepdims=True)
            out = out + jnp.where(head == h, per_bucket, 0.0)
        o_ref[...] = out

    return pl.pallas_call(
        body, name="rel_bias_grad",
        out_shape=jax.ShapeDtypeStruct((N_BUCKETS, N_Q_HEADS), F32),
        scratch_shapes=[pltpu.VMEM((N_BUCKETS * N_Q_HEADS, 2 * BLOCK), F32)],
    )(dbias, bucket)


def _lanes_from(x, start, width):
    n = x.shape[1]
    return pltpu.roll(x, (n - start) % n, 1)[:, 0:width]


def _adamw_w_ada(me, cond_all, packed_all, w, m, v, tr):
    r, cols = w.shape

    def body(me_ref, c_ref, p_ref, w_ref, m_ref, v_ref, g_ref, d_ref, mo_ref, vo_ref):
        dmod = jnp.concatenate([p_ref[k][:, OFF_DMOD:OFF_DMOD + N_MOD * D_MODEL] for k in range(N_DEV)], axis=0)
        mine = _lanes_from(dmod, me_ref[0] * cols, cols)
        pad = lambda a: jnp.concatenate([a, jnp.zeros((128 - N_DEV, a.shape[1]), F32)], axis=0)
        g = _dot_tn(pad(c_ref[...]), pad(mine))
        g_ref[...] = g
        d_ref[...], mo_ref[...], vo_ref[...] = _adam_math(w_ref[...], g, m_ref[...], v_ref[...])

    tile = pl.BlockSpec((tr, cols), lambda i, me_ref: (i, 0))
    return pl.pallas_call(
        body, name="adamw_w_ada",
        grid_spec=pltpu.PrefetchScalarGridSpec(
            num_scalar_prefetch=1, grid=(r // tr,),
            in_specs=[pl.BlockSpec((N_DEV, tr), lambda i, me_ref: (0, i)),
                      pl.BlockSpec(packed_all.shape, lambda i, me_ref: (0, 0, 0)), tile, tile, tile],
            out_specs=[tile] * 4),
        out_shape=[jax.ShapeDtypeStruct((r, cols), F32)] * 4,
        compiler_params=_params(("arbitrary",)),
    )(me, cond_all, packed_all, w, m, v)


SMALL_PARAMS = (("rel_bias", None), ("b_ada", (OFF_DMOD, N_MOD * D_MODEL)), ("g_norm1", (OFF_GN1, D_MODEL)),
                ("sinks", (OFF_SINK, N_Q_HEADS)), ("conv_w", None), ("g_attn_out", (OFF_GATT, ATTN_WIDTH)),
                ("g_conv_out", (OFF_GCV, CONV_WIDTH)), ("g_norm2", (OFF_GN2, D_MODEL)),
                ("g_final", (OFF_GFIN, D_MODEL)))


def _small_update(me, packed_all, rel_all, state, after):
    n_p = len(SMALL_PARAMS)
    flat = [a for triple in state for a in triple]
    conv_cols = state[4][0].shape[1]

    def body(me_ref, p_ref, r_ref, *refs):
        ins = refs[:3 * n_p]
        loss_ref, outs = refs[3 * n_p + len(after)], refs[3 * n_p + len(after) + 1:]
        small, rel = p_ref[0], r_ref[0]
        for k in range(1, N_DEV):
            small = small + p_ref[k]
            rel = rel + r_ref[k]
        loss_ref[...] = small[:, OFF_LOSS:OFF_LOSS + 128]
        taps = jnp.concatenate([small[:, OFF_CONVW + CONV_WIDTH * j:OFF_CONVW + CONV_WIDTH * (j + 1)]
                                for j in range(3)] + [jnp.zeros((5, CONV_WIDTH), F32)], axis=0)
        conv_g = _lanes_from(taps, me_ref[0] * conv_cols, conv_cols)[0:3, :]
        for i, (name, lanes) in enumerate(SMALL_PARAMS):
            g = rel if name == "rel_bias" else conv_g if name == "conv_w" else small[:, lanes[0]:lanes[0] + lanes[1]]
            w_ref, m_ref, v_ref = ins[3 * i:3 * i + 3]
            outs[4 * i][...] = g
            outs[4 * i + 1][...], outs[4 * i + 2][...], outs[4 * i + 3][...] = _adam_math(
                w_ref[...], g, m_ref[...], v_ref[...])

    vmem = pl.BlockSpec(memory_space=pltpu.VMEM)
    out_shape = [jax.ShapeDtypeStruct((1, 128), F32)]
    for w, _, _ in state:
        out_shape += [jax.ShapeDtypeStruct(w.shape, F32)] * 4
    outs = pl.pallas_call(
        body, name="small_update",
        in_specs=[pl.BlockSpec(memory_space=pltpu.SMEM), vmem, vmem] + [vmem] * len(flat)
        + [pl.BlockSpec(memory_space=pl.ANY)] * len(after),
        out_shape=out_shape,
    )(me, packed_all, rel_all, *flat, *after)
    return outs[0], [tuple(outs[1 + 4 * i:5 + 4 * i]) for i in range(n_p)]


def _adam_math(w, g, m, v):
    m = ADAM_B1 * m + (1.0 - ADAM_B1) * g
    v = ADAM_B2 * v + (1.0 - ADAM_B2) * (g * g)
    m_hat = m / (1.0 - ADAM_B1 ** ADAM_STEP)
    v_hat = v / (1.0 - ADAM_B2 ** ADAM_STEP)
    delta = -ADAM_LR * (m_hat / (jnp.sqrt(v_hat) + ADAM_EPS) + ADAM_WD * w)
    return delta, m, v


def _adamw_parts(w, m, v, local, land, me, tr, name):
    r, c = w.shape

    def body(me_ref, w_ref, m_ref, v_ref, own_ref, land_ref, g_ref, d_ref, mo_ref, vo_ref):
        g = own_ref[0].astype(F32)
        for k in range(N_DEV - 1):
            g = g + land_ref[k].astype(F32)
        g_ref[...] = g
        d_ref[...], mo_ref[...], vo_ref[...] = _adam_math(w_ref[...], g, m_ref[...], v_ref[...])

    tile = pl.BlockSpec((tr, c), lambda i, me_ref: (i, 0))
    return pl.pallas_call(
        body, name=name,
        grid_spec=pltpu.PrefetchScalarGridSpec(
            num_scalar_prefetch=1, grid=(r // tr,),
            in_specs=[tile, tile, tile, pl.BlockSpec((1, tr, c), lambda i, me_ref: (me_ref[0], i, 0)),
                      pl.BlockSpec((N_DEV - 1, tr, c), lambda i, me_ref: (0, i, 0))],
            out_specs=[tile] * 4),
        out_shape=[jax.ShapeDtypeStruct((r, c), F32)] * 4,
        compiler_params=_params(("arbitrary",)),
    )(me, w, m, v, local, land)


def _behind(a, token):
    return a + token[0:a.shape[0], 0:1]


def _local_step(x, target, mod, w_in_t, bias, weights_out_gu, weights_down, g_norm1, sinks, conv_w, g_attn,
                g_conv, g_norm2, g_final, exchange):
    s = x.shape[0]
    tm = min(512, s)
    tm_small = min(256, s)
    bucket = _bucket_table()

    h, q, kv, gb, gc, xc = _in_proj(x, mod, g_norm1, w_in_t, tm)
    attn, merged, lse = _mixer_fwd(q, kv, gb, gc, xc, bias, sinks, conv_w, g_attn, g_conv)
    w_out, w_gu_t = weights_out_gu(merged)
    o1, x1 = _out_proj(merged, x, mod, w_out, tm)
    w_down = weights_down(x1)
    h2, act, do2, dgu, dx1, do1, dmerged, sm_2 = _ffn(x1, o1, mod, g_norm2, w_gu_t, w_down, w_out, g_final, target,
                                                      tm_small)
    ts = min(WEIGHT_GRAD_ROWS, s)
    tok_down = exchange("w_down", _weight_grad(act, do2, D_FF // 2, ts, "w_down_grad"))
    tok_gu = exchange("w_gu", _weight_grad(dgu, h2, D_FF // 2, ts, "w_gu_grad", after=tok_down))
    tok_out = exchange("w_out", _weight_grad(merged, do1, D_MODEL, ts, "w_out_grad", after=tok_gu))
    dproj, dbias, dsink, sm_mix = _mixer_bwd(
        tok_out, q, kv, gb, gc, xc, bias, sinks, conv_w, g_attn, g_conv, attn, lse, dmerged)
    tok_in = exchange("w_in", _weight_grad(dproj, h, IN_PROJ_WIDTH // 2, ts, "w_in_grad"))
    dx, sm_1 = _in_proj_bwd(tok_in, dproj, x, dx1, mod, g_norm1, w_in_t, min(1024, s))
    d_rel = _rel_bias_grad(dbias, bucket)

    packed = jnp.concatenate([
        sm_1[0], sm_1[1], sm_2[7], sm_2[0], sm_2[1], sm_2[3],
        sm_1[2],
        sm_mix[5, 0:128],
        sm_mix[0], sm_mix[1],
        sm_2[2],
        sm_2[4],
        sm_mix[2], sm_mix[3], sm_mix[4],
        sm_2[6, 0:128],
    ])[None, :]
    return dx, packed, d_rel


def kernel(x, c, rel_bias, w_ada, b_ada, g_norm1, w_in, sinks, conv_w, g_attn_out, g_conv_out, w_out, g_norm2, w_gu, w_down, g_final, loss_target, m_rel_bias, m_w_ada, m_b_ada, m_g_norm1, m_w_in, m_sinks, m_conv_w, m_g_attn_out, m_g_conv_out, m_w_out, m_g_norm2, m_w_gu, m_w_down, m_g_final, v_rel_bias, v_w_ada, v_b_ada, v_g_norm1, v_w_in, v_sinks, v_conv_w, v_g_attn_out, v_g_conv_out, v_w_out, v_g_norm2, v_w_gu, v_w_down, v_g_final):
    me = _linear(_mesh_position())
    me_arr = jnp.reshape(me, (1,)).astype(jnp.int32)
    ada_cols = w_ada.shape[2]
    tm = min(512, x.shape[1])

    b_cols = lax.dynamic_slice_in_dim(b_ada, me * ada_cols, ada_cols, axis=1)
    cond_all, conv_w_all, mod_all, w_in_blocks, staged, bias = _open_step(
        c, conv_w[0], w_ada[0], b_cols, w_in[0].T, [w_out[0], w_gu[0].T, w_down[0]], rel_bias, _bucket_table())
    cond_all = cond_all[:, 0, :]
    conv_w_full = conv_w_all.transpose(1, 0, 2).reshape(3, CONV_WIDTH)
    mod = lax.dynamic_index_in_dim(mod_all, me, axis=1, keepdims=False).reshape(N_MOD, D_MODEL)
    mod = jnp.concatenate([mod, jnp.zeros((2, D_MODEL), F32)], axis=0)
    w_in_t = w_in_blocks.reshape(IN_PROJ_WIDTH, D_MODEL)
    gather_sems, staged, gather_token = _gather_start(staged, "gather_start_weights")
    mod = _behind(mod, gather_token)

    def weights_out_gu(after):
        got = _gather_pass_on(_gather_wait(gather_sems[0:4], staged[0:2], [after], "gather_wait_out_gu"),
                              "gather_pass_on_out_gu")
        return got[0].reshape(D_MODEL, D_MODEL), got[1].reshape(2 * D_FF, D_MODEL)

    def weights_down(after):
        got = _gather_pass_on(_gather_wait(gather_sems[4:6], staged[2:3], [after], "gather_wait_down"),
                              "gather_pass_on_down")
        return got[0].reshape(D_FF, D_MODEL)

    started = {}

    def exchange(name, dw):
        st = _exchange_start(dw.reshape(N_DEV, dw.shape[0] // N_DEV, dw.shape[1]), "exchange_start_" + name)
        started[name] = st
        return st[4]

    dx, packed, d_rel = _local_step(
        x[0], loss_target[0], mod, w_in_t, bias, weights_out_gu, weights_down, g_norm1, sinks[0], conv_w_full,
        g_attn_out, g_conv_out, g_norm2, g_final[None, :], exchange)

    def zone(a):
        return lax.dynamic_update_slice(jnp.zeros((N_DEV,) + a.shape, F32), a[None], (me,) + (0,) * a.ndim)

    shared = _share_start([packed, d_rel], [zone(packed), zone(d_rel)], "share_small_start")

    def finish(name, after, w, m, v, tr):
        src, land = _exchange_wait(started[name], after, "exchange_wait_" + name)
        return _adamw_parts(w, m, v, src, land, me_arr, tr, "adamw_" + name)

    g_down, d_down, nm_down, nv_down = finish("w_down", [shared[2][0]], w_down[0], m_w_down[0], v_w_down[0], 176)
    g_gu, d_gu, nm_gu, nv_gu = finish("w_gu", [nv_down], w_gu[0].T, m_w_gu[0].T, v_w_gu[0].T, 352)
    g_out, d_out, nm_out, nv_out = finish("w_out", [nv_gu], w_out[0], m_w_out[0], v_w_out[0], 128)

    packed_all, rel_all = _share_wait(shared, [nv_out], "share_small_wait")
    g_ada, d_ada, nm_ada, nv_ada = _adamw_w_ada(me_arr, cond_all, packed_all, w_ada[0], m_w_ada[0], v_w_ada[0], 256)
    as_rows = {"conv_w": lambda a: a[0], "g_final": lambda a: a[None, :]}
    small_state = {
        "rel_bias": (rel_bias, m_rel_bias, v_rel_bias), "b_ada": (b_ada, m_b_ada, v_b_ada),
        "g_norm1": (g_norm1, m_g_norm1, v_g_norm1), "sinks": (sinks, m_sinks, v_sinks),
        "conv_w": (conv_w, m_conv_w, v_conv_w), "g_attn_out": (g_attn_out, m_g_attn_out, v_g_attn_out),
        "g_conv_out": (g_conv_out, m_g_conv_out, v_g_conv_out), "g_norm2": (g_norm2, m_g_norm2, v_g_norm2),
        "g_final": (g_final, m_g_final, v_g_final),
    }
    state = [tuple(as_rows.get(name, lambda a: a)(a) for a in small_state[name]) for name, _ in SMALL_PARAMS]
    loss_row, small_out = _small_update(me_arr, packed_all, rel_all, state, [])
    loss = loss_row[0, 0]
    small_res = {name: tuple(a.reshape(small_state[name][0].shape) for a in res)
                 for (name, _), res in zip(SMALL_PARAMS, small_out)}

    g_in, d_in, nm_in, nv_in = finish("w_in", [loss_row, nv_ada], w_in[0].T, m_w_in[0].T, v_w_in[0].T, 144)

    big = {
        "w_ada": (g_ada[None], d_ada[None], nm_ada[None], nv_ada[None]),
        "w_in": (g_in.T[None], d_in.T[None], nm_in.T[None], nv_in.T[None]),
        "w_out": (g_out[None], d_out[None], nm_out[None], nv_out[None]),
        "w_gu": (g_gu.T[None], d_gu.T[None], nm_gu.T[None], nv_gu.T[None]),
        "w_down": (g_down[None], d_down[None], nm_down[None], nv_down[None]),
    }
    order = ["rel_bias", "w_ada", "b_ada", "g_norm1", "w_in", "sinks", "conv_w", "g_attn_out", "g_conv_out", "w_out",
             "g_norm2", "w_gu", "w_down", "g_final"]
    results = [big[k] if k in big else small_res[k] for k in order]
    return (loss, dx[None], *[r[0] for r in results], *[r[1] for r in results], *[r[2] for r in results],
            *[r[3] for r in results])
```

```python
import math

import jax
import jax.numpy as jnp
from jax import lax
from jax.experimental import pallas as pl
from jax.experimental.pallas import tpu as pltpu

F32 = jnp.float32
BF16 = jnp.bfloat16

D_MODEL = 1024
HEAD_DIM = 64
N_Q_HEADS = 8
ATTN_WIDTH = 512
KV_WIDTH = 128
CONV_WIDTH = 512
IN_PROJ_WIDTH = 2304
D_FF = 2816
N_MOD = 6
N_BUCKETS = 32
MAX_DISTANCE = 128
BLOCK = 128
EPS = 1e-6
NEG_INF = -1e30
SCALE = HEAD_DIM ** -0.5
N_DEV = 8

ADAM_LR = 0.001
ADAM_B1 = 0.9
ADAM_B2 = 0.999
ADAM_EPS = 1e-08
ADAM_WD = 0.01
ADAM_STEP = 10

SH1, SC1, G1, SH2, SC2, G2 = range(6)

VMEM_LIMIT_LARGE = 60 * 1024 * 1024
WEIGHT_GRAD_ROWS = 2048
FFN_CHUNKS = 1
PREV_ROWS = 16
MIXER_BLOCKS = 4
MESH_ID = pl.DeviceIdType.MESH

OFF_DMOD = 0
OFF_GN1 = OFF_DMOD + N_MOD * D_MODEL
OFF_SINK = OFF_GN1 + D_MODEL
OFF_GATT = OFF_SINK + 128
OFF_GCV = OFF_GATT + ATTN_WIDTH
OFF_GN2 = OFF_GCV + CONV_WIDTH
OFF_GFIN = OFF_GN2 + D_MODEL
OFF_CONVW = OFF_GFIN + D_MODEL
OFF_LOSS = OFF_CONVW + 3 * CONV_WIDTH
PACKED = OFF_LOSS + 128


def _params(sem=None, vmem=None):
    return pltpu.CompilerParams(dimension_semantics=sem, vmem_limit_bytes=vmem)


def _coming_behind(body):
    def skipping(after_ref, *refs):
        body(*refs)

    return skipping


ANY_SPEC = pl.BlockSpec(memory_space=pl.ANY)


def _full(shape):
    nd = len(shape)
    return pl.BlockSpec(shape, lambda *_: (0,) * nd)


def _rows(tm, width):
    return pl.BlockSpec((tm, width), lambda i, *_: (i, 0))


def _sigmoid(x):
    return 1.0 / (1.0 + jnp.exp(-x))


def _rsqrt_mean_sq(x):
    return lax.rsqrt(jnp.mean(x * x, axis=-1, keepdims=True) + EPS)


def _colsum(x):
    return jnp.sum(x, axis=0, keepdims=True)


def _dot(a, b):
    return jnp.dot(a, b, preferred_element_type=F32)


def _dot_nt(a, b):
    return lax.dot_general(a, b, (((1,), (1,)), ((), ())), preferred_element_type=F32)


def _dot_tn(a, b):
    return lax.dot_general(a, b, (((0,), (0,)), ((), ())), preferred_element_type=F32)


def _mesh_position():
    return lax.axis_index("x"), lax.axis_index("y"), lax.axis_index("c")


def _linear(p):
    return 4 * p[0] + 2 * p[1] + p[2]


def _peer(k):
    x, y, c = _mesh_position()
    return (1 - x if k & 4 else x, 1 - y if k & 2 else y, 1 - c if k & 1 else c)


HBM_SPEC = pl.BlockSpec(memory_space=pltpu.HBM)
SEM_SPEC = pl.BlockSpec(memory_space=pltpu.SEMAPHORE)
DATAFLOW = pltpu.SideEffectType.DATAFLOW_SIDE_EFFECTING


def _exchange_start(src, name):
    r, c = src.shape[1:]

    def body(src_ref, land_ref, send_sems, recv_sems, src_thru, land_thru, token):
        for k in range(1, N_DEV):
            peer = _peer(k)
            pltpu.make_async_remote_copy(
                src_ref=src_ref.at[_linear(peer)], dst_ref=land_ref.at[k - 1],
                send_sem=send_sems.at[k - 1], recv_sem=recv_sems.at[k - 1],
                device_id=peer, device_id_type=MESH_ID).start()
        token[...] = jnp.zeros_like(token)

    land = lax.empty((N_DEV - 1, r, c), src.dtype)
    return pl.pallas_call(
        body, name=name,
        out_shape=(pltpu.SemaphoreType.DMA((N_DEV - 1,)), pltpu.SemaphoreType.DMA((N_DEV - 1,)),
                   pltpu.HBM(src.shape, src.dtype), pltpu.HBM(land.shape, land.dtype),
                   jax.ShapeDtypeStruct((8, 128), F32)),
        in_specs=(HBM_SPEC, HBM_SPEC),
        out_specs=(SEM_SPEC, SEM_SPEC, HBM_SPEC, HBM_SPEC, pl.BlockSpec(memory_space=pltpu.VMEM)),
        input_output_aliases={0: 2, 1: 3},
        compiler_params=pltpu.CompilerParams(has_side_effects=DATAFLOW),
    )(pltpu.with_memory_space_constraint(src, pltpu.HBM), pltpu.with_memory_space_constraint(land, pltpu.HBM))


def _exchange_wait(started, after, name):
    send_sems, recv_sems, src_thru, land_thru, _ = started

    def body(src_ref, land_ref, send_sems, recv_sems, *rest):
        for k in range(1, N_DEV):
            cp = pltpu.make_async_remote_copy(
                src_ref=src_ref.at[0], dst_ref=land_ref.at[k - 1],
                send_sem=send_sems.at[k - 1], recv_sem=recv_sems.at[k - 1],
                device_id=_peer(k), device_id_type=MESH_ID)
            cp.wait_send()
            cp.wait_recv()

    return pl.pallas_call(
        body, name=name,
        out_shape=(pltpu.HBM(src_thru.shape, src_thru.dtype), pltpu.HBM(land_thru.shape, land_thru.dtype)),
        in_specs=(HBM_SPEC, HBM_SPEC, SEM_SPEC, SEM_SPEC) + (pl.BlockSpec(memory_space=pl.ANY),) * len(after),
        out_specs=(HBM_SPEC, HBM_SPEC), input_output_aliases={0: 0, 1: 1},
        compiler_params=pltpu.CompilerParams(has_side_effects=DATAFLOW),
    )(src_thru, land_thru, send_sems, recv_sems, *after)


def _share_start(arrs, zones, name):
    n = len(arrs)

    def body(*refs):
        src_refs, zone_refs, sems = refs[:n], refs[n:2 * n], refs[2 * n:4 * n]
        me = _linear(_mesh_position())
        for a in range(n):
            for k in range(1, N_DEV):
                pltpu.make_async_remote_copy(
                    src_ref=src_refs[a], dst_ref=zone_refs[a].at[me],
                    send_sem=sems[2 * a].at[k - 1], recv_sem=sems[2 * a + 1].at[k - 1],
                    device_id=_peer(k), device_id_type=MESH_ID).start()

    outs = pl.pallas_call(
        body, name=name,
        out_shape=tuple(pltpu.SemaphoreType.DMA((N_DEV - 1,)) for _ in range(2 * n))
        + tuple(pltpu.HBM(a.shape, a.dtype) for a in arrs) + tuple(pltpu.HBM(z.shape, z.dtype) for z in zones),
        in_specs=(HBM_SPEC,) * (2 * n),
        out_specs=(SEM_SPEC,) * (2 * n) + (HBM_SPEC,) * (2 * n),
        input_output_aliases={i: 2 * n + i for i in range(2 * n)},
        compiler_params=pltpu.CompilerParams(has_side_effects=DATAFLOW),
    )(*[pltpu.with_memory_space_constraint(a, pltpu.HBM) for a in list(arrs) + list(zones)])
    return outs[:2 * n], outs[2 * n:3 * n], outs[3 * n:]


def _share_wait(started, after, name):
    sems, arrs, zones = started
    n = len(arrs)

    def body(*refs):
        src_refs, zone_refs, sem_refs = refs[:n], refs[n:2 * n], refs[2 * n:4 * n]
        for a in range(n):
            for k in range(1, N_DEV):
                cp = pltpu.make_async_remote_copy(
                    src_ref=src_refs[a], dst_ref=zone_refs[a].at[_linear(_peer(k))],
                    send_sem=sem_refs[2 * a].at[k - 1], recv_sem=sem_refs[2 * a + 1].at[k - 1],
                    device_id=_peer(k), device_id_type=MESH_ID)
                cp.wait_send()
                cp.wait_recv()

    outs = pl.pallas_call(
        body, name=name,
        out_shape=tuple(pltpu.HBM(a.shape, a.dtype) for a in arrs) + tuple(pltpu.HBM(z.shape, z.dtype) for z in zones),
        in_specs=(HBM_SPEC,) * (2 * n) + (SEM_SPEC,) * (2 * n) + (pl.BlockSpec(memory_space=pl.ANY),) * len(after),
        out_specs=(HBM_SPEC,) * (2 * n), input_output_aliases={i: i for i in range(2 * n)},
        compiler_params=pltpu.CompilerParams(has_side_effects=DATAFLOW),
    )(*arrs, *zones, *sems, *after)
    return list(outs[n:])


def _same_core_peers():
    x, y, c = _mesh_position()
    return [(x, y, 1 - c), (1 - x, y, c), (x, 1 - y, c), (1 - x, 1 - y, c)]


def _gather_start(bufs, name):
    n = len(bufs)

    def body(*refs):
        buf_refs, rest = refs[:n], refs[n:]
        sems, token = rest[:2 * n], rest[-1]
        me = _linear(_mesh_position())
        for a in range(n):
            for k, peer in enumerate(_same_core_peers()):
                pltpu.make_async_remote_copy(
                    src_ref=buf_refs[a].at[me], dst_ref=buf_refs[a].at[me],
                    send_sem=sems[2 * a].at[k], recv_sem=sems[2 * a + 1].at[k],
                    device_id=peer, device_id_type=MESH_ID).start()
        token[...] = jnp.zeros_like(token)

    outs = pl.pallas_call(
        body, name=name,
        out_shape=tuple(pltpu.SemaphoreType.DMA((4,)) for _ in range(2 * n))
        + tuple(pltpu.HBM(b.shape, b.dtype) for b in bufs) + (jax.ShapeDtypeStruct((8, 128), F32),),
        in_specs=(HBM_SPEC,) * n,
        out_specs=(SEM_SPEC,) * (2 * n) + (HBM_SPEC,) * n + (pl.BlockSpec(memory_space=pltpu.VMEM),),
        input_output_aliases={a: 2 * n + a for a in range(n)},
        compiler_params=pltpu.CompilerParams(has_side_effects=DATAFLOW),
    )(*[pltpu.with_memory_space_constraint(b, pltpu.HBM) for b in bufs])
    return outs[:2 * n], outs[2 * n:3 * n], outs[3 * n]


def _gather_wait(sems, bufs, after, name):
    n = len(bufs)

    def body(*refs):
        buf_refs, sem_refs = refs[:n], refs[n:3 * n]
        x, y, c = _mesh_position()
        me = _linear((x, y, c))
        for a in range(n):
            for k, peer in enumerate(_same_core_peers()):
                cp = pltpu.make_async_remote_copy(
                    src_ref=buf_refs[a].at[me], dst_ref=buf_refs[a].at[_linear(peer)],
                    send_sem=sem_refs[2 * a].at[k], recv_sem=sem_refs[2 * a + 1].at[k],
                    device_id=peer, device_id_type=MESH_ID)
                cp.wait_send()
                cp.wait_recv()

    return list(pl.pallas_call(
        body, name=name,
        out_shape=tuple(pltpu.HBM(b.shape, b.dtype) for b in bufs),
        in_specs=(HBM_SPEC,) * n + (SEM_SPEC,) * (2 * n) + (pl.BlockSpec(memory_space=pl.ANY),) * len(after),
        out_specs=(HBM_SPEC,) * n, input_output_aliases={a: a for a in range(n)},
        compiler_params=pltpu.CompilerParams(has_side_effects=DATAFLOW),
    )(*bufs, *sems, *after))


def _gather_pass_on(bufs, name):
    n = len(bufs)

    def body(*refs):
        out_refs = refs[n:2 * n]
        send_sems, recv_sems = refs[2 * n:]
        x, y, c = _mesh_position()
        sibling = (x, y, 1 - c)
        chips = [(1 - x, y), (x, 1 - y), (1 - x, 1 - y)]
        copies = []
        for a in range(n):
            for j, chip in enumerate(chips):
                block = out_refs[a].at[_linear((*chip, c))]
                copies.append(pltpu.make_async_remote_copy(
                    src_ref=block, dst_ref=block, send_sem=send_sems.at[3 * a + j], recv_sem=recv_sems.at[3 * a + j],
                    device_id=sibling, device_id_type=MESH_ID))
                copies[-1].start()
        for a in range(n):
            for j, chip in enumerate(chips):
                copies[3 * a + j].wait_send()
                theirs = out_refs[a].at[_linear((*chip, 1 - c))]
                pltpu.make_async_remote_copy(
                    src_ref=theirs, dst_ref=theirs, send_sem=send_sems.at[3 * a + j], recv_sem=recv_sems.at[3 * a + j],
                    device_id=sibling, device_id_type=MESH_ID).wait_recv()

    hbm = pl.BlockSpec(memory_space=pl.ANY)
    return list(pl.pallas_call(
        body, name=name,
        out_shape=[jax.ShapeDtypeStruct(b.shape, b.dtype) for b in bufs],
        in_specs=[hbm] * n, out_specs=[hbm] * n, input_output_aliases={a: a for a in range(n)},
        scratch_shapes=[pltpu.SemaphoreType.DMA((3 * n,)), pltpu.SemaphoreType.DMA((3 * n,))],
    )(*bufs))


def _open_step(c, conv_w, w_ada, b_cols, w_in_t, later, rel_bias, bucket):
    cols = w_ada.shape[1]
    n_later = len(later)

    def body(c_ref, cw_ref, wa_ref, b_ref, w_ref, *rest):
        later_refs, rb_ref, bk_ref = rest[:n_later], rest[n_later], rest[n_later + 1]
        cond_ref, conv_ref, mod_ref, win_ref = rest[n_later + 2:n_later + 6]
        staged_refs, bias_ref = rest[n_later + 6:2 * n_later + 6], rest[2 * n_later + 6]
        cond_own, mod_own, stage = rest[2 * n_later + 7:2 * n_later + 10]
        later_stage = rest[2 * n_later + 10:3 * n_later + 10]
        s_send, s_recv, w_send, w_recv, local_sems = rest[3 * n_later + 10:]
        x, y, cc = _mesh_position()
        me = _linear((x, y, cc))
        sibling = (x, y, 1 - cc)
        chips = [(1 - x, y), (x, 1 - y), (1 - x, 1 - y)]
        v = c_ref[...]
        cond_own[...] = v * _sigmoid(v)
        stage[...] = w_ref[...].astype(BF16)

        def small(rnd, a, k, src, dst, slot):
            return pltpu.make_async_remote_copy(
                src_ref=src, dst_ref=dst.at[slot], send_sem=s_send.at[rnd, a, k - 1], recv_sem=s_recv.at[rnd, a, k - 1],
                device_id=_peer(k), device_id_type=MESH_ID)

        def block(p):
            return win_ref.at[_linear(p)]

        def big(k, blk, to, src=None):
            return pltpu.make_async_remote_copy(
                src_ref=block(blk) if src is None else src, dst_ref=block(blk),
                send_sem=w_send.at[k], recv_sem=w_recv.at[k], device_id=to, device_id_type=MESH_ID)

        mine = [pltpu.make_async_copy(cond_own, cond_ref.at[me], local_sems.at[0]),
                pltpu.make_async_copy(cw_ref, conv_ref.at[me], local_sems.at[1]),
                pltpu.make_async_copy(stage, block((x, y, cc)), local_sems.at[2])]
        for cp in mine:
            cp.start()
        sends = []
        for k in range(1, N_DEV):
            sends += [small(0, 0, k, cond_own, cond_ref, me), small(0, 1, k, cw_ref, conv_ref, me)]
        for cp in sends:
            cp.start()
        first = [big(0, (x, y, cc), sibling, src=stage)]
        first += [big(1 + j, (x, y, cc), (*chip, cc), src=stage) for j, chip in enumerate(chips)]
        for cp in first:
            cp.start()
        for a in range(n_later):
            later_stage[a][...] = later_refs[a][...].astype(BF16)
            mine.append(pltpu.make_async_copy(later_stage[a], staged_refs[a].at[me], local_sems.at[4 + a]))
            mine[-1].start()
        _fill_bias_table(rb_ref, bk_ref, bias_ref)
        for k in range(1, N_DEV):
            small(0, 0, k, cond_own, cond_ref, _linear(_peer(k))).wait_recv()
            small(0, 1, k, cw_ref, conv_ref, _linear(_peer(k))).wait_recv()
        mine[0].wait()
        cond_all = jnp.concatenate([cond_ref[k] for k in range(N_DEV)], axis=0)
        mod_own[...] = _dot(cond_all, wa_ref[...]) + b_ref[...]
        mine.append(pltpu.make_async_copy(mod_own, mod_ref.at[me], local_sems.at[3]))
        mine[-1].start()
        second = [small(1, 0, k, mod_own, mod_ref, me) for k in range(1, N_DEV)]
        for cp in second:
            cp.start()
        passed = []
        for j, chip in enumerate(chips):
            big(1 + j, (*chip, cc), (x, y, cc)).wait_recv()
            fwd = big(4 + j, (*chip, cc), sibling)
            fwd.start()
            passed.append(fwd)
        big(0, sibling, (x, y, cc)).wait_recv()
        for j, chip in enumerate(chips):
            big(4 + j, (*chip, 1 - cc), (x, y, cc)).wait_recv()
        for k in range(1, N_DEV):
            small(1, 0, k, mod_own, mod_ref, _linear(_peer(k))).wait_recv()
        for cp in sends + first + second + passed:
            cp.wait_send()
        for cp in mine[1:]:
            cp.wait()

    vmem = pl.BlockSpec(memory_space=pltpu.VMEM)
    outs = pl.pallas_call(
        body, name="open_step",
        out_shape=[jax.ShapeDtypeStruct((N_DEV,) + c.shape, F32), jax.ShapeDtypeStruct((N_DEV,) + conv_w.shape, F32),
                   jax.ShapeDtypeStruct((N_DEV, N_DEV, cols), F32),
                   jax.ShapeDtypeStruct((N_DEV,) + w_in_t.shape, BF16)]
        + [jax.ShapeDtypeStruct((N_DEV,) + a.shape, BF16) for a in later]
        + [jax.ShapeDtypeStruct((N_Q_HEADS, BLOCK, 2 * BLOCK), F32)],
        in_specs=[vmem] * (5 + n_later) + [pl.BlockSpec(memory_space=pltpu.SMEM), vmem],
        out_specs=[vmem, vmem, vmem, ANY_SPEC] + [ANY_SPEC] * n_later + [vmem],
        scratch_shapes=[pltpu.VMEM(c.shape, F32), pltpu.VMEM((N_DEV, cols), F32), pltpu.VMEM(w_in_t.shape, BF16)]
        + [pltpu.VMEM(a.shape, BF16) for a in later]
        + [pltpu.SemaphoreType.DMA((2, 2, N_DEV - 1)), pltpu.SemaphoreType.DMA((2, 2, N_DEV - 1)),
           pltpu.SemaphoreType.DMA((7,)), pltpu.SemaphoreType.DMA((7,)),
           pltpu.SemaphoreType.DMA((4 + n_later,))],
        compiler_params=_params(vmem=VMEM_LIMIT_LARGE),
    )(c, conv_w, w_ada, b_cols, w_in_t, *later, rel_bias, bucket)
    return outs[0], outs[1], outs[2], outs[3], list(outs[4:4 + n_later]), outs[4 + n_later]


def _in_proj(x, mod, g_norm1, w_in, tm):
    s = x.shape[0]

    def body(x_ref, mod_ref, g_ref, w_ref, h_ref, q_ref, kv_ref, gb_ref, gc_ref, xc_ref):
        xf = x_ref[...]
        n = xf * _rsqrt_mean_sq(xf) * g_ref[...]
        h = (n * (1.0 + mod_ref[SC1:SC1 + 1, :]) + mod_ref[SH1:SH1 + 1, :]).astype(BF16)
        h_ref[...] = h
        p = _dot_nt(h, w_ref[...])
        q_ref[...] = p[:, 0:512].astype(BF16)
        kv_ref[...] = p[:, 512:768].astype(BF16)
        gb_ref[...] = p[:, 768:1280].astype(BF16)
        gc_ref[...] = p[:, 1280:1792].astype(BF16)
        xc_ref[...] = p[:, 1792:2304].astype(BF16)

    return pl.pallas_call(
        body, name="in_proj", grid=(s // tm,),
        in_specs=[_rows(tm, D_MODEL), _full((8, D_MODEL)), _full((1, D_MODEL)), _full((IN_PROJ_WIDTH, D_MODEL))],
        out_specs=[_rows(tm, D_MODEL), _rows(tm, 512), _rows(tm, 256), _rows(tm, 512), _rows(tm, 512), _rows(tm, 512)],
        out_shape=[jax.ShapeDtypeStruct((s, D_MODEL), BF16), jax.ShapeDtypeStruct((s, 512), BF16),
                   jax.ShapeDtypeStruct((s, 256), BF16), jax.ShapeDtypeStruct((s, 512), BF16),
                   jax.ShapeDtypeStruct((s, 512), BF16), jax.ShapeDtypeStruct((s, 512), BF16)],
        compiler_params=_params(("arbitrary",), VMEM_LIMIT_LARGE),
    )(x, mod, g_norm1, w_in)


def _t5_bucket(dist):
    max_exact = N_BUCKETS // 2
    is_small = dist < max_exact
    d = jnp.maximum(dist, 1).astype(F32)
    large = max_exact + (jnp.log(d / max_exact) / math.log(MAX_DISTANCE / max_exact)
                         * (N_BUCKETS - max_exact)).astype(jnp.int32)
    large = jnp.minimum(large, N_BUCKETS - 1)
    return jnp.where(is_small, dist, large)


def _bucket_table():
    qi = jnp.arange(BLOCK, dtype=jnp.int32)[:, None]
    sj = jnp.arange(2 * BLOCK, dtype=jnp.int32)[None, :]
    return _t5_bucket(jnp.maximum(qi + BLOCK - sj, 0))


def _window_mask():
    qi = lax.broadcasted_iota(jnp.int32, (BLOCK, 2 * BLOCK), 0)
    sj = lax.broadcasted_iota(jnp.int32, (BLOCK, 2 * BLOCK), 1)
    dist = qi + BLOCK - sj
    return (dist >= 0) & (dist < BLOCK)


def _fill_bias_table(rb_ref, bk_ref, o_ref):
    bk = bk_ref[...]
    inside = _window_mask()
    for h in range(N_Q_HEADS):
        acc = jnp.zeros((BLOCK, 2 * BLOCK), F32)
        for b in range(N_BUCKETS):
            acc = jnp.where(bk == b, rb_ref[b, h], acc)
        o_ref[h] = jnp.where(inside, acc, NEG_INF)


def _load_kv_window(kv_ref, n):
    prev = jnp.maximum(n - 1, 0)
    kvw = jnp.concatenate([kv_ref[pl.ds(pl.multiple_of(prev * BLOCK, BLOCK), BLOCK), :],
                           kv_ref[pl.ds(pl.multiple_of(n * BLOCK, BLOCK), BLOCK), :]], axis=0)
    k, v = kvw[:, 0:128], kvw[:, 128:256]
    k_sw = pltpu.roll(k.astype(F32), 64, 1).astype(BF16)
    v_sw = pltpu.roll(v.astype(F32), 64, 1).astype(BF16)
    return (k, k_sw), (v, v_sw)


def _conv_taps(gc, xc, gc_prev, xc_prev, n):
    u = gc * xc
    before = jnp.where(n > 0, gc_prev.astype(F32) * xc_prev.astype(F32), 0.0)
    last = before.shape[0] - 1
    row = lax.broadcasted_iota(jnp.int32, u.shape, 0)
    u1 = jnp.where(row == 0, before[last:last + 1, :], pltpu.roll(u, 1, 0))
    u2 = jnp.where(row == 0, before[last - 1:last, :],
                   jnp.where(row == 1, before[last:last + 1, :], pltpu.roll(u, 2, 0)))
    return u, u1, u2


def _mixer_fwd(q, kv, gb, gc, xc, bias, sinks, conv_w, g_attn, g_conv):
    s = q.shape[0]
    nb = s // BLOCK

    per_step = min(MIXER_BLOCKS, nb)
    tile = per_step * BLOCK

    def one_block(n, slot, before, sink_ref, q_ref, kv_ref, gb_ref, gc_ref, xc_ref, bias_ref, cw_ref, ga_ref,
                  gcv_ref, attn_ref, merged_ref, lse_ref, p_ref):
        rows = slice(slot * BLOCK, (slot + 1) * BLOCK)
        ks, vs = _load_kv_window(kv_ref, n)
        lane = lax.broadcasted_iota(jnp.int32, (BLOCK, BLOCK), 1)
        low = lane < HEAD_DIM
        col = lax.broadcasted_iota(jnp.int32, (BLOCK, 2 * BLOCK), 1)
        no_prev = (col < BLOCK) & (n == 0)
        lse_all = jnp.zeros((BLOCK, BLOCK), F32)
        pairs = []
        for p in range(4):
            qp = q_ref[rows, 128 * p:128 * (p + 1)].astype(F32)
            kvh = p // 2
            res = []
            for e in range(2):
                h = 2 * p + e
                qm = jnp.where(low if e == 0 else ~low, qp, 0.0).astype(BF16)
                sw = 0 if kvh == e else 1
                sc = _dot_nt(qm, ks[sw]) * SCALE + bias_ref[h]
                sc = jnp.where(no_prev, NEG_INF, sc)
                sink = sink_ref[h]
                m = jnp.maximum(jnp.max(sc, axis=-1, keepdims=True), sink)
                pe = jnp.exp(sc - m)
                den = jnp.sum(pe, axis=-1, keepdims=True) + jnp.exp(sink - m)
                pb = (pe * (1.0 / den)).astype(BF16)
                p_ref[slot, h] = pb
                res.append(_dot(pb, vs[sw]))
                lse_all = lse_all + jnp.where(lane == h, m + jnp.log(den), 0.0)
            pairs.append(jnp.where(low, res[0], res[1]))
        attn = jnp.concatenate(pairs, axis=1)
        attn_ref[rows, :] = attn
        lse_ref[rows, :] = lse_all
        u, u1, u2 = _conv_taps(gc_ref[rows, :].astype(F32), xc_ref[rows, :].astype(F32), before[0], before[1], n)
        cw = cw_ref[...]
        cv = gb_ref[rows, :].astype(F32) * (cw[0:1, :] * u2 + cw[1:2, :] * u1 + cw[2:3, :] * u)
        an = attn * _rsqrt_mean_sq(attn) * ga_ref[...]
        cn = cv * _rsqrt_mean_sq(cv) * gcv_ref[...]
        merged_ref[rows, :] = jnp.concatenate([an, cn], axis=1).astype(BF16)

    def body(sink_ref, q_ref, kv_ref, gb_ref, gc_ref, xc_ref, gcp_ref, xcp_ref, *rest):
        step = pl.program_id(0)
        for sub in range(per_step):
            ahead = slice(sub * BLOCK - PREV_ROWS, sub * BLOCK)
            before = (gcp_ref[...], xcp_ref[...]) if sub == 0 else (gc_ref[ahead, :], xc_ref[ahead, :])
            one_block(step * per_step + sub, sub, before, sink_ref, q_ref, kv_ref, gb_ref, gc_ref, xc_ref, *rest)

    blk = lambda w: pl.BlockSpec((tile, w), lambda n: (n, 0))
    prev8 = pl.BlockSpec((PREV_ROWS, 512), lambda n: (jnp.maximum(n * (tile // PREV_ROWS) - 1, 0), 0))
    return pl.pallas_call(
        body, name="mixer_fwd", grid=(nb // per_step,),
        in_specs=[pl.BlockSpec(memory_space=pltpu.SMEM), blk(512), _full((s, 256)), blk(512), blk(512), blk(512),
                  prev8, prev8, _full((N_Q_HEADS, BLOCK, 2 * BLOCK)), _full((3, 512)), _full((1, 512)),
                  _full((1, 512))],
        out_specs=[blk(512), blk(1024), blk(128),
                   pl.BlockSpec((per_step, N_Q_HEADS, BLOCK, 2 * BLOCK), lambda n: (n, 0, 0, 0))],
        out_shape=[jax.ShapeDtypeStruct((s, 512), F32), jax.ShapeDtypeStruct((s, 1024), BF16),
                   jax.ShapeDtypeStruct((s, 128), F32),
                   jax.ShapeDtypeStruct((nb, N_Q_HEADS, BLOCK, 2 * BLOCK), BF16)],
        compiler_params=_params(("arbitrary",)),
    )(sinks, q, kv, gb, gc, xc, gc, xc, bias, conv_w, g_attn, g_conv)


def _out_proj(merged, x, mod, w_out, tm):
    s = x.shape[0]

    def body(m_ref, x_ref, mod_ref, w_ref, o_ref, x1_ref):
        o = _dot(m_ref[...], w_ref[...])
        o_ref[...] = o.astype(BF16)
        x1_ref[...] = x_ref[...] + mod_ref[G1:G1 + 1, :] * o

    return pl.pallas_call(
        body, name="out_proj", grid=(s // tm,),
        in_specs=[_rows(tm, D_MODEL), _rows(tm, D_MODEL), _full((8, D_MODEL)), _full((D_MODEL, D_MODEL))],
        out_specs=[_rows(tm, D_MODEL), _rows(tm, D_MODEL)],
        out_shape=[jax.ShapeDtypeStruct((s, D_MODEL), BF16), jax.ShapeDtypeStruct((s, D_MODEL), F32)],
        compiler_params=_params(("arbitrary",)),
    )(merged, x, mod, w_out)


def _resident(shape):
    nd = len(shape)
    return pl.BlockSpec(shape, lambda *_: (0,) * nd, pipeline_mode=pl.Buffered(1))


def _ffn(x1, o1, mod, g_norm2, w_gu, w_down, w_out, g_final, target, tm):
    s = x1.shape[0]
    chunk = D_FF // FFN_CHUNKS

    def body(x_ref, o1_ref, mod_ref, g_ref, wgu_ref, wd_ref, wo_ref, gf_ref, t_ref,
             h_ref, act_ref, do_ref, dgu_ref, dx1_ref, do1_ref, dm_ref, small_ref):
        @pl.when(pl.program_id(0) == 0)
        def _():
            small_ref[...] = jnp.zeros_like(small_ref)

        xf = x_ref[...]
        n = xf * _rsqrt_mean_sq(xf) * g_ref[...]
        h = (n * (1.0 + mod_ref[SC2:SC2 + 1, :]) + mod_ref[SH2:SH2 + 1, :]).astype(BF16)
        h_ref[...] = h
        gates, ups, o = [], [], None
        for j in range(FFN_CHUNKS):
            lo = j * chunk
            gate = _dot_nt(h, wgu_ref[lo:lo + chunk, :])
            up = _dot_nt(h, wgu_ref[D_FF + lo:D_FF + lo + chunk, :])
            sg = _sigmoid(gate)
            act = (gate * sg * up).astype(BF16)
            act_ref[:, lo:lo + chunk] = act
            gates.append((up * (sg * (1.0 + gate * (1.0 - sg)))).astype(BF16))
            ups.append((gate * sg).astype(BF16))
            part = _dot(act, wd_ref[lo:lo + chunk, :])
            o = part if o is None else o + part
        g2 = mod_ref[G2:G2 + 1, :]
        x2 = xf + g2 * o
        r = _rsqrt_mean_sq(x2)
        xn = x2 * r
        gf = gf_ref[...]
        err = xn * gf - t_ref[...]
        dy = err * (1.0 / D_MODEL)
        dxn = dy * gf
        dx2 = r * (dxn - xn * jnp.mean(dxn * xn, axis=-1, keepdims=True))
        small_ref[4:5, :] += _colsum(dy * xn)
        small_ref[5:6, :] += _colsum(err * err)
        small_ref[3:4, :] += _colsum(dx2 * o)
        do = (dx2 * g2).astype(BF16)
        do_ref[...] = do
        dh = None
        for j in range(FFN_CHUNKS):
            lo = j * chunk
            dact = _dot_nt(do, wd_ref[lo:lo + chunk, :])
            dgate = (dact * gates[j].astype(F32)).astype(BF16)
            dup = (dact * ups[j].astype(F32)).astype(BF16)
            dgu_ref[:, lo:lo + chunk] = dgate
            dgu_ref[:, D_FF + lo:D_FF + lo + chunk] = dup
            part = _dot(dgate, wgu_ref[lo:lo + chunk, :]) + _dot(dup, wgu_ref[D_FF + lo:D_FF + lo + chunk, :])
            dh = part if dh is None else dh + part
        dx1 = dx2 + _norm_mod_bwd(dh, xf, g_ref[...], mod_ref[SC2:SC2 + 1, :], small_ref)
        dx1_ref[...] = dx1.astype(BF16)
        small_ref[7:8, :] += _colsum(dx1 * o1_ref[...].astype(F32))
        do1 = (dx1 * mod_ref[G1:G1 + 1, :]).astype(BF16)
        do1_ref[...] = do1
        dm_ref[...] = _dot_nt(do1, wo_ref[...]).astype(BF16)

        @pl.when(pl.program_id(0) == pl.num_programs(0) - 1)
        def _():
            total = jnp.sum(small_ref[5:6, :], axis=-1, keepdims=True) * (0.5 / D_MODEL)
            small_ref[6:7, :] = jnp.broadcast_to(total, (1, D_MODEL))

    narrow = jax.ShapeDtypeStruct((s, D_MODEL), BF16)
    return pl.pallas_call(
        body, name="ffn", grid=(s // tm,),
        in_specs=[_rows(tm, D_MODEL), _rows(tm, D_MODEL), _full((8, D_MODEL)), _full((1, D_MODEL)),
                  _resident((2 * D_FF, D_MODEL)), _resident((D_FF, D_MODEL)), _resident((D_MODEL, D_MODEL)),
                  _full((1, D_MODEL)), _rows(tm, D_MODEL)],
        out_specs=[_rows(tm, D_MODEL), _rows(tm, D_FF), _rows(tm, D_MODEL), _rows(tm, 2 * D_FF), _rows(tm, D_MODEL),
                   _rows(tm, D_MODEL), _rows(tm, D_MODEL), _full((8, D_MODEL))],
        out_shape=[narrow, jax.ShapeDtypeStruct((s, D_FF), BF16), narrow, jax.ShapeDtypeStruct((s, 2 * D_FF), BF16),
                   narrow, narrow, narrow, jax.ShapeDtypeStruct((8, D_MODEL), F32)],
        compiler_params=_params(("arbitrary",), VMEM_LIMIT_LARGE),
    )(x1, o1, mod, g_norm2, w_gu, w_down, w_out, g_final, target)


def _norm_mod_bwd(dh, xf, g, scale_row, small_ref):
    r = _rsqrt_mean_sq(xf)
    xn = xf * r
    small_ref[0:1, :] += _colsum(dh)
    small_ref[1:2, :] += _colsum(dh * (xn * g))
    dn = dh * (1.0 + scale_row)
    small_ref[2:3, :] += _colsum(dn * xn)
    dxn = dn * g
    return r * (dxn - xn * jnp.mean(dxn * xn, axis=-1, keepdims=True))


def _group_norm_bwd(dm, a, g):
    r = _rsqrt_mean_sq(a)
    an = a * r
    dan = dm * g
    return r * (dan - an * jnp.mean(dan * an, axis=-1, keepdims=True)), _colsum(dm * an)


def _mixer_bwd(after, q, kv, gb, gc, xc, probs, sinks, conv_w, g_attn, g_conv, attn, lse, dmerged):
    s = q.shape[0]
    nb = s // BLOCK

    per_step = min(MIXER_BLOCKS, nb)
    tile = per_step * BLOCK
    steps = nb // per_step

    def one_block(n, slot, before, nxt, sink_ref, q_ref, kv_ref, gb_ref, gc_ref, xc_ref, p_ref, cw_ref, ga_ref,
                  gcv_ref, attn_ref, lse_ref, dm_ref, dproj_ref, dbias_ref, dsink_ref, small_ref):
        rows = slice(slot * BLOCK, (slot + 1) * BLOCK)
        next_dy, next_dkv = nxt
        dm = dm_ref[rows, :].astype(F32)
        gbv, gcv_, xcv = gb_ref[rows, :].astype(F32), gc_ref[rows, :].astype(F32), xc_ref[rows, :].astype(F32)
        u, u1, u2 = _conv_taps(gcv_, xcv, before[0], before[1], n)
        cw = cw_ref[...]
        yv = cw[0:1, :] * u2 + cw[1:2, :] * u1 + cw[2:3, :] * u
        dcv, dg_conv = _group_norm_bwd(dm[:, 512:1024], gbv * yv, gcv_ref[...])
        small_ref[1:2, :] += dg_conv
        dproj_ref[rows, 768:1280] = (dcv * yv).astype(BF16)
        dy = dcv * gbv
        row = lax.broadcasted_iota(jnp.int32, dy.shape, 0)
        d1 = jnp.where(row == BLOCK - 1, next_dy[0:1, :], pltpu.roll(dy, BLOCK - 1, 0))
        d2 = jnp.where(row == BLOCK - 2, next_dy[0:1, :],
                       jnp.where(row == BLOCK - 1, next_dy[1:2, :], pltpu.roll(dy, BLOCK - 2, 0)))
        du = cw[2:3, :] * dy + cw[1:2, :] * d1 + cw[0:1, :] * d2
        dproj_ref[rows, 1280:1792] = (du * xcv).astype(BF16)
        dproj_ref[rows, 1792:2304] = (du * gcv_).astype(BF16)
        small_ref[2:3, :] += _colsum(dy * u2)
        small_ref[3:4, :] += _colsum(dy * u1)
        small_ref[4:5, :] += _colsum(dy * u)

        attn_v = attn_ref[rows, :]
        dout, dg_attn = _group_norm_bwd(dm[:, 0:512], attn_v, ga_ref[...])
        small_ref[0:1, :] += dg_attn
        ks, vs = _load_kv_window(kv_ref, n)
        lane = lax.broadcasted_iota(jnp.int32, (BLOCK, BLOCK), 1)
        low = lane < HEAD_DIM
        lse_all = lse_ref[rows, :]
        dsink = jnp.zeros((BLOCK, BLOCK), F32)
        dq_pairs = []
        dk_groups, dv_groups = [], []
        for kvh in range(2):
            ds_rows, pr_rows, q_rows, do_rows = [], [], [], []
            for p in (2 * kvh, 2 * kvh + 1):
                qp = q_ref[rows, 128 * p:128 * (p + 1)].astype(F32)
                do_p = dout[:, 128 * p:128 * (p + 1)]
                prod = do_p * attn_v[:, 128 * p:128 * (p + 1)]
                res = []
                for e in range(2):
                    h = 2 * p + e
                    half = low if e == 0 else ~low
                    qm = jnp.where(half, qp, 0.0).astype(BF16)
                    dom = jnp.where(half, do_p, 0.0).astype(BF16)
                    delta = jnp.sum(jnp.where(half, prod, 0.0), axis=-1, keepdims=True)
                    lse_h = jnp.sum(jnp.where(lane == h, lse_all, 0.0), axis=-1, keepdims=True)
                    sw = 0 if kvh == e else 1
                    pb = p_ref[slot, h]
                    dp = _dot_nt(dom, vs[sw])
                    ds = pb.astype(F32) * (dp - delta)
                    dbias_ref[h] += ds
                    dsink = dsink + jnp.where(lane == h, -jnp.exp(sink_ref[h] - lse_h) * delta, 0.0)
                    dsb = ds.astype(BF16)
                    res.append(_dot(dsb, ks[sw]) * SCALE)
                    ds_rows.append(dsb)
                    pr_rows.append(pb)
                    q_rows.append(qm)
                    do_rows.append(dom)
                dq_pairs.append(jnp.where(low, res[0], res[1]))
            dk_g = _dot_tn(jnp.concatenate(ds_rows, axis=0), jnp.concatenate(q_rows, axis=0)) * SCALE
            dv_g = _dot_tn(jnp.concatenate(pr_rows, axis=0), jnp.concatenate(do_rows, axis=0))
            dk_groups.append(dk_g + pltpu.roll(dk_g, 64, 1))
            dv_groups.append(dv_g + pltpu.roll(dv_g, 64, 1))
        dproj_ref[rows, 0:512] = jnp.concatenate(dq_pairs, axis=1).astype(BF16)
        dsink_ref[...] += dsink
        low_kv = lax.broadcasted_iota(jnp.int32, (2 * BLOCK, BLOCK), 1) < HEAD_DIM
        dkv_win = jnp.concatenate([jnp.where(low_kv, dk_groups[0], dk_groups[1]),
                                   jnp.where(low_kv, dv_groups[0], dv_groups[1])], axis=1)
        dproj_ref[rows, 512:768] = (dkv_win[BLOCK:2 * BLOCK, :] + next_dkv).astype(BF16)
        return dy[0:8, :], dkv_win[0:BLOCK, :]

    def body(sink_ref, q_ref, kv_ref, gb_ref, gc_ref, xc_ref, gcp_ref, xcp_ref, *rest):
        refs, dy_ref, dkv_ref = rest[:-2], rest[-2], rest[-1]
        dbias_ref, dsink_ref, small_ref = refs[8], refs[9], refs[10]
        step = pl.program_id(0)

        @pl.when(step == 0)
        def _():
            dbias_ref[...] = jnp.zeros_like(dbias_ref)
            dsink_ref[...] = jnp.zeros_like(dsink_ref)
            small_ref[...] = jnp.zeros_like(small_ref)
            dy_ref[...] = jnp.zeros_like(dy_ref)
            dkv_ref[...] = jnp.zeros_like(dkv_ref)

        nxt = (dy_ref[...], dkv_ref[...])
        for sub in reversed(range(per_step)):
            ahead = slice(sub * BLOCK - PREV_ROWS, sub * BLOCK)
            before = (gcp_ref[...], xcp_ref[...]) if sub == 0 else (gc_ref[ahead, :], xc_ref[ahead, :])
            nxt = one_block((steps - 1 - step) * per_step + sub, sub, before, nxt,
                            sink_ref, q_ref, kv_ref, gb_ref, gc_ref, xc_ref, *refs)
        dy_ref[...], dkv_ref[...] = nxt

        @pl.when(step == steps - 1)
        def _():
            small_ref[5:6, :] = jnp.concatenate([_colsum(dsink_ref[...]), jnp.zeros((1, 512 - BLOCK), F32)], axis=1)

    blk = lambda w: pl.BlockSpec((tile, w), lambda t: (steps - 1 - t, 0))
    prev8 = pl.BlockSpec((PREV_ROWS, 512),
                         lambda t: (jnp.maximum((steps - 1 - t) * (tile // PREV_ROWS) - 1, 0), 0))
    bf = lambda w: jax.ShapeDtypeStruct((s, w), BF16)
    return pl.pallas_call(
        _coming_behind(body), name="mixer_bwd", grid=(steps,),
        in_specs=[ANY_SPEC, pl.BlockSpec(memory_space=pltpu.SMEM), blk(512), _full((s, 256)), blk(512), blk(512), blk(512),
                  prev8, prev8,
                  pl.BlockSpec((per_step, N_Q_HEADS, BLOCK, 2 * BLOCK), lambda t: (steps - 1 - t, 0, 0, 0)),
                  _full((3, 512)), _full((1, 512)), _full((1, 512)), blk(512), blk(128), blk(1024)],
        out_specs=[blk(IN_PROJ_WIDTH), _full((N_Q_HEADS, BLOCK, 2 * BLOCK)), _full((BLOCK, BLOCK)), _full((8, 512))],
        out_shape=[bf(IN_PROJ_WIDTH), jax.ShapeDtypeStruct((N_Q_HEADS, BLOCK, 2 * BLOCK), F32),
                   jax.ShapeDtypeStruct((BLOCK, BLOCK), F32), jax.ShapeDtypeStruct((8, 512), F32)],
        scratch_shapes=[pltpu.VMEM((8, 512), F32), pltpu.VMEM((BLOCK, 2 * KV_WIDTH), F32)],
        compiler_params=_params(("arbitrary",), VMEM_LIMIT_LARGE),
    )(after, sinks, q, kv, gb, gc, xc, gc, xc, probs, conv_w, g_attn, g_conv, attn, lse, dmerged)


def _in_proj_bwd(after, dproj, x, dx1, mod, g_norm1, w_in, tm):
    s = x.shape[0]

    def body(dproj_ref, x_ref, dx1_ref, mod_ref, g_ref, w_ref, dx_ref, small_ref):
        @pl.when(pl.program_id(0) == 0)
        def _():
            small_ref[...] = jnp.zeros_like(small_ref)

        dh = _dot(dproj_ref[...], w_ref[...])
        dx_ref[...] = dx1_ref[...].astype(F32) + _norm_mod_bwd(dh, x_ref[...], g_ref[...], mod_ref[SC1:SC1 + 1, :],
                                                               small_ref)

    return pl.pallas_call(
        _coming_behind(body), name="in_proj_bwd", grid=(s // tm,),
        in_specs=[ANY_SPEC, _rows(tm, IN_PROJ_WIDTH), _rows(tm, D_MODEL), _rows(tm, D_MODEL), _full((8, D_MODEL)),
                  _full((1, D_MODEL)), _full((IN_PROJ_WIDTH, D_MODEL))],
        out_specs=[_rows(tm, D_MODEL), _full((8, D_MODEL))],
        out_shape=[jax.ShapeDtypeStruct((s, D_MODEL), F32), jax.ShapeDtypeStruct((8, D_MODEL), F32)],
        compiler_params=_params(("arbitrary",), VMEM_LIMIT_LARGE),
    )(after, dproj, x, dx1, mod, g_norm1, w_in)


def _weight_grad(a, b, tk, ts, name, after=None):
    s, k = a.shape
    n = b.shape[1]
    nt = s // ts
    extra = [] if after is None else [after]

    def body(a_ref, b_ref, *rest):
        o_ref, acc_ref = rest[-2:]
        t = pl.program_id(1)
        @pl.when(t == 0)
        def _():
            acc_ref[...] = jnp.zeros_like(acc_ref)

        acc = acc_ref[...] + _dot_tn(a_ref[...], b_ref[...])
        acc_ref[...] = acc
        o_ref[...] = acc.astype(BF16)

    return pl.pallas_call(
        body, name=name, grid=(k // tk, nt),
        in_specs=[pl.BlockSpec((ts, tk), lambda i, t: (t, i)), pl.BlockSpec((ts, n), lambda i, t: (t, 0))]
        + [ANY_SPEC] * len(extra),
        out_specs=pl.BlockSpec((tk, n), lambda i, t: (i, 0)),
        out_shape=jax.ShapeDtypeStruct((k, n), BF16),
        scratch_shapes=[pltpu.VMEM((tk, n), F32)],
        compiler_params=_params(("arbitrary", "arbitrary"), VMEM_LIMIT_LARGE),
    )(a, b, *extra)


def _rel_bias_grad(dbias, bucket):
    def body(db_ref, bk_ref, o_ref, rows_ref):
        bk = bk_ref[...]
        for b in range(N_BUCKETS):
            sel = (bk == b).astype(F32)
            for h in range(N_Q_HEADS):
                rows_ref[N_BUCKETS * h + b:N_BUCKETS * h + b + 1, :] = _colsum(db_ref[h] * sel)
        head = lax.broadcasted_iota(jnp.int32, (N_BUCKETS, N_Q_HEADS), 1)
        out = jnp.zeros((N_BUCKETS, N_Q_HEADS), F32)
        for h in range(N_Q_HEADS):
            per_bucket = jnp.sum(rows_ref[N_BUCKETS * h:N_BUCKETS * (h + 1), :], axis=-1, keepdims=True)
            out = out + jnp.where(head == h, per_bucket, 0.0)
        o_ref[...] = out

    return pl.pallas_call(
        body, name="rel_bias_grad",
        out_shape=jax.ShapeDtypeStruct((N_BUCKETS, N_Q_HEADS), F32),
        scratch_shapes=[pltpu.VMEM((N_BUCKETS * N_Q_HEADS, 2 * BLOCK), F32)],
    )(dbias, bucket)


def _lanes_from(x, start, width):
    n = x.shape[1]
    return pltpu.roll(x, (n - start) % n, 1)[:, 0:width]


def _adamw_w_ada(me, cond_all, packed_all, w, m, v, tr):
    r, cols = w.shape

    def body(me_ref, c_ref, p_ref, w_ref, m_ref, v_ref, g_ref, d_ref, mo_ref, vo_ref):
        dmod = jnp.concatenate([p_ref[k][:, OFF_DMOD:OFF_DMOD + N_MOD * D_MODEL] for k in range(N_DEV)], axis=0)
        mine = _lanes_from(dmod, me_ref[0] * cols, cols)
        pad = lambda a: jnp.concatenate([a, jnp.zeros((128 - N_DEV, a.shape[1]), F32)], axis=0)
        g = _dot_tn(pad(c_ref[...]), pad(mine))
        g_ref[...] = g
        d_ref[...], mo_ref[...], vo_ref[...] = _adam_math(w_ref[...], g, m_ref[...], v_ref[...])

    tile = pl.BlockSpec((tr, cols), lambda i, me_ref: (i, 0))
    return pl.pallas_call(
        body, name="adamw_w_ada",
        grid_spec=pltpu.PrefetchScalarGridSpec(
            num_scalar_prefetch=1, grid=(r // tr,),
            in_specs=[pl.BlockSpec((N_DEV, tr), lambda i, me_ref: (0, i)),
                      pl.BlockSpec(packed_all.shape, lambda i, me_ref: (0, 0, 0)), tile, tile, tile],
            out_specs=[tile] * 4),
        out_shape=[jax.ShapeDtypeStruct((r, cols), F32)] * 4,
        compiler_params=_params(("arbitrary",)),
    )(me, cond_all, packed_all, w, m, v)


SMALL_PARAMS = (("rel_bias", None), ("b_ada", (OFF_DMOD, N_MOD * D_MODEL)), ("g_norm1", (OFF_GN1, D_MODEL)),
                ("sinks", (OFF_SINK, N_Q_HEADS)), ("conv_w", None), ("g_attn_out", (OFF_GATT, ATTN_WIDTH)),
                ("g_conv_out", (OFF_GCV, CONV_WIDTH)), ("g_norm2", (OFF_GN2, D_MODEL)),
                ("g_final", (OFF_GFIN, D_MODEL)))


def _small_update(me, packed_all, rel_all, state, after):
    n_p = len(SMALL_PARAMS)
    flat = [a for triple in state for a in triple]
    conv_cols = state[4][0].shape[1]

    def body(me_ref, p_ref, r_ref, *refs):
        ins = refs[:3 * n_p]
        loss_ref, outs = refs[3 * n_p + len(after)], refs[3 * n_p + len(after) + 1:]
        small, rel = p_ref[0], r_ref[0]
        for k in range(1, N_DEV):
            small = small + p_ref[k]
            rel = rel + r_ref[k]
        loss_ref[...] = small[:, OFF_LOSS:OFF_LOSS + 128]
        taps = jnp.concatenate([small[:, OFF_CONVW + CONV_WIDTH * j:OFF_CONVW + CONV_WIDTH * (j + 1)]
                                for j in range(3)] + [jnp.zeros((5, CONV_WIDTH), F32)], axis=0)
        conv_g = _lanes_from(taps, me_ref[0] * conv_cols, conv_cols)[0:3, :]
        for i, (name, lanes) in enumerate(SMALL_PARAMS):
            g = rel if name == "rel_bias" else conv_g if name == "conv_w" else small[:, lanes[0]:lanes[0] + lanes[1]]
            w_ref, m_ref, v_ref = ins[3 * i:3 * i + 3]
            outs[4 * i][...] = g
            outs[4 * i + 1][...], outs[4 * i + 2][...], outs[4 * i + 3][...] = _adam_math(
                w_ref[...], g, m_ref[...], v_ref[...])

    vmem = pl.BlockSpec(memory_space=pltpu.VMEM)
    out_shape = [jax.ShapeDtypeStruct((1, 128), F32)]
    for w, _, _ in state:
        out_shape += [jax.ShapeDtypeStruct(w.shape, F32)] * 4
    outs = pl.pallas_call(
        body, name="small_update",
        in_specs=[pl.BlockSpec(memory_space=pltpu.SMEM), vmem, vmem] + [vmem] * len(flat)
        + [pl.BlockSpec(memory_space=pl.ANY)] * len(after),
        out_shape=out_shape,
    )(me, packed_all, rel_all, *flat, *after)
    return outs[0], [tuple(outs[1 + 4 * i:5 + 4 * i]) for i in range(n_p)]


def _adam_math(w, g, m, v):
    m = ADAM_B1 * m + (1.0 - ADAM_B1) * g
    v = ADAM_B2 * v + (1.0 - ADAM_B2) * (g * g)
    m_hat = m / (1.0 - ADAM_B1 ** ADAM_STEP)
    v_hat = v / (1.0 - ADAM_B2 ** ADAM_STEP)
    delta = -ADAM_LR * (m_hat / (jnp.sqrt(v_hat) + ADAM_EPS) + ADAM_WD * w)
    return delta, m, v


def _adamw_parts(w, m, v, local, land, me, tr, name):
    r, c = w.shape

    def body(me_ref, w_ref, m_ref, v_ref, own_ref, land_ref, g_ref, d_ref, mo_ref, vo_ref):
        g = own_ref[0].astype(F32)
        for k in range(N_DEV - 1):
            g = g + land_ref[k].astype(F32)
        g_ref[...] = g
        d_ref[...], mo_ref[...], vo_ref[...] = _adam_math(w_ref[...], g, m_ref[...], v_ref[...])

    tile = pl.BlockSpec((tr, c), lambda i, me_ref: (i, 0))
    return pl.pallas_call(
        body, name=name,
        grid_spec=pltpu.PrefetchScalarGridSpec(
            num_scalar_prefetch=1, grid=(r // tr,),
            in_specs=[tile, tile, tile, pl.BlockSpec((1, tr, c), lambda i, me_ref: (me_ref[0], i, 0)),
                      pl.BlockSpec((N_DEV - 1, tr, c), lambda i, me_ref: (0, i, 0))],
            out_specs=[tile] * 4),
        out_shape=[jax.ShapeDtypeStruct((r, c), F32)] * 4,
        compiler_params=_params(("arbitrary",)),
    )(me, w, m, v, local, land)


def _behind(a, token):
    return a + token[0:a.shape[0], 0:1]


def _local_step(x, target, mod, w_in_t, bias, weights_out_gu, weights_down, g_norm1, sinks, conv_w, g_attn,
                g_conv, g_norm2, g_final, exchange):
    s = x.shape[0]
    tm = min(512, s)
    tm_small = min(256, s)
    bucket = _bucket_table()

    h, q, kv, gb, gc, xc = _in_proj(x, mod, g_norm1, w_in_t, tm)
    attn, merged, lse, probs = _mixer_fwd(q, kv, gb, gc, xc, bias, sinks, conv_w, g_attn, g_conv)
    w_out, w_gu_t = weights_out_gu(merged)
    o1, x1 = _out_proj(merged, x, mod, w_out, tm)
    w_down = weights_down(x1)
    h2, act, do2, dgu, dx1, do1, dmerged, sm_2 = _ffn(x1, o1, mod, g_norm2, w_gu_t, w_down, w_out, g_final, target,
                                                      tm_small)
    ts = min(WEIGHT_GRAD_ROWS, s)
    tok_down = exchange("w_down", _weight_grad(act, do2, D_FF // 2, ts, "w_down_grad"))
    tok_gu = exchange("w_gu", _weight_grad(dgu, h2, D_FF // 2, ts, "w_gu_grad", after=tok_down))
    tok_out = exchange("w_out", _weight_grad(merged, do1, D_MODEL, ts, "w_out_grad", after=tok_gu))
    dproj, dbias, dsink, sm_mix = _mixer_bwd(
        tok_out, q, kv, gb, gc, xc, probs, sinks, conv_w, g_attn, g_conv, attn, lse, dmerged)
    tok_in = exchange("w_in", _weight_grad(dproj, h, IN_PROJ_WIDTH // 2, ts, "w_in_grad"))
    dx, sm_1 = _in_proj_bwd(tok_in, dproj, x, dx1, mod, g_norm1, w_in_t, min(1024, s))
    d_rel = _rel_bias_grad(dbias, bucket)

    packed = jnp.concatenate([
        sm_1[0], sm_1[1], sm_2[7], sm_2[0], sm_2[1], sm_2[3],
        sm_1[2],
        sm_mix[5, 0:128],
        sm_mix[0], sm_mix[1],
        sm_2[2],
        sm_2[4],
        sm_mix[2], sm_mix[3], sm_mix[4],
        sm_2[6, 0:128],
    ])[None, :]
    return dx, packed, d_rel


def kernel(x, c, rel_bias, w_ada, b_ada, g_norm1, w_in, sinks, conv_w, g_attn_out, g_conv_out, w_out, g_norm2, w_gu, w_down, g_final, loss_target, m_rel_bias, m_w_ada, m_b_ada, m_g_norm1, m_w_in, m_sinks, m_conv_w, m_g_attn_out, m_g_conv_out, m_w_out, m_g_norm2, m_w_gu, m_w_down, m_g_final, v_rel_bias, v_w_ada, v_b_ada, v_g_norm1, v_w_in, v_sinks, v_conv_w, v_g_attn_out, v_g_conv_out, v_w_out, v_g_norm2, v_w_gu, v_w_down, v_g_final):
    me = _linear(_mesh_position())
    me_arr = jnp.reshape(me, (1,)).astype(jnp.int32)
    ada_cols = w_ada.shape[2]
    tm = min(512, x.shape[1])

    b_cols = lax.dynamic_slice_in_dim(b_ada, me * ada_cols, ada_cols, axis=1)
    cond_all, conv_w_all, mod_all, w_in_blocks, staged, bias = _open_step(
        c, conv_w[0], w_ada[0], b_cols, w_in[0].T, [w_out[0], w_gu[0].T, w_down[0]], rel_bias, _bucket_table())
    cond_all = cond_all[:, 0, :]
    conv_w_full = conv_w_all.transpose(1, 0, 2).reshape(3, CONV_WIDTH)
    mod = lax.dynamic_index_in_dim(mod_all, me, axis=1, keepdims=False).reshape(N_MOD, D_MODEL)
    mod = jnp.concatenate([mod, jnp.zeros((2, D_MODEL), F32)], axis=0)
    w_in_t = w_in_blocks.reshape(IN_PROJ_WIDTH, D_MODEL)
    gather_sems, staged, gather_token = _gather_start(staged, "gather_start_weights")
    mod = _behind(mod, gather_token)

    def weights_out_gu(after):
        got = _gather_pass_on(_gather_wait(gather_sems[0:4], staged[0:2], [after], "gather_wait_out_gu"),
                              "gather_pass_on_out_gu")
        return got[0].reshape(D_MODEL, D_MODEL), got[1].reshape(2 * D_FF, D_MODEL)

    def weights_down(after):
        got = _gather_pass_on(_gather_wait(gather_sems[4:6], staged[2:3], [after], "gather_wait_down"),
                              "gather_pass_on_down")
        return got[0].reshape(D_FF, D_MODEL)

    started = {}

    def exchange(name, dw):
        st = _exchange_start(dw.reshape(N_DEV, dw.shape[0] // N_DEV, dw.shape[1]), "exchange_start_" + name)
        started[name] = st
        return st[4]

    dx, packed, d_rel = _local_step(
        x[0], loss_target[0], mod, w_in_t, bias, weights_out_gu, weights_down, g_norm1, sinks[0], conv_w_full,
        g_attn_out, g_conv_out, g_norm2, g_final[None, :], exchange)

    def zone(a):
        return lax.dynamic_update_slice(jnp.zeros((N_DEV,) + a.shape, F32), a[None], (me,) + (0,) * a.ndim)

    shared = _share_start([packed, d_rel], [zone(packed), zone(d_rel)], "share_small_start")

    def finish(name, after, w, m, v, tr):
        src, land = _exchange_wait(started[name], after, "exchange_wait_" + name)
        return _adamw_parts(w, m, v, src, land, me_arr, tr, "adamw_" + name)

    g_down, d_down, nm_down, nv_down = finish("w_down", [shared[2][0]], w_down[0], m_w_down[0], v_w_down[0], 176)
    g_gu, d_gu, nm_gu, nv_gu = finish("w_gu", [nv_down], w_gu[0].T, m_w_gu[0].T, v_w_gu[0].T, 352)
    g_out, d_out, nm_out, nv_out = finish("w_out", [nv_gu], w_out[0], m_w_out[0], v_w_out[0], 128)

    packed_all, rel_all = _share_wait(shared, [nv_out], "share_small_wait")
    g_ada, d_ada, nm_ada, nv_ada = _adamw_w_ada(me_arr, cond_all, packed_all, w_ada[0], m_w_ada[0], v_w_ada[0], 256)
    as_rows = {"conv_w": lambda a: a[0], "g_final": lambda a: a[None, :]}
    small_state = {
        "rel_bias": (rel_bias, m_rel_bias, v_rel_bias), "b_ada": (b_ada, m_b_ada, v_b_ada),
        "g_norm1": (g_norm1, m_g_norm1, v_g_norm1), "sinks": (sinks, m_sinks, v_sinks),
        "conv_w": (conv_w, m_conv_w, v_conv_w), "g_attn_out": (g_attn_out, m_g_attn_out, v_g_attn_out),
        "g_conv_out": (g_conv_out, m_g_conv_out, v_g_conv_out), "g_norm2": (g_norm2, m_g_norm2, v_g_norm2),
        "g_final": (g_final, m_g_final, v_g_final),
    }
    state = [tuple(as_rows.get(name, lambda a: a)(a) for a in small_state[name]) for name, _ in SMALL_PARAMS]
    loss_row, small_out = _small_update(me_arr, packed_all, rel_all, state, [])
    loss = loss_row[0, 0]
    small_res = {name: tuple(a.reshape(small_state[name][0].shape) for a in res)
                 for (name, _), res in zip(SMALL_PARAMS, small_out)}

    g_in, d_in, nm_in, nv_in = finish("w_in", [loss_row, nv_ada], w_in[0].T, m_w_in[0].T, v_w_in[0].T, 144)

    big = {
        "w_ada": (g_ada[None], d_ada[None], nm_ada[None], nv_ada[None]),
        "w_in": (g_in.T[None], d_in.T[None], nm_in.T[None], nv_in.T[None]),
        "w_out": (g_out[None], d_out[None], nm_out[None], nv_out[None]),
        "w_gu": (g_gu.T[None], d_gu.T[None], nm_gu.T[None], nv_gu.T[None]),
        "w_down": (g_down[None], d_down[None], nm_down[None], nv_down[None]),
    }
    order = ["rel_bias", "w_ada", "b_ada", "g_norm1", "w_in", "sinks", "conv_w", "g_attn_out", "g_conv_out", "w_out",
             "g_norm2", "w_gu", "w_down", "g_final"]
    results = [big[k] if k in big else small_res[k] for k in order]
    return (loss, dx[None], *[r[0] for r in results], *[r[1] for r in results], *[r[2] for r in results],
            *[r[3] for r in results])
```

```python
import math

import jax
import jax.numpy as jnp
from jax import lax
from jax.experimental import pallas as pl
from jax.experimental.pallas import tpu as pltpu

F32 = jnp.float32
BF16 = jnp.bfloat16

D_MODEL = 1024
HEAD_DIM = 64
N_Q_HEADS = 8
ATTN_WIDTH = 512
KV_WIDTH = 128
CONV_WIDTH = 512
IN_PROJ_WIDTH = 2304
D_FF = 2816
N_MOD = 6
N_BUCKETS = 32
MAX_DISTANCE = 128
BLOCK = 128
EPS = 1e-6
NEG_INF = -1e30
SCALE = HEAD_DIM ** -0.5
N_DEV = 8

ADAM_LR = 0.001
ADAM_B1 = 0.9
ADAM_B2 = 0.999
ADAM_EPS = 1e-08
ADAM_WD = 0.01
ADAM_STEP = 10

SH1, SC1, G1, SH2, SC2, G2 = range(6)

VMEM_LIMIT_LARGE = 60 * 1024 * 1024
WEIGHT_GRAD_ROWS = 2048
FFN_CHUNKS = 1
PREV_ROWS = 16
MIXER_BLOCKS = 4
MESH_ID = pl.DeviceIdType.MESH

OFF_DMOD = 0
OFF_GN1 = OFF_DMOD + N_MOD * D_MODEL
OFF_SINK = OFF_GN1 + D_MODEL
OFF_GATT = OFF_SINK + 128
OFF_GCV = OFF_GATT + ATTN_WIDTH
OFF_GN2 = OFF_GCV + CONV_WIDTH
OFF_GFIN = OFF_GN2 + D_MODEL
OFF_CONVW = OFF_GFIN + D_MODEL
OFF_LOSS = OFF_CONVW + 3 * CONV_WIDTH
PACKED = OFF_LOSS + 128


def _params(sem=None, vmem=None):
    return pltpu.CompilerParams(dimension_semantics=sem, vmem_limit_bytes=vmem)


def _coming_behind(body):
    def skipping(after_ref, *refs):
        body(*refs)

    return skipping


ANY_SPEC = pl.BlockSpec(memory_space=pl.ANY)


def _full(shape):
    nd = len(shape)
    return pl.BlockSpec(shape, lambda *_: (0,) * nd)


def _rows(tm, width):
    return pl.BlockSpec((tm, width), lambda i, *_: (i, 0))


def _sigmoid(x):
    return 1.0 / (1.0 + jnp.exp(-x))


def _rsqrt_mean_sq(x):
    return lax.rsqrt(jnp.mean(x * x, axis=-1, keepdims=True) + EPS)


def _colsum(x):
    return jnp.sum(x, axis=0, keepdims=True)


def _dot(a, b):
    return jnp.dot(a, b, preferred_element_type=F32)


def _dot_nt(a, b):
    return lax.dot_general(a, b, (((1,), (1,)), ((), ())), preferred_element_type=F32)


def _dot_tn(a, b):
    return lax.dot_general(a, b, (((0,), (0,)), ((), ())), preferred_element_type=F32)


def _mesh_position():
    return lax.axis_index("x"), lax.axis_index("y"), lax.axis_index("c")


def _linear(p):
    return 4 * p[0] + 2 * p[1] + p[2]


def _peer(k):
    x, y, c = _mesh_position()
    return (1 - x if k & 4 else x, 1 - y if k & 2 else y, 1 - c if k & 1 else c)


HBM_SPEC = pl.BlockSpec(memory_space=pltpu.HBM)
SEM_SPEC = pl.BlockSpec(memory_space=pltpu.SEMAPHORE)
DATAFLOW = pltpu.SideEffectType.DATAFLOW_SIDE_EFFECTING


def _exchange_start(src, name):
    r, c = src.shape[1:]

    def body(src_ref, land_ref, send_sems, recv_sems, src_thru, land_thru, token):
        for k in range(1, N_DEV):
            peer = _peer(k)
            pltpu.make_async_remote_copy(
                src_ref=src_ref.at[_linear(peer)], dst_ref=land_ref.at[k - 1],
                send_sem=send_sems.at[k - 1], recv_sem=recv_sems.at[k - 1],
                device_id=peer, device_id_type=MESH_ID).start()
        token[...] = jnp.zeros_like(token)

    land = lax.empty((N_DEV - 1, r, c), src.dtype)
    return pl.pallas_call(
        body, name=name,
        out_shape=(pltpu.SemaphoreType.DMA((N_DEV - 1,)), pltpu.SemaphoreType.DMA((N_DEV - 1,)),
                   pltpu.HBM(src.shape, src.dtype), pltpu.HBM(land.shape, land.dtype),
                   jax.ShapeDtypeStruct((8, 128), F32)),
        in_specs=(HBM_SPEC, HBM_SPEC),
        out_specs=(SEM_SPEC, SEM_SPEC, HBM_SPEC, HBM_SPEC, pl.BlockSpec(memory_space=pltpu.VMEM)),
        input_output_aliases={0: 2, 1: 3},
        compiler_params=pltpu.CompilerParams(has_side_effects=DATAFLOW),
    )(pltpu.with_memory_space_constraint(src, pltpu.HBM), pltpu.with_memory_space_constraint(land, pltpu.HBM))


def _exchange_wait(started, after, name):
    send_sems, recv_sems, src_thru, land_thru, _ = started

    def body(src_ref, land_ref, send_sems, recv_sems, *rest):
        for k in range(1, N_DEV):
            cp = pltpu.make_async_remote_copy(
                src_ref=src_ref.at[0], dst_ref=land_ref.at[k - 1],
                send_sem=send_sems.at[k - 1], recv_sem=recv_sems.at[k - 1],
                device_id=_peer(k), device_id_type=MESH_ID)
            cp.wait_send()
            cp.wait_recv()

    return pl.pallas_call(
        body, name=name,
        out_shape=(pltpu.HBM(src_thru.shape, src_thru.dtype), pltpu.HBM(land_thru.shape, land_thru.dtype)),
        in_specs=(HBM_SPEC, HBM_SPEC, SEM_SPEC, SEM_SPEC) + (pl.BlockSpec(memory_space=pl.ANY),) * len(after),
        out_specs=(HBM_SPEC, HBM_SPEC), input_output_aliases={0: 0, 1: 1},
        compiler_params=pltpu.CompilerParams(has_side_effects=DATAFLOW),
    )(src_thru, land_thru, send_sems, recv_sems, *after)


def _share_start(arrs, zones, name):
    n = len(arrs)

    def body(*refs):
        src_refs, zone_refs, sems = refs[:n], refs[n:2 * n], refs[2 * n:4 * n]
        me = _linear(_mesh_position())
        for a in range(n):
            for k in range(1, N_DEV):
                pltpu.make_async_remote_copy(
                    src_ref=src_refs[a], dst_ref=zone_refs[a].at[me],
                    send_sem=sems[2 * a].at[k - 1], recv_sem=sems[2 * a + 1].at[k - 1],
                    device_id=_peer(k), device_id_type=MESH_ID).start()

    outs = pl.pallas_call(
        body, name=name,
        out_shape=tuple(pltpu.SemaphoreType.DMA((N_DEV - 1,)) for _ in range(2 * n))
        + tuple(pltpu.HBM(a.shape, a.dtype) for a in arrs) + tuple(pltpu.HBM(z.shape, z.dtype) for z in zones),
        in_specs=(HBM_SPEC,) * (2 * n),
        out_specs=(SEM_SPEC,) * (2 * n) + (HBM_SPEC,) * (2 * n),
        input_output_aliases={i: 2 * n + i for i in range(2 * n)},
        compiler_params=pltpu.CompilerParams(has_side_effects=DATAFLOW),
    )(*[pltpu.with_memory_space_constraint(a, pltpu.HBM) for a in list(arrs) + list(zones)])
    return outs[:2 * n], outs[2 * n:3 * n], outs[3 * n:]


def _share_wait(started, after, name):
    sems, arrs, zones = started
    n = len(arrs)

    def body(*refs):
        src_refs, zone_refs, sem_refs = refs[:n], refs[n:2 * n], refs[2 * n:4 * n]
        for a in range(n):
            for k in range(1, N_DEV):
                cp = pltpu.make_async_remote_copy(
                    src_ref=src_refs[a], dst_ref=zone_refs[a].at[_linear(_peer(k))],
                    send_sem=sem_refs[2 * a].at[k - 1], recv_sem=sem_refs[2 * a + 1].at[k - 1],
                    device_id=_peer(k), device_id_type=MESH_ID)
                cp.wait_send()
                cp.wait_recv()

    outs = pl.pallas_call(
        body, name=name,
        out_shape=tuple(pltpu.HBM(a.shape, a.dtype) for a in arrs) + tuple(pltpu.HBM(z.shape, z.dtype) for z in zones),
        in_specs=(HBM_SPEC,) * (2 * n) + (SEM_SPEC,) * (2 * n) + (pl.BlockSpec(memory_space=pl.ANY),) * len(after),
        out_specs=(HBM_SPEC,) * (2 * n), input_output_aliases={i: i for i in range(2 * n)},
        compiler_params=pltpu.CompilerParams(has_side_effects=DATAFLOW),
    )(*arrs, *zones, *sems, *after)
    return list(outs[n:])


def _same_core_peers():
    x, y, c = _mesh_position()
    return [(x, y, 1 - c), (1 - x, y, c), (x, 1 - y, c), (1 - x, 1 - y, c)]


def _gather_start(bufs, name):
    n = len(bufs)

    def body(*refs):
        buf_refs, rest = refs[:n], refs[n:]
        sems, token = rest[:2 * n], rest[-1]
        me = _linear(_mesh_position())
        for a in range(n):
            for k, peer in enumerate(_same_core_peers()):
                pltpu.make_async_remote_copy(
                    src_ref=buf_refs[a].at[me], dst_ref=buf_refs[a].at[me],
                    send_sem=sems[2 * a].at[k], recv_sem=sems[2 * a + 1].at[k],
                    device_id=peer, device_id_type=MESH_ID).start()
        token[...] = jnp.zeros_like(token)

    outs = pl.pallas_call(
        body, name=name,
        out_shape=tuple(pltpu.SemaphoreType.DMA((4,)) for _ in range(2 * n))
        + tuple(pltpu.HBM(b.shape, b.dtype) for b in bufs) + (jax.ShapeDtypeStruct((8, 128), F32),),
        in_specs=(HBM_SPEC,) * n,
        out_specs=(SEM_SPEC,) * (2 * n) + (HBM_SPEC,) * n + (pl.BlockSpec(memory_space=pltpu.VMEM),),
        input_output_aliases={a: 2 * n + a for a in range(n)},
        compiler_params=pltpu.CompilerParams(has_side_effects=DATAFLOW),
    )(*[pltpu.with_memory_space_constraint(b, pltpu.HBM) for b in bufs])
    return outs[:2 * n], outs[2 * n:3 * n], outs[3 * n]


def _gather_wait(sems, bufs, after, name):
    n = len(bufs)

    def body(*refs):
        buf_refs, sem_refs = refs[:n], refs[n:3 * n]
        x, y, c = _mesh_position()
        me = _linear((x, y, c))
        for a in range(n):
            for k, peer in enumerate(_same_core_peers()):
                cp = pltpu.make_async_remote_copy(
                    src_ref=buf_refs[a].at[me], dst_ref=buf_refs[a].at[_linear(peer)],
                    send_sem=sem_refs[2 * a].at[k], recv_sem=sem_refs[2 * a + 1].at[k],
                    device_id=peer, device_id_type=MESH_ID)
                cp.wait_send()
                cp.wait_recv()

    return list(pl.pallas_call(
        body, name=name,
        out_shape=tuple(pltpu.HBM(b.shape, b.dtype) for b in bufs),
        in_specs=(HBM_SPEC,) * n + (SEM_SPEC,) * (2 * n) + (pl.BlockSpec(memory_space=pl.ANY),) * len(after),
        out_specs=(HBM_SPEC,) * n, input_output_aliases={a: a for a in range(n)},
        compiler_params=pltpu.CompilerParams(has_side_effects=DATAFLOW),
    )(*bufs, *sems, *after))


def _gather_pass_on(bufs, name):
    n = len(bufs)

    def body(*refs):
        out_refs = refs[n:2 * n]
        send_sems, recv_sems = refs[2 * n:]
        x, y, c = _mesh_position()
        sibling = (x, y, 1 - c)
        chips = [(1 - x, y), (x, 1 - y), (1 - x, 1 - y)]
        copies = []
        for a in range(n):
            for j, chip in enumerate(chips):
                block = out_refs[a].at[_linear((*chip, c))]
                copies.append(pltpu.make_async_remote_copy(
                    src_ref=block, dst_ref=block, send_sem=send_sems.at[3 * a + j], recv_sem=recv_sems.at[3 * a + j],
                    device_id=sibling, device_id_type=MESH_ID))
                copies[-1].start()
        for a in range(n):
            for j, chip in enumerate(chips):
                copies[3 * a + j].wait_send()
                theirs = out_refs[a].at[_linear((*chip, 1 - c))]
                pltpu.make_async_remote_copy(
                    src_ref=theirs, dst_ref=theirs, send_sem=send_sems.at[3 * a + j], recv_sem=recv_sems.at[3 * a + j],
                    device_id=sibling, device_id_type=MESH_ID).wait_recv()

    hbm = pl.BlockSpec(memory_space=pl.ANY)
    return list(pl.pallas_call(
        body, name=name,
        out_shape=[jax.ShapeDtypeStruct(b.shape, b.dtype) for b in bufs],
        in_specs=[hbm] * n, out_specs=[hbm] * n, input_output_aliases={a: a for a in range(n)},
        scratch_shapes=[pltpu.SemaphoreType.DMA((3 * n,)), pltpu.SemaphoreType.DMA((3 * n,))],
    )(*bufs))


def _open_step(c, conv_w, w_ada, b_cols, w_in_t, later, rel_bias, bucket):
    cols = w_ada.shape[1]
    n_later = len(later)

    def body(c_ref, cw_ref, wa_ref, b_ref, w_ref, *rest):
        later_refs, rb_ref, bk_ref = rest[:n_later], rest[n_later], rest[n_later + 1]
        cond_ref, conv_ref, mod_ref, win_ref = rest[n_later + 2:n_later + 6]
        staged_refs, bias_ref = rest[n_later + 6:2 * n_later + 6], rest[2 * n_later + 6]
        cond_own, mod_own, stage = rest[2 * n_later + 7:2 * n_later + 10]
        later_stage = rest[2 * n_later + 10:3 * n_later + 10]
        s_send, s_recv, w_send, w_recv, local_sems = rest[3 * n_later + 10:]
        x, y, cc = _mesh_position()
        me = _linear((x, y, cc))
        sibling = (x, y, 1 - cc)
        chips = [(1 - x, y), (x, 1 - y), (1 - x, 1 - y)]
        v = c_ref[...]
        cond_own[...] = v * _sigmoid(v)
        stage[...] = w_ref[...].astype(BF16)

        def small(rnd, a, k, src, dst, slot):
            return pltpu.make_async_remote_copy(
                src_ref=src, dst_ref=dst.at[slot], send_sem=s_send.at[rnd, a, k - 1], recv_sem=s_recv.at[rnd, a, k - 1],
                device_id=_peer(k), device_id_type=MESH_ID)

        def block(p):
            return win_ref.at[_linear(p)]

        def big(k, blk, to, src=None):
            return pltpu.make_async_remote_copy(
                src_ref=block(blk) if src is None else src, dst_ref=block(blk),
                send_sem=w_send.at[k], recv_sem=w_recv.at[k], device_id=to, device_id_type=MESH_ID)

        mine = [pltpu.make_async_copy(cond_own, cond_ref.at[me], local_sems.at[0]),
                pltpu.make_async_copy(cw_ref, conv_ref.at[me], local_sems.at[1]),
                pltpu.make_async_copy(stage, block((x, y, cc)), local_sems.at[2])]
        for cp in mine:
            cp.start()
        sends = []
        for k in range(1, N_DEV):
            sends += [small(0, 0, k, cond_own, cond_ref, me), small(0, 1, k, cw_ref, conv_ref, me)]
        for cp in sends:
            cp.start()
        first = [big(0, (x, y, cc), sibling, src=stage)]
        first += [big(1 + j, (x, y, cc), (*chip, cc), src=stage) for j, chip in enumerate(chips)]
        for cp in first:
            cp.start()
        for a in range(n_later):
            later_stage[a][...] = later_refs[a][...].astype(BF16)
            mine.append(pltpu.make_async_copy(later_stage[a], staged_refs[a].at[me], local_sems.at[4 + a]))
            mine[-1].start()
        _fill_bias_table(rb_ref, bk_ref, bias_ref)
        for k in range(1, N_DEV):
            small(0, 0, k, cond_own, cond_ref, _linear(_peer(k))).wait_recv()
            small(0, 1, k, cw_ref, conv_ref, _linear(_peer(k))).wait_recv()
        mine[0].wait()
        cond_all = jnp.concatenate([cond_ref[k] for k in range(N_DEV)], axis=0)
        mod_own[...] = _dot(cond_all, wa_ref[...]) + b_ref[...]
        mine.append(pltpu.make_async_copy(mod_own, mod_ref.at[me], local_sems.at[3]))
        mine[-1].start()
        second = [small(1, 0, k, mod_own, mod_ref, me) for k in range(1, N_DEV)]
        for cp in second:
            cp.start()
        passed = []
        for j, chip in enumerate(chips):
            big(1 + j, (*chip, cc), (x, y, cc)).wait_recv()
            fwd = big(4 + j, (*chip, cc), sibling)
            fwd.start()
            passed.append(fwd)
        big(0, sibling, (x, y, cc)).wait_recv()
        for j, chip in enumerate(chips):
            big(4 + j, (*chip, 1 - cc), (x, y, cc)).wait_recv()
        for k in range(1, N_DEV):
            small(1, 0, k, mod_own, mod_ref, _linear(_peer(k))).wait_recv()
        for cp in sends + first + second + passed:
            cp.wait_send()
        for cp in mine[1:]:
            cp.wait()

    vmem = pl.BlockSpec(memory_space=pltpu.VMEM)
    outs = pl.pallas_call(
        body, name="open_step",
        out_shape=[jax.ShapeDtypeStruct((N_DEV,) + c.shape, F32), jax.ShapeDtypeStruct((N_DEV,) + conv_w.shape, F32),
                   jax.ShapeDtypeStruct((N_DEV, N_DEV, cols), F32),
                   jax.ShapeDtypeStruct((N_DEV,) + w_in_t.shape, BF16)]
        + [jax.ShapeDtypeStruct((N_DEV,) + a.shape, BF16) for a in later]
        + [jax.ShapeDtypeStruct((N_Q_HEADS, BLOCK, 2 * BLOCK), F32)],
        in_specs=[vmem] * (5 + n_later) + [pl.BlockSpec(memory_space=pltpu.SMEM), vmem],
        out_specs=[vmem, vmem, vmem, ANY_SPEC] + [ANY_SPEC] * n_later + [vmem],
        scratch_shapes=[pltpu.VMEM(c.shape, F32), pltpu.VMEM((N_DEV, cols), F32), pltpu.VMEM(w_in_t.shape, BF16)]
        + [pltpu.VMEM(a.shape, BF16) for a in later]
        + [pltpu.SemaphoreType.DMA((2, 2, N_DEV - 1)), pltpu.SemaphoreType.DMA((2, 2, N_DEV - 1)),
           pltpu.SemaphoreType.DMA((7,)), pltpu.SemaphoreType.DMA((7,)),
           pltpu.SemaphoreType.DMA((4 + n_later,))],
        compiler_params=_params(vmem=VMEM_LIMIT_LARGE),
    )(c, conv_w, w_ada, b_cols, w_in_t, *later, rel_bias, bucket)
    return outs[0], outs[1], outs[2], outs[3], list(outs[4:4 + n_later]), outs[4 + n_later]


def _in_proj(x, mod, g_norm1, w_in, tm):
    s = x.shape[0]

    def body(x_ref, mod_ref, g_ref, w_ref, h_ref, q_ref, kv_ref, gb_ref, gc_ref, xc_ref):
        xf = x_ref[...]
        n = xf * _rsqrt_mean_sq(xf) * g_ref[...]
        h = (n * (1.0 + mod_ref[SC1:SC1 + 1, :]) + mod_ref[SH1:SH1 + 1, :]).astype(BF16)
        h_ref[...] = h
        p = _dot_nt(h, w_ref[...])
        q_ref[...] = p[:, 0:512].astype(BF16)
        kv_ref[...] = p[:, 512:768].astype(BF16)
        gb_ref[...] = p[:, 768:1280].astype(BF16)
        gc_ref[...] = p[:, 1280:1792].astype(BF16)
        xc_ref[...] = p[:, 1792:2304].astype(BF16)

    return pl.pallas_call(
        body, name="in_proj", grid=(s // tm,),
        in_specs=[_rows(tm, D_MODEL), _full((8, D_MODEL)), _full((1, D_MODEL)), _full((IN_PROJ_WIDTH, D_MODEL))],
        out_specs=[_rows(tm, D_MODEL), _rows(tm, 512), _rows(tm, 256), _rows(tm, 512), _rows(tm, 512), _rows(tm, 512)],
        out_shape=[jax.ShapeDtypeStruct((s, D_MODEL), BF16), jax.ShapeDtypeStruct((s, 512), BF16),
                   jax.ShapeDtypeStruct((s, 256), BF16), jax.ShapeDtypeStruct((s, 512), BF16),
                   jax.ShapeDtypeStruct((s, 512), BF16), jax.ShapeDtypeStruct((s, 512), BF16)],
        compiler_params=_params(("arbitrary",), VMEM_LIMIT_LARGE),
    )(x, mod, g_norm1, w_in)


def _t5_bucket(dist):
    max_exact = N_BUCKETS // 2
    is_small = dist < max_exact
    d = jnp.maximum(dist, 1).astype(F32)
    large = max_exact + (jnp.log(d / max_exact) / math.log(MAX_DISTANCE / max_exact)
                         * (N_BUCKETS - max_exact)).astype(jnp.int32)
    large = jnp.minimum(large, N_BUCKETS - 1)
    return jnp.where(is_small, dist, large)


def _bucket_table():
    qi = jnp.arange(BLOCK, dtype=jnp.int32)[:, None]
    sj = jnp.arange(2 * BLOCK, dtype=jnp.int32)[None, :]
    return _t5_bucket(jnp.maximum(qi + BLOCK - sj, 0))


def _window_mask():
    qi = lax.broadcasted_iota(jnp.int32, (BLOCK, 2 * BLOCK), 0)
    sj = lax.broadcasted_iota(jnp.int32, (BLOCK, 2 * BLOCK), 1)
    dist = qi + BLOCK - sj
    return (dist >= 0) & (dist < BLOCK)


def _fill_bias_table(rb_ref, bk_ref, o_ref):
    bk = bk_ref[...]
    inside = _window_mask()
    for h in range(N_Q_HEADS):
        acc = jnp.zeros((BLOCK, 2 * BLOCK), F32)
        for b in range(N_BUCKETS):
            acc = jnp.where(bk == b, rb_ref[b, h], acc)
        o_ref[h] = jnp.where(inside, acc, NEG_INF)


def _load_kv_window(kv_ref, n):
    prev = jnp.maximum(n - 1, 0)
    kvw = jnp.concatenate([kv_ref[pl.ds(pl.multiple_of(prev * BLOCK, BLOCK), BLOCK), :],
                           kv_ref[pl.ds(pl.multiple_of(n * BLOCK, BLOCK), BLOCK), :]], axis=0)
    k, v = kvw[:, 0:128], kvw[:, 128:256]
    k_sw = pltpu.roll(k.astype(F32), 64, 1).astype(BF16)
    v_sw = pltpu.roll(v.astype(F32), 64, 1).astype(BF16)
    return (k, k_sw), (v, v_sw)


def _conv_taps(gc, xc, gc_prev, xc_prev, n):
    u = gc * xc
    before = jnp.where(n > 0, gc_prev.astype(F32) * xc_prev.astype(F32), 0.0)
    last = before.shape[0] - 1
    row = lax.broadcasted_iota(jnp.int32, u.shape, 0)
    u1 = jnp.where(row == 0, before[last:last + 1, :], pltpu.roll(u, 1, 0))
    u2 = jnp.where(row == 0, before[last - 1:last, :],
                   jnp.where(row == 1, before[last:last + 1, :], pltpu.roll(u, 2, 0)))
    return u, u1, u2


def _mixer_fwd(q, kv, gb, gc, xc, bias, sinks, conv_w, g_attn, g_conv):
    s = q.shape[0]
    nb = s // BLOCK

    per_step = min(MIXER_BLOCKS, nb)
    tile = per_step * BLOCK

    def one_block(n, slot, before, sink_ref, q_ref, kv_ref, gb_ref, gc_ref, xc_ref, bias_ref, cw_ref, ga_ref,
                  gcv_ref, attn_ref, merged_ref, lse_ref, p_ref):
        rows = slice(slot * BLOCK, (slot + 1) * BLOCK)
        ks, vs = _load_kv_window(kv_ref, n)
        lane = lax.broadcasted_iota(jnp.int32, (BLOCK, BLOCK), 1)
        low = lane < HEAD_DIM
        col = lax.broadcasted_iota(jnp.int32, (BLOCK, 2 * BLOCK), 1)
        no_prev = (col < BLOCK) & (n == 0)
        lse_all = jnp.zeros((BLOCK, BLOCK), F32)
        pairs = []
        for p in range(4):
            qp = q_ref[rows, 128 * p:128 * (p + 1)].astype(F32)
            kvh = p // 2
            res = []
            for e in range(2):
                h = 2 * p + e
                qm = jnp.where(low if e == 0 else ~low, qp, 0.0).astype(BF16)
                sw = 0 if kvh == e else 1
                sc = _dot_nt(qm, ks[sw]) * SCALE + bias_ref[h]
                sc = jnp.where(no_prev, NEG_INF, sc)
                sink = sink_ref[h]
                m = jnp.maximum(jnp.max(sc, axis=-1, keepdims=True), sink)
                pe = jnp.exp(sc - m)
                den = jnp.sum(pe, axis=-1, keepdims=True) + jnp.exp(sink - m)
                pb = (pe * (1.0 / den)).astype(BF16)
                p_ref[slot, h] = pb
                res.append(_dot(pb, vs[sw]))
                lse_all = lse_all + jnp.where(lane == h, m + jnp.log(den), 0.0)
            pairs.append(jnp.where(low, res[0], res[1]))
        attn = jnp.concatenate(pairs, axis=1)
        attn_ref[rows, :] = attn
        lse_ref[rows, :] = lse_all
        u, u1, u2 = _conv_taps(gc_ref[rows, :].astype(F32), xc_ref[rows, :].astype(F32), before[0], before[1], n)
        cw = cw_ref[...]
        cv = gb_ref[rows, :].astype(F32) * (cw[0:1, :] * u2 + cw[1:2, :] * u1 + cw[2:3, :] * u)
        an = attn * _rsqrt_mean_sq(attn) * ga_ref[...]
        cn = cv * _rsqrt_mean_sq(cv) * gcv_ref[...]
        merged_ref[rows, :] = jnp.concatenate([an, cn], axis=1).astype(BF16)

    def body(sink_ref, q_ref, kv_ref, gb_ref, gc_ref, xc_ref, gcp_ref, xcp_ref, *rest):
        step = pl.program_id(0)
        for sub in range(per_step):
            ahead = slice(sub * BLOCK - PREV_ROWS, sub * BLOCK)
            before = (gcp_ref[...], xcp_ref[...]) if sub == 0 else (gc_ref[ahead, :], xc_ref[ahead, :])
            one_block(step * per_step + sub, sub, before, sink_ref, q_ref, kv_ref, gb_ref, gc_ref, xc_ref, *rest)

    blk = lambda w: pl.BlockSpec((tile, w), lambda n: (n, 0))
    prev8 = pl.BlockSpec((PREV_ROWS, 512), lambda n: (jnp.maximum(n * (tile // PREV_ROWS) - 1, 0), 0))
    return pl.pallas_call(
        body, name="mixer_fwd", grid=(nb // per_step,),
        in_specs=[pl.BlockSpec(memory_space=pltpu.SMEM), blk(512), _full((s, 256)), blk(512), blk(512), blk(512),
                  prev8, prev8, _full((N_Q_HEADS, BLOCK, 2 * BLOCK)), _full((3, 512)), _full((1, 512)),
                  _full((1, 512))],
        out_specs=[blk(512), blk(1024), blk(128),
                   pl.BlockSpec((per_step, N_Q_HEADS, BLOCK, 2 * BLOCK), lambda n: (n, 0, 0, 0))],
        out_shape=[jax.ShapeDtypeStruct((s, 512), F32), jax.ShapeDtypeStruct((s, 1024), BF16),
                   jax.ShapeDtypeStruct((s, 128), F32),
                   jax.ShapeDtypeStruct((nb, N_Q_HEADS, BLOCK, 2 * BLOCK), BF16)],
        compiler_params=_params(("arbitrary",)),
    )(sinks, q, kv, gb, gc, xc, gc, xc, bias, conv_w, g_attn, g_conv)


def _out_proj(merged, x, mod, w_out, tm):
    s = x.shape[0]

    def body(m_ref, x_ref, mod_ref, w_ref, o_ref, x1_ref):
        o = _dot(m_ref[...], w_ref[...])
        o_ref[...] = o.astype(BF16)
        x1_ref[...] = x_ref[...] + mod_ref[G1:G1 + 1, :] * o

    return pl.pallas_call(
        body, name="out_proj", grid=(s // tm,),
        in_specs=[_rows(tm, D_MODEL), _rows(tm, D_MODEL), _full((8, D_MODEL)), _full((D_MODEL, D_MODEL))],
        out_specs=[_rows(tm, D_MODEL), _rows(tm, D_MODEL)],
        out_shape=[jax.ShapeDtypeStruct((s, D_MODEL), BF16), jax.ShapeDtypeStruct((s, D_MODEL), F32)],
        compiler_params=_params(("arbitrary",)),
    )(merged, x, mod, w_out)


def _resident(shape):
    nd = len(shape)
    return pl.BlockSpec(shape, lambda *_: (0,) * nd, pipeline_mode=pl.Buffered(1))


def _ffn(x1, o1, merged, mod, g_norm2, w_gu, w_down, w_out, g_final, target, tm):
    s = x1.shape[0]
    chunk = D_FF // FFN_CHUNKS

    def body(x_ref, o1_ref, mg_ref, mod_ref, g_ref, wgu_ref, wd_ref, wo_ref, gf_ref, t_ref,
             h_ref, act_ref, do_ref, dgu_ref, dx1_ref, dwo_ref, dm_ref, small_ref, dwo_acc):
        @pl.when(pl.program_id(0) == 0)
        def _():
            small_ref[...] = jnp.zeros_like(small_ref)
            dwo_acc[...] = jnp.zeros_like(dwo_acc)

        xf = x_ref[...]
        n = xf * _rsqrt_mean_sq(xf) * g_ref[...]
        h = (n * (1.0 + mod_ref[SC2:SC2 + 1, :]) + mod_ref[SH2:SH2 + 1, :]).astype(BF16)
        h_ref[...] = h
        gates, ups, o = [], [], None
        for j in range(FFN_CHUNKS):
            lo = j * chunk
            gate = _dot_nt(h, wgu_ref[lo:lo + chunk, :])
            up = _dot_nt(h, wgu_ref[D_FF + lo:D_FF + lo + chunk, :])
            sg = _sigmoid(gate)
            act = (gate * sg * up).astype(BF16)
            act_ref[:, lo:lo + chunk] = act
            gates.append((up * (sg * (1.0 + gate * (1.0 - sg)))).astype(BF16))
            ups.append((gate * sg).astype(BF16))
            part = _dot(act, wd_ref[lo:lo + chunk, :])
            o = part if o is None else o + part
        g2 = mod_ref[G2:G2 + 1, :]
        x2 = xf + g2 * o
        r = _rsqrt_mean_sq(x2)
        xn = x2 * r
        gf = gf_ref[...]
        err = xn * gf - t_ref[...]
        dy = err * (1.0 / D_MODEL)
        dxn = dy * gf
        dx2 = r * (dxn - xn * jnp.mean(dxn * xn, axis=-1, keepdims=True))
        small_ref[4:5, :] += _colsum(dy * xn)
        small_ref[5:6, :] += _colsum(err * err)
        small_ref[3:4, :] += _colsum(dx2 * o)
        do = (dx2 * g2).astype(BF16)
        do_ref[...] = do
        dh = None
        for j in range(FFN_CHUNKS):
            lo = j * chunk
            dact = _dot_nt(do, wd_ref[lo:lo + chunk, :])
            dgate = (dact * gates[j].astype(F32)).astype(BF16)
            dup = (dact * ups[j].astype(F32)).astype(BF16)
            dgu_ref[:, lo:lo + chunk] = dgate
            dgu_ref[:, D_FF + lo:D_FF + lo + chunk] = dup
            part = _dot(dgate, wgu_ref[lo:lo + chunk, :]) + _dot(dup, wgu_ref[D_FF + lo:D_FF + lo + chunk, :])
            dh = part if dh is None else dh + part
        dx1 = dx2 + _norm_mod_bwd(dh, xf, g_ref[...], mod_ref[SC2:SC2 + 1, :], small_ref)
        dx1_ref[...] = dx1.astype(BF16)
        small_ref[7:8, :] += _colsum(dx1 * o1_ref[...].astype(F32))
        do1 = (dx1 * mod_ref[G1:G1 + 1, :]).astype(BF16)
        dm_ref[...] = _dot_nt(do1, wo_ref[...]).astype(BF16)
        dwo = dwo_acc[...] + _dot_tn(mg_ref[...], do1)
        dwo_acc[...] = dwo
        dwo_ref[...] = dwo.astype(BF16)

        @pl.when(pl.program_id(0) == pl.num_programs(0) - 1)
        def _():
            total = jnp.sum(small_ref[5:6, :], axis=-1, keepdims=True) * (0.5 / D_MODEL)
            small_ref[6:7, :] = jnp.broadcast_to(total, (1, D_MODEL))

    narrow = jax.ShapeDtypeStruct((s, D_MODEL), BF16)
    return pl.pallas_call(
        body, name="ffn", grid=(s // tm,),
        in_specs=[_rows(tm, D_MODEL), _rows(tm, D_MODEL), _rows(tm, D_MODEL), _full((8, D_MODEL)), _full((1, D_MODEL)),
                  _resident((2 * D_FF, D_MODEL)), _resident((D_FF, D_MODEL)), _resident((D_MODEL, D_MODEL)),
                  _full((1, D_MODEL)), _rows(tm, D_MODEL)],
        out_specs=[_rows(tm, D_MODEL), _rows(tm, D_FF), _rows(tm, D_MODEL), _rows(tm, 2 * D_FF), _rows(tm, D_MODEL),
                   _full((D_MODEL, D_MODEL)), _rows(tm, D_MODEL), _full((8, D_MODEL))],
        out_shape=[narrow, jax.ShapeDtypeStruct((s, D_FF), BF16), narrow, jax.ShapeDtypeStruct((s, 2 * D_FF), BF16),
                   narrow, jax.ShapeDtypeStruct((D_MODEL, D_MODEL), BF16), narrow,
                   jax.ShapeDtypeStruct((8, D_MODEL), F32)],
        scratch_shapes=[pltpu.VMEM((D_MODEL, D_MODEL), F32)],
        compiler_params=_params(("arbitrary",), VMEM_LIMIT_LARGE),
    )(x1, o1, merged, mod, g_norm2, w_gu, w_down, w_out, g_final, target)


def _norm_mod_bwd(dh, xf, g, scale_row, small_ref):
    r = _rsqrt_mean_sq(xf)
    xn = xf * r
    small_ref[0:1, :] += _colsum(dh)
    small_ref[1:2, :] += _colsum(dh * (xn * g))
    dn = dh * (1.0 + scale_row)
    small_ref[2:3, :] += _colsum(dn * xn)
    dxn = dn * g
    return r * (dxn - xn * jnp.mean(dxn * xn, axis=-1, keepdims=True))


def _group_norm_bwd(dm, a, g):
    r = _rsqrt_mean_sq(a)
    an = a * r
    dan = dm * g
    return r * (dan - an * jnp.mean(dan * an, axis=-1, keepdims=True)), _colsum(dm * an)


def _mixer_bwd(after, q, kv, gb, gc, xc, probs, sinks, conv_w, g_attn, g_conv, attn, lse, dmerged):
    s = q.shape[0]
    nb = s // BLOCK

    per_step = min(MIXER_BLOCKS, nb)
    tile = per_step * BLOCK
    steps = nb // per_step

    def one_block(n, slot, before, nxt, sink_ref, q_ref, kv_ref, gb_ref, gc_ref, xc_ref, p_ref, cw_ref, ga_ref,
                  gcv_ref, attn_ref, lse_ref, dm_ref, dproj_ref, dbias_ref, dsink_ref, small_ref):
        rows = slice(slot * BLOCK, (slot + 1) * BLOCK)
        next_dy, next_dkv = nxt
        dm = dm_ref[rows, :].astype(F32)
        gbv, gcv_, xcv = gb_ref[rows, :].astype(F32), gc_ref[rows, :].astype(F32), xc_ref[rows, :].astype(F32)
        u, u1, u2 = _conv_taps(gcv_, xcv, before[0], before[1], n)
        cw = cw_ref[...]
        yv = cw[0:1, :] * u2 + cw[1:2, :] * u1 + cw[2:3, :] * u
        dcv, dg_conv = _group_norm_bwd(dm[:, 512:1024], gbv * yv, gcv_ref[...])
        small_ref[1:2, :] += dg_conv
        dproj_ref[rows, 768:1280] = (dcv * yv).astype(BF16)
        dy = dcv * gbv
        row = lax.broadcasted_iota(jnp.int32, dy.shape, 0)
        d1 = jnp.where(row == BLOCK - 1, next_dy[0:1, :], pltpu.roll(dy, BLOCK - 1, 0))
        d2 = jnp.where(row == BLOCK - 2, next_dy[0:1, :],
                       jnp.where(row == BLOCK - 1, next_dy[1:2, :], pltpu.roll(dy, BLOCK - 2, 0)))
        du = cw[2:3, :] * dy + cw[1:2, :] * d1 + cw[0:1, :] * d2
        dproj_ref[rows, 1280:1792] = (du * xcv).astype(BF16)
        dproj_ref[rows, 1792:2304] = (du * gcv_).astype(BF16)
        small_ref[2:3, :] += _colsum(dy * u2)
        small_ref[3:4, :] += _colsum(dy * u1)
        small_ref[4:5, :] += _colsum(dy * u)

        attn_v = attn_ref[rows, :]
        dout, dg_attn = _group_norm_bwd(dm[:, 0:512], attn_v, ga_ref[...])
        small_ref[0:1, :] += dg_attn
        ks, vs = _load_kv_window(kv_ref, n)
        lane = lax.broadcasted_iota(jnp.int32, (BLOCK, BLOCK), 1)
        low = lane < HEAD_DIM
        lse_all = lse_ref[rows, :]
        dsink = jnp.zeros((BLOCK, BLOCK), F32)
        dq_pairs = []
        dk_groups, dv_groups = [], []
        for kvh in range(2):
            ds_rows, pr_rows, q_rows, do_rows = [], [], [], []
            for p in (2 * kvh, 2 * kvh + 1):
                qp = q_ref[rows, 128 * p:128 * (p + 1)].astype(F32)
                do_p = dout[:, 128 * p:128 * (p + 1)]
                prod = do_p * attn_v[:, 128 * p:128 * (p + 1)]
                res = []
                for e in range(2):
                    h = 2 * p + e
                    half = low if e == 0 else ~low
                    qm = jnp.where(half, qp, 0.0).astype(BF16)
                    dom = jnp.where(half, do_p, 0.0).astype(BF16)
                    delta = jnp.sum(jnp.where(half, prod, 0.0), axis=-1, keepdims=True)
                    lse_h = jnp.sum(jnp.where(lane == h, lse_all, 0.0), axis=-1, keepdims=True)
                    sw = 0 if kvh == e else 1
                    pb = p_ref[slot, h]
                    dp = _dot_nt(dom, vs[sw])
                    ds = pb.astype(F32) * (dp - delta)
                    dbias_ref[h] += ds
                    dsink = dsink + jnp.where(lane == h, -jnp.exp(sink_ref[h] - lse_h) * delta, 0.0)
                    dsb = ds.astype(BF16)
                    res.append(_dot(dsb, ks[sw]) * SCALE)
                    ds_rows.append(dsb)
                    pr_rows.append(pb)
                    q_rows.append(qm)
                    do_rows.append(dom)
                dq_pairs.append(jnp.where(low, res[0], res[1]))
            dk_g = _dot_tn(jnp.concatenate(ds_rows, axis=0), jnp.concatenate(q_rows, axis=0)) * SCALE
            dv_g = _dot_tn(jnp.concatenate(pr_rows, axis=0), jnp.concatenate(do_rows, axis=0))
            dk_groups.append(dk_g + pltpu.roll(dk_g, 64, 1))
            dv_groups.append(dv_g + pltpu.roll(dv_g, 64, 1))
        dproj_ref[rows, 0:512] = jnp.concatenate(dq_pairs, axis=1).astype(BF16)
        dsink_ref[...] += dsink
        low_kv = lax.broadcasted_iota(jnp.int32, (2 * BLOCK, BLOCK), 1) < HEAD_DIM
        dkv_win = jnp.concatenate([jnp.where(low_kv, dk_groups[0], dk_groups[1]),
                                   jnp.where(low_kv, dv_groups[0], dv_groups[1])], axis=1)
        dproj_ref[rows, 512:768] = (dkv_win[BLOCK:2 * BLOCK, :] + next_dkv).astype(BF16)
        return dy[0:8, :], dkv_win[0:BLOCK, :]

    def body(sink_ref, q_ref, kv_ref, gb_ref, gc_ref, xc_ref, gcp_ref, xcp_ref, *rest):
        refs, dy_ref, dkv_ref = rest[:-2], rest[-2], rest[-1]
        dbias_ref, dsink_ref, small_ref = refs[8], refs[9], refs[10]
        step = pl.program_id(0)

        @pl.when(step == 0)
        def _():
            dbias_ref[...] = jnp.zeros_like(dbias_ref)
            dsink_ref[...] = jnp.zeros_like(dsink_ref)
            small_ref[...] = jnp.zeros_like(small_ref)
            dy_ref[...] = jnp.zeros_like(dy_ref)
            dkv_ref[...] = jnp.zeros_like(dkv_ref)

        nxt = (dy_ref[...], dkv_ref[...])
        for sub in reversed(range(per_step)):
            ahead = slice(sub * BLOCK - PREV_ROWS, sub * BLOCK)
            before = (gcp_ref[...], xcp_ref[...]) if sub == 0 else (gc_ref[ahead, :], xc_ref[ahead, :])
            nxt = one_block((steps - 1 - step) * per_step + sub, sub, before, nxt,
                            sink_ref, q_ref, kv_ref, gb_ref, gc_ref, xc_ref, *refs)
        dy_ref[...], dkv_ref[...] = nxt

        @pl.when(step == steps - 1)
        def _():
            small_ref[5:6, :] = jnp.concatenate([_colsum(dsink_ref[...]), jnp.zeros((1, 512 - BLOCK), F32)], axis=1)

    blk = lambda w: pl.BlockSpec((tile, w), lambda t: (steps - 1 - t, 0))
    prev8 = pl.BlockSpec((PREV_ROWS, 512),
                         lambda t: (jnp.maximum((steps - 1 - t) * (tile // PREV_ROWS) - 1, 0), 0))
    bf = lambda w: jax.ShapeDtypeStruct((s, w), BF16)
    return pl.pallas_call(
        _coming_behind(body), name="mixer_bwd", grid=(steps,),
        in_specs=[ANY_SPEC, pl.BlockSpec(memory_space=pltpu.SMEM), blk(512), _full((s, 256)), blk(512), blk(512), blk(512),
                  prev8, prev8,
                  pl.BlockSpec((per_step, N_Q_HEADS, BLOCK, 2 * BLOCK), lambda t: (steps - 1 - t, 0, 0, 0)),
                  _full((3, 512)), _full((1, 512)), _full((1, 512)), blk(512), blk(128), blk(1024)],
        out_specs=[blk(IN_PROJ_WIDTH), _full((N_Q_HEADS, BLOCK, 2 * BLOCK)), _full((BLOCK, BLOCK)), _full((8, 512))],
        out_shape=[bf(IN_PROJ_WIDTH), jax.ShapeDtypeStruct((N_Q_HEADS, BLOCK, 2 * BLOCK), F32),
                   jax.ShapeDtypeStruct((BLOCK, BLOCK), F32), jax.ShapeDtypeStruct((8, 512), F32)],
        scratch_shapes=[pltpu.VMEM((8, 512), F32), pltpu.VMEM((BLOCK, 2 * KV_WIDTH), F32)],
        compiler_params=_params(("arbitrary",), VMEM_LIMIT_LARGE),
    )(after, sinks, q, kv, gb, gc, xc, gc, xc, probs, conv_w, g_attn, g_conv, attn, lse, dmerged)


def _in_proj_bwd(after, dproj, x, dx1, mod, g_norm1, w_in, tm):
    s = x.shape[0]

    def body(dproj_ref, x_ref, dx1_ref, mod_ref, g_ref, w_ref, dx_ref, small_ref):
        @pl.when(pl.program_id(0) == 0)
        def _():
            small_ref[...] = jnp.zeros_like(small_ref)

        dh = _dot(dproj_ref[...], w_ref[...])
        dx_ref[...] = dx1_ref[...].astype(F32) + _norm_mod_bwd(dh, x_ref[...], g_ref[...], mod_ref[SC1:SC1 + 1, :],
                                                               small_ref)

    return pl.pallas_call(
        _coming_behind(body), name="in_proj_bwd", grid=(s // tm,),
        in_specs=[ANY_SPEC, _rows(tm, IN_PROJ_WIDTH), _rows(tm, D_MODEL), _rows(tm, D_MODEL), _full((8, D_MODEL)),
                  _full((1, D_MODEL)), _full((IN_PROJ_WIDTH, D_MODEL))],
        out_specs=[_rows(tm, D_MODEL), _full((8, D_MODEL))],
        out_shape=[jax.ShapeDtypeStruct((s, D_MODEL), F32), jax.ShapeDtypeStruct((8, D_MODEL), F32)],
        compiler_params=_params(("arbitrary",), VMEM_LIMIT_LARGE),
    )(after, dproj, x, dx1, mod, g_norm1, w_in)


def _weight_grad(a, b, tk, ts, name, after=None):
    s, k = a.shape
    n = b.shape[1]
    nt = s // ts
    extra = [] if after is None else [after]

    def body(a_ref, b_ref, *rest):
        o_ref, acc_ref = rest[-2:]
        t = pl.program_id(1)
        @pl.when(t == 0)
        def _():
            acc_ref[...] = jnp.zeros_like(acc_ref)

        acc = acc_ref[...] + _dot_tn(a_ref[...], b_ref[...])
        acc_ref[...] = acc
        o_ref[...] = acc.astype(BF16)

    return pl.pallas_call(
        body, name=name, grid=(k // tk, nt),
        in_specs=[pl.BlockSpec((ts, tk), lambda i, t: (t, i)), pl.BlockSpec((ts, n), lambda i, t: (t, 0))]
        + [ANY_SPEC] * len(extra),
        out_specs=pl.BlockSpec((tk, n), lambda i, t: (i, 0)),
        out_shape=jax.ShapeDtypeStruct((k, n), BF16),
        scratch_shapes=[pltpu.VMEM((tk, n), F32)],
        compiler_params=_params(("arbitrary", "arbitrary"), VMEM_LIMIT_LARGE),
    )(a, b, *extra)


def _rel_bias_grad(dbias, bucket):
    def body(db_ref, bk_ref, o_ref, rows_ref):
        bk = bk_ref[...]
        for b in range(N_BUCKETS):
            sel = (bk == b).astype(F32)
            for h in range(N_Q_HEADS):
                rows_ref[N_BUCKETS * h + b:N_BUCKETS * h + b + 1, :] = _colsum(db_ref[h] * sel)
        head = lax.broadcasted_iota(jnp.int32, (N_BUCKETS, N_Q_HEADS), 1)
        out = jnp.zeros((N_BUCKETS, N_Q_HEADS), F32)
        for h in range(N_Q_HEADS):
            per_bucket = jnp.sum(rows_ref[N_BUCKETS * h:N_BUCKETS * (h + 1), :], axis=-1, keepdims=True)
            out = out + jnp.where(head == h, per_bucket, 0.0)
        o_ref[...] = out

    return pl.pallas_call(
        body, name="rel_bias_grad",
        out_shape=jax.ShapeDtypeStruct((N_BUCKETS, N_Q_HEADS), F32),
        scratch_shapes=[pltpu.VMEM((N_BUCKETS * N_Q_HEADS, 2 * BLOCK), F32)],
    )(dbias, bucket)


def _lanes_from(x, start, width):
    n = x.shape[1]
    return pltpu.roll(x, (n - start) % n, 1)[:, 0:width]


def _adamw_w_ada(me, cond_all, packed_all, w, m, v, tr):
    r, cols = w.shape

    def body(me_ref, c_ref, p_ref, w_ref, m_ref, v_ref, g_ref, d_ref, mo_ref, vo_ref):
        dmod = jnp.concatenate([p_ref[k][:, OFF_DMOD:OFF_DMOD + N_MOD * D_MODEL] for k in range(N_DEV)], axis=0)
        mine = _lanes_from(dmod, me_ref[0] * cols, cols)
        pad = lambda a: jnp.concatenate([a, jnp.zeros((128 - N_DEV, a.shape[1]), F32)], axis=0)
        g = _dot_tn(pad(c_ref[...]), pad(mine))
        g_ref[...] = g
        d_ref[...], mo_ref[...], vo_ref[...] = _adam_math(w_ref[...], g, m_ref[...], v_ref[...])

    tile = pl.BlockSpec((tr, cols), lambda i, me_ref: (i, 0))
    return pl.pallas_call(
        body, name="adamw_w_ada",
        grid_spec=pltpu.PrefetchScalarGridSpec(
            num_scalar_prefetch=1, grid=(r // tr,),
            in_specs=[pl.BlockSpec((N_DEV, tr), lambda i, me_ref: (0, i)),
                      pl.BlockSpec(packed_all.shape, lambda i, me_ref: (0, 0, 0)), tile, tile, tile],
            out_specs=[tile] * 4),
        out_shape=[jax.ShapeDtypeStruct((r, cols), F32)] * 4,
        compiler_params=_params(("arbitrary",)),
    )(me, cond_all, packed_all, w, m, v)


SMALL_PARAMS = (("rel_bias", None), ("b_ada", (OFF_DMOD, N_MOD * D_MODEL)), ("g_norm1", (OFF_GN1, D_MODEL)),
                ("sinks", (OFF_SINK, N_Q_HEADS)), ("conv_w", None), ("g_attn_out", (OFF_GATT, ATTN_WIDTH)),
                ("g_conv_out", (OFF_GCV, CONV_WIDTH)), ("g_norm2", (OFF_GN2, D_MODEL)),
                ("g_final", (OFF_GFIN, D_MODEL)))


def _small_update(me, packed_all, rel_all, state, after):
    n_p = len(SMALL_PARAMS)
    flat = [a for triple in state for a in triple]
    conv_cols = state[4][0].shape[1]

    def body(me_ref, p_ref, r_ref, *refs):
        ins = refs[:3 * n_p]
        loss_ref, outs = refs[3 * n_p + len(after)], refs[3 * n_p + len(after) + 1:]
        small, rel = p_ref[0], r_ref[0]
        for k in range(1, N_DEV):
            small = small + p_ref[k]
            rel = rel + r_ref[k]
        loss_ref[...] = small[:, OFF_LOSS:OFF_LOSS + 128]
        taps = jnp.concatenate([small[:, OFF_CONVW + CONV_WIDTH * j:OFF_CONVW + CONV_WIDTH * (j + 1)]
                                for j in range(3)] + [jnp.zeros((5, CONV_WIDTH), F32)], axis=0)
        conv_g = _lanes_from(taps, me_ref[0] * conv_cols, conv_cols)[0:3, :]
        for i, (name, lanes) in enumerate(SMALL_PARAMS):
            g = rel if name == "rel_bias" else conv_g if name == "conv_w" else small[:, lanes[0]:lanes[0] + lanes[1]]
            w_ref, m_ref, v_ref = ins[3 * i:3 * i + 3]
            outs[4 * i][...] = g
            outs[4 * i + 1][...], outs[4 * i + 2][...], outs[4 * i + 3][...] = _adam_math(
                w_ref[...], g, m_ref[...], v_ref[...])

    vmem = pl.BlockSpec(memory_space=pltpu.VMEM)
    out_shape = [jax.ShapeDtypeStruct((1, 128), F32)]
    for w, _, _ in state:
        out_shape += [jax.ShapeDtypeStruct(w.shape, F32)] * 4
    outs = pl.pallas_call(
        body, name="small_update",
        in_specs=[pl.BlockSpec(memory_space=pltpu.SMEM), vmem, vmem] + [vmem] * len(flat)
        + [pl.BlockSpec(memory_space=pl.ANY)] * len(after),
        out_shape=out_shape,
    )(me, packed_all, rel_all, *flat, *after)
    return outs[0], [tuple(outs[1 + 4 * i:5 + 4 * i]) for i in range(n_p)]


def _adam_math(w, g, m, v):
    m = ADAM_B1 * m + (1.0 - ADAM_B1) * g
    v = ADAM_B2 * v + (1.0 - ADAM_B2) * (g * g)
    m_hat = m / (1.0 - ADAM_B1 ** ADAM_STEP)
    v_hat = v / (1.0 - ADAM_B2 ** ADAM_STEP)
    delta = -ADAM_LR * (m_hat / (jnp.sqrt(v_hat) + ADAM_EPS) + ADAM_WD * w)
    return delta, m, v


def _adamw_parts(w, m, v, local, land, me, tr, name):
    r, c = w.shape

    def body(me_ref, w_ref, m_ref, v_ref, own_ref, land_ref, g_ref, d_ref, mo_ref, vo_ref):
        g = own_ref[0].astype(F32)
        for k in range(N_DEV - 1):
            g = g + land_ref[k].astype(F32)
        g_ref[...] = g
        d_ref[...], mo_ref[...], vo_ref[...] = _adam_math(w_ref[...], g, m_ref[...], v_ref[...])

    tile = pl.BlockSpec((tr, c), lambda i, me_ref: (i, 0))
    return pl.pallas_call(
        body, name=name,
        grid_spec=pltpu.PrefetchScalarGridSpec(
            num_scalar_prefetch=1, grid=(r // tr,),
            in_specs=[tile, tile, tile, pl.BlockSpec((1, tr, c), lambda i, me_ref: (me_ref[0], i, 0)),
                      pl.BlockSpec((N_DEV - 1, tr, c), lambda i, me_ref: (0, i, 0))],
            out_specs=[tile] * 4),
        out_shape=[jax.ShapeDtypeStruct((r, c), F32)] * 4,
        compiler_params=_params(("arbitrary",)),
    )(me, w, m, v, local, land)


def _behind(a, token):
    return a + token[0:a.shape[0], 0:1]


def _local_step(x, target, mod, w_in_t, bias, weights_out_gu, weights_down, g_norm1, sinks, conv_w, g_attn,
                g_conv, g_norm2, g_final, exchange):
    s = x.shape[0]
    tm = min(512, s)
    tm_small = min(256, s)
    bucket = _bucket_table()

    h, q, kv, gb, gc, xc = _in_proj(x, mod, g_norm1, w_in_t, tm)
    attn, merged, lse, probs = _mixer_fwd(q, kv, gb, gc, xc, bias, sinks, conv_w, g_attn, g_conv)
    w_out, w_gu_t = weights_out_gu(merged)
    o1, x1 = _out_proj(merged, x, mod, w_out, tm)
    w_down = weights_down(x1)
    h2, act, do2, dgu, dx1, dw_out, dmerged, sm_2 = _ffn(x1, o1, merged, mod, g_norm2, w_gu_t, w_down, w_out, g_final,
                                                         target, tm_small)
    ts = min(WEIGHT_GRAD_ROWS, s)
    tok_out = exchange("w_out", dw_out)
    tok_down = exchange("w_down", _weight_grad(act, do2, D_FF // 2, ts, "w_down_grad", after=tok_out))
    tok_gu = exchange("w_gu", _weight_grad(dgu, h2, D_FF // 2, ts, "w_gu_grad", after=tok_down))
    dproj, dbias, dsink, sm_mix = _mixer_bwd(
        tok_gu, q, kv, gb, gc, xc, probs, sinks, conv_w, g_attn, g_conv, attn, lse, dmerged)
    tok_in = exchange("w_in", _weight_grad(dproj, h, IN_PROJ_WIDTH // 2, ts, "w_in_grad"))
    dx, sm_1 = _in_proj_bwd(tok_in, dproj, x, dx1, mod, g_norm1, w_in_t, min(1024, s))
    d_rel = _rel_bias_grad(dbias, bucket)

    packed = jnp.concatenate([
        sm_1[0], sm_1[1], sm_2[7], sm_2[0], sm_2[1], sm_2[3],
        sm_1[2],
        sm_mix[5, 0:128],
        sm_mix[0], sm_mix[1],
        sm_2[2],
        sm_2[4],
        sm_mix[2], sm_mix[3], sm_mix[4],
        sm_2[6, 0:128],
    ])[None, :]
    return dx, packed, d_rel


def kernel(x, c, rel_bias, w_ada, b_ada, g_norm1, w_in, sinks, conv_w, g_attn_out, g_conv_out, w_out, g_norm2, w_gu, w_down, g_final, loss_target, m_rel_bias, m_w_ada, m_b_ada, m_g_norm1, m_w_in, m_sinks, m_conv_w, m_g_attn_out, m_g_conv_out, m_w_out, m_g_norm2, m_w_gu, m_w_down, m_g_final, v_rel_bias, v_w_ada, v_b_ada, v_g_norm1, v_w_in, v_sinks, v_conv_w, v_g_attn_out, v_g_conv_out, v_w_out, v_g_norm2, v_w_gu, v_w_down, v_g_final):
    me = _linear(_mesh_position())
    me_arr = jnp.reshape(me, (1,)).astype(jnp.int32)
    ada_cols = w_ada.shape[2]
    tm = min(512, x.shape[1])

    b_cols = lax.dynamic_slice_in_dim(b_ada, me * ada_cols, ada_cols, axis=1)
    cond_all, conv_w_all, mod_all, w_in_blocks, staged, bias = _open_step(
        c, conv_w[0], w_ada[0], b_cols, w_in[0].T, [w_out[0], w_gu[0].T, w_down[0]], rel_bias, _bucket_table())
    cond_all = cond_all[:, 0, :]
    conv_w_full = conv_w_all.transpose(1, 0, 2).reshape(3, CONV_WIDTH)
    mod = lax.dynamic_index_in_dim(mod_all, me, axis=1, keepdims=False).reshape(N_MOD, D_MODEL)
    mod = jnp.concatenate([mod, jnp.zeros((2, D_MODEL), F32)], axis=0)
    w_in_t = w_in_blocks.reshape(IN_PROJ_WIDTH, D_MODEL)
    gather_sems, staged, gather_token = _gather_start(staged, "gather_start_weights")
    mod = _behind(mod, gather_token)

    def weights_out_gu(after):
        got = _gather_pass_on(_gather_wait(gather_sems[0:4], staged[0:2], [after], "gather_wait_out_gu"),
                              "gather_pass_on_out_gu")
        return got[0].reshape(D_MODEL, D_MODEL), got[1].reshape(2 * D_FF, D_MODEL)

    def weights_down(after):
        got = _gather_pass_on(_gather_wait(gather_sems[4:6], staged[2:3], [after], "gather_wait_down"),
                              "gather_pass_on_down")
        return got[0].reshape(D_FF, D_MODEL)

    started = {}

    def exchange(name, dw):
        st = _exchange_start(dw.reshape(N_DEV, dw.shape[0] // N_DEV, dw.shape[1]), "exchange_start_" + name)
        started[name] = st
        return st[4]

    dx, packed, d_rel = _local_step(
        x[0], loss_target[0], mod, w_in_t, bias, weights_out_gu, weights_down, g_norm1, sinks[0], conv_w_full,
        g_attn_out, g_conv_out, g_norm2, g_final[None, :], exchange)

    def zone(a):
        return lax.dynamic_update_slice(jnp.zeros((N_DEV,) + a.shape, F32), a[None], (me,) + (0,) * a.ndim)

    shared = _share_start([packed, d_rel], [zone(packed), zone(d_rel)], "share_small_start")

    def finish(name, after, w, m, v, tr):
        src, land = _exchange_wait(started[name], after, "exchange_wait_" + name)
        return _adamw_parts(w, m, v, src, land, me_arr, tr, "adamw_" + name)

    g_down, d_down, nm_down, nv_down = finish("w_down", [shared[2][0]], w_down[0], m_w_down[0], v_w_down[0], 176)
    g_gu, d_gu, nm_gu, nv_gu = finish("w_gu", [nv_down], w_gu[0].T, m_w_gu[0].T, v_w_gu[0].T, 352)
    g_out, d_out, nm_out, nv_out = finish("w_out", [nv_gu], w_out[0], m_w_out[0], v_w_out[0], 128)

    packed_all, rel_all = _share_wait(shared, [nv_out], "share_small_wait")
    g_ada, d_ada, nm_ada, nv_ada = _adamw_w_ada(me_arr, cond_all, packed_all, w_ada[0], m_w_ada[0], v_w_ada[0], 256)
    as_rows = {"conv_w": lambda a: a[0], "g_final": lambda a: a[None, :]}
    small_state = {
        "rel_bias": (rel_bias, m_rel_bias, v_rel_bias), "b_ada": (b_ada, m_b_ada, v_b_ada),
        "g_norm1": (g_norm1, m_g_norm1, v_g_norm1), "sinks": (sinks, m_sinks, v_sinks),
        "conv_w": (conv_w, m_conv_w, v_conv_w), "g_attn_out": (g_attn_out, m_g_attn_out, v_g_attn_out),
        "g_conv_out": (g_conv_out, m_g_conv_out, v_g_conv_out), "g_norm2": (g_norm2, m_g_norm2, v_g_norm2),
        "g_final": (g_final, m_g_final, v_g_final),
    }
    state = [tuple(as_rows.get(name, lambda a: a)(a) for a in small_state[name]) for name, _ in SMALL_PARAMS]
    loss_row, small_out = _small_update(me_arr, packed_all, rel_all, state, [])
    loss = loss_row[0, 0]
    small_res = {name: tuple(a.reshape(small_state[name][0].shape) for a in res)
                 for (name, _), res in zip(SMALL_PARAMS, small_out)}

    g_in, d_in, nm_in, nv_in = finish("w_in", [loss_row, nv_ada], w_in[0].T, m_w_in[0].T, v_w_in[0].T, 144)

    big = {
        "w_ada": (g_ada[None], d_ada[None], nm_ada[None], nv_ada[None]),
        "w_in": (g_in.T[None], d_in.T[None], nm_in.T[None], nv_in.T[None]),
        "w_out": (g_out[None], d_out[None], nm_out[None], nv_out[None]),
        "w_gu": (g_gu.T[None], d_gu.T[None], nm_gu.T[None], nv_gu.T[None]),
        "w_down": (g_down[None], d_down[None], nm_down[None], nv_down[None]),
    }
    order = ["rel_bias", "w_ada", "b_ada", "g_norm1", "w_in", "sinks", "conv_w", "g_attn_out", "g_conv_out", "w_out",
             "g_norm2", "w_gu", "w_down", "g_final"]
    results = [big[k] if k in big else small_res[k] for k in order]
    return (loss, dx[None], *[r[0] for r in results], *[r[1] for r in results], *[r[2] for r in results],
            *[r[3] for r in results])
```

```python
import math

import jax
import jax.numpy as jnp
from jax import lax
from jax.experimental import pallas as pl
from jax.experimental.pallas import tpu as pltpu

F32 = jnp.float32
BF16 = jnp.bfloat16

D_MODEL = 1024
HEAD_DIM = 64
N_Q_HEADS = 8
ATTN_WIDTH = 512
KV_WIDTH = 128
CONV_WIDTH = 512
IN_PROJ_WIDTH = 2304
D_FF = 2816
N_MOD = 6
N_BUCKETS = 32
MAX_DISTANCE = 128
BLOCK = 128
EPS = 1e-6
NEG_INF = -1e30
SCALE = HEAD_DIM ** -0.5
N_DEV = 8

ADAM_LR = 0.001
ADAM_B1 = 0.9
ADAM_B2 = 0.999
ADAM_EPS = 1e-08
ADAM_WD = 0.01
ADAM_STEP = 10

SH1, SC1, G1, SH2, SC2, G2 = range(6)

VMEM_LIMIT_LARGE = 60 * 1024 * 1024
WEIGHT_GRAD_ROWS = 2048
FFN_CHUNKS = 1
PREV_ROWS = 16
MIXER_BLOCKS = 4
MESH_ID = pl.DeviceIdType.MESH

OFF_DMOD = 0
OFF_GN1 = OFF_DMOD + N_MOD * D_MODEL
OFF_SINK = OFF_GN1 + D_MODEL
OFF_GATT = OFF_SINK + 128
OFF_GCV = OFF_GATT + ATTN_WIDTH
OFF_GN2 = OFF_GCV + CONV_WIDTH
OFF_GFIN = OFF_GN2 + D_MODEL
OFF_CONVW = OFF_GFIN + D_MODEL
OFF_LOSS = OFF_CONVW + 3 * CONV_WIDTH
PACKED = OFF_LOSS + 128


def _params(sem=None, vmem=None):
    return pltpu.CompilerParams(dimension_semantics=sem, vmem_limit_bytes=vmem)


def _coming_behind(body):
    def skipping(after_ref, *refs):
        body(*refs)

    return skipping


ANY_SPEC = pl.BlockSpec(memory_space=pl.ANY)


def _full(shape):
    nd = len(shape)
    return pl.BlockSpec(shape, lambda *_: (0,) * nd)


def _rows(tm, width):
    return pl.BlockSpec((tm, width), lambda i, *_: (i, 0))


def _sigmoid(x):
    return 1.0 / (1.0 + jnp.exp(-x))


def _rsqrt_mean_sq(x):
    return lax.rsqrt(jnp.mean(x * x, axis=-1, keepdims=True) + EPS)


def _colsum(x):
    return jnp.sum(x, axis=0, keepdims=True)


def _dot(a, b):
    return jnp.dot(a, b, preferred_element_type=F32)


def _dot_nt(a, b):
    return lax.dot_general(a, b, (((1,), (1,)), ((), ())), preferred_element_type=F32)


def _dot_tn(a, b):
    return lax.dot_general(a, b, (((0,), (0,)), ((), ())), preferred_element_type=F32)


def _mesh_position():
    return lax.axis_index("x"), lax.axis_index("y"), lax.axis_index("c")


def _linear(p):
    return 4 * p[0] + 2 * p[1] + p[2]


def _peer(k):
    x, y, c = _mesh_position()
    return (1 - x if k & 4 else x, 1 - y if k & 2 else y, 1 - c if k & 1 else c)


HBM_SPEC = pl.BlockSpec(memory_space=pltpu.HBM)
SEM_SPEC = pl.BlockSpec(memory_space=pltpu.SEMAPHORE)
DATAFLOW = pltpu.SideEffectType.DATAFLOW_SIDE_EFFECTING


def _exchange_start(src, name):
    r, c = src.shape[1:]

    def body(src_ref, land_ref, send_sems, recv_sems, src_thru, land_thru, token):
        for k in range(1, N_DEV):
            peer = _peer(k)
            pltpu.make_async_remote_copy(
                src_ref=src_ref.at[_linear(peer)], dst_ref=land_ref.at[k - 1],
                send_sem=send_sems.at[k - 1], recv_sem=recv_sems.at[k - 1],
                device_id=peer, device_id_type=MESH_ID).start()
        token[...] = jnp.zeros_like(token)

    land = lax.empty((N_DEV - 1, r, c), src.dtype)
    return pl.pallas_call(
        body, name=name,
        out_shape=(pltpu.SemaphoreType.DMA((N_DEV - 1,)), pltpu.SemaphoreType.DMA((N_DEV - 1,)),
                   pltpu.HBM(src.shape, src.dtype), pltpu.HBM(land.shape, land.dtype),
                   jax.ShapeDtypeStruct((8, 128), F32)),
        in_specs=(HBM_SPEC, HBM_SPEC),
        out_specs=(SEM_SPEC, SEM_SPEC, HBM_SPEC, HBM_SPEC, pl.BlockSpec(memory_space=pltpu.VMEM)),
        input_output_aliases={0: 2, 1: 3},
        compiler_params=pltpu.CompilerParams(has_side_effects=DATAFLOW),
    )(pltpu.with_memory_space_constraint(src, pltpu.HBM), pltpu.with_memory_space_constraint(land, pltpu.HBM))


def _exchange_wait(started, after, name):
    send_sems, recv_sems, src_thru, land_thru, _ = started

    def body(src_ref, land_ref, send_sems, recv_sems, *rest):
        for k in range(1, N_DEV):
            cp = pltpu.make_async_remote_copy(
                src_ref=src_ref.at[0], dst_ref=land_ref.at[k - 1],
                send_sem=send_sems.at[k - 1], recv_sem=recv_sems.at[k - 1],
                device_id=_peer(k), device_id_type=MESH_ID)
            cp.wait_send()
            cp.wait_recv()

    return pl.pallas_call(
        body, name=name,
        out_shape=(pltpu.HBM(src_thru.shape, src_thru.dtype), pltpu.HBM(land_thru.shape, land_thru.dtype)),
        in_specs=(HBM_SPEC, HBM_SPEC, SEM_SPEC, SEM_SPEC) + (pl.BlockSpec(memory_space=pl.ANY),) * len(after),
        out_specs=(HBM_SPEC, HBM_SPEC), input_output_aliases={0: 0, 1: 1},
        compiler_params=pltpu.CompilerParams(has_side_effects=DATAFLOW),
    )(src_thru, land_thru, send_sems, recv_sems, *after)


def _share_start(arrs, zones, name):
    n = len(arrs)

    def body(*refs):
        src_refs, zone_refs, sems = refs[:n], refs[n:2 * n], refs[2 * n:4 * n]
        me = _linear(_mesh_position())
        for a in range(n):
            for k in range(1, N_DEV):
                pltpu.make_async_remote_copy(
                    src_ref=src_refs[a], dst_ref=zone_refs[a].at[me],
                    send_sem=sems[2 * a].at[k - 1], recv_sem=sems[2 * a + 1].at[k - 1],
                    device_id=_peer(k), device_id_type=MESH_ID).start()

    outs = pl.pallas_call(
        body, name=name,
        out_shape=tuple(pltpu.SemaphoreType.DMA((N_DEV - 1,)) for _ in range(2 * n))
        + tuple(pltpu.HBM(a.shape, a.dtype) for a in arrs) + tuple(pltpu.HBM(z.shape, z.dtype) for z in zones),
        in_specs=(HBM_SPEC,) * (2 * n),
        out_specs=(SEM_SPEC,) * (2 * n) + (HBM_SPEC,) * (2 * n),
        input_output_aliases={i: 2 * n + i for i in range(2 * n)},
        compiler_params=pltpu.CompilerParams(has_side_effects=DATAFLOW),
    )(*[pltpu.with_memory_space_constraint(a, pltpu.HBM) for a in list(arrs) + list(zones)])
    return outs[:2 * n], outs[2 * n:3 * n], outs[3 * n:]


def _share_wait(started, after, name):
    sems, arrs, zones = started
    n = len(arrs)

    def body(*refs):
        src_refs, zone_refs, sem_refs = refs[:n], refs[n:2 * n], refs[2 * n:4 * n]
        for a in range(n):
            for k in range(1, N_DEV):
                cp = pltpu.make_async_remote_copy(
                    src_ref=src_refs[a], dst_ref=zone_refs[a].at[_linear(_peer(k))],
                    send_sem=sem_refs[2 * a].at[k - 1], recv_sem=sem_refs[2 * a + 1].at[k - 1],
                    device_id=_peer(k), device_id_type=MESH_ID)
                cp.wait_send()
                cp.wait_recv()

    outs = pl.pallas_call(
        body, name=name,
        out_shape=tuple(pltpu.HBM(a.shape, a.dtype) for a in arrs) + tuple(pltpu.HBM(z.shape, z.dtype) for z in zones),
        in_specs=(HBM_SPEC,) * (2 * n) + (SEM_SPEC,) * (2 * n) + (pl.BlockSpec(memory_space=pl.ANY),) * len(after),
        out_specs=(HBM_SPEC,) * (2 * n), input_output_aliases={i: i for i in range(2 * n)},
        compiler_params=pltpu.CompilerParams(has_side_effects=DATAFLOW),
    )(*arrs, *zones, *sems, *after)
    return list(outs[n:])


def _same_core_peers():
    x, y, c = _mesh_position()
    return [(x, y, 1 - c), (1 - x, y, c), (x, 1 - y, c), (1 - x, 1 - y, c)]


def _gather_start(bufs, name):
    n = len(bufs)

    def body(*refs):
        buf_refs, rest = refs[:n], refs[n:]
        sems, token = rest[:2 * n], rest[-1]
        me = _linear(_mesh_position())
        for a in range(n):
            for k, peer in enumerate(_same_core_peers()):
                pltpu.make_async_remote_copy(
                    src_ref=buf_refs[a].at[me], dst_ref=buf_refs[a].at[me],
                    send_sem=sems[2 * a].at[k], recv_sem=sems[2 * a + 1].at[k],
                    device_id=peer, device_id_type=MESH_ID).start()
        token[...] = jnp.zeros_like(token)

    outs = pl.pallas_call(
        body, name=name,
        out_shape=tuple(pltpu.SemaphoreType.DMA((4,)) for _ in range(2 * n))
        + tuple(pltpu.HBM(b.shape, b.dtype) for b in bufs) + (jax.ShapeDtypeStruct((8, 128), F32),),
        in_specs=(HBM_SPEC,) * n,
        out_specs=(SEM_SPEC,) * (2 * n) + (HBM_SPEC,) * n + (pl.BlockSpec(memory_space=pltpu.VMEM),),
        input_output_aliases={a: 2 * n + a for a in range(n)},
        compiler_params=pltpu.CompilerParams(has_side_effects=DATAFLOW),
    )(*[pltpu.with_memory_space_constraint(b, pltpu.HBM) for b in bufs])
    return outs[:2 * n], outs[2 * n:3 * n], outs[3 * n]


def _gather_wait(sems, bufs, after, name):
    n = len(bufs)

    def body(*refs):
        buf_refs, sem_refs = refs[:n], refs[n:3 * n]
        x, y, c = _mesh_position()
        me = _linear((x, y, c))
        for a in range(n):
            for k, peer in enumerate(_same_core_peers()):
                cp = pltpu.make_async_remote_copy(
                    src_ref=buf_refs[a].at[me], dst_ref=buf_refs[a].at[_linear(peer)],
                    send_sem=sem_refs[2 * a].at[k], recv_sem=sem_refs[2 * a + 1].at[k],
                    device_id=peer, device_id_type=MESH_ID)
                cp.wait_send()
                cp.wait_recv()

    return list(pl.pallas_call(
        body, name=name,
        out_shape=tuple(pltpu.HBM(b.shape, b.dtype) for b in bufs),
        in_specs=(HBM_SPEC,) * n + (SEM_SPEC,) * (2 * n) + (pl.BlockSpec(memory_space=pl.ANY),) * len(after),
        out_specs=(HBM_SPEC,) * n, input_output_aliases={a: a for a in range(n)},
        compiler_params=pltpu.CompilerParams(has_side_effects=DATAFLOW),
    )(*bufs, *sems, *after))


def _gather_pass_on(bufs, name):
    n = len(bufs)

    def body(*refs):
        out_refs = refs[n:2 * n]
        send_sems, recv_sems = refs[2 * n:]
        x, y, c = _mesh_position()
        sibling = (x, y, 1 - c)
        chips = [(1 - x, y), (x, 1 - y), (1 - x, 1 - y)]
        copies = []
        for a in range(n):
            for j, chip in enumerate(chips):
                block = out_refs[a].at[_linear((*chip, c))]
                copies.append(pltpu.make_async_remote_copy(
                    src_ref=block, dst_ref=block, send_sem=send_sems.at[3 * a + j], recv_sem=recv_sems.at[3 * a + j],
                    device_id=sibling, device_id_type=MESH_ID))
                copies[-1].start()
        for a in range(n):
            for j, chip in enumerate(chips):
                copies[3 * a + j].wait_send()
                theirs = out_refs[a].at[_linear((*chip, 1 - c))]
                pltpu.make_async_remote_copy(
                    src_ref=theirs, dst_ref=theirs, send_sem=send_sems.at[3 * a + j], recv_sem=recv_sems.at[3 * a + j],
                    device_id=sibling, device_id_type=MESH_ID).wait_recv()

    hbm = pl.BlockSpec(memory_space=pl.ANY)
    return list(pl.pallas_call(
        body, name=name,
        out_shape=[jax.ShapeDtypeStruct(b.shape, b.dtype) for b in bufs],
        in_specs=[hbm] * n, out_specs=[hbm] * n, input_output_aliases={a: a for a in range(n)},
        scratch_shapes=[pltpu.SemaphoreType.DMA((3 * n,)), pltpu.SemaphoreType.DMA((3 * n,))],
    )(*bufs))


def _open_step(c, conv_w, w_ada, b_cols, w_in_t, later, rel_bias, bucket):
    cols = w_ada.shape[1]
    n_later = len(later)

    def body(c_ref, cw_ref, wa_ref, b_ref, w_ref, *rest):
        later_refs, rb_ref, bk_ref = rest[:n_later], rest[n_later], rest[n_later + 1]
        cond_ref, conv_ref, mod_ref, win_ref = rest[n_later + 2:n_later + 6]
        staged_refs, bias_ref = rest[n_later + 6:2 * n_later + 6], rest[2 * n_later + 6]
        cond_own, mod_own, stage = rest[2 * n_later + 7:2 * n_later + 10]
        later_stage = rest[2 * n_later + 10:3 * n_later + 10]
        s_send, s_recv, w_send, w_recv, local_sems = rest[3 * n_later + 10:]
        x, y, cc = _mesh_position()
        me = _linear((x, y, cc))
        sibling = (x, y, 1 - cc)
        chips = [(1 - x, y), (x, 1 - y), (1 - x, 1 - y)]
        v = c_ref[...]
        cond_own[...] = v * _sigmoid(v)
        stage[...] = w_ref[...].astype(BF16)

        def small(rnd, a, k, src, dst, slot):
            return pltpu.make_async_remote_copy(
                src_ref=src, dst_ref=dst.at[slot], send_sem=s_send.at[rnd, a, k - 1], recv_sem=s_recv.at[rnd, a, k - 1],
                device_id=_peer(k), device_id_type=MESH_ID)

        def block(p):
            return win_ref.at[_linear(p)]

        def big(k, blk, to, src=None):
            return pltpu.make_async_remote_copy(
                src_ref=block(blk) if src is None else src, dst_ref=block(blk),
                send_sem=w_send.at[k], recv_sem=w_recv.at[k], device_id=to, device_id_type=MESH_ID)

        mine = [pltpu.make_async_copy(cond_own, cond_ref.at[me], local_sems.at[0]),
                pltpu.make_async_copy(cw_ref, conv_ref.at[me], local_sems.at[1]),
                pltpu.make_async_copy(stage, block((x, y, cc)), local_sems.at[2])]
        for cp in mine:
            cp.start()
        sends = []
        for k in range(1, N_DEV):
            sends += [small(0, 0, k, cond_own, cond_ref, me), small(0, 1, k, cw_ref, conv_ref, me)]
        for cp in sends:
            cp.start()
        first = [big(0, (x, y, cc), sibling, src=stage)]
        first += [big(1 + j, (x, y, cc), (*chip, cc), src=stage) for j, chip in enumerate(chips)]
        for cp in first:
            cp.start()
        for a in range(n_later):
            later_stage[a][...] = later_refs[a][...].astype(BF16)
            mine.append(pltpu.make_async_copy(later_stage[a], staged_refs[a].at[me], local_sems.at[4 + a]))
            mine[-1].start()
        _fill_bias_table(rb_ref, bk_ref, bias_ref)
        for k in range(1, N_DEV):
            small(0, 0, k, cond_own, cond_ref, _linear(_peer(k))).wait_recv()
            small(0, 1, k, cw_ref, conv_ref, _linear(_peer(k))).wait_recv()
        mine[0].wait()
        cond_all = jnp.concatenate([cond_ref[k] for k in range(N_DEV)], axis=0)
        mod_own[...] = _dot(cond_all, wa_ref[...]) + b_ref[...]
        mine.append(pltpu.make_async_copy(mod_own, mod_ref.at[me], local_sems.at[3]))
        mine[-1].start()
        second = [small(1, 0, k, mod_own, mod_ref, me) for k in range(1, N_DEV)]
        for cp in second:
            cp.start()
        passed = []
        for j, chip in enumerate(chips):
            big(1 + j, (*chip, cc), (x, y, cc)).wait_recv()
            fwd = big(4 + j, (*chip, cc), sibling)
            fwd.start()
            passed.append(fwd)
        big(0, sibling, (x, y, cc)).wait_recv()
        for j, chip in enumerate(chips):
            big(4 + j, (*chip, 1 - cc), (x, y, cc)).wait_recv()
        for k in range(1, N_DEV):
            small(1, 0, k, mod_own, mod_ref, _linear(_peer(k))).wait_recv()
        for cp in sends + first + second + passed:
            cp.wait_send()
        for cp in mine[1:]:
            cp.wait()

    vmem = pl.BlockSpec(memory_space=pltpu.VMEM)
    outs = pl.pallas_call(
        body, name="open_step",
        out_shape=[jax.ShapeDtypeStruct((N_DEV,) + c.shape, F32), jax.ShapeDtypeStruct((N_DEV,) + conv_w.shape, F32),
                   jax.ShapeDtypeStruct((N_DEV, N_DEV, cols), F32),
                   jax.ShapeDtypeStruct((N_DEV,) + w_in_t.shape, BF16)]
        + [jax.ShapeDtypeStruct((N_DEV,) + a.shape, BF16) for a in later]
        + [jax.ShapeDtypeStruct((N_Q_HEADS, BLOCK, 2 * BLOCK), F32)],
        in_specs=[vmem] * (5 + n_later) + [pl.BlockSpec(memory_space=pltpu.SMEM), vmem],
        out_specs=[vmem, vmem, vmem, ANY_SPEC] + [ANY_SPEC] * n_later + [vmem],
        scratch_shapes=[pltpu.VMEM(c.shape, F32), pltpu.VMEM((N_DEV, cols), F32), pltpu.VMEM(w_in_t.shape, BF16)]
        + [pltpu.VMEM(a.shape, BF16) for a in later]
        + [pltpu.SemaphoreType.DMA((2, 2, N_DEV - 1)), pltpu.SemaphoreType.DMA((2, 2, N_DEV - 1)),
           pltpu.SemaphoreType.DMA((7,)), pltpu.SemaphoreType.DMA((7,)),
           pltpu.SemaphoreType.DMA((4 + n_later,))],
        compiler_params=_params(vmem=VMEM_LIMIT_LARGE),
    )(c, conv_w, w_ada, b_cols, w_in_t, *later, rel_bias, bucket)
    return outs[0], outs[1], outs[2], outs[3], list(outs[4:4 + n_later]), outs[4 + n_later]


def _in_proj(x, mod, g_norm1, w_in, tm):
    s = x.shape[0]

    def body(x_ref, mod_ref, g_ref, w_ref, h_ref, q_ref, kv_ref, gb_ref, gc_ref, xc_ref):
        xf = x_ref[...]
        n = xf * _rsqrt_mean_sq(xf) * g_ref[...]
        h = (n * (1.0 + mod_ref[SC1:SC1 + 1, :]) + mod_ref[SH1:SH1 + 1, :]).astype(BF16)
        h_ref[...] = h
        p = _dot_nt(h, w_ref[...])
        q_ref[...] = p[:, 0:512].astype(BF16)
        kv_ref[...] = p[:, 512:768].astype(BF16)
        gb_ref[...] = p[:, 768:1280].astype(BF16)
        gc_ref[...] = p[:, 1280:1792].astype(BF16)
        xc_ref[...] = p[:, 1792:2304].astype(BF16)

    return pl.pallas_call(
        body, name="in_proj", grid=(s // tm,),
        in_specs=[_rows(tm, D_MODEL), _full((8, D_MODEL)), _full((1, D_MODEL)), _full((IN_PROJ_WIDTH, D_MODEL))],
        out_specs=[_rows(tm, D_MODEL), _rows(tm, 512), _rows(tm, 256), _rows(tm, 512), _rows(tm, 512), _rows(tm, 512)],
        out_shape=[jax.ShapeDtypeStruct((s, D_MODEL), BF16), jax.ShapeDtypeStruct((s, 512), BF16),
                   jax.ShapeDtypeStruct((s, 256), BF16), jax.ShapeDtypeStruct((s, 512), BF16),
                   jax.ShapeDtypeStruct((s, 512), BF16), jax.ShapeDtypeStruct((s, 512), BF16)],
        compiler_params=_params(("arbitrary",), VMEM_LIMIT_LARGE),
    )(x, mod, g_norm1, w_in)


def _t5_bucket(dist):
    max_exact = N_BUCKETS // 2
    is_small = dist < max_exact
    d = jnp.maximum(dist, 1).astype(F32)
    large = max_exact + (jnp.log(d / max_exact) / math.log(MAX_DISTANCE / max_exact)
                         * (N_BUCKETS - max_exact)).astype(jnp.int32)
    large = jnp.minimum(large, N_BUCKETS - 1)
    return jnp.where(is_small, dist, large)


def _bucket_table():
    qi = jnp.arange(BLOCK, dtype=jnp.int32)[:, None]
    sj = jnp.arange(2 * BLOCK, dtype=jnp.int32)[None, :]
    return _t5_bucket(jnp.maximum(qi + BLOCK - sj, 0))


def _window_mask():
    qi = lax.broadcasted_iota(jnp.int32, (BLOCK, 2 * BLOCK), 0)
    sj = lax.broadcasted_iota(jnp.int32, (BLOCK, 2 * BLOCK), 1)
    dist = qi + BLOCK - sj
    return (dist >= 0) & (dist < BLOCK)


def _fill_bias_table(rb_ref, bk_ref, o_ref):
    bk = bk_ref[...]
    inside = _window_mask()
    for h in range(N_Q_HEADS):
        acc = jnp.zeros((BLOCK, 2 * BLOCK), F32)
        for b in range(N_BUCKETS):
            acc = jnp.where(bk == b, rb_ref[b, h], acc)
        o_ref[h] = jnp.where(inside, acc, NEG_INF)


def _load_kv_window(kv_ref, n):
    prev = jnp.maximum(n - 1, 0)
    kvw = jnp.concatenate([kv_ref[pl.ds(pl.multiple_of(prev * BLOCK, BLOCK), BLOCK), :],
                           kv_ref[pl.ds(pl.multiple_of(n * BLOCK, BLOCK), BLOCK), :]], axis=0)
    k, v = kvw[:, 0:128], kvw[:, 128:256]
    k_sw = pltpu.roll(k.astype(F32), 64, 1).astype(BF16)
    v_sw = pltpu.roll(v.astype(F32), 64, 1).astype(BF16)
    return (k, k_sw), (v, v_sw)


def _conv_taps(gc, xc, gc_prev, xc_prev, n):
    u = gc * xc
    before = jnp.where(n > 0, gc_prev.astype(F32) * xc_prev.astype(F32), 0.0)
    last = before.shape[0] - 1
    row = lax.broadcasted_iota(jnp.int32, u.shape, 0)
    u1 = jnp.where(row == 0, before[last:last + 1, :], pltpu.roll(u, 1, 0))
    u2 = jnp.where(row == 0, before[last - 1:last, :],
                   jnp.where(row == 1, before[last:last + 1, :], pltpu.roll(u, 2, 0)))
    return u, u1, u2


def _mixer_fwd(q, kv, gb, gc, xc, bias, sinks, conv_w, g_attn, g_conv):
    s = q.shape[0]
    nb = s // BLOCK

    per_step = min(MIXER_BLOCKS, nb)
    tile = per_step * BLOCK

    def one_block(n, slot, before, sink_ref, q_ref, kv_ref, gb_ref, gc_ref, xc_ref, bias_ref, cw_ref, ga_ref,
                  gcv_ref, attn_ref, merged_ref, lse_ref, p_ref):
        rows = slice(slot * BLOCK, (slot + 1) * BLOCK)
        ks, vs = _load_kv_window(kv_ref, n)
        lane = lax.broadcasted_iota(jnp.int32, (BLOCK, BLOCK), 1)
        low = lane < HEAD_DIM
        col = lax.broadcasted_iota(jnp.int32, (BLOCK, 2 * BLOCK), 1)
        no_prev = (col < BLOCK) & (n == 0)
        lse_all = jnp.zeros((BLOCK, BLOCK), F32)
        pairs = []
        for p in range(4):
            qp = q_ref[rows, 128 * p:128 * (p + 1)].astype(F32)
            kvh = p // 2
            res = []
            for e in range(2):
                h = 2 * p + e
                qm = jnp.where(low if e == 0 else ~low, qp, 0.0).astype(BF16)
                sw = 0 if kvh == e else 1
                sc = _dot_nt(qm, ks[sw]) * SCALE + bias_ref[h]
                sc = jnp.where(no_prev, NEG_INF, sc)
                sink = sink_ref[h]
                m = jnp.maximum(jnp.max(sc, axis=-1, keepdims=True), sink)
                pe = jnp.exp(sc - m)
                den = jnp.sum(pe, axis=-1, keepdims=True) + jnp.exp(sink - m)
                pb = (pe * (1.0 / den)).astype(BF16)
                p_ref[slot, h] = pb
                res.append(_dot(pb, vs[sw]))
                lse_all = lse_all + jnp.where(lane == h, m + jnp.log(den), 0.0)
            pairs.append(jnp.where(low, res[0], res[1]))
        attn = jnp.concatenate(pairs, axis=1)
        attn_ref[rows, :] = attn
        lse_ref[rows, :] = lse_all
        u, u1, u2 = _conv_taps(gc_ref[rows, :].astype(F32), xc_ref[rows, :].astype(F32), before[0], before[1], n)
        cw = cw_ref[...]
        cv = gb_ref[rows, :].astype(F32) * (cw[0:1, :] * u2 + cw[1:2, :] * u1 + cw[2:3, :] * u)
        an = attn * _rsqrt_mean_sq(attn) * ga_ref[...]
        cn = cv * _rsqrt_mean_sq(cv) * gcv_ref[...]
        merged_ref[rows, :] = jnp.concatenate([an, cn], axis=1).astype(BF16)

    def body(sink_ref, q_ref, kv_ref, gb_ref, gc_ref, xc_ref, gcp_ref, xcp_ref, *rest):
        step = pl.program_id(0)
        for sub in range(per_step):
            ahead = slice(sub * BLOCK - PREV_ROWS, sub * BLOCK)
            before = (gcp_ref[...], xcp_ref[...]) if sub == 0 else (gc_ref[ahead, :], xc_ref[ahead, :])
            one_block(step * per_step + sub, sub, before, sink_ref, q_ref, kv_ref, gb_ref, gc_ref, xc_ref, *rest)

    blk = lambda w: pl.BlockSpec((tile, w), lambda n: (n, 0))
    prev8 = pl.BlockSpec((PREV_ROWS, 512), lambda n: (jnp.maximum(n * (tile // PREV_ROWS) - 1, 0), 0))
    return pl.pallas_call(
        body, name="mixer_fwd", grid=(nb // per_step,),
        in_specs=[pl.BlockSpec(memory_space=pltpu.SMEM), blk(512), _full((s, 256)), blk(512), blk(512), blk(512),
                  prev8, prev8, _full((N_Q_HEADS, BLOCK, 2 * BLOCK)), _full((3, 512)), _full((1, 512)),
                  _full((1, 512))],
        out_specs=[blk(512), blk(1024), blk(128),
                   pl.BlockSpec((per_step, N_Q_HEADS, BLOCK, 2 * BLOCK), lambda n: (n, 0, 0, 0))],
        out_shape=[jax.ShapeDtypeStruct((s, 512), F32), jax.ShapeDtypeStruct((s, 1024), BF16),
                   jax.ShapeDtypeStruct((s, 128), F32),
                   jax.ShapeDtypeStruct((nb, N_Q_HEADS, BLOCK, 2 * BLOCK), BF16)],
        compiler_params=_params(("arbitrary",)),
    )(sinks, q, kv, gb, gc, xc, gc, xc, bias, conv_w, g_attn, g_conv)


def _out_proj(merged, x, mod, w_out, tm):
    s = x.shape[0]

    def body(m_ref, x_ref, mod_ref, w_ref, o_ref, x1_ref):
        o = _dot(m_ref[...], w_ref[...])
        o_ref[...] = o.astype(BF16)
        x1_ref[...] = x_ref[...] + mod_ref[G1:G1 + 1, :] * o

    return pl.pallas_call(
        body, name="out_proj", grid=(s // tm,),
        in_specs=[_rows(tm, D_MODEL), _rows(tm, D_MODEL), _full((8, D_MODEL)), _full((D_MODEL, D_MODEL))],
        out_specs=[_rows(tm, D_MODEL), _rows(tm, D_MODEL)],
        out_shape=[jax.ShapeDtypeStruct((s, D_MODEL), BF16), jax.ShapeDtypeStruct((s, D_MODEL), F32)],
        compiler_params=_params(("arbitrary",)),
    )(merged, x, mod, w_out)


def _resident(shape):
    nd = len(shape)
    return pl.BlockSpec(shape, lambda *_: (0,) * nd, pipeline_mode=pl.Buffered(1))


def _ffn(x1, o1, merged, mod, g_norm2, w_gu, w_down, w_out, g_final, target, tm):
    s = x1.shape[0]
    chunk = D_FF // FFN_CHUNKS

    def body(x_ref, o1_ref, mg_ref, mod_ref, g_ref, wgu_ref, wd_ref, wo_ref, gf_ref, t_ref,
             h_ref, act_ref, do_ref, dgu_ref, dx1_ref, dwo_ref, dm_ref, small_ref, dwo_acc):
        @pl.when(pl.program_id(0) == 0)
        def _():
            small_ref[...] = jnp.zeros_like(small_ref)
            dwo_acc[...] = jnp.zeros_like(dwo_acc)

        xf = x_ref[...]
        n = xf * _rsqrt_mean_sq(xf) * g_ref[...]
        h = (n * (1.0 + mod_ref[SC2:SC2 + 1, :]) + mod_ref[SH2:SH2 + 1, :]).astype(BF16)
        h_ref[...] = h
        gates, ups, o = [], [], None
        for j in range(FFN_CHUNKS):
            lo = j * chunk
            gate = _dot_nt(h, wgu_ref[lo:lo + chunk, :])
            up = _dot_nt(h, wgu_ref[D_FF + lo:D_FF + lo + chunk, :])
            sg = _sigmoid(gate)
            act = (gate * sg * up).astype(BF16)
            act_ref[:, lo:lo + chunk] = act
            gates.append((up * (sg * (1.0 + gate * (1.0 - sg)))).astype(BF16))
            ups.append((gate * sg).astype(BF16))
            part = _dot(act, wd_ref[lo:lo + chunk, :])
            o = part if o is None else o + part
        g2 = mod_ref[G2:G2 + 1, :]
        x2 = xf + g2 * o
        r = _rsqrt_mean_sq(x2)
        xn = x2 * r
        gf = gf_ref[...]
        err = xn * gf - t_ref[...]
        dy = err * (1.0 / D_MODEL)
        dxn = dy * gf
        dx2 = r * (dxn - xn * jnp.mean(dxn * xn, axis=-1, keepdims=True))
        small_ref[4:5, :] += _colsum(dy * xn)
        small_ref[5:6, :] += _colsum(err * err)
        small_ref[3:4, :] += _colsum(dx2 * o)
        do = (dx2 * g2).astype(BF16)
        do_ref[...] = do
        dh = None
        for j in range(FFN_CHUNKS):
            lo = j * chunk
            dact = _dot_nt(do, wd_ref[lo:lo + chunk, :])
            dgate = (dact * gates[j].astype(F32)).astype(BF16)
            dup = (dact * ups[j].astype(F32)).astype(BF16)
            dgu_ref[:, lo:lo + chunk] = dgate
            dgu_ref[:, D_FF + lo:D_FF + lo + chunk] = dup
            part = _dot(dgate, wgu_ref[lo:lo + chunk, :]) + _dot(dup, wgu_ref[D_FF + lo:D_FF + lo + chunk, :])
            dh = part if dh is None else dh + part
        dx1 = dx2 + _norm_mod_bwd(dh, xf, g_ref[...], mod_ref[SC2:SC2 + 1, :], small_ref)
        dx1_ref[...] = dx1.astype(BF16)
        small_ref[7:8, :] += _colsum(dx1 * o1_ref[...].astype(F32))
        do1 = (dx1 * mod_ref[G1:G1 + 1, :]).astype(BF16)
        dm_ref[...] = _dot_nt(do1, wo_ref[...]).astype(BF16)
        dwo = dwo_acc[...] + _dot_tn(mg_ref[...], do1)
        dwo_acc[...] = dwo
        dwo_ref[...] = dwo.astype(BF16)

        @pl.when(pl.program_id(0) == pl.num_programs(0) - 1)
        def _():
            total = jnp.sum(small_ref[5:6, :], axis=-1, keepdims=True) * (0.5 / D_MODEL)
            small_ref[6:7, :] = jnp.broadcast_to(total, (1, D_MODEL))

    narrow = jax.ShapeDtypeStruct((s, D_MODEL), BF16)
    return pl.pallas_call(
        body, name="ffn", grid=(s // tm,),
        in_specs=[_rows(tm, D_MODEL), _rows(tm, D_MODEL), _rows(tm, D_MODEL), _full((8, D_MODEL)), _full((1, D_MODEL)),
                  _resident((2 * D_FF, D_MODEL)), _resident((D_FF, D_MODEL)), _resident((D_MODEL, D_MODEL)),
                  _full((1, D_MODEL)), _rows(tm, D_MODEL)],
        out_specs=[_rows(tm, D_MODEL), _rows(tm, D_FF), _rows(tm, D_MODEL), _rows(tm, 2 * D_FF), _rows(tm, D_MODEL),
                   _full((D_MODEL, D_MODEL)), _rows(tm, D_MODEL), _full((8, D_MODEL))],
        out_shape=[narrow, jax.ShapeDtypeStruct((s, D_FF), BF16), narrow, jax.ShapeDtypeStruct((s, 2 * D_FF), BF16),
                   narrow, jax.ShapeDtypeStruct((D_MODEL, D_MODEL), BF16), narrow,
                   jax.ShapeDtypeStruct((8, D_MODEL), F32)],
        scratch_shapes=[pltpu.VMEM((D_MODEL, D_MODEL), F32)],
        compiler_params=_params(("arbitrary",), VMEM_LIMIT_LARGE),
    )(x1, o1, merged, mod, g_norm2, w_gu, w_down, w_out, g_final, target)


def _norm_mod_bwd(dh, xf, g, scale_row, small_ref):
    r = _rsqrt_mean_sq(xf)
    xn = xf * r
    small_ref[0:1, :] += _colsum(dh)
    small_ref[1:2, :] += _colsum(dh * (xn * g))
    dn = dh * (1.0 + scale_row)
    small_ref[2:3, :] += _colsum(dn * xn)
    dxn = dn * g
    return r * (dxn - xn * jnp.mean(dxn * xn, axis=-1, keepdims=True))


def _group_norm_bwd(dm, a, g):
    r = _rsqrt_mean_sq(a)
    an = a * r
    dan = dm * g
    return r * (dan - an * jnp.mean(dan * an, axis=-1, keepdims=True)), _colsum(dm * an)


def _sum_by_bucket(db_ref, bk_ref, o_ref, rows_ref):
    bk = bk_ref[...]
    for b in range(N_BUCKETS):
        sel = (bk == b).astype(F32)
        for h in range(N_Q_HEADS):
            rows_ref[N_BUCKETS * h + b:N_BUCKETS * h + b + 1, :] = _colsum(db_ref[h] * sel)
    head = lax.broadcasted_iota(jnp.int32, (N_BUCKETS, N_Q_HEADS), 1)
    out = jnp.zeros((N_BUCKETS, N_Q_HEADS), F32)
    for h in range(N_Q_HEADS):
        per_bucket = jnp.sum(rows_ref[N_BUCKETS * h:N_BUCKETS * (h + 1), :], axis=-1, keepdims=True)
        out = out + jnp.where(head == h, per_bucket, 0.0)
    o_ref[...] = out


def _mixer_bwd(after, q, kv, gb, gc, xc, probs, sinks, conv_w, g_attn, g_conv, attn, lse, dmerged, bucket):
    s = q.shape[0]
    nb = s // BLOCK

    per_step = min(MIXER_BLOCKS, nb)
    tile = per_step * BLOCK
    steps = nb // per_step

    def one_block(n, slot, before, nxt, sink_ref, q_ref, kv_ref, gb_ref, gc_ref, xc_ref, p_ref, cw_ref, ga_ref,
                  gcv_ref, attn_ref, lse_ref, dm_ref, dproj_ref, dbias_ref, dsink_ref, small_ref):
        rows = slice(slot * BLOCK, (slot + 1) * BLOCK)
        next_dy, next_dkv = nxt
        dm = dm_ref[rows, :].astype(F32)
        gbv, gcv_, xcv = gb_ref[rows, :].astype(F32), gc_ref[rows, :].astype(F32), xc_ref[rows, :].astype(F32)
        u, u1, u2 = _conv_taps(gcv_, xcv, before[0], before[1], n)
        cw = cw_ref[...]
        yv = cw[0:1, :] * u2 + cw[1:2, :] * u1 + cw[2:3, :] * u
        dcv, dg_conv = _group_norm_bwd(dm[:, 512:1024], gbv * yv, gcv_ref[...])
        small_ref[1:2, :] += dg_conv
        dproj_ref[rows, 768:1280] = (dcv * yv).astype(BF16)
        dy = dcv * gbv
        row = lax.broadcasted_iota(jnp.int32, dy.shape, 0)
        d1 = jnp.where(row == BLOCK - 1, next_dy[0:1, :], pltpu.roll(dy, BLOCK - 1, 0))
        d2 = jnp.where(row == BLOCK - 2, next_dy[0:1, :],
                       jnp.where(row == BLOCK - 1, next_dy[1:2, :], pltpu.roll(dy, BLOCK - 2, 0)))
        du = cw[2:3, :] * dy + cw[1:2, :] * d1 + cw[0:1, :] * d2
        dproj_ref[rows, 1280:1792] = (du * xcv).astype(BF16)
        dproj_ref[rows, 1792:2304] = (du * gcv_).astype(BF16)
        small_ref[2:3, :] += _colsum(dy * u2)
        small_ref[3:4, :] += _colsum(dy * u1)
        small_ref[4:5, :] += _colsum(dy * u)

        attn_v = attn_ref[rows, :]
        dout, dg_attn = _group_norm_bwd(dm[:, 0:512], attn_v, ga_ref[...])
        small_ref[0:1, :] += dg_attn
        ks, vs = _load_kv_window(kv_ref, n)
        lane = lax.broadcasted_iota(jnp.int32, (BLOCK, BLOCK), 1)
        low = lane < HEAD_DIM
        lse_all = lse_ref[rows, :]
        dsink = jnp.zeros((BLOCK, BLOCK), F32)
        dq_pairs = []
        dk_groups, dv_groups = [], []
        for kvh in range(2):
            ds_rows, pr_rows, q_rows, do_rows = [], [], [], []
            for p in (2 * kvh, 2 * kvh + 1):
                qp = q_ref[rows, 128 * p:128 * (p + 1)].astype(F32)
                do_p = dout[:, 128 * p:128 * (p + 1)]
                prod = do_p * attn_v[:, 128 * p:128 * (p + 1)]
                res = []
                for e in range(2):
                    h = 2 * p + e
                    half = low if e == 0 else ~low
                    qm = jnp.where(half, qp, 0.0).astype(BF16)
                    dom = jnp.where(half, do_p, 0.0).astype(BF16)
                    delta = jnp.sum(jnp.where(half, prod, 0.0), axis=-1, keepdims=True)
                    lse_h = jnp.sum(jnp.where(lane == h, lse_all, 0.0), axis=-1, keepdims=True)
                    sw = 0 if kvh == e else 1
                    pb = p_ref[slot, h]
                    dp = _dot_nt(dom, vs[sw])
                    ds = pb.astype(F32) * (dp - delta)
                    dbias_ref[h] += ds
                    dsink = dsink + jnp.where(lane == h, -jnp.exp(sink_ref[h] - lse_h) * delta, 0.0)
                    dsb = ds.astype(BF16)
                    res.append(_dot(dsb, ks[sw]) * SCALE)
                    ds_rows.append(dsb)
                    pr_rows.append(pb)
                    q_rows.append(qm)
                    do_rows.append(dom)
                dq_pairs.append(jnp.where(low, res[0], res[1]))
            dk_g = _dot_tn(jnp.concatenate(ds_rows, axis=0), jnp.concatenate(q_rows, axis=0)) * SCALE
            dv_g = _dot_tn(jnp.concatenate(pr_rows, axis=0), jnp.concatenate(do_rows, axis=0))
            dk_groups.append(dk_g + pltpu.roll(dk_g, 64, 1))
            dv_groups.append(dv_g + pltpu.roll(dv_g, 64, 1))
        dproj_ref[rows, 0:512] = jnp.concatenate(dq_pairs, axis=1).astype(BF16)
        dsink_ref[...] += dsink
        low_kv = lax.broadcasted_iota(jnp.int32, (2 * BLOCK, BLOCK), 1) < HEAD_DIM
        dkv_win = jnp.concatenate([jnp.where(low_kv, dk_groups[0], dk_groups[1]),
                                   jnp.where(low_kv, dv_groups[0], dv_groups[1])], axis=1)
        dproj_ref[rows, 512:768] = (dkv_win[BLOCK:2 * BLOCK, :] + next_dkv).astype(BF16)
        return dy[0:8, :], dkv_win[0:BLOCK, :]

    def body(sink_ref, q_ref, kv_ref, gb_ref, gc_ref, xc_ref, gcp_ref, xcp_ref, p_ref, cw_ref, ga_ref, gcv_ref,
             attn_ref, lse_ref, dm_ref, bk_ref, dproj_ref, drel_ref, dsink_ref, small_ref,
             dy_ref, dkv_ref, dbias_ref, rows_ref):
        refs = (p_ref, cw_ref, ga_ref, gcv_ref, attn_ref, lse_ref, dm_ref, dproj_ref, dbias_ref, dsink_ref, small_ref)
        step = pl.program_id(0)

        @pl.when(step == 0)
        def _():
            dbias_ref[...] = jnp.zeros_like(dbias_ref)
            dsink_ref[...] = jnp.zeros_like(dsink_ref)
            small_ref[...] = jnp.zeros_like(small_ref)
            dy_ref[...] = jnp.zeros_like(dy_ref)
            dkv_ref[...] = jnp.zeros_like(dkv_ref)

        nxt = (dy_ref[...], dkv_ref[...])
        for sub in reversed(range(per_step)):
            ahead = slice(sub * BLOCK - PREV_ROWS, sub * BLOCK)
            before = (gcp_ref[...], xcp_ref[...]) if sub == 0 else (gc_ref[ahead, :], xc_ref[ahead, :])
            nxt = one_block((steps - 1 - step) * per_step + sub, sub, before, nxt,
                            sink_ref, q_ref, kv_ref, gb_ref, gc_ref, xc_ref, *refs)
        dy_ref[...], dkv_ref[...] = nxt

        @pl.when(step == steps - 1)
        def _():
            small_ref[5:6, :] = jnp.concatenate([_colsum(dsink_ref[...]), jnp.zeros((1, 512 - BLOCK), F32)], axis=1)
            _sum_by_bucket(dbias_ref, bk_ref, drel_ref, rows_ref)

    blk = lambda w: pl.BlockSpec((tile, w), lambda t: (steps - 1 - t, 0))
    prev8 = pl.BlockSpec((PREV_ROWS, 512),
                         lambda t: (jnp.maximum((steps - 1 - t) * (tile // PREV_ROWS) - 1, 0), 0))
    bf = lambda w: jax.ShapeDtypeStruct((s, w), BF16)
    return pl.pallas_call(
        _coming_behind(body), name="mixer_bwd", grid=(steps,),
        in_specs=[ANY_SPEC, pl.BlockSpec(memory_space=pltpu.SMEM), blk(512), _full((s, 256)), blk(512), blk(512), blk(512),
                  prev8, prev8,
                  pl.BlockSpec((per_step, N_Q_HEADS, BLOCK, 2 * BLOCK), lambda t: (steps - 1 - t, 0, 0, 0)),
                  _full((3, 512)), _full((1, 512)), _full((1, 512)), blk(512), blk(128), blk(1024),
                  _full((BLOCK, 2 * BLOCK))],
        out_specs=[blk(IN_PROJ_WIDTH), _full((N_BUCKETS, N_Q_HEADS)), _full((BLOCK, BLOCK)), _full((8, 512))],
        out_shape=[bf(IN_PROJ_WIDTH), jax.ShapeDtypeStruct((N_BUCKETS, N_Q_HEADS), F32),
                   jax.ShapeDtypeStruct((BLOCK, BLOCK), F32), jax.ShapeDtypeStruct((8, 512), F32)],
        scratch_shapes=[pltpu.VMEM((8, 512), F32), pltpu.VMEM((BLOCK, 2 * KV_WIDTH), F32),
                        pltpu.VMEM((N_Q_HEADS, BLOCK, 2 * BLOCK), F32),
                        pltpu.VMEM((N_BUCKETS * N_Q_HEADS, 2 * BLOCK), F32)],
        compiler_params=_params(("arbitrary",), VMEM_LIMIT_LARGE),
    )(after, sinks, q, kv, gb, gc, xc, gc, xc, probs, conv_w, g_attn, g_conv, attn, lse, dmerged, bucket)


def _in_proj_bwd(after, dproj, x, dx1, mod, g_norm1, w_in, tm):
    s = x.shape[0]

    def body(dproj_ref, x_ref, dx1_ref, mod_ref, g_ref, w_ref, dx_ref, small_ref):
        @pl.when(pl.program_id(0) == 0)
        def _():
            small_ref[...] = jnp.zeros_like(small_ref)

        dh = _dot(dproj_ref[...], w_ref[...])
        dx_ref[...] = dx1_ref[...].astype(F32) + _norm_mod_bwd(dh, x_ref[...], g_ref[...], mod_ref[SC1:SC1 + 1, :],
                                                               small_ref)

    return pl.pallas_call(
        _coming_behind(body), name="in_proj_bwd", grid=(s // tm,),
        in_specs=[ANY_SPEC, _rows(tm, IN_PROJ_WIDTH), _rows(tm, D_MODEL), _rows(tm, D_MODEL), _full((8, D_MODEL)),
                  _full((1, D_MODEL)), _full((IN_PROJ_WIDTH, D_MODEL))],
        out_specs=[_rows(tm, D_MODEL), _full((8, D_MODEL))],
        out_shape=[jax.ShapeDtypeStruct((s, D_MODEL), F32), jax.ShapeDtypeStruct((8, D_MODEL), F32)],
        compiler_params=_params(("arbitrary",), VMEM_LIMIT_LARGE),
    )(after, dproj, x, dx1, mod, g_norm1, w_in)


def _weight_grad(a, b, tk, ts, name, after=None):
    s, k = a.shape
    n = b.shape[1]
    nt = s // ts
    extra = [] if after is None else [after]

    def body(a_ref, b_ref, *rest):
        o_ref, acc_ref = rest[-2:]
        t = pl.program_id(1)
        @pl.when(t == 0)
        def _():
            acc_ref[...] = jnp.zeros_like(acc_ref)

        acc = acc_ref[...] + _dot_tn(a_ref[...], b_ref[...])
        acc_ref[...] = acc
        o_ref[...] = acc.astype(BF16)

    return pl.pallas_call(
        body, name=name, grid=(k // tk, nt),
        in_specs=[pl.BlockSpec((ts, tk), lambda i, t: (t, i)), pl.BlockSpec((ts, n), lambda i, t: (t, 0))]
        + [ANY_SPEC] * len(extra),
        out_specs=pl.BlockSpec((tk, n), lambda i, t: (i, 0)),
        out_shape=jax.ShapeDtypeStruct((k, n), BF16),
        scratch_shapes=[pltpu.VMEM((tk, n), F32)],
        compiler_params=_params(("arbitrary", "arbitrary"), VMEM_LIMIT_LARGE),
    )(a, b, *extra)


def _lanes_from(x, start, width):
    n = x.shape[1]
    return pltpu.roll(x, (n - start) % n, 1)[:, 0:width]


def _adamw_w_ada(me, cond_all, packed_all, w, m, v, tr):
    r, cols = w.shape

    def body(me_ref, c_ref, p_ref, w_ref, m_ref, v_ref, g_ref, d_ref, mo_ref, vo_ref):
        dmod = jnp.concatenate([p_ref[k][:, OFF_DMOD:OFF_DMOD + N_MOD * D_MODEL] for k in range(N_DEV)], axis=0)
        mine = _lanes_from(dmod, me_ref[0] * cols, cols)
        pad = lambda a: jnp.concatenate([a, jnp.zeros((128 - N_DEV, a.shape[1]), F32)], axis=0)
        g = _dot_tn(pad(c_ref[...]), pad(mine))
        g_ref[...] = g
        d_ref[...], mo_ref[...], vo_ref[...] = _adam_math(w_ref[...], g, m_ref[...], v_ref[...])

    tile = pl.BlockSpec((tr, cols), lambda i, me_ref: (i, 0))
    return pl.pallas_call(
        body, name="adamw_w_ada",
        grid_spec=pltpu.PrefetchScalarGridSpec(
            num_scalar_prefetch=1, grid=(r // tr,),
            in_specs=[pl.BlockSpec((N_DEV, tr), lambda i, me_ref: (0, i)),
                      pl.BlockSpec(packed_all.shape, lambda i, me_ref: (0, 0, 0)), tile, tile, tile],
            out_specs=[tile] * 4),
        out_shape=[jax.ShapeDtypeStruct((r, cols), F32)] * 4,
        compiler_params=_params(("arbitrary",)),
    )(me, cond_all, packed_all, w, m, v)


SMALL_PARAMS = (("rel_bias", None), ("b_ada", (OFF_DMOD, N_MOD * D_MODEL)), ("g_norm1", (OFF_GN1, D_MODEL)),
                ("sinks", (OFF_SINK, N_Q_HEADS)), ("conv_w", None), ("g_attn_out", (OFF_GATT, ATTN_WIDTH)),
                ("g_conv_out", (OFF_GCV, CONV_WIDTH)), ("g_norm2", (OFF_GN2, D_MODEL)),
                ("g_final", (OFF_GFIN, D_MODEL)))


def _small_update(me, packed_all, rel_all, state, after):
    n_p = len(SMALL_PARAMS)
    flat = [a for triple in state for a in triple]
    conv_cols = state[4][0].shape[1]

    def body(me_ref, p_ref, r_ref, *refs):
        ins = refs[:3 * n_p]
        loss_ref, outs = refs[3 * n_p + len(after)], refs[3 * n_p + len(after) + 1:]
        small, rel = p_ref[0], r_ref[0]
        for k in range(1, N_DEV):
            small = small + p_ref[k]
            rel = rel + r_ref[k]
        loss_ref[...] = small[:, OFF_LOSS:OFF_LOSS + 128]
        taps = jnp.concatenate([small[:, OFF_CONVW + CONV_WIDTH * j:OFF_CONVW + CONV_WIDTH * (j + 1)]
                                for j in range(3)] + [jnp.zeros((5, CONV_WIDTH), F32)], axis=0)
        conv_g = _lanes_from(taps, me_ref[0] * conv_cols, conv_cols)[0:3, :]
        for i, (name, lanes) in enumerate(SMALL_PARAMS):
            g = rel if name == "rel_bias" else conv_g if name == "conv_w" else small[:, lanes[0]:lanes[0] + lanes[1]]
            w_ref, m_ref, v_ref = ins[3 * i:3 * i + 3]
            outs[4 * i][...] = g
            outs[4 * i + 1][...], outs[4 * i + 2][...], outs[4 * i + 3][...] = _adam_math(
                w_ref[...], g, m_ref[...], v_ref[...])

    vmem = pl.BlockSpec(memory_space=pltpu.VMEM)
    out_shape = [jax.ShapeDtypeStruct((1, 128), F32)]
    for w, _, _ in state:
        out_shape += [jax.ShapeDtypeStruct(w.shape, F32)] * 4
    outs = pl.pallas_call(
        body, name="small_update",
        in_specs=[pl.BlockSpec(memory_space=pltpu.SMEM), vmem, vmem] + [vmem] * len(flat)
        + [pl.BlockSpec(memory_space=pl.ANY)] * len(after),
        out_shape=out_shape,
    )(me, packed_all, rel_all, *flat, *after)
    return outs[0], [tuple(outs[1 + 4 * i:5 + 4 * i]) for i in range(n_p)]


def _adam_math(w, g, m, v):
    m = ADAM_B1 * m + (1.0 - ADAM_B1) * g
    v = ADAM_B2 * v + (1.0 - ADAM_B2) * (g * g)
    m_hat = m / (1.0 - ADAM_B1 ** ADAM_STEP)
    v_hat = v / (1.0 - ADAM_B2 ** ADAM_STEP)
    delta = -ADAM_LR * (m_hat / (jnp.sqrt(v_hat) + ADAM_EPS) + ADAM_WD * w)
    return delta, m, v


def _adamw_parts(w, m, v, local, land, me, tr, name):
    r, c = w.shape

    def body(me_ref, w_ref, m_ref, v_ref, own_ref, land_ref, g_ref, d_ref, mo_ref, vo_ref):
        g = own_ref[0].astype(F32)
        for k in range(N_DEV - 1):
            g = g + land_ref[k].astype(F32)
        g_ref[...] = g
        d_ref[...], mo_ref[...], vo_ref[...] = _adam_math(w_ref[...], g, m_ref[...], v_ref[...])

    tile = pl.BlockSpec((tr, c), lambda i, me_ref: (i, 0))
    return pl.pallas_call(
        body, name=name,
        grid_spec=pltpu.PrefetchScalarGridSpec(
            num_scalar_prefetch=1, grid=(r // tr,),
            in_specs=[tile, tile, tile, pl.BlockSpec((1, tr, c), lambda i, me_ref: (me_ref[0], i, 0)),
                      pl.BlockSpec((N_DEV - 1, tr, c), lambda i, me_ref: (0, i, 0))],
            out_specs=[tile] * 4),
        out_shape=[jax.ShapeDtypeStruct((r, c), F32)] * 4,
        compiler_params=_params(("arbitrary",)),
    )(me, w, m, v, local, land)


def _behind(a, token):
    return a + token[0:a.shape[0], 0:1]


def _local_step(x, target, mod, w_in_t, bias, weights_out_gu, weights_down, g_norm1, sinks, conv_w, g_attn,
                g_conv, g_norm2, g_final, exchange):
    s = x.shape[0]
    tm = min(512, s)
    tm_small = min(256, s)
    bucket = _bucket_table()

    h, q, kv, gb, gc, xc = _in_proj(x, mod, g_norm1, w_in_t, tm)
    attn, merged, lse, probs = _mixer_fwd(q, kv, gb, gc, xc, bias, sinks, conv_w, g_attn, g_conv)
    w_out, w_gu_t = weights_out_gu(merged)
    o1, x1 = _out_proj(merged, x, mod, w_out, tm)
    w_down = weights_down(x1)
    h2, act, do2, dgu, dx1, dw_out, dmerged, sm_2 = _ffn(x1, o1, merged, mod, g_norm2, w_gu_t, w_down, w_out, g_final,
                                                         target, tm_small)
    ts = min(WEIGHT_GRAD_ROWS, s)
    tok_out = exchange("w_out", dw_out)
    tok_down = exchange("w_down", _weight_grad(act, do2, D_FF // 2, ts, "w_down_grad", after=tok_out))
    tok_gu = exchange("w_gu", _weight_grad(dgu, h2, D_FF // 2, ts, "w_gu_grad", after=tok_down))
    dproj, d_rel, dsink, sm_mix = _mixer_bwd(
        tok_gu, q, kv, gb, gc, xc, probs, sinks, conv_w, g_attn, g_conv, attn, lse, dmerged, bucket)
    tok_in = exchange("w_in", _weight_grad(dproj, h, IN_PROJ_WIDTH // 2, ts, "w_in_grad"))
    dx, sm_1 = _in_proj_bwd(tok_in, dproj, x, dx1, mod, g_norm1, w_in_t, min(1024, s))

    packed = jnp.concatenate([
        sm_1[0], sm_1[1], sm_2[7], sm_2[0], sm_2[1], sm_2[3],
        sm_1[2],
        sm_mix[5, 0:128],
        sm_mix[0], sm_mix[1],
        sm_2[2],
        sm_2[4],
        sm_mix[2], sm_mix[3], sm_mix[4],
        sm_2[6, 0:128],
    ])[None, :]
    return dx, packed, d_rel


def kernel(x, c, rel_bias, w_ada, b_ada, g_norm1, w_in, sinks, conv_w, g_attn_out, g_conv_out, w_out, g_norm2, w_gu, w_down, g_final, loss_target, m_rel_bias, m_w_ada, m_b_ada, m_g_norm1, m_w_in, m_sinks, m_conv_w, m_g_attn_out, m_g_conv_out, m_w_out, m_g_norm2, m_w_gu, m_w_down, m_g_final, v_rel_bias, v_w_ada, v_b_ada, v_g_norm1, v_w_in, v_sinks, v_conv_w, v_g_attn_out, v_g_conv_out, v_w_out, v_g_norm2, v_w_gu, v_w_down, v_g_final):
    me = _linear(_mesh_position())
    me_arr = jnp.reshape(me, (1,)).astype(jnp.int32)
    ada_cols = w_ada.shape[2]
    tm = min(512, x.shape[1])

    b_cols = lax.dynamic_slice_in_dim(b_ada, me * ada_cols, ada_cols, axis=1)
    cond_all, conv_w_all, mod_all, w_in_blocks, staged, bias = _open_step(
        c, conv_w[0], w_ada[0], b_cols, w_in[0].T, [w_out[0], w_gu[0].T, w_down[0]], rel_bias, _bucket_table())
    cond_all = cond_all[:, 0, :]
    conv_w_full = conv_w_all.transpose(1, 0, 2).reshape(3, CONV_WIDTH)
    mod = lax.dynamic_index_in_dim(mod_all, me, axis=1, keepdims=False).reshape(N_MOD, D_MODEL)
    mod = jnp.concatenate([mod, jnp.zeros((2, D_MODEL), F32)], axis=0)
    w_in_t = w_in_blocks.reshape(IN_PROJ_WIDTH, D_MODEL)
    gather_sems, staged, gather_token = _gather_start(staged, "gather_start_weights")
    mod = _behind(mod, gather_token)

    def weights_out_gu(after):
        got = _gather_pass_on(_gather_wait(gather_sems[0:4], staged[0:2], [after], "gather_wait_out_gu"),
                              "gather_pass_on_out_gu")
        return got[0].reshape(D_MODEL, D_MODEL), got[1].reshape(2 * D_FF, D_MODEL)

    def weights_down(after):
        got = _gather_pass_on(_gather_wait(gather_sems[4:6], staged[2:3], [after], "gather_wait_down"),
                              "gather_pass_on_down")
        return got[0].reshape(D_FF, D_MODEL)

    started = {}

    def exchange(name, dw):
        st = _exchange_start(dw.reshape(N_DEV, dw.shape[0] // N_DEV, dw.shape[1]), "exchange_start_" + name)
        started[name] = st
        return st[4]

    dx, packed, d_rel = _local_step(
        x[0], loss_target[0], mod, w_in_t, bias, weights_out_gu, weights_down, g_norm1, sinks[0], conv_w_full,
        g_attn_out, g_conv_out, g_norm2, g_final[None, :], exchange)

    def zone(a):
        return lax.dynamic_update_slice(jnp.zeros((N_DEV,) + a.shape, F32), a[None], (me,) + (0,) * a.ndim)

    shared = _share_start([packed, d_rel], [zone(packed), zone(d_rel)], "share_small_start")

    def finish(name, after, w, m, v, tr):
        src, land = _exchange_wait(started[name], after, "exchange_wait_" + name)
        return _adamw_parts(w, m, v, src, land, me_arr, tr, "adamw_" + name)

    g_down, d_down, nm_down, nv_down = finish("w_down", [shared[2][0]], w_down[0], m_w_down[0], v_w_down[0], 176)
    g_gu, d_gu, nm_gu, nv_gu = finish("w_gu", [nv_down], w_gu[0].T, m_w_gu[0].T, v_w_gu[0].T, 352)
    g_out, d_out, nm_out, nv_out = finish("w_out", [nv_gu], w_out[0], m_w_out[0], v_w_out[0], 128)

    packed_all, rel_all = _share_wait(shared, [nv_out], "share_small_wait")
    g_ada, d_ada, nm_ada, nv_ada = _adamw_w_ada(me_arr, cond_all, packed_all, w_ada[0], m_w_ada[0], v_w_ada[0], 256)
    as_rows = {"conv_w": lambda a: a[0], "g_final": lambda a: a[None, :]}
    small_state = {
        "rel_bias": (rel_bias, m_rel_bias, v_rel_bias), "b_ada": (b_ada, m_b_ada, v_b_ada),
        "g_norm1": (g_norm1, m_g_norm1, v_g_norm1), "sinks": (sinks, m_sinks, v_sinks),
        "conv_w": (conv_w, m_conv_w, v_conv_w), "g_attn_out": (g_attn_out, m_g_attn_out, v_g_attn_out),
        "g_conv_out": (g_conv_out, m_g_conv_out, v_g_conv_out), "g_norm2": (g_norm2, m_g_norm2, v_g_norm2),
        "g_final": (g_final, m_g_final, v_g_final),
    }
    state = [tuple(as_rows.get(name, lambda a: a)(a) for a in small_state[name]) for name, _ in SMALL_PARAMS]
    loss_row, small_out = _small_update(me_arr, packed_all, rel_all, state, [])
    loss = loss_row[0, 0]
    small_res = {name: tuple(a.reshape(small_state[name][0].shape) for a in res)
                 for (name, _), res in zip(SMALL_PARAMS, small_out)}

    g_in, d_in, nm_in, nv_in = finish("w_in", [loss_row, nv_ada], w_in[0].T, m_w_in[0].T, v_w_in[0].T, 144)

    big = {
        "w_ada": (g_ada[None], d_ada[None], nm_ada[None], nv_ada[None]),
        "w_in": (g_in.T[None], d_in.T[None], nm_in.T[None], nv_in.T[None]),
        "w_out": (g_out[None], d_out[None], nm_out[None], nv_out[None]),
        "w_gu": (g_gu.T[None], d_gu.T[None], nm_gu.T[None], nv_gu.T[None]),
        "w_down": (g_down[None], d_down[None], nm_down[None], nv_down[None]),
    }
    order = ["rel_bias", "w_ada", "b_ada", "g_norm1", "w_in", "sinks", "conv_w", "g_attn_out", "g_conv_out", "w_out",
             "g_norm2", "w_gu", "w_down", "g_final"]
    results = [big[k] if k in big else small_res[k] for k in order]
    return (loss, dx[None], *[r[0] for r in results], *[r[1] for r in results], *[r[2] for r in results],
            *[r[3] for r in results])
```

```python
import math

import jax
import jax.numpy as jnp
from jax import lax
from jax.experimental import pallas as pl
from jax.experimental.pallas import tpu as pltpu

F32 = jnp.float32
BF16 = jnp.bfloat16

D_MODEL = 1024
HEAD_DIM = 64
N_Q_HEADS = 8
ATTN_WIDTH = 512
KV_WIDTH = 128
CONV_WIDTH = 512
IN_PROJ_WIDTH = 2304
D_FF = 2816
N_MOD = 6
N_BUCKETS = 32
MAX_DISTANCE = 128
BLOCK = 128
EPS = 1e-6
NEG_INF = -1e30
SCALE = HEAD_DIM ** -0.5
N_DEV = 8

ADAM_LR = 0.001
ADAM_B1 = 0.9
ADAM_B2 = 0.999
ADAM_EPS = 1e-08
ADAM_WD = 0.01
ADAM_STEP = 10

SH1, SC1, G1, SH2, SC2, G2 = range(6)

VMEM_LIMIT_LARGE = 60 * 1024 * 1024
WEIGHT_GRAD_ROWS = 2048
FFN_CHUNKS = 1
PREV_ROWS = 16
MIXER_BLOCKS = 4
MESH_ID = pl.DeviceIdType.MESH

OFF_DMOD = 0
OFF_GN1 = OFF_DMOD + N_MOD * D_MODEL
OFF_SINK = OFF_GN1 + D_MODEL
OFF_GATT = OFF_SINK + 128
OFF_GCV = OFF_GATT + ATTN_WIDTH
OFF_GN2 = OFF_GCV + CONV_WIDTH
OFF_GFIN = OFF_GN2 + D_MODEL
OFF_CONVW = OFF_GFIN + D_MODEL
OFF_LOSS = OFF_CONVW + 3 * CONV_WIDTH
PACKED = OFF_LOSS + 128


def _params(sem=None, vmem=None):
    return pltpu.CompilerParams(dimension_semantics=sem, vmem_limit_bytes=vmem)


def _coming_behind(body):
    def skipping(after_ref, *refs):
        body(*refs)

    return skipping


ANY_SPEC = pl.BlockSpec(memory_space=pl.ANY)


def _full(shape):
    nd = len(shape)
    return pl.BlockSpec(shape, lambda *_: (0,) * nd)


def _rows(tm, width):
    return pl.BlockSpec((tm, width), lambda i, *_: (i, 0))


def _sigmoid(x):
    return 1.0 / (1.0 + jnp.exp(-x))


def _rsqrt_mean_sq(x):
    return lax.rsqrt(jnp.mean(x * x, axis=-1, keepdims=True) + EPS)


def _colsum(x):
    return jnp.sum(x, axis=0, keepdims=True)


def _dot(a, b):
    return jnp.dot(a, b, preferred_element_type=F32)


def _dot_nt(a, b):
    return lax.dot_general(a, b, (((1,), (1,)), ((), ())), preferred_element_type=F32)


def _dot_tn(a, b):
    return lax.dot_general(a, b, (((0,), (0,)), ((), ())), preferred_element_type=F32)


def _mesh_position():
    return lax.axis_index("x"), lax.axis_index("y"), lax.axis_index("c")


def _linear(p):
    return 4 * p[0] + 2 * p[1] + p[2]


def _peer(k):
    x, y, c = _mesh_position()
    return (1 - x if k & 4 else x, 1 - y if k & 2 else y, 1 - c if k & 1 else c)


HBM_SPEC = pl.BlockSpec(memory_space=pltpu.HBM)
SEM_SPEC = pl.BlockSpec(memory_space=pltpu.SEMAPHORE)
DATAFLOW = pltpu.SideEffectType.DATAFLOW_SIDE_EFFECTING


def _exchange_start(src, name):
    r, c = src.shape[1:]

    def body(src_ref, land_ref, send_sems, recv_sems, src_thru, land_thru, token):
        for k in range(1, N_DEV):
            peer = _peer(k)
            pltpu.make_async_remote_copy(
                src_ref=src_ref.at[_linear(peer)], dst_ref=land_ref.at[k - 1],
                send_sem=send_sems.at[k - 1], recv_sem=recv_sems.at[k - 1],
                device_id=peer, device_id_type=MESH_ID).start()
        token[...] = jnp.zeros_like(token)

    land = lax.empty((N_DEV - 1, r, c), src.dtype)
    return pl.pallas_call(
        body, name=name,
        out_shape=(pltpu.SemaphoreType.DMA((N_DEV - 1,)), pltpu.SemaphoreType.DMA((N_DEV - 1,)),
                   pltpu.HBM(src.shape, src.dtype), pltpu.HBM(land.shape, land.dtype),
                   jax.ShapeDtypeStruct((8, 128), F32)),
        in_specs=(HBM_SPEC, HBM_SPEC),
        out_specs=(SEM_SPEC, SEM_SPEC, HBM_SPEC, HBM_SPEC, pl.BlockSpec(memory_space=pltpu.VMEM)),
        input_output_aliases={0: 2, 1: 3},
        compiler_params=pltpu.CompilerParams(has_side_effects=DATAFLOW),
    )(pltpu.with_memory_space_constraint(src, pltpu.HBM), pltpu.with_memory_space_constraint(land, pltpu.HBM))


def _exchange_wait(started, after, name):
    send_sems, recv_sems, src_thru, land_thru, _ = started

    def body(src_ref, land_ref, send_sems, recv_sems, *rest):
        for k in range(1, N_DEV):
            cp = pltpu.make_async_remote_copy(
                src_ref=src_ref.at[0], dst_ref=land_ref.at[k - 1],
                send_sem=send_sems.at[k - 1], recv_sem=recv_sems.at[k - 1],
                device_id=_peer(k), device_id_type=MESH_ID)
            cp.wait_send()
            cp.wait_recv()

    return pl.pallas_call(
        body, name=name,
        out_shape=(pltpu.HBM(src_thru.shape, src_thru.dtype), pltpu.HBM(land_thru.shape, land_thru.dtype)),
        in_specs=(HBM_SPEC, HBM_SPEC, SEM_SPEC, SEM_SPEC) + (pl.BlockSpec(memory_space=pl.ANY),) * len(after),
        out_specs=(HBM_SPEC, HBM_SPEC), input_output_aliases={0: 0, 1: 1},
        compiler_params=pltpu.CompilerParams(has_side_effects=DATAFLOW),
    )(src_thru, land_thru, send_sems, recv_sems, *after)


def _share_start(arrs, zones, name):
    n = len(arrs)

    def body(*refs):
        src_refs, zone_refs, sems = refs[:n], refs[n:2 * n], refs[2 * n:4 * n]
        me = _linear(_mesh_position())
        for a in range(n):
            for k in range(1, N_DEV):
                pltpu.make_async_remote_copy(
                    src_ref=src_refs[a], dst_ref=zone_refs[a].at[me],
                    send_sem=sems[2 * a].at[k - 1], recv_sem=sems[2 * a + 1].at[k - 1],
                    device_id=_peer(k), device_id_type=MESH_ID).start()

    outs = pl.pallas_call(
        body, name=name,
        out_shape=tuple(pltpu.SemaphoreType.DMA((N_DEV - 1,)) for _ in range(2 * n))
        + tuple(pltpu.HBM(a.shape, a.dtype) for a in arrs) + tuple(pltpu.HBM(z.shape, z.dtype) for z in zones),
        in_specs=(HBM_SPEC,) * (2 * n),
        out_specs=(SEM_SPEC,) * (2 * n) + (HBM_SPEC,) * (2 * n),
        input_output_aliases={i: 2 * n + i for i in range(2 * n)},
        compiler_params=pltpu.CompilerParams(has_side_effects=DATAFLOW),
    )(*[pltpu.with_memory_space_constraint(a, pltpu.HBM) for a in list(arrs) + list(zones)])
    return outs[:2 * n], outs[2 * n:3 * n], outs[3 * n:]


def _share_wait(started, after, name):
    sems, arrs, zones = started
    n = len(arrs)

    def body(*refs):
        src_refs, zone_refs, sem_refs = refs[:n], refs[n:2 * n], refs[2 * n:4 * n]
        for a in range(n):
            for k in range(1, N_DEV):
                cp = pltpu.make_async_remote_copy(
                    src_ref=src_refs[a], dst_ref=zone_refs[a].at[_linear(_peer(k))],
                    send_sem=sem_refs[2 * a].at[k - 1], recv_sem=sem_refs[2 * a + 1].at[k - 1],
                    device_id=_peer(k), device_id_type=MESH_ID)
                cp.wait_send()
                cp.wait_recv()

    outs = pl.pallas_call(
        body, name=name,
        out_shape=tuple(pltpu.HBM(a.shape, a.dtype) for a in arrs) + tuple(pltpu.HBM(z.shape, z.dtype) for z in zones),
        in_specs=(HBM_SPEC,) * (2 * n) + (SEM_SPEC,) * (2 * n) + (pl.BlockSpec(memory_space=pl.ANY),) * len(after),
        out_specs=(HBM_SPEC,) * (2 * n), input_output_aliases={i: i for i in range(2 * n)},
        compiler_params=pltpu.CompilerParams(has_side_effects=DATAFLOW),
    )(*arrs, *zones, *sems, *after)
    return list(outs[n:])


def _same_core_peers():
    x, y, c = _mesh_position()
    return [(x, y, 1 - c), (1 - x, y, c), (x, 1 - y, c), (1 - x, 1 - y, c)]


def _gather_start(bufs, name):
    n = len(bufs)

    def body(*refs):
        buf_refs, rest = refs[:n], refs[n:]
        sems, token = rest[:2 * n], rest[-1]
        me = _linear(_mesh_position())
        for a in range(n):
            for k, peer in enumerate(_same_core_peers()):
                pltpu.make_async_remote_copy(
                    src_ref=buf_refs[a].at[me], dst_ref=buf_refs[a].at[me],
                    send_sem=sems[2 * a].at[k], recv_sem=sems[2 * a + 1].at[k],
                    device_id=peer, device_id_type=MESH_ID).start()
        token[...] = jnp.zeros_like(token)

    outs = pl.pallas_call(
        body, name=name,
        out_shape=tuple(pltpu.SemaphoreType.DMA((4,)) for _ in range(2 * n))
        + tuple(pltpu.HBM(b.shape, b.dtype) for b in bufs) + (jax.ShapeDtypeStruct((8, 128), F32),),
        in_specs=(HBM_SPEC,) * n,
        out_specs=(SEM_SPEC,) * (2 * n) + (HBM_SPEC,) * n + (pl.BlockSpec(memory_space=pltpu.VMEM),),
        input_output_aliases={a: 2 * n + a for a in range(n)},
        compiler_params=pltpu.CompilerParams(has_side_effects=DATAFLOW),
    )(*[pltpu.with_memory_space_constraint(b, pltpu.HBM) for b in bufs])
    return outs[:2 * n], outs[2 * n:3 * n], outs[3 * n]


def _gather_wait(sems, bufs, after, name):
    n = len(bufs)

    def body(*refs):
        buf_refs, sem_refs = refs[:n], refs[n:3 * n]
        x, y, c = _mesh_position()
        me = _linear((x, y, c))
        for a in range(n):
            for k, peer in enumerate(_same_core_peers()):
                cp = pltpu.make_async_remote_copy(
                    src_ref=buf_refs[a].at[me], dst_ref=buf_refs[a].at[_linear(peer)],
                    send_sem=sem_refs[2 * a].at[k], recv_sem=sem_refs[2 * a + 1].at[k],
                    device_id=peer, device_id_type=MESH_ID)
                cp.wait_send()
                cp.wait_recv()

    return list(pl.pallas_call(
        body, name=name,
        out_shape=tuple(pltpu.HBM(b.shape, b.dtype) for b in bufs),
        in_specs=(HBM_SPEC,) * n + (SEM_SPEC,) * (2 * n) + (pl.BlockSpec(memory_space=pl.ANY),) * len(after),
        out_specs=(HBM_SPEC,) * n, input_output_aliases={a: a for a in range(n)},
        compiler_params=pltpu.CompilerParams(has_side_effects=DATAFLOW),
    )(*bufs, *sems, *after))


def _gather_pass_on(bufs, name):
    n = len(bufs)

    def body(*refs):
        out_refs = refs[n:2 * n]
        send_sems, recv_sems = refs[2 * n:]
        x, y, c = _mesh_position()
        sibling = (x, y, 1 - c)
        chips = [(1 - x, y), (x, 1 - y), (1 - x, 1 - y)]
        copies = []
        for a in range(n):
            for j, chip in enumerate(chips):
                block = out_refs[a].at[_linear((*chip, c))]
                copies.append(pltpu.make_async_remote_copy(
                    src_ref=block, dst_ref=block, send_sem=send_sems.at[3 * a + j], recv_sem=recv_sems.at[3 * a + j],
                    device_id=sibling, device_id_type=MESH_ID))
                copies[-1].start()
        for a in range(n):
            for j, chip in enumerate(chips):
                copies[3 * a + j].wait_send()
                theirs = out_refs[a].at[_linear((*chip, 1 - c))]
                pltpu.make_async_remote_copy(
                    src_ref=theirs, dst_ref=theirs, send_sem=send_sems.at[3 * a + j], recv_sem=recv_sems.at[3 * a + j],
                    device_id=sibling, device_id_type=MESH_ID).wait_recv()

    hbm = pl.BlockSpec(memory_space=pl.ANY)
    return list(pl.pallas_call(
        body, name=name,
        out_shape=[jax.ShapeDtypeStruct(b.shape, b.dtype) for b in bufs],
        in_specs=[hbm] * n, out_specs=[hbm] * n, input_output_aliases={a: a for a in range(n)},
        scratch_shapes=[pltpu.SemaphoreType.DMA((3 * n,)), pltpu.SemaphoreType.DMA((3 * n,))],
    )(*bufs))


def _open_step(c, conv_w, w_ada, b_cols, w_in_t, later, rel_bias, bucket):
    cols = w_ada.shape[1]
    n_later = len(later)

    def body(c_ref, cw_ref, wa_ref, b_ref, w_ref, *rest):
        later_refs, rb_ref, bk_ref = rest[:n_later], rest[n_later], rest[n_later + 1]
        cond_ref, conv_ref, mod_ref, win_ref = rest[n_later + 2:n_later + 6]
        staged_refs, bias_ref = rest[n_later + 6:2 * n_later + 6], rest[2 * n_later + 6]
        cond_own, mod_own, stage = rest[2 * n_later + 7:2 * n_later + 10]
        later_stage = rest[2 * n_later + 10:3 * n_later + 10]
        s_send, s_recv, w_send, w_recv, local_sems = rest[3 * n_later + 10:]
        x, y, cc = _mesh_position()
        me = _linear((x, y, cc))
        sibling = (x, y, 1 - cc)
        chips = [(1 - x, y), (x, 1 - y), (1 - x, 1 - y)]
        v = c_ref[...]
        cond_own[...] = v * _sigmoid(v)
        stage[...] = w_ref[...].astype(BF16)

        def small(rnd, a, k, src, dst, slot):
            return pltpu.make_async_remote_copy(
                src_ref=src, dst_ref=dst.at[slot], send_sem=s_send.at[rnd, a, k - 1], recv_sem=s_recv.at[rnd, a, k - 1],
                device_id=_peer(k), device_id_type=MESH_ID)

        def block(p):
            return win_ref.at[_linear(p)]

        def big(k, blk, to, src=None):
            return pltpu.make_async_remote_copy(
                src_ref=block(blk) if src is None else src, dst_ref=block(blk),
                send_sem=w_send.at[k], recv_sem=w_recv.at[k], device_id=to, device_id_type=MESH_ID)

        mine = [pltpu.make_async_copy(cond_own, cond_ref.at[me], local_sems.at[0]),
                pltpu.make_async_copy(cw_ref, conv_ref.at[me], local_sems.at[1]),
                pltpu.make_async_copy(stage, block((x, y, cc)), local_sems.at[2])]
        for cp in mine:
            cp.start()
        sends = []
        for k in range(1, N_DEV):
            sends += [small(0, 0, k, cond_own, cond_ref, me), small(0, 1, k, cw_ref, conv_ref, me)]
        for cp in sends:
            cp.start()
        first = [big(0, (x, y, cc), sibling, src=stage)]
        first += [big(1 + j, (x, y, cc), (*chip, cc), src=stage) for j, chip in enumerate(chips)]
        for cp in first:
            cp.start()
        for a in range(n_later):
            later_stage[a][...] = later_refs[a][...].astype(BF16)
            mine.append(pltpu.make_async_copy(later_stage[a], staged_refs[a].at[me], local_sems.at[4 + a]))
            mine[-1].start()
        _fill_bias_table(rb_ref, bk_ref, bias_ref)
        for k in range(1, N_DEV):
            small(0, 0, k, cond_own, cond_ref, _linear(_peer(k))).wait_recv()
            small(0, 1, k, cw_ref, conv_ref, _linear(_peer(k))).wait_recv()
        mine[0].wait()
        cond_all = jnp.concatenate([cond_ref[k] for k in range(N_DEV)], axis=0)
        mod_own[...] = _dot(cond_all, wa_ref[...]) + b_ref[...]
        mine.append(pltpu.make_async_copy(mod_own, mod_ref.at[me], local_sems.at[3]))
        mine[-1].start()
        second = [small(1, 0, k, mod_own, mod_ref, me) for k in range(1, N_DEV)]
        for cp in second:
            cp.start()
        passed = []
        for j, chip in enumerate(chips):
            big(1 + j, (*chip, cc), (x, y, cc)).wait_recv()
            fwd = big(4 + j, (*chip, cc), sibling)
            fwd.start()
            passed.append(fwd)
        big(0, sibling, (x, y, cc)).wait_recv()
        for j, chip in enumerate(chips):
            big(4 + j, (*chip, 1 - cc), (x, y, cc)).wait_recv()
        for k in range(1, N_DEV):
            small(1, 0, k, mod_own, mod_ref, _linear(_peer(k))).wait_recv()
        for cp in sends + first + second + passed:
            cp.wait_send()
        for cp in mine[1:]:
            cp.wait()

    vmem = pl.BlockSpec(memory_space=pltpu.VMEM)
    outs = pl.pallas_call(
        body, name="open_step",
        out_shape=[jax.ShapeDtypeStruct((N_DEV,) + c.shape, F32), jax.ShapeDtypeStruct((N_DEV,) + conv_w.shape, F32),
                   jax.ShapeDtypeStruct((N_DEV, N_DEV, cols), F32),
                   jax.ShapeDtypeStruct((N_DEV,) + w_in_t.shape, BF16)]
        + [jax.ShapeDtypeStruct((N_DEV,) + a.shape, BF16) for a in later]
        + [jax.ShapeDtypeStruct((N_Q_HEADS, BLOCK, 2 * BLOCK), F32)],
        in_specs=[vmem] * (5 + n_later) + [pl.BlockSpec(memory_space=pltpu.SMEM), vmem],
        out_specs=[vmem, vmem, vmem, ANY_SPEC] + [ANY_SPEC] * n_later + [vmem],
        scratch_shapes=[pltpu.VMEM(c.shape, F32), pltpu.VMEM((N_DEV, cols), F32), pltpu.VMEM(w_in_t.shape, BF16)]
        + [pltpu.VMEM(a.shape, BF16) for a in later]
        + [pltpu.SemaphoreType.DMA((2, 2, N_DEV - 1)), pltpu.SemaphoreType.DMA((2, 2, N_DEV - 1)),
           pltpu.SemaphoreType.DMA((7,)), pltpu.SemaphoreType.DMA((7,)),
           pltpu.SemaphoreType.DMA((4 + n_later,))],
        compiler_params=_params(vmem=VMEM_LIMIT_LARGE),
    )(c, conv_w, w_ada, b_cols, w_in_t, *later, rel_bias, bucket)
    return outs[0], outs[1], outs[2], outs[3], list(outs[4:4 + n_later]), outs[4 + n_later]


def _in_proj(x, mod, g_norm1, w_in, tm):
    s = x.shape[0]

    def body(x_ref, mod_ref, g_ref, w_ref, h_ref, q_ref, kv_ref, gb_ref, gc_ref, xc_ref):
        xf = x_ref[...]
        n = xf * _rsqrt_mean_sq(xf) * g_ref[...]
        h = (n * (1.0 + mod_ref[SC1:SC1 + 1, :]) + mod_ref[SH1:SH1 + 1, :]).astype(BF16)
        h_ref[...] = h
        p = _dot_nt(h, w_ref[...])
        q_ref[...] = p[:, 0:512].astype(BF16)
        kv_ref[...] = p[:, 512:768].astype(BF16)
        gb_ref[...] = p[:, 768:1280].astype(BF16)
        gc_ref[...] = p[:, 1280:1792].astype(BF16)
        xc_ref[...] = p[:, 1792:2304].astype(BF16)

    return pl.pallas_call(
        body, name="in_proj", grid=(s // tm,),
        in_specs=[_rows(tm, D_MODEL), _full((8, D_MODEL)), _full((1, D_MODEL)), _full((IN_PROJ_WIDTH, D_MODEL))],
        out_specs=[_rows(tm, D_MODEL), _rows(tm, 512), _rows(tm, 256), _rows(tm, 512), _rows(tm, 512), _rows(tm, 512)],
        out_shape=[jax.ShapeDtypeStruct((s, D_MODEL), BF16), jax.ShapeDtypeStruct((s, 512), BF16),
                   jax.ShapeDtypeStruct((s, 256), BF16), jax.ShapeDtypeStruct((s, 512), BF16),
                   jax.ShapeDtypeStruct((s, 512), BF16), jax.ShapeDtypeStruct((s, 512), BF16)],
        compiler_params=_params(("arbitrary",), VMEM_LIMIT_LARGE),
    )(x, mod, g_norm1, w_in)


def _t5_bucket(dist):
    max_exact = N_BUCKETS // 2
    is_small = dist < max_exact
    d = jnp.maximum(dist, 1).astype(F32)
    large = max_exact + (jnp.log(d / max_exact) / math.log(MAX_DISTANCE / max_exact)
                         * (N_BUCKETS - max_exact)).astype(jnp.int32)
    large = jnp.minimum(large, N_BUCKETS - 1)
    return jnp.where(is_small, dist, large)


def _bucket_table():
    qi = jnp.arange(BLOCK, dtype=jnp.int32)[:, None]
    sj = jnp.arange(2 * BLOCK, dtype=jnp.int32)[None, :]
    return _t5_bucket(jnp.maximum(qi + BLOCK - sj, 0))


def _window_mask():
    qi = lax.broadcasted_iota(jnp.int32, (BLOCK, 2 * BLOCK), 0)
    sj = lax.broadcasted_iota(jnp.int32, (BLOCK, 2 * BLOCK), 1)
    dist = qi + BLOCK - sj
    return (dist >= 0) & (dist < BLOCK)


def _fill_bias_table(rb_ref, bk_ref, o_ref):
    bk = bk_ref[...]
    inside = _window_mask()
    for h in range(N_Q_HEADS):
        acc = jnp.zeros((BLOCK, 2 * BLOCK), F32)
        for b in range(N_BUCKETS):
            acc = jnp.where(bk == b, rb_ref[b, h], acc)
        o_ref[h] = jnp.where(inside, acc, NEG_INF)


def _load_kv_window(kv_ref, n):
    prev = jnp.maximum(n - 1, 0)
    kvw = jnp.concatenate([kv_ref[pl.ds(pl.multiple_of(prev * BLOCK, BLOCK), BLOCK), :],
                           kv_ref[pl.ds(pl.multiple_of(n * BLOCK, BLOCK), BLOCK), :]], axis=0)
    k, v = kvw[:, 0:128], kvw[:, 128:256]
    k_sw = pltpu.roll(k.astype(F32), 64, 1).astype(BF16)
    v_sw = pltpu.roll(v.astype(F32), 64, 1).astype(BF16)
    return (k, k_sw), (v, v_sw)


def _conv_taps(gc, xc, gc_prev, xc_prev, n):
    u = gc * xc
    before = jnp.where(n > 0, gc_prev.astype(F32) * xc_prev.astype(F32), 0.0)
    last = before.shape[0] - 1
    row = lax.broadcasted_iota(jnp.int32, u.shape, 0)
    u1 = jnp.where(row == 0, before[last:last + 1, :], pltpu.roll(u, 1, 0))
    u2 = jnp.where(row == 0, before[last - 1:last, :],
                   jnp.where(row == 1, before[last:last + 1, :], pltpu.roll(u, 2, 0)))
    return u, u1, u2


def _mixer_fwd(q, kv, gb, gc, xc, bias, sinks, conv_w, g_attn, g_conv):
    s = q.shape[0]
    nb = s // BLOCK

    per_step = min(MIXER_BLOCKS, nb)
    tile = per_step * BLOCK

    def one_block(n, slot, before, sink_ref, q_ref, kv_ref, gb_ref, gc_ref, xc_ref, bias_ref, cw_ref, ga_ref,
                  gcv_ref, attn_ref, merged_ref, lse_ref, p_ref):
        rows = slice(slot * BLOCK, (slot + 1) * BLOCK)
        ks, vs = _load_kv_window(kv_ref, n)
        lane = lax.broadcasted_iota(jnp.int32, (BLOCK, BLOCK), 1)
        low = lane < HEAD_DIM
        col = lax.broadcasted_iota(jnp.int32, (BLOCK, 2 * BLOCK), 1)
        no_prev = (col < BLOCK) & (n == 0)
        lse_all = jnp.zeros((BLOCK, BLOCK), F32)
        pairs = []
        for p in range(4):
            qp = q_ref[rows, 128 * p:128 * (p + 1)].astype(F32)
            kvh = p // 2
            res = []
            for e in range(2):
                h = 2 * p + e
                qm = jnp.where(low if e == 0 else ~low, qp, 0.0).astype(BF16)
                sw = 0 if kvh == e else 1
                sc = _dot_nt(qm, ks[sw]) * SCALE + bias_ref[h]
                sc = jnp.where(no_prev, NEG_INF, sc)
                sink = sink_ref[h]
                m = jnp.maximum(jnp.max(sc, axis=-1, keepdims=True), sink)
                pe = jnp.exp(sc - m)
                den = jnp.sum(pe, axis=-1, keepdims=True) + jnp.exp(sink - m)
                pb = (pe * (1.0 / den)).astype(BF16)
                p_ref[slot, h] = pb
                res.append(_dot(pb, vs[sw]))
                lse_all = lse_all + jnp.where(lane == h, m + jnp.log(den), 0.0)
            pairs.append(jnp.where(low, res[0], res[1]))
        attn = jnp.concatenate(pairs, axis=1)
        attn_ref[rows, :] = attn
        lse_ref[rows, :] = lse_all
        u, u1, u2 = _conv_taps(gc_ref[rows, :].astype(F32), xc_ref[rows, :].astype(F32), before[0], before[1], n)
        cw = cw_ref[...]
        cv = gb_ref[rows, :].astype(F32) * (cw[0:1, :] * u2 + cw[1:2, :] * u1 + cw[2:3, :] * u)
        an = attn * _rsqrt_mean_sq(attn) * ga_ref[...]
        cn = cv * _rsqrt_mean_sq(cv) * gcv_ref[...]
        merged_ref[rows, :] = jnp.concatenate([an, cn], axis=1).astype(BF16)

    def body(sink_ref, q_ref, kv_ref, gb_ref, gc_ref, xc_ref, gcp_ref, xcp_ref, *rest):
        step = pl.program_id(0)
        for sub in range(per_step):
            ahead = slice(sub * BLOCK - PREV_ROWS, sub * BLOCK)
            before = (gcp_ref[...], xcp_ref[...]) if sub == 0 else (gc_ref[ahead, :], xc_ref[ahead, :])
            one_block(step * per_step + sub, sub, before, sink_ref, q_ref, kv_ref, gb_ref, gc_ref, xc_ref, *rest)

    blk = lambda w: pl.BlockSpec((tile, w), lambda n: (n, 0))
    prev8 = pl.BlockSpec((PREV_ROWS, 512), lambda n: (jnp.maximum(n * (tile // PREV_ROWS) - 1, 0), 0))
    return pl.pallas_call(
        body, name="mixer_fwd", grid=(nb // per_step,),
        in_specs=[pl.BlockSpec(memory_space=pltpu.SMEM), blk(512), _full((s, 256)), blk(512), blk(512), blk(512),
                  prev8, prev8, _full((N_Q_HEADS, BLOCK, 2 * BLOCK)), _full((3, 512)), _full((1, 512)),
                  _full((1, 512))],
        out_specs=[blk(512), blk(1024), blk(128),
                   pl.BlockSpec((per_step, N_Q_HEADS, BLOCK, 2 * BLOCK), lambda n: (n, 0, 0, 0))],
        out_shape=[jax.ShapeDtypeStruct((s, 512), F32), jax.ShapeDtypeStruct((s, 1024), BF16),
                   jax.ShapeDtypeStruct((s, 128), F32),
                   jax.ShapeDtypeStruct((nb, N_Q_HEADS, BLOCK, 2 * BLOCK), BF16)],
        compiler_params=_params(("arbitrary",)),
    )(sinks, q, kv, gb, gc, xc, gc, xc, bias, conv_w, g_attn, g_conv)


def _out_proj(merged, x, mod, w_out, tm):
    s = x.shape[0]

    def body(m_ref, x_ref, mod_ref, w_ref, o_ref, x1_ref):
        o = _dot(m_ref[...], w_ref[...])
        o_ref[...] = o.astype(BF16)
        x1_ref[...] = x_ref[...] + mod_ref[G1:G1 + 1, :] * o

    return pl.pallas_call(
        body, name="out_proj", grid=(s // tm,),
        in_specs=[_rows(tm, D_MODEL), _rows(tm, D_MODEL), _full((8, D_MODEL)), _full((D_MODEL, D_MODEL))],
        out_specs=[_rows(tm, D_MODEL), _rows(tm, D_MODEL)],
        out_shape=[jax.ShapeDtypeStruct((s, D_MODEL), BF16), jax.ShapeDtypeStruct((s, D_MODEL), F32)],
        compiler_params=_params(("arbitrary",)),
    )(merged, x, mod, w_out)


def _resident(shape):
    nd = len(shape)
    return pl.BlockSpec(shape, lambda *_: (0,) * nd, pipeline_mode=pl.Buffered(1))


def _ffn(x1, o1, merged, mod, g_norm2, w_gu, w_down, w_out, g_final, target, tm):
    s = x1.shape[0]
    chunk = D_FF // FFN_CHUNKS

    def body(x_ref, o1_ref, mg_ref, mod_ref, g_ref, wgu_ref, wd_ref, wo_ref, gf_ref, t_ref,
             h_ref, act_ref, do_ref, dgu_ref, dx1_ref, dwo_ref, dm_ref, small_ref, dwo_acc):
        @pl.when(pl.program_id(0) == 0)
        def _():
            small_ref[...] = jnp.zeros_like(small_ref)
            dwo_acc[...] = jnp.zeros_like(dwo_acc)

        xf = x_ref[...]
        n = xf * _rsqrt_mean_sq(xf) * g_ref[...]
        h = (n * (1.0 + mod_ref[SC2:SC2 + 1, :]) + mod_ref[SH2:SH2 + 1, :]).astype(BF16)
        h_ref[...] = h
        gates, ups, o = [], [], None
        for j in range(FFN_CHUNKS):
            lo = j * chunk
            gate = _dot_nt(h, wgu_ref[lo:lo + chunk, :])
            up = _dot_nt(h, wgu_ref[D_FF + lo:D_FF + lo + chunk, :])
            sg = _sigmoid(gate)
            act = (gate * sg * up).astype(BF16)
            act_ref[:, lo:lo + chunk] = act
            gates.append((up * (sg * (1.0 + gate * (1.0 - sg)))).astype(BF16))
            ups.append((gate * sg).astype(BF16))
            part = _dot(act, wd_ref[lo:lo + chunk, :])
            o = part if o is None else o + part
        g2 = mod_ref[G2:G2 + 1, :]
        x2 = xf + g2 * o
        r = _rsqrt_mean_sq(x2)
        xn = x2 * r
        gf = gf_ref[...]
        err = xn * gf - t_ref[...]
        dy = err * (1.0 / D_MODEL)
        dxn = dy * gf
        dx2 = r * (dxn - xn * jnp.mean(dxn * xn, axis=-1, keepdims=True))
        small_ref[4:5, :] += _colsum(dy * xn)
        small_ref[5:6, :] += _colsum(err * err)
        small_ref[3:4, :] += _colsum(dx2 * o)
        do = (dx2 * g2).astype(BF16)
        do_ref[...] = do
        dh = None
        for j in range(FFN_CHUNKS):
            lo = j * chunk
            dact = _dot_nt(do, wd_ref[lo:lo + chunk, :])
            dgate = (dact * gates[j].astype(F32)).astype(BF16)
            dup = (dact * ups[j].astype(F32)).astype(BF16)
            dgu_ref[:, lo:lo + chunk] = dgate
            dgu_ref[:, D_FF + lo:D_FF + lo + chunk] = dup
            part = _dot(dgate, wgu_ref[lo:lo + chunk, :]) + _dot(dup, wgu_ref[D_FF + lo:D_FF + lo + chunk, :])
            dh = part if dh is None else dh + part
        dx1 = dx2 + _norm_mod_bwd(dh, xf, g_ref[...], mod_ref[SC2:SC2 + 1, :], small_ref)
        dx1_ref[...] = dx1.astype(BF16)
        small_ref[7:8, :] += _colsum(dx1 * o1_ref[...].astype(F32))
        do1 = (dx1 * mod_ref[G1:G1 + 1, :]).astype(BF16)
        dm_ref[...] = _dot_nt(do1, wo_ref[...]).astype(BF16)
        dwo = dwo_acc[...] + _dot_tn(mg_ref[...], do1)
        dwo_acc[...] = dwo
        dwo_ref[...] = dwo.astype(BF16)

        @pl.when(pl.program_id(0) == pl.num_programs(0) - 1)
        def _():
            total = jnp.sum(small_ref[5:6, :], axis=-1, keepdims=True) * (0.5 / D_MODEL)
            small_ref[6:7, :] = jnp.broadcast_to(total, (1, D_MODEL))

    narrow = jax.ShapeDtypeStruct((s, D_MODEL), BF16)
    return pl.pallas_call(
        body, name="ffn", grid=(s // tm,),
        in_specs=[_rows(tm, D_MODEL), _rows(tm, D_MODEL), _rows(tm, D_MODEL), _full((8, D_MODEL)), _full((1, D_MODEL)),
                  _resident((2 * D_FF, D_MODEL)), _resident((D_FF, D_MODEL)), _resident((D_MODEL, D_MODEL)),
                  _full((1, D_MODEL)), _rows(tm, D_MODEL)],
        out_specs=[_rows(tm, D_MODEL), _rows(tm, D_FF), _rows(tm, D_MODEL), _rows(tm, 2 * D_FF), _rows(tm, D_MODEL),
                   _full((D_MODEL, D_MODEL)), _rows(tm, D_MODEL), _full((8, D_MODEL))],
        out_shape=[narrow, jax.ShapeDtypeStruct((s, D_FF), BF16), narrow, jax.ShapeDtypeStruct((s, 2 * D_FF), BF16),
                   narrow, jax.ShapeDtypeStruct((D_MODEL, D_MODEL), BF16), narrow,
                   jax.ShapeDtypeStruct((8, D_MODEL), F32)],
        scratch_shapes=[pltpu.VMEM((D_MODEL, D_MODEL), F32)],
        compiler_params=_params(("arbitrary",), VMEM_LIMIT_LARGE),
    )(x1, o1, merged, mod, g_norm2, w_gu, w_down, w_out, g_final, target)


def _norm_mod_bwd(dh, xf, g, scale_row, small_ref):
    r = _rsqrt_mean_sq(xf)
    xn = xf * r
    small_ref[0:1, :] += _colsum(dh)
    small_ref[1:2, :] += _colsum(dh * (xn * g))
    dn = dh * (1.0 + scale_row)
    small_ref[2:3, :] += _colsum(dn * xn)
    dxn = dn * g
    return r * (dxn - xn * jnp.mean(dxn * xn, axis=-1, keepdims=True))


def _group_norm_bwd(dm, a, g):
    r = _rsqrt_mean_sq(a)
    an = a * r
    dan = dm * g
    return r * (dan - an * jnp.mean(dan * an, axis=-1, keepdims=True)), _colsum(dm * an)


def _sum_by_bucket(db_ref, bk_ref, o_ref, rows_ref):
    bk = bk_ref[...]
    for b in range(N_BUCKETS):
        sel = (bk == b).astype(F32)
        for h in range(N_Q_HEADS):
            rows_ref[N_BUCKETS * h + b:N_BUCKETS * h + b + 1, :] = _colsum(db_ref[h] * sel)
    head = lax.broadcasted_iota(jnp.int32, (N_BUCKETS, N_Q_HEADS), 1)
    out = jnp.zeros((N_BUCKETS, N_Q_HEADS), F32)
    for h in range(N_Q_HEADS):
        per_bucket = jnp.sum(rows_ref[N_BUCKETS * h:N_BUCKETS * (h + 1), :], axis=-1, keepdims=True)
        out = out + jnp.where(head == h, per_bucket, 0.0)
    o_ref[...] = out


def _mixer_bwd(after, q, kv, gb, gc, xc, probs, sinks, conv_w, g_attn, g_conv, attn, lse, dmerged, bucket):
    s = q.shape[0]
    nb = s // BLOCK

    per_step = min(MIXER_BLOCKS, nb)
    tile = per_step * BLOCK
    steps = nb // per_step

    def one_block(n, slot, before, nxt, sink_ref, q_ref, kv_ref, gb_ref, gc_ref, xc_ref, p_ref, cw_ref, ga_ref,
                  gcv_ref, attn_ref, lse_ref, dm_ref, dproj_ref, dbias_ref, dsink_ref, small_ref):
        rows = slice(slot * BLOCK, (slot + 1) * BLOCK)
        next_dy, next_dkv = nxt
        dm = dm_ref[rows, :].astype(F32)
        gbv, gcv_, xcv = gb_ref[rows, :].astype(F32), gc_ref[rows, :].astype(F32), xc_ref[rows, :].astype(F32)
        u, u1, u2 = _conv_taps(gcv_, xcv, before[0], before[1], n)
        cw = cw_ref[...]
        yv = cw[0:1, :] * u2 + cw[1:2, :] * u1 + cw[2:3, :] * u
        dcv, dg_conv = _group_norm_bwd(dm[:, 512:1024], gbv * yv, gcv_ref[...])
        small_ref[1:2, :] += dg_conv
        dproj_ref[rows, 768:1280] = (dcv * yv).astype(BF16)
        dy = dcv * gbv
        row = lax.broadcasted_iota(jnp.int32, dy.shape, 0)
        d1 = jnp.where(row == BLOCK - 1, next_dy[0:1, :], pltpu.roll(dy, BLOCK - 1, 0))
        d2 = jnp.where(row == BLOCK - 2, next_dy[0:1, :],
                       jnp.where(row == BLOCK - 1, next_dy[1:2, :], pltpu.roll(dy, BLOCK - 2, 0)))
        du = cw[2:3, :] * dy + cw[1:2, :] * d1 + cw[0:1, :] * d2
        dproj_ref[rows, 1280:1792] = (du * xcv).astype(BF16)
        dproj_ref[rows, 1792:2304] = (du * gcv_).astype(BF16)
        small_ref[2:3, :] += _colsum(dy * u2)
        small_ref[3:4, :] += _colsum(dy * u1)
        small_ref[4:5, :] += _colsum(dy * u)

        attn_v = attn_ref[rows, :]
        dout, dg_attn = _group_norm_bwd(dm[:, 0:512], attn_v, ga_ref[...])
        small_ref[0:1, :] += dg_attn
        ks, vs = _load_kv_window(kv_ref, n)
        lane = lax.broadcasted_iota(jnp.int32, (BLOCK, BLOCK), 1)
        low = lane < HEAD_DIM
        lse_all = lse_ref[rows, :]
        dsink = jnp.zeros((BLOCK, BLOCK), F32)
        dq_pairs = []
        dk_groups, dv_groups = [], []
        for kvh in range(2):
            ds_rows, pr_rows, q_rows, do_rows = [], [], [], []
            for p in (2 * kvh, 2 * kvh + 1):
                qp = q_ref[rows, 128 * p:128 * (p + 1)].astype(F32)
                do_p = dout[:, 128 * p:128 * (p + 1)]
                prod = do_p * attn_v[:, 128 * p:128 * (p + 1)]
                res = []
                for e in range(2):
                    h = 2 * p + e
                    half = low if e == 0 else ~low
                    qm = jnp.where(half, qp, 0.0).astype(BF16)
                    dom = jnp.where(half, do_p, 0.0).astype(BF16)
                    delta = jnp.sum(jnp.where(half, prod, 0.0), axis=-1, keepdims=True)
                    lse_h = jnp.sum(jnp.where(lane == h, lse_all, 0.0), axis=-1, keepdims=True)
                    sw = 0 if kvh == e else 1
                    pb = p_ref[slot, h]
                    dp = _dot_nt(dom, vs[sw])
                    ds = pb.astype(F32) * (dp - delta)
                    dbias_ref[h] += ds
                    dsink = dsink + jnp.where(lane == h, -jnp.exp(sink_ref[h] - lse_h) * delta, 0.0)
                    dsb = ds.astype(BF16)
                    res.append(_dot(dsb, ks[sw]) * SCALE)
                    ds_rows.append(dsb)
                    pr_rows.append(pb)
                    q_rows.append(qm)
                    do_rows.append(dom)
                dq_pairs.append(jnp.where(low, res[0], res[1]))
            dk_g = _dot_tn(jnp.concatenate(ds_rows, axis=0), jnp.concatenate(q_rows, axis=0)) * SCALE
            dv_g = _dot_tn(jnp.concatenate(pr_rows, axis=0), jnp.concatenate(do_rows, axis=0))
            dk_groups.append(dk_g + pltpu.roll(dk_g, 64, 1))
            dv_groups.append(dv_g + pltpu.roll(dv_g, 64, 1))
        dproj_ref[rows, 0:512] = jnp.concatenate(dq_pairs, axis=1).astype(BF16)
        dsink_ref[...] += dsink
        low_kv = lax.broadcasted_iota(jnp.int32, (2 * BLOCK, BLOCK), 1) < HEAD_DIM
        dkv_win = jnp.concatenate([jnp.where(low_kv, dk_groups[0], dk_groups[1]),
                                   jnp.where(low_kv, dv_groups[0], dv_groups[1])], axis=1)
        dproj_ref[rows, 512:768] = (dkv_win[BLOCK:2 * BLOCK, :] + next_dkv).astype(BF16)
        return dy[0:8, :], dkv_win[0:BLOCK, :]

    def body(sink_ref, q_ref, kv_ref, gb_ref, gc_ref, xc_ref, gcp_ref, xcp_ref, p_ref, cw_ref, ga_ref, gcv_ref,
             attn_ref, lse_ref, dm_ref, bk_ref, dproj_ref, drel_ref, dsink_ref, small_ref,
             dy_ref, dkv_ref, dbias_ref, rows_ref):
        refs = (p_ref, cw_ref, ga_ref, gcv_ref, attn_ref, lse_ref, dm_ref, dproj_ref, dbias_ref, dsink_ref, small_ref)
        step = pl.program_id(0)

        @pl.when(step == 0)
        def _():
            dbias_ref[...] = jnp.zeros_like(dbias_ref)
            dsink_ref[...] = jnp.zeros_like(dsink_ref)
            small_ref[...] = jnp.zeros_like(small_ref)
            dy_ref[...] = jnp.zeros_like(dy_ref)
            dkv_ref[...] = jnp.zeros_like(dkv_ref)

        nxt = (dy_ref[...], dkv_ref[...])
        for sub in reversed(range(per_step)):
            ahead = slice(sub * BLOCK - PREV_ROWS, sub * BLOCK)
            before = (gcp_ref[...], xcp_ref[...]) if sub == 0 else (gc_ref[ahead, :], xc_ref[ahead, :])
            nxt = one_block((steps - 1 - step) * per_step + sub, sub, before, nxt,
                            sink_ref, q_ref, kv_ref, gb_ref, gc_ref, xc_ref, *refs)
        dy_ref[...], dkv_ref[...] = nxt

        @pl.when(step == steps - 1)
        def _():
            small_ref[5:6, :] = jnp.concatenate([_colsum(dsink_ref[...]), jnp.zeros((1, 512 - BLOCK), F32)], axis=1)
            _sum_by_bucket(dbias_ref, bk_ref, drel_ref, rows_ref)

    blk = lambda w: pl.BlockSpec((tile, w), lambda t: (steps - 1 - t, 0))
    prev8 = pl.BlockSpec((PREV_ROWS, 512),
                         lambda t: (jnp.maximum((steps - 1 - t) * (tile // PREV_ROWS) - 1, 0), 0))
    bf = lambda w: jax.ShapeDtypeStruct((s, w), BF16)
    return pl.pallas_call(
        _coming_behind(body), name="mixer_bwd", grid=(steps,),
        in_specs=[ANY_SPEC, pl.BlockSpec(memory_space=pltpu.SMEM), blk(512), _full((s, 256)), blk(512), blk(512), blk(512),
                  prev8, prev8,
                  pl.BlockSpec((per_step, N_Q_HEADS, BLOCK, 2 * BLOCK), lambda t: (steps - 1 - t, 0, 0, 0)),
                  _full((3, 512)), _full((1, 512)), _full((1, 512)), blk(512), blk(128), blk(1024),
                  _full((BLOCK, 2 * BLOCK))],
        out_specs=[blk(IN_PROJ_WIDTH), _full((N_BUCKETS, N_Q_HEADS)), _full((BLOCK, BLOCK)), _full((8, 512))],
        out_shape=[bf(IN_PROJ_WIDTH), jax.ShapeDtypeStruct((N_BUCKETS, N_Q_HEADS), F32),
                   jax.ShapeDtypeStruct((BLOCK, BLOCK), F32), jax.ShapeDtypeStruct((8, 512), F32)],
        scratch_shapes=[pltpu.VMEM((8, 512), F32), pltpu.VMEM((BLOCK, 2 * KV_WIDTH), F32),
                        pltpu.VMEM((N_Q_HEADS, BLOCK, 2 * BLOCK), F32),
                        pltpu.VMEM((N_BUCKETS * N_Q_HEADS, 2 * BLOCK), F32)],
        compiler_params=_params(("arbitrary",), VMEM_LIMIT_LARGE),
    )(after, sinks, q, kv, gb, gc, xc, gc, xc, probs, conv_w, g_attn, g_conv, attn, lse, dmerged, bucket)


def _in_proj_bwd(after, dproj, x, dx1, mod, g_norm1, w_in, tm):
    s = x.shape[0]

    def body(dproj_ref, x_ref, dx1_ref, mod_ref, g_ref, w_ref, dx_ref, small_ref):
        @pl.when(pl.program_id(0) == 0)
        def _():
            small_ref[...] = jnp.zeros_like(small_ref)

        dh = _dot(dproj_ref[...], w_ref[...])
        dx_ref[...] = dx1_ref[...].astype(F32) + _norm_mod_bwd(dh, x_ref[...], g_ref[...], mod_ref[SC1:SC1 + 1, :],
                                                               small_ref)

    return pl.pallas_call(
        _coming_behind(body), name="in_proj_bwd", grid=(s // tm,),
        in_specs=[ANY_SPEC, _rows(tm, IN_PROJ_WIDTH), _rows(tm, D_MODEL), _rows(tm, D_MODEL), _full((8, D_MODEL)),
                  _full((1, D_MODEL)), _full((IN_PROJ_WIDTH, D_MODEL))],
        out_specs=[_rows(tm, D_MODEL), _full((8, D_MODEL))],
        out_shape=[jax.ShapeDtypeStruct((s, D_MODEL), F32), jax.ShapeDtypeStruct((8, D_MODEL), F32)],
        compiler_params=_params(("arbitrary",), VMEM_LIMIT_LARGE),
    )(after, dproj, x, dx1, mod, g_norm1, w_in)


def _weight_grad(a, b, tk, ts, name, after=None):
    s, k = a.shape
    n = b.shape[1]
    nt = s // ts
    extra = [] if after is None else [after]

    def body(a_ref, b_ref, *rest):
        o_ref, acc_ref = rest[-2:]
        t = pl.program_id(1)
        @pl.when(t == 0)
        def _():
            acc_ref[...] = jnp.zeros_like(acc_ref)

        acc = acc_ref[...] + _dot_tn(a_ref[...], b_ref[...])
        acc_ref[...] = acc
        o_ref[...] = acc.astype(BF16)

    return pl.pallas_call(
        body, name=name, grid=(k // tk, nt),
        in_specs=[pl.BlockSpec((ts, tk), lambda i, t: (t, i)), pl.BlockSpec((ts, n), lambda i, t: (t, 0))]
        + [ANY_SPEC] * len(extra),
        out_specs=pl.BlockSpec((tk, n), lambda i, t: (i, 0)),
        out_shape=jax.ShapeDtypeStruct((k, n), BF16),
        scratch_shapes=[pltpu.VMEM((tk, n), F32)],
        compiler_params=_params(("arbitrary", "arbitrary"), VMEM_LIMIT_LARGE),
    )(a, b, *extra)


def _lanes_from(x, start, width):
    n = x.shape[1]
    return pltpu.roll(x, (n - start) % n, 1)[:, 0:width]


def _adamw_w_ada(me, cond_all, packed_all, w, m, v, tr):
    r, cols = w.shape

    def body(me_ref, c_ref, p_ref, w_ref, m_ref, v_ref, g_ref, d_ref, mo_ref, vo_ref):
        dmod = jnp.concatenate([p_ref[k][:, OFF_DMOD:OFF_DMOD + N_MOD * D_MODEL] for k in range(N_DEV)], axis=0)
        mine = _lanes_from(dmod, me_ref[0] * cols, cols)
        pad = lambda a: jnp.concatenate([a, jnp.zeros((128 - N_DEV, a.shape[1]), F32)], axis=0)
        g = _dot_tn(pad(c_ref[...]), pad(mine))
        g_ref[...] = g
        d_ref[...], mo_ref[...], vo_ref[...] = _adam_math(w_ref[...], g, m_ref[...], v_ref[...])

    tile = pl.BlockSpec((tr, cols), lambda i, me_ref: (i, 0))
    return pl.pallas_call(
        body, name="adamw_w_ada",
        grid_spec=pltpu.PrefetchScalarGridSpec(
            num_scalar_prefetch=1, grid=(r // tr,),
            in_specs=[pl.BlockSpec((N_DEV, tr), lambda i, me_ref: (0, i)),
                      pl.BlockSpec(packed_all.shape, lambda i, me_ref: (0, 0, 0)), tile, tile, tile],
            out_specs=[tile] * 4),
        out_shape=[jax.ShapeDtypeStruct((r, cols), F32)] * 4,
        compiler_params=_params(("arbitrary",)),
    )(me, cond_all, packed_all, w, m, v)


SMALL_PARAMS = (("rel_bias", None), ("b_ada", (OFF_DMOD, N_MOD * D_MODEL)), ("g_norm1", (OFF_GN1, D_MODEL)),
                ("sinks", (OFF_SINK, N_Q_HEADS)), ("conv_w", None), ("g_attn_out", (OFF_GATT, ATTN_WIDTH)),
                ("g_conv_out", (OFF_GCV, CONV_WIDTH)), ("g_norm2", (OFF_GN2, D_MODEL)),
                ("g_final", (OFF_GFIN, D_MODEL)))


def _small_update(me, packed_all, rel_all, state, after):
    n_p = len(SMALL_PARAMS)
    flat = [a for triple in state for a in triple]
    conv_cols = state[4][0].shape[1]

    def body(me_ref, p_ref, r_ref, *refs):
        ins = refs[:3 * n_p]
        loss_ref, outs = refs[3 * n_p + len(after)], refs[3 * n_p + len(after) + 1:]
        small, rel = p_ref[0], r_ref[0]
        for k in range(1, N_DEV):
            small = small + p_ref[k]
            rel = rel + r_ref[k]
        loss_ref[...] = small[:, OFF_LOSS:OFF_LOSS + 128]
        taps = jnp.concatenate([small[:, OFF_CONVW + CONV_WIDTH * j:OFF_CONVW + CONV_WIDTH * (j + 1)]
                                for j in range(3)] + [jnp.zeros((5, CONV_WIDTH), F32)], axis=0)
        conv_g = _lanes_from(taps, me_ref[0] * conv_cols, conv_cols)[0:3, :]
        for i, (name, lanes) in enumerate(SMALL_PARAMS):
            g = rel if name == "rel_bias" else conv_g if name == "conv_w" else small[:, lanes[0]:lanes[0] + lanes[1]]
            w_ref, m_ref, v_ref = ins[3 * i:3 * i + 3]
            outs[4 * i][...] = g
            outs[4 * i + 1][...], outs[4 * i + 2][...], outs[4 * i + 3][...] = _adam_math(
                w_ref[...], g, m_ref[...], v_ref[...])

    vmem = pl.BlockSpec(memory_space=pltpu.VMEM)
    out_shape = [jax.ShapeDtypeStruct((1, 128), F32)]
    for w, _, _ in state:
        out_shape += [jax.ShapeDtypeStruct(w.shape, F32)] * 4
    outs = pl.pallas_call(
        body, name="small_update",
        in_specs=[pl.BlockSpec(memory_space=pltpu.SMEM), vmem, vmem] + [vmem] * len(flat)
        + [pl.BlockSpec(memory_space=pl.ANY)] * len(after),
        out_shape=out_shape,
    )(me, packed_all, rel_all, *flat, *after)
    return outs[0], [tuple(outs[1 + 4 * i:5 + 4 * i]) for i in range(n_p)]


def _adam_math(w, g, m, v):
    m = ADAM_B1 * m + (1.0 - ADAM_B1) * g
    v = ADAM_B2 * v + (1.0 - ADAM_B2) * (g * g)
    m_hat = m / (1.0 - ADAM_B1 ** ADAM_STEP)
    v_hat = v / (1.0 - ADAM_B2 ** ADAM_STEP)
    delta = -ADAM_LR * (m_hat / (jnp.sqrt(v_hat) + ADAM_EPS) + ADAM_WD * w)
    return delta, m, v


def _adamw_parts(w, m, v, local, land, me, tr, name):
    r, c = w.shape

    def body(me_ref, w_ref, m_ref, v_ref, own_ref, land_ref, g_ref, d_ref, mo_ref, vo_ref):
        g = own_ref[0].astype(F32)
        for k in range(N_DEV - 1):
            g = g + land_ref[k].astype(F32)
        g_ref[...] = g
        d_ref[...], mo_ref[...], vo_ref[...] = _adam_math(w_ref[...], g, m_ref[...], v_ref[...])

    tile = pl.BlockSpec((tr, c), lambda i, me_ref: (i, 0))
    return pl.pallas_call(
        body, name=name,
        grid_spec=pltpu.PrefetchScalarGridSpec(
            num_scalar_prefetch=1, grid=(r // tr,),
            in_specs=[tile, tile, tile, pl.BlockSpec((1, tr, c), lambda i, me_ref: (me_ref[0], i, 0)),
                      pl.BlockSpec((N_DEV - 1, tr, c), lambda i, me_ref: (0, i, 0))],
            out_specs=[tile] * 4),
        out_shape=[jax.ShapeDtypeStruct((r, c), F32)] * 4,
        compiler_params=_params(("arbitrary",)),
    )(me, w, m, v, local, land)


def _behind(a, token):
    return a + token[0:a.shape[0], 0:1]


def _local_step(x, target, mod, w_in_t, bias, weights_out_gu, weights_down, g_norm1, sinks, conv_w, g_attn,
                g_conv, g_norm2, g_final, exchange):
    s = x.shape[0]
    tm = min(512, s)
    tm_small = min(256, s)
    bucket = _bucket_table()

    h, q, kv, gb, gc, xc = _in_proj(x, mod, g_norm1, w_in_t, tm)
    attn, merged, lse, probs = _mixer_fwd(q, kv, gb, gc, xc, bias, sinks, conv_w, g_attn, g_conv)
    w_out, w_gu_t = weights_out_gu(merged)
    o1, x1 = _out_proj(merged, x, mod, w_out, tm)
    w_down = weights_down(x1)
    h2, act, do2, dgu, dx1, dw_out, dmerged, sm_2 = _ffn(x1, o1, merged, mod, g_norm2, w_gu_t, w_down, w_out, g_final,
                                                         target, tm_small)
    ts = min(WEIGHT_GRAD_ROWS, s)
    tok_out = exchange("w_out", dw_out)
    tok_down = exchange("w_down", _weight_grad(act, do2, D_FF // 2, ts, "w_down_grad", after=tok_out))
    tok_gu = exchange("w_gu", _weight_grad(dgu, h2, D_FF // 2, ts, "w_gu_grad", after=tok_down))
    dproj, d_rel, dsink, sm_mix = _mixer_bwd(
        tok_gu, q, kv, gb, gc, xc, probs, sinks, conv_w, g_attn, g_conv, attn, lse, dmerged, bucket)
    tok_in = exchange("w_in", _weight_grad(dproj, h, IN_PROJ_WIDTH // 2, ts, "w_in_grad"))
    dx, sm_1 = _in_proj_bwd(tok_in, dproj, x, dx1, mod, g_norm1, w_in_t, min(1024, s))

    packed = jnp.concatenate([
        sm_1[0], sm_1[1], sm_2[7], sm_2[0], sm_2[1], sm_2[3],
        sm_1[2],
        sm_mix[5, 0:128],
        sm_mix[0], sm_mix[1],
        sm_2[2],
        sm_2[4],
        sm_mix[2], sm_mix[3], sm_mix[4],
        sm_2[6, 0:128],
    ])[None, :]
    return dx, packed, d_rel


def kernel(x, c, rel_bias, w_ada, b_ada, g_norm1, w_in, sinks, conv_w, g_attn_out, g_conv_out, w_out, g_norm2, w_gu, w_down, g_final, loss_target, m_rel_bias, m_w_ada, m_b_ada, m_g_norm1, m_w_in, m_sinks, m_conv_w, m_g_attn_out, m_g_conv_out, m_w_out, m_g_norm2, m_w_gu, m_w_down, m_g_final, v_rel_bias, v_w_ada, v_b_ada, v_g_norm1, v_w_in, v_sinks, v_conv_w, v_g_attn_out, v_g_conv_out, v_w_out, v_g_norm2, v_w_gu, v_w_down, v_g_final):
    me = _linear(_mesh_position())
    me_arr = jnp.reshape(me, (1,)).astype(jnp.int32)
    ada_cols = w_ada.shape[2]
    tm = min(512, x.shape[1])

    b_cols = lax.dynamic_slice_in_dim(b_ada, me * ada_cols, ada_cols, axis=1)
    cond_all, conv_w_all, mod_all, w_in_blocks, staged, bias = _open_step(
        c, conv_w[0], w_ada[0], b_cols, w_in[0].T, [w_out[0], w_gu[0].T, w_down[0]], rel_bias, _bucket_table())
    cond_all = cond_all[:, 0, :]
    conv_w_full = conv_w_all.transpose(1, 0, 2).reshape(3, CONV_WIDTH)
    mod = lax.dynamic_index_in_dim(mod_all, me, axis=1, keepdims=False).reshape(N_MOD, D_MODEL)
    mod = jnp.concatenate([mod, jnp.zeros((2, D_MODEL), F32)], axis=0)
    w_in_t = w_in_blocks.reshape(IN_PROJ_WIDTH, D_MODEL)
    gather_sems, staged, gather_token = _gather_start(staged, "gather_start_weights")
    mod = _behind(mod, gather_token)

    def weights_out_gu(after):
        got = _gather_pass_on(_gather_wait(gather_sems[0:4], staged[0:2], [after], "gather_wait_out_gu"),
                              "gather_pass_on_out_gu")
        return got[0].reshape(D_MODEL, D_MODEL), got[1].reshape(2 * D_FF, D_MODEL)

    def weights_down(after):
        got = _gather_pass_on(_gather_wait(gather_sems[4:6], staged[2:3], [after], "gather_wait_down"),
                              "gather_pass_on_down")
        return got[0].reshape(D_FF, D_MODEL)

    started = {}

    def exchange(name, dw):
        st = _exchange_start(dw.reshape(N_DEV, dw.shape[0] // N_DEV, dw.shape[1]), "exchange_start_" + name)
        started[name] = st
        return st[4]

    dx, packed, d_rel = _local_step(
        x[0], loss_target[0], mod, w_in_t, bias, weights_out_gu, weights_down, g_norm1, sinks[0], conv_w_full,
        g_attn_out, g_conv_out, g_norm2, g_final[None, :], exchange)

    def zone(a):
        return lax.dynamic_update_slice(jnp.zeros((N_DEV,) + a.shape, F32), a[None], (me,) + (0,) * a.ndim)

    shared = _share_start([packed, d_rel], [zone(packed), zone(d_rel)], "share_small_start")

    def finish(name, after, w, m, v, tr):
        src, land = _exchange_wait(started[name], after, "exchange_wait_" + name)
        return _adamw_parts(w, m, v, src, land, me_arr, tr, "adamw_" + name)

    g_down, d_down, nm_down, nv_down = finish("w_down", [shared[2][0]], w_down[0], m_w_down[0], v_w_down[0], 176)
    g_gu, d_gu, nm_gu, nv_gu = finish("w_gu", [nv_down], w_gu[0].T, m_w_gu[0].T, v_w_gu[0].T, 352)
    g_out, d_out, nm_out, nv_out = finish("w_out", [nv_gu], w_out[0], m_w_out[0], v_w_out[0], 128)
    g_in, d_in, nm_in, nv_in = finish("w_in", [nv_out], w_in[0].T, m_w_in[0].T, v_w_in[0].T, 144)

    packed_all, rel_all = _share_wait(shared, [nv_in], "share_small_wait")
    g_ada, d_ada, nm_ada, nv_ada = _adamw_w_ada(me_arr, cond_all, packed_all, w_ada[0], m_w_ada[0], v_w_ada[0], 256)
    as_rows = {"conv_w": lambda a: a[0], "g_final": lambda a: a[None, :]}
    small_state = {
        "rel_bias": (rel_bias, m_rel_bias, v_rel_bias), "b_ada": (b_ada, m_b_ada, v_b_ada),
        "g_norm1": (g_norm1, m_g_norm1, v_g_norm1), "sinks": (sinks, m_sinks, v_sinks),
        "conv_w": (conv_w, m_conv_w, v_conv_w), "g_attn_out": (g_attn_out, m_g_attn_out, v_g_attn_out),
        "g_conv_out": (g_conv_out, m_g_conv_out, v_g_conv_out), "g_norm2": (g_norm2, m_g_norm2, v_g_norm2),
        "g_final": (g_final, m_g_final, v_g_final),
    }
    state = [tuple(as_rows.get(name, lambda a: a)(a) for a in small_state[name]) for name, _ in SMALL_PARAMS]
    loss_row, small_out = _small_update(me_arr, packed_all, rel_all, state, [])
    loss = loss_row[0, 0]
    small_res = {name: tuple(a.reshape(small_state[name][0].shape) for a in res)
                 for (name, _), res in zip(SMALL_PARAMS, small_out)}

    big = {
        "w_ada": (g_ada[None], d_ada[None], nm_ada[None], nv_ada[None]),
        "w_in": (g_in.T[None], d_in.T[None], nm_in.T[None], nv_in.T[None]),
        "w_out": (g_out[None], d_out[None], nm_out[None], nv_out[None]),
        "w_gu": (g_gu.T[None], d_gu.T[None], nm_gu.T[None], nv_gu.T[None]),
        "w_down": (g_down[None], d_down[None], nm_down[None], nv_down[None]),
    }
    order = ["rel_bias", "w_ada", "b_ada", "g_norm1", "w_in", "sinks", "conv_w", "g_attn_out", "g_conv_out", "w_out",
             "g_norm2", "w_gu", "w_down", "g_final"]
    results = [big[k] if k in big else small_res[k] for k in order]
    return (loss, dx[None], *[r[0] for r in results], *[r[1] for r in results], *[r[2] for r in results],
            *[r[3] for r in results])
```

```python
import math

import jax
import jax.numpy as jnp
from jax import lax
from jax.experimental import pallas as pl
from jax.experimental.pallas import tpu as pltpu

F32 = jnp.float32
BF16 = jnp.bfloat16

D_MODEL = 1024
HEAD_DIM = 64
N_Q_HEADS = 8
ATTN_WIDTH = 512
KV_WIDTH = 128
CONV_WIDTH = 512
IN_PROJ_WIDTH = 2304
D_FF = 2816
N_MOD = 6
N_BUCKETS = 32
MAX_DISTANCE = 128
BLOCK = 128
REL_LANES = 128
EPS = 1e-6
NEG_INF = -1e30
SCALE = HEAD_DIM ** -0.5
N_DEV = 8

ADAM_LR = 0.001
ADAM_B1 = 0.9
ADAM_B2 = 0.999
ADAM_EPS = 1e-08
ADAM_WD = 0.01
ADAM_STEP = 10

SH1, SC1, G1, SH2, SC2, G2 = range(6)

VMEM_LIMIT_LARGE = 60 * 1024 * 1024
WEIGHT_GRAD_ROWS = 2048
FFN_CHUNKS = 1
PREV_ROWS = 16
MIXER_BLOCKS = 4
MESH_ID = pl.DeviceIdType.MESH

OFF_DMOD = 0
OFF_GN1 = OFF_DMOD + N_MOD * D_MODEL
OFF_SINK = OFF_GN1 + D_MODEL
OFF_GATT = OFF_SINK + 128
OFF_GCV = OFF_GATT + ATTN_WIDTH
OFF_GN2 = OFF_GCV + CONV_WIDTH
OFF_GFIN = OFF_GN2 + D_MODEL
OFF_CONVW = OFF_GFIN + D_MODEL
OFF_LOSS = OFF_CONVW + 3 * CONV_WIDTH
PACKED = OFF_LOSS + 128


def _params(sem=None, vmem=None):
    return pltpu.CompilerParams(dimension_semantics=sem, vmem_limit_bytes=vmem)


def _coming_behind(body):
    def skipping(after_ref, *refs):
        body(*refs)

    return skipping


ANY_SPEC = pl.BlockSpec(memory_space=pl.ANY)


def _full(shape):
    nd = len(shape)
    return pl.BlockSpec(shape, lambda *_: (0,) * nd)


def _rows(tm, width):
    return pl.BlockSpec((tm, width), lambda i, *_: (i, 0))


def _sigmoid(x):
    return 1.0 / (1.0 + jnp.exp(-x))


def _rsqrt_mean_sq(x):
    return lax.rsqrt(jnp.mean(x * x, axis=-1, keepdims=True) + EPS)


def _colsum(x):
    return jnp.sum(x, axis=0, keepdims=True)


def _dot(a, b):
    return jnp.dot(a, b, preferred_element_type=F32)


def _dot_nt(a, b):
    return lax.dot_general(a, b, (((1,), (1,)), ((), ())), preferred_element_type=F32)


def _dot_tn(a, b):
    return lax.dot_general(a, b, (((0,), (0,)), ((), ())), preferred_element_type=F32)


def _mesh_position():
    return lax.axis_index("x"), lax.axis_index("y"), lax.axis_index("c")


def _linear(p):
    return 4 * p[0] + 2 * p[1] + p[2]


def _peer(k):
    x, y, c = _mesh_position()
    return (1 - x if k & 4 else x, 1 - y if k & 2 else y, 1 - c if k & 1 else c)


HBM_SPEC = pl.BlockSpec(memory_space=pltpu.HBM)
SEM_SPEC = pl.BlockSpec(memory_space=pltpu.SEMAPHORE)
DATAFLOW = pltpu.SideEffectType.DATAFLOW_SIDE_EFFECTING


def _exchange_start(src, name):
    r, c = src.shape[1:]

    def body(src_ref, land_ref, send_sems, recv_sems, src_thru, land_thru, token):
        for k in range(1, N_DEV):
            peer = _peer(k)
            pltpu.make_async_remote_copy(
                src_ref=src_ref.at[_linear(peer)], dst_ref=land_ref.at[k - 1],
                send_sem=send_sems.at[k - 1], recv_sem=recv_sems.at[k - 1],
                device_id=peer, device_id_type=MESH_ID).start()
        token[...] = jnp.zeros_like(token)

    land = lax.empty((N_DEV - 1, r, c), src.dtype)
    return pl.pallas_call(
        body, name=name,
        out_shape=(pltpu.SemaphoreType.DMA((N_DEV - 1,)), pltpu.SemaphoreType.DMA((N_DEV - 1,)),
                   pltpu.HBM(src.shape, src.dtype), pltpu.HBM(land.shape, land.dtype),
                   jax.ShapeDtypeStruct((8, 128), F32)),
        in_specs=(HBM_SPEC, HBM_SPEC),
        out_specs=(SEM_SPEC, SEM_SPEC, HBM_SPEC, HBM_SPEC, pl.BlockSpec(memory_space=pltpu.VMEM)),
        input_output_aliases={0: 2, 1: 3},
        compiler_params=pltpu.CompilerParams(has_side_effects=DATAFLOW),
    )(pltpu.with_memory_space_constraint(src, pltpu.HBM), pltpu.with_memory_space_constraint(land, pltpu.HBM))


def _exchange_wait(started, after, name):
    send_sems, recv_sems, src_thru, land_thru, _ = started

    def body(src_ref, land_ref, send_sems, recv_sems, *rest):
        for k in range(1, N_DEV):
            cp = pltpu.make_async_remote_copy(
                src_ref=src_ref.at[0], dst_ref=land_ref.at[k - 1],
                send_sem=send_sems.at[k - 1], recv_sem=recv_sems.at[k - 1],
                device_id=_peer(k), device_id_type=MESH_ID)
            cp.wait_send()
            cp.wait_recv()

    return pl.pallas_call(
        body, name=name,
        out_shape=(pltpu.HBM(src_thru.shape, src_thru.dtype), pltpu.HBM(land_thru.shape, land_thru.dtype)),
        in_specs=(HBM_SPEC, HBM_SPEC, SEM_SPEC, SEM_SPEC) + (pl.BlockSpec(memory_space=pl.ANY),) * len(after),
        out_specs=(HBM_SPEC, HBM_SPEC), input_output_aliases={0: 0, 1: 1},
        compiler_params=pltpu.CompilerParams(has_side_effects=DATAFLOW),
    )(src_thru, land_thru, send_sems, recv_sems, *after)


def _share_start(arrs, zones, name):
    n = len(arrs)

    def body(*refs):
        src_refs, zone_refs, sems = refs[:n], refs[n:2 * n], refs[2 * n:4 * n]
        me = _linear(_mesh_position())
        for a in range(n):
            for k in range(1, N_DEV):
                pltpu.make_async_remote_copy(
                    src_ref=src_refs[a], dst_ref=zone_refs[a].at[me],
                    send_sem=sems[2 * a].at[k - 1], recv_sem=sems[2 * a + 1].at[k - 1],
                    device_id=_peer(k), device_id_type=MESH_ID).start()

    outs = pl.pallas_call(
        body, name=name,
        out_shape=tuple(pltpu.SemaphoreType.DMA((N_DEV - 1,)) for _ in range(2 * n))
        + tuple(pltpu.HBM(a.shape, a.dtype) for a in arrs) + tuple(pltpu.HBM(z.shape, z.dtype) for z in zones),
        in_specs=(HBM_SPEC,) * (2 * n),
        out_specs=(SEM_SPEC,) * (2 * n) + (HBM_SPEC,) * (2 * n),
        input_output_aliases={i: 2 * n + i for i in range(2 * n)},
        compiler_params=pltpu.CompilerParams(has_side_effects=DATAFLOW),
    )(*[pltpu.with_memory_space_constraint(a, pltpu.HBM) for a in list(arrs) + list(zones)])
    return outs[:2 * n], outs[2 * n:3 * n], outs[3 * n:]


def _share_wait(started, after, name):
    sems, arrs, zones = started
    n = len(arrs)

    def body(*refs):
        src_refs, zone_refs, sem_refs = refs[:n], refs[n:2 * n], refs[2 * n:4 * n]
        for a in range(n):
            for k in range(1, N_DEV):
                cp = pltpu.make_async_remote_copy(
                    src_ref=src_refs[a], dst_ref=zone_refs[a].at[_linear(_peer(k))],
                    send_sem=sem_refs[2 * a].at[k - 1], recv_sem=sem_refs[2 * a + 1].at[k - 1],
                    device_id=_peer(k), device_id_type=MESH_ID)
                cp.wait_send()
                cp.wait_recv()

    outs = pl.pallas_call(
        body, name=name,
        out_shape=tuple(pltpu.HBM(a.shape, a.dtype) for a in arrs) + tuple(pltpu.HBM(z.shape, z.dtype) for z in zones),
        in_specs=(HBM_SPEC,) * (2 * n) + (SEM_SPEC,) * (2 * n) + (pl.BlockSpec(memory_space=pl.ANY),) * len(after),
        out_specs=(HBM_SPEC,) * (2 * n), input_output_aliases={i: i for i in range(2 * n)},
        compiler_params=pltpu.CompilerParams(has_side_effects=DATAFLOW),
    )(*arrs, *zones, *sems, *after)
    return list(outs[n:])


def _same_core_peers():
    x, y, c = _mesh_position()
    return [(x, y, 1 - c), (1 - x, y, c), (x, 1 - y, c), (1 - x, 1 - y, c)]


def _gather_start(bufs, name):
    n = len(bufs)

    def body(*refs):
        buf_refs, rest = refs[:n], refs[n:]
        sems, token = rest[:2 * n], rest[-1]
        me = _linear(_mesh_position())
        for a in range(n):
            for k, peer in enumerate(_same_core_peers()):
                pltpu.make_async_remote_copy(
                    src_ref=buf_refs[a].at[me], dst_ref=buf_refs[a].at[me],
                    send_sem=sems[2 * a].at[k], recv_sem=sems[2 * a + 1].at[k],
                    device_id=peer, device_id_type=MESH_ID).start()
        token[...] = jnp.zeros_like(token)

    outs = pl.pallas_call(
        body, name=name,
        out_shape=tuple(pltpu.SemaphoreType.DMA((4,)) for _ in range(2 * n))
        + tuple(pltpu.HBM(b.shape, b.dtype) for b in bufs) + (jax.ShapeDtypeStruct((8, 128), F32),),
        in_specs=(HBM_SPEC,) * n,
        out_specs=(SEM_SPEC,) * (2 * n) + (HBM_SPEC,) * n + (pl.BlockSpec(memory_space=pltpu.VMEM),),
        input_output_aliases={a: 2 * n + a for a in range(n)},
        compiler_params=pltpu.CompilerParams(has_side_effects=DATAFLOW),
    )(*[pltpu.with_memory_space_constraint(b, pltpu.HBM) for b in bufs])
    return outs[:2 * n], outs[2 * n:3 * n], outs[3 * n]


def _gather_wait(sems, bufs, after, name):
    n = len(bufs)

    def body(*refs):
        buf_refs, sem_refs = refs[:n], refs[n:3 * n]
        x, y, c = _mesh_position()
        me = _linear((x, y, c))
        for a in range(n):
            for k, peer in enumerate(_same_core_peers()):
                cp = pltpu.make_async_remote_copy(
                    src_ref=buf_refs[a].at[me], dst_ref=buf_refs[a].at[_linear(peer)],
                    send_sem=sem_refs[2 * a].at[k], recv_sem=sem_refs[2 * a + 1].at[k],
                    device_id=peer, device_id_type=MESH_ID)
                cp.wait_send()
                cp.wait_recv()

    return list(pl.pallas_call(
        body, name=name,
        out_shape=tuple(pltpu.HBM(b.shape, b.dtype) for b in bufs),
        in_specs=(HBM_SPEC,) * n + (SEM_SPEC,) * (2 * n) + (pl.BlockSpec(memory_space=pl.ANY),) * len(after),
        out_specs=(HBM_SPEC,) * n, input_output_aliases={a: a for a in range(n)},
        compiler_params=pltpu.CompilerParams(has_side_effects=DATAFLOW),
    )(*bufs, *sems, *after))


def _gather_pass_on(bufs, name):
    n = len(bufs)

    def body(*refs):
        out_refs = refs[n:2 * n]
        send_sems, recv_sems = refs[2 * n:]
        x, y, c = _mesh_position()
        sibling = (x, y, 1 - c)
        chips = [(1 - x, y), (x, 1 - y), (1 - x, 1 - y)]
        copies = []
        for a in range(n):
            for j, chip in enumerate(chips):
                block = out_refs[a].at[_linear((*chip, c))]
                copies.append(pltpu.make_async_remote_copy(
                    src_ref=block, dst_ref=block, send_sem=send_sems.at[3 * a + j], recv_sem=recv_sems.at[3 * a + j],
                    device_id=sibling, device_id_type=MESH_ID))
                copies[-1].start()
        for a in range(n):
            for j, chip in enumerate(chips):
                copies[3 * a + j].wait_send()
                theirs = out_refs[a].at[_linear((*chip, 1 - c))]
                pltpu.make_async_remote_copy(
                    src_ref=theirs, dst_ref=theirs, send_sem=send_sems.at[3 * a + j], recv_sem=recv_sems.at[3 * a + j],
                    device_id=sibling, device_id_type=MESH_ID).wait_recv()

    hbm = pl.BlockSpec(memory_space=pl.ANY)
    return list(pl.pallas_call(
        body, name=name,
        out_shape=[jax.ShapeDtypeStruct(b.shape, b.dtype) for b in bufs],
        in_specs=[hbm] * n, out_specs=[hbm] * n, input_output_aliases={a: a for a in range(n)},
        scratch_shapes=[pltpu.SemaphoreType.DMA((3 * n,)), pltpu.SemaphoreType.DMA((3 * n,))],
    )(*bufs))


def _open_step(c, conv_w, w_ada, b_cols, w_in_t, later, rel_bias, bucket):
    cols = w_ada.shape[1]
    n_later = len(later)

    def body(c_ref, cw_ref, wa_ref, b_ref, w_ref, *rest):
        later_refs, rb_ref, bk_ref = rest[:n_later], rest[n_later], rest[n_later + 1]
        cond_ref, conv_ref, mod_ref, win_ref = rest[n_later + 2:n_later + 6]
        staged_refs, bias_ref = rest[n_later + 6:2 * n_later + 6], rest[2 * n_later + 6]
        cond_own, mod_own, stage = rest[2 * n_later + 7:2 * n_later + 10]
        later_stage = rest[2 * n_later + 10:3 * n_later + 10]
        s_send, s_recv, w_send, w_recv, local_sems = rest[3 * n_later + 10:]
        x, y, cc = _mesh_position()
        me = _linear((x, y, cc))
        sibling = (x, y, 1 - cc)
        chips = [(1 - x, y), (x, 1 - y), (1 - x, 1 - y)]
        v = c_ref[...]
        cond_own[...] = v * _sigmoid(v)
        stage[...] = w_ref[...].astype(BF16)

        def small(rnd, a, k, src, dst, slot):
            return pltpu.make_async_remote_copy(
                src_ref=src, dst_ref=dst.at[slot], send_sem=s_send.at[rnd, a, k - 1], recv_sem=s_recv.at[rnd, a, k - 1],
                device_id=_peer(k), device_id_type=MESH_ID)

        def block(p):
            return win_ref.at[_linear(p)]

        def big(k, blk, to, src=None):
            return pltpu.make_async_remote_copy(
                src_ref=block(blk) if src is None else src, dst_ref=block(blk),
                send_sem=w_send.at[k], recv_sem=w_recv.at[k], device_id=to, device_id_type=MESH_ID)

        mine = [pltpu.make_async_copy(cond_own, cond_ref.at[me], local_sems.at[0]),
                pltpu.make_async_copy(cw_ref, conv_ref.at[me], local_sems.at[1]),
                pltpu.make_async_copy(stage, block((x, y, cc)), local_sems.at[2])]
        for cp in mine:
            cp.start()
        sends = []
        for k in range(1, N_DEV):
            sends += [small(0, 0, k, cond_own, cond_ref, me), small(0, 1, k, cw_ref, conv_ref, me)]
        for cp in sends:
            cp.start()
        first = [big(0, (x, y, cc), sibling, src=stage)]
        first += [big(1 + j, (x, y, cc), (*chip, cc), src=stage) for j, chip in enumerate(chips)]
        for cp in first:
            cp.start()
        for a in range(n_later):
            later_stage[a][...] = later_refs[a][...].astype(BF16)
            mine.append(pltpu.make_async_copy(later_stage[a], staged_refs[a].at[me], local_sems.at[4 + a]))
            mine[-1].start()
        _fill_bias_table(rb_ref, bk_ref, bias_ref)
        for k in range(1, N_DEV):
            small(0, 0, k, cond_own, cond_ref, _linear(_peer(k))).wait_recv()
            small(0, 1, k, cw_ref, conv_ref, _linear(_peer(k))).wait_recv()
        mine[0].wait()
        cond_all = jnp.concatenate([cond_ref[k] for k in range(N_DEV)], axis=0)
        mod_own[...] = _dot(cond_all, wa_ref[...]) + b_ref[...]
        mine.append(pltpu.make_async_copy(mod_own, mod_ref.at[me], local_sems.at[3]))
        mine[-1].start()
        second = [small(1, 0, k, mod_own, mod_ref, me) for k in range(1, N_DEV)]
        for cp in second:
            cp.start()
        passed = []
        for j, chip in enumerate(chips):
            big(1 + j, (*chip, cc), (x, y, cc)).wait_recv()
            fwd = big(4 + j, (*chip, cc), sibling)
            fwd.start()
            passed.append(fwd)
        big(0, sibling, (x, y, cc)).wait_recv()
        for j, chip in enumerate(chips):
            big(4 + j, (*chip, 1 - cc), (x, y, cc)).wait_recv()
        for k in range(1, N_DEV):
            small(1, 0, k, mod_own, mod_ref, _linear(_peer(k))).wait_recv()
        for cp in sends + first + second + passed:
            cp.wait_send()
        for cp in mine[1:]:
            cp.wait()

    vmem = pl.BlockSpec(memory_space=pltpu.VMEM)
    outs = pl.pallas_call(
        body, name="open_step",
        out_shape=[jax.ShapeDtypeStruct((N_DEV,) + c.shape, F32), jax.ShapeDtypeStruct((N_DEV,) + conv_w.shape, F32),
                   jax.ShapeDtypeStruct((N_DEV, N_DEV, cols), F32),
                   jax.ShapeDtypeStruct((N_DEV,) + w_in_t.shape, BF16)]
        + [jax.ShapeDtypeStruct((N_DEV,) + a.shape, BF16) for a in later]
        + [jax.ShapeDtypeStruct((N_Q_HEADS, BLOCK, 2 * BLOCK), F32)],
        in_specs=[vmem] * (5 + n_later) + [pl.BlockSpec(memory_space=pltpu.SMEM), vmem],
        out_specs=[vmem, vmem, vmem, ANY_SPEC] + [ANY_SPEC] * n_later + [vmem],
        scratch_shapes=[pltpu.VMEM(c.shape, F32), pltpu.VMEM((N_DEV, cols), F32), pltpu.VMEM(w_in_t.shape, BF16)]
        + [pltpu.VMEM(a.shape, BF16) for a in later]
        + [pltpu.SemaphoreType.DMA((2, 2, N_DEV - 1)), pltpu.SemaphoreType.DMA((2, 2, N_DEV - 1)),
           pltpu.SemaphoreType.DMA((7,)), pltpu.SemaphoreType.DMA((7,)),
           pltpu.SemaphoreType.DMA((4 + n_later,))],
        compiler_params=_params(vmem=VMEM_LIMIT_LARGE),
    )(c, conv_w, w_ada, b_cols, w_in_t, *later, rel_bias, bucket)
    return outs[0], outs[1], outs[2], outs[3], list(outs[4:4 + n_later]), outs[4 + n_later]


def _in_proj(x, mod, g_norm1, w_in, tm):
    s = x.shape[0]

    def body(x_ref, mod_ref, g_ref, w_ref, h_ref, q_ref, kv_ref, gb_ref, gc_ref, xc_ref):
        xf = x_ref[...]
        n = xf * _rsqrt_mean_sq(xf) * g_ref[...]
        h = (n * (1.0 + mod_ref[SC1:SC1 + 1, :]) + mod_ref[SH1:SH1 + 1, :]).astype(BF16)
        h_ref[...] = h
        p = _dot_nt(h, w_ref[...])
        q_ref[...] = p[:, 0:512].astype(BF16)
        kv_ref[...] = p[:, 512:768].astype(BF16)
        gb_ref[...] = p[:, 768:1280].astype(BF16)
        gc_ref[...] = p[:, 1280:1792].astype(BF16)
        xc_ref[...] = p[:, 1792:2304].astype(BF16)

    return pl.pallas_call(
        body, name="in_proj", grid=(s // tm,),
        in_specs=[_rows(tm, D_MODEL), _full((8, D_MODEL)), _full((1, D_MODEL)), _full((IN_PROJ_WIDTH, D_MODEL))],
        out_specs=[_rows(tm, D_MODEL), _rows(tm, 512), _rows(tm, 256), _rows(tm, 512), _rows(tm, 512), _rows(tm, 512)],
        out_shape=[jax.ShapeDtypeStruct((s, D_MODEL), BF16), jax.ShapeDtypeStruct((s, 512), BF16),
                   jax.ShapeDtypeStruct((s, 256), BF16), jax.ShapeDtypeStruct((s, 512), BF16),
                   jax.ShapeDtypeStruct((s, 512), BF16), jax.ShapeDtypeStruct((s, 512), BF16)],
        compiler_params=_params(("arbitrary",), VMEM_LIMIT_LARGE),
    )(x, mod, g_norm1, w_in)


def _t5_bucket(dist):
    max_exact = N_BUCKETS // 2
    is_small = dist < max_exact
    d = jnp.maximum(dist, 1).astype(F32)
    large = max_exact + (jnp.log(d / max_exact) / math.log(MAX_DISTANCE / max_exact)
                         * (N_BUCKETS - max_exact)).astype(jnp.int32)
    large = jnp.minimum(large, N_BUCKETS - 1)
    return jnp.where(is_small, dist, large)


def _bucket_table():
    qi = jnp.arange(BLOCK, dtype=jnp.int32)[:, None]
    sj = jnp.arange(2 * BLOCK, dtype=jnp.int32)[None, :]
    return _t5_bucket(jnp.maximum(qi + BLOCK - sj, 0))


def _window_mask():
    qi = lax.broadcasted_iota(jnp.int32, (BLOCK, 2 * BLOCK), 0)
    sj = lax.broadcasted_iota(jnp.int32, (BLOCK, 2 * BLOCK), 1)
    dist = qi + BLOCK - sj
    return (dist >= 0) & (dist < BLOCK)


def _fill_bias_table(rb_ref, bk_ref, o_ref):
    bk = bk_ref[...]
    inside = _window_mask()
    for h in range(N_Q_HEADS):
        acc = jnp.zeros((BLOCK, 2 * BLOCK), F32)
        for b in range(N_BUCKETS):
            acc = jnp.where(bk == b, rb_ref[h, b], acc)
        o_ref[h] = jnp.where(inside, acc, NEG_INF)


def _load_kv_window(kv_ref, n):
    prev = jnp.maximum(n - 1, 0)
    kvw = jnp.concatenate([kv_ref[pl.ds(pl.multiple_of(prev * BLOCK, BLOCK), BLOCK), :],
                           kv_ref[pl.ds(pl.multiple_of(n * BLOCK, BLOCK), BLOCK), :]], axis=0)
    k, v = kvw[:, 0:128], kvw[:, 128:256]
    k_sw = pltpu.roll(k.astype(F32), 64, 1).astype(BF16)
    v_sw = pltpu.roll(v.astype(F32), 64, 1).astype(BF16)
    return (k, k_sw), (v, v_sw)


def _conv_taps(gc, xc, gc_prev, xc_prev, n):
    u = gc * xc
    before = jnp.where(n > 0, gc_prev.astype(F32) * xc_prev.astype(F32), 0.0)
    last = before.shape[0] - 1
    row = lax.broadcasted_iota(jnp.int32, u.shape, 0)
    u1 = jnp.where(row == 0, before[last:last + 1, :], pltpu.roll(u, 1, 0))
    u2 = jnp.where(row == 0, before[last - 1:last, :],
                   jnp.where(row == 1, before[last:last + 1, :], pltpu.roll(u, 2, 0)))
    return u, u1, u2


def _mixer_fwd(q, kv, gb, gc, xc, bias, sinks, conv_w, g_attn, g_conv):
    s = q.shape[0]
    nb = s // BLOCK

    per_step = min(MIXER_BLOCKS, nb)
    tile = per_step * BLOCK

    def one_block(n, slot, before, sink_ref, q_ref, kv_ref, gb_ref, gc_ref, xc_ref, bias_ref, cw_ref, ga_ref,
                  gcv_ref, attn_ref, merged_ref, lse_ref, p_ref):
        rows = slice(slot * BLOCK, (slot + 1) * BLOCK)
        ks, vs = _load_kv_window(kv_ref, n)
        lane = lax.broadcasted_iota(jnp.int32, (BLOCK, BLOCK), 1)
        low = lane < HEAD_DIM
        col = lax.broadcasted_iota(jnp.int32, (BLOCK, 2 * BLOCK), 1)
        no_prev = (col < BLOCK) & (n == 0)
        lse_all = jnp.zeros((BLOCK, BLOCK), F32)
        pairs = []
        for p in range(4):
            qp = q_ref[rows, 128 * p:128 * (p + 1)].astype(F32)
            kvh = p // 2
            res = []
            for e in range(2):
                h = 2 * p + e
                qm = jnp.where(low if e == 0 else ~low, qp, 0.0).astype(BF16)
                sw = 0 if kvh == e else 1
                sc = _dot_nt(qm, ks[sw]) * SCALE + bias_ref[h]
                sc = jnp.where(no_prev, NEG_INF, sc)
                sink = sink_ref[h]
                m = jnp.maximum(jnp.max(sc, axis=-1, keepdims=True), sink)
                pe = jnp.exp(sc - m)
                den = jnp.sum(pe, axis=-1, keepdims=True) + jnp.exp(sink - m)
                pb = (pe * (1.0 / den)).astype(BF16)
                p_ref[slot, h] = pb
                res.append(_dot(pb, vs[sw]))
                lse_all = lse_all + jnp.where(lane == h, m + jnp.log(den), 0.0)
            pairs.append(jnp.where(low, res[0], res[1]))
        attn = jnp.concatenate(pairs, axis=1)
        attn_ref[rows, :] = attn
        lse_ref[rows, :] = lse_all
        u, u1, u2 = _conv_taps(gc_ref[rows, :].astype(F32), xc_ref[rows, :].astype(F32), before[0], before[1], n)
        cw = cw_ref[...]
        cv = gb_ref[rows, :].astype(F32) * (cw[0:1, :] * u2 + cw[1:2, :] * u1 + cw[2:3, :] * u)
        an = attn * _rsqrt_mean_sq(attn) * ga_ref[...]
        cn = cv * _rsqrt_mean_sq(cv) * gcv_ref[...]
        merged_ref[rows, :] = jnp.concatenate([an, cn], axis=1).astype(BF16)

    def body(sink_ref, q_ref, kv_ref, gb_ref, gc_ref, xc_ref, gcp_ref, xcp_ref, *rest):
        step = pl.program_id(0)
        for sub in range(per_step):
            ahead = slice(sub * BLOCK - PREV_ROWS, sub * BLOCK)
            before = (gcp_ref[...], xcp_ref[...]) if sub == 0 else (gc_ref[ahead, :], xc_ref[ahead, :])
            one_block(step * per_step + sub, sub, before, sink_ref, q_ref, kv_ref, gb_ref, gc_ref, xc_ref, *rest)

    blk = lambda w: pl.BlockSpec((tile, w), lambda n: (n, 0))
    prev8 = pl.BlockSpec((PREV_ROWS, 512), lambda n: (jnp.maximum(n * (tile // PREV_ROWS) - 1, 0), 0))
    return pl.pallas_call(
        body, name="mixer_fwd", grid=(nb // per_step,),
        in_specs=[pl.BlockSpec(memory_space=pltpu.SMEM), blk(512), _full((s, 256)), blk(512), blk(512), blk(512),
                  prev8, prev8, _full((N_Q_HEADS, BLOCK, 2 * BLOCK)), _full((3, 512)), _full((1, 512)),
                  _full((1, 512))],
        out_specs=[blk(512), blk(1024), blk(128),
                   pl.BlockSpec((per_step, N_Q_HEADS, BLOCK, 2 * BLOCK), lambda n: (n, 0, 0, 0))],
        out_shape=[jax.ShapeDtypeStruct((s, 512), F32), jax.ShapeDtypeStruct((s, 1024), BF16),
                   jax.ShapeDtypeStruct((s, 128), F32),
                   jax.ShapeDtypeStruct((nb, N_Q_HEADS, BLOCK, 2 * BLOCK), BF16)],
        compiler_params=_params(("arbitrary",)),
    )(sinks, q, kv, gb, gc, xc, gc, xc, bias, conv_w, g_attn, g_conv)


def _out_proj(merged, x, mod, w_out, tm):
    s = x.shape[0]

    def body(m_ref, x_ref, mod_ref, w_ref, o_ref, x1_ref):
        o = _dot(m_ref[...], w_ref[...])
        o_ref[...] = o.astype(BF16)
        x1_ref[...] = x_ref[...] + mod_ref[G1:G1 + 1, :] * o

    return pl.pallas_call(
        body, name="out_proj", grid=(s // tm,),
        in_specs=[_rows(tm, D_MODEL), _rows(tm, D_MODEL), _full((8, D_MODEL)), _full((D_MODEL, D_MODEL))],
        out_specs=[_rows(tm, D_MODEL), _rows(tm, D_MODEL)],
        out_shape=[jax.ShapeDtypeStruct((s, D_MODEL), BF16), jax.ShapeDtypeStruct((s, D_MODEL), F32)],
        compiler_params=_params(("arbitrary",)),
    )(merged, x, mod, w_out)


def _resident(shape):
    nd = len(shape)
    return pl.BlockSpec(shape, lambda *_: (0,) * nd, pipeline_mode=pl.Buffered(1))


def _ffn(x1, o1, merged, mod, g_norm2, w_gu, w_down, w_out, g_final, target, tm):
    s = x1.shape[0]
    chunk = D_FF // FFN_CHUNKS

    def body(x_ref, o1_ref, mg_ref, mod_ref, g_ref, wgu_ref, wd_ref, wo_ref, gf_ref, t_ref,
             h_ref, act_ref, do_ref, dgu_ref, dx1_ref, dwo_ref, dm_ref, small_ref, dwo_acc):
        @pl.when(pl.program_id(0) == 0)
        def _():
            small_ref[...] = jnp.zeros_like(small_ref)
            dwo_acc[...] = jnp.zeros_like(dwo_acc)

        xf = x_ref[...]
        n = xf * _rsqrt_mean_sq(xf) * g_ref[...]
        h = (n * (1.0 + mod_ref[SC2:SC2 + 1, :]) + mod_ref[SH2:SH2 + 1, :]).astype(BF16)
        h_ref[...] = h
        gates, ups, o = [], [], None
        for j in range(FFN_CHUNKS):
            lo = j * chunk
            gate = _dot_nt(h, wgu_ref[lo:lo + chunk, :])
            up = _dot_nt(h, wgu_ref[D_FF + lo:D_FF + lo + chunk, :])
            sg = _sigmoid(gate)
            act = (gate * sg * up).astype(BF16)
            act_ref[:, lo:lo + chunk] = act
            gates.append((up * (sg * (1.0 + gate * (1.0 - sg)))).astype(BF16))
            ups.append((gate * sg).astype(BF16))
            part = _dot(act, wd_ref[lo:lo + chunk, :])
            o = part if o is None else o + part
        g2 = mod_ref[G2:G2 + 1, :]
        x2 = xf + g2 * o
        r = _rsqrt_mean_sq(x2)
        xn = x2 * r
        gf = gf_ref[...]
        err = xn * gf - t_ref[...]
        dy = err * (1.0 / D_MODEL)
        dxn = dy * gf
        dx2 = r * (dxn - xn * jnp.mean(dxn * xn, axis=-1, keepdims=True))
        small_ref[4:5, :] += _colsum(dy * xn)
        small_ref[5:6, :] += _colsum(err * err)
        small_ref[3:4, :] += _colsum(dx2 * o)
        do = (dx2 * g2).astype(BF16)
        do_ref[...] = do
        dh = None
        for j in range(FFN_CHUNKS):
            lo = j * chunk
            dact = _dot_nt(do, wd_ref[lo:lo + chunk, :])
            dgate = (dact * gates[j].astype(F32)).astype(BF16)
            dup = (dact * ups[j].astype(F32)).astype(BF16)
            dgu_ref[:, lo:lo + chunk] = dgate
            dgu_ref[:, D_FF + lo:D_FF + lo + chunk] = dup
            part = _dot(dgate, wgu_ref[lo:lo + chunk, :]) + _dot(dup, wgu_ref[D_FF + lo:D_FF + lo + chunk, :])
            dh = part if dh is None else dh + part
        dx1 = dx2 + _norm_mod_bwd(dh, xf, g_ref[...], mod_ref[SC2:SC2 + 1, :], small_ref)
        dx1_ref[...] = dx1.astype(BF16)
        small_ref[7:8, :] += _colsum(dx1 * o1_ref[...].astype(F32))
        do1 = (dx1 * mod_ref[G1:G1 + 1, :]).astype(BF16)
        dm_ref[...] = _dot_nt(do1, wo_ref[...]).astype(BF16)
        dwo = dwo_acc[...] + _dot_tn(mg_ref[...], do1)
        dwo_acc[...] = dwo
        dwo_ref[...] = dwo.astype(BF16)

        @pl.when(pl.program_id(0) == pl.num_programs(0) - 1)
        def _():
            total = jnp.sum(small_ref[5:6, :], axis=-1, keepdims=True) * (0.5 / D_MODEL)
            small_ref[6:7, :] = jnp.broadcast_to(total, (1, D_MODEL))

    narrow = jax.ShapeDtypeStruct((s, D_MODEL), BF16)
    return pl.pallas_call(
        body, name="ffn", grid=(s // tm,),
        in_specs=[_rows(tm, D_MODEL), _rows(tm, D_MODEL), _rows(tm, D_MODEL), _full((8, D_MODEL)), _full((1, D_MODEL)),
                  _resident((2 * D_FF, D_MODEL)), _resident((D_FF, D_MODEL)), _resident((D_MODEL, D_MODEL)),
                  _full((1, D_MODEL)), _rows(tm, D_MODEL)],
        out_specs=[_rows(tm, D_MODEL), _rows(tm, D_FF), _rows(tm, D_MODEL), _rows(tm, 2 * D_FF), _rows(tm, D_MODEL),
                   _full((D_MODEL, D_MODEL)), _rows(tm, D_MODEL), _full((8, D_MODEL))],
        out_shape=[narrow, jax.ShapeDtypeStruct((s, D_FF), BF16), narrow, jax.ShapeDtypeStruct((s, 2 * D_FF), BF16),
                   narrow, jax.ShapeDtypeStruct((D_MODEL, D_MODEL), BF16), narrow,
                   jax.ShapeDtypeStruct((8, D_MODEL), F32)],
        scratch_shapes=[pltpu.VMEM((D_MODEL, D_MODEL), F32)],
        compiler_params=_params(("arbitrary",), VMEM_LIMIT_LARGE),
    )(x1, o1, merged, mod, g_norm2, w_gu, w_down, w_out, g_final, target)


def _norm_mod_bwd(dh, xf, g, scale_row, small_ref):
    r = _rsqrt_mean_sq(xf)
    xn = xf * r
    small_ref[0:1, :] += _colsum(dh)
    small_ref[1:2, :] += _colsum(dh * (xn * g))
    dn = dh * (1.0 + scale_row)
    small_ref[2:3, :] += _colsum(dn * xn)
    dxn = dn * g
    return r * (dxn - xn * jnp.mean(dxn * xn, axis=-1, keepdims=True))


def _group_norm_bwd(dm, a, g):
    r = _rsqrt_mean_sq(a)
    an = a * r
    dan = dm * g
    return r * (dan - an * jnp.mean(dan * an, axis=-1, keepdims=True)), _colsum(dm * an)


def _sum_by_bucket(db_ref, bk_ref, o_ref, rows_ref):
    bk = bk_ref[...]
    for b in range(N_BUCKETS):
        sel = (bk == b).astype(F32)
        for h in range(N_Q_HEADS):
            rows_ref[N_BUCKETS * h + b:N_BUCKETS * h + b + 1, :] = _colsum(db_ref[h] * sel)
    head = lax.broadcasted_iota(jnp.int32, (N_BUCKETS, REL_LANES), 1)
    out = jnp.zeros((N_BUCKETS, REL_LANES), F32)
    for h in range(N_Q_HEADS):
        per_bucket = jnp.sum(rows_ref[N_BUCKETS * h:N_BUCKETS * (h + 1), :], axis=-1, keepdims=True)
        out = out + jnp.where(head == h, per_bucket, 0.0)
    o_ref[...] = out


def _mixer_bwd(after, q, kv, gb, gc, xc, probs, sinks, conv_w, g_attn, g_conv, attn, lse, dmerged, bucket):
    s = q.shape[0]
    nb = s // BLOCK

    per_step = min(MIXER_BLOCKS, nb)
    tile = per_step * BLOCK
    steps = nb // per_step

    def one_block(n, slot, before, nxt, sink_ref, q_ref, kv_ref, gb_ref, gc_ref, xc_ref, p_ref, cw_ref, ga_ref,
                  gcv_ref, attn_ref, lse_ref, dm_ref, dproj_ref, dbias_ref, dsink_ref, small_ref):
        rows = slice(slot * BLOCK, (slot + 1) * BLOCK)
        next_dy, next_dkv = nxt
        dm = dm_ref[rows, :].astype(F32)
        gbv, gcv_, xcv = gb_ref[rows, :].astype(F32), gc_ref[rows, :].astype(F32), xc_ref[rows, :].astype(F32)
        u, u1, u2 = _conv_taps(gcv_, xcv, before[0], before[1], n)
        cw = cw_ref[...]
        yv = cw[0:1, :] * u2 + cw[1:2, :] * u1 + cw[2:3, :] * u
        dcv, dg_conv = _group_norm_bwd(dm[:, 512:1024], gbv * yv, gcv_ref[...])
        small_ref[1:2, :] += dg_conv
        dproj_ref[rows, 768:1280] = (dcv * yv).astype(BF16)
        dy = dcv * gbv
        row = lax.broadcasted_iota(jnp.int32, dy.shape, 0)
        d1 = jnp.where(row == BLOCK - 1, next_dy[0:1, :], pltpu.roll(dy, BLOCK - 1, 0))
        d2 = jnp.where(row == BLOCK - 2, next_dy[0:1, :],
                       jnp.where(row == BLOCK - 1, next_dy[1:2, :], pltpu.roll(dy, BLOCK - 2, 0)))
        du = cw[2:3, :] * dy + cw[1:2, :] * d1 + cw[0:1, :] * d2
        dproj_ref[rows, 1280:1792] = (du * xcv).astype(BF16)
        dproj_ref[rows, 1792:2304] = (du * gcv_).astype(BF16)
        small_ref[2:3, :] += _colsum(dy * u2)
        small_ref[3:4, :] += _colsum(dy * u1)
        small_ref[4:5, :] += _colsum(dy * u)

        attn_v = attn_ref[rows, :]
        dout, dg_attn = _group_norm_bwd(dm[:, 0:512], attn_v, ga_ref[...])
        small_ref[0:1, :] += dg_attn
        ks, vs = _load_kv_window(kv_ref, n)
        lane = lax.broadcasted_iota(jnp.int32, (BLOCK, BLOCK), 1)
        low = lane < HEAD_DIM
        lse_all = lse_ref[rows, :]
        dsink = jnp.zeros((BLOCK, BLOCK), F32)
        dq_pairs = []
        dk_groups, dv_groups = [], []
        for kvh in range(2):
            ds_rows, pr_rows, q_rows, do_rows = [], [], [], []
            for p in (2 * kvh, 2 * kvh + 1):
                qp = q_ref[rows, 128 * p:128 * (p + 1)].astype(F32)
                do_p = dout[:, 128 * p:128 * (p + 1)]
                prod = do_p * attn_v[:, 128 * p:128 * (p + 1)]
                res = []
                for e in range(2):
                    h = 2 * p + e
                    half = low if e == 0 else ~low
                    qm = jnp.where(half, qp, 0.0).astype(BF16)
                    dom = jnp.where(half, do_p, 0.0).astype(BF16)
                    delta = jnp.sum(jnp.where(half, prod, 0.0), axis=-1, keepdims=True)
                    lse_h = jnp.sum(jnp.where(lane == h, lse_all, 0.0), axis=-1, keepdims=True)
                    sw = 0 if kvh == e else 1
                    pb = p_ref[slot, h]
                    dp = _dot_nt(dom, vs[sw])
                    ds = pb.astype(F32) * (dp - delta)
                    dbias_ref[h] += ds
                    dsink = dsink + jnp.where(lane == h, -jnp.exp(sink_ref[h] - lse_h) * delta, 0.0)
                    dsb = ds.astype(BF16)
                    res.append(_dot(dsb, ks[sw]) * SCALE)
                    ds_rows.append(dsb)
                    pr_rows.append(pb)
                    q_rows.append(qm)
                    do_rows.append(dom)
                dq_pairs.append(jnp.where(low, res[0], res[1]))
            dk_g = _dot_tn(jnp.concatenate(ds_rows, axis=0), jnp.concatenate(q_rows, axis=0)) * SCALE
            dv_g = _dot_tn(jnp.concatenate(pr_rows, axis=0), jnp.concatenate(do_rows, axis=0))
            dk_groups.append(dk_g + pltpu.roll(dk_g, 64, 1))
            dv_groups.append(dv_g + pltpu.roll(dv_g, 64, 1))
        dproj_ref[rows, 0:512] = jnp.concatenate(dq_pairs, axis=1).astype(BF16)
        dsink_ref[...] += dsink
        low_kv = lax.broadcasted_iota(jnp.int32, (2 * BLOCK, BLOCK), 1) < HEAD_DIM
        dkv_win = jnp.concatenate([jnp.where(low_kv, dk_groups[0], dk_groups[1]),
                                   jnp.where(low_kv, dv_groups[0], dv_groups[1])], axis=1)
        dproj_ref[rows, 512:768] = (dkv_win[BLOCK:2 * BLOCK, :] + next_dkv).astype(BF16)
        return dy[0:8, :], dkv_win[0:BLOCK, :]

    def body(sink_ref, q_ref, kv_ref, gb_ref, gc_ref, xc_ref, gcp_ref, xcp_ref, p_ref, cw_ref, ga_ref, gcv_ref,
             attn_ref, lse_ref, dm_ref, bk_ref, dproj_ref, drel_ref, dsink_ref, small_ref,
             dy_ref, dkv_ref, dbias_ref, rows_ref):
        refs = (p_ref, cw_ref, ga_ref, gcv_ref, attn_ref, lse_ref, dm_ref, dproj_ref, dbias_ref, dsink_ref, small_ref)
        step = pl.program_id(0)

        @pl.when(step == 0)
        def _():
            dbias_ref[...] = jnp.zeros_like(dbias_ref)
            dsink_ref[...] = jnp.zeros_like(dsink_ref)
            small_ref[...] = jnp.zeros_like(small_ref)
            dy_ref[...] = jnp.zeros_like(dy_ref)
            dkv_ref[...] = jnp.zeros_like(dkv_ref)

        nxt = (dy_ref[...], dkv_ref[...])
        for sub in reversed(range(per_step)):
            ahead = slice(sub * BLOCK - PREV_ROWS, sub * BLOCK)
            before = (gcp_ref[...], xcp_ref[...]) if sub == 0 else (gc_ref[ahead, :], xc_ref[ahead, :])
            nxt = one_block((steps - 1 - step) * per_step + sub, sub, before, nxt,
                            sink_ref, q_ref, kv_ref, gb_ref, gc_ref, xc_ref, *refs)
        dy_ref[...], dkv_ref[...] = nxt

        @pl.when(step == steps - 1)
        def _():
            small_ref[5:6, :] = jnp.concatenate([_colsum(dsink_ref[...]), jnp.zeros((1, 512 - BLOCK), F32)], axis=1)
            _sum_by_bucket(dbias_ref, bk_ref, drel_ref, rows_ref)

    blk = lambda w: pl.BlockSpec((tile, w), lambda t: (steps - 1 - t, 0))
    prev8 = pl.BlockSpec((PREV_ROWS, 512),
                         lambda t: (jnp.maximum((steps - 1 - t) * (tile // PREV_ROWS) - 1, 0), 0))
    bf = lambda w: jax.ShapeDtypeStruct((s, w), BF16)
    return pl.pallas_call(
        _coming_behind(body), name="mixer_bwd", grid=(steps,),
        in_specs=[ANY_SPEC, pl.BlockSpec(memory_space=pltpu.SMEM), blk(512), _full((s, 256)), blk(512), blk(512), blk(512),
                  prev8, prev8,
                  pl.BlockSpec((per_step, N_Q_HEADS, BLOCK, 2 * BLOCK), lambda t: (steps - 1 - t, 0, 0, 0)),
                  _full((3, 512)), _full((1, 512)), _full((1, 512)), blk(512), blk(128), blk(1024),
                  _full((BLOCK, 2 * BLOCK))],
        out_specs=[blk(IN_PROJ_WIDTH), _full((N_BUCKETS, REL_LANES)), _full((BLOCK, BLOCK)), _full((8, 512))],
        out_shape=[bf(IN_PROJ_WIDTH), jax.ShapeDtypeStruct((N_BUCKETS, REL_LANES), F32),
                   jax.ShapeDtypeStruct((BLOCK, BLOCK), F32), jax.ShapeDtypeStruct((8, 512), F32)],
        scratch_shapes=[pltpu.VMEM((8, 512), F32), pltpu.VMEM((BLOCK, 2 * KV_WIDTH), F32),
                        pltpu.VMEM((N_Q_HEADS, BLOCK, 2 * BLOCK), F32),
                        pltpu.VMEM((N_BUCKETS * N_Q_HEADS, 2 * BLOCK), F32)],
        compiler_params=_params(("arbitrary",), VMEM_LIMIT_LARGE),
    )(after, sinks, q, kv, gb, gc, xc, gc, xc, probs, conv_w, g_attn, g_conv, attn, lse, dmerged, bucket)


def _in_proj_bwd(after, dproj, x, dx1, mod, g_norm1, w_in, tm):
    s = x.shape[0]

    def body(dproj_ref, x_ref, dx1_ref, mod_ref, g_ref, w_ref, dx_ref, small_ref):
        @pl.when(pl.program_id(0) == 0)
        def _():
            small_ref[...] = jnp.zeros_like(small_ref)

        dh = _dot(dproj_ref[...], w_ref[...])
        dx_ref[...] = dx1_ref[...].astype(F32) + _norm_mod_bwd(dh, x_ref[...], g_ref[...], mod_ref[SC1:SC1 + 1, :],
                                                               small_ref)

    return pl.pallas_call(
        _coming_behind(body), name="in_proj_bwd", grid=(s // tm,),
        in_specs=[ANY_SPEC, _rows(tm, IN_PROJ_WIDTH), _rows(tm, D_MODEL), _rows(tm, D_MODEL), _full((8, D_MODEL)),
                  _full((1, D_MODEL)), _full((IN_PROJ_WIDTH, D_MODEL))],
        out_specs=[_rows(tm, D_MODEL), _full((8, D_MODEL))],
        out_shape=[jax.ShapeDtypeStruct((s, D_MODEL), F32), jax.ShapeDtypeStruct((8, D_MODEL), F32)],
        compiler_params=_params(("arbitrary",), VMEM_LIMIT_LARGE),
    )(after, dproj, x, dx1, mod, g_norm1, w_in)


def _weight_grad(a, b, tk, ts, name, after=None):
    s, k = a.shape
    n = b.shape[1]
    nt = s // ts
    extra = [] if after is None else [after]

    def body(a_ref, b_ref, *rest):
        o_ref, acc_ref = rest[-2:]
        t = pl.program_id(1)
        @pl.when(t == 0)
        def _():
            acc_ref[...] = jnp.zeros_like(acc_ref)

        acc = acc_ref[...] + _dot_tn(a_ref[...], b_ref[...])
        acc_ref[...] = acc
        o_ref[...] = acc.astype(BF16)

    return pl.pallas_call(
        body, name=name, grid=(k // tk, nt),
        in_specs=[pl.BlockSpec((ts, tk), lambda i, t: (t, i)), pl.BlockSpec((ts, n), lambda i, t: (t, 0))]
        + [ANY_SPEC] * len(extra),
        out_specs=pl.BlockSpec((tk, n), lambda i, t: (i, 0)),
        out_shape=jax.ShapeDtypeStruct((k, n), BF16),
        scratch_shapes=[pltpu.VMEM((tk, n), F32)],
        compiler_params=_params(("arbitrary", "arbitrary"), VMEM_LIMIT_LARGE),
    )(a, b, *extra)


def _lanes_from(x, start, width):
    n = x.shape[1]
    return pltpu.roll(x, (n - start) % n, 1)[:, 0:width]


def _adamw_w_ada(me, cond_all, packed_all, w, m, v, tr):
    r, cols = w.shape

    def body(me_ref, c_ref, p_ref, w_ref, m_ref, v_ref, g_ref, d_ref, mo_ref, vo_ref):
        dmod = jnp.concatenate([p_ref[k][:, OFF_DMOD:OFF_DMOD + N_MOD * D_MODEL] for k in range(N_DEV)], axis=0)
        mine = _lanes_from(dmod, me_ref[0] * cols, cols)
        pad = lambda a: jnp.concatenate([a, jnp.zeros((128 - N_DEV, a.shape[1]), F32)], axis=0)
        g = _dot_tn(pad(c_ref[...]), pad(mine))
        g_ref[...] = g
        d_ref[...], mo_ref[...], vo_ref[...] = _adam_math(w_ref[...], g, m_ref[...], v_ref[...])

    tile = pl.BlockSpec((tr, cols), lambda i, me_ref: (i, 0))
    return pl.pallas_call(
        body, name="adamw_w_ada",
        grid_spec=pltpu.PrefetchScalarGridSpec(
            num_scalar_prefetch=1, grid=(r // tr,),
            in_specs=[pl.BlockSpec((N_DEV, tr), lambda i, me_ref: (0, i)),
                      pl.BlockSpec(packed_all.shape, lambda i, me_ref: (0, 0, 0)), tile, tile, tile],
            out_specs=[tile] * 4),
        out_shape=[jax.ShapeDtypeStruct((r, cols), F32)] * 4,
        compiler_params=_params(("arbitrary",)),
    )(me, cond_all, packed_all, w, m, v)


SMALL_PARAMS = (("rel_bias", None), ("b_ada", (OFF_DMOD, N_MOD * D_MODEL)), ("g_norm1", (OFF_GN1, D_MODEL)),
                ("sinks", (OFF_SINK, N_Q_HEADS)), ("conv_w", None), ("g_attn_out", (OFF_GATT, ATTN_WIDTH)),
                ("g_conv_out", (OFF_GCV, CONV_WIDTH)), ("g_norm2", (OFF_GN2, D_MODEL)),
                ("g_final", (OFF_GFIN, D_MODEL)))


def _small_update(me, packed_all, rel_all, state, after):
    n_p = len(SMALL_PARAMS)
    flat = [a for triple in state for a in triple]
    conv_cols = state[4][0].shape[1]

    def body(me_ref, p_ref, r_ref, *refs):
        ins = refs[:3 * n_p]
        loss_ref, outs = refs[3 * n_p + len(after)], refs[3 * n_p + len(after) + 1:]
        small, rel = p_ref[0], r_ref[0]
        for k in range(1, N_DEV):
            small = small + p_ref[k]
            rel = rel + r_ref[k]
        rel = jnp.concatenate([rel, jnp.zeros((REL_LANES - N_BUCKETS, REL_LANES), F32)], axis=0).T
        rel = rel[0:N_Q_HEADS, 0:N_BUCKETS]
        loss_ref[...] = small[:, OFF_LOSS:OFF_LOSS + 128]
        taps = jnp.concatenate([small[:, OFF_CONVW + CONV_WIDTH * j:OFF_CONVW + CONV_WIDTH * (j + 1)]
                                for j in range(3)] + [jnp.zeros((5, CONV_WIDTH), F32)], axis=0)
        conv_g = _lanes_from(taps, me_ref[0] * conv_cols, conv_cols)[0:3, :]
        for i, (name, lanes) in enumerate(SMALL_PARAMS):
            g = rel if name == "rel_bias" else conv_g if name == "conv_w" else small[:, lanes[0]:lanes[0] + lanes[1]]
            w_ref, m_ref, v_ref = ins[3 * i:3 * i + 3]
            outs[4 * i][...] = g
            outs[4 * i + 1][...], outs[4 * i + 2][...], outs[4 * i + 3][...] = _adam_math(
                w_ref[...], g, m_ref[...], v_ref[...])

    vmem = pl.BlockSpec(memory_space=pltpu.VMEM)
    out_shape = [jax.ShapeDtypeStruct((1, 128), F32)]
    for w, _, _ in state:
        out_shape += [jax.ShapeDtypeStruct(w.shape, F32)] * 4
    outs = pl.pallas_call(
        body, name="small_update",
        in_specs=[pl.BlockSpec(memory_space=pltpu.SMEM), vmem, vmem] + [vmem] * len(flat)
        + [pl.BlockSpec(memory_space=pl.ANY)] * len(after),
        out_shape=out_shape,
    )(me, packed_all, rel_all, *flat, *after)
    return outs[0], [tuple(outs[1 + 4 * i:5 + 4 * i]) for i in range(n_p)]


def _adam_math(w, g, m, v):
    m = ADAM_B1 * m + (1.0 - ADAM_B1) * g
    v = ADAM_B2 * v + (1.0 - ADAM_B2) * (g * g)
    m_hat = m / (1.0 - ADAM_B1 ** ADAM_STEP)
    v_hat = v / (1.0 - ADAM_B2 ** ADAM_STEP)
    delta = -ADAM_LR * (m_hat / (jnp.sqrt(v_hat) + ADAM_EPS) + ADAM_WD * w)
    return delta, m, v


def _adamw_parts(w, m, v, local, land, me, tr, name):
    r, c = w.shape

    def body(me_ref, w_ref, m_ref, v_ref, own_ref, land_ref, g_ref, d_ref, mo_ref, vo_ref):
        g = own_ref[0].astype(F32)
        for k in range(N_DEV - 1):
            g = g + land_ref[k].astype(F32)
        g_ref[...] = g
        d_ref[...], mo_ref[...], vo_ref[...] = _adam_math(w_ref[...], g, m_ref[...], v_ref[...])

    tile = pl.BlockSpec((tr, c), lambda i, me_ref: (i, 0))
    return pl.pallas_call(
        body, name=name,
        grid_spec=pltpu.PrefetchScalarGridSpec(
            num_scalar_prefetch=1, grid=(r // tr,),
            in_specs=[tile, tile, tile, pl.BlockSpec((1, tr, c), lambda i, me_ref: (me_ref[0], i, 0)),
                      pl.BlockSpec((N_DEV - 1, tr, c), lambda i, me_ref: (0, i, 0))],
            out_specs=[tile] * 4),
        out_shape=[jax.ShapeDtypeStruct((r, c), F32)] * 4,
        compiler_params=_params(("arbitrary",)),
    )(me, w, m, v, local, land)


def _behind(a, token):
    return a + token[0:a.shape[0], 0:1]


def _local_step(x, target, mod, w_in_t, bias, weights_out_gu, weights_down, g_norm1, sinks, conv_w, g_attn,
                g_conv, g_norm2, g_final, exchange):
    s = x.shape[0]
    tm = min(512, s)
    tm_small = min(256, s)
    bucket = _bucket_table()

    h, q, kv, gb, gc, xc = _in_proj(x, mod, g_norm1, w_in_t, tm)
    attn, merged, lse, probs = _mixer_fwd(q, kv, gb, gc, xc, bias, sinks, conv_w, g_attn, g_conv)
    w_out, w_gu_t = weights_out_gu(merged)
    o1, x1 = _out_proj(merged, x, mod, w_out, tm)
    w_down = weights_down(x1)
    h2, act, do2, dgu, dx1, dw_out, dmerged, sm_2 = _ffn(x1, o1, merged, mod, g_norm2, w_gu_t, w_down, w_out, g_final,
                                                         target, tm_small)
    ts = min(WEIGHT_GRAD_ROWS, s)
    tok_out = exchange("w_out", dw_out)
    tok_down = exchange("w_down", _weight_grad(act, do2, D_FF // 2, ts, "w_down_grad", after=tok_out))
    tok_gu = exchange("w_gu", _weight_grad(dgu, h2, D_FF // 2, ts, "w_gu_grad", after=tok_down))
    dproj, d_rel, dsink, sm_mix = _mixer_bwd(
        tok_gu, q, kv, gb, gc, xc, probs, sinks, conv_w, g_attn, g_conv, attn, lse, dmerged, bucket)
    tok_in = exchange("w_in", _weight_grad(dproj, h, IN_PROJ_WIDTH // 2, ts, "w_in_grad"))
    dx, sm_1 = _in_proj_bwd(tok_in, dproj, x, dx1, mod, g_norm1, w_in_t, min(1024, s))

    packed = jnp.concatenate([
        sm_1[0], sm_1[1], sm_2[7], sm_2[0], sm_2[1], sm_2[3],
        sm_1[2],
        sm_mix[5, 0:128],
        sm_mix[0], sm_mix[1],
        sm_2[2],
        sm_2[4],
        sm_mix[2], sm_mix[3], sm_mix[4],
        sm_2[6, 0:128],
    ])[None, :]
    return dx, packed, d_rel


def kernel(x, c, rel_bias, w_ada, b_ada, g_norm1, w_in, sinks, conv_w, g_attn_out, g_conv_out, w_out, g_norm2, w_gu, w_down, g_final, loss_target, m_rel_bias, m_w_ada, m_b_ada, m_g_norm1, m_w_in, m_sinks, m_conv_w, m_g_attn_out, m_g_conv_out, m_w_out, m_g_norm2, m_w_gu, m_w_down, m_g_final, v_rel_bias, v_w_ada, v_b_ada, v_g_norm1, v_w_in, v_sinks, v_conv_w, v_g_attn_out, v_g_conv_out, v_w_out, v_g_norm2, v_w_gu, v_w_down, v_g_final):
    me = _linear(_mesh_position())
    me_arr = jnp.reshape(me, (1,)).astype(jnp.int32)
    ada_cols = w_ada.shape[2]
    tm = min(512, x.shape[1])

    b_cols = lax.dynamic_slice_in_dim(b_ada, me * ada_cols, ada_cols, axis=1)
    cond_all, conv_w_all, mod_all, w_in_blocks, staged, bias = _open_step(
        c, conv_w[0], w_ada[0], b_cols, w_in[0].T, [w_out[0], w_gu[0].T, w_down[0]], rel_bias.T, _bucket_table())
    cond_all = cond_all[:, 0, :]
    conv_w_full = conv_w_all.transpose(1, 0, 2).reshape(3, CONV_WIDTH)
    mod = lax.dynamic_index_in_dim(mod_all, me, axis=1, keepdims=False).reshape(N_MOD, D_MODEL)
    mod = jnp.concatenate([mod, jnp.zeros((2, D_MODEL), F32)], axis=0)
    w_in_t = w_in_blocks.reshape(IN_PROJ_WIDTH, D_MODEL)
    gather_sems, staged, gather_token = _gather_start(staged, "gather_start_weights")
    mod = _behind(mod, gather_token)

    def weights_out_gu(after):
        got = _gather_pass_on(_gather_wait(gather_sems[0:4], staged[0:2], [after], "gather_wait_out_gu"),
                              "gather_pass_on_out_gu")
        return got[0].reshape(D_MODEL, D_MODEL), got[1].reshape(2 * D_FF, D_MODEL)

    def weights_down(after):
        got = _gather_pass_on(_gather_wait(gather_sems[4:6], staged[2:3], [after], "gather_wait_down"),
                              "gather_pass_on_down")
        return got[0].reshape(D_FF, D_MODEL)

    started = {}

    def exchange(name, dw):
        st = _exchange_start(dw.reshape(N_DEV, dw.shape[0] // N_DEV, dw.shape[1]), "exchange_start_" + name)
        started[name] = st
        return st[4]

    dx, packed, d_rel = _local_step(
        x[0], loss_target[0], mod, w_in_t, bias, weights_out_gu, weights_down, g_norm1, sinks[0], conv_w_full,
        g_attn_out, g_conv_out, g_norm2, g_final[None, :], exchange)

    def zone(a):
        return lax.dynamic_update_slice(jnp.zeros((N_DEV,) + a.shape, F32), a[None], (me,) + (0,) * a.ndim)

    shared = _share_start([packed, d_rel], [zone(packed), zone(d_rel)], "share_small_start")

    def finish(name, after, w, m, v, tr):
        src, land = _exchange_wait(started[name], after, "exchange_wait_" + name)
        return _adamw_parts(w, m, v, src, land, me_arr, tr, "adamw_" + name)

    g_down, d_down, nm_down, nv_down = finish("w_down", [shared[2][0]], w_down[0], m_w_down[0], v_w_down[0], 176)
    g_gu, d_gu, nm_gu, nv_gu = finish("w_gu", [nv_down], w_gu[0].T, m_w_gu[0].T, v_w_gu[0].T, 352)
    g_out, d_out, nm_out, nv_out = finish("w_out", [nv_gu], w_out[0], m_w_out[0], v_w_out[0], 128)

    packed_all, rel_all = _share_wait(shared, [nv_out], "share_small_wait")
    g_ada, d_ada, nm_ada, nv_ada = _adamw_w_ada(me_arr, cond_all, packed_all, w_ada[0], m_w_ada[0], v_w_ada[0], 256)
    as_rows = {"conv_w": lambda a: a[0], "g_final": lambda a: a[None, :], "rel_bias": lambda a: a.T}
    small_state = {
        "rel_bias": (rel_bias, m_rel_bias, v_rel_bias), "b_ada": (b_ada, m_b_ada, v_b_ada),
        "g_norm1": (g_norm1, m_g_norm1, v_g_norm1), "sinks": (sinks, m_sinks, v_sinks),
        "conv_w": (conv_w, m_conv_w, v_conv_w), "g_attn_out": (g_attn_out, m_g_attn_out, v_g_attn_out),
        "g_conv_out": (g_conv_out, m_g_conv_out, v_g_conv_out), "g_norm2": (g_norm2, m_g_norm2, v_g_norm2),
        "g_final": (g_final, m_g_final, v_g_final),
    }
    state = [tuple(as_rows.get(name, lambda a: a)(a) for a in small_state[name]) for name, _ in SMALL_PARAMS]
    loss_row, small_out = _small_update(me_arr, packed_all, rel_all, state, [])
    loss = loss_row[0, 0]
    small_res = {name: tuple(a.T if name == "rel_bias" else a.reshape(small_state[name][0].shape) for a in res)
                 for (name, _), res in zip(SMALL_PARAMS, small_out)}

    g_in, d_in, nm_in, nv_in = finish("w_in", [loss_row, nv_ada], w_in[0].T, m_w_in[0].T, v_w_in[0].T, 144)

    big = {
        "w_ada": (g_ada[None], d_ada[None], nm_ada[None], nv_ada[None]),
        "w_in": (g_in.T[None], d_in.T[None], nm_in.T[None], nv_in.T[None]),
        "w_out": (g_out[None], d_out[None], nm_out[None], nv_out[None]),
        "w_gu": (g_gu.T[None], d_gu.T[None], nm_gu.T[None], nv_gu.T[None]),
        "w_down": (g_down[None], d_down[None], nm_down[None], nv_down[None]),
    }
    order = ["rel_bias", "w_ada", "b_ada", "g_norm1", "w_in", "sinks", "conv_w", "g_attn_out", "g_conv_out", "w_out",
             "g_norm2", "w_gu", "w_down", "g_final"]
    results = [big[k] if k in big else small_res[k] for k in order]
    return (loss, dx[None], *[r[0] for r in results], *[r[1] for r in results], *[r[2] for r in results],
            *[r[3] for r in results])
```

```python
import math

import jax
import jax.numpy as jnp
from jax import lax
from jax.experimental import pallas as pl
from jax.experimental.pallas import tpu as pltpu

F32 = jnp.float32
BF16 = jnp.bfloat16

D_MODEL = 1024
HEAD_DIM = 64
N_Q_HEADS = 8
ATTN_WIDTH = 512
KV_WIDTH = 128
CONV_WIDTH = 512
IN_PROJ_WIDTH = 2304
D_FF = 2816
N_MOD = 6
N_BUCKETS = 32
MAX_DISTANCE = 128
BLOCK = 128
REL_LANES = 128
EPS = 1e-6
NEG_INF = -1e30
SCALE = HEAD_DIM ** -0.5
N_DEV = 8

ADAM_LR = 0.001
ADAM_B1 = 0.9
ADAM_B2 = 0.999
ADAM_EPS = 1e-08
ADAM_WD = 0.01
ADAM_STEP = 10

SH1, SC1, G1, SH2, SC2, G2 = range(6)

VMEM_LIMIT_LARGE = 60 * 1024 * 1024
WEIGHT_GRAD_ROWS = 2048
FFN_CHUNKS = 1
PREV_ROWS = 16
MIXER_BLOCKS = 4
MESH_ID = pl.DeviceIdType.MESH

OFF_DMOD = 0
OFF_GN1 = OFF_DMOD + N_MOD * D_MODEL
OFF_SINK = OFF_GN1 + D_MODEL
OFF_GATT = OFF_SINK + 128
OFF_GCV = OFF_GATT + ATTN_WIDTH
OFF_GN2 = OFF_GCV + CONV_WIDTH
OFF_GFIN = OFF_GN2 + D_MODEL
OFF_CONVW = OFF_GFIN + D_MODEL
OFF_LOSS = OFF_CONVW + 3 * CONV_WIDTH
PACKED = OFF_LOSS + 128


def _params(sem=None, vmem=None):
    return pltpu.CompilerParams(dimension_semantics=sem, vmem_limit_bytes=vmem)


def _coming_behind(body):
    def skipping(after_ref, *refs):
        body(*refs)

    return skipping


ANY_SPEC = pl.BlockSpec(memory_space=pl.ANY)


def _full(shape):
    nd = len(shape)
    return pl.BlockSpec(shape, lambda *_: (0,) * nd)


def _rows(tm, width):
    return pl.BlockSpec((tm, width), lambda i, *_: (i, 0))


def _sigmoid(x):
    return 1.0 / (1.0 + jnp.exp(-x))


def _rsqrt_mean_sq(x):
    return lax.rsqrt(jnp.mean(x * x, axis=-1, keepdims=True) + EPS)


def _colsum(x):
    return jnp.sum(x, axis=0, keepdims=True)


def _dot(a, b):
    return jnp.dot(a, b, preferred_element_type=F32)


def _dot_nt(a, b):
    return lax.dot_general(a, b, (((1,), (1,)), ((), ())), preferred_element_type=F32)


def _dot_tn(a, b):
    return lax.dot_general(a, b, (((0,), (0,)), ((), ())), preferred_element_type=F32)


def _mesh_position():
    return lax.axis_index("x"), lax.axis_index("y"), lax.axis_index("c")


def _linear(p):
    return 4 * p[0] + 2 * p[1] + p[2]


def _peer(k):
    x, y, c = _mesh_position()
    return (1 - x if k & 4 else x, 1 - y if k & 2 else y, 1 - c if k & 1 else c)


HBM_SPEC = pl.BlockSpec(memory_space=pltpu.HBM)
SEM_SPEC = pl.BlockSpec(memory_space=pltpu.SEMAPHORE)
DATAFLOW = pltpu.SideEffectType.DATAFLOW_SIDE_EFFECTING


def _exchange_start(src, name):
    r, c = src.shape[1:]

    def body(src_ref, land_ref, send_sems, recv_sems, src_thru, land_thru, token):
        for k in range(1, N_DEV):
            peer = _peer(k)
            pltpu.make_async_remote_copy(
                src_ref=src_ref.at[_linear(peer)], dst_ref=land_ref.at[k - 1],
                send_sem=send_sems.at[k - 1], recv_sem=recv_sems.at[k - 1],
                device_id=peer, device_id_type=MESH_ID).start()
        token[...] = jnp.zeros_like(token)

    land = lax.empty((N_DEV - 1, r, c), src.dtype)
    return pl.pallas_call(
        body, name=name,
        out_shape=(pltpu.SemaphoreType.DMA((N_DEV - 1,)), pltpu.SemaphoreType.DMA((N_DEV - 1,)),
                   pltpu.HBM(src.shape, src.dtype), pltpu.HBM(land.shape, land.dtype),
                   jax.ShapeDtypeStruct((8, 128), F32)),
        in_specs=(HBM_SPEC, HBM_SPEC),
        out_specs=(SEM_SPEC, SEM_SPEC, HBM_SPEC, HBM_SPEC, pl.BlockSpec(memory_space=pltpu.VMEM)),
        input_output_aliases={0: 2, 1: 3},
        compiler_params=pltpu.CompilerParams(has_side_effects=DATAFLOW),
    )(pltpu.with_memory_space_constraint(src, pltpu.HBM), pltpu.with_memory_space_constraint(land, pltpu.HBM))


def _exchange_wait(started, after, name):
    send_sems, recv_sems, src_thru, land_thru, _ = started

    def body(src_ref, land_ref, send_sems, recv_sems, *rest):
        for k in range(1, N_DEV):
            cp = pltpu.make_async_remote_copy(
                src_ref=src_ref.at[0], dst_ref=land_ref.at[k - 1],
                send_sem=send_sems.at[k - 1], recv_sem=recv_sems.at[k - 1],
                device_id=_peer(k), device_id_type=MESH_ID)
            cp.wait_send()
            cp.wait_recv()

    return pl.pallas_call(
        body, name=name,
        out_shape=(pltpu.HBM(src_thru.shape, src_thru.dtype), pltpu.HBM(land_thru.shape, land_thru.dtype)),
        in_specs=(HBM_SPEC, HBM_SPEC, SEM_SPEC, SEM_SPEC) + (pl.BlockSpec(memory_space=pl.ANY),) * len(after),
        out_specs=(HBM_SPEC, HBM_SPEC), input_output_aliases={0: 0, 1: 1},
        compiler_params=pltpu.CompilerParams(has_side_effects=DATAFLOW),
    )(src_thru, land_thru, send_sems, recv_sems, *after)


def _share_start(arrs, zones, name):
    n = len(arrs)

    def body(*refs):
        src_refs, zone_refs, sems = refs[:n], refs[n:2 * n], refs[2 * n:4 * n]
        me = _linear(_mesh_position())
        for a in range(n):
            for k in range(1, N_DEV):
                pltpu.make_async_remote_copy(
                    src_ref=src_refs[a], dst_ref=zone_refs[a].at[me],
                    send_sem=sems[2 * a].at[k - 1], recv_sem=sems[2 * a + 1].at[k - 1],
                    device_id=_peer(k), device_id_type=MESH_ID).start()

    outs = pl.pallas_call(
        body, name=name,
        out_shape=tuple(pltpu.SemaphoreType.DMA((N_DEV - 1,)) for _ in range(2 * n))
        + tuple(pltpu.HBM(a.shape, a.dtype) for a in arrs) + tuple(pltpu.HBM(z.shape, z.dtype) for z in zones),
        in_specs=(HBM_SPEC,) * (2 * n),
        out_specs=(SEM_SPEC,) * (2 * n) + (HBM_SPEC,) * (2 * n),
        input_output_aliases={i: 2 * n + i for i in range(2 * n)},
        compiler_params=pltpu.CompilerParams(has_side_effects=DATAFLOW),
    )(*[pltpu.with_memory_space_constraint(a, pltpu.HBM) for a in list(arrs) + list(zones)])
    return outs[:2 * n], outs[2 * n:3 * n], outs[3 * n:]


def _share_wait(started, after, name):
    sems, arrs, zones = started
    n = len(arrs)

    def body(*refs):
        src_refs, zone_refs, sem_refs = refs[:n], refs[n:2 * n], refs[2 * n:4 * n]
        for a in range(n):
            for k in range(1, N_DEV):
                cp = pltpu.make_async_remote_copy(
                    src_ref=src_refs[a], dst_ref=zone_refs[a].at[_linear(_peer(k))],
                    send_sem=sem_refs[2 * a].at[k - 1], recv_sem=sem_refs[2 * a + 1].at[k - 1],
                    device_id=_peer(k), device_id_type=MESH_ID)
                cp.wait_send()
                cp.wait_recv()

    outs = pl.pallas_call(
        body, name=name,
        out_shape=tuple(pltpu.HBM(a.shape, a.dtype) for a in arrs) + tuple(pltpu.HBM(z.shape, z.dtype) for z in zones),
        in_specs=(HBM_SPEC,) * (2 * n) + (SEM_SPEC,) * (2 * n) + (pl.BlockSpec(memory_space=pl.ANY),) * len(after),
        out_specs=(HBM_SPEC,) * (2 * n), input_output_aliases={i: i for i in range(2 * n)},
        compiler_params=pltpu.CompilerParams(has_side_effects=DATAFLOW),
    )(*arrs, *zones, *sems, *after)
    return list(outs[n:])


def _same_core_peers():
    x, y, c = _mesh_position()
    return [(x, y, 1 - c), (1 - x, y, c), (x, 1 - y, c), (1 - x, 1 - y, c)]


def _gather_start(bufs, name):
    n = len(bufs)

    def body(*refs):
        buf_refs, rest = refs[:n], refs[n:]
        sems, token = rest[:2 * n], rest[-1]
        me = _linear(_mesh_position())
        for a in range(n):
            for k, peer in enumerate(_same_core_peers()):
                pltpu.make_async_remote_copy(
                    src_ref=buf_refs[a].at[me], dst_ref=buf_refs[a].at[me],
                    send_sem=sems[2 * a].at[k], recv_sem=sems[2 * a + 1].at[k],
                    device_id=peer, device_id_type=MESH_ID).start()
        token[...] = jnp.zeros_like(token)

    outs = pl.pallas_call(
        body, name=name,
        out_shape=tuple(pltpu.SemaphoreType.DMA((4,)) for _ in range(2 * n))
        + tuple(pltpu.HBM(b.shape, b.dtype) for b in bufs) + (jax.ShapeDtypeStruct((8, 128), F32),),
        in_specs=(HBM_SPEC,) * n,
        out_specs=(SEM_SPEC,) * (2 * n) + (HBM_SPEC,) * n + (pl.BlockSpec(memory_space=pltpu.VMEM),),
        input_output_aliases={a: 2 * n + a for a in range(n)},
        compiler_params=pltpu.CompilerParams(has_side_effects=DATAFLOW),
    )(*[pltpu.with_memory_space_constraint(b, pltpu.HBM) for b in bufs])
    return outs[:2 * n], outs[2 * n:3 * n], outs[3 * n]


def _gather_wait(sems, bufs, after, name):
    n = len(bufs)

    def body(*refs):
        buf_refs, sem_refs = refs[:n], refs[n:3 * n]
        x, y, c = _mesh_position()
        me = _linear((x, y, c))
        for a in range(n):
            for k, peer in enumerate(_same_core_peers()):
                cp = pltpu.make_async_remote_copy(
                    src_ref=buf_refs[a].at[me], dst_ref=buf_refs[a].at[_linear(peer)],
                    send_sem=sem_refs[2 * a].at[k], recv_sem=sem_refs[2 * a + 1].at[k],
                    device_id=peer, device_id_type=MESH_ID)
                cp.wait_send()
                cp.wait_recv()

    return list(pl.pallas_call(
        body, name=name,
        out_shape=tuple(pltpu.HBM(b.shape, b.dtype) for b in bufs),
        in_specs=(HBM_SPEC,) * n + (SEM_SPEC,) * (2 * n) + (pl.BlockSpec(memory_space=pl.ANY),) * len(after),
        out_specs=(HBM_SPEC,) * n, input_output_aliases={a: a for a in range(n)},
        compiler_params=pltpu.CompilerParams(has_side_effects=DATAFLOW),
    )(*bufs, *sems, *after))


def _gather_pass_on(bufs, name):
    n = len(bufs)

    def body(*refs):
        out_refs = refs[n:2 * n]
        send_sems, recv_sems = refs[2 * n:]
        x, y, c = _mesh_position()
        sibling = (x, y, 1 - c)
        chips = [(1 - x, y), (x, 1 - y), (1 - x, 1 - y)]
        copies = []
        for a in range(n):
            for j, chip in enumerate(chips):
                block = out_refs[a].at[_linear((*chip, c))]
                copies.append(pltpu.make_async_remote_copy(
                    src_ref=block, dst_ref=block, send_sem=send_sems.at[3 * a + j], recv_sem=recv_sems.at[3 * a + j],
                    device_id=sibling, device_id_type=MESH_ID))
                copies[-1].start()
        for a in range(n):
            for j, chip in enumerate(chips):
                copies[3 * a + j].wait_send()
                theirs = out_refs[a].at[_linear((*chip, 1 - c))]
                pltpu.make_async_remote_copy(
                    src_ref=theirs, dst_ref=theirs, send_sem=send_sems.at[3 * a + j], recv_sem=recv_sems.at[3 * a + j],
                    device_id=sibling, device_id_type=MESH_ID).wait_recv()

    hbm = pl.BlockSpec(memory_space=pl.ANY)
    return list(pl.pallas_call(
        body, name=name,
        out_shape=[jax.ShapeDtypeStruct(b.shape, b.dtype) for b in bufs],
        in_specs=[hbm] * n, out_specs=[hbm] * n, input_output_aliases={a: a for a in range(n)},
        scratch_shapes=[pltpu.SemaphoreType.DMA((3 * n,)), pltpu.SemaphoreType.DMA((3 * n,))],
    )(*bufs))


def _open_step(c, conv_w, w_ada, b_cols, w_in_t, later, rel_bias, bucket):
    cols = w_ada.shape[1]
    n_later = len(later)

    def body(c_ref, cw_ref, wa_ref, b_ref, w_ref, *rest):
        later_refs, rb_ref, bk_ref = rest[:n_later], rest[n_later], rest[n_later + 1]
        cond_ref, conv_ref, mod_ref, win_ref = rest[n_later + 2:n_later + 6]
        staged_refs, bias_ref = rest[n_later + 6:2 * n_later + 6], rest[2 * n_later + 6]
        cond_own, mod_own, stage = rest[2 * n_later + 7:2 * n_later + 10]
        later_stage = rest[2 * n_later + 10:3 * n_later + 10]
        s_send, s_recv, w_send, w_recv, local_sems = rest[3 * n_later + 10:]
        x, y, cc = _mesh_position()
        me = _linear((x, y, cc))
        sibling = (x, y, 1 - cc)
        chips = [(1 - x, y), (x, 1 - y), (1 - x, 1 - y)]
        v = c_ref[...]
        cond_own[...] = v * _sigmoid(v)
        stage[...] = w_ref[...].astype(BF16)

        def small(rnd, a, k, src, dst, slot):
            return pltpu.make_async_remote_copy(
                src_ref=src, dst_ref=dst.at[slot], send_sem=s_send.at[rnd, a, k - 1], recv_sem=s_recv.at[rnd, a, k - 1],
                device_id=_peer(k), device_id_type=MESH_ID)

        def block(p):
            return win_ref.at[_linear(p)]

        def big(k, blk, to, src=None):
            return pltpu.make_async_remote_copy(
                src_ref=block(blk) if src is None else src, dst_ref=block(blk),
                send_sem=w_send.at[k], recv_sem=w_recv.at[k], device_id=to, device_id_type=MESH_ID)

        mine = [pltpu.make_async_copy(cond_own, cond_ref.at[me], local_sems.at[0]),
                pltpu.make_async_copy(cw_ref, conv_ref.at[me], local_sems.at[1]),
                pltpu.make_async_copy(stage, block((x, y, cc)), local_sems.at[2])]
        for cp in mine:
            cp.start()
        sends = []
        for k in range(1, N_DEV):
            sends += [small(0, 0, k, cond_own, cond_ref, me), small(0, 1, k, cw_ref, conv_ref, me)]
        for cp in sends:
            cp.start()
        first = [big(0, (x, y, cc), sibling, src=stage)]
        first += [big(1 + j, (x, y, cc), (*chip, cc), src=stage) for j, chip in enumerate(chips)]
        for cp in first:
            cp.start()
        for a in range(n_later):
            later_stage[a][...] = later_refs[a][...].astype(BF16)
            mine.append(pltpu.make_async_copy(later_stage[a], staged_refs[a].at[me], local_sems.at[4 + a]))
            mine[-1].start()
        _fill_bias_table(rb_ref, bk_ref, bias_ref)
        for k in range(1, N_DEV):
            small(0, 0, k, cond_own, cond_ref, _linear(_peer(k))).wait_recv()
            small(0, 1, k, cw_ref, conv_ref, _linear(_peer(k))).wait_recv()
        mine[0].wait()
        cond_all = jnp.concatenate([cond_ref[k] for k in range(N_DEV)], axis=0)
        mod_own[...] = _dot(cond_all, wa_ref[...]) + b_ref[...]
        mine.append(pltpu.make_async_copy(mod_own, mod_ref.at[me], local_sems.at[3]))
        mine[-1].start()
        second = [small(1, 0, k, mod_own, mod_ref, me) for k in range(1, N_DEV)]
        for cp in second:
            cp.start()
        passed = []
        for j, chip in enumerate(chips):
            big(1 + j, (*chip, cc), (x, y, cc)).wait_recv()
            fwd = big(4 + j, (*chip, cc), sibling)
            fwd.start()
            passed.append(fwd)
        big(0, sibling, (x, y, cc)).wait_recv()
        for j, chip in enumerate(chips):
            big(4 + j, (*chip, 1 - cc), (x, y, cc)).wait_recv()
        for k in range(1, N_DEV):
            small(1, 0, k, mod_own, mod_ref, _linear(_peer(k))).wait_recv()
        for cp in sends + first + second + passed:
            cp.wait_send()
        for cp in mine[1:]:
            cp.wait()

    vmem = pl.BlockSpec(memory_space=pltpu.VMEM)
    outs = pl.pallas_call(
        body, name="open_step",
        out_shape=[jax.ShapeDtypeStruct((N_DEV,) + c.shape, F32), jax.ShapeDtypeStruct((N_DEV,) + conv_w.shape, F32),
                   jax.ShapeDtypeStruct((N_DEV, N_DEV, cols), F32),
                   jax.ShapeDtypeStruct((N_DEV,) + w_in_t.shape, BF16)]
        + [jax.ShapeDtypeStruct((N_DEV,) + a.shape, BF16) for a in later]
        + [jax.ShapeDtypeStruct((N_Q_HEADS, BLOCK, 2 * BLOCK), F32)],
        in_specs=[vmem] * (5 + n_later) + [pl.BlockSpec(memory_space=pltpu.SMEM), vmem],
        out_specs=[vmem, vmem, vmem, ANY_SPEC] + [ANY_SPEC] * n_later + [vmem],
        scratch_shapes=[pltpu.VMEM(c.shape, F32), pltpu.VMEM((N_DEV, cols), F32), pltpu.VMEM(w_in_t.shape, BF16)]
        + [pltpu.VMEM(a.shape, BF16) for a in later]
        + [pltpu.SemaphoreType.DMA((2, 2, N_DEV - 1)), pltpu.SemaphoreType.DMA((2, 2, N_DEV - 1)),
           pltpu.SemaphoreType.DMA((7,)), pltpu.SemaphoreType.DMA((7,)),
           pltpu.SemaphoreType.DMA((4 + n_later,))],
        compiler_params=_params(vmem=VMEM_LIMIT_LARGE),
    )(c, conv_w, w_ada, b_cols, w_in_t, *later, rel_bias, bucket)
    return outs[0], outs[1], outs[2], outs[3], list(outs[4:4 + n_later]), outs[4 + n_later]


def _in_proj(x, mod, g_norm1, w_in, tm):
    s = x.shape[0]

    def body(x_ref, mod_ref, g_ref, w_ref, h_ref, q_ref, kv_ref, gb_ref, gc_ref, xc_ref):
        xf = x_ref[...]
        n = xf * _rsqrt_mean_sq(xf) * g_ref[...]
        h = (n * (1.0 + mod_ref[SC1:SC1 + 1, :]) + mod_ref[SH1:SH1 + 1, :]).astype(BF16)
        h_ref[...] = h
        p = _dot_nt(h, w_ref[...])
        q_ref[...] = p[:, 0:512].astype(BF16)
        kv_ref[...] = p[:, 512:768].astype(BF16)
        gb_ref[...] = p[:, 768:1280].astype(BF16)
        gc_ref[...] = p[:, 1280:1792].astype(BF16)
        xc_ref[...] = p[:, 1792:2304].astype(BF16)

    return pl.pallas_call(
        body, name="in_proj", grid=(s // tm,),
        in_specs=[_rows(tm, D_MODEL), _full((8, D_MODEL)), _full((1, D_MODEL)), _full((IN_PROJ_WIDTH, D_MODEL))],
        out_specs=[_rows(tm, D_MODEL), _rows(tm, 512), _rows(tm, 256), _rows(tm, 512), _rows(tm, 512), _rows(tm, 512)],
        out_shape=[jax.ShapeDtypeStruct((s, D_MODEL), BF16), jax.ShapeDtypeStruct((s, 512), BF16),
                   jax.ShapeDtypeStruct((s, 256), BF16), jax.ShapeDtypeStruct((s, 512), BF16),
                   jax.ShapeDtypeStruct((s, 512), BF16), jax.ShapeDtypeStruct((s, 512), BF16)],
        compiler_params=_params(("arbitrary",), VMEM_LIMIT_LARGE),
    )(x, mod, g_norm1, w_in)


def _t5_bucket(dist):
    max_exact = N_BUCKETS // 2
    is_small = dist < max_exact
    d = jnp.maximum(dist, 1).astype(F32)
    large = max_exact + (jnp.log(d / max_exact) / math.log(MAX_DISTANCE / max_exact)
                         * (N_BUCKETS - max_exact)).astype(jnp.int32)
    large = jnp.minimum(large, N_BUCKETS - 1)
    return jnp.where(is_small, dist, large)


def _bucket_table():
    qi = jnp.arange(BLOCK, dtype=jnp.int32)[:, None]
    sj = jnp.arange(2 * BLOCK, dtype=jnp.int32)[None, :]
    return _t5_bucket(jnp.maximum(qi + BLOCK - sj, 0))


def _window_mask():
    qi = lax.broadcasted_iota(jnp.int32, (BLOCK, 2 * BLOCK), 0)
    sj = lax.broadcasted_iota(jnp.int32, (BLOCK, 2 * BLOCK), 1)
    dist = qi + BLOCK - sj
    return (dist >= 0) & (dist < BLOCK)


def _fill_bias_table(rb_ref, bk_ref, o_ref):
    bk = bk_ref[...]
    inside = _window_mask()
    for h in range(N_Q_HEADS):
        acc = jnp.zeros((BLOCK, 2 * BLOCK), F32)
        for b in range(N_BUCKETS):
            acc = jnp.where(bk == b, rb_ref[h, b], acc)
        o_ref[h] = jnp.where(inside, acc, NEG_INF)


def _load_kv_window(kv_ref, n):
    prev = jnp.maximum(n - 1, 0)
    kvw = jnp.concatenate([kv_ref[pl.ds(pl.multiple_of(prev * BLOCK, BLOCK), BLOCK), :],
                           kv_ref[pl.ds(pl.multiple_of(n * BLOCK, BLOCK), BLOCK), :]], axis=0)
    k, v = kvw[:, 0:128], kvw[:, 128:256]
    k_sw = pltpu.roll(k.astype(F32), 64, 1).astype(BF16)
    v_sw = pltpu.roll(v.astype(F32), 64, 1).astype(BF16)
    return (k, k_sw), (v, v_sw)


def _conv_taps(gc, xc, gc_prev, xc_prev, n):
    u = gc * xc
    before = jnp.where(n > 0, gc_prev.astype(F32) * xc_prev.astype(F32), 0.0)
    last = before.shape[0] - 1
    row = lax.broadcasted_iota(jnp.int32, u.shape, 0)
    u1 = jnp.where(row == 0, before[last:last + 1, :], pltpu.roll(u, 1, 0))
    u2 = jnp.where(row == 0, before[last - 1:last, :],
                   jnp.where(row == 1, before[last:last + 1, :], pltpu.roll(u, 2, 0)))
    return u, u1, u2


def _mixer_fwd(q, kv, gb, gc, xc, bias, sinks, conv_w, g_attn, g_conv):
    s = q.shape[0]
    nb = s // BLOCK

    per_step = min(MIXER_BLOCKS, nb)
    tile = per_step * BLOCK

    def one_block(n, slot, before, sink_ref, q_ref, kv_ref, gb_ref, gc_ref, xc_ref, bias_ref, cw_ref, ga_ref,
                  gcv_ref, attn_ref, merged_ref, lse_ref, p_ref):
        rows = slice(slot * BLOCK, (slot + 1) * BLOCK)
        ks, vs = _load_kv_window(kv_ref, n)
        lane = lax.broadcasted_iota(jnp.int32, (BLOCK, BLOCK), 1)
        low = lane < HEAD_DIM
        col = lax.broadcasted_iota(jnp.int32, (BLOCK, 2 * BLOCK), 1)
        no_prev = (col < BLOCK) & (n == 0)
        lse_all = jnp.zeros((BLOCK, BLOCK), F32)
        pairs = []
        for p in range(4):
            qp = q_ref[rows, 128 * p:128 * (p + 1)].astype(F32)
            kvh = p // 2
            res = []
            for e in range(2):
                h = 2 * p + e
                qm = jnp.where(low if e == 0 else ~low, qp, 0.0).astype(BF16)
                sw = 0 if kvh == e else 1
                sc = _dot_nt(qm, ks[sw]) * SCALE + bias_ref[h]
                sc = jnp.where(no_prev, NEG_INF, sc)
                sink = sink_ref[h]
                m = jnp.maximum(jnp.max(sc, axis=-1, keepdims=True), sink)
                pe = jnp.exp(sc - m)
                den = jnp.sum(pe, axis=-1, keepdims=True) + jnp.exp(sink - m)
                pb = (pe * (1.0 / den)).astype(BF16)
                p_ref[slot, h] = pb
                res.append(_dot(pb, vs[sw]))
                lse_all = lse_all + jnp.where(lane == h, m + jnp.log(den), 0.0)
            pairs.append(jnp.where(low, res[0], res[1]))
        attn = jnp.concatenate(pairs, axis=1)
        attn_ref[rows, :] = attn
        lse_ref[rows, :] = lse_all
        u, u1, u2 = _conv_taps(gc_ref[rows, :].astype(F32), xc_ref[rows, :].astype(F32), before[0], before[1], n)
        cw = cw_ref[...]
        cv = gb_ref[rows, :].astype(F32) * (cw[0:1, :] * u2 + cw[1:2, :] * u1 + cw[2:3, :] * u)
        an = attn * _rsqrt_mean_sq(attn) * ga_ref[...]
        cn = cv * _rsqrt_mean_sq(cv) * gcv_ref[...]
        merged_ref[rows, :] = jnp.concatenate([an, cn], axis=1).astype(BF16)

    def body(sink_ref, q_ref, kv_ref, gb_ref, gc_ref, xc_ref, gcp_ref, xcp_ref, *rest):
        step = pl.program_id(0)
        for sub in range(per_step):
            ahead = slice(sub * BLOCK - PREV_ROWS, sub * BLOCK)
            before = (gcp_ref[...], xcp_ref[...]) if sub == 0 else (gc_ref[ahead, :], xc_ref[ahead, :])
            one_block(step * per_step + sub, sub, before, sink_ref, q_ref, kv_ref, gb_ref, gc_ref, xc_ref, *rest)

    blk = lambda w: pl.BlockSpec((tile, w), lambda n: (n, 0))
    prev8 = pl.BlockSpec((PREV_ROWS, 512), lambda n: (jnp.maximum(n * (tile // PREV_ROWS) - 1, 0), 0))
    return pl.pallas_call(
        body, name="mixer_fwd", grid=(nb // per_step,),
        in_specs=[pl.BlockSpec(memory_space=pltpu.SMEM), blk(512), _full((s, 256)), blk(512), blk(512), blk(512),
                  prev8, prev8, _full((N_Q_HEADS, BLOCK, 2 * BLOCK)), _full((3, 512)), _full((1, 512)),
                  _full((1, 512))],
        out_specs=[blk(512), blk(1024), blk(128),
                   pl.BlockSpec((per_step, N_Q_HEADS, BLOCK, 2 * BLOCK), lambda n: (n, 0, 0, 0))],
        out_shape=[jax.ShapeDtypeStruct((s, 512), F32), jax.ShapeDtypeStruct((s, 1024), BF16),
                   jax.ShapeDtypeStruct((s, 128), F32),
                   jax.ShapeDtypeStruct((nb, N_Q_HEADS, BLOCK, 2 * BLOCK), BF16)],
        compiler_params=_params(("arbitrary",)),
    )(sinks, q, kv, gb, gc, xc, gc, xc, bias, conv_w, g_attn, g_conv)


def _out_proj(merged, x, mod, w_out, tm):
    s = x.shape[0]

    def body(m_ref, x_ref, mod_ref, w_ref, o_ref, x1_ref):
        o = _dot(m_ref[...], w_ref[...])
        o_ref[...] = o.astype(BF16)
        x1_ref[...] = x_ref[...] + mod_ref[G1:G1 + 1, :] * o

    return pl.pallas_call(
        body, name="out_proj", grid=(s // tm,),
        in_specs=[_rows(tm, D_MODEL), _rows(tm, D_MODEL), _full((8, D_MODEL)), _full((D_MODEL, D_MODEL))],
        out_specs=[_rows(tm, D_MODEL), _rows(tm, D_MODEL)],
        out_shape=[jax.ShapeDtypeStruct((s, D_MODEL), BF16), jax.ShapeDtypeStruct((s, D_MODEL), F32)],
        compiler_params=_params(("arbitrary",)),
    )(merged, x, mod, w_out)


def _resident(shape):
    nd = len(shape)
    return pl.BlockSpec(shape, lambda *_: (0,) * nd, pipeline_mode=pl.Buffered(1))


def _ffn(x1, o1, merged, mod, g_norm2, w_gu, w_down, w_out, g_final, target, tm):
    s = x1.shape[0]
    chunk = D_FF // FFN_CHUNKS

    def body(x_ref, o1_ref, mg_ref, mod_ref, g_ref, wgu_ref, wd_ref, wo_ref, gf_ref, t_ref,
             h_ref, act_ref, do_ref, dgu_ref, dx1_ref, dwo_ref, dm_ref, small_ref, dwo_acc):
        @pl.when(pl.program_id(0) == 0)
        def _():
            small_ref[...] = jnp.zeros_like(small_ref)
            dwo_acc[...] = jnp.zeros_like(dwo_acc)

        xf = x_ref[...]
        n = xf * _rsqrt_mean_sq(xf) * g_ref[...]
        h = (n * (1.0 + mod_ref[SC2:SC2 + 1, :]) + mod_ref[SH2:SH2 + 1, :]).astype(BF16)
        h_ref[...] = h
        gates, ups, o = [], [], None
        for j in range(FFN_CHUNKS):
            lo = j * chunk
            gate = _dot_nt(h, wgu_ref[lo:lo + chunk, :])
            up = _dot_nt(h, wgu_ref[D_FF + lo:D_FF + lo + chunk, :])
            sg = _sigmoid(gate)
            act = (gate * sg * up).astype(BF16)
            act_ref[:, lo:lo + chunk] = act
            gates.append((up * (sg * (1.0 + gate * (1.0 - sg)))).astype(BF16))
            ups.append((gate * sg).astype(BF16))
            part = _dot(act, wd_ref[lo:lo + chunk, :])
            o = part if o is None else o + part
        g2 = mod_ref[G2:G2 + 1, :]
        x2 = xf + g2 * o
        r = _rsqrt_mean_sq(x2)
        xn = x2 * r
        gf = gf_ref[...]
        err = xn * gf - t_ref[...]
        dy = err * (1.0 / D_MODEL)
        dxn = dy * gf
        dx2 = r * (dxn - xn * jnp.mean(dxn * xn, axis=-1, keepdims=True))
        small_ref[4:5, :] += _colsum(dy * xn)
        small_ref[5:6, :] += _colsum(err * err)
        small_ref[3:4, :] += _colsum(dx2 * o)
        do = (dx2 * g2).astype(BF16)
        do_ref[...] = do
        dh = None
        for j in range(FFN_CHUNKS):
            lo = j * chunk
            dact = _dot_nt(do, wd_ref[lo:lo + chunk, :])
            dgate = (dact * gates[j].astype(F32)).astype(BF16)
            dup = (dact * ups[j].astype(F32)).astype(BF16)
            dgu_ref[:, lo:lo + chunk] = dgate
            dgu_ref[:, D_FF + lo:D_FF + lo + chunk] = dup
            part = _dot(dgate, wgu_ref[lo:lo + chunk, :]) + _dot(dup, wgu_ref[D_FF + lo:D_FF + lo + chunk, :])
            dh = part if dh is None else dh + part
        dx1 = dx2 + _norm_mod_bwd(dh, xf, g_ref[...], mod_ref[SC2:SC2 + 1, :], small_ref)
        dx1_ref[...] = dx1.astype(BF16)
        small_ref[7:8, :] += _colsum(dx1 * o1_ref[...].astype(F32))
        do1 = (dx1 * mod_ref[G1:G1 + 1, :]).astype(BF16)
        dm_ref[...] = _dot_nt(do1, wo_ref[...]).astype(BF16)
        dwo = dwo_acc[...] + _dot_tn(mg_ref[...], do1)
        dwo_acc[...] = dwo
        dwo_ref[...] = dwo.astype(BF16)

        @pl.when(pl.program_id(0) == pl.num_programs(0) - 1)
        def _():
            total = jnp.sum(small_ref[5:6, :], axis=-1, keepdims=True) * (0.5 / D_MODEL)
            small_ref[6:7, :] = jnp.broadcast_to(total, (1, D_MODEL))

    narrow = jax.ShapeDtypeStruct((s, D_MODEL), BF16)
    return pl.pallas_call(
        body, name="ffn", grid=(s // tm,),
        in_specs=[_rows(tm, D_MODEL), _rows(tm, D_MODEL), _rows(tm, D_MODEL), _full((8, D_MODEL)), _full((1, D_MODEL)),
                  _resident((2 * D_FF, D_MODEL)), _resident((D_FF, D_MODEL)), _resident((D_MODEL, D_MODEL)),
                  _full((1, D_MODEL)), _rows(tm, D_MODEL)],
        out_specs=[_rows(tm, D_MODEL), _rows(tm, D_FF), _rows(tm, D_MODEL), _rows(tm, 2 * D_FF), _rows(tm, D_MODEL),
                   _full((D_MODEL, D_MODEL)), _rows(tm, D_MODEL), _full((8, D_MODEL))],
        out_shape=[narrow, jax.ShapeDtypeStruct((s, D_FF), BF16), narrow, jax.ShapeDtypeStruct((s, 2 * D_FF), BF16),
                   narrow, jax.ShapeDtypeStruct((D_MODEL, D_MODEL), BF16), narrow,
                   jax.ShapeDtypeStruct((8, D_MODEL), F32)],
        scratch_shapes=[pltpu.VMEM((D_MODEL, D_MODEL), F32)],
        compiler_params=_params(("arbitrary",), VMEM_LIMIT_LARGE),
    )(x1, o1, merged, mod, g_norm2, w_gu, w_down, w_out, g_final, target)


def _norm_mod_bwd(dh, xf, g, scale_row, small_ref):
    r = _rsqrt_mean_sq(xf)
    xn = xf * r
    small_ref[0:1, :] += _colsum(dh)
    small_ref[1:2, :] += _colsum(dh * (xn * g))
    dn = dh * (1.0 + scale_row)
    small_ref[2:3, :] += _colsum(dn * xn)
    dxn = dn * g
    return r * (dxn - xn * jnp.mean(dxn * xn, axis=-1, keepdims=True))


def _group_norm_bwd(dm, a, g):
    r = _rsqrt_mean_sq(a)
    an = a * r
    dan = dm * g
    return r * (dan - an * jnp.mean(dan * an, axis=-1, keepdims=True)), _colsum(dm * an)


def _sum_by_bucket(db_ref, bk_ref, o_ref, rows_ref):
    bk = bk_ref[...]
    for b in range(N_BUCKETS):
        sel = (bk == b).astype(F32)
        for h in range(N_Q_HEADS):
            rows_ref[N_BUCKETS * h + b:N_BUCKETS * h + b + 1, :] = _colsum(db_ref[h] * sel)
    head = lax.broadcasted_iota(jnp.int32, (N_BUCKETS, REL_LANES), 1)
    out = jnp.zeros((N_BUCKETS, REL_LANES), F32)
    for h in range(N_Q_HEADS):
        per_bucket = jnp.sum(rows_ref[N_BUCKETS * h:N_BUCKETS * (h + 1), :], axis=-1, keepdims=True)
        out = out + jnp.where(head == h, per_bucket, 0.0)
    o_ref[...] = out


def _mixer_bwd(after, q, kv, gb, gc, xc, probs, sinks, conv_w, g_attn, g_conv, attn, lse, dmerged, bucket):
    s = q.shape[0]
    nb = s // BLOCK

    per_step = min(MIXER_BLOCKS, nb)
    tile = per_step * BLOCK
    steps = nb // per_step

    def one_block(n, slot, before, nxt, sink_ref, q_ref, kv_ref, gb_ref, gc_ref, xc_ref, p_ref, cw_ref, ga_ref,
                  gcv_ref, attn_ref, lse_ref, dm_ref, dproj_ref, dbias_ref, dsink_ref, small_ref):
        rows = slice(slot * BLOCK, (slot + 1) * BLOCK)
        next_dy, next_dkv = nxt
        dm = dm_ref[rows, :].astype(F32)
        gbv, gcv_, xcv = gb_ref[rows, :].astype(F32), gc_ref[rows, :].astype(F32), xc_ref[rows, :].astype(F32)
        u, u1, u2 = _conv_taps(gcv_, xcv, before[0], before[1], n)
        cw = cw_ref[...]
        yv = cw[0:1, :] * u2 + cw[1:2, :] * u1 + cw[2:3, :] * u
        dcv, dg_conv = _group_norm_bwd(dm[:, 512:1024], gbv * yv, gcv_ref[...])
        small_ref[1:2, :] += dg_conv
        dproj_ref[rows, 768:1280] = (dcv * yv).astype(BF16)
        dy = dcv * gbv
        row = lax.broadcasted_iota(jnp.int32, dy.shape, 0)
        d1 = jnp.where(row == BLOCK - 1, next_dy[0:1, :], pltpu.roll(dy, BLOCK - 1, 0))
        d2 = jnp.where(row == BLOCK - 2, next_dy[0:1, :],
                       jnp.where(row == BLOCK - 1, next_dy[1:2, :], pltpu.roll(dy, BLOCK - 2, 0)))
        du = cw[2:3, :] * dy + cw[1:2, :] * d1 + cw[0:1, :] * d2
        dproj_ref[rows, 1280:1792] = (du * xcv).astype(BF16)
        dproj_ref[rows, 1792:2304] = (du * gcv_).astype(BF16)
        small_ref[2:3, :] += _colsum(dy * u2)
        small_ref[3:4, :] += _colsum(dy * u1)
        small_ref[4:5, :] += _colsum(dy * u)

        attn_v = attn_ref[rows, :]
        dout, dg_attn = _group_norm_bwd(dm[:, 0:512], attn_v, ga_ref[...])
        small_ref[0:1, :] += dg_attn
        ks, vs = _load_kv_window(kv_ref, n)
        lane = lax.broadcasted_iota(jnp.int32, (BLOCK, BLOCK), 1)
        low = lane < HEAD_DIM
        lse_all = lse_ref[rows, :]
        dsink = jnp.zeros((BLOCK, BLOCK), F32)
        dq_pairs = []
        dk_groups, dv_groups = [], []
        for kvh in range(2):
            ds_rows, pr_rows, q_rows, do_rows = [], [], [], []
            for p in (2 * kvh, 2 * kvh + 1):
                qp = q_ref[rows, 128 * p:128 * (p + 1)].astype(F32)
                do_p = dout[:, 128 * p:128 * (p + 1)]
                prod = do_p * attn_v[:, 128 * p:128 * (p + 1)]
                res = []
                for e in range(2):
                    h = 2 * p + e
                    half = low if e == 0 else ~low
                    qm = jnp.where(half, qp, 0.0).astype(BF16)
                    dom = jnp.where(half, do_p, 0.0).astype(BF16)
                    delta = jnp.sum(jnp.where(half, prod, 0.0), axis=-1, keepdims=True)
                    lse_h = jnp.sum(jnp.where(lane == h, lse_all, 0.0), axis=-1, keepdims=True)
                    sw = 0 if kvh == e else 1
                    pb = p_ref[slot, h]
                    dp = _dot_nt(dom, vs[sw])
                    ds = pb.astype(F32) * (dp - delta)
                    dbias_ref[h] += ds
                    dsink = dsink + jnp.where(lane == h, -jnp.exp(sink_ref[h] - lse_h) * delta, 0.0)
                    dsb = ds.astype(BF16)
                    res.append(_dot(dsb, ks[sw]) * SCALE)
                    ds_rows.append(dsb)
                    pr_rows.append(pb)
                    q_rows.append(qm)
                    do_rows.append(dom)
                dq_pairs.append(jnp.where(low, res[0], res[1]))
            dk_g = _dot_tn(jnp.concatenate(ds_rows, axis=0), jnp.concatenate(q_rows, axis=0)) * SCALE
            dv_g = _dot_tn(jnp.concatenate(pr_rows, axis=0), jnp.concatenate(do_rows, axis=0))
            dk_groups.append(dk_g + pltpu.roll(dk_g, 64, 1))
            dv_groups.append(dv_g + pltpu.roll(dv_g, 64, 1))
        dproj_ref[rows, 0:512] = jnp.concatenate(dq_pairs, axis=1).astype(BF16)
        dsink_ref[...] += dsink
        low_kv = lax.broadcasted_iota(jnp.int32, (2 * BLOCK, BLOCK), 1) < HEAD_DIM
        dkv_win = jnp.concatenate([jnp.where(low_kv, dk_groups[0], dk_groups[1]),
                                   jnp.where(low_kv, dv_groups[0], dv_groups[1])], axis=1)
        dproj_ref[rows, 512:768] = (dkv_win[BLOCK:2 * BLOCK, :] + next_dkv).astype(BF16)
        return dy[0:8, :], dkv_win[0:BLOCK, :]

    def body(sink_ref, q_ref, kv_ref, gb_ref, gc_ref, xc_ref, gcp_ref, xcp_ref, p_ref, cw_ref, ga_ref, gcv_ref,
             attn_ref, lse_ref, dm_ref, bk_ref, dproj_ref, drel_ref, dsink_ref, small_ref,
             dy_ref, dkv_ref, dbias_ref, rows_ref):
        refs = (p_ref, cw_ref, ga_ref, gcv_ref, attn_ref, lse_ref, dm_ref, dproj_ref, dbias_ref, dsink_ref, small_ref)
        step = pl.program_id(0)

        @pl.when(step == 0)
        def _():
            dbias_ref[...] = jnp.zeros_like(dbias_ref)
            dsink_ref[...] = jnp.zeros_like(dsink_ref)
            small_ref[...] = jnp.zeros_like(small_ref)
            dy_ref[...] = jnp.zeros_like(dy_ref)
            dkv_ref[...] = jnp.zeros_like(dkv_ref)

        nxt = (dy_ref[...], dkv_ref[...])
        for sub in reversed(range(per_step)):
            ahead = slice(sub * BLOCK - PREV_ROWS, sub * BLOCK)
            before = (gcp_ref[...], xcp_ref[...]) if sub == 0 else (gc_ref[ahead, :], xc_ref[ahead, :])
            nxt = one_block((steps - 1 - step) * per_step + sub, sub, before, nxt,
                            sink_ref, q_ref, kv_ref, gb_ref, gc_ref, xc_ref, *refs)
        dy_ref[...], dkv_ref[...] = nxt

        @pl.when(step == steps - 1)
        def _():
            small_ref[5:6, :] = jnp.concatenate([_colsum(dsink_ref[...]), jnp.zeros((1, 512 - BLOCK), F32)], axis=1)
            _sum_by_bucket(dbias_ref, bk_ref, drel_ref, rows_ref)

    blk = lambda w: pl.BlockSpec((tile, w), lambda t: (steps - 1 - t, 0))
    prev8 = pl.BlockSpec((PREV_ROWS, 512),
                         lambda t: (jnp.maximum((steps - 1 - t) * (tile // PREV_ROWS) - 1, 0), 0))
    bf = lambda w: jax.ShapeDtypeStruct((s, w), BF16)
    return pl.pallas_call(
        _coming_behind(body), name="mixer_bwd", grid=(steps,),
        in_specs=[ANY_SPEC, pl.BlockSpec(memory_space=pltpu.SMEM), blk(512), _full((s, 256)), blk(512), blk(512), blk(512),
                  prev8, prev8,
                  pl.BlockSpec((per_step, N_Q_HEADS, BLOCK, 2 * BLOCK), lambda t: (steps - 1 - t, 0, 0, 0)),
                  _full((3, 512)), _full((1, 512)), _full((1, 512)), blk(512), blk(128), blk(1024),
                  _full((BLOCK, 2 * BLOCK))],
        out_specs=[blk(IN_PROJ_WIDTH), _full((N_BUCKETS, REL_LANES)), _full((BLOCK, BLOCK)), _full((8, 512))],
        out_shape=[bf(IN_PROJ_WIDTH), jax.ShapeDtypeStruct((N_BUCKETS, REL_LANES), F32),
                   jax.ShapeDtypeStruct((BLOCK, BLOCK), F32), jax.ShapeDtypeStruct((8, 512), F32)],
        scratch_shapes=[pltpu.VMEM((8, 512), F32), pltpu.VMEM((BLOCK, 2 * KV_WIDTH), F32),
                        pltpu.VMEM((N_Q_HEADS, BLOCK, 2 * BLOCK), F32),
                        pltpu.VMEM((N_BUCKETS * N_Q_HEADS, 2 * BLOCK), F32)],
        compiler_params=_params(("arbitrary",), VMEM_LIMIT_LARGE),
    )(after, sinks, q, kv, gb, gc, xc, gc, xc, probs, conv_w, g_attn, g_conv, attn, lse, dmerged, bucket)


def _in_proj_bwd(after, dproj, x, dx1, mod, g_norm1, w_in, tm):
    s = x.shape[0]

    def body(dproj_ref, x_ref, dx1_ref, mod_ref, g_ref, w_ref, dx_ref, small_ref):
        @pl.when(pl.program_id(0) == 0)
        def _():
            small_ref[...] = jnp.zeros_like(small_ref)

        dh = _dot(dproj_ref[...], w_ref[...])
        dx_ref[...] = dx1_ref[...].astype(F32) + _norm_mod_bwd(dh, x_ref[...], g_ref[...], mod_ref[SC1:SC1 + 1, :],
                                                               small_ref)

    return pl.pallas_call(
        _coming_behind(body), name="in_proj_bwd", grid=(s // tm,),
        in_specs=[ANY_SPEC, _rows(tm, IN_PROJ_WIDTH), _rows(tm, D_MODEL), _rows(tm, D_MODEL), _full((8, D_MODEL)),
                  _full((1, D_MODEL)), _full((IN_PROJ_WIDTH, D_MODEL))],
        out_specs=[_rows(tm, D_MODEL), _full((8, D_MODEL))],
        out_shape=[jax.ShapeDtypeStruct((s, D_MODEL), F32), jax.ShapeDtypeStruct((8, D_MODEL), F32)],
        compiler_params=_params(("arbitrary",), VMEM_LIMIT_LARGE),
    )(after, dproj, x, dx1, mod, g_norm1, w_in)


def _weight_grad(a, b, tk, ts, name, after=None):
    s, k = a.shape
    n = b.shape[1]
    nt = s // ts
    extra = [] if after is None else [after]

    def body(a_ref, b_ref, *rest):
        o_ref, acc_ref = rest[-2:]
        t = pl.program_id(1)
        @pl.when(t == 0)
        def _():
            acc_ref[...] = jnp.zeros_like(acc_ref)

        acc = acc_ref[...] + _dot_tn(a_ref[...], b_ref[...])
        acc_ref[...] = acc
        o_ref[...] = acc.astype(BF16)

    return pl.pallas_call(
        body, name=name, grid=(k // tk, nt),
        in_specs=[pl.BlockSpec((ts, tk), lambda i, t: (t, i)), pl.BlockSpec((ts, n), lambda i, t: (t, 0))]
        + [ANY_SPEC] * len(extra),
        out_specs=pl.BlockSpec((tk, n), lambda i, t: (i, 0)),
        out_shape=jax.ShapeDtypeStruct((k, n), BF16),
        scratch_shapes=[pltpu.VMEM((tk, n), F32)],
        compiler_params=_params(("arbitrary", "arbitrary"), VMEM_LIMIT_LARGE),
    )(a, b, *extra)


def _lanes_from(x, start, width):
    n = x.shape[1]
    return pltpu.roll(x, (n - start) % n, 1)[:, 0:width]


def _adamw_w_ada(me, cond_all, packed_all, w, m, v, tr):
    r, cols = w.shape

    def body(me_ref, c_ref, p_ref, w_ref, m_ref, v_ref, g_ref, d_ref, mo_ref, vo_ref):
        dmod = jnp.concatenate([p_ref[k][:, OFF_DMOD:OFF_DMOD + N_MOD * D_MODEL] for k in range(N_DEV)], axis=0)
        mine = _lanes_from(dmod, me_ref[0] * cols, cols)
        pad = lambda a: jnp.concatenate([a, jnp.zeros((128 - N_DEV, a.shape[1]), F32)], axis=0)
        g = _dot_tn(pad(c_ref[...]), pad(mine))
        g_ref[...] = g
        d_ref[...], mo_ref[...], vo_ref[...] = _adam_math(w_ref[...], g, m_ref[...], v_ref[...])

    tile = pl.BlockSpec((tr, cols), lambda i, me_ref: (i, 0))
    return pl.pallas_call(
        body, name="adamw_w_ada",
        grid_spec=pltpu.PrefetchScalarGridSpec(
            num_scalar_prefetch=1, grid=(r // tr,),
            in_specs=[pl.BlockSpec((N_DEV, tr), lambda i, me_ref: (0, i)),
                      pl.BlockSpec(packed_all.shape, lambda i, me_ref: (0, 0, 0)), tile, tile, tile],
            out_specs=[tile] * 4),
        out_shape=[jax.ShapeDtypeStruct((r, cols), F32)] * 4,
        compiler_params=_params(("arbitrary",)),
    )(me, cond_all, packed_all, w, m, v)


SMALL_PARAMS = (("rel_bias", None), ("b_ada", (OFF_DMOD, N_MOD * D_MODEL)), ("g_norm1", (OFF_GN1, D_MODEL)),
                ("sinks", (OFF_SINK, N_Q_HEADS)), ("conv_w", None), ("g_attn_out", (OFF_GATT, ATTN_WIDTH)),
                ("g_conv_out", (OFF_GCV, CONV_WIDTH)), ("g_norm2", (OFF_GN2, D_MODEL)),
                ("g_final", (OFF_GFIN, D_MODEL)))


def _small_update(me, packed_all, rel_all, state, after):
    n_p = len(SMALL_PARAMS)
    flat = [a for triple in state for a in triple]
    conv_cols = state[4][0].shape[-1]

    def body(me_ref, p_ref, r_ref, *refs):
        ins = refs[:3 * n_p]
        loss_ref, outs = refs[3 * n_p + len(after)], refs[3 * n_p + len(after) + 1:]
        small, rel = p_ref[0], r_ref[0]
        for k in range(1, N_DEV):
            small = small + p_ref[k]
            rel = rel + r_ref[k]
        rel = jnp.concatenate([rel, jnp.zeros((REL_LANES - N_BUCKETS, REL_LANES), F32)], axis=0).T
        rel = rel[0:N_Q_HEADS, 0:N_BUCKETS]
        loss_ref[...] = small[:, OFF_LOSS:OFF_LOSS + 128]
        taps = jnp.concatenate([small[:, OFF_CONVW + CONV_WIDTH * j:OFF_CONVW + CONV_WIDTH * (j + 1)]
                                for j in range(3)] + [jnp.zeros((5, CONV_WIDTH), F32)], axis=0)
        conv_g = _lanes_from(taps, me_ref[0] * conv_cols, conv_cols)[0:3, :]
        for i, (name, lanes) in enumerate(SMALL_PARAMS):
            w_ref, m_ref, v_ref = ins[3 * i:3 * i + 3]
            if name == "conv_w":
                for j in range(3):
                    outs[4 * i][j] = conv_g[j:j + 1, :]
                    outs[4 * i + 1][j], outs[4 * i + 2][j], outs[4 * i + 3][j] = _adam_math(
                        w_ref[j], conv_g[j:j + 1, :], m_ref[j], v_ref[j])
                continue
            g = rel if name == "rel_bias" else small[:, lanes[0]:lanes[0] + lanes[1]]
            outs[4 * i][...] = g
            outs[4 * i + 1][...], outs[4 * i + 2][...], outs[4 * i + 3][...] = _adam_math(
                w_ref[...], g, m_ref[...], v_ref[...])

    vmem = pl.BlockSpec(memory_space=pltpu.VMEM)
    out_shape = [jax.ShapeDtypeStruct((1, 128), F32)]
    for w, _, _ in state:
        out_shape += [jax.ShapeDtypeStruct(w.shape, F32)] * 4
    outs = pl.pallas_call(
        body, name="small_update",
        in_specs=[pl.BlockSpec(memory_space=pltpu.SMEM), vmem, vmem] + [vmem] * len(flat)
        + [pl.BlockSpec(memory_space=pl.ANY)] * len(after),
        out_shape=out_shape,
    )(me, packed_all, rel_all, *flat, *after)
    return outs[0], [tuple(outs[1 + 4 * i:5 + 4 * i]) for i in range(n_p)]


def _adam_math(w, g, m, v):
    m = ADAM_B1 * m + (1.0 - ADAM_B1) * g
    v = ADAM_B2 * v + (1.0 - ADAM_B2) * (g * g)
    m_hat = m / (1.0 - ADAM_B1 ** ADAM_STEP)
    v_hat = v / (1.0 - ADAM_B2 ** ADAM_STEP)
    delta = -ADAM_LR * (m_hat / (jnp.sqrt(v_hat) + ADAM_EPS) + ADAM_WD * w)
    return delta, m, v


def _adamw_parts(w, m, v, local, land, me, tr, name):
    r, c = w.shape

    def body(me_ref, w_ref, m_ref, v_ref, own_ref, land_ref, g_ref, d_ref, mo_ref, vo_ref):
        g = own_ref[0].astype(F32)
        for k in range(N_DEV - 1):
            g = g + land_ref[k].astype(F32)
        g_ref[...] = g
        d_ref[...], mo_ref[...], vo_ref[...] = _adam_math(w_ref[...], g, m_ref[...], v_ref[...])

    tile = pl.BlockSpec((tr, c), lambda i, me_ref: (i, 0))
    return pl.pallas_call(
        body, name=name,
        grid_spec=pltpu.PrefetchScalarGridSpec(
            num_scalar_prefetch=1, grid=(r // tr,),
            in_specs=[tile, tile, tile, pl.BlockSpec((1, tr, c), lambda i, me_ref: (me_ref[0], i, 0)),
                      pl.BlockSpec((N_DEV - 1, tr, c), lambda i, me_ref: (0, i, 0))],
            out_specs=[tile] * 4),
        out_shape=[jax.ShapeDtypeStruct((r, c), F32)] * 4,
        compiler_params=_params(("arbitrary",)),
    )(me, w, m, v, local, land)


def _behind(a, token):
    return a + token[0:a.shape[0], 0:1]


def _local_step(x, target, mod, w_in_t, bias, weights_out_gu, weights_down, g_norm1, sinks, conv_w, g_attn,
                g_conv, g_norm2, g_final, exchange):
    s = x.shape[0]
    tm = min(512, s)
    tm_small = min(256, s)
    bucket = _bucket_table()

    h, q, kv, gb, gc, xc = _in_proj(x, mod, g_norm1, w_in_t, tm)
    attn, merged, lse, probs = _mixer_fwd(q, kv, gb, gc, xc, bias, sinks, conv_w, g_attn, g_conv)
    w_out, w_gu_t = weights_out_gu(merged)
    o1, x1 = _out_proj(merged, x, mod, w_out, tm)
    w_down = weights_down(x1)
    h2, act, do2, dgu, dx1, dw_out, dmerged, sm_2 = _ffn(x1, o1, merged, mod, g_norm2, w_gu_t, w_down, w_out, g_final,
                                                         target, tm_small)
    ts = min(WEIGHT_GRAD_ROWS, s)
    tok_out = exchange("w_out", dw_out)
    tok_down = exchange("w_down", _weight_grad(act, do2, D_FF // 2, ts, "w_down_grad", after=tok_out))
    tok_gu = exchange("w_gu", _weight_grad(dgu, h2, D_FF // 2, ts, "w_gu_grad", after=tok_down))
    dproj, d_rel, dsink, sm_mix = _mixer_bwd(
        tok_gu, q, kv, gb, gc, xc, probs, sinks, conv_w, g_attn, g_conv, attn, lse, dmerged, bucket)
    tok_in = exchange("w_in", _weight_grad(dproj, h, IN_PROJ_WIDTH // 2, ts, "w_in_grad"))
    dx, sm_1 = _in_proj_bwd(tok_in, dproj, x, dx1, mod, g_norm1, w_in_t, min(1024, s))

    packed = jnp.concatenate([
        sm_1[0], sm_1[1], sm_2[7], sm_2[0], sm_2[1], sm_2[3],
        sm_1[2],
        sm_mix[5, 0:128],
        sm_mix[0], sm_mix[1],
        sm_2[2],
        sm_2[4],
        sm_mix[2], sm_mix[3], sm_mix[4],
        sm_2[6, 0:128],
    ])[None, :]
    return dx, packed, d_rel


def kernel(x, c, rel_bias, w_ada, b_ada, g_norm1, w_in, sinks, conv_w, g_attn_out, g_conv_out, w_out, g_norm2, w_gu, w_down, g_final, loss_target, m_rel_bias, m_w_ada, m_b_ada, m_g_norm1, m_w_in, m_sinks, m_conv_w, m_g_attn_out, m_g_conv_out, m_w_out, m_g_norm2, m_w_gu, m_w_down, m_g_final, v_rel_bias, v_w_ada, v_b_ada, v_g_norm1, v_w_in, v_sinks, v_conv_w, v_g_attn_out, v_g_conv_out, v_w_out, v_g_norm2, v_w_gu, v_w_down, v_g_final):
    me = _linear(_mesh_position())
    me_arr = jnp.reshape(me, (1,)).astype(jnp.int32)
    ada_cols = w_ada.shape[2]
    tm = min(512, x.shape[1])

    b_cols = lax.dynamic_slice_in_dim(b_ada, me * ada_cols, ada_cols, axis=1)
    cond_all, conv_w_all, mod_all, w_in_blocks, staged, bias = _open_step(
        c, conv_w.transpose(1, 0, 2), w_ada[0], b_cols, w_in[0].T, [w_out[0], w_gu[0].T, w_down[0]], rel_bias.T, _bucket_table())
    cond_all = cond_all[:, 0, :]
    conv_w_full = conv_w_all.reshape(N_DEV, 3, -1).transpose(1, 0, 2).reshape(3, CONV_WIDTH)
    mod = lax.dynamic_index_in_dim(mod_all, me, axis=1, keepdims=False).reshape(N_MOD, D_MODEL)
    mod = jnp.concatenate([mod, jnp.zeros((2, D_MODEL), F32)], axis=0)
    w_in_t = w_in_blocks.reshape(IN_PROJ_WIDTH, D_MODEL)
    gather_sems, staged, gather_token = _gather_start(staged, "gather_start_weights")
    mod = _behind(mod, gather_token)

    def weights_out_gu(after):
        got = _gather_pass_on(_gather_wait(gather_sems[0:4], staged[0:2], [after], "gather_wait_out_gu"),
                              "gather_pass_on_out_gu")
        return got[0].reshape(D_MODEL, D_MODEL), got[1].reshape(2 * D_FF, D_MODEL)

    def weights_down(after):
        got = _gather_pass_on(_gather_wait(gather_sems[4:6], staged[2:3], [after], "gather_wait_down"),
                              "gather_pass_on_down")
        return got[0].reshape(D_FF, D_MODEL)

    started = {}

    def exchange(name, dw):
        st = _exchange_start(dw.reshape(N_DEV, dw.shape[0] // N_DEV, dw.shape[1]), "exchange_start_" + name)
        started[name] = st
        return st[4]

    dx, packed, d_rel = _local_step(
        x[0], loss_target[0], mod, w_in_t, bias, weights_out_gu, weights_down, g_norm1, sinks[0], conv_w_full,
        g_attn_out, g_conv_out, g_norm2, g_final[None, :], exchange)

    def zone(a):
        return lax.dynamic_update_slice(jnp.zeros((N_DEV,) + a.shape, F32), a[None], (me,) + (0,) * a.ndim)

    shared = _share_start([packed, d_rel], [zone(packed), zone(d_rel)], "share_small_start")

    def finish(name, after, w, m, v, tr):
        src, land = _exchange_wait(started[name], after, "exchange_wait_" + name)
        return _adamw_parts(w, m, v, src, land, me_arr, tr, "adamw_" + name)

    g_down, d_down, nm_down, nv_down = finish("w_down", [shared[2][0]], w_down[0], m_w_down[0], v_w_down[0], 176)
    g_gu, d_gu, nm_gu, nv_gu = finish("w_gu", [nv_down], w_gu[0].T, m_w_gu[0].T, v_w_gu[0].T, 352)
    g_out, d_out, nm_out, nv_out = finish("w_out", [nv_gu], w_out[0], m_w_out[0], v_w_out[0], 128)

    packed_all, rel_all = _share_wait(shared, [nv_out], "share_small_wait")
    g_ada, d_ada, nm_ada, nv_ada = _adamw_w_ada(me_arr, cond_all, packed_all, w_ada[0], m_w_ada[0], v_w_ada[0], 256)
    as_rows = {"conv_w": lambda a: a.transpose(1, 0, 2), "g_final": lambda a: a[None, :], "rel_bias": lambda a: a.T}
    small_state = {
        "rel_bias": (rel_bias, m_rel_bias, v_rel_bias), "b_ada": (b_ada, m_b_ada, v_b_ada),
        "g_norm1": (g_norm1, m_g_norm1, v_g_norm1), "sinks": (sinks, m_sinks, v_sinks),
        "conv_w": (conv_w, m_conv_w, v_conv_w), "g_attn_out": (g_attn_out, m_g_attn_out, v_g_attn_out),
        "g_conv_out": (g_conv_out, m_g_conv_out, v_g_conv_out), "g_norm2": (g_norm2, m_g_norm2, v_g_norm2),
        "g_final": (g_final, m_g_final, v_g_final),
    }
    state = [tuple(as_rows.get(name, lambda a: a)(a) for a in small_state[name]) for name, _ in SMALL_PARAMS]
    loss_row, small_out = _small_update(me_arr, packed_all, rel_all, state, [])
    loss = loss_row[0, 0]
    back = {"rel_bias": lambda a: a.T, "conv_w": lambda a: a.transpose(1, 0, 2)}
    small_res = {name: tuple(back[name](a) if name in back else a.reshape(small_state[name][0].shape) for a in res)
                 for (name, _), res in zip(SMALL_PARAMS, small_out)}

    g_in, d_in, nm_in, nv_in = finish("w_in", [loss_row, nv_ada], w_in[0].T, m_w_in[0].T, v_w_in[0].T, 144)

    big = {
        "w_ada": (g_ada[None], d_ada[None], nm_ada[None], nv_ada[None]),
        "w_in": (g_in.T[None], d_in.T[None], nm_in.T[None], nv_in.T[None]),
        "w_out": (g_out[None], d_out[None], nm_out[None], nv_out[None]),
        "w_gu": (g_gu.T[None], d_gu.T[None], nm_gu.T[None], nv_gu.T[None]),
        "w_down": (g_down[None], d_down[None], nm_down[None], nv_down[None]),
    }
    order = ["rel_bias", "w_ada", "b_ada", "g_norm1", "w_in", "sinks", "conv_w", "g_attn_out", "g_conv_out", "w_out",
             "g_norm2", "w_gu", "w_down", "g_final"]
    results = [big[k] if k in big else small_res[k] for k in order]
    return (loss, dx[None], *[r[0] for r in results], *[r[1] for r in results], *[r[2] for r in results],
            *[r[3] for r in results])
```

```python
import math

import jax
import jax.numpy as jnp
from jax import lax
from jax.experimental import pallas as pl
from jax.experimental.pallas import tpu as pltpu

F32 = jnp.float32
BF16 = jnp.bfloat16

D_MODEL = 1024
HEAD_DIM = 64
N_Q_HEADS = 8
ATTN_WIDTH = 512
KV_WIDTH = 128
CONV_WIDTH = 512
IN_PROJ_WIDTH = 2304
D_FF = 2816
N_MOD = 6
N_BUCKETS = 32
MAX_DISTANCE = 128
BLOCK = 128
REL_LANES = 128
EPS = 1e-6
NEG_INF = -1e30
SCALE = HEAD_DIM ** -0.5
N_DEV = 8

ADAM_LR = 0.001
ADAM_B1 = 0.9
ADAM_B2 = 0.999
ADAM_EPS = 1e-08
ADAM_WD = 0.01
ADAM_STEP = 10

SH1, SC1, G1, SH2, SC2, G2 = range(6)

VMEM_LIMIT_LARGE = 60 * 1024 * 1024
WEIGHT_GRAD_ROWS = 2048
FFN_CHUNKS = 1
PREV_ROWS = 16
MIXER_BLOCKS = 4
MESH_ID = pl.DeviceIdType.MESH

OFF_DMOD = 0
OFF_GN1 = OFF_DMOD + N_MOD * D_MODEL
OFF_SINK = OFF_GN1 + D_MODEL
OFF_GATT = OFF_SINK + 128
OFF_GCV = OFF_GATT + ATTN_WIDTH
OFF_GN2 = OFF_GCV + CONV_WIDTH
OFF_GFIN = OFF_GN2 + D_MODEL
OFF_CONVW = OFF_GFIN + D_MODEL
OFF_LOSS = OFF_CONVW + 3 * CONV_WIDTH
PACKED = OFF_LOSS + 128


def _params(sem=None, vmem=None):
    return pltpu.CompilerParams(dimension_semantics=sem, vmem_limit_bytes=vmem)


def _coming_behind(body):
    def skipping(after_ref, *refs):
        body(*refs)

    return skipping


ANY_SPEC = pl.BlockSpec(memory_space=pl.ANY)


def _full(shape):
    nd = len(shape)
    return pl.BlockSpec(shape, lambda *_: (0,) * nd)


def _rows(tm, width):
    return pl.BlockSpec((tm, width), lambda i, *_: (i, 0))


def _sigmoid(x):
    return 1.0 / (1.0 + jnp.exp(-x))


def _rsqrt_mean_sq(x):
    return lax.rsqrt(jnp.mean(x * x, axis=-1, keepdims=True) + EPS)


def _colsum(x):
    return jnp.sum(x, axis=0, keepdims=True)


def _dot(a, b):
    return jnp.dot(a, b, preferred_element_type=F32)


def _dot_nt(a, b):
    return lax.dot_general(a, b, (((1,), (1,)), ((), ())), preferred_element_type=F32)


def _dot_tn(a, b):
    return lax.dot_general(a, b, (((0,), (0,)), ((), ())), preferred_element_type=F32)


def _mesh_position():
    return lax.axis_index("x"), lax.axis_index("y"), lax.axis_index("c")


def _linear(p):
    return 4 * p[0] + 2 * p[1] + p[2]


def _peer(k):
    x, y, c = _mesh_position()
    return (1 - x if k & 4 else x, 1 - y if k & 2 else y, 1 - c if k & 1 else c)


HBM_SPEC = pl.BlockSpec(memory_space=pltpu.HBM)
SEM_SPEC = pl.BlockSpec(memory_space=pltpu.SEMAPHORE)
DATAFLOW = pltpu.SideEffectType.DATAFLOW_SIDE_EFFECTING


def _exchange_start(src, name):
    r, c = src.shape[1:]

    def body(src_ref, land_ref, send_sems, recv_sems, src_thru, land_thru, token):
        for k in range(1, N_DEV):
            peer = _peer(k)
            pltpu.make_async_remote_copy(
                src_ref=src_ref.at[_linear(peer)], dst_ref=land_ref.at[k - 1],
                send_sem=send_sems.at[k - 1], recv_sem=recv_sems.at[k - 1],
                device_id=peer, device_id_type=MESH_ID).start()
        token[...] = jnp.zeros_like(token)

    land = lax.empty((N_DEV - 1, r, c), src.dtype)
    return pl.pallas_call(
        body, name=name,
        out_shape=(pltpu.SemaphoreType.DMA((N_DEV - 1,)), pltpu.SemaphoreType.DMA((N_DEV - 1,)),
                   pltpu.HBM(src.shape, src.dtype), pltpu.HBM(land.shape, land.dtype),
                   jax.ShapeDtypeStruct((8, 128), F32)),
        in_specs=(HBM_SPEC, HBM_SPEC),
        out_specs=(SEM_SPEC, SEM_SPEC, HBM_SPEC, HBM_SPEC, pl.BlockSpec(memory_space=pltpu.VMEM)),
        input_output_aliases={0: 2, 1: 3},
        compiler_params=pltpu.CompilerParams(has_side_effects=DATAFLOW),
    )(pltpu.with_memory_space_constraint(src, pltpu.HBM), pltpu.with_memory_space_constraint(land, pltpu.HBM))


def _exchange_wait(started, after, name):
    send_sems, recv_sems, src_thru, land_thru, _ = started

    def body(src_ref, land_ref, send_sems, recv_sems, *rest):
        for k in range(1, N_DEV):
            cp = pltpu.make_async_remote_copy(
                src_ref=src_ref.at[0], dst_ref=land_ref.at[k - 1],
                send_sem=send_sems.at[k - 1], recv_sem=recv_sems.at[k - 1],
                device_id=_peer(k), device_id_type=MESH_ID)
            cp.wait_send()
            cp.wait_recv()

    return pl.pallas_call(
        body, name=name,
        out_shape=(pltpu.HBM(src_thru.shape, src_thru.dtype), pltpu.HBM(land_thru.shape, land_thru.dtype)),
        in_specs=(HBM_SPEC, HBM_SPEC, SEM_SPEC, SEM_SPEC) + (pl.BlockSpec(memory_space=pl.ANY),) * len(after),
        out_specs=(HBM_SPEC, HBM_SPEC), input_output_aliases={0: 0, 1: 1},
        compiler_params=pltpu.CompilerParams(has_side_effects=DATAFLOW),
    )(src_thru, land_thru, send_sems, recv_sems, *after)


def _share_start(arrs, zones, name):
    n = len(arrs)

    def body(*refs):
        src_refs, zone_refs, sems = refs[:n], refs[n:2 * n], refs[2 * n:4 * n]
        me = _linear(_mesh_position())
        for a in range(n):
            for k in range(1, N_DEV):
                pltpu.make_async_remote_copy(
                    src_ref=src_refs[a], dst_ref=zone_refs[a].at[me],
                    send_sem=sems[2 * a].at[k - 1], recv_sem=sems[2 * a + 1].at[k - 1],
                    device_id=_peer(k), device_id_type=MESH_ID).start()

    outs = pl.pallas_call(
        body, name=name,
        out_shape=tuple(pltpu.SemaphoreType.DMA((N_DEV - 1,)) for _ in range(2 * n))
        + tuple(pltpu.HBM(a.shape, a.dtype) for a in arrs) + tuple(pltpu.HBM(z.shape, z.dtype) for z in zones),
        in_specs=(HBM_SPEC,) * (2 * n),
        out_specs=(SEM_SPEC,) * (2 * n) + (HBM_SPEC,) * (2 * n),
        input_output_aliases={i: 2 * n + i for i in range(2 * n)},
        compiler_params=pltpu.CompilerParams(has_side_effects=DATAFLOW),
    )(*[pltpu.with_memory_space_constraint(a, pltpu.HBM) for a in list(arrs) + list(zones)])
    return outs[:2 * n], outs[2 * n:3 * n], outs[3 * n:]


def _share_wait(started, after, name):
    sems, arrs, zones = started
    n = len(arrs)

    def body(*refs):
        src_refs, zone_refs, sem_refs = refs[:n], refs[n:2 * n], refs[2 * n:4 * n]
        for a in range(n):
            for k in range(1, N_DEV):
                cp = pltpu.make_async_remote_copy(
                    src_ref=src_refs[a], dst_ref=zone_refs[a].at[_linear(_peer(k))],
                    send_sem=sem_refs[2 * a].at[k - 1], recv_sem=sem_refs[2 * a + 1].at[k - 1],
                    device_id=_peer(k), device_id_type=MESH_ID)
                cp.wait_send()
                cp.wait_recv()

    outs = pl.pallas_call(
        body, name=name,
        out_shape=tuple(pltpu.HBM(a.shape, a.dtype) for a in arrs) + tuple(pltpu.HBM(z.shape, z.dtype) for z in zones),
        in_specs=(HBM_SPEC,) * (2 * n) + (SEM_SPEC,) * (2 * n) + (pl.BlockSpec(memory_space=pl.ANY),) * len(after),
        out_specs=(HBM_SPEC,) * (2 * n), input_output_aliases={i: i for i in range(2 * n)},
        compiler_params=pltpu.CompilerParams(has_side_effects=DATAFLOW),
    )(*arrs, *zones, *sems, *after)
    return list(outs[n:])


def _same_core_peers():
    x, y, c = _mesh_position()
    return [(x, y, 1 - c), (1 - x, y, c), (x, 1 - y, c), (1 - x, 1 - y, c)]


def _gather_start(bufs, name):
    n = len(bufs)

    def body(*refs):
        buf_refs, rest = refs[:n], refs[n:]
        sems, token = rest[:2 * n], rest[-1]
        me = _linear(_mesh_position())
        for a in range(n):
            for k, peer in enumerate(_same_core_peers()):
                pltpu.make_async_remote_copy(
                    src_ref=buf_refs[a].at[me], dst_ref=buf_refs[a].at[me],
                    send_sem=sems[2 * a].at[k], recv_sem=sems[2 * a + 1].at[k],
                    device_id=peer, device_id_type=MESH_ID).start()
        token[...] = jnp.zeros_like(token)

    outs = pl.pallas_call(
        body, name=name,
        out_shape=tuple(pltpu.SemaphoreType.DMA((4,)) for _ in range(2 * n))
        + tuple(pltpu.HBM(b.shape, b.dtype) for b in bufs) + (jax.ShapeDtypeStruct((8, 128), F32),),
        in_specs=(HBM_SPEC,) * n,
        out_specs=(SEM_SPEC,) * (2 * n) + (HBM_SPEC,) * n + (pl.BlockSpec(memory_space=pltpu.VMEM),),
        input_output_aliases={a: 2 * n + a for a in range(n)},
        compiler_params=pltpu.CompilerParams(has_side_effects=DATAFLOW),
    )(*[pltpu.with_memory_space_constraint(b, pltpu.HBM) for b in bufs])
    return outs[:2 * n], outs[2 * n:3 * n], outs[3 * n]


def _gather_wait(sems, bufs, after, name):
    n = len(bufs)

    def body(*refs):
        buf_refs, sem_refs = refs[:n], refs[n:3 * n]
        x, y, c = _mesh_position()
        me = _linear((x, y, c))
        for a in range(n):
            for k, peer in enumerate(_same_core_peers()):
                cp = pltpu.make_async_remote_copy(
                    src_ref=buf_refs[a].at[me], dst_ref=buf_refs[a].at[_linear(peer)],
                    send_sem=sem_refs[2 * a].at[k], recv_sem=sem_refs[2 * a + 1].at[k],
                    device_id=peer, device_id_type=MESH_ID)
                cp.wait_send()
                cp.wait_recv()

    return list(pl.pallas_call(
        body, name=name,
        out_shape=tuple(pltpu.HBM(b.shape, b.dtype) for b in bufs),
        in_specs=(HBM_SPEC,) * n + (SEM_SPEC,) * (2 * n) + (pl.BlockSpec(memory_space=pl.ANY),) * len(after),
        out_specs=(HBM_SPEC,) * n, input_output_aliases={a: a for a in range(n)},
        compiler_params=pltpu.CompilerParams(has_side_effects=DATAFLOW),
    )(*bufs, *sems, *after))


def _gather_pass_on(bufs, name):
    n = len(bufs)

    def body(*refs):
        out_refs = refs[n:2 * n]
        send_sems, recv_sems = refs[2 * n:]
        x, y, c = _mesh_position()
        sibling = (x, y, 1 - c)
        chips = [(1 - x, y), (x, 1 - y), (1 - x, 1 - y)]
        copies = []
        for a in range(n):
            for j, chip in enumerate(chips):
                block = out_refs[a].at[_linear((*chip, c))]
                copies.append(pltpu.make_async_remote_copy(
                    src_ref=block, dst_ref=block, send_sem=send_sems.at[3 * a + j], recv_sem=recv_sems.at[3 * a + j],
                    device_id=sibling, device_id_type=MESH_ID))
                copies[-1].start()
        for a in range(n):
            for j, chip in enumerate(chips):
                copies[3 * a + j].wait_send()
                theirs = out_refs[a].at[_linear((*chip, 1 - c))]
                pltpu.make_async_remote_copy(
                    src_ref=theirs, dst_ref=theirs, send_sem=send_sems.at[3 * a + j], recv_sem=recv_sems.at[3 * a + j],
                    device_id=sibling, device_id_type=MESH_ID).wait_recv()

    hbm = pl.BlockSpec(memory_space=pl.ANY)
    return list(pl.pallas_call(
        body, name=name,
        out_shape=[jax.ShapeDtypeStruct(b.shape, b.dtype) for b in bufs],
        in_specs=[hbm] * n, out_specs=[hbm] * n, input_output_aliases={a: a for a in range(n)},
        scratch_shapes=[pltpu.SemaphoreType.DMA((3 * n,)), pltpu.SemaphoreType.DMA((3 * n,))],
    )(*bufs))


def _open_step(c, conv_w, w_ada, b_cols, w_in_t, later, rel_bias, bucket):
    cols = w_ada.shape[1]
    n_later = len(later)

    def body(c_ref, cw_ref, wa_ref, b_ref, w_ref, *rest):
        later_refs, rb_ref, bk_ref = rest[:n_later], rest[n_later], rest[n_later + 1]
        cond_ref, conv_ref, mod_ref, win_ref = rest[n_later + 2:n_later + 6]
        staged_refs, bias_ref, rows_ref = rest[n_later + 6:2 * n_later + 6], rest[2 * n_later + 6], rest[2 * n_later + 7]
        cond_own, mod_own, stage = rest[2 * n_later + 8:2 * n_later + 11]
        later_stage = rest[2 * n_later + 11:3 * n_later + 11]
        s_send, s_recv, w_send, w_recv, local_sems = rest[3 * n_later + 11:]
        x, y, cc = _mesh_position()
        me = _linear((x, y, cc))
        sibling = (x, y, 1 - cc)
        chips = [(1 - x, y), (x, 1 - y), (1 - x, 1 - y)]
        v = c_ref[...]
        cond_own[...] = v * _sigmoid(v)
        stage[...] = w_ref[...].astype(BF16)

        def small(rnd, a, k, src, dst, slot):
            return pltpu.make_async_remote_copy(
                src_ref=src, dst_ref=dst.at[slot], send_sem=s_send.at[rnd, a, k - 1], recv_sem=s_recv.at[rnd, a, k - 1],
                device_id=_peer(k), device_id_type=MESH_ID)

        def block(p):
            return win_ref.at[_linear(p)]

        def big(k, blk, to, src=None):
            return pltpu.make_async_remote_copy(
                src_ref=block(blk) if src is None else src, dst_ref=block(blk),
                send_sem=w_send.at[k], recv_sem=w_recv.at[k], device_id=to, device_id_type=MESH_ID)

        mine = [pltpu.make_async_copy(cond_own, cond_ref.at[me], local_sems.at[0]),
                pltpu.make_async_copy(cw_ref, conv_ref.at[me], local_sems.at[1]),
                pltpu.make_async_copy(stage, block((x, y, cc)), local_sems.at[2])]
        for cp in mine:
            cp.start()
        sends = []
        for k in range(1, N_DEV):
            sends += [small(0, 0, k, cond_own, cond_ref, me), small(0, 1, k, cw_ref, conv_ref, me)]
        for cp in sends:
            cp.start()
        first = [big(0, (x, y, cc), sibling, src=stage)]
        first += [big(1 + j, (x, y, cc), (*chip, cc), src=stage) for j, chip in enumerate(chips)]
        for cp in first:
            cp.start()
        for a in range(n_later):
            later_stage[a][...] = later_refs[a][...].astype(BF16)
            mine.append(pltpu.make_async_copy(later_stage[a], staged_refs[a].at[me], local_sems.at[4 + a]))
            mine[-1].start()
        _fill_bias_table(rb_ref, bk_ref, bias_ref)
        for k in range(1, N_DEV):
            small(0, 0, k, cond_own, cond_ref, _linear(_peer(k))).wait_recv()
            small(0, 1, k, cw_ref, conv_ref, _linear(_peer(k))).wait_recv()
        mine[0].wait()
        cond_all = jnp.concatenate([cond_ref[k] for k in range(N_DEV)], axis=0)
        mod_own[...] = _dot(cond_all, wa_ref[...]) + b_ref[me]
        mine.append(pltpu.make_async_copy(mod_own, mod_ref.at[me], local_sems.at[3]))
        mine[-1].start()
        second = [small(1, 0, k, mod_own, mod_ref, me) for k in range(1, N_DEV)]
        for cp in second:
            cp.start()
        passed = []
        for j, chip in enumerate(chips):
            big(1 + j, (*chip, cc), (x, y, cc)).wait_recv()
            fwd = big(4 + j, (*chip, cc), sibling)
            fwd.start()
            passed.append(fwd)
        big(0, sibling, (x, y, cc)).wait_recv()
        for j, chip in enumerate(chips):
            big(4 + j, (*chip, 1 - cc), (x, y, cc)).wait_recv()
        for k in range(1, N_DEV):
            small(1, 0, k, mod_own, mod_ref, _linear(_peer(k))).wait_recv()
        for cp in sends + first + second + passed:
            cp.wait_send()
        for cp in mine[1:]:
            cp.wait()
        flat = jnp.concatenate([mod_ref[j, pl.ds(me, 1), :] for j in range(N_DEV)], axis=1)
        rows_ref[...] = jnp.concatenate([flat[:, D_MODEL * r:D_MODEL * (r + 1)] for r in range(N_MOD)]
                                        + [jnp.zeros((8 - N_MOD, D_MODEL), F32)], axis=0)

    vmem = pl.BlockSpec(memory_space=pltpu.VMEM)
    outs = pl.pallas_call(
        body, name="open_step",
        out_shape=[jax.ShapeDtypeStruct((N_DEV,) + c.shape, F32), jax.ShapeDtypeStruct((N_DEV,) + conv_w.shape, F32),
                   jax.ShapeDtypeStruct((N_DEV, N_DEV, cols), F32),
                   jax.ShapeDtypeStruct((N_DEV,) + w_in_t.shape, BF16)]
        + [jax.ShapeDtypeStruct((N_DEV,) + a.shape, BF16) for a in later]
        + [jax.ShapeDtypeStruct((N_Q_HEADS, BLOCK, 2 * BLOCK), F32), jax.ShapeDtypeStruct((8, D_MODEL), F32)],
        in_specs=[vmem] * (5 + n_later) + [pl.BlockSpec(memory_space=pltpu.SMEM), vmem],
        out_specs=[vmem, vmem, vmem, ANY_SPEC] + [ANY_SPEC] * n_later + [vmem, vmem],
        scratch_shapes=[pltpu.VMEM(c.shape, F32), pltpu.VMEM((N_DEV, cols), F32), pltpu.VMEM(w_in_t.shape, BF16)]
        + [pltpu.VMEM(a.shape, BF16) for a in later]
        + [pltpu.SemaphoreType.DMA((2, 2, N_DEV - 1)), pltpu.SemaphoreType.DMA((2, 2, N_DEV - 1)),
           pltpu.SemaphoreType.DMA((7,)), pltpu.SemaphoreType.DMA((7,)),
           pltpu.SemaphoreType.DMA((4 + n_later,))],
        compiler_params=_params(vmem=VMEM_LIMIT_LARGE),
    )(c, conv_w, w_ada, b_cols, w_in_t, *later, rel_bias, bucket)
    return outs[0], outs[1], outs[5 + n_later], outs[3], list(outs[4:4 + n_later]), outs[4 + n_later]


def _in_proj(after, x, mod, g_norm1, w_in, tm):
    s = x.shape[0]

    def body(x_ref, mod_ref, g_ref, w_ref, h_ref, q_ref, kv_ref, gb_ref, gc_ref, xc_ref):
        xf = x_ref[...]
        n = xf * _rsqrt_mean_sq(xf) * g_ref[...]
        h = (n * (1.0 + mod_ref[SC1:SC1 + 1, :]) + mod_ref[SH1:SH1 + 1, :]).astype(BF16)
        h_ref[...] = h
        p = _dot_nt(h, w_ref[...])
        q_ref[...] = p[:, 0:512].astype(BF16)
        kv_ref[...] = p[:, 512:768].astype(BF16)
        gb_ref[...] = p[:, 768:1280].astype(BF16)
        gc_ref[...] = p[:, 1280:1792].astype(BF16)
        xc_ref[...] = p[:, 1792:2304].astype(BF16)

    return pl.pallas_call(
        _coming_behind(body), name="in_proj", grid=(s // tm,),
        in_specs=[ANY_SPEC, _rows(tm, D_MODEL), _full((8, D_MODEL)), _full((1, D_MODEL)), _full((IN_PROJ_WIDTH, D_MODEL))],
        out_specs=[_rows(tm, D_MODEL), _rows(tm, 512), _rows(tm, 256), _rows(tm, 512), _rows(tm, 512), _rows(tm, 512)],
        out_shape=[jax.ShapeDtypeStruct((s, D_MODEL), BF16), jax.ShapeDtypeStruct((s, 512), BF16),
                   jax.ShapeDtypeStruct((s, 256), BF16), jax.ShapeDtypeStruct((s, 512), BF16),
                   jax.ShapeDtypeStruct((s, 512), BF16), jax.ShapeDtypeStruct((s, 512), BF16)],
        compiler_params=_params(("arbitrary",), VMEM_LIMIT_LARGE),
    )(after, x, mod, g_norm1, w_in)


def _t5_bucket(dist):
    max_exact = N_BUCKETS // 2
    is_small = dist < max_exact
    d = jnp.maximum(dist, 1).astype(F32)
    large = max_exact + (jnp.log(d / max_exact) / math.log(MAX_DISTANCE / max_exact)
                         * (N_BUCKETS - max_exact)).astype(jnp.int32)
    large = jnp.minimum(large, N_BUCKETS - 1)
    return jnp.where(is_small, dist, large)


def _bucket_table():
    qi = jnp.arange(BLOCK, dtype=jnp.int32)[:, None]
    sj = jnp.arange(2 * BLOCK, dtype=jnp.int32)[None, :]
    return _t5_bucket(jnp.maximum(qi + BLOCK - sj, 0))


def _window_mask():
    qi = lax.broadcasted_iota(jnp.int32, (BLOCK, 2 * BLOCK), 0)
    sj = lax.broadcasted_iota(jnp.int32, (BLOCK, 2 * BLOCK), 1)
    dist = qi + BLOCK - sj
    return (dist >= 0) & (dist < BLOCK)


def _fill_bias_table(rb_ref, bk_ref, o_ref):
    bk = bk_ref[...]
    inside = _window_mask()
    for h in range(N_Q_HEADS):
        acc = jnp.zeros((BLOCK, 2 * BLOCK), F32)
        for b in range(N_BUCKETS):
            acc = jnp.where(bk == b, rb_ref[h, b], acc)
        o_ref[h] = jnp.where(inside, acc, NEG_INF)


def _load_kv_window(kv_ref, n):
    prev = jnp.maximum(n - 1, 0)
    kvw = jnp.concatenate([kv_ref[pl.ds(pl.multiple_of(prev * BLOCK, BLOCK), BLOCK), :],
                           kv_ref[pl.ds(pl.multiple_of(n * BLOCK, BLOCK), BLOCK), :]], axis=0)
    k, v = kvw[:, 0:128], kvw[:, 128:256]
    k_sw = pltpu.roll(k.astype(F32), 64, 1).astype(BF16)
    v_sw = pltpu.roll(v.astype(F32), 64, 1).astype(BF16)
    return (k, k_sw), (v, v_sw)


def _conv_taps(gc, xc, gc_prev, xc_prev, n):
    u = gc * xc
    before = jnp.where(n > 0, gc_prev.astype(F32) * xc_prev.astype(F32), 0.0)
    last = before.shape[0] - 1
    row = lax.broadcasted_iota(jnp.int32, u.shape, 0)
    u1 = jnp.where(row == 0, before[last:last + 1, :], pltpu.roll(u, 1, 0))
    u2 = jnp.where(row == 0, before[last - 1:last, :],
                   jnp.where(row == 1, before[last:last + 1, :], pltpu.roll(u, 2, 0)))
    return u, u1, u2


def _mixer_fwd(q, kv, gb, gc, xc, bias, sinks, conv_w, g_attn, g_conv):
    s = q.shape[0]
    nb = s // BLOCK

    per_step = min(MIXER_BLOCKS, nb)
    tile = per_step * BLOCK

    def one_block(n, slot, before, sink_ref, q_ref, kv_ref, gb_ref, gc_ref, xc_ref, bias_ref, cw_ref, ga_ref,
                  gcv_ref, attn_ref, merged_ref, lse_ref, p_ref):
        rows = slice(slot * BLOCK, (slot + 1) * BLOCK)
        ks, vs = _load_kv_window(kv_ref, n)
        lane = lax.broadcasted_iota(jnp.int32, (BLOCK, BLOCK), 1)
        low = lane < HEAD_DIM
        col = lax.broadcasted_iota(jnp.int32, (BLOCK, 2 * BLOCK), 1)
        no_prev = (col < BLOCK) & (n == 0)
        lse_all = jnp.zeros((BLOCK, BLOCK), F32)
        pairs = []
        for p in range(4):
            qp = q_ref[rows, 128 * p:128 * (p + 1)].astype(F32)
            kvh = p // 2
            res = []
            for e in range(2):
                h = 2 * p + e
                qm = jnp.where(low if e == 0 else ~low, qp, 0.0).astype(BF16)
                sw = 0 if kvh == e else 1
                sc = _dot_nt(qm, ks[sw]) * SCALE + bias_ref[h]
                sc = jnp.where(no_prev, NEG_INF, sc)
                sink = sink_ref[h]
                m = jnp.maximum(jnp.max(sc, axis=-1, keepdims=True), sink)
                pe = jnp.exp(sc - m)
                den = jnp.sum(pe, axis=-1, keepdims=True) + jnp.exp(sink - m)
                pb = (pe * (1.0 / den)).astype(BF16)
                p_ref[slot, h] = pb
                res.append(_dot(pb, vs[sw]))
                lse_all = lse_all + jnp.where(lane == h, m + jnp.log(den), 0.0)
            pairs.append(jnp.where(low, res[0], res[1]))
        attn = jnp.concatenate(pairs, axis=1)
        attn_ref[rows, :] = attn
        lse_ref[rows, :] = lse_all
        u, u1, u2 = _conv_taps(gc_ref[rows, :].astype(F32), xc_ref[rows, :].astype(F32), before[0], before[1], n)
        cw = cw_ref[...]
        cv = gb_ref[rows, :].astype(F32) * (cw[0:1, :] * u2 + cw[1:2, :] * u1 + cw[2:3, :] * u)
        an = attn * _rsqrt_mean_sq(attn) * ga_ref[...]
        cn = cv * _rsqrt_mean_sq(cv) * gcv_ref[...]
        merged_ref[rows, :] = jnp.concatenate([an, cn], axis=1).astype(BF16)

    def body(sink_ref, q_ref, kv_ref, gb_ref, gc_ref, xc_ref, gcp_ref, xcp_ref, *rest):
        step = pl.program_id(0)
        for sub in range(per_step):
            ahead = slice(sub * BLOCK - PREV_ROWS, sub * BLOCK)
            before = (gcp_ref[...], xcp_ref[...]) if sub == 0 else (gc_ref[ahead, :], xc_ref[ahead, :])
            one_block(step * per_step + sub, sub, before, sink_ref, q_ref, kv_ref, gb_ref, gc_ref, xc_ref, *rest)

    blk = lambda w: pl.BlockSpec((tile, w), lambda n: (n, 0))
    prev8 = pl.BlockSpec((PREV_ROWS, 512), lambda n: (jnp.maximum(n * (tile // PREV_ROWS) - 1, 0), 0))
    return pl.pallas_call(
        body, name="mixer_fwd", grid=(nb // per_step,),
        in_specs=[pl.BlockSpec(memory_space=pltpu.SMEM), blk(512), _full((s, 256)), blk(512), blk(512), blk(512),
                  prev8, prev8, _full((N_Q_HEADS, BLOCK, 2 * BLOCK)), _full((3, 512)), _full((1, 512)),
                  _full((1, 512))],
        out_specs=[blk(512), blk(1024), blk(128),
                   pl.BlockSpec((per_step, N_Q_HEADS, BLOCK, 2 * BLOCK), lambda n: (n, 0, 0, 0))],
        out_shape=[jax.ShapeDtypeStruct((s, 512), F32), jax.ShapeDtypeStruct((s, 1024), BF16),
                   jax.ShapeDtypeStruct((s, 128), F32),
                   jax.ShapeDtypeStruct((nb, N_Q_HEADS, BLOCK, 2 * BLOCK), BF16)],
        compiler_params=_params(("arbitrary",)),
    )(sinks, q, kv, gb, gc, xc, gc, xc, bias, conv_w, g_attn, g_conv)


def _out_proj(merged, x, mod, w_out, tm):
    s = x.shape[0]

    def body(m_ref, x_ref, mod_ref, w_ref, o_ref, x1_ref):
        o = _dot(m_ref[...], w_ref[...])
        o_ref[...] = o.astype(BF16)
        x1_ref[...] = x_ref[...] + mod_ref[G1:G1 + 1, :] * o

    return pl.pallas_call(
        body, name="out_proj", grid=(s // tm,),
        in_specs=[_rows(tm, D_MODEL), _rows(tm, D_MODEL), _full((8, D_MODEL)), _full((D_MODEL, D_MODEL))],
        out_specs=[_rows(tm, D_MODEL), _rows(tm, D_MODEL)],
        out_shape=[jax.ShapeDtypeStruct((s, D_MODEL), BF16), jax.ShapeDtypeStruct((s, D_MODEL), F32)],
        compiler_params=_params(("arbitrary",)),
    )(merged, x, mod, w_out)


def _resident(shape):
    nd = len(shape)
    return pl.BlockSpec(shape, lambda *_: (0,) * nd, pipeline_mode=pl.Buffered(1))


def _ffn(x1, o1, merged, mod, g_norm2, w_gu, w_down, w_out, g_final, target, tm):
    s = x1.shape[0]
    chunk = D_FF // FFN_CHUNKS

    def body(x_ref, o1_ref, mg_ref, mod_ref, g_ref, wgu_ref, wd_ref, wo_ref, gf_ref, t_ref,
             h_ref, act_ref, do_ref, dgu_ref, dx1_ref, dwo_ref, dm_ref, small_ref, dwo_acc):
        @pl.when(pl.program_id(0) == 0)
        def _():
            small_ref[...] = jnp.zeros_like(small_ref)
            dwo_acc[...] = jnp.zeros_like(dwo_acc)

        xf = x_ref[...]
        n = xf * _rsqrt_mean_sq(xf) * g_ref[...]
        h = (n * (1.0 + mod_ref[SC2:SC2 + 1, :]) + mod_ref[SH2:SH2 + 1, :]).astype(BF16)
        h_ref[...] = h
        gates, ups, o = [], [], None
        for j in range(FFN_CHUNKS):
            lo = j * chunk
            gate = _dot_nt(h, wgu_ref[lo:lo + chunk, :])
            up = _dot_nt(h, wgu_ref[D_FF + lo:D_FF + lo + chunk, :])
            sg = _sigmoid(gate)
            act = (gate * sg * up).astype(BF16)
            act_ref[:, lo:lo + chunk] = act
            gates.append((up * (sg * (1.0 + gate * (1.0 - sg)))).astype(BF16))
            ups.append((gate * sg).astype(BF16))
            part = _dot(act, wd_ref[lo:lo + chunk, :])
            o = part if o is None else o + part
        g2 = mod_ref[G2:G2 + 1, :]
        x2 = xf + g2 * o
        r = _rsqrt_mean_sq(x2)
        xn = x2 * r
        gf = gf_ref[...]
        err = xn * gf - t_ref[...]
        dy = err * (1.0 / D_MODEL)
        dxn = dy * gf
        dx2 = r * (dxn - xn * jnp.mean(dxn * xn, axis=-1, keepdims=True))
        small_ref[4:5, :] += _colsum(dy * xn)
        small_ref[5:6, :] += _colsum(err * err)
        small_ref[3:4, :] += _colsum(dx2 * o)
        do = (dx2 * g2).astype(BF16)
        do_ref[...] = do
        dh = None
        for j in range(FFN_CHUNKS):
            lo = j * chunk
            dact = _dot_nt(do, wd_ref[lo:lo + chunk, :])
            dgate = (dact * gates[j].astype(F32)).astype(BF16)
            dup = (dact * ups[j].astype(F32)).astype(BF16)
            dgu_ref[:, lo:lo + chunk] = dgate
            dgu_ref[:, D_FF + lo:D_FF + lo + chunk] = dup
            part = _dot(dgate, wgu_ref[lo:lo + chunk, :]) + _dot(dup, wgu_ref[D_FF + lo:D_FF + lo + chunk, :])
            dh = part if dh is None else dh + part
        dx1 = dx2 + _norm_mod_bwd(dh, xf, g_ref[...], mod_ref[SC2:SC2 + 1, :], small_ref)
        dx1_ref[...] = dx1.astype(BF16)
        small_ref[7:8, :] += _colsum(dx1 * o1_ref[...].astype(F32))
        do1 = (dx1 * mod_ref[G1:G1 + 1, :]).astype(BF16)
        dm_ref[...] = _dot_nt(do1, wo_ref[...]).astype(BF16)
        dwo = dwo_acc[...] + _dot_tn(mg_ref[...], do1)
        dwo_acc[...] = dwo
        dwo_ref[...] = dwo.astype(BF16)

        @pl.when(pl.program_id(0) == pl.num_programs(0) - 1)
        def _():
            total = jnp.sum(small_ref[5:6, :], axis=-1, keepdims=True) * (0.5 / D_MODEL)
            small_ref[6:7, :] = jnp.broadcast_to(total, (1, D_MODEL))

    narrow = jax.ShapeDtypeStruct((s, D_MODEL), BF16)
    return pl.pallas_call(
        body, name="ffn", grid=(s // tm,),
        in_specs=[_rows(tm, D_MODEL), _rows(tm, D_MODEL), _rows(tm, D_MODEL), _full((8, D_MODEL)), _full((1, D_MODEL)),
                  _resident((2 * D_FF, D_MODEL)), _resident((D_FF, D_MODEL)), _resident((D_MODEL, D_MODEL)),
                  _full((1, D_MODEL)), _rows(tm, D_MODEL)],
        out_specs=[_rows(tm, D_MODEL), _rows(tm, D_FF), _rows(tm, D_MODEL), _rows(tm, 2 * D_FF), _rows(tm, D_MODEL),
                   _full((D_MODEL, D_MODEL)), _rows(tm, D_MODEL), _full((8, D_MODEL))],
        out_shape=[narrow, jax.ShapeDtypeStruct((s, D_FF), BF16), narrow, jax.ShapeDtypeStruct((s, 2 * D_FF), BF16),
                   narrow, jax.ShapeDtypeStruct((D_MODEL, D_MODEL), BF16), narrow,
                   jax.ShapeDtypeStruct((8, D_MODEL), F32)],
        scratch_shapes=[pltpu.VMEM((D_MODEL, D_MODEL), F32)],
        compiler_params=_params(("arbitrary",), VMEM_LIMIT_LARGE),
    )(x1, o1, merged, mod, g_norm2, w_gu, w_down, w_out, g_final, target)


def _norm_mod_bwd(dh, xf, g, scale_row, small_ref):
    r = _rsqrt_mean_sq(xf)
    xn = xf * r
    small_ref[0:1, :] += _colsum(dh)
    small_ref[1:2, :] += _colsum(dh * (xn * g))
    dn = dh * (1.0 + scale_row)
    small_ref[2:3, :] += _colsum(dn * xn)
    dxn = dn * g
    return r * (dxn - xn * jnp.mean(dxn * xn, axis=-1, keepdims=True))


def _group_norm_bwd(dm, a, g):
    r = _rsqrt_mean_sq(a)
    an = a * r
    dan = dm * g
    return r * (dan - an * jnp.mean(dan * an, axis=-1, keepdims=True)), _colsum(dm * an)


def _sum_by_bucket(db_ref, bk_ref, o_ref, rows_ref):
    bk = bk_ref[...]
    for b in range(N_BUCKETS):
        sel = (bk == b).astype(F32)
        for h in range(N_Q_HEADS):
            rows_ref[N_BUCKETS * h + b:N_BUCKETS * h + b + 1, :] = _colsum(db_ref[h] * sel)
    head = lax.broadcasted_iota(jnp.int32, (N_BUCKETS, REL_LANES), 1)
    out = jnp.zeros((N_BUCKETS, REL_LANES), F32)
    for h in range(N_Q_HEADS):
        per_bucket = jnp.sum(rows_ref[N_BUCKETS * h:N_BUCKETS * (h + 1), :], axis=-1, keepdims=True)
        out = out + jnp.where(head == h, per_bucket, 0.0)
    o_ref[...] = out


def _mixer_bwd(after, q, kv, gb, gc, xc, probs, sinks, conv_w, g_attn, g_conv, attn, lse, dmerged, bucket):
    s = q.shape[0]
    nb = s // BLOCK

    per_step = min(MIXER_BLOCKS, nb)
    tile = per_step * BLOCK
    steps = nb // per_step

    def one_block(n, slot, before, nxt, sink_ref, q_ref, kv_ref, gb_ref, gc_ref, xc_ref, p_ref, cw_ref, ga_ref,
                  gcv_ref, attn_ref, lse_ref, dm_ref, dproj_ref, dbias_ref, dsink_ref, small_ref):
        rows = slice(slot * BLOCK, (slot + 1) * BLOCK)
        next_dy, next_dkv = nxt
        dm = dm_ref[rows, :].astype(F32)
        gbv, gcv_, xcv = gb_ref[rows, :].astype(F32), gc_ref[rows, :].astype(F32), xc_ref[rows, :].astype(F32)
        u, u1, u2 = _conv_taps(gcv_, xcv, before[0], before[1], n)
        cw = cw_ref[...]
        yv = cw[0:1, :] * u2 + cw[1:2, :] * u1 + cw[2:3, :] * u
        dcv, dg_conv = _group_norm_bwd(dm[:, 512:1024], gbv * yv, gcv_ref[...])
        small_ref[1:2, :] += dg_conv
        dproj_ref[rows, 768:1280] = (dcv * yv).astype(BF16)
        dy = dcv * gbv
        row = lax.broadcasted_iota(jnp.int32, dy.shape, 0)
        d1 = jnp.where(row == BLOCK - 1, next_dy[0:1, :], pltpu.roll(dy, BLOCK - 1, 0))
        d2 = jnp.where(row == BLOCK - 2, next_dy[0:1, :],
                       jnp.where(row == BLOCK - 1, next_dy[1:2, :], pltpu.roll(dy, BLOCK - 2, 0)))
        du = cw[2:3, :] * dy + cw[1:2, :] * d1 + cw[0:1, :] * d2
        dproj_ref[rows, 1280:1792] = (du * xcv).astype(BF16)
        dproj_ref[rows, 1792:2304] = (du * gcv_).astype(BF16)
        small_ref[2:3, :] += _colsum(dy * u2)
        small_ref[3:4, :] += _colsum(dy * u1)
        small_ref[4:5, :] += _colsum(dy * u)

        attn_v = attn_ref[rows, :]
        dout, dg_attn = _group_norm_bwd(dm[:, 0:512], attn_v, ga_ref[...])
        small_ref[0:1, :] += dg_attn
        ks, vs = _load_kv_window(kv_ref, n)
        lane = lax.broadcasted_iota(jnp.int32, (BLOCK, BLOCK), 1)
        low = lane < HEAD_DIM
        lse_all = lse_ref[rows, :]
        dsink = jnp.zeros((BLOCK, BLOCK), F32)
        dq_pairs = []
        dk_groups, dv_groups = [], []
        for kvh in range(2):
            ds_rows, pr_rows, q_rows, do_rows = [], [], [], []
            for p in (2 * kvh, 2 * kvh + 1):
                qp = q_ref[rows, 128 * p:128 * (p + 1)].astype(F32)
                do_p = dout[:, 128 * p:128 * (p + 1)]
                prod = do_p * attn_v[:, 128 * p:128 * (p + 1)]
                res = []
                for e in range(2):
                    h = 2 * p + e
                    half = low if e == 0 else ~low
                    qm = jnp.where(half, qp, 0.0).astype(BF16)
                    dom = jnp.where(half, do_p, 0.0).astype(BF16)
                    delta = jnp.sum(jnp.where(half, prod, 0.0), axis=-1, keepdims=True)
                    lse_h = jnp.sum(jnp.where(lane == h, lse_all, 0.0), axis=-1, keepdims=True)
                    sw = 0 if kvh == e else 1
                    pb = p_ref[slot, h]
                    dp = _dot_nt(dom, vs[sw])
                    ds = pb.astype(F32) * (dp - delta)
                    dbias_ref[h] += ds
                    dsink = dsink + jnp.where(lane == h, -jnp.exp(sink_ref[h] - lse_h) * delta, 0.0)
                    dsb = ds.astype(BF16)
                    res.append(_dot(dsb, ks[sw]) * SCALE)
                    ds_rows.append(dsb)
                    pr_rows.append(pb)
                    q_rows.append(qm)
                    do_rows.append(dom)
                dq_pairs.append(jnp.where(low, res[0], res[1]))
            dk_g = _dot_tn(jnp.concatenate(ds_rows, axis=0), jnp.concatenate(q_rows, axis=0)) * SCALE
            dv_g = _dot_tn(jnp.concatenate(pr_rows, axis=0), jnp.concatenate(do_rows, axis=0))
            dk_groups.append(dk_g + pltpu.roll(dk_g, 64, 1))
            dv_groups.append(dv_g + pltpu.roll(dv_g, 64, 1))
        dproj_ref[rows, 0:512] = jnp.concatenate(dq_pairs, axis=1).astype(BF16)
        dsink_ref[...] += dsink
        low_kv = lax.broadcasted_iota(jnp.int32, (2 * BLOCK, BLOCK), 1) < HEAD_DIM
        dkv_win = jnp.concatenate([jnp.where(low_kv, dk_groups[0], dk_groups[1]),
                                   jnp.where(low_kv, dv_groups[0], dv_groups[1])], axis=1)
        dproj_ref[rows, 512:768] = (dkv_win[BLOCK:2 * BLOCK, :] + next_dkv).astype(BF16)
        return dy[0:8, :], dkv_win[0:BLOCK, :]

    def body(sink_ref, q_ref, kv_ref, gb_ref, gc_ref, xc_ref, gcp_ref, xcp_ref, p_ref, cw_ref, ga_ref, gcv_ref,
             attn_ref, lse_ref, dm_ref, bk_ref, dproj_ref, drel_ref, dsink_ref, small_ref,
             dy_ref, dkv_ref, dbias_ref, rows_ref):
        refs = (p_ref, cw_ref, ga_ref, gcv_ref, attn_ref, lse_ref, dm_ref, dproj_ref, dbias_ref, dsink_ref, small_ref)
        step = pl.program_id(0)

        @pl.when(step == 0)
        def _():
            dbias_ref[...] = jnp.zeros_like(dbias_ref)
            dsink_ref[...] = jnp.zeros_like(dsink_ref)
            small_ref[...] = jnp.zeros_like(small_ref)
            dy_ref[...] = jnp.zeros_like(dy_ref)
            dkv_ref[...] = jnp.zeros_like(dkv_ref)

        nxt = (dy_ref[...], dkv_ref[...])
        for sub in reversed(range(per_step)):
            ahead = slice(sub * BLOCK - PREV_ROWS, sub * BLOCK)
            before = (gcp_ref[...], xcp_ref[...]) if sub == 0 else (gc_ref[ahead, :], xc_ref[ahead, :])
            nxt = one_block((steps - 1 - step) * per_step + sub, sub, before, nxt,
                            sink_ref, q_ref, kv_ref, gb_ref, gc_ref, xc_ref, *refs)
        dy_ref[...], dkv_ref[...] = nxt

        @pl.when(step == steps - 1)
        def _():
            small_ref[5:6, :] = jnp.concatenate([_colsum(dsink_ref[...]), jnp.zeros((1, 512 - BLOCK), F32)], axis=1)
            _sum_by_bucket(dbias_ref, bk_ref, drel_ref, rows_ref)

    blk = lambda w: pl.BlockSpec((tile, w), lambda t: (steps - 1 - t, 0))
    prev8 = pl.BlockSpec((PREV_ROWS, 512),
                         lambda t: (jnp.maximum((steps - 1 - t) * (tile // PREV_ROWS) - 1, 0), 0))
    bf = lambda w: jax.ShapeDtypeStruct((s, w), BF16)
    return pl.pallas_call(
        _coming_behind(body), name="mixer_bwd", grid=(steps,),
        in_specs=[ANY_SPEC, pl.BlockSpec(memory_space=pltpu.SMEM), blk(512), _full((s, 256)), blk(512), blk(512), blk(512),
                  prev8, prev8,
                  pl.BlockSpec((per_step, N_Q_HEADS, BLOCK, 2 * BLOCK), lambda t: (steps - 1 - t, 0, 0, 0)),
                  _full((3, 512)), _full((1, 512)), _full((1, 512)), blk(512), blk(128), blk(1024),
                  _full((BLOCK, 2 * BLOCK))],
        out_specs=[blk(IN_PROJ_WIDTH), _full((N_BUCKETS, REL_LANES)), _full((BLOCK, BLOCK)), _full((8, 512))],
        out_shape=[bf(IN_PROJ_WIDTH), jax.ShapeDtypeStruct((N_BUCKETS, REL_LANES), F32),
                   jax.ShapeDtypeStruct((BLOCK, BLOCK), F32), jax.ShapeDtypeStruct((8, 512), F32)],
        scratch_shapes=[pltpu.VMEM((8, 512), F32), pltpu.VMEM((BLOCK, 2 * KV_WIDTH), F32),
                        pltpu.VMEM((N_Q_HEADS, BLOCK, 2 * BLOCK), F32),
                        pltpu.VMEM((N_BUCKETS * N_Q_HEADS, 2 * BLOCK), F32)],
        compiler_params=_params(("arbitrary",), VMEM_LIMIT_LARGE),
    )(after, sinks, q, kv, gb, gc, xc, gc, xc, probs, conv_w, g_attn, g_conv, attn, lse, dmerged, bucket)


def _in_proj_bwd(after, dproj, x, dx1, mod, g_norm1, w_in, tm):
    s = x.shape[0]

    def body(dproj_ref, x_ref, dx1_ref, mod_ref, g_ref, w_ref, dx_ref, small_ref):
        @pl.when(pl.program_id(0) == 0)
        def _():
            small_ref[...] = jnp.zeros_like(small_ref)

        dh = _dot(dproj_ref[...], w_ref[...])
        dx_ref[...] = dx1_ref[...].astype(F32) + _norm_mod_bwd(dh, x_ref[...], g_ref[...], mod_ref[SC1:SC1 + 1, :],
                                                               small_ref)

    return pl.pallas_call(
        _coming_behind(body), name="in_proj_bwd", grid=(s // tm,),
        in_specs=[ANY_SPEC, _rows(tm, IN_PROJ_WIDTH), _rows(tm, D_MODEL), _rows(tm, D_MODEL), _full((8, D_MODEL)),
                  _full((1, D_MODEL)), _full((IN_PROJ_WIDTH, D_MODEL))],
        out_specs=[_rows(tm, D_MODEL), _full((8, D_MODEL))],
        out_shape=[jax.ShapeDtypeStruct((s, D_MODEL), F32), jax.ShapeDtypeStruct((8, D_MODEL), F32)],
        compiler_params=_params(("arbitrary",), VMEM_LIMIT_LARGE),
    )(after, dproj, x, dx1, mod, g_norm1, w_in)


def _weight_grad(a, b, tk, ts, name, after=None):
    s, k = a.shape
    n = b.shape[1]
    nt = s // ts
    extra = [] if after is None else [after]

    def body(a_ref, b_ref, *rest):
        o_ref, acc_ref = rest[-2:]
        t = pl.program_id(1)
        @pl.when(t == 0)
        def _():
            acc_ref[...] = jnp.zeros_like(acc_ref)

        acc = acc_ref[...] + _dot_tn(a_ref[...], b_ref[...])
        acc_ref[...] = acc
        o_ref[...] = acc.astype(BF16)

    return pl.pallas_call(
        body, name=name, grid=(k // tk, nt),
        in_specs=[pl.BlockSpec((ts, tk), lambda i, t: (t, i)), pl.BlockSpec((ts, n), lambda i, t: (t, 0))]
        + [ANY_SPEC] * len(extra),
        out_specs=pl.BlockSpec((tk, n), lambda i, t: (i, 0)),
        out_shape=jax.ShapeDtypeStruct((k, n), BF16),
        scratch_shapes=[pltpu.VMEM((tk, n), F32)],
        compiler_params=_params(("arbitrary", "arbitrary"), VMEM_LIMIT_LARGE),
    )(a, b, *extra)


def _lanes_from(x, start, width):
    n = x.shape[1]
    return pltpu.roll(x, (n - start) % n, 1)[:, 0:width]


def _adamw_w_ada(me, cond_all, packed_all, w, m, v, tr):
    r, cols = w.shape

    def body(me_ref, c_ref, p_ref, w_ref, m_ref, v_ref, g_ref, d_ref, mo_ref, vo_ref):
        dmod = jnp.concatenate([p_ref[k][:, OFF_DMOD:OFF_DMOD + N_MOD * D_MODEL] for k in range(N_DEV)], axis=0)
        mine = _lanes_from(dmod, me_ref[0] * cols, cols)
        pad = lambda a: jnp.concatenate([a, jnp.zeros((128 - N_DEV, a.shape[1]), F32)], axis=0)
        g = _dot_tn(pad(c_ref[...]), pad(mine))
        g_ref[...] = g
        d_ref[...], mo_ref[...], vo_ref[...] = _adam_math(w_ref[...], g, m_ref[...], v_ref[...])

    tile = pl.BlockSpec((tr, cols), lambda i, me_ref: (i, 0))
    return pl.pallas_call(
        body, name="adamw_w_ada",
        grid_spec=pltpu.PrefetchScalarGridSpec(
            num_scalar_prefetch=1, grid=(r // tr,),
            in_specs=[pl.BlockSpec((N_DEV, tr), lambda i, me_ref: (0, i)),
                      pl.BlockSpec(packed_all.shape, lambda i, me_ref: (0, 0, 0)), tile, tile, tile],
            out_specs=[tile] * 4),
        out_shape=[jax.ShapeDtypeStruct((r, cols), F32)] * 4,
        compiler_params=_params(("arbitrary",)),
    )(me, cond_all, packed_all, w, m, v)


SMALL_PARAMS = (("rel_bias", None), ("b_ada", (OFF_DMOD, N_MOD * D_MODEL)), ("g_norm1", (OFF_GN1, D_MODEL)),
                ("sinks", (OFF_SINK, N_Q_HEADS)), ("conv_w", None), ("g_attn_out", (OFF_GATT, ATTN_WIDTH)),
                ("g_conv_out", (OFF_GCV, CONV_WIDTH)), ("g_norm2", (OFF_GN2, D_MODEL)),
                ("g_final", (OFF_GFIN, D_MODEL)))


def _small_update(me, packed_all, rel_all, state, after):
    n_p = len(SMALL_PARAMS)
    flat = [a for triple in state for a in triple]
    conv_cols = state[4][0].shape[-1]

    def body(me_ref, p_ref, r_ref, *refs):
        ins = refs[:3 * n_p]
        loss_ref, outs = refs[3 * n_p + len(after)], refs[3 * n_p + len(after) + 1:]
        small, rel = p_ref[0], r_ref[0]
        for k in range(1, N_DEV):
            small = small + p_ref[k]
            rel = rel + r_ref[k]
        rel = jnp.concatenate([rel, jnp.zeros((REL_LANES - N_BUCKETS, REL_LANES), F32)], axis=0).T
        rel = rel[0:N_Q_HEADS, 0:N_BUCKETS]
        loss_ref[...] = small[:, OFF_LOSS:OFF_LOSS + 128]
        taps = jnp.concatenate([small[:, OFF_CONVW + CONV_WIDTH * j:OFF_CONVW + CONV_WIDTH * (j + 1)]
                                for j in range(3)] + [jnp.zeros((5, CONV_WIDTH), F32)], axis=0)
        conv_g = _lanes_from(taps, me_ref[0] * conv_cols, conv_cols)[0:3, :]
        for i, (name, lanes) in enumerate(SMALL_PARAMS):
            w_ref, m_ref, v_ref = ins[3 * i:3 * i + 3]
            if name == "conv_w":
                for j in range(3):
                    outs[4 * i][j] = conv_g[j:j + 1, :]
                    outs[4 * i + 1][j], outs[4 * i + 2][j], outs[4 * i + 3][j] = _adam_math(
                        w_ref[j], conv_g[j:j + 1, :], m_ref[j], v_ref[j])
                continue
            g = rel if name == "rel_bias" else small[:, lanes[0]:lanes[0] + lanes[1]]
            outs[4 * i][...] = g
            outs[4 * i + 1][...], outs[4 * i + 2][...], outs[4 * i + 3][...] = _adam_math(
                w_ref[...], g, m_ref[...], v_ref[...])

    vmem = pl.BlockSpec(memory_space=pltpu.VMEM)
    out_shape = [jax.ShapeDtypeStruct((1, 128), F32)]
    for w, _, _ in state:
        out_shape += [jax.ShapeDtypeStruct(w.shape, F32)] * 4
    outs = pl.pallas_call(
        body, name="small_update",
        in_specs=[pl.BlockSpec(memory_space=pltpu.SMEM), vmem, vmem] + [vmem] * len(flat)
        + [pl.BlockSpec(memory_space=pl.ANY)] * len(after),
        out_shape=out_shape,
    )(me, packed_all, rel_all, *flat, *after)
    return outs[0], [tuple(outs[1 + 4 * i:5 + 4 * i]) for i in range(n_p)]


def _adam_math(w, g, m, v):
    m = ADAM_B1 * m + (1.0 - ADAM_B1) * g
    v = ADAM_B2 * v + (1.0 - ADAM_B2) * (g * g)
    m_hat = m / (1.0 - ADAM_B1 ** ADAM_STEP)
    v_hat = v / (1.0 - ADAM_B2 ** ADAM_STEP)
    delta = -ADAM_LR * (m_hat / (jnp.sqrt(v_hat) + ADAM_EPS) + ADAM_WD * w)
    return delta, m, v


def _adamw_parts(w, m, v, local, land, me, tr, name):
    r, c = w.shape

    def body(me_ref, w_ref, m_ref, v_ref, own_ref, land_ref, g_ref, d_ref, mo_ref, vo_ref):
        g = own_ref[0].astype(F32)
        for k in range(N_DEV - 1):
            g = g + land_ref[k].astype(F32)
        g_ref[...] = g
        d_ref[...], mo_ref[...], vo_ref[...] = _adam_math(w_ref[...], g, m_ref[...], v_ref[...])

    tile = pl.BlockSpec((tr, c), lambda i, me_ref: (i, 0))
    return pl.pallas_call(
        body, name=name,
        grid_spec=pltpu.PrefetchScalarGridSpec(
            num_scalar_prefetch=1, grid=(r // tr,),
            in_specs=[tile, tile, tile, pl.BlockSpec((1, tr, c), lambda i, me_ref: (me_ref[0], i, 0)),
                      pl.BlockSpec((N_DEV - 1, tr, c), lambda i, me_ref: (0, i, 0))],
            out_specs=[tile] * 4),
        out_shape=[jax.ShapeDtypeStruct((r, c), F32)] * 4,
        compiler_params=_params(("arbitrary",)),
    )(me, w, m, v, local, land)


def _local_step(x, target, mod, w_in_t, bias, weights_out_gu, weights_down, g_norm1, sinks, conv_w, g_attn,
                g_conv, g_norm2, g_final, exchange, start_after):
    s = x.shape[0]
    tm = min(512, s)
    tm_small = min(256, s)
    bucket = _bucket_table()

    h, q, kv, gb, gc, xc = _in_proj(start_after, x, mod, g_norm1, w_in_t, tm)
    attn, merged, lse, probs = _mixer_fwd(q, kv, gb, gc, xc, bias, sinks, conv_w, g_attn, g_conv)
    w_out, w_gu_t = weights_out_gu(merged)
    o1, x1 = _out_proj(merged, x, mod, w_out, tm)
    w_down = weights_down(x1)
    h2, act, do2, dgu, dx1, dw_out, dmerged, sm_2 = _ffn(x1, o1, merged, mod, g_norm2, w_gu_t, w_down, w_out, g_final,
                                                         target, tm_small)
    ts = min(WEIGHT_GRAD_ROWS, s)
    tok_out = exchange("w_out", dw_out)
    tok_down = exchange("w_down", _weight_grad(act, do2, D_FF // 2, ts, "w_down_grad", after=tok_out))
    tok_gu = exchange("w_gu", _weight_grad(dgu, h2, D_FF // 2, ts, "w_gu_grad", after=tok_down))
    dproj, d_rel, dsink, sm_mix = _mixer_bwd(
        tok_gu, q, kv, gb, gc, xc, probs, sinks, conv_w, g_attn, g_conv, attn, lse, dmerged, bucket)
    tok_in = exchange("w_in", _weight_grad(dproj, h, IN_PROJ_WIDTH // 2, ts, "w_in_grad"))
    dx, sm_1 = _in_proj_bwd(tok_in, dproj, x, dx1, mod, g_norm1, w_in_t, min(1024, s))

    packed = jnp.concatenate([
        sm_1[0], sm_1[1], sm_2[7], sm_2[0], sm_2[1], sm_2[3],
        sm_1[2],
        sm_mix[5, 0:128],
        sm_mix[0], sm_mix[1],
        sm_2[2],
        sm_2[4],
        sm_mix[2], sm_mix[3], sm_mix[4],
        sm_2[6, 0:128],
    ])[None, :]
    return dx, packed, d_rel


def kernel(x, c, rel_bias, w_ada, b_ada, g_norm1, w_in, sinks, conv_w, g_attn_out, g_conv_out, w_out, g_norm2, w_gu, w_down, g_final, loss_target, m_rel_bias, m_w_ada, m_b_ada, m_g_norm1, m_w_in, m_sinks, m_conv_w, m_g_attn_out, m_g_conv_out, m_w_out, m_g_norm2, m_w_gu, m_w_down, m_g_final, v_rel_bias, v_w_ada, v_b_ada, v_g_norm1, v_w_in, v_sinks, v_conv_w, v_g_attn_out, v_g_conv_out, v_w_out, v_g_norm2, v_w_gu, v_w_down, v_g_final):
    me = _linear(_mesh_position())
    me_arr = jnp.reshape(me, (1,)).astype(jnp.int32)
    ada_cols = w_ada.shape[2]
    tm = min(512, x.shape[1])

    b_cols = b_ada.reshape(N_DEV, 1, ada_cols)
    cond_all, conv_w_all, mod, w_in_blocks, staged, bias = _open_step(
        c, conv_w.transpose(1, 0, 2), w_ada[0], b_cols, w_in[0].T, [w_out[0], w_gu[0].T, w_down[0]], rel_bias.T, _bucket_table())
    cond_all = cond_all[:, 0, :]
    conv_w_full = conv_w_all.reshape(N_DEV, 3, -1).transpose(1, 0, 2).reshape(3, CONV_WIDTH)
    w_in_t = w_in_blocks.reshape(IN_PROJ_WIDTH, D_MODEL)
    gather_sems, staged, gather_token = _gather_start(staged, "gather_start_weights")

    def weights_out_gu(after):
        got = _gather_pass_on(_gather_wait(gather_sems[0:4], staged[0:2], [after], "gather_wait_out_gu"),
                              "gather_pass_on_out_gu")
        return got[0].reshape(D_MODEL, D_MODEL), got[1].reshape(2 * D_FF, D_MODEL)

    def weights_down(after):
        got = _gather_pass_on(_gather_wait(gather_sems[4:6], staged[2:3], [after], "gather_wait_down"),
                              "gather_pass_on_down")
        return got[0].reshape(D_FF, D_MODEL)

    started = {}

    def exchange(name, dw):
        st = _exchange_start(dw.reshape(N_DEV, dw.shape[0] // N_DEV, dw.shape[1]), "exchange_start_" + name)
        started[name] = st
        return st[4]

    dx, packed, d_rel = _local_step(
        x[0], loss_target[0], mod, w_in_t, bias, weights_out_gu, weights_down, g_norm1, sinks[0], conv_w_full,
        g_attn_out, g_conv_out, g_norm2, g_final[None, :], exchange, gather_token)

    def zone(a):
        return lax.dynamic_update_slice(jnp.zeros((N_DEV,) + a.shape, F32), a[None], (me,) + (0,) * a.ndim)

    shared = _share_start([packed, d_rel], [zone(packed), zone(d_rel)], "share_small_start")

    def finish(name, after, w, m, v, tr):
        src, land = _exchange_wait(started[name], after, "exchange_wait_" + name)
        return _adamw_parts(w, m, v, src, land, me_arr, tr, "adamw_" + name)

    g_down, d_down, nm_down, nv_down = finish("w_down", [shared[2][0]], w_down[0], m_w_down[0], v_w_down[0], 176)
    g_gu, d_gu, nm_gu, nv_gu = finish("w_gu", [nv_down], w_gu[0].T, m_w_gu[0].T, v_w_gu[0].T, 352)
    g_out, d_out, nm_out, nv_out = finish("w_out", [nv_gu], w_out[0], m_w_out[0], v_w_out[0], 128)

    packed_all, rel_all = _share_wait(shared, [nv_out], "share_small_wait")
    g_ada, d_ada, nm_ada, nv_ada = _adamw_w_ada(me_arr, cond_all, packed_all, w_ada[0], m_w_ada[0], v_w_ada[0], 256)
    as_rows = {"conv_w": lambda a: a.transpose(1, 0, 2), "g_final": lambda a: a[None, :], "rel_bias": lambda a: a.T}
    small_state = {
        "rel_bias": (rel_bias, m_rel_bias, v_rel_bias), "b_ada": (b_ada, m_b_ada, v_b_ada),
        "g_norm1": (g_norm1, m_g_norm1, v_g_norm1), "sinks": (sinks, m_sinks, v_sinks),
        "conv_w": (conv_w, m_conv_w, v_conv_w), "g_attn_out": (g_attn_out, m_g_attn_out, v_g_attn_out),
        "g_conv_out": (g_conv_out, m_g_conv_out, v_g_conv_out), "g_norm2": (g_norm2, m_g_norm2, v_g_norm2),
        "g_final": (g_final, m_g_final, v_g_final),
    }
    state = [tuple(as_rows.get(name, lambda a: a)(a) for a in small_state[name]) for name, _ in SMALL_PARAMS]
    loss_row, small_out = _small_update(me_arr, packed_all, rel_all, state, [])
    loss = loss_row[0, 0]
    back = {"rel_bias": lambda a: a.T, "conv_w": lambda a: a.transpose(1, 0, 2)}
    small_res = {name: tuple(back[name](a) if name in back else a.reshape(small_state[name][0].shape) for a in res)
                 for (name, _), res in zip(SMALL_PARAMS, small_out)}

    g_in, d_in, nm_in, nv_in = finish("w_in", [loss_row, nv_ada], w_in[0].T, m_w_in[0].T, v_w_in[0].T, 144)

    big = {
        "w_ada": (g_ada[None], d_ada[None], nm_ada[None], nv_ada[None]),
        "w_in": (g_in.T[None], d_in.T[None], nm_in.T[None], nv_in.T[None]),
        "w_out": (g_out[None], d_out[None], nm_out[None], nv_out[None]),
        "w_gu": (g_gu.T[None], d_gu.T[None], nm_gu.T[None], nv_gu.T[None]),
        "w_down": (g_down[None], d_down[None], nm_down[None], nv_down[None]),
    }
    order = ["rel_bias", "w_ada", "b_ada", "g_norm1", "w_in", "sinks", "conv_w", "g_attn_out", "g_conv_out", "w_out",
             "g_norm2", "w_gu", "w_down", "g_final"]
    results = [big[k] if k in big else small_res[k] for k in order]
    return (loss, dx[None], *[r[0] for r in results], *[r[1] for r in results], *[r[2] for r in results],
            *[r[3] for r in results])
```

```python
import math

import jax
import jax.numpy as jnp
import numpy as np
from jax import lax
from jax.experimental import pallas as pl
from jax.experimental.pallas import tpu as pltpu

F32 = jnp.float32
BF16 = jnp.bfloat16

D_MODEL = 1024
HEAD_DIM = 64
N_Q_HEADS = 8
ATTN_WIDTH = 512
KV_WIDTH = 128
CONV_WIDTH = 512
IN_PROJ_WIDTH = 2304
D_FF = 2816
N_MOD = 6
N_BUCKETS = 32
MAX_DISTANCE = 128
BLOCK = 128
REL_LANES = 128
EPS = 1e-6
NEG_INF = -1e30
SCALE = HEAD_DIM ** -0.5
N_DEV = 8

ADAM_LR = 0.001
ADAM_B1 = 0.9
ADAM_B2 = 0.999
ADAM_EPS = 1e-08
ADAM_WD = 0.01
ADAM_STEP = 10

SH1, SC1, G1, SH2, SC2, G2 = range(6)

VMEM_LIMIT_LARGE = 60 * 1024 * 1024
WEIGHT_GRAD_ROWS = 2048
FFN_CHUNKS = 1
PREV_ROWS = 16
MIXER_BLOCKS = 4
MESH_ID = pl.DeviceIdType.MESH

OFF_DMOD = 0
OFF_GN1 = OFF_DMOD + N_MOD * D_MODEL
OFF_SINK = OFF_GN1 + D_MODEL
OFF_GATT = OFF_SINK + 128
OFF_GCV = OFF_GATT + ATTN_WIDTH
OFF_GN2 = OFF_GCV + CONV_WIDTH
OFF_GFIN = OFF_GN2 + D_MODEL
OFF_CONVW = OFF_GFIN + D_MODEL
OFF_LOSS = OFF_CONVW + 3 * CONV_WIDTH
PACKED = OFF_LOSS + 128


def _params(sem=None, vmem=None):
    return pltpu.CompilerParams(dimension_semantics=sem, vmem_limit_bytes=vmem)


def _coming_behind(body):
    def skipping(after_ref, *refs):
        body(*refs)

    return skipping


ANY_SPEC = pl.BlockSpec(memory_space=pl.ANY)


def _full(shape):
    nd = len(shape)
    return pl.BlockSpec(shape, lambda *_: (0,) * nd)


def _rows(tm, width):
    return pl.BlockSpec((tm, width), lambda i, *_: (i, 0))


def _sigmoid(x):
    return 1.0 / (1.0 + jnp.exp(-x))


def _rsqrt_mean_sq(x):
    return lax.rsqrt(jnp.mean(x * x, axis=-1, keepdims=True) + EPS)


def _colsum(x):
    return jnp.sum(x, axis=0, keepdims=True)


def _dot(a, b):
    return jnp.dot(a, b, preferred_element_type=F32)


def _dot_nt(a, b):
    return lax.dot_general(a, b, (((1,), (1,)), ((), ())), preferred_element_type=F32)


def _dot_tn(a, b):
    return lax.dot_general(a, b, (((0,), (0,)), ((), ())), preferred_element_type=F32)


def _mesh_position():
    return lax.axis_index("x"), lax.axis_index("y"), lax.axis_index("c")


def _linear(p):
    return 4 * p[0] + 2 * p[1] + p[2]


def _peer(k):
    x, y, c = _mesh_position()
    return (1 - x if k & 4 else x, 1 - y if k & 2 else y, 1 - c if k & 1 else c)


HBM_SPEC = pl.BlockSpec(memory_space=pltpu.HBM)
SEM_SPEC = pl.BlockSpec(memory_space=pltpu.SEMAPHORE)
DATAFLOW = pltpu.SideEffectType.DATAFLOW_SIDE_EFFECTING


def _exchange_start(src, name):
    r, c = src.shape[1:]

    def body(src_ref, land_ref, send_sems, recv_sems, src_thru, land_thru, token):
        for k in range(1, N_DEV):
            peer = _peer(k)
            pltpu.make_async_remote_copy(
                src_ref=src_ref.at[_linear(peer)], dst_ref=land_ref.at[k - 1],
                send_sem=send_sems.at[k - 1], recv_sem=recv_sems.at[k - 1],
                device_id=peer, device_id_type=MESH_ID).start()
        token[...] = jnp.zeros_like(token)

    land = lax.empty((N_DEV - 1, r, c), src.dtype)
    return pl.pallas_call(
        body, name=name,
        out_shape=(pltpu.SemaphoreType.DMA((N_DEV - 1,)), pltpu.SemaphoreType.DMA((N_DEV - 1,)),
                   pltpu.HBM(src.shape, src.dtype), pltpu.HBM(land.shape, land.dtype),
                   jax.ShapeDtypeStruct((8, 128), F32)),
        in_specs=(HBM_SPEC, HBM_SPEC),
        out_specs=(SEM_SPEC, SEM_SPEC, HBM_SPEC, HBM_SPEC, pl.BlockSpec(memory_space=pltpu.VMEM)),
        input_output_aliases={0: 2, 1: 3},
        compiler_params=pltpu.CompilerParams(has_side_effects=DATAFLOW),
    )(pltpu.with_memory_space_constraint(src, pltpu.HBM), pltpu.with_memory_space_constraint(land, pltpu.HBM))


def _exchange_wait(started, after, name):
    send_sems, recv_sems, src_thru, land_thru, _ = started

    def body(src_ref, land_ref, send_sems, recv_sems, *rest):
        for k in range(1, N_DEV):
            cp = pltpu.make_async_remote_copy(
                src_ref=src_ref.at[0], dst_ref=land_ref.at[k - 1],
                send_sem=send_sems.at[k - 1], recv_sem=recv_sems.at[k - 1],
                device_id=_peer(k), device_id_type=MESH_ID)
            cp.wait_send()
            cp.wait_recv()

    return pl.pallas_call(
        body, name=name,
        out_shape=(pltpu.HBM(src_thru.shape, src_thru.dtype), pltpu.HBM(land_thru.shape, land_thru.dtype)),
        in_specs=(HBM_SPEC, HBM_SPEC, SEM_SPEC, SEM_SPEC) + (pl.BlockSpec(memory_space=pl.ANY),) * len(after),
        out_specs=(HBM_SPEC, HBM_SPEC), input_output_aliases={0: 0, 1: 1},
        compiler_params=pltpu.CompilerParams(has_side_effects=DATAFLOW),
    )(src_thru, land_thru, send_sems, recv_sems, *after)


def _share_start(arrs, zones, name):
    n = len(arrs)

    def body(*refs):
        src_refs, zone_refs, sems = refs[:n], refs[n:2 * n], refs[2 * n:4 * n]
        me = _linear(_mesh_position())
        for a in range(n):
            for k in range(1, N_DEV):
                pltpu.make_async_remote_copy(
                    src_ref=src_refs[a], dst_ref=zone_refs[a].at[me],
                    send_sem=sems[2 * a].at[k - 1], recv_sem=sems[2 * a + 1].at[k - 1],
                    device_id=_peer(k), device_id_type=MESH_ID).start()

    outs = pl.pallas_call(
        body, name=name,
        out_shape=tuple(pltpu.SemaphoreType.DMA((N_DEV - 1,)) for _ in range(2 * n))
        + tuple(pltpu.HBM(a.shape, a.dtype) for a in arrs) + tuple(pltpu.HBM(z.shape, z.dtype) for z in zones),
        in_specs=(HBM_SPEC,) * (2 * n),
        out_specs=(SEM_SPEC,) * (2 * n) + (HBM_SPEC,) * (2 * n),
        input_output_aliases={i: 2 * n + i for i in range(2 * n)},
        compiler_params=pltpu.CompilerParams(has_side_effects=DATAFLOW),
    )(*[pltpu.with_memory_space_constraint(a, pltpu.HBM) for a in list(arrs) + list(zones)])
    return outs[:2 * n], outs[2 * n:3 * n], outs[3 * n:]


def _share_wait(started, after, name):
    sems, arrs, zones = started
    n = len(arrs)

    def body(*refs):
        src_refs, zone_refs, sem_refs = refs[:n], refs[n:2 * n], refs[2 * n:4 * n]
        for a in range(n):
            for k in range(1, N_DEV):
                cp = pltpu.make_async_remote_copy(
                    src_ref=src_refs[a], dst_ref=zone_refs[a].at[_linear(_peer(k))],
                    send_sem=sem_refs[2 * a].at[k - 1], recv_sem=sem_refs[2 * a + 1].at[k - 1],
                    device_id=_peer(k), device_id_type=MESH_ID)
                cp.wait_send()
                cp.wait_recv()

    outs = pl.pallas_call(
        body, name=name,
        out_shape=tuple(pltpu.HBM(a.shape, a.dtype) for a in arrs) + tuple(pltpu.HBM(z.shape, z.dtype) for z in zones),
        in_specs=(HBM_SPEC,) * (2 * n) + (SEM_SPEC,) * (2 * n) + (pl.BlockSpec(memory_space=pl.ANY),) * len(after),
        out_specs=(HBM_SPEC,) * (2 * n), input_output_aliases={i: i for i in range(2 * n)},
        compiler_params=pltpu.CompilerParams(has_side_effects=DATAFLOW),
    )(*arrs, *zones, *sems, *after)
    return list(outs[n:])


def _same_core_peers():
    x, y, c = _mesh_position()
    return [(x, y, 1 - c), (1 - x, y, c), (x, 1 - y, c), (1 - x, 1 - y, c)]


def _gather_start(bufs, name):
    n = len(bufs)

    def body(*refs):
        buf_refs, rest = refs[:n], refs[n:]
        sems, token = rest[:2 * n], rest[-1]
        me = _linear(_mesh_position())
        for a in range(n):
            for k, peer in enumerate(_same_core_peers()):
                pltpu.make_async_remote_copy(
                    src_ref=buf_refs[a].at[me], dst_ref=buf_refs[a].at[me],
                    send_sem=sems[2 * a].at[k], recv_sem=sems[2 * a + 1].at[k],
                    device_id=peer, device_id_type=MESH_ID).start()
        token[...] = jnp.zeros_like(token)

    outs = pl.pallas_call(
        body, name=name,
        out_shape=tuple(pltpu.SemaphoreType.DMA((4,)) for _ in range(2 * n))
        + tuple(pltpu.HBM(b.shape, b.dtype) for b in bufs) + (jax.ShapeDtypeStruct((8, 128), F32),),
        in_specs=(HBM_SPEC,) * n,
        out_specs=(SEM_SPEC,) * (2 * n) + (HBM_SPEC,) * n + (pl.BlockSpec(memory_space=pltpu.VMEM),),
        input_output_aliases={a: 2 * n + a for a in range(n)},
        compiler_params=pltpu.CompilerParams(has_side_effects=DATAFLOW),
    )(*[pltpu.with_memory_space_constraint(b, pltpu.HBM) for b in bufs])
    return outs[:2 * n], outs[2 * n:3 * n], outs[3 * n]


def _gather_wait(sems, bufs, after, name):
    n = len(bufs)

    def body(*refs):
        buf_refs, sem_refs = refs[:n], refs[n:3 * n]
        x, y, c = _mesh_position()
        me = _linear((x, y, c))
        for a in range(n):
            for k, peer in enumerate(_same_core_peers()):
                cp = pltpu.make_async_remote_copy(
                    src_ref=buf_refs[a].at[me], dst_ref=buf_refs[a].at[_linear(peer)],
                    send_sem=sem_refs[2 * a].at[k], recv_sem=sem_refs[2 * a + 1].at[k],
                    device_id=peer, device_id_type=MESH_ID)
                cp.wait_send()
                cp.wait_recv()

    return list(pl.pallas_call(
        body, name=name,
        out_shape=tuple(pltpu.HBM(b.shape, b.dtype) for b in bufs),
        in_specs=(HBM_SPEC,) * n + (SEM_SPEC,) * (2 * n) + (pl.BlockSpec(memory_space=pl.ANY),) * len(after),
        out_specs=(HBM_SPEC,) * n, input_output_aliases={a: a for a in range(n)},
        compiler_params=pltpu.CompilerParams(has_side_effects=DATAFLOW),
    )(*bufs, *sems, *after))


def _gather_pass_on(bufs, name):
    n = len(bufs)

    def body(*refs):
        out_refs = refs[n:2 * n]
        send_sems, recv_sems = refs[2 * n:]
        x, y, c = _mesh_position()
        sibling = (x, y, 1 - c)
        chips = [(1 - x, y), (x, 1 - y), (1 - x, 1 - y)]
        copies = []
        for a in range(n):
            for j, chip in enumerate(chips):
                block = out_refs[a].at[_linear((*chip, c))]
                copies.append(pltpu.make_async_remote_copy(
                    src_ref=block, dst_ref=block, send_sem=send_sems.at[3 * a + j], recv_sem=recv_sems.at[3 * a + j],
                    device_id=sibling, device_id_type=MESH_ID))
                copies[-1].start()
        for a in range(n):
            for j, chip in enumerate(chips):
                copies[3 * a + j].wait_send()
                theirs = out_refs[a].at[_linear((*chip, 1 - c))]
                pltpu.make_async_remote_copy(
                    src_ref=theirs, dst_ref=theirs, send_sem=send_sems.at[3 * a + j], recv_sem=recv_sems.at[3 * a + j],
                    device_id=sibling, device_id_type=MESH_ID).wait_recv()

    hbm = pl.BlockSpec(memory_space=pl.ANY)
    return list(pl.pallas_call(
        body, name=name,
        out_shape=[jax.ShapeDtypeStruct(b.shape, b.dtype) for b in bufs],
        in_specs=[hbm] * n, out_specs=[hbm] * n, input_output_aliases={a: a for a in range(n)},
        scratch_shapes=[pltpu.SemaphoreType.DMA((3 * n,)), pltpu.SemaphoreType.DMA((3 * n,))],
    )(*bufs))


def _open_step(c, conv_w, w_ada, b_cols, w_in_t, later, rel_bias, bucket):
    cols = w_ada.shape[1]
    n_later = len(later)

    def body(c_ref, cw_ref, wa_ref, b_ref, w_ref, *rest):
        later_refs, rb_ref, bk_ref = rest[:n_later], rest[n_later], rest[n_later + 1]
        cond_ref, conv_ref, mod_ref, win_ref = rest[n_later + 2:n_later + 6]
        staged_refs, bias_ref, rows_ref = rest[n_later + 6:2 * n_later + 6], rest[2 * n_later + 6], rest[2 * n_later + 7]
        cond_own, mod_own, stage = rest[2 * n_later + 8:2 * n_later + 11]
        later_stage = rest[2 * n_later + 11:3 * n_later + 11]
        s_send, s_recv, w_send, w_recv, local_sems = rest[3 * n_later + 11:]
        x, y, cc = _mesh_position()
        me = _linear((x, y, cc))
        sibling = (x, y, 1 - cc)
        chips = [(1 - x, y), (x, 1 - y), (1 - x, 1 - y)]
        v = c_ref[...]
        cond_own[...] = v * _sigmoid(v)
        stage[...] = w_ref[...].astype(BF16)

        def small(rnd, a, k, src, dst, slot):
            return pltpu.make_async_remote_copy(
                src_ref=src, dst_ref=dst.at[slot], send_sem=s_send.at[rnd, a, k - 1], recv_sem=s_recv.at[rnd, a, k - 1],
                device_id=_peer(k), device_id_type=MESH_ID)

        def block(p):
            return win_ref.at[_linear(p)]

        def big(k, blk, to, src=None):
            return pltpu.make_async_remote_copy(
                src_ref=block(blk) if src is None else src, dst_ref=block(blk),
                send_sem=w_send.at[k], recv_sem=w_recv.at[k], device_id=to, device_id_type=MESH_ID)

        mine = [pltpu.make_async_copy(cond_own, cond_ref.at[me], local_sems.at[0]),
                pltpu.make_async_copy(cw_ref, conv_ref.at[me], local_sems.at[1]),
                pltpu.make_async_copy(stage, block((x, y, cc)), local_sems.at[2])]
        for cp in mine:
            cp.start()
        sends = []
        for k in range(1, N_DEV):
            sends += [small(0, 0, k, cond_own, cond_ref, me), small(0, 1, k, cw_ref, conv_ref, me)]
        for cp in sends:
            cp.start()
        first = [big(0, (x, y, cc), sibling, src=stage)]
        first += [big(1 + j, (x, y, cc), (*chip, cc), src=stage) for j, chip in enumerate(chips)]
        for cp in first:
            cp.start()
        for a in range(n_later):
            later_stage[a][...] = later_refs[a][...].astype(BF16)
            mine.append(pltpu.make_async_copy(later_stage[a], staged_refs[a].at[me], local_sems.at[4 + a]))
            mine[-1].start()
        _fill_bias_table(rb_ref, bk_ref, bias_ref)
        for k in range(1, N_DEV):
            small(0, 0, k, cond_own, cond_ref, _linear(_peer(k))).wait_recv()
            small(0, 1, k, cw_ref, conv_ref, _linear(_peer(k))).wait_recv()
        mine[0].wait()
        cond_all = jnp.concatenate([cond_ref[k] for k in range(N_DEV)], axis=0)
        mod_own[...] = _dot(cond_all, wa_ref[...]) + b_ref[me]
        mine.append(pltpu.make_async_copy(mod_own, mod_ref.at[me], local_sems.at[3]))
        mine[-1].start()
        second = [small(1, 0, k, mod_own, mod_ref, me) for k in range(1, N_DEV)]
        for cp in second:
            cp.start()
        passed = []
        for j, chip in enumerate(chips):
            big(1 + j, (*chip, cc), (x, y, cc)).wait_recv()
            fwd = big(4 + j, (*chip, cc), sibling)
            fwd.start()
            passed.append(fwd)
        big(0, sibling, (x, y, cc)).wait_recv()
        for j, chip in enumerate(chips):
            big(4 + j, (*chip, 1 - cc), (x, y, cc)).wait_recv()
        for k in range(1, N_DEV):
            small(1, 0, k, mod_own, mod_ref, _linear(_peer(k))).wait_recv()
        for cp in sends + first + second + passed:
            cp.wait_send()
        for cp in mine[1:]:
            cp.wait()
        flat = jnp.concatenate([mod_ref[j, pl.ds(me, 1), :] for j in range(N_DEV)], axis=1)
        rows_ref[...] = jnp.concatenate([flat[:, D_MODEL * r:D_MODEL * (r + 1)] for r in range(N_MOD)]
                                        + [jnp.zeros((8 - N_MOD, D_MODEL), F32)], axis=0)

    vmem = pl.BlockSpec(memory_space=pltpu.VMEM)
    outs = pl.pallas_call(
        body, name="open_step",
        out_shape=[jax.ShapeDtypeStruct((N_DEV,) + c.shape, F32), jax.ShapeDtypeStruct((N_DEV,) + conv_w.shape, F32),
                   jax.ShapeDtypeStruct((N_DEV, N_DEV, cols), F32),
                   jax.ShapeDtypeStruct((N_DEV,) + w_in_t.shape, BF16)]
        + [jax.ShapeDtypeStruct((N_DEV,) + a.shape, BF16) for a in later]
        + [jax.ShapeDtypeStruct((N_Q_HEADS, BLOCK, 2 * BLOCK), F32), jax.ShapeDtypeStruct((8, D_MODEL), F32)],
        in_specs=[vmem] * (5 + n_later) + [pl.BlockSpec(memory_space=pltpu.SMEM), vmem],
        out_specs=[vmem, vmem, vmem, ANY_SPEC] + [ANY_SPEC] * n_later + [vmem, vmem],
        scratch_shapes=[pltpu.VMEM(c.shape, F32), pltpu.VMEM((N_DEV, cols), F32), pltpu.VMEM(w_in_t.shape, BF16)]
        + [pltpu.VMEM(a.shape, BF16) for a in later]
        + [pltpu.SemaphoreType.DMA((2, 2, N_DEV - 1)), pltpu.SemaphoreType.DMA((2, 2, N_DEV - 1)),
           pltpu.SemaphoreType.DMA((7,)), pltpu.SemaphoreType.DMA((7,)),
           pltpu.SemaphoreType.DMA((4 + n_later,))],
        compiler_params=_params(vmem=VMEM_LIMIT_LARGE),
    )(c, conv_w, w_ada, b_cols, w_in_t, *later, rel_bias, bucket)
    return outs[0], outs[1], outs[5 + n_later], outs[3], list(outs[4:4 + n_later]), outs[4 + n_later]


def _in_proj(after, x, mod, g_norm1, w_in, tm):
    s = x.shape[0]

    def body(x_ref, mod_ref, g_ref, w_ref, h_ref, q_ref, kv_ref, gb_ref, gc_ref, xc_ref):
        xf = x_ref[...]
        n = xf * _rsqrt_mean_sq(xf) * g_ref[...]
        h = (n * (1.0 + mod_ref[SC1:SC1 + 1, :]) + mod_ref[SH1:SH1 + 1, :]).astype(BF16)
        h_ref[...] = h
        p = _dot_nt(h, w_ref[...])
        q_ref[...] = p[:, 0:512].astype(BF16)
        kv_ref[...] = p[:, 512:768].astype(BF16)
        gb_ref[...] = p[:, 768:1280].astype(BF16)
        gc_ref[...] = p[:, 1280:1792].astype(BF16)
        xc_ref[...] = p[:, 1792:2304].astype(BF16)

    return pl.pallas_call(
        _coming_behind(body), name="in_proj", grid=(s // tm,),
        in_specs=[ANY_SPEC, _rows(tm, D_MODEL), _full((8, D_MODEL)), _full((1, D_MODEL)), _full((IN_PROJ_WIDTH, D_MODEL))],
        out_specs=[_rows(tm, D_MODEL), _rows(tm, 512), _rows(tm, 256), _rows(tm, 512), _rows(tm, 512), _rows(tm, 512)],
        out_shape=[jax.ShapeDtypeStruct((s, D_MODEL), BF16), jax.ShapeDtypeStruct((s, 512), BF16),
                   jax.ShapeDtypeStruct((s, 256), BF16), jax.ShapeDtypeStruct((s, 512), BF16),
                   jax.ShapeDtypeStruct((s, 512), BF16), jax.ShapeDtypeStruct((s, 512), BF16)],
        compiler_params=_params(("arbitrary",), VMEM_LIMIT_LARGE),
    )(after, x, mod, g_norm1, w_in)


def _t5_bucket(dist):
    max_exact = N_BUCKETS // 2
    is_small = dist < max_exact
    d = np.maximum(dist, 1).astype(np.float32)
    large = max_exact + (np.log(d / max_exact) / math.log(MAX_DISTANCE / max_exact)
                         * (N_BUCKETS - max_exact)).astype(np.int32)
    large = np.minimum(large, N_BUCKETS - 1)
    return np.where(is_small, dist, large).astype(np.int32)


def _bucket_table():
    qi = np.arange(BLOCK, dtype=np.int32)[:, None]
    sj = np.arange(2 * BLOCK, dtype=np.int32)[None, :]
    return jnp.asarray(_t5_bucket(np.maximum(qi + BLOCK - sj, 0)))


def _window_mask():
    qi = lax.broadcasted_iota(jnp.int32, (BLOCK, 2 * BLOCK), 0)
    sj = lax.broadcasted_iota(jnp.int32, (BLOCK, 2 * BLOCK), 1)
    dist = qi + BLOCK - sj
    return (dist >= 0) & (dist < BLOCK)


def _fill_bias_table(rb_ref, bk_ref, o_ref):
    bk = bk_ref[...]
    inside = _window_mask()
    for h in range(N_Q_HEADS):
        acc = jnp.zeros((BLOCK, 2 * BLOCK), F32)
        for b in range(N_BUCKETS):
            acc = jnp.where(bk == b, rb_ref[h, b], acc)
        o_ref[h] = jnp.where(inside, acc, NEG_INF)


def _load_kv_window(kv_ref, n):
    prev = jnp.maximum(n - 1, 0)
    kvw = jnp.concatenate([kv_ref[pl.ds(pl.multiple_of(prev * BLOCK, BLOCK), BLOCK), :],
                           kv_ref[pl.ds(pl.multiple_of(n * BLOCK, BLOCK), BLOCK), :]], axis=0)
    k, v = kvw[:, 0:128], kvw[:, 128:256]
    k_sw = pltpu.roll(k.astype(F32), 64, 1).astype(BF16)
    v_sw = pltpu.roll(v.astype(F32), 64, 1).astype(BF16)
    return (k, k_sw), (v, v_sw)


def _conv_taps(gc, xc, gc_prev, xc_prev, n):
    u = gc * xc
    before = jnp.where(n > 0, gc_prev.astype(F32) * xc_prev.astype(F32), 0.0)
    last = before.shape[0] - 1
    row = lax.broadcasted_iota(jnp.int32, u.shape, 0)
    u1 = jnp.where(row == 0, before[last:last + 1, :], pltpu.roll(u, 1, 0))
    u2 = jnp.where(row == 0, before[last - 1:last, :],
                   jnp.where(row == 1, before[last:last + 1, :], pltpu.roll(u, 2, 0)))
    return u, u1, u2


def _mixer_fwd(q, kv, gb, gc, xc, bias, sinks, conv_w, g_attn, g_conv):
    s = q.shape[0]
    nb = s // BLOCK

    per_step = min(MIXER_BLOCKS, nb)
    tile = per_step * BLOCK

    def one_block(n, slot, before, sink_ref, q_ref, kv_ref, gb_ref, gc_ref, xc_ref, bias_ref, cw_ref, ga_ref,
                  gcv_ref, attn_ref, merged_ref, lse_ref, p_ref):
        rows = slice(slot * BLOCK, (slot + 1) * BLOCK)
        ks, vs = _load_kv_window(kv_ref, n)
        lane = lax.broadcasted_iota(jnp.int32, (BLOCK, BLOCK), 1)
        low = lane < HEAD_DIM
        col = lax.broadcasted_iota(jnp.int32, (BLOCK, 2 * BLOCK), 1)
        no_prev = (col < BLOCK) & (n == 0)
        lse_all = jnp.zeros((BLOCK, BLOCK), F32)
        pairs = []
        for p in range(4):
            qp = q_ref[rows, 128 * p:128 * (p + 1)].astype(F32)
            kvh = p // 2
            res = []
            for e in range(2):
                h = 2 * p + e
                qm = jnp.where(low if e == 0 else ~low, qp, 0.0).astype(BF16)
                sw = 0 if kvh == e else 1
                sc = _dot_nt(qm, ks[sw]) * SCALE + bias_ref[h]
                sc = jnp.where(no_prev, NEG_INF, sc)
                sink = sink_ref[h]
                m = jnp.maximum(jnp.max(sc, axis=-1, keepdims=True), sink)
                pe = jnp.exp(sc - m)
                den = jnp.sum(pe, axis=-1, keepdims=True) + jnp.exp(sink - m)
                pb = (pe * (1.0 / den)).astype(BF16)
                p_ref[slot, h] = pb
                res.append(_dot(pb, vs[sw]))
                lse_all = lse_all + jnp.where(lane == h, m + jnp.log(den), 0.0)
            pairs.append(jnp.where(low, res[0], res[1]))
        attn = jnp.concatenate(pairs, axis=1)
        attn_ref[rows, :] = attn
        lse_ref[rows, :] = lse_all
        u, u1, u2 = _conv_taps(gc_ref[rows, :].astype(F32), xc_ref[rows, :].astype(F32), before[0], before[1], n)
        cw = cw_ref[...]
        cv = gb_ref[rows, :].astype(F32) * (cw[0:1, :] * u2 + cw[1:2, :] * u1 + cw[2:3, :] * u)
        an = attn * _rsqrt_mean_sq(attn) * ga_ref[...]
        cn = cv * _rsqrt_mean_sq(cv) * gcv_ref[...]
        merged_ref[rows, :] = jnp.concatenate([an, cn], axis=1).astype(BF16)

    def body(sink_ref, q_ref, kv_ref, gb_ref, gc_ref, xc_ref, gcp_ref, xcp_ref, *rest):
        step = pl.program_id(0)
        for sub in range(per_step):
            ahead = slice(sub * BLOCK - PREV_ROWS, sub * BLOCK)
            before = (gcp_ref[...], xcp_ref[...]) if sub == 0 else (gc_ref[ahead, :], xc_ref[ahead, :])
            one_block(step * per_step + sub, sub, before, sink_ref, q_ref, kv_ref, gb_ref, gc_ref, xc_ref, *rest)

    blk = lambda w: pl.BlockSpec((tile, w), lambda n: (n, 0))
    prev8 = pl.BlockSpec((PREV_ROWS, 512), lambda n: (jnp.maximum(n * (tile // PREV_ROWS) - 1, 0), 0))
    return pl.pallas_call(
        body, name="mixer_fwd", grid=(nb // per_step,),
        in_specs=[pl.BlockSpec(memory_space=pltpu.SMEM), blk(512), _full((s, 256)), blk(512), blk(512), blk(512),
                  prev8, prev8, _full((N_Q_HEADS, BLOCK, 2 * BLOCK)), _full((3, 512)), _full((1, 512)),
                  _full((1, 512))],
        out_specs=[blk(512), blk(1024), blk(128),
                   pl.BlockSpec((per_step, N_Q_HEADS, BLOCK, 2 * BLOCK), lambda n: (n, 0, 0, 0))],
        out_shape=[jax.ShapeDtypeStruct((s, 512), F32), jax.ShapeDtypeStruct((s, 1024), BF16),
                   jax.ShapeDtypeStruct((s, 128), F32),
                   jax.ShapeDtypeStruct((nb, N_Q_HEADS, BLOCK, 2 * BLOCK), BF16)],
        compiler_params=_params(("arbitrary",)),
    )(sinks, q, kv, gb, gc, xc, gc, xc, bias, conv_w, g_attn, g_conv)


def _out_proj(merged, x, mod, w_out, tm):
    s = x.shape[0]

    def body(m_ref, x_ref, mod_ref, w_ref, o_ref, x1_ref):
        o = _dot(m_ref[...], w_ref[...])
        o_ref[...] = o.astype(BF16)
        x1_ref[...] = x_ref[...] + mod_ref[G1:G1 + 1, :] * o

    return pl.pallas_call(
        body, name="out_proj", grid=(s // tm,),
        in_specs=[_rows(tm, D_MODEL), _rows(tm, D_MODEL), _full((8, D_MODEL)), _full((D_MODEL, D_MODEL))],
        out_specs=[_rows(tm, D_MODEL), _rows(tm, D_MODEL)],
        out_shape=[jax.ShapeDtypeStruct((s, D_MODEL), BF16), jax.ShapeDtypeStruct((s, D_MODEL), F32)],
        compiler_params=_params(("arbitrary",)),
    )(merged, x, mod, w_out)


def _resident(shape):
    nd = len(shape)
    return pl.BlockSpec(shape, lambda *_: (0,) * nd, pipeline_mode=pl.Buffered(1))


def _ffn(x1, o1, merged, mod, g_norm2, w_gu, w_down, w_out, g_final, target, tm):
    s = x1.shape[0]
    chunk = D_FF // FFN_CHUNKS

    def body(x_ref, o1_ref, mg_ref, mod_ref, g_ref, wgu_ref, wd_ref, wo_ref, gf_ref, t_ref,
             h_ref, act_ref, do_ref, dgu_ref, dx1_ref, dwo_ref, dm_ref, small_ref, dwo_acc):
        @pl.when(pl.program_id(0) == 0)
        def _():
            small_ref[...] = jnp.zeros_like(small_ref)
            dwo_acc[...] = jnp.zeros_like(dwo_acc)

        xf = x_ref[...]
        n = xf * _rsqrt_mean_sq(xf) * g_ref[...]
        h = (n * (1.0 + mod_ref[SC2:SC2 + 1, :]) + mod_ref[SH2:SH2 + 1, :]).astype(BF16)
        h_ref[...] = h
        gates, ups, o = [], [], None
        for j in range(FFN_CHUNKS):
            lo = j * chunk
            gate = _dot_nt(h, wgu_ref[lo:lo + chunk, :])
            up = _dot_nt(h, wgu_ref[D_FF + lo:D_FF + lo + chunk, :])
            sg = _sigmoid(gate)
            act = (gate * sg * up).astype(BF16)
            act_ref[:, lo:lo + chunk] = act
            gates.append((up * (sg * (1.0 + gate * (1.0 - sg)))).astype(BF16))
            ups.append((gate * sg).astype(BF16))
            part = _dot(act, wd_ref[lo:lo + chunk, :])
            o = part if o is None else o + part
        g2 = mod_ref[G2:G2 + 1, :]
        x2 = xf + g2 * o
        r = _rsqrt_mean_sq(x2)
        xn = x2 * r
        gf = gf_ref[...]
        err = xn * gf - t_ref[...]
        dy = err * (1.0 / D_MODEL)
        dxn = dy * gf
        dx2 = r * (dxn - xn * jnp.mean(dxn * xn, axis=-1, keepdims=True))
        small_ref[4:5, :] += _colsum(dy * xn)
        small_ref[5:6, :] += _colsum(err * err)
        small_ref[3:4, :] += _colsum(dx2 * o)
        do = (dx2 * g2).astype(BF16)
        do_ref[...] = do
        dh = None
        for j in range(FFN_CHUNKS):
            lo = j * chunk
            dact = _dot_nt(do, wd_ref[lo:lo + chunk, :])
            dgate = (dact * gates[j].astype(F32)).astype(BF16)
            dup = (dact * ups[j].astype(F32)).astype(BF16)
            dgu_ref[:, lo:lo + chunk] = dgate
            dgu_ref[:, D_FF + lo:D_FF + lo + chunk] = dup
            part = _dot(dgate, wgu_ref[lo:lo + chunk, :]) + _dot(dup, wgu_ref[D_FF + lo:D_FF + lo + chunk, :])
            dh = part if dh is None else dh + part
        dx1 = dx2 + _norm_mod_bwd(dh, xf, g_ref[...], mod_ref[SC2:SC2 + 1, :], small_ref)
        dx1_ref[...] = dx1.astype(BF16)
        small_ref[7:8, :] += _colsum(dx1 * o1_ref[...].astype(F32))
        do1 = (dx1 * mod_ref[G1:G1 + 1, :]).astype(BF16)
        dm_ref[...] = _dot_nt(do1, wo_ref[...]).astype(BF16)
        dwo = dwo_acc[...] + _dot_tn(mg_ref[...], do1)
        dwo_acc[...] = dwo
        dwo_ref[...] = dwo.astype(BF16)

        @pl.when(pl.program_id(0) == pl.num_programs(0) - 1)
        def _():
            total = jnp.sum(small_ref[5:6, :], axis=-1, keepdims=True) * (0.5 / D_MODEL)
            small_ref[6:7, :] = jnp.broadcast_to(total, (1, D_MODEL))

    narrow = jax.ShapeDtypeStruct((s, D_MODEL), BF16)
    return pl.pallas_call(
        body, name="ffn", grid=(s // tm,),
        in_specs=[_rows(tm, D_MODEL), _rows(tm, D_MODEL), _rows(tm, D_MODEL), _full((8, D_MODEL)), _full((1, D_MODEL)),
                  _resident((2 * D_FF, D_MODEL)), _resident((D_FF, D_MODEL)), _resident((D_MODEL, D_MODEL)),
                  _full((1, D_MODEL)), _rows(tm, D_MODEL)],
        out_specs=[_rows(tm, D_MODEL), _rows(tm, D_FF), _rows(tm, D_MODEL), _rows(tm, 2 * D_FF), _rows(tm, D_MODEL),
                   _full((D_MODEL, D_MODEL)), _rows(tm, D_MODEL), _full((8, D_MODEL))],
        out_shape=[narrow, jax.ShapeDtypeStruct((s, D_FF), BF16), narrow, jax.ShapeDtypeStruct((s, 2 * D_FF), BF16),
                   narrow, jax.ShapeDtypeStruct((D_MODEL, D_MODEL), BF16), narrow,
                   jax.ShapeDtypeStruct((8, D_MODEL), F32)],
        scratch_shapes=[pltpu.VMEM((D_MODEL, D_MODEL), F32)],
        compiler_params=_params(("arbitrary",), VMEM_LIMIT_LARGE),
    )(x1, o1, merged, mod, g_norm2, w_gu, w_down, w_out, g_final, target)


def _norm_mod_bwd(dh, xf, g, scale_row, small_ref):
    r = _rsqrt_mean_sq(xf)
    xn = xf * r
    small_ref[0:1, :] += _colsum(dh)
    small_ref[1:2, :] += _colsum(dh * (xn * g))
    dn = dh * (1.0 + scale_row)
    small_ref[2:3, :] += _colsum(dn * xn)
    dxn = dn * g
    return r * (dxn - xn * jnp.mean(dxn * xn, axis=-1, keepdims=True))


def _group_norm_bwd(dm, a, g):
    r = _rsqrt_mean_sq(a)
    an = a * r
    dan = dm * g
    return r * (dan - an * jnp.mean(dan * an, axis=-1, keepdims=True)), _colsum(dm * an)


def _sum_by_bucket(db_ref, bk_ref, o_ref, rows_ref):
    bk = bk_ref[...]
    for b in range(N_BUCKETS):
        sel = (bk == b).astype(F32)
        for h in range(N_Q_HEADS):
            rows_ref[N_BUCKETS * h + b:N_BUCKETS * h + b + 1, :] = _colsum(db_ref[h] * sel)
    head = lax.broadcasted_iota(jnp.int32, (N_BUCKETS, REL_LANES), 1)
    out = jnp.zeros((N_BUCKETS, REL_LANES), F32)
    for h in range(N_Q_HEADS):
        per_bucket = jnp.sum(rows_ref[N_BUCKETS * h:N_BUCKETS * (h + 1), :], axis=-1, keepdims=True)
        out = out + jnp.where(head == h, per_bucket, 0.0)
    o_ref[...] = out


def _mixer_bwd(after, q, kv, gb, gc, xc, probs, sinks, conv_w, g_attn, g_conv, attn, lse, dmerged, bucket):
    s = q.shape[0]
    nb = s // BLOCK

    per_step = min(MIXER_BLOCKS, nb)
    tile = per_step * BLOCK
    steps = nb // per_step

    def one_block(n, slot, before, nxt, sink_ref, q_ref, kv_ref, gb_ref, gc_ref, xc_ref, p_ref, cw_ref, ga_ref,
                  gcv_ref, attn_ref, lse_ref, dm_ref, dproj_ref, dbias_ref, dsink_ref, small_ref):
        rows = slice(slot * BLOCK, (slot + 1) * BLOCK)
        next_dy, next_dkv = nxt
        dm = dm_ref[rows, :].astype(F32)
        gbv, gcv_, xcv = gb_ref[rows, :].astype(F32), gc_ref[rows, :].astype(F32), xc_ref[rows, :].astype(F32)
        u, u1, u2 = _conv_taps(gcv_, xcv, before[0], before[1], n)
        cw = cw_ref[...]
        yv = cw[0:1, :] * u2 + cw[1:2, :] * u1 + cw[2:3, :] * u
        dcv, dg_conv = _group_norm_bwd(dm[:, 512:1024], gbv * yv, gcv_ref[...])
        small_ref[1:2, :] += dg_conv
        dproj_ref[rows, 768:1280] = (dcv * yv).astype(BF16)
        dy = dcv * gbv
        row = lax.broadcasted_iota(jnp.int32, dy.shape, 0)
        d1 = jnp.where(row == BLOCK - 1, next_dy[0:1, :], pltpu.roll(dy, BLOCK - 1, 0))
        d2 = jnp.where(row == BLOCK - 2, next_dy[0:1, :],
                       jnp.where(row == BLOCK - 1, next_dy[1:2, :], pltpu.roll(dy, BLOCK - 2, 0)))
        du = cw[2:3, :] * dy + cw[1:2, :] * d1 + cw[0:1, :] * d2
        dproj_ref[rows, 1280:1792] = (du * xcv).astype(BF16)
        dproj_ref[rows, 1792:2304] = (du * gcv_).astype(BF16)
        small_ref[2:3, :] += _colsum(dy * u2)
        small_ref[3:4, :] += _colsum(dy * u1)
        small_ref[4:5, :] += _colsum(dy * u)

        attn_v = attn_ref[rows, :]
        dout, dg_attn = _group_norm_bwd(dm[:, 0:512], attn_v, ga_ref[...])
        small_ref[0:1, :] += dg_attn
        ks, vs = _load_kv_window(kv_ref, n)
        lane = lax.broadcasted_iota(jnp.int32, (BLOCK, BLOCK), 1)
        low = lane < HEAD_DIM
        lse_all = lse_ref[rows, :]
        dsink = jnp.zeros((BLOCK, BLOCK), F32)
        dq_pairs = []
        dk_groups, dv_groups = [], []
        for kvh in range(2):
            ds_rows, pr_rows, q_rows, do_rows = [], [], [], []
            for p in (2 * kvh, 2 * kvh + 1):
                qp = q_ref[rows, 128 * p:128 * (p + 1)].astype(F32)
                do_p = dout[:, 128 * p:128 * (p + 1)]
                prod = do_p * attn_v[:, 128 * p:128 * (p + 1)]
                res = []
                for e in range(2):
                    h = 2 * p + e
                    half = low if e == 0 else ~low
                    qm = jnp.where(half, qp, 0.0).astype(BF16)
                    dom = jnp.where(half, do_p, 0.0).astype(BF16)
                    delta = jnp.sum(jnp.where(half, prod, 0.0), axis=-1, keepdims=True)
                    lse_h = jnp.sum(jnp.where(lane == h, lse_all, 0.0), axis=-1, keepdims=True)
                    sw = 0 if kvh == e else 1
                    pb = p_ref[slot, h]
                    dp = _dot_nt(dom, vs[sw])
                    ds = pb.astype(F32) * (dp - delta)
                    dbias_ref[h] += ds
                    dsink = dsink + jnp.where(lane == h, -jnp.exp(sink_ref[h] - lse_h) * delta, 0.0)
                    dsb = ds.astype(BF16)
                    res.append(_dot(dsb, ks[sw]) * SCALE)
                    ds_rows.append(dsb)
                    pr_rows.append(pb)
                    q_rows.append(qm)
                    do_rows.append(dom)
                dq_pairs.append(jnp.where(low, res[0], res[1]))
            dk_g = _dot_tn(jnp.concatenate(ds_rows, axis=0), jnp.concatenate(q_rows, axis=0)) * SCALE
            dv_g = _dot_tn(jnp.concatenate(pr_rows, axis=0), jnp.concatenate(do_rows, axis=0))
            dk_groups.append(dk_g + pltpu.roll(dk_g, 64, 1))
            dv_groups.append(dv_g + pltpu.roll(dv_g, 64, 1))
        dproj_ref[rows, 0:512] = jnp.concatenate(dq_pairs, axis=1).astype(BF16)
        dsink_ref[...] += dsink
        low_kv = lax.broadcasted_iota(jnp.int32, (2 * BLOCK, BLOCK), 1) < HEAD_DIM
        dkv_win = jnp.concatenate([jnp.where(low_kv, dk_groups[0], dk_groups[1]),
                                   jnp.where(low_kv, dv_groups[0], dv_groups[1])], axis=1)
        dproj_ref[rows, 512:768] = (dkv_win[BLOCK:2 * BLOCK, :] + next_dkv).astype(BF16)
        return dy[0:8, :], dkv_win[0:BLOCK, :]

    def body(sink_ref, q_ref, kv_ref, gb_ref, gc_ref, xc_ref, gcp_ref, xcp_ref, p_ref, cw_ref, ga_ref, gcv_ref,
             attn_ref, lse_ref, dm_ref, bk_ref, dproj_ref, drel_ref, dsink_ref, small_ref,
             dy_ref, dkv_ref, dbias_ref, rows_ref):
        refs = (p_ref, cw_ref, ga_ref, gcv_ref, attn_ref, lse_ref, dm_ref, dproj_ref, dbias_ref, dsink_ref, small_ref)
        step = pl.program_id(0)

        @pl.when(step == 0)
        def _():
            dbias_ref[...] = jnp.zeros_like(dbias_ref)
            dsink_ref[...] = jnp.zeros_like(dsink_ref)
            small_ref[...] = jnp.zeros_like(small_ref)
            dy_ref[...] = jnp.zeros_like(dy_ref)
            dkv_ref[...] = jnp.zeros_like(dkv_ref)

        nxt = (dy_ref[...], dkv_ref[...])
        for sub in reversed(range(per_step)):
            ahead = slice(sub * BLOCK - PREV_ROWS, sub * BLOCK)
            before = (gcp_ref[...], xcp_ref[...]) if sub == 0 else (gc_ref[ahead, :], xc_ref[ahead, :])
            nxt = one_block((steps - 1 - step) * per_step + sub, sub, before, nxt,
                            sink_ref, q_ref, kv_ref, gb_ref, gc_ref, xc_ref, *refs)
        dy_ref[...], dkv_ref[...] = nxt

        @pl.when(step == steps - 1)
        def _():
            small_ref[5:6, :] = jnp.concatenate([_colsum(dsink_ref[...]), jnp.zeros((1, 512 - BLOCK), F32)], axis=1)
            _sum_by_bucket(dbias_ref, bk_ref, drel_ref, rows_ref)

    blk = lambda w: pl.BlockSpec((tile, w), lambda t: (steps - 1 - t, 0))
    prev8 = pl.BlockSpec((PREV_ROWS, 512),
                         lambda t: (jnp.maximum((steps - 1 - t) * (tile // PREV_ROWS) - 1, 0), 0))
    bf = lambda w: jax.ShapeDtypeStruct((s, w), BF16)
    return pl.pallas_call(
        _coming_behind(body), name="mixer_bwd", grid=(steps,),
        in_specs=[ANY_SPEC, pl.BlockSpec(memory_space=pltpu.SMEM), blk(512), _full((s, 256)), blk(512), blk(512), blk(512),
                  prev8, prev8,
                  pl.BlockSpec((per_step, N_Q_HEADS, BLOCK, 2 * BLOCK), lambda t: (steps - 1 - t, 0, 0, 0)),
                  _full((3, 512)), _full((1, 512)), _full((1, 512)), blk(512), blk(128), blk(1024),
                  _full((BLOCK, 2 * BLOCK))],
        out_specs=[blk(IN_PROJ_WIDTH), _full((N_BUCKETS, REL_LANES)), _full((BLOCK, BLOCK)), _full((8, 512))],
        out_shape=[bf(IN_PROJ_WIDTH), jax.ShapeDtypeStruct((N_BUCKETS, REL_LANES), F32),
                   jax.ShapeDtypeStruct((BLOCK, BLOCK), F32), jax.ShapeDtypeStruct((8, 512), F32)],
        scratch_shapes=[pltpu.VMEM((8, 512), F32), pltpu.VMEM((BLOCK, 2 * KV_WIDTH), F32),
                        pltpu.VMEM((N_Q_HEADS, BLOCK, 2 * BLOCK), F32),
                        pltpu.VMEM((N_BUCKETS * N_Q_HEADS, 2 * BLOCK), F32)],
        compiler_params=_params(("arbitrary",), VMEM_LIMIT_LARGE),
    )(after, sinks, q, kv, gb, gc, xc, gc, xc, probs, conv_w, g_attn, g_conv, attn, lse, dmerged, bucket)


def _in_proj_bwd(after, dproj, x, dx1, mod, g_norm1, w_in, tm):
    s = x.shape[0]

    def body(dproj_ref, x_ref, dx1_ref, mod_ref, g_ref, w_ref, dx_ref, small_ref):
        @pl.when(pl.program_id(0) == 0)
        def _():
            small_ref[...] = jnp.zeros_like(small_ref)

        dh = _dot(dproj_ref[...], w_ref[...])
        dx_ref[...] = dx1_ref[...].astype(F32) + _norm_mod_bwd(dh, x_ref[...], g_ref[...], mod_ref[SC1:SC1 + 1, :],
                                                               small_ref)

    return pl.pallas_call(
        _coming_behind(body), name="in_proj_bwd", grid=(s // tm,),
        in_specs=[ANY_SPEC, _rows(tm, IN_PROJ_WIDTH), _rows(tm, D_MODEL), _rows(tm, D_MODEL), _full((8, D_MODEL)),
                  _full((1, D_MODEL)), _full((IN_PROJ_WIDTH, D_MODEL))],
        out_specs=[_rows(tm, D_MODEL), _full((8, D_MODEL))],
        out_shape=[jax.ShapeDtypeStruct((s, D_MODEL), F32), jax.ShapeDtypeStruct((8, D_MODEL), F32)],
        compiler_params=_params(("arbitrary",), VMEM_LIMIT_LARGE),
    )(after, dproj, x, dx1, mod, g_norm1, w_in)


def _weight_grad(a, b, tk, ts, name, after=None):
    s, k = a.shape
    n = b.shape[1]
    nt = s // ts
    extra = [] if after is None else [after]

    def body(a_ref, b_ref, *rest):
        o_ref, acc_ref = rest[-2:]
        t = pl.program_id(1)
        @pl.when(t == 0)
        def _():
            acc_ref[...] = jnp.zeros_like(acc_ref)

        acc = acc_ref[...] + _dot_tn(a_ref[...], b_ref[...])
        acc_ref[...] = acc
        o_ref[...] = acc.astype(BF16)

    return pl.pallas_call(
        body, name=name, grid=(k // tk, nt),
        in_specs=[pl.BlockSpec((ts, tk), lambda i, t: (t, i)), pl.BlockSpec((ts, n), lambda i, t: (t, 0))]
        + [ANY_SPEC] * len(extra),
        out_specs=pl.BlockSpec((tk, n), lambda i, t: (i, 0)),
        out_shape=jax.ShapeDtypeStruct((k, n), BF16),
        scratch_shapes=[pltpu.VMEM((tk, n), F32)],
        compiler_params=_params(("arbitrary", "arbitrary"), VMEM_LIMIT_LARGE),
    )(a, b, *extra)


def _lanes_from(x, start, width):
    n = x.shape[1]
    return pltpu.roll(x, (n - start) % n, 1)[:, 0:width]


def _adamw_w_ada(me, cond_all, packed_all, w, m, v, tr):
    r, cols = w.shape

    def body(me_ref, c_ref, p_ref, w_ref, m_ref, v_ref, g_ref, d_ref, mo_ref, vo_ref):
        dmod = jnp.concatenate([p_ref[k][:, OFF_DMOD:OFF_DMOD + N_MOD * D_MODEL] for k in range(N_DEV)], axis=0)
        mine = _lanes_from(dmod, me_ref[0] * cols, cols)
        pad = lambda a: jnp.concatenate([a, jnp.zeros((128 - N_DEV, a.shape[1]), F32)], axis=0)
        cond = jnp.concatenate([c_ref[k] for k in range(N_DEV)], axis=0)
        g = _dot_tn(pad(cond), pad(mine))
        g_ref[...] = g
        d_ref[...], mo_ref[...], vo_ref[...] = _adam_math(w_ref[...], g, m_ref[...], v_ref[...])

    tile = pl.BlockSpec((tr, cols), lambda i, me_ref: (i, 0))
    return pl.pallas_call(
        body, name="adamw_w_ada",
        grid_spec=pltpu.PrefetchScalarGridSpec(
            num_scalar_prefetch=1, grid=(r // tr,),
            in_specs=[pl.BlockSpec((N_DEV, 1, tr), lambda i, me_ref: (0, 0, i)),
                      pl.BlockSpec(packed_all.shape, lambda i, me_ref: (0, 0, 0)), tile, tile, tile],
            out_specs=[tile] * 4),
        out_shape=[jax.ShapeDtypeStruct((r, cols), F32)] * 4,
        compiler_params=_params(("arbitrary",)),
    )(me, cond_all, packed_all, w, m, v)


SMALL_PARAMS = (("rel_bias", None), ("b_ada", (OFF_DMOD, N_MOD * D_MODEL)), ("g_norm1", (OFF_GN1, D_MODEL)),
                ("sinks", (OFF_SINK, N_Q_HEADS)), ("conv_w", None), ("g_attn_out", (OFF_GATT, ATTN_WIDTH)),
                ("g_conv_out", (OFF_GCV, CONV_WIDTH)), ("g_norm2", (OFF_GN2, D_MODEL)),
                ("g_final", (OFF_GFIN, D_MODEL)))


def _small_update(me, packed_all, rel_all, state, after):
    n_p = len(SMALL_PARAMS)
    flat = [a for triple in state for a in triple]
    conv_cols = state[4][0].shape[-1]

    def body(me_ref, p_ref, r_ref, *refs):
        ins = refs[:3 * n_p]
        loss_ref, outs = refs[3 * n_p + len(after)], refs[3 * n_p + len(after) + 1:]
        small, rel = p_ref[0], r_ref[0]
        for k in range(1, N_DEV):
            small = small + p_ref[k]
            rel = rel + r_ref[k]
        rel = jnp.concatenate([rel, jnp.zeros((REL_LANES - N_BUCKETS, REL_LANES), F32)], axis=0).T
        rel = rel[0:N_Q_HEADS, 0:N_BUCKETS]
        loss_ref[...] = small[:, OFF_LOSS:OFF_LOSS + 128]
        taps = jnp.concatenate([small[:, OFF_CONVW + CONV_WIDTH * j:OFF_CONVW + CONV_WIDTH * (j + 1)]
                                for j in range(3)] + [jnp.zeros((5, CONV_WIDTH), F32)], axis=0)
        conv_g = _lanes_from(taps, me_ref[0] * conv_cols, conv_cols)[0:3, :]
        for i, (name, lanes) in enumerate(SMALL_PARAMS):
            w_ref, m_ref, v_ref = ins[3 * i:3 * i + 3]
            if name == "conv_w":
                for j in range(3):
                    outs[4 * i][j] = conv_g[j:j + 1, :]
                    outs[4 * i + 1][j], outs[4 * i + 2][j], outs[4 * i + 3][j] = _adam_math(
                        w_ref[j], conv_g[j:j + 1, :], m_ref[j], v_ref[j])
                continue
            g = rel if name == "rel_bias" else small[:, lanes[0]:lanes[0] + lanes[1]]
            outs[4 * i][...] = g
            outs[4 * i + 1][...], outs[4 * i + 2][...], outs[4 * i + 3][...] = _adam_math(
                w_ref[...], g, m_ref[...], v_ref[...])

    vmem = pl.BlockSpec(memory_space=pltpu.VMEM)
    out_shape = [jax.ShapeDtypeStruct((1, 128), F32)]
    for w, _, _ in state:
        out_shape += [jax.ShapeDtypeStruct(w.shape, F32)] * 4
    outs = pl.pallas_call(
        body, name="small_update",
        in_specs=[pl.BlockSpec(memory_space=pltpu.SMEM), vmem, vmem] + [vmem] * len(flat)
        + [pl.BlockSpec(memory_space=pl.ANY)] * len(after),
        out_shape=out_shape,
    )(me, packed_all, rel_all, *flat, *after)
    return outs[0], [tuple(outs[1 + 4 * i:5 + 4 * i]) for i in range(n_p)]


def _adam_math(w, g, m, v):
    m = ADAM_B1 * m + (1.0 - ADAM_B1) * g
    v = ADAM_B2 * v + (1.0 - ADAM_B2) * (g * g)
    m_hat = m / (1.0 - ADAM_B1 ** ADAM_STEP)
    v_hat = v / (1.0 - ADAM_B2 ** ADAM_STEP)
    delta = -ADAM_LR * (m_hat / (jnp.sqrt(v_hat) + ADAM_EPS) + ADAM_WD * w)
    return delta, m, v


def _adamw_parts(w, m, v, local, land, me, tr, name):
    r, c = w.shape

    def body(me_ref, w_ref, m_ref, v_ref, own_ref, land_ref, g_ref, d_ref, mo_ref, vo_ref):
        g = own_ref[0].astype(F32)
        for k in range(N_DEV - 1):
            g = g + land_ref[k].astype(F32)
        g_ref[...] = g
        d_ref[...], mo_ref[...], vo_ref[...] = _adam_math(w_ref[...], g, m_ref[...], v_ref[...])

    tile = pl.BlockSpec((tr, c), lambda i, me_ref: (i, 0))
    return pl.pallas_call(
        body, name=name,
        grid_spec=pltpu.PrefetchScalarGridSpec(
            num_scalar_prefetch=1, grid=(r // tr,),
            in_specs=[tile, tile, tile, pl.BlockSpec((1, tr, c), lambda i, me_ref: (me_ref[0], i, 0)),
                      pl.BlockSpec((N_DEV - 1, tr, c), lambda i, me_ref: (0, i, 0))],
            out_specs=[tile] * 4),
        out_shape=[jax.ShapeDtypeStruct((r, c), F32)] * 4,
        compiler_params=_params(("arbitrary",)),
    )(me, w, m, v, local, land)


def _local_step(x, target, mod, w_in_t, bias, weights_out_gu, weights_down, g_norm1, sinks, conv_w, g_attn,
                g_conv, g_norm2, g_final, exchange, start_after):
    s = x.shape[0]
    tm = min(512, s)
    tm_small = min(256, s)
    bucket = _bucket_table()

    h, q, kv, gb, gc, xc = _in_proj(start_after, x, mod, g_norm1, w_in_t, tm)
    attn, merged, lse, probs = _mixer_fwd(q, kv, gb, gc, xc, bias, sinks, conv_w, g_attn, g_conv)
    w_out, w_gu_t = weights_out_gu(merged)
    o1, x1 = _out_proj(merged, x, mod, w_out, tm)
    w_down = weights_down(x1)
    h2, act, do2, dgu, dx1, dw_out, dmerged, sm_2 = _ffn(x1, o1, merged, mod, g_norm2, w_gu_t, w_down, w_out, g_final,
                                                         target, tm_small)
    ts = min(WEIGHT_GRAD_ROWS, s)
    tok_out = exchange("w_out", dw_out)
    tok_down = exchange("w_down", _weight_grad(act, do2, D_FF // 2, ts, "w_down_grad", after=tok_out))
    tok_gu = exchange("w_gu", _weight_grad(dgu, h2, D_FF // 2, ts, "w_gu_grad", after=tok_down))
    dproj, d_rel, dsink, sm_mix = _mixer_bwd(
        tok_gu, q, kv, gb, gc, xc, probs, sinks, conv_w, g_attn, g_conv, attn, lse, dmerged, bucket)
    tok_in = exchange("w_in", _weight_grad(dproj, h, IN_PROJ_WIDTH // 2, ts, "w_in_grad"))
    dx, sm_1 = _in_proj_bwd(tok_in, dproj, x, dx1, mod, g_norm1, w_in_t, min(1024, s))

    packed = jnp.concatenate([
        sm_1[0:1], sm_1[1:2], sm_2[7:8], sm_2[0:1], sm_2[1:2], sm_2[3:4],
        sm_1[2:3],
        sm_mix[5:6, 0:128],
        sm_mix[0:1], sm_mix[1:2],
        sm_2[2:3],
        sm_2[4:5],
        sm_mix[2:3], sm_mix[3:4], sm_mix[4:5],
        sm_2[6:7, 0:128],
    ], axis=1)
    return dx, packed, d_rel


def kernel(x, c, rel_bias, w_ada, b_ada, g_norm1, w_in, sinks, conv_w, g_attn_out, g_conv_out, w_out, g_norm2, w_gu, w_down, g_final, loss_target, m_rel_bias, m_w_ada, m_b_ada, m_g_norm1, m_w_in, m_sinks, m_conv_w, m_g_attn_out, m_g_conv_out, m_w_out, m_g_norm2, m_w_gu, m_w_down, m_g_final, v_rel_bias, v_w_ada, v_b_ada, v_g_norm1, v_w_in, v_sinks, v_conv_w, v_g_attn_out, v_g_conv_out, v_w_out, v_g_norm2, v_w_gu, v_w_down, v_g_final):
    me = _linear(_mesh_position())
    me_arr = jnp.reshape(me, (1,)).astype(jnp.int32)
    ada_cols = w_ada.shape[2]
    tm = min(512, x.shape[1])

    b_cols = b_ada.reshape(N_DEV, 1, ada_cols)
    cond_all, conv_w_all, mod, w_in_blocks, staged, bias = _open_step(
        c, conv_w.transpose(1, 0, 2), w_ada[0], b_cols, w_in[0].T, [w_out[0], w_gu[0].T, w_down[0]], rel_bias.T, _bucket_table())
    conv_w_full = conv_w_all.reshape(N_DEV, 3, -1).transpose(1, 0, 2).reshape(3, CONV_WIDTH)
    w_in_t = w_in_blocks.reshape(IN_PROJ_WIDTH, D_MODEL)
    gather_sems, staged, gather_token = _gather_start(staged, "gather_start_weights")

    def weights_out_gu(after):
        got = _gather_pass_on(_gather_wait(gather_sems[0:4], staged[0:2], [after], "gather_wait_out_gu"),
                              "gather_pass_on_out_gu")
        return got[0].reshape(D_MODEL, D_MODEL), got[1].reshape(2 * D_FF, D_MODEL)

    def weights_down(after):
        got = _gather_pass_on(_gather_wait(gather_sems[4:6], staged[2:3], [after], "gather_wait_down"),
                              "gather_pass_on_down")
        return got[0].reshape(D_FF, D_MODEL)

    started = {}

    def exchange(name, dw):
        st = _exchange_start(dw.reshape(N_DEV, dw.shape[0] // N_DEV, dw.shape[1]), "exchange_start_" + name)
        started[name] = st
        return st[4]

    dx, packed, d_rel = _local_step(
        x[0], loss_target[0], mod, w_in_t, bias, weights_out_gu, weights_down, g_norm1, sinks[0], conv_w_full,
        g_attn_out, g_conv_out, g_norm2, g_final[None, :], exchange, gather_token)

    def zone(a):
        return lax.dynamic_update_slice(jnp.zeros((N_DEV,) + a.shape, F32), a[None], (me,) + (0,) * a.ndim)

    shared = _share_start([packed, d_rel], [zone(packed), zone(d_rel)], "share_small_start")

    def finish(name, after, w, m, v, tr):
        src, land = _exchange_wait(started[name], after, "exchange_wait_" + name)
        return _adamw_parts(w, m, v, src, land, me_arr, tr, "adamw_" + name)

    g_down, d_down, nm_down, nv_down = finish("w_down", [shared[2][0]], w_down[0], m_w_down[0], v_w_down[0], 176)
    g_gu, d_gu, nm_gu, nv_gu = finish("w_gu", [nv_down], w_gu[0].T, m_w_gu[0].T, v_w_gu[0].T, 352)
    g_out, d_out, nm_out, nv_out = finish("w_out", [nv_gu], w_out[0], m_w_out[0], v_w_out[0], 128)

    packed_all, rel_all = _share_wait(shared, [nv_out], "share_small_wait")
    g_ada, d_ada, nm_ada, nv_ada = _adamw_w_ada(me_arr, cond_all, packed_all, w_ada[0], m_w_ada[0], v_w_ada[0], 256)
    as_rows = {"conv_w": lambda a: a.transpose(1, 0, 2), "g_final": lambda a: a[None, :], "rel_bias": lambda a: a.T}
    small_state = {
        "rel_bias": (rel_bias, m_rel_bias, v_rel_bias), "b_ada": (b_ada, m_b_ada, v_b_ada),
        "g_norm1": (g_norm1, m_g_norm1, v_g_norm1), "sinks": (sinks, m_sinks, v_sinks),
        "conv_w": (conv_w, m_conv_w, v_conv_w), "g_attn_out": (g_attn_out, m_g_attn_out, v_g_attn_out),
        "g_conv_out": (g_conv_out, m_g_conv_out, v_g_conv_out), "g_norm2": (g_norm2, m_g_norm2, v_g_norm2),
        "g_final": (g_final, m_g_final, v_g_final),
    }
    state = [tuple(as_rows.get(name, lambda a: a)(a) for a in small_state[name]) for name, _ in SMALL_PARAMS]
    loss_row, small_out = _small_update(me_arr, packed_all, rel_all, state, [])
    loss = loss_row[0, 0]
    back = {"rel_bias": lambda a: a.T, "conv_w": lambda a: a.transpose(1, 0, 2)}
    small_res = {name: tuple(back[name](a) if name in back else a.reshape(small_state[name][0].shape) for a in res)
                 for (name, _), res in zip(SMALL_PARAMS, small_out)}

    g_in, d_in, nm_in, nv_in = finish("w_in", [loss_row, nv_ada], w_in[0].T, m_w_in[0].T, v_w_in[0].T, 144)

    big = {
        "w_ada": (g_ada[None], d_ada[None], nm_ada[None], nv_ada[None]),
        "w_in": (g_in.T[None], d_in.T[None], nm_in.T[None], nv_in.T[None]),
        "w_out": (g_out[None], d_out[None], nm_out[None], nv_out[None]),
        "w_gu": (g_gu.T[None], d_gu.T[None], nm_gu.T[None], nv_gu.T[None]),
        "w_down": (g_down[None], d_down[None], nm_down[None], nv_down[None]),
    }
    order = ["rel_bias", "w_ada", "b_ada", "g_norm1", "w_in", "sinks", "conv_w", "g_attn_out", "g_conv_out", "w_out",
             "g_norm2", "w_gu", "w_down", "g_final"]
    results = [big[k] if k in big else small_res[k] for k in order]
    return (loss, dx[None], *[r[0] for r in results], *[r[1] for r in results], *[r[2] for r in results],
            *[r[3] for r in results])
```

```python
import math

import jax
import jax.numpy as jnp
import numpy as np
from jax import lax
from jax.experimental import pallas as pl
from jax.experimental.pallas import tpu as pltpu

F32 = jnp.float32
BF16 = jnp.bfloat16

D_MODEL = 1024
HEAD_DIM = 64
N_Q_HEADS = 8
ATTN_WIDTH = 512
KV_WIDTH = 128
CONV_WIDTH = 512
IN_PROJ_WIDTH = 2304
D_FF = 2816
N_MOD = 6
N_BUCKETS = 32
MAX_DISTANCE = 128
BLOCK = 128
REL_LANES = 128
EPS = 1e-6
NEG_INF = -1e30
SCALE = HEAD_DIM ** -0.5
N_DEV = 8

ADAM_LR = 0.001
ADAM_B1 = 0.9
ADAM_B2 = 0.999
ADAM_EPS = 1e-08
ADAM_WD = 0.01
ADAM_STEP = 10

SH1, SC1, G1, SH2, SC2, G2 = range(6)

VMEM_LIMIT_LARGE = 60 * 1024 * 1024
WEIGHT_GRAD_ROWS = 2048
FFN_CHUNKS = 1
PREV_ROWS = 16
MIXER_BLOCKS = 4
MESH_ID = pl.DeviceIdType.MESH

OFF_DMOD = 0
OFF_GN1 = OFF_DMOD + N_MOD * D_MODEL
OFF_SINK = OFF_GN1 + D_MODEL
OFF_GATT = OFF_SINK + 128
OFF_GCV = OFF_GATT + ATTN_WIDTH
OFF_GN2 = OFF_GCV + CONV_WIDTH
OFF_GFIN = OFF_GN2 + D_MODEL
OFF_CONVW = OFF_GFIN + D_MODEL
OFF_LOSS = OFF_CONVW + 3 * CONV_WIDTH
PACKED = OFF_LOSS + 128


def _params(sem=None, vmem=None):
    return pltpu.CompilerParams(dimension_semantics=sem, vmem_limit_bytes=vmem)


def _coming_behind(body):
    def skipping(after_ref, *refs):
        body(*refs)

    return skipping


ANY_SPEC = pl.BlockSpec(memory_space=pl.ANY)


def _full(shape):
    nd = len(shape)
    return pl.BlockSpec(shape, lambda *_: (0,) * nd)


def _rows(tm, width):
    return pl.BlockSpec((tm, width), lambda i, *_: (i, 0))


def _sigmoid(x):
    return 1.0 / (1.0 + jnp.exp(-x))


def _rsqrt_mean_sq(x):
    return lax.rsqrt(jnp.mean(x * x, axis=-1, keepdims=True) + EPS)


def _colsum(x):
    return jnp.sum(x, axis=0, keepdims=True)


def _dot(a, b):
    return jnp.dot(a, b, preferred_element_type=F32)


def _dot_nt(a, b):
    return lax.dot_general(a, b, (((1,), (1,)), ((), ())), preferred_element_type=F32)


def _dot_tn(a, b):
    return lax.dot_general(a, b, (((0,), (0,)), ((), ())), preferred_element_type=F32)


def _mesh_position():
    return lax.axis_index("x"), lax.axis_index("y"), lax.axis_index("c")


def _linear(p):
    return 4 * p[0] + 2 * p[1] + p[2]


def _peer(k):
    x, y, c = _mesh_position()
    return (1 - x if k & 4 else x, 1 - y if k & 2 else y, 1 - c if k & 1 else c)


HBM_SPEC = pl.BlockSpec(memory_space=pltpu.HBM)
SEM_SPEC = pl.BlockSpec(memory_space=pltpu.SEMAPHORE)
DATAFLOW = pltpu.SideEffectType.DATAFLOW_SIDE_EFFECTING


def _exchange_start(src, name):
    r, c = src.shape[1:]

    def body(src_ref, land_ref, send_sems, recv_sems, src_thru, land_thru, token):
        for k in range(1, N_DEV):
            peer = _peer(k)
            pltpu.make_async_remote_copy(
                src_ref=src_ref.at[_linear(peer)], dst_ref=land_ref.at[k - 1],
                send_sem=send_sems.at[k - 1], recv_sem=recv_sems.at[k - 1],
                device_id=peer, device_id_type=MESH_ID).start()
        token[...] = jnp.zeros_like(token)

    land = lax.empty((N_DEV - 1, r, c), src.dtype)
    return pl.pallas_call(
        body, name=name,
        out_shape=(pltpu.SemaphoreType.DMA((N_DEV - 1,)), pltpu.SemaphoreType.DMA((N_DEV - 1,)),
                   pltpu.HBM(src.shape, src.dtype), pltpu.HBM(land.shape, land.dtype),
                   jax.ShapeDtypeStruct((8, 128), F32)),
        in_specs=(HBM_SPEC, HBM_SPEC),
        out_specs=(SEM_SPEC, SEM_SPEC, HBM_SPEC, HBM_SPEC, pl.BlockSpec(memory_space=pltpu.VMEM)),
        input_output_aliases={0: 2, 1: 3},
        compiler_params=pltpu.CompilerParams(has_side_effects=DATAFLOW),
    )(pltpu.with_memory_space_constraint(src, pltpu.HBM), pltpu.with_memory_space_constraint(land, pltpu.HBM))


def _exchange_wait(started, after, name):
    send_sems, recv_sems, src_thru, land_thru, _ = started

    def body(src_ref, land_ref, send_sems, recv_sems, *rest):
        for k in range(1, N_DEV):
            cp = pltpu.make_async_remote_copy(
                src_ref=src_ref.at[0], dst_ref=land_ref.at[k - 1],
                send_sem=send_sems.at[k - 1], recv_sem=recv_sems.at[k - 1],
                device_id=_peer(k), device_id_type=MESH_ID)
            cp.wait_send()
            cp.wait_recv()

    return pl.pallas_call(
        body, name=name,
        out_shape=(pltpu.HBM(src_thru.shape, src_thru.dtype), pltpu.HBM(land_thru.shape, land_thru.dtype)),
        in_specs=(HBM_SPEC, HBM_SPEC, SEM_SPEC, SEM_SPEC) + (pl.BlockSpec(memory_space=pl.ANY),) * len(after),
        out_specs=(HBM_SPEC, HBM_SPEC), input_output_aliases={0: 0, 1: 1},
        compiler_params=pltpu.CompilerParams(has_side_effects=DATAFLOW),
    )(src_thru, land_thru, send_sems, recv_sems, *after)


def _share_start(arrs, zones, name):
    n = len(arrs)

    def body(*refs):
        src_refs, zone_refs, sems = refs[:n], refs[n:2 * n], refs[2 * n:4 * n]
        me = _linear(_mesh_position())
        for a in range(n):
            for k in range(1, N_DEV):
                pltpu.make_async_remote_copy(
                    src_ref=src_refs[a], dst_ref=zone_refs[a].at[me],
                    send_sem=sems[2 * a].at[k - 1], recv_sem=sems[2 * a + 1].at[k - 1],
                    device_id=_peer(k), device_id_type=MESH_ID).start()

    outs = pl.pallas_call(
        body, name=name,
        out_shape=tuple(pltpu.SemaphoreType.DMA((N_DEV - 1,)) for _ in range(2 * n))
        + tuple(pltpu.HBM(a.shape, a.dtype) for a in arrs) + tuple(pltpu.HBM(z.shape, z.dtype) for z in zones),
        in_specs=(HBM_SPEC,) * (2 * n),
        out_specs=(SEM_SPEC,) * (2 * n) + (HBM_SPEC,) * (2 * n),
        input_output_aliases={i: 2 * n + i for i in range(2 * n)},
        compiler_params=pltpu.CompilerParams(has_side_effects=DATAFLOW),
    )(*[pltpu.with_memory_space_constraint(a, pltpu.HBM) for a in list(arrs) + list(zones)])
    return outs[:2 * n], outs[2 * n:3 * n], outs[3 * n:]


def _share_wait(started, after, name):
    sems, arrs, zones = started
    n = len(arrs)

    def body(*refs):
        src_refs, zone_refs, sem_refs = refs[:n], refs[n:2 * n], refs[2 * n:4 * n]
        for a in range(n):
            for k in range(1, N_DEV):
                cp = pltpu.make_async_remote_copy(
                    src_ref=src_refs[a], dst_ref=zone_refs[a].at[_linear(_peer(k))],
                    send_sem=sem_refs[2 * a].at[k - 1], recv_sem=sem_refs[2 * a + 1].at[k - 1],
                    device_id=_peer(k), device_id_type=MESH_ID)
                cp.wait_send()
                cp.wait_recv()

    outs = pl.pallas_call(
        body, name=name,
        out_shape=tuple(pltpu.HBM(a.shape, a.dtype) for a in arrs) + tuple(pltpu.HBM(z.shape, z.dtype) for z in zones),
        in_specs=(HBM_SPEC,) * (2 * n) + (SEM_SPEC,) * (2 * n) + (pl.BlockSpec(memory_space=pl.ANY),) * len(after),
        out_specs=(HBM_SPEC,) * (2 * n), input_output_aliases={i: i for i in range(2 * n)},
        compiler_params=pltpu.CompilerParams(has_side_effects=DATAFLOW),
    )(*arrs, *zones, *sems, *after)
    return list(outs[n:])


def _same_core_peers():
    x, y, c = _mesh_position()
    return [(x, y, 1 - c), (1 - x, y, c), (x, 1 - y, c), (1 - x, 1 - y, c)]


def _gather_start(bufs, name):
    n = len(bufs)

    def body(*refs):
        buf_refs, rest = refs[:n], refs[n:]
        sems, token = rest[:2 * n], rest[-1]
        me = _linear(_mesh_position())
        for a in range(n):
            for k, peer in enumerate(_same_core_peers()):
                pltpu.make_async_remote_copy(
                    src_ref=buf_refs[a].at[me], dst_ref=buf_refs[a].at[me],
                    send_sem=sems[2 * a].at[k], recv_sem=sems[2 * a + 1].at[k],
                    device_id=peer, device_id_type=MESH_ID).start()
        token[...] = jnp.zeros_like(token)

    outs = pl.pallas_call(
        body, name=name,
        out_shape=tuple(pltpu.SemaphoreType.DMA((4,)) for _ in range(2 * n))
        + tuple(pltpu.HBM(b.shape, b.dtype) for b in bufs) + (jax.ShapeDtypeStruct((8, 128), F32),),
        in_specs=(HBM_SPEC,) * n,
        out_specs=(SEM_SPEC,) * (2 * n) + (HBM_SPEC,) * n + (pl.BlockSpec(memory_space=pltpu.VMEM),),
        input_output_aliases={a: 2 * n + a for a in range(n)},
        compiler_params=pltpu.CompilerParams(has_side_effects=DATAFLOW),
    )(*[pltpu.with_memory_space_constraint(b, pltpu.HBM) for b in bufs])
    return outs[:2 * n], outs[2 * n:3 * n], outs[3 * n]


def _gather_wait(sems, bufs, after, name):
    n = len(bufs)

    def body(*refs):
        buf_refs, sem_refs = refs[:n], refs[n:3 * n]
        x, y, c = _mesh_position()
        me = _linear((x, y, c))
        for a in range(n):
            for k, peer in enumerate(_same_core_peers()):
                cp = pltpu.make_async_remote_copy(
                    src_ref=buf_refs[a].at[me], dst_ref=buf_refs[a].at[_linear(peer)],
                    send_sem=sem_refs[2 * a].at[k], recv_sem=sem_refs[2 * a + 1].at[k],
                    device_id=peer, device_id_type=MESH_ID)
                cp.wait_send()
                cp.wait_recv()

    return list(pl.pallas_call(
        body, name=name,
        out_shape=tuple(pltpu.HBM(b.shape, b.dtype) for b in bufs),
        in_specs=(HBM_SPEC,) * n + (SEM_SPEC,) * (2 * n) + (pl.BlockSpec(memory_space=pl.ANY),) * len(after),
        out_specs=(HBM_SPEC,) * n, input_output_aliases={a: a for a in range(n)},
        compiler_params=pltpu.CompilerParams(has_side_effects=DATAFLOW),
    )(*bufs, *sems, *after))


def _gather_pass_on(bufs, name):
    n = len(bufs)

    def body(*refs):
        out_refs = refs[n:2 * n]
        send_sems, recv_sems = refs[2 * n:]
        x, y, c = _mesh_position()
        sibling = (x, y, 1 - c)
        chips = [(1 - x, y), (x, 1 - y), (1 - x, 1 - y)]
        copies = []
        for a in range(n):
            for j, chip in enumerate(chips):
                block = out_refs[a].at[_linear((*chip, c))]
                copies.append(pltpu.make_async_remote_copy(
                    src_ref=block, dst_ref=block, send_sem=send_sems.at[3 * a + j], recv_sem=recv_sems.at[3 * a + j],
                    device_id=sibling, device_id_type=MESH_ID))
                copies[-1].start()
        for a in range(n):
            for j, chip in enumerate(chips):
                copies[3 * a + j].wait_send()
                theirs = out_refs[a].at[_linear((*chip, 1 - c))]
                pltpu.make_async_remote_copy(
                    src_ref=theirs, dst_ref=theirs, send_sem=send_sems.at[3 * a + j], recv_sem=recv_sems.at[3 * a + j],
                    device_id=sibling, device_id_type=MESH_ID).wait_recv()

    hbm = pl.BlockSpec(memory_space=pl.ANY)
    return list(pl.pallas_call(
        body, name=name,
        out_shape=[jax.ShapeDtypeStruct(b.shape, b.dtype) for b in bufs],
        in_specs=[hbm] * n, out_specs=[hbm] * n, input_output_aliases={a: a for a in range(n)},
        scratch_shapes=[pltpu.SemaphoreType.DMA((3 * n,)), pltpu.SemaphoreType.DMA((3 * n,))],
    )(*bufs))


def _open_step(c, conv_w, w_ada, b_cols, w_in_t, later, rel_bias, bucket):
    cols = w_ada.shape[1]
    n_later = len(later)

    def body(c_ref, cw_ref, wa_ref, b_ref, w_ref, *rest):
        later_refs, rb_ref, bk_ref = rest[:n_later], rest[n_later], rest[n_later + 1]
        cond_ref, conv_ref, mod_ref, win_ref = rest[n_later + 2:n_later + 6]
        staged_refs, bias_ref, rows_ref = rest[n_later + 6:2 * n_later + 6], rest[2 * n_later + 6], rest[2 * n_later + 7]
        cond_own, mod_own, stage = rest[2 * n_later + 8:2 * n_later + 11]
        later_stage = rest[2 * n_later + 11:3 * n_later + 11]
        s_send, s_recv, w_send, w_recv, local_sems = rest[3 * n_later + 11:]
        x, y, cc = _mesh_position()
        me = _linear((x, y, cc))
        sibling = (x, y, 1 - cc)
        chips = [(1 - x, y), (x, 1 - y), (1 - x, 1 - y)]
        v = c_ref[...]
        cond_own[...] = v * _sigmoid(v)
        stage[...] = w_ref[...].astype(BF16)

        def small(rnd, a, k, src, dst, slot):
            return pltpu.make_async_remote_copy(
                src_ref=src, dst_ref=dst.at[slot], send_sem=s_send.at[rnd, a, k - 1], recv_sem=s_recv.at[rnd, a, k - 1],
                device_id=_peer(k), device_id_type=MESH_ID)

        def block(p):
            return win_ref.at[_linear(p)]

        def big(k, blk, to, src=None):
            return pltpu.make_async_remote_copy(
                src_ref=block(blk) if src is None else src, dst_ref=block(blk),
                send_sem=w_send.at[k], recv_sem=w_recv.at[k], device_id=to, device_id_type=MESH_ID)

        mine = [pltpu.make_async_copy(cond_own, cond_ref.at[me], local_sems.at[0]),
                pltpu.make_async_copy(cw_ref, conv_ref.at[me], local_sems.at[1]),
                pltpu.make_async_copy(stage, block((x, y, cc)), local_sems.at[2])]
        for cp in mine:
            cp.start()
        sends = []
        for k in range(1, N_DEV):
            sends += [small(0, 0, k, cond_own, cond_ref, me), small(0, 1, k, cw_ref, conv_ref, me)]
        for cp in sends:
            cp.start()
        first = [big(0, (x, y, cc), sibling, src=stage)]
        first += [big(1 + j, (x, y, cc), (*chip, cc), src=stage) for j, chip in enumerate(chips)]
        for cp in first:
            cp.start()
        for a in range(n_later):
            later_stage[a][...] = later_refs[a][...].astype(BF16)
            mine.append(pltpu.make_async_copy(later_stage[a], staged_refs[a].at[me], local_sems.at[4 + a]))
            mine[-1].start()
        _fill_bias_table(rb_ref, bk_ref, bias_ref)
        for k in range(1, N_DEV):
            small(0, 0, k, cond_own, cond_ref, _linear(_peer(k))).wait_recv()
            small(0, 1, k, cw_ref, conv_ref, _linear(_peer(k))).wait_recv()
        mine[0].wait()
        cond_all = jnp.concatenate([cond_ref[k] for k in range(N_DEV)], axis=0)
        mod_own[...] = _dot(cond_all, wa_ref[...]) + b_ref[me]
        mine.append(pltpu.make_async_copy(mod_own, mod_ref.at[me], local_sems.at[3]))
        mine[-1].start()
        second = [small(1, 0, k, mod_own, mod_ref, me) for k in range(1, N_DEV)]
        for cp in second:
            cp.start()
        passed = []
        for j, chip in enumerate(chips):
            big(1 + j, (*chip, cc), (x, y, cc)).wait_recv()
            fwd = big(4 + j, (*chip, cc), sibling)
            fwd.start()
            passed.append(fwd)
        big(0, sibling, (x, y, cc)).wait_recv()
        for j, chip in enumerate(chips):
            big(4 + j, (*chip, 1 - cc), (x, y, cc)).wait_recv()
        for k in range(1, N_DEV):
            small(1, 0, k, mod_own, mod_ref, _linear(_peer(k))).wait_recv()
        for cp in sends + first + second + passed:
            cp.wait_send()
        for cp in mine[1:]:
            cp.wait()
        flat = jnp.concatenate([mod_ref[j, pl.ds(me, 1), :] for j in range(N_DEV)], axis=1)
        rows_ref[...] = jnp.concatenate([flat[:, D_MODEL * r:D_MODEL * (r + 1)] for r in range(N_MOD)]
                                        + [jnp.zeros((8 - N_MOD, D_MODEL), F32)], axis=0)

    vmem = pl.BlockSpec(memory_space=pltpu.VMEM)
    outs = pl.pallas_call(
        body, name="open_step",
        out_shape=[jax.ShapeDtypeStruct((N_DEV,) + c.shape, F32), jax.ShapeDtypeStruct((N_DEV,) + conv_w.shape, F32),
                   jax.ShapeDtypeStruct((N_DEV, N_DEV, cols), F32),
                   jax.ShapeDtypeStruct((N_DEV,) + w_in_t.shape, BF16)]
        + [jax.ShapeDtypeStruct((N_DEV,) + a.shape, BF16) for a in later]
        + [jax.ShapeDtypeStruct((N_Q_HEADS, BLOCK, 2 * BLOCK), F32), jax.ShapeDtypeStruct((8, D_MODEL), F32)],
        in_specs=[vmem] * (5 + n_later) + [pl.BlockSpec(memory_space=pltpu.SMEM), vmem],
        out_specs=[vmem, vmem, vmem, ANY_SPEC] + [ANY_SPEC] * n_later + [vmem, vmem],
        scratch_shapes=[pltpu.VMEM(c.shape, F32), pltpu.VMEM((N_DEV, cols), F32), pltpu.VMEM(w_in_t.shape, BF16)]
        + [pltpu.VMEM(a.shape, BF16) for a in later]
        + [pltpu.SemaphoreType.DMA((2, 2, N_DEV - 1)), pltpu.SemaphoreType.DMA((2, 2, N_DEV - 1)),
           pltpu.SemaphoreType.DMA((7,)), pltpu.SemaphoreType.DMA((7,)),
           pltpu.SemaphoreType.DMA((4 + n_later,))],
        compiler_params=_params(vmem=VMEM_LIMIT_LARGE),
    )(c, conv_w, w_ada, b_cols, w_in_t, *later, rel_bias, bucket)
    return outs[0], outs[1], outs[5 + n_later], outs[3], list(outs[4:4 + n_later]), outs[4 + n_later]


def _in_proj(after, x, mod, g_norm1, w_in, tm):
    s = x.shape[0]

    def body(x_ref, mod_ref, g_ref, w_ref, h_ref, q_ref, kv_ref, gb_ref, gc_ref, xc_ref):
        xf = x_ref[...]
        n = xf * _rsqrt_mean_sq(xf) * g_ref[...]
        h = (n * (1.0 + mod_ref[SC1:SC1 + 1, :]) + mod_ref[SH1:SH1 + 1, :]).astype(BF16)
        h_ref[...] = h
        p = _dot_nt(h, w_ref[...])
        q_ref[...] = p[:, 0:512].astype(BF16)
        kv_ref[...] = p[:, 512:768].astype(BF16)
        gb_ref[...] = p[:, 768:1280].astype(BF16)
        gc_ref[...] = p[:, 1280:1792].astype(BF16)
        xc_ref[...] = p[:, 1792:2304].astype(BF16)

    return pl.pallas_call(
        _coming_behind(body), name="in_proj", grid=(s // tm,),
        in_specs=[ANY_SPEC, _rows(tm, D_MODEL), _full((8, D_MODEL)), _full((1, D_MODEL)), _full((IN_PROJ_WIDTH, D_MODEL))],
        out_specs=[_rows(tm, D_MODEL), _rows(tm, 512), _rows(tm, 256), _rows(tm, 512), _rows(tm, 512), _rows(tm, 512)],
        out_shape=[jax.ShapeDtypeStruct((s, D_MODEL), BF16), jax.ShapeDtypeStruct((s, 512), BF16),
                   jax.ShapeDtypeStruct((s, 256), BF16), jax.ShapeDtypeStruct((s, 512), BF16),
                   jax.ShapeDtypeStruct((s, 512), BF16), jax.ShapeDtypeStruct((s, 512), BF16)],
        compiler_params=_params(("arbitrary",), VMEM_LIMIT_LARGE),
    )(after, x, mod, g_norm1, w_in)


def _t5_bucket(dist):
    max_exact = N_BUCKETS // 2
    is_small = dist < max_exact
    d = np.maximum(dist, 1).astype(np.float32)
    large = max_exact + (np.log(d / max_exact) / math.log(MAX_DISTANCE / max_exact)
                         * (N_BUCKETS - max_exact)).astype(np.int32)
    large = np.minimum(large, N_BUCKETS - 1)
    return np.where(is_small, dist, large).astype(np.int32)


def _bucket_table():
    qi = np.arange(BLOCK, dtype=np.int32)[:, None]
    sj = np.arange(2 * BLOCK, dtype=np.int32)[None, :]
    return jnp.asarray(_t5_bucket(np.maximum(qi + BLOCK - sj, 0)))


def _window_mask():
    qi = lax.broadcasted_iota(jnp.int32, (BLOCK, 2 * BLOCK), 0)
    sj = lax.broadcasted_iota(jnp.int32, (BLOCK, 2 * BLOCK), 1)
    dist = qi + BLOCK - sj
    return (dist >= 0) & (dist < BLOCK)


def _fill_bias_table(rb_ref, bk_ref, o_ref):
    bk = bk_ref[...]
    inside = _window_mask()
    for h in range(N_Q_HEADS):
        acc = jnp.zeros((BLOCK, 2 * BLOCK), F32)
        for b in range(N_BUCKETS):
            acc = jnp.where(bk == b, rb_ref[h, b], acc)
        o_ref[h] = jnp.where(inside, acc, NEG_INF)


def _load_kv_window(kv_ref, n):
    prev = jnp.maximum(n - 1, 0)
    kvw = jnp.concatenate([kv_ref[pl.ds(pl.multiple_of(prev * BLOCK, BLOCK), BLOCK), :],
                           kv_ref[pl.ds(pl.multiple_of(n * BLOCK, BLOCK), BLOCK), :]], axis=0)
    k, v = kvw[:, 0:128], kvw[:, 128:256]
    k_sw = pltpu.roll(k.astype(F32), 64, 1).astype(BF16)
    v_sw = pltpu.roll(v.astype(F32), 64, 1).astype(BF16)
    return (k, k_sw), (v, v_sw)


def _conv_taps(gc, xc, gc_prev, xc_prev, n):
    u = gc * xc
    before = jnp.where(n > 0, gc_prev.astype(F32) * xc_prev.astype(F32), 0.0)
    last = before.shape[0] - 1
    row = lax.broadcasted_iota(jnp.int32, u.shape, 0)
    u1 = jnp.where(row == 0, before[last:last + 1, :], pltpu.roll(u, 1, 0))
    u2 = jnp.where(row == 0, before[last - 1:last, :],
                   jnp.where(row == 1, before[last:last + 1, :], pltpu.roll(u, 2, 0)))
    return u, u1, u2


def _mixer_fwd(q, kv, gb, gc, xc, bias, sinks, conv_w, g_attn, g_conv):
    s = q.shape[0]
    nb = s // BLOCK

    per_step = min(MIXER_BLOCKS, nb)
    tile = per_step * BLOCK

    def one_block(n, slot, before, sink_ref, q_ref, kv_ref, gb_ref, gc_ref, xc_ref, bias_ref, cw_ref, ga_ref,
                  gcv_ref, attn_ref, merged_ref, lse_ref, p_ref):
        rows = slice(slot * BLOCK, (slot + 1) * BLOCK)
        ks, vs = _load_kv_window(kv_ref, n)
        lane = lax.broadcasted_iota(jnp.int32, (BLOCK, BLOCK), 1)
        low = lane < HEAD_DIM
        col = lax.broadcasted_iota(jnp.int32, (BLOCK, 2 * BLOCK), 1)
        no_prev = (col < BLOCK) & (n == 0)
        lse_all = jnp.zeros((BLOCK, BLOCK), F32)
        pairs = []
        for p in range(4):
            qp = q_ref[rows, 128 * p:128 * (p + 1)].astype(F32)
            kvh = p // 2
            res = []
            for e in range(2):
                h = 2 * p + e
                qm = jnp.where(low if e == 0 else ~low, qp, 0.0).astype(BF16)
                sw = 0 if kvh == e else 1
                sc = _dot_nt(qm, ks[sw]) * SCALE + bias_ref[h]
                sc = jnp.where(no_prev, NEG_INF, sc)
                sink = sink_ref[h]
                m = jnp.maximum(jnp.max(sc, axis=-1, keepdims=True), sink)
                pe = jnp.exp(sc - m)
                den = jnp.sum(pe, axis=-1, keepdims=True) + jnp.exp(sink - m)
                pb = (pe * (1.0 / den)).astype(BF16)
                p_ref[slot, h] = pb
                res.append(_dot(pb, vs[sw]))
                lse_all = lse_all + jnp.where(lane == h, m + jnp.log(den), 0.0)
            pairs.append(jnp.where(low, res[0], res[1]))
        attn = jnp.concatenate(pairs, axis=1)
        attn_ref[rows, :] = attn
        lse_ref[rows, :] = lse_all
        u, u1, u2 = _conv_taps(gc_ref[rows, :].astype(F32), xc_ref[rows, :].astype(F32), before[0], before[1], n)
        cw = cw_ref[...]
        cv = gb_ref[rows, :].astype(F32) * (cw[0:1, :] * u2 + cw[1:2, :] * u1 + cw[2:3, :] * u)
        an = attn * _rsqrt_mean_sq(attn) * ga_ref[...]
        cn = cv * _rsqrt_mean_sq(cv) * gcv_ref[...]
        merged_ref[rows, :] = jnp.concatenate([an, cn], axis=1).astype(BF16)

    def body(sink_ref, q_ref, kv_ref, gb_ref, gc_ref, xc_ref, gcp_ref, xcp_ref, *rest):
        step = pl.program_id(0)
        for sub in range(per_step):
            ahead = slice(sub * BLOCK - PREV_ROWS, sub * BLOCK)
            before = (gcp_ref[...], xcp_ref[...]) if sub == 0 else (gc_ref[ahead, :], xc_ref[ahead, :])
            one_block(step * per_step + sub, sub, before, sink_ref, q_ref, kv_ref, gb_ref, gc_ref, xc_ref, *rest)

    blk = lambda w: pl.BlockSpec((tile, w), lambda n: (n, 0))
    prev8 = pl.BlockSpec((PREV_ROWS, 512), lambda n: (jnp.maximum(n * (tile // PREV_ROWS) - 1, 0), 0))
    return pl.pallas_call(
        body, name="mixer_fwd", grid=(nb // per_step,),
        in_specs=[pl.BlockSpec(memory_space=pltpu.SMEM), blk(512), _full((s, 256)), blk(512), blk(512), blk(512),
                  prev8, prev8, _full((N_Q_HEADS, BLOCK, 2 * BLOCK)), _full((3, 512)), _full((1, 512)),
                  _full((1, 512))],
        out_specs=[blk(512), blk(1024), blk(128),
                   pl.BlockSpec((per_step, N_Q_HEADS, BLOCK, 2 * BLOCK), lambda n: (n, 0, 0, 0))],
        out_shape=[jax.ShapeDtypeStruct((s, 512), F32), jax.ShapeDtypeStruct((s, 1024), BF16),
                   jax.ShapeDtypeStruct((s, 128), F32),
                   jax.ShapeDtypeStruct((nb, N_Q_HEADS, BLOCK, 2 * BLOCK), BF16)],
        compiler_params=_params(("arbitrary",)),
    )(sinks, q, kv, gb, gc, xc, gc, xc, bias, conv_w, g_attn, g_conv)


def _out_proj(merged, x, mod, w_out, tm):
    s = x.shape[0]

    def body(m_ref, x_ref, mod_ref, w_ref, o_ref, x1_ref):
        o = _dot(m_ref[...], w_ref[...])
        o_ref[...] = o.astype(BF16)
        x1_ref[...] = x_ref[...] + mod_ref[G1:G1 + 1, :] * o

    return pl.pallas_call(
        body, name="out_proj", grid=(s // tm,),
        in_specs=[_rows(tm, D_MODEL), _rows(tm, D_MODEL), _full((8, D_MODEL)), _full((D_MODEL, D_MODEL))],
        out_specs=[_rows(tm, D_MODEL), _rows(tm, D_MODEL)],
        out_shape=[jax.ShapeDtypeStruct((s, D_MODEL), BF16), jax.ShapeDtypeStruct((s, D_MODEL), F32)],
        compiler_params=_params(("arbitrary",)),
    )(merged, x, mod, w_out)


def _resident(shape):
    nd = len(shape)
    return pl.BlockSpec(shape, lambda *_: (0,) * nd, pipeline_mode=pl.Buffered(1))


def _ffn(x1, o1, merged, mod, g_norm2, w_gu, w_down, w_out, g_final, target, tm):
    s = x1.shape[0]
    chunk = D_FF // FFN_CHUNKS

    def body(x_ref, o1_ref, mg_ref, mod_ref, g_ref, wgu_ref, wd_ref, wo_ref, gf_ref, t_ref,
             h_ref, act_ref, do_ref, dgu_ref, dx1_ref, dwo_ref, dm_ref, small_ref, dwo_acc):
        @pl.when(pl.program_id(0) == 0)
        def _():
            small_ref[...] = jnp.zeros_like(small_ref)
            dwo_acc[...] = jnp.zeros_like(dwo_acc)

        xf = x_ref[...]
        n = xf * _rsqrt_mean_sq(xf) * g_ref[...]
        h = (n * (1.0 + mod_ref[SC2:SC2 + 1, :]) + mod_ref[SH2:SH2 + 1, :]).astype(BF16)
        h_ref[...] = h
        gates, ups, o = [], [], None
        for j in range(FFN_CHUNKS):
            lo = j * chunk
            gate = _dot_nt(h, wgu_ref[lo:lo + chunk, :])
            up = _dot_nt(h, wgu_ref[D_FF + lo:D_FF + lo + chunk, :])
            sg = _sigmoid(gate)
            act = (gate * sg * up).astype(BF16)
            act_ref[:, lo:lo + chunk] = act
            gates.append((up * (sg * (1.0 + gate * (1.0 - sg)))).astype(BF16))
            ups.append((gate * sg).astype(BF16))
            part = _dot(act, wd_ref[lo:lo + chunk, :])
            o = part if o is None else o + part
        g2 = mod_ref[G2:G2 + 1, :]
        x2 = xf + g2 * o
        r = _rsqrt_mean_sq(x2)
        xn = x2 * r
        gf = gf_ref[...]
        err = xn * gf - t_ref[...]
        dy = err * (1.0 / D_MODEL)
        dxn = dy * gf
        dx2 = r * (dxn - xn * jnp.mean(dxn * xn, axis=-1, keepdims=True))
        small_ref[4:5, :] += _colsum(dy * xn)
        small_ref[5:6, :] += _colsum(err * err)
        small_ref[3:4, :] += _colsum(dx2 * o)
        do = (dx2 * g2).astype(BF16)
        do_ref[...] = do
        dh = None
        for j in range(FFN_CHUNKS):
            lo = j * chunk
            dact = _dot_nt(do, wd_ref[lo:lo + chunk, :])
            dgate = (dact * gates[j].astype(F32)).astype(BF16)
            dup = (dact * ups[j].astype(F32)).astype(BF16)
            dgu_ref[:, lo:lo + chunk] = dgate
            dgu_ref[:, D_FF + lo:D_FF + lo + chunk] = dup
            part = _dot(dgate, wgu_ref[lo:lo + chunk, :]) + _dot(dup, wgu_ref[D_FF + lo:D_FF + lo + chunk, :])
            dh = part if dh is None else dh + part
        dx1 = dx2 + _norm_mod_bwd(dh, xf, g_ref[...], mod_ref[SC2:SC2 + 1, :], small_ref)
        dx1_ref[...] = dx1.astype(BF16)
        small_ref[7:8, :] += _colsum(dx1 * o1_ref[...].astype(F32))
        do1 = (dx1 * mod_ref[G1:G1 + 1, :]).astype(BF16)
        dm_ref[...] = _dot_nt(do1, wo_ref[...]).astype(BF16)
        dwo = dwo_acc[...] + _dot_tn(mg_ref[...], do1)
        dwo_acc[...] = dwo
        dwo_ref[...] = dwo.astype(BF16)

        @pl.when(pl.program_id(0) == pl.num_programs(0) - 1)
        def _():
            total = jnp.sum(small_ref[5:6, :], axis=-1, keepdims=True) * (0.5 / D_MODEL)
            small_ref[6:7, :] = jnp.broadcast_to(total, (1, D_MODEL))

    narrow = jax.ShapeDtypeStruct((s, D_MODEL), BF16)
    return pl.pallas_call(
        body, name="ffn", grid=(s // tm,),
        in_specs=[_rows(tm, D_MODEL), _rows(tm, D_MODEL), _rows(tm, D_MODEL), _full((8, D_MODEL)), _full((1, D_MODEL)),
                  _resident((2 * D_FF, D_MODEL)), _resident((D_FF, D_MODEL)), _resident((D_MODEL, D_MODEL)),
                  _full((1, D_MODEL)), _rows(tm, D_MODEL)],
        out_specs=[_rows(tm, D_MODEL), _rows(tm, D_FF), _rows(tm, D_MODEL), _rows(tm, 2 * D_FF), _rows(tm, D_MODEL),
                   _full((D_MODEL, D_MODEL)), _rows(tm, D_MODEL), _full((8, D_MODEL))],
        out_shape=[narrow, jax.ShapeDtypeStruct((s, D_FF), BF16), narrow, jax.ShapeDtypeStruct((s, 2 * D_FF), BF16),
                   narrow, jax.ShapeDtypeStruct((D_MODEL, D_MODEL), BF16), narrow,
                   jax.ShapeDtypeStruct((8, D_MODEL), F32)],
        scratch_shapes=[pltpu.VMEM((D_MODEL, D_MODEL), F32)],
        compiler_params=_params(("arbitrary",), VMEM_LIMIT_LARGE),
    )(x1, o1, merged, mod, g_norm2, w_gu, w_down, w_out, g_final, target)


def _norm_mod_bwd(dh, xf, g, scale_row, small_ref):
    r = _rsqrt_mean_sq(xf)
    xn = xf * r
    small_ref[0:1, :] += _colsum(dh)
    small_ref[1:2, :] += _colsum(dh * (xn * g))
    dn = dh * (1.0 + scale_row)
    small_ref[2:3, :] += _colsum(dn * xn)
    dxn = dn * g
    return r * (dxn - xn * jnp.mean(dxn * xn, axis=-1, keepdims=True))


def _group_norm_bwd(dm, a, g):
    r = _rsqrt_mean_sq(a)
    an = a * r
    dan = dm * g
    return r * (dan - an * jnp.mean(dan * an, axis=-1, keepdims=True)), _colsum(dm * an)


def _sum_by_bucket(db_ref, bk_ref, o_ref, rows_ref):
    bk = bk_ref[...]
    for b in range(N_BUCKETS):
        sel = (bk == b).astype(F32)
        for h in range(N_Q_HEADS):
            rows_ref[N_BUCKETS * h + b:N_BUCKETS * h + b + 1, :] = _colsum(db_ref[h] * sel)
    head = lax.broadcasted_iota(jnp.int32, (N_BUCKETS, REL_LANES), 1)
    out = jnp.zeros((N_BUCKETS, REL_LANES), F32)
    for h in range(N_Q_HEADS):
        per_bucket = jnp.sum(rows_ref[N_BUCKETS * h:N_BUCKETS * (h + 1), :], axis=-1, keepdims=True)
        out = out + jnp.where(head == h, per_bucket, 0.0)
    o_ref[...] = out


def _mixer_bwd(after, q, kv, gb, gc, xc, probs, sinks, conv_w, g_attn, g_conv, attn, lse, dmerged, bucket):
    s = q.shape[0]
    nb = s // BLOCK

    per_step = min(MIXER_BLOCKS, nb)
    tile = per_step * BLOCK
    steps = nb // per_step

    def one_block(n, slot, before, nxt, sink_ref, q_ref, kv_ref, gb_ref, gc_ref, xc_ref, p_ref, cw_ref, ga_ref,
                  gcv_ref, attn_ref, lse_ref, dm_ref, dproj_ref, dbias_ref, dsink_ref, small_ref):
        rows = slice(slot * BLOCK, (slot + 1) * BLOCK)
        next_dy, next_dkv = nxt
        dm = dm_ref[rows, :].astype(F32)
        gbv, gcv_, xcv = gb_ref[rows, :].astype(F32), gc_ref[rows, :].astype(F32), xc_ref[rows, :].astype(F32)
        u, u1, u2 = _conv_taps(gcv_, xcv, before[0], before[1], n)
        cw = cw_ref[...]
        yv = cw[0:1, :] * u2 + cw[1:2, :] * u1 + cw[2:3, :] * u
        dcv, dg_conv = _group_norm_bwd(dm[:, 512:1024], gbv * yv, gcv_ref[...])
        small_ref[1:2, :] += dg_conv
        dproj_ref[rows, 768:1280] = (dcv * yv).astype(BF16)
        dy = dcv * gbv
        row = lax.broadcasted_iota(jnp.int32, dy.shape, 0)
        d1 = jnp.where(row == BLOCK - 1, next_dy[0:1, :], pltpu.roll(dy, BLOCK - 1, 0))
        d2 = jnp.where(row == BLOCK - 2, next_dy[0:1, :],
                       jnp.where(row == BLOCK - 1, next_dy[1:2, :], pltpu.roll(dy, BLOCK - 2, 0)))
        du = cw[2:3, :] * dy + cw[1:2, :] * d1 + cw[0:1, :] * d2
        dproj_ref[rows, 1280:1792] = (du * xcv).astype(BF16)
        dproj_ref[rows, 1792:2304] = (du * gcv_).astype(BF16)
        small_ref[2:3, :] += _colsum(dy * u2)
        small_ref[3:4, :] += _colsum(dy * u1)
        small_ref[4:5, :] += _colsum(dy * u)

        attn_v = attn_ref[rows, :]
        dout, dg_attn = _group_norm_bwd(dm[:, 0:512], attn_v, ga_ref[...])
        small_ref[0:1, :] += dg_attn
        ks, vs = _load_kv_window(kv_ref, n)
        lane = lax.broadcasted_iota(jnp.int32, (BLOCK, BLOCK), 1)
        low = lane < HEAD_DIM
        lse_all = lse_ref[rows, :]
        dsink = jnp.zeros((BLOCK, BLOCK), F32)
        dq_pairs = []
        dk_groups, dv_groups = [], []
        for kvh in range(2):
            ds_rows, pr_rows, q_rows, do_rows = [], [], [], []
            for p in (2 * kvh, 2 * kvh + 1):
                qp = q_ref[rows, 128 * p:128 * (p + 1)].astype(F32)
                do_p = dout[:, 128 * p:128 * (p + 1)]
                prod = do_p * attn_v[:, 128 * p:128 * (p + 1)]
                res = []
                for e in range(2):
                    h = 2 * p + e
                    half = low if e == 0 else ~low
                    qm = jnp.where(half, qp, 0.0).astype(BF16)
                    dom = jnp.where(half, do_p, 0.0).astype(BF16)
                    delta = jnp.sum(jnp.where(half, prod, 0.0), axis=-1, keepdims=True)
                    lse_h = jnp.sum(jnp.where(lane == h, lse_all, 0.0), axis=-1, keepdims=True)
                    sw = 0 if kvh == e else 1
                    pb = p_ref[slot, h]
                    dp = _dot_nt(dom, vs[sw])
                    ds = pb.astype(F32) * (dp - delta)
                    dbias_ref[h] += ds
                    dsink = dsink + jnp.where(lane == h, -jnp.exp(sink_ref[h] - lse_h) * delta, 0.0)
                    dsb = ds.astype(BF16)
                    res.append(_dot(dsb, ks[sw]) * SCALE)
                    ds_rows.append(dsb)
                    pr_rows.append(pb)
                    q_rows.append(qm)
                    do_rows.append(dom)
                dq_pairs.append(jnp.where(low, res[0], res[1]))
            dk_g = _dot_tn(jnp.concatenate(ds_rows, axis=0), jnp.concatenate(q_rows, axis=0)) * SCALE
            dv_g = _dot_tn(jnp.concatenate(pr_rows, axis=0), jnp.concatenate(do_rows, axis=0))
            dk_groups.append(dk_g + pltpu.roll(dk_g, 64, 1))
            dv_groups.append(dv_g + pltpu.roll(dv_g, 64, 1))
        dproj_ref[rows, 0:512] = jnp.concatenate(dq_pairs, axis=1).astype(BF16)
        dsink_ref[...] += dsink
        low_kv = lax.broadcasted_iota(jnp.int32, (2 * BLOCK, BLOCK), 1) < HEAD_DIM
        dkv_win = jnp.concatenate([jnp.where(low_kv, dk_groups[0], dk_groups[1]),
                                   jnp.where(low_kv, dv_groups[0], dv_groups[1])], axis=1)
        dproj_ref[rows, 512:768] = (dkv_win[BLOCK:2 * BLOCK, :] + next_dkv).astype(BF16)
        return dy[0:8, :], dkv_win[0:BLOCK, :]

    def body(sink_ref, q_ref, kv_ref, gb_ref, gc_ref, xc_ref, gcp_ref, xcp_ref, p_ref, cw_ref, ga_ref, gcv_ref,
             attn_ref, lse_ref, dm_ref, bk_ref, dproj_ref, drel_ref, dsink_ref, small_ref,
             dy_ref, dkv_ref, dbias_ref, rows_ref):
        refs = (p_ref, cw_ref, ga_ref, gcv_ref, attn_ref, lse_ref, dm_ref, dproj_ref, dbias_ref, dsink_ref, small_ref)
        step = pl.program_id(0)

        @pl.when(step == 0)
        def _():
            dbias_ref[...] = jnp.zeros_like(dbias_ref)
            dsink_ref[...] = jnp.zeros_like(dsink_ref)
            small_ref[...] = jnp.zeros_like(small_ref)
            dy_ref[...] = jnp.zeros_like(dy_ref)
            dkv_ref[...] = jnp.zeros_like(dkv_ref)

        nxt = (dy_ref[...], dkv_ref[...])
        for sub in reversed(range(per_step)):
            ahead = slice(sub * BLOCK - PREV_ROWS, sub * BLOCK)
            before = (gcp_ref[...], xcp_ref[...]) if sub == 0 else (gc_ref[ahead, :], xc_ref[ahead, :])
            nxt = one_block((steps - 1 - step) * per_step + sub, sub, before, nxt,
                            sink_ref, q_ref, kv_ref, gb_ref, gc_ref, xc_ref, *refs)
        dy_ref[...], dkv_ref[...] = nxt

        @pl.when(step == steps - 1)
        def _():
            small_ref[5:6, :] = jnp.concatenate([_colsum(dsink_ref[...]), jnp.zeros((1, 512 - BLOCK), F32)], axis=1)
            _sum_by_bucket(dbias_ref, bk_ref, drel_ref, rows_ref)

    blk = lambda w: pl.BlockSpec((tile, w), lambda t: (steps - 1 - t, 0))
    prev8 = pl.BlockSpec((PREV_ROWS, 512),
                         lambda t: (jnp.maximum((steps - 1 - t) * (tile // PREV_ROWS) - 1, 0), 0))
    bf = lambda w: jax.ShapeDtypeStruct((s, w), BF16)
    return pl.pallas_call(
        _coming_behind(body), name="mixer_bwd", grid=(steps,),
        in_specs=[ANY_SPEC, pl.BlockSpec(memory_space=pltpu.SMEM), blk(512), _full((s, 256)), blk(512), blk(512), blk(512),
                  prev8, prev8,
                  pl.BlockSpec((per_step, N_Q_HEADS, BLOCK, 2 * BLOCK), lambda t: (steps - 1 - t, 0, 0, 0)),
                  _full((3, 512)), _full((1, 512)), _full((1, 512)), blk(512), blk(128), blk(1024),
                  _full((BLOCK, 2 * BLOCK))],
        out_specs=[blk(IN_PROJ_WIDTH), _full((N_BUCKETS, REL_LANES)), _full((BLOCK, BLOCK)), _full((8, 512))],
        out_shape=[bf(IN_PROJ_WIDTH), jax.ShapeDtypeStruct((N_BUCKETS, REL_LANES), F32),
                   jax.ShapeDtypeStruct((BLOCK, BLOCK), F32), jax.ShapeDtypeStruct((8, 512), F32)],
        scratch_shapes=[pltpu.VMEM((8, 512), F32), pltpu.VMEM((BLOCK, 2 * KV_WIDTH), F32),
                        pltpu.VMEM((N_Q_HEADS, BLOCK, 2 * BLOCK), F32),
                        pltpu.VMEM((N_BUCKETS * N_Q_HEADS, 2 * BLOCK), F32)],
        compiler_params=_params(("arbitrary",), VMEM_LIMIT_LARGE),
    )(after, sinks, q, kv, gb, gc, xc, gc, xc, probs, conv_w, g_attn, g_conv, attn, lse, dmerged, bucket)


def _in_proj_bwd(after, dproj, x, dx1, mod, g_norm1, w_in, tm):
    s = x.shape[0]

    def body(dproj_ref, x_ref, dx1_ref, mod_ref, g_ref, w_ref, dx_ref, small_ref):
        @pl.when(pl.program_id(0) == 0)
        def _():
            small_ref[...] = jnp.zeros_like(small_ref)

        dh = _dot(dproj_ref[...], w_ref[...])
        dx_ref[...] = dx1_ref[...].astype(F32) + _norm_mod_bwd(dh, x_ref[...], g_ref[...], mod_ref[SC1:SC1 + 1, :],
                                                               small_ref)

    return pl.pallas_call(
        _coming_behind(body), name="in_proj_bwd", grid=(s // tm,),
        in_specs=[ANY_SPEC, _rows(tm, IN_PROJ_WIDTH), _rows(tm, D_MODEL), _rows(tm, D_MODEL), _full((8, D_MODEL)),
                  _full((1, D_MODEL)), _full((IN_PROJ_WIDTH, D_MODEL))],
        out_specs=[_rows(tm, D_MODEL), _full((8, D_MODEL))],
        out_shape=[jax.ShapeDtypeStruct((s, D_MODEL), F32), jax.ShapeDtypeStruct((8, D_MODEL), F32)],
        compiler_params=_params(("arbitrary",), VMEM_LIMIT_LARGE),
    )(after, dproj, x, dx1, mod, g_norm1, w_in)


def _weight_grad(a, b, tk, ts, name, after=None):
    s, k = a.shape
    n = b.shape[1]
    nt = s // ts
    extra = [] if after is None else [after]

    def body(a_ref, b_ref, *rest):
        o_ref, acc_ref = rest[-2:]
        t = pl.program_id(1)
        @pl.when(t == 0)
        def _():
            acc_ref[...] = jnp.zeros_like(acc_ref)

        acc = acc_ref[...] + _dot_tn(a_ref[...], b_ref[...])
        acc_ref[...] = acc
        o_ref[...] = acc.astype(BF16)

    return pl.pallas_call(
        body, name=name, grid=(k // tk, nt),
        in_specs=[pl.BlockSpec((ts, tk), lambda i, t: (t, i)), pl.BlockSpec((ts, n), lambda i, t: (t, 0))]
        + [ANY_SPEC] * len(extra),
        out_specs=pl.BlockSpec((tk, n), lambda i, t: (i, 0)),
        out_shape=jax.ShapeDtypeStruct((k, n), BF16),
        scratch_shapes=[pltpu.VMEM((tk, n), F32)],
        compiler_params=_params(("arbitrary", "arbitrary"), VMEM_LIMIT_LARGE),
    )(a, b, *extra)


def _lanes_from(x, start, width):
    n = x.shape[1]
    return pltpu.roll(x, (n - start) % n, 1)[:, 0:width]


def _adamw_w_ada(me, cond_all, packed_all, w, m, v, tr):
    r, cols = w.shape

    def body(me_ref, c_ref, p_ref, w_ref, m_ref, v_ref, g_ref, d_ref, mo_ref, vo_ref):
        dmod = jnp.concatenate([p_ref[k][:, OFF_DMOD:OFF_DMOD + N_MOD * D_MODEL] for k in range(N_DEV)], axis=0)
        mine = _lanes_from(dmod, me_ref[0] * cols, cols)
        pad = lambda a: jnp.concatenate([a, jnp.zeros((128 - N_DEV, a.shape[1]), F32)], axis=0)
        cond = jnp.concatenate([c_ref[k] for k in range(N_DEV)], axis=0)
        g = _dot_tn(pad(cond), pad(mine))
        g_ref[...] = g
        d_ref[...], mo_ref[...], vo_ref[...] = _adam_math(w_ref[...], g, m_ref[...], v_ref[...])

    tile = pl.BlockSpec((tr, cols), lambda i, me_ref: (i, 0))
    return pl.pallas_call(
        body, name="adamw_w_ada",
        grid_spec=pltpu.PrefetchScalarGridSpec(
            num_scalar_prefetch=1, grid=(r // tr,),
            in_specs=[pl.BlockSpec((N_DEV, 1, tr), lambda i, me_ref: (0, 0, i)),
                      pl.BlockSpec(packed_all.shape, lambda i, me_ref: (0, 0, 0)), tile, tile, tile],
            out_specs=[tile] * 4),
        out_shape=[jax.ShapeDtypeStruct((r, cols), F32)] * 4,
        compiler_params=_params(("arbitrary",)),
    )(me, cond_all, packed_all, w, m, v)


def _share_wait_adamw_w_ada(started, after, me, cond_all, w, m, v, tr):
    sems, arrs, zones = started
    n = len(arrs)
    r, cols = w.shape
    first = 4 * n + len(after)

    def body(me_ref, *refs):
        src_refs, zone_refs, sem_refs = refs[:n], refs[n:2 * n], refs[2 * n:4 * n]
        c_ref, w_ref, m_ref, v_ref = refs[first:first + 4]
        g_ref, d_ref, mo_ref, vo_ref, p_ref, load_sem = refs[first + 4 + 2 * n:]
        for a in range(n):
            for k in range(1, N_DEV):
                cp = pltpu.make_async_remote_copy(
                    src_ref=src_refs[a], dst_ref=zone_refs[a].at[_linear(_peer(k))],
                    send_sem=sem_refs[2 * a].at[k - 1], recv_sem=sem_refs[2 * a + 1].at[k - 1],
                    device_id=_peer(k), device_id_type=MESH_ID)
                cp.wait_send()
                cp.wait_recv()
        load = pltpu.make_async_copy(zone_refs[0], p_ref, load_sem.at[0])
        load.start()
        load.wait()
        dmod = jnp.concatenate([p_ref[k][:, OFF_DMOD:OFF_DMOD + N_MOD * D_MODEL] for k in range(N_DEV)], axis=0)
        pad = lambda a: jnp.concatenate([a, jnp.zeros((128 - N_DEV, a.shape[1]), F32)], axis=0)
        mine = pad(_lanes_from(dmod, me_ref[0] * cols, cols))
        for i in range(r // tr):
            rows = slice(tr * i, tr * (i + 1))
            cond = jnp.concatenate([c_ref[k][:, rows] for k in range(N_DEV)], axis=0)
            g = _dot_tn(pad(cond), mine)
            g_ref[rows, :] = g
            d_ref[rows, :], mo_ref[rows, :], vo_ref[rows, :] = _adam_math(
                w_ref[rows, :], g, m_ref[rows, :], v_ref[rows, :])

    vmem = pl.BlockSpec(memory_space=pltpu.VMEM)
    outs = pl.pallas_call(
        body, name="share_small_wait_adamw_w_ada",
        out_shape=tuple(pltpu.HBM(a.shape, a.dtype) for a in arrs) + tuple(pltpu.HBM(z.shape, z.dtype) for z in zones)
        + (jax.ShapeDtypeStruct((r, cols), F32),) * 4,
        in_specs=(pl.BlockSpec(memory_space=pltpu.SMEM),) + (HBM_SPEC,) * (2 * n) + (SEM_SPEC,) * (2 * n)
        + (ANY_SPEC,) * len(after) + (vmem,) * 4,
        out_specs=(HBM_SPEC,) * (2 * n) + (vmem,) * 4,
        input_output_aliases={1 + i: i for i in range(2 * n)},
        scratch_shapes=[pltpu.VMEM(zones[0].shape, F32), pltpu.SemaphoreType.DMA((1,))],
        compiler_params=pltpu.CompilerParams(has_side_effects=DATAFLOW, vmem_limit_bytes=VMEM_LIMIT_LARGE),
    )(me, *arrs, *zones, *sems, *after, cond_all, w, m, v)
    return list(outs[n:2 * n]), tuple(outs[2 * n:])


SMALL_PARAMS = (("rel_bias", None), ("b_ada", (OFF_DMOD, N_MOD * D_MODEL)), ("g_norm1", (OFF_GN1, D_MODEL)),
                ("sinks", (OFF_SINK, N_Q_HEADS)), ("conv_w", None), ("g_attn_out", (OFF_GATT, ATTN_WIDTH)),
                ("g_conv_out", (OFF_GCV, CONV_WIDTH)), ("g_norm2", (OFF_GN2, D_MODEL)),
                ("g_final", (OFF_GFIN, D_MODEL)))


def _small_update(me, packed_all, rel_all, state, after):
    n_p = len(SMALL_PARAMS)
    flat = [a for triple in state for a in triple]
    conv_cols = state[4][0].shape[-1]

    def body(me_ref, p_ref, r_ref, *refs):
        ins = refs[:3 * n_p]
        loss_ref, outs = refs[3 * n_p + len(after)], refs[3 * n_p + len(after) + 1:]
        small, rel = p_ref[0], r_ref[0]
        for k in range(1, N_DEV):
            small = small + p_ref[k]
            rel = rel + r_ref[k]
        rel = jnp.concatenate([rel, jnp.zeros((REL_LANES - N_BUCKETS, REL_LANES), F32)], axis=0).T
        rel = rel[0:N_Q_HEADS, 0:N_BUCKETS]
        loss_ref[...] = small[:, OFF_LOSS:OFF_LOSS + 128]
        taps = jnp.concatenate([small[:, OFF_CONVW + CONV_WIDTH * j:OFF_CONVW + CONV_WIDTH * (j + 1)]
                                for j in range(3)] + [jnp.zeros((5, CONV_WIDTH), F32)], axis=0)
        conv_g = _lanes_from(taps, me_ref[0] * conv_cols, conv_cols)[0:3, :]
        for i, (name, lanes) in enumerate(SMALL_PARAMS):
            w_ref, m_ref, v_ref = ins[3 * i:3 * i + 3]
            if name == "conv_w":
                for j in range(3):
                    outs[4 * i][j] = conv_g[j:j + 1, :]
                    outs[4 * i + 1][j], outs[4 * i + 2][j], outs[4 * i + 3][j] = _adam_math(
                        w_ref[j], conv_g[j:j + 1, :], m_ref[j], v_ref[j])
                continue
            g = rel if name == "rel_bias" else small[:, lanes[0]:lanes[0] + lanes[1]]
            outs[4 * i][...] = g
            outs[4 * i + 1][...], outs[4 * i + 2][...], outs[4 * i + 3][...] = _adam_math(
                w_ref[...], g, m_ref[...], v_ref[...])

    vmem = pl.BlockSpec(memory_space=pltpu.VMEM)
    out_shape = [jax.ShapeDtypeStruct((1, 128), F32)]
    for w, _, _ in state:
        out_shape += [jax.ShapeDtypeStruct(w.shape, F32)] * 4
    outs = pl.pallas_call(
        body, name="small_update",
        in_specs=[pl.BlockSpec(memory_space=pltpu.SMEM), vmem, vmem] + [vmem] * len(flat)
        + [pl.BlockSpec(memory_space=pl.ANY)] * len(after),
        out_shape=out_shape,
    )(me, packed_all, rel_all, *flat, *after)
    return outs[0], [tuple(outs[1 + 4 * i:5 + 4 * i]) for i in range(n_p)]


def _adam_math(w, g, m, v):
    m = ADAM_B1 * m + (1.0 - ADAM_B1) * g
    v = ADAM_B2 * v + (1.0 - ADAM_B2) * (g * g)
    m_hat = m / (1.0 - ADAM_B1 ** ADAM_STEP)
    v_hat = v / (1.0 - ADAM_B2 ** ADAM_STEP)
    delta = -ADAM_LR * (m_hat / (jnp.sqrt(v_hat) + ADAM_EPS) + ADAM_WD * w)
    return delta, m, v


def _adamw_parts(w, m, v, local, land, me, tr, name):
    r, c = w.shape

    def body(me_ref, w_ref, m_ref, v_ref, own_ref, land_ref, g_ref, d_ref, mo_ref, vo_ref):
        g = own_ref[0].astype(F32)
        for k in range(N_DEV - 1):
            g = g + land_ref[k].astype(F32)
        g_ref[...] = g
        d_ref[...], mo_ref[...], vo_ref[...] = _adam_math(w_ref[...], g, m_ref[...], v_ref[...])

    tile = pl.BlockSpec((tr, c), lambda i, me_ref: (i, 0))
    return pl.pallas_call(
        body, name=name,
        grid_spec=pltpu.PrefetchScalarGridSpec(
            num_scalar_prefetch=1, grid=(r // tr,),
            in_specs=[tile, tile, tile, pl.BlockSpec((1, tr, c), lambda i, me_ref: (me_ref[0], i, 0)),
                      pl.BlockSpec((N_DEV - 1, tr, c), lambda i, me_ref: (0, i, 0))],
            out_specs=[tile] * 4),
        out_shape=[jax.ShapeDtypeStruct((r, c), F32)] * 4,
        compiler_params=_params(("arbitrary",)),
    )(me, w, m, v, local, land)


def _local_step(x, target, mod, w_in_t, bias, weights_out_gu, weights_down, g_norm1, sinks, conv_w, g_attn,
                g_conv, g_norm2, g_final, exchange, start_after):
    s = x.shape[0]
    tm = min(512, s)
    tm_small = min(256, s)
    bucket = _bucket_table()

    h, q, kv, gb, gc, xc = _in_proj(start_after, x, mod, g_norm1, w_in_t, tm)
    attn, merged, lse, probs = _mixer_fwd(q, kv, gb, gc, xc, bias, sinks, conv_w, g_attn, g_conv)
    w_out, w_gu_t = weights_out_gu(merged)
    o1, x1 = _out_proj(merged, x, mod, w_out, tm)
    w_down = weights_down(x1)
    h2, act, do2, dgu, dx1, dw_out, dmerged, sm_2 = _ffn(x1, o1, merged, mod, g_norm2, w_gu_t, w_down, w_out, g_final,
                                                         target, tm_small)
    ts = min(WEIGHT_GRAD_ROWS, s)
    tok_out = exchange("w_out", dw_out)
    tok_down = exchange("w_down", _weight_grad(act, do2, D_FF // 2, ts, "w_down_grad", after=tok_out))
    tok_gu = exchange("w_gu", _weight_grad(dgu, h2, D_FF // 2, ts, "w_gu_grad", after=tok_down))
    dproj, d_rel, dsink, sm_mix = _mixer_bwd(
        tok_gu, q, kv, gb, gc, xc, probs, sinks, conv_w, g_attn, g_conv, attn, lse, dmerged, bucket)
    tok_in = exchange("w_in", _weight_grad(dproj, h, IN_PROJ_WIDTH // 2, ts, "w_in_grad"))
    dx, sm_1 = _in_proj_bwd(tok_in, dproj, x, dx1, mod, g_norm1, w_in_t, min(1024, s))

    packed = jnp.concatenate([
        sm_1[0:1], sm_1[1:2], sm_2[7:8], sm_2[0:1], sm_2[1:2], sm_2[3:4],
        sm_1[2:3],
        sm_mix[5:6, 0:128],
        sm_mix[0:1], sm_mix[1:2],
        sm_2[2:3],
        sm_2[4:5],
        sm_mix[2:3], sm_mix[3:4], sm_mix[4:5],
        sm_2[6:7, 0:128],
    ], axis=1)
    return dx, packed, d_rel


def kernel(x, c, rel_bias, w_ada, b_ada, g_norm1, w_in, sinks, conv_w, g_attn_out, g_conv_out, w_out, g_norm2, w_gu, w_down, g_final, loss_target, m_rel_bias, m_w_ada, m_b_ada, m_g_norm1, m_w_in, m_sinks, m_conv_w, m_g_attn_out, m_g_conv_out, m_w_out, m_g_norm2, m_w_gu, m_w_down, m_g_final, v_rel_bias, v_w_ada, v_b_ada, v_g_norm1, v_w_in, v_sinks, v_conv_w, v_g_attn_out, v_g_conv_out, v_w_out, v_g_norm2, v_w_gu, v_w_down, v_g_final):
    me = _linear(_mesh_position())
    me_arr = jnp.reshape(me, (1,)).astype(jnp.int32)
    ada_cols = w_ada.shape[2]
    tm = min(512, x.shape[1])

    b_cols = b_ada.reshape(N_DEV, 1, ada_cols)
    cond_all, conv_w_all, mod, w_in_blocks, staged, bias = _open_step(
        c, conv_w.transpose(1, 0, 2), w_ada[0], b_cols, w_in[0].T, [w_out[0], w_gu[0].T, w_down[0]], rel_bias.T, _bucket_table())
    conv_w_full = conv_w_all.reshape(N_DEV, 3, -1).transpose(1, 0, 2).reshape(3, CONV_WIDTH)
    w_in_t = w_in_blocks.reshape(IN_PROJ_WIDTH, D_MODEL)
    gather_sems, staged, gather_token = _gather_start(staged, "gather_start_weights")

    def weights_out_gu(after):
        got = _gather_pass_on(_gather_wait(gather_sems[0:4], staged[0:2], [after], "gather_wait_out_gu"),
                              "gather_pass_on_out_gu")
        return got[0].reshape(D_MODEL, D_MODEL), got[1].reshape(2 * D_FF, D_MODEL)

    def weights_down(after):
        got = _gather_pass_on(_gather_wait(gather_sems[4:6], staged[2:3], [after], "gather_wait_down"),
                              "gather_pass_on_down")
        return got[0].reshape(D_FF, D_MODEL)

    started = {}

    def exchange(name, dw):
        st = _exchange_start(dw.reshape(N_DEV, dw.shape[0] // N_DEV, dw.shape[1]), "exchange_start_" + name)
        started[name] = st
        return st[4]

    dx, packed, d_rel = _local_step(
        x[0], loss_target[0], mod, w_in_t, bias, weights_out_gu, weights_down, g_norm1, sinks[0], conv_w_full,
        g_attn_out, g_conv_out, g_norm2, g_final[None, :], exchange, gather_token)

    def zone(a):
        return lax.dynamic_update_slice(jnp.zeros((N_DEV,) + a.shape, F32), a[None], (me,) + (0,) * a.ndim)

    shared = _share_start([packed, d_rel], [zone(packed), zone(d_rel)], "share_small_start")

    def finish(name, after, w, m, v, tr):
        src, land = _exchange_wait(started[name], after, "exchange_wait_" + name)
        return _adamw_parts(w, m, v, src, land, me_arr, tr, "adamw_" + name)

    g_down, d_down, nm_down, nv_down = finish("w_down", [shared[2][0]], w_down[0], m_w_down[0], v_w_down[0], 176)
    g_gu, d_gu, nm_gu, nv_gu = finish("w_gu", [nv_down], w_gu[0].T, m_w_gu[0].T, v_w_gu[0].T, 352)
    g_out, d_out, nm_out, nv_out = finish("w_out", [nv_gu], w_out[0], m_w_out[0], v_w_out[0], 128)

    (packed_all, rel_all), (g_ada, d_ada, nm_ada, nv_ada) = _share_wait_adamw_w_ada(
        shared, [nv_out], me_arr, cond_all, w_ada[0], m_w_ada[0], v_w_ada[0], 256)
    as_rows = {"conv_w": lambda a: a.transpose(1, 0, 2), "g_final": lambda a: a[None, :], "rel_bias": lambda a: a.T}
    small_state = {
        "rel_bias": (rel_bias, m_rel_bias, v_rel_bias), "b_ada": (b_ada, m_b_ada, v_b_ada),
        "g_norm1": (g_norm1, m_g_norm1, v_g_norm1), "sinks": (sinks, m_sinks, v_sinks),
        "conv_w": (conv_w, m_conv_w, v_conv_w), "g_attn_out": (g_attn_out, m_g_attn_out, v_g_attn_out),
        "g_conv_out": (g_conv_out, m_g_conv_out, v_g_conv_out), "g_norm2": (g_norm2, m_g_norm2, v_g_norm2),
        "g_final": (g_final, m_g_final, v_g_final),
    }
    state = [tuple(as_rows.get(name, lambda a: a)(a) for a in small_state[name]) for name, _ in SMALL_PARAMS]
    loss_row, small_out = _small_update(me_arr, packed_all, rel_all, state, [])
    loss = loss_row[0, 0]
    back = {"rel_bias": lambda a: a.T, "conv_w": lambda a: a.transpose(1, 0, 2)}
    small_res = {name: tuple(back[name](a) if name in back else a.reshape(small_state[name][0].shape) for a in res)
                 for (name, _), res in zip(SMALL_PARAMS, small_out)}

    g_in, d_in, nm_in, nv_in = finish("w_in", [loss_row, nv_ada], w_in[0].T, m_w_in[0].T, v_w_in[0].T, 144)

    big = {
        "w_ada": (g_ada[None], d_ada[None], nm_ada[None], nv_ada[None]),
        "w_in": (g_in.T[None], d_in.T[None], nm_in.T[None], nv_in.T[None]),
        "w_out": (g_out[None], d_out[None], nm_out[None], nv_out[None]),
        "w_gu": (g_gu.T[None], d_gu.T[None], nm_gu.T[None], nv_gu.T[None]),
        "w_down": (g_down[None], d_down[None], nm_down[None], nv_down[None]),
    }
    order = ["rel_bias", "w_ada", "b_ada", "g_norm1", "w_in", "sinks", "conv_w", "g_attn_out", "g_conv_out", "w_out",
             "g_norm2", "w_gu", "w_down", "g_final"]
    results = [big[k] if k in big else small_res[k] for k in order]
    return (loss, dx[None], *[r[0] for r in results], *[r[1] for r in results], *[r[2] for r in results],
            *[r[3] for r in results])
```

```python
import math

import jax
import jax.numpy as jnp
import numpy as np
from jax import lax
from jax.experimental import pallas as pl
from jax.experimental.pallas import tpu as pltpu

F32 = jnp.float32
BF16 = jnp.bfloat16

D_MODEL = 1024
HEAD_DIM = 64
N_Q_HEADS = 8
ATTN_WIDTH = 512
KV_WIDTH = 128
CONV_WIDTH = 512
IN_PROJ_WIDTH = 2304
D_FF = 2816
N_MOD = 6
N_BUCKETS = 32
MAX_DISTANCE = 128
BLOCK = 128
REL_LANES = 128
EPS = 1e-6
NEG_INF = -1e30
SCALE = HEAD_DIM ** -0.5
N_DEV = 8

ADAM_LR = 0.001
ADAM_B1 = 0.9
ADAM_B2 = 0.999
ADAM_EPS = 1e-08
ADAM_WD = 0.01
ADAM_STEP = 10

SH1, SC1, G1, SH2, SC2, G2 = range(6)

VMEM_LIMIT_LARGE = 60 * 1024 * 1024
WEIGHT_GRAD_ROWS = 2048
FFN_CHUNKS = 1
PREV_ROWS = 16
MIXER_BLOCKS = 4
MESH_ID = pl.DeviceIdType.MESH

OFF_DMOD = 0
OFF_GN1 = OFF_DMOD + N_MOD * D_MODEL
OFF_SINK = OFF_GN1 + D_MODEL
OFF_GATT = OFF_SINK + 128
OFF_GCV = OFF_GATT + ATTN_WIDTH
OFF_GN2 = OFF_GCV + CONV_WIDTH
OFF_GFIN = OFF_GN2 + D_MODEL
OFF_CONVW = OFF_GFIN + D_MODEL
OFF_LOSS = OFF_CONVW + 3 * CONV_WIDTH
PACKED = OFF_LOSS + 128


def _params(sem=None, vmem=None):
    return pltpu.CompilerParams(dimension_semantics=sem, vmem_limit_bytes=vmem)


def _coming_behind(body):
    def skipping(after_ref, *refs):
        body(*refs)

    return skipping


ANY_SPEC = pl.BlockSpec(memory_space=pl.ANY)


def _full(shape):
    nd = len(shape)
    return pl.BlockSpec(shape, lambda *_: (0,) * nd)


def _rows(tm, width):
    return pl.BlockSpec((tm, width), lambda i, *_: (i, 0))


def _sigmoid(x):
    return 1.0 / (1.0 + jnp.exp(-x))


def _rsqrt_mean_sq(x):
    return lax.rsqrt(jnp.mean(x * x, axis=-1, keepdims=True) + EPS)


def _colsum(x):
    return jnp.sum(x, axis=0, keepdims=True)


def _dot(a, b):
    return jnp.dot(a, b, preferred_element_type=F32)


def _dot_nt(a, b):
    return lax.dot_general(a, b, (((1,), (1,)), ((), ())), preferred_element_type=F32)


def _dot_tn(a, b):
    return lax.dot_general(a, b, (((0,), (0,)), ((), ())), preferred_element_type=F32)


def _mesh_position():
    return lax.axis_index("x"), lax.axis_index("y"), lax.axis_index("c")


def _linear(p):
    return 4 * p[0] + 2 * p[1] + p[2]


def _peer(k):
    x, y, c = _mesh_position()
    return (1 - x if k & 4 else x, 1 - y if k & 2 else y, 1 - c if k & 1 else c)


HBM_SPEC = pl.BlockSpec(memory_space=pltpu.HBM)
SEM_SPEC = pl.BlockSpec(memory_space=pltpu.SEMAPHORE)
DATAFLOW = pltpu.SideEffectType.DATAFLOW_SIDE_EFFECTING


def _exchange_start(src, name):
    r, c = src.shape[1:]

    def body(src_ref, land_ref, send_sems, recv_sems, src_thru, land_thru, token):
        for k in range(1, N_DEV):
            peer = _peer(k)
            pltpu.make_async_remote_copy(
                src_ref=src_ref.at[_linear(peer)], dst_ref=land_ref.at[k - 1],
                send_sem=send_sems.at[k - 1], recv_sem=recv_sems.at[k - 1],
                device_id=peer, device_id_type=MESH_ID).start()
        token[...] = jnp.zeros_like(token)

    land = lax.empty((N_DEV - 1, r, c), src.dtype)
    return pl.pallas_call(
        body, name=name,
        out_shape=(pltpu.SemaphoreType.DMA((N_DEV - 1,)), pltpu.SemaphoreType.DMA((N_DEV - 1,)),
                   pltpu.HBM(src.shape, src.dtype), pltpu.HBM(land.shape, land.dtype),
                   jax.ShapeDtypeStruct((8, 128), F32)),
        in_specs=(HBM_SPEC, HBM_SPEC),
        out_specs=(SEM_SPEC, SEM_SPEC, HBM_SPEC, HBM_SPEC, pl.BlockSpec(memory_space=pltpu.VMEM)),
        input_output_aliases={0: 2, 1: 3},
        compiler_params=pltpu.CompilerParams(has_side_effects=DATAFLOW),
    )(pltpu.with_memory_space_constraint(src, pltpu.HBM), pltpu.with_memory_space_constraint(land, pltpu.HBM))


def _exchange_wait(started, after, name):
    send_sems, recv_sems, src_thru, land_thru, _ = started

    def body(src_ref, land_ref, send_sems, recv_sems, *rest):
        for k in range(1, N_DEV):
            cp = pltpu.make_async_remote_copy(
                src_ref=src_ref.at[0], dst_ref=land_ref.at[k - 1],
                send_sem=send_sems.at[k - 1], recv_sem=recv_sems.at[k - 1],
                device_id=_peer(k), device_id_type=MESH_ID)
            cp.wait_send()
            cp.wait_recv()

    return pl.pallas_call(
        body, name=name,
        out_shape=(pltpu.HBM(src_thru.shape, src_thru.dtype), pltpu.HBM(land_thru.shape, land_thru.dtype)),
        in_specs=(HBM_SPEC, HBM_SPEC, SEM_SPEC, SEM_SPEC) + (pl.BlockSpec(memory_space=pl.ANY),) * len(after),
        out_specs=(HBM_SPEC, HBM_SPEC), input_output_aliases={0: 0, 1: 1},
        compiler_params=pltpu.CompilerParams(has_side_effects=DATAFLOW),
    )(src_thru, land_thru, send_sems, recv_sems, *after)


def _share_start(arrs, zones, name):
    n = len(arrs)

    def body(*refs):
        src_refs, zone_refs, sems = refs[:n], refs[n:2 * n], refs[2 * n:4 * n]
        me = _linear(_mesh_position())
        for a in range(n):
            for k in range(1, N_DEV):
                pltpu.make_async_remote_copy(
                    src_ref=src_refs[a], dst_ref=zone_refs[a].at[me],
                    send_sem=sems[2 * a].at[k - 1], recv_sem=sems[2 * a + 1].at[k - 1],
                    device_id=_peer(k), device_id_type=MESH_ID).start()

    outs = pl.pallas_call(
        body, name=name,
        out_shape=tuple(pltpu.SemaphoreType.DMA((N_DEV - 1,)) for _ in range(2 * n))
        + tuple(pltpu.HBM(a.shape, a.dtype) for a in arrs) + tuple(pltpu.HBM(z.shape, z.dtype) for z in zones),
        in_specs=(HBM_SPEC,) * (2 * n),
        out_specs=(SEM_SPEC,) * (2 * n) + (HBM_SPEC,) * (2 * n),
        input_output_aliases={i: 2 * n + i for i in range(2 * n)},
        compiler_params=pltpu.CompilerParams(has_side_effects=DATAFLOW),
    )(*[pltpu.with_memory_space_constraint(a, pltpu.HBM) for a in list(arrs) + list(zones)])
    return outs[:2 * n], outs[2 * n:3 * n], outs[3 * n:]


def _same_core_peers():
    x, y, c = _mesh_position()
    return [(x, y, 1 - c), (1 - x, y, c), (x, 1 - y, c), (1 - x, 1 - y, c)]


def _gather_start(bufs, name):
    n = len(bufs)

    def body(*refs):
        buf_refs, rest = refs[:n], refs[n:]
        sems, token = rest[:2 * n], rest[-1]
        me = _linear(_mesh_position())
        for a in range(n):
            for k, peer in enumerate(_same_core_peers()):
                pltpu.make_async_remote_copy(
                    src_ref=buf_refs[a].at[me], dst_ref=buf_refs[a].at[me],
                    send_sem=sems[2 * a].at[k], recv_sem=sems[2 * a + 1].at[k],
                    device_id=peer, device_id_type=MESH_ID).start()
        token[...] = jnp.zeros_like(token)

    outs = pl.pallas_call(
        body, name=name,
        out_shape=tuple(pltpu.SemaphoreType.DMA((4,)) for _ in range(2 * n))
        + tuple(pltpu.HBM(b.shape, b.dtype) for b in bufs) + (jax.ShapeDtypeStruct((8, 128), F32),),
        in_specs=(HBM_SPEC,) * n,
        out_specs=(SEM_SPEC,) * (2 * n) + (HBM_SPEC,) * n + (pl.BlockSpec(memory_space=pltpu.VMEM),),
        input_output_aliases={a: 2 * n + a for a in range(n)},
        compiler_params=pltpu.CompilerParams(has_side_effects=DATAFLOW),
    )(*[pltpu.with_memory_space_constraint(b, pltpu.HBM) for b in bufs])
    return outs[:2 * n], outs[2 * n:3 * n], outs[3 * n]


def _gather_wait(sems, bufs, after, name):
    n = len(bufs)

    def body(*refs):
        buf_refs, sem_refs = refs[:n], refs[n:3 * n]
        x, y, c = _mesh_position()
        me = _linear((x, y, c))
        for a in range(n):
            for k, peer in enumerate(_same_core_peers()):
                cp = pltpu.make_async_remote_copy(
                    src_ref=buf_refs[a].at[me], dst_ref=buf_refs[a].at[_linear(peer)],
                    send_sem=sem_refs[2 * a].at[k], recv_sem=sem_refs[2 * a + 1].at[k],
                    device_id=peer, device_id_type=MESH_ID)
                cp.wait_send()
                cp.wait_recv()

    return list(pl.pallas_call(
        body, name=name,
        out_shape=tuple(pltpu.HBM(b.shape, b.dtype) for b in bufs),
        in_specs=(HBM_SPEC,) * n + (SEM_SPEC,) * (2 * n) + (pl.BlockSpec(memory_space=pl.ANY),) * len(after),
        out_specs=(HBM_SPEC,) * n, input_output_aliases={a: a for a in range(n)},
        compiler_params=pltpu.CompilerParams(has_side_effects=DATAFLOW),
    )(*bufs, *sems, *after))


def _gather_pass_on(bufs, name):
    n = len(bufs)

    def body(*refs):
        out_refs = refs[n:2 * n]
        send_sems, recv_sems = refs[2 * n:]
        x, y, c = _mesh_position()
        sibling = (x, y, 1 - c)
        chips = [(1 - x, y), (x, 1 - y), (1 - x, 1 - y)]
        copies = []
        for a in range(n):
            for j, chip in enumerate(chips):
                block = out_refs[a].at[_linear((*chip, c))]
                copies.append(pltpu.make_async_remote_copy(
                    src_ref=block, dst_ref=block, send_sem=send_sems.at[3 * a + j], recv_sem=recv_sems.at[3 * a + j],
                    device_id=sibling, device_id_type=MESH_ID))
                copies[-1].start()
        for a in range(n):
            for j, chip in enumerate(chips):
                copies[3 * a + j].wait_send()
                theirs = out_refs[a].at[_linear((*chip, 1 - c))]
                pltpu.make_async_remote_copy(
                    src_ref=theirs, dst_ref=theirs, send_sem=send_sems.at[3 * a + j], recv_sem=recv_sems.at[3 * a + j],
                    device_id=sibling, device_id_type=MESH_ID).wait_recv()

    hbm = pl.BlockSpec(memory_space=pl.ANY)
    return list(pl.pallas_call(
        body, name=name,
        out_shape=[jax.ShapeDtypeStruct(b.shape, b.dtype) for b in bufs],
        in_specs=[hbm] * n, out_specs=[hbm] * n, input_output_aliases={a: a for a in range(n)},
        scratch_shapes=[pltpu.SemaphoreType.DMA((3 * n,)), pltpu.SemaphoreType.DMA((3 * n,))],
    )(*bufs))


def _open_step(c, conv_w, w_ada, b_cols, w_in_t, later, rel_bias, bucket):
    cols = w_ada.shape[1]
    n_later = len(later)

    def body(c_ref, cw_ref, wa_ref, b_ref, w_ref, *rest):
        later_refs, rb_ref, bk_ref = rest[:n_later], rest[n_later], rest[n_later + 1]
        cond_ref, conv_ref, mod_ref, win_ref = rest[n_later + 2:n_later + 6]
        staged_refs, bias_ref, rows_ref = rest[n_later + 6:2 * n_later + 6], rest[2 * n_later + 6], rest[2 * n_later + 7]
        cond_own, mod_own, stage = rest[2 * n_later + 8:2 * n_later + 11]
        later_stage = rest[2 * n_later + 11:3 * n_later + 11]
        wa_own, later_own = rest[3 * n_later + 11], rest[3 * n_later + 12:4 * n_later + 12]
        s_send, s_recv, w_send, w_recv, local_sems, load_sems = rest[4 * n_later + 12:]
        x, y, cc = _mesh_position()
        me = _linear((x, y, cc))
        sibling = (x, y, 1 - cc)
        chips = [(1 - x, y), (x, 1 - y), (1 - x, 1 - y)]
        v = c_ref[...]
        cond_own[...] = v * _sigmoid(v)
        stage[...] = w_ref[...].astype(BF16)

        def small(rnd, a, k, src, dst, slot):
            return pltpu.make_async_remote_copy(
                src_ref=src, dst_ref=dst.at[slot], send_sem=s_send.at[rnd, a, k - 1], recv_sem=s_recv.at[rnd, a, k - 1],
                device_id=_peer(k), device_id_type=MESH_ID)

        def block(p):
            return win_ref.at[_linear(p)]

        def big(k, blk, to, src=None):
            return pltpu.make_async_remote_copy(
                src_ref=block(blk) if src is None else src, dst_ref=block(blk),
                send_sem=w_send.at[k], recv_sem=w_recv.at[k], device_id=to, device_id_type=MESH_ID)

        mine = [pltpu.make_async_copy(cond_own, cond_ref.at[me], local_sems.at[0]),
                pltpu.make_async_copy(cw_ref, conv_ref.at[me], local_sems.at[1]),
                pltpu.make_async_copy(stage, block((x, y, cc)), local_sems.at[2])]
        for cp in mine:
            cp.start()
        sends = []
        for k in range(1, N_DEV):
            sends += [small(0, 0, k, cond_own, cond_ref, me), small(0, 1, k, cw_ref, conv_ref, me)]
        for cp in sends:
            cp.start()
        first = [big(0, (x, y, cc), sibling, src=stage)]
        first += [big(1 + j, (x, y, cc), (*chip, cc), src=stage) for j, chip in enumerate(chips)]
        for cp in first:
            cp.start()
        loads = [pltpu.make_async_copy(wa_ref, wa_own, load_sems.at[0])]
        loads += [pltpu.make_async_copy(later_refs[a], later_own[a], load_sems.at[1 + a]) for a in range(n_later)]
        for cp in loads:
            cp.start()
        for a in range(n_later):
            loads[1 + a].wait()
            later_stage[a][...] = later_own[a][...].astype(BF16)
            mine.append(pltpu.make_async_copy(later_stage[a], staged_refs[a].at[me], local_sems.at[4 + a]))
            mine[-1].start()
        _fill_bias_table(rb_ref, bk_ref, bias_ref)
        for k in range(1, N_DEV):
            small(0, 0, k, cond_own, cond_ref, _linear(_peer(k))).wait_recv()
            small(0, 1, k, cw_ref, conv_ref, _linear(_peer(k))).wait_recv()
        mine[0].wait()
        cond_all = jnp.concatenate([cond_ref[k] for k in range(N_DEV)], axis=0)
        loads[0].wait()
        mod_own[...] = _dot(cond_all, wa_own[...]) + b_ref[me]
        mine.append(pltpu.make_async_copy(mod_own, mod_ref.at[me], local_sems.at[3]))
        mine[-1].start()
        second = [small(1, 0, k, mod_own, mod_ref, me) for k in range(1, N_DEV)]
        for cp in second:
            cp.start()
        passed = []
        for j, chip in enumerate(chips):
            big(1 + j, (*chip, cc), (x, y, cc)).wait_recv()
            fwd = big(4 + j, (*chip, cc), sibling)
            fwd.start()
            passed.append(fwd)
        big(0, sibling, (x, y, cc)).wait_recv()
        for j, chip in enumerate(chips):
            big(4 + j, (*chip, 1 - cc), (x, y, cc)).wait_recv()
        for k in range(1, N_DEV):
            small(1, 0, k, mod_own, mod_ref, _linear(_peer(k))).wait_recv()
        for cp in sends + first + second + passed:
            cp.wait_send()
        for cp in mine[1:]:
            cp.wait()
        flat = jnp.concatenate([mod_ref[j, pl.ds(me, 1), :] for j in range(N_DEV)], axis=1)
        rows_ref[...] = jnp.concatenate([flat[:, D_MODEL * r:D_MODEL * (r + 1)] for r in range(N_MOD)]
                                        + [jnp.zeros((8 - N_MOD, D_MODEL), F32)], axis=0)

    vmem = pl.BlockSpec(memory_space=pltpu.VMEM)
    outs = pl.pallas_call(
        body, name="open_step",
        out_shape=[jax.ShapeDtypeStruct((N_DEV,) + c.shape, F32), jax.ShapeDtypeStruct((N_DEV,) + conv_w.shape, F32),
                   jax.ShapeDtypeStruct((N_DEV, N_DEV, cols), F32),
                   jax.ShapeDtypeStruct((N_DEV,) + w_in_t.shape, BF16)]
        + [jax.ShapeDtypeStruct((N_DEV,) + a.shape, BF16) for a in later]
        + [jax.ShapeDtypeStruct((N_Q_HEADS, BLOCK, 2 * BLOCK), F32), jax.ShapeDtypeStruct((8, D_MODEL), F32)],
        in_specs=[vmem, vmem, ANY_SPEC, vmem, vmem] + [ANY_SPEC] * n_later + [pl.BlockSpec(memory_space=pltpu.SMEM), vmem],
        out_specs=[vmem, vmem, vmem, ANY_SPEC] + [ANY_SPEC] * n_later + [vmem, vmem],
        scratch_shapes=[pltpu.VMEM(c.shape, F32), pltpu.VMEM((N_DEV, cols), F32), pltpu.VMEM(w_in_t.shape, BF16)]
        + [pltpu.VMEM(a.shape, BF16) for a in later]
        + [pltpu.VMEM(w_ada.shape, F32)] + [pltpu.VMEM(a.shape, F32) for a in later]
        + [pltpu.SemaphoreType.DMA((2, 2, N_DEV - 1)), pltpu.SemaphoreType.DMA((2, 2, N_DEV - 1)),
           pltpu.SemaphoreType.DMA((7,)), pltpu.SemaphoreType.DMA((7,)),
           pltpu.SemaphoreType.DMA((4 + n_later,)), pltpu.SemaphoreType.DMA((1 + n_later,))],
        compiler_params=_params(vmem=VMEM_LIMIT_LARGE),
    )(c, conv_w, w_ada, b_cols, w_in_t, *later, rel_bias, bucket)
    return outs[0], outs[1], outs[5 + n_later], outs[3], list(outs[4:4 + n_later]), outs[4 + n_later]


def _in_proj(after, x, mod, g_norm1, w_in, tm):
    s = x.shape[0]

    def body(x_ref, mod_ref, g_ref, w_ref, h_ref, q_ref, kv_ref, gb_ref, gc_ref, xc_ref):
        xf = x_ref[...]
        n = xf * _rsqrt_mean_sq(xf) * g_ref[...]
        h = (n * (1.0 + mod_ref[SC1:SC1 + 1, :]) + mod_ref[SH1:SH1 + 1, :]).astype(BF16)
        h_ref[...] = h
        p = _dot_nt(h, w_ref[...])
        q_ref[...] = p[:, 0:512].astype(BF16)
        kv_ref[...] = p[:, 512:768].astype(BF16)
        gb_ref[...] = p[:, 768:1280].astype(BF16)
        gc_ref[...] = p[:, 1280:1792].astype(BF16)
        xc_ref[...] = p[:, 1792:2304].astype(BF16)

    return pl.pallas_call(
        _coming_behind(body), name="in_proj", grid=(s // tm,),
        in_specs=[ANY_SPEC, _rows(tm, D_MODEL), _full((8, D_MODEL)), _full((1, D_MODEL)), _full((IN_PROJ_WIDTH, D_MODEL))],
        out_specs=[_rows(tm, D_MODEL), _rows(tm, 512), _rows(tm, 256), _rows(tm, 512), _rows(tm, 512), _rows(tm, 512)],
        out_shape=[jax.ShapeDtypeStruct((s, D_MODEL), BF16), jax.ShapeDtypeStruct((s, 512), BF16),
                   jax.ShapeDtypeStruct((s, 256), BF16), jax.ShapeDtypeStruct((s, 512), BF16),
                   jax.ShapeDtypeStruct((s, 512), BF16), jax.ShapeDtypeStruct((s, 512), BF16)],
        compiler_params=_params(("arbitrary",), VMEM_LIMIT_LARGE),
    )(after, x, mod, g_norm1, w_in)


def _t5_bucket(dist):
    max_exact = N_BUCKETS // 2
    is_small = dist < max_exact
    d = np.maximum(dist, 1).astype(np.float32)
    large = max_exact + (np.log(d / max_exact) / math.log(MAX_DISTANCE / max_exact)
                         * (N_BUCKETS - max_exact)).astype(np.int32)
    large = np.minimum(large, N_BUCKETS - 1)
    return np.where(is_small, dist, large).astype(np.int32)


def _bucket_table():
    qi = np.arange(BLOCK, dtype=np.int32)[:, None]
    sj = np.arange(2 * BLOCK, dtype=np.int32)[None, :]
    return jnp.asarray(_t5_bucket(np.maximum(qi + BLOCK - sj, 0)))


def _window_mask():
    qi = lax.broadcasted_iota(jnp.int32, (BLOCK, 2 * BLOCK), 0)
    sj = lax.broadcasted_iota(jnp.int32, (BLOCK, 2 * BLOCK), 1)
    dist = qi + BLOCK - sj
    return (dist >= 0) & (dist < BLOCK)


def _fill_bias_table(rb_ref, bk_ref, o_ref):
    bk = bk_ref[...]
    inside = _window_mask()
    for h in range(N_Q_HEADS):
        acc = jnp.zeros((BLOCK, 2 * BLOCK), F32)
        for b in range(N_BUCKETS):
            acc = jnp.where(bk == b, rb_ref[h, b], acc)
        o_ref[h] = jnp.where(inside, acc, NEG_INF)


def _load_kv_window(kv_ref, n):
    prev = jnp.maximum(n - 1, 0)
    kvw = jnp.concatenate([kv_ref[pl.ds(pl.multiple_of(prev * BLOCK, BLOCK), BLOCK), :],
                           kv_ref[pl.ds(pl.multiple_of(n * BLOCK, BLOCK), BLOCK), :]], axis=0)
    k, v = kvw[:, 0:128], kvw[:, 128:256]
    k_sw = pltpu.roll(k.astype(F32), 64, 1).astype(BF16)
    v_sw = pltpu.roll(v.astype(F32), 64, 1).astype(BF16)
    return (k, k_sw), (v, v_sw)


def _conv_taps(gc, xc, gc_prev, xc_prev, n):
    u = gc * xc
    before = jnp.where(n > 0, gc_prev.astype(F32) * xc_prev.astype(F32), 0.0)
    last = before.shape[0] - 1
    row = lax.broadcasted_iota(jnp.int32, u.shape, 0)
    u1 = jnp.where(row == 0, before[last:last + 1, :], pltpu.roll(u, 1, 0))
    u2 = jnp.where(row == 0, before[last - 1:last, :],
                   jnp.where(row == 1, before[last:last + 1, :], pltpu.roll(u, 2, 0)))
    return u, u1, u2


def _mixer_fwd(q, kv, gb, gc, xc, bias, sinks, conv_w, g_attn, g_conv):
    s = q.shape[0]
    nb = s // BLOCK

    per_step = min(MIXER_BLOCKS, nb)
    tile = per_step * BLOCK

    def one_block(n, slot, before, sink_ref, q_ref, kv_ref, gb_ref, gc_ref, xc_ref, bias_ref, cw_ref, ga_ref,
                  gcv_ref, attn_ref, merged_ref, lse_ref, p_ref):
        rows = slice(slot * BLOCK, (slot + 1) * BLOCK)
        ks, vs = _load_kv_window(kv_ref, n)
        lane = lax.broadcasted_iota(jnp.int32, (BLOCK, BLOCK), 1)
        low = lane < HEAD_DIM
        col = lax.broadcasted_iota(jnp.int32, (BLOCK, 2 * BLOCK), 1)
        no_prev = (col < BLOCK) & (n == 0)
        lse_all = jnp.zeros((BLOCK, BLOCK), F32)
        pairs = []
        for p in range(4):
            qp = q_ref[rows, 128 * p:128 * (p + 1)].astype(F32)
            kvh = p // 2
            res = []
            for e in range(2):
                h = 2 * p + e
                qm = jnp.where(low if e == 0 else ~low, qp, 0.0).astype(BF16)
                sw = 0 if kvh == e else 1
                sc = _dot_nt(qm, ks[sw]) * SCALE + bias_ref[h]
                sc = jnp.where(no_prev, NEG_INF, sc)
                sink = sink_ref[h]
                m = jnp.maximum(jnp.max(sc, axis=-1, keepdims=True), sink)
                pe = jnp.exp(sc - m)
                den = jnp.sum(pe, axis=-1, keepdims=True) + jnp.exp(sink - m)
                pb = (pe * (1.0 / den)).astype(BF16)
                p_ref[slot, h] = pb
                res.append(_dot(pb, vs[sw]))
                lse_all = lse_all + jnp.where(lane == h, m + jnp.log(den), 0.0)
            pairs.append(jnp.where(low, res[0], res[1]))
        attn = jnp.concatenate(pairs, axis=1)
        attn_ref[rows, :] = attn
        lse_ref[rows, :] = lse_all
        u, u1, u2 = _conv_taps(gc_ref[rows, :].astype(F32), xc_ref[rows, :].astype(F32), before[0], before[1], n)
        cw = cw_ref[...]
        cv = gb_ref[rows, :].astype(F32) * (cw[0:1, :] * u2 + cw[1:2, :] * u1 + cw[2:3, :] * u)
        an = attn * _rsqrt_mean_sq(attn) * ga_ref[...]
        cn = cv * _rsqrt_mean_sq(cv) * gcv_ref[...]
        merged_ref[rows, :] = jnp.concatenate([an, cn], axis=1).astype(BF16)

    def body(sink_ref, q_ref, kv_ref, gb_ref, gc_ref, xc_ref, gcp_ref, xcp_ref, *rest):
        step = pl.program_id(0)
        for sub in range(per_step):
            ahead = slice(sub * BLOCK - PREV_ROWS, sub * BLOCK)
            before = (gcp_ref[...], xcp_ref[...]) if sub == 0 else (gc_ref[ahead, :], xc_ref[ahead, :])
            one_block(step * per_step + sub, sub, before, sink_ref, q_ref, kv_ref, gb_ref, gc_ref, xc_ref, *rest)

    blk = lambda w: pl.BlockSpec((tile, w), lambda n: (n, 0))
    prev8 = pl.BlockSpec((PREV_ROWS, 512), lambda n: (jnp.maximum(n * (tile // PREV_ROWS) - 1, 0), 0))
    return pl.pallas_call(
        body, name="mixer_fwd", grid=(nb // per_step,),
        in_specs=[pl.BlockSpec(memory_space=pltpu.SMEM), blk(512), _full((s, 256)), blk(512), blk(512), blk(512),
                  prev8, prev8, _full((N_Q_HEADS, BLOCK, 2 * BLOCK)), _full((3, 512)), _full((1, 512)),
                  _full((1, 512))],
        out_specs=[blk(512), blk(1024), blk(128),
                   pl.BlockSpec((per_step, N_Q_HEADS, BLOCK, 2 * BLOCK), lambda n: (n, 0, 0, 0))],
        out_shape=[jax.ShapeDtypeStruct((s, 512), F32), jax.ShapeDtypeStruct((s, 1024), BF16),
                   jax.ShapeDtypeStruct((s, 128), F32),
                   jax.ShapeDtypeStruct((nb, N_Q_HEADS, BLOCK, 2 * BLOCK), BF16)],
        compiler_params=_params(("arbitrary",)),
    )(sinks, q, kv, gb, gc, xc, gc, xc, bias, conv_w, g_attn, g_conv)


def _out_proj(merged, x, mod, w_out, tm):
    s = x.shape[0]

    def body(m_ref, x_ref, mod_ref, w_ref, o_ref, x1_ref):
        o = _dot(m_ref[...], w_ref[...])
        o_ref[...] = o.astype(BF16)
        x1_ref[...] = x_ref[...] + mod_ref[G1:G1 + 1, :] * o

    return pl.pallas_call(
        body, name="out_proj", grid=(s // tm,),
        in_specs=[_rows(tm, D_MODEL), _rows(tm, D_MODEL), _full((8, D_MODEL)), _full((D_MODEL, D_MODEL))],
        out_specs=[_rows(tm, D_MODEL), _rows(tm, D_MODEL)],
        out_shape=[jax.ShapeDtypeStruct((s, D_MODEL), BF16), jax.ShapeDtypeStruct((s, D_MODEL), F32)],
        compiler_params=_params(("arbitrary",)),
    )(merged, x, mod, w_out)


def _resident(shape):
    nd = len(shape)
    return pl.BlockSpec(shape, lambda *_: (0,) * nd, pipeline_mode=pl.Buffered(1))


def _ffn(x1, o1, merged, mod, g_norm2, w_gu, w_down, w_out, g_final, target, tm):
    s = x1.shape[0]
    chunk = D_FF // FFN_CHUNKS

    def body(x_ref, o1_ref, mg_ref, mod_ref, g_ref, wgu_ref, wd_ref, wo_ref, gf_ref, t_ref,
             h_ref, act_ref, do_ref, dgu_ref, dx1_ref, dwo_ref, dm_ref, small_ref, dwo_acc):
        @pl.when(pl.program_id(0) == 0)
        def _():
            small_ref[...] = jnp.zeros_like(small_ref)
            dwo_acc[...] = jnp.zeros_like(dwo_acc)

        xf = x_ref[...]
        n = xf * _rsqrt_mean_sq(xf) * g_ref[...]
        h = (n * (1.0 + mod_ref[SC2:SC2 + 1, :]) + mod_ref[SH2:SH2 + 1, :]).astype(BF16)
        h_ref[...] = h
        gates, ups, o = [], [], None
        for j in range(FFN_CHUNKS):
            lo = j * chunk
            gate = _dot_nt(h, wgu_ref[lo:lo + chunk, :])
            up = _dot_nt(h, wgu_ref[D_FF + lo:D_FF + lo + chunk, :])
            sg = _sigmoid(gate)
            act = (gate * sg * up).astype(BF16)
            act_ref[:, lo:lo + chunk] = act
            gates.append((up * (sg * (1.0 + gate * (1.0 - sg)))).astype(BF16))
            ups.append((gate * sg).astype(BF16))
            part = _dot(act, wd_ref[lo:lo + chunk, :])
            o = part if o is None else o + part
        g2 = mod_ref[G2:G2 + 1, :]
        x2 = xf + g2 * o
        r = _rsqrt_mean_sq(x2)
        xn = x2 * r
        gf = gf_ref[...]
        err = xn * gf - t_ref[...]
        dy = err * (1.0 / D_MODEL)
        dxn = dy * gf
        dx2 = r * (dxn - xn * jnp.mean(dxn * xn, axis=-1, keepdims=True))
        small_ref[4:5, :] += _colsum(dy * xn)
        small_ref[5:6, :] += _colsum(err * err)
        small_ref[3:4, :] += _colsum(dx2 * o)
        do = (dx2 * g2).astype(BF16)
        do_ref[...] = do
        dh = None
        for j in range(FFN_CHUNKS):
            lo = j * chunk
            dact = _dot_nt(do, wd_ref[lo:lo + chunk, :])
            dgate = (dact * gates[j].astype(F32)).astype(BF16)
            dup = (dact * ups[j].astype(F32)).astype(BF16)
            dgu_ref[:, lo:lo + chunk] = dgate
            dgu_ref[:, D_FF + lo:D_FF + lo + chunk] = dup
            part = _dot(dgate, wgu_ref[lo:lo + chunk, :]) + _dot(dup, wgu_ref[D_FF + lo:D_FF + lo + chunk, :])
            dh = part if dh is None else dh + part
        dx1 = dx2 + _norm_mod_bwd(dh, xf, g_ref[...], mod_ref[SC2:SC2 + 1, :], small_ref)
        dx1_ref[...] = dx1.astype(BF16)
        small_ref[7:8, :] += _colsum(dx1 * o1_ref[...].astype(F32))
        do1 = (dx1 * mod_ref[G1:G1 + 1, :]).astype(BF16)
        dm_ref[...] = _dot_nt(do1, wo_ref[...]).astype(BF16)
        dwo = dwo_acc[...] + _dot_tn(mg_ref[...], do1)
        dwo_acc[...] = dwo
        dwo_ref[...] = dwo.astype(BF16)

        @pl.when(pl.program_id(0) == pl.num_programs(0) - 1)
        def _():
            total = jnp.sum(small_ref[5:6, :], axis=-1, keepdims=True) * (0.5 / D_MODEL)
            small_ref[6:7, :] = jnp.broadcast_to(total, (1, D_MODEL))

    narrow = jax.ShapeDtypeStruct((s, D_MODEL), BF16)
    return pl.pallas_call(
        body, name="ffn", grid=(s // tm,),
        in_specs=[_rows(tm, D_MODEL), _rows(tm, D_MODEL), _rows(tm, D_MODEL), _full((8, D_MODEL)), _full((1, D_MODEL)),
                  _resident((2 * D_FF, D_MODEL)), _resident((D_FF, D_MODEL)), _resident((D_MODEL, D_MODEL)),
                  _full((1, D_MODEL)), _rows(tm, D_MODEL)],
        out_specs=[_rows(tm, D_MODEL), _rows(tm, D_FF), _rows(tm, D_MODEL), _rows(tm, 2 * D_FF), _rows(tm, D_MODEL),
                   _full((D_MODEL, D_MODEL)), _rows(tm, D_MODEL), _full((8, D_MODEL))],
        out_shape=[narrow, jax.ShapeDtypeStruct((s, D_FF), BF16), narrow, jax.ShapeDtypeStruct((s, 2 * D_FF), BF16),
                   narrow, jax.ShapeDtypeStruct((D_MODEL, D_MODEL), BF16), narrow,
                   jax.ShapeDtypeStruct((8, D_MODEL), F32)],
        scratch_shapes=[pltpu.VMEM((D_MODEL, D_MODEL), F32)],
        compiler_params=_params(("arbitrary",), VMEM_LIMIT_LARGE),
    )(x1, o1, merged, mod, g_norm2, w_gu, w_down, w_out, g_final, target)


def _norm_mod_bwd(dh, xf, g, scale_row, small_ref):
    r = _rsqrt_mean_sq(xf)
    xn = xf * r
    small_ref[0:1, :] += _colsum(dh)
    small_ref[1:2, :] += _colsum(dh * (xn * g))
    dn = dh * (1.0 + scale_row)
    small_ref[2:3, :] += _colsum(dn * xn)
    dxn = dn * g
    return r * (dxn - xn * jnp.mean(dxn * xn, axis=-1, keepdims=True))


def _group_norm_bwd(dm, a, g):
    r = _rsqrt_mean_sq(a)
    an = a * r
    dan = dm * g
    return r * (dan - an * jnp.mean(dan * an, axis=-1, keepdims=True)), _colsum(dm * an)


def _sum_by_bucket(db_ref, bk_ref, o_ref, rows_ref):
    bk = bk_ref[...]
    for b in range(N_BUCKETS):
        sel = (bk == b).astype(F32)
        for h in range(N_Q_HEADS):
            rows_ref[N_BUCKETS * h + b:N_BUCKETS * h + b + 1, :] = _colsum(db_ref[h] * sel)
    head = lax.broadcasted_iota(jnp.int32, (N_BUCKETS, REL_LANES), 1)
    out = jnp.zeros((N_BUCKETS, REL_LANES), F32)
    for h in range(N_Q_HEADS):
        per_bucket = jnp.sum(rows_ref[N_BUCKETS * h:N_BUCKETS * (h + 1), :], axis=-1, keepdims=True)
        out = out + jnp.where(head == h, per_bucket, 0.0)
    o_ref[...] = out


def _mixer_bwd(after, q, kv, gb, gc, xc, probs, sinks, conv_w, g_attn, g_conv, attn, lse, dmerged, bucket):
    s = q.shape[0]
    nb = s // BLOCK

    per_step = min(MIXER_BLOCKS, nb)
    tile = per_step * BLOCK
    steps = nb // per_step

    def one_block(n, slot, before, nxt, sink_ref, q_ref, kv_ref, gb_ref, gc_ref, xc_ref, p_ref, cw_ref, ga_ref,
                  gcv_ref, attn_ref, lse_ref, dm_ref, dproj_ref, dbias_ref, dsink_ref, small_ref):
        rows = slice(slot * BLOCK, (slot + 1) * BLOCK)
        next_dy, next_dkv = nxt
        dm = dm_ref[rows, :].astype(F32)
        gbv, gcv_, xcv = gb_ref[rows, :].astype(F32), gc_ref[rows, :].astype(F32), xc_ref[rows, :].astype(F32)
        u, u1, u2 = _conv_taps(gcv_, xcv, before[0], before[1], n)
        cw = cw_ref[...]
        yv = cw[0:1, :] * u2 + cw[1:2, :] * u1 + cw[2:3, :] * u
        dcv, dg_conv = _group_norm_bwd(dm[:, 512:1024], gbv * yv, gcv_ref[...])
        small_ref[1:2, :] += dg_conv
        dproj_ref[rows, 768:1280] = (dcv * yv).astype(BF16)
        dy = dcv * gbv
        row = lax.broadcasted_iota(jnp.int32, dy.shape, 0)
        d1 = jnp.where(row == BLOCK - 1, next_dy[0:1, :], pltpu.roll(dy, BLOCK - 1, 0))
        d2 = jnp.where(row == BLOCK - 2, next_dy[0:1, :],
                       jnp.where(row == BLOCK - 1, next_dy[1:2, :], pltpu.roll(dy, BLOCK - 2, 0)))
        du = cw[2:3, :] * dy + cw[1:2, :] * d1 + cw[0:1, :] * d2
        dproj_ref[rows, 1280:1792] = (du * xcv).astype(BF16)
        dproj_ref[rows, 1792:2304] = (du * gcv_).astype(BF16)
        small_ref[2:3, :] += _colsum(dy * u2)
        small_ref[3:4, :] += _colsum(dy * u1)
        small_ref[4:5, :] += _colsum(dy * u)

        attn_v = attn_ref[rows, :]
        dout, dg_attn = _group_norm_bwd(dm[:, 0:512], attn_v, ga_ref[...])
        small_ref[0:1, :] += dg_attn
        ks, vs = _load_kv_window(kv_ref, n)
        lane = lax.broadcasted_iota(jnp.int32, (BLOCK, BLOCK), 1)
        low = lane < HEAD_DIM
        lse_all = lse_ref[rows, :]
        dsink = jnp.zeros((BLOCK, BLOCK), F32)
        dq_pairs = []
        dk_groups, dv_groups = [], []
        for kvh in range(2):
            ds_rows, pr_rows, q_rows, do_rows = [], [], [], []
            for p in (2 * kvh, 2 * kvh + 1):
                qp = q_ref[rows, 128 * p:128 * (p + 1)].astype(F32)
                do_p = dout[:, 128 * p:128 * (p + 1)]
                prod = do_p * attn_v[:, 128 * p:128 * (p + 1)]
                res = []
                for e in range(2):
                    h = 2 * p + e
                    half = low if e == 0 else ~low
                    qm = jnp.where(half, qp, 0.0).astype(BF16)
                    dom = jnp.where(half, do_p, 0.0).astype(BF16)
                    delta = jnp.sum(jnp.where(half, prod, 0.0), axis=-1, keepdims=True)
                    lse_h = jnp.sum(jnp.where(lane == h, lse_all, 0.0), axis=-1, keepdims=True)
                    sw = 0 if kvh == e else 1
                    pb = p_ref[slot, h]
                    dp = _dot_nt(dom, vs[sw])
                    ds = pb.astype(F32) * (dp - delta)
                    dbias_ref[h] += ds
                    dsink = dsink + jnp.where(lane == h, -jnp.exp(sink_ref[h] - lse_h) * delta, 0.0)
                    dsb = ds.astype(BF16)
                    res.append(_dot(dsb, ks[sw]) * SCALE)
                    ds_rows.append(dsb)
                    pr_rows.append(pb)
                    q_rows.append(qm)
                    do_rows.append(dom)
                dq_pairs.append(jnp.where(low, res[0], res[1]))
            dk_g = _dot_tn(jnp.concatenate(ds_rows, axis=0), jnp.concatenate(q_rows, axis=0)) * SCALE
            dv_g = _dot_tn(jnp.concatenate(pr_rows, axis=0), jnp.concatenate(do_rows, axis=0))
            dk_groups.append(dk_g + pltpu.roll(dk_g, 64, 1))
            dv_groups.append(dv_g + pltpu.roll(dv_g, 64, 1))
        dproj_ref[rows, 0:512] = jnp.concatenate(dq_pairs, axis=1).astype(BF16)
        dsink_ref[...] += dsink
        low_kv = lax.broadcasted_iota(jnp.int32, (2 * BLOCK, BLOCK), 1) < HEAD_DIM
        dkv_win = jnp.concatenate([jnp.where(low_kv, dk_groups[0], dk_groups[1]),
                                   jnp.where(low_kv, dv_groups[0], dv_groups[1])], axis=1)
        dproj_ref[rows, 512:768] = (dkv_win[BLOCK:2 * BLOCK, :] + next_dkv).astype(BF16)
        return dy[0:8, :], dkv_win[0:BLOCK, :]

    def body(sink_ref, q_ref, kv_ref, gb_ref, gc_ref, xc_ref, gcp_ref, xcp_ref, p_ref, cw_ref, ga_ref, gcv_ref,
             attn_ref, lse_ref, dm_ref, bk_ref, dproj_ref, drel_ref, dsink_ref, small_ref,
             dy_ref, dkv_ref, dbias_ref, rows_ref):
        refs = (p_ref, cw_ref, ga_ref, gcv_ref, attn_ref, lse_ref, dm_ref, dproj_ref, dbias_ref, dsink_ref, small_ref)
        step = pl.program_id(0)

        @pl.when(step == 0)
        def _():
            dbias_ref[...] = jnp.zeros_like(dbias_ref)
            dsink_ref[...] = jnp.zeros_like(dsink_ref)
            small_ref[...] = jnp.zeros_like(small_ref)
            dy_ref[...] = jnp.zeros_like(dy_ref)
            dkv_ref[...] = jnp.zeros_like(dkv_ref)

        nxt = (dy_ref[...], dkv_ref[...])
        for sub in reversed(range(per_step)):
            ahead = slice(sub * BLOCK - PREV_ROWS, sub * BLOCK)
            before = (gcp_ref[...], xcp_ref[...]) if sub == 0 else (gc_ref[ahead, :], xc_ref[ahead, :])
            nxt = one_block((steps - 1 - step) * per_step + sub, sub, before, nxt,
                            sink_ref, q_ref, kv_ref, gb_ref, gc_ref, xc_ref, *refs)
        dy_ref[...], dkv_ref[...] = nxt

        @pl.when(step == steps - 1)
        def _():
            small_ref[5:6, :] = jnp.concatenate([_colsum(dsink_ref[...]), jnp.zeros((1, 512 - BLOCK), F32)], axis=1)
            _sum_by_bucket(dbias_ref, bk_ref, drel_ref, rows_ref)

    blk = lambda w: pl.BlockSpec((tile, w), lambda t: (steps - 1 - t, 0))
    prev8 = pl.BlockSpec((PREV_ROWS, 512),
                         lambda t: (jnp.maximum((steps - 1 - t) * (tile // PREV_ROWS) - 1, 0), 0))
    bf = lambda w: jax.ShapeDtypeStruct((s, w), BF16)
    return pl.pallas_call(
        _coming_behind(body), name="mixer_bwd", grid=(steps,),
        in_specs=[ANY_SPEC, pl.BlockSpec(memory_space=pltpu.SMEM), blk(512), _full((s, 256)), blk(512), blk(512), blk(512),
                  prev8, prev8,
                  pl.BlockSpec((per_step, N_Q_HEADS, BLOCK, 2 * BLOCK), lambda t: (steps - 1 - t, 0, 0, 0)),
                  _full((3, 512)), _full((1, 512)), _full((1, 512)), blk(512), blk(128), blk(1024),
                  _full((BLOCK, 2 * BLOCK))],
        out_specs=[blk(IN_PROJ_WIDTH), _full((N_BUCKETS, REL_LANES)), _full((BLOCK, BLOCK)), _full((8, 512))],
        out_shape=[bf(IN_PROJ_WIDTH), jax.ShapeDtypeStruct((N_BUCKETS, REL_LANES), F32),
                   jax.ShapeDtypeStruct((BLOCK, BLOCK), F32), jax.ShapeDtypeStruct((8, 512), F32)],
        scratch_shapes=[pltpu.VMEM((8, 512), F32), pltpu.VMEM((BLOCK, 2 * KV_WIDTH), F32),
                        pltpu.VMEM((N_Q_HEADS, BLOCK, 2 * BLOCK), F32),
                        pltpu.VMEM((N_BUCKETS * N_Q_HEADS, 2 * BLOCK), F32)],
        compiler_params=_params(("arbitrary",), VMEM_LIMIT_LARGE),
    )(after, sinks, q, kv, gb, gc, xc, gc, xc, probs, conv_w, g_attn, g_conv, attn, lse, dmerged, bucket)


def _in_proj_bwd(after, dproj, x, dx1, mod, g_norm1, w_in, tm):
    s = x.shape[0]

    def body(dproj_ref, x_ref, dx1_ref, mod_ref, g_ref, w_ref, dx_ref, small_ref):
        @pl.when(pl.program_id(0) == 0)
        def _():
            small_ref[...] = jnp.zeros_like(small_ref)

        dh = _dot(dproj_ref[...], w_ref[...])
        dx_ref[...] = dx1_ref[...].astype(F32) + _norm_mod_bwd(dh, x_ref[...], g_ref[...], mod_ref[SC1:SC1 + 1, :],
                                                               small_ref)

    return pl.pallas_call(
        _coming_behind(body), name="in_proj_bwd", grid=(s // tm,),
        in_specs=[ANY_SPEC, _rows(tm, IN_PROJ_WIDTH), _rows(tm, D_MODEL), _rows(tm, D_MODEL), _full((8, D_MODEL)),
                  _full((1, D_MODEL)), _full((IN_PROJ_WIDTH, D_MODEL))],
        out_specs=[_rows(tm, D_MODEL), _full((8, D_MODEL))],
        out_shape=[jax.ShapeDtypeStruct((s, D_MODEL), F32), jax.ShapeDtypeStruct((8, D_MODEL), F32)],
        compiler_params=_params(("arbitrary",), VMEM_LIMIT_LARGE),
    )(after, dproj, x, dx1, mod, g_norm1, w_in)


def _weight_grad(a, b, tk, ts, name, after=None):
    s, k = a.shape
    n = b.shape[1]
    nt = s // ts
    extra = [] if after is None else [after]

    def body(a_ref, b_ref, *rest):
        o_ref, acc_ref = rest[-2:]
        t = pl.program_id(1)
        @pl.when(t == 0)
        def _():
            acc_ref[...] = jnp.zeros_like(acc_ref)

        acc = acc_ref[...] + _dot_tn(a_ref[...], b_ref[...])
        acc_ref[...] = acc
        o_ref[...] = acc.astype(BF16)

    return pl.pallas_call(
        body, name=name, grid=(k // tk, nt),
        in_specs=[pl.BlockSpec((ts, tk), lambda i, t: (t, i)), pl.BlockSpec((ts, n), lambda i, t: (t, 0))]
        + [ANY_SPEC] * len(extra),
        out_specs=pl.BlockSpec((tk, n), lambda i, t: (i, 0)),
        out_shape=jax.ShapeDtypeStruct((k, n), BF16),
        scratch_shapes=[pltpu.VMEM((tk, n), F32)],
        compiler_params=_params(("arbitrary", "arbitrary"), VMEM_LIMIT_LARGE),
    )(a, b, *extra)


def _lanes_from(x, start, width):
    n = x.shape[1]
    return pltpu.roll(x, (n - start) % n, 1)[:, 0:width]


def _share_wait_adamw_w_ada(started, after, me, cond_all, w, m, v, tr):
    sems, arrs, zones = started
    n = len(arrs)
    r, cols = w.shape
    first = 4 * n + len(after)

    def body(me_ref, *refs):
        src_refs, zone_refs, sem_refs = refs[:n], refs[n:2 * n], refs[2 * n:4 * n]
        c_ref, w_ref, m_ref, v_ref = refs[first:first + 4]
        g_ref, d_ref, mo_ref, vo_ref, p_ref, load_sem = refs[first + 4 + 2 * n:]
        for a in range(n):
            for k in range(1, N_DEV):
                cp = pltpu.make_async_remote_copy(
                    src_ref=src_refs[a], dst_ref=zone_refs[a].at[_linear(_peer(k))],
                    send_sem=sem_refs[2 * a].at[k - 1], recv_sem=sem_refs[2 * a + 1].at[k - 1],
                    device_id=_peer(k), device_id_type=MESH_ID)
                cp.wait_send()
                cp.wait_recv()
        load = pltpu.make_async_copy(zone_refs[0], p_ref, load_sem.at[0])
        load.start()
        load.wait()
        dmod = jnp.concatenate([p_ref[k][:, OFF_DMOD:OFF_DMOD + N_MOD * D_MODEL] for k in range(N_DEV)], axis=0)
        pad = lambda a: jnp.concatenate([a, jnp.zeros((128 - N_DEV, a.shape[1]), F32)], axis=0)
        mine = pad(_lanes_from(dmod, me_ref[0] * cols, cols))
        for i in range(r // tr):
            rows = slice(tr * i, tr * (i + 1))
            cond = jnp.concatenate([c_ref[k][:, rows] for k in range(N_DEV)], axis=0)
            g = _dot_tn(pad(cond), mine)
            g_ref[rows, :] = g
            d_ref[rows, :], mo_ref[rows, :], vo_ref[rows, :] = _adam_math(
                w_ref[rows, :], g, m_ref[rows, :], v_ref[rows, :])

    vmem = pl.BlockSpec(memory_space=pltpu.VMEM)
    outs = pl.pallas_call(
        body, name="share_small_wait_adamw_w_ada",
        out_shape=tuple(pltpu.HBM(a.shape, a.dtype) for a in arrs) + tuple(pltpu.HBM(z.shape, z.dtype) for z in zones)
        + (jax.ShapeDtypeStruct((r, cols), F32),) * 4,
        in_specs=(pl.BlockSpec(memory_space=pltpu.SMEM),) + (HBM_SPEC,) * (2 * n) + (SEM_SPEC,) * (2 * n)
        + (ANY_SPEC,) * len(after) + (vmem,) * 4,
        out_specs=(HBM_SPEC,) * (2 * n) + (vmem,) * 4,
        input_output_aliases={1 + i: i for i in range(2 * n)},
        scratch_shapes=[pltpu.VMEM(zones[0].shape, F32), pltpu.SemaphoreType.DMA((1,))],
        compiler_params=pltpu.CompilerParams(has_side_effects=DATAFLOW, vmem_limit_bytes=VMEM_LIMIT_LARGE),
    )(me, *arrs, *zones, *sems, *after, cond_all, w, m, v)
    return list(outs[n:2 * n]), tuple(outs[2 * n:])


SMALL_PARAMS = (("rel_bias", None), ("b_ada", (OFF_DMOD, N_MOD * D_MODEL)), ("g_norm1", (OFF_GN1, D_MODEL)),
                ("sinks", (OFF_SINK, N_Q_HEADS)), ("conv_w", None), ("g_attn_out", (OFF_GATT, ATTN_WIDTH)),
                ("g_conv_out", (OFF_GCV, CONV_WIDTH)), ("g_norm2", (OFF_GN2, D_MODEL)),
                ("g_final", (OFF_GFIN, D_MODEL)))


def _small_update(me, packed_all, rel_all, state, after):
    n_p = len(SMALL_PARAMS)
    flat = [a for triple in state for a in triple]
    conv_cols = state[4][0].shape[-1]

    def body(me_ref, p_ref, r_ref, *refs):
        ins = refs[:3 * n_p]
        loss_ref, outs = refs[3 * n_p + len(after)], refs[3 * n_p + len(after) + 1:]
        small, rel = p_ref[0], r_ref[0]
        for k in range(1, N_DEV):
            small = small + p_ref[k]
            rel = rel + r_ref[k]
        rel = jnp.concatenate([rel, jnp.zeros((REL_LANES - N_BUCKETS, REL_LANES), F32)], axis=0).T
        rel = rel[0:N_Q_HEADS, 0:N_BUCKETS]
        loss_ref[...] = small[:, OFF_LOSS:OFF_LOSS + 128]
        taps = jnp.concatenate([small[:, OFF_CONVW + CONV_WIDTH * j:OFF_CONVW + CONV_WIDTH * (j + 1)]
                                for j in range(3)] + [jnp.zeros((5, CONV_WIDTH), F32)], axis=0)
        conv_g = _lanes_from(taps, me_ref[0] * conv_cols, conv_cols)[0:3, :]
        for i, (name, lanes) in enumerate(SMALL_PARAMS):
            w_ref, m_ref, v_ref = ins[3 * i:3 * i + 3]
            if name == "conv_w":
                for j in range(3):
                    outs[4 * i][j] = conv_g[j:j + 1, :]
                    outs[4 * i + 1][j], outs[4 * i + 2][j], outs[4 * i + 3][j] = _adam_math(
                        w_ref[j], conv_g[j:j + 1, :], m_ref[j], v_ref[j])
                continue
            g = rel if name == "rel_bias" else small[:, lanes[0]:lanes[0] + lanes[1]]
            outs[4 * i][...] = g
            outs[4 * i + 1][...], outs[4 * i + 2][...], outs[4 * i + 3][...] = _adam_math(
                w_ref[...], g, m_ref[...], v_ref[...])

    vmem = pl.BlockSpec(memory_space=pltpu.VMEM)
    out_shape = [jax.ShapeDtypeStruct((1, 128), F32)]
    for w, _, _ in state:
        out_shape += [jax.ShapeDtypeStruct(w.shape, F32)] * 4
    outs = pl.pallas_call(
        body, name="small_update",
        in_specs=[pl.BlockSpec(memory_space=pltpu.SMEM), vmem, vmem] + [vmem] * len(flat)
        + [pl.BlockSpec(memory_space=pl.ANY)] * len(after),
        out_shape=out_shape,
    )(me, packed_all, rel_all, *flat, *after)
    return outs[0], [tuple(outs[1 + 4 * i:5 + 4 * i]) for i in range(n_p)]


def _adam_math(w, g, m, v):
    m = ADAM_B1 * m + (1.0 - ADAM_B1) * g
    v = ADAM_B2 * v + (1.0 - ADAM_B2) * (g * g)
    m_hat = m / (1.0 - ADAM_B1 ** ADAM_STEP)
    v_hat = v / (1.0 - ADAM_B2 ** ADAM_STEP)
    delta = -ADAM_LR * (m_hat / (jnp.sqrt(v_hat) + ADAM_EPS) + ADAM_WD * w)
    return delta, m, v


def _adamw_parts(w, m, v, local, land, me, tr, name):
    r, c = w.shape

    def body(me_ref, w_ref, m_ref, v_ref, own_ref, land_ref, g_ref, d_ref, mo_ref, vo_ref):
        g = own_ref[0].astype(F32)
        for k in range(N_DEV - 1):
            g = g + land_ref[k].astype(F32)
        g_ref[...] = g
        d_ref[...], mo_ref[...], vo_ref[...] = _adam_math(w_ref[...], g, m_ref[...], v_ref[...])

    tile = pl.BlockSpec((tr, c), lambda i, me_ref: (i, 0))
    return pl.pallas_call(
        body, name=name,
        grid_spec=pltpu.PrefetchScalarGridSpec(
            num_scalar_prefetch=1, grid=(r // tr,),
            in_specs=[tile, tile, tile, pl.BlockSpec((1, tr, c), lambda i, me_ref: (me_ref[0], i, 0)),
                      pl.BlockSpec((N_DEV - 1, tr, c), lambda i, me_ref: (0, i, 0))],
            out_specs=[tile] * 4),
        out_shape=[jax.ShapeDtypeStruct((r, c), F32)] * 4,
        compiler_params=_params(("arbitrary",)),
    )(me, w, m, v, local, land)


def _local_step(x, target, mod, w_in_t, bias, weights_out_gu, weights_down, g_norm1, sinks, conv_w, g_attn,
                g_conv, g_norm2, g_final, exchange, start_after):
    s = x.shape[0]
    tm = min(512, s)
    tm_small = min(256, s)
    bucket = _bucket_table()

    h, q, kv, gb, gc, xc = _in_proj(start_after, x, mod, g_norm1, w_in_t, tm)
    attn, merged, lse, probs = _mixer_fwd(q, kv, gb, gc, xc, bias, sinks, conv_w, g_attn, g_conv)
    w_out, w_gu_t = weights_out_gu(merged)
    o1, x1 = _out_proj(merged, x, mod, w_out, tm)
    w_down = weights_down(x1)
    h2, act, do2, dgu, dx1, dw_out, dmerged, sm_2 = _ffn(x1, o1, merged, mod, g_norm2, w_gu_t, w_down, w_out, g_final,
                                                         target, tm_small)
    ts = min(WEIGHT_GRAD_ROWS, s)
    tok_out = exchange("w_out", dw_out)
    tok_down = exchange("w_down", _weight_grad(act, do2, D_FF // 2, ts, "w_down_grad", after=tok_out))
    tok_gu = exchange("w_gu", _weight_grad(dgu, h2, D_FF // 2, ts, "w_gu_grad", after=tok_down))
    dproj, d_rel, dsink, sm_mix = _mixer_bwd(
        tok_gu, q, kv, gb, gc, xc, probs, sinks, conv_w, g_attn, g_conv, attn, lse, dmerged, bucket)
    tok_in = exchange("w_in", _weight_grad(dproj, h, IN_PROJ_WIDTH // 2, ts, "w_in_grad"))
    dx, sm_1 = _in_proj_bwd(tok_in, dproj, x, dx1, mod, g_norm1, w_in_t, min(1024, s))

    packed = jnp.concatenate([
        sm_1[0:1], sm_1[1:2], sm_2[7:8], sm_2[0:1], sm_2[1:2], sm_2[3:4],
        sm_1[2:3],
        sm_mix[5:6, 0:128],
        sm_mix[0:1], sm_mix[1:2],
        sm_2[2:3],
        sm_2[4:5],
        sm_mix[2:3], sm_mix[3:4], sm_mix[4:5],
        sm_2[6:7, 0:128],
    ], axis=1)
    return dx, packed, d_rel


def kernel(x, c, rel_bias, w_ada, b_ada, g_norm1, w_in, sinks, conv_w, g_attn_out, g_conv_out, w_out, g_norm2, w_gu, w_down, g_final, loss_target, m_rel_bias, m_w_ada, m_b_ada, m_g_norm1, m_w_in, m_sinks, m_conv_w, m_g_attn_out, m_g_conv_out, m_w_out, m_g_norm2, m_w_gu, m_w_down, m_g_final, v_rel_bias, v_w_ada, v_b_ada, v_g_norm1, v_w_in, v_sinks, v_conv_w, v_g_attn_out, v_g_conv_out, v_w_out, v_g_norm2, v_w_gu, v_w_down, v_g_final):
    me = _linear(_mesh_position())
    me_arr = jnp.reshape(me, (1,)).astype(jnp.int32)
    ada_cols = w_ada.shape[2]
    tm = min(512, x.shape[1])

    b_cols = b_ada.reshape(N_DEV, 1, ada_cols)
    cond_all, conv_w_all, mod, w_in_blocks, staged, bias = _open_step(
        c, conv_w.transpose(1, 0, 2), w_ada[0], b_cols, w_in[0].T, [w_out[0], w_gu[0].T, w_down[0]], rel_bias.T, _bucket_table())
    conv_w_full = conv_w_all.reshape(N_DEV, 3, -1).transpose(1, 0, 2).reshape(3, CONV_WIDTH)
    w_in_t = w_in_blocks.reshape(IN_PROJ_WIDTH, D_MODEL)
    gather_sems, staged, gather_token = _gather_start(staged, "gather_start_weights")

    def weights_out_gu(after):
        got = _gather_pass_on(_gather_wait(gather_sems[0:4], staged[0:2], [after], "gather_wait_out_gu"),
                              "gather_pass_on_out_gu")
        return got[0].reshape(D_MODEL, D_MODEL), got[1].reshape(2 * D_FF, D_MODEL)

    def weights_down(after):
        got = _gather_pass_on(_gather_wait(gather_sems[4:6], staged[2:3], [after], "gather_wait_down"),
                              "gather_pass_on_down")
        return got[0].reshape(D_FF, D_MODEL)

    started = {}

    def exchange(name, dw):
        st = _exchange_start(dw.reshape(N_DEV, dw.shape[0] // N_DEV, dw.shape[1]), "exchange_start_" + name)
        started[name] = st
        return st[4]

    dx, packed, d_rel = _local_step(
        x[0], loss_target[0], mod, w_in_t, bias, weights_out_gu, weights_down, g_norm1, sinks[0], conv_w_full,
        g_attn_out, g_conv_out, g_norm2, g_final[None, :], exchange, gather_token)

    def zone(a):
        return lax.dynamic_update_slice(jnp.zeros((N_DEV,) + a.shape, F32), a[None], (me,) + (0,) * a.ndim)

    shared = _share_start([packed, d_rel], [zone(packed), zone(d_rel)], "share_small_start")

    def finish(name, after, w, m, v, tr):
        src, land = _exchange_wait(started[name], after, "exchange_wait_" + name)
        return _adamw_parts(w, m, v, src, land, me_arr, tr, "adamw_" + name)

    g_down, d_down, nm_down, nv_down = finish("w_down", [shared[2][0]], w_down[0], m_w_down[0], v_w_down[0], 176)
    g_gu, d_gu, nm_gu, nv_gu = finish("w_gu", [nv_down], w_gu[0].T, m_w_gu[0].T, v_w_gu[0].T, 352)
    g_out, d_out, nm_out, nv_out = finish("w_out", [nv_gu], w_out[0], m_w_out[0], v_w_out[0], 128)

    (packed_all, rel_all), (g_ada, d_ada, nm_ada, nv_ada) = _share_wait_adamw_w_ada(
        shared, [nv_out], me_arr, cond_all, w_ada[0], m_w_ada[0], v_w_ada[0], 256)
    as_rows = {"conv_w": lambda a: a.transpose(1, 0, 2), "g_final": lambda a: a[None, :], "rel_bias": lambda a: a.T}
    small_state = {
        "rel_bias": (rel_bias, m_rel_bias, v_rel_bias), "b_ada": (b_ada, m_b_ada, v_b_ada),
        "g_norm1": (g_norm1, m_g_norm1, v_g_norm1), "sinks": (sinks, m_sinks, v_sinks),
        "conv_w": (conv_w, m_conv_w, v_conv_w), "g_attn_out": (g_attn_out, m_g_attn_out, v_g_attn_out),
        "g_conv_out": (g_conv_out, m_g_conv_out, v_g_conv_out), "g_norm2": (g_norm2, m_g_norm2, v_g_norm2),
        "g_final": (g_final, m_g_final, v_g_final),
    }
    state = [tuple(as_rows.get(name, lambda a: a)(a) for a in small_state[name]) for name, _ in SMALL_PARAMS]
    loss_row, small_out = _small_update(me_arr, packed_all, rel_all, state, [])
    loss = loss_row[0, 0]
    back = {"rel_bias": lambda a: a.T, "conv_w": lambda a: a.transpose(1, 0, 2)}
    small_res = {name: tuple(back[name](a) if name in back else a.reshape(small_state[name][0].shape) for a in res)
                 for (name, _), res in zip(SMALL_PARAMS, small_out)}

    g_in, d_in, nm_in, nv_in = finish("w_in", [loss_row, nv_ada], w_in[0].T, m_w_in[0].T, v_w_in[0].T, 144)

    big = {
        "w_ada": (g_ada[None], d_ada[None], nm_ada[None], nv_ada[None]),
        "w_in": (g_in.T[None], d_in.T[None], nm_in.T[None], nv_in.T[None]),
        "w_out": (g_out[None], d_out[None], nm_out[None], nv_out[None]),
        "w_gu": (g_gu.T[None], d_gu.T[None], nm_gu.T[None], nv_gu.T[None]),
        "w_down": (g_down[None], d_down[None], nm_down[None], nv_down[None]),
    }
    order = ["rel_bias", "w_ada", "b_ada", "g_norm1", "w_in", "sinks", "conv_w", "g_attn_out", "g_conv_out", "w_out",
             "g_norm2", "w_gu", "w_down", "g_final"]
    results = [big[k] if k in big else small_res[k] for k in order]
    return (loss, dx[None], *[r[0] for r in results], *[r[1] for r in results], *[r[2] for r in results],
            *[r[3] for r in results])
```

```python
import math

import jax
import jax.numpy as jnp
import numpy as np
from jax import lax
from jax.experimental import pallas as pl
from jax.experimental.pallas import tpu as pltpu

F32 = jnp.float32
BF16 = jnp.bfloat16

D_MODEL = 1024
HEAD_DIM = 64
N_Q_HEADS = 8
ATTN_WIDTH = 512
KV_WIDTH = 128
CONV_WIDTH = 512
IN_PROJ_WIDTH = 2304
D_FF = 2816
N_MOD = 6
N_BUCKETS = 32
MAX_DISTANCE = 128
BLOCK = 128
REL_LANES = 128
EPS = 1e-6
NEG_INF = -1e30
SCALE = HEAD_DIM ** -0.5
N_DEV = 8

ADAM_LR = 0.001
ADAM_B1 = 0.9
ADAM_B2 = 0.999
ADAM_EPS = 1e-08
ADAM_WD = 0.01
ADAM_STEP = 10

SH1, SC1, G1, SH2, SC2, G2 = range(6)

VMEM_LIMIT_LARGE = 60 * 1024 * 1024
WEIGHT_GRAD_ROWS = 2048
FFN_CHUNKS = 1
PREV_ROWS = 16
MIXER_BLOCKS = 4
MESH_ID = pl.DeviceIdType.MESH

OFF_DMOD = 0
OFF_GN1 = OFF_DMOD + N_MOD * D_MODEL
OFF_SINK = OFF_GN1 + D_MODEL
OFF_GATT = OFF_SINK + 128
OFF_GCV = OFF_GATT + ATTN_WIDTH
OFF_GN2 = OFF_GCV + CONV_WIDTH
OFF_GFIN = OFF_GN2 + D_MODEL
OFF_CONVW = OFF_GFIN + D_MODEL
OFF_LOSS = OFF_CONVW + 3 * CONV_WIDTH
PACKED = OFF_LOSS + 128


def _params(sem=None, vmem=None):
    return pltpu.CompilerParams(dimension_semantics=sem, vmem_limit_bytes=vmem)


def _coming_behind(body):
    def skipping(after_ref, *refs):
        body(*refs)

    return skipping


ANY_SPEC = pl.BlockSpec(memory_space=pl.ANY)


def _full(shape):
    nd = len(shape)
    return pl.BlockSpec(shape, lambda *_: (0,) * nd)


def _rows(tm, width):
    return pl.BlockSpec((tm, width), lambda i, *_: (i, 0))


def _sigmoid(x):
    return 1.0 / (1.0 + jnp.exp(-x))


def _rsqrt_mean_sq(x):
    return lax.rsqrt(jnp.mean(x * x, axis=-1, keepdims=True) + EPS)


def _colsum(x):
    return jnp.sum(x, axis=0, keepdims=True)


def _dot(a, b):
    return jnp.dot(a, b, preferred_element_type=F32)


def _dot_nt(a, b):
    return lax.dot_general(a, b, (((1,), (1,)), ((), ())), preferred_element_type=F32)


def _dot_tn(a, b):
    return lax.dot_general(a, b, (((0,), (0,)), ((), ())), preferred_element_type=F32)


def _mesh_position():
    return lax.axis_index("x"), lax.axis_index("y"), lax.axis_index("c")


def _linear(p):
    return 4 * p[0] + 2 * p[1] + p[2]


def _peer(k):
    x, y, c = _mesh_position()
    return (1 - x if k & 4 else x, 1 - y if k & 2 else y, 1 - c if k & 1 else c)


HBM_SPEC = pl.BlockSpec(memory_space=pltpu.HBM)
SEM_SPEC = pl.BlockSpec(memory_space=pltpu.SEMAPHORE)
DATAFLOW = pltpu.SideEffectType.DATAFLOW_SIDE_EFFECTING


def _exchange_start(src, name):
    r, c = src.shape[1:]

    def body(src_ref, land_ref, send_sems, recv_sems, src_thru, land_thru, token):
        for k in range(1, N_DEV):
            peer = _peer(k)
            pltpu.make_async_remote_copy(
                src_ref=src_ref.at[_linear(peer)], dst_ref=land_ref.at[k - 1],
                send_sem=send_sems.at[k - 1], recv_sem=recv_sems.at[k - 1],
                device_id=peer, device_id_type=MESH_ID).start()
        token[...] = jnp.zeros_like(token)

    land = lax.empty((N_DEV - 1, r, c), src.dtype)
    return pl.pallas_call(
        body, name=name,
        out_shape=(pltpu.SemaphoreType.DMA((N_DEV - 1,)), pltpu.SemaphoreType.DMA((N_DEV - 1,)),
                   pltpu.HBM(src.shape, src.dtype), pltpu.HBM(land.shape, land.dtype),
                   jax.ShapeDtypeStruct((8, 128), F32)),
        in_specs=(HBM_SPEC, HBM_SPEC),
        out_specs=(SEM_SPEC, SEM_SPEC, HBM_SPEC, HBM_SPEC, pl.BlockSpec(memory_space=pltpu.VMEM)),
        input_output_aliases={0: 2, 1: 3},
        compiler_params=pltpu.CompilerParams(has_side_effects=DATAFLOW),
    )(pltpu.with_memory_space_constraint(src, pltpu.HBM), pltpu.with_memory_space_constraint(land, pltpu.HBM))


def _exchange_wait(started, after, name):
    send_sems, recv_sems, src_thru, land_thru, _ = started

    def body(src_ref, land_ref, send_sems, recv_sems, *rest):
        for k in range(1, N_DEV):
            cp = pltpu.make_async_remote_copy(
                src_ref=src_ref.at[0], dst_ref=land_ref.at[k - 1],
                send_sem=send_sems.at[k - 1], recv_sem=recv_sems.at[k - 1],
                device_id=_peer(k), device_id_type=MESH_ID)
            cp.wait_send()
            cp.wait_recv()

    return pl.pallas_call(
        body, name=name,
        out_shape=(pltpu.HBM(src_thru.shape, src_thru.dtype), pltpu.HBM(land_thru.shape, land_thru.dtype)),
        in_specs=(HBM_SPEC, HBM_SPEC, SEM_SPEC, SEM_SPEC) + (pl.BlockSpec(memory_space=pl.ANY),) * len(after),
        out_specs=(HBM_SPEC, HBM_SPEC), input_output_aliases={0: 0, 1: 1},
        compiler_params=pltpu.CompilerParams(has_side_effects=DATAFLOW),
    )(src_thru, land_thru, send_sems, recv_sems, *after)


def _share_start(arrs, zones, name):
    n = len(arrs)

    def body(*refs):
        src_refs, zone_refs, sems = refs[:n], refs[n:2 * n], refs[2 * n:4 * n]
        me = _linear(_mesh_position())
        for a in range(n):
            for k in range(1, N_DEV):
                pltpu.make_async_remote_copy(
                    src_ref=src_refs[a], dst_ref=zone_refs[a].at[me],
                    send_sem=sems[2 * a].at[k - 1], recv_sem=sems[2 * a + 1].at[k - 1],
                    device_id=_peer(k), device_id_type=MESH_ID).start()

    outs = pl.pallas_call(
        body, name=name,
        out_shape=tuple(pltpu.SemaphoreType.DMA((N_DEV - 1,)) for _ in range(2 * n))
        + tuple(pltpu.HBM(a.shape, a.dtype) for a in arrs) + tuple(pltpu.HBM(z.shape, z.dtype) for z in zones),
        in_specs=(HBM_SPEC,) * (2 * n),
        out_specs=(SEM_SPEC,) * (2 * n) + (HBM_SPEC,) * (2 * n),
        input_output_aliases={i: 2 * n + i for i in range(2 * n)},
        compiler_params=pltpu.CompilerParams(has_side_effects=DATAFLOW),
    )(*[pltpu.with_memory_space_constraint(a, pltpu.HBM) for a in list(arrs) + list(zones)])
    return outs[:2 * n], outs[2 * n:3 * n], outs[3 * n:]


def _same_core_peers():
    x, y, c = _mesh_position()
    return [(x, y, 1 - c), (1 - x, y, c), (x, 1 - y, c), (1 - x, 1 - y, c)]


def _gather_start(bufs, name):
    n = len(bufs)

    def body(*refs):
        buf_refs, rest = refs[:n], refs[n:]
        sems, token = rest[:2 * n], rest[-1]
        me = _linear(_mesh_position())
        for a in range(n):
            for k, peer in enumerate(_same_core_peers()):
                pltpu.make_async_remote_copy(
                    src_ref=buf_refs[a].at[me], dst_ref=buf_refs[a].at[me],
                    send_sem=sems[2 * a].at[k], recv_sem=sems[2 * a + 1].at[k],
                    device_id=peer, device_id_type=MESH_ID).start()
        token[...] = jnp.zeros_like(token)

    outs = pl.pallas_call(
        body, name=name,
        out_shape=tuple(pltpu.SemaphoreType.DMA((4,)) for _ in range(2 * n))
        + tuple(pltpu.HBM(b.shape, b.dtype) for b in bufs) + (jax.ShapeDtypeStruct((8, 128), F32),),
        in_specs=(HBM_SPEC,) * n,
        out_specs=(SEM_SPEC,) * (2 * n) + (HBM_SPEC,) * n + (pl.BlockSpec(memory_space=pltpu.VMEM),),
        input_output_aliases={a: 2 * n + a for a in range(n)},
        compiler_params=pltpu.CompilerParams(has_side_effects=DATAFLOW),
    )(*[pltpu.with_memory_space_constraint(b, pltpu.HBM) for b in bufs])
    return outs[:2 * n], outs[2 * n:3 * n], outs[3 * n]


def _gather_wait(sems, bufs, after, name):
    n = len(bufs)

    def body(*refs):
        buf_refs, sem_refs = refs[:n], refs[n:3 * n]
        x, y, c = _mesh_position()
        me = _linear((x, y, c))
        for a in range(n):
            for k, peer in enumerate(_same_core_peers()):
                cp = pltpu.make_async_remote_copy(
                    src_ref=buf_refs[a].at[me], dst_ref=buf_refs[a].at[_linear(peer)],
                    send_sem=sem_refs[2 * a].at[k], recv_sem=sem_refs[2 * a + 1].at[k],
                    device_id=peer, device_id_type=MESH_ID)
                cp.wait_send()
                cp.wait_recv()

    return list(pl.pallas_call(
        body, name=name,
        out_shape=tuple(pltpu.HBM(b.shape, b.dtype) for b in bufs),
        in_specs=(HBM_SPEC,) * n + (SEM_SPEC,) * (2 * n) + (pl.BlockSpec(memory_space=pl.ANY),) * len(after),
        out_specs=(HBM_SPEC,) * n, input_output_aliases={a: a for a in range(n)},
        compiler_params=pltpu.CompilerParams(has_side_effects=DATAFLOW),
    )(*bufs, *sems, *after))


def _gather_pass_on(bufs, name):
    n = len(bufs)

    def body(*refs):
        out_refs = refs[n:2 * n]
        send_sems, recv_sems = refs[2 * n:]
        x, y, c = _mesh_position()
        sibling = (x, y, 1 - c)
        chips = [(1 - x, y), (x, 1 - y), (1 - x, 1 - y)]
        copies = []
        for a in range(n):
            for j, chip in enumerate(chips):
                block = out_refs[a].at[_linear((*chip, c))]
                copies.append(pltpu.make_async_remote_copy(
                    src_ref=block, dst_ref=block, send_sem=send_sems.at[3 * a + j], recv_sem=recv_sems.at[3 * a + j],
                    device_id=sibling, device_id_type=MESH_ID))
                copies[-1].start()
        for a in range(n):
            for j, chip in enumerate(chips):
                copies[3 * a + j].wait_send()
                theirs = out_refs[a].at[_linear((*chip, 1 - c))]
                pltpu.make_async_remote_copy(
                    src_ref=theirs, dst_ref=theirs, send_sem=send_sems.at[3 * a + j], recv_sem=recv_sems.at[3 * a + j],
                    device_id=sibling, device_id_type=MESH_ID).wait_recv()

    hbm = pl.BlockSpec(memory_space=pl.ANY)
    return list(pl.pallas_call(
        body, name=name,
        out_shape=[jax.ShapeDtypeStruct(b.shape, b.dtype) for b in bufs],
        in_specs=[hbm] * n, out_specs=[hbm] * n, input_output_aliases={a: a for a in range(n)},
        scratch_shapes=[pltpu.SemaphoreType.DMA((3 * n,)), pltpu.SemaphoreType.DMA((3 * n,))],
    )(*bufs))


def _open_step(c, conv_w, w_ada, b_cols, w_in_t, later, rel_bias, bucket):
    cols = w_ada.shape[1]
    n_later = len(later)

    def body(c_ref, cw_ref, wa_ref, b_ref, w_ref, *rest):
        later_refs, rb_ref, bk_ref = rest[:n_later], rest[n_later], rest[n_later + 1]
        cond_ref, conv_ref, mod_ref, win_ref = rest[n_later + 2:n_later + 6]
        staged_refs, bias_ref, rows_ref = rest[n_later + 6:2 * n_later + 6], rest[2 * n_later + 6], rest[2 * n_later + 7]
        cond_own, mod_own, stage = rest[2 * n_later + 8:2 * n_later + 11]
        later_stage = rest[2 * n_later + 11:3 * n_later + 11]
        wa_own, later_own = rest[3 * n_later + 11], rest[3 * n_later + 12:4 * n_later + 12]
        s_send, s_recv, w_send, w_recv, local_sems, load_sems = rest[4 * n_later + 12:]
        x, y, cc = _mesh_position()
        me = _linear((x, y, cc))
        sibling = (x, y, 1 - cc)
        chips = [(1 - x, y), (x, 1 - y), (1 - x, 1 - y)]
        v = c_ref[...]
        cond_own[...] = v * _sigmoid(v)
        stage[...] = w_ref[...].astype(BF16)

        def small(rnd, a, k, src, dst, slot):
            return pltpu.make_async_remote_copy(
                src_ref=src, dst_ref=dst.at[slot], send_sem=s_send.at[rnd, a, k - 1], recv_sem=s_recv.at[rnd, a, k - 1],
                device_id=_peer(k), device_id_type=MESH_ID)

        def block(p):
            return win_ref.at[_linear(p)]

        def big(k, blk, to, src=None):
            return pltpu.make_async_remote_copy(
                src_ref=block(blk) if src is None else src, dst_ref=block(blk),
                send_sem=w_send.at[k], recv_sem=w_recv.at[k], device_id=to, device_id_type=MESH_ID)

        mine = [pltpu.make_async_copy(cond_own, cond_ref.at[me], local_sems.at[0]),
                pltpu.make_async_copy(cw_ref, conv_ref.at[me], local_sems.at[1]),
                pltpu.make_async_copy(stage, block((x, y, cc)), local_sems.at[2])]
        for cp in mine:
            cp.start()
        sends = []
        for k in range(1, N_DEV):
            sends += [small(0, 0, k, cond_own, cond_ref, me), small(0, 1, k, cw_ref, conv_ref, me)]
        for cp in sends:
            cp.start()
        first = [big(0, (x, y, cc), sibling, src=stage)]
        first += [big(1 + j, (x, y, cc), (*chip, cc), src=stage) for j, chip in enumerate(chips)]
        for cp in first:
            cp.start()
        loads = [pltpu.make_async_copy(wa_ref, wa_own, load_sems.at[0])]
        loads += [pltpu.make_async_copy(later_refs[a], later_own[a], load_sems.at[1 + a]) for a in range(n_later)]
        for cp in loads:
            cp.start()
        for a in range(n_later):
            loads[1 + a].wait()
            later_stage[a][...] = later_own[a][...].astype(BF16)
            mine.append(pltpu.make_async_copy(later_stage[a], staged_refs[a].at[me], local_sems.at[4 + a]))
            mine[-1].start()
        _fill_bias_table(rb_ref, bk_ref, bias_ref)
        for k in range(1, N_DEV):
            small(0, 0, k, cond_own, cond_ref, _linear(_peer(k))).wait_recv()
            small(0, 1, k, cw_ref, conv_ref, _linear(_peer(k))).wait_recv()
        mine[0].wait()
        cond_all = jnp.concatenate([cond_ref[k] for k in range(N_DEV)], axis=0)
        loads[0].wait()
        mod_own[...] = _dot(cond_all, wa_own[...]) + b_ref[me]
        mine.append(pltpu.make_async_copy(mod_own, mod_ref.at[me], local_sems.at[3]))
        mine[-1].start()
        second = [small(1, 0, k, mod_own, mod_ref, me) for k in range(1, N_DEV)]
        for cp in second:
            cp.start()
        passed = []
        for j, chip in enumerate(chips):
            big(1 + j, (*chip, cc), (x, y, cc)).wait_recv()
            fwd = big(4 + j, (*chip, cc), sibling)
            fwd.start()
            passed.append(fwd)
        big(0, sibling, (x, y, cc)).wait_recv()
        for j, chip in enumerate(chips):
            big(4 + j, (*chip, 1 - cc), (x, y, cc)).wait_recv()
        for k in range(1, N_DEV):
            small(1, 0, k, mod_own, mod_ref, _linear(_peer(k))).wait_recv()
        for cp in sends + first + second + passed:
            cp.wait_send()
        for cp in mine[1:]:
            cp.wait()
        flat = jnp.concatenate([mod_ref[j, pl.ds(me, 1), :] for j in range(N_DEV)], axis=1)
        rows_ref[...] = jnp.concatenate([flat[:, D_MODEL * r:D_MODEL * (r + 1)] for r in range(N_MOD)]
                                        + [jnp.zeros((8 - N_MOD, D_MODEL), F32)], axis=0)

    vmem = pl.BlockSpec(memory_space=pltpu.VMEM)
    outs = pl.pallas_call(
        body, name="open_step",
        out_shape=[jax.ShapeDtypeStruct((N_DEV,) + c.shape, F32), jax.ShapeDtypeStruct((N_DEV,) + conv_w.shape, F32),
                   jax.ShapeDtypeStruct((N_DEV, N_DEV, cols), F32),
                   jax.ShapeDtypeStruct((N_DEV,) + w_in_t.shape, BF16)]
        + [jax.ShapeDtypeStruct((N_DEV,) + a.shape, BF16) for a in later]
        + [jax.ShapeDtypeStruct((N_Q_HEADS, BLOCK, 2 * BLOCK), F32), jax.ShapeDtypeStruct((8, D_MODEL), F32)],
        in_specs=[vmem, vmem, ANY_SPEC, vmem, vmem] + [ANY_SPEC] * n_later + [pl.BlockSpec(memory_space=pltpu.SMEM), vmem],
        out_specs=[vmem, vmem, vmem, ANY_SPEC] + [ANY_SPEC] * n_later + [vmem, vmem],
        scratch_shapes=[pltpu.VMEM(c.shape, F32), pltpu.VMEM((N_DEV, cols), F32), pltpu.VMEM(w_in_t.shape, BF16)]
        + [pltpu.VMEM(a.shape, BF16) for a in later]
        + [pltpu.VMEM(w_ada.shape, F32)] + [pltpu.VMEM(a.shape, F32) for a in later]
        + [pltpu.SemaphoreType.DMA((2, 2, N_DEV - 1)), pltpu.SemaphoreType.DMA((2, 2, N_DEV - 1)),
           pltpu.SemaphoreType.DMA((7,)), pltpu.SemaphoreType.DMA((7,)),
           pltpu.SemaphoreType.DMA((4 + n_later,)), pltpu.SemaphoreType.DMA((1 + n_later,))],
        compiler_params=_params(vmem=VMEM_LIMIT_LARGE),
    )(c, conv_w, w_ada, b_cols, w_in_t, *later, rel_bias, bucket)
    return outs[0], outs[1], outs[5 + n_later], outs[3], list(outs[4:4 + n_later]), outs[4 + n_later]


def _in_proj(after, x, mod, g_norm1, w_in, tm):
    s = x.shape[0]

    def body(x_ref, mod_ref, g_ref, w_ref, h_ref, q_ref, kv_ref, gb_ref, gc_ref, xc_ref):
        xf = x_ref[...]
        n = xf * _rsqrt_mean_sq(xf) * g_ref[...]
        h = (n * (1.0 + mod_ref[SC1:SC1 + 1, :]) + mod_ref[SH1:SH1 + 1, :]).astype(BF16)
        h_ref[...] = h
        p = _dot_nt(h, w_ref[...])
        q_ref[...] = p[:, 0:512].astype(BF16)
        kv_ref[...] = p[:, 512:768].astype(BF16)
        gb_ref[...] = p[:, 768:1280].astype(BF16)
        gc_ref[...] = p[:, 1280:1792].astype(BF16)
        xc_ref[...] = p[:, 1792:2304].astype(BF16)

    return pl.pallas_call(
        _coming_behind(body), name="in_proj", grid=(s // tm,),
        in_specs=[ANY_SPEC, _rows(tm, D_MODEL), _full((8, D_MODEL)), _full((1, D_MODEL)), _full((IN_PROJ_WIDTH, D_MODEL))],
        out_specs=[_rows(tm, D_MODEL), _rows(tm, 512), _rows(tm, 256), _rows(tm, 512), _rows(tm, 512), _rows(tm, 512)],
        out_shape=[jax.ShapeDtypeStruct((s, D_MODEL), BF16), jax.ShapeDtypeStruct((s, 512), BF16),
                   jax.ShapeDtypeStruct((s, 256), BF16), jax.ShapeDtypeStruct((s, 512), BF16),
                   jax.ShapeDtypeStruct((s, 512), BF16), jax.ShapeDtypeStruct((s, 512), BF16)],
        compiler_params=_params(("arbitrary",), VMEM_LIMIT_LARGE),
    )(after, x, mod, g_norm1, w_in)


def _t5_bucket(dist):
    max_exact = N_BUCKETS // 2
    is_small = dist < max_exact
    d = np.maximum(dist, 1).astype(np.float32)
    large = max_exact + (np.log(d / max_exact) / math.log(MAX_DISTANCE / max_exact)
                         * (N_BUCKETS - max_exact)).astype(np.int32)
    large = np.minimum(large, N_BUCKETS - 1)
    return np.where(is_small, dist, large).astype(np.int32)


def _bucket_table():
    qi = np.arange(BLOCK, dtype=np.int32)[:, None]
    sj = np.arange(2 * BLOCK, dtype=np.int32)[None, :]
    return jnp.asarray(_t5_bucket(np.maximum(qi + BLOCK - sj, 0)))


def _window_mask():
    qi = lax.broadcasted_iota(jnp.int32, (BLOCK, 2 * BLOCK), 0)
    sj = lax.broadcasted_iota(jnp.int32, (BLOCK, 2 * BLOCK), 1)
    dist = qi + BLOCK - sj
    return (dist >= 0) & (dist < BLOCK)


def _fill_bias_table(rb_ref, bk_ref, o_ref):
    bk = bk_ref[...]
    inside = _window_mask()
    for h in range(N_Q_HEADS):
        acc = jnp.zeros((BLOCK, 2 * BLOCK), F32)
        for b in range(N_BUCKETS):
            acc = jnp.where(bk == b, rb_ref[h, b], acc)
        o_ref[h] = jnp.where(inside, acc, NEG_INF)


def _load_kv_window(kv_ref, n):
    prev = jnp.maximum(n - 1, 0)
    kvw = jnp.concatenate([kv_ref[pl.ds(pl.multiple_of(prev * BLOCK, BLOCK), BLOCK), :],
                           kv_ref[pl.ds(pl.multiple_of(n * BLOCK, BLOCK), BLOCK), :]], axis=0)
    k, v = kvw[:, 0:128], kvw[:, 128:256]
    k_sw = pltpu.roll(k.astype(F32), 64, 1).astype(BF16)
    v_sw = pltpu.roll(v.astype(F32), 64, 1).astype(BF16)
    return (k, k_sw), (v, v_sw)


def _conv_taps(gc, xc, gc_prev, xc_prev, n):
    u = gc * xc
    before = jnp.where(n > 0, gc_prev.astype(F32) * xc_prev.astype(F32), 0.0)
    last = before.shape[0] - 1
    row = lax.broadcasted_iota(jnp.int32, u.shape, 0)
    u1 = jnp.where(row == 0, before[last:last + 1, :], pltpu.roll(u, 1, 0))
    u2 = jnp.where(row == 0, before[last - 1:last, :],
                   jnp.where(row == 1, before[last:last + 1, :], pltpu.roll(u, 2, 0)))
    return u, u1, u2


def _mixer_fwd(q, kv, gb, gc, xc, bias, sinks, conv_w, g_attn, g_conv):
    s = q.shape[0]
    nb = s // BLOCK

    per_step = min(MIXER_BLOCKS, nb)
    tile = per_step * BLOCK

    def one_block(n, slot, before, sink_ref, q_ref, kv_ref, gb_ref, gc_ref, xc_ref, bias_ref, cw_ref, ga_ref,
                  gcv_ref, attn_ref, merged_ref, lse_ref, p_ref):
        rows = slice(slot * BLOCK, (slot + 1) * BLOCK)
        ks, vs = _load_kv_window(kv_ref, n)
        lane = lax.broadcasted_iota(jnp.int32, (BLOCK, BLOCK), 1)
        low = lane < HEAD_DIM
        col = lax.broadcasted_iota(jnp.int32, (BLOCK, 2 * BLOCK), 1)
        no_prev = (col < BLOCK) & (n == 0)
        lse_all = jnp.zeros((BLOCK, BLOCK), F32)
        pairs = []
        for p in range(4):
            qp = q_ref[rows, 128 * p:128 * (p + 1)].astype(F32)
            kvh = p // 2
            res = []
            for e in range(2):
                h = 2 * p + e
                qm = jnp.where(low if e == 0 else ~low, qp, 0.0).astype(BF16)
                sw = 0 if kvh == e else 1
                sc = _dot_nt(qm, ks[sw]) * SCALE + bias_ref[h]
                sc = jnp.where(no_prev, NEG_INF, sc)
                sink = sink_ref[h]
                m = jnp.maximum(jnp.max(sc, axis=-1, keepdims=True), sink)
                pe = jnp.exp(sc - m)
                den = jnp.sum(pe, axis=-1, keepdims=True) + jnp.exp(sink - m)
                pb = (pe * (1.0 / den)).astype(BF16)
                p_ref[slot, h] = pb
                res.append(_dot(pb, vs[sw]))
                lse_all = lse_all + jnp.where(lane == h, m + jnp.log(den), 0.0)
            pairs.append(jnp.where(low, res[0], res[1]))
        attn = jnp.concatenate(pairs, axis=1)
        attn_ref[rows, :] = attn
        lse_ref[rows, :] = lse_all
        u, u1, u2 = _conv_taps(gc_ref[rows, :].astype(F32), xc_ref[rows, :].astype(F32), before[0], before[1], n)
        cw = cw_ref[...]
        cv = gb_ref[rows, :].astype(F32) * (cw[0:1, :] * u2 + cw[1:2, :] * u1 + cw[2:3, :] * u)
        an = attn * _rsqrt_mean_sq(attn) * ga_ref[...]
        cn = cv * _rsqrt_mean_sq(cv) * gcv_ref[...]
        merged_ref[rows, :] = jnp.concatenate([an, cn], axis=1).astype(BF16)

    def body(sink_ref, q_ref, kv_ref, gb_ref, gc_ref, xc_ref, gcp_ref, xcp_ref, *rest):
        step = pl.program_id(0)
        for sub in range(per_step):
            ahead = slice(sub * BLOCK - PREV_ROWS, sub * BLOCK)
            before = (gcp_ref[...], xcp_ref[...]) if sub == 0 else (gc_ref[ahead, :], xc_ref[ahead, :])
            one_block(step * per_step + sub, sub, before, sink_ref, q_ref, kv_ref, gb_ref, gc_ref, xc_ref, *rest)

    blk = lambda w: pl.BlockSpec((tile, w), lambda n: (n, 0))
    prev8 = pl.BlockSpec((PREV_ROWS, 512), lambda n: (jnp.maximum(n * (tile // PREV_ROWS) - 1, 0), 0))
    return pl.pallas_call(
        body, name="mixer_fwd", grid=(nb // per_step,),
        in_specs=[pl.BlockSpec(memory_space=pltpu.SMEM), blk(512), _full((s, 256)), blk(512), blk(512), blk(512),
                  prev8, prev8, _full((N_Q_HEADS, BLOCK, 2 * BLOCK)), _full((3, 512)), _full((1, 512)),
                  _full((1, 512))],
        out_specs=[blk(512), blk(1024), blk(128),
                   pl.BlockSpec((per_step, N_Q_HEADS, BLOCK, 2 * BLOCK), lambda n: (n, 0, 0, 0))],
        out_shape=[jax.ShapeDtypeStruct((s, 512), F32), jax.ShapeDtypeStruct((s, 1024), BF16),
                   jax.ShapeDtypeStruct((s, 128), F32),
                   jax.ShapeDtypeStruct((nb, N_Q_HEADS, BLOCK, 2 * BLOCK), BF16)],
        compiler_params=_params(("arbitrary",)),
    )(sinks, q, kv, gb, gc, xc, gc, xc, bias, conv_w, g_attn, g_conv)


def _out_proj(merged, x, mod, w_out, tm):
    s = x.shape[0]

    def body(m_ref, x_ref, mod_ref, w_ref, o_ref, x1_ref):
        o = _dot(m_ref[...], w_ref[...])
        o_ref[...] = o.astype(BF16)
        x1_ref[...] = x_ref[...] + mod_ref[G1:G1 + 1, :] * o

    return pl.pallas_call(
        body, name="out_proj", grid=(s // tm,),
        in_specs=[_rows(tm, D_MODEL), _rows(tm, D_MODEL), _full((8, D_MODEL)), _full((D_MODEL, D_MODEL))],
        out_specs=[_rows(tm, D_MODEL), _rows(tm, D_MODEL)],
        out_shape=[jax.ShapeDtypeStruct((s, D_MODEL), BF16), jax.ShapeDtypeStruct((s, D_MODEL), F32)],
        compiler_params=_params(("arbitrary",)),
    )(merged, x, mod, w_out)


def _resident(shape):
    nd = len(shape)
    return pl.BlockSpec(shape, lambda *_: (0,) * nd, pipeline_mode=pl.Buffered(1))


def _ffn(x1, o1, merged, mod, g_norm2, w_gu, w_down, w_out, g_final, target, tm):
    s = x1.shape[0]
    chunk = D_FF // FFN_CHUNKS

    def body(x_ref, o1_ref, mg_ref, mod_ref, g_ref, wgu_ref, wd_ref, wo_ref, gf_ref, t_ref,
             h_ref, act_ref, do_ref, dgu_ref, dx1_ref, dwo_ref, dm_ref, small_ref, dwo_acc):
        @pl.when(pl.program_id(0) == 0)
        def _():
            small_ref[...] = jnp.zeros_like(small_ref)
            dwo_acc[...] = jnp.zeros_like(dwo_acc)

        xf = x_ref[...]
        n = xf * _rsqrt_mean_sq(xf) * g_ref[...]
        h = (n * (1.0 + mod_ref[SC2:SC2 + 1, :]) + mod_ref[SH2:SH2 + 1, :]).astype(BF16)
        h_ref[...] = h
        gates, ups, o = [], [], None
        for j in range(FFN_CHUNKS):
            lo = j * chunk
            gate = _dot_nt(h, wgu_ref[lo:lo + chunk, :])
            up = _dot_nt(h, wgu_ref[D_FF + lo:D_FF + lo + chunk, :])
            sg = _sigmoid(gate)
            act = (gate * sg * up).astype(BF16)
            act_ref[:, lo:lo + chunk] = act
            gates.append((up * (sg * (1.0 + gate * (1.0 - sg)))).astype(BF16))
            ups.append((gate * sg).astype(BF16))
            part = _dot(act, wd_ref[lo:lo + chunk, :])
            o = part if o is None else o + part
        g2 = mod_ref[G2:G2 + 1, :]
        x2 = xf + g2 * o
        r = _rsqrt_mean_sq(x2)
        xn = x2 * r
        gf = gf_ref[...]
        err = xn * gf - t_ref[...]
        dy = err * (1.0 / D_MODEL)
        dxn = dy * gf
        dx2 = r * (dxn - xn * jnp.mean(dxn * xn, axis=-1, keepdims=True))
        small_ref[4:5, :] += _colsum(dy * xn)
        small_ref[5:6, :] += _colsum(err * err)
        small_ref[3:4, :] += _colsum(dx2 * o)
        do = (dx2 * g2).astype(BF16)
        do_ref[...] = do
        dh = None
        for j in range(FFN_CHUNKS):
            lo = j * chunk
            dact = _dot_nt(do, wd_ref[lo:lo + chunk, :])
            dgate = (dact * gates[j].astype(F32)).astype(BF16)
            dup = (dact * ups[j].astype(F32)).astype(BF16)
            dgu_ref[:, lo:lo + chunk] = dgate
            dgu_ref[:, D_FF + lo:D_FF + lo + chunk] = dup
            part = _dot(dgate, wgu_ref[lo:lo + chunk, :]) + _dot(dup, wgu_ref[D_FF + lo:D_FF + lo + chunk, :])
            dh = part if dh is None else dh + part
        dx1 = dx2 + _norm_mod_bwd(dh, xf, g_ref[...], mod_ref[SC2:SC2 + 1, :], small_ref)
        dx1_ref[...] = dx1.astype(BF16)
        small_ref[7:8, :] += _colsum(dx1 * o1_ref[...].astype(F32))
        do1 = (dx1 * mod_ref[G1:G1 + 1, :]).astype(BF16)
        dm_ref[...] = _dot_nt(do1, wo_ref[...]).astype(BF16)
        dwo = dwo_acc[...] + _dot_tn(mg_ref[...], do1)
        dwo_acc[...] = dwo
        dwo_ref[...] = dwo.astype(BF16)

        @pl.when(pl.program_id(0) == pl.num_programs(0) - 1)
        def _():
            total = jnp.sum(small_ref[5:6, :], axis=-1, keepdims=True) * (0.5 / D_MODEL)
            small_ref[6:7, :] = jnp.broadcast_to(total, (1, D_MODEL))

    narrow = jax.ShapeDtypeStruct((s, D_MODEL), BF16)
    return pl.pallas_call(
        body, name="ffn", grid=(s // tm,),
        in_specs=[_rows(tm, D_MODEL), _rows(tm, D_MODEL), _rows(tm, D_MODEL), _full((8, D_MODEL)), _full((1, D_MODEL)),
                  _resident((2 * D_FF, D_MODEL)), _resident((D_FF, D_MODEL)), _resident((D_MODEL, D_MODEL)),
                  _full((1, D_MODEL)), _rows(tm, D_MODEL)],
        out_specs=[_rows(tm, D_MODEL), _rows(tm, D_FF), _rows(tm, D_MODEL), _rows(tm, 2 * D_FF), _rows(tm, D_MODEL),
                   _full((D_MODEL, D_MODEL)), _rows(tm, D_MODEL), _full((8, D_MODEL))],
        out_shape=[narrow, jax.ShapeDtypeStruct((s, D_FF), BF16), narrow, jax.ShapeDtypeStruct((s, 2 * D_FF), BF16),
                   narrow, jax.ShapeDtypeStruct((D_MODEL, D_MODEL), BF16), narrow,
                   jax.ShapeDtypeStruct((8, D_MODEL), F32)],
        scratch_shapes=[pltpu.VMEM((D_MODEL, D_MODEL), F32)],
        compiler_params=_params(("arbitrary",), VMEM_LIMIT_LARGE),
    )(x1, o1, merged, mod, g_norm2, w_gu, w_down, w_out, g_final, target)


def _norm_mod_bwd(dh, xf, g, scale_row, small_ref):
    r = _rsqrt_mean_sq(xf)
    xn = xf * r
    small_ref[0:1, :] += _colsum(dh)
    small_ref[1:2, :] += _colsum(dh * (xn * g))
    dn = dh * (1.0 + scale_row)
    small_ref[2:3, :] += _colsum(dn * xn)
    dxn = dn * g
    return r * (dxn - xn * jnp.mean(dxn * xn, axis=-1, keepdims=True))


def _group_norm_bwd(dm, a, g):
    r = _rsqrt_mean_sq(a)
    an = a * r
    dan = dm * g
    return r * (dan - an * jnp.mean(dan * an, axis=-1, keepdims=True)), _colsum(dm * an)


def _sum_by_bucket(db_ref, bk_ref, o_ref, rows_ref):
    bk = bk_ref[...]
    for b in range(N_BUCKETS):
        sel = (bk == b).astype(F32)
        for h in range(N_Q_HEADS):
            rows_ref[N_BUCKETS * h + b:N_BUCKETS * h + b + 1, :] = _colsum(db_ref[h] * sel)
    head = lax.broadcasted_iota(jnp.int32, (N_BUCKETS, REL_LANES), 1)
    out = jnp.zeros((N_BUCKETS, REL_LANES), F32)
    for h in range(N_Q_HEADS):
        per_bucket = jnp.sum(rows_ref[N_BUCKETS * h:N_BUCKETS * (h + 1), :], axis=-1, keepdims=True)
        out = out + jnp.where(head == h, per_bucket, 0.0)
    o_ref[...] = out


def _mixer_bwd(after, q, kv, gb, gc, xc, probs, sinks, conv_w, g_attn, g_conv, attn, lse, dmerged, bucket):
    s = q.shape[0]
    nb = s // BLOCK

    per_step = min(MIXER_BLOCKS, nb)
    tile = per_step * BLOCK
    steps = nb // per_step

    def one_block(n, slot, before, nxt, sink_ref, q_ref, kv_ref, gb_ref, gc_ref, xc_ref, p_ref, cw_ref, ga_ref,
                  gcv_ref, attn_ref, lse_ref, dm_ref, dproj_ref, dbias_ref, dsink_ref, small_ref):
        rows = slice(slot * BLOCK, (slot + 1) * BLOCK)
        next_dy, next_dkv = nxt
        dm = dm_ref[rows, :].astype(F32)
        gbv, gcv_, xcv = gb_ref[rows, :].astype(F32), gc_ref[rows, :].astype(F32), xc_ref[rows, :].astype(F32)
        u, u1, u2 = _conv_taps(gcv_, xcv, before[0], before[1], n)
        cw = cw_ref[...]
        yv = cw[0:1, :] * u2 + cw[1:2, :] * u1 + cw[2:3, :] * u
        dcv, dg_conv = _group_norm_bwd(dm[:, 512:1024], gbv * yv, gcv_ref[...])
        small_ref[1:2, :] += dg_conv
        dproj_ref[rows, 768:1280] = (dcv * yv).astype(BF16)
        dy = dcv * gbv
        row = lax.broadcasted_iota(jnp.int32, dy.shape, 0)
        d1 = jnp.where(row == BLOCK - 1, next_dy[0:1, :], pltpu.roll(dy, BLOCK - 1, 0))
        d2 = jnp.where(row == BLOCK - 2, next_dy[0:1, :],
                       jnp.where(row == BLOCK - 1, next_dy[1:2, :], pltpu.roll(dy, BLOCK - 2, 0)))
        du = cw[2:3, :] * dy + cw[1:2, :] * d1 + cw[0:1, :] * d2
        dproj_ref[rows, 1280:1792] = (du * xcv).astype(BF16)
        dproj_ref[rows, 1792:2304] = (du * gcv_).astype(BF16)
        small_ref[2:3, :] += _colsum(dy * u2)
        small_ref[3:4, :] += _colsum(dy * u1)
        small_ref[4:5, :] += _colsum(dy * u)

        attn_v = attn_ref[rows, :]
        dout, dg_attn = _group_norm_bwd(dm[:, 0:512], attn_v, ga_ref[...])
        small_ref[0:1, :] += dg_attn
        ks, vs = _load_kv_window(kv_ref, n)
        lane = lax.broadcasted_iota(jnp.int32, (BLOCK, BLOCK), 1)
        low = lane < HEAD_DIM
        lse_all = lse_ref[rows, :]
        dsink = jnp.zeros((BLOCK, BLOCK), F32)
        dq_pairs = []
        dk_groups, dv_groups = [], []
        for kvh in range(2):
            ds_rows, pr_rows, q_rows, do_rows = [], [], [], []
            for p in (2 * kvh, 2 * kvh + 1):
                qp = q_ref[rows, 128 * p:128 * (p + 1)].astype(F32)
                do_p = dout[:, 128 * p:128 * (p + 1)]
                prod = do_p * attn_v[:, 128 * p:128 * (p + 1)]
                res = []
                for e in range(2):
                    h = 2 * p + e
                    half = low if e == 0 else ~low
                    qm = jnp.where(half, qp, 0.0).astype(BF16)
                    dom = jnp.where(half, do_p, 0.0).astype(BF16)
                    delta = jnp.sum(jnp.where(half, prod, 0.0), axis=-1, keepdims=True)
                    lse_h = jnp.sum(jnp.where(lane == h, lse_all, 0.0), axis=-1, keepdims=True)
                    sw = 0 if kvh == e else 1
                    pb = p_ref[slot, h]
                    dp = _dot_nt(dom, vs[sw])
                    ds = pb.astype(F32) * (dp - delta)
                    dbias_ref[h] += ds
                    dsink = dsink + jnp.where(lane == h, -jnp.exp(sink_ref[h] - lse_h) * delta, 0.0)
                    dsb = ds.astype(BF16)
                    res.append(_dot(dsb, ks[sw]) * SCALE)
                    ds_rows.append(dsb)
                    pr_rows.append(pb)
                    q_rows.append(qm)
                    do_rows.append(dom)
                dq_pairs.append(jnp.where(low, res[0], res[1]))
            dk_g = _dot_tn(jnp.concatenate(ds_rows, axis=0), jnp.concatenate(q_rows, axis=0)) * SCALE
            dv_g = _dot_tn(jnp.concatenate(pr_rows, axis=0), jnp.concatenate(do_rows, axis=0))
            dk_groups.append(dk_g + pltpu.roll(dk_g, 64, 1))
            dv_groups.append(dv_g + pltpu.roll(dv_g, 64, 1))
        dproj_ref[rows, 0:512] = jnp.concatenate(dq_pairs, axis=1).astype(BF16)
        dsink_ref[...] += dsink
        low_kv = lax.broadcasted_iota(jnp.int32, (2 * BLOCK, BLOCK), 1) < HEAD_DIM
        dkv_win = jnp.concatenate([jnp.where(low_kv, dk_groups[0], dk_groups[1]),
                                   jnp.where(low_kv, dv_groups[0], dv_groups[1])], axis=1)
        dproj_ref[rows, 512:768] = (dkv_win[BLOCK:2 * BLOCK, :] + next_dkv).astype(BF16)
        return dy[0:8, :], dkv_win[0:BLOCK, :]

    def body(sink_ref, q_ref, kv_ref, gb_ref, gc_ref, xc_ref, gcp_ref, xcp_ref, p_ref, cw_ref, ga_ref, gcv_ref,
             attn_ref, lse_ref, dm_ref, bk_ref, dproj_ref, drel_ref, dsink_ref, small_ref,
             dy_ref, dkv_ref, dbias_ref, rows_ref):
        refs = (p_ref, cw_ref, ga_ref, gcv_ref, attn_ref, lse_ref, dm_ref, dproj_ref, dbias_ref, dsink_ref, small_ref)
        step = pl.program_id(0)

        @pl.when(step == 0)
        def _():
            dbias_ref[...] = jnp.zeros_like(dbias_ref)
            dsink_ref[...] = jnp.zeros_like(dsink_ref)
            small_ref[...] = jnp.zeros_like(small_ref)
            dy_ref[...] = jnp.zeros_like(dy_ref)
            dkv_ref[...] = jnp.zeros_like(dkv_ref)

        nxt = (dy_ref[...], dkv_ref[...])
        for sub in reversed(range(per_step)):
            ahead = slice(sub * BLOCK - PREV_ROWS, sub * BLOCK)
            before = (gcp_ref[...], xcp_ref[...]) if sub == 0 else (gc_ref[ahead, :], xc_ref[ahead, :])
            nxt = one_block((steps - 1 - step) * per_step + sub, sub, before, nxt,
                            sink_ref, q_ref, kv_ref, gb_ref, gc_ref, xc_ref, *refs)
        dy_ref[...], dkv_ref[...] = nxt

        @pl.when(step == steps - 1)
        def _():
            small_ref[5:6, :] = jnp.concatenate([_colsum(dsink_ref[...]), jnp.zeros((1, 512 - BLOCK), F32)], axis=1)
            _sum_by_bucket(dbias_ref, bk_ref, drel_ref, rows_ref)

    blk = lambda w: pl.BlockSpec((tile, w), lambda t: (steps - 1 - t, 0))
    prev8 = pl.BlockSpec((PREV_ROWS, 512),
                         lambda t: (jnp.maximum((steps - 1 - t) * (tile // PREV_ROWS) - 1, 0), 0))
    bf = lambda w: jax.ShapeDtypeStruct((s, w), BF16)
    return pl.pallas_call(
        _coming_behind(body), name="mixer_bwd", grid=(steps,),
        in_specs=[ANY_SPEC, pl.BlockSpec(memory_space=pltpu.SMEM), blk(512), _full((s, 256)), blk(512), blk(512), blk(512),
                  prev8, prev8,
                  pl.BlockSpec((per_step, N_Q_HEADS, BLOCK, 2 * BLOCK), lambda t: (steps - 1 - t, 0, 0, 0)),
                  _full((3, 512)), _full((1, 512)), _full((1, 512)), blk(512), blk(128), blk(1024),
                  _full((BLOCK, 2 * BLOCK))],
        out_specs=[blk(IN_PROJ_WIDTH), _full((N_BUCKETS, REL_LANES)), _full((BLOCK, BLOCK)), _full((8, 512))],
        out_shape=[bf(IN_PROJ_WIDTH), jax.ShapeDtypeStruct((N_BUCKETS, REL_LANES), F32),
                   jax.ShapeDtypeStruct((BLOCK, BLOCK), F32), jax.ShapeDtypeStruct((8, 512), F32)],
        scratch_shapes=[pltpu.VMEM((8, 512), F32), pltpu.VMEM((BLOCK, 2 * KV_WIDTH), F32),
                        pltpu.VMEM((N_Q_HEADS, BLOCK, 2 * BLOCK), F32),
                        pltpu.VMEM((N_BUCKETS * N_Q_HEADS, 2 * BLOCK), F32)],
        compiler_params=_params(("arbitrary",), VMEM_LIMIT_LARGE),
    )(after, sinks, q, kv, gb, gc, xc, gc, xc, probs, conv_w, g_attn, g_conv, attn, lse, dmerged, bucket)


def _in_proj_bwd(after, dproj, x, dx1, mod, g_norm1, w_in, tm):
    s = x.shape[0]

    def body(dproj_ref, x_ref, dx1_ref, mod_ref, g_ref, w_ref, dx_ref, small_ref):
        @pl.when(pl.program_id(0) == 0)
        def _():
            small_ref[...] = jnp.zeros_like(small_ref)

        dh = _dot(dproj_ref[...], w_ref[...])
        dx_ref[...] = dx1_ref[...].astype(F32) + _norm_mod_bwd(dh, x_ref[...], g_ref[...], mod_ref[SC1:SC1 + 1, :],
                                                               small_ref)

    return pl.pallas_call(
        _coming_behind(body), name="in_proj_bwd", grid=(s // tm,),
        in_specs=[ANY_SPEC, _rows(tm, IN_PROJ_WIDTH), _rows(tm, D_MODEL), _rows(tm, D_MODEL), _full((8, D_MODEL)),
                  _full((1, D_MODEL)), _full((IN_PROJ_WIDTH, D_MODEL))],
        out_specs=[_rows(tm, D_MODEL), _full((8, D_MODEL))],
        out_shape=[jax.ShapeDtypeStruct((s, D_MODEL), F32), jax.ShapeDtypeStruct((8, D_MODEL), F32)],
        compiler_params=_params(("arbitrary",), VMEM_LIMIT_LARGE),
    )(after, dproj, x, dx1, mod, g_norm1, w_in)


def _weight_grad(a, b, tk, ts, name, after=None):
    s, k = a.shape
    n = b.shape[1]
    nt = s // ts
    extra = [] if after is None else [after]

    def body(a_ref, b_ref, *rest):
        o_ref, acc_ref = rest[-2:]
        t = pl.program_id(1)
        @pl.when(t == 0)
        def _():
            acc_ref[...] = jnp.zeros_like(acc_ref)

        acc = acc_ref[...] + _dot_tn(a_ref[...], b_ref[...])
        acc_ref[...] = acc
        o_ref[...] = acc.astype(BF16)

    return pl.pallas_call(
        body, name=name, grid=(k // tk, nt),
        in_specs=[pl.BlockSpec((ts, tk), lambda i, t: (t, i)), pl.BlockSpec((ts, n), lambda i, t: (t, 0))]
        + [ANY_SPEC] * len(extra),
        out_specs=pl.BlockSpec((tk, n), lambda i, t: (i, 0)),
        out_shape=jax.ShapeDtypeStruct((k, n), BF16),
        scratch_shapes=[pltpu.VMEM((tk, n), F32)],
        compiler_params=_params(("arbitrary", "arbitrary"), VMEM_LIMIT_LARGE),
    )(a, b, *extra)


def _lanes_from(x, start, width):
    n = x.shape[1]
    return pltpu.roll(x, (n - start) % n, 1)[:, 0:width]


def _share_wait_adamw_w_ada(started, after, me, cond_all, w, m, v, tr):
    sems, arrs, zones = started
    n = len(arrs)
    r, cols = w.shape
    first = 4 * n + len(after)

    def body(me_ref, *refs):
        src_refs, zone_refs, sem_refs = refs[:n], refs[n:2 * n], refs[2 * n:4 * n]
        c_ref, w_ref, m_ref, v_ref = refs[first:first + 4]
        g_ref, d_ref, mo_ref, vo_ref, p_ref, load_sem = refs[first + 4 + 2 * n:]
        for a in range(n):
            for k in range(1, N_DEV):
                cp = pltpu.make_async_remote_copy(
                    src_ref=src_refs[a], dst_ref=zone_refs[a].at[_linear(_peer(k))],
                    send_sem=sem_refs[2 * a].at[k - 1], recv_sem=sem_refs[2 * a + 1].at[k - 1],
                    device_id=_peer(k), device_id_type=MESH_ID)
                cp.wait_send()
                cp.wait_recv()
        load = pltpu.make_async_copy(zone_refs[0], p_ref, load_sem.at[0])
        load.start()
        load.wait()
        dmod = jnp.concatenate([p_ref[k][:, OFF_DMOD:OFF_DMOD + N_MOD * D_MODEL] for k in range(N_DEV)], axis=0)
        pad = lambda a: jnp.concatenate([a, jnp.zeros((128 - N_DEV, a.shape[1]), F32)], axis=0)
        mine = pad(_lanes_from(dmod, me_ref[0] * cols, cols))
        for i in range(r // tr):
            rows = slice(tr * i, tr * (i + 1))
            cond = jnp.concatenate([c_ref[k][:, rows] for k in range(N_DEV)], axis=0)
            g = _dot_tn(pad(cond), mine)
            g_ref[rows, :] = g
            d_ref[rows, :], mo_ref[rows, :], vo_ref[rows, :] = _adam_math(
                w_ref[rows, :], g, m_ref[rows, :], v_ref[rows, :])

    vmem = pl.BlockSpec(memory_space=pltpu.VMEM)
    outs = pl.pallas_call(
        body, name="share_small_wait_adamw_w_ada",
        out_shape=tuple(pltpu.HBM(a.shape, a.dtype) for a in arrs) + tuple(pltpu.HBM(z.shape, z.dtype) for z in zones)
        + (jax.ShapeDtypeStruct((r, cols), F32),) * 4,
        in_specs=(pl.BlockSpec(memory_space=pltpu.SMEM),) + (HBM_SPEC,) * (2 * n) + (SEM_SPEC,) * (2 * n)
        + (ANY_SPEC,) * len(after) + (vmem,) * 4,
        out_specs=(HBM_SPEC,) * (2 * n) + (vmem,) * 4,
        input_output_aliases={1 + i: i for i in range(2 * n)},
        scratch_shapes=[pltpu.VMEM(zones[0].shape, F32), pltpu.SemaphoreType.DMA((1,))],
        compiler_params=pltpu.CompilerParams(has_side_effects=DATAFLOW, vmem_limit_bytes=VMEM_LIMIT_LARGE),
    )(me, *arrs, *zones, *sems, *after, cond_all, w, m, v)
    return list(outs[n:2 * n]), tuple(outs[2 * n:])


SMALL_PARAMS = (("rel_bias", None), ("b_ada", (OFF_DMOD, N_MOD * D_MODEL)), ("g_norm1", (OFF_GN1, D_MODEL)),
                ("sinks", (OFF_SINK, N_Q_HEADS)), ("conv_w", None), ("g_attn_out", (OFF_GATT, ATTN_WIDTH)),
                ("g_conv_out", (OFF_GCV, CONV_WIDTH)), ("g_norm2", (OFF_GN2, D_MODEL)),
                ("g_final", (OFF_GFIN, D_MODEL)))


def _small_update(me, packed_all, rel_all, state, after):
    n_p = len(SMALL_PARAMS)
    flat = [a for triple in state for a in triple]
    conv_cols = state[4][0].shape[-1]

    def body(me_ref, p_ref, r_ref, *refs):
        ins = refs[:3 * n_p]
        loss_ref, outs = refs[3 * n_p + len(after)], refs[3 * n_p + len(after) + 1:]
        small, rel = p_ref[0], r_ref[0]
        for k in range(1, N_DEV):
            small = small + p_ref[k]
            rel = rel + r_ref[k]
        rel = jnp.concatenate([rel, jnp.zeros((REL_LANES - N_BUCKETS, REL_LANES), F32)], axis=0).T
        rel = rel[0:N_Q_HEADS, 0:N_BUCKETS]
        loss_ref[...] = small[:, OFF_LOSS:OFF_LOSS + 128]
        taps = jnp.concatenate([small[:, OFF_CONVW + CONV_WIDTH * j:OFF_CONVW + CONV_WIDTH * (j + 1)]
                                for j in range(3)] + [jnp.zeros((5, CONV_WIDTH), F32)], axis=0)
        conv_g = _lanes_from(taps, me_ref[0] * conv_cols, conv_cols)[0:3, :]
        for i, (name, lanes) in enumerate(SMALL_PARAMS):
            w_ref, m_ref, v_ref = ins[3 * i:3 * i + 3]
            if name == "conv_w":
                for j in range(3):
                    outs[4 * i][j] = conv_g[j:j + 1, :]
                    outs[4 * i + 1][j], outs[4 * i + 2][j], outs[4 * i + 3][j] = _adam_math(
                        w_ref[j], conv_g[j:j + 1, :], m_ref[j], v_ref[j])
                continue
            g = rel if name == "rel_bias" else small[:, lanes[0]:lanes[0] + lanes[1]]
            outs[4 * i][...] = g
            outs[4 * i + 1][...], outs[4 * i + 2][...], outs[4 * i + 3][...] = _adam_math(
                w_ref[...], g, m_ref[...], v_ref[...])

    vmem = pl.BlockSpec(memory_space=pltpu.VMEM)
    out_shape = [jax.ShapeDtypeStruct((1, 128), F32)]
    for w, _, _ in state:
        out_shape += [jax.ShapeDtypeStruct(w.shape, F32)] * 4
    outs = pl.pallas_call(
        body, name="small_update",
        in_specs=[pl.BlockSpec(memory_space=pltpu.SMEM), vmem, vmem] + [vmem] * len(flat)
        + [pl.BlockSpec(memory_space=pl.ANY)] * len(after),
        out_shape=out_shape,
    )(me, packed_all, rel_all, *flat, *after)
    return outs[0], [tuple(outs[1 + 4 * i:5 + 4 * i]) for i in range(n_p)]


def _adam_math(w, g, m, v):
    m = ADAM_B1 * m + (1.0 - ADAM_B1) * g
    v = ADAM_B2 * v + (1.0 - ADAM_B2) * (g * g)
    m_hat = m / (1.0 - ADAM_B1 ** ADAM_STEP)
    v_hat = v / (1.0 - ADAM_B2 ** ADAM_STEP)
    delta = -ADAM_LR * (m_hat / (jnp.sqrt(v_hat) + ADAM_EPS) + ADAM_WD * w)
    return delta, m, v


def _adamw_parts(w, m, v, local, land, me, tr, name):
    r, c = w.shape

    def body(me_ref, w_ref, m_ref, v_ref, own_ref, land_ref, g_ref, d_ref, mo_ref, vo_ref):
        g = own_ref[0].astype(F32)
        for k in range(N_DEV - 1):
            g = g + land_ref[k].astype(F32)
        g_ref[...] = g
        d_ref[...], mo_ref[...], vo_ref[...] = _adam_math(w_ref[...], g, m_ref[...], v_ref[...])

    tile = pl.BlockSpec((tr, c), lambda i, me_ref: (i, 0))
    return pl.pallas_call(
        body, name=name,
        grid_spec=pltpu.PrefetchScalarGridSpec(
            num_scalar_prefetch=1, grid=(r // tr,),
            in_specs=[tile, tile, tile, pl.BlockSpec((1, tr, c), lambda i, me_ref: (me_ref[0], i, 0)),
                      pl.BlockSpec((N_DEV - 1, tr, c), lambda i, me_ref: (0, i, 0))],
            out_specs=[tile] * 4),
        out_shape=[jax.ShapeDtypeStruct((r, c), F32)] * 4,
        compiler_params=_params(("arbitrary",)),
    )(me, w, m, v, local, land)


def _local_step(x, target, mod, w_in_t, bias, weights_out_gu, weights_down, g_norm1, sinks, conv_w, g_attn,
                g_conv, g_norm2, g_final, exchange, start_after, share):
    s = x.shape[0]
    tm = min(512, s)
    tm_small = min(256, s)
    bucket = _bucket_table()

    h, q, kv, gb, gc, xc = _in_proj(start_after, x, mod, g_norm1, w_in_t, tm)
    attn, merged, lse, probs = _mixer_fwd(q, kv, gb, gc, xc, bias, sinks, conv_w, g_attn, g_conv)
    w_out, w_gu_t = weights_out_gu(merged)
    o1, x1 = _out_proj(merged, x, mod, w_out, tm)
    w_down = weights_down(x1)
    h2, act, do2, dgu, dx1, dw_out, dmerged, sm_2 = _ffn(x1, o1, merged, mod, g_norm2, w_gu_t, w_down, w_out, g_final,
                                                         target, tm_small)
    ts = min(WEIGHT_GRAD_ROWS, s)
    tok_out = exchange("w_out", dw_out)
    tok_down = exchange("w_down", _weight_grad(act, do2, D_FF // 2, ts, "w_down_grad", after=tok_out))
    tok_gu = exchange("w_gu", _weight_grad(dgu, h2, D_FF // 2, ts, "w_gu_grad", after=tok_down))
    dproj, d_rel, dsink, sm_mix = _mixer_bwd(
        tok_gu, q, kv, gb, gc, xc, probs, sinks, conv_w, g_attn, g_conv, attn, lse, dmerged, bucket)
    dx, sm_1 = _in_proj_bwd(d_rel, dproj, x, dx1, mod, g_norm1, w_in_t, min(1024, s))

    packed = jnp.concatenate([
        sm_1[0:1], sm_1[1:2], sm_2[7:8], sm_2[0:1], sm_2[1:2], sm_2[3:4],
        sm_1[2:3],
        sm_mix[5:6, 0:128],
        sm_mix[0:1], sm_mix[1:2],
        sm_2[2:3],
        sm_2[4:5],
        sm_mix[2:3], sm_mix[3:4], sm_mix[4:5],
        sm_2[6:7, 0:128],
    ], axis=1)
    tok_share = share(packed, d_rel)
    exchange("w_in", _weight_grad(dproj, h, IN_PROJ_WIDTH // 2, ts, "w_in_grad", after=tok_share))
    return dx


def kernel(x, c, rel_bias, w_ada, b_ada, g_norm1, w_in, sinks, conv_w, g_attn_out, g_conv_out, w_out, g_norm2, w_gu, w_down, g_final, loss_target, m_rel_bias, m_w_ada, m_b_ada, m_g_norm1, m_w_in, m_sinks, m_conv_w, m_g_attn_out, m_g_conv_out, m_w_out, m_g_norm2, m_w_gu, m_w_down, m_g_final, v_rel_bias, v_w_ada, v_b_ada, v_g_norm1, v_w_in, v_sinks, v_conv_w, v_g_attn_out, v_g_conv_out, v_w_out, v_g_norm2, v_w_gu, v_w_down, v_g_final):
    me = _linear(_mesh_position())
    me_arr = jnp.reshape(me, (1,)).astype(jnp.int32)
    ada_cols = w_ada.shape[2]
    tm = min(512, x.shape[1])

    b_cols = b_ada.reshape(N_DEV, 1, ada_cols)
    cond_all, conv_w_all, mod, w_in_blocks, staged, bias = _open_step(
        c, conv_w.transpose(1, 0, 2), w_ada[0], b_cols, w_in[0].T, [w_out[0], w_gu[0].T, w_down[0]], rel_bias.T, _bucket_table())
    conv_w_full = conv_w_all.reshape(N_DEV, 3, -1).transpose(1, 0, 2).reshape(3, CONV_WIDTH)
    w_in_t = w_in_blocks.reshape(IN_PROJ_WIDTH, D_MODEL)
    gather_sems, staged, gather_token = _gather_start(staged, "gather_start_weights")

    def weights_out_gu(after):
        got = _gather_pass_on(_gather_wait(gather_sems[0:4], staged[0:2], [after], "gather_wait_out_gu"),
                              "gather_pass_on_out_gu")
        return got[0].reshape(D_MODEL, D_MODEL), got[1].reshape(2 * D_FF, D_MODEL)

    def weights_down(after):
        got = _gather_pass_on(_gather_wait(gather_sems[4:6], staged[2:3], [after], "gather_wait_down"),
                              "gather_pass_on_down")
        return got[0].reshape(D_FF, D_MODEL)

    started = {}

    def exchange(name, dw):
        st = _exchange_start(dw.reshape(N_DEV, dw.shape[0] // N_DEV, dw.shape[1]), "exchange_start_" + name)
        started[name] = st
        return st[4]

    def zone(a):
        return lax.dynamic_update_slice(jnp.zeros((N_DEV,) + a.shape, F32), a[None], (me,) + (0,) * a.ndim)

    def share(packed, d_rel):
        started["small"] = _share_start([packed, d_rel], [zone(packed), zone(d_rel)], "share_small_start")
        return started["small"][1][0]

    dx = _local_step(
        x[0], loss_target[0], mod, w_in_t, bias, weights_out_gu, weights_down, g_norm1, sinks[0], conv_w_full,
        g_attn_out, g_conv_out, g_norm2, g_final[None, :], exchange, gather_token, share)
    shared = started["small"]

    def finish(name, after, w, m, v, tr):
        src, land = _exchange_wait(started[name], after, "exchange_wait_" + name)
        return _adamw_parts(w, m, v, src, land, me_arr, tr, "adamw_" + name)

    g_down, d_down, nm_down, nv_down = finish("w_down", [started["w_in"][4]], w_down[0], m_w_down[0], v_w_down[0], 176)
    g_gu, d_gu, nm_gu, nv_gu = finish("w_gu", [nv_down], w_gu[0].T, m_w_gu[0].T, v_w_gu[0].T, 352)
    g_out, d_out, nm_out, nv_out = finish("w_out", [nv_gu], w_out[0], m_w_out[0], v_w_out[0], 128)

    (packed_all, rel_all), (g_ada, d_ada, nm_ada, nv_ada) = _share_wait_adamw_w_ada(
        shared, [nv_out], me_arr, cond_all, w_ada[0], m_w_ada[0], v_w_ada[0], 256)
    as_rows = {"conv_w": lambda a: a.transpose(1, 0, 2), "g_final": lambda a: a[None, :], "rel_bias": lambda a: a.T}
    small_state = {
        "rel_bias": (rel_bias, m_rel_bias, v_rel_bias), "b_ada": (b_ada, m_b_ada, v_b_ada),
        "g_norm1": (g_norm1, m_g_norm1, v_g_norm1), "sinks": (sinks, m_sinks, v_sinks),
        "conv_w": (conv_w, m_conv_w, v_conv_w), "g_attn_out": (g_attn_out, m_g_attn_out, v_g_attn_out),
        "g_conv_out": (g_conv_out, m_g_conv_out, v_g_conv_out), "g_norm2": (g_norm2, m_g_norm2, v_g_norm2),
        "g_final": (g_final, m_g_final, v_g_final),
    }
    state = [tuple(as_rows.get(name, lambda a: a)(a) for a in small_state[name]) for name, _ in SMALL_PARAMS]
    loss_row, small_out = _small_update(me_arr, packed_all, rel_all, state, [])
    loss = loss_row[0, 0]
    back = {"rel_bias": lambda a: a.T, "conv_w": lambda a: a.transpose(1, 0, 2)}
    small_res = {name: tuple(back[name](a) if name in back else a.reshape(small_state[name][0].shape) for a in res)
                 for (name, _), res in zip(SMALL_PARAMS, small_out)}

    g_in, d_in, nm_in, nv_in = finish("w_in", [loss_row, nv_ada], w_in[0].T, m_w_in[0].T, v_w_in[0].T, 144)

    big = {
        "w_ada": (g_ada[None], d_ada[None], nm_ada[None], nv_ada[None]),
        "w_in": (g_in.T[None], d_in.T[None], nm_in.T[None], nv_in.T[None]),
        "w_out": (g_out[None], d_out[None], nm_out[None], nv_out[None]),
        "w_gu": (g_gu.T[None], d_gu.T[None], nm_gu.T[None], nv_gu.T[None]),
        "w_down": (g_down[None], d_down[None], nm_down[None], nv_down[None]),
    }
    order = ["rel_bias", "w_ada", "b_ada", "g_norm1", "w_in", "sinks", "conv_w", "g_attn_out", "g_conv_out", "w_out",
             "g_norm2", "w_gu", "w_down", "g_final"]
    results = [big[k] if k in big else small_res[k] for k in order]
    return (loss, dx[None], *[r[0] for r in results], *[r[1] for r in results], *[r[2] for r in results],
            *[r[3] for r in results])
```

```python
import math

import jax
import jax.numpy as jnp
import numpy as np
from jax import lax
from jax.experimental import pallas as pl
from jax.experimental.pallas import tpu as pltpu

F32 = jnp.float32
BF16 = jnp.bfloat16

D_MODEL = 1024
HEAD_DIM = 64
N_Q_HEADS = 8
ATTN_WIDTH = 512
KV_WIDTH = 128
CONV_WIDTH = 512
IN_PROJ_WIDTH = 2304
D_FF = 2816
N_MOD = 6
N_BUCKETS = 32
MAX_DISTANCE = 128
BLOCK = 128
REL_LANES = 128
EPS = 1e-6
NEG_INF = -1e30
SCALE = HEAD_DIM ** -0.5
N_DEV = 8

ADAM_LR = 0.001
ADAM_B1 = 0.9
ADAM_B2 = 0.999
ADAM_EPS = 1e-08
ADAM_WD = 0.01
ADAM_STEP = 10

SH1, SC1, G1, SH2, SC2, G2 = range(6)

VMEM_LIMIT_LARGE = 60 * 1024 * 1024
WEIGHT_GRAD_ROWS = 2048
FFN_CHUNKS = 1
PREV_ROWS = 16
MIXER_BLOCKS = 4
MESH_ID = pl.DeviceIdType.MESH

OFF_DMOD = 0
OFF_GN1 = OFF_DMOD + N_MOD * D_MODEL
OFF_SINK = OFF_GN1 + D_MODEL
OFF_GATT = OFF_SINK + 128
OFF_GCV = OFF_GATT + ATTN_WIDTH
OFF_GN2 = OFF_GCV + CONV_WIDTH
OFF_GFIN = OFF_GN2 + D_MODEL
OFF_CONVW = OFF_GFIN + D_MODEL
OFF_LOSS = OFF_CONVW + 3 * CONV_WIDTH
PACKED = OFF_LOSS + 128


def _params(sem=None, vmem=None):
    return pltpu.CompilerParams(dimension_semantics=sem, vmem_limit_bytes=vmem)


def _coming_behind(body):
    def skipping(after_ref, *refs):
        body(*refs)

    return skipping


ANY_SPEC = pl.BlockSpec(memory_space=pl.ANY)


def _full(shape):
    nd = len(shape)
    return pl.BlockSpec(shape, lambda *_: (0,) * nd)


def _rows(tm, width):
    return pl.BlockSpec((tm, width), lambda i, *_: (i, 0))


def _sigmoid(x):
    return 1.0 / (1.0 + jnp.exp(-x))


def _rsqrt_mean_sq(x):
    return lax.rsqrt(jnp.mean(x * x, axis=-1, keepdims=True) + EPS)


def _colsum(x):
    return jnp.sum(x, axis=0, keepdims=True)


def _dot(a, b):
    return jnp.dot(a, b, preferred_element_type=F32)


def _dot_nt(a, b):
    return lax.dot_general(a, b, (((1,), (1,)), ((), ())), preferred_element_type=F32)


def _dot_tn(a, b):
    return lax.dot_general(a, b, (((0,), (0,)), ((), ())), preferred_element_type=F32)


def _mesh_position():
    return lax.axis_index("x"), lax.axis_index("y"), lax.axis_index("c")


def _linear(p):
    return 4 * p[0] + 2 * p[1] + p[2]


def _peer(k):
    x, y, c = _mesh_position()
    return (1 - x if k & 4 else x, 1 - y if k & 2 else y, 1 - c if k & 1 else c)


HBM_SPEC = pl.BlockSpec(memory_space=pltpu.HBM)
SEM_SPEC = pl.BlockSpec(memory_space=pltpu.SEMAPHORE)
DATAFLOW = pltpu.SideEffectType.DATAFLOW_SIDE_EFFECTING


def _exchange_start(src, name):
    r, c = src.shape[1:]

    def body(src_ref, land_ref, send_sems, recv_sems, src_thru, land_thru, token):
        for k in range(1, N_DEV):
            peer = _peer(k)
            pltpu.make_async_remote_copy(
                src_ref=src_ref.at[_linear(peer)], dst_ref=land_ref.at[k - 1],
                send_sem=send_sems.at[k - 1], recv_sem=recv_sems.at[k - 1],
                device_id=peer, device_id_type=MESH_ID).start()
        token[...] = jnp.zeros_like(token)

    land = lax.empty((N_DEV - 1, r, c), src.dtype)
    return pl.pallas_call(
        body, name=name,
        out_shape=(pltpu.SemaphoreType.DMA((N_DEV - 1,)), pltpu.SemaphoreType.DMA((N_DEV - 1,)),
                   pltpu.HBM(src.shape, src.dtype), pltpu.HBM(land.shape, land.dtype),
                   jax.ShapeDtypeStruct((8, 128), F32)),
        in_specs=(HBM_SPEC, HBM_SPEC),
        out_specs=(SEM_SPEC, SEM_SPEC, HBM_SPEC, HBM_SPEC, pl.BlockSpec(memory_space=pltpu.VMEM)),
        input_output_aliases={0: 2, 1: 3},
        compiler_params=pltpu.CompilerParams(has_side_effects=DATAFLOW),
    )(pltpu.with_memory_space_constraint(src, pltpu.HBM), pltpu.with_memory_space_constraint(land, pltpu.HBM))


def _exchange_wait(started, after, name):
    send_sems, recv_sems, src_thru, land_thru, _ = started

    def body(src_ref, land_ref, send_sems, recv_sems, *rest):
        for k in range(1, N_DEV):
            cp = pltpu.make_async_remote_copy(
                src_ref=src_ref.at[0], dst_ref=land_ref.at[k - 1],
                send_sem=send_sems.at[k - 1], recv_sem=recv_sems.at[k - 1],
                device_id=_peer(k), device_id_type=MESH_ID)
            cp.wait_send()
            cp.wait_recv()

    return pl.pallas_call(
        body, name=name,
        out_shape=(pltpu.HBM(src_thru.shape, src_thru.dtype), pltpu.HBM(land_thru.shape, land_thru.dtype)),
        in_specs=(HBM_SPEC, HBM_SPEC, SEM_SPEC, SEM_SPEC) + (pl.BlockSpec(memory_space=pl.ANY),) * len(after),
        out_specs=(HBM_SPEC, HBM_SPEC), input_output_aliases={0: 0, 1: 1},
        compiler_params=pltpu.CompilerParams(has_side_effects=DATAFLOW),
    )(src_thru, land_thru, send_sems, recv_sems, *after)


def _share_start(arrs, zones, name):
    n = len(arrs)

    def body(*refs):
        src_refs, zone_refs, sems = refs[:n], refs[n:2 * n], refs[2 * n:4 * n]
        me = _linear(_mesh_position())
        for a in range(n):
            for k in range(1, N_DEV):
                pltpu.make_async_remote_copy(
                    src_ref=src_refs[a], dst_ref=zone_refs[a].at[me],
                    send_sem=sems[2 * a].at[k - 1], recv_sem=sems[2 * a + 1].at[k - 1],
                    device_id=_peer(k), device_id_type=MESH_ID).start()

    outs = pl.pallas_call(
        body, name=name,
        out_shape=tuple(pltpu.SemaphoreType.DMA((N_DEV - 1,)) for _ in range(2 * n))
        + tuple(pltpu.HBM(a.shape, a.dtype) for a in arrs) + tuple(pltpu.HBM(z.shape, z.dtype) for z in zones),
        in_specs=(HBM_SPEC,) * (2 * n),
        out_specs=(SEM_SPEC,) * (2 * n) + (HBM_SPEC,) * (2 * n),
        input_output_aliases={i: 2 * n + i for i in range(2 * n)},
        compiler_params=pltpu.CompilerParams(has_side_effects=DATAFLOW),
    )(*[pltpu.with_memory_space_constraint(a, pltpu.HBM) for a in list(arrs) + list(zones)])
    return outs[:2 * n], outs[2 * n:3 * n], outs[3 * n:]


def _same_core_peers():
    x, y, c = _mesh_position()
    return [(x, y, 1 - c), (1 - x, y, c), (x, 1 - y, c), (1 - x, 1 - y, c)]


def _gather_start(bufs, name):
    n = len(bufs)

    def body(*refs):
        buf_refs, rest = refs[:n], refs[n:]
        sems, token = rest[:2 * n], rest[-1]
        me = _linear(_mesh_position())
        for a in range(n):
            for k, peer in enumerate(_same_core_peers()):
                pltpu.make_async_remote_copy(
                    src_ref=buf_refs[a].at[me], dst_ref=buf_refs[a].at[me],
                    send_sem=sems[2 * a].at[k], recv_sem=sems[2 * a + 1].at[k],
                    device_id=peer, device_id_type=MESH_ID).start()
        token[...] = jnp.zeros_like(token)

    outs = pl.pallas_call(
        body, name=name,
        out_shape=tuple(pltpu.SemaphoreType.DMA((4,)) for _ in range(2 * n))
        + tuple(pltpu.HBM(b.shape, b.dtype) for b in bufs) + (jax.ShapeDtypeStruct((8, 128), F32),),
        in_specs=(HBM_SPEC,) * n,
        out_specs=(SEM_SPEC,) * (2 * n) + (HBM_SPEC,) * n + (pl.BlockSpec(memory_space=pltpu.VMEM),),
        input_output_aliases={a: 2 * n + a for a in range(n)},
        compiler_params=pltpu.CompilerParams(has_side_effects=DATAFLOW),
    )(*[pltpu.with_memory_space_constraint(b, pltpu.HBM) for b in bufs])
    return outs[:2 * n], outs[2 * n:3 * n], outs[3 * n]


def _gather_wait(sems, bufs, after, name):
    n = len(bufs)

    def body(*refs):
        buf_refs, sem_refs = refs[:n], refs[n:3 * n]
        x, y, c = _mesh_position()
        me = _linear((x, y, c))
        for a in range(n):
            for k, peer in enumerate(_same_core_peers()):
                cp = pltpu.make_async_remote_copy(
                    src_ref=buf_refs[a].at[me], dst_ref=buf_refs[a].at[_linear(peer)],
                    send_sem=sem_refs[2 * a].at[k], recv_sem=sem_refs[2 * a + 1].at[k],
                    device_id=peer, device_id_type=MESH_ID)
                cp.wait_send()
                cp.wait_recv()

    return list(pl.pallas_call(
        body, name=name,
        out_shape=tuple(pltpu.HBM(b.shape, b.dtype) for b in bufs),
        in_specs=(HBM_SPEC,) * n + (SEM_SPEC,) * (2 * n) + (pl.BlockSpec(memory_space=pl.ANY),) * len(after),
        out_specs=(HBM_SPEC,) * n, input_output_aliases={a: a for a in range(n)},
        compiler_params=pltpu.CompilerParams(has_side_effects=DATAFLOW),
    )(*bufs, *sems, *after))


def _gather_pass_on(bufs, name):
    n = len(bufs)

    def body(*refs):
        out_refs = refs[n:2 * n]
        send_sems, recv_sems = refs[2 * n:]
        x, y, c = _mesh_position()
        sibling = (x, y, 1 - c)
        chips = [(1 - x, y), (x, 1 - y), (1 - x, 1 - y)]
        copies = []
        for a in range(n):
            for j, chip in enumerate(chips):
                block = out_refs[a].at[_linear((*chip, c))]
                copies.append(pltpu.make_async_remote_copy(
                    src_ref=block, dst_ref=block, send_sem=send_sems.at[3 * a + j], recv_sem=recv_sems.at[3 * a + j],
                    device_id=sibling, device_id_type=MESH_ID))
                copies[-1].start()
        for a in range(n):
            for j, chip in enumerate(chips):
                copies[3 * a + j].wait_send()
                theirs = out_refs[a].at[_linear((*chip, 1 - c))]
                pltpu.make_async_remote_copy(
                    src_ref=theirs, dst_ref=theirs, send_sem=send_sems.at[3 * a + j], recv_sem=recv_sems.at[3 * a + j],
                    device_id=sibling, device_id_type=MESH_ID).wait_recv()

    hbm = pl.BlockSpec(memory_space=pl.ANY)
    return list(pl.pallas_call(
        body, name=name,
        out_shape=[jax.ShapeDtypeStruct(b.shape, b.dtype) for b in bufs],
        in_specs=[hbm] * n, out_specs=[hbm] * n, input_output_aliases={a: a for a in range(n)},
        scratch_shapes=[pltpu.SemaphoreType.DMA((3 * n,)), pltpu.SemaphoreType.DMA((3 * n,))],
    )(*bufs))


def _open_step(c, conv_w, w_ada, b_cols, w_in_t, later, rel_bias, bucket):
    cols = w_ada.shape[1]
    n_later = len(later)

    def body(c_ref, cw_ref, wa_ref, b_ref, w_ref, *rest):
        later_refs, rb_ref, bk_ref = rest[:n_later], rest[n_later], rest[n_later + 1]
        cond_ref, conv_ref, mod_ref, win_ref = rest[n_later + 2:n_later + 6]
        staged_refs, bias_ref, rows_ref = rest[n_later + 6:2 * n_later + 6], rest[2 * n_later + 6], rest[2 * n_later + 7]
        cond_own, mod_own, stage = rest[2 * n_later + 8:2 * n_later + 11]
        later_stage = rest[2 * n_later + 11:3 * n_later + 11]
        wa_own, later_own = rest[3 * n_later + 11], rest[3 * n_later + 12:4 * n_later + 12]
        s_send, s_recv, w_send, w_recv, local_sems, load_sems = rest[4 * n_later + 12:]
        x, y, cc = _mesh_position()
        me = _linear((x, y, cc))
        sibling = (x, y, 1 - cc)
        chips = [(1 - x, y), (x, 1 - y), (1 - x, 1 - y)]
        v = c_ref[...]
        cond_own[...] = v * _sigmoid(v)
        stage[...] = w_ref[...].astype(BF16)

        def small(rnd, a, k, src, dst, slot):
            return pltpu.make_async_remote_copy(
                src_ref=src, dst_ref=dst.at[slot], send_sem=s_send.at[rnd, a, k - 1], recv_sem=s_recv.at[rnd, a, k - 1],
                device_id=_peer(k), device_id_type=MESH_ID)

        def block(p):
            return win_ref.at[_linear(p)]

        def big(k, blk, to, src=None):
            return pltpu.make_async_remote_copy(
                src_ref=block(blk) if src is None else src, dst_ref=block(blk),
                send_sem=w_send.at[k], recv_sem=w_recv.at[k], device_id=to, device_id_type=MESH_ID)

        mine = [pltpu.make_async_copy(cond_own, cond_ref.at[me], local_sems.at[0]),
                pltpu.make_async_copy(cw_ref, conv_ref.at[me], local_sems.at[1]),
                pltpu.make_async_copy(stage, block((x, y, cc)), local_sems.at[2])]
        for cp in mine:
            cp.start()
        sends = []
        for k in range(1, N_DEV):
            sends += [small(0, 0, k, cond_own, cond_ref, me), small(0, 1, k, cw_ref, conv_ref, me)]
        for cp in sends:
            cp.start()
        first = [big(0, (x, y, cc), sibling, src=stage)]
        first += [big(1 + j, (x, y, cc), (*chip, cc), src=stage) for j, chip in enumerate(chips)]
        for cp in first:
            cp.start()
        loads = [pltpu.make_async_copy(wa_ref, wa_own, load_sems.at[0])]
        loads += [pltpu.make_async_copy(later_refs[a], later_own[a], load_sems.at[1 + a]) for a in range(n_later)]
        for cp in loads:
            cp.start()
        for a in range(n_later):
            loads[1 + a].wait()
            later_stage[a][...] = later_own[a][...].astype(BF16)
            mine.append(pltpu.make_async_copy(later_stage[a], staged_refs[a].at[me], local_sems.at[4 + a]))
            mine[-1].start()
        _fill_bias_table(rb_ref, bk_ref, bias_ref)
        for k in range(1, N_DEV):
            small(0, 0, k, cond_own, cond_ref, _linear(_peer(k))).wait_recv()
            small(0, 1, k, cw_ref, conv_ref, _linear(_peer(k))).wait_recv()
        mine[0].wait()
        cond_all = jnp.concatenate([cond_ref[k] for k in range(N_DEV)], axis=0)
        loads[0].wait()
        mod_own[...] = _dot(cond_all, wa_own[...]) + b_ref[me]
        mine.append(pltpu.make_async_copy(mod_own, mod_ref.at[me], local_sems.at[3]))
        mine[-1].start()
        second = [small(1, 0, k, mod_own, mod_ref, me) for k in range(1, N_DEV)]
        for cp in second:
            cp.start()
        passed = []
        for j, chip in enumerate(chips):
            big(1 + j, (*chip, cc), (x, y, cc)).wait_recv()
            fwd = big(4 + j, (*chip, cc), sibling)
            fwd.start()
            passed.append(fwd)
        big(0, sibling, (x, y, cc)).wait_recv()
        for j, chip in enumerate(chips):
            big(4 + j, (*chip, 1 - cc), (x, y, cc)).wait_recv()
        for k in range(1, N_DEV):
            small(1, 0, k, mod_own, mod_ref, _linear(_peer(k))).wait_recv()
        for cp in sends + first + second + passed:
            cp.wait_send()
        for cp in mine[1:]:
            cp.wait()
        flat = jnp.concatenate([mod_ref[j, pl.ds(me, 1), :] for j in range(N_DEV)], axis=1)
        rows_ref[...] = jnp.concatenate([flat[:, D_MODEL * r:D_MODEL * (r + 1)] for r in range(N_MOD)]
                                        + [jnp.zeros((8 - N_MOD, D_MODEL), F32)], axis=0)

    vmem = pl.BlockSpec(memory_space=pltpu.VMEM)
    outs = pl.pallas_call(
        body, name="open_step",
        out_shape=[jax.ShapeDtypeStruct((N_DEV,) + c.shape, F32), jax.ShapeDtypeStruct((N_DEV,) + conv_w.shape, F32),
                   jax.ShapeDtypeStruct((N_DEV, N_DEV, cols), F32),
                   jax.ShapeDtypeStruct((N_DEV,) + w_in_t.shape, BF16)]
        + [jax.ShapeDtypeStruct((N_DEV,) + a.shape, BF16) for a in later]
        + [jax.ShapeDtypeStruct((N_Q_HEADS, BLOCK, 2 * BLOCK), F32), jax.ShapeDtypeStruct((8, D_MODEL), F32)],
        in_specs=[vmem, vmem, ANY_SPEC, vmem, vmem] + [ANY_SPEC] * n_later + [pl.BlockSpec(memory_space=pltpu.SMEM), vmem],
        out_specs=[vmem, vmem, vmem, ANY_SPEC] + [ANY_SPEC] * n_later + [vmem, vmem],
        scratch_shapes=[pltpu.VMEM(c.shape, F32), pltpu.VMEM((N_DEV, cols), F32), pltpu.VMEM(w_in_t.shape, BF16)]
        + [pltpu.VMEM(a.shape, BF16) for a in later]
        + [pltpu.VMEM(w_ada.shape, F32)] + [pltpu.VMEM(a.shape, F32) for a in later]
        + [pltpu.SemaphoreType.DMA((2, 2, N_DEV - 1)), pltpu.SemaphoreType.DMA((2, 2, N_DEV - 1)),
           pltpu.SemaphoreType.DMA((7,)), pltpu.SemaphoreType.DMA((7,)),
           pltpu.SemaphoreType.DMA((4 + n_later,)), pltpu.SemaphoreType.DMA((1 + n_later,))],
        compiler_params=_params(vmem=VMEM_LIMIT_LARGE),
    )(c, conv_w, w_ada, b_cols, w_in_t, *later, rel_bias, bucket)
    return outs[0], outs[1], outs[5 + n_later], outs[3], list(outs[4:4 + n_later]), outs[4 + n_later]


def _in_proj(after, x, mod, g_norm1, w_in, tm):
    s = x.shape[0]

    def body(x_ref, mod_ref, g_ref, w_ref, h_ref, q_ref, kv_ref, gb_ref, gc_ref, xc_ref):
        xf = x_ref[...]
        n = xf * _rsqrt_mean_sq(xf) * g_ref[...]
        h = (n * (1.0 + mod_ref[SC1:SC1 + 1, :]) + mod_ref[SH1:SH1 + 1, :]).astype(BF16)
        h_ref[...] = h
        p = _dot_nt(h, w_ref[...])
        q_ref[...] = p[:, 0:512].astype(BF16)
        kv_ref[...] = p[:, 512:768].astype(BF16)
        gb_ref[...] = p[:, 768:1280].astype(BF16)
        gc_ref[...] = p[:, 1280:1792].astype(BF16)
        xc_ref[...] = p[:, 1792:2304].astype(BF16)

    return pl.pallas_call(
        _coming_behind(body), name="in_proj", grid=(s // tm,),
        in_specs=[ANY_SPEC, _rows(tm, D_MODEL), _full((8, D_MODEL)), _full((1, D_MODEL)), _full((IN_PROJ_WIDTH, D_MODEL))],
        out_specs=[_rows(tm, D_MODEL), _rows(tm, 512), _rows(tm, 256), _rows(tm, 512), _rows(tm, 512), _rows(tm, 512)],
        out_shape=[jax.ShapeDtypeStruct((s, D_MODEL), BF16), jax.ShapeDtypeStruct((s, 512), BF16),
                   jax.ShapeDtypeStruct((s, 256), BF16), jax.ShapeDtypeStruct((s, 512), BF16),
                   jax.ShapeDtypeStruct((s, 512), BF16), jax.ShapeDtypeStruct((s, 512), BF16)],
        compiler_params=_params(("arbitrary",), VMEM_LIMIT_LARGE),
    )(after, x, mod, g_norm1, w_in)


def _t5_bucket(dist):
    max_exact = N_BUCKETS // 2
    is_small = dist < max_exact
    d = np.maximum(dist, 1).astype(np.float32)
    large = max_exact + (np.log(d / max_exact) / math.log(MAX_DISTANCE / max_exact)
                         * (N_BUCKETS - max_exact)).astype(np.int32)
    large = np.minimum(large, N_BUCKETS - 1)
    return np.where(is_small, dist, large).astype(np.int32)


def _bucket_table():
    qi = np.arange(BLOCK, dtype=np.int32)[:, None]
    sj = np.arange(2 * BLOCK, dtype=np.int32)[None, :]
    return jnp.asarray(_t5_bucket(np.maximum(qi + BLOCK - sj, 0)))


def _window_mask():
    qi = lax.broadcasted_iota(jnp.int32, (BLOCK, 2 * BLOCK), 0)
    sj = lax.broadcasted_iota(jnp.int32, (BLOCK, 2 * BLOCK), 1)
    dist = qi + BLOCK - sj
    return (dist >= 0) & (dist < BLOCK)


def _fill_bias_table(rb_ref, bk_ref, o_ref):
    bk = bk_ref[...]
    inside = _window_mask()
    for h in range(N_Q_HEADS):
        acc = jnp.zeros((BLOCK, 2 * BLOCK), F32)
        for b in range(N_BUCKETS):
            acc = jnp.where(bk == b, rb_ref[h, b], acc)
        o_ref[h] = jnp.where(inside, acc, NEG_INF)


def _load_kv_window(kv_ref, n):
    prev = jnp.maximum(n - 1, 0)
    kvw = jnp.concatenate([kv_ref[pl.ds(pl.multiple_of(prev * BLOCK, BLOCK), BLOCK), :],
                           kv_ref[pl.ds(pl.multiple_of(n * BLOCK, BLOCK), BLOCK), :]], axis=0)
    k, v = kvw[:, 0:128], kvw[:, 128:256]
    k_sw = pltpu.roll(k.astype(F32), 64, 1).astype(BF16)
    v_sw = pltpu.roll(v.astype(F32), 64, 1).astype(BF16)
    return (k, k_sw), (v, v_sw)


def _conv_taps(gc, xc, gc_prev, xc_prev, n):
    u = gc * xc
    before = jnp.where(n > 0, gc_prev.astype(F32) * xc_prev.astype(F32), 0.0)
    last = before.shape[0] - 1
    row = lax.broadcasted_iota(jnp.int32, u.shape, 0)
    u1 = jnp.where(row == 0, before[last:last + 1, :], pltpu.roll(u, 1, 0))
    u2 = jnp.where(row == 0, before[last - 1:last, :],
                   jnp.where(row == 1, before[last:last + 1, :], pltpu.roll(u, 2, 0)))
    return u, u1, u2


def _mixer_fwd(q, kv, gb, gc, xc, bias, sinks, conv_w, g_attn, g_conv):
    s = q.shape[0]
    nb = s // BLOCK

    per_step = min(MIXER_BLOCKS, nb)
    tile = per_step * BLOCK

    def one_block(n, slot, before, sink_ref, q_ref, kv_ref, gb_ref, gc_ref, xc_ref, bias_ref, cw_ref, ga_ref,
                  gcv_ref, attn_ref, merged_ref, lse_ref, p_ref):
        rows = slice(slot * BLOCK, (slot + 1) * BLOCK)
        ks, vs = _load_kv_window(kv_ref, n)
        lane = lax.broadcasted_iota(jnp.int32, (BLOCK, BLOCK), 1)
        low = lane < HEAD_DIM
        col = lax.broadcasted_iota(jnp.int32, (BLOCK, 2 * BLOCK), 1)
        no_prev = (col < BLOCK) & (n == 0)
        lse_all = jnp.zeros((BLOCK, BLOCK), F32)
        pairs = []
        for p in range(4):
            qp = q_ref[rows, 128 * p:128 * (p + 1)].astype(F32)
            kvh = p // 2
            res = []
            for e in range(2):
                h = 2 * p + e
                qm = jnp.where(low if e == 0 else ~low, qp, 0.0).astype(BF16)
                sw = 0 if kvh == e else 1
                sc = _dot_nt(qm, ks[sw]) * SCALE + bias_ref[h]
                sc = jnp.where(no_prev, NEG_INF, sc)
                sink = sink_ref[h]
                m = jnp.maximum(jnp.max(sc, axis=-1, keepdims=True), sink)
                pe = jnp.exp(sc - m)
                den = jnp.sum(pe, axis=-1, keepdims=True) + jnp.exp(sink - m)
                pb = (pe * (1.0 / den)).astype(BF16)
                p_ref[slot, h] = pb
                res.append(_dot(pb, vs[sw]))
                lse_all = lse_all + jnp.where(lane == h, m + jnp.log(den), 0.0)
            pairs.append(jnp.where(low, res[0], res[1]))
        attn = jnp.concatenate(pairs, axis=1)
        attn_ref[rows, :] = attn
        lse_ref[rows, :] = lse_all
        u, u1, u2 = _conv_taps(gc_ref[rows, :].astype(F32), xc_ref[rows, :].astype(F32), before[0], before[1], n)
        cw = cw_ref[...]
        cv = gb_ref[rows, :].astype(F32) * (cw[0:1, :] * u2 + cw[1:2, :] * u1 + cw[2:3, :] * u)
        an = attn * _rsqrt_mean_sq(attn) * ga_ref[...]
        cn = cv * _rsqrt_mean_sq(cv) * gcv_ref[...]
        merged_ref[rows, :] = jnp.concatenate([an, cn], axis=1).astype(BF16)

    def body(sink_ref, q_ref, kv_ref, gb_ref, gc_ref, xc_ref, gcp_ref, xcp_ref, *rest):
        step = pl.program_id(0)
        for sub in range(per_step):
            ahead = slice(sub * BLOCK - PREV_ROWS, sub * BLOCK)
            before = (gcp_ref[...], xcp_ref[...]) if sub == 0 else (gc_ref[ahead, :], xc_ref[ahead, :])
            one_block(step * per_step + sub, sub, before, sink_ref, q_ref, kv_ref, gb_ref, gc_ref, xc_ref, *rest)

    blk = lambda w: pl.BlockSpec((tile, w), lambda n: (n, 0))
    prev8 = pl.BlockSpec((PREV_ROWS, 512), lambda n: (jnp.maximum(n * (tile // PREV_ROWS) - 1, 0), 0))
    return pl.pallas_call(
        body, name="mixer_fwd", grid=(nb // per_step,),
        in_specs=[pl.BlockSpec(memory_space=pltpu.SMEM), blk(512), _full((s, 256)), blk(512), blk(512), blk(512),
                  prev8, prev8, _full((N_Q_HEADS, BLOCK, 2 * BLOCK)), _full((3, 512)), _full((1, 512)),
                  _full((1, 512))],
        out_specs=[blk(512), blk(1024), blk(128),
                   pl.BlockSpec((per_step, N_Q_HEADS, BLOCK, 2 * BLOCK), lambda n: (n, 0, 0, 0))],
        out_shape=[jax.ShapeDtypeStruct((s, 512), F32), jax.ShapeDtypeStruct((s, 1024), BF16),
                   jax.ShapeDtypeStruct((s, 128), F32),
                   jax.ShapeDtypeStruct((nb, N_Q_HEADS, BLOCK, 2 * BLOCK), BF16)],
        compiler_params=_params(("arbitrary",)),
    )(sinks, q, kv, gb, gc, xc, gc, xc, bias, conv_w, g_attn, g_conv)


def _out_proj(merged, x, mod, w_out, tm):
    s = x.shape[0]

    def body(m_ref, x_ref, mod_ref, w_ref, o_ref, x1_ref):
        o = _dot(m_ref[...], w_ref[...])
        o_ref[...] = o.astype(BF16)
        x1_ref[...] = x_ref[...] + mod_ref[G1:G1 + 1, :] * o

    return pl.pallas_call(
        body, name="out_proj", grid=(s // tm,),
        in_specs=[_rows(tm, D_MODEL), _rows(tm, D_MODEL), _full((8, D_MODEL)), _full((D_MODEL, D_MODEL))],
        out_specs=[_rows(tm, D_MODEL), _rows(tm, D_MODEL)],
        out_shape=[jax.ShapeDtypeStruct((s, D_MODEL), BF16), jax.ShapeDtypeStruct((s, D_MODEL), F32)],
        compiler_params=_params(("arbitrary",)),
    )(merged, x, mod, w_out)


def _resident(shape):
    nd = len(shape)
    return pl.BlockSpec(shape, lambda *_: (0,) * nd, pipeline_mode=pl.Buffered(1))


def _ffn(x1, o1, merged, mod, g_norm2, w_gu, w_down, w_out, g_final, target, tm):
    s = x1.shape[0]
    chunk = D_FF // FFN_CHUNKS

    def body(x_ref, o1_ref, mg_ref, mod_ref, g_ref, wgu_ref, wd_ref, wo_ref, gf_ref, t_ref,
             h_ref, act_ref, do_ref, dgu_ref, dx1_ref, dwo_ref, dm_ref, small_ref, dwo_acc):
        @pl.when(pl.program_id(0) == 0)
        def _():
            small_ref[...] = jnp.zeros_like(small_ref)
            dwo_acc[...] = jnp.zeros_like(dwo_acc)

        xf = x_ref[...]
        n = xf * _rsqrt_mean_sq(xf) * g_ref[...]
        h = (n * (1.0 + mod_ref[SC2:SC2 + 1, :]) + mod_ref[SH2:SH2 + 1, :]).astype(BF16)
        h_ref[...] = h
        gates, ups, o = [], [], None
        for j in range(FFN_CHUNKS):
            lo = j * chunk
            gate = _dot_nt(h, wgu_ref[lo:lo + chunk, :])
            up = _dot_nt(h, wgu_ref[D_FF + lo:D_FF + lo + chunk, :])
            sg = _sigmoid(gate)
            act = (gate * sg * up).astype(BF16)
            act_ref[:, lo:lo + chunk] = act
            gates.append((up * (sg * (1.0 + gate * (1.0 - sg)))).astype(BF16))
            ups.append((gate * sg).astype(BF16))
            part = _dot(act, wd_ref[lo:lo + chunk, :])
            o = part if o is None else o + part
        g2 = mod_ref[G2:G2 + 1, :]
        x2 = xf + g2 * o
        r = _rsqrt_mean_sq(x2)
        xn = x2 * r
        gf = gf_ref[...]
        err = xn * gf - t_ref[...]
        dy = err * (1.0 / D_MODEL)
        dxn = dy * gf
        dx2 = r * (dxn - xn * jnp.mean(dxn * xn, axis=-1, keepdims=True))
        small_ref[4:5, :] += _colsum(dy * xn)
        small_ref[5:6, :] += _colsum(err * err)
        small_ref[3:4, :] += _colsum(dx2 * o)
        do = (dx2 * g2).astype(BF16)
        do_ref[...] = do
        dh = None
        for j in range(FFN_CHUNKS):
            lo = j * chunk
            dact = _dot_nt(do, wd_ref[lo:lo + chunk, :])
            dgate = (dact * gates[j].astype(F32)).astype(BF16)
            dup = (dact * ups[j].astype(F32)).astype(BF16)
            dgu_ref[:, lo:lo + chunk] = dgate
            dgu_ref[:, D_FF + lo:D_FF + lo + chunk] = dup
            part = _dot(dgate, wgu_ref[lo:lo + chunk, :]) + _dot(dup, wgu_ref[D_FF + lo:D_FF + lo + chunk, :])
            dh = part if dh is None else dh + part
        dx1 = dx2 + _norm_mod_bwd(dh, xf, g_ref[...], mod_ref[SC2:SC2 + 1, :], small_ref)
        dx1_ref[...] = dx1.astype(BF16)
        small_ref[7:8, :] += _colsum(dx1 * o1_ref[...].astype(F32))
        do1 = (dx1 * mod_ref[G1:G1 + 1, :]).astype(BF16)
        dm_ref[...] = _dot_nt(do1, wo_ref[...]).astype(BF16)
        dwo = dwo_acc[...] + _dot_tn(mg_ref[...], do1)
        dwo_acc[...] = dwo
        dwo_ref[...] = dwo.astype(BF16)

        @pl.when(pl.program_id(0) == pl.num_programs(0) - 1)
        def _():
            total = jnp.sum(small_ref[5:6, :], axis=-1, keepdims=True) * (0.5 / D_MODEL)
            small_ref[6:7, :] = jnp.broadcast_to(total, (1, D_MODEL))

    narrow = jax.ShapeDtypeStruct((s, D_MODEL), BF16)
    return pl.pallas_call(
        body, name="ffn", grid=(s // tm,),
        in_specs=[_rows(tm, D_MODEL), _rows(tm, D_MODEL), _rows(tm, D_MODEL), _full((8, D_MODEL)), _full((1, D_MODEL)),
                  _resident((2 * D_FF, D_MODEL)), _resident((D_FF, D_MODEL)), _resident((D_MODEL, D_MODEL)),
                  _full((1, D_MODEL)), _rows(tm, D_MODEL)],
        out_specs=[_rows(tm, D_MODEL), _rows(tm, D_FF), _rows(tm, D_MODEL), _rows(tm, 2 * D_FF), _rows(tm, D_MODEL),
                   _full((D_MODEL, D_MODEL)), _rows(tm, D_MODEL), _full((8, D_MODEL))],
        out_shape=[narrow, jax.ShapeDtypeStruct((s, D_FF), BF16), narrow, jax.ShapeDtypeStruct((s, 2 * D_FF), BF16),
                   narrow, jax.ShapeDtypeStruct((D_MODEL, D_MODEL), BF16), narrow,
                   jax.ShapeDtypeStruct((8, D_MODEL), F32)],
        scratch_shapes=[pltpu.VMEM((D_MODEL, D_MODEL), F32)],
        compiler_params=_params(("arbitrary",), VMEM_LIMIT_LARGE),
    )(x1, o1, merged, mod, g_norm2, w_gu, w_down, w_out, g_final, target)


def _norm_mod_bwd(dh, xf, g, scale_row, small_ref):
    r = _rsqrt_mean_sq(xf)
    xn = xf * r
    small_ref[0:1, :] += _colsum(dh)
    small_ref[1:2, :] += _colsum(dh * (xn * g))
    dn = dh * (1.0 + scale_row)
    small_ref[2:3, :] += _colsum(dn * xn)
    dxn = dn * g
    return r * (dxn - xn * jnp.mean(dxn * xn, axis=-1, keepdims=True))


def _group_norm_bwd(dm, a, g):
    r = _rsqrt_mean_sq(a)
    an = a * r
    dan = dm * g
    return r * (dan - an * jnp.mean(dan * an, axis=-1, keepdims=True)), _colsum(dm * an)


def _sum_by_bucket(db_ref, bk_ref, o_ref, rows_ref):
    bk = bk_ref[...]
    for b in range(N_BUCKETS):
        sel = (bk == b).astype(F32)
        for h in range(N_Q_HEADS):
            rows_ref[N_BUCKETS * h + b:N_BUCKETS * h + b + 1, :] = _colsum(db_ref[h] * sel)
    head = lax.broadcasted_iota(jnp.int32, (N_BUCKETS, REL_LANES), 1)
    out = jnp.zeros((N_BUCKETS, REL_LANES), F32)
    for h in range(N_Q_HEADS):
        per_bucket = jnp.sum(rows_ref[N_BUCKETS * h:N_BUCKETS * (h + 1), :], axis=-1, keepdims=True)
        out = out + jnp.where(head == h, per_bucket, 0.0)
    o_ref[...] = out


def _mixer_bwd(after, q, kv, gb, gc, xc, probs, sinks, conv_w, g_attn, g_conv, attn, lse, dmerged, bucket):
    s = q.shape[0]
    nb = s // BLOCK

    per_step = min(MIXER_BLOCKS, nb)
    tile = per_step * BLOCK
    steps = nb // per_step

    def one_block(n, slot, before, nxt, sink_ref, q_ref, kv_ref, gb_ref, gc_ref, xc_ref, p_ref, cw_ref, ga_ref,
                  gcv_ref, attn_ref, lse_ref, dm_ref, dproj_ref, dbias_ref, dsink_ref, small_ref):
        rows = slice(slot * BLOCK, (slot + 1) * BLOCK)
        next_dy, next_dkv = nxt
        dm = dm_ref[rows, :].astype(F32)
        gbv, gcv_, xcv = gb_ref[rows, :].astype(F32), gc_ref[rows, :].astype(F32), xc_ref[rows, :].astype(F32)
        u, u1, u2 = _conv_taps(gcv_, xcv, before[0], before[1], n)
        cw = cw_ref[...]
        yv = cw[0:1, :] * u2 + cw[1:2, :] * u1 + cw[2:3, :] * u
        dcv, dg_conv = _group_norm_bwd(dm[:, 512:1024], gbv * yv, gcv_ref[...])
        small_ref[1:2, :] += dg_conv
        dproj_ref[rows, 768:1280] = (dcv * yv).astype(BF16)
        dy = dcv * gbv
        row = lax.broadcasted_iota(jnp.int32, dy.shape, 0)
        d1 = jnp.where(row == BLOCK - 1, next_dy[0:1, :], pltpu.roll(dy, BLOCK - 1, 0))
        d2 = jnp.where(row == BLOCK - 2, next_dy[0:1, :],
                       jnp.where(row == BLOCK - 1, next_dy[1:2, :], pltpu.roll(dy, BLOCK - 2, 0)))
        du = cw[2:3, :] * dy + cw[1:2, :] * d1 + cw[0:1, :] * d2
        dproj_ref[rows, 1280:1792] = (du * xcv).astype(BF16)
        dproj_ref[rows, 1792:2304] = (du * gcv_).astype(BF16)
        small_ref[2:3, :] += _colsum(dy * u2)
        small_ref[3:4, :] += _colsum(dy * u1)
        small_ref[4:5, :] += _colsum(dy * u)

        attn_v = attn_ref[rows, :]
        dout, dg_attn = _group_norm_bwd(dm[:, 0:512], attn_v, ga_ref[...])
        small_ref[0:1, :] += dg_attn
        ks, vs = _load_kv_window(kv_ref, n)
        lane = lax.broadcasted_iota(jnp.int32, (BLOCK, BLOCK), 1)
        low = lane < HEAD_DIM
        lse_all = lse_ref[rows, :]
        dsink = jnp.zeros((BLOCK, BLOCK), F32)
        dq_pairs = []
        dk_groups, dv_groups = [], []
        for kvh in range(2):
            ds_rows, pr_rows, q_rows, do_rows = [], [], [], []
            for p in (2 * kvh, 2 * kvh + 1):
                qp = q_ref[rows, 128 * p:128 * (p + 1)].astype(F32)
                do_p = dout[:, 128 * p:128 * (p + 1)]
                prod = do_p * attn_v[:, 128 * p:128 * (p + 1)]
                res = []
                for e in range(2):
                    h = 2 * p + e
                    half = low if e == 0 else ~low
                    qm = jnp.where(half, qp, 0.0).astype(BF16)
                    dom = jnp.where(half, do_p, 0.0).astype(BF16)
                    delta = jnp.sum(jnp.where(half, prod, 0.0), axis=-1, keepdims=True)
                    lse_h = jnp.sum(jnp.where(lane == h, lse_all, 0.0), axis=-1, keepdims=True)
                    sw = 0 if kvh == e else 1
                    pb = p_ref[slot, h]
                    dp = _dot_nt(dom, vs[sw])
                    ds = pb.astype(F32) * (dp - delta)
                    dbias_ref[h] += ds
                    dsink = dsink + jnp.where(lane == h, -jnp.exp(sink_ref[h] - lse_h) * delta, 0.0)
                    dsb = ds.astype(BF16)
                    res.append(_dot(dsb, ks[sw]) * SCALE)
                    ds_rows.append(dsb)
                    pr_rows.append(pb)
                    q_rows.append(qm)
                    do_rows.append(dom)
                dq_pairs.append(jnp.where(low, res[0], res[1]))
            dk_g = _dot_tn(jnp.concatenate(ds_rows, axis=0), jnp.concatenate(q_rows, axis=0)) * SCALE
            dv_g = _dot_tn(jnp.concatenate(pr_rows, axis=0), jnp.concatenate(do_rows, axis=0))
            dk_groups.append(dk_g + pltpu.roll(dk_g, 64, 1))
            dv_groups.append(dv_g + pltpu.roll(dv_g, 64, 1))
        dproj_ref[rows, 0:512] = jnp.concatenate(dq_pairs, axis=1).astype(BF16)
        dsink_ref[...] += dsink
        low_kv = lax.broadcasted_iota(jnp.int32, (2 * BLOCK, BLOCK), 1) < HEAD_DIM
        dkv_win = jnp.concatenate([jnp.where(low_kv, dk_groups[0], dk_groups[1]),
                                   jnp.where(low_kv, dv_groups[0], dv_groups[1])], axis=1)
        dproj_ref[rows, 512:768] = (dkv_win[BLOCK:2 * BLOCK, :] + next_dkv).astype(BF16)
        return dy[0:8, :], dkv_win[0:BLOCK, :]

    def body(sink_ref, q_ref, kv_ref, gb_ref, gc_ref, xc_ref, gcp_ref, xcp_ref, p_ref, cw_ref, ga_ref, gcv_ref,
             attn_ref, lse_ref, dm_ref, bk_ref, dproj_ref, drel_ref, dsink_ref, small_ref,
             dy_ref, dkv_ref, dbias_ref, rows_ref):
        refs = (p_ref, cw_ref, ga_ref, gcv_ref, attn_ref, lse_ref, dm_ref, dproj_ref, dbias_ref, dsink_ref, small_ref)
        step = pl.program_id(0)

        @pl.when(step == 0)
        def _():
            dbias_ref[...] = jnp.zeros_like(dbias_ref)
            dsink_ref[...] = jnp.zeros_like(dsink_ref)
            small_ref[...] = jnp.zeros_like(small_ref)
            dy_ref[...] = jnp.zeros_like(dy_ref)
            dkv_ref[...] = jnp.zeros_like(dkv_ref)

        nxt = (dy_ref[...], dkv_ref[...])
        for sub in reversed(range(per_step)):
            ahead = slice(sub * BLOCK - PREV_ROWS, sub * BLOCK)
            before = (gcp_ref[...], xcp_ref[...]) if sub == 0 else (gc_ref[ahead, :], xc_ref[ahead, :])
            nxt = one_block((steps - 1 - step) * per_step + sub, sub, before, nxt,
                            sink_ref, q_ref, kv_ref, gb_ref, gc_ref, xc_ref, *refs)
        dy_ref[...], dkv_ref[...] = nxt

        @pl.when(step == steps - 1)
        def _():
            small_ref[5:6, :] = jnp.concatenate([_colsum(dsink_ref[...]), jnp.zeros((1, 512 - BLOCK), F32)], axis=1)
            _sum_by_bucket(dbias_ref, bk_ref, drel_ref, rows_ref)

    blk = lambda w: pl.BlockSpec((tile, w), lambda t: (steps - 1 - t, 0))
    prev8 = pl.BlockSpec((PREV_ROWS, 512),
                         lambda t: (jnp.maximum((steps - 1 - t) * (tile // PREV_ROWS) - 1, 0), 0))
    bf = lambda w: jax.ShapeDtypeStruct((s, w), BF16)
    return pl.pallas_call(
        _coming_behind(body), name="mixer_bwd", grid=(steps,),
        in_specs=[ANY_SPEC, pl.BlockSpec(memory_space=pltpu.SMEM), blk(512), _full((s, 256)), blk(512), blk(512), blk(512),
                  prev8, prev8,
                  pl.BlockSpec((per_step, N_Q_HEADS, BLOCK, 2 * BLOCK), lambda t: (steps - 1 - t, 0, 0, 0)),
                  _full((3, 512)), _full((1, 512)), _full((1, 512)), blk(512), blk(128), blk(1024),
                  _full((BLOCK, 2 * BLOCK))],
        out_specs=[blk(IN_PROJ_WIDTH), _full((N_BUCKETS, REL_LANES)), _full((BLOCK, BLOCK)), _full((8, 512))],
        out_shape=[bf(IN_PROJ_WIDTH), jax.ShapeDtypeStruct((N_BUCKETS, REL_LANES), F32),
                   jax.ShapeDtypeStruct((BLOCK, BLOCK), F32), jax.ShapeDtypeStruct((8, 512), F32)],
        scratch_shapes=[pltpu.VMEM((8, 512), F32), pltpu.VMEM((BLOCK, 2 * KV_WIDTH), F32),
                        pltpu.VMEM((N_Q_HEADS, BLOCK, 2 * BLOCK), F32),
                        pltpu.VMEM((N_BUCKETS * N_Q_HEADS, 2 * BLOCK), F32)],
        compiler_params=_params(("arbitrary",), VMEM_LIMIT_LARGE),
    )(after, sinks, q, kv, gb, gc, xc, gc, xc, probs, conv_w, g_attn, g_conv, attn, lse, dmerged, bucket)


def _in_proj_bwd(after, dproj, x, dx1, mod, g_norm1, w_in, tm):
    s = x.shape[0]

    def body(dproj_ref, x_ref, dx1_ref, mod_ref, g_ref, w_ref, dx_ref, small_ref):
        @pl.when(pl.program_id(0) == 0)
        def _():
            small_ref[...] = jnp.zeros_like(small_ref)

        dh = _dot(dproj_ref[...], w_ref[...])
        dx_ref[...] = dx1_ref[...].astype(F32) + _norm_mod_bwd(dh, x_ref[...], g_ref[...], mod_ref[SC1:SC1 + 1, :],
                                                               small_ref)

    return pl.pallas_call(
        _coming_behind(body), name="in_proj_bwd", grid=(s // tm,),
        in_specs=[ANY_SPEC, _rows(tm, IN_PROJ_WIDTH), _rows(tm, D_MODEL), _rows(tm, D_MODEL), _full((8, D_MODEL)),
                  _full((1, D_MODEL)), _full((IN_PROJ_WIDTH, D_MODEL))],
        out_specs=[_rows(tm, D_MODEL), _full((8, D_MODEL))],
        out_shape=[jax.ShapeDtypeStruct((s, D_MODEL), F32), jax.ShapeDtypeStruct((8, D_MODEL), F32)],
        compiler_params=_params(("arbitrary",), VMEM_LIMIT_LARGE),
    )(after, dproj, x, dx1, mod, g_norm1, w_in)


def _weight_grad(a, b, tk, ts, name, after=None, cols=None):
    s, k = a.shape
    col, n = (0, b.shape[1]) if cols is None else cols
    nt = s // ts
    extra = [] if after is None else [after]

    def body(a_ref, b_ref, *rest):
        o_ref, acc_ref = rest[-2:]
        t = pl.program_id(1)
        @pl.when(t == 0)
        def _():
            acc_ref[...] = jnp.zeros_like(acc_ref)

        acc = acc_ref[...] + _dot_tn(a_ref[...], b_ref[...])
        acc_ref[...] = acc
        o_ref[...] = acc.astype(BF16)

    return pl.pallas_call(
        body, name=name, grid=(k // tk, nt),
        in_specs=[pl.BlockSpec((ts, tk), lambda i, t: (t, i)), pl.BlockSpec((ts, n), lambda i, t: (t, col))]
        + [ANY_SPEC] * len(extra),
        out_specs=pl.BlockSpec((tk, n), lambda i, t: (i, 0)),
        out_shape=jax.ShapeDtypeStruct((k, n), BF16),
        scratch_shapes=[pltpu.VMEM((tk, n), F32)],
        compiler_params=_params(("arbitrary", "arbitrary"), VMEM_LIMIT_LARGE),
    )(a, b, *extra)


def _lanes_from(x, start, width):
    n = x.shape[1]
    return pltpu.roll(x, (n - start) % n, 1)[:, 0:width]


def _share_wait_adamw_w_ada(started, after, me, cond_all, w, m, v, tr):
    sems, arrs, zones = started
    n = len(arrs)
    r, cols = w.shape
    first = 4 * n + len(after)

    def body(me_ref, *refs):
        src_refs, zone_refs, sem_refs = refs[:n], refs[n:2 * n], refs[2 * n:4 * n]
        c_ref, w_ref, m_ref, v_ref = refs[first:first + 4]
        g_ref, d_ref, mo_ref, vo_ref, p_ref, load_sem = refs[first + 4 + 2 * n:]
        for a in range(n):
            for k in range(1, N_DEV):
                cp = pltpu.make_async_remote_copy(
                    src_ref=src_refs[a], dst_ref=zone_refs[a].at[_linear(_peer(k))],
                    send_sem=sem_refs[2 * a].at[k - 1], recv_sem=sem_refs[2 * a + 1].at[k - 1],
                    device_id=_peer(k), device_id_type=MESH_ID)
                cp.wait_send()
                cp.wait_recv()
        load = pltpu.make_async_copy(zone_refs[0], p_ref, load_sem.at[0])
        load.start()
        load.wait()
        dmod = jnp.concatenate([p_ref[k][:, OFF_DMOD:OFF_DMOD + N_MOD * D_MODEL] for k in range(N_DEV)], axis=0)
        pad = lambda a: jnp.concatenate([a, jnp.zeros((128 - N_DEV, a.shape[1]), F32)], axis=0)
        mine = pad(_lanes_from(dmod, me_ref[0] * cols, cols))
        for i in range(r // tr):
            rows = slice(tr * i, tr * (i + 1))
            cond = jnp.concatenate([c_ref[k][:, rows] for k in range(N_DEV)], axis=0)
            g = _dot_tn(pad(cond), mine)
            g_ref[rows, :] = g
            d_ref[rows, :], mo_ref[rows, :], vo_ref[rows, :] = _adam_math(
                w_ref[rows, :], g, m_ref[rows, :], v_ref[rows, :])

    vmem = pl.BlockSpec(memory_space=pltpu.VMEM)
    outs = pl.pallas_call(
        body, name="share_small_wait_adamw_w_ada",
        out_shape=tuple(pltpu.HBM(a.shape, a.dtype) for a in arrs) + tuple(pltpu.HBM(z.shape, z.dtype) for z in zones)
        + (jax.ShapeDtypeStruct((r, cols), F32),) * 4,
        in_specs=(pl.BlockSpec(memory_space=pltpu.SMEM),) + (HBM_SPEC,) * (2 * n) + (SEM_SPEC,) * (2 * n)
        + (ANY_SPEC,) * len(after) + (vmem,) * 4,
        out_specs=(HBM_SPEC,) * (2 * n) + (vmem,) * 4,
        input_output_aliases={1 + i: i for i in range(2 * n)},
        scratch_shapes=[pltpu.VMEM(zones[0].shape, F32), pltpu.SemaphoreType.DMA((1,))],
        compiler_params=pltpu.CompilerParams(has_side_effects=DATAFLOW, vmem_limit_bytes=VMEM_LIMIT_LARGE),
    )(me, *arrs, *zones, *sems, *after, cond_all, w, m, v)
    return list(outs[n:2 * n]), tuple(outs[2 * n:])


SMALL_PARAMS = (("rel_bias", None), ("b_ada", (OFF_DMOD, N_MOD * D_MODEL)), ("g_norm1", (OFF_GN1, D_MODEL)),
                ("sinks", (OFF_SINK, N_Q_HEADS)), ("conv_w", None), ("g_attn_out", (OFF_GATT, ATTN_WIDTH)),
                ("g_conv_out", (OFF_GCV, CONV_WIDTH)), ("g_norm2", (OFF_GN2, D_MODEL)),
                ("g_final", (OFF_GFIN, D_MODEL)))


def _small_update(me, packed_all, rel_all, state, after):
    n_p = len(SMALL_PARAMS)
    flat = [a for triple in state for a in triple]
    conv_cols = state[4][0].shape[-1]

    def body(me_ref, p_ref, r_ref, *refs):
        ins = refs[:3 * n_p]
        loss_ref, outs = refs[3 * n_p + len(after)], refs[3 * n_p + len(after) + 1:]
        small, rel = p_ref[0], r_ref[0]
        for k in range(1, N_DEV):
            small = small + p_ref[k]
            rel = rel + r_ref[k]
        rel = jnp.concatenate([rel, jnp.zeros((REL_LANES - N_BUCKETS, REL_LANES), F32)], axis=0).T
        rel = rel[0:N_Q_HEADS, 0:N_BUCKETS]
        loss_ref[...] = small[:, OFF_LOSS:OFF_LOSS + 128]
        taps = jnp.concatenate([small[:, OFF_CONVW + CONV_WIDTH * j:OFF_CONVW + CONV_WIDTH * (j + 1)]
                                for j in range(3)] + [jnp.zeros((5, CONV_WIDTH), F32)], axis=0)
        conv_g = _lanes_from(taps, me_ref[0] * conv_cols, conv_cols)[0:3, :]
        for i, (name, lanes) in enumerate(SMALL_PARAMS):
            w_ref, m_ref, v_ref = ins[3 * i:3 * i + 3]
            if name == "conv_w":
                for j in range(3):
                    outs[4 * i][j] = conv_g[j:j + 1, :]
                    outs[4 * i + 1][j], outs[4 * i + 2][j], outs[4 * i + 3][j] = _adam_math(
                        w_ref[j], conv_g[j:j + 1, :], m_ref[j], v_ref[j])
                continue
            g = rel if name == "rel_bias" else small[:, lanes[0]:lanes[0] + lanes[1]]
            outs[4 * i][...] = g
            outs[4 * i + 1][...], outs[4 * i + 2][...], outs[4 * i + 3][...] = _adam_math(
                w_ref[...], g, m_ref[...], v_ref[...])

    vmem = pl.BlockSpec(memory_space=pltpu.VMEM)
    out_shape = [jax.ShapeDtypeStruct((1, 128), F32)]
    for w, _, _ in state:
        out_shape += [jax.ShapeDtypeStruct(w.shape, F32)] * 4
    outs = pl.pallas_call(
        body, name="small_update",
        in_specs=[pl.BlockSpec(memory_space=pltpu.SMEM), vmem, vmem] + [vmem] * len(flat)
        + [pl.BlockSpec(memory_space=pl.ANY)] * len(after),
        out_shape=out_shape,
    )(me, packed_all, rel_all, *flat, *after)
    return outs[0], [tuple(outs[1 + 4 * i:5 + 4 * i]) for i in range(n_p)]


def _adam_math(w, g, m, v):
    m = ADAM_B1 * m + (1.0 - ADAM_B1) * g
    v = ADAM_B2 * v + (1.0 - ADAM_B2) * (g * g)
    m_hat = m / (1.0 - ADAM_B1 ** ADAM_STEP)
    v_hat = v / (1.0 - ADAM_B2 ** ADAM_STEP)
    delta = -ADAM_LR * (m_hat / (jnp.sqrt(v_hat) + ADAM_EPS) + ADAM_WD * w)
    return delta, m, v


def _adamw_parts(w, m, v, local, land, me, tr, name):
    r, c = w.shape
    locals_, lands = (local, land) if isinstance(local, list) else ([local], [land])
    n = len(locals_)

    def body(me_ref, w_ref, m_ref, v_ref, *refs):
        g_ref, d_ref, mo_ref, vo_ref = refs[2 * n:]
        blocks = []
        for own_ref, land_ref in zip(refs[:n], refs[n:2 * n]):
            g = own_ref[0].astype(F32)
            for k in range(N_DEV - 1):
                g = g + land_ref[k].astype(F32)
            blocks.append(g)
        g = blocks[0] if n == 1 else jnp.concatenate(blocks, axis=1)
        g_ref[...] = g
        d_ref[...], mo_ref[...], vo_ref[...] = _adam_math(w_ref[...], g, m_ref[...], v_ref[...])

    tile = pl.BlockSpec((tr, c), lambda i, me_ref: (i, 0))
    return pl.pallas_call(
        body, name=name,
        grid_spec=pltpu.PrefetchScalarGridSpec(
            num_scalar_prefetch=1, grid=(r // tr,),
            in_specs=[tile, tile, tile]
            + [pl.BlockSpec((1, tr, a.shape[2]), lambda i, me_ref: (me_ref[0], i, 0)) for a in locals_]
            + [pl.BlockSpec((N_DEV - 1, tr, a.shape[2]), lambda i, me_ref: (0, i, 0)) for a in lands],
            out_specs=[tile] * 4),
        out_shape=[jax.ShapeDtypeStruct((r, c), F32)] * 4,
        compiler_params=_params(("arbitrary",)),
    )(me, w, m, v, *locals_, *lands)


def _local_step(x, target, mod, w_in_t, bias, weights_out_gu, weights_down, g_norm1, sinks, conv_w, g_attn,
                g_conv, g_norm2, g_final, exchange, start_after):
    s = x.shape[0]
    tm = min(512, s)
    tm_small = min(256, s)
    bucket = _bucket_table()

    h, q, kv, gb, gc, xc = _in_proj(start_after, x, mod, g_norm1, w_in_t, tm)
    attn, merged, lse, probs = _mixer_fwd(q, kv, gb, gc, xc, bias, sinks, conv_w, g_attn, g_conv)
    w_out, w_gu_t = weights_out_gu(merged)
    o1, x1 = _out_proj(merged, x, mod, w_out, tm)
    w_down = weights_down(x1)
    h2, act, do2, dgu, dx1, dw_out, dmerged, sm_2 = _ffn(x1, o1, merged, mod, g_norm2, w_gu_t, w_down, w_out, g_final,
                                                         target, tm_small)
    ts = min(WEIGHT_GRAD_ROWS, s)
    tok_out = exchange("w_out", dw_out)
    tok_down = exchange("w_down", _weight_grad(act, do2, D_FF // 2, ts, "w_down_grad", after=tok_out))
    tok_gu = exchange("w_gu", _weight_grad(dgu, h2, D_FF // 2, ts, "w_gu_grad", after=tok_down))
    dproj, d_rel, dsink, sm_mix = _mixer_bwd(
        tok_gu, q, kv, gb, gc, xc, probs, sinks, conv_w, g_attn, g_conv, attn, lse, dmerged, bucket)
    half = D_MODEL // 2
    tok_half = exchange("w_in_a", _weight_grad(dproj, h, IN_PROJ_WIDTH // 2, ts, "w_in_grad_a", cols=(0, half)))
    tok_in = exchange("w_in", _weight_grad(dproj, h, IN_PROJ_WIDTH // 2, ts, "w_in_grad_b", after=tok_half,
                                           cols=(1, half)))
    dx, sm_1 = _in_proj_bwd(tok_in, dproj, x, dx1, mod, g_norm1, w_in_t, min(1024, s))

    packed = jnp.concatenate([
        sm_1[0:1], sm_1[1:2], sm_2[7:8], sm_2[0:1], sm_2[1:2], sm_2[3:4],
        sm_1[2:3],
        sm_mix[5:6, 0:128],
        sm_mix[0:1], sm_mix[1:2],
        sm_2[2:3],
        sm_2[4:5],
        sm_mix[2:3], sm_mix[3:4], sm_mix[4:5],
        sm_2[6:7, 0:128],
    ], axis=1)
    return dx, packed, d_rel


def kernel(x, c, rel_bias, w_ada, b_ada, g_norm1, w_in, sinks, conv_w, g_attn_out, g_conv_out, w_out, g_norm2, w_gu, w_down, g_final, loss_target, m_rel_bias, m_w_ada, m_b_ada, m_g_norm1, m_w_in, m_sinks, m_conv_w, m_g_attn_out, m_g_conv_out, m_w_out, m_g_norm2, m_w_gu, m_w_down, m_g_final, v_rel_bias, v_w_ada, v_b_ada, v_g_norm1, v_w_in, v_sinks, v_conv_w, v_g_attn_out, v_g_conv_out, v_w_out, v_g_norm2, v_w_gu, v_w_down, v_g_final):
    me = _linear(_mesh_position())
    me_arr = jnp.reshape(me, (1,)).astype(jnp.int32)
    ada_cols = w_ada.shape[2]
    tm = min(512, x.shape[1])

    b_cols = b_ada.reshape(N_DEV, 1, ada_cols)
    cond_all, conv_w_all, mod, w_in_blocks, staged, bias = _open_step(
        c, conv_w.transpose(1, 0, 2), w_ada[0], b_cols, w_in[0].T, [w_out[0], w_gu[0].T, w_down[0]], rel_bias.T, _bucket_table())
    conv_w_full = conv_w_all.reshape(N_DEV, 3, -1).transpose(1, 0, 2).reshape(3, CONV_WIDTH)
    w_in_t = w_in_blocks.reshape(IN_PROJ_WIDTH, D_MODEL)
    gather_sems, staged, gather_token = _gather_start(staged, "gather_start_weights")

    def weights_out_gu(after):
        got = _gather_pass_on(_gather_wait(gather_sems[0:4], staged[0:2], [after], "gather_wait_out_gu"),
                              "gather_pass_on_out_gu")
        return got[0].reshape(D_MODEL, D_MODEL), got[1].reshape(2 * D_FF, D_MODEL)

    def weights_down(after):
        got = _gather_pass_on(_gather_wait(gather_sems[4:6], staged[2:3], [after], "gather_wait_down"),
                              "gather_pass_on_down")
        return got[0].reshape(D_FF, D_MODEL)

    started = {}

    def exchange(name, dw):
        st = _exchange_start(dw.reshape(N_DEV, dw.shape[0] // N_DEV, dw.shape[1]), "exchange_start_" + name)
        started[name] = st
        return st[4]

    dx, packed, d_rel = _local_step(
        x[0], loss_target[0], mod, w_in_t, bias, weights_out_gu, weights_down, g_norm1, sinks[0], conv_w_full,
        g_attn_out, g_conv_out, g_norm2, g_final[None, :], exchange, gather_token)

    def zone(a):
        return lax.dynamic_update_slice(jnp.zeros((N_DEV,) + a.shape, F32), a[None], (me,) + (0,) * a.ndim)

    shared = _share_start([packed, d_rel], [zone(packed), zone(d_rel)], "share_small_start")

    def finish(name, after, w, m, v, tr):
        src, land = _exchange_wait(started[name], after, "exchange_wait_" + name)
        return _adamw_parts(w, m, v, src, land, me_arr, tr, "adamw_" + name)

    g_down, d_down, nm_down, nv_down = finish("w_down", [shared[2][0]], w_down[0], m_w_down[0], v_w_down[0], 176)
    g_gu, d_gu, nm_gu, nv_gu = finish("w_gu", [nv_down], w_gu[0].T, m_w_gu[0].T, v_w_gu[0].T, 352)
    g_out, d_out, nm_out, nv_out = finish("w_out", [nv_gu], w_out[0], m_w_out[0], v_w_out[0], 128)

    (packed_all, rel_all), (g_ada, d_ada, nm_ada, nv_ada) = _share_wait_adamw_w_ada(
        shared, [nv_out], me_arr, cond_all, w_ada[0], m_w_ada[0], v_w_ada[0], 256)
    as_rows = {"conv_w": lambda a: a.transpose(1, 0, 2), "g_final": lambda a: a[None, :], "rel_bias": lambda a: a.T}
    small_state = {
        "rel_bias": (rel_bias, m_rel_bias, v_rel_bias), "b_ada": (b_ada, m_b_ada, v_b_ada),
        "g_norm1": (g_norm1, m_g_norm1, v_g_norm1), "sinks": (sinks, m_sinks, v_sinks),
        "conv_w": (conv_w, m_conv_w, v_conv_w), "g_attn_out": (g_attn_out, m_g_attn_out, v_g_attn_out),
        "g_conv_out": (g_conv_out, m_g_conv_out, v_g_conv_out), "g_norm2": (g_norm2, m_g_norm2, v_g_norm2),
        "g_final": (g_final, m_g_final, v_g_final),
    }
    state = [tuple(as_rows.get(name, lambda a: a)(a) for a in small_state[name]) for name, _ in SMALL_PARAMS]
    loss_row, small_out = _small_update(me_arr, packed_all, rel_all, state, [])
    loss = loss_row[0, 0]
    back = {"rel_bias": lambda a: a.T, "conv_w": lambda a: a.transpose(1, 0, 2)}
    small_res = {name: tuple(back[name](a) if name in back else a.reshape(small_state[name][0].shape) for a in res)
                 for (name, _), res in zip(SMALL_PARAMS, small_out)}

    src_a, land_a = _exchange_wait(started["w_in_a"], [loss_row, nv_ada], "exchange_wait_w_in_a")
    src_b, land_b = _exchange_wait(started["w_in"], [land_a], "exchange_wait_w_in")
    g_in, d_in, nm_in, nv_in = _adamw_parts(w_in[0].T, m_w_in[0].T, v_w_in[0].T, [src_a, src_b], [land_a, land_b],
                                            me_arr, 144, "adamw_w_in")

    big = {
        "w_ada": (g_ada[None], d_ada[None], nm_ada[None], nv_ada[None]),
        "w_in": (g_in.T[None], d_in.T[None], nm_in.T[None], nv_in.T[None]),
        "w_out": (g_out[None], d_out[None], nm_out[None], nv_out[None]),
        "w_gu": (g_gu.T[None], d_gu.T[None], nm_gu.T[None], nv_gu.T[None]),
        "w_down": (g_down[None], d_down[None], nm_down[None], nv_down[None]),
    }
    order = ["rel_bias", "w_ada", "b_ada", "g_norm1", "w_in", "sinks", "conv_w", "g_attn_out", "g_conv_out", "w_out",
             "g_norm2", "w_gu", "w_down", "g_final"]
    results = [big[k] if k in big else small_res[k] for k in order]
    return (loss, dx[None], *[r[0] for r in results], *[r[1] for r in results], *[r[2] for r in results],
            *[r[3] for r in results])
```
